```python
import math
import jax, jax.numpy as jnp
from jax import lax
import numpy as np

D_MODEL = 1024
BATCH = 8
SEQ = 4096
DEPTH = 1

D_MIX = D_MODEL
D_POOL = D_MIX // 2
D_SSM = D_MIX - D_POOL
POOL_WINDOWS = (2, 4, 8, 16)
N_POOL_GROUPS = len(POOL_WINDOWS)
POOL_GROUP = D_POOL // N_POOL_GROUPS
SSM_GROUP = 16
N_SSM_GROUPS = D_SSM // SSM_GROUP
SSM_STATE = 64
DT_MIN = 0.001
DT_MAX = 0.1
D_FF = 2816
CONV_WIDTH = 3
EPS = 1e-6

kernel_name = "hybrid_pool_s5_convffn_encoder"


def _rmsnorm(x, g):
    xf = x.astype(jnp.float32)
    inv = lax.rsqrt(jnp.mean(xf * xf, axis=-1, keepdims=True) + EPS)
    return (xf * inv).astype(x.dtype) * g


def _window_bounds(window, length):
    half = window // 2
    t = np.arange(length)
    lo = np.clip(t - half, 0, length)
    hi = np.clip(t + half, 0, length)
    return lo.astype(np.int32), hi.astype(np.int32), (hi - lo).astype(np.float32)


def _pool_mixer(u, pool_w, pool_scale):
    L = u.shape[1]
    outs = []
    for gi, w in enumerate(POOL_WINDOWS):
        ug = u[..., gi * POOL_GROUP:(gi + 1) * POOL_GROUP].astype(jnp.float32)
        cs = jnp.concatenate([jnp.zeros_like(ug[:, :1]), jnp.cumsum(ug, axis=1)], axis=1)
        lo, hi, cnt = _window_bounds(w, L)
        mean = (jnp.take(cs, hi, axis=1) - jnp.take(cs, lo, axis=1)) / cnt[None, :, None]
        pooled = (mean - ug).astype(u.dtype)
        outs.append(jnp.einsum('blc,cd->bld', pooled, pool_w[gi]))
    return jnp.concatenate(outs, axis=-1) * pool_scale


def _cmul(ar, ai, br, bi):
    return ar * br - ai * bi, ar * bi + ai * br


def _scan_combine(e_i, e_j):
    ai_re, ai_im, bi_re, bi_im = e_i
    aj_re, aj_im, bj_re, bj_im = e_j
    a_re, a_im = _cmul(aj_re, aj_im, ai_re, ai_im)
    t_re, t_im = _cmul(aj_re, aj_im, bi_re, bi_im)
    return a_re, a_im, t_re + bj_re, t_im + bj_im


def _s5_direction(u, log_neg_a_re, a_im, log_dt, b_re, b_im, c_re, c_im, reverse):
    L = u.shape[1]
    log_neg_a_re = log_neg_a_re.astype(jnp.float32)
    a_im = a_im.astype(jnp.float32)
    dt = jnp.exp(log_dt.astype(jnp.float32))[:, None]
    a_re = -jnp.exp(log_neg_a_re)
    mag = jnp.exp(a_re * dt)
    ang = a_im * dt
    lam_re, lam_im = mag * jnp.cos(ang), mag * jnp.sin(ang)
    den = a_re * a_re + a_im * a_im
    f_re = ((lam_re - 1.0) * a_re + lam_im * a_im) / den
    f_im = (lam_im * a_re - (lam_re - 1.0) * a_im) / den
    bu_re = jnp.einsum('blgh,gnh->blgn', u, b_re.astype(jnp.float32))
    bu_im = jnp.einsum('blgh,gnh->blgn', u, b_im.astype(jnp.float32))
    in_re, in_im = _cmul(f_re, f_im, bu_re, bu_im)
    shape = (1, L) + lam_re.shape
    lam_re_l = jnp.broadcast_to(lam_re[None, None], shape)
    lam_im_l = jnp.broadcast_to(lam_im[None, None], shape)
    _, _, s_re, s_im = lax.associative_scan(
        _scan_combine, (lam_re_l, lam_im_l, in_re, in_im), reverse=reverse, axis=1)
    return (jnp.einsum('blgn,ghn->blgh', s_re, c_re.astype(jnp.float32))
            - jnp.einsum('blgn,ghn->blgh', s_im, c_im.astype(jnp.float32)))


def _s5_mixer(u, ssm_log_neg_a_re, ssm_a_im, ssm_log_dt, ssm_b_re, ssm_b_im,
              ssm_c_re, ssm_c_im, ssm_d, glu_w, glu_b):
    Bsz, L, _ = u.shape
    uf = u.astype(jnp.float32).reshape(Bsz, L, N_SSM_GROUPS, SSM_GROUP)
    y_fwd = _s5_direction(uf, ssm_log_neg_a_re[0], ssm_a_im[0], ssm_log_dt[0], ssm_b_re[0],
                          ssm_b_im[0], ssm_c_re[0], ssm_c_im[0], reverse=False)
    y_bwd = _s5_direction(uf, ssm_log_neg_a_re[1], ssm_a_im[1], ssm_log_dt[1], ssm_b_re[1],
                          ssm_b_im[1], ssm_c_re[1], ssm_c_im[1], reverse=True)
    y = (y_fwd + y_bwd).reshape(Bsz, L, D_SSM).astype(u.dtype) + ssm_d * u
    z = jax.nn.gelu(y)
    return z * jax.nn.sigmoid(jnp.einsum('bld,de->ble', z, glu_w) + glu_b)


def _dwconv_centred(h, w, b):
    hp = jnp.pad(h, ((0, 0), (1, 1), (0, 0)))
    return hp[:, :-2] * w[0] + hp[:, 1:-1] * w[1] + hp[:, 2:] * w[2] + b


def _fwd_setup_inputs(seed: int = 0) -> dict:
    key = jax.random.key(seed)
    ks = jax.random.split(key, 24)
    G, N, H = N_SSM_GROUPS, SSM_STATE, SSM_GROUP
    f32 = jnp.float32
    nrm = lambda k, shape, s: (jax.random.normal(k, shape, f32) * s)
    x = jax.random.normal(ks[0], (BATCH, SEQ, D_MODEL), f32)
    norm_mix_g = 1.0 + nrm(ks[1], (D_MODEL,), 0.02)
    w_in = nrm(ks[2], (D_MODEL, D_MIX), D_MODEL ** -0.5)
    pool_w = nrm(ks[3], (N_POOL_GROUPS, POOL_GROUP, POOL_GROUP), POOL_GROUP ** -0.5)
    pool_scale = 1.0 + nrm(ks[4], (D_POOL,), 0.02)
    ssm_log_neg_a_re = math.log(0.5) + nrm(ks[5], (2, G, N), 0.01)
    ssm_a_im = math.pi * jnp.arange(N, dtype=f32)[None, None, :] + nrm(ks[6], (2, G, N), 0.01)
    ssm_log_dt = jax.random.uniform(ks[7], (2, G), f32, math.log(DT_MIN), math.log(DT_MAX))
    ssm_b_re = nrm(ks[8], (2, G, N, H), (2.0 * H) ** -0.5)
    ssm_b_im = nrm(ks[9], (2, G, N, H), (2.0 * H) ** -0.5)
    ssm_c_re = nrm(ks[10], (2, G, H, N), (2.0 * N) ** -0.5)
    ssm_c_im = nrm(ks[11], (2, G, H, N), (2.0 * N) ** -0.5)
    ssm_d = nrm(ks[12], (D_SSM,), 1.0)
    glu_w = nrm(ks[13], (D_SSM, D_SSM), D_SSM ** -0.5)
    glu_b = nrm(ks[14], (D_SSM,), 0.01)
    out_norm_pool_g = 1.0 + nrm(ks[15], (D_POOL,), 0.02)
    out_norm_ssm_g = 1.0 + nrm(ks[16], (D_SSM,), 0.02)
    w_out = nrm(ks[17], (D_MIX, D_MODEL), D_MIX ** -0.5)
    norm_ffn_g = 1.0 + nrm(ks[18], (D_MODEL,), 0.02)
    w_up = nrm(ks[19], (D_MODEL, 2 * D_FF), D_MODEL ** -0.5)
    conv_w = nrm(ks[20], (CONV_WIDTH, 2 * D_FF), CONV_WIDTH ** -0.5)
    conv_b = nrm(ks[21], (2 * D_FF,), 0.01)
    w_down = nrm(ks[22], (D_FF, D_MODEL), D_FF ** -0.5)
    final_norm_g = 1.0 + nrm(ks[23], (D_MODEL,), 0.02)
    return {"x": x, "norm_mix_g": norm_mix_g, "w_in": w_in, "pool_w": pool_w,
            "pool_scale": pool_scale, "ssm_log_neg_a_re": ssm_log_neg_a_re, "ssm_a_im": ssm_a_im,
            "ssm_log_dt": ssm_log_dt, "ssm_b_re": ssm_b_re, "ssm_b_im": ssm_b_im,
            "ssm_c_re": ssm_c_re, "ssm_c_im": ssm_c_im, "ssm_d": ssm_d, "glu_w": glu_w,
            "glu_b": glu_b, "out_norm_pool_g": out_norm_pool_g, "out_norm_ssm_g": out_norm_ssm_g,
            "w_out": w_out, "norm_ffn_g": norm_ffn_g, "w_up": w_up, "conv_w": conv_w,
            "conv_b": conv_b, "w_down": w_down, "final_norm_g": final_norm_g}


def _fwd_reference(x, norm_mix_g, w_in, pool_w, pool_scale, ssm_log_neg_a_re, ssm_a_im, ssm_log_dt,
              ssm_b_re, ssm_b_im, ssm_c_re, ssm_c_im, ssm_d, glu_w, glu_b, out_norm_pool_g,
              out_norm_ssm_g, w_out, norm_ffn_g, w_up, conv_w, conv_b, w_down, final_norm_g):
    h = x
    for _ in range(DEPTH):
        u = jnp.einsum('bld,de->ble', _rmsnorm(h, norm_mix_g), w_in)
        u_pool, u_ssm = u[..., :D_POOL], u[..., D_POOL:]
        y_pool = _pool_mixer(u_pool, pool_w, pool_scale)
        y_ssm = _s5_mixer(u_ssm, ssm_log_neg_a_re, ssm_a_im, ssm_log_dt, ssm_b_re, ssm_b_im,
                          ssm_c_re, ssm_c_im, ssm_d, glu_w, glu_b)
        y = jnp.concatenate([_rmsnorm(y_pool, out_norm_pool_g),
                             _rmsnorm(y_ssm, out_norm_ssm_g)], axis=-1)
        h = h + jnp.einsum('ble,ed->bld', y, w_out)
        up = jnp.einsum('bld,df->blf', _rmsnorm(h, norm_ffn_g), w_up)
        up = _dwconv_centred(up, conv_w, conv_b)
        val, gate = up[..., :D_FF], up[..., D_FF:]
        h = h + jnp.einsum('blf,fd->bld', val * jax.nn.silu(gate), w_down)
    return _rmsnorm(h, final_norm_g)


import jax as _jax
import jax.numpy as _jnp

TWIN_FORMAT = 'train_step'
FWD_PARAMS = ['x', 'norm_mix_g', 'w_in', 'pool_w', 'pool_scale', 'ssm_log_neg_a_re', 'ssm_a_im', 'ssm_log_dt', 'ssm_b_re', 'ssm_b_im', 'ssm_c_re', 'ssm_c_im', 'ssm_d', 'glu_w', 'glu_b', 'out_norm_pool_g', 'out_norm_ssm_g', 'w_out', 'norm_ffn_g', 'w_up', 'conv_w', 'conv_b', 'w_down', 'final_norm_g']
TWIN_WEIGHTS = ['norm_mix_g', 'w_in', 'pool_w', 'pool_scale', 'ssm_log_neg_a_re', 'ssm_a_im', 'ssm_log_dt', 'ssm_b_re', 'ssm_b_im', 'ssm_c_re', 'ssm_c_im', 'ssm_d', 'glu_w', 'glu_b', 'out_norm_pool_g', 'out_norm_ssm_g', 'w_out', 'norm_ffn_g', 'w_up', 'conv_w', 'conv_b', 'w_down', 'final_norm_g']
TWIN_DIFF_INPUT = 'x'
TWIN_INPUTS = ['x', 'norm_mix_g', 'w_in', 'pool_w', 'pool_scale', 'ssm_log_neg_a_re', 'ssm_a_im', 'ssm_log_dt', 'ssm_b_re', 'ssm_b_im', 'ssm_c_re', 'ssm_c_im', 'ssm_d', 'glu_w', 'glu_b', 'out_norm_pool_g', 'out_norm_ssm_g', 'w_out', 'norm_ffn_g', 'w_up', 'conv_w', 'conv_b', 'w_down', 'final_norm_g', 'loss_target', 'm_norm_mix_g', 'm_w_in', 'm_pool_w', 'm_pool_scale', 'm_ssm_log_neg_a_re', 'm_ssm_a_im', 'm_ssm_log_dt', 'm_ssm_b_re', 'm_ssm_b_im', 'm_ssm_c_re', 'm_ssm_c_im', 'm_ssm_d', 'm_glu_w', 'm_glu_b', 'm_out_norm_pool_g', 'm_out_norm_ssm_g', 'm_w_out', 'm_norm_ffn_g', 'm_w_up', 'm_conv_w', 'm_conv_b', 'm_w_down', 'm_final_norm_g', 'v_norm_mix_g', 'v_w_in', 'v_pool_w', 'v_pool_scale', 'v_ssm_log_neg_a_re', 'v_ssm_a_im', 'v_ssm_log_dt', 'v_ssm_b_re', 'v_ssm_b_im', 'v_ssm_c_re', 'v_ssm_c_im', 'v_ssm_d', 'v_glu_w', 'v_glu_b', 'v_out_norm_pool_g', 'v_out_norm_ssm_g', 'v_w_out', 'v_norm_ffn_g', 'v_w_up', 'v_conv_w', 'v_conv_b', 'v_w_down', 'v_final_norm_g']
TWIN_OUTPUTS = ['loss', 'grad_x', 'grad_norm_mix_g', 'grad_w_in', 'grad_pool_w', 'grad_pool_scale', 'grad_ssm_log_neg_a_re', 'grad_ssm_a_im', 'grad_ssm_log_dt', 'grad_ssm_b_re', 'grad_ssm_b_im', 'grad_ssm_c_re', 'grad_ssm_c_im', 'grad_ssm_d', 'grad_glu_w', 'grad_glu_b', 'grad_out_norm_pool_g', 'grad_out_norm_ssm_g', 'grad_w_out', 'grad_norm_ffn_g', 'grad_w_up', 'grad_conv_w', 'grad_conv_b', 'grad_w_down', 'grad_final_norm_g', 'delta_norm_mix_g', 'delta_w_in', 'delta_pool_w', 'delta_pool_scale', 'delta_ssm_log_neg_a_re', 'delta_ssm_a_im', 'delta_ssm_log_dt', 'delta_ssm_b_re', 'delta_ssm_b_im', 'delta_ssm_c_re', 'delta_ssm_c_im', 'delta_ssm_d', 'delta_glu_w', 'delta_glu_b', 'delta_out_norm_pool_g', 'delta_out_norm_ssm_g', 'delta_w_out', 'delta_norm_ffn_g', 'delta_w_up', 'delta_conv_w', 'delta_conv_b', 'delta_w_down', 'delta_final_norm_g', 'new_m_norm_mix_g', 'new_m_w_in', 'new_m_pool_w', 'new_m_pool_scale', 'new_m_ssm_log_neg_a_re', 'new_m_ssm_a_im', 'new_m_ssm_log_dt', 'new_m_ssm_b_re', 'new_m_ssm_b_im', 'new_m_ssm_c_re', 'new_m_ssm_c_im', 'new_m_ssm_d', 'new_m_glu_w', 'new_m_glu_b', 'new_m_out_norm_pool_g', 'new_m_out_norm_ssm_g', 'new_m_w_out', 'new_m_norm_ffn_g', 'new_m_w_up', 'new_m_conv_w', 'new_m_conv_b', 'new_m_w_down', 'new_m_final_norm_g', 'new_v_norm_mix_g', 'new_v_w_in', 'new_v_pool_w', 'new_v_pool_scale', 'new_v_ssm_log_neg_a_re', 'new_v_ssm_a_im', 'new_v_ssm_log_dt', 'new_v_ssm_b_re', 'new_v_ssm_b_im', 'new_v_ssm_c_re', 'new_v_ssm_c_im', 'new_v_ssm_d', 'new_v_glu_w', 'new_v_glu_b', 'new_v_out_norm_pool_g', 'new_v_out_norm_ssm_g', 'new_v_w_out', 'new_v_norm_ffn_g', 'new_v_w_up', 'new_v_conv_w', 'new_v_conv_b', 'new_v_w_down', 'new_v_final_norm_g']
TWIN_LEAF_KINDS = {'loss': 'loss', 'grad_x': 'grad_x', 'grad_norm_mix_g': 'grad_w', 'grad_w_in': 'grad_w', 'grad_pool_w': 'grad_w', 'grad_pool_scale': 'grad_w', 'grad_ssm_log_neg_a_re': 'grad_w', 'grad_ssm_a_im': 'grad_w', 'grad_ssm_log_dt': 'grad_w', 'grad_ssm_b_re': 'grad_w', 'grad_ssm_b_im': 'grad_w', 'grad_ssm_c_re': 'grad_w', 'grad_ssm_c_im': 'grad_w', 'grad_ssm_d': 'grad_w', 'grad_glu_w': 'grad_w', 'grad_glu_b': 'grad_w', 'grad_out_norm_pool_g': 'grad_w', 'grad_out_norm_ssm_g': 'grad_w', 'grad_w_out': 'grad_w', 'grad_norm_ffn_g': 'grad_w', 'grad_w_up': 'grad_w', 'grad_conv_w': 'grad_w', 'grad_conv_b': 'grad_w', 'grad_w_down': 'grad_w', 'grad_final_norm_g': 'grad_w', 'delta_norm_mix_g': 'delta_w', 'delta_w_in': 'delta_w', 'delta_pool_w': 'delta_w', 'delta_pool_scale': 'delta_w', 'delta_ssm_log_neg_a_re': 'delta_w', 'delta_ssm_a_im': 'delta_w', 'delta_ssm_log_dt': 'delta_w', 'delta_ssm_b_re': 'delta_w', 'delta_ssm_b_im': 'delta_w', 'delta_ssm_c_re': 'delta_w', 'delta_ssm_c_im': 'delta_w', 'delta_ssm_d': 'delta_w', 'delta_glu_w': 'delta_w', 'delta_glu_b': 'delta_w', 'delta_out_norm_pool_g': 'delta_w', 'delta_out_norm_ssm_g': 'delta_w', 'delta_w_out': 'delta_w', 'delta_norm_ffn_g': 'delta_w', 'delta_w_up': 'delta_w', 'delta_conv_w': 'delta_w', 'delta_conv_b': 'delta_w', 'delta_w_down': 'delta_w', 'delta_final_norm_g': 'delta_w', 'new_m_norm_mix_g': 'new_m', 'new_m_w_in': 'new_m', 'new_m_pool_w': 'new_m', 'new_m_pool_scale': 'new_m', 'new_m_ssm_log_neg_a_re': 'new_m', 'new_m_ssm_a_im': 'new_m', 'new_m_ssm_log_dt': 'new_m', 'new_m_ssm_b_re': 'new_m', 'new_m_ssm_b_im': 'new_m', 'new_m_ssm_c_re': 'new_m', 'new_m_ssm_c_im': 'new_m', 'new_m_ssm_d': 'new_m', 'new_m_glu_w': 'new_m', 'new_m_glu_b': 'new_m', 'new_m_out_norm_pool_g': 'new_m', 'new_m_out_norm_ssm_g': 'new_m', 'new_m_w_out': 'new_m', 'new_m_norm_ffn_g': 'new_m', 'new_m_w_up': 'new_m', 'new_m_conv_w': 'new_m', 'new_m_conv_b': 'new_m', 'new_m_w_down': 'new_m', 'new_m_final_norm_g': 'new_m', 'new_v_norm_mix_g': 'new_v', 'new_v_w_in': 'new_v', 'new_v_pool_w': 'new_v', 'new_v_pool_scale': 'new_v', 'new_v_ssm_log_neg_a_re': 'new_v', 'new_v_ssm_a_im': 'new_v', 'new_v_ssm_log_dt': 'new_v', 'new_v_ssm_b_re': 'new_v', 'new_v_ssm_b_im': 'new_v', 'new_v_ssm_c_re': 'new_v', 'new_v_ssm_c_im': 'new_v', 'new_v_ssm_d': 'new_v', 'new_v_glu_w': 'new_v', 'new_v_glu_b': 'new_v', 'new_v_out_norm_pool_g': 'new_v', 'new_v_out_norm_ssm_g': 'new_v', 'new_v_w_out': 'new_v', 'new_v_norm_ffn_g': 'new_v', 'new_v_w_up': 'new_v', 'new_v_conv_w': 'new_v', 'new_v_conv_b': 'new_v', 'new_v_w_down': 'new_v', 'new_v_final_norm_g': 'new_v'}


def _forward(args):
    return _fwd_reference(*[args[k] for k in FWD_PARAMS])


def _output_shape():
    def fwd():
        inp = _fwd_setup_inputs(0)
        return _fwd_reference(*[inp[k] for k in FWD_PARAMS])
    out = _jax.eval_shape(fwd)
    return out.shape, out.dtype

N_MICROBATCH = 1
ADAM_LR = 0.001
ADAM_B1 = 0.9
ADAM_B2 = 0.999
ADAM_EPS = 1e-08
ADAM_WD = 0.01
ADAM_STEP = 10
PER_EXAMPLE_BATCH_AXIS = {'x': 0, 'loss_target': 0}
SHARED_INPUTS = []
_WEIGHT_DTYPES = {'norm_mix_g': _jnp.float32, 'w_in': _jnp.float32, 'pool_w': _jnp.float32, 'pool_scale': _jnp.float32, 'ssm_log_neg_a_re': _jnp.float32, 'ssm_a_im': _jnp.float32, 'ssm_log_dt': _jnp.float32, 'ssm_b_re': _jnp.float32, 'ssm_b_im': _jnp.float32, 'ssm_c_re': _jnp.float32, 'ssm_c_im': _jnp.float32, 'ssm_d': _jnp.float32, 'glu_w': _jnp.float32, 'glu_b': _jnp.float32, 'out_norm_pool_g': _jnp.float32, 'out_norm_ssm_g': _jnp.float32, 'w_out': _jnp.float32, 'norm_ffn_g': _jnp.float32, 'w_up': _jnp.float32, 'conv_w': _jnp.float32, 'conv_b': _jnp.float32, 'w_down': _jnp.float32, 'final_norm_g': _jnp.float32}
MOMENT_SCALE = {'norm_mix_g': 1.399845e-01, 'w_in': 1.414806e-01, 'pool_w': 1.370914e-01, 'pool_scale': 1.388738e-01, 'ssm_log_neg_a_re': 3.847026e-03, 'ssm_a_im': 6.727133e-03, 'ssm_log_dt': 3.895969e+00, 'ssm_b_re': 4.623115e-03, 'ssm_b_im': 4.717535e-03, 'ssm_c_re': 9.529098e-03, 'ssm_c_im': 9.360025e-03, 'ssm_d': 1.703974e-01, 'glu_w': 3.801253e-02, 'glu_b': 5.921650e-02, 'out_norm_pool_g': 1.395128e-01, 'out_norm_ssm_g': 1.410462e-01, 'w_out': 1.452304e-01, 'norm_ffn_g': 1.036418e-01, 'w_up': 4.238675e-02, 'conv_w': 4.295419e-02, 'conv_b': 4.310125e-02, 'w_down': 6.985060e-02, 'final_norm_g': 3.210010e+01}


def _to_microbatches(a, axis):
    t = _jnp.moveaxis(a, axis, 0)
    t = t.reshape((N_MICROBATCH, t.shape[0] // N_MICROBATCH) + t.shape[1:])
    return _jnp.moveaxis(t, 1, axis + 1)


def setup_inputs(seed: int = 0) -> dict:
    inp = _fwd_setup_inputs(seed)
    key = _jax.random.fold_in(_jax.random.key(seed), 7919)
    shape, _ = _output_shape()
    out = dict(inp)
    out["loss_target"] = _jax.random.normal(_jax.random.fold_in(key, 0), shape, _jnp.float32)
    for i, name in enumerate(TWIN_WEIGHTS):
        w = inp[name].astype(_jnp.float32)
        if MOMENT_SCALE is None:
            s = _jnp.sqrt(_jnp.mean(_jnp.square(w)) + 1e-30)
        else:
            s = MOMENT_SCALE[name]
        km, kv = _jax.random.split(_jax.random.fold_in(key, i + 1))
        out[name] = w
        out["m_" + name] = s * _jax.random.normal(km, w.shape, _jnp.float32)
        out["v_" + name] = (s * s) * _jax.random.uniform(kv, w.shape, _jnp.float32, 0.5, 1.5)
    if N_MICROBATCH > 1:
        for name, axis in PER_EXAMPLE_BATCH_AXIS.items():
            out[name] = _to_microbatches(out[name], axis)
    return {'x': out['x'], 'norm_mix_g': out['norm_mix_g'], 'w_in': out['w_in'], 'pool_w': out['pool_w'], 'pool_scale': out['pool_scale'], 'ssm_log_neg_a_re': out['ssm_log_neg_a_re'], 'ssm_a_im': out['ssm_a_im'], 'ssm_log_dt': out['ssm_log_dt'], 'ssm_b_re': out['ssm_b_re'], 'ssm_b_im': out['ssm_b_im'], 'ssm_c_re': out['ssm_c_re'], 'ssm_c_im': out['ssm_c_im'], 'ssm_d': out['ssm_d'], 'glu_w': out['glu_w'], 'glu_b': out['glu_b'], 'out_norm_pool_g': out['out_norm_pool_g'], 'out_norm_ssm_g': out['out_norm_ssm_g'], 'w_out': out['w_out'], 'norm_ffn_g': out['norm_ffn_g'], 'w_up': out['w_up'], 'conv_w': out['conv_w'], 'conv_b': out['conv_b'], 'w_down': out['w_down'], 'final_norm_g': out['final_norm_g'], 'loss_target': out['loss_target'], 'm_norm_mix_g': out['m_norm_mix_g'], 'm_w_in': out['m_w_in'], 'm_pool_w': out['m_pool_w'], 'm_pool_scale': out['m_pool_scale'], 'm_ssm_log_neg_a_re': out['m_ssm_log_neg_a_re'], 'm_ssm_a_im': out['m_ssm_a_im'], 'm_ssm_log_dt': out['m_ssm_log_dt'], 'm_ssm_b_re': out['m_ssm_b_re'], 'm_ssm_b_im': out['m_ssm_b_im'], 'm_ssm_c_re': out['m_ssm_c_re'], 'm_ssm_c_im': out['m_ssm_c_im'], 'm_ssm_d': out['m_ssm_d'], 'm_glu_w': out['m_glu_w'], 'm_glu_b': out['m_glu_b'], 'm_out_norm_pool_g': out['m_out_norm_pool_g'], 'm_out_norm_ssm_g': out['m_out_norm_ssm_g'], 'm_w_out': out['m_w_out'], 'm_norm_ffn_g': out['m_norm_ffn_g'], 'm_w_up': out['m_w_up'], 'm_conv_w': out['m_conv_w'], 'm_conv_b': out['m_conv_b'], 'm_w_down': out['m_w_down'], 'm_final_norm_g': out['m_final_norm_g'], 'v_norm_mix_g': out['v_norm_mix_g'], 'v_w_in': out['v_w_in'], 'v_pool_w': out['v_pool_w'], 'v_pool_scale': out['v_pool_scale'], 'v_ssm_log_neg_a_re': out['v_ssm_log_neg_a_re'], 'v_ssm_a_im': out['v_ssm_a_im'], 'v_ssm_log_dt': out['v_ssm_log_dt'], 'v_ssm_b_re': out['v_ssm_b_re'], 'v_ssm_b_im': out['v_ssm_b_im'], 'v_ssm_c_re': out['v_ssm_c_re'], 'v_ssm_c_im': out['v_ssm_c_im'], 'v_ssm_d': out['v_ssm_d'], 'v_glu_w': out['v_glu_w'], 'v_glu_b': out['v_glu_b'], 'v_out_norm_pool_g': out['v_out_norm_pool_g'], 'v_out_norm_ssm_g': out['v_out_norm_ssm_g'], 'v_w_out': out['v_w_out'], 'v_norm_ffn_g': out['v_norm_ffn_g'], 'v_w_up': out['v_w_up'], 'v_conv_w': out['v_conv_w'], 'v_conv_b': out['v_conv_b'], 'v_w_down': out['v_w_down'], 'v_final_norm_g': out['v_final_norm_g']}


def _loss(weights, diff, rest, loss_target):
    with _jax.named_scope("forward"):
        args = {**rest, TWIN_DIFF_INPUT: diff, **{k: w.astype(_WEIGHT_DTYPES[k]) for k, w in weights.items()}}
        y = _forward(args)
    with _jax.named_scope("loss_head"):
        err = _jnp.square(y.astype(_jnp.float32) - loss_target)
        return 0.5 * _jnp.sum(_jnp.mean(err, axis=-1)) if err.ndim else 0.5 * err


def _adamw(w, g, m, v):
    m = ADAM_B1 * m + (1.0 - ADAM_B1) * g
    v = ADAM_B2 * v + (1.0 - ADAM_B2) * _jnp.square(g)
    m_hat = m / (1.0 - ADAM_B1 ** ADAM_STEP)
    v_hat = v / (1.0 - ADAM_B2 ** ADAM_STEP)
    delta = -ADAM_LR * (m_hat / (_jnp.sqrt(v_hat) + ADAM_EPS) + ADAM_WD * w)
    return delta, m, v


def reference(x, norm_mix_g, w_in, pool_w, pool_scale, ssm_log_neg_a_re, ssm_a_im, ssm_log_dt, ssm_b_re, ssm_b_im, ssm_c_re, ssm_c_im, ssm_d, glu_w, glu_b, out_norm_pool_g, out_norm_ssm_g, w_out, norm_ffn_g, w_up, conv_w, conv_b, w_down, final_norm_g, loss_target, m_norm_mix_g, m_w_in, m_pool_w, m_pool_scale, m_ssm_log_neg_a_re, m_ssm_a_im, m_ssm_log_dt, m_ssm_b_re, m_ssm_b_im, m_ssm_c_re, m_ssm_c_im, m_ssm_d, m_glu_w, m_glu_b, m_out_norm_pool_g, m_out_norm_ssm_g, m_w_out, m_norm_ffn_g, m_w_up, m_conv_w, m_conv_b, m_w_down, m_final_norm_g, v_norm_mix_g, v_w_in, v_pool_w, v_pool_scale, v_ssm_log_neg_a_re, v_ssm_a_im, v_ssm_log_dt, v_ssm_b_re, v_ssm_b_im, v_ssm_c_re, v_ssm_c_im, v_ssm_d, v_glu_w, v_glu_b, v_out_norm_pool_g, v_out_norm_ssm_g, v_w_out, v_norm_ffn_g, v_w_up, v_conv_w, v_conv_b, v_w_down, v_final_norm_g):
    given = dict(x=x, norm_mix_g=norm_mix_g, w_in=w_in, pool_w=pool_w, pool_scale=pool_scale, ssm_log_neg_a_re=ssm_log_neg_a_re, ssm_a_im=ssm_a_im, ssm_log_dt=ssm_log_dt, ssm_b_re=ssm_b_re, ssm_b_im=ssm_b_im, ssm_c_re=ssm_c_re, ssm_c_im=ssm_c_im, ssm_d=ssm_d, glu_w=glu_w, glu_b=glu_b, out_norm_pool_g=out_norm_pool_g, out_norm_ssm_g=out_norm_ssm_g, w_out=w_out, norm_ffn_g=norm_ffn_g, w_up=w_up, conv_w=conv_w, conv_b=conv_b, w_down=w_down, final_norm_g=final_norm_g, loss_target=loss_target, m_norm_mix_g=m_norm_mix_g, m_w_in=m_w_in, m_pool_w=m_pool_w, m_pool_scale=m_pool_scale, m_ssm_log_neg_a_re=m_ssm_log_neg_a_re, m_ssm_a_im=m_ssm_a_im, m_ssm_log_dt=m_ssm_log_dt, m_ssm_b_re=m_ssm_b_re, m_ssm_b_im=m_ssm_b_im, m_ssm_c_re=m_ssm_c_re, m_ssm_c_im=m_ssm_c_im, m_ssm_d=m_ssm_d, m_glu_w=m_glu_w, m_glu_b=m_glu_b, m_out_norm_pool_g=m_out_norm_pool_g, m_out_norm_ssm_g=m_out_norm_ssm_g, m_w_out=m_w_out, m_norm_ffn_g=m_norm_ffn_g, m_w_up=m_w_up, m_conv_w=m_conv_w, m_conv_b=m_conv_b, m_w_down=m_w_down, m_final_norm_g=m_final_norm_g, v_norm_mix_g=v_norm_mix_g, v_w_in=v_w_in, v_pool_w=v_pool_w, v_pool_scale=v_pool_scale, v_ssm_log_neg_a_re=v_ssm_log_neg_a_re, v_ssm_a_im=v_ssm_a_im, v_ssm_log_dt=v_ssm_log_dt, v_ssm_b_re=v_ssm_b_re, v_ssm_b_im=v_ssm_b_im, v_ssm_c_re=v_ssm_c_re, v_ssm_c_im=v_ssm_c_im, v_ssm_d=v_ssm_d, v_glu_w=v_glu_w, v_glu_b=v_glu_b, v_out_norm_pool_g=v_out_norm_pool_g, v_out_norm_ssm_g=v_out_norm_ssm_g, v_w_out=v_w_out, v_norm_ffn_g=v_norm_ffn_g, v_w_up=v_w_up, v_conv_w=v_conv_w, v_conv_b=v_conv_b, v_w_down=v_w_down, v_final_norm_g=v_final_norm_g)
    weights = {n: given[n] for n in TWIN_WEIGHTS}
    shared = {n: given[n] for n in SHARED_INPUTS}
    per_example = {n: given[n] for n in ['x']}
    grad_fn = _jax.value_and_grad(_loss, argnums=(0, 1))

    def one_microbatch(ex, loss_target):
        ex = dict(ex)
        diff = ex.pop(TWIN_DIFF_INPUT)
        return grad_fn(weights, diff, {**shared, **ex}, loss_target)

    if N_MICROBATCH == 1:
        loss, (grad_w, grad_x) = one_microbatch(per_example, given["loss_target"])
    else:
        def body(carry, xs):
            loss_sum, grad_sum = carry
            l_k, (gw_k, gx_k) = one_microbatch(xs[0], xs[1])
            with _jax.named_scope("update"):
                return (loss_sum + l_k, _jax.tree.map(_jnp.add, grad_sum, gw_k)), gx_k

        init = (_jnp.zeros((), _jnp.float32), _jax.tree.map(_jnp.zeros_like, weights))
        (loss, grad_w), grad_x = _jax.lax.scan(body, init, (per_example, given["loss_target"]))
    with _jax.named_scope("update"):
        delta_w, new_m, new_v = {}, {}, {}
        for n in TWIN_WEIGHTS:
            delta_w[n], new_m[n], new_v[n] = _adamw(weights[n], grad_w[n], given["m_" + n], given["v_" + n])
    return (loss, grad_x, *[grad_w[n] for n in TWIN_WEIGHTS], *[delta_w[n] for n in TWIN_WEIGHTS],
            *[new_m[n] for n in TWIN_WEIGHTS], *[new_v[n] for n in TWIN_WEIGHTS])
```

```python
import numpy as np
import jax
import jax.numpy as jnp
from jax import lax
from jax.experimental import pallas as pl
from jax.experimental.pallas import tpu as pltpu

F32 = jnp.float32
BF16 = jnp.bfloat16
MESH = pl.DeviceIdType.MESH

EPS = 1e-6
POOL_WINDOWS = (2, 4, 8, 16)
POOL_GROUP = 128
SSM_GROUP = 16
SSM_STATE = 64
N_SSM_GROUPS = 32
N_STATE = N_SSM_GROUPS * SSM_STATE
QUAD = 256
N_QUAD = N_STATE // QUAD
SLAB = 256
D_SSM = 512
D_POOL = 512
D_FF = 2816
FF_BLK = 1408
HALO = 8
LANES = 128
ADAM_LR, ADAM_B1, ADAM_B2, ADAM_EPS, ADAM_WD, ADAM_STEP = 0.001, 0.9, 0.999, 1e-08, 0.01, 10
VMEM_LIMIT = 56 * 2 ** 20

TL = 512
TF = 256
TC = 256


def _cp(*sem):
    return pltpu.CompilerParams(dimension_semantics=sem, vmem_limit_bytes=VMEM_LIMIT)


def _dot_nn(a, b):
    return jnp.dot(a, b, preferred_element_type=F32)


def _dot_nt(a, b):
    return lax.dot_general(a, b, (((1,), (1,)), ((), ())), preferred_element_type=F32)


def _dot_tn(a, b):
    return lax.dot_general(a, b, (((0,), (0,)), ((), ())), preferred_element_type=F32)


def _rms_fwd(x, g):
    inv = lax.rsqrt(jnp.mean(x * x, axis=-1, keepdims=True) + EPS)
    xh = x * inv
    return xh * g, xh, inv


def _rms_bwd(dy, xh, inv, g):
    dg = jnp.sum(dy * xh, axis=0, keepdims=True)
    dxh = dy * g
    dx = inv * (dxh - xh * jnp.mean(dxh * xh, axis=-1, keepdims=True))
    return dx, dg


_GELU_C = 0.7978845608028654
_GELU_A = 0.044715


def _gelu(y):
    t = jnp.tanh(_GELU_C * (y + _GELU_A * (y * y * y)))
    return 0.5 * y * (1.0 + t), t


def _gelu_grad(y, t):
    return 0.5 * (1.0 + t) + 0.5 * y * (1.0 - t * t) * (_GELU_C * (1.0 + 3.0 * _GELU_A * y * y))


def _sigmoid(x):
    return 1.0 / (1.0 + jnp.exp(-x))


def _full(shape):
    n = len(shape)
    return pl.BlockSpec(shape, lambda *_: (0,) * n)


def _fill_ext(ext_ref, prev_ref, cur_ref, next_ref, i, n, rows):
    ext_ref[0:HALO, :] = jnp.where(i > 0, prev_ref[...], 0.0).astype(ext_ref.dtype)
    ext_ref[HALO:HALO + rows, :] = cur_ref[...]
    ext_ref[HALO + rows:2 * HALO + rows, :] = jnp.where(i < n - 1, next_ref[...], 0.0).astype(ext_ref.dtype)


def _in_proj(x, g, w):
    L, D = x.shape
    E = w.shape[1]

    def body(x_ref, g_ref, w_ref, u_ref, xn_ref):
        y, _, _ = _rms_fwd(x_ref[...], g_ref[...])
        yb = y.astype(BF16)
        xn_ref[...] = yb
        u_ref[...] = _dot_nn(yb, w_ref[...])

    return pl.pallas_call(
        body, name="in_proj", grid=(L // TL,),
        in_specs=[pl.BlockSpec((TL, D), lambda i: (i, 0)), _full((1, D)), _full(w.shape)],
        out_specs=[pl.BlockSpec((TL, E), lambda i: (i, 0)), pl.BlockSpec((TL, D), lambda i: (i, 0))],
        out_shape=[jax.ShapeDtypeStruct((L, E), F32), jax.ShapeDtypeStruct((L, D), BF16)],
        compiler_params=_cp("parallel"))(x, g, w)


def _halo_specs_1d(rows, width, L, col):
    rb = rows // HALO
    last = L // HALO - 1
    return [pl.BlockSpec((HALO, width), lambda i: (jnp.maximum(i * rb - 1, 0), col)),
            pl.BlockSpec((rows, width), lambda i: (i, col)),
            pl.BlockSpec((HALO, width), lambda i: (jnp.minimum((i + 1) * rb, last), col))]


def _pooled_from_ext(ext_ref, t0, rows, L):
    t = t0 + lax.broadcasted_iota(jnp.int32, (rows, 1), 0)
    outs = []
    for gi, w in enumerate(POOL_WINDOWS):
        half = w // 2
        cs = slice(gi * POOL_GROUP, (gi + 1) * POOL_GROUP)
        acc = ext_ref[pl.ds(HALO - half, rows), cs]
        for s in range(-half + 1, half):
            acc = acc + ext_ref[pl.ds(HALO + s, rows), cs]
        cnt = (jnp.minimum(t + half, L) - jnp.maximum(t - half, 0)).astype(F32)
        outs.append(acc / cnt - ext_ref[pl.ds(HALO, rows), cs])
    return outs


def _pool_fwd(u, pool_w_b, pool_scale, g_pool):
    L = u.shape[0]
    n = L // TL

    def body(prev_ref, cur_ref, next_ref, pw_ref, ps_ref, g_ref, out_ref, ext_ref):
        i = pl.program_id(0)
        _fill_ext(ext_ref, prev_ref, cur_ref, next_ref, i, n, TL)
        pooled = _pooled_from_ext(ext_ref, i * TL, TL, L)
        ypre = jnp.concatenate([_dot_nn(pooled[gi].astype(BF16), pw_ref[gi]) for gi in range(4)], axis=-1)
        yn, _, _ = _rms_fwd(ypre * ps_ref[...], g_ref[...])
        out_ref[...] = yn.astype(BF16)

    return pl.pallas_call(
        body, name="pool_fwd", grid=(n,),
        in_specs=_halo_specs_1d(TL, D_POOL, L, 0) + [_full(pool_w_b.shape), _full((1, D_POOL)), _full((1, D_POOL))],
        out_specs=pl.BlockSpec((TL, D_POOL), lambda i: (i, 0)),
        out_shape=jax.ShapeDtypeStruct((L, D_POOL), BF16),
        scratch_shapes=[pltpu.VMEM((TL + 2 * HALO, D_POOL), F32)],
        compiler_params=_cp("parallel"))(u, u, u, pool_w_b, pool_scale, g_pool)


def _pool_bwd_local(dh1, u, w_out_b, pool_w_b, pool_scale, g_pool):
    L = u.shape[0]
    n = L // TL
    D = dh1.shape[1]

    def body(dh_ref, prev_ref, cur_ref, next_ref, wo_ref, pw_ref, ps_ref, g_ref,
             dp_ref, gpw_ref, gps_ref, gg_ref, ext_ref):
        i = pl.program_id(0)

        @pl.when(i == 0)
        def _():
            gpw_ref[...] = jnp.zeros_like(gpw_ref)
            gps_ref[...] = jnp.zeros_like(gps_ref)
            gg_ref[...] = jnp.zeros_like(gg_ref)

        _fill_ext(ext_ref, prev_ref, cur_ref, next_ref, i, n, TL)
        pooled = [p.astype(BF16) for p in _pooled_from_ext(ext_ref, i * TL, TL, L)]
        ypre = jnp.concatenate([_dot_nn(pooled[gi], pw_ref[gi]) for gi in range(4)], axis=-1)
        ps = ps_ref[...]
        g = g_ref[...]
        _, xh, inv = _rms_fwd(ypre * ps, g)
        d_yn = _dot_nt(dh_ref[...].astype(BF16), wo_ref[...])
        d_y, dg = _rms_bwd(d_yn, xh, inv, g)
        gg_ref[...] += dg
        gps_ref[...] += jnp.sum(d_y * ypre, axis=0, keepdims=True)
        d_ypre = (d_y * ps).astype(BF16)
        for gi in range(4):
            cs = slice(gi * POOL_GROUP, (gi + 1) * POOL_GROUP)
            dp_ref[:, cs] = _dot_nt(d_ypre[:, cs], pw_ref[gi])
            gpw_ref[gi] += _dot_tn(pooled[gi], d_ypre[:, cs])

    return pl.pallas_call(
        body, name="pool_bwd_local", grid=(n,),
        in_specs=[pl.BlockSpec((TL, D), lambda i: (i, 0))] + _halo_specs_1d(TL, D_POOL, L, 0)
        + [pl.BlockSpec((D_POOL, D), lambda i: (0, 0)), _full(pool_w_b.shape), _full((1, D_POOL)), _full((1, D_POOL))],
        out_specs=[pl.BlockSpec((TL, D_POOL), lambda i: (i, 0)), _full(pool_w_b.shape),
                   _full((1, D_POOL)), _full((1, D_POOL))],
        out_shape=[jax.ShapeDtypeStruct((L, D_POOL), F32), jax.ShapeDtypeStruct(pool_w_b.shape, F32),
                   jax.ShapeDtypeStruct((1, D_POOL), F32), jax.ShapeDtypeStruct((1, D_POOL), F32)],
        scratch_shapes=[pltpu.VMEM((TL + 2 * HALO, D_POOL), F32)],
        compiler_params=_cp("arbitrary"))(dh1, u, u, u, w_out_b, pool_w_b, pool_scale, g_pool)


def _pool_bwd_window(d_pooled):
    L = d_pooled.shape[0]
    n = L // TL
    R = TL + 2 * HALO

    def body(prev_ref, cur_ref, next_ref, out_ref, ext_ref, q_ref):
        i = pl.program_id(0)
        _fill_ext(ext_ref, prev_ref, cur_ref, next_ref, i, n, TL)
        tr = i * TL - HALO + lax.broadcasted_iota(jnp.int32, (R, 1), 0)
        for gi, w in enumerate(POOL_WINDOWS):
            half = w // 2
            cs = slice(gi * POOL_GROUP, (gi + 1) * POOL_GROUP)
            cnt = jnp.maximum(jnp.minimum(tr + half, L) - jnp.maximum(tr - half, 0), 1).astype(F32)
            q_ref[:, cs] = ext_ref[:, cs] / cnt
        for gi, w in enumerate(POOL_WINDOWS):
            half = w // 2
            cs = slice(gi * POOL_GROUP, (gi + 1) * POOL_GROUP)
            acc = q_ref[pl.ds(HALO - half + 1, TL), cs]
            for s in range(-half + 2, half + 1):
                acc = acc + q_ref[pl.ds(HALO + s, TL), cs]
            out_ref[:, cs] = acc - ext_ref[pl.ds(HALO, TL), cs]

    return pl.pallas_call(
        body, name="pool_bwd_window", grid=(n,),
        in_specs=_halo_specs_1d(TL, D_POOL, L, 0),
        out_specs=pl.BlockSpec((TL, D_POOL), lambda i: (i, 0)),
        out_shape=jax.ShapeDtypeStruct((L, D_POOL), F32),
        scratch_shapes=[pltpu.VMEM((R, D_POOL), F32), pltpu.VMEM((R, D_POOL), F32)],
        compiler_params=_cp("parallel"))(d_pooled, d_pooled, d_pooled)


def _ssm_param_fn(lnar, aim, ldt):
    dt = jnp.exp(ldt)
    a_re = -jnp.exp(lnar)
    mag = jnp.exp(a_re * dt)
    ang = aim * dt
    lr, li = mag * jnp.cos(ang), mag * jnp.sin(ang)
    den = a_re * a_re + aim * aim
    fr = ((lr - 1.0) * a_re + li * aim) / den
    fi = (li * a_re - (lr - 1.0) * aim) / den
    return lr, li, fr, fi


def _ssm_params(lnar, aim, ldt):
    def body(a_ref, b_ref, c_ref, lr_ref, li_ref, fr_ref, fi_ref):
        lr, li, fr, fi = _ssm_param_fn(a_ref[...], b_ref[...], c_ref[...])
        lr_ref[...] = lr
        li_ref[...] = li
        fr_ref[...] = fr
        fi_ref[...] = fi

    sh = jax.ShapeDtypeStruct(lnar.shape, F32)
    return pl.pallas_call(body, name="ssm_params", out_shape=[sh] * 4)(lnar, aim, ldt)


def _ssm_params_bwd(lnar, aim, ldt, glr, gli, gfr, gfi):
    def body(a_ref, b_ref, c_ref, g0, g1, g2, g3, da_ref, db_ref, dc_ref):
        _, vjp = jax.vjp(_ssm_param_fn, a_ref[...], b_ref[...], c_ref[...])
        da, db, dc = vjp((g0[...], g1[...], g2[...], g3[...]))
        da_ref[...] = da
        db_ref[...] = db
        dc_ref[...] = jnp.sum(dc, axis=1, keepdims=True)

    return pl.pallas_call(
        body, name="ssm_params_bwd",
        out_shape=[jax.ShapeDtypeStruct(lnar.shape, F32), jax.ShapeDtypeStruct(aim.shape, F32),
                   jax.ShapeDtypeStruct((ldt.shape[0], 1), F32)])(lnar, aim, ldt, glr, gli, gfr, gfi)


def _scan_tables(lam4):
    def build(lr, li, reverse, out_ref, k):
        row = lax.broadcasted_iota(jnp.int32, (8, N_STATE), 0)
        lrb = jnp.broadcast_to(lr, (8, N_STATE))
        lib = jnp.broadcast_to(li, (8, N_STATE))
        pr, pi = lrb, lib
        for s, sh in enumerate((1, 2, 4)):
            mask = (row < 8 - sh) if reverse else (row >= sh)
            out_ref[k, 2 * s] = jnp.where(mask, pr, 0.0)
            out_ref[k, 2 * s + 1] = jnp.where(mask, pi, 0.0)
            pr, pi = pr * pr - pi * pi, 2.0 * pr * pi
        pr, pi = lrb, lib
        p8r = jnp.zeros((8, N_STATE), F32)
        p8i = jnp.zeros((8, N_STATE), F32)
        for j in range(8):
            r = 7 - j if reverse else j
            p8r = jnp.where(row == r, pr, p8r)
            p8i = jnp.where(row == r, pi, p8i)
            pr, pi = pr * lrb - pi * lib, pr * lib + pi * lrb
        out_ref[k, 6] = p8r
        out_ref[k, 7] = p8i

    def body(lam_ref, out_ref):
        l0r, l0i, l1r, l1i = (lam_ref[j:j + 1, :] for j in range(4))
        build(l0r, l0i, False, out_ref, 0)
        build(l0r, -l0i, True, out_ref, 1)
        build(l1r, l1i, True, out_ref, 2)
        build(l1r, -l1i, False, out_ref, 3)

    return pl.pallas_call(body, name="scan_tables",
                          out_shape=jax.ShapeDtypeStruct((4, 8, 8, N_STATE), F32))(lam4)


_QUAD_MASK = np.zeros((N_QUAD, 16, 4), np.float32)
for _q in range(N_QUAD):
    for _gl in range(4):
        _QUAD_MASK[_q, 4 * (_q % 4) + _gl, _gl] = 1.0


def _expand_b(b):
    bq = jnp.transpose(b.reshape(N_QUAD, 4, SSM_STATE, SSM_GROUP), (0, 3, 1, 2))
    e = bq[:, None] * _QUAD_MASK[:, :, None, :, None]
    return e.reshape(N_QUAD, SLAB, QUAD)


def _extract_b(e):
    e = e.reshape(N_QUAD, 16, SSM_GROUP, 4, SSM_STATE) * _QUAD_MASK[:, :, None, :, None]
    return jnp.transpose(jnp.sum(e, axis=1), (0, 2, 3, 1)).reshape(N_SSM_GROUPS, SSM_STATE, SSM_GROUP)


def _expand_c(c):
    cq = jnp.transpose(c.reshape(N_QUAD, 4, SSM_GROUP, SSM_STATE), (0, 1, 3, 2))
    e = cq[:, :, :, None, :] * jnp.transpose(_QUAD_MASK, (0, 2, 1))[:, :, None, :, None]
    return e.reshape(N_QUAD, QUAD, SLAB)


def _extract_c(e):
    e = e.reshape(N_QUAD, 4, SSM_STATE, 16, SSM_GROUP) * jnp.transpose(_QUAD_MASK, (0, 2, 1))[:, :, None, :, None]
    return jnp.transpose(jnp.sum(e, axis=3), (0, 1, 3, 2)).reshape(N_SSM_GROUPS, SSM_GROUP, SSM_STATE)


def _scan_rows(src_re, src_im, dst_re, dst_im, tab_ref, k, carry_re, carry_im, rows, reverse, s_refs=None):
    ng = rows // 8
    edge = 0 if reverse else 7
    row_id = lax.broadcasted_iota(jnp.int32, (8, LANES), 0)
    sums = []
    for lt in range(N_STATE // LANES):
        sl = slice(lt * LANES, (lt + 1) * LANES)
        tabs = [tab_ref[k, j, :, sl] for j in range(8)]

        def step(r, c, sl=sl, tabs=tabs):
            cr, ci = c[0], c[1]
            row = pl.multiple_of((ng - 1 - r) * 8 if reverse else r * 8, 8)
            xr = src_re[pl.ds(row, 8), sl]
            xi = src_im[pl.ds(row, 8), sl]
            for s, sh in enumerate((1, 2, 4)):
                amt = 8 - sh if reverse else sh
                rr = pltpu.roll(xr, amt, 0)
                ri = pltpu.roll(xi, amt, 0)
                mr, mi = tabs[2 * s], tabs[2 * s + 1]
                xr, xi = xr + mr * rr - mi * ri, xi + mr * ri + mi * rr
            xr, xi = xr + tabs[6] * cr - tabs[7] * ci, xi + tabs[6] * ci + tabs[7] * cr
            dst_re[pl.ds(row, 8), sl] = xr
            dst_im[pl.ds(row, 8), sl] = xi
            ncr = jnp.broadcast_to(xr[edge:edge + 1, :], (8, LANES))
            nci = jnp.broadcast_to(xi[edge:edge + 1, :], (8, LANES))
            if s_refs is None:
                return ncr, nci
            amt = 7 if reverse else 1
            far = 7 if reverse else 0
            nr = jnp.where(row_id == far, cr, pltpu.roll(xr, amt, 0))
            ni = jnp.where(row_id == far, ci, pltpu.roll(xi, amt, 0))
            sr = s_refs[0][pl.ds(row, 8), sl]
            si = s_refs[1][pl.ds(row, 8), sl]
            return ncr, nci, c[2] + nr * sr + ni * si, c[3] + ni * sr - nr * si

        init = (carry_re[:, sl], carry_im[:, sl])
        if s_refs is not None:
            init = init + (jnp.zeros((8, LANES), F32), jnp.zeros((8, LANES), F32))
        out = lax.fori_loop(0, ng, step, init)
        carry_re[:, sl] = out[0]
        carry_im[:, sl] = out[1]
        if s_refs is not None:
            sums.append((jnp.sum(out[2], axis=0, keepdims=True), jnp.sum(out[3], axis=0, keepdims=True)))
    return sums


def _ssm_scan_fwd(u, b_re, b_im, c_re, c_im, f2, tables, k, reverse):
    L = u.shape[0]
    nc = L // TC
    chunk = (lambda i: nc - 1 - i) if reverse else (lambda i: i)

    def body(u_ref, bre_ref, bim_ref, cre_ref, cim_ref, f_ref, tab_ref,
             y_ref, sre_ref, sim_ref, in_re, in_im, carry_re, carry_im):
        @pl.when(pl.program_id(0) == 0)
        def _():
            carry_re[...] = jnp.zeros_like(carry_re)
            carry_im[...] = jnp.zeros_like(carry_im)

        ub = u_ref[...].astype(BF16)
        for q in range(N_QUAD):
            qs = slice(q * QUAD, (q + 1) * QUAD)
            us = ub[:, (q // 4) * SLAB:(q // 4 + 1) * SLAB]
            bur = _dot_nn(us, bre_ref[q])
            bui = _dot_nn(us, bim_ref[q])
            fr = f_ref[0:1, qs]
            fi = f_ref[1:2, qs]
            in_re[:, qs] = fr * bur - fi * bui
            in_im[:, qs] = fr * bui + fi * bur
        _scan_rows(in_re, in_im, sre_ref, sim_ref, tab_ref, k, carry_re, carry_im, TC, reverse)
        for j in range(D_SSM // SLAB):
            acc = jnp.zeros((TC, SLAB), F32)
            for q in range(4 * j, 4 * j + 4):
                qs = slice(q * QUAD, (q + 1) * QUAD)
                acc = acc + _dot_nn(sre_ref[:, qs].astype(BF16), cre_ref[q])
                acc = acc - _dot_nn(sim_ref[:, qs].astype(BF16), cim_ref[q])
            y_ref[:, j * SLAB:(j + 1) * SLAB] = acc

    return pl.pallas_call(
        body, name="ssm_scan_rev" if reverse else "ssm_scan_fwd", grid=(nc,),
        in_specs=[pl.BlockSpec((TC, D_SSM), lambda i: (chunk(i), 1))]
        + [_full(b_re.shape)] * 4 + [_full(f2.shape), _full(tables.shape)],
        out_specs=[pl.BlockSpec((TC, D_SSM), lambda i: (chunk(i), 0)),
                   pl.BlockSpec((TC, N_STATE), lambda i: (chunk(i), 0)),
                   pl.BlockSpec((TC, N_STATE), lambda i: (chunk(i), 0))],
        out_shape=[jax.ShapeDtypeStruct((L, D_SSM), F32), jax.ShapeDtypeStruct((L, N_STATE), F32),
                   jax.ShapeDtypeStruct((L, N_STATE), F32)],
        scratch_shapes=[pltpu.VMEM((TC, N_STATE), F32), pltpu.VMEM((TC, N_STATE), F32),
                        pltpu.VMEM((8, N_STATE), F32), pltpu.VMEM((8, N_STATE), F32)],
        compiler_params=_cp("arbitrary"))(u, b_re, b_im, c_re, c_im, f2, tables)


def _ssm_scan_bwd(dy, u, s_re, s_im, b_re, b_im, c_re, c_im, f2, tables, k, reverse):
    L = u.shape[0]
    nc = L // TC
    chunk = (lambda i: nc - 1 - i) if reverse else (lambda i: i)

    def body(dy_ref, u_ref, sre_ref, sim_ref, bre_ref, bim_ref, cre_ref, cim_ref, f_ref, tab_ref,
             du_ref, gbr_ref, gbi_ref, gcr_ref, gci_ref, gv_ref, a_re, a_im, carry_re, carry_im):
        @pl.when(pl.program_id(0) == 0)
        def _():
            carry_re[...] = jnp.zeros_like(carry_re)
            carry_im[...] = jnp.zeros_like(carry_im)
            for r in (gbr_ref, gbi_ref, gcr_ref, gci_ref, gv_ref):
                r[...] = jnp.zeros_like(r)

        dyb = dy_ref[...].astype(BF16)
        ub = u_ref[...].astype(BF16)
        for q in range(N_QUAD):
            qs = slice(q * QUAD, (q + 1) * QUAD)
            ds = dyb[:, (q // 4) * SLAB:(q // 4 + 1) * SLAB]
            a_re[:, qs] = _dot_nt(ds, cre_ref[q])
            a_im[:, qs] = -_dot_nt(ds, cim_ref[q])
            gcr_ref[q] += _dot_tn(sre_ref[:, qs].astype(BF16), ds)
            gci_ref[q] -= _dot_tn(sim_ref[:, qs].astype(BF16), ds)
        sums = _scan_rows(a_re, a_im, a_re, a_im, tab_ref, k, carry_re, carry_im, TC, reverse,
                          s_refs=(sre_ref, sim_ref))
        for lt, (glr, gli) in enumerate(sums):
            sl = slice(lt * LANES, (lt + 1) * LANES)
            gv_ref[0:1, sl] += glr
            gv_ref[1:2, sl] += gli
        for j in range(D_SSM // SLAB):
            us = ub[:, j * SLAB:(j + 1) * SLAB]
            acc = jnp.zeros((TC, SLAB), F32)
            for q in range(4 * j, 4 * j + 4):
                qs = slice(q * QUAD, (q + 1) * QUAD)
                ar = a_re[:, qs]
                ai = a_im[:, qs]
                bur = _dot_nn(us, bre_ref[q])
                bui = _dot_nn(us, bim_ref[q])
                gv_ref[2:3, qs] += jnp.sum(ar * bur + ai * bui, axis=0, keepdims=True)
                gv_ref[3:4, qs] += jnp.sum(ai * bur - ar * bui, axis=0, keepdims=True)
                fr = f_ref[0:1, qs]
                fi = f_ref[1:2, qs]
                dbr = (fr * ar + fi * ai).astype(BF16)
                dbi = (fr * ai - fi * ar).astype(BF16)
                gbr_ref[q] += _dot_tn(us, dbr)
                gbi_ref[q] += _dot_tn(us, dbi)
                acc = acc + _dot_nt(dbr, bre_ref[q]) + _dot_nt(dbi, bim_ref[q])
            du_ref[:, j * SLAB:(j + 1) * SLAB] = acc

    big = jax.ShapeDtypeStruct(b_re.shape, F32)
    return pl.pallas_call(
        body, name="ssm_bwd_rev" if reverse else "ssm_bwd_fwd", grid=(nc,),
        in_specs=[pl.BlockSpec((TC, D_SSM), lambda i: (chunk(i), 0)),
                  pl.BlockSpec((TC, D_SSM), lambda i: (chunk(i), 1)),
                  pl.BlockSpec((TC, N_STATE), lambda i: (chunk(i), 0)),
                  pl.BlockSpec((TC, N_STATE), lambda i: (chunk(i), 0))]
        + [_full(b_re.shape)] * 4 + [_full(f2.shape), _full(tables.shape)],
        out_specs=[pl.BlockSpec((TC, D_SSM), lambda i: (chunk(i), 0))] + [_full(b_re.shape)] * 4
        + [_full((4, N_STATE))],
        out_shape=[jax.ShapeDtypeStruct((L, D_SSM), F32), big, big, big, big,
                   jax.ShapeDtypeStruct((4, N_STATE), F32)],
        scratch_shapes=[pltpu.VMEM((TC, N_STATE), F32), pltpu.VMEM((TC, N_STATE), F32),
                        pltpu.VMEM((8, N_STATE), F32), pltpu.VMEM((8, N_STATE), F32)],
        compiler_params=_cp("arbitrary"))(dy, u, s_re, s_im, b_re, b_im, c_re, c_im, f2, tables)


def _ssm_post(yf, yb, u, d, glu_w, glu_b):
    y = yf + yb + d * u
    z, t = _gelu(y)
    zb = z.astype(BF16)
    gate = _sigmoid(_dot_nn(zb, glu_w) + glu_b)
    return y, z, t, zb, gate


def _mix_out(yn_pool, yf, yb, u, x, ssm_d, glu_w_b, glu_b, g_ssm, w_out_b, g_ffn):
    L, D = x.shape

    def body(ynp_ref, yf_ref, yb_ref, u_ref, x_ref, d_ref, gw_ref, gb_ref, gs_ref, wo_ref, gf_ref,
             h1_ref, hn_ref, ycat_ref):
        _, z, _, _, gate = _ssm_post(yf_ref[...], yb_ref[...], u_ref[...], d_ref[...], gw_ref[...], gb_ref[...])
        yns, _, _ = _rms_fwd(z * gate, gs_ref[...])
        ynsb = yns.astype(BF16)
        ynp = ynp_ref[...]
        ycat_ref[:, 0:D_POOL] = ynp
        ycat_ref[:, D_POOL:D] = ynsb
        h1 = x_ref[...] + _dot_nn(ynp, wo_ref[0:D_POOL, :]) + _dot_nn(ynsb, wo_ref[D_POOL:D, :])
        h1_ref[...] = h1
        hn, _, _ = _rms_fwd(h1, gf_ref[...])
        hn_ref[...] = hn.astype(BF16)

    half = lambda c: pl.BlockSpec((TL, D_SSM), lambda i: (i, c))
    row = pl.BlockSpec((TL, D), lambda i: (i, 0))
    return pl.pallas_call(
        body, name="mix_out", grid=(L // TL,),
        in_specs=[half(0), half(0), half(0), half(1), row, _full((1, D_SSM)), _full(glu_w_b.shape),
                  _full((1, D_SSM)), _full((1, D_SSM)), _full(w_out_b.shape), _full((1, D))],
        out_specs=[row, row, row],
        out_shape=[jax.ShapeDtypeStruct((L, D), F32), jax.ShapeDtypeStruct((L, D), BF16),
                   jax.ShapeDtypeStruct((L, D), BF16)],
        compiler_params=_cp("parallel"))(yn_pool, yf, yb, u, x, ssm_d, glu_w_b, glu_b, g_ssm, w_out_b, g_ffn)


def _ssm_bwd_local(dh1, yf, yb, u, ssm_d, glu_w_b, glu_b, g_ssm, w_out_b):
    L, D = dh1.shape

    def body(dh_ref, yf_ref, yb_ref, u_ref, d_ref, gw_ref, gb_ref, gs_ref, wo_ref,
             dy_ref, du_ref, ggw_ref, ggb_ref, gd_ref, ggs_ref):
        @pl.when(pl.program_id(0) == 0)
        def _():
            for r in (ggw_ref, ggb_ref, gd_ref, ggs_ref):
                r[...] = jnp.zeros_like(r)

        u = u_ref[...]
        d = d_ref[...]
        y, z, t, zb, gate = _ssm_post(yf_ref[...], yb_ref[...], u, d, gw_ref[...], gb_ref[...])
        gs = gs_ref[...]
        _, xh, inv = _rms_fwd(z * gate, gs)
        d_yn = _dot_nt(dh_ref[...].astype(BF16), wo_ref[...])
        d_o, dgs = _rms_bwd(d_yn, xh, inv, gs)
        ggs_ref[...] += dgs
        d_zg = d_o * z * gate * (1.0 - gate)
        d_zgb = d_zg.astype(BF16)
        ggb_ref[...] += jnp.sum(d_zg, axis=0, keepdims=True)
        ggw_ref[...] += _dot_tn(zb, d_zgb)
        d_z = d_o * gate + _dot_nt(d_zgb, gw_ref[...])
        d_y = d_z * _gelu_grad(y, t)
        gd_ref[...] += jnp.sum(d_y * u, axis=0, keepdims=True)
        dy_ref[...] = d_y
        du_ref[...] = d_y * d

    half = lambda c: pl.BlockSpec((TL, D_SSM), lambda i: (i, c))
    vec = _full((1, D_SSM))
    return pl.pallas_call(
        body, name="ssm_bwd_local", grid=(L // TL,),
        in_specs=[pl.BlockSpec((TL, D), lambda i: (i, 0)), half(0), half(0), half(1), vec, _full(glu_w_b.shape),
                  vec, vec, pl.BlockSpec((D_SSM, D), lambda i: (1, 0))],
        out_specs=[half(0), half(0), _full(glu_w_b.shape), vec, vec, vec],
        out_shape=[jax.ShapeDtypeStruct((L, D_SSM), F32), jax.ShapeDtypeStruct((L, D_SSM), F32),
                   jax.ShapeDtypeStruct(glu_w_b.shape, F32)] + [jax.ShapeDtypeStruct((1, D_SSM), F32)] * 3,
        compiler_params=_cp("arbitrary"))(dh1, yf, yb, u, ssm_d, glu_w_b, glu_b, g_ssm, w_out_b)


def _in_bwd(du_pool, du_a, du_b, du_c, dh1, x, g, w_in_b):
    L, D = x.shape

    def body(p_ref, a_ref, b_ref, c_ref, dh_ref, x_ref, g_ref, w_ref, dx_ref, dub_ref, gg_ref):
        @pl.when(pl.program_id(0) == 0)
        def _():
            gg_ref[...] = jnp.zeros_like(gg_ref)

        dub_ref[:, 0:D_POOL] = p_ref[...].astype(BF16)
        dub_ref[:, D_POOL:D] = (a_ref[...] + b_ref[...] + c_ref[...]).astype(BF16)
        d_xn = _dot_nt(dub_ref[...], w_ref[...])
        gv = g_ref[...]
        _, xh, inv = _rms_fwd(x_ref[...], gv)
        dx, dg = _rms_bwd(d_xn, xh, inv, gv)
        gg_ref[...] += dg
        dx_ref[...] = dh_ref[...] + dx

    half = pl.BlockSpec((TL, D_SSM), lambda i: (i, 0))
    row = pl.BlockSpec((TL, D), lambda i: (i, 0))
    return pl.pallas_call(
        body, name="in_bwd", grid=(L // TL,),
        in_specs=[half, half, half, half, row, row, _full((1, D)), _full(w_in_b.shape)],
        out_specs=[row, row, _full((1, D))],
        out_shape=[jax.ShapeDtypeStruct((L, D), F32), jax.ShapeDtypeStruct((L, D), BF16),
                   jax.ShapeDtypeStruct((1, D), F32)],
        compiler_params=_cp("arbitrary"))(du_pool, du_a, du_b, du_c, dh1, x, g, w_in_b)


def _ffn_up(hn, w_up4):
    L, D = hn.shape

    def body(h_ref, w_ref, o_ref):
        o_ref[...] = _dot_nn(h_ref[...], w_ref[...])

    return pl.pallas_call(
        body, name="ffn_up", grid=(4, L // TF),
        in_specs=[pl.BlockSpec((TF, D), lambda j, i: (i, 0)), pl.BlockSpec((None, D, FF_BLK), lambda j, i: (j, 0, 0))],
        out_specs=pl.BlockSpec((TF, FF_BLK), lambda j, i: (i, j)),
        out_shape=jax.ShapeDtypeStruct((L, 4 * FF_BLK), F32),
        compiler_params=_cp("parallel", "parallel"))(hn, w_up4)


def _halo_specs_2d(rows, width, L, col, order):
    rb = rows // HALO
    last = L // HALO - 1
    if order == "ik":
        wrap = lambda f: (lambda i, k: f(i, k))
    else:
        wrap = lambda f: (lambda k, i: f(i, k))
    return [pl.BlockSpec((HALO, width), wrap(lambda i, k: (jnp.maximum(i * rb - 1, 0), col(k)))),
            pl.BlockSpec((rows, width), wrap(lambda i, k: (i, col(k)))),
            pl.BlockSpec((HALO, width), wrap(lambda i, k: (jnp.minimum((i + 1) * rb, last), col(k))))]


def _conv3(ext_ref, w_ref, b_ref, rows):
    return (ext_ref[pl.ds(HALO - 1, rows), :] * w_ref[0:1, :] + ext_ref[pl.ds(HALO, rows), :] * w_ref[1:2, :]
            + ext_ref[pl.ds(HALO + 1, rows), :] * w_ref[2:3, :] + b_ref[...])


def _ffn_down_loss(up, conv_w, conv_b, w_down_b, h1, target, g_final):
    L, D = h1.shape
    n = L // TF
    nk = D_FF // FF_BLK

    def body(vp, vc, vn, gp, gc, gn, wv_ref, wg_ref, bv_ref, bg_ref, wd_ref, h1_ref, t_ref, gf_ref,
             a_ref, dh2_ref, loss_ref, gg_ref, ext_v, ext_g, acc_ref):
        i = pl.program_id(0)
        k = pl.program_id(1)

        @pl.when((i == 0) & (k == 0))
        def _():
            loss_ref[...] = jnp.zeros_like(loss_ref)
            gg_ref[...] = jnp.zeros_like(gg_ref)

        @pl.when(k == 0)
        def _():
            acc_ref[...] = jnp.zeros_like(acc_ref)

        _fill_ext(ext_v, vp, vc, vn, i, n, TF)
        _fill_ext(ext_g, gp, gc, gn, i, n, TF)
        val = _conv3(ext_v, wv_ref, bv_ref, TF)
        gate = _conv3(ext_g, wg_ref, bg_ref, TF)
        a = (val * (gate * _sigmoid(gate))).astype(BF16)
        a_ref[...] = a
        acc_ref[...] += _dot_nn(a, wd_ref[...])

        @pl.when(k == nk - 1)
        def _():
            gf = gf_ref[...]
            y, xh, inv = _rms_fwd(h1_ref[...] + acc_ref[...], gf)
            diff = y - t_ref[...]
            part = 0.5 * jnp.sum(jnp.mean(diff * diff, axis=-1, keepdims=True), axis=0, keepdims=True)
            loss_ref[...] += jnp.broadcast_to(part, loss_ref.shape)
            dx, dg = _rms_bwd(diff * (1.0 / D), xh, inv, gf)
            gg_ref[...] += dg
            dh2_ref[...] = dx

    row = pl.BlockSpec((TF, D), lambda i, k: (i, 0))
    cw = lambda off: pl.BlockSpec((3, FF_BLK), lambda i, k: (0, k + off))
    cb = lambda off: pl.BlockSpec((1, FF_BLK), lambda i, k: (0, k + off))
    return pl.pallas_call(
        body, name="ffn_down_loss", grid=(n, nk),
        in_specs=_halo_specs_2d(TF, FF_BLK, L, lambda k: k, "ik") + _halo_specs_2d(TF, FF_BLK, L, lambda k: k + nk, "ik")
        + [cw(0), cw(nk), cb(0), cb(nk), pl.BlockSpec((FF_BLK, D), lambda i, k: (k, 0)), row, row, _full((1, D))],
        out_specs=[pl.BlockSpec((TF, FF_BLK), lambda i, k: (i, k)), row, _full((1, LANES)), _full((1, D))],
        out_shape=[jax.ShapeDtypeStruct((L, D_FF), BF16), jax.ShapeDtypeStruct((L, D), F32),
                   jax.ShapeDtypeStruct((1, LANES), F32), jax.ShapeDtypeStruct((1, D), F32)],
        scratch_shapes=[pltpu.VMEM((TF + 2 * HALO, FF_BLK), F32), pltpu.VMEM((TF + 2 * HALO, FF_BLK), F32),
                        pltpu.VMEM((TF, D), F32)],
        compiler_params=_cp("arbitrary", "arbitrary"))(
            up, up, up, up, up, up, conv_w, conv_w, conv_b, conv_b, w_down_b, h1, target, g_final)


def _ffn_act_bwd(up, conv_w, conv_b, w_down_b, dh2):
    L, D = dh2.shape
    n = L // TF
    nk = D_FF // FF_BLK

    def body(vp, vc, vn, gp, gc, gn, wv_ref, wg_ref, bv_ref, bg_ref, wd_ref, dh_ref,
             dv_ref, dg_ref, gcv_ref, gcg_ref, ext_v, ext_g):
        i = pl.program_id(1)

        @pl.when(i == 0)
        def _():
            gcv_ref[...] = jnp.zeros_like(gcv_ref)
            gcg_ref[...] = jnp.zeros_like(gcg_ref)

        _fill_ext(ext_v, vp, vc, vn, i, n, TF)
        _fill_ext(ext_g, gp, gc, gn, i, n, TF)
        val = _conv3(ext_v, wv_ref, bv_ref, TF)
        gate = _conv3(ext_g, wg_ref, bg_ref, TF)
        d_a = _dot_nt(dh_ref[...].astype(BF16), wd_ref[...])
        sg = _sigmoid(gate)
        d_val = d_a * (gate * sg)
        d_gate = d_a * val * (sg * (1.0 + gate * (1.0 - sg)))
        dv_ref[...] = d_val
        dg_ref[...] = d_gate
        for dref, ext, gref in ((d_val, ext_v, gcv_ref), (d_gate, ext_g, gcg_ref)):
            for j in range(3):
                gref[j:j + 1, :] += jnp.sum(dref * ext[pl.ds(HALO - 1 + j, TF), :], axis=0, keepdims=True)
            gref[3:4, :] += jnp.sum(dref, axis=0, keepdims=True)

    cw = lambda off: pl.BlockSpec((3, FF_BLK), lambda k, i: (0, k + off))
    cb = lambda off: pl.BlockSpec((1, FF_BLK), lambda k, i: (0, k + off))
    blk = pl.BlockSpec((TF, FF_BLK), lambda k, i: (i, k))
    acc = pl.BlockSpec((4, FF_BLK), lambda k, i: (0, k))
    return pl.pallas_call(
        body, name="ffn_act_bwd", grid=(nk, n),
        in_specs=_halo_specs_2d(TF, FF_BLK, L, lambda k: k, "ki") + _halo_specs_2d(TF, FF_BLK, L, lambda k: k + nk, "ki")
        + [cw(0), cw(nk), cb(0), cb(nk), pl.BlockSpec((FF_BLK, D), lambda k, i: (k, 0)),
           pl.BlockSpec((TF, D), lambda k, i: (i, 0))],
        out_specs=[blk, blk, acc, acc],
        out_shape=[jax.ShapeDtypeStruct((L, D_FF), F32), jax.ShapeDtypeStruct((L, D_FF), F32),
                   jax.ShapeDtypeStruct((4, D_FF), F32), jax.ShapeDtypeStruct((4, D_FF), F32)],
        scratch_shapes=[pltpu.VMEM((TF + 2 * HALO, FF_BLK), F32), pltpu.VMEM((TF + 2 * HALO, FF_BLK), F32)],
        compiler_params=_cp("arbitrary", "arbitrary"))(
            up, up, up, up, up, up, conv_w, conv_w, conv_b, conv_b, w_down_b, dh2)


def _ffn_up_bwd(d_val, d_gate, conv_w, w_up4, h1, dh2, g_ffn):
    L, D = h1.shape
    n = L // TF
    nk = D_FF // FF_BLK

    def body(vp, vc, vn, gp, gc, gn, wv_ref, wg_ref, uv_ref, ug_ref, h1_ref, dh2_ref, g_ref,
             duv_ref, dug_ref, dh1_ref, gg_ref, ext_v, ext_g, acc_ref):
        i = pl.program_id(0)
        k = pl.program_id(1)

        @pl.when((i == 0) & (k == 0))
        def _():
            gg_ref[...] = jnp.zeros_like(gg_ref)

        @pl.when(k == 0)
        def _():
            acc_ref[...] = jnp.zeros_like(acc_ref)

        _fill_ext(ext_v, vp, vc, vn, i, n, TF)
        _fill_ext(ext_g, gp, gc, gn, i, n, TF)
        for ext, w_ref, o_ref, wu_ref in ((ext_v, wv_ref, duv_ref, uv_ref), (ext_g, wg_ref, dug_ref, ug_ref)):
            d_up = (ext[pl.ds(HALO + 1, TF), :] * w_ref[0:1, :] + ext[pl.ds(HALO, TF), :] * w_ref[1:2, :]
                    + ext[pl.ds(HALO - 1, TF), :] * w_ref[2:3, :]).astype(BF16)
            o_ref[...] = d_up
            acc_ref[...] += _dot_nt(d_up, wu_ref[...])

        @pl.when(k == nk - 1)
        def _():
            g = g_ref[...]
            _, xh, inv = _rms_fwd(h1_ref[...], g)
            dx, dg = _rms_bwd(acc_ref[...], xh, inv, g)
            gg_ref[...] += dg
            dh1_ref[...] = dh2_ref[...] + dx

    row = pl.BlockSpec((TF, D), lambda i, k: (i, 0))
    cw = lambda off: pl.BlockSpec((3, FF_BLK), lambda i, k: (0, k + off))
    wu = lambda off: pl.BlockSpec((None, D, FF_BLK), lambda i, k: (k + off, 0, 0))
    blk = pl.BlockSpec((TF, FF_BLK), lambda i, k: (i, k))
    return pl.pallas_call(
        body, name="ffn_up_bwd", grid=(n, nk),
        in_specs=_halo_specs_2d(TF, FF_BLK, L, lambda k: k, "ik") + _halo_specs_2d(TF, FF_BLK, L, lambda k: k, "ik")
        + [cw(0), cw(nk), wu(0), wu(nk), row, row, _full((1, D))],
        out_specs=[blk, blk, row, _full((1, D))],
        out_shape=[jax.ShapeDtypeStruct((L, D_FF), BF16), jax.ShapeDtypeStruct((L, D_FF), BF16),
                   jax.ShapeDtypeStruct((L, D), F32), jax.ShapeDtypeStruct((1, D), F32)],
        scratch_shapes=[pltpu.VMEM((TF + 2 * HALO, FF_BLK), F32), pltpu.VMEM((TF + 2 * HALO, FF_BLK), F32),
                        pltpu.VMEM((TF, D), F32)],
        compiler_params=_cp("arbitrary", "arbitrary"))(
            d_val, d_val, d_val, d_gate, d_gate, d_gate, conv_w, conv_w, w_up4, w_up4, h1, dh2, g_ffn)


def _matmul_tn(a, b, tm, tn, name, tk=512):
    L, M = a.shape
    N = b.shape[1]
    a = a.astype(BF16)
    b = b.astype(BF16)

    def body(a_ref, b_ref, o_ref):
        @pl.when(pl.program_id(2) == 0)
        def _():
            o_ref[...] = jnp.zeros_like(o_ref)

        o_ref[...] += _dot_tn(a_ref[...], b_ref[...])

    return pl.pallas_call(
        body, name=name, grid=(M // tm, N // tn, L // tk),
        in_specs=[pl.BlockSpec((tk, tm), lambda m, n, l: (l, m)), pl.BlockSpec((tk, tn), lambda m, n, l: (l, n))],
        out_specs=pl.BlockSpec((tm, tn), lambda m, n, l: (m, n)),
        out_shape=jax.ShapeDtypeStruct((M, N), F32),
        compiler_params=_cp("parallel", "parallel", "arbitrary"))(a, b)


def _row_tile(rows):
    for t in (512, 352, 256, 128, 64, 8):
        if rows % t == 0:
            return t
    return rows


def _add_half(g, r, c_arr, name):
    _, _, R, C = g.shape
    tr = _row_tile(R)

    def body(c_ref, g_ref, r_ref, o_ref):
        o_ref[...] = g_ref[...] + r_ref[...]

    return pl.pallas_call(
        body, name=name,
        grid_spec=pltpu.PrefetchScalarGridSpec(
            num_scalar_prefetch=1, grid=(4, R // tr),
            in_specs=[pl.BlockSpec((None, None, tr, C), lambda j, i, c: (j, c[0], i, 0)),
                      pl.BlockSpec((None, tr, C), lambda j, i, c: (j, i, 0))],
            out_specs=pl.BlockSpec((None, tr, C), lambda j, i, c: (j, i, 0))),
        out_shape=jax.ShapeDtypeStruct(r.shape, F32),
        compiler_params=_cp("parallel", "parallel"))(c_arr, g, r)


def _add2(a, b, name):
    R, C = a.shape
    tr = _row_tile(R)

    def body(a_ref, b_ref, o_ref):
        o_ref[...] = a_ref[...] + b_ref[...]

    spec = pl.BlockSpec((tr, C), lambda i: (i, 0))
    return pl.pallas_call(body, name=name, grid=(R // tr,), in_specs=[spec, spec], out_specs=spec,
                          out_shape=jax.ShapeDtypeStruct(a.shape, F32), compiler_params=_cp("parallel"))(a, b)


def _sum4(p, name):
    _, R, C = p.shape
    tr = _row_tile(R)

    def body(p_ref, o_ref):
        o_ref[...] = ((p_ref[0] + p_ref[1]) + p_ref[2]) + p_ref[3]

    return pl.pallas_call(
        body, name=name, grid=(R // tr,),
        in_specs=[pl.BlockSpec((4, tr, C), lambda i: (0, i, 0))],
        out_specs=pl.BlockSpec((tr, C), lambda i: (i, 0)),
        out_shape=jax.ShapeDtypeStruct((R, C), F32), compiler_params=_cp("parallel"))(p)


def _adamw(w, g, m, v, name):
    R, C = w.shape
    tr = _row_tile(R)

    def body(w_ref, g_ref, m_ref, v_ref, d_ref, nm_ref, nv_ref):
        gv = g_ref[...]
        nm = ADAM_B1 * m_ref[...] + (1.0 - ADAM_B1) * gv
        nv = ADAM_B2 * v_ref[...] + (1.0 - ADAM_B2) * (gv * gv)
        m_hat = nm / (1.0 - ADAM_B1 ** ADAM_STEP)
        v_hat = nv / (1.0 - ADAM_B2 ** ADAM_STEP)
        d_ref[...] = -ADAM_LR * (m_hat / (jnp.sqrt(v_hat) + ADAM_EPS) + ADAM_WD * w_ref[...])
        nm_ref[...] = nm
        nv_ref[...] = nv

    spec = pl.BlockSpec((tr, C), lambda i: (i, 0))
    sh = jax.ShapeDtypeStruct((R, C), F32)
    return pl.pallas_call(body, name=name, grid=(R // tr,), in_specs=[spec] * 4, out_specs=[spec] * 3,
                          out_shape=[sh] * 3, compiler_params=_cp("parallel"))(w, g, m, v)


_ANY = pl.BlockSpec(memory_space=pl.ANY)


def _position():
    return lax.axis_index("x"), lax.axis_index("y"), lax.axis_index("c")


def _sibling_swap(arrs, name, half=False):
    n = len(arrs)
    out_shape = [jax.ShapeDtypeStruct(a.shape[:1] + a.shape[2:] if half else a.shape, a.dtype) for a in arrs]

    def body(*refs):
        ins, outs = refs[:n], refs[n:2 * n]
        send_sems, recv_sems = refs[2 * n:]
        x, y, c = _position()
        copies = []
        for k in range(n):
            src = ins[k].at[:, 1 - c] if half else ins[k]
            copies.append(pltpu.make_async_remote_copy(
                src_ref=src, dst_ref=outs[k], send_sem=send_sems.at[k], recv_sem=recv_sems.at[k],
                device_id=(x, y, 1 - c), device_id_type=MESH))
        for cp in copies:
            cp.start()
        for cp in copies:
            cp.wait()

    return pl.pallas_call(
        body, name=name, in_specs=[_ANY] * n, out_specs=[_ANY] * n, out_shape=out_shape,
        scratch_shapes=[pltpu.SemaphoreType.DMA((n,)), pltpu.SemaphoreType.DMA((n,))],
        compiler_params=pltpu.CompilerParams(has_side_effects=True))(*arrs)


def _chip_exchange(arrs, name, scatter):
    n = len(arrs)
    out_shape = [jax.ShapeDtypeStruct(a.shape if scatter else (4,) + a.shape, a.dtype) for a in arrs]

    def body(*refs):
        ins, outs = refs[:n], refs[n:2 * n]
        send_sems, recv_sems, local_sems = refs[2 * n:]
        x, y, c = _position()
        me = 2 * x + y
        local, sent, landed = [], [], []
        for k in range(n):
            lc = pltpu.make_async_copy(ins[k].at[me] if scatter else ins[k], outs[k].at[me], local_sems.at[k])
            lc.start()
            local.append(lc)
            for d in (1, 2, 3):
                j = jnp.bitwise_xor(me, d)
                peer = (jnp.right_shift(j, 1), jnp.bitwise_and(j, 1), c)
                s = 3 * k + d - 1
                cp = pltpu.make_async_remote_copy(
                    src_ref=ins[k].at[j] if scatter else ins[k], dst_ref=outs[k].at[me],
                    send_sem=send_sems.at[s], recv_sem=recv_sems.at[s], device_id=peer, device_id_type=MESH)
                cp.start()
                sent.append(cp)
                landed.append(pltpu.make_async_remote_copy(
                    src_ref=ins[k].at[j] if scatter else ins[k], dst_ref=outs[k].at[j],
                    send_sem=send_sems.at[s], recv_sem=recv_sems.at[s], device_id=peer, device_id_type=MESH))
        for cp in sent:
            cp.wait_send()
        for cp in landed:
            cp.wait_recv()
        for lc in local:
            lc.wait()

    return pl.pallas_call(
        body, name=name, in_specs=[_ANY] * n, out_specs=[_ANY] * n, out_shape=out_shape,
        scratch_shapes=[pltpu.SemaphoreType.DMA((3 * n,)), pltpu.SemaphoreType.DMA((3 * n,)),
                        pltpu.SemaphoreType.DMA((n,))],
        compiler_params=pltpu.CompilerParams(has_side_effects=True))(*arrs)


def _pair_gather(arrs, name):
    n = len(arrs)
    out_shape = [jax.ShapeDtypeStruct((2,) + a.shape, a.dtype) for a in arrs]

    def body(*refs):
        ins, outs = refs[:n], refs[n:2 * n]
        send_sems, recv_sems, local_sems = refs[2 * n:]
        x, y, c = _position()
        local, remote = [], []
        for k in range(n):
            lc = pltpu.make_async_copy(ins[k], outs[k].at[c], local_sems.at[k])
            lc.start()
            local.append(lc)
            cp = pltpu.make_async_remote_copy(
                src_ref=ins[k], dst_ref=outs[k].at[c], send_sem=send_sems.at[k], recv_sem=recv_sems.at[k],
                device_id=(x, y, 1 - c), device_id_type=MESH)
            cp.start()
            remote.append(cp)
        for k in range(n):
            remote[k].wait_send()
            pltpu.make_async_remote_copy(
                src_ref=ins[k], dst_ref=outs[k].at[1 - c], send_sem=send_sems.at[k], recv_sem=recv_sems.at[k],
                device_id=(x, y, 1 - c), device_id_type=MESH).wait_recv()
            local[k].wait()

    return pl.pallas_call(
        body, name=name, in_specs=[_ANY] * n, out_specs=[_ANY] * n, out_shape=out_shape,
        scratch_shapes=[pltpu.SemaphoreType.DMA((n,)), pltpu.SemaphoreType.DMA((n,)), pltpu.SemaphoreType.DMA((n,))],
        compiler_params=pltpu.CompilerParams(has_side_effects=True))(*arrs)


def _pack(arrs, row_multiple):
    parts = []
    for a in arrs:
        flat = a.reshape(-1).astype(F32)
        pad = (-flat.shape[0]) % LANES
        parts.append(jnp.pad(flat, (0, pad)) if pad else flat)
    flat = jnp.concatenate(parts)
    rows = -(-flat.shape[0] // LANES)
    rows_p = -(-rows // row_multiple) * row_multiple
    return jnp.pad(flat, (0, rows_p * LANES - flat.shape[0])).reshape(rows_p, LANES)


def _unpack(packed, shapes):
    flat = packed.reshape(-1)
    outs, off = [], 0
    for sh in shapes:
        size = int(np.prod(sh))
        outs.append(flat[off:off + size].reshape(sh))
        off += size + (-size) % LANES
    return outs


SMALL = ["norm_mix_g", "pool_w", "pool_scale", "ssm_log_neg_a_re", "ssm_a_im", "ssm_log_dt", "ssm_b_re", "ssm_b_im",
         "ssm_c_re", "ssm_c_im", "ssm_d", "glu_b", "out_norm_pool_g", "out_norm_ssm_g", "norm_ffn_g", "conv_b",
         "final_norm_g"]
BIG = ["w_in", "glu_w", "w_out", "w_up", "w_down"]
WEIGHTS = ['norm_mix_g', 'w_in', 'pool_w', 'pool_scale', 'ssm_log_neg_a_re', 'ssm_a_im', 'ssm_log_dt', 'ssm_b_re',
           'ssm_b_im', 'ssm_c_re', 'ssm_c_im', 'ssm_d', 'glu_w', 'glu_b', 'out_norm_pool_g', 'out_norm_ssm_g', 'w_out',
           'norm_ffn_g', 'w_up', 'conv_w', 'conv_b', 'w_down', 'final_norm_g']


def _local_step(x, target, p, full):
    L, D = x.shape
    row = lambda a: a.reshape(1, -1)
    w_in, glu_w, w_out, w_up4, w_down, conv_w = (full[k] for k in ("w_in", "glu_w", "w_out", "w_up", "w_down", "conv_w"))
    pool_w_b = p["pool_w"].astype(BF16)
    g_mix, g_pool, g_ssm, g_ffn, g_fin = (row(p[k]) for k in (
        "norm_mix_g", "out_norm_pool_g", "out_norm_ssm_g", "norm_ffn_g", "final_norm_g"))
    pool_scale, ssm_d, glu_b, conv_b = (row(p[k]) for k in ("pool_scale", "ssm_d", "glu_b", "conv_b"))

    lnar = p["ssm_log_neg_a_re"].reshape(2 * N_SSM_GROUPS, SSM_STATE)
    aim = p["ssm_a_im"].reshape(2 * N_SSM_GROUPS, SSM_STATE)
    ldt = jnp.broadcast_to(p["ssm_log_dt"].reshape(2 * N_SSM_GROUPS, 1), lnar.shape)
    lam_re, lam_im, f_re, f_im = _ssm_params(lnar, aim, ldt)
    flat2 = lambda a: a.reshape(2, N_STATE)
    lam4 = jnp.stack([flat2(lam_re)[0], flat2(lam_im)[0], flat2(lam_re)[1], flat2(lam_im)[1]])
    tables = _scan_tables(lam4)
    f2 = [jnp.stack([flat2(f_re)[d], flat2(f_im)[d]]) for d in range(2)]
    bexp = [[_expand_b(p[k][d]).astype(BF16) for k in ("ssm_b_re", "ssm_b_im")] for d in range(2)]
    cexp = [[_expand_c(p[k][d]).astype(BF16) for k in ("ssm_c_re", "ssm_c_im")] for d in range(2)]

    u, xn = _in_proj(x, g_mix, w_in)
    yn_pool = _pool_fwd(u, pool_w_b, pool_scale, g_pool)
    y0, s0r, s0i = _ssm_scan_fwd(u, bexp[0][0], bexp[0][1], cexp[0][0], cexp[0][1], f2[0], tables, 0, False)
    y1, s1r, s1i = _ssm_scan_fwd(u, bexp[1][0], bexp[1][1], cexp[1][0], cexp[1][1], f2[1], tables, 2, True)
    h1, hn, ycat = _mix_out(yn_pool, y0, y1, u, x, ssm_d, glu_w, glu_b, g_ssm, w_out, g_ffn)
    up = _ffn_up(hn, w_up4)
    a, dh2, loss, g_final = _ffn_down_loss(up, conv_w, conv_b, w_down, h1, target, g_fin)

    d_val, d_gate, gcv, gcg = _ffn_act_bwd(up, conv_w, conv_b, w_down, dh2)
    g_w_down = _matmul_tn(a, dh2, FF_BLK, D, "grad_w_down")
    d_up_v, d_up_g, dh1, g_ffn_g = _ffn_up_bwd(d_val, d_gate, conv_w, w_up4, h1, dh2, g_ffn)
    g_w_up = jnp.concatenate([
        _matmul_tn(hn, d_up_v, 512, FF_BLK, "grad_w_up_val").reshape(D, 2, FF_BLK),
        _matmul_tn(hn, d_up_g, 512, FF_BLK, "grad_w_up_gate").reshape(D, 2, FF_BLK)], axis=1)
    g_w_up = jnp.transpose(g_w_up, (1, 0, 2))
    g_w_out = _matmul_tn(ycat, dh1, 512, D, "grad_w_out")
    d_pooled, g_pool_w, g_pool_scale, g_pool_g = _pool_bwd_local(dh1, u, w_out, pool_w_b, pool_scale, g_pool)
    du_pool = _pool_bwd_window(d_pooled)
    dy, du_direct, g_glu_w, g_glu_b, g_ssm_d, g_ssm_g = _ssm_bwd_local(dh1, y0, y1, u, ssm_d, glu_w, glu_b, g_ssm, w_out)
    du0, gb0r, gb0i, gc0r, gc0i, gv0 = _ssm_scan_bwd(dy, u, s0r, s0i, bexp[0][0], bexp[0][1], cexp[0][0], cexp[0][1],
                                                     f2[0], tables, 1, True)
    du1, gb1r, gb1i, gc1r, gc1i, gv1 = _ssm_scan_bwd(dy, u, s1r, s1i, bexp[1][0], bexp[1][1], cexp[1][0], cexp[1][1],
                                                     f2[1], tables, 3, False)
    gvec = lambda j: jnp.stack([gv0[j], gv1[j]]).reshape(2 * N_SSM_GROUPS, SSM_STATE)
    g_lnar, g_aim, g_ldt = _ssm_params_bwd(lnar, aim, ldt, gvec(0), gvec(1), gvec(2), gvec(3))
    grad_x, d_u_b, g_mix_g = _in_bwd(du_pool, du_direct, du0, du1, dh1, x, g_mix, w_in)
    g_w_in = _matmul_tn(xn, d_u_b, 512, D, "grad_w_in")

    small = {
        "norm_mix_g": g_mix_g, "pool_w": g_pool_w, "pool_scale": g_pool_scale,
        "ssm_log_neg_a_re": g_lnar, "ssm_a_im": g_aim, "ssm_log_dt": g_ldt,
        "ssm_b_re": jnp.stack([_extract_b(gb0r), _extract_b(gb1r)]),
        "ssm_b_im": jnp.stack([_extract_b(gb0i), _extract_b(gb1i)]),
        "ssm_c_re": jnp.stack([_extract_c(gc0r), _extract_c(gc1r)]),
        "ssm_c_im": jnp.stack([_extract_c(gc0i), _extract_c(gc1i)]),
        "ssm_d": g_ssm_d, "glu_b": g_glu_b, "out_norm_pool_g": g_pool_g, "out_norm_ssm_g": g_ssm_g,
        "norm_ffn_g": g_ffn_g, "conv_b": jnp.concatenate([gcv[3], gcg[3]]), "final_norm_g": g_final,
        "conv_w": jnp.concatenate([gcv[0:3], gcg[0:3]], axis=1),
    }
    big = {"w_in": g_w_in, "glu_w": g_glu_w, "w_out": g_w_out, "w_up": g_w_up, "w_down": g_w_down}
    return loss, grad_x, small, big


def kernel(x, norm_mix_g, w_in, pool_w, pool_scale, ssm_log_neg_a_re, ssm_a_im, ssm_log_dt, ssm_b_re, ssm_b_im, ssm_c_re, ssm_c_im, ssm_d, glu_w, glu_b, out_norm_pool_g, out_norm_ssm_g, w_out, norm_ffn_g, w_up, conv_w, conv_b, w_down, final_norm_g, loss_target, m_norm_mix_g, m_w_in, m_pool_w, m_pool_scale, m_ssm_log_neg_a_re, m_ssm_a_im, m_ssm_log_dt, m_ssm_b_re, m_ssm_b_im, m_ssm_c_re, m_ssm_c_im, m_ssm_d, m_glu_w, m_glu_b, m_out_norm_pool_g, m_out_norm_ssm_g, m_w_out, m_norm_ffn_g, m_w_up, m_conv_w, m_conv_b, m_w_down, m_final_norm_g, v_norm_mix_g, v_w_in, v_pool_w, v_pool_scale, v_ssm_log_neg_a_re, v_ssm_a_im, v_ssm_log_dt, v_ssm_b_re, v_ssm_b_im, v_ssm_c_re, v_ssm_c_im, v_ssm_d, v_glu_w, v_glu_b, v_out_norm_pool_g, v_out_norm_ssm_g, v_w_out, v_norm_ffn_g, v_w_up, v_conv_w, v_conv_b, v_w_down, v_final_norm_g):
    args = locals()
    w = {k: args[k] for k in WEIGHTS}
    m = {k: args["m_" + k] for k in WEIGHTS}
    v = {k: args["v_" + k] for k in WEIGHTS}
    chip = 2 * lax.axis_index("x") + lax.axis_index("y")
    c_arr = lax.axis_index("c").astype(jnp.int32).reshape(1)

    gathered = _chip_exchange([w[k].astype(BF16) for k in BIG] + [conv_w], "gather_weights", scatter=False)
    full = {k: g.reshape((-1,) + g.shape[2:]) for k, g in zip(BIG, gathered)}
    full["w_up"] = gathered[3]
    full["conv_w"] = jnp.transpose(gathered[5], (1, 0, 2)).reshape(3, -1)

    loss, grad_x, g_small, g_big = _local_step(x[0], loss_target[0], w, full)
    loss = lax.psum(loss[0, 0], ("x", "y", "c"))

    halves = []
    for k in BIG:
        g = g_big[k]
        halves.append(g.reshape(4, 2, g.shape[-2] // (2 if k == "w_up" else 8), g.shape[-1]))
    from_sibling = _sibling_swap(halves, "reduce_pair", half=True)
    chip_sums = [_add_half(h, r, c_arr, "sum_pair_" + k) for k, h, r in zip(BIG, halves, from_sibling)]
    from_chips = _chip_exchange(chip_sums, "reduce_chips", scatter=True)
    mine = [_sum4(r, "sum_chips_" + k) for k, r in zip(BIG, from_chips)]
    shards = _pair_gather(mine, "gather_halves")
    grads = {k: s.reshape(w[k].shape) for k, s in zip(BIG, shards)}

    packed = _pack([g_small[k] for k in SMALL] + [g_small["conv_w"]], 512)
    pair = _add2(packed, _sibling_swap([packed], "allreduce_pair")[0], "sum_small_pair")
    total = _sum4(_chip_exchange([pair], "allreduce_chips", scatter=False)[0], "sum_small_chips")
    shapes = [w[k].shape for k in SMALL] + [(3, 4 * FF_BLK)]
    for k, g in zip(SMALL + ["conv_w_full"], _unpack(total, shapes)):
        grads[k] = g
    grads["conv_w"] = lax.dynamic_slice_in_dim(grads.pop("conv_w_full"), chip * FF_BLK, FF_BLK, axis=1)

    delta, new_m, new_v = {}, {}, {}
    for k in BIG:
        delta[k], new_m[k], new_v[k] = _adamw(w[k], grads[k], m[k], v[k], "adamw_" + k)
    small_keys = SMALL + ["conv_w"]
    packs = [_pack([d[k] for k in small_keys], 512) for d in (w, grads, m, v)]
    outs = _adamw(*packs, "adamw_small")
    small_shapes = [w[k].shape for k in small_keys]
    for d, o in zip((delta, new_m, new_v), outs):
        for k, a in zip(small_keys, _unpack(o, small_shapes)):
            d[k] = a

    return (loss, grad_x[None], *[grads[k] for k in WEIGHTS], *[delta[k] for k in WEIGHTS],
            *[new_m[k] for k in WEIGHTS], *[new_v[k] for k in WEIGHTS])
```

```python
import numpy as np
import jax
import jax.numpy as jnp
from jax import lax
from jax.experimental import pallas as pl
from jax.experimental.pallas import tpu as pltpu

F32 = jnp.float32
BF16 = jnp.bfloat16
MESH = pl.DeviceIdType.MESH

EPS = 1e-6
POOL_WINDOWS = (2, 4, 8, 16)
POOL_GROUP = 128
SSM_GROUP = 16
SSM_STATE = 64
N_SSM_GROUPS = 32
N_STATE = N_SSM_GROUPS * SSM_STATE
QUAD = 256
N_QUAD = N_STATE // QUAD
SLAB = 256
D_SSM = 512
D_POOL = 512
D_FF = 2816
FF_BLK = 1408
HALO = 8
LANES = 128
ADAM_LR, ADAM_B1, ADAM_B2, ADAM_EPS, ADAM_WD, ADAM_STEP = 0.001, 0.9, 0.999, 1e-08, 0.01, 10
VMEM_LIMIT = 56 * 2 ** 20

TL = 512
TF = 256
TC = 256
SCAN_W = 512


def _cp(*sem):
    return pltpu.CompilerParams(dimension_semantics=sem, vmem_limit_bytes=VMEM_LIMIT)


def _dot_nn(a, b):
    return jnp.dot(a, b, preferred_element_type=F32)


def _dot_nt(a, b):
    return lax.dot_general(a, b, (((1,), (1,)), ((), ())), preferred_element_type=F32)


def _dot_tn(a, b):
    return lax.dot_general(a, b, (((0,), (0,)), ((), ())), preferred_element_type=F32)


def _rms_fwd(x, g):
    inv = lax.rsqrt(jnp.mean(x * x, axis=-1, keepdims=True) + EPS)
    xh = x * inv
    return xh * g, xh, inv


def _rms_bwd(dy, xh, inv, g):
    dg = jnp.sum(dy * xh, axis=0, keepdims=True)
    dxh = dy * g
    dx = inv * (dxh - xh * jnp.mean(dxh * xh, axis=-1, keepdims=True))
    return dx, dg


_GELU_C = 0.7978845608028654
_GELU_A = 0.044715


def _gelu(y):
    t = jnp.tanh(_GELU_C * (y + _GELU_A * (y * y * y)))
    return 0.5 * y * (1.0 + t), t


def _gelu_grad(y, t):
    return 0.5 * (1.0 + t) + 0.5 * y * (1.0 - t * t) * (_GELU_C * (1.0 + 3.0 * _GELU_A * y * y))


def _sigmoid(x):
    return 1.0 / (1.0 + jnp.exp(-x))


def _full(shape):
    n = len(shape)
    return pl.BlockSpec(shape, lambda *_: (0,) * n)


def _fill_ext(ext_ref, prev_ref, cur_ref, next_ref, i, n, rows):
    ext_ref[0:HALO, :] = jnp.where(i > 0, prev_ref[...], 0.0).astype(ext_ref.dtype)
    ext_ref[HALO:HALO + rows, :] = cur_ref[...]
    ext_ref[HALO + rows:2 * HALO + rows, :] = jnp.where(i < n - 1, next_ref[...], 0.0).astype(ext_ref.dtype)


def _in_proj(x, g, w):
    L, D = x.shape
    E = w.shape[1]

    def body(x_ref, g_ref, w_ref, u_ref, xn_ref):
        y, _, _ = _rms_fwd(x_ref[...], g_ref[...])
        yb = y.astype(BF16)
        xn_ref[...] = yb
        u_ref[...] = _dot_nn(yb, w_ref[...])

    return pl.pallas_call(
        body, name="in_proj", grid=(L // TL,),
        in_specs=[pl.BlockSpec((TL, D), lambda i: (i, 0)), _full((1, D)), _full(w.shape)],
        out_specs=[pl.BlockSpec((TL, E), lambda i: (i, 0)), pl.BlockSpec((TL, D), lambda i: (i, 0))],
        out_shape=[jax.ShapeDtypeStruct((L, E), F32), jax.ShapeDtypeStruct((L, D), BF16)],
        compiler_params=_cp("parallel"))(x, g, w)


def _halo_specs_1d(rows, width, L, col):
    rb = rows // HALO
    last = L // HALO - 1
    return [pl.BlockSpec((HALO, width), lambda i: (jnp.maximum(i * rb - 1, 0), col)),
            pl.BlockSpec((rows, width), lambda i: (i, col)),
            pl.BlockSpec((HALO, width), lambda i: (jnp.minimum((i + 1) * rb, last), col))]


def _pooled_from_ext(ext_ref, t0, rows, L):
    t = t0 + lax.broadcasted_iota(jnp.int32, (rows, 1), 0)
    outs = []
    for gi, w in enumerate(POOL_WINDOWS):
        half = w // 2
        cs = slice(gi * POOL_GROUP, (gi + 1) * POOL_GROUP)
        acc = ext_ref[pl.ds(HALO - half, rows), cs]
        for s in range(-half + 1, half):
            acc = acc + ext_ref[pl.ds(HALO + s, rows), cs]
        cnt = (jnp.minimum(t + half, L) - jnp.maximum(t - half, 0)).astype(F32)
        outs.append(acc / cnt - ext_ref[pl.ds(HALO, rows), cs])
    return outs


def _pool_fwd(u, pool_w_b, pool_scale, g_pool):
    L = u.shape[0]
    n = L // TL

    def body(prev_ref, cur_ref, next_ref, pw_ref, ps_ref, g_ref, out_ref, ext_ref):
        i = pl.program_id(0)
        _fill_ext(ext_ref, prev_ref, cur_ref, next_ref, i, n, TL)
        pooled = _pooled_from_ext(ext_ref, i * TL, TL, L)
        ypre = jnp.concatenate([_dot_nn(pooled[gi].astype(BF16), pw_ref[gi]) for gi in range(4)], axis=-1)
        yn, _, _ = _rms_fwd(ypre * ps_ref[...], g_ref[...])
        out_ref[...] = yn.astype(BF16)

    return pl.pallas_call(
        body, name="pool_fwd", grid=(n,),
        in_specs=_halo_specs_1d(TL, D_POOL, L, 0) + [_full(pool_w_b.shape), _full((1, D_POOL)), _full((1, D_POOL))],
        out_specs=pl.BlockSpec((TL, D_POOL), lambda i: (i, 0)),
        out_shape=jax.ShapeDtypeStruct((L, D_POOL), BF16),
        scratch_shapes=[pltpu.VMEM((TL + 2 * HALO, D_POOL), F32)],
        compiler_params=_cp("parallel"))(u, u, u, pool_w_b, pool_scale, g_pool)


def _pool_bwd_local(dh1, u, w_out_b, pool_w_b, pool_scale, g_pool, comm=None):
    L = u.shape[0]
    n = L // TL
    D = dh1.shape[1]

    def body(dh_ref, prev_ref, cur_ref, next_ref, wo_ref, pw_ref, ps_ref, g_ref,
             dp_ref, gpw_ref, gps_ref, gg_ref, ext_ref):
        i = pl.program_id(0)

        @pl.when(i == 0)
        def _():
            gpw_ref[...] = jnp.zeros_like(gpw_ref)
            gps_ref[...] = jnp.zeros_like(gps_ref)
            gg_ref[...] = jnp.zeros_like(gg_ref)

        _fill_ext(ext_ref, prev_ref, cur_ref, next_ref, i, n, TL)
        pooled = [p.astype(BF16) for p in _pooled_from_ext(ext_ref, i * TL, TL, L)]
        ypre = jnp.concatenate([_dot_nn(pooled[gi], pw_ref[gi]) for gi in range(4)], axis=-1)
        ps = ps_ref[...]
        g = g_ref[...]
        _, xh, inv = _rms_fwd(ypre * ps, g)
        d_yn = _dot_nt(dh_ref[...].astype(BF16), wo_ref[...])
        d_y, dg = _rms_bwd(d_yn, xh, inv, g)
        gg_ref[...] += dg
        gps_ref[...] += jnp.sum(d_y * ypre, axis=0, keepdims=True)
        d_ypre = (d_y * ps).astype(BF16)
        for gi in range(4):
            cs = slice(gi * POOL_GROUP, (gi + 1) * POOL_GROUP)
            dp_ref[:, cs] = _dot_nt(d_ypre[:, cs], pw_ref[gi])
            gpw_ref[gi] += _dot_tn(pooled[gi], d_ypre[:, cs])

    return _hosted_call(
        body, comm, name="pool_bwd_local", grid=(n,),
        in_specs=[pl.BlockSpec((TL, D), lambda i: (i, 0))] + _halo_specs_1d(TL, D_POOL, L, 0)
        + [pl.BlockSpec((D_POOL, D), lambda i: (0, 0)), _full(pool_w_b.shape), _full((1, D_POOL)), _full((1, D_POOL))],
        out_specs=[pl.BlockSpec((TL, D_POOL), lambda i: (i, 0)), _full(pool_w_b.shape),
                   _full((1, D_POOL)), _full((1, D_POOL))],
        out_shape=[jax.ShapeDtypeStruct((L, D_POOL), F32), jax.ShapeDtypeStruct(pool_w_b.shape, F32),
                   jax.ShapeDtypeStruct((1, D_POOL), F32), jax.ShapeDtypeStruct((1, D_POOL), F32)],
        scratch_shapes=[pltpu.VMEM((TL + 2 * HALO, D_POOL), F32)],
        args=(dh1, u, u, u, w_out_b, pool_w_b, pool_scale, g_pool))


def _pool_bwd_window(d_pooled):
    L = d_pooled.shape[0]
    n = L // TL
    R = TL + 2 * HALO

    def body(prev_ref, cur_ref, next_ref, out_ref, ext_ref, q_ref):
        i = pl.program_id(0)
        _fill_ext(ext_ref, prev_ref, cur_ref, next_ref, i, n, TL)
        tr = i * TL - HALO + lax.broadcasted_iota(jnp.int32, (R, 1), 0)
        for gi, w in enumerate(POOL_WINDOWS):
            half = w // 2
            cs = slice(gi * POOL_GROUP, (gi + 1) * POOL_GROUP)
            cnt = jnp.maximum(jnp.minimum(tr + half, L) - jnp.maximum(tr - half, 0), 1).astype(F32)
            q_ref[:, cs] = ext_ref[:, cs] / cnt
        for gi, w in enumerate(POOL_WINDOWS):
            half = w // 2
            cs = slice(gi * POOL_GROUP, (gi + 1) * POOL_GROUP)
            acc = q_ref[pl.ds(HALO - half + 1, TL), cs]
            for s in range(-half + 2, half + 1):
                acc = acc + q_ref[pl.ds(HALO + s, TL), cs]
            out_ref[:, cs] = acc - ext_ref[pl.ds(HALO, TL), cs]

    return pl.pallas_call(
        body, name="pool_bwd_window", grid=(n,),
        in_specs=_halo_specs_1d(TL, D_POOL, L, 0),
        out_specs=pl.BlockSpec((TL, D_POOL), lambda i: (i, 0)),
        out_shape=jax.ShapeDtypeStruct((L, D_POOL), F32),
        scratch_shapes=[pltpu.VMEM((R, D_POOL), F32), pltpu.VMEM((R, D_POOL), F32)],
        compiler_params=_cp("parallel"))(d_pooled, d_pooled, d_pooled)


def _ssm_param_fn(lnar, aim, ldt):
    dt = jnp.exp(ldt)
    a_re = -jnp.exp(lnar)
    mag = jnp.exp(a_re * dt)
    ang = aim * dt
    lr, li = mag * jnp.cos(ang), mag * jnp.sin(ang)
    den = a_re * a_re + aim * aim
    fr = ((lr - 1.0) * a_re + li * aim) / den
    fi = (li * a_re - (lr - 1.0) * aim) / den
    return lr, li, fr, fi


def _ssm_params(lnar, aim, ldt):
    def body(a_ref, b_ref, c_ref, lr_ref, li_ref, fr_ref, fi_ref):
        lr, li, fr, fi = _ssm_param_fn(a_ref[...], b_ref[...], c_ref[...])
        lr_ref[...] = lr
        li_ref[...] = li
        fr_ref[...] = fr
        fi_ref[...] = fi

    sh = jax.ShapeDtypeStruct(lnar.shape, F32)
    return pl.pallas_call(body, name="ssm_params", out_shape=[sh] * 4)(lnar, aim, ldt)


def _ssm_params_bwd(lnar, aim, ldt, glr, gli, gfr, gfi):
    def body(a_ref, b_ref, c_ref, g0, g1, g2, g3, da_ref, db_ref, dc_ref):
        _, vjp = jax.vjp(_ssm_param_fn, a_ref[...], b_ref[...], c_ref[...])
        da, db, dc = vjp((g0[...], g1[...], g2[...], g3[...]))
        da_ref[...] = da
        db_ref[...] = db
        dc_ref[...] = jnp.sum(dc, axis=1, keepdims=True)

    return pl.pallas_call(
        body, name="ssm_params_bwd",
        out_shape=[jax.ShapeDtypeStruct(lnar.shape, F32), jax.ShapeDtypeStruct(aim.shape, F32),
                   jax.ShapeDtypeStruct((ldt.shape[0], 1), F32)])(lnar, aim, ldt, glr, gli, gfr, gfi)


def _scan_tables(lam4):
    def build(lr, li, reverse, out_ref, k):
        row = lax.broadcasted_iota(jnp.int32, (8, N_STATE), 0)
        lrb = jnp.broadcast_to(lr, (8, N_STATE))
        lib = jnp.broadcast_to(li, (8, N_STATE))
        pr, pi = lrb, lib
        for s, sh in enumerate((1, 2, 4)):
            mask = (row < 8 - sh) if reverse else (row >= sh)
            out_ref[k, 2 * s] = jnp.where(mask, pr, 0.0)
            out_ref[k, 2 * s + 1] = jnp.where(mask, pi, 0.0)
            pr, pi = pr * pr - pi * pi, 2.0 * pr * pi
        pr, pi = lrb, lib
        p8r = jnp.zeros((8, N_STATE), F32)
        p8i = jnp.zeros((8, N_STATE), F32)
        for j in range(8):
            r = 7 - j if reverse else j
            p8r = jnp.where(row == r, pr, p8r)
            p8i = jnp.where(row == r, pi, p8i)
            pr, pi = pr * lrb - pi * lib, pr * lib + pi * lrb
        out_ref[k, 6] = p8r
        out_ref[k, 7] = p8i

    def body(lam_ref, out_ref):
        l0r, l0i, l1r, l1i = (lam_ref[j:j + 1, :] for j in range(4))
        build(l0r, l0i, False, out_ref, 0)
        build(l0r, -l0i, True, out_ref, 1)
        build(l1r, l1i, True, out_ref, 2)
        build(l1r, -l1i, False, out_ref, 3)

    return pl.pallas_call(body, name="scan_tables",
                          out_shape=jax.ShapeDtypeStruct((4, 8, 8, N_STATE), F32))(lam4)


_QUAD_MASK = np.zeros((N_QUAD, 16, 4), np.float32)
for _q in range(N_QUAD):
    for _gl in range(4):
        _QUAD_MASK[_q, 4 * (_q % 4) + _gl, _gl] = 1.0


def _expand_b(b):
    bq = jnp.transpose(b.reshape(N_QUAD, 4, SSM_STATE, SSM_GROUP), (0, 3, 1, 2))
    e = bq[:, None] * _QUAD_MASK[:, :, None, :, None]
    return e.reshape(N_QUAD, SLAB, QUAD)


def _extract_b(e):
    e = e.reshape(N_QUAD, 16, SSM_GROUP, 4, SSM_STATE) * _QUAD_MASK[:, :, None, :, None]
    return jnp.transpose(jnp.sum(e, axis=1), (0, 2, 3, 1)).reshape(N_SSM_GROUPS, SSM_STATE, SSM_GROUP)


def _expand_c(c):
    cq = jnp.transpose(c.reshape(N_QUAD, 4, SSM_GROUP, SSM_STATE), (0, 1, 3, 2))
    e = cq[:, :, :, None, :] * jnp.transpose(_QUAD_MASK, (0, 2, 1))[:, :, None, :, None]
    return e.reshape(N_QUAD, QUAD, SLAB)


def _extract_c(e):
    e = e.reshape(N_QUAD, 4, SSM_STATE, 16, SSM_GROUP) * jnp.transpose(_QUAD_MASK, (0, 2, 1))[:, :, None, :, None]
    return jnp.transpose(jnp.sum(e, axis=3), (0, 1, 3, 2)).reshape(N_SSM_GROUPS, SSM_GROUP, SSM_STATE)


def _scan_rows(src_re, src_im, dst_re, dst_im, tab_ref, k, carry_re, carry_im, rows, reverse, s_refs=None):
    ng = rows // 8
    edge = 0 if reverse else 7
    row_id = lax.broadcasted_iota(jnp.int32, (8, SCAN_W), 0)
    sums = []
    for lt in range(N_STATE // SCAN_W):
        sl = slice(lt * SCAN_W, (lt + 1) * SCAN_W)

        def step(r, c, sl=sl):
            tabs = [tab_ref[k, j, :, sl] for j in range(8)]
            cr, ci = c[0], c[1]
            row = pl.multiple_of((ng - 1 - r) * 8 if reverse else r * 8, 8)
            xr = src_re[pl.ds(row, 8), sl]
            xi = src_im[pl.ds(row, 8), sl]
            for s, sh in enumerate((1, 2, 4)):
                amt = 8 - sh if reverse else sh
                rr = pltpu.roll(xr, amt, 0)
                ri = pltpu.roll(xi, amt, 0)
                mr, mi = tabs[2 * s], tabs[2 * s + 1]
                xr, xi = xr + mr * rr - mi * ri, xi + mr * ri + mi * rr
            xr, xi = xr + tabs[6] * cr - tabs[7] * ci, xi + tabs[6] * ci + tabs[7] * cr
            dst_re[pl.ds(row, 8), sl] = xr
            dst_im[pl.ds(row, 8), sl] = xi
            ncr = jnp.broadcast_to(xr[edge:edge + 1, :], (8, SCAN_W))
            nci = jnp.broadcast_to(xi[edge:edge + 1, :], (8, SCAN_W))
            if s_refs is None:
                return ncr, nci
            amt = 7 if reverse else 1
            far = 7 if reverse else 0
            nr = jnp.where(row_id == far, cr, pltpu.roll(xr, amt, 0))
            ni = jnp.where(row_id == far, ci, pltpu.roll(xi, amt, 0))
            sr = s_refs[0][pl.ds(row, 8), sl]
            si = s_refs[1][pl.ds(row, 8), sl]
            return ncr, nci, c[2] + nr * sr + ni * si, c[3] + ni * sr - nr * si

        init = (carry_re[:, sl], carry_im[:, sl])
        if s_refs is not None:
            init = init + (jnp.zeros((8, SCAN_W), F32), jnp.zeros((8, SCAN_W), F32))
        out = lax.fori_loop(0, ng, step, init)
        carry_re[:, sl] = out[0]
        carry_im[:, sl] = out[1]
        if s_refs is not None:
            sums.append((jnp.sum(out[2], axis=0, keepdims=True), jnp.sum(out[3], axis=0, keepdims=True)))
    return sums


def _ssm_scan_fwd(u, b_re, b_im, c_re, c_im, f2, tables, k, reverse, comm=None):
    L = u.shape[0]
    nc = L // TC
    chunk = (lambda i: nc - 1 - i) if reverse else (lambda i: i)

    def body(u_ref, bre_ref, bim_ref, cre_ref, cim_ref, f_ref, tab_ref,
             y_ref, sre_ref, sim_ref, in_re, in_im, carry_re, carry_im):
        @pl.when(pl.program_id(0) == 0)
        def _():
            carry_re[...] = jnp.zeros_like(carry_re)
            carry_im[...] = jnp.zeros_like(carry_im)

        ub = u_ref[...].astype(BF16)
        for q in range(N_QUAD):
            qs = slice(q * QUAD, (q + 1) * QUAD)
            us = ub[:, (q // 4) * SLAB:(q // 4 + 1) * SLAB]
            bur = _dot_nn(us, bre_ref[q])
            bui = _dot_nn(us, bim_ref[q])
            fr = f_ref[0:1, qs]
            fi = f_ref[1:2, qs]
            in_re[:, qs] = fr * bur - fi * bui
            in_im[:, qs] = fr * bui + fi * bur
        _scan_rows(in_re, in_im, sre_ref, sim_ref, tab_ref, k, carry_re, carry_im, TC, reverse)
        for j in range(D_SSM // SLAB):
            acc = jnp.zeros((TC, SLAB), F32)
            for q in range(4 * j, 4 * j + 4):
                qs = slice(q * QUAD, (q + 1) * QUAD)
                acc = acc + _dot_nn(sre_ref[:, qs].astype(BF16), cre_ref[q])
                acc = acc - _dot_nn(sim_ref[:, qs].astype(BF16), cim_ref[q])
            y_ref[:, j * SLAB:(j + 1) * SLAB] = acc

    return _hosted_call(
        body, comm, name="ssm_scan_rev" if reverse else "ssm_scan_fwd", grid=(nc,),
        in_specs=[pl.BlockSpec((TC, D_SSM), lambda i: (chunk(i), 1))]
        + [_full(b_re.shape)] * 4 + [_full(f2.shape), _full(tables.shape)],
        out_specs=[pl.BlockSpec((TC, D_SSM), lambda i: (chunk(i), 0)),
                   pl.BlockSpec((TC, N_STATE), lambda i: (chunk(i), 0)),
                   pl.BlockSpec((TC, N_STATE), lambda i: (chunk(i), 0))],
        out_shape=[jax.ShapeDtypeStruct((L, D_SSM), F32), jax.ShapeDtypeStruct((L, N_STATE), F32),
                   jax.ShapeDtypeStruct((L, N_STATE), F32)],
        scratch_shapes=[pltpu.VMEM((TC, N_STATE), F32), pltpu.VMEM((TC, N_STATE), F32),
                        pltpu.VMEM((8, N_STATE), F32), pltpu.VMEM((8, N_STATE), F32)],
        args=(u, b_re, b_im, c_re, c_im, f2, tables))


def _ssm_scan_bwd(dy, u, s_re, s_im, b_re, b_im, c_re, c_im, f2, tables, k, reverse, comm=None):
    L = u.shape[0]
    nc = L // TC
    chunk = (lambda i: nc - 1 - i) if reverse else (lambda i: i)

    def body(dy_ref, u_ref, sre_ref, sim_ref, bre_ref, bim_ref, cre_ref, cim_ref, f_ref, tab_ref,
             du_ref, gbr_ref, gbi_ref, gcr_ref, gci_ref, gv_ref, a_re, a_im, carry_re, carry_im):
        @pl.when(pl.program_id(0) == 0)
        def _():
            carry_re[...] = jnp.zeros_like(carry_re)
            carry_im[...] = jnp.zeros_like(carry_im)
            for r in (gbr_ref, gbi_ref, gcr_ref, gci_ref, gv_ref):
                r[...] = jnp.zeros_like(r)

        dyb = dy_ref[...].astype(BF16)
        ub = u_ref[...].astype(BF16)
        for q in range(N_QUAD):
            qs = slice(q * QUAD, (q + 1) * QUAD)
            ds = dyb[:, (q // 4) * SLAB:(q // 4 + 1) * SLAB]
            a_re[:, qs] = _dot_nt(ds, cre_ref[q])
            a_im[:, qs] = -_dot_nt(ds, cim_ref[q])
            gcr_ref[q] += _dot_tn(sre_ref[:, qs].astype(BF16), ds)
            gci_ref[q] -= _dot_tn(sim_ref[:, qs].astype(BF16), ds)
        sums = _scan_rows(a_re, a_im, a_re, a_im, tab_ref, k, carry_re, carry_im, TC, reverse,
                          s_refs=(sre_ref, sim_ref))
        for lt, (glr, gli) in enumerate(sums):
            sl = slice(lt * SCAN_W, (lt + 1) * SCAN_W)
            gv_ref[0:1, sl] += glr
            gv_ref[1:2, sl] += gli
        for j in range(D_SSM // SLAB):
            us = ub[:, j * SLAB:(j + 1) * SLAB]
            acc = jnp.zeros((TC, SLAB), F32)
            for q in range(4 * j, 4 * j + 4):
                qs = slice(q * QUAD, (q + 1) * QUAD)
                ar = a_re[:, qs]
                ai = a_im[:, qs]
                bur = _dot_nn(us, bre_ref[q])
                bui = _dot_nn(us, bim_ref[q])
                gv_ref[2:3, qs] += jnp.sum(ar * bur + ai * bui, axis=0, keepdims=True)
                gv_ref[3:4, qs] += jnp.sum(ai * bur - ar * bui, axis=0, keepdims=True)
                fr = f_ref[0:1, qs]
                fi = f_ref[1:2, qs]
                dbr = (fr * ar + fi * ai).astype(BF16)
                dbi = (fr * ai - fi * ar).astype(BF16)
                gbr_ref[q] += _dot_tn(us, dbr)
                gbi_ref[q] += _dot_tn(us, dbi)
                acc = acc + _dot_nt(dbr, bre_ref[q]) + _dot_nt(dbi, bim_ref[q])
            du_ref[:, j * SLAB:(j + 1) * SLAB] = acc

    big = jax.ShapeDtypeStruct(b_re.shape, F32)
    return _hosted_call(
        body, comm, name="ssm_bwd_rev" if reverse else "ssm_bwd_fwd", grid=(nc,),
        in_specs=[pl.BlockSpec((TC, D_SSM), lambda i: (chunk(i), 0)),
                  pl.BlockSpec((TC, D_SSM), lambda i: (chunk(i), 1)),
                  pl.BlockSpec((TC, N_STATE), lambda i: (chunk(i), 0)),
                  pl.BlockSpec((TC, N_STATE), lambda i: (chunk(i), 0))]
        + [_full(b_re.shape)] * 4 + [_full(f2.shape), _full(tables.shape)],
        out_specs=[pl.BlockSpec((TC, D_SSM), lambda i: (chunk(i), 0))] + [_full(b_re.shape)] * 4
        + [_full((4, N_STATE))],
        out_shape=[jax.ShapeDtypeStruct((L, D_SSM), F32), big, big, big, big,
                   jax.ShapeDtypeStruct((4, N_STATE), F32)],
        scratch_shapes=[pltpu.VMEM((TC, N_STATE), F32), pltpu.VMEM((TC, N_STATE), F32),
                        pltpu.VMEM((8, N_STATE), F32), pltpu.VMEM((8, N_STATE), F32)],
        args=(dy, u, s_re, s_im, b_re, b_im, c_re, c_im, f2, tables))


def _ssm_post(yf, yb, u, d, glu_w, glu_b):
    y = yf + yb + d * u
    z, t = _gelu(y)
    zb = z.astype(BF16)
    gate = _sigmoid(_dot_nn(zb, glu_w) + glu_b)
    return y, z, t, zb, gate


def _mix_out(yn_pool, yf, yb, u, x, ssm_d, glu_w_b, glu_b, g_ssm, w_out_b, g_ffn):
    L, D = x.shape

    def body(ynp_ref, yf_ref, yb_ref, u_ref, x_ref, d_ref, gw_ref, gb_ref, gs_ref, wo_ref, gf_ref,
             h1_ref, hn_ref, ycat_ref):
        _, z, _, _, gate = _ssm_post(yf_ref[...], yb_ref[...], u_ref[...], d_ref[...], gw_ref[...], gb_ref[...])
        yns, _, _ = _rms_fwd(z * gate, gs_ref[...])
        ynsb = yns.astype(BF16)
        ynp = ynp_ref[...]
        ycat_ref[:, 0:D_POOL] = ynp
        ycat_ref[:, D_POOL:D] = ynsb
        h1 = x_ref[...] + _dot_nn(ynp, wo_ref[0:D_POOL, :]) + _dot_nn(ynsb, wo_ref[D_POOL:D, :])
        h1_ref[...] = h1
        hn, _, _ = _rms_fwd(h1, gf_ref[...])
        hn_ref[...] = hn.astype(BF16)

    half = lambda c: pl.BlockSpec((TL, D_SSM), lambda i: (i, c))
    row = pl.BlockSpec((TL, D), lambda i: (i, 0))
    return pl.pallas_call(
        body, name="mix_out", grid=(L // TL,),
        in_specs=[half(0), half(0), half(0), half(1), row, _full((1, D_SSM)), _full(glu_w_b.shape),
                  _full((1, D_SSM)), _full((1, D_SSM)), _full(w_out_b.shape), _full((1, D))],
        out_specs=[row, row, row],
        out_shape=[jax.ShapeDtypeStruct((L, D), F32), jax.ShapeDtypeStruct((L, D), BF16),
                   jax.ShapeDtypeStruct((L, D), BF16)],
        compiler_params=_cp("parallel"))(yn_pool, yf, yb, u, x, ssm_d, glu_w_b, glu_b, g_ssm, w_out_b, g_ffn)


def _ssm_bwd_local(dh1, yf, yb, u, ssm_d, glu_w_b, glu_b, g_ssm, w_out_b):
    L, D = dh1.shape

    def body(dh_ref, yf_ref, yb_ref, u_ref, d_ref, gw_ref, gb_ref, gs_ref, wo_ref,
             dy_ref, du_ref, ggw_ref, ggb_ref, gd_ref, ggs_ref):
        @pl.when(pl.program_id(0) == 0)
        def _():
            for r in (ggw_ref, ggb_ref, gd_ref, ggs_ref):
                r[...] = jnp.zeros_like(r)

        u = u_ref[...]
        d = d_ref[...]
        y, z, t, zb, gate = _ssm_post(yf_ref[...], yb_ref[...], u, d, gw_ref[...], gb_ref[...])
        gs = gs_ref[...]
        _, xh, inv = _rms_fwd(z * gate, gs)
        d_yn = _dot_nt(dh_ref[...].astype(BF16), wo_ref[...])
        d_o, dgs = _rms_bwd(d_yn, xh, inv, gs)
        ggs_ref[...] += dgs
        d_zg = d_o * z * gate * (1.0 - gate)
        d_zgb = d_zg.astype(BF16)
        ggb_ref[...] += jnp.sum(d_zg, axis=0, keepdims=True)
        ggw_ref[...] += _dot_tn(zb, d_zgb)
        d_z = d_o * gate + _dot_nt(d_zgb, gw_ref[...])
        d_y = d_z * _gelu_grad(y, t)
        gd_ref[...] += jnp.sum(d_y * u, axis=0, keepdims=True)
        dy_ref[...] = d_y
        du_ref[...] = d_y * d

    half = lambda c: pl.BlockSpec((TL, D_SSM), lambda i: (i, c))
    vec = _full((1, D_SSM))
    return pl.pallas_call(
        body, name="ssm_bwd_local", grid=(L // TL,),
        in_specs=[pl.BlockSpec((TL, D), lambda i: (i, 0)), half(0), half(0), half(1), vec, _full(glu_w_b.shape),
                  vec, vec, pl.BlockSpec((D_SSM, D), lambda i: (1, 0))],
        out_specs=[half(0), half(0), _full(glu_w_b.shape), vec, vec, vec],
        out_shape=[jax.ShapeDtypeStruct((L, D_SSM), F32), jax.ShapeDtypeStruct((L, D_SSM), F32),
                   jax.ShapeDtypeStruct(glu_w_b.shape, F32)] + [jax.ShapeDtypeStruct((1, D_SSM), F32)] * 3,
        compiler_params=_cp("arbitrary"))(dh1, yf, yb, u, ssm_d, glu_w_b, glu_b, g_ssm, w_out_b)


def _in_bwd(du_pool, du_a, du_b, du_c, dh1, x, g, w_in_b):
    L, D = x.shape

    def body(p_ref, a_ref, b_ref, c_ref, dh_ref, x_ref, g_ref, w_ref, dx_ref, dub_ref, gg_ref):
        @pl.when(pl.program_id(0) == 0)
        def _():
            gg_ref[...] = jnp.zeros_like(gg_ref)

        dub_ref[:, 0:D_POOL] = p_ref[...].astype(BF16)
        dub_ref[:, D_POOL:D] = (a_ref[...] + b_ref[...] + c_ref[...]).astype(BF16)
        d_xn = _dot_nt(dub_ref[...], w_ref[...])
        gv = g_ref[...]
        _, xh, inv = _rms_fwd(x_ref[...], gv)
        dx, dg = _rms_bwd(d_xn, xh, inv, gv)
        gg_ref[...] += dg
        dx_ref[...] = dh_ref[...] + dx

    half = pl.BlockSpec((TL, D_SSM), lambda i: (i, 0))
    row = pl.BlockSpec((TL, D), lambda i: (i, 0))
    return pl.pallas_call(
        body, name="in_bwd", grid=(L // TL,),
        in_specs=[half, half, half, half, row, row, _full((1, D)), _full(w_in_b.shape)],
        out_specs=[row, row, _full((1, D))],
        out_shape=[jax.ShapeDtypeStruct((L, D), F32), jax.ShapeDtypeStruct((L, D), BF16),
                   jax.ShapeDtypeStruct((1, D), F32)],
        compiler_params=_cp("arbitrary"))(du_pool, du_a, du_b, du_c, dh1, x, g, w_in_b)


def _ffn_up(hn, w_up4):
    L, D = hn.shape

    def body(h_ref, w_ref, o_ref):
        o_ref[...] = _dot_nn(h_ref[...], w_ref[...])

    return pl.pallas_call(
        body, name="ffn_up", grid=(4, L // TF),
        in_specs=[pl.BlockSpec((TF, D), lambda j, i: (i, 0)), pl.BlockSpec((None, D, FF_BLK), lambda j, i: (j, 0, 0))],
        out_specs=pl.BlockSpec((TF, FF_BLK), lambda j, i: (i, j)),
        out_shape=jax.ShapeDtypeStruct((L, 4 * FF_BLK), F32),
        compiler_params=_cp("parallel", "parallel"))(hn, w_up4)


def _halo_specs_2d(rows, width, L, col, order):
    rb = rows // HALO
    last = L // HALO - 1
    if order == "ik":
        wrap = lambda f: (lambda i, k: f(i, k))
    else:
        wrap = lambda f: (lambda k, i: f(i, k))
    return [pl.BlockSpec((HALO, width), wrap(lambda i, k: (jnp.maximum(i * rb - 1, 0), col(k)))),
            pl.BlockSpec((rows, width), wrap(lambda i, k: (i, col(k)))),
            pl.BlockSpec((HALO, width), wrap(lambda i, k: (jnp.minimum((i + 1) * rb, last), col(k))))]


def _conv3(ext_ref, w_ref, b_ref, rows):
    return (ext_ref[pl.ds(HALO - 1, rows), :] * w_ref[0:1, :] + ext_ref[pl.ds(HALO, rows), :] * w_ref[1:2, :]
            + ext_ref[pl.ds(HALO + 1, rows), :] * w_ref[2:3, :] + b_ref[...])


def _ffn_down_loss(up, conv_w, conv_b, w_down_b, h1, target, g_final):
    L, D = h1.shape
    n = L // TF
    nk = D_FF // FF_BLK

    def body(vp, vc, vn, gp, gc, gn, wv_ref, wg_ref, bv_ref, bg_ref, wd_ref, h1_ref, t_ref, gf_ref,
             a_ref, dh2_ref, loss_ref, gg_ref, ext_v, ext_g, acc_ref):
        i = pl.program_id(0)
        k = pl.program_id(1)

        @pl.when((i == 0) & (k == 0))
        def _():
            loss_ref[...] = jnp.zeros_like(loss_ref)
            gg_ref[...] = jnp.zeros_like(gg_ref)

        @pl.when(k == 0)
        def _():
            acc_ref[...] = jnp.zeros_like(acc_ref)

        _fill_ext(ext_v, vp, vc, vn, i, n, TF)
        _fill_ext(ext_g, gp, gc, gn, i, n, TF)
        val = _conv3(ext_v, wv_ref, bv_ref, TF)
        gate = _conv3(ext_g, wg_ref, bg_ref, TF)
        a = (val * (gate * _sigmoid(gate))).astype(BF16)
        a_ref[...] = a
        acc_ref[...] += _dot_nn(a, wd_ref[...])

        @pl.when(k == nk - 1)
        def _():
            gf = gf_ref[...]
            y, xh, inv = _rms_fwd(h1_ref[...] + acc_ref[...], gf)
            diff = y - t_ref[...]
            part = 0.5 * jnp.sum(jnp.mean(diff * diff, axis=-1, keepdims=True), axis=0, keepdims=True)
            loss_ref[...] += jnp.broadcast_to(part, loss_ref.shape)
            dx, dg = _rms_bwd(diff * (1.0 / D), xh, inv, gf)
            gg_ref[...] += dg
            dh2_ref[...] = dx

    row = pl.BlockSpec((TF, D), lambda i, k: (i, 0))
    cw = lambda off: pl.BlockSpec((3, FF_BLK), lambda i, k: (0, k + off))
    cb = lambda off: pl.BlockSpec((1, FF_BLK), lambda i, k: (0, k + off))
    return pl.pallas_call(
        body, name="ffn_down_loss", grid=(n, nk),
        in_specs=_halo_specs_2d(TF, FF_BLK, L, lambda k: k, "ik") + _halo_specs_2d(TF, FF_BLK, L, lambda k: k + nk, "ik")
        + [cw(0), cw(nk), cb(0), cb(nk), pl.BlockSpec((FF_BLK, D), lambda i, k: (k, 0)), row, row, _full((1, D))],
        out_specs=[pl.BlockSpec((TF, FF_BLK), lambda i, k: (i, k)), row, _full((1, LANES)), _full((1, D))],
        out_shape=[jax.ShapeDtypeStruct((L, D_FF), BF16), jax.ShapeDtypeStruct((L, D), F32),
                   jax.ShapeDtypeStruct((1, LANES), F32), jax.ShapeDtypeStruct((1, D), F32)],
        scratch_shapes=[pltpu.VMEM((TF + 2 * HALO, FF_BLK), F32), pltpu.VMEM((TF + 2 * HALO, FF_BLK), F32),
                        pltpu.VMEM((TF, D), F32)],
        compiler_params=_cp("arbitrary", "arbitrary"))(
            up, up, up, up, up, up, conv_w, conv_w, conv_b, conv_b, w_down_b, h1, target, g_final)


def _ffn_act_bwd(up, conv_w, conv_b, w_down_b, dh2):
    L, D = dh2.shape
    n = L // TF
    nk = D_FF // FF_BLK

    def body(vp, vc, vn, gp, gc, gn, wv_ref, wg_ref, bv_ref, bg_ref, wd_ref, dh_ref,
             dv_ref, dg_ref, gcv_ref, gcg_ref, ext_v, ext_g):
        i = pl.program_id(1)

        @pl.when(i == 0)
        def _():
            gcv_ref[...] = jnp.zeros_like(gcv_ref)
            gcg_ref[...] = jnp.zeros_like(gcg_ref)

        _fill_ext(ext_v, vp, vc, vn, i, n, TF)
        _fill_ext(ext_g, gp, gc, gn, i, n, TF)
        val = _conv3(ext_v, wv_ref, bv_ref, TF)
        gate = _conv3(ext_g, wg_ref, bg_ref, TF)
        d_a = _dot_nt(dh_ref[...].astype(BF16), wd_ref[...])
        sg = _sigmoid(gate)
        d_val = d_a * (gate * sg)
        d_gate = d_a * val * (sg * (1.0 + gate * (1.0 - sg)))
        dv_ref[...] = d_val
        dg_ref[...] = d_gate
        for dref, ext, gref in ((d_val, ext_v, gcv_ref), (d_gate, ext_g, gcg_ref)):
            for j in range(3):
                gref[j:j + 1, :] += jnp.sum(dref * ext[pl.ds(HALO - 1 + j, TF), :], axis=0, keepdims=True)
            gref[3:4, :] += jnp.sum(dref, axis=0, keepdims=True)

    cw = lambda off: pl.BlockSpec((3, FF_BLK), lambda k, i: (0, k + off))
    cb = lambda off: pl.BlockSpec((1, FF_BLK), lambda k, i: (0, k + off))
    blk = pl.BlockSpec((TF, FF_BLK), lambda k, i: (i, k))
    acc = pl.BlockSpec((4, FF_BLK), lambda k, i: (0, k))
    return pl.pallas_call(
        body, name="ffn_act_bwd", grid=(nk, n),
        in_specs=_halo_specs_2d(TF, FF_BLK, L, lambda k: k, "ki") + _halo_specs_2d(TF, FF_BLK, L, lambda k: k + nk, "ki")
        + [cw(0), cw(nk), cb(0), cb(nk), pl.BlockSpec((FF_BLK, D), lambda k, i: (k, 0)),
           pl.BlockSpec((TF, D), lambda k, i: (i, 0))],
        out_specs=[blk, blk, acc, acc],
        out_shape=[jax.ShapeDtypeStruct((L, D_FF), F32), jax.ShapeDtypeStruct((L, D_FF), F32),
                   jax.ShapeDtypeStruct((4, D_FF), F32), jax.ShapeDtypeStruct((4, D_FF), F32)],
        scratch_shapes=[pltpu.VMEM((TF + 2 * HALO, FF_BLK), F32), pltpu.VMEM((TF + 2 * HALO, FF_BLK), F32)],
        compiler_params=_cp("arbitrary", "arbitrary"))(
            up, up, up, up, up, up, conv_w, conv_w, conv_b, conv_b, w_down_b, dh2)


def _ffn_up_bwd(d_val, d_gate, conv_w, w_up4, h1, dh2, g_ffn):
    L, D = h1.shape
    n = L // TF
    nk = D_FF // FF_BLK

    def body(vp, vc, vn, gp, gc, gn, wv_ref, wg_ref, uv_ref, ug_ref, h1_ref, dh2_ref, g_ref,
             duv_ref, dug_ref, dh1_ref, gg_ref, ext_v, ext_g, acc_ref):
        i = pl.program_id(0)
        k = pl.program_id(1)

        @pl.when((i == 0) & (k == 0))
        def _():
            gg_ref[...] = jnp.zeros_like(gg_ref)

        @pl.when(k == 0)
        def _():
            acc_ref[...] = jnp.zeros_like(acc_ref)

        _fill_ext(ext_v, vp, vc, vn, i, n, TF)
        _fill_ext(ext_g, gp, gc, gn, i, n, TF)
        for ext, w_ref, o_ref, wu_ref in ((ext_v, wv_ref, duv_ref, uv_ref), (ext_g, wg_ref, dug_ref, ug_ref)):
            d_up = (ext[pl.ds(HALO + 1, TF), :] * w_ref[0:1, :] + ext[pl.ds(HALO, TF), :] * w_ref[1:2, :]
                    + ext[pl.ds(HALO - 1, TF), :] * w_ref[2:3, :]).astype(BF16)
            o_ref[...] = d_up
            acc_ref[...] += _dot_nt(d_up, wu_ref[...])

        @pl.when(k == nk - 1)
        def _():
            g = g_ref[...]
            _, xh, inv = _rms_fwd(h1_ref[...], g)
            dx, dg = _rms_bwd(acc_ref[...], xh, inv, g)
            gg_ref[...] += dg
            dh1_ref[...] = dh2_ref[...] + dx

    row = pl.BlockSpec((TF, D), lambda i, k: (i, 0))
    cw = lambda off: pl.BlockSpec((3, FF_BLK), lambda i, k: (0, k + off))
    wu = lambda off: pl.BlockSpec((None, D, FF_BLK), lambda i, k: (k + off, 0, 0))
    blk = pl.BlockSpec((TF, FF_BLK), lambda i, k: (i, k))
    return pl.pallas_call(
        body, name="ffn_up_bwd", grid=(n, nk),
        in_specs=_halo_specs_2d(TF, FF_BLK, L, lambda k: k, "ik") + _halo_specs_2d(TF, FF_BLK, L, lambda k: k, "ik")
        + [cw(0), cw(nk), wu(0), wu(nk), row, row, _full((1, D))],
        out_specs=[blk, blk, row, _full((1, D))],
        out_shape=[jax.ShapeDtypeStruct((L, D_FF), BF16), jax.ShapeDtypeStruct((L, D_FF), BF16),
                   jax.ShapeDtypeStruct((L, D), F32), jax.ShapeDtypeStruct((1, D), F32)],
        scratch_shapes=[pltpu.VMEM((TF + 2 * HALO, FF_BLK), F32), pltpu.VMEM((TF + 2 * HALO, FF_BLK), F32),
                        pltpu.VMEM((TF, D), F32)],
        compiler_params=_cp("arbitrary", "arbitrary"))(
            d_val, d_val, d_val, d_gate, d_gate, d_gate, conv_w, conv_w, w_up4, w_up4, h1, dh2, g_ffn)


def _matmul_tn(a, b, tm, tn, name, tk=512):
    L, M = a.shape
    N = b.shape[1]
    a = a.astype(BF16)
    b = b.astype(BF16)

    def body(a_ref, b_ref, o_ref):
        @pl.when(pl.program_id(2) == 0)
        def _():
            o_ref[...] = jnp.zeros_like(o_ref)

        o_ref[...] += _dot_tn(a_ref[...], b_ref[...])

    return pl.pallas_call(
        body, name=name, grid=(M // tm, N // tn, L // tk),
        in_specs=[pl.BlockSpec((tk, tm), lambda m, n, l: (l, m)), pl.BlockSpec((tk, tn), lambda m, n, l: (l, n))],
        out_specs=pl.BlockSpec((tm, tn), lambda m, n, l: (m, n)),
        out_shape=jax.ShapeDtypeStruct((M, N), F32),
        compiler_params=_cp("parallel", "parallel", "arbitrary"))(a, b)


def _row_tile(rows):
    for t in (512, 352, 256, 128, 64, 8):
        if rows % t == 0:
            return t
    return rows


def _add_half(g, r, c_arr, name, out_dtype=F32):
    _, _, R, C = g.shape
    tr = _row_tile(R)

    def body(c_ref, g_ref, r_ref, o_ref):
        o_ref[...] = (g_ref[...] + r_ref[...]).astype(out_dtype)

    return pl.pallas_call(
        body, name=name,
        grid_spec=pltpu.PrefetchScalarGridSpec(
            num_scalar_prefetch=1, grid=(g.shape[0], R // tr),
            in_specs=[pl.BlockSpec((None, None, tr, C), lambda j, i, c: (j, c[0], i, 0)),
                      pl.BlockSpec((None, tr, C), lambda j, i, c: (j, i, 0))],
            out_specs=pl.BlockSpec((None, tr, C), lambda j, i, c: (j, i, 0))),
        out_shape=jax.ShapeDtypeStruct(r.shape, out_dtype),
        compiler_params=_cp("parallel", "parallel"))(c_arr, g, r)


def _add2(a, b, name):
    R, C = a.shape
    tr = _row_tile(R)

    def body(a_ref, b_ref, o_ref):
        o_ref[...] = a_ref[...] + b_ref[...]

    spec = pl.BlockSpec((tr, C), lambda i: (i, 0))
    return pl.pallas_call(body, name=name, grid=(R // tr,), in_specs=[spec, spec], out_specs=spec,
                          out_shape=jax.ShapeDtypeStruct(a.shape, F32), compiler_params=_cp("parallel"))(a, b)


def _sum4(p, name):
    _, R, C = p.shape
    tr = _row_tile(R)

    def body(p_ref, o_ref):
        q = [p_ref[j].astype(F32) for j in range(4)]
        o_ref[...] = ((q[0] + q[1]) + q[2]) + q[3]

    return pl.pallas_call(
        body, name=name, grid=(R // tr,),
        in_specs=[pl.BlockSpec((4, tr, C), lambda i: (0, i, 0))],
        out_specs=pl.BlockSpec((tr, C), lambda i: (i, 0)),
        out_shape=jax.ShapeDtypeStruct((R, C), F32), compiler_params=_cp("parallel"))(p)


def _adamw(w, g, m, v, name):
    R, C = w.shape
    tr = _row_tile(R)

    def body(w_ref, g_ref, m_ref, v_ref, d_ref, nm_ref, nv_ref):
        gv = g_ref[...]
        nm = ADAM_B1 * m_ref[...] + (1.0 - ADAM_B1) * gv
        nv = ADAM_B2 * v_ref[...] + (1.0 - ADAM_B2) * (gv * gv)
        m_hat = nm / (1.0 - ADAM_B1 ** ADAM_STEP)
        v_hat = nv / (1.0 - ADAM_B2 ** ADAM_STEP)
        d_ref[...] = -ADAM_LR * (m_hat / (jnp.sqrt(v_hat) + ADAM_EPS) + ADAM_WD * w_ref[...])
        nm_ref[...] = nm
        nv_ref[...] = nv

    spec = pl.BlockSpec((tr, C), lambda i: (i, 0))
    sh = jax.ShapeDtypeStruct((R, C), F32)
    return pl.pallas_call(body, name=name, grid=(R // tr,), in_specs=[spec] * 4, out_specs=[spec] * 3,
                          out_shape=[sh] * 3, compiler_params=_cp("parallel"))(w, g, m, v)


_ANY = pl.BlockSpec(memory_space=pl.ANY)


def _position():
    return lax.axis_index("x"), lax.axis_index("y"), lax.axis_index("c")


class _Comm:
    def __init__(self, arrs, out_shape, sems, start, finish):
        self.arrs, self.out_shape, self.sems, self.start, self.finish = arrs, out_shape, sems, start, finish


def _comm_call(comm, name):
    n, m = len(comm.arrs), len(comm.out_shape)

    def body(*refs):
        ins, outs, sems = refs[:n], refs[n:n + m], refs[n + m:]
        comm.start(ins, outs, sems)
        comm.finish(ins, outs, sems)

    return pl.pallas_call(
        body, name=name, in_specs=[_ANY] * n, out_specs=[_ANY] * m, out_shape=comm.out_shape,
        scratch_shapes=comm.sems, compiler_params=pltpu.CompilerParams(has_side_effects=True))(*comm.arrs)


def _hosted_call(body, comm, *, name, grid, in_specs, out_specs, out_shape, scratch_shapes, args):
    sem = ("arbitrary",) * len(grid)
    if comm is None:
        return pl.pallas_call(body, name=name, grid=grid, in_specs=in_specs, out_specs=out_specs, out_shape=out_shape,
                              scratch_shapes=scratch_shapes, compiler_params=_cp(*sem))(*args), []
    n_in, n_out, n_scr = len(in_specs), len(out_specs), len(scratch_shapes)
    ci, co = len(comm.arrs), len(comm.out_shape)

    def full(*refs):
        ins, refs = refs[:n_in], refs[n_in:]
        cins, refs = refs[:ci], refs[ci:]
        outs, refs = refs[:n_out], refs[n_out:]
        couts, refs = refs[:co], refs[co:]
        scr, csems = refs[:n_scr], refs[n_scr:]
        first, last = True, True
        for d, size in enumerate(grid):
            first = first & (pl.program_id(d) == 0)
            last = last & (pl.program_id(d) == size - 1)

        @pl.when(first)
        def _():
            comm.start(cins, couts, csems)

        body(*ins, *outs, *scr)

        @pl.when(last)
        def _():
            comm.finish(cins, couts, csems)

    res = pl.pallas_call(
        full, name=name, grid=grid, in_specs=list(in_specs) + [_ANY] * ci, out_specs=list(out_specs) + [_ANY] * co,
        out_shape=list(out_shape) + list(comm.out_shape), scratch_shapes=list(scratch_shapes) + list(comm.sems),
        compiler_params=_cp(*sem))(*args, *comm.arrs)
    return res[:n_out], res[n_out:]


def _comm_join(*comms):
    def parts(xs, attr):
        out, at = [], 0
        for cm in comms:
            n = len(getattr(cm, attr))
            out.append(xs[at:at + n])
            at += n
        return out

    def start(ins, outs, sems):
        for cm, i, o, s in zip(comms, parts(ins, "arrs"), parts(outs, "out_shape"), parts(sems, "sems")):
            cm.start(i, o, s)

    def finish(ins, outs, sems):
        for cm, i, o, s in zip(comms, parts(ins, "arrs"), parts(outs, "out_shape"), parts(sems, "sems")):
            cm.finish(i, o, s)

    cat = lambda attr: [x for cm in comms for x in getattr(cm, attr)]
    return _Comm(cat("arrs"), cat("out_shape"), cat("sems"), start, finish)


def _dma_sems(*counts):
    return [pltpu.SemaphoreType.DMA((n,)) for n in counts]


def _comm_pair_swap(arrs, half=False):
    n = len(arrs)
    out_shape = [jax.ShapeDtypeStruct(a.shape[:1] + a.shape[2:] if half else a.shape, a.dtype) for a in arrs]

    def copies(ins, outs, sems):
        x, y, c = _position()
        return [pltpu.make_async_remote_copy(
            src_ref=ins[k].at[:, 1 - c] if half else ins[k], dst_ref=outs[k], send_sem=sems[0].at[k],
            recv_sem=sems[1].at[k], device_id=(x, y, 1 - c), device_id_type=MESH) for k in range(n)]

    def start(ins, outs, sems):
        for cp in copies(ins, outs, sems):
            cp.start()

    def finish(ins, outs, sems):
        for cp in copies(ins, outs, sems):
            cp.wait()

    return _Comm(arrs, out_shape, _dma_sems(n, n), start, finish)


def _chip_of(j, c):
    return (jnp.right_shift(j, 1), jnp.bitwise_and(j, 1), c)


def _comm_chip_exchange(arrs, scatter):
    n = len(arrs)
    out_shape = [jax.ShapeDtypeStruct(a.shape if scatter else (4,) + a.shape, a.dtype) for a in arrs]

    def copies(ins, outs, sems):
        x, y, c = _position()
        me = 2 * x + y
        local, sent, landed = [], [], []
        for k in range(n):
            local.append(pltpu.make_async_copy(ins[k].at[me] if scatter else ins[k], outs[k].at[me], sems[2].at[k]))
            for d in (1, 2, 3):
                j = jnp.bitwise_xor(me, d)
                s = 3 * k + d - 1
                src = ins[k].at[j] if scatter else ins[k]
                for dst, group in ((outs[k].at[me], sent), (outs[k].at[j], landed)):
                    group.append(pltpu.make_async_remote_copy(
                        src_ref=src, dst_ref=dst, send_sem=sems[0].at[s], recv_sem=sems[1].at[s],
                        device_id=_chip_of(j, c), device_id_type=MESH))
        return local, sent, landed

    def start(ins, outs, sems):
        local, sent, _ = copies(ins, outs, sems)
        for cp in local + sent:
            cp.start()

    def finish(ins, outs, sems):
        local, sent, landed = copies(ins, outs, sems)
        for cp in sent:
            cp.wait_send()
        for cp in landed:
            cp.wait_recv()
        for cp in local:
            cp.wait()

    return _Comm(arrs, out_shape, _dma_sems(3 * n, 3 * n, n), start, finish)


def _comm_pair_gather(arrs):
    n = len(arrs)
    out_shape = [jax.ShapeDtypeStruct((2,) + a.shape, a.dtype) for a in arrs]

    def copies(ins, outs, sems):
        x, y, c = _position()
        local, sent, landed = [], [], []
        for k in range(n):
            local.append(pltpu.make_async_copy(ins[k], outs[k].at[c], sems[2].at[k]))
            for dst, group in ((outs[k].at[c], sent), (outs[k].at[1 - c], landed)):
                group.append(pltpu.make_async_remote_copy(
                    src_ref=ins[k], dst_ref=dst, send_sem=sems[0].at[k], recv_sem=sems[1].at[k],
                    device_id=(x, y, 1 - c), device_id_type=MESH))
        return local, sent, landed

    def start(ins, outs, sems):
        local, sent, _ = copies(ins, outs, sems)
        for cp in local + sent:
            cp.start()

    def finish(ins, outs, sems):
        local, sent, landed = copies(ins, outs, sems)
        for cp in sent:
            cp.wait_send()
        for cp in landed:
            cp.wait_recv()
        for cp in local:
            cp.wait()

    return _Comm(arrs, out_shape, _dma_sems(n, n, n), start, finish)


def _comm_gather_split(shards, whole):
    n, nw = len(shards), len(whole)
    arrs = list(shards) + list(whole)
    out_shape = [jax.ShapeDtypeStruct((4,) + a.shape, a.dtype) for a in arrs]

    def copies(ins, outs, sems):
        x, y, c = _position()
        me = 2 * x + y
        local, sent, landed, passed, passed_in = [], [], [], [], []
        for k in range(n + nw):
            local.append(pltpu.make_async_copy(ins[k], outs[k].at[me], sems[4].at[k]))
            for d in (1, 2, 3):
                j = jnp.bitwise_xor(me, d)
                s = 3 * k + d - 1
                if k >= n:
                    src, mine, theirs = ins[k], outs[k].at[me], outs[k].at[j]
                else:
                    h = shards[k].shape[0] // 2
                    rows = pl.ds(pl.multiple_of(c * h, 16), h)
                    other = pl.ds(pl.multiple_of((1 - c) * h, 16), h)
                    src, mine, theirs = ins[k].at[rows], outs[k].at[me, rows], outs[k].at[j, rows]
                    for dst, group in ((theirs, passed), (outs[k].at[j, other], passed_in)):
                        group.append(pltpu.make_async_remote_copy(
                            src_ref=theirs, dst_ref=dst, send_sem=sems[2].at[s], recv_sem=sems[3].at[s],
                            device_id=(x, y, 1 - c), device_id_type=MESH))
                for dst, group in ((mine, sent), (theirs, landed)):
                    group.append(pltpu.make_async_remote_copy(
                        src_ref=src, dst_ref=dst, send_sem=sems[0].at[s], recv_sem=sems[1].at[s],
                        device_id=_chip_of(j, c), device_id_type=MESH))
        return local, sent, landed, passed, passed_in

    def start(ins, outs, sems):
        local, sent, _, _, _ = copies(ins, outs, sems)
        for cp in local + sent:
            cp.start()

    def finish(ins, outs, sems):
        local, sent, landed, passed, passed_in = copies(ins, outs, sems)
        for cp in landed[:3 * n]:
            cp.wait_recv()
        for cp in passed:
            cp.start()
        for cp in landed[3 * n:]:
            cp.wait_recv()
        for cp in sent:
            cp.wait_send()
        for cp in passed:
            cp.wait_send()
        for cp in passed_in:
            cp.wait_recv()
        for cp in local:
            cp.wait()

    t = 3 * (n + nw)
    return _Comm(arrs, out_shape, _dma_sems(t, t, max(3 * n, 1), max(3 * n, 1), n + nw), start, finish)


def _pack(arrs, row_multiple):
    parts = []
    for a in arrs:
        flat = a.reshape(-1).astype(F32)
        pad = (-flat.shape[0]) % LANES
        parts.append(jnp.pad(flat, (0, pad)) if pad else flat)
    flat = jnp.concatenate(parts)
    rows = -(-flat.shape[0] // LANES)
    rows_p = -(-rows // row_multiple) * row_multiple
    return jnp.pad(flat, (0, rows_p * LANES - flat.shape[0])).reshape(rows_p, LANES)


def _unpack(packed, shapes):
    flat = packed.reshape(-1)
    outs, off = [], 0
    for sh in shapes:
        size = int(np.prod(sh))
        outs.append(flat[off:off + size].reshape(sh))
        off += size + (-size) % LANES
    return outs


SMALL = ["norm_mix_g", "pool_w", "pool_scale", "ssm_log_neg_a_re", "ssm_a_im", "ssm_log_dt", "ssm_b_re", "ssm_b_im",
         "ssm_c_re", "ssm_c_im", "ssm_d", "glu_b", "out_norm_pool_g", "out_norm_ssm_g", "norm_ffn_g", "conv_b",
         "final_norm_g"]
BIG = ["w_in", "glu_w", "w_out", "w_up", "w_down"]
WEIGHTS = ['norm_mix_g', 'w_in', 'pool_w', 'pool_scale', 'ssm_log_neg_a_re', 'ssm_a_im', 'ssm_log_dt', 'ssm_b_re',
           'ssm_b_im', 'ssm_c_re', 'ssm_c_im', 'ssm_d', 'glu_w', 'glu_b', 'out_norm_pool_g', 'out_norm_ssm_g', 'w_out',
           'norm_ffn_g', 'w_up', 'conv_w', 'conv_b', 'w_down', 'final_norm_g']


def _local_step(x, target, p, full, shards=None, c_arr=None):
    L, D = x.shape
    dist = shards is not None
    row = lambda a: a.reshape(1, -1)
    w_in = full["w_in"]
    pool_w_b = p["pool_w"].astype(BF16)
    g_mix, g_pool, g_ssm, g_ffn, g_fin = (row(p[k]) for k in (
        "norm_mix_g", "out_norm_pool_g", "out_norm_ssm_g", "norm_ffn_g", "final_norm_g"))
    pool_scale, ssm_d, glu_b, conv_b = (row(p[k]) for k in ("pool_scale", "ssm_d", "glu_b", "conv_b"))

    lnar = p["ssm_log_neg_a_re"].reshape(2 * N_SSM_GROUPS, SSM_STATE)
    aim = p["ssm_a_im"].reshape(2 * N_SSM_GROUPS, SSM_STATE)
    ldt = jnp.broadcast_to(p["ssm_log_dt"].reshape(2 * N_SSM_GROUPS, 1), lnar.shape)
    lam_re, lam_im, f_re, f_im = _ssm_params(lnar, aim, ldt)
    flat2 = lambda a: a.reshape(2, N_STATE)
    lam4 = jnp.stack([flat2(lam_re)[0], flat2(lam_im)[0], flat2(lam_re)[1], flat2(lam_im)[1]])
    tables = _scan_tables(lam4)
    f2 = [jnp.stack([flat2(f_re)[d], flat2(f_im)[d]]) for d in range(2)]
    bexp = [[_expand_b(p[k][d]).astype(BF16) for k in ("ssm_b_re", "ssm_b_im")] for d in range(2)]
    cexp = [[_expand_c(p[k][d]).astype(BF16) for k in ("ssm_c_re", "ssm_c_im")] for d in range(2)]
    ssm_args = [(bexp[d][0], bexp[d][1], cexp[d][0], cexp[d][1], f2[d], tables) for d in range(2)]

    u, xn = _in_proj(x, g_mix, w_in)
    yn_pool = _pool_fwd(u, pool_w_b, pool_scale, g_pool)
    gather1 = _comm_gather_split([shards[k] for k in ("glu_w", "w_out", "w_down")], [shards["conv_w"]]) if dist else None
    (y0, s0r, s0i), got1 = _ssm_scan_fwd(u, *ssm_args[0], 0, False, comm=gather1)
    gather2 = _comm_gather_split([shards["w_up"]], []) if dist else None
    (y1, s1r, s1i), got2 = _ssm_scan_fwd(u, *ssm_args[1], 2, True, comm=gather2)
    if dist:
        glu_w, w_out, w_down = (g.reshape((-1,) + g.shape[2:]) for g in got1[:3])
        conv_w = jnp.transpose(got1[3], (1, 0, 2)).reshape(3, -1)
        w_up4 = got2[0]
    else:
        glu_w, w_out, w_up4, w_down, conv_w = (full[k] for k in ("glu_w", "w_out", "w_up", "w_down", "conv_w"))
    h1, hn, ycat = _mix_out(yn_pool, y0, y1, u, x, ssm_d, glu_w, glu_b, g_ssm, w_out, g_ffn)
    up = _ffn_up(hn, w_up4)
    a, dh2, loss, g_final = _ffn_down_loss(up, conv_w, conv_b, w_down, h1, target, g_fin)

    d_val, d_gate, gcv, gcg = _ffn_act_bwd(up, conv_w, conv_b, w_down, dh2)
    g_w_down = _matmul_tn(a, dh2, FF_BLK, D, "grad_w_down")
    d_up_v, d_up_g, dh1, g_ffn_g = _ffn_up_bwd(d_val, d_gate, conv_w, w_up4, h1, dh2, g_ffn)
    g_w_up = jnp.concatenate([
        _matmul_tn(hn, d_up_v, 512, FF_BLK, "grad_w_up_val").reshape(D, 2, FF_BLK),
        _matmul_tn(hn, d_up_g, 512, FF_BLK, "grad_w_up_gate").reshape(D, 2, FF_BLK)], axis=1)
    g_w_up = jnp.transpose(g_w_up, (1, 0, 2))
    g_w_out = _matmul_tn(ycat, dh1, 512, D, "grad_w_out")
    late = ("w_up", "w_down")
    halves = [g_w_up.reshape(4, 2, D // 2, FF_BLK), g_w_down.reshape(4, 2, D_FF // 8, D)]
    (d_pooled, g_pool_w, g_pool_scale, g_pool_g), from_sibling = _pool_bwd_local(
        dh1, u, w_out, pool_w_b, pool_scale, g_pool, comm=_comm_pair_swap(halves, half=True) if dist else None)
    du_pool = _pool_bwd_window(d_pooled)
    dy, du_direct, g_glu_w, g_glu_b, g_ssm_d, g_ssm_g = _ssm_bwd_local(dh1, y0, y1, u, ssm_d, glu_w, glu_b, g_ssm, w_out)
    reduce2 = None
    if dist:
        chip_sums = [_add_half(h, r, c_arr, "sum_pair_" + k, BF16) for k, h, r in zip(late, halves, from_sibling)]
        reduce2 = _comm_chip_exchange(chip_sums, scatter=True)
    (du0, gb0r, gb0i, gc0r, gc0i, gv0), from_chips = _ssm_scan_bwd(dy, u, s0r, s0i, *ssm_args[0], 1, True, comm=reduce2)
    reduce3 = _comm_pair_gather([_sum4(r, "sum_chips_" + k) for k, r in zip(late, from_chips)]) if dist else None
    (du1, gb1r, gb1i, gc1r, gc1i, gv1), shards_out = _ssm_scan_bwd(dy, u, s1r, s1i, *ssm_args[1], 3, False, comm=reduce3)
    gvec = lambda j: jnp.stack([gv0[j], gv1[j]]).reshape(2 * N_SSM_GROUPS, SSM_STATE)
    g_lnar, g_aim, g_ldt = _ssm_params_bwd(lnar, aim, ldt, gvec(0), gvec(1), gvec(2), gvec(3))
    grad_x, d_u_b, g_mix_g = _in_bwd(du_pool, du_direct, du0, du1, dh1, x, g_mix, w_in)
    g_w_in = _matmul_tn(xn, d_u_b, 512, D, "grad_w_in")

    small = {
        "norm_mix_g": g_mix_g, "pool_w": g_pool_w, "pool_scale": g_pool_scale,
        "ssm_log_neg_a_re": g_lnar, "ssm_a_im": g_aim, "ssm_log_dt": g_ldt,
        "ssm_b_re": jnp.stack([_extract_b(gb0r), _extract_b(gb1r)]),
        "ssm_b_im": jnp.stack([_extract_b(gb0i), _extract_b(gb1i)]),
        "ssm_c_re": jnp.stack([_extract_c(gc0r), _extract_c(gc1r)]),
        "ssm_c_im": jnp.stack([_extract_c(gc0i), _extract_c(gc1i)]),
        "ssm_d": g_ssm_d, "glu_b": g_glu_b, "out_norm_pool_g": g_pool_g, "out_norm_ssm_g": g_ssm_g,
        "norm_ffn_g": g_ffn_g, "conv_b": jnp.concatenate([gcv[3], gcg[3]]), "final_norm_g": g_final,
        "conv_w": jnp.concatenate([gcv[0:3], gcg[0:3]], axis=1),
    }
    big = {"w_in": g_w_in, "glu_w": g_glu_w, "w_out": g_w_out}
    reduced = dict(zip(late, shards_out))
    if not dist:
        big.update({"w_up": g_w_up, "w_down": g_w_down})
    return loss, grad_x, small, big, reduced


def kernel(x, norm_mix_g, w_in, pool_w, pool_scale, ssm_log_neg_a_re, ssm_a_im, ssm_log_dt, ssm_b_re, ssm_b_im, ssm_c_re, ssm_c_im, ssm_d, glu_w, glu_b, out_norm_pool_g, out_norm_ssm_g, w_out, norm_ffn_g, w_up, conv_w, conv_b, w_down, final_norm_g, loss_target, m_norm_mix_g, m_w_in, m_pool_w, m_pool_scale, m_ssm_log_neg_a_re, m_ssm_a_im, m_ssm_log_dt, m_ssm_b_re, m_ssm_b_im, m_ssm_c_re, m_ssm_c_im, m_ssm_d, m_glu_w, m_glu_b, m_out_norm_pool_g, m_out_norm_ssm_g, m_w_out, m_norm_ffn_g, m_w_up, m_conv_w, m_conv_b, m_w_down, m_final_norm_g, v_norm_mix_g, v_w_in, v_pool_w, v_pool_scale, v_ssm_log_neg_a_re, v_ssm_a_im, v_ssm_log_dt, v_ssm_b_re, v_ssm_b_im, v_ssm_c_re, v_ssm_c_im, v_ssm_d, v_glu_w, v_glu_b, v_out_norm_pool_g, v_out_norm_ssm_g, v_w_out, v_norm_ffn_g, v_w_up, v_conv_w, v_conv_b, v_w_down, v_final_norm_g):
    args = locals()
    w = {k: args[k] for k in WEIGHTS}
    m = {k: args["m_" + k] for k in WEIGHTS}
    v = {k: args["v_" + k] for k in WEIGHTS}
    chip = 2 * lax.axis_index("x") + lax.axis_index("y")
    c_arr = lax.axis_index("c").astype(jnp.int32).reshape(1)

    shards = {k: w[k].astype(BF16) for k in BIG}
    shards["conv_w"] = conv_w
    w_in_full = _comm_call(_comm_chip_exchange([shards["w_in"]], scatter=False), "gather_w_in")[0]
    loss, grad_x, g_small, g_big, reduced = _local_step(
        x[0], loss_target[0], w, {"w_in": w_in_full.reshape(-1, w_in_full.shape[-1])}, shards, c_arr)
    loss = lax.psum(loss[0, 0], ("x", "y", "c"))

    tail = ("w_in", "glu_w", "w_out")
    packed = _pack([g_small[k] for k in SMALL] + [g_small["conv_w"]], 1024)
    halves = [g_big[k].reshape(4, 2, g_big[k].shape[0] // 8, g_big[k].shape[1]) for k in tail]
    halves.append(packed.reshape(1, 2, packed.shape[0] // 2, LANES))
    from_sibling = _comm_call(_comm_pair_swap(halves, half=True), "reduce_pair")
    names = tail + ("small",)
    sums = [_add_half(h, r, c_arr, "sum_pair_" + k, F32 if k == "small" else BF16)
            for k, h, r in zip(names, halves, from_sibling)]
    from_chips = _comm_call(_comm_join(_comm_chip_exchange(sums[:3], scatter=True),
                                       _comm_chip_exchange([sums[3][0]], scatter=False)), "reduce_chips")
    mine = [_sum4(r, "sum_chips_" + k) for k, r in zip(names, from_chips)]
    joined = _comm_call(_comm_pair_gather(mine), "gather_halves")
    grads = {k: s.reshape(w[k].shape) for k, s in list(reduced.items()) + list(zip(tail, joined[:3]))}
    shapes = [w[k].shape for k in SMALL] + [(3, 4 * FF_BLK)]
    for k, g in zip(SMALL + ["conv_w_full"], _unpack(joined[3], shapes)):
        grads[k] = g
    grads["conv_w"] = lax.dynamic_slice_in_dim(grads.pop("conv_w_full"), chip * FF_BLK, FF_BLK, axis=1)

    delta, new_m, new_v = {}, {}, {}
    for k in BIG:
        delta[k], new_m[k], new_v[k] = _adamw(w[k], grads[k], m[k], v[k], "adamw_" + k)
    small_keys = SMALL + ["conv_w"]
    packs = [_pack([d[k] for k in small_keys], 512) for d in (w, grads, m, v)]
    outs = _adamw(*packs, "adamw_small")
    small_shapes = [w[k].shape for k in small_keys]
    for d, o in zip((delta, new_m, new_v), outs):
        for k, a in zip(small_keys, _unpack(o, small_shapes)):
            d[k] = a

    return (loss, grad_x[None], *[grads[k] for k in WEIGHTS], *[delta[k] for k in WEIGHTS],
            *[new_m[k] for k in WEIGHTS], *[new_v[k] for k in WEIGHTS])
```

```python
import numpy as np
import jax
import jax.numpy as jnp
from jax import lax
from jax.experimental import pallas as pl
from jax.experimental.pallas import tpu as pltpu

F32 = jnp.float32
BF16 = jnp.bfloat16
MESH = pl.DeviceIdType.MESH

EPS = 1e-6
POOL_WINDOWS = (2, 4, 8, 16)
POOL_GROUP = 128
SSM_GROUP = 16
SSM_STATE = 64
N_SSM_GROUPS = 32
N_STATE = N_SSM_GROUPS * SSM_STATE
QUAD = 256
N_QUAD = N_STATE // QUAD
SLAB = 256
D_SSM = 512
D_POOL = 512
D_FF = 2816
FF_BLK = 1408
HALO = 8
LANES = 128
ADAM_LR, ADAM_B1, ADAM_B2, ADAM_EPS, ADAM_WD, ADAM_STEP = 0.001, 0.9, 0.999, 1e-08, 0.01, 10
VMEM_LIMIT = 56 * 2 ** 20

TL = 512
TF = 256
TC = 256
SCAN_W = 512


def _cp(*sem):
    return pltpu.CompilerParams(dimension_semantics=sem, vmem_limit_bytes=VMEM_LIMIT)


def _dot_nn(a, b):
    return jnp.dot(a, b, preferred_element_type=F32)


def _dot_nt(a, b):
    return lax.dot_general(a, b, (((1,), (1,)), ((), ())), preferred_element_type=F32)


def _dot_tn(a, b):
    return lax.dot_general(a, b, (((0,), (0,)), ((), ())), preferred_element_type=F32)


def _rms_fwd(x, g):
    inv = lax.rsqrt(jnp.mean(x * x, axis=-1, keepdims=True) + EPS)
    xh = x * inv
    return xh * g, xh, inv


def _rms_bwd(dy, xh, inv, g):
    dg = jnp.sum(dy * xh, axis=0, keepdims=True)
    dxh = dy * g
    dx = inv * (dxh - xh * jnp.mean(dxh * xh, axis=-1, keepdims=True))
    return dx, dg


_GELU_C = 0.7978845608028654
_GELU_A = 0.044715


def _gelu(y):
    t = jnp.tanh(_GELU_C * (y + _GELU_A * (y * y * y)))
    return 0.5 * y * (1.0 + t), t


def _gelu_grad(y, t):
    return 0.5 * (1.0 + t) + 0.5 * y * (1.0 - t * t) * (_GELU_C * (1.0 + 3.0 * _GELU_A * y * y))


def _sigmoid(x):
    return 1.0 / (1.0 + jnp.exp(-x))


def _full(shape):
    n = len(shape)
    return pl.BlockSpec(shape, lambda *_: (0,) * n)


def _fill_ext(ext_ref, prev_ref, cur_ref, next_ref, i, n, rows):
    ext_ref[0:HALO, :] = jnp.where(i > 0, prev_ref[...], 0.0).astype(ext_ref.dtype)
    ext_ref[HALO:HALO + rows, :] = cur_ref[...]
    ext_ref[HALO + rows:2 * HALO + rows, :] = jnp.where(i < n - 1, next_ref[...], 0.0).astype(ext_ref.dtype)


def _in_proj(x, g, w):
    L, D = x.shape
    E = w.shape[1]

    def body(x_ref, g_ref, w_ref, u_ref, xn_ref):
        y, _, _ = _rms_fwd(x_ref[...], g_ref[...])
        yb = y.astype(BF16)
        xn_ref[...] = yb
        u_ref[...] = _dot_nn(yb, w_ref[...])

    return pl.pallas_call(
        body, name="in_proj", grid=(L // TL,),
        in_specs=[pl.BlockSpec((TL, D), lambda i: (i, 0)), _full((1, D)), _full(w.shape)],
        out_specs=[pl.BlockSpec((TL, E), lambda i: (i, 0)), pl.BlockSpec((TL, D), lambda i: (i, 0))],
        out_shape=[jax.ShapeDtypeStruct((L, E), F32), jax.ShapeDtypeStruct((L, D), BF16)],
        compiler_params=_cp("parallel"))(x, g, w)


def _halo_specs_1d(rows, width, L, col):
    rb = rows // HALO
    last = L // HALO - 1
    return [pl.BlockSpec((HALO, width), lambda i: (jnp.maximum(i * rb - 1, 0), col)),
            pl.BlockSpec((rows, width), lambda i: (i, col)),
            pl.BlockSpec((HALO, width), lambda i: (jnp.minimum((i + 1) * rb, last), col))]


def _pooled_from_ext(ext_ref, t0, rows, L):
    t = t0 + lax.broadcasted_iota(jnp.int32, (rows, 1), 0)
    outs = []
    for gi, w in enumerate(POOL_WINDOWS):
        half = w // 2
        cs = slice(gi * POOL_GROUP, (gi + 1) * POOL_GROUP)
        acc = ext_ref[pl.ds(HALO - half, rows), cs]
        for s in range(-half + 1, half):
            acc = acc + ext_ref[pl.ds(HALO + s, rows), cs]
        cnt = (jnp.minimum(t + half, L) - jnp.maximum(t - half, 0)).astype(F32)
        outs.append(acc / cnt - ext_ref[pl.ds(HALO, rows), cs])
    return outs


def _pool_fwd(u, pool_w_b, pool_scale, g_pool):
    L = u.shape[0]
    n = L // TL

    def body(prev_ref, cur_ref, next_ref, pw_ref, ps_ref, g_ref, out_ref, ext_ref):
        i = pl.program_id(0)
        _fill_ext(ext_ref, prev_ref, cur_ref, next_ref, i, n, TL)
        pooled = _pooled_from_ext(ext_ref, i * TL, TL, L)
        ypre = jnp.concatenate([_dot_nn(pooled[gi].astype(BF16), pw_ref[gi]) for gi in range(4)], axis=-1)
        yn, _, _ = _rms_fwd(ypre * ps_ref[...], g_ref[...])
        out_ref[...] = yn.astype(BF16)

    return pl.pallas_call(
        body, name="pool_fwd", grid=(n,),
        in_specs=_halo_specs_1d(TL, D_POOL, L, 0) + [_full(pool_w_b.shape), _full((1, D_POOL)), _full((1, D_POOL))],
        out_specs=pl.BlockSpec((TL, D_POOL), lambda i: (i, 0)),
        out_shape=jax.ShapeDtypeStruct((L, D_POOL), BF16),
        scratch_shapes=[pltpu.VMEM((TL + 2 * HALO, D_POOL), F32)],
        compiler_params=_cp("parallel"))(u, u, u, pool_w_b, pool_scale, g_pool)


def _pool_bwd_local(dh1, u, w_out_b, pool_w_b, pool_scale, g_pool, comm=None):
    L = u.shape[0]
    n = L // TL
    D = dh1.shape[1]

    def body(dh_ref, prev_ref, cur_ref, next_ref, wo_ref, pw_ref, ps_ref, g_ref,
             dp_ref, gpw_ref, gps_ref, gg_ref, ext_ref):
        i = pl.program_id(0)

        @pl.when(i == 0)
        def _():
            gpw_ref[...] = jnp.zeros_like(gpw_ref)
            gps_ref[...] = jnp.zeros_like(gps_ref)
            gg_ref[...] = jnp.zeros_like(gg_ref)

        _fill_ext(ext_ref, prev_ref, cur_ref, next_ref, i, n, TL)
        pooled = [p.astype(BF16) for p in _pooled_from_ext(ext_ref, i * TL, TL, L)]
        ypre = jnp.concatenate([_dot_nn(pooled[gi], pw_ref[gi]) for gi in range(4)], axis=-1)
        ps = ps_ref[...]
        g = g_ref[...]
        _, xh, inv = _rms_fwd(ypre * ps, g)
        d_yn = _dot_nt(dh_ref[...].astype(BF16), wo_ref[...])
        d_y, dg = _rms_bwd(d_yn, xh, inv, g)
        gg_ref[...] += dg
        gps_ref[...] += jnp.sum(d_y * ypre, axis=0, keepdims=True)
        d_ypre = (d_y * ps).astype(BF16)
        for gi in range(4):
            cs = slice(gi * POOL_GROUP, (gi + 1) * POOL_GROUP)
            dp_ref[:, cs] = _dot_nt(d_ypre[:, cs], pw_ref[gi])
            gpw_ref[gi] += _dot_tn(pooled[gi], d_ypre[:, cs])

    return _hosted_call(
        body, comm, name="pool_bwd_local", grid=(n,),
        in_specs=[pl.BlockSpec((TL, D), lambda i: (i, 0))] + _halo_specs_1d(TL, D_POOL, L, 0)
        + [pl.BlockSpec((D_POOL, D), lambda i: (0, 0)), _full(pool_w_b.shape), _full((1, D_POOL)), _full((1, D_POOL))],
        out_specs=[pl.BlockSpec((TL, D_POOL), lambda i: (i, 0)), _full(pool_w_b.shape),
                   _full((1, D_POOL)), _full((1, D_POOL))],
        out_shape=[jax.ShapeDtypeStruct((L, D_POOL), F32), jax.ShapeDtypeStruct(pool_w_b.shape, F32),
                   jax.ShapeDtypeStruct((1, D_POOL), F32), jax.ShapeDtypeStruct((1, D_POOL), F32)],
        scratch_shapes=[pltpu.VMEM((TL + 2 * HALO, D_POOL), F32)],
        args=(dh1, u, u, u, w_out_b, pool_w_b, pool_scale, g_pool))


def _pool_bwd_window(d_pooled):
    L = d_pooled.shape[0]
    n = L // TL
    R = TL + 2 * HALO

    def body(prev_ref, cur_ref, next_ref, out_ref, ext_ref, q_ref):
        i = pl.program_id(0)
        _fill_ext(ext_ref, prev_ref, cur_ref, next_ref, i, n, TL)
        tr = i * TL - HALO + lax.broadcasted_iota(jnp.int32, (R, 1), 0)
        for gi, w in enumerate(POOL_WINDOWS):
            half = w // 2
            cs = slice(gi * POOL_GROUP, (gi + 1) * POOL_GROUP)
            cnt = jnp.maximum(jnp.minimum(tr + half, L) - jnp.maximum(tr - half, 0), 1).astype(F32)
            q_ref[:, cs] = ext_ref[:, cs] / cnt
        for gi, w in enumerate(POOL_WINDOWS):
            half = w // 2
            cs = slice(gi * POOL_GROUP, (gi + 1) * POOL_GROUP)
            acc = q_ref[pl.ds(HALO - half + 1, TL), cs]
            for s in range(-half + 2, half + 1):
                acc = acc + q_ref[pl.ds(HALO + s, TL), cs]
            out_ref[:, cs] = acc - ext_ref[pl.ds(HALO, TL), cs]

    return pl.pallas_call(
        body, name="pool_bwd_window", grid=(n,),
        in_specs=_halo_specs_1d(TL, D_POOL, L, 0),
        out_specs=pl.BlockSpec((TL, D_POOL), lambda i: (i, 0)),
        out_shape=jax.ShapeDtypeStruct((L, D_POOL), F32),
        scratch_shapes=[pltpu.VMEM((R, D_POOL), F32), pltpu.VMEM((R, D_POOL), F32)],
        compiler_params=_cp("parallel"))(d_pooled, d_pooled, d_pooled)


def _ssm_param_fn(lnar, aim, ldt):
    dt = jnp.exp(ldt)
    a_re = -jnp.exp(lnar)
    mag = jnp.exp(a_re * dt)
    ang = aim * dt
    lr, li = mag * jnp.cos(ang), mag * jnp.sin(ang)
    den = a_re * a_re + aim * aim
    fr = ((lr - 1.0) * a_re + li * aim) / den
    fi = (li * a_re - (lr - 1.0) * aim) / den
    return lr, li, fr, fi


def _ssm_params(lnar, aim, ldt):
    def body(a_ref, b_ref, c_ref, lr_ref, li_ref, fr_ref, fi_ref):
        lr, li, fr, fi = _ssm_param_fn(a_ref[...], b_ref[...], c_ref[...])
        lr_ref[...] = lr
        li_ref[...] = li
        fr_ref[...] = fr
        fi_ref[...] = fi

    sh = jax.ShapeDtypeStruct(lnar.shape, F32)
    return pl.pallas_call(body, name="ssm_params", out_shape=[sh] * 4)(lnar, aim, ldt)


def _ssm_params_bwd(lnar, aim, ldt, glr, gli, gfr, gfi):
    def body(a_ref, b_ref, c_ref, g0, g1, g2, g3, da_ref, db_ref, dc_ref):
        _, vjp = jax.vjp(_ssm_param_fn, a_ref[...], b_ref[...], c_ref[...])
        da, db, dc = vjp((g0[...], g1[...], g2[...], g3[...]))
        da_ref[...] = da
        db_ref[...] = db
        dc_ref[...] = jnp.sum(dc, axis=1, keepdims=True)

    return pl.pallas_call(
        body, name="ssm_params_bwd",
        out_shape=[jax.ShapeDtypeStruct(lnar.shape, F32), jax.ShapeDtypeStruct(aim.shape, F32),
                   jax.ShapeDtypeStruct((ldt.shape[0], 1), F32)])(lnar, aim, ldt, glr, gli, gfr, gfi)


def _scan_tables(lam4):
    def build(lr, li, reverse, out_ref, k):
        row = lax.broadcasted_iota(jnp.int32, (8, N_STATE), 0)
        lrb = jnp.broadcast_to(lr, (8, N_STATE))
        lib = jnp.broadcast_to(li, (8, N_STATE))
        pr, pi = lrb, lib
        for s, sh in enumerate((1, 2, 4)):
            mask = (row < 8 - sh) if reverse else (row >= sh)
            out_ref[k, 2 * s] = jnp.where(mask, pr, 0.0)
            out_ref[k, 2 * s + 1] = jnp.where(mask, pi, 0.0)
            pr, pi = pr * pr - pi * pi, 2.0 * pr * pi
        pr, pi = lrb, lib
        p8r = jnp.zeros((8, N_STATE), F32)
        p8i = jnp.zeros((8, N_STATE), F32)
        for j in range(8):
            r = 7 - j if reverse else j
            p8r = jnp.where(row == r, pr, p8r)
            p8i = jnp.where(row == r, pi, p8i)
            pr, pi = pr * lrb - pi * lib, pr * lib + pi * lrb
        out_ref[k, 6] = p8r
        out_ref[k, 7] = p8i

    def body(lam_ref, out_ref):
        l0r, l0i, l1r, l1i = (lam_ref[j:j + 1, :] for j in range(4))
        build(l0r, l0i, False, out_ref, 0)
        build(l0r, -l0i, True, out_ref, 1)
        build(l1r, l1i, True, out_ref, 2)
        build(l1r, -l1i, False, out_ref, 3)

    return pl.pallas_call(body, name="scan_tables",
                          out_shape=jax.ShapeDtypeStruct((4, 8, 8, N_STATE), F32))(lam4)


def _b_block(g):
    q, gl = divmod(g, 4)
    r0, c0 = gl * SSM_STATE, (q % 4) * 4 * SSM_GROUP + gl * SSM_GROUP
    return q, slice(r0, r0 + SSM_STATE), slice(c0, c0 + SSM_GROUP)


def _c_block(g):
    q, rows, cols = _b_block(g)
    return q, cols, rows


def _ssm_expand(b_re, b_im, c_re, c_im):
    def body(bre_ref, bim_ref, cre_ref, cim_ref, *rest):
        outs, tmp = rest[:8], rest[8]
        for d in range(2):
            for j, (src, where) in enumerate(((bre_ref, _b_block), (bim_ref, _b_block),
                                              (cre_ref, _c_block), (cim_ref, _c_block))):
                tmp[...] = jnp.zeros_like(tmp)
                for g in range(N_SSM_GROUPS):
                    q, rows, cols = where(g)
                    tmp[q, rows, cols] = src[d, g]
                outs[4 * d + j][...] = tmp[...].astype(BF16)

    dense = jax.ShapeDtypeStruct((N_QUAD, QUAD, SLAB), BF16)
    return pl.pallas_call(body, name="ssm_expand", out_shape=[dense] * 8,
                          scratch_shapes=[pltpu.VMEM((N_QUAD, QUAD, SLAB), F32)],
                          compiler_params=pltpu.CompilerParams(vmem_limit_bytes=VMEM_LIMIT))(b_re, b_im, c_re, c_im)


def _scan_rows(src_re, src_im, dst_re, dst_im, tab_ref, k, carry_re, carry_im, rows, reverse, s_refs=None):
    ng = rows // 8
    edge = 0 if reverse else 7
    row_id = lax.broadcasted_iota(jnp.int32, (8, SCAN_W), 0)
    sums = []
    for lt in range(N_STATE // SCAN_W):
        sl = slice(lt * SCAN_W, (lt + 1) * SCAN_W)

        def step(r, c, sl=sl):
            tabs = [tab_ref[k, j, :, sl] for j in range(8)]
            cr, ci = c[0], c[1]
            row = pl.multiple_of((ng - 1 - r) * 8 if reverse else r * 8, 8)
            xr = src_re[pl.ds(row, 8), sl]
            xi = src_im[pl.ds(row, 8), sl]
            for s, sh in enumerate((1, 2, 4)):
                amt = 8 - sh if reverse else sh
                rr = pltpu.roll(xr, amt, 0)
                ri = pltpu.roll(xi, amt, 0)
                mr, mi = tabs[2 * s], tabs[2 * s + 1]
                xr, xi = xr + mr * rr - mi * ri, xi + mr * ri + mi * rr
            xr, xi = xr + tabs[6] * cr - tabs[7] * ci, xi + tabs[6] * ci + tabs[7] * cr
            dst_re[pl.ds(row, 8), sl] = xr
            dst_im[pl.ds(row, 8), sl] = xi
            ncr = jnp.broadcast_to(xr[edge:edge + 1, :], (8, SCAN_W))
            nci = jnp.broadcast_to(xi[edge:edge + 1, :], (8, SCAN_W))
            if s_refs is None:
                return ncr, nci
            amt = 7 if reverse else 1
            far = 7 if reverse else 0
            nr = jnp.where(row_id == far, cr, pltpu.roll(xr, amt, 0))
            ni = jnp.where(row_id == far, ci, pltpu.roll(xi, amt, 0))
            sr = s_refs[0][pl.ds(row, 8), sl]
            si = s_refs[1][pl.ds(row, 8), sl]
            return ncr, nci, c[2] + nr * sr + ni * si, c[3] + ni * sr - nr * si

        init = (carry_re[:, sl], carry_im[:, sl])
        if s_refs is not None:
            init = init + (jnp.zeros((8, SCAN_W), F32), jnp.zeros((8, SCAN_W), F32))
        out = lax.fori_loop(0, ng, step, init)
        carry_re[:, sl] = out[0]
        carry_im[:, sl] = out[1]
        if s_refs is not None:
            sums.append((jnp.sum(out[2], axis=0, keepdims=True), jnp.sum(out[3], axis=0, keepdims=True)))
    return sums


def _ssm_scan_fwd(u, b_re, b_im, c_re, c_im, f2, tables, k, reverse, comm=None):
    L = u.shape[0]
    nc = L // TC
    chunk = (lambda i: nc - 1 - i) if reverse else (lambda i: i)

    def body(u_ref, bre_ref, bim_ref, cre_ref, cim_ref, f_ref, tab_ref,
             y_ref, sre_ref, sim_ref, in_re, in_im, carry_re, carry_im):
        @pl.when(pl.program_id(0) == 0)
        def _():
            carry_re[...] = jnp.zeros_like(carry_re)
            carry_im[...] = jnp.zeros_like(carry_im)

        ub = u_ref[...].astype(BF16)
        for q in range(N_QUAD):
            qs = slice(q * QUAD, (q + 1) * QUAD)
            us = ub[:, (q // 4) * SLAB:(q // 4 + 1) * SLAB]
            bur = _dot_nt(us, bre_ref[q])
            bui = _dot_nt(us, bim_ref[q])
            fr = f_ref[0:1, qs]
            fi = f_ref[1:2, qs]
            in_re[:, qs] = fr * bur - fi * bui
            in_im[:, qs] = fr * bui + fi * bur
        _scan_rows(in_re, in_im, sre_ref, sim_ref, tab_ref, k, carry_re, carry_im, TC, reverse)
        for j in range(D_SSM // SLAB):
            acc = jnp.zeros((TC, SLAB), F32)
            for q in range(4 * j, 4 * j + 4):
                qs = slice(q * QUAD, (q + 1) * QUAD)
                acc = acc + _dot_nt(sre_ref[:, qs].astype(BF16), cre_ref[q])
                acc = acc - _dot_nt(sim_ref[:, qs].astype(BF16), cim_ref[q])
            y_ref[:, j * SLAB:(j + 1) * SLAB] = acc

    return _hosted_call(
        body, comm, name="ssm_scan_rev" if reverse else "ssm_scan_fwd", grid=(nc,),
        in_specs=[pl.BlockSpec((TC, D_SSM), lambda i: (chunk(i), 1))]
        + [_full(b_re.shape)] * 4 + [_full(f2.shape), _full(tables.shape)],
        out_specs=[pl.BlockSpec((TC, D_SSM), lambda i: (chunk(i), 0)),
                   pl.BlockSpec((TC, N_STATE), lambda i: (chunk(i), 0)),
                   pl.BlockSpec((TC, N_STATE), lambda i: (chunk(i), 0))],
        out_shape=[jax.ShapeDtypeStruct((L, D_SSM), F32), jax.ShapeDtypeStruct((L, N_STATE), F32),
                   jax.ShapeDtypeStruct((L, N_STATE), F32)],
        scratch_shapes=[pltpu.VMEM((TC, N_STATE), F32), pltpu.VMEM((TC, N_STATE), F32),
                        pltpu.VMEM((8, N_STATE), F32), pltpu.VMEM((8, N_STATE), F32)],
        args=(u, b_re, b_im, c_re, c_im, f2, tables))


def _ssm_scan_bwd(dy, u, s_re, s_im, b_re, b_im, c_re, c_im, f2, tables, k, reverse, comm=None):
    L = u.shape[0]
    nc = L // TC
    chunk = (lambda i: nc - 1 - i) if reverse else (lambda i: i)

    def body(dy_ref, u_ref, sre_ref, sim_ref, bre_ref, bim_ref, cre_ref, cim_ref, f_ref, tab_ref,
             du_ref, ob_re, ob_im, oc_re, oc_im, gv_ref,
             a_re, a_im, carry_re, carry_im, gbr_ref, gbi_ref, gcr_ref, gci_ref):
        @pl.when(pl.program_id(0) == 0)
        def _():
            carry_re[...] = jnp.zeros_like(carry_re)
            carry_im[...] = jnp.zeros_like(carry_im)
            for r in (gbr_ref, gbi_ref, gcr_ref, gci_ref, gv_ref):
                r[...] = jnp.zeros_like(r)

        dyb = dy_ref[...].astype(BF16)
        ub = u_ref[...].astype(BF16)
        for q in range(N_QUAD):
            qs = slice(q * QUAD, (q + 1) * QUAD)
            ds = dyb[:, (q // 4) * SLAB:(q // 4 + 1) * SLAB]
            a_re[:, qs] = _dot_nn(ds, cre_ref[q])
            a_im[:, qs] = -_dot_nn(ds, cim_ref[q])
            gcr_ref[q] += _dot_tn(ds, sre_ref[:, qs].astype(BF16))
            gci_ref[q] -= _dot_tn(ds, sim_ref[:, qs].astype(BF16))
        sums = _scan_rows(a_re, a_im, a_re, a_im, tab_ref, k, carry_re, carry_im, TC, reverse,
                          s_refs=(sre_ref, sim_ref))
        for lt, (glr, gli) in enumerate(sums):
            sl = slice(lt * SCAN_W, (lt + 1) * SCAN_W)
            gv_ref[0:1, sl] += glr
            gv_ref[1:2, sl] += gli
        for j in range(D_SSM // SLAB):
            us = ub[:, j * SLAB:(j + 1) * SLAB]
            acc = jnp.zeros((TC, SLAB), F32)
            for q in range(4 * j, 4 * j + 4):
                qs = slice(q * QUAD, (q + 1) * QUAD)
                ar = a_re[:, qs]
                ai = a_im[:, qs]
                bur = _dot_nt(us, bre_ref[q])
                bui = _dot_nt(us, bim_ref[q])
                gv_ref[2:3, qs] += jnp.sum(ar * bur + ai * bui, axis=0, keepdims=True)
                gv_ref[3:4, qs] += jnp.sum(ai * bur - ar * bui, axis=0, keepdims=True)
                fr = f_ref[0:1, qs]
                fi = f_ref[1:2, qs]
                dbr = (fr * ar + fi * ai).astype(BF16)
                dbi = (fr * ai - fi * ar).astype(BF16)
                gbr_ref[q] += _dot_tn(dbr, us)
                gbi_ref[q] += _dot_tn(dbi, us)
                acc = acc + _dot_nn(dbr, bre_ref[q]) + _dot_nn(dbi, bim_ref[q])
            du_ref[:, j * SLAB:(j + 1) * SLAB] = acc

        @pl.when(pl.program_id(0) == nc - 1)
        def _():
            for g in range(N_SSM_GROUPS):
                q, rows, cols = _b_block(g)
                ob_re[g] = gbr_ref[q, rows, cols]
                ob_im[g] = gbi_ref[q, rows, cols]
                oc_re[g] = gcr_ref[q, cols, rows]
                oc_im[g] = gci_ref[q, cols, rows]

    gb = jax.ShapeDtypeStruct((N_SSM_GROUPS, SSM_STATE, SSM_GROUP), F32)
    gc = jax.ShapeDtypeStruct((N_SSM_GROUPS, SSM_GROUP, SSM_STATE), F32)
    dense = pltpu.VMEM((N_QUAD, QUAD, SLAB), F32)
    return _hosted_call(
        body, comm, name="ssm_bwd_rev" if reverse else "ssm_bwd_fwd", grid=(nc,),
        in_specs=[pl.BlockSpec((TC, D_SSM), lambda i: (chunk(i), 0)),
                  pl.BlockSpec((TC, D_SSM), lambda i: (chunk(i), 1)),
                  pl.BlockSpec((TC, N_STATE), lambda i: (chunk(i), 0)),
                  pl.BlockSpec((TC, N_STATE), lambda i: (chunk(i), 0))]
        + [_full(b_re.shape)] * 4 + [_full(f2.shape), _full(tables.shape)],
        out_specs=[pl.BlockSpec((TC, D_SSM), lambda i: (chunk(i), 0)), _full(gb.shape), _full(gb.shape),
                   _full(gc.shape), _full(gc.shape), _full((4, N_STATE))],
        out_shape=[jax.ShapeDtypeStruct((L, D_SSM), F32), gb, gb, gc, gc, jax.ShapeDtypeStruct((4, N_STATE), F32)],
        scratch_shapes=[pltpu.VMEM((TC, N_STATE), F32), pltpu.VMEM((TC, N_STATE), F32),
                        pltpu.VMEM((8, N_STATE), F32), pltpu.VMEM((8, N_STATE), F32), dense, dense, dense, dense],
        args=(dy, u, s_re, s_im, b_re, b_im, c_re, c_im, f2, tables))


def _ssm_post(yf, yb, u, d, glu_w, glu_b):
    y = yf + yb + d * u
    z, t = _gelu(y)
    zb = z.astype(BF16)
    gate = _sigmoid(_dot_nn(zb, glu_w) + glu_b)
    return y, z, t, zb, gate


def _mix_out(yn_pool, yf, yb, u, x, ssm_d, glu_w_b, glu_b, g_ssm, w_out_b, g_ffn):
    L, D = x.shape

    def body(ynp_ref, yf_ref, yb_ref, u_ref, x_ref, d_ref, gw_ref, gb_ref, gs_ref, wo_ref, gf_ref,
             h1_ref, hn_ref, ycat_ref):
        _, z, _, _, gate = _ssm_post(yf_ref[...], yb_ref[...], u_ref[...], d_ref[...], gw_ref[...], gb_ref[...])
        yns, _, _ = _rms_fwd(z * gate, gs_ref[...])
        ynsb = yns.astype(BF16)
        ynp = ynp_ref[...]
        ycat_ref[:, 0:D_POOL] = ynp
        ycat_ref[:, D_POOL:D] = ynsb
        h1 = x_ref[...] + _dot_nn(ynp, wo_ref[0:D_POOL, :]) + _dot_nn(ynsb, wo_ref[D_POOL:D, :])
        h1_ref[...] = h1
        hn, _, _ = _rms_fwd(h1, gf_ref[...])
        hn_ref[...] = hn.astype(BF16)

    half = lambda c: pl.BlockSpec((TL, D_SSM), lambda i: (i, c))
    row = pl.BlockSpec((TL, D), lambda i: (i, 0))
    return pl.pallas_call(
        body, name="mix_out", grid=(L // TL,),
        in_specs=[half(0), half(0), half(0), half(1), row, _full((1, D_SSM)), _full(glu_w_b.shape),
                  _full((1, D_SSM)), _full((1, D_SSM)), _full(w_out_b.shape), _full((1, D))],
        out_specs=[row, row, row],
        out_shape=[jax.ShapeDtypeStruct((L, D), F32), jax.ShapeDtypeStruct((L, D), BF16),
                   jax.ShapeDtypeStruct((L, D), BF16)],
        compiler_params=_cp("parallel"))(yn_pool, yf, yb, u, x, ssm_d, glu_w_b, glu_b, g_ssm, w_out_b, g_ffn)


def _ssm_bwd_local(dh1, yf, yb, u, ssm_d, glu_w_b, glu_b, g_ssm, w_out_b):
    L, D = dh1.shape

    def body(dh_ref, yf_ref, yb_ref, u_ref, d_ref, gw_ref, gb_ref, gs_ref, wo_ref,
             dy_ref, du_ref, ggw_ref, ggb_ref, gd_ref, ggs_ref):
        @pl.when(pl.program_id(0) == 0)
        def _():
            for r in (ggw_ref, ggb_ref, gd_ref, ggs_ref):
                r[...] = jnp.zeros_like(r)

        u = u_ref[...]
        d = d_ref[...]
        y, z, t, zb, gate = _ssm_post(yf_ref[...], yb_ref[...], u, d, gw_ref[...], gb_ref[...])
        gs = gs_ref[...]
        _, xh, inv = _rms_fwd(z * gate, gs)
        d_yn = _dot_nt(dh_ref[...].astype(BF16), wo_ref[...])
        d_o, dgs = _rms_bwd(d_yn, xh, inv, gs)
        ggs_ref[...] += dgs
        d_zg = d_o * z * gate * (1.0 - gate)
        d_zgb = d_zg.astype(BF16)
        ggb_ref[...] += jnp.sum(d_zg, axis=0, keepdims=True)
        ggw_ref[...] += _dot_tn(zb, d_zgb)
        d_z = d_o * gate + _dot_nt(d_zgb, gw_ref[...])
        d_y = d_z * _gelu_grad(y, t)
        gd_ref[...] += jnp.sum(d_y * u, axis=0, keepdims=True)
        dy_ref[...] = d_y
        du_ref[...] = d_y * d

    half = lambda c: pl.BlockSpec((TL, D_SSM), lambda i: (i, c))
    vec = _full((1, D_SSM))
    return pl.pallas_call(
        body, name="ssm_bwd_local", grid=(L // TL,),
        in_specs=[pl.BlockSpec((TL, D), lambda i: (i, 0)), half(0), half(0), half(1), vec, _full(glu_w_b.shape),
                  vec, vec, pl.BlockSpec((D_SSM, D), lambda i: (1, 0))],
        out_specs=[half(0), half(0), _full(glu_w_b.shape), vec, vec, vec],
        out_shape=[jax.ShapeDtypeStruct((L, D_SSM), F32), jax.ShapeDtypeStruct((L, D_SSM), F32),
                   jax.ShapeDtypeStruct(glu_w_b.shape, F32)] + [jax.ShapeDtypeStruct((1, D_SSM), F32)] * 3,
        compiler_params=_cp("arbitrary"))(dh1, yf, yb, u, ssm_d, glu_w_b, glu_b, g_ssm, w_out_b)


def _in_bwd(du_pool, du_a, du_b, du_c, dh1, x, g, w_in_b):
    L, D = x.shape

    def body(p_ref, a_ref, b_ref, c_ref, dh_ref, x_ref, g_ref, w_ref, dx_ref, dub_ref, gg_ref):
        @pl.when(pl.program_id(0) == 0)
        def _():
            gg_ref[...] = jnp.zeros_like(gg_ref)

        dub_ref[:, 0:D_POOL] = p_ref[...].astype(BF16)
        dub_ref[:, D_POOL:D] = (a_ref[...] + b_ref[...] + c_ref[...]).astype(BF16)
        d_xn = _dot_nt(dub_ref[...], w_ref[...])
        gv = g_ref[...]
        _, xh, inv = _rms_fwd(x_ref[...], gv)
        dx, dg = _rms_bwd(d_xn, xh, inv, gv)
        gg_ref[...] += dg
        dx_ref[...] = dh_ref[...] + dx

    half = pl.BlockSpec((TL, D_SSM), lambda i: (i, 0))
    row = pl.BlockSpec((TL, D), lambda i: (i, 0))
    return pl.pallas_call(
        body, name="in_bwd", grid=(L // TL,),
        in_specs=[half, half, half, half, row, row, _full((1, D)), _full(w_in_b.shape)],
        out_specs=[row, row, _full((1, D))],
        out_shape=[jax.ShapeDtypeStruct((L, D), F32), jax.ShapeDtypeStruct((L, D), BF16),
                   jax.ShapeDtypeStruct((1, D), F32)],
        compiler_params=_cp("arbitrary"))(du_pool, du_a, du_b, du_c, dh1, x, g, w_in_b)


def _ffn_up(hn, w_up4):
    L, D = hn.shape

    def body(h_ref, w_ref, o_ref):
        o_ref[...] = _dot_nn(h_ref[...], w_ref[...])

    return pl.pallas_call(
        body, name="ffn_up", grid=(4, L // TF),
        in_specs=[pl.BlockSpec((TF, D), lambda j, i: (i, 0)), pl.BlockSpec((None, D, FF_BLK), lambda j, i: (j, 0, 0))],
        out_specs=pl.BlockSpec((TF, FF_BLK), lambda j, i: (i, j)),
        out_shape=jax.ShapeDtypeStruct((L, 4 * FF_BLK), F32),
        compiler_params=_cp("parallel", "parallel"))(hn, w_up4)


def _halo_specs_2d(rows, width, L, col, order):
    rb = rows // HALO
    last = L // HALO - 1
    if order == "ik":
        wrap = lambda f: (lambda i, k: f(i, k))
    else:
        wrap = lambda f: (lambda k, i: f(i, k))
    return [pl.BlockSpec((HALO, width), wrap(lambda i, k: (jnp.maximum(i * rb - 1, 0), col(k)))),
            pl.BlockSpec((rows, width), wrap(lambda i, k: (i, col(k)))),
            pl.BlockSpec((HALO, width), wrap(lambda i, k: (jnp.minimum((i + 1) * rb, last), col(k))))]


def _conv3(ext_ref, w_ref, b_ref, rows):
    return (ext_ref[pl.ds(HALO - 1, rows), :] * w_ref[0:1, :] + ext_ref[pl.ds(HALO, rows), :] * w_ref[1:2, :]
            + ext_ref[pl.ds(HALO + 1, rows), :] * w_ref[2:3, :] + b_ref[...])


def _ffn_down_loss(up, conv_w, conv_b, w_down_b, h1, target, g_final):
    L, D = h1.shape
    n = L // TF
    nk = D_FF // FF_BLK

    def body(vp, vc, vn, gp, gc, gn, wv_ref, wg_ref, bv_ref, bg_ref, wd_ref, h1_ref, t_ref, gf_ref,
             a_ref, dh2_ref, dh2b_ref, loss_ref, gg_ref, ext_v, ext_g, acc_ref):
        i = pl.program_id(0)
        k = pl.program_id(1)

        @pl.when((i == 0) & (k == 0))
        def _():
            loss_ref[...] = jnp.zeros_like(loss_ref)
            gg_ref[...] = jnp.zeros_like(gg_ref)

        @pl.when(k == 0)
        def _():
            acc_ref[...] = jnp.zeros_like(acc_ref)

        _fill_ext(ext_v, vp, vc, vn, i, n, TF)
        _fill_ext(ext_g, gp, gc, gn, i, n, TF)
        val = _conv3(ext_v, wv_ref, bv_ref, TF)
        gate = _conv3(ext_g, wg_ref, bg_ref, TF)
        a = (val * (gate * _sigmoid(gate))).astype(BF16)
        a_ref[...] = a
        acc_ref[...] += _dot_nn(a, wd_ref[...])

        @pl.when(k == nk - 1)
        def _():
            gf = gf_ref[...]
            y, xh, inv = _rms_fwd(h1_ref[...] + acc_ref[...], gf)
            diff = y - t_ref[...]
            part = 0.5 * jnp.sum(jnp.mean(diff * diff, axis=-1, keepdims=True), axis=0, keepdims=True)
            loss_ref[...] += jnp.broadcast_to(part, loss_ref.shape)
            dx, dg = _rms_bwd(diff * (1.0 / D), xh, inv, gf)
            gg_ref[...] += dg
            dh2_ref[...] = dx
            dh2b_ref[...] = dx.astype(BF16)

    row = pl.BlockSpec((TF, D), lambda i, k: (i, 0))
    cw = lambda off: pl.BlockSpec((3, FF_BLK), lambda i, k: (0, k + off))
    cb = lambda off: pl.BlockSpec((1, FF_BLK), lambda i, k: (0, k + off))
    return pl.pallas_call(
        body, name="ffn_down_loss", grid=(n, nk),
        in_specs=_halo_specs_2d(TF, FF_BLK, L, lambda k: k, "ik") + _halo_specs_2d(TF, FF_BLK, L, lambda k: k + nk, "ik")
        + [cw(0), cw(nk), cb(0), cb(nk), pl.BlockSpec((FF_BLK, D), lambda i, k: (k, 0)), row, row, _full((1, D))],
        out_specs=[pl.BlockSpec((TF, FF_BLK), lambda i, k: (i, k)), row, row, _full((1, LANES)), _full((1, D))],
        out_shape=[jax.ShapeDtypeStruct((L, D_FF), BF16), jax.ShapeDtypeStruct((L, D), F32),
                   jax.ShapeDtypeStruct((L, D), BF16), jax.ShapeDtypeStruct((1, LANES), F32),
                   jax.ShapeDtypeStruct((1, D), F32)],
        scratch_shapes=[pltpu.VMEM((TF + 2 * HALO, FF_BLK), F32), pltpu.VMEM((TF + 2 * HALO, FF_BLK), F32),
                        pltpu.VMEM((TF, D), F32)],
        compiler_params=_cp("arbitrary", "arbitrary"))(
            up, up, up, up, up, up, conv_w, conv_w, conv_b, conv_b, w_down_b, h1, target, g_final)


def _ffn_act_bwd(up, conv_w, conv_b, w_down_b, dh2):
    L, D = dh2.shape
    n = L // TF
    nk = D_FF // FF_BLK

    def body(vp, vc, vn, gp, gc, gn, wv_ref, wg_ref, bv_ref, bg_ref, wd_ref, dh_ref,
             dv_ref, dg_ref, gcv_ref, gcg_ref, ext_v, ext_g):
        i = pl.program_id(1)

        @pl.when(i == 0)
        def _():
            gcv_ref[...] = jnp.zeros_like(gcv_ref)
            gcg_ref[...] = jnp.zeros_like(gcg_ref)

        _fill_ext(ext_v, vp, vc, vn, i, n, TF)
        _fill_ext(ext_g, gp, gc, gn, i, n, TF)
        val = _conv3(ext_v, wv_ref, bv_ref, TF)
        gate = _conv3(ext_g, wg_ref, bg_ref, TF)
        d_a = _dot_nt(dh_ref[...].astype(BF16), wd_ref[...])
        sg = _sigmoid(gate)
        d_val = d_a * (gate * sg)
        d_gate = d_a * val * (sg * (1.0 + gate * (1.0 - sg)))
        dv_ref[...] = d_val
        dg_ref[...] = d_gate
        for dref, ext, gref in ((d_val, ext_v, gcv_ref), (d_gate, ext_g, gcg_ref)):
            for j in range(3):
                gref[j:j + 1, :] += jnp.sum(dref * ext[pl.ds(HALO - 1 + j, TF), :], axis=0, keepdims=True)
            gref[3:4, :] += jnp.sum(dref, axis=0, keepdims=True)

    cw = lambda off: pl.BlockSpec((3, FF_BLK), lambda k, i: (0, k + off))
    cb = lambda off: pl.BlockSpec((1, FF_BLK), lambda k, i: (0, k + off))
    blk = pl.BlockSpec((TF, FF_BLK), lambda k, i: (i, k))
    acc = pl.BlockSpec((4, FF_BLK), lambda k, i: (0, k))
    return pl.pallas_call(
        body, name="ffn_act_bwd", grid=(nk, n),
        in_specs=_halo_specs_2d(TF, FF_BLK, L, lambda k: k, "ki") + _halo_specs_2d(TF, FF_BLK, L, lambda k: k + nk, "ki")
        + [cw(0), cw(nk), cb(0), cb(nk), pl.BlockSpec((FF_BLK, D), lambda k, i: (k, 0)),
           pl.BlockSpec((TF, D), lambda k, i: (i, 0))],
        out_specs=[blk, blk, acc, acc],
        out_shape=[jax.ShapeDtypeStruct((L, D_FF), F32), jax.ShapeDtypeStruct((L, D_FF), F32),
                   jax.ShapeDtypeStruct((4, D_FF), F32), jax.ShapeDtypeStruct((4, D_FF), F32)],
        scratch_shapes=[pltpu.VMEM((TF + 2 * HALO, FF_BLK), F32), pltpu.VMEM((TF + 2 * HALO, FF_BLK), F32)],
        compiler_params=_cp("arbitrary", "arbitrary"))(
            up, up, up, up, up, up, conv_w, conv_w, conv_b, conv_b, w_down_b, dh2)


def _ffn_up_bwd(d_val, d_gate, conv_w, w_up4, h1, dh2, g_ffn):
    L, D = h1.shape
    n = L // TF
    nk = D_FF // FF_BLK

    def body(vp, vc, vn, gp, gc, gn, wv_ref, wg_ref, uv_ref, ug_ref, h1_ref, dh2_ref, g_ref,
             dup_ref, dh1_ref, dh1b_ref, gg_ref, ext_v, ext_g, acc_ref):
        i = pl.program_id(0)
        k = pl.program_id(1)

        @pl.when((i == 0) & (k == 0))
        def _():
            gg_ref[...] = jnp.zeros_like(gg_ref)

        @pl.when(k == 0)
        def _():
            acc_ref[...] = jnp.zeros_like(acc_ref)

        _fill_ext(ext_v, vp, vc, vn, i, n, TF)
        _fill_ext(ext_g, gp, gc, gn, i, n, TF)
        for j, (ext, w_ref, wu_ref) in enumerate(((ext_v, wv_ref, uv_ref), (ext_g, wg_ref, ug_ref))):
            d_up = (ext[pl.ds(HALO + 1, TF), :] * w_ref[0:1, :] + ext[pl.ds(HALO, TF), :] * w_ref[1:2, :]
                    + ext[pl.ds(HALO - 1, TF), :] * w_ref[2:3, :]).astype(BF16)
            dup_ref[j] = d_up
            acc_ref[...] += _dot_nt(d_up, wu_ref[...])

        @pl.when(k == nk - 1)
        def _():
            g = g_ref[...]
            _, xh, inv = _rms_fwd(h1_ref[...], g)
            dx, dg = _rms_bwd(acc_ref[...], xh, inv, g)
            gg_ref[...] += dg
            dh1 = dh2_ref[...] + dx
            dh1_ref[...] = dh1
            dh1b_ref[...] = dh1.astype(BF16)

    row = pl.BlockSpec((TF, D), lambda i, k: (i, 0))
    cw = lambda off: pl.BlockSpec((3, FF_BLK), lambda i, k: (0, k + off))
    wu = lambda off: pl.BlockSpec((None, D, FF_BLK), lambda i, k: (k + off, 0, 0))
    return pl.pallas_call(
        body, name="ffn_up_bwd", grid=(n, nk),
        in_specs=_halo_specs_2d(TF, FF_BLK, L, lambda k: k, "ik") + _halo_specs_2d(TF, FF_BLK, L, lambda k: k, "ik")
        + [cw(0), cw(nk), wu(0), wu(nk), row, row, _full((1, D))],
        out_specs=[pl.BlockSpec((2, None, TF, FF_BLK), lambda i, k: (0, k, i, 0)), row, row, _full((1, D))],
        out_shape=[jax.ShapeDtypeStruct((2, nk, L, FF_BLK), BF16), jax.ShapeDtypeStruct((L, D), F32),
                   jax.ShapeDtypeStruct((L, D), BF16), jax.ShapeDtypeStruct((1, D), F32)],
        scratch_shapes=[pltpu.VMEM((TF + 2 * HALO, FF_BLK), F32), pltpu.VMEM((TF + 2 * HALO, FF_BLK), F32),
                        pltpu.VMEM((TF, D), F32)],
        compiler_params=_cp("arbitrary", "arbitrary"))(
            d_val, d_val, d_val, d_gate, d_gate, d_gate, conv_w, conv_w, w_up4, w_up4, h1, dh2, g_ffn)


def _matmul_tn(a, b, tm, tn, name, tk=512):
    L, M = a.shape
    N = b.shape[1]

    def body(a_ref, b_ref, o_ref):
        @pl.when(pl.program_id(2) == 0)
        def _():
            o_ref[...] = jnp.zeros_like(o_ref)

        o_ref[...] += _dot_tn(a_ref[...], b_ref[...])

    return pl.pallas_call(
        body, name=name, grid=(M // tm, N // tn, L // tk),
        in_specs=[pl.BlockSpec((tk, tm), lambda m, n, l: (l, m)), pl.BlockSpec((tk, tn), lambda m, n, l: (l, n))],
        out_specs=pl.BlockSpec((tm, tn), lambda m, n, l: (m, n)),
        out_shape=jax.ShapeDtypeStruct((M, N), F32),
        compiler_params=_cp("parallel", "parallel", "arbitrary"))(a, b)


def _matmul_tn_blocks(a, b, tm, name, tk=512):
    L, M = a.shape
    J, _, N = b.shape

    def body(a_ref, b_ref, o_ref):
        @pl.when(pl.program_id(2) == 0)
        def _():
            o_ref[...] = jnp.zeros_like(o_ref)

        o_ref[...] += _dot_tn(a_ref[...], b_ref[...])

    return pl.pallas_call(
        body, name=name, grid=(M // tm, J, L // tk),
        in_specs=[pl.BlockSpec((tk, tm), lambda m, j, l: (l, m)), pl.BlockSpec((None, tk, N), lambda m, j, l: (j, l, 0))],
        out_specs=pl.BlockSpec((None, tm, N), lambda m, j, l: (j, m, 0)),
        out_shape=jax.ShapeDtypeStruct((J, M, N), F32),
        compiler_params=_cp("parallel", "parallel", "arbitrary"))(a, b)


def _row_tile(rows):
    for t in (512, 352, 256, 128, 64, 8):
        if rows % t == 0:
            return t
    return rows


def _add_half(g, r, c_arr, name, out_dtype=F32):
    _, _, R, C = g.shape
    tr = _row_tile(R)

    def body(c_ref, g_ref, r_ref, o_ref):
        o_ref[...] = (g_ref[...] + r_ref[...]).astype(out_dtype)

    return pl.pallas_call(
        body, name=name,
        grid_spec=pltpu.PrefetchScalarGridSpec(
            num_scalar_prefetch=1, grid=(g.shape[0], R // tr),
            in_specs=[pl.BlockSpec((None, None, tr, C), lambda j, i, c: (j, c[0], i, 0)),
                      pl.BlockSpec((None, tr, C), lambda j, i, c: (j, i, 0))],
            out_specs=pl.BlockSpec((None, tr, C), lambda j, i, c: (j, i, 0))),
        out_shape=jax.ShapeDtypeStruct(r.shape, out_dtype),
        compiler_params=_cp("parallel", "parallel"))(c_arr, g, r)


def _add2(a, b, name):
    R, C = a.shape
    tr = _row_tile(R)

    def body(a_ref, b_ref, o_ref):
        o_ref[...] = a_ref[...] + b_ref[...]

    spec = pl.BlockSpec((tr, C), lambda i: (i, 0))
    return pl.pallas_call(body, name=name, grid=(R // tr,), in_specs=[spec, spec], out_specs=spec,
                          out_shape=jax.ShapeDtypeStruct(a.shape, F32), compiler_params=_cp("parallel"))(a, b)


def _sum4(p, name):
    _, R, C = p.shape
    tr = _row_tile(R)

    def body(p_ref, o_ref):
        q = [p_ref[j].astype(F32) for j in range(4)]
        o_ref[...] = ((q[0] + q[1]) + q[2]) + q[3]

    return pl.pallas_call(
        body, name=name, grid=(R // tr,),
        in_specs=[pl.BlockSpec((4, tr, C), lambda i: (0, i, 0))],
        out_specs=pl.BlockSpec((tr, C), lambda i: (i, 0)),
        out_shape=jax.ShapeDtypeStruct((R, C), F32), compiler_params=_cp("parallel"))(p)


def _adamw_refs(w_ref, g_ref, m_ref, v_ref, d_ref, nm_ref, nv_ref):
    gv = g_ref[...]
    nm = ADAM_B1 * m_ref[...] + (1.0 - ADAM_B1) * gv
    nv = ADAM_B2 * v_ref[...] + (1.0 - ADAM_B2) * (gv * gv)
    m_hat = nm / (1.0 - ADAM_B1 ** ADAM_STEP)
    v_hat = nv / (1.0 - ADAM_B2 ** ADAM_STEP)
    d_ref[...] = -ADAM_LR * (m_hat / (jnp.sqrt(v_hat) + ADAM_EPS) + ADAM_WD * w_ref[...])
    nm_ref[...] = nm
    nv_ref[...] = nv


def _adamw_many(ws, gs, ms, vs, name):
    n = len(ws)

    def body(*refs):
        for k in range(n):
            _adamw_refs(*(refs[j * n + k] for j in range(7)))

    out_shape = [jax.ShapeDtypeStruct(w.shape, F32) for w in ws] * 3
    res = pl.pallas_call(body, name=name, out_shape=out_shape,
                         compiler_params=pltpu.CompilerParams(vmem_limit_bytes=VMEM_LIMIT))(*ws, *gs, *ms, *vs)
    return res[:n], res[n:2 * n], res[2 * n:]


def _adamw(w, g, m, v, name):
    R, C = w.shape
    tr = _row_tile(R)
    body = lambda *refs: _adamw_refs(*refs)

    spec = pl.BlockSpec((tr, C), lambda i: (i, 0))
    sh = jax.ShapeDtypeStruct((R, C), F32)
    return pl.pallas_call(body, name=name, grid=(R // tr,), in_specs=[spec] * 4, out_specs=[spec] * 3,
                          out_shape=[sh] * 3, compiler_params=_cp("parallel"))(w, g, m, v)


_ANY = pl.BlockSpec(memory_space=pl.ANY)


def _position():
    return lax.axis_index("x"), lax.axis_index("y"), lax.axis_index("c")


class _Comm:
    def __init__(self, arrs, out_shape, sems, start, finish):
        self.arrs, self.out_shape, self.sems, self.start, self.finish = arrs, out_shape, sems, start, finish


def _comm_call(comm, name):
    n, m = len(comm.arrs), len(comm.out_shape)

    def body(*refs):
        ins, outs, sems = refs[:n], refs[n:n + m], refs[n + m:]
        comm.start(ins, outs, sems)
        comm.finish(ins, outs, sems)

    return pl.pallas_call(
        body, name=name, in_specs=[_ANY] * n, out_specs=[_ANY] * m, out_shape=comm.out_shape,
        scratch_shapes=comm.sems, compiler_params=pltpu.CompilerParams(has_side_effects=True))(*comm.arrs)


def _hosted_call(body, comm, *, name, grid, in_specs, out_specs, out_shape, scratch_shapes, args):
    sem = ("arbitrary",) * len(grid)
    if comm is None:
        return pl.pallas_call(body, name=name, grid=grid, in_specs=in_specs, out_specs=out_specs, out_shape=out_shape,
                              scratch_shapes=scratch_shapes, compiler_params=_cp(*sem))(*args), []
    n_in, n_out, n_scr = len(in_specs), len(out_specs), len(scratch_shapes)
    ci, co = len(comm.arrs), len(comm.out_shape)

    def full(*refs):
        ins, refs = refs[:n_in], refs[n_in:]
        cins, refs = refs[:ci], refs[ci:]
        outs, refs = refs[:n_out], refs[n_out:]
        couts, refs = refs[:co], refs[co:]
        scr, csems = refs[:n_scr], refs[n_scr:]
        first, last = True, True
        for d, size in enumerate(grid):
            first = first & (pl.program_id(d) == 0)
            last = last & (pl.program_id(d) == size - 1)

        @pl.when(first)
        def _():
            comm.start(cins, couts, csems)

        body(*ins, *outs, *scr)

        @pl.when(last)
        def _():
            comm.finish(cins, couts, csems)

    res = pl.pallas_call(
        full, name=name, grid=grid, in_specs=list(in_specs) + [_ANY] * ci, out_specs=list(out_specs) + [_ANY] * co,
        out_shape=list(out_shape) + list(comm.out_shape), scratch_shapes=list(scratch_shapes) + list(comm.sems),
        compiler_params=_cp(*sem))(*args, *comm.arrs)
    return res[:n_out], res[n_out:]


def _comm_join(*comms):
    def parts(xs, attr):
        out, at = [], 0
        for cm in comms:
            n = len(getattr(cm, attr))
            out.append(xs[at:at + n])
            at += n
        return out

    def start(ins, outs, sems):
        for cm, i, o, s in zip(comms, parts(ins, "arrs"), parts(outs, "out_shape"), parts(sems, "sems")):
            cm.start(i, o, s)

    def finish(ins, outs, sems):
        for cm, i, o, s in zip(comms, parts(ins, "arrs"), parts(outs, "out_shape"), parts(sems, "sems")):
            cm.finish(i, o, s)

    cat = lambda attr: [x for cm in comms for x in getattr(cm, attr)]
    return _Comm(cat("arrs"), cat("out_shape"), cat("sems"), start, finish)


def _dma_sems(*counts):
    return [pltpu.SemaphoreType.DMA((n,)) for n in counts]


def _comm_pair_swap(arrs, half=False):
    n = len(arrs)
    out_shape = [jax.ShapeDtypeStruct(a.shape[:1] + a.shape[2:] if half else a.shape, a.dtype) for a in arrs]

    def copies(ins, outs, sems):
        x, y, c = _position()
        return [pltpu.make_async_remote_copy(
            src_ref=ins[k].at[:, 1 - c] if half else ins[k], dst_ref=outs[k], send_sem=sems[0].at[k],
            recv_sem=sems[1].at[k], device_id=(x, y, 1 - c), device_id_type=MESH) for k in range(n)]

    def start(ins, outs, sems):
        for cp in copies(ins, outs, sems):
            cp.start()

    def finish(ins, outs, sems):
        for cp in copies(ins, outs, sems):
            cp.wait()

    return _Comm(arrs, out_shape, _dma_sems(n, n), start, finish)


def _chip_of(j, c):
    return (jnp.right_shift(j, 1), jnp.bitwise_and(j, 1), c)


def _comm_chip_exchange(arrs, scatter):
    n = len(arrs)
    out_shape = [jax.ShapeDtypeStruct(a.shape if scatter else (4,) + a.shape, a.dtype) for a in arrs]

    def copies(ins, outs, sems):
        x, y, c = _position()
        me = 2 * x + y
        local, sent, landed = [], [], []
        for k in range(n):
            local.append(pltpu.make_async_copy(ins[k].at[me] if scatter else ins[k], outs[k].at[me], sems[2].at[k]))
            for d in (1, 2, 3):
                j = jnp.bitwise_xor(me, d)
                s = 3 * k + d - 1
                src = ins[k].at[j] if scatter else ins[k]
                for dst, group in ((outs[k].at[me], sent), (outs[k].at[j], landed)):
                    group.append(pltpu.make_async_remote_copy(
                        src_ref=src, dst_ref=dst, send_sem=sems[0].at[s], recv_sem=sems[1].at[s],
                        device_id=_chip_of(j, c), device_id_type=MESH))
        return local, sent, landed

    def start(ins, outs, sems):
        local, sent, _ = copies(ins, outs, sems)
        for cp in local + sent:
            cp.start()

    def finish(ins, outs, sems):
        local, sent, landed = copies(ins, outs, sems)
        for cp in sent:
            cp.wait_send()
        for cp in landed:
            cp.wait_recv()
        for cp in local:
            cp.wait()

    return _Comm(arrs, out_shape, _dma_sems(3 * n, 3 * n, n), start, finish)


def _comm_pair_gather(arrs):
    n = len(arrs)
    out_shape = [jax.ShapeDtypeStruct((2,) + a.shape, a.dtype) for a in arrs]

    def copies(ins, outs, sems):
        x, y, c = _position()
        local, sent, landed = [], [], []
        for k in range(n):
            local.append(pltpu.make_async_copy(ins[k], outs[k].at[c], sems[2].at[k]))
            for dst, group in ((outs[k].at[c], sent), (outs[k].at[1 - c], landed)):
                group.append(pltpu.make_async_remote_copy(
                    src_ref=ins[k], dst_ref=dst, send_sem=sems[0].at[k], recv_sem=sems[1].at[k],
                    device_id=(x, y, 1 - c), device_id_type=MESH))
        return local, sent, landed

    def start(ins, outs, sems):
        local, sent, _ = copies(ins, outs, sems)
        for cp in local + sent:
            cp.start()

    def finish(ins, outs, sems):
        local, sent, landed = copies(ins, outs, sems)
        for cp in sent:
            cp.wait_send()
        for cp in landed:
            cp.wait_recv()
        for cp in local:
            cp.wait()

    return _Comm(arrs, out_shape, _dma_sems(n, n, n), start, finish)


def _comm_gather_split(shards, whole):
    n, nw = len(shards), len(whole)
    arrs = list(shards) + list(whole)
    out_shape = [jax.ShapeDtypeStruct((4,) + a.shape, a.dtype) for a in arrs]

    def copies(ins, outs, sems):
        x, y, c = _position()
        me = 2 * x + y
        local, sent, landed, passed, passed_in = [], [], [], [], []
        for k in range(n + nw):
            local.append(pltpu.make_async_copy(ins[k], outs[k].at[me], sems[4].at[k]))
            for d in (1, 2, 3):
                j = jnp.bitwise_xor(me, d)
                s = 3 * k + d - 1
                if k >= n:
                    src, mine, theirs = ins[k], outs[k].at[me], outs[k].at[j]
                else:
                    h = shards[k].shape[0] // 2
                    rows = pl.ds(pl.multiple_of(c * h, 16), h)
                    other = pl.ds(pl.multiple_of((1 - c) * h, 16), h)
                    src, mine, theirs = ins[k].at[rows], outs[k].at[me, rows], outs[k].at[j, rows]
                    for dst, group in ((theirs, passed), (outs[k].at[j, other], passed_in)):
                        group.append(pltpu.make_async_remote_copy(
                            src_ref=theirs, dst_ref=dst, send_sem=sems[2].at[s], recv_sem=sems[3].at[s],
                            device_id=(x, y, 1 - c), device_id_type=MESH))
                for dst, group in ((mine, sent), (theirs, landed)):
                    group.append(pltpu.make_async_remote_copy(
                        src_ref=src, dst_ref=dst, send_sem=sems[0].at[s], recv_sem=sems[1].at[s],
                        device_id=_chip_of(j, c), device_id_type=MESH))
        return local, sent, landed, passed, passed_in

    def start(ins, outs, sems):
        local, sent, _, _, _ = copies(ins, outs, sems)
        for cp in local + sent:
            cp.start()

    def finish(ins, outs, sems):
        local, sent, landed, passed, passed_in = copies(ins, outs, sems)
        for cp in landed[:3 * n]:
            cp.wait_recv()
        for cp in passed:
            cp.start()
        for cp in landed[3 * n:]:
            cp.wait_recv()
        for cp in sent:
            cp.wait_send()
        for cp in passed:
            cp.wait_send()
        for cp in passed_in:
            cp.wait_recv()
        for cp in local:
            cp.wait()

    t = 3 * (n + nw)
    return _Comm(arrs, out_shape, _dma_sems(t, t, max(3 * n, 1), max(3 * n, 1), n + nw), start, finish)


def _pack(arrs, row_multiple):
    parts = []
    for a in arrs:
        flat = a.reshape(-1).astype(F32)
        pad = (-flat.shape[0]) % LANES
        parts.append(jnp.pad(flat, (0, pad)) if pad else flat)
    flat = jnp.concatenate(parts)
    rows = -(-flat.shape[0] // LANES)
    rows_p = -(-rows // row_multiple) * row_multiple
    return jnp.pad(flat, (0, rows_p * LANES - flat.shape[0])).reshape(rows_p, LANES)


def _unpack(packed, shapes):
    flat = packed.reshape(-1)
    outs, off = [], 0
    for sh in shapes:
        size = int(np.prod(sh))
        outs.append(flat[off:off + size].reshape(sh))
        off += size + (-size) % LANES
    return outs


SMALL = ["norm_mix_g", "pool_w", "pool_scale", "ssm_log_neg_a_re", "ssm_a_im", "ssm_log_dt", "ssm_b_re", "ssm_b_im",
         "ssm_c_re", "ssm_c_im", "ssm_d", "glu_b", "out_norm_pool_g", "out_norm_ssm_g", "norm_ffn_g", "conv_b",
         "final_norm_g"]
BIG = ["w_in", "glu_w", "w_out", "w_up", "w_down"]
WEIGHTS = ['norm_mix_g', 'w_in', 'pool_w', 'pool_scale', 'ssm_log_neg_a_re', 'ssm_a_im', 'ssm_log_dt', 'ssm_b_re',
           'ssm_b_im', 'ssm_c_re', 'ssm_c_im', 'ssm_d', 'glu_w', 'glu_b', 'out_norm_pool_g', 'out_norm_ssm_g', 'w_out',
           'norm_ffn_g', 'w_up', 'conv_w', 'conv_b', 'w_down', 'final_norm_g']


def _local_step(x, target, p, full, shards=None, c_arr=None):
    L, D = x.shape
    dist = shards is not None
    row = lambda a: a.reshape(1, -1)
    w_in = full["w_in"]
    pool_w_b = p["pool_w"].astype(BF16)
    g_mix, g_pool, g_ssm, g_ffn, g_fin = (row(p[k]) for k in (
        "norm_mix_g", "out_norm_pool_g", "out_norm_ssm_g", "norm_ffn_g", "final_norm_g"))
    pool_scale, ssm_d, glu_b, conv_b = (row(p[k]) for k in ("pool_scale", "ssm_d", "glu_b", "conv_b"))

    lnar = p["ssm_log_neg_a_re"].reshape(2 * N_SSM_GROUPS, SSM_STATE)
    aim = p["ssm_a_im"].reshape(2 * N_SSM_GROUPS, SSM_STATE)
    ldt = jnp.broadcast_to(p["ssm_log_dt"].reshape(2 * N_SSM_GROUPS, 1), lnar.shape)
    lam_re, lam_im, f_re, f_im = _ssm_params(lnar, aim, ldt)
    flat2 = lambda a: a.reshape(2, N_STATE)
    lam4 = jnp.stack([flat2(lam_re)[0], flat2(lam_im)[0], flat2(lam_re)[1], flat2(lam_im)[1]])
    tables = _scan_tables(lam4)
    f2 = [jnp.stack([flat2(f_re)[d], flat2(f_im)[d]]) for d in range(2)]
    dense = _ssm_expand(p["ssm_b_re"], p["ssm_b_im"], p["ssm_c_re"], p["ssm_c_im"])
    ssm_args = [tuple(dense[4 * d:4 * d + 4]) + (f2[d], tables) for d in range(2)]

    u, xn = _in_proj(x, g_mix, w_in)
    yn_pool = _pool_fwd(u, pool_w_b, pool_scale, g_pool)
    gather1 = _comm_gather_split([shards[k] for k in ("glu_w", "w_out", "w_down")], [shards["conv_w"]]) if dist else None
    (y0, s0r, s0i), got1 = _ssm_scan_fwd(u, *ssm_args[0], 0, False, comm=gather1)
    gather2 = _comm_gather_split([shards["w_up"]], []) if dist else None
    (y1, s1r, s1i), got2 = _ssm_scan_fwd(u, *ssm_args[1], 2, True, comm=gather2)
    if dist:
        glu_w, w_out, w_down = (g.reshape((-1,) + g.shape[2:]) for g in got1[:3])
        conv_w = jnp.transpose(got1[3], (1, 0, 2)).reshape(3, -1)
        w_up4 = got2[0]
    else:
        glu_w, w_out, w_up4, w_down, conv_w = (full[k] for k in ("glu_w", "w_out", "w_up", "w_down", "conv_w"))
    h1, hn, ycat = _mix_out(yn_pool, y0, y1, u, x, ssm_d, glu_w, glu_b, g_ssm, w_out, g_ffn)
    up = _ffn_up(hn, w_up4)
    a, dh2, dh2_b, loss, g_final = _ffn_down_loss(up, conv_w, conv_b, w_down, h1, target, g_fin)

    d_val, d_gate, gcv, gcg = _ffn_act_bwd(up, conv_w, conv_b, w_down, dh2_b)
    g_w_down = _matmul_tn(a, dh2_b, FF_BLK, D, "grad_w_down")
    d_up, dh1, dh1_b, g_ffn_g = _ffn_up_bwd(d_val, d_gate, conv_w, w_up4, h1, dh2, g_ffn)
    g_w_up = _matmul_tn_blocks(hn, d_up.reshape(4, L, FF_BLK), 512, "grad_w_up")
    g_w_out = _matmul_tn(ycat, dh1_b, 512, D, "grad_w_out")
    late = ("w_up", "w_down")
    halves = [g_w_up.reshape(4, 2, D // 2, FF_BLK), g_w_down.reshape(4, 2, D_FF // 8, D)]
    (d_pooled, g_pool_w, g_pool_scale, g_pool_g), from_sibling = _pool_bwd_local(
        dh1_b, u, w_out, pool_w_b, pool_scale, g_pool, comm=_comm_pair_swap(halves, half=True) if dist else None)
    du_pool = _pool_bwd_window(d_pooled)
    dy, du_direct, g_glu_w, g_glu_b, g_ssm_d, g_ssm_g = _ssm_bwd_local(dh1_b, y0, y1, u, ssm_d, glu_w, glu_b, g_ssm, w_out)
    reduce2 = None
    if dist:
        chip_sums = [_add_half(h, r, c_arr, "sum_pair_" + k, BF16) for k, h, r in zip(late, halves, from_sibling)]
        reduce2 = _comm_chip_exchange(chip_sums, scatter=True)
    (du0, gb0r, gb0i, gc0r, gc0i, gv0), from_chips = _ssm_scan_bwd(dy, u, s0r, s0i, *ssm_args[0], 1, True, comm=reduce2)
    reduce3 = _comm_pair_gather([_sum4(r, "sum_chips_" + k) for k, r in zip(late, from_chips)]) if dist else None
    (du1, gb1r, gb1i, gc1r, gc1i, gv1), shards_out = _ssm_scan_bwd(dy, u, s1r, s1i, *ssm_args[1], 3, False, comm=reduce3)
    gvec = lambda j: jnp.stack([gv0[j], gv1[j]]).reshape(2 * N_SSM_GROUPS, SSM_STATE)
    g_lnar, g_aim, g_ldt = _ssm_params_bwd(lnar, aim, ldt, gvec(0), gvec(1), gvec(2), gvec(3))
    grad_x, d_u_b, g_mix_g = _in_bwd(du_pool, du_direct, du0, du1, dh1, x, g_mix, w_in)
    g_w_in = _matmul_tn(xn, d_u_b, 512, D, "grad_w_in")

    small = {
        "norm_mix_g": g_mix_g, "pool_w": g_pool_w, "pool_scale": g_pool_scale,
        "ssm_log_neg_a_re": g_lnar, "ssm_a_im": g_aim, "ssm_log_dt": g_ldt,
        "ssm_b_re": jnp.stack([gb0r, gb1r]), "ssm_b_im": jnp.stack([gb0i, gb1i]),
        "ssm_c_re": jnp.stack([gc0r, gc1r]), "ssm_c_im": jnp.stack([gc0i, gc1i]),
        "ssm_d": g_ssm_d, "glu_b": g_glu_b, "out_norm_pool_g": g_pool_g, "out_norm_ssm_g": g_ssm_g,
        "norm_ffn_g": g_ffn_g, "conv_b": jnp.concatenate([gcv[3], gcg[3]]), "final_norm_g": g_final,
        "conv_w": jnp.concatenate([gcv[0:3], gcg[0:3]], axis=1),
    }
    big = {"w_in": g_w_in, "glu_w": g_glu_w, "w_out": g_w_out}
    reduced = dict(zip(late, shards_out))
    if not dist:
        big.update({"w_up": g_w_up, "w_down": g_w_down})
    return loss, grad_x, small, big, reduced


def kernel(x, norm_mix_g, w_in, pool_w, pool_scale, ssm_log_neg_a_re, ssm_a_im, ssm_log_dt, ssm_b_re, ssm_b_im, ssm_c_re, ssm_c_im, ssm_d, glu_w, glu_b, out_norm_pool_g, out_norm_ssm_g, w_out, norm_ffn_g, w_up, conv_w, conv_b, w_down, final_norm_g, loss_target, m_norm_mix_g, m_w_in, m_pool_w, m_pool_scale, m_ssm_log_neg_a_re, m_ssm_a_im, m_ssm_log_dt, m_ssm_b_re, m_ssm_b_im, m_ssm_c_re, m_ssm_c_im, m_ssm_d, m_glu_w, m_glu_b, m_out_norm_pool_g, m_out_norm_ssm_g, m_w_out, m_norm_ffn_g, m_w_up, m_conv_w, m_conv_b, m_w_down, m_final_norm_g, v_norm_mix_g, v_w_in, v_pool_w, v_pool_scale, v_ssm_log_neg_a_re, v_ssm_a_im, v_ssm_log_dt, v_ssm_b_re, v_ssm_b_im, v_ssm_c_re, v_ssm_c_im, v_ssm_d, v_glu_w, v_glu_b, v_out_norm_pool_g, v_out_norm_ssm_g, v_w_out, v_norm_ffn_g, v_w_up, v_conv_w, v_conv_b, v_w_down, v_final_norm_g):
    args = locals()
    w = {k: args[k] for k in WEIGHTS}
    m = {k: args["m_" + k] for k in WEIGHTS}
    v = {k: args["v_" + k] for k in WEIGHTS}
    chip = 2 * lax.axis_index("x") + lax.axis_index("y")
    c_arr = lax.axis_index("c").astype(jnp.int32).reshape(1)

    shards = {k: w[k].astype(BF16) for k in BIG}
    shards["conv_w"] = conv_w
    w_in_full = _comm_call(_comm_chip_exchange([shards["w_in"]], scatter=False), "gather_w_in")[0]
    loss, grad_x, g_small, g_big, reduced = _local_step(
        x[0], loss_target[0], w, {"w_in": w_in_full.reshape(-1, w_in_full.shape[-1])}, shards, c_arr)

    tail = ("w_in", "glu_w", "w_out")
    packed = _pack([loss] + [g_small[k] for k in SMALL] + [g_small["conv_w"]], 1024)
    halves = [g_big[k].reshape(4, 2, g_big[k].shape[0] // 8, g_big[k].shape[1]) for k in tail]
    halves.append(packed.reshape(1, 2, packed.shape[0] // 2, LANES))
    from_sibling = _comm_call(_comm_pair_swap(halves, half=True), "reduce_pair")
    names = tail + ("small",)
    sums = [_add_half(h, r, c_arr, "sum_pair_" + k, F32 if k == "small" else BF16)
            for k, h, r in zip(names, halves, from_sibling)]
    from_chips = _comm_call(_comm_join(_comm_chip_exchange(sums[:3], scatter=True),
                                       _comm_chip_exchange([sums[3][0]], scatter=False)), "reduce_chips")
    mine = [_sum4(r, "sum_chips_" + k) for k, r in zip(names, from_chips)]
    joined = _comm_call(_comm_pair_gather(mine), "gather_halves")
    grads = {k: s.reshape(w[k].shape) for k, s in list(reduced.items()) + list(zip(tail, joined[:3]))}
    shapes = [loss.shape] + [w[k].shape for k in SMALL] + [(3, 4 * FF_BLK)]
    for k, g in zip(["loss"] + SMALL + ["conv_w_full"], _unpack(joined[3], shapes)):
        grads[k] = g
    loss = grads.pop("loss")[0, 0]
    grads["conv_w"] = lax.dynamic_slice_in_dim(grads.pop("conv_w_full"), chip * FF_BLK, FF_BLK, axis=1)

    delta, new_m, new_v = {}, {}, {}
    for k in BIG:
        delta[k], new_m[k], new_v[k] = _adamw(w[k], grads[k], m[k], v[k], "adamw_" + k)
    wide = ["ssm_b_re", "ssm_b_im"]
    for keys, name in ((wide, "adamw_ssm_b"), ([k for k in SMALL + ["conv_w"] if k not in wide], "adamw_small")):
        outs = _adamw_many(*([d[k] for k in keys] for d in (w, grads, m, v)), name)
        for d, o in zip((delta, new_m, new_v), outs):
            d.update(zip(keys, o))

    return (loss, grad_x[None], *[grads[k] for k in WEIGHTS], *[delta[k] for k in WEIGHTS],
            *[new_m[k] for k in WEIGHTS], *[new_v[k] for k in WEIGHTS])
```

```python
import numpy as np
import jax
import jax.numpy as jnp
from jax import lax
from jax.experimental import pallas as pl
from jax.experimental.pallas import tpu as pltpu

F32 = jnp.float32
BF16 = jnp.bfloat16
MESH = pl.DeviceIdType.MESH

EPS = 1e-6
POOL_WINDOWS = (2, 4, 8, 16)
POOL_GROUP = 128
SSM_GROUP = 16
SSM_STATE = 64
N_SSM_GROUPS = 32
N_STATE = N_SSM_GROUPS * SSM_STATE
QUAD = 256
N_QUAD = N_STATE // QUAD
SLAB = 256
D_SSM = 512
D_POOL = 512
D_FF = 2816
FF_BLK = 1408
HALO = 8
HALO_B = 16
LANES = 128
ADAM_LR, ADAM_B1, ADAM_B2, ADAM_EPS, ADAM_WD, ADAM_STEP = 0.001, 0.9, 0.999, 1e-08, 0.01, 10
VMEM_LIMIT = 56 * 2 ** 20

TL = 512
TF = 256
TC = 256
SCAN_W = 512


def _cp(*sem):
    return pltpu.CompilerParams(dimension_semantics=sem, vmem_limit_bytes=VMEM_LIMIT)


def _dot_nn(a, b):
    return jnp.dot(a, b, preferred_element_type=F32)


def _dot_nt(a, b):
    return lax.dot_general(a, b, (((1,), (1,)), ((), ())), preferred_element_type=F32)


def _dot_tn(a, b):
    return lax.dot_general(a, b, (((0,), (0,)), ((), ())), preferred_element_type=F32)


def _rms_fwd(x, g):
    inv = lax.rsqrt(jnp.mean(x * x, axis=-1, keepdims=True) + EPS)
    xh = x * inv
    return xh * g, xh, inv


def _rms_bwd(dy, xh, inv, g):
    dg = jnp.sum(dy * xh, axis=0, keepdims=True)
    dxh = dy * g
    dx = inv * (dxh - xh * jnp.mean(dxh * xh, axis=-1, keepdims=True))
    return dx, dg


_GELU_C = 0.7978845608028654
_GELU_A = 0.044715


def _gelu(y):
    t = jnp.tanh(_GELU_C * (y + _GELU_A * (y * y * y)))
    return 0.5 * y * (1.0 + t), t


def _gelu_grad(y, t):
    return 0.5 * (1.0 + t) + 0.5 * y * (1.0 - t * t) * (_GELU_C * (1.0 + 3.0 * _GELU_A * y * y))


def _sigmoid(x):
    return 1.0 / (1.0 + jnp.exp(-x))


def _full(shape):
    n = len(shape)
    return pl.BlockSpec(shape, lambda *_: (0,) * n)


def _fill_ext(ext_ref, prev_ref, cur_ref, next_ref, i, n, rows):
    ext_ref[0:HALO, :] = jnp.where(i > 0, prev_ref[...], 0.0).astype(ext_ref.dtype)
    ext_ref[HALO:HALO + rows, :] = cur_ref[...]
    ext_ref[HALO + rows:2 * HALO + rows, :] = jnp.where(i < n - 1, next_ref[...], 0.0).astype(ext_ref.dtype)


def _in_proj(x, g, w):
    L, D = x.shape
    E = w.shape[1]

    def body(x_ref, g_ref, w_ref, u_ref, xn_ref):
        y, _, _ = _rms_fwd(x_ref[...], g_ref[...])
        yb = y.astype(BF16)
        xn_ref[...] = yb
        u_ref[...] = _dot_nn(yb, w_ref[...])

    return pl.pallas_call(
        body, name="in_proj", grid=(L // TL,),
        in_specs=[pl.BlockSpec((TL, D), lambda i: (i, 0)), _full((1, D)), _full(w.shape)],
        out_specs=[pl.BlockSpec((TL, E), lambda i: (i, 0)), pl.BlockSpec((TL, D), lambda i: (i, 0))],
        out_shape=[jax.ShapeDtypeStruct((L, E), F32), jax.ShapeDtypeStruct((L, D), BF16)],
        compiler_params=_cp("parallel"))(x, g, w)


def _halo_specs_1d(rows, width, L, col):
    rb = rows // HALO
    last = L // HALO - 1
    return [pl.BlockSpec((HALO, width), lambda i: (jnp.maximum(i * rb - 1, 0), col)),
            pl.BlockSpec((rows, width), lambda i: (i, col)),
            pl.BlockSpec((HALO, width), lambda i: (jnp.minimum((i + 1) * rb, last), col))]


def _pooled_from_ext(ext_ref, t0, rows, L):
    t = t0 + lax.broadcasted_iota(jnp.int32, (rows, 1), 0)
    outs = []
    for gi, w in enumerate(POOL_WINDOWS):
        half = w // 2
        cs = slice(gi * POOL_GROUP, (gi + 1) * POOL_GROUP)
        acc = ext_ref[pl.ds(HALO - half, rows), cs]
        for s in range(-half + 1, half):
            acc = acc + ext_ref[pl.ds(HALO + s, rows), cs]
        cnt = (jnp.minimum(t + half, L) - jnp.maximum(t - half, 0)).astype(F32)
        outs.append(acc / cnt - ext_ref[pl.ds(HALO, rows), cs])
    return outs


def _pool_fwd(u, pool_w_b, pool_scale, g_pool):
    L = u.shape[0]
    n = L // TL

    def body(prev_ref, cur_ref, next_ref, pw_ref, ps_ref, g_ref, out_ref, ext_ref):
        i = pl.program_id(0)
        _fill_ext(ext_ref, prev_ref, cur_ref, next_ref, i, n, TL)
        pooled = _pooled_from_ext(ext_ref, i * TL, TL, L)
        ypre = jnp.concatenate([_dot_nn(pooled[gi].astype(BF16), pw_ref[gi]) for gi in range(4)], axis=-1)
        yn, _, _ = _rms_fwd(ypre * ps_ref[...], g_ref[...])
        out_ref[...] = yn.astype(BF16)

    return pl.pallas_call(
        body, name="pool_fwd", grid=(n,),
        in_specs=_halo_specs_1d(TL, D_POOL, L, 0) + [_full(pool_w_b.shape), _full((1, D_POOL)), _full((1, D_POOL))],
        out_specs=pl.BlockSpec((TL, D_POOL), lambda i: (i, 0)),
        out_shape=jax.ShapeDtypeStruct((L, D_POOL), BF16),
        scratch_shapes=[pltpu.VMEM((TL + 2 * HALO, D_POOL), F32)],
        compiler_params=_cp("parallel"))(u, u, u, pool_w_b, pool_scale, g_pool)


def _pool_bwd_local(dh1, u, w_out_b, pool_w_b, pool_scale, g_pool, comm=None):
    L = u.shape[0]
    n = L // TL
    D = dh1.shape[1]

    def body(dh_ref, prev_ref, cur_ref, next_ref, wo_ref, pw_ref, ps_ref, g_ref,
             dp_ref, gpw_ref, gps_ref, gg_ref, ext_ref):
        i = pl.program_id(0)

        @pl.when(i == 0)
        def _():
            gpw_ref[...] = jnp.zeros_like(gpw_ref)
            gps_ref[...] = jnp.zeros_like(gps_ref)
            gg_ref[...] = jnp.zeros_like(gg_ref)

        _fill_ext(ext_ref, prev_ref, cur_ref, next_ref, i, n, TL)
        pooled = [p.astype(BF16) for p in _pooled_from_ext(ext_ref, i * TL, TL, L)]
        ypre = jnp.concatenate([_dot_nn(pooled[gi], pw_ref[gi]) for gi in range(4)], axis=-1)
        ps = ps_ref[...]
        g = g_ref[...]
        _, xh, inv = _rms_fwd(ypre * ps, g)
        d_yn = _dot_nt(dh_ref[...].astype(BF16), wo_ref[...])
        d_y, dg = _rms_bwd(d_yn, xh, inv, g)
        gg_ref[...] += dg
        gps_ref[...] += jnp.sum(d_y * ypre, axis=0, keepdims=True)
        d_ypre = (d_y * ps).astype(BF16)
        for gi in range(4):
            cs = slice(gi * POOL_GROUP, (gi + 1) * POOL_GROUP)
            dp_ref[:, cs] = _dot_nt(d_ypre[:, cs], pw_ref[gi])
            gpw_ref[gi] += _dot_tn(pooled[gi], d_ypre[:, cs])

    return _hosted_call(
        body, comm, name="pool_bwd_local", grid=(n,),
        in_specs=[pl.BlockSpec((TL, D), lambda i: (i, 0))] + _halo_specs_1d(TL, D_POOL, L, 0)
        + [pl.BlockSpec((D_POOL, D), lambda i: (0, 0)), _full(pool_w_b.shape), _full((1, D_POOL)), _full((1, D_POOL))],
        out_specs=[pl.BlockSpec((TL, D_POOL), lambda i: (i, 0)), _full(pool_w_b.shape),
                   _full((1, D_POOL)), _full((1, D_POOL))],
        out_shape=[jax.ShapeDtypeStruct((L, D_POOL), F32), jax.ShapeDtypeStruct(pool_w_b.shape, F32),
                   jax.ShapeDtypeStruct((1, D_POOL), F32), jax.ShapeDtypeStruct((1, D_POOL), F32)],
        scratch_shapes=[pltpu.VMEM((TL + 2 * HALO, D_POOL), F32)],
        args=(dh1, u, u, u, w_out_b, pool_w_b, pool_scale, g_pool))


def _pool_bwd_window(d_pooled):
    L = d_pooled.shape[0]
    n = L // TL
    R = TL + 2 * HALO

    def body(prev_ref, cur_ref, next_ref, out_ref, ext_ref, q_ref):
        i = pl.program_id(0)
        _fill_ext(ext_ref, prev_ref, cur_ref, next_ref, i, n, TL)
        tr = i * TL - HALO + lax.broadcasted_iota(jnp.int32, (R, 1), 0)
        for gi, w in enumerate(POOL_WINDOWS):
            half = w // 2
            cs = slice(gi * POOL_GROUP, (gi + 1) * POOL_GROUP)
            cnt = jnp.maximum(jnp.minimum(tr + half, L) - jnp.maximum(tr - half, 0), 1).astype(F32)
            q_ref[:, cs] = ext_ref[:, cs] / cnt
        for gi, w in enumerate(POOL_WINDOWS):
            half = w // 2
            cs = slice(gi * POOL_GROUP, (gi + 1) * POOL_GROUP)
            acc = q_ref[pl.ds(HALO - half + 1, TL), cs]
            for s in range(-half + 2, half + 1):
                acc = acc + q_ref[pl.ds(HALO + s, TL), cs]
            out_ref[:, cs] = acc - ext_ref[pl.ds(HALO, TL), cs]

    return pl.pallas_call(
        body, name="pool_bwd_window", grid=(n,),
        in_specs=_halo_specs_1d(TL, D_POOL, L, 0),
        out_specs=pl.BlockSpec((TL, D_POOL), lambda i: (i, 0)),
        out_shape=jax.ShapeDtypeStruct((L, D_POOL), F32),
        scratch_shapes=[pltpu.VMEM((R, D_POOL), F32), pltpu.VMEM((R, D_POOL), F32)],
        compiler_params=_cp("parallel"))(d_pooled, d_pooled, d_pooled)


def _ssm_param_fn(lnar, aim, ldt):
    dt = jnp.exp(ldt)
    a_re = -jnp.exp(lnar)
    mag = jnp.exp(a_re * dt)
    ang = aim * dt
    lr, li = mag * jnp.cos(ang), mag * jnp.sin(ang)
    den = a_re * a_re + aim * aim
    fr = ((lr - 1.0) * a_re + li * aim) / den
    fi = (li * a_re - (lr - 1.0) * aim) / den
    return lr, li, fr, fi


def _ssm_params(lnar, aim, ldt):
    def body(a_ref, b_ref, c_ref, lr_ref, li_ref, fr_ref, fi_ref):
        lr, li, fr, fi = _ssm_param_fn(a_ref[...], b_ref[...], c_ref[...])
        lr_ref[...] = lr
        li_ref[...] = li
        fr_ref[...] = fr
        fi_ref[...] = fi

    sh = jax.ShapeDtypeStruct(lnar.shape, F32)
    return pl.pallas_call(body, name="ssm_params", out_shape=[sh] * 4)(lnar, aim, ldt)


def _ssm_params_bwd(lnar, aim, ldt, glr, gli, gfr, gfi):
    def body(a_ref, b_ref, c_ref, g0, g1, g2, g3, da_ref, db_ref, dc_ref):
        _, vjp = jax.vjp(_ssm_param_fn, a_ref[...], b_ref[...], c_ref[...])
        da, db, dc = vjp((g0[...], g1[...], g2[...], g3[...]))
        da_ref[...] = da
        db_ref[...] = db
        dc_ref[...] = jnp.sum(dc, axis=1, keepdims=True)

    return pl.pallas_call(
        body, name="ssm_params_bwd",
        out_shape=[jax.ShapeDtypeStruct(lnar.shape, F32), jax.ShapeDtypeStruct(aim.shape, F32),
                   jax.ShapeDtypeStruct((ldt.shape[0], 1), F32)])(lnar, aim, ldt, glr, gli, gfr, gfi)


def _scan_tables(lam4):
    def build(lr, li, reverse, out_ref, k):
        row = lax.broadcasted_iota(jnp.int32, (8, N_STATE), 0)
        lrb = jnp.broadcast_to(lr, (8, N_STATE))
        lib = jnp.broadcast_to(li, (8, N_STATE))
        pr, pi = lrb, lib
        for s, sh in enumerate((1, 2, 4)):
            mask = (row < 8 - sh) if reverse else (row >= sh)
            out_ref[k, 2 * s] = jnp.where(mask, pr, 0.0)
            out_ref[k, 2 * s + 1] = jnp.where(mask, pi, 0.0)
            pr, pi = pr * pr - pi * pi, 2.0 * pr * pi
        pr, pi = lrb, lib
        p8r = jnp.zeros((8, N_STATE), F32)
        p8i = jnp.zeros((8, N_STATE), F32)
        for j in range(8):
            r = 7 - j if reverse else j
            p8r = jnp.where(row == r, pr, p8r)
            p8i = jnp.where(row == r, pi, p8i)
            pr, pi = pr * lrb - pi * lib, pr * lib + pi * lrb
        out_ref[k, 6] = p8r
        out_ref[k, 7] = p8i

    def body(lam_ref, out_ref):
        l0r, l0i, l1r, l1i = (lam_ref[j:j + 1, :] for j in range(4))
        build(l0r, l0i, False, out_ref, 0)
        build(l0r, -l0i, True, out_ref, 1)
        build(l1r, l1i, True, out_ref, 2)
        build(l1r, -l1i, False, out_ref, 3)

    return pl.pallas_call(body, name="scan_tables",
                          out_shape=jax.ShapeDtypeStruct((4, 8, 8, N_STATE), F32))(lam4)


def _b_block(g):
    q, gl = divmod(g, 4)
    r0, c0 = gl * SSM_STATE, (q % 4) * 4 * SSM_GROUP + gl * SSM_GROUP
    return q, slice(r0, r0 + SSM_STATE), slice(c0, c0 + SSM_GROUP)


def _c_block(g):
    q, rows, cols = _b_block(g)
    return q, cols, rows


def _ssm_expand(b_re, b_im, c_re, c_im):
    def body(bre_ref, bim_ref, cre_ref, cim_ref, *rest):
        outs, tmp = rest[:8], rest[8]
        for d in range(2):
            for j, (src, where) in enumerate(((bre_ref, _b_block), (bim_ref, _b_block),
                                              (cre_ref, _c_block), (cim_ref, _c_block))):
                tmp[...] = jnp.zeros_like(tmp)
                for g in range(N_SSM_GROUPS):
                    q, rows, cols = where(g)
                    tmp[q, rows, cols] = src[d, g]
                outs[4 * d + j][...] = tmp[...].astype(BF16)

    dense = jax.ShapeDtypeStruct((N_QUAD, QUAD, SLAB), BF16)
    return pl.pallas_call(body, name="ssm_expand", out_shape=[dense] * 8,
                          scratch_shapes=[pltpu.VMEM((N_QUAD, QUAD, SLAB), F32)],
                          compiler_params=pltpu.CompilerParams(vmem_limit_bytes=VMEM_LIMIT))(b_re, b_im, c_re, c_im)


def _scan_rows(src_re, src_im, dst_re, dst_im, tab_ref, k, carry_re, carry_im, rows, reverse, s_refs=None):
    ng = rows // 8
    edge = 0 if reverse else 7
    row_id = lax.broadcasted_iota(jnp.int32, (8, SCAN_W), 0)
    sums = []
    for lt in range(N_STATE // SCAN_W):
        sl = slice(lt * SCAN_W, (lt + 1) * SCAN_W)

        def step(r, c, sl=sl):
            tabs = [tab_ref[k, j, :, sl] for j in range(8)]
            cr, ci = c[0], c[1]
            row = pl.multiple_of((ng - 1 - r) * 8 if reverse else r * 8, 8)
            xr = src_re[pl.ds(row, 8), sl]
            xi = src_im[pl.ds(row, 8), sl]
            for s, sh in enumerate((1, 2, 4)):
                amt = 8 - sh if reverse else sh
                rr = pltpu.roll(xr, amt, 0)
                ri = pltpu.roll(xi, amt, 0)
                mr, mi = tabs[2 * s], tabs[2 * s + 1]
                xr, xi = xr + mr * rr - mi * ri, xi + mr * ri + mi * rr
            xr, xi = xr + tabs[6] * cr - tabs[7] * ci, xi + tabs[6] * ci + tabs[7] * cr
            dst_re[pl.ds(row, 8), sl] = xr
            dst_im[pl.ds(row, 8), sl] = xi
            ncr = jnp.broadcast_to(xr[edge:edge + 1, :], (8, SCAN_W))
            nci = jnp.broadcast_to(xi[edge:edge + 1, :], (8, SCAN_W))
            if s_refs is None:
                return ncr, nci
            amt = 7 if reverse else 1
            far = 7 if reverse else 0
            nr = jnp.where(row_id == far, cr, pltpu.roll(xr, amt, 0))
            ni = jnp.where(row_id == far, ci, pltpu.roll(xi, amt, 0))
            sr = s_refs[0][pl.ds(row, 8), sl]
            si = s_refs[1][pl.ds(row, 8), sl]
            return ncr, nci, c[2] + nr * sr + ni * si, c[3] + ni * sr - nr * si

        init = (carry_re[:, sl], carry_im[:, sl])
        if s_refs is not None:
            init = init + (jnp.zeros((8, SCAN_W), F32), jnp.zeros((8, SCAN_W), F32))
        out = lax.fori_loop(0, ng, step, init)
        carry_re[:, sl] = out[0]
        carry_im[:, sl] = out[1]
        if s_refs is not None:
            sums.append((jnp.sum(out[2], axis=0, keepdims=True), jnp.sum(out[3], axis=0, keepdims=True)))
    return sums


def _ssm_scan_fwd(u, b_re, b_im, c_re, c_im, f2, tables, k, reverse, comm=None):
    L = u.shape[0]
    nc = L // TC
    chunk = (lambda i: nc - 1 - i) if reverse else (lambda i: i)

    def body(u_ref, bre_ref, bim_ref, cre_ref, cim_ref, f_ref, tab_ref,
             y_ref, sre_ref, sim_ref, in_re, in_im, carry_re, carry_im):
        @pl.when(pl.program_id(0) == 0)
        def _():
            carry_re[...] = jnp.zeros_like(carry_re)
            carry_im[...] = jnp.zeros_like(carry_im)

        ub = u_ref[...].astype(BF16)
        for q in range(N_QUAD):
            qs = slice(q * QUAD, (q + 1) * QUAD)
            us = ub[:, (q // 4) * SLAB:(q // 4 + 1) * SLAB]
            bur = _dot_nt(us, bre_ref[q])
            bui = _dot_nt(us, bim_ref[q])
            fr = f_ref[0:1, qs]
            fi = f_ref[1:2, qs]
            in_re[:, qs] = fr * bur - fi * bui
            in_im[:, qs] = fr * bui + fi * bur
        _scan_rows(in_re, in_im, sre_ref, sim_ref, tab_ref, k, carry_re, carry_im, TC, reverse)
        for j in range(D_SSM // SLAB):
            acc = jnp.zeros((TC, SLAB), F32)
            for q in range(4 * j, 4 * j + 4):
                qs = slice(q * QUAD, (q + 1) * QUAD)
                acc = acc + _dot_nt(sre_ref[:, qs].astype(BF16), cre_ref[q])
                acc = acc - _dot_nt(sim_ref[:, qs].astype(BF16), cim_ref[q])
            y_ref[:, j * SLAB:(j + 1) * SLAB] = acc

    return _hosted_call(
        body, comm, name="ssm_scan_rev" if reverse else "ssm_scan_fwd", grid=(nc,),
        in_specs=[pl.BlockSpec((TC, D_SSM), lambda i: (chunk(i), 1))]
        + [_full(b_re.shape)] * 4 + [_full(f2.shape), _full(tables.shape)],
        out_specs=[pl.BlockSpec((TC, D_SSM), lambda i: (chunk(i), 0)),
                   pl.BlockSpec((TC, N_STATE), lambda i: (chunk(i), 0)),
                   pl.BlockSpec((TC, N_STATE), lambda i: (chunk(i), 0))],
        out_shape=[jax.ShapeDtypeStruct((L, D_SSM), F32), jax.ShapeDtypeStruct((L, N_STATE), F32),
                   jax.ShapeDtypeStruct((L, N_STATE), F32)],
        scratch_shapes=[pltpu.VMEM((TC, N_STATE), F32), pltpu.VMEM((TC, N_STATE), F32),
                        pltpu.VMEM((8, N_STATE), F32), pltpu.VMEM((8, N_STATE), F32)],
        args=(u, b_re, b_im, c_re, c_im, f2, tables))


def _ssm_scan_bwd(dy, u, s_re, s_im, b_re, b_im, c_re, c_im, f2, tables, k, reverse, comm=None):
    L = u.shape[0]
    nc = L // TC
    chunk = (lambda i: nc - 1 - i) if reverse else (lambda i: i)

    def body(dy_ref, u_ref, sre_ref, sim_ref, bre_ref, bim_ref, cre_ref, cim_ref, f_ref, tab_ref,
             du_ref, ob_re, ob_im, oc_re, oc_im, gv_ref,
             a_re, a_im, carry_re, carry_im, gbr_ref, gbi_ref, gcr_ref, gci_ref):
        @pl.when(pl.program_id(0) == 0)
        def _():
            carry_re[...] = jnp.zeros_like(carry_re)
            carry_im[...] = jnp.zeros_like(carry_im)
            for r in (gbr_ref, gbi_ref, gcr_ref, gci_ref, gv_ref):
                r[...] = jnp.zeros_like(r)

        dyb = dy_ref[...].astype(BF16)
        ub = u_ref[...].astype(BF16)
        for q in range(N_QUAD):
            qs = slice(q * QUAD, (q + 1) * QUAD)
            ds = dyb[:, (q // 4) * SLAB:(q // 4 + 1) * SLAB]
            a_re[:, qs] = _dot_nn(ds, cre_ref[q])
            a_im[:, qs] = -_dot_nn(ds, cim_ref[q])
            gcr_ref[q] += _dot_tn(ds, sre_ref[:, qs].astype(BF16))
            gci_ref[q] -= _dot_tn(ds, sim_ref[:, qs].astype(BF16))
        sums = _scan_rows(a_re, a_im, a_re, a_im, tab_ref, k, carry_re, carry_im, TC, reverse,
                          s_refs=(sre_ref, sim_ref))
        for lt, (glr, gli) in enumerate(sums):
            sl = slice(lt * SCAN_W, (lt + 1) * SCAN_W)
            gv_ref[0:1, sl] += glr
            gv_ref[1:2, sl] += gli
        for j in range(D_SSM // SLAB):
            us = ub[:, j * SLAB:(j + 1) * SLAB]
            acc = jnp.zeros((TC, SLAB), F32)
            for q in range(4 * j, 4 * j + 4):
                qs = slice(q * QUAD, (q + 1) * QUAD)
                ar = a_re[:, qs]
                ai = a_im[:, qs]
                bur = _dot_nt(us, bre_ref[q])
                bui = _dot_nt(us, bim_ref[q])
                gv_ref[2:3, qs] += jnp.sum(ar * bur + ai * bui, axis=0, keepdims=True)
                gv_ref[3:4, qs] += jnp.sum(ai * bur - ar * bui, axis=0, keepdims=True)
                fr = f_ref[0:1, qs]
                fi = f_ref[1:2, qs]
                dbr = (fr * ar + fi * ai).astype(BF16)
                dbi = (fr * ai - fi * ar).astype(BF16)
                gbr_ref[q] += _dot_tn(dbr, us)
                gbi_ref[q] += _dot_tn(dbi, us)
                acc = acc + _dot_nn(dbr, bre_ref[q]) + _dot_nn(dbi, bim_ref[q])
            du_ref[:, j * SLAB:(j + 1) * SLAB] = acc

        @pl.when(pl.program_id(0) == nc - 1)
        def _():
            for g in range(N_SSM_GROUPS):
                q, rows, cols = _b_block(g)
                ob_re[g] = gbr_ref[q, rows, cols]
                ob_im[g] = gbi_ref[q, rows, cols]
                oc_re[g] = gcr_ref[q, cols, rows]
                oc_im[g] = gci_ref[q, cols, rows]

    gb = jax.ShapeDtypeStruct((N_SSM_GROUPS, SSM_STATE, SSM_GROUP), F32)
    gc = jax.ShapeDtypeStruct((N_SSM_GROUPS, SSM_GROUP, SSM_STATE), F32)
    dense = pltpu.VMEM((N_QUAD, QUAD, SLAB), F32)
    return _hosted_call(
        body, comm, name="ssm_bwd_rev" if reverse else "ssm_bwd_fwd", grid=(nc,),
        in_specs=[pl.BlockSpec((TC, D_SSM), lambda i: (chunk(i), 0)),
                  pl.BlockSpec((TC, D_SSM), lambda i: (chunk(i), 1)),
                  pl.BlockSpec((TC, N_STATE), lambda i: (chunk(i), 0)),
                  pl.BlockSpec((TC, N_STATE), lambda i: (chunk(i), 0))]
        + [_full(b_re.shape)] * 4 + [_full(f2.shape), _full(tables.shape)],
        out_specs=[pl.BlockSpec((TC, D_SSM), lambda i: (chunk(i), 0)), _full(gb.shape), _full(gb.shape),
                   _full(gc.shape), _full(gc.shape), _full((4, N_STATE))],
        out_shape=[jax.ShapeDtypeStruct((L, D_SSM), F32), gb, gb, gc, gc, jax.ShapeDtypeStruct((4, N_STATE), F32)],
        scratch_shapes=[pltpu.VMEM((TC, N_STATE), F32), pltpu.VMEM((TC, N_STATE), F32),
                        pltpu.VMEM((8, N_STATE), F32), pltpu.VMEM((8, N_STATE), F32), dense, dense, dense, dense],
        args=(dy, u, s_re, s_im, b_re, b_im, c_re, c_im, f2, tables))


def _ssm_post(yf, yb, u, d, glu_w, glu_b):
    y = yf + yb + d * u
    z, t = _gelu(y)
    zb = z.astype(BF16)
    gate = _sigmoid(_dot_nn(zb, glu_w) + glu_b)
    return y, z, t, zb, gate


def _mix_out(yn_pool, yf, yb, u, x, ssm_d, glu_w_b, glu_b, g_ssm, w_out_b, g_ffn):
    L, D = x.shape

    def body(ynp_ref, yf_ref, yb_ref, u_ref, x_ref, d_ref, gw_ref, gb_ref, gs_ref, wo_ref, gf_ref,
             h1_ref, hn_ref, ycat_ref):
        _, z, _, _, gate = _ssm_post(yf_ref[...], yb_ref[...], u_ref[...], d_ref[...], gw_ref[...], gb_ref[...])
        yns, _, _ = _rms_fwd(z * gate, gs_ref[...])
        ynsb = yns.astype(BF16)
        ynp = ynp_ref[...]
        ycat_ref[:, 0:D_POOL] = ynp
        ycat_ref[:, D_POOL:D] = ynsb
        h1 = x_ref[...] + _dot_nn(ynp, wo_ref[0:D_POOL, :]) + _dot_nn(ynsb, wo_ref[D_POOL:D, :])
        h1_ref[...] = h1
        hn, _, _ = _rms_fwd(h1, gf_ref[...])
        hn_ref[...] = hn.astype(BF16)

    half = lambda c: pl.BlockSpec((TL, D_SSM), lambda i: (i, c))
    row = pl.BlockSpec((TL, D), lambda i: (i, 0))
    return pl.pallas_call(
        body, name="mix_out", grid=(L // TL,),
        in_specs=[half(0), half(0), half(0), half(1), row, _full((1, D_SSM)), _full(glu_w_b.shape),
                  _full((1, D_SSM)), _full((1, D_SSM)), _full(w_out_b.shape), _full((1, D))],
        out_specs=[row, row, row],
        out_shape=[jax.ShapeDtypeStruct((L, D), F32), jax.ShapeDtypeStruct((L, D), BF16),
                   jax.ShapeDtypeStruct((L, D), BF16)],
        compiler_params=_cp("parallel"))(yn_pool, yf, yb, u, x, ssm_d, glu_w_b, glu_b, g_ssm, w_out_b, g_ffn)


def _ssm_bwd_local(dh1, yf, yb, u, ssm_d, glu_w_b, glu_b, g_ssm, w_out_b):
    L, D = dh1.shape

    def body(dh_ref, yf_ref, yb_ref, u_ref, d_ref, gw_ref, gb_ref, gs_ref, wo_ref,
             dy_ref, du_ref, ggw_ref, ggb_ref, gd_ref, ggs_ref):
        @pl.when(pl.program_id(0) == 0)
        def _():
            for r in (ggw_ref, ggb_ref, gd_ref, ggs_ref):
                r[...] = jnp.zeros_like(r)

        u = u_ref[...]
        d = d_ref[...]
        y, z, t, zb, gate = _ssm_post(yf_ref[...], yb_ref[...], u, d, gw_ref[...], gb_ref[...])
        gs = gs_ref[...]
        _, xh, inv = _rms_fwd(z * gate, gs)
        d_yn = _dot_nt(dh_ref[...].astype(BF16), wo_ref[...])
        d_o, dgs = _rms_bwd(d_yn, xh, inv, gs)
        ggs_ref[...] += dgs
        d_zg = d_o * z * gate * (1.0 - gate)
        d_zgb = d_zg.astype(BF16)
        ggb_ref[...] += jnp.sum(d_zg, axis=0, keepdims=True)
        ggw_ref[...] += _dot_tn(zb, d_zgb)
        d_z = d_o * gate + _dot_nt(d_zgb, gw_ref[...])
        d_y = d_z * _gelu_grad(y, t)
        gd_ref[...] += jnp.sum(d_y * u, axis=0, keepdims=True)
        dy_ref[...] = d_y
        du_ref[...] = d_y * d

    half = lambda c: pl.BlockSpec((TL, D_SSM), lambda i: (i, c))
    vec = _full((1, D_SSM))
    return pl.pallas_call(
        body, name="ssm_bwd_local", grid=(L // TL,),
        in_specs=[pl.BlockSpec((TL, D), lambda i: (i, 0)), half(0), half(0), half(1), vec, _full(glu_w_b.shape),
                  vec, vec, pl.BlockSpec((D_SSM, D), lambda i: (1, 0))],
        out_specs=[half(0), half(0), _full(glu_w_b.shape), vec, vec, vec],
        out_shape=[jax.ShapeDtypeStruct((L, D_SSM), F32), jax.ShapeDtypeStruct((L, D_SSM), F32),
                   jax.ShapeDtypeStruct(glu_w_b.shape, F32)] + [jax.ShapeDtypeStruct((1, D_SSM), F32)] * 3,
        compiler_params=_cp("arbitrary"))(dh1, yf, yb, u, ssm_d, glu_w_b, glu_b, g_ssm, w_out_b)


def _in_bwd(du_pool, du_a, du_b, du_c, dh1, x, g, w_in_b):
    L, D = x.shape

    def body(p_ref, a_ref, b_ref, c_ref, dh_ref, x_ref, g_ref, w_ref, dx_ref, dub_ref, gg_ref):
        @pl.when(pl.program_id(0) == 0)
        def _():
            gg_ref[...] = jnp.zeros_like(gg_ref)

        dub_ref[:, 0:D_POOL] = p_ref[...].astype(BF16)
        dub_ref[:, D_POOL:D] = (a_ref[...] + b_ref[...] + c_ref[...]).astype(BF16)
        d_xn = _dot_nt(dub_ref[...], w_ref[...])
        gv = g_ref[...]
        _, xh, inv = _rms_fwd(x_ref[...], gv)
        dx, dg = _rms_bwd(d_xn, xh, inv, gv)
        gg_ref[...] += dg
        dx_ref[...] = dh_ref[...] + dx

    half = pl.BlockSpec((TL, D_SSM), lambda i: (i, 0))
    row = pl.BlockSpec((TL, D), lambda i: (i, 0))
    return pl.pallas_call(
        body, name="in_bwd", grid=(L // TL,),
        in_specs=[half, half, half, half, row, row, _full((1, D)), _full(w_in_b.shape)],
        out_specs=[row, row, _full((1, D))],
        out_shape=[jax.ShapeDtypeStruct((L, D), F32), jax.ShapeDtypeStruct((L, D), BF16),
                   jax.ShapeDtypeStruct((1, D), F32)],
        compiler_params=_cp("arbitrary"))(du_pool, du_a, du_b, du_c, dh1, x, g, w_in_b)


def _ffn_up(hn, w_up4):
    L, D = hn.shape

    def body(h_ref, w_ref, o_ref):
        o_ref[...] = _dot_nn(h_ref[...], w_ref[...]).astype(BF16)

    return pl.pallas_call(
        body, name="ffn_up", grid=(4, L // TF),
        in_specs=[pl.BlockSpec((TF, D), lambda j, i: (i, 0)), pl.BlockSpec((None, D, FF_BLK), lambda j, i: (j, 0, 0))],
        out_specs=pl.BlockSpec((TF, FF_BLK), lambda j, i: (i, j)),
        out_shape=jax.ShapeDtypeStruct((L, 4 * FF_BLK), BF16),
        compiler_params=_cp("parallel", "parallel"))(hn, w_up4)


def _halo_specs_2d(rows, width, L, col, order):
    rb = rows // HALO_B
    last = L // HALO_B - 1
    if order == "ik":
        wrap = lambda f: (lambda i, k: f(i, k))
    else:
        wrap = lambda f: (lambda k, i: f(i, k))
    return [pl.BlockSpec((HALO_B, width), wrap(lambda i, k: (jnp.maximum(i * rb - 1, 0), col(k)))),
            pl.BlockSpec((rows, width), wrap(lambda i, k: (i, col(k)))),
            pl.BlockSpec((HALO_B, width), wrap(lambda i, k: (jnp.minimum((i + 1) * rb, last), col(k))))]


def _shift_mats(rows):
    r = lax.broadcasted_iota(jnp.int32, (rows, rows), 0)
    c = lax.broadcasted_iota(jnp.int32, (rows, rows), 1)
    return (c == r - 1).astype(BF16), (c == r + 1).astype(BF16)


def _neighbours(x, prev_ref, next_ref, cs, i, n, mats):
    rows = x.shape[0]
    row = lax.broadcasted_iota(jnp.int32, (rows, 1), 0)
    before = jnp.where(i > 0, prev_ref[:, cs].astype(F32)[HALO_B - 1:HALO_B, :], 0.0)
    after = jnp.where(i < n - 1, next_ref[:, cs].astype(F32)[0:1, :], 0.0)
    return (jnp.where(row == 0, before, _dot_nn(mats[0], x)),
            jnp.where(row == rows - 1, after, _dot_nn(mats[1], x)))


def _conv3(x, before, after, w, b):
    return before * w[0:1, :] + x.astype(F32) * w[1:2, :] + after * w[2:3, :] + b


def _col_chunks(width, size=256):
    return [slice(c, min(c + size, width)) for c in range(0, width, size)]


def _ffn_down_loss(up, conv_w, conv_b, w_down_b, h1, target, g_final):
    L, D = h1.shape
    n = L // TF
    nk = D_FF // FF_BLK

    def body(vp, vc, vn, gp, gc, gn, wv_ref, wg_ref, bv_ref, bg_ref, wd_ref, h1_ref, t_ref, gf_ref,
             a_ref, dh2_ref, dh2b_ref, loss_ref, gg_ref, acc_ref):
        i = pl.program_id(0)
        k = pl.program_id(1)

        @pl.when((i == 0) & (k == 0))
        def _():
            loss_ref[...] = jnp.zeros_like(loss_ref)
            gg_ref[...] = jnp.zeros_like(gg_ref)

        @pl.when(k == 0)
        def _():
            acc_ref[...] = jnp.zeros_like(acc_ref)

        mats = _shift_mats(TF)
        for cs in _col_chunks(FF_BLK):
            xv, xg = vc[:, cs], gc[:, cs]
            val = _conv3(xv, *_neighbours(xv, vp, vn, cs, i, n, mats), wv_ref[:, cs], bv_ref[:, cs])
            gate = _conv3(xg, *_neighbours(xg, gp, gn, cs, i, n, mats), wg_ref[:, cs], bg_ref[:, cs])
            a_ref[:, cs] = (val * (gate * _sigmoid(gate))).astype(BF16)
        acc_ref[...] += _dot_nn(a_ref[...], wd_ref[...])

        @pl.when(k == nk - 1)
        def _():
            gf = gf_ref[...]
            y, xh, inv = _rms_fwd(h1_ref[...] + acc_ref[...], gf)
            diff = y - t_ref[...]
            part = 0.5 * jnp.sum(jnp.mean(diff * diff, axis=-1, keepdims=True), axis=0, keepdims=True)
            loss_ref[...] += jnp.broadcast_to(part, loss_ref.shape)
            dx, dg = _rms_bwd(diff * (1.0 / D), xh, inv, gf)
            gg_ref[...] += dg
            dh2_ref[...] = dx
            dh2b_ref[...] = dx.astype(BF16)

    row = pl.BlockSpec((TF, D), lambda i, k: (i, 0))
    cw = lambda off: pl.BlockSpec((3, FF_BLK), lambda i, k: (0, k + off))
    cb = lambda off: pl.BlockSpec((1, FF_BLK), lambda i, k: (0, k + off))
    return pl.pallas_call(
        body, name="ffn_down_loss", grid=(n, nk),
        in_specs=_halo_specs_2d(TF, FF_BLK, L, lambda k: k, "ik") + _halo_specs_2d(TF, FF_BLK, L, lambda k: k + nk, "ik")
        + [cw(0), cw(nk), cb(0), cb(nk), pl.BlockSpec((FF_BLK, D), lambda i, k: (k, 0)), row, row, _full((1, D))],
        out_specs=[pl.BlockSpec((TF, FF_BLK), lambda i, k: (i, k)), row, row, _full((1, LANES)), _full((1, D))],
        out_shape=[jax.ShapeDtypeStruct((L, D_FF), BF16), jax.ShapeDtypeStruct((L, D), F32),
                   jax.ShapeDtypeStruct((L, D), BF16), jax.ShapeDtypeStruct((1, LANES), F32),
                   jax.ShapeDtypeStruct((1, D), F32)],
        scratch_shapes=[pltpu.VMEM((TF, D), F32)],
        compiler_params=_cp("arbitrary", "arbitrary"))(
            up, up, up, up, up, up, conv_w, conv_w, conv_b, conv_b, w_down_b, h1, target, g_final)


def _ffn_act_bwd(up, conv_w, conv_b, w_down_b, dh2):
    L, D = dh2.shape
    n = L // TF
    nk = D_FF // FF_BLK

    def body(vp, vc, vn, gp, gc, gn, wv_ref, wg_ref, bv_ref, bg_ref, wd_ref, dh_ref,
             dv_ref, dg_ref, gcv_ref, gcg_ref):
        i = pl.program_id(1)

        @pl.when(i == 0)
        def _():
            gcv_ref[...] = jnp.zeros_like(gcv_ref)
            gcg_ref[...] = jnp.zeros_like(gcg_ref)

        mats = _shift_mats(TF)
        dh = dh_ref[...]
        for cs in _col_chunks(FF_BLK):
            xv, xg = vc[:, cs], gc[:, cs]
            v_rows = _neighbours(xv, vp, vn, cs, i, n, mats)
            g_rows = _neighbours(xg, gp, gn, cs, i, n, mats)
            val = _conv3(xv, *v_rows, wv_ref[:, cs], bv_ref[:, cs])
            gate = _conv3(xg, *g_rows, wg_ref[:, cs], bg_ref[:, cs])
            d_a = _dot_nt(dh, wd_ref[cs, :])
            sg = _sigmoid(gate)
            d_val = d_a * (gate * sg)
            d_gate = d_a * val * (sg * (1.0 + gate * (1.0 - sg)))
            dv_ref[:, cs] = d_val.astype(BF16)
            dg_ref[:, cs] = d_gate.astype(BF16)
            for d, x, (before, after), gref in ((d_val, xv, v_rows, gcv_ref), (d_gate, xg, g_rows, gcg_ref)):
                for j, shifted in enumerate((before, x.astype(F32), after)):
                    gref[j:j + 1, cs] += jnp.sum(d * shifted, axis=0, keepdims=True)
                gref[3:4, cs] += jnp.sum(d, axis=0, keepdims=True)

    cw = lambda off: pl.BlockSpec((3, FF_BLK), lambda k, i: (0, k + off))
    cb = lambda off: pl.BlockSpec((1, FF_BLK), lambda k, i: (0, k + off))
    blk = pl.BlockSpec((TF, FF_BLK), lambda k, i: (i, k))
    acc = pl.BlockSpec((4, FF_BLK), lambda k, i: (0, k))
    return pl.pallas_call(
        body, name="ffn_act_bwd", grid=(nk, n),
        in_specs=_halo_specs_2d(TF, FF_BLK, L, lambda k: k, "ki") + _halo_specs_2d(TF, FF_BLK, L, lambda k: k + nk, "ki")
        + [cw(0), cw(nk), cb(0), cb(nk), pl.BlockSpec((FF_BLK, D), lambda k, i: (k, 0)),
           pl.BlockSpec((TF, D), lambda k, i: (i, 0))],
        out_specs=[blk, blk, acc, acc],
        out_shape=[jax.ShapeDtypeStruct((L, D_FF), BF16), jax.ShapeDtypeStruct((L, D_FF), BF16),
                   jax.ShapeDtypeStruct((4, D_FF), F32), jax.ShapeDtypeStruct((4, D_FF), F32)],
        compiler_params=_cp("arbitrary", "arbitrary"))(
            up, up, up, up, up, up, conv_w, conv_w, conv_b, conv_b, w_down_b, dh2)


def _ffn_up_bwd(d_val, d_gate, conv_w, w_up4, h1, dh2, g_ffn):
    L, D = h1.shape
    n = L // TF
    nk = D_FF // FF_BLK

    def body(vp, vc, vn, gp, gc, gn, wv_ref, wg_ref, uv_ref, ug_ref, h1_ref, dh2_ref, g_ref,
             dup_ref, dh1_ref, dh1b_ref, gg_ref, acc_ref):
        i = pl.program_id(0)
        k = pl.program_id(1)

        @pl.when((i == 0) & (k == 0))
        def _():
            gg_ref[...] = jnp.zeros_like(gg_ref)

        @pl.when(k == 0)
        def _():
            acc_ref[...] = jnp.zeros_like(acc_ref)

        mats = _shift_mats(TF)
        for j, (blocks, w_ref, wu_ref) in enumerate((((vp, vc, vn), wv_ref, uv_ref), ((gp, gc, gn), wg_ref, ug_ref))):
            for cs in _col_chunks(FF_BLK):
                d = blocks[1][:, cs]
                before, after = _neighbours(d, blocks[0], blocks[2], cs, i, n, mats)
                w = w_ref[:, cs]
                dup_ref[j, :, cs] = (after * w[0:1, :] + d.astype(F32) * w[1:2, :] + before * w[2:3, :]).astype(BF16)
            acc_ref[...] += _dot_nt(dup_ref[j], wu_ref[...])

        @pl.when(k == nk - 1)
        def _():
            g = g_ref[...]
            _, xh, inv = _rms_fwd(h1_ref[...], g)
            dx, dg = _rms_bwd(acc_ref[...], xh, inv, g)
            gg_ref[...] += dg
            dh1 = dh2_ref[...] + dx
            dh1_ref[...] = dh1
            dh1b_ref[...] = dh1.astype(BF16)

    row = pl.BlockSpec((TF, D), lambda i, k: (i, 0))
    cw = lambda off: pl.BlockSpec((3, FF_BLK), lambda i, k: (0, k + off))
    wu = lambda off: pl.BlockSpec((None, D, FF_BLK), lambda i, k: (k + off, 0, 0))
    return pl.pallas_call(
        body, name="ffn_up_bwd", grid=(n, nk),
        in_specs=_halo_specs_2d(TF, FF_BLK, L, lambda k: k, "ik") + _halo_specs_2d(TF, FF_BLK, L, lambda k: k, "ik")
        + [cw(0), cw(nk), wu(0), wu(nk), row, row, _full((1, D))],
        out_specs=[pl.BlockSpec((2, None, TF, FF_BLK), lambda i, k: (0, k, i, 0)), row, row, _full((1, D))],
        out_shape=[jax.ShapeDtypeStruct((2, nk, L, FF_BLK), BF16), jax.ShapeDtypeStruct((L, D), F32),
                   jax.ShapeDtypeStruct((L, D), BF16), jax.ShapeDtypeStruct((1, D), F32)],
        scratch_shapes=[pltpu.VMEM((TF, D), F32)],
        compiler_params=_cp("arbitrary", "arbitrary"))(
            d_val, d_val, d_val, d_gate, d_gate, d_gate, conv_w, conv_w, w_up4, w_up4, h1, dh2, g_ffn)


def _matmul_tn(a, b, tm, tn, name, tk=512):
    L, M = a.shape
    N = b.shape[1]

    def body(a_ref, b_ref, o_ref):
        @pl.when(pl.program_id(2) == 0)
        def _():
            o_ref[...] = jnp.zeros_like(o_ref)

        o_ref[...] += _dot_tn(a_ref[...], b_ref[...])

    return pl.pallas_call(
        body, name=name, grid=(M // tm, N // tn, L // tk),
        in_specs=[pl.BlockSpec((tk, tm), lambda m, n, l: (l, m)), pl.BlockSpec((tk, tn), lambda m, n, l: (l, n))],
        out_specs=pl.BlockSpec((tm, tn), lambda m, n, l: (m, n)),
        out_shape=jax.ShapeDtypeStruct((M, N), F32),
        compiler_params=_cp("parallel", "parallel", "arbitrary"))(a, b)


def _matmul_tn_blocks(a, b, tm, name, tk=512):
    L, M = a.shape
    J, _, N = b.shape

    def body(a_ref, b_ref, o_ref):
        @pl.when(pl.program_id(2) == 0)
        def _():
            o_ref[...] = jnp.zeros_like(o_ref)

        o_ref[...] += _dot_tn(a_ref[...], b_ref[...])

    return pl.pallas_call(
        body, name=name, grid=(M // tm, J, L // tk),
        in_specs=[pl.BlockSpec((tk, tm), lambda m, j, l: (l, m)), pl.BlockSpec((None, tk, N), lambda m, j, l: (j, l, 0))],
        out_specs=pl.BlockSpec((None, tm, N), lambda m, j, l: (j, m, 0)),
        out_shape=jax.ShapeDtypeStruct((J, M, N), F32),
        compiler_params=_cp("parallel", "parallel", "arbitrary"))(a, b)


def _row_tile(rows):
    for t in (512, 352, 256, 128, 64, 8):
        if rows % t == 0:
            return t
    return rows


def _add_half(g, r, c_arr, name, out_dtype=F32):
    _, _, R, C = g.shape
    tr = _row_tile(R)

    def body(c_ref, g_ref, r_ref, o_ref):
        o_ref[...] = (g_ref[...] + r_ref[...]).astype(out_dtype)

    return pl.pallas_call(
        body, name=name,
        grid_spec=pltpu.PrefetchScalarGridSpec(
            num_scalar_prefetch=1, grid=(g.shape[0], R // tr),
            in_specs=[pl.BlockSpec((None, None, tr, C), lambda j, i, c: (j, c[0], i, 0)),
                      pl.BlockSpec((None, tr, C), lambda j, i, c: (j, i, 0))],
            out_specs=pl.BlockSpec((None, tr, C), lambda j, i, c: (j, i, 0))),
        out_shape=jax.ShapeDtypeStruct(r.shape, out_dtype),
        compiler_params=_cp("parallel", "parallel"))(c_arr, g, r)


def _add2(a, b, name):
    R, C = a.shape
    tr = _row_tile(R)

    def body(a_ref, b_ref, o_ref):
        o_ref[...] = a_ref[...] + b_ref[...]

    spec = pl.BlockSpec((tr, C), lambda i: (i, 0))
    return pl.pallas_call(body, name=name, grid=(R // tr,), in_specs=[spec, spec], out_specs=spec,
                          out_shape=jax.ShapeDtypeStruct(a.shape, F32), compiler_params=_cp("parallel"))(a, b)


def _sum4(p, name):
    _, R, C = p.shape
    tr = _row_tile(R)

    def body(p_ref, o_ref):
        q = [p_ref[j].astype(F32) for j in range(4)]
        o_ref[...] = ((q[0] + q[1]) + q[2]) + q[3]

    return pl.pallas_call(
        body, name=name, grid=(R // tr,),
        in_specs=[pl.BlockSpec((4, tr, C), lambda i: (0, i, 0))],
        out_specs=pl.BlockSpec((tr, C), lambda i: (i, 0)),
        out_shape=jax.ShapeDtypeStruct((R, C), F32), compiler_params=_cp("parallel"))(p)


def _adamw_refs(w_ref, g_ref, m_ref, v_ref, d_ref, nm_ref, nv_ref):
    gv = g_ref[...]
    nm = ADAM_B1 * m_ref[...] + (1.0 - ADAM_B1) * gv
    nv = ADAM_B2 * v_ref[...] + (1.0 - ADAM_B2) * (gv * gv)
    m_hat = nm / (1.0 - ADAM_B1 ** ADAM_STEP)
    v_hat = nv / (1.0 - ADAM_B2 ** ADAM_STEP)
    d_ref[...] = -ADAM_LR * (m_hat / (jnp.sqrt(v_hat) + ADAM_EPS) + ADAM_WD * w_ref[...])
    nm_ref[...] = nm
    nv_ref[...] = nv


def _adamw_many(ws, gs, ms, vs, name):
    n = len(ws)

    def body(*refs):
        for k in range(n):
            _adamw_refs(*(refs[j * n + k] for j in range(7)))

    out_shape = [jax.ShapeDtypeStruct(w.shape, F32) for w in ws] * 3
    res = pl.pallas_call(body, name=name, out_shape=out_shape,
                         compiler_params=pltpu.CompilerParams(vmem_limit_bytes=VMEM_LIMIT))(*ws, *gs, *ms, *vs)
    return res[:n], res[n:2 * n], res[2 * n:]


def _adamw(w, g, m, v, name):
    R, C = w.shape
    tr = _row_tile(R)
    body = lambda *refs: _adamw_refs(*refs)

    spec = pl.BlockSpec((tr, C), lambda i: (i, 0))
    sh = jax.ShapeDtypeStruct((R, C), F32)
    return pl.pallas_call(body, name=name, grid=(R // tr,), in_specs=[spec] * 4, out_specs=[spec] * 3,
                          out_shape=[sh] * 3, compiler_params=_cp("parallel"))(w, g, m, v)


def _join_rows(own, other, c_arr, name):
    R, C = own.shape
    tr = _row_tile(R)

    def body(c_ref, own_ref, other_ref, o_ref):
        o_ref[...] = jnp.where(pl.program_id(0) == c_ref[0], own_ref[...], other_ref[...])

    half = pl.BlockSpec((tr, C), lambda h, i, c: (i, 0))
    return pl.pallas_call(
        body, name=name,
        grid_spec=pltpu.PrefetchScalarGridSpec(
            num_scalar_prefetch=1, grid=(2, R // tr), in_specs=[half, half],
            out_specs=pl.BlockSpec((tr, C), lambda h, i, c: (h * (R // tr) + i, 0))),
        out_shape=jax.ShapeDtypeStruct((2 * R, C), F32),
        compiler_params=_cp("parallel", "parallel"))(c_arr, own, other)


def _adamw_halves(w, own, other, m, v, c_arr, name):
    R, C = own.shape
    tr = _row_tile(R)

    def body(c_ref, w_ref, own_ref, other_ref, m_ref, v_ref, g_ref, d_ref, nm_ref, nv_ref):
        g_ref[...] = jnp.where(pl.program_id(0) == c_ref[0], own_ref[...], other_ref[...])
        _adamw_refs(w_ref, g_ref, m_ref, v_ref, d_ref, nm_ref, nv_ref)

    half = pl.BlockSpec((tr, C), lambda h, i, c: (i, 0))
    full = pl.BlockSpec((tr, C), lambda h, i, c: (h * (R // tr) + i, 0))
    sh = jax.ShapeDtypeStruct((2 * R, C), F32)
    return pl.pallas_call(
        body, name=name,
        grid_spec=pltpu.PrefetchScalarGridSpec(
            num_scalar_prefetch=1, grid=(2, R // tr), in_specs=[full, half, half, full, full], out_specs=[full] * 4),
        out_shape=[sh] * 4, compiler_params=_cp("parallel", "parallel"))(c_arr, w, own, other, m, v)


_ANY = pl.BlockSpec(memory_space=pl.ANY)


def _position():
    return lax.axis_index("x"), lax.axis_index("y"), lax.axis_index("c")


class _Comm:
    def __init__(self, arrs, out_shape, sems, start, finish):
        self.arrs, self.out_shape, self.sems, self.start, self.finish = arrs, out_shape, sems, start, finish


def _comm_call(comm, name):
    n, m = len(comm.arrs), len(comm.out_shape)

    def body(*refs):
        ins, outs, sems = refs[:n], refs[n:n + m], refs[n + m:]
        comm.start(ins, outs, sems)
        comm.finish(ins, outs, sems)

    return pl.pallas_call(
        body, name=name, in_specs=[_ANY] * n, out_specs=[_ANY] * m, out_shape=comm.out_shape,
        scratch_shapes=comm.sems, compiler_params=pltpu.CompilerParams(has_side_effects=True))(*comm.arrs)


def _hosted_call(body, comm, *, name, grid, in_specs, out_specs, out_shape, scratch_shapes, args):
    sem = ("arbitrary",) * len(grid)
    if comm is None:
        return pl.pallas_call(body, name=name, grid=grid, in_specs=in_specs, out_specs=out_specs, out_shape=out_shape,
                              scratch_shapes=scratch_shapes, compiler_params=_cp(*sem))(*args), []
    n_in, n_out, n_scr = len(in_specs), len(out_specs), len(scratch_shapes)
    ci, co = len(comm.arrs), len(comm.out_shape)

    def full(*refs):
        ins, refs = refs[:n_in], refs[n_in:]
        cins, refs = refs[:ci], refs[ci:]
        outs, refs = refs[:n_out], refs[n_out:]
        couts, refs = refs[:co], refs[co:]
        scr, csems = refs[:n_scr], refs[n_scr:]
        first, last = True, True
        for d, size in enumerate(grid):
            first = first & (pl.program_id(d) == 0)
            last = last & (pl.program_id(d) == size - 1)

        @pl.when(first)
        def _():
            comm.start(cins, couts, csems)

        body(*ins, *outs, *scr)

        @pl.when(last)
        def _():
            comm.finish(cins, couts, csems)

    res = pl.pallas_call(
        full, name=name, grid=grid, in_specs=list(in_specs) + [_ANY] * ci, out_specs=list(out_specs) + [_ANY] * co,
        out_shape=list(out_shape) + list(comm.out_shape), scratch_shapes=list(scratch_shapes) + list(comm.sems),
        compiler_params=_cp(*sem))(*args, *comm.arrs)
    return res[:n_out], res[n_out:]


def _comm_join(*comms):
    def parts(xs, attr):
        out, at = [], 0
        for cm in comms:
            n = len(getattr(cm, attr))
            out.append(xs[at:at + n])
            at += n
        return out

    def start(ins, outs, sems):
        for cm, i, o, s in zip(comms, parts(ins, "arrs"), parts(outs, "out_shape"), parts(sems, "sems")):
            cm.start(i, o, s)

    def finish(ins, outs, sems):
        for cm, i, o, s in zip(comms, parts(ins, "arrs"), parts(outs, "out_shape"), parts(sems, "sems")):
            cm.finish(i, o, s)

    cat = lambda attr: [x for cm in comms for x in getattr(cm, attr)]
    return _Comm(cat("arrs"), cat("out_shape"), cat("sems"), start, finish)


def _dma_sems(*counts):
    return [pltpu.SemaphoreType.DMA((n,)) for n in counts]


def _comm_pair_swap(arrs, half=False):
    n = len(arrs)
    out_shape = [jax.ShapeDtypeStruct(a.shape[:1] + a.shape[2:] if half else a.shape, a.dtype) for a in arrs]

    def copies(ins, outs, sems):
        x, y, c = _position()
        return [pltpu.make_async_remote_copy(
            src_ref=ins[k].at[:, 1 - c] if half else ins[k], dst_ref=outs[k], send_sem=sems[0].at[k],
            recv_sem=sems[1].at[k], device_id=(x, y, 1 - c), device_id_type=MESH) for k in range(n)]

    def start(ins, outs, sems):
        for cp in copies(ins, outs, sems):
            cp.start()

    def finish(ins, outs, sems):
        for cp in copies(ins, outs, sems):
            cp.wait()

    return _Comm(arrs, out_shape, _dma_sems(n, n), start, finish)


def _chip_of(j, c):
    return (jnp.right_shift(j, 1), jnp.bitwise_and(j, 1), c)


def _comm_chip_exchange(arrs, scatter):
    n = len(arrs)
    out_shape = [jax.ShapeDtypeStruct(a.shape if scatter else (4,) + a.shape, a.dtype) for a in arrs]

    def copies(ins, outs, sems):
        x, y, c = _position()
        me = 2 * x + y
        local, sent, landed = [], [], []
        for k in range(n):
            local.append(pltpu.make_async_copy(ins[k].at[me] if scatter else ins[k], outs[k].at[me], sems[2].at[k]))
            for d in (1, 2, 3):
                j = jnp.bitwise_xor(me, d)
                s = 3 * k + d - 1
                src = ins[k].at[j] if scatter else ins[k]
                for dst, group in ((outs[k].at[me], sent), (outs[k].at[j], landed)):
                    group.append(pltpu.make_async_remote_copy(
                        src_ref=src, dst_ref=dst, send_sem=sems[0].at[s], recv_sem=sems[1].at[s],
                        device_id=_chip_of(j, c), device_id_type=MESH))
        return local, sent, landed

    def start(ins, outs, sems):
        local, sent, _ = copies(ins, outs, sems)
        for cp in local + sent:
            cp.start()

    def finish(ins, outs, sems):
        local, sent, landed = copies(ins, outs, sems)
        for cp in sent:
            cp.wait_send()
        for cp in landed:
            cp.wait_recv()
        for cp in local:
            cp.wait()

    return _Comm(arrs, out_shape, _dma_sems(3 * n, 3 * n, n), start, finish)


def _comm_pair_gather(arrs):
    n = len(arrs)
    out_shape = [jax.ShapeDtypeStruct((2,) + a.shape, a.dtype) for a in arrs]

    def copies(ins, outs, sems):
        x, y, c = _position()
        local, sent, landed = [], [], []
        for k in range(n):
            local.append(pltpu.make_async_copy(ins[k], outs[k].at[c], sems[2].at[k]))
            for dst, group in ((outs[k].at[c], sent), (outs[k].at[1 - c], landed)):
                group.append(pltpu.make_async_remote_copy(
                    src_ref=ins[k], dst_ref=dst, send_sem=sems[0].at[k], recv_sem=sems[1].at[k],
                    device_id=(x, y, 1 - c), device_id_type=MESH))
        return local, sent, landed

    def start(ins, outs, sems):
        local, sent, _ = copies(ins, outs, sems)
        for cp in local + sent:
            cp.start()

    def finish(ins, outs, sems):
        local, sent, landed = copies(ins, outs, sems)
        for cp in sent:
            cp.wait_send()
        for cp in landed:
            cp.wait_recv()
        for cp in local:
            cp.wait()

    return _Comm(arrs, out_shape, _dma_sems(n, n, n), start, finish)


def _comm_gather_split(shards, whole):
    n, nw = len(shards), len(whole)
    arrs = list(shards) + list(whole)
    out_shape = [jax.ShapeDtypeStruct((4,) + a.shape, a.dtype) for a in arrs]

    def copies(ins, outs, sems):
        x, y, c = _position()
        me = 2 * x + y
        local, sent, landed, passed, passed_in = [], [], [], [], []
        for k in range(n + nw):
            local.append(pltpu.make_async_copy(ins[k], outs[k].at[me], sems[4].at[k]))
            for d in (1, 2, 3):
                j = jnp.bitwise_xor(me, d)
                s = 3 * k + d - 1
                if k >= n:
                    src, mine, theirs = ins[k], outs[k].at[me], outs[k].at[j]
                else:
                    h = shards[k].shape[0] // 2
                    rows = pl.ds(pl.multiple_of(c * h, 16), h)
                    other = pl.ds(pl.multiple_of((1 - c) * h, 16), h)
                    src, mine, theirs = ins[k].at[rows], outs[k].at[me, rows], outs[k].at[j, rows]
                    for dst, group in ((theirs, passed), (outs[k].at[j, other], passed_in)):
                        group.append(pltpu.make_async_remote_copy(
                            src_ref=theirs, dst_ref=dst, send_sem=sems[2].at[s], recv_sem=sems[3].at[s],
                            device_id=(x, y, 1 - c), device_id_type=MESH))
                for dst, group in ((mine, sent), (theirs, landed)):
                    group.append(pltpu.make_async_remote_copy(
                        src_ref=src, dst_ref=dst, send_sem=sems[0].at[s], recv_sem=sems[1].at[s],
                        device_id=_chip_of(j, c), device_id_type=MESH))
        return local, sent, landed, passed, passed_in

    def start(ins, outs, sems):
        local, sent, _, _, _ = copies(ins, outs, sems)
        for cp in local + sent:
            cp.start()

    def finish(ins, outs, sems):
        local, sent, landed, passed, passed_in = copies(ins, outs, sems)
        for cp in landed[:3 * n]:
            cp.wait_recv()
        for cp in passed:
            cp.start()
        for cp in landed[3 * n:]:
            cp.wait_recv()
        for cp in sent:
            cp.wait_send()
        for cp in passed:
            cp.wait_send()
        for cp in passed_in:
            cp.wait_recv()
        for cp in local:
            cp.wait()

    t = 3 * (n + nw)
    return _Comm(arrs, out_shape, _dma_sems(t, t, max(3 * n, 1), max(3 * n, 1), n + nw), start, finish)


def _pack(arrs, row_multiple):
    parts = []
    for a in arrs:
        flat = a.reshape(-1).astype(F32)
        pad = (-flat.shape[0]) % LANES
        parts.append(jnp.pad(flat, (0, pad)) if pad else flat)
    flat = jnp.concatenate(parts)
    rows = -(-flat.shape[0] // LANES)
    rows_p = -(-rows // row_multiple) * row_multiple
    return jnp.pad(flat, (0, rows_p * LANES - flat.shape[0])).reshape(rows_p, LANES)


def _unpack(packed, shapes):
    flat = packed.reshape(-1)
    outs, off = [], 0
    for sh in shapes:
        size = int(np.prod(sh))
        outs.append(flat[off:off + size].reshape(sh))
        off += size + (-size) % LANES
    return outs


SMALL = ["norm_mix_g", "pool_w", "pool_scale", "ssm_log_neg_a_re", "ssm_a_im", "ssm_log_dt", "ssm_b_re", "ssm_b_im",
         "ssm_c_re", "ssm_c_im", "ssm_d", "glu_b", "out_norm_pool_g", "out_norm_ssm_g", "norm_ffn_g", "conv_b",
         "final_norm_g"]
BIG = ["w_in", "glu_w", "w_out", "w_up", "w_down"]
WEIGHTS = ['norm_mix_g', 'w_in', 'pool_w', 'pool_scale', 'ssm_log_neg_a_re', 'ssm_a_im', 'ssm_log_dt', 'ssm_b_re',
           'ssm_b_im', 'ssm_c_re', 'ssm_c_im', 'ssm_d', 'glu_w', 'glu_b', 'out_norm_pool_g', 'out_norm_ssm_g', 'w_out',
           'norm_ffn_g', 'w_up', 'conv_w', 'conv_b', 'w_down', 'final_norm_g']


def _local_step(x, target, p, full, shards=None, c_arr=None):
    L, D = x.shape
    dist = shards is not None
    row = lambda a: a.reshape(1, -1)
    w_in = full["w_in"]
    pool_w_b = p["pool_w"].astype(BF16)
    g_mix, g_pool, g_ssm, g_ffn, g_fin = (row(p[k]) for k in (
        "norm_mix_g", "out_norm_pool_g", "out_norm_ssm_g", "norm_ffn_g", "final_norm_g"))
    pool_scale, ssm_d, glu_b, conv_b = (row(p[k]) for k in ("pool_scale", "ssm_d", "glu_b", "conv_b"))

    lnar = p["ssm_log_neg_a_re"].reshape(2 * N_SSM_GROUPS, SSM_STATE)
    aim = p["ssm_a_im"].reshape(2 * N_SSM_GROUPS, SSM_STATE)
    ldt = jnp.broadcast_to(p["ssm_log_dt"].reshape(2 * N_SSM_GROUPS, 1), lnar.shape)
    lam_re, lam_im, f_re, f_im = _ssm_params(lnar, aim, ldt)
    flat2 = lambda a: a.reshape(2, N_STATE)
    lam4 = jnp.stack([flat2(lam_re)[0], flat2(lam_im)[0], flat2(lam_re)[1], flat2(lam_im)[1]])
    tables = _scan_tables(lam4)
    f2 = [jnp.stack([flat2(f_re)[d], flat2(f_im)[d]]) for d in range(2)]
    dense = _ssm_expand(p["ssm_b_re"], p["ssm_b_im"], p["ssm_c_re"], p["ssm_c_im"])
    ssm_args = [tuple(dense[4 * d:4 * d + 4]) + (f2[d], tables) for d in range(2)]

    u, xn = _in_proj(x, g_mix, w_in)
    yn_pool = _pool_fwd(u, pool_w_b, pool_scale, g_pool)
    gather1 = _comm_gather_split([shards[k] for k in ("glu_w", "w_out", "w_down")], [shards["conv_w"]]) if dist else None
    (y0, s0r, s0i), got1 = _ssm_scan_fwd(u, *ssm_args[0], 0, False, comm=gather1)
    gather2 = _comm_gather_split([shards["w_up"]], []) if dist else None
    (y1, s1r, s1i), got2 = _ssm_scan_fwd(u, *ssm_args[1], 2, True, comm=gather2)
    if dist:
        glu_w, w_out, w_down = (g.reshape((-1,) + g.shape[2:]) for g in got1[:3])
        conv_w = jnp.transpose(got1[3], (1, 0, 2)).reshape(3, -1)
        w_up4 = got2[0]
    else:
        glu_w, w_out, w_up4, w_down, conv_w = (full[k] for k in ("glu_w", "w_out", "w_up", "w_down", "conv_w"))
    h1, hn, ycat = _mix_out(yn_pool, y0, y1, u, x, ssm_d, glu_w, glu_b, g_ssm, w_out, g_ffn)
    up = _ffn_up(hn, w_up4)
    a, dh2, dh2_b, loss, g_final = _ffn_down_loss(up, conv_w, conv_b, w_down, h1, target, g_fin)

    d_val, d_gate, gcv, gcg = _ffn_act_bwd(up, conv_w, conv_b, w_down, dh2_b)
    g_w_down = _matmul_tn(a, dh2_b, FF_BLK, D, "grad_w_down")
    d_up, dh1, dh1_b, g_ffn_g = _ffn_up_bwd(d_val, d_gate, conv_w, w_up4, h1, dh2, g_ffn)
    g_w_up = _matmul_tn_blocks(hn, d_up.reshape(4, L, FF_BLK), 512, "grad_w_up")
    g_w_out = _matmul_tn(ycat, dh1_b, 512, D, "grad_w_out")
    late = ("w_up", "w_down")
    halves = [g_w_up.reshape(4, 2, D // 2, FF_BLK), g_w_down.reshape(4, 2, D_FF // 8, D)]
    (d_pooled, g_pool_w, g_pool_scale, g_pool_g), from_sibling = _pool_bwd_local(
        dh1_b, u, w_out, pool_w_b, pool_scale, g_pool, comm=_comm_pair_swap(halves, half=True) if dist else None)
    du_pool = _pool_bwd_window(d_pooled)
    dy, du_direct, g_glu_w, g_glu_b, g_ssm_d, g_ssm_g = _ssm_bwd_local(dh1_b, y0, y1, u, ssm_d, glu_w, glu_b, g_ssm, w_out)
    reduce2 = None
    if dist:
        chip_sums = [_add_half(h, r, c_arr, "sum_pair_" + k, BF16) for k, h, r in zip(late, halves, from_sibling)]
        reduce2 = _comm_chip_exchange(chip_sums, scatter=True)
    (du0, gb0r, gb0i, gc0r, gc0i, gv0), from_chips = _ssm_scan_bwd(dy, u, s0r, s0i, *ssm_args[0], 1, True, comm=reduce2)
    reduce3 = _comm_pair_gather([_sum4(r, "sum_chips_" + k) for k, r in zip(late, from_chips)]) if dist else None
    (du1, gb1r, gb1i, gc1r, gc1i, gv1), shards_out = _ssm_scan_bwd(dy, u, s1r, s1i, *ssm_args[1], 3, False, comm=reduce3)
    gvec = lambda j: jnp.stack([gv0[j], gv1[j]]).reshape(2 * N_SSM_GROUPS, SSM_STATE)
    g_lnar, g_aim, g_ldt = _ssm_params_bwd(lnar, aim, ldt, gvec(0), gvec(1), gvec(2), gvec(3))
    grad_x, d_u_b, g_mix_g = _in_bwd(du_pool, du_direct, du0, du1, dh1, x, g_mix, w_in)
    g_w_in = _matmul_tn(xn, d_u_b, 512, D, "grad_w_in")

    small = {
        "norm_mix_g": g_mix_g, "pool_w": g_pool_w, "pool_scale": g_pool_scale,
        "ssm_log_neg_a_re": g_lnar, "ssm_a_im": g_aim, "ssm_log_dt": g_ldt,
        "ssm_b_re": jnp.stack([gb0r, gb1r]), "ssm_b_im": jnp.stack([gb0i, gb1i]),
        "ssm_c_re": jnp.stack([gc0r, gc1r]), "ssm_c_im": jnp.stack([gc0i, gc1i]),
        "ssm_d": g_ssm_d, "glu_b": g_glu_b, "out_norm_pool_g": g_pool_g, "out_norm_ssm_g": g_ssm_g,
        "norm_ffn_g": g_ffn_g, "conv_b": jnp.concatenate([gcv[3], gcg[3]]), "final_norm_g": g_final,
        "conv_w": jnp.concatenate([gcv[0:3], gcg[0:3]], axis=1),
    }
    big = {"w_in": g_w_in, "glu_w": g_glu_w, "w_out": g_w_out}
    reduced = dict(zip(late, shards_out))
    if not dist:
        big.update({"w_up": g_w_up, "w_down": g_w_down})
    return loss, grad_x, small, big, reduced


def kernel(x, norm_mix_g, w_in, pool_w, pool_scale, ssm_log_neg_a_re, ssm_a_im, ssm_log_dt, ssm_b_re, ssm_b_im, ssm_c_re, ssm_c_im, ssm_d, glu_w, glu_b, out_norm_pool_g, out_norm_ssm_g, w_out, norm_ffn_g, w_up, conv_w, conv_b, w_down, final_norm_g, loss_target, m_norm_mix_g, m_w_in, m_pool_w, m_pool_scale, m_ssm_log_neg_a_re, m_ssm_a_im, m_ssm_log_dt, m_ssm_b_re, m_ssm_b_im, m_ssm_c_re, m_ssm_c_im, m_ssm_d, m_glu_w, m_glu_b, m_out_norm_pool_g, m_out_norm_ssm_g, m_w_out, m_norm_ffn_g, m_w_up, m_conv_w, m_conv_b, m_w_down, m_final_norm_g, v_norm_mix_g, v_w_in, v_pool_w, v_pool_scale, v_ssm_log_neg_a_re, v_ssm_a_im, v_ssm_log_dt, v_ssm_b_re, v_ssm_b_im, v_ssm_c_re, v_ssm_c_im, v_ssm_d, v_glu_w, v_glu_b, v_out_norm_pool_g, v_out_norm_ssm_g, v_w_out, v_norm_ffn_g, v_w_up, v_conv_w, v_conv_b, v_w_down, v_final_norm_g):
    args = locals()
    w = {k: args[k] for k in WEIGHTS}
    m = {k: args["m_" + k] for k in WEIGHTS}
    v = {k: args["v_" + k] for k in WEIGHTS}
    chip = 2 * lax.axis_index("x") + lax.axis_index("y")
    c_arr = lax.axis_index("c").astype(jnp.int32).reshape(1)

    shards = {k: w[k].astype(BF16) for k in BIG}
    shards["conv_w"] = conv_w
    w_in_full = _comm_call(_comm_chip_exchange([shards["w_in"]], scatter=False), "gather_w_in")[0]
    loss, grad_x, g_small, g_big, reduced = _local_step(
        x[0], loss_target[0], w, {"w_in": w_in_full.reshape(-1, w_in_full.shape[-1])}, shards, c_arr)

    tail = ("w_in", "glu_w", "w_out")
    packed = _pack([loss] + [g_small[k] for k in SMALL] + [g_small["conv_w"]], 1024)
    halves = [g_big[k].reshape(4, 2, g_big[k].shape[0] // 8, g_big[k].shape[1]) for k in tail]
    halves.append(packed.reshape(1, 2, packed.shape[0] // 2, LANES))
    from_sibling = _comm_call(_comm_pair_swap(halves, half=True), "reduce_pair")
    names = tail + ("small",)
    sums = [_add_half(h, r, c_arr, "sum_pair_" + k, F32 if k == "small" else BF16)
            for k, h, r in zip(names, halves, from_sibling)]
    from_chips = _comm_call(_comm_join(_comm_chip_exchange(sums[:3], scatter=True),
                                       _comm_chip_exchange([sums[3][0]], scatter=False)), "reduce_chips")
    mine = [_sum4(r, "sum_chips_" + k) for k, r in zip(names, from_chips)]
    theirs = _comm_call(_comm_pair_swap(mine), "swap_halves")
    grads = {k: s.reshape(w[k].shape) for k, s in reduced.items()}
    shapes = [loss.shape] + [w[k].shape for k in SMALL] + [(3, 4 * FF_BLK)]
    small_all = _join_rows(mine[3], theirs[3], c_arr, "join_small")
    for k, g in zip(["loss"] + SMALL + ["conv_w_full"], _unpack(small_all, shapes)):
        grads[k] = g
    loss = grads.pop("loss")[0, 0]
    grads["conv_w"] = lax.dynamic_slice_in_dim(grads.pop("conv_w_full"), chip * FF_BLK, FF_BLK, axis=1)

    delta, new_m, new_v = {}, {}, {}
    for k in reduced:
        delta[k], new_m[k], new_v[k] = _adamw(w[k], grads[k], m[k], v[k], "adamw_" + k)
    for k, own, other in zip(tail, mine, theirs):
        grads[k], delta[k], new_m[k], new_v[k] = _adamw_halves(w[k], own, other, m[k], v[k], c_arr, "adamw_" + k)
    wide = ["ssm_b_re", "ssm_b_im"]
    for keys, name in ((wide, "adamw_ssm_b"), ([k for k in SMALL + ["conv_w"] if k not in wide], "adamw_small")):
        outs = _adamw_many(*([d[k] for k in keys] for d in (w, grads, m, v)), name)
        for d, o in zip((delta, new_m, new_v), outs):
            d.update(zip(keys, o))

    return (loss, grad_x[None], *[grads[k] for k in WEIGHTS], *[delta[k] for k in WEIGHTS],
            *[new_m[k] for k in WEIGHTS], *[new_v[k] for k in WEIGHTS])
```

```python
import numpy as np
import jax
import jax.numpy as jnp
from jax import lax
from jax.experimental import pallas as pl
from jax.experimental.pallas import tpu as pltpu

F32 = jnp.float32
BF16 = jnp.bfloat16
MESH = pl.DeviceIdType.MESH

EPS = 1e-6
POOL_WINDOWS = (2, 4, 8, 16)
POOL_GROUP = 128
SSM_GROUP = 16
SSM_STATE = 64
N_SSM_GROUPS = 32
N_STATE = N_SSM_GROUPS * SSM_STATE
QUAD = 256
N_QUAD = N_STATE // QUAD
SLAB = 256
D_SSM = 512
D_POOL = 512
D_FF = 2816
FF_BLK = 1408
HALO = 8
HALO_B = 16
LANES = 128
ADAM_LR, ADAM_B1, ADAM_B2, ADAM_EPS, ADAM_WD, ADAM_STEP = 0.001, 0.9, 0.999, 1e-08, 0.01, 10
VMEM_LIMIT = 56 * 2 ** 20

TL = 512
TF = 256
TC = 256
LT = N_STATE // LANES
SCAN_UNROLL = 8


def _cp(*sem):
    return pltpu.CompilerParams(dimension_semantics=sem, vmem_limit_bytes=VMEM_LIMIT)


def _dot_nn(a, b):
    return jnp.dot(a, b, preferred_element_type=F32)


def _dot_nt(a, b):
    return lax.dot_general(a, b, (((1,), (1,)), ((), ())), preferred_element_type=F32)


def _dot_tn(a, b):
    return lax.dot_general(a, b, (((0,), (0,)), ((), ())), preferred_element_type=F32)


def _rms_fwd(x, g):
    inv = lax.rsqrt(jnp.mean(x * x, axis=-1, keepdims=True) + EPS)
    xh = x * inv
    return xh * g, xh, inv


def _rms_bwd(dy, xh, inv, g):
    dg = jnp.sum(dy * xh, axis=0, keepdims=True)
    dxh = dy * g
    dx = inv * (dxh - xh * jnp.mean(dxh * xh, axis=-1, keepdims=True))
    return dx, dg


_GELU_C = 0.7978845608028654
_GELU_A = 0.044715


def _gelu(y):
    t = jnp.tanh(_GELU_C * (y + _GELU_A * (y * y * y)))
    return 0.5 * y * (1.0 + t), t


def _gelu_grad(y, t):
    return 0.5 * (1.0 + t) + 0.5 * y * (1.0 - t * t) * (_GELU_C * (1.0 + 3.0 * _GELU_A * y * y))


def _sigmoid(x):
    return 1.0 / (1.0 + jnp.exp(-x))


def _full(shape):
    n = len(shape)
    return pl.BlockSpec(shape, lambda *_: (0,) * n)


def _fill_ext(ext_ref, prev_ref, cur_ref, next_ref, i, n, rows):
    ext_ref[0:HALO, :] = jnp.where(i > 0, prev_ref[...], 0.0).astype(ext_ref.dtype)
    ext_ref[HALO:HALO + rows, :] = cur_ref[...]
    ext_ref[HALO + rows:2 * HALO + rows, :] = jnp.where(i < n - 1, next_ref[...], 0.0).astype(ext_ref.dtype)


def _in_proj(x, g, w):
    L, D = x.shape
    E = w.shape[1]

    def body(x_ref, g_ref, w_ref, u_ref, xn_ref):
        y, _, _ = _rms_fwd(x_ref[...], g_ref[...])
        yb = y.astype(BF16)
        xn_ref[...] = yb
        u_ref[...] = _dot_nn(yb, w_ref[...])

    return pl.pallas_call(
        body, name="in_proj", grid=(L // TL,),
        in_specs=[pl.BlockSpec((TL, D), lambda i: (i, 0)), _full((1, D)), _full(w.shape)],
        out_specs=[pl.BlockSpec((TL, E), lambda i: (i, 0)), pl.BlockSpec((TL, D), lambda i: (i, 0))],
        out_shape=[jax.ShapeDtypeStruct((L, E), F32), jax.ShapeDtypeStruct((L, D), BF16)],
        compiler_params=_cp("parallel"))(x, g, w)


def _halo_specs_1d(rows, width, L, col):
    rb = rows // HALO
    last = L // HALO - 1
    return [pl.BlockSpec((HALO, width), lambda i: (jnp.maximum(i * rb - 1, 0), col)),
            pl.BlockSpec((rows, width), lambda i: (i, col)),
            pl.BlockSpec((HALO, width), lambda i: (jnp.minimum((i + 1) * rb, last), col))]


def _pooled_from_ext(ext_ref, t0, rows, L):
    t = t0 + lax.broadcasted_iota(jnp.int32, (rows, 1), 0)
    outs = []
    for gi, w in enumerate(POOL_WINDOWS):
        half = w // 2
        cs = slice(gi * POOL_GROUP, (gi + 1) * POOL_GROUP)
        acc = ext_ref[pl.ds(HALO - half, rows), cs]
        for s in range(-half + 1, half):
            acc = acc + ext_ref[pl.ds(HALO + s, rows), cs]
        cnt = (jnp.minimum(t + half, L) - jnp.maximum(t - half, 0)).astype(F32)
        outs.append(acc / cnt - ext_ref[pl.ds(HALO, rows), cs])
    return outs


def _pool_fwd(u, pool_w_b, pool_scale, g_pool):
    L = u.shape[0]
    n = L // TL

    def body(prev_ref, cur_ref, next_ref, pw_ref, ps_ref, g_ref, out_ref, ext_ref):
        i = pl.program_id(0)
        _fill_ext(ext_ref, prev_ref, cur_ref, next_ref, i, n, TL)
        pooled = _pooled_from_ext(ext_ref, i * TL, TL, L)
        ypre = jnp.concatenate([_dot_nn(pooled[gi].astype(BF16), pw_ref[gi]) for gi in range(4)], axis=-1)
        yn, _, _ = _rms_fwd(ypre * ps_ref[...], g_ref[...])
        out_ref[...] = yn.astype(BF16)

    return pl.pallas_call(
        body, name="pool_fwd", grid=(n,),
        in_specs=_halo_specs_1d(TL, D_POOL, L, 0) + [_full(pool_w_b.shape), _full((1, D_POOL)), _full((1, D_POOL))],
        out_specs=pl.BlockSpec((TL, D_POOL), lambda i: (i, 0)),
        out_shape=jax.ShapeDtypeStruct((L, D_POOL), BF16),
        scratch_shapes=[pltpu.VMEM((TL + 2 * HALO, D_POOL), F32)],
        compiler_params=_cp("parallel"))(u, u, u, pool_w_b, pool_scale, g_pool)


def _pool_bwd_local(dh1, u, w_out_b, pool_w_b, pool_scale, g_pool, comm=None):
    L = u.shape[0]
    n = L // TL
    D = dh1.shape[1]

    def body(dh_ref, prev_ref, cur_ref, next_ref, wo_ref, pw_ref, ps_ref, g_ref,
             dp_ref, gpw_ref, gps_ref, gg_ref, ext_ref):
        i = pl.program_id(0)

        @pl.when(i == 0)
        def _():
            gpw_ref[...] = jnp.zeros_like(gpw_ref)
            gps_ref[...] = jnp.zeros_like(gps_ref)
            gg_ref[...] = jnp.zeros_like(gg_ref)

        _fill_ext(ext_ref, prev_ref, cur_ref, next_ref, i, n, TL)
        pooled = [p.astype(BF16) for p in _pooled_from_ext(ext_ref, i * TL, TL, L)]
        ypre = jnp.concatenate([_dot_nn(pooled[gi], pw_ref[gi]) for gi in range(4)], axis=-1)
        ps = ps_ref[...]
        g = g_ref[...]
        _, xh, inv = _rms_fwd(ypre * ps, g)
        d_yn = _dot_nt(dh_ref[...].astype(BF16), wo_ref[...])
        d_y, dg = _rms_bwd(d_yn, xh, inv, g)
        gg_ref[...] += dg
        gps_ref[...] += jnp.sum(d_y * ypre, axis=0, keepdims=True)
        d_ypre = (d_y * ps).astype(BF16)
        for gi in range(4):
            cs = slice(gi * POOL_GROUP, (gi + 1) * POOL_GROUP)
            dp_ref[:, cs] = _dot_nt(d_ypre[:, cs], pw_ref[gi])
            gpw_ref[gi] += _dot_tn(pooled[gi], d_ypre[:, cs])

    return _hosted_call(
        body, comm, name="pool_bwd_local", grid=(n,),
        in_specs=[pl.BlockSpec((TL, D), lambda i: (i, 0))] + _halo_specs_1d(TL, D_POOL, L, 0)
        + [pl.BlockSpec((D_POOL, D), lambda i: (0, 0)), _full(pool_w_b.shape), _full((1, D_POOL)), _full((1, D_POOL))],
        out_specs=[pl.BlockSpec((TL, D_POOL), lambda i: (i, 0)), _full(pool_w_b.shape),
                   _full((1, D_POOL)), _full((1, D_POOL))],
        out_shape=[jax.ShapeDtypeStruct((L, D_POOL), F32), jax.ShapeDtypeStruct(pool_w_b.shape, F32),
                   jax.ShapeDtypeStruct((1, D_POOL), F32), jax.ShapeDtypeStruct((1, D_POOL), F32)],
        scratch_shapes=[pltpu.VMEM((TL + 2 * HALO, D_POOL), F32)],
        args=(dh1, u, u, u, w_out_b, pool_w_b, pool_scale, g_pool))


def _pool_bwd_window(d_pooled):
    L = d_pooled.shape[0]
    n = L // TL
    R = TL + 2 * HALO

    def body(prev_ref, cur_ref, next_ref, out_ref, ext_ref, q_ref):
        i = pl.program_id(0)
        _fill_ext(ext_ref, prev_ref, cur_ref, next_ref, i, n, TL)
        tr = i * TL - HALO + lax.broadcasted_iota(jnp.int32, (R, 1), 0)
        for gi, w in enumerate(POOL_WINDOWS):
            half = w // 2
            cs = slice(gi * POOL_GROUP, (gi + 1) * POOL_GROUP)
            cnt = jnp.maximum(jnp.minimum(tr + half, L) - jnp.maximum(tr - half, 0), 1).astype(F32)
            q_ref[:, cs] = ext_ref[:, cs] / cnt
        for gi, w in enumerate(POOL_WINDOWS):
            half = w // 2
            cs = slice(gi * POOL_GROUP, (gi + 1) * POOL_GROUP)
            acc = q_ref[pl.ds(HALO - half + 1, TL), cs]
            for s in range(-half + 2, half + 1):
                acc = acc + q_ref[pl.ds(HALO + s, TL), cs]
            out_ref[:, cs] = acc - ext_ref[pl.ds(HALO, TL), cs]

    return pl.pallas_call(
        body, name="pool_bwd_window", grid=(n,),
        in_specs=_halo_specs_1d(TL, D_POOL, L, 0),
        out_specs=pl.BlockSpec((TL, D_POOL), lambda i: (i, 0)),
        out_shape=jax.ShapeDtypeStruct((L, D_POOL), F32),
        scratch_shapes=[pltpu.VMEM((R, D_POOL), F32), pltpu.VMEM((R, D_POOL), F32)],
        compiler_params=_cp("parallel"))(d_pooled, d_pooled, d_pooled)


def _ssm_param_fn(lnar, aim, ldt):
    dt = jnp.exp(ldt)
    a_re = -jnp.exp(lnar)
    mag = jnp.exp(a_re * dt)
    ang = aim * dt
    lr, li = mag * jnp.cos(ang), mag * jnp.sin(ang)
    den = a_re * a_re + aim * aim
    fr = ((lr - 1.0) * a_re + li * aim) / den
    fi = (li * a_re - (lr - 1.0) * aim) / den
    return lr, li, fr, fi


def _ssm_params(lnar, aim, ldt):
    def body(a_ref, b_ref, c_ref, lr_ref, li_ref, fr_ref, fi_ref):
        lr, li, fr, fi = _ssm_param_fn(a_ref[...], b_ref[...], c_ref[...])
        lr_ref[...] = lr
        li_ref[...] = li
        fr_ref[...] = fr
        fi_ref[...] = fi

    sh = jax.ShapeDtypeStruct(lnar.shape, F32)
    return pl.pallas_call(body, name="ssm_params", out_shape=[sh] * 4)(lnar, aim, ldt)


def _ssm_params_bwd(lnar, aim, ldt, glr, gli, gfr, gfi):
    def body(a_ref, b_ref, c_ref, g0, g1, g2, g3, da_ref, db_ref, dc_ref):
        _, vjp = jax.vjp(_ssm_param_fn, a_ref[...], b_ref[...], c_ref[...])
        da, db, dc = vjp((g0[...], g1[...], g2[...], g3[...]))
        da_ref[...] = da
        db_ref[...] = db
        dc_ref[...] = jnp.sum(dc, axis=1, keepdims=True)

    return pl.pallas_call(
        body, name="ssm_params_bwd",
        out_shape=[jax.ShapeDtypeStruct(lnar.shape, F32), jax.ShapeDtypeStruct(aim.shape, F32),
                   jax.ShapeDtypeStruct((ldt.shape[0], 1), F32)])(lnar, aim, ldt, glr, gli, gfr, gfi)


def _b_block(g):
    q, gl = divmod(g, 4)
    r0, c0 = gl * SSM_STATE, (q % 4) * 4 * SSM_GROUP + gl * SSM_GROUP
    return q, slice(r0, r0 + SSM_STATE), slice(c0, c0 + SSM_GROUP)


def _c_block(g):
    q, rows, cols = _b_block(g)
    return q, cols, rows


def _ssm_expand(b_re, b_im, c_re, c_im):
    def body(bre_ref, bim_ref, cre_ref, cim_ref, *rest):
        outs, tmp = rest[:8], rest[8]
        for d in range(2):
            for j, (src, where) in enumerate(((bre_ref, _b_block), (bim_ref, _b_block),
                                              (cre_ref, _c_block), (cim_ref, _c_block))):
                tmp[...] = jnp.zeros_like(tmp)
                for g in range(N_SSM_GROUPS):
                    q, rows, cols = where(g)
                    tmp[q, rows, cols] = src[d, g]
                outs[4 * d + j][...] = tmp[...].astype(BF16)

    dense = jax.ShapeDtypeStruct((N_QUAD, QUAD, SLAB), BF16)
    return pl.pallas_call(body, name="ssm_expand", out_shape=[dense] * 8,
                          scratch_shapes=[pltpu.VMEM((N_QUAD, QUAD, SLAB), F32)],
                          compiler_params=pltpu.CompilerParams(vmem_limit_bytes=VMEM_LIMIT))(b_re, b_im, c_re, c_im)


def _tile_rows(j):
    return pl.ds(j * TC, TC)


def _token_rows(t):
    return pl.ds(t, LT, stride=TC)


def _put_tiles(ref, q, x):
    for h in range(QUAD // LANES):
        ref[_tile_rows(2 * q + h), :] = x[:, h * LANES:(h + 1) * LANES]


def _get_tiles(ref, q):
    return jnp.concatenate([ref[_tile_rows(2 * q + h), :] for h in range(QUAD // LANES)], axis=1)


def _token_order(reverse):
    return (lambda k: TC - 1 - k) if reverse else (lambda k: k)


def _ssm_scan_fwd(u, b_re, b_im, c_re, c_im, f2, lam, reverse, comm=None):
    L = u.shape[0]
    nc = L // TC
    chunk = (lambda i: nc - 1 - i) if reverse else (lambda i: i)
    token = _token_order(reverse)

    def body(u_ref, bre_ref, bim_ref, cre_ref, cim_ref, f_ref, lam_ref,
             y_ref, sre_ref, sim_ref, in_re, in_im, carry_re, carry_im):
        @pl.when(pl.program_id(0) == 0)
        def _():
            carry_re[...] = jnp.zeros_like(carry_re)
            carry_im[...] = jnp.zeros_like(carry_im)

        ub = u_ref[...].astype(BF16)
        for q in range(N_QUAD):
            qs = slice(q * QUAD, (q + 1) * QUAD)
            us = ub[:, (q // 4) * SLAB:(q // 4 + 1) * SLAB]
            bur = _dot_nt(us, bre_ref[q])
            bui = _dot_nt(us, bim_ref[q])
            fr = f_ref[0:1, qs]
            fi = f_ref[1:2, qs]
            _put_tiles(in_re, q, fr * bur - fi * bui)
            _put_tiles(in_im, q, fr * bui + fi * bur)

        lr, li = lam_ref[0], lam_ref[1]

        def step(k, s):
            rows = _token_rows(token(k))
            sr = lr * s[0] - li * s[1] + in_re[rows, :]
            si = lr * s[1] + li * s[0] + in_im[rows, :]
            sre_ref[rows, :] = sr
            sim_ref[rows, :] = si
            return sr, si

        sr, si = lax.fori_loop(0, TC, step, (carry_re[...], carry_im[...]), unroll=SCAN_UNROLL)
        carry_re[...] = sr
        carry_im[...] = si
        for j in range(D_SSM // SLAB):
            acc = jnp.zeros((TC, SLAB), F32)
            for q in range(4 * j, 4 * j + 4):
                acc = acc + _dot_nt(_get_tiles(sre_ref, q).astype(BF16), cre_ref[q])
                acc = acc - _dot_nt(_get_tiles(sim_ref, q).astype(BF16), cim_ref[q])
            y_ref[:, j * SLAB:(j + 1) * SLAB] = acc

    states = pl.BlockSpec((None, LT * TC, LANES), lambda i: (chunk(i), 0, 0))
    return _hosted_call(
        body, comm, name="ssm_scan_rev" if reverse else "ssm_scan_fwd", grid=(nc,),
        in_specs=[pl.BlockSpec((TC, D_SSM), lambda i: (chunk(i), 1))]
        + [_full(b_re.shape)] * 4 + [_full(f2.shape), _full(lam.shape)],
        out_specs=[pl.BlockSpec((TC, D_SSM), lambda i: (chunk(i), 0)), states, states],
        out_shape=[jax.ShapeDtypeStruct((L, D_SSM), F32), jax.ShapeDtypeStruct((nc, LT * TC, LANES), F32),
                   jax.ShapeDtypeStruct((nc, LT * TC, LANES), F32)],
        scratch_shapes=[pltpu.VMEM((LT * TC, LANES), F32), pltpu.VMEM((LT * TC, LANES), F32),
                        pltpu.VMEM((LT, LANES), F32), pltpu.VMEM((LT, LANES), F32)],
        args=(u, b_re, b_im, c_re, c_im, f2, lam))


def _ssm_scan_bwd(dy, u, s_re, s_im, b_re, b_im, c_re, c_im, f2, lam, reverse, comm=None):
    L = u.shape[0]
    nc = L // TC
    chunk = (lambda i: nc - 1 - i) if reverse else (lambda i: i)
    token = _token_order(reverse)

    def body(dy_ref, u_ref, sre_ref, sim_ref, bre_ref, bim_ref, cre_ref, cim_ref, f_ref, lam_ref,
             du_ref, ob_re, ob_im, oc_re, oc_im, gv_ref, gl_ref,
             a_re, a_im, carry_re, carry_im, gbr_ref, gbi_ref, gcr_ref, gci_ref):
        @pl.when(pl.program_id(0) == 0)
        def _():
            carry_re[...] = jnp.zeros_like(carry_re)
            carry_im[...] = jnp.zeros_like(carry_im)
            for r in (gbr_ref, gbi_ref, gcr_ref, gci_ref, gv_ref, gl_ref):
                r[...] = jnp.zeros_like(r)

        dyb = dy_ref[...].astype(BF16)
        ub = u_ref[...].astype(BF16)
        for q in range(N_QUAD):
            ds = dyb[:, (q // 4) * SLAB:(q // 4 + 1) * SLAB]
            _put_tiles(a_re, q, _dot_nn(ds, cre_ref[q]))
            _put_tiles(a_im, q, -_dot_nn(ds, cim_ref[q]))
            gcr_ref[q] += _dot_tn(ds, _get_tiles(sre_ref, q).astype(BF16))
            gci_ref[q] -= _dot_tn(ds, _get_tiles(sim_ref, q).astype(BF16))

        lr, li = lam_ref[0], lam_ref[1]

        def step(k, c):
            ar, ai, glr, gli = c
            rows = _token_rows(token(k))
            sr, si = sre_ref[rows, :], sim_ref[rows, :]
            glr = glr + (ar * sr + ai * si)
            gli = gli + (ai * sr - ar * si)
            nr = lr * ar + li * ai + a_re[rows, :]
            ni = lr * ai - li * ar + a_im[rows, :]
            a_re[rows, :] = nr
            a_im[rows, :] = ni
            return nr, ni, glr, gli

        zero = jnp.zeros((LT, LANES), F32)
        ar, ai, glr, gli = lax.fori_loop(0, TC, step, (carry_re[...], carry_im[...], zero, zero), unroll=SCAN_UNROLL)
        carry_re[...] = ar
        carry_im[...] = ai
        gl_ref[0] += glr
        gl_ref[1] += gli
        for j in range(D_SSM // SLAB):
            us = ub[:, j * SLAB:(j + 1) * SLAB]
            acc = jnp.zeros((TC, SLAB), F32)
            for q in range(4 * j, 4 * j + 4):
                qs = slice(q * QUAD, (q + 1) * QUAD)
                ar = _get_tiles(a_re, q)
                ai = _get_tiles(a_im, q)
                bur = _dot_nt(us, bre_ref[q])
                bui = _dot_nt(us, bim_ref[q])
                gv_ref[0:1, qs] += jnp.sum(ar * bur + ai * bui, axis=0, keepdims=True)
                gv_ref[1:2, qs] += jnp.sum(ai * bur - ar * bui, axis=0, keepdims=True)
                fr = f_ref[0:1, qs]
                fi = f_ref[1:2, qs]
                dbr = (fr * ar + fi * ai).astype(BF16)
                dbi = (fr * ai - fi * ar).astype(BF16)
                gbr_ref[q] += _dot_tn(dbr, us)
                gbi_ref[q] += _dot_tn(dbi, us)
                acc = acc + _dot_nn(dbr, bre_ref[q]) + _dot_nn(dbi, bim_ref[q])
            du_ref[:, j * SLAB:(j + 1) * SLAB] = acc

        @pl.when(pl.program_id(0) == nc - 1)
        def _():
            for g in range(N_SSM_GROUPS):
                q, rows, cols = _b_block(g)
                ob_re[g] = gbr_ref[q, rows, cols]
                ob_im[g] = gbi_ref[q, rows, cols]
                oc_re[g] = gcr_ref[q, cols, rows]
                oc_im[g] = gci_ref[q, cols, rows]

    gb = jax.ShapeDtypeStruct((N_SSM_GROUPS, SSM_STATE, SSM_GROUP), F32)
    gc = jax.ShapeDtypeStruct((N_SSM_GROUPS, SSM_GROUP, SSM_STATE), F32)
    dense = pltpu.VMEM((N_QUAD, QUAD, SLAB), F32)
    states = pl.BlockSpec((None, LT * TC, LANES), lambda i: (chunk(i), 0, 0))
    return _hosted_call(
        body, comm, name="ssm_bwd_rev" if reverse else "ssm_bwd_fwd", grid=(nc,),
        in_specs=[pl.BlockSpec((TC, D_SSM), lambda i: (chunk(i), 0)),
                  pl.BlockSpec((TC, D_SSM), lambda i: (chunk(i), 1)), states, states]
        + [_full(b_re.shape)] * 4 + [_full(f2.shape), _full(lam.shape)],
        out_specs=[pl.BlockSpec((TC, D_SSM), lambda i: (chunk(i), 0)), _full(gb.shape), _full(gb.shape),
                   _full(gc.shape), _full(gc.shape), _full((2, N_STATE)), _full((2, LT, LANES))],
        out_shape=[jax.ShapeDtypeStruct((L, D_SSM), F32), gb, gb, gc, gc, jax.ShapeDtypeStruct((2, N_STATE), F32),
                   jax.ShapeDtypeStruct((2, LT, LANES), F32)],
        scratch_shapes=[pltpu.VMEM((LT * TC, LANES), F32), pltpu.VMEM((LT * TC, LANES), F32),
                        pltpu.VMEM((LT, LANES), F32), pltpu.VMEM((LT, LANES), F32), dense, dense, dense, dense],
        args=(dy, u, s_re, s_im, b_re, b_im, c_re, c_im, f2, lam))


def _ssm_post(yf, yb, u, d, glu_w, glu_b):
    y = yf + yb + d * u
    z, t = _gelu(y)
    zb = z.astype(BF16)
    gate = _sigmoid(_dot_nn(zb, glu_w) + glu_b)
    return y, z, t, zb, gate


def _mix_out(yn_pool, yf, yb, u, x, ssm_d, glu_w_b, glu_b, g_ssm, w_out_b, g_ffn):
    L, D = x.shape

    def body(ynp_ref, yf_ref, yb_ref, u_ref, x_ref, d_ref, gw_ref, gb_ref, gs_ref, wo_ref, gf_ref,
             h1_ref, hn_ref, ycat_ref):
        _, z, _, _, gate = _ssm_post(yf_ref[...], yb_ref[...], u_ref[...], d_ref[...], gw_ref[...], gb_ref[...])
        yns, _, _ = _rms_fwd(z * gate, gs_ref[...])
        ynsb = yns.astype(BF16)
        ynp = ynp_ref[...]
        ycat_ref[:, 0:D_POOL] = ynp
        ycat_ref[:, D_POOL:D] = ynsb
        h1 = x_ref[...] + _dot_nn(ynp, wo_ref[0:D_POOL, :]) + _dot_nn(ynsb, wo_ref[D_POOL:D, :])
        h1_ref[...] = h1
        hn, _, _ = _rms_fwd(h1, gf_ref[...])
        hn_ref[...] = hn.astype(BF16)

    half = lambda c: pl.BlockSpec((TL, D_SSM), lambda i: (i, c))
    row = pl.BlockSpec((TL, D), lambda i: (i, 0))
    return pl.pallas_call(
        body, name="mix_out", grid=(L // TL,),
        in_specs=[half(0), half(0), half(0), half(1), row, _full((1, D_SSM)), _full(glu_w_b.shape),
                  _full((1, D_SSM)), _full((1, D_SSM)), _full(w_out_b.shape), _full((1, D))],
        out_specs=[row, row, row],
        out_shape=[jax.ShapeDtypeStruct((L, D), F32), jax.ShapeDtypeStruct((L, D), BF16),
                   jax.ShapeDtypeStruct((L, D), BF16)],
        compiler_params=_cp("parallel"))(yn_pool, yf, yb, u, x, ssm_d, glu_w_b, glu_b, g_ssm, w_out_b, g_ffn)


def _ssm_bwd_local(dh1, yf, yb, u, ssm_d, glu_w_b, glu_b, g_ssm, w_out_b):
    L, D = dh1.shape

    def body(dh_ref, yf_ref, yb_ref, u_ref, d_ref, gw_ref, gb_ref, gs_ref, wo_ref,
             dy_ref, du_ref, ggw_ref, ggb_ref, gd_ref, ggs_ref):
        @pl.when(pl.program_id(0) == 0)
        def _():
            for r in (ggw_ref, ggb_ref, gd_ref, ggs_ref):
                r[...] = jnp.zeros_like(r)

        u = u_ref[...]
        d = d_ref[...]
        y, z, t, zb, gate = _ssm_post(yf_ref[...], yb_ref[...], u, d, gw_ref[...], gb_ref[...])
        gs = gs_ref[...]
        _, xh, inv = _rms_fwd(z * gate, gs)
        d_yn = _dot_nt(dh_ref[...].astype(BF16), wo_ref[...])
        d_o, dgs = _rms_bwd(d_yn, xh, inv, gs)
        ggs_ref[...] += dgs
        d_zg = d_o * z * gate * (1.0 - gate)
        d_zgb = d_zg.astype(BF16)
        ggb_ref[...] += jnp.sum(d_zg, axis=0, keepdims=True)
        ggw_ref[...] += _dot_tn(zb, d_zgb)
        d_z = d_o * gate + _dot_nt(d_zgb, gw_ref[...])
        d_y = d_z * _gelu_grad(y, t)
        gd_ref[...] += jnp.sum(d_y * u, axis=0, keepdims=True)
        dy_ref[...] = d_y
        du_ref[...] = d_y * d

    half = lambda c: pl.BlockSpec((TL, D_SSM), lambda i: (i, c))
    vec = _full((1, D_SSM))
    return pl.pallas_call(
        body, name="ssm_bwd_local", grid=(L // TL,),
        in_specs=[pl.BlockSpec((TL, D), lambda i: (i, 0)), half(0), half(0), half(1), vec, _full(glu_w_b.shape),
                  vec, vec, pl.BlockSpec((D_SSM, D), lambda i: (1, 0))],
        out_specs=[half(0), half(0), _full(glu_w_b.shape), vec, vec, vec],
        out_shape=[jax.ShapeDtypeStruct((L, D_SSM), F32), jax.ShapeDtypeStruct((L, D_SSM), F32),
                   jax.ShapeDtypeStruct(glu_w_b.shape, F32)] + [jax.ShapeDtypeStruct((1, D_SSM), F32)] * 3,
        compiler_params=_cp("arbitrary"))(dh1, yf, yb, u, ssm_d, glu_w_b, glu_b, g_ssm, w_out_b)


def _in_bwd(du_pool, du_a, du_b, du_c, dh1, x, g, w_in_b):
    L, D = x.shape

    def body(p_ref, a_ref, b_ref, c_ref, dh_ref, x_ref, g_ref, w_ref, dx_ref, dub_ref, gg_ref):
        @pl.when(pl.program_id(0) == 0)
        def _():
            gg_ref[...] = jnp.zeros_like(gg_ref)

        dub_ref[:, 0:D_POOL] = p_ref[...].astype(BF16)
        dub_ref[:, D_POOL:D] = (a_ref[...] + b_ref[...] + c_ref[...]).astype(BF16)
        d_xn = _dot_nt(dub_ref[...], w_ref[...])
        gv = g_ref[...]
        _, xh, inv = _rms_fwd(x_ref[...], gv)
        dx, dg = _rms_bwd(d_xn, xh, inv, gv)
        gg_ref[...] += dg
        dx_ref[...] = dh_ref[...] + dx

    half = pl.BlockSpec((TL, D_SSM), lambda i: (i, 0))
    row = pl.BlockSpec((TL, D), lambda i: (i, 0))
    return pl.pallas_call(
        body, name="in_bwd", grid=(L // TL,),
        in_specs=[half, half, half, half, row, row, _full((1, D)), _full(w_in_b.shape)],
        out_specs=[row, row, _full((1, D))],
        out_shape=[jax.ShapeDtypeStruct((L, D), F32), jax.ShapeDtypeStruct((L, D), BF16),
                   jax.ShapeDtypeStruct((1, D), F32)],
        compiler_params=_cp("arbitrary"))(du_pool, du_a, du_b, du_c, dh1, x, g, w_in_b)


def _ffn_up(hn, w_up4):
    L, D = hn.shape

    def body(h_ref, w_ref, o_ref):
        o_ref[...] = _dot_nn(h_ref[...], w_ref[...]).astype(BF16)

    return pl.pallas_call(
        body, name="ffn_up", grid=(4, L // TF),
        in_specs=[pl.BlockSpec((TF, D), lambda j, i: (i, 0)), pl.BlockSpec((None, D, FF_BLK), lambda j, i: (j, 0, 0))],
        out_specs=pl.BlockSpec((TF, FF_BLK), lambda j, i: (i, j)),
        out_shape=jax.ShapeDtypeStruct((L, 4 * FF_BLK), BF16),
        compiler_params=_cp("parallel", "parallel"))(hn, w_up4)


def _halo_specs_2d(rows, width, L, col, order):
    rb = rows // HALO_B
    last = L // HALO_B - 1
    if order == "ik":
        wrap = lambda f: (lambda i, k: f(i, k))
    else:
        wrap = lambda f: (lambda k, i: f(i, k))
    return [pl.BlockSpec((HALO_B, width), wrap(lambda i, k: (jnp.maximum(i * rb - 1, 0), col(k)))),
            pl.BlockSpec((rows, width), wrap(lambda i, k: (i, col(k)))),
            pl.BlockSpec((HALO_B, width), wrap(lambda i, k: (jnp.minimum((i + 1) * rb, last), col(k))))]


def _shift_mats(rows):
    r = lax.broadcasted_iota(jnp.int32, (rows, rows), 0)
    c = lax.broadcasted_iota(jnp.int32, (rows, rows), 1)
    return (c == r - 1).astype(BF16), (c == r + 1).astype(BF16)


def _neighbours(x, prev_ref, next_ref, cs, i, n, mats):
    rows = x.shape[0]
    row = lax.broadcasted_iota(jnp.int32, (rows, 1), 0)
    before = jnp.where(i > 0, prev_ref[:, cs].astype(F32)[HALO_B - 1:HALO_B, :], 0.0)
    after = jnp.where(i < n - 1, next_ref[:, cs].astype(F32)[0:1, :], 0.0)
    return (jnp.where(row == 0, before, _dot_nn(mats[0], x)),
            jnp.where(row == rows - 1, after, _dot_nn(mats[1], x)))


def _conv3(x, before, after, w, b):
    return before * w[0:1, :] + x.astype(F32) * w[1:2, :] + after * w[2:3, :] + b


def _col_chunks(width, size=256):
    return [slice(c, min(c + size, width)) for c in range(0, width, size)]


def _ffn_down_loss(up, conv_w, conv_b, w_down_b, h1, target, g_final):
    L, D = h1.shape
    n = L // TF
    nk = D_FF // FF_BLK

    def body(vp, vc, vn, gp, gc, gn, wv_ref, wg_ref, bv_ref, bg_ref, wd_ref, h1_ref, t_ref, gf_ref,
             a_ref, dh2_ref, dh2b_ref, loss_ref, gg_ref, acc_ref):
        i = pl.program_id(0)
        k = pl.program_id(1)

        @pl.when((i == 0) & (k == 0))
        def _():
            loss_ref[...] = jnp.zeros_like(loss_ref)
            gg_ref[...] = jnp.zeros_like(gg_ref)

        @pl.when(k == 0)
        def _():
            acc_ref[...] = jnp.zeros_like(acc_ref)

        mats = _shift_mats(TF)
        for cs in _col_chunks(FF_BLK):
            xv, xg = vc[:, cs], gc[:, cs]
            val = _conv3(xv, *_neighbours(xv, vp, vn, cs, i, n, mats), wv_ref[:, cs], bv_ref[:, cs])
            gate = _conv3(xg, *_neighbours(xg, gp, gn, cs, i, n, mats), wg_ref[:, cs], bg_ref[:, cs])
            a_ref[:, cs] = (val * (gate * _sigmoid(gate))).astype(BF16)
        acc_ref[...] += _dot_nn(a_ref[...], wd_ref[...])

        @pl.when(k == nk - 1)
        def _():
            gf = gf_ref[...]
            y, xh, inv = _rms_fwd(h1_ref[...] + acc_ref[...], gf)
            diff = y - t_ref[...]
            part = 0.5 * jnp.sum(jnp.mean(diff * diff, axis=-1, keepdims=True), axis=0, keepdims=True)
            loss_ref[...] += jnp.broadcast_to(part, loss_ref.shape)
            dx, dg = _rms_bwd(diff * (1.0 / D), xh, inv, gf)
            gg_ref[...] += dg
            dh2_ref[...] = dx
            dh2b_ref[...] = dx.astype(BF16)

    row = pl.BlockSpec((TF, D), lambda i, k: (i, 0))
    cw = lambda off: pl.BlockSpec((3, FF_BLK), lambda i, k: (0, k + off))
    cb = lambda off: pl.BlockSpec((1, FF_BLK), lambda i, k: (0, k + off))
    return pl.pallas_call(
        body, name="ffn_down_loss", grid=(n, nk),
        in_specs=_halo_specs_2d(TF, FF_BLK, L, lambda k: k, "ik") + _halo_specs_2d(TF, FF_BLK, L, lambda k: k + nk, "ik")
        + [cw(0), cw(nk), cb(0), cb(nk), pl.BlockSpec((FF_BLK, D), lambda i, k: (k, 0)), row, row, _full((1, D))],
        out_specs=[pl.BlockSpec((TF, FF_BLK), lambda i, k: (i, k)), row, row, _full((1, LANES)), _full((1, D))],
        out_shape=[jax.ShapeDtypeStruct((L, D_FF), BF16), jax.ShapeDtypeStruct((L, D), F32),
                   jax.ShapeDtypeStruct((L, D), BF16), jax.ShapeDtypeStruct((1, LANES), F32),
                   jax.ShapeDtypeStruct((1, D), F32)],
        scratch_shapes=[pltpu.VMEM((TF, D), F32)],
        compiler_params=_cp("arbitrary", "arbitrary"))(
            up, up, up, up, up, up, conv_w, conv_w, conv_b, conv_b, w_down_b, h1, target, g_final)


def _ffn_act_bwd(up, conv_w, conv_b, w_down_b, dh2):
    L, D = dh2.shape
    n = L // TF
    nk = D_FF // FF_BLK

    def body(vp, vc, vn, gp, gc, gn, wv_ref, wg_ref, bv_ref, bg_ref, wd_ref, dh_ref,
             dv_ref, dg_ref, gcv_ref, gcg_ref):
        i = pl.program_id(1)

        @pl.when(i == 0)
        def _():
            gcv_ref[...] = jnp.zeros_like(gcv_ref)
            gcg_ref[...] = jnp.zeros_like(gcg_ref)

        mats = _shift_mats(TF)
        dh = dh_ref[...]
        for cs in _col_chunks(FF_BLK):
            xv, xg = vc[:, cs], gc[:, cs]
            v_rows = _neighbours(xv, vp, vn, cs, i, n, mats)
            g_rows = _neighbours(xg, gp, gn, cs, i, n, mats)
            val = _conv3(xv, *v_rows, wv_ref[:, cs], bv_ref[:, cs])
            gate = _conv3(xg, *g_rows, wg_ref[:, cs], bg_ref[:, cs])
            d_a = _dot_nt(dh, wd_ref[cs, :])
            sg = _sigmoid(gate)
            d_val = d_a * (gate * sg)
            d_gate = d_a * val * (sg * (1.0 + gate * (1.0 - sg)))
            dv_ref[:, cs] = d_val.astype(BF16)
            dg_ref[:, cs] = d_gate.astype(BF16)
            for d, x, (before, after), gref in ((d_val, xv, v_rows, gcv_ref), (d_gate, xg, g_rows, gcg_ref)):
                for j, shifted in enumerate((before, x.astype(F32), after)):
                    gref[j:j + 1, cs] += jnp.sum(d * shifted, axis=0, keepdims=True)
                gref[3:4, cs] += jnp.sum(d, axis=0, keepdims=True)

    cw = lambda off: pl.BlockSpec((3, FF_BLK), lambda k, i: (0, k + off))
    cb = lambda off: pl.BlockSpec((1, FF_BLK), lambda k, i: (0, k + off))
    blk = pl.BlockSpec((TF, FF_BLK), lambda k, i: (i, k))
    acc = pl.BlockSpec((4, FF_BLK), lambda k, i: (0, k))
    return pl.pallas_call(
        body, name="ffn_act_bwd", grid=(nk, n),
        in_specs=_halo_specs_2d(TF, FF_BLK, L, lambda k: k, "ki") + _halo_specs_2d(TF, FF_BLK, L, lambda k: k + nk, "ki")
        + [cw(0), cw(nk), cb(0), cb(nk), pl.BlockSpec((FF_BLK, D), lambda k, i: (k, 0)),
           pl.BlockSpec((TF, D), lambda k, i: (i, 0))],
        out_specs=[blk, blk, acc, acc],
        out_shape=[jax.ShapeDtypeStruct((L, D_FF), BF16), jax.ShapeDtypeStruct((L, D_FF), BF16),
                   jax.ShapeDtypeStruct((4, D_FF), F32), jax.ShapeDtypeStruct((4, D_FF), F32)],
        compiler_params=_cp("arbitrary", "arbitrary"))(
            up, up, up, up, up, up, conv_w, conv_w, conv_b, conv_b, w_down_b, dh2)


def _ffn_up_bwd(d_val, d_gate, conv_w, w_up4, h1, dh2, g_ffn):
    L, D = h1.shape
    n = L // TF
    nk = D_FF // FF_BLK

    def body(vp, vc, vn, gp, gc, gn, wv_ref, wg_ref, uv_ref, ug_ref, h1_ref, dh2_ref, g_ref,
             dup_ref, dh1_ref, dh1b_ref, gg_ref, acc_ref):
        i = pl.program_id(0)
        k = pl.program_id(1)

        @pl.when((i == 0) & (k == 0))
        def _():
            gg_ref[...] = jnp.zeros_like(gg_ref)

        @pl.when(k == 0)
        def _():
            acc_ref[...] = jnp.zeros_like(acc_ref)

        mats = _shift_mats(TF)
        for j, (blocks, w_ref, wu_ref) in enumerate((((vp, vc, vn), wv_ref, uv_ref), ((gp, gc, gn), wg_ref, ug_ref))):
            for cs in _col_chunks(FF_BLK):
                d = blocks[1][:, cs]
                before, after = _neighbours(d, blocks[0], blocks[2], cs, i, n, mats)
                w = w_ref[:, cs]
                dup_ref[j, :, cs] = (after * w[0:1, :] + d.astype(F32) * w[1:2, :] + before * w[2:3, :]).astype(BF16)
            acc_ref[...] += _dot_nt(dup_ref[j], wu_ref[...])

        @pl.when(k == nk - 1)
        def _():
            g = g_ref[...]
            _, xh, inv = _rms_fwd(h1_ref[...], g)
            dx, dg = _rms_bwd(acc_ref[...], xh, inv, g)
            gg_ref[...] += dg
            dh1 = dh2_ref[...] + dx
            dh1_ref[...] = dh1
            dh1b_ref[...] = dh1.astype(BF16)

    row = pl.BlockSpec((TF, D), lambda i, k: (i, 0))
    cw = lambda off: pl.BlockSpec((3, FF_BLK), lambda i, k: (0, k + off))
    wu = lambda off: pl.BlockSpec((None, D, FF_BLK), lambda i, k: (k + off, 0, 0))
    return pl.pallas_call(
        body, name="ffn_up_bwd", grid=(n, nk),
        in_specs=_halo_specs_2d(TF, FF_BLK, L, lambda k: k, "ik") + _halo_specs_2d(TF, FF_BLK, L, lambda k: k, "ik")
        + [cw(0), cw(nk), wu(0), wu(nk), row, row, _full((1, D))],
        out_specs=[pl.BlockSpec((2, None, TF, FF_BLK), lambda i, k: (0, k, i, 0)), row, row, _full((1, D))],
        out_shape=[jax.ShapeDtypeStruct((2, nk, L, FF_BLK), BF16), jax.ShapeDtypeStruct((L, D), F32),
                   jax.ShapeDtypeStruct((L, D), BF16), jax.ShapeDtypeStruct((1, D), F32)],
        scratch_shapes=[pltpu.VMEM((TF, D), F32)],
        compiler_params=_cp("arbitrary", "arbitrary"))(
            d_val, d_val, d_val, d_gate, d_gate, d_gate, conv_w, conv_w, w_up4, w_up4, h1, dh2, g_ffn)


def _matmul_tn(a, b, tm, tn, name, tk=512):
    L, M = a.shape
    N = b.shape[1]

    def body(a_ref, b_ref, o_ref):
        @pl.when(pl.program_id(2) == 0)
        def _():
            o_ref[...] = jnp.zeros_like(o_ref)

        o_ref[...] += _dot_tn(a_ref[...], b_ref[...])

    return pl.pallas_call(
        body, name=name, grid=(M // tm, N // tn, L // tk),
        in_specs=[pl.BlockSpec((tk, tm), lambda m, n, l: (l, m)), pl.BlockSpec((tk, tn), lambda m, n, l: (l, n))],
        out_specs=pl.BlockSpec((tm, tn), lambda m, n, l: (m, n)),
        out_shape=jax.ShapeDtypeStruct((M, N), F32),
        compiler_params=_cp("parallel", "parallel", "arbitrary"))(a, b)


def _matmul_tn_blocks(a, b, tm, name, tk=512):
    L, M = a.shape
    J, _, N = b.shape

    def body(a_ref, b_ref, o_ref):
        @pl.when(pl.program_id(2) == 0)
        def _():
            o_ref[...] = jnp.zeros_like(o_ref)

        o_ref[...] += _dot_tn(a_ref[...], b_ref[...])

    return pl.pallas_call(
        body, name=name, grid=(M // tm, J, L // tk),
        in_specs=[pl.BlockSpec((tk, tm), lambda m, j, l: (l, m)), pl.BlockSpec((None, tk, N), lambda m, j, l: (j, l, 0))],
        out_specs=pl.BlockSpec((None, tm, N), lambda m, j, l: (j, m, 0)),
        out_shape=jax.ShapeDtypeStruct((J, M, N), F32),
        compiler_params=_cp("parallel", "parallel", "arbitrary"))(a, b)


def _row_tile(rows):
    for t in (512, 352, 256, 128, 64, 8):
        if rows % t == 0:
            return t
    return rows


def _add_half(g, r, c_arr, name, out_dtype=F32):
    _, _, R, C = g.shape
    tr = _row_tile(R)

    def body(c_ref, g_ref, r_ref, o_ref):
        o_ref[...] = (g_ref[...] + r_ref[...]).astype(out_dtype)

    return pl.pallas_call(
        body, name=name,
        grid_spec=pltpu.PrefetchScalarGridSpec(
            num_scalar_prefetch=1, grid=(g.shape[0], R // tr),
            in_specs=[pl.BlockSpec((None, None, tr, C), lambda j, i, c: (j, c[0], i, 0)),
                      pl.BlockSpec((None, tr, C), lambda j, i, c: (j, i, 0))],
            out_specs=pl.BlockSpec((None, tr, C), lambda j, i, c: (j, i, 0))),
        out_shape=jax.ShapeDtypeStruct(r.shape, out_dtype),
        compiler_params=_cp("parallel", "parallel"))(c_arr, g, r)


def _add2(a, b, name):
    R, C = a.shape
    tr = _row_tile(R)

    def body(a_ref, b_ref, o_ref):
        o_ref[...] = a_ref[...] + b_ref[...]

    spec = pl.BlockSpec((tr, C), lambda i: (i, 0))
    return pl.pallas_call(body, name=name, grid=(R // tr,), in_specs=[spec, spec], out_specs=spec,
                          out_shape=jax.ShapeDtypeStruct(a.shape, F32), compiler_params=_cp("parallel"))(a, b)


def _sum4(p, name):
    _, R, C = p.shape
    tr = _row_tile(R)

    def body(p_ref, o_ref):
        q = [p_ref[j].astype(F32) for j in range(4)]
        o_ref[...] = ((q[0] + q[1]) + q[2]) + q[3]

    return pl.pallas_call(
        body, name=name, grid=(R // tr,),
        in_specs=[pl.BlockSpec((4, tr, C), lambda i: (0, i, 0))],
        out_specs=pl.BlockSpec((tr, C), lambda i: (i, 0)),
        out_shape=jax.ShapeDtypeStruct((R, C), F32), compiler_params=_cp("parallel"))(p)


def _adamw_refs(w_ref, g_ref, m_ref, v_ref, d_ref, nm_ref, nv_ref):
    gv = g_ref[...]
    nm = ADAM_B1 * m_ref[...] + (1.0 - ADAM_B1) * gv
    nv = ADAM_B2 * v_ref[...] + (1.0 - ADAM_B2) * (gv * gv)
    m_hat = nm / (1.0 - ADAM_B1 ** ADAM_STEP)
    v_hat = nv / (1.0 - ADAM_B2 ** ADAM_STEP)
    d_ref[...] = -ADAM_LR * (m_hat / (jnp.sqrt(v_hat) + ADAM_EPS) + ADAM_WD * w_ref[...])
    nm_ref[...] = nm
    nv_ref[...] = nv


def _adamw_many(ws, gs, ms, vs, name):
    n = len(ws)

    def body(*refs):
        for k in range(n):
            _adamw_refs(*(refs[j * n + k] for j in range(7)))

    out_shape = [jax.ShapeDtypeStruct(w.shape, F32) for w in ws] * 3
    res = pl.pallas_call(body, name=name, out_shape=out_shape,
                         compiler_params=pltpu.CompilerParams(vmem_limit_bytes=VMEM_LIMIT))(*ws, *gs, *ms, *vs)
    return res[:n], res[n:2 * n], res[2 * n:]


def _adamw(w, g, m, v, name):
    R, C = w.shape
    tr = _row_tile(R)
    body = lambda *refs: _adamw_refs(*refs)

    spec = pl.BlockSpec((tr, C), lambda i: (i, 0))
    sh = jax.ShapeDtypeStruct((R, C), F32)
    return pl.pallas_call(body, name=name, grid=(R // tr,), in_specs=[spec] * 4, out_specs=[spec] * 3,
                          out_shape=[sh] * 3, compiler_params=_cp("parallel"))(w, g, m, v)


def _join_rows(own, other, c_arr, name):
    R, C = own.shape
    tr = _row_tile(R)

    def body(c_ref, own_ref, other_ref, o_ref):
        o_ref[...] = jnp.where(pl.program_id(0) == c_ref[0], own_ref[...], other_ref[...])

    half = pl.BlockSpec((tr, C), lambda h, i, c: (i, 0))
    return pl.pallas_call(
        body, name=name,
        grid_spec=pltpu.PrefetchScalarGridSpec(
            num_scalar_prefetch=1, grid=(2, R // tr), in_specs=[half, half],
            out_specs=pl.BlockSpec((tr, C), lambda h, i, c: (h * (R // tr) + i, 0))),
        out_shape=jax.ShapeDtypeStruct((2 * R, C), F32),
        compiler_params=_cp("parallel", "parallel"))(c_arr, own, other)


def _adamw_halves(w, own, other, m, v, c_arr, name):
    R, C = own.shape
    tr = _row_tile(R)

    def body(c_ref, w_ref, own_ref, other_ref, m_ref, v_ref, g_ref, d_ref, nm_ref, nv_ref):
        g_ref[...] = jnp.where(pl.program_id(0) == c_ref[0], own_ref[...], other_ref[...])
        _adamw_refs(w_ref, g_ref, m_ref, v_ref, d_ref, nm_ref, nv_ref)

    half = pl.BlockSpec((tr, C), lambda h, i, c: (i, 0))
    full = pl.BlockSpec((tr, C), lambda h, i, c: (h * (R // tr) + i, 0))
    sh = jax.ShapeDtypeStruct((2 * R, C), F32)
    return pl.pallas_call(
        body, name=name,
        grid_spec=pltpu.PrefetchScalarGridSpec(
            num_scalar_prefetch=1, grid=(2, R // tr), in_specs=[full, half, half, full, full], out_specs=[full] * 4),
        out_shape=[sh] * 4, compiler_params=_cp("parallel", "parallel"))(c_arr, w, own, other, m, v)


_ANY = pl.BlockSpec(memory_space=pl.ANY)


def _position():
    return lax.axis_index("x"), lax.axis_index("y"), lax.axis_index("c")


class _Comm:
    def __init__(self, arrs, out_shape, sems, start, finish):
        self.arrs, self.out_shape, self.sems, self.start, self.finish = arrs, out_shape, sems, start, finish


def _comm_call(comm, name):
    n, m = len(comm.arrs), len(comm.out_shape)

    def body(*refs):
        ins, outs, sems = refs[:n], refs[n:n + m], refs[n + m:]
        comm.start(ins, outs, sems)
        comm.finish(ins, outs, sems)

    return pl.pallas_call(
        body, name=name, in_specs=[_ANY] * n, out_specs=[_ANY] * m, out_shape=comm.out_shape,
        scratch_shapes=comm.sems, compiler_params=pltpu.CompilerParams(has_side_effects=True))(*comm.arrs)


def _hosted_call(body, comm, *, name, grid, in_specs, out_specs, out_shape, scratch_shapes, args):
    sem = ("arbitrary",) * len(grid)
    if comm is None:
        return pl.pallas_call(body, name=name, grid=grid, in_specs=in_specs, out_specs=out_specs, out_shape=out_shape,
                              scratch_shapes=scratch_shapes, compiler_params=_cp(*sem))(*args), []
    n_in, n_out, n_scr = len(in_specs), len(out_specs), len(scratch_shapes)
    ci, co = len(comm.arrs), len(comm.out_shape)

    def full(*refs):
        ins, refs = refs[:n_in], refs[n_in:]
        cins, refs = refs[:ci], refs[ci:]
        outs, refs = refs[:n_out], refs[n_out:]
        couts, refs = refs[:co], refs[co:]
        scr, csems = refs[:n_scr], refs[n_scr:]
        first, last = True, True
        for d, size in enumerate(grid):
            first = first & (pl.program_id(d) == 0)
            last = last & (pl.program_id(d) == size - 1)

        @pl.when(first)
        def _():
            comm.start(cins, couts, csems)

        body(*ins, *outs, *scr)

        @pl.when(last)
        def _():
            comm.finish(cins, couts, csems)

    res = pl.pallas_call(
        full, name=name, grid=grid, in_specs=list(in_specs) + [_ANY] * ci, out_specs=list(out_specs) + [_ANY] * co,
        out_shape=list(out_shape) + list(comm.out_shape), scratch_shapes=list(scratch_shapes) + list(comm.sems),
        compiler_params=_cp(*sem))(*args, *comm.arrs)
    return res[:n_out], res[n_out:]


def _comm_join(*comms):
    def parts(xs, attr):
        out, at = [], 0
        for cm in comms:
            n = len(getattr(cm, attr))
            out.append(xs[at:at + n])
            at += n
        return out

    def start(ins, outs, sems):
        for cm, i, o, s in zip(comms, parts(ins, "arrs"), parts(outs, "out_shape"), parts(sems, "sems")):
            cm.start(i, o, s)

    def finish(ins, outs, sems):
        for cm, i, o, s in zip(comms, parts(ins, "arrs"), parts(outs, "out_shape"), parts(sems, "sems")):
            cm.finish(i, o, s)

    cat = lambda attr: [x for cm in comms for x in getattr(cm, attr)]
    return _Comm(cat("arrs"), cat("out_shape"), cat("sems"), start, finish)


def _dma_sems(*counts):
    return [pltpu.SemaphoreType.DMA((n,)) for n in counts]


def _comm_pair_swap(arrs, half=False):
    n = len(arrs)
    out_shape = [jax.ShapeDtypeStruct(a.shape[:1] + a.shape[2:] if half else a.shape, a.dtype) for a in arrs]

    def copies(ins, outs, sems):
        x, y, c = _position()
        return [pltpu.make_async_remote_copy(
            src_ref=ins[k].at[:, 1 - c] if half else ins[k], dst_ref=outs[k], send_sem=sems[0].at[k],
            recv_sem=sems[1].at[k], device_id=(x, y, 1 - c), device_id_type=MESH) for k in range(n)]

    def start(ins, outs, sems):
        for cp in copies(ins, outs, sems):
            cp.start()

    def finish(ins, outs, sems):
        for cp in copies(ins, outs, sems):
            cp.wait()

    return _Comm(arrs, out_shape, _dma_sems(n, n), start, finish)


def _chip_of(j, c):
    return (jnp.right_shift(j, 1), jnp.bitwise_and(j, 1), c)


def _comm_chip_exchange(arrs, scatter):
    n = len(arrs)
    out_shape = [jax.ShapeDtypeStruct(a.shape if scatter else (4,) + a.shape, a.dtype) for a in arrs]

    def copies(ins, outs, sems):
        x, y, c = _position()
        me = 2 * x + y
        local, sent, landed = [], [], []
        for k in range(n):
            local.append(pltpu.make_async_copy(ins[k].at[me] if scatter else ins[k], outs[k].at[me], sems[2].at[k]))
            for d in (1, 2, 3):
                j = jnp.bitwise_xor(me, d)
                s = 3 * k + d - 1
                src = ins[k].at[j] if scatter else ins[k]
                for dst, group in ((outs[k].at[me], sent), (outs[k].at[j], landed)):
                    group.append(pltpu.make_async_remote_copy(
                        src_ref=src, dst_ref=dst, send_sem=sems[0].at[s], recv_sem=sems[1].at[s],
                        device_id=_chip_of(j, c), device_id_type=MESH))
        return local, sent, landed

    def start(ins, outs, sems):
        local, sent, _ = copies(ins, outs, sems)
        for cp in local + sent:
            cp.start()

    def finish(ins, outs, sems):
        local, sent, landed = copies(ins, outs, sems)
        for cp in sent:
            cp.wait_send()
        for cp in landed:
            cp.wait_recv()
        for cp in local:
            cp.wait()

    return _Comm(arrs, out_shape, _dma_sems(3 * n, 3 * n, n), start, finish)


def _comm_pair_gather(arrs):
    n = len(arrs)
    out_shape = [jax.ShapeDtypeStruct((2,) + a.shape, a.dtype) for a in arrs]

    def copies(ins, outs, sems):
        x, y, c = _position()
        local, sent, landed = [], [], []
        for k in range(n):
            local.append(pltpu.make_async_copy(ins[k], outs[k].at[c], sems[2].at[k]))
            for dst, group in ((outs[k].at[c], sent), (outs[k].at[1 - c], landed)):
                group.append(pltpu.make_async_remote_copy(
                    src_ref=ins[k], dst_ref=dst, send_sem=sems[0].at[k], recv_sem=sems[1].at[k],
                    device_id=(x, y, 1 - c), device_id_type=MESH))
        return local, sent, landed

    def start(ins, outs, sems):
        local, sent, _ = copies(ins, outs, sems)
        for cp in local + sent:
            cp.start()

    def finish(ins, outs, sems):
        local, sent, landed = copies(ins, outs, sems)
        for cp in sent:
            cp.wait_send()
        for cp in landed:
            cp.wait_recv()
        for cp in local:
            cp.wait()

    return _Comm(arrs, out_shape, _dma_sems(n, n, n), start, finish)


def _comm_gather_split(shards, whole):
    n, nw = len(shards), len(whole)
    arrs = list(shards) + list(whole)
    out_shape = [jax.ShapeDtypeStruct((4,) + a.shape, a.dtype) for a in arrs]

    def copies(ins, outs, sems):
        x, y, c = _position()
        me = 2 * x + y
        local, sent, landed, passed, passed_in = [], [], [], [], []
        for k in range(n + nw):
            local.append(pltpu.make_async_copy(ins[k], outs[k].at[me], sems[4].at[k]))
            for d in (1, 2, 3):
                j = jnp.bitwise_xor(me, d)
                s = 3 * k + d - 1
                if k >= n:
                    src, mine, theirs = ins[k], outs[k].at[me], outs[k].at[j]
                else:
                    h = shards[k].shape[0] // 2
                    rows = pl.ds(pl.multiple_of(c * h, 16), h)
                    other = pl.ds(pl.multiple_of((1 - c) * h, 16), h)
                    src, mine, theirs = ins[k].at[rows], outs[k].at[me, rows], outs[k].at[j, rows]
                    for dst, group in ((theirs, passed), (outs[k].at[j, other], passed_in)):
                        group.append(pltpu.make_async_remote_copy(
                            src_ref=theirs, dst_ref=dst, send_sem=sems[2].at[s], recv_sem=sems[3].at[s],
                            device_id=(x, y, 1 - c), device_id_type=MESH))
                for dst, group in ((mine, sent), (theirs, landed)):
                    group.append(pltpu.make_async_remote_copy(
                        src_ref=src, dst_ref=dst, send_sem=sems[0].at[s], recv_sem=sems[1].at[s],
                        device_id=_chip_of(j, c), device_id_type=MESH))
        return local, sent, landed, passed, passed_in

    def start(ins, outs, sems):
        local, sent, _, _, _ = copies(ins, outs, sems)
        for cp in local + sent:
            cp.start()

    def finish(ins, outs, sems):
        local, sent, landed, passed, passed_in = copies(ins, outs, sems)
        for cp in landed[:3 * n]:
            cp.wait_recv()
        for cp in passed:
            cp.start()
        for cp in landed[3 * n:]:
            cp.wait_recv()
        for cp in sent:
            cp.wait_send()
        for cp in passed:
            cp.wait_send()
        for cp in passed_in:
            cp.wait_recv()
        for cp in local:
            cp.wait()

    t = 3 * (n + nw)
    return _Comm(arrs, out_shape, _dma_sems(t, t, max(3 * n, 1), max(3 * n, 1), n + nw), start, finish)


def _pack(arrs, row_multiple):
    parts = []
    for a in arrs:
        flat = a.reshape(-1).astype(F32)
        pad = (-flat.shape[0]) % LANES
        parts.append(jnp.pad(flat, (0, pad)) if pad else flat)
    flat = jnp.concatenate(parts)
    rows = -(-flat.shape[0] // LANES)
    rows_p = -(-rows // row_multiple) * row_multiple
    return jnp.pad(flat, (0, rows_p * LANES - flat.shape[0])).reshape(rows_p, LANES)


def _unpack(packed, shapes):
    flat = packed.reshape(-1)
    outs, off = [], 0
    for sh in shapes:
        size = int(np.prod(sh))
        outs.append(flat[off:off + size].reshape(sh))
        off += size + (-size) % LANES
    return outs


SMALL = ["norm_mix_g", "pool_w", "pool_scale", "ssm_log_neg_a_re", "ssm_a_im", "ssm_log_dt", "ssm_b_re", "ssm_b_im",
         "ssm_c_re", "ssm_c_im", "ssm_d", "glu_b", "out_norm_pool_g", "out_norm_ssm_g", "norm_ffn_g", "conv_b",
         "final_norm_g"]
BIG = ["w_in", "glu_w", "w_out", "w_up", "w_down"]
WEIGHTS = ['norm_mix_g', 'w_in', 'pool_w', 'pool_scale', 'ssm_log_neg_a_re', 'ssm_a_im', 'ssm_log_dt', 'ssm_b_re',
           'ssm_b_im', 'ssm_c_re', 'ssm_c_im', 'ssm_d', 'glu_w', 'glu_b', 'out_norm_pool_g', 'out_norm_ssm_g', 'w_out',
           'norm_ffn_g', 'w_up', 'conv_w', 'conv_b', 'w_down', 'final_norm_g']


def _local_step(x, target, p, full, shards=None, c_arr=None):
    L, D = x.shape
    dist = shards is not None
    row = lambda a: a.reshape(1, -1)
    w_in = full["w_in"]
    pool_w_b = p["pool_w"].astype(BF16)
    g_mix, g_pool, g_ssm, g_ffn, g_fin = (row(p[k]) for k in (
        "norm_mix_g", "out_norm_pool_g", "out_norm_ssm_g", "norm_ffn_g", "final_norm_g"))
    pool_scale, ssm_d, glu_b, conv_b = (row(p[k]) for k in ("pool_scale", "ssm_d", "glu_b", "conv_b"))

    lnar = p["ssm_log_neg_a_re"].reshape(2 * N_SSM_GROUPS, SSM_STATE)
    aim = p["ssm_a_im"].reshape(2 * N_SSM_GROUPS, SSM_STATE)
    ldt = jnp.broadcast_to(p["ssm_log_dt"].reshape(2 * N_SSM_GROUPS, 1), lnar.shape)
    lam_re, lam_im, f_re, f_im = _ssm_params(lnar, aim, ldt)
    flat2 = lambda a: a.reshape(2, N_STATE)
    lam = [jnp.stack([flat2(lam_re)[d], flat2(lam_im)[d]]).reshape(2, LT, LANES) for d in range(2)]
    f2 = [jnp.stack([flat2(f_re)[d], flat2(f_im)[d]]) for d in range(2)]
    dense = _ssm_expand(p["ssm_b_re"], p["ssm_b_im"], p["ssm_c_re"], p["ssm_c_im"])
    ssm_args = [tuple(dense[4 * d:4 * d + 4]) + (f2[d], lam[d]) for d in range(2)]

    u, xn = _in_proj(x, g_mix, w_in)
    yn_pool = _pool_fwd(u, pool_w_b, pool_scale, g_pool)
    gather1 = _comm_gather_split([shards[k] for k in ("glu_w", "w_out", "w_down")], [shards["conv_w"]]) if dist else None
    (y0, s0r, s0i), got1 = _ssm_scan_fwd(u, *ssm_args[0], False, comm=gather1)
    gather2 = _comm_gather_split([shards["w_up"]], []) if dist else None
    (y1, s1r, s1i), got2 = _ssm_scan_fwd(u, *ssm_args[1], True, comm=gather2)
    if dist:
        glu_w, w_out, w_down = (g.reshape((-1,) + g.shape[2:]) for g in got1[:3])
        conv_w = jnp.transpose(got1[3], (1, 0, 2)).reshape(3, -1)
        w_up4 = got2[0]
    else:
        glu_w, w_out, w_up4, w_down, conv_w = (full[k] for k in ("glu_w", "w_out", "w_up", "w_down", "conv_w"))
    h1, hn, ycat = _mix_out(yn_pool, y0, y1, u, x, ssm_d, glu_w, glu_b, g_ssm, w_out, g_ffn)
    up = _ffn_up(hn, w_up4)
    a, dh2, dh2_b, loss, g_final = _ffn_down_loss(up, conv_w, conv_b, w_down, h1, target, g_fin)

    d_val, d_gate, gcv, gcg = _ffn_act_bwd(up, conv_w, conv_b, w_down, dh2_b)
    g_w_down = _matmul_tn(a, dh2_b, FF_BLK, D, "grad_w_down")
    d_up, dh1, dh1_b, g_ffn_g = _ffn_up_bwd(d_val, d_gate, conv_w, w_up4, h1, dh2, g_ffn)
    g_w_up = _matmul_tn_blocks(hn, d_up.reshape(4, L, FF_BLK), 512, "grad_w_up")
    g_w_out = _matmul_tn(ycat, dh1_b, 512, D, "grad_w_out")
    late = ("w_up", "w_down")
    halves = [g_w_up.reshape(4, 2, D // 2, FF_BLK), g_w_down.reshape(4, 2, D_FF // 8, D)]
    (d_pooled, g_pool_w, g_pool_scale, g_pool_g), from_sibling = _pool_bwd_local(
        dh1_b, u, w_out, pool_w_b, pool_scale, g_pool, comm=_comm_pair_swap(halves, half=True) if dist else None)
    du_pool = _pool_bwd_window(d_pooled)
    dy, du_direct, g_glu_w, g_glu_b, g_ssm_d, g_ssm_g = _ssm_bwd_local(dh1_b, y0, y1, u, ssm_d, glu_w, glu_b, g_ssm, w_out)
    reduce2 = None
    if dist:
        chip_sums = [_add_half(h, r, c_arr, "sum_pair_" + k, BF16) for k, h, r in zip(late, halves, from_sibling)]
        reduce2 = _comm_chip_exchange(chip_sums, scatter=True)
    (du0, gb0r, gb0i, gc0r, gc0i, gf0, gl0), from_chips = _ssm_scan_bwd(
        dy, u, s0r, s0i, *ssm_args[0], True, comm=reduce2)
    reduce3 = _comm_pair_gather([_sum4(r, "sum_chips_" + k) for k, r in zip(late, from_chips)]) if dist else None
    (du1, gb1r, gb1i, gc1r, gc1i, gf1, gl1), shards_out = _ssm_scan_bwd(
        dy, u, s1r, s1i, *ssm_args[1], False, comm=reduce3)
    per_state = lambda a, b, j: jnp.stack([a[j], b[j]]).reshape(2 * N_SSM_GROUPS, SSM_STATE)
    g_lnar, g_aim, g_ldt = _ssm_params_bwd(lnar, aim, ldt, per_state(gl0, gl1, 0), per_state(gl0, gl1, 1),
                                           per_state(gf0, gf1, 0), per_state(gf0, gf1, 1))
    grad_x, d_u_b, g_mix_g = _in_bwd(du_pool, du_direct, du0, du1, dh1, x, g_mix, w_in)
    g_w_in = _matmul_tn(xn, d_u_b, 512, D, "grad_w_in")

    small = {
        "norm_mix_g": g_mix_g, "pool_w": g_pool_w, "pool_scale": g_pool_scale,
        "ssm_log_neg_a_re": g_lnar, "ssm_a_im": g_aim, "ssm_log_dt": g_ldt,
        "ssm_b_re": jnp.stack([gb0r, gb1r]), "ssm_b_im": jnp.stack([gb0i, gb1i]),
        "ssm_c_re": jnp.stack([gc0r, gc1r]), "ssm_c_im": jnp.stack([gc0i, gc1i]),
        "ssm_d": g_ssm_d, "glu_b": g_glu_b, "out_norm_pool_g": g_pool_g, "out_norm_ssm_g": g_ssm_g,
        "norm_ffn_g": g_ffn_g, "conv_b": jnp.concatenate([gcv[3], gcg[3]]), "final_norm_g": g_final,
        "conv_w": jnp.concatenate([gcv[0:3], gcg[0:3]], axis=1),
    }
    big = {"w_in": g_w_in, "glu_w": g_glu_w, "w_out": g_w_out}
    reduced = dict(zip(late, shards_out))
    if not dist:
        big.update({"w_up": g_w_up, "w_down": g_w_down})
    return loss, grad_x, small, big, reduced


def kernel(x, norm_mix_g, w_in, pool_w, pool_scale, ssm_log_neg_a_re, ssm_a_im, ssm_log_dt, ssm_b_re, ssm_b_im, ssm_c_re, ssm_c_im, ssm_d, glu_w, glu_b, out_norm_pool_g, out_norm_ssm_g, w_out, norm_ffn_g, w_up, conv_w, conv_b, w_down, final_norm_g, loss_target, m_norm_mix_g, m_w_in, m_pool_w, m_pool_scale, m_ssm_log_neg_a_re, m_ssm_a_im, m_ssm_log_dt, m_ssm_b_re, m_ssm_b_im, m_ssm_c_re, m_ssm_c_im, m_ssm_d, m_glu_w, m_glu_b, m_out_norm_pool_g, m_out_norm_ssm_g, m_w_out, m_norm_ffn_g, m_w_up, m_conv_w, m_conv_b, m_w_down, m_final_norm_g, v_norm_mix_g, v_w_in, v_pool_w, v_pool_scale, v_ssm_log_neg_a_re, v_ssm_a_im, v_ssm_log_dt, v_ssm_b_re, v_ssm_b_im, v_ssm_c_re, v_ssm_c_im, v_ssm_d, v_glu_w, v_glu_b, v_out_norm_pool_g, v_out_norm_ssm_g, v_w_out, v_norm_ffn_g, v_w_up, v_conv_w, v_conv_b, v_w_down, v_final_norm_g):
    args = locals()
    w = {k: args[k] for k in WEIGHTS}
    m = {k: args["m_" + k] for k in WEIGHTS}
    v = {k: args["v_" + k] for k in WEIGHTS}
    chip = 2 * lax.axis_index("x") + lax.axis_index("y")
    c_arr = lax.axis_index("c").astype(jnp.int32).reshape(1)

    shards = {k: w[k].astype(BF16) for k in BIG}
    shards["conv_w"] = conv_w
    w_in_full = _comm_call(_comm_chip_exchange([shards["w_in"]], scatter=False), "gather_w_in")[0]
    loss, grad_x, g_small, g_big, reduced = _local_step(
        x[0], loss_target[0], w, {"w_in": w_in_full.reshape(-1, w_in_full.shape[-1])}, shards, c_arr)

    tail = ("w_in", "glu_w", "w_out")
    packed = _pack([loss] + [g_small[k] for k in SMALL] + [g_small["conv_w"]], 1024)
    halves = [g_big[k].reshape(4, 2, g_big[k].shape[0] // 8, g_big[k].shape[1]) for k in tail]
    halves.append(packed.reshape(1, 2, packed.shape[0] // 2, LANES))
    from_sibling = _comm_call(_comm_pair_swap(halves, half=True), "reduce_pair")
    names = tail + ("small",)
    sums = [_add_half(h, r, c_arr, "sum_pair_" + k, F32 if k == "small" else BF16)
            for k, h, r in zip(names, halves, from_sibling)]
    from_chips = _comm_call(_comm_join(_comm_chip_exchange(sums[:3], scatter=True),
                                       _comm_chip_exchange([sums[3][0]], scatter=False)), "reduce_chips")
    mine = [_sum4(r, "sum_chips_" + k) for k, r in zip(names, from_chips)]
    theirs = _comm_call(_comm_pair_swap(mine), "swap_halves")
    grads = {k: s.reshape(w[k].shape) for k, s in reduced.items()}
    shapes = [loss.shape] + [w[k].shape for k in SMALL] + [(3, 4 * FF_BLK)]
    small_all = _join_rows(mine[3], theirs[3], c_arr, "join_small")
    for k, g in zip(["loss"] + SMALL + ["conv_w_full"], _unpack(small_all, shapes)):
        grads[k] = g
    loss = grads.pop("loss")[0, 0]
    grads["conv_w"] = lax.dynamic_slice_in_dim(grads.pop("conv_w_full"), chip * FF_BLK, FF_BLK, axis=1)

    delta, new_m, new_v = {}, {}, {}
    for k in reduced:
        delta[k], new_m[k], new_v[k] = _adamw(w[k], grads[k], m[k], v[k], "adamw_" + k)
    for k, own, other in zip(tail, mine, theirs):
        grads[k], delta[k], new_m[k], new_v[k] = _adamw_halves(w[k], own, other, m[k], v[k], c_arr, "adamw_" + k)
    wide = ["ssm_b_re", "ssm_b_im"]
    for keys, name in ((wide, "adamw_ssm_b"), ([k for k in SMALL + ["conv_w"] if k not in wide], "adamw_small")):
        outs = _adamw_many(*([d[k] for k in keys] for d in (w, grads, m, v)), name)
        for d, o in zip((delta, new_m, new_v), outs):
            d.update(zip(keys, o))

    return (loss, grad_x[None], *[grads[k] for k in WEIGHTS], *[delta[k] for k in WEIGHTS],
            *[new_m[k] for k in WEIGHTS], *[new_v[k] for k in WEIGHTS])
```

```python
import numpy as np
import jax
import jax.numpy as jnp
from jax import lax
from jax.experimental import pallas as pl
from jax.experimental.pallas import tpu as pltpu

F32 = jnp.float32
BF16 = jnp.bfloat16
MESH = pl.DeviceIdType.MESH

EPS = 1e-6
POOL_WINDOWS = (2, 4, 8, 16)
POOL_GROUP = 128
SSM_GROUP = 16
SSM_STATE = 64
N_SSM_GROUPS = 32
N_STATE = N_SSM_GROUPS * SSM_STATE
QUAD = 256
N_QUAD = N_STATE // QUAD
SLAB = 256
D_SSM = 512
D_POOL = 512
D_FF = 2816
FF_BLK = 1408
HALO = 8
HALO_B = 16
LANES = 128
ADAM_LR, ADAM_B1, ADAM_B2, ADAM_EPS, ADAM_WD, ADAM_STEP = 0.001, 0.9, 0.999, 1e-08, 0.01, 10
VMEM_LIMIT = 56 * 2 ** 20

TL = 512
TF = 256
TC = 256
SCAN_W = 512


def _cp(*sem):
    return pltpu.CompilerParams(dimension_semantics=sem, vmem_limit_bytes=VMEM_LIMIT)


def _dot_nn(a, b):
    return jnp.dot(a, b, preferred_element_type=F32)


def _dot_nt(a, b):
    return lax.dot_general(a, b, (((1,), (1,)), ((), ())), preferred_element_type=F32)


def _dot_tn(a, b):
    return lax.dot_general(a, b, (((0,), (0,)), ((), ())), preferred_element_type=F32)


def _rms_fwd(x, g):
    inv = lax.rsqrt(jnp.mean(x * x, axis=-1, keepdims=True) + EPS)
    xh = x * inv
    return xh * g, xh, inv


def _rms_bwd(dy, xh, inv, g):
    dg = jnp.sum(dy * xh, axis=0, keepdims=True)
    dxh = dy * g
    dx = inv * (dxh - xh * jnp.mean(dxh * xh, axis=-1, keepdims=True))
    return dx, dg


_GELU_C = 0.7978845608028654
_GELU_A = 0.044715


def _gelu(y):
    t = jnp.tanh(_GELU_C * (y + _GELU_A * (y * y * y)))
    return 0.5 * y * (1.0 + t), t


def _gelu_grad(y, t):
    return 0.5 * (1.0 + t) + 0.5 * y * (1.0 - t * t) * (_GELU_C * (1.0 + 3.0 * _GELU_A * y * y))


def _sigmoid(x):
    return 1.0 / (1.0 + jnp.exp(-x))


def _full(shape):
    n = len(shape)
    return pl.BlockSpec(shape, lambda *_: (0,) * n)


def _fill_ext(ext_ref, prev_ref, cur_ref, next_ref, i, n, rows):
    ext_ref[0:HALO, :] = jnp.where(i > 0, prev_ref[...], 0.0).astype(ext_ref.dtype)
    ext_ref[HALO:HALO + rows, :] = cur_ref[...]
    ext_ref[HALO + rows:2 * HALO + rows, :] = jnp.where(i < n - 1, next_ref[...], 0.0).astype(ext_ref.dtype)


def _in_proj(x, g, w):
    L, D = x.shape
    E = w.shape[1]

    def body(x_ref, g_ref, w_ref, u_ref, xn_ref):
        y, _, _ = _rms_fwd(x_ref[...], g_ref[...])
        yb = y.astype(BF16)
        xn_ref[...] = yb
        u_ref[...] = _dot_nn(yb, w_ref[...])

    return pl.pallas_call(
        body, name="in_proj", grid=(L // TL,),
        in_specs=[pl.BlockSpec((TL, D), lambda i: (i, 0)), _full((1, D)), _full(w.shape)],
        out_specs=[pl.BlockSpec((TL, E), lambda i: (i, 0)), pl.BlockSpec((TL, D), lambda i: (i, 0))],
        out_shape=[jax.ShapeDtypeStruct((L, E), F32), jax.ShapeDtypeStruct((L, D), BF16)],
        compiler_params=_cp("parallel"))(x, g, w)


def _halo_specs_1d(rows, width, L, col):
    rb = rows // HALO
    last = L // HALO - 1
    return [pl.BlockSpec((HALO, width), lambda i: (jnp.maximum(i * rb - 1, 0), col)),
            pl.BlockSpec((rows, width), lambda i: (i, col)),
            pl.BlockSpec((HALO, width), lambda i: (jnp.minimum((i + 1) * rb, last), col))]


def _pooled_from_ext(ext_ref, t0, rows, L):
    t = t0 + lax.broadcasted_iota(jnp.int32, (rows, 1), 0)
    outs = []
    for gi, w in enumerate(POOL_WINDOWS):
        half = w // 2
        cs = slice(gi * POOL_GROUP, (gi + 1) * POOL_GROUP)
        acc = ext_ref[pl.ds(HALO - half, rows), cs]
        for s in range(-half + 1, half):
            acc = acc + ext_ref[pl.ds(HALO + s, rows), cs]
        cnt = (jnp.minimum(t + half, L) - jnp.maximum(t - half, 0)).astype(F32)
        outs.append(acc / cnt - ext_ref[pl.ds(HALO, rows), cs])
    return outs


def _pool_fwd(u, pool_w_b, pool_scale, g_pool):
    L = u.shape[0]
    n = L // TL

    def body(prev_ref, cur_ref, next_ref, pw_ref, ps_ref, g_ref, out_ref, ext_ref):
        i = pl.program_id(0)
        _fill_ext(ext_ref, prev_ref, cur_ref, next_ref, i, n, TL)
        pooled = _pooled_from_ext(ext_ref, i * TL, TL, L)
        ypre = jnp.concatenate([_dot_nn(pooled[gi].astype(BF16), pw_ref[gi]) for gi in range(4)], axis=-1)
        yn, _, _ = _rms_fwd(ypre * ps_ref[...], g_ref[...])
        out_ref[...] = yn.astype(BF16)

    return pl.pallas_call(
        body, name="pool_fwd", grid=(n,),
        in_specs=_halo_specs_1d(TL, D_POOL, L, 0) + [_full(pool_w_b.shape), _full((1, D_POOL)), _full((1, D_POOL))],
        out_specs=pl.BlockSpec((TL, D_POOL), lambda i: (i, 0)),
        out_shape=jax.ShapeDtypeStruct((L, D_POOL), BF16),
        scratch_shapes=[pltpu.VMEM((TL + 2 * HALO, D_POOL), F32)],
        compiler_params=_cp("parallel"))(u, u, u, pool_w_b, pool_scale, g_pool)


def _pool_bwd_local(dh1, u, w_out_b, pool_w_b, pool_scale, g_pool, comm=None):
    L = u.shape[0]
    n = L // TL
    D = dh1.shape[1]

    def body(dh_ref, prev_ref, cur_ref, next_ref, wo_ref, pw_ref, ps_ref, g_ref,
             dp_ref, gpw_ref, gps_ref, gg_ref, ext_ref):
        i = pl.program_id(0)

        @pl.when(i == 0)
        def _():
            gpw_ref[...] = jnp.zeros_like(gpw_ref)
            gps_ref[...] = jnp.zeros_like(gps_ref)
            gg_ref[...] = jnp.zeros_like(gg_ref)

        _fill_ext(ext_ref, prev_ref, cur_ref, next_ref, i, n, TL)
        pooled = [p.astype(BF16) for p in _pooled_from_ext(ext_ref, i * TL, TL, L)]
        ypre = jnp.concatenate([_dot_nn(pooled[gi], pw_ref[gi]) for gi in range(4)], axis=-1)
        ps = ps_ref[...]
        g = g_ref[...]
        _, xh, inv = _rms_fwd(ypre * ps, g)
        d_yn = _dot_nt(dh_ref[...].astype(BF16), wo_ref[...])
        d_y, dg = _rms_bwd(d_yn, xh, inv, g)
        gg_ref[...] += dg
        gps_ref[...] += jnp.sum(d_y * ypre, axis=0, keepdims=True)
        d_ypre = (d_y * ps).astype(BF16)
        for gi in range(4):
            cs = slice(gi * POOL_GROUP, (gi + 1) * POOL_GROUP)
            dp_ref[:, cs] = _dot_nt(d_ypre[:, cs], pw_ref[gi])
            gpw_ref[gi] += _dot_tn(pooled[gi], d_ypre[:, cs])

    return _hosted_call(
        body, comm, name="pool_bwd_local", grid=(n,),
        in_specs=[pl.BlockSpec((TL, D), lambda i: (i, 0))] + _halo_specs_1d(TL, D_POOL, L, 0)
        + [pl.BlockSpec((D_POOL, D), lambda i: (0, 0)), _full(pool_w_b.shape), _full((1, D_POOL)), _full((1, D_POOL))],
        out_specs=[pl.BlockSpec((TL, D_POOL), lambda i: (i, 0)), _full(pool_w_b.shape),
                   _full((1, D_POOL)), _full((1, D_POOL))],
        out_shape=[jax.ShapeDtypeStruct((L, D_POOL), F32), jax.ShapeDtypeStruct(pool_w_b.shape, F32),
                   jax.ShapeDtypeStruct((1, D_POOL), F32), jax.ShapeDtypeStruct((1, D_POOL), F32)],
        scratch_shapes=[pltpu.VMEM((TL + 2 * HALO, D_POOL), F32)],
        args=(dh1, u, u, u, w_out_b, pool_w_b, pool_scale, g_pool))


def _pool_bwd_window(d_pooled):
    L = d_pooled.shape[0]
    n = L // TL
    R = TL + 2 * HALO

    def body(prev_ref, cur_ref, next_ref, out_ref, ext_ref, q_ref):
        i = pl.program_id(0)
        _fill_ext(ext_ref, prev_ref, cur_ref, next_ref, i, n, TL)
        tr = i * TL - HALO + lax.broadcasted_iota(jnp.int32, (R, 1), 0)
        for gi, w in enumerate(POOL_WINDOWS):
            half = w // 2
            cs = slice(gi * POOL_GROUP, (gi + 1) * POOL_GROUP)
            cnt = jnp.maximum(jnp.minimum(tr + half, L) - jnp.maximum(tr - half, 0), 1).astype(F32)
            q_ref[:, cs] = ext_ref[:, cs] / cnt
        for gi, w in enumerate(POOL_WINDOWS):
            half = w // 2
            cs = slice(gi * POOL_GROUP, (gi + 1) * POOL_GROUP)
            acc = q_ref[pl.ds(HALO - half + 1, TL), cs]
            for s in range(-half + 2, half + 1):
                acc = acc + q_ref[pl.ds(HALO + s, TL), cs]
            out_ref[:, cs] = acc - ext_ref[pl.ds(HALO, TL), cs]

    return pl.pallas_call(
        body, name="pool_bwd_window", grid=(n,),
        in_specs=_halo_specs_1d(TL, D_POOL, L, 0),
        out_specs=pl.BlockSpec((TL, D_POOL), lambda i: (i, 0)),
        out_shape=jax.ShapeDtypeStruct((L, D_POOL), F32),
        scratch_shapes=[pltpu.VMEM((R, D_POOL), F32), pltpu.VMEM((R, D_POOL), F32)],
        compiler_params=_cp("parallel"))(d_pooled, d_pooled, d_pooled)


def _ssm_param_fn(lnar, aim, ldt):
    dt = jnp.exp(ldt)
    a_re = -jnp.exp(lnar)
    mag = jnp.exp(a_re * dt)
    ang = aim * dt
    lr, li = mag * jnp.cos(ang), mag * jnp.sin(ang)
    den = a_re * a_re + aim * aim
    fr = ((lr - 1.0) * a_re + li * aim) / den
    fi = (li * a_re - (lr - 1.0) * aim) / den
    return lr, li, fr, fi


def _ssm_params(lnar, aim, ldt):
    def body(a_ref, b_ref, c_ref, lr_ref, li_ref, fr_ref, fi_ref):
        lr, li, fr, fi = _ssm_param_fn(a_ref[...], b_ref[...], c_ref[...])
        lr_ref[...] = lr
        li_ref[...] = li
        fr_ref[...] = fr
        fi_ref[...] = fi

    sh = jax.ShapeDtypeStruct(lnar.shape, F32)
    return pl.pallas_call(body, name="ssm_params", out_shape=[sh] * 4)(lnar, aim, ldt)


def _ssm_params_bwd(lnar, aim, ldt, glr, gli, gfr, gfi):
    def body(a_ref, b_ref, c_ref, g0, g1, g2, g3, da_ref, db_ref, dc_ref):
        _, vjp = jax.vjp(_ssm_param_fn, a_ref[...], b_ref[...], c_ref[...])
        da, db, dc = vjp((g0[...], g1[...], g2[...], g3[...]))
        da_ref[...] = da
        db_ref[...] = db
        dc_ref[...] = jnp.sum(dc, axis=1, keepdims=True)

    return pl.pallas_call(
        body, name="ssm_params_bwd",
        out_shape=[jax.ShapeDtypeStruct(lnar.shape, F32), jax.ShapeDtypeStruct(aim.shape, F32),
                   jax.ShapeDtypeStruct((ldt.shape[0], 1), F32)])(lnar, aim, ldt, glr, gli, gfr, gfi)


def _scan_tables(lam4):
    def build(lr, li, reverse, out_ref, k):
        row = lax.broadcasted_iota(jnp.int32, (8, N_STATE), 0)
        lrb = jnp.broadcast_to(lr, (8, N_STATE))
        lib = jnp.broadcast_to(li, (8, N_STATE))
        pr, pi = lrb, lib
        for s, sh in enumerate((1, 2, 4)):
            mask = (row < 8 - sh) if reverse else (row >= sh)
            out_ref[k, 2 * s] = jnp.where(mask, pr, 0.0)
            out_ref[k, 2 * s + 1] = jnp.where(mask, pi, 0.0)
            pr, pi = pr * pr - pi * pi, 2.0 * pr * pi
        pr, pi = lrb, lib
        p8r = jnp.zeros((8, N_STATE), F32)
        p8i = jnp.zeros((8, N_STATE), F32)
        for j in range(8):
            r = 7 - j if reverse else j
            p8r = jnp.where(row == r, pr, p8r)
            p8i = jnp.where(row == r, pi, p8i)
            pr, pi = pr * lrb - pi * lib, pr * lib + pi * lrb
        out_ref[k, 6] = p8r
        out_ref[k, 7] = p8i

    def body(lam_ref, out_ref):
        l0r, l0i, l1r, l1i = (lam_ref[j:j + 1, :] for j in range(4))
        build(l0r, l0i, False, out_ref, 0)
        build(l0r, -l0i, True, out_ref, 1)
        build(l1r, l1i, True, out_ref, 2)
        build(l1r, -l1i, False, out_ref, 3)

    return pl.pallas_call(body, name="scan_tables",
                          out_shape=jax.ShapeDtypeStruct((4, 8, 8, N_STATE), F32))(lam4)


def _b_block(g):
    q, gl = divmod(g, 4)
    r0, c0 = gl * SSM_STATE, (q % 4) * 4 * SSM_GROUP + gl * SSM_GROUP
    return q, slice(r0, r0 + SSM_STATE), slice(c0, c0 + SSM_GROUP)


def _c_block(g):
    q, rows, cols = _b_block(g)
    return q, cols, rows


def _ssm_expand(b_re, b_im, c_re, c_im):
    def body(bre_ref, bim_ref, cre_ref, cim_ref, *rest):
        outs, tmp = rest[:8], rest[8]
        for d in range(2):
            for j, (src, where) in enumerate(((bre_ref, _b_block), (bim_ref, _b_block),
                                              (cre_ref, _c_block), (cim_ref, _c_block))):
                tmp[...] = jnp.zeros_like(tmp)
                for g in range(N_SSM_GROUPS):
                    q, rows, cols = where(g)
                    tmp[q, rows, cols] = src[d, g]
                outs[4 * d + j][...] = tmp[...].astype(BF16)

    dense = jax.ShapeDtypeStruct((N_QUAD, QUAD, SLAB), BF16)
    return pl.pallas_call(body, name="ssm_expand", out_shape=[dense] * 8,
                          scratch_shapes=[pltpu.VMEM((N_QUAD, QUAD, SLAB), F32)],
                          compiler_params=pltpu.CompilerParams(vmem_limit_bytes=VMEM_LIMIT))(b_re, b_im, c_re, c_im)


def _scan_rows(src_re, src_im, dst_re, dst_im, tab_ref, k, carry_re, carry_im, rows, reverse, s_refs=None):
    ng = rows // 8
    edge = 0 if reverse else 7
    row_id = lax.broadcasted_iota(jnp.int32, (8, SCAN_W), 0)
    sums = []
    for lt in range(N_STATE // SCAN_W):
        sl = slice(lt * SCAN_W, (lt + 1) * SCAN_W)

        def step(r, c, sl=sl):
            tabs = [tab_ref[k, j, :, sl] for j in range(8)]
            cr, ci = c[0], c[1]
            row = pl.multiple_of((ng - 1 - r) * 8 if reverse else r * 8, 8)
            xr = src_re[pl.ds(row, 8), sl]
            xi = src_im[pl.ds(row, 8), sl]
            for s, sh in enumerate((1, 2, 4)):
                amt = 8 - sh if reverse else sh
                rr = pltpu.roll(xr, amt, 0)
                ri = pltpu.roll(xi, amt, 0)
                mr, mi = tabs[2 * s], tabs[2 * s + 1]
                xr, xi = xr + mr * rr - mi * ri, xi + mr * ri + mi * rr
            xr, xi = xr + tabs[6] * cr - tabs[7] * ci, xi + tabs[6] * ci + tabs[7] * cr
            dst_re[pl.ds(row, 8), sl] = xr
            dst_im[pl.ds(row, 8), sl] = xi
            ncr = jnp.broadcast_to(xr[edge:edge + 1, :], (8, SCAN_W))
            nci = jnp.broadcast_to(xi[edge:edge + 1, :], (8, SCAN_W))
            if s_refs is None:
                return ncr, nci
            amt = 7 if reverse else 1
            far = 7 if reverse else 0
            nr = jnp.where(row_id == far, cr, pltpu.roll(xr, amt, 0))
            ni = jnp.where(row_id == far, ci, pltpu.roll(xi, amt, 0))
            sr = s_refs[0][pl.ds(row, 8), sl]
            si = s_refs[1][pl.ds(row, 8), sl]
            return ncr, nci, c[2] + nr * sr + ni * si, c[3] + ni * sr - nr * si

        init = (carry_re[:, sl], carry_im[:, sl])
        if s_refs is not None:
            init = init + (jnp.zeros((8, SCAN_W), F32), jnp.zeros((8, SCAN_W), F32))
        out = lax.fori_loop(0, ng, step, init)
        carry_re[:, sl] = out[0]
        carry_im[:, sl] = out[1]
        if s_refs is not None:
            sums.append((jnp.sum(out[2], axis=0, keepdims=True), jnp.sum(out[3], axis=0, keepdims=True)))
    return sums


def _ssm_scan_fwd(u, b_re, b_im, c_re, c_im, f2, tables, k, reverse, comm=None):
    L = u.shape[0]
    nc = L // TC
    chunk = (lambda i: nc - 1 - i) if reverse else (lambda i: i)

    def body(u_ref, bre_ref, bim_ref, cre_ref, cim_ref, f_ref, tab_ref,
             y_ref, sre_ref, sim_ref, in_re, in_im, carry_re, carry_im):
        @pl.when(pl.program_id(0) == 0)
        def _():
            carry_re[...] = jnp.zeros_like(carry_re)
            carry_im[...] = jnp.zeros_like(carry_im)

        ub = u_ref[...].astype(BF16)
        for q in range(N_QUAD):
            qs = slice(q * QUAD, (q + 1) * QUAD)
            us = ub[:, (q // 4) * SLAB:(q // 4 + 1) * SLAB]
            bur = _dot_nt(us, bre_ref[q])
            bui = _dot_nt(us, bim_ref[q])
            fr = f_ref[0:1, qs]
            fi = f_ref[1:2, qs]
            in_re[:, qs] = fr * bur - fi * bui
            in_im[:, qs] = fr * bui + fi * bur
        _scan_rows(in_re, in_im, sre_ref, sim_ref, tab_ref, k, carry_re, carry_im, TC, reverse)
        for j in range(D_SSM // SLAB):
            acc = jnp.zeros((TC, SLAB), F32)
            for q in range(4 * j, 4 * j + 4):
                qs = slice(q * QUAD, (q + 1) * QUAD)
                acc = acc + _dot_nt(sre_ref[:, qs].astype(BF16), cre_ref[q])
                acc = acc - _dot_nt(sim_ref[:, qs].astype(BF16), cim_ref[q])
            y_ref[:, j * SLAB:(j + 1) * SLAB] = acc

    return _hosted_call(
        body, comm, name="ssm_scan_rev" if reverse else "ssm_scan_fwd", grid=(nc,),
        in_specs=[pl.BlockSpec((TC, D_SSM), lambda i: (chunk(i), 1))]
        + [_full(b_re.shape)] * 4 + [_full(f2.shape), _full(tables.shape)],
        out_specs=[pl.BlockSpec((TC, D_SSM), lambda i: (chunk(i), 0)),
                   pl.BlockSpec((TC, N_STATE), lambda i: (chunk(i), 0)),
                   pl.BlockSpec((TC, N_STATE), lambda i: (chunk(i), 0))],
        out_shape=[jax.ShapeDtypeStruct((L, D_SSM), F32), jax.ShapeDtypeStruct((L, N_STATE), F32),
                   jax.ShapeDtypeStruct((L, N_STATE), F32)],
        scratch_shapes=[pltpu.VMEM((TC, N_STATE), F32), pltpu.VMEM((TC, N_STATE), F32),
                        pltpu.VMEM((8, N_STATE), F32), pltpu.VMEM((8, N_STATE), F32)],
        args=(u, b_re, b_im, c_re, c_im, f2, tables))


def _quad_channels(q):
    c0 = (q // 4) * SLAB + (q % 4) * 4 * SSM_GROUP
    return slice(c0, c0 + 4 * SSM_GROUP)


def _ssm_scan_bwd(dy, u, s_re, s_im, b_re, b_im, c_re, c_im, f2, tables, k, reverse, comm=None):
    L = u.shape[0]
    nc = L // TC
    chunk = (lambda i: nc - 1 - i) if reverse else (lambda i: i)

    def body(dy_ref, u_ref, sre_ref, sim_ref, bre_ref, bim_ref, cre_ref, cim_ref, f_ref, tab_ref,
             du_ref, ob_re, ob_im, oc_re, oc_im, gv_ref,
             a_re, a_im, carry_re, carry_im, gbr_ref, gbi_ref, gcr_ref, gci_ref):
        @pl.when(pl.program_id(0) == 0)
        def _():
            carry_re[...] = jnp.zeros_like(carry_re)
            carry_im[...] = jnp.zeros_like(carry_im)
            for r in (gbr_ref, gbi_ref, gcr_ref, gci_ref, gv_ref):
                r[...] = jnp.zeros_like(r)

        dyb = dy_ref[...].astype(BF16)
        ub = u_ref[...].astype(BF16)
        for q in range(N_QUAD):
            qs = slice(q * QUAD, (q + 1) * QUAD)
            ds = dyb[:, (q // 4) * SLAB:(q // 4 + 1) * SLAB]
            a_re[:, qs] = _dot_nn(ds, cre_ref[q])
            a_im[:, qs] = -_dot_nn(ds, cim_ref[q])
            dq = dyb[:, _quad_channels(q)]
            gcr_ref[q] += _dot_tn(dq, sre_ref[:, qs].astype(BF16))
            gci_ref[q] -= _dot_tn(dq, sim_ref[:, qs].astype(BF16))
        sums = _scan_rows(a_re, a_im, a_re, a_im, tab_ref, k, carry_re, carry_im, TC, reverse,
                          s_refs=(sre_ref, sim_ref))
        for lt, (glr, gli) in enumerate(sums):
            sl = slice(lt * SCAN_W, (lt + 1) * SCAN_W)
            gv_ref[0:1, sl] += glr
            gv_ref[1:2, sl] += gli
        for j in range(D_SSM // SLAB):
            us = ub[:, j * SLAB:(j + 1) * SLAB]
            acc = jnp.zeros((TC, SLAB), F32)
            for q in range(4 * j, 4 * j + 4):
                qs = slice(q * QUAD, (q + 1) * QUAD)
                ar = a_re[:, qs]
                ai = a_im[:, qs]
                bur = _dot_nt(us, bre_ref[q])
                bui = _dot_nt(us, bim_ref[q])
                gv_ref[2:3, qs] += jnp.sum(ar * bur + ai * bui, axis=0, keepdims=True)
                gv_ref[3:4, qs] += jnp.sum(ai * bur - ar * bui, axis=0, keepdims=True)
                fr = f_ref[0:1, qs]
                fi = f_ref[1:2, qs]
                dbr = (fr * ar + fi * ai).astype(BF16)
                dbi = (fr * ai - fi * ar).astype(BF16)
                uq = ub[:, _quad_channels(q)]
                gbr_ref[q] += _dot_tn(uq, dbr)
                gbi_ref[q] += _dot_tn(uq, dbi)
                acc = acc + _dot_nn(dbr, bre_ref[q]) + _dot_nn(dbi, bim_ref[q])
            du_ref[:, j * SLAB:(j + 1) * SLAB] = acc

        @pl.when(pl.program_id(0) == nc - 1)
        def _():
            for g in range(N_SSM_GROUPS):
                q, gl = divmod(g, 4)
                rows = slice(gl * SSM_GROUP, (gl + 1) * SSM_GROUP)
                cols = slice(gl * SSM_STATE, (gl + 1) * SSM_STATE)
                for out, acc_ref in ((ob_re, gbr_ref), (ob_im, gbi_ref), (oc_re, gcr_ref), (oc_im, gci_ref)):
                    out[g] = acc_ref[q, rows, cols]

    gshape = jax.ShapeDtypeStruct((N_SSM_GROUPS, SSM_GROUP, SSM_STATE), F32)
    compact = pltpu.VMEM((N_QUAD, 4 * SSM_GROUP, QUAD), F32)
    return _hosted_call(
        body, comm, name="ssm_bwd_rev" if reverse else "ssm_bwd_fwd", grid=(nc,),
        in_specs=[pl.BlockSpec((TC, D_SSM), lambda i: (chunk(i), 0)),
                  pl.BlockSpec((TC, D_SSM), lambda i: (chunk(i), 1)),
                  pl.BlockSpec((TC, N_STATE), lambda i: (chunk(i), 0)),
                  pl.BlockSpec((TC, N_STATE), lambda i: (chunk(i), 0))]
        + [_full(b_re.shape)] * 4 + [_full(f2.shape), _full(tables.shape)],
        out_specs=[pl.BlockSpec((TC, D_SSM), lambda i: (chunk(i), 0))] + [_full(gshape.shape)] * 4
        + [_full((4, N_STATE))],
        out_shape=[jax.ShapeDtypeStruct((L, D_SSM), F32), gshape, gshape, gshape, gshape,
                   jax.ShapeDtypeStruct((4, N_STATE), F32)],
        scratch_shapes=[pltpu.VMEM((TC, N_STATE), F32), pltpu.VMEM((TC, N_STATE), F32),
                        pltpu.VMEM((8, N_STATE), F32), pltpu.VMEM((8, N_STATE), F32),
                        compact, compact, compact, compact],
        args=(dy, u, s_re, s_im, b_re, b_im, c_re, c_im, f2, tables))


def _ssm_post(yf, yb, u, d, glu_w, glu_b):
    y = yf + yb + d * u
    z, t = _gelu(y)
    zb = z.astype(BF16)
    gate = _sigmoid(_dot_nn(zb, glu_w) + glu_b)
    return y, z, t, zb, gate


def _mix_out(yn_pool, yf, yb, u, x, ssm_d, glu_w_b, glu_b, g_ssm, w_out_b, g_ffn):
    L, D = x.shape

    def body(ynp_ref, yf_ref, yb_ref, u_ref, x_ref, d_ref, gw_ref, gb_ref, gs_ref, wo_ref, gf_ref,
             h1_ref, hn_ref, ycat_ref):
        _, z, _, _, gate = _ssm_post(yf_ref[...], yb_ref[...], u_ref[...], d_ref[...], gw_ref[...], gb_ref[...])
        yns, _, _ = _rms_fwd(z * gate, gs_ref[...])
        ynsb = yns.astype(BF16)
        ynp = ynp_ref[...]
        ycat_ref[:, 0:D_POOL] = ynp
        ycat_ref[:, D_POOL:D] = ynsb
        h1 = x_ref[...] + _dot_nn(ynp, wo_ref[0:D_POOL, :]) + _dot_nn(ynsb, wo_ref[D_POOL:D, :])
        h1_ref[...] = h1
        hn, _, _ = _rms_fwd(h1, gf_ref[...])
        hn_ref[...] = hn.astype(BF16)

    half = lambda c: pl.BlockSpec((TL, D_SSM), lambda i: (i, c))
    row = pl.BlockSpec((TL, D), lambda i: (i, 0))
    return pl.pallas_call(
        body, name="mix_out", grid=(L // TL,),
        in_specs=[half(0), half(0), half(0), half(1), row, _full((1, D_SSM)), _full(glu_w_b.shape),
                  _full((1, D_SSM)), _full((1, D_SSM)), _full(w_out_b.shape), _full((1, D))],
        out_specs=[row, row, row],
        out_shape=[jax.ShapeDtypeStruct((L, D), F32), jax.ShapeDtypeStruct((L, D), BF16),
                   jax.ShapeDtypeStruct((L, D), BF16)],
        compiler_params=_cp("parallel"))(yn_pool, yf, yb, u, x, ssm_d, glu_w_b, glu_b, g_ssm, w_out_b, g_ffn)


def _ssm_bwd_local(dh1, yf, yb, u, ssm_d, glu_w_b, glu_b, g_ssm, w_out_b):
    L, D = dh1.shape

    def body(dh_ref, yf_ref, yb_ref, u_ref, d_ref, gw_ref, gb_ref, gs_ref, wo_ref,
             dy_ref, du_ref, ggw_ref, ggb_ref, gd_ref, ggs_ref):
        @pl.when(pl.program_id(0) == 0)
        def _():
            for r in (ggw_ref, ggb_ref, gd_ref, ggs_ref):
                r[...] = jnp.zeros_like(r)

        u = u_ref[...]
        d = d_ref[...]
        y, z, t, zb, gate = _ssm_post(yf_ref[...], yb_ref[...], u, d, gw_ref[...], gb_ref[...])
        gs = gs_ref[...]
        _, xh, inv = _rms_fwd(z * gate, gs)
        d_yn = _dot_nt(dh_ref[...].astype(BF16), wo_ref[...])
        d_o, dgs = _rms_bwd(d_yn, xh, inv, gs)
        ggs_ref[...] += dgs
        d_zg = d_o * z * gate * (1.0 - gate)
        d_zgb = d_zg.astype(BF16)
        ggb_ref[...] += jnp.sum(d_zg, axis=0, keepdims=True)
        ggw_ref[...] += _dot_tn(zb, d_zgb)
        d_z = d_o * gate + _dot_nt(d_zgb, gw_ref[...])
        d_y = d_z * _gelu_grad(y, t)
        gd_ref[...] += jnp.sum(d_y * u, axis=0, keepdims=True)
        dy_ref[...] = d_y
        du_ref[...] = d_y * d

    half = lambda c: pl.BlockSpec((TL, D_SSM), lambda i: (i, c))
    vec = _full((1, D_SSM))
    return pl.pallas_call(
        body, name="ssm_bwd_local", grid=(L // TL,),
        in_specs=[pl.BlockSpec((TL, D), lambda i: (i, 0)), half(0), half(0), half(1), vec, _full(glu_w_b.shape),
                  vec, vec, pl.BlockSpec((D_SSM, D), lambda i: (1, 0))],
        out_specs=[half(0), half(0), _full(glu_w_b.shape), vec, vec, vec],
        out_shape=[jax.ShapeDtypeStruct((L, D_SSM), F32), jax.ShapeDtypeStruct((L, D_SSM), F32),
                   jax.ShapeDtypeStruct(glu_w_b.shape, F32)] + [jax.ShapeDtypeStruct((1, D_SSM), F32)] * 3,
        compiler_params=_cp("arbitrary"))(dh1, yf, yb, u, ssm_d, glu_w_b, glu_b, g_ssm, w_out_b)


def _in_bwd(du_pool, du_a, du_b, du_c, dh1, x, g, w_in_b):
    L, D = x.shape

    def body(p_ref, a_ref, b_ref, c_ref, dh_ref, x_ref, g_ref, w_ref, dx_ref, dub_ref, gg_ref):
        @pl.when(pl.program_id(0) == 0)
        def _():
            gg_ref[...] = jnp.zeros_like(gg_ref)

        dub_ref[:, 0:D_POOL] = p_ref[...].astype(BF16)
        dub_ref[:, D_POOL:D] = (a_ref[...] + b_ref[...] + c_ref[...]).astype(BF16)
        d_xn = _dot_nt(dub_ref[...], w_ref[...])
        gv = g_ref[...]
        _, xh, inv = _rms_fwd(x_ref[...], gv)
        dx, dg = _rms_bwd(d_xn, xh, inv, gv)
        gg_ref[...] += dg
        dx_ref[...] = dh_ref[...] + dx

    half = pl.BlockSpec((TL, D_SSM), lambda i: (i, 0))
    row = pl.BlockSpec((TL, D), lambda i: (i, 0))
    return pl.pallas_call(
        body, name="in_bwd", grid=(L // TL,),
        in_specs=[half, half, half, half, row, row, _full((1, D)), _full(w_in_b.shape)],
        out_specs=[row, row, _full((1, D))],
        out_shape=[jax.ShapeDtypeStruct((L, D), F32), jax.ShapeDtypeStruct((L, D), BF16),
                   jax.ShapeDtypeStruct((1, D), F32)],
        compiler_params=_cp("arbitrary"))(du_pool, du_a, du_b, du_c, dh1, x, g, w_in_b)


def _ffn_up(hn, w_up4):
    L, D = hn.shape

    def body(h_ref, w_ref, o_ref):
        o_ref[...] = _dot_nn(h_ref[...], w_ref[...]).astype(BF16)

    return pl.pallas_call(
        body, name="ffn_up", grid=(4, L // TF),
        in_specs=[pl.BlockSpec((TF, D), lambda j, i: (i, 0)), pl.BlockSpec((None, D, FF_BLK), lambda j, i: (j, 0, 0))],
        out_specs=pl.BlockSpec((TF, FF_BLK), lambda j, i: (i, j)),
        out_shape=jax.ShapeDtypeStruct((L, 4 * FF_BLK), BF16),
        compiler_params=_cp("parallel", "parallel"))(hn, w_up4)


def _halo_specs_2d(rows, width, L, col, order):
    rb = rows // HALO_B
    last = L // HALO_B - 1
    if order == "ik":
        wrap = lambda f: (lambda i, k: f(i, k))
    else:
        wrap = lambda f: (lambda k, i: f(i, k))
    return [pl.BlockSpec((HALO_B, width), wrap(lambda i, k: (jnp.maximum(i * rb - 1, 0), col(k)))),
            pl.BlockSpec((rows, width), wrap(lambda i, k: (i, col(k)))),
            pl.BlockSpec((HALO_B, width), wrap(lambda i, k: (jnp.minimum((i + 1) * rb, last), col(k))))]


def _shift_mats(rows):
    r = lax.broadcasted_iota(jnp.int32, (rows, rows), 0)
    c = lax.broadcasted_iota(jnp.int32, (rows, rows), 1)
    return (c == r - 1).astype(BF16), (c == r + 1).astype(BF16)


def _neighbours(x, prev_ref, next_ref, cs, i, n, mats):
    rows = x.shape[0]
    row = lax.broadcasted_iota(jnp.int32, (rows, 1), 0)
    before = jnp.where(i > 0, prev_ref[:, cs].astype(F32)[HALO_B - 1:HALO_B, :], 0.0)
    after = jnp.where(i < n - 1, next_ref[:, cs].astype(F32)[0:1, :], 0.0)
    return (jnp.where(row == 0, before, _dot_nn(mats[0], x)),
            jnp.where(row == rows - 1, after, _dot_nn(mats[1], x)))


def _conv3(x, before, after, w, b):
    return before * w[0:1, :] + x.astype(F32) * w[1:2, :] + after * w[2:3, :] + b


def _col_chunks(width, size=256):
    return [slice(c, min(c + size, width)) for c in range(0, width, size)]


def _ffn_down_loss(up, conv_w, conv_b, w_down_b, h1, target, g_final):
    L, D = h1.shape
    n = L // TF
    nk = D_FF // FF_BLK

    def body(vp, vc, vn, gp, gc, gn, wv_ref, wg_ref, bv_ref, bg_ref, wd_ref, h1_ref, t_ref, gf_ref,
             a_ref, dh2_ref, dh2b_ref, loss_ref, gg_ref, acc_ref):
        i = pl.program_id(0)
        k = pl.program_id(1)

        @pl.when((i == 0) & (k == 0))
        def _():
            loss_ref[...] = jnp.zeros_like(loss_ref)
            gg_ref[...] = jnp.zeros_like(gg_ref)

        @pl.when(k == 0)
        def _():
            acc_ref[...] = jnp.zeros_like(acc_ref)

        mats = _shift_mats(TF)
        for cs in _col_chunks(FF_BLK):
            xv, xg = vc[:, cs], gc[:, cs]
            val = _conv3(xv, *_neighbours(xv, vp, vn, cs, i, n, mats), wv_ref[:, cs], bv_ref[:, cs])
            gate = _conv3(xg, *_neighbours(xg, gp, gn, cs, i, n, mats), wg_ref[:, cs], bg_ref[:, cs])
            a_ref[:, cs] = (val * (gate * _sigmoid(gate))).astype(BF16)
        acc_ref[...] += _dot_nn(a_ref[...], wd_ref[...])

        @pl.when(k == nk - 1)
        def _():
            gf = gf_ref[...]
            y, xh, inv = _rms_fwd(h1_ref[...] + acc_ref[...], gf)
            diff = y - t_ref[...]
            part = 0.5 * jnp.sum(jnp.mean(diff * diff, axis=-1, keepdims=True), axis=0, keepdims=True)
            loss_ref[...] += jnp.broadcast_to(part, loss_ref.shape)
            dx, dg = _rms_bwd(diff * (1.0 / D), xh, inv, gf)
            gg_ref[...] += dg
            dh2_ref[...] = dx
            dh2b_ref[...] = dx.astype(BF16)

    row = pl.BlockSpec((TF, D), lambda i, k: (i, 0))
    cw = lambda off: pl.BlockSpec((3, FF_BLK), lambda i, k: (0, k + off))
    cb = lambda off: pl.BlockSpec((1, FF_BLK), lambda i, k: (0, k + off))
    return pl.pallas_call(
        body, name="ffn_down_loss", grid=(n, nk),
        in_specs=_halo_specs_2d(TF, FF_BLK, L, lambda k: k, "ik") + _halo_specs_2d(TF, FF_BLK, L, lambda k: k + nk, "ik")
        + [cw(0), cw(nk), cb(0), cb(nk), pl.BlockSpec((FF_BLK, D), lambda i, k: (k, 0)), row, row, _full((1, D))],
        out_specs=[pl.BlockSpec((TF, FF_BLK), lambda i, k: (i, k)), row, row, _full((1, LANES)), _full((1, D))],
        out_shape=[jax.ShapeDtypeStruct((L, D_FF), BF16), jax.ShapeDtypeStruct((L, D), F32),
                   jax.ShapeDtypeStruct((L, D), BF16), jax.ShapeDtypeStruct((1, LANES), F32),
                   jax.ShapeDtypeStruct((1, D), F32)],
        scratch_shapes=[pltpu.VMEM((TF, D), F32)],
        compiler_params=_cp("arbitrary", "arbitrary"))(
            up, up, up, up, up, up, conv_w, conv_w, conv_b, conv_b, w_down_b, h1, target, g_final)


def _ffn_act_bwd(up, conv_w, conv_b, w_down_b, dh2):
    L, D = dh2.shape
    n = L // TF
    nk = D_FF // FF_BLK

    def body(vp, vc, vn, gp, gc, gn, wv_ref, wg_ref, bv_ref, bg_ref, wd_ref, dh_ref,
             dv_ref, dg_ref, gcv_ref, gcg_ref):
        i = pl.program_id(1)

        @pl.when(i == 0)
        def _():
            gcv_ref[...] = jnp.zeros_like(gcv_ref)
            gcg_ref[...] = jnp.zeros_like(gcg_ref)

        mats = _shift_mats(TF)
        dh = dh_ref[...]
        for cs in _col_chunks(FF_BLK):
            xv, xg = vc[:, cs], gc[:, cs]
            v_rows = _neighbours(xv, vp, vn, cs, i, n, mats)
            g_rows = _neighbours(xg, gp, gn, cs, i, n, mats)
            val = _conv3(xv, *v_rows, wv_ref[:, cs], bv_ref[:, cs])
            gate = _conv3(xg, *g_rows, wg_ref[:, cs], bg_ref[:, cs])
            d_a = _dot_nt(dh, wd_ref[cs, :])
            sg = _sigmoid(gate)
            d_val = d_a * (gate * sg)
            d_gate = d_a * val * (sg * (1.0 + gate * (1.0 - sg)))
            dv_ref[:, cs] = d_val.astype(BF16)
            dg_ref[:, cs] = d_gate.astype(BF16)
            for d, x, (before, after), gref in ((d_val, xv, v_rows, gcv_ref), (d_gate, xg, g_rows, gcg_ref)):
                for j, shifted in enumerate((before, x.astype(F32), after)):
                    gref[j:j + 1, cs] += jnp.sum(d * shifted, axis=0, keepdims=True)
                gref[3:4, cs] += jnp.sum(d, axis=0, keepdims=True)

    cw = lambda off: pl.BlockSpec((3, FF_BLK), lambda k, i: (0, k + off))
    cb = lambda off: pl.BlockSpec((1, FF_BLK), lambda k, i: (0, k + off))
    blk = pl.BlockSpec((TF, FF_BLK), lambda k, i: (i, k))
    acc = pl.BlockSpec((4, FF_BLK), lambda k, i: (0, k))
    return pl.pallas_call(
        body, name="ffn_act_bwd", grid=(nk, n),
        in_specs=_halo_specs_2d(TF, FF_BLK, L, lambda k: k, "ki") + _halo_specs_2d(TF, FF_BLK, L, lambda k: k + nk, "ki")
        + [cw(0), cw(nk), cb(0), cb(nk), pl.BlockSpec((FF_BLK, D), lambda k, i: (k, 0)),
           pl.BlockSpec((TF, D), lambda k, i: (i, 0))],
        out_specs=[blk, blk, acc, acc],
        out_shape=[jax.ShapeDtypeStruct((L, D_FF), BF16), jax.ShapeDtypeStruct((L, D_FF), BF16),
                   jax.ShapeDtypeStruct((4, D_FF), F32), jax.ShapeDtypeStruct((4, D_FF), F32)],
        compiler_params=_cp("arbitrary", "arbitrary"))(
            up, up, up, up, up, up, conv_w, conv_w, conv_b, conv_b, w_down_b, dh2)


def _ffn_up_bwd(d_val, d_gate, conv_w, w_up4, h1, dh2, g_ffn):
    L, D = h1.shape
    n = L // TF
    nk = D_FF // FF_BLK

    def body(vp, vc, vn, gp, gc, gn, wv_ref, wg_ref, uv_ref, ug_ref, h1_ref, dh2_ref, g_ref,
             dup_ref, dh1_ref, dh1b_ref, gg_ref, acc_ref):
        i = pl.program_id(0)
        k = pl.program_id(1)

        @pl.when((i == 0) & (k == 0))
        def _():
            gg_ref[...] = jnp.zeros_like(gg_ref)

        @pl.when(k == 0)
        def _():
            acc_ref[...] = jnp.zeros_like(acc_ref)

        mats = _shift_mats(TF)
        for j, (blocks, w_ref, wu_ref) in enumerate((((vp, vc, vn), wv_ref, uv_ref), ((gp, gc, gn), wg_ref, ug_ref))):
            for cs in _col_chunks(FF_BLK):
                d = blocks[1][:, cs]
                before, after = _neighbours(d, blocks[0], blocks[2], cs, i, n, mats)
                w = w_ref[:, cs]
                dup_ref[j, :, cs] = (after * w[0:1, :] + d.astype(F32) * w[1:2, :] + before * w[2:3, :]).astype(BF16)
            acc_ref[...] += _dot_nt(dup_ref[j], wu_ref[...])

        @pl.when(k == nk - 1)
        def _():
            g = g_ref[...]
            _, xh, inv = _rms_fwd(h1_ref[...], g)
            dx, dg = _rms_bwd(acc_ref[...], xh, inv, g)
            gg_ref[...] += dg
            dh1 = dh2_ref[...] + dx
            dh1_ref[...] = dh1
            dh1b_ref[...] = dh1.astype(BF16)

    row = pl.BlockSpec((TF, D), lambda i, k: (i, 0))
    cw = lambda off: pl.BlockSpec((3, FF_BLK), lambda i, k: (0, k + off))
    wu = lambda off: pl.BlockSpec((None, D, FF_BLK), lambda i, k: (k + off, 0, 0))
    return pl.pallas_call(
        body, name="ffn_up_bwd", grid=(n, nk),
        in_specs=_halo_specs_2d(TF, FF_BLK, L, lambda k: k, "ik") + _halo_specs_2d(TF, FF_BLK, L, lambda k: k, "ik")
        + [cw(0), cw(nk), wu(0), wu(nk), row, row, _full((1, D))],
        out_specs=[pl.BlockSpec((2, None, TF, FF_BLK), lambda i, k: (0, k, i, 0)), row, row, _full((1, D))],
        out_shape=[jax.ShapeDtypeStruct((2, nk, L, FF_BLK), BF16), jax.ShapeDtypeStruct((L, D), F32),
                   jax.ShapeDtypeStruct((L, D), BF16), jax.ShapeDtypeStruct((1, D), F32)],
        scratch_shapes=[pltpu.VMEM((TF, D), F32)],
        compiler_params=_cp("arbitrary", "arbitrary"))(
            d_val, d_val, d_val, d_gate, d_gate, d_gate, conv_w, conv_w, w_up4, w_up4, h1, dh2, g_ffn)


def _matmul_tn(a, b, tm, tn, name, tk=512):
    L, M = a.shape
    N = b.shape[1]

    def body(a_ref, b_ref, o_ref):
        @pl.when(pl.program_id(2) == 0)
        def _():
            o_ref[...] = jnp.zeros_like(o_ref)

        o_ref[...] += _dot_tn(a_ref[...], b_ref[...])

    return pl.pallas_call(
        body, name=name, grid=(M // tm, N // tn, L // tk),
        in_specs=[pl.BlockSpec((tk, tm), lambda m, n, l: (l, m)), pl.BlockSpec((tk, tn), lambda m, n, l: (l, n))],
        out_specs=pl.BlockSpec((tm, tn), lambda m, n, l: (m, n)),
        out_shape=jax.ShapeDtypeStruct((M, N), F32),
        compiler_params=_cp("parallel", "parallel", "arbitrary"))(a, b)


def _matmul_tn_blocks(a, b, tm, name, tk=512):
    L, M = a.shape
    J, _, N = b.shape

    def body(a_ref, b_ref, o_ref):
        @pl.when(pl.program_id(2) == 0)
        def _():
            o_ref[...] = jnp.zeros_like(o_ref)

        o_ref[...] += _dot_tn(a_ref[...], b_ref[...])

    return pl.pallas_call(
        body, name=name, grid=(M // tm, J, L // tk),
        in_specs=[pl.BlockSpec((tk, tm), lambda m, j, l: (l, m)), pl.BlockSpec((None, tk, N), lambda m, j, l: (j, l, 0))],
        out_specs=pl.BlockSpec((None, tm, N), lambda m, j, l: (j, m, 0)),
        out_shape=jax.ShapeDtypeStruct((J, M, N), F32),
        compiler_params=_cp("parallel", "parallel", "arbitrary"))(a, b)


def _row_tile(rows):
    for t in (512, 352, 256, 128, 64, 8):
        if rows % t == 0:
            return t
    return rows


def _add_half(g, r, c_arr, name, out_dtype=F32):
    _, _, R, C = g.shape
    tr = _row_tile(R)

    def body(c_ref, g_ref, r_ref, o_ref):
        o_ref[...] = (g_ref[...] + r_ref[...]).astype(out_dtype)

    return pl.pallas_call(
        body, name=name,
        grid_spec=pltpu.PrefetchScalarGridSpec(
            num_scalar_prefetch=1, grid=(g.shape[0], R // tr),
            in_specs=[pl.BlockSpec((None, None, tr, C), lambda j, i, c: (j, c[0], i, 0)),
                      pl.BlockSpec((None, tr, C), lambda j, i, c: (j, i, 0))],
            out_specs=pl.BlockSpec((None, tr, C), lambda j, i, c: (j, i, 0))),
        out_shape=jax.ShapeDtypeStruct(r.shape, out_dtype),
        compiler_params=_cp("parallel", "parallel"))(c_arr, g, r)


def _add2(a, b, name):
    R, C = a.shape
    tr = _row_tile(R)

    def body(a_ref, b_ref, o_ref):
        o_ref[...] = a_ref[...] + b_ref[...]

    spec = pl.BlockSpec((tr, C), lambda i: (i, 0))
    return pl.pallas_call(body, name=name, grid=(R // tr,), in_specs=[spec, spec], out_specs=spec,
                          out_shape=jax.ShapeDtypeStruct(a.shape, F32), compiler_params=_cp("parallel"))(a, b)


def _sum4(p, name):
    _, R, C = p.shape
    tr = _row_tile(R)

    def body(p_ref, o_ref):
        q = [p_ref[j].astype(F32) for j in range(4)]
        o_ref[...] = ((q[0] + q[1]) + q[2]) + q[3]

    return pl.pallas_call(
        body, name=name, grid=(R // tr,),
        in_specs=[pl.BlockSpec((4, tr, C), lambda i: (0, i, 0))],
        out_specs=pl.BlockSpec((tr, C), lambda i: (i, 0)),
        out_shape=jax.ShapeDtypeStruct((R, C), F32), compiler_params=_cp("parallel"))(p)


def _adamw_refs(w_ref, g_ref, m_ref, v_ref, d_ref, nm_ref, nv_ref):
    gv = g_ref[...]
    nm = ADAM_B1 * m_ref[...] + (1.0 - ADAM_B1) * gv
    nv = ADAM_B2 * v_ref[...] + (1.0 - ADAM_B2) * (gv * gv)
    m_hat = nm / (1.0 - ADAM_B1 ** ADAM_STEP)
    v_hat = nv / (1.0 - ADAM_B2 ** ADAM_STEP)
    d_ref[...] = -ADAM_LR * (m_hat / (jnp.sqrt(v_hat) + ADAM_EPS) + ADAM_WD * w_ref[...])
    nm_ref[...] = nm
    nv_ref[...] = nv


def _adamw_many(ws, gs, ms, vs, name):
    n = len(ws)

    def body(*refs):
        for k in range(n):
            _adamw_refs(*(refs[j * n + k] for j in range(7)))

    out_shape = [jax.ShapeDtypeStruct(w.shape, F32) for w in ws] * 3
    res = pl.pallas_call(body, name=name, out_shape=out_shape,
                         compiler_params=pltpu.CompilerParams(vmem_limit_bytes=VMEM_LIMIT))(*ws, *gs, *ms, *vs)
    return res[:n], res[n:2 * n], res[2 * n:]


def _adamw(w, g, m, v, name):
    R, C = w.shape
    tr = _row_tile(R)
    body = lambda *refs: _adamw_refs(*refs)

    spec = pl.BlockSpec((tr, C), lambda i: (i, 0))
    sh = jax.ShapeDtypeStruct((R, C), F32)
    return pl.pallas_call(body, name=name, grid=(R // tr,), in_specs=[spec] * 4, out_specs=[spec] * 3,
                          out_shape=[sh] * 3, compiler_params=_cp("parallel"))(w, g, m, v)


def _join_rows(own, other, c_arr, name):
    R, C = own.shape
    tr = _row_tile(R)

    def body(c_ref, own_ref, other_ref, o_ref):
        o_ref[...] = jnp.where(pl.program_id(0) == c_ref[0], own_ref[...], other_ref[...])

    half = pl.BlockSpec((tr, C), lambda h, i, c: (i, 0))
    return pl.pallas_call(
        body, name=name,
        grid_spec=pltpu.PrefetchScalarGridSpec(
            num_scalar_prefetch=1, grid=(2, R // tr), in_specs=[half, half],
            out_specs=pl.BlockSpec((tr, C), lambda h, i, c: (h * (R // tr) + i, 0))),
        out_shape=jax.ShapeDtypeStruct((2 * R, C), F32),
        compiler_params=_cp("parallel", "parallel"))(c_arr, own, other)


def _adamw_halves(w, own, other, m, v, c_arr, name):
    R, C = own.shape
    tr = _row_tile(R)

    def body(c_ref, w_ref, own_ref, other_ref, m_ref, v_ref, g_ref, d_ref, nm_ref, nv_ref):
        g_ref[...] = jnp.where(pl.program_id(0) == c_ref[0], own_ref[...], other_ref[...])
        _adamw_refs(w_ref, g_ref, m_ref, v_ref, d_ref, nm_ref, nv_ref)

    half = pl.BlockSpec((tr, C), lambda h, i, c: (i, 0))
    full = pl.BlockSpec((tr, C), lambda h, i, c: (h * (R // tr) + i, 0))
    sh = jax.ShapeDtypeStruct((2 * R, C), F32)
    return pl.pallas_call(
        body, name=name,
        grid_spec=pltpu.PrefetchScalarGridSpec(
            num_scalar_prefetch=1, grid=(2, R // tr), in_specs=[full, half, half, full, full], out_specs=[full] * 4),
        out_shape=[sh] * 4, compiler_params=_cp("parallel", "parallel"))(c_arr, w, own, other, m, v)


_ANY = pl.BlockSpec(memory_space=pl.ANY)


def _position():
    return lax.axis_index("x"), lax.axis_index("y"), lax.axis_index("c")


class _Comm:
    def __init__(self, arrs, out_shape, sems, start, finish):
        self.arrs, self.out_shape, self.sems, self.start, self.finish = arrs, out_shape, sems, start, finish


def _comm_call(comm, name):
    n, m = len(comm.arrs), len(comm.out_shape)

    def body(*refs):
        ins, outs, sems = refs[:n], refs[n:n + m], refs[n + m:]
        comm.start(ins, outs, sems)
        comm.finish(ins, outs, sems)

    return pl.pallas_call(
        body, name=name, in_specs=[_ANY] * n, out_specs=[_ANY] * m, out_shape=comm.out_shape,
        scratch_shapes=comm.sems, compiler_params=pltpu.CompilerParams(has_side_effects=True))(*comm.arrs)


def _hosted_call(body, comm, *, name, grid, in_specs, out_specs, out_shape, scratch_shapes, args):
    sem = ("arbitrary",) * len(grid)
    if comm is None:
        return pl.pallas_call(body, name=name, grid=grid, in_specs=in_specs, out_specs=out_specs, out_shape=out_shape,
                              scratch_shapes=scratch_shapes, compiler_params=_cp(*sem))(*args), []
    n_in, n_out, n_scr = len(in_specs), len(out_specs), len(scratch_shapes)
    ci, co = len(comm.arrs), len(comm.out_shape)

    def full(*refs):
        ins, refs = refs[:n_in], refs[n_in:]
        cins, refs = refs[:ci], refs[ci:]
        outs, refs = refs[:n_out], refs[n_out:]
        couts, refs = refs[:co], refs[co:]
        scr, csems = refs[:n_scr], refs[n_scr:]
        first, last = True, True
        for d, size in enumerate(grid):
            first = first & (pl.program_id(d) == 0)
            last = last & (pl.program_id(d) == size - 1)

        @pl.when(first)
        def _():
            comm.start(cins, couts, csems)

        body(*ins, *outs, *scr)

        @pl.when(last)
        def _():
            comm.finish(cins, couts, csems)

    res = pl.pallas_call(
        full, name=name, grid=grid, in_specs=list(in_specs) + [_ANY] * ci, out_specs=list(out_specs) + [_ANY] * co,
        out_shape=list(out_shape) + list(comm.out_shape), scratch_shapes=list(scratch_shapes) + list(comm.sems),
        compiler_params=_cp(*sem))(*args, *comm.arrs)
    return res[:n_out], res[n_out:]


def _comm_join(*comms):
    def parts(xs, attr):
        out, at = [], 0
        for cm in comms:
            n = len(getattr(cm, attr))
            out.append(xs[at:at + n])
            at += n
        return out

    def start(ins, outs, sems):
        for cm, i, o, s in zip(comms, parts(ins, "arrs"), parts(outs, "out_shape"), parts(sems, "sems")):
            cm.start(i, o, s)

    def finish(ins, outs, sems):
        for cm, i, o, s in zip(comms, parts(ins, "arrs"), parts(outs, "out_shape"), parts(sems, "sems")):
            cm.finish(i, o, s)

    cat = lambda attr: [x for cm in comms for x in getattr(cm, attr)]
    return _Comm(cat("arrs"), cat("out_shape"), cat("sems"), start, finish)


def _dma_sems(*counts):
    return [pltpu.SemaphoreType.DMA((n,)) for n in counts]


def _comm_pair_swap(arrs, half=False):
    n = len(arrs)
    out_shape = [jax.ShapeDtypeStruct(a.shape[:1] + a.shape[2:] if half else a.shape, a.dtype) for a in arrs]

    def copies(ins, outs, sems):
        x, y, c = _position()
        return [pltpu.make_async_remote_copy(
            src_ref=ins[k].at[:, 1 - c] if half else ins[k], dst_ref=outs[k], send_sem=sems[0].at[k],
            recv_sem=sems[1].at[k], device_id=(x, y, 1 - c), device_id_type=MESH) for k in range(n)]

    def start(ins, outs, sems):
        for cp in copies(ins, outs, sems):
            cp.start()

    def finish(ins, outs, sems):
        for cp in copies(ins, outs, sems):
            cp.wait()

    return _Comm(arrs, out_shape, _dma_sems(n, n), start, finish)


def _chip_of(j, c):
    return (jnp.right_shift(j, 1), jnp.bitwise_and(j, 1), c)


def _comm_chip_exchange(arrs, scatter):
    n = len(arrs)
    out_shape = [jax.ShapeDtypeStruct(a.shape if scatter else (4,) + a.shape, a.dtype) for a in arrs]

    def copies(ins, outs, sems):
        x, y, c = _position()
        me = 2 * x + y
        local, sent, landed = [], [], []
        for k in range(n):
            local.append(pltpu.make_async_copy(ins[k].at[me] if scatter else ins[k], outs[k].at[me], sems[2].at[k]))
            for d in (1, 2, 3):
                j = jnp.bitwise_xor(me, d)
                s = 3 * k + d - 1
                src = ins[k].at[j] if scatter else ins[k]
                for dst, group in ((outs[k].at[me], sent), (outs[k].at[j], landed)):
                    group.append(pltpu.make_async_remote_copy(
                        src_ref=src, dst_ref=dst, send_sem=sems[0].at[s], recv_sem=sems[1].at[s],
                        device_id=_chip_of(j, c), device_id_type=MESH))
        return local, sent, landed

    def start(ins, outs, sems):
        local, sent, _ = copies(ins, outs, sems)
        for cp in local + sent:
            cp.start()

    def finish(ins, outs, sems):
        local, sent, landed = copies(ins, outs, sems)
        for cp in sent:
            cp.wait_send()
        for cp in landed:
            cp.wait_recv()
        for cp in local:
            cp.wait()

    return _Comm(arrs, out_shape, _dma_sems(3 * n, 3 * n, n), start, finish)


def _comm_pair_gather(arrs):
    n = len(arrs)
    out_shape = [jax.ShapeDtypeStruct((2,) + a.shape, a.dtype) for a in arrs]

    def copies(ins, outs, sems):
        x, y, c = _position()
        local, sent, landed = [], [], []
        for k in range(n):
            local.append(pltpu.make_async_copy(ins[k], outs[k].at[c], sems[2].at[k]))
            for dst, group in ((outs[k].at[c], sent), (outs[k].at[1 - c], landed)):
                group.append(pltpu.make_async_remote_copy(
                    src_ref=ins[k], dst_ref=dst, send_sem=sems[0].at[k], recv_sem=sems[1].at[k],
                    device_id=(x, y, 1 - c), device_id_type=MESH))
        return local, sent, landed

    def start(ins, outs, sems):
        local, sent, _ = copies(ins, outs, sems)
        for cp in local + sent:
            cp.start()

    def finish(ins, outs, sems):
        local, sent, landed = copies(ins, outs, sems)
        for cp in sent:
            cp.wait_send()
        for cp in landed:
            cp.wait_recv()
        for cp in local:
            cp.wait()

    return _Comm(arrs, out_shape, _dma_sems(n, n, n), start, finish)


def _comm_gather_split(shards, whole):
    n, nw = len(shards), len(whole)
    arrs = list(shards) + list(whole)
    out_shape = [jax.ShapeDtypeStruct((4,) + a.shape, a.dtype) for a in arrs]

    def copies(ins, outs, sems):
        x, y, c = _position()
        me = 2 * x + y
        local, sent, landed, passed, passed_in = [], [], [], [], []
        for k in range(n + nw):
            local.append(pltpu.make_async_copy(ins[k], outs[k].at[me], sems[4].at[k]))
            for d in (1, 2, 3):
                j = jnp.bitwise_xor(me, d)
                s = 3 * k + d - 1
                if k >= n:
                    src, mine, theirs = ins[k], outs[k].at[me], outs[k].at[j]
                else:
                    h = shards[k].shape[0] // 2
                    rows = pl.ds(pl.multiple_of(c * h, 16), h)
                    other = pl.ds(pl.multiple_of((1 - c) * h, 16), h)
                    src, mine, theirs = ins[k].at[rows], outs[k].at[me, rows], outs[k].at[j, rows]
                    for dst, group in ((theirs, passed), (outs[k].at[j, other], passed_in)):
                        group.append(pltpu.make_async_remote_copy(
                            src_ref=theirs, dst_ref=dst, send_sem=sems[2].at[s], recv_sem=sems[3].at[s],
                            device_id=(x, y, 1 - c), device_id_type=MESH))
                for dst, group in ((mine, sent), (theirs, landed)):
                    group.append(pltpu.make_async_remote_copy(
                        src_ref=src, dst_ref=dst, send_sem=sems[0].at[s], recv_sem=sems[1].at[s],
                        device_id=_chip_of(j, c), device_id_type=MESH))
        return local, sent, landed, passed, passed_in

    def start(ins, outs, sems):
        local, sent, _, _, _ = copies(ins, outs, sems)
        for cp in local + sent:
            cp.start()

    def finish(ins, outs, sems):
        local, sent, landed, passed, passed_in = copies(ins, outs, sems)
        for cp in landed[:3 * n]:
            cp.wait_recv()
        for cp in passed:
            cp.start()
        for cp in landed[3 * n:]:
            cp.wait_recv()
        for cp in sent:
            cp.wait_send()
        for cp in passed:
            cp.wait_send()
        for cp in passed_in:
            cp.wait_recv()
        for cp in local:
            cp.wait()

    t = 3 * (n + nw)
    return _Comm(arrs, out_shape, _dma_sems(t, t, max(3 * n, 1), max(3 * n, 1), n + nw), start, finish)


def _pack(arrs, row_multiple):
    parts = []
    for a in arrs:
        flat = a.reshape(-1).astype(F32)
        pad = (-flat.shape[0]) % LANES
        parts.append(jnp.pad(flat, (0, pad)) if pad else flat)
    flat = jnp.concatenate(parts)
    rows = -(-flat.shape[0] // LANES)
    rows_p = -(-rows // row_multiple) * row_multiple
    return jnp.pad(flat, (0, rows_p * LANES - flat.shape[0])).reshape(rows_p, LANES)


def _unpack(packed, shapes):
    flat = packed.reshape(-1)
    outs, off = [], 0
    for sh in shapes:
        size = int(np.prod(sh))
        outs.append(flat[off:off + size].reshape(sh))
        off += size + (-size) % LANES
    return outs


SMALL = ["norm_mix_g", "pool_w", "pool_scale", "ssm_log_neg_a_re", "ssm_a_im", "ssm_log_dt", "ssm_b_re", "ssm_b_im",
         "ssm_c_re", "ssm_c_im", "ssm_d", "glu_b", "out_norm_pool_g", "out_norm_ssm_g", "norm_ffn_g", "conv_b",
         "final_norm_g"]
BIG = ["w_in", "glu_w", "w_out", "w_up", "w_down"]
WEIGHTS = ['norm_mix_g', 'w_in', 'pool_w', 'pool_scale', 'ssm_log_neg_a_re', 'ssm_a_im', 'ssm_log_dt', 'ssm_b_re',
           'ssm_b_im', 'ssm_c_re', 'ssm_c_im', 'ssm_d', 'glu_w', 'glu_b', 'out_norm_pool_g', 'out_norm_ssm_g', 'w_out',
           'norm_ffn_g', 'w_up', 'conv_w', 'conv_b', 'w_down', 'final_norm_g']


def _local_step(x, target, p, full, shards=None, c_arr=None):
    L, D = x.shape
    dist = shards is not None
    row = lambda a: a.reshape(1, -1)
    w_in = full["w_in"]
    pool_w_b = p["pool_w"].astype(BF16)
    g_mix, g_pool, g_ssm, g_ffn, g_fin = (row(p[k]) for k in (
        "norm_mix_g", "out_norm_pool_g", "out_norm_ssm_g", "norm_ffn_g", "final_norm_g"))
    pool_scale, ssm_d, glu_b, conv_b = (row(p[k]) for k in ("pool_scale", "ssm_d", "glu_b", "conv_b"))

    lnar = p["ssm_log_neg_a_re"].reshape(2 * N_SSM_GROUPS, SSM_STATE)
    aim = p["ssm_a_im"].reshape(2 * N_SSM_GROUPS, SSM_STATE)
    ldt = jnp.broadcast_to(p["ssm_log_dt"].reshape(2 * N_SSM_GROUPS, 1), lnar.shape)
    lam_re, lam_im, f_re, f_im = _ssm_params(lnar, aim, ldt)
    flat2 = lambda a: a.reshape(2, N_STATE)
    lam4 = jnp.stack([flat2(lam_re)[0], flat2(lam_im)[0], flat2(lam_re)[1], flat2(lam_im)[1]])
    tables = _scan_tables(lam4)
    f2 = [jnp.stack([flat2(f_re)[d], flat2(f_im)[d]]) for d in range(2)]
    dense = _ssm_expand(p["ssm_b_re"], p["ssm_b_im"], p["ssm_c_re"], p["ssm_c_im"])
    ssm_args = [tuple(dense[4 * d:4 * d + 4]) + (f2[d], tables) for d in range(2)]

    u, xn = _in_proj(x, g_mix, w_in)
    yn_pool = _pool_fwd(u, pool_w_b, pool_scale, g_pool)
    gather1 = _comm_gather_split([shards[k] for k in ("glu_w", "w_out", "w_down")], [shards["conv_w"]]) if dist else None
    (y0, s0r, s0i), got1 = _ssm_scan_fwd(u, *ssm_args[0], 0, False, comm=gather1)
    gather2 = _comm_gather_split([shards["w_up"]], []) if dist else None
    (y1, s1r, s1i), got2 = _ssm_scan_fwd(u, *ssm_args[1], 2, True, comm=gather2)
    if dist:
        glu_w, w_out, w_down = (g.reshape((-1,) + g.shape[2:]) for g in got1[:3])
        conv_w = jnp.transpose(got1[3], (1, 0, 2)).reshape(3, -1)
        w_up4 = got2[0]
    else:
        glu_w, w_out, w_up4, w_down, conv_w = (full[k] for k in ("glu_w", "w_out", "w_up", "w_down", "conv_w"))
    h1, hn, ycat = _mix_out(yn_pool, y0, y1, u, x, ssm_d, glu_w, glu_b, g_ssm, w_out, g_ffn)
    up = _ffn_up(hn, w_up4)
    a, dh2, dh2_b, loss, g_final = _ffn_down_loss(up, conv_w, conv_b, w_down, h1, target, g_fin)

    d_val, d_gate, gcv, gcg = _ffn_act_bwd(up, conv_w, conv_b, w_down, dh2_b)
    g_w_down = _matmul_tn(a, dh2_b, FF_BLK, D, "grad_w_down")
    d_up, dh1, dh1_b, g_ffn_g = _ffn_up_bwd(d_val, d_gate, conv_w, w_up4, h1, dh2, g_ffn)
    g_w_up = _matmul_tn_blocks(hn, d_up.reshape(4, L, FF_BLK), 512, "grad_w_up")
    g_w_out = _matmul_tn(ycat, dh1_b, 512, D, "grad_w_out")
    late = ("w_up", "w_down")
    halves = [g_w_up.reshape(4, 2, D // 2, FF_BLK), g_w_down.reshape(4, 2, D_FF // 8, D)]
    (d_pooled, g_pool_w, g_pool_scale, g_pool_g), from_sibling = _pool_bwd_local(
        dh1_b, u, w_out, pool_w_b, pool_scale, g_pool, comm=_comm_pair_swap(halves, half=True) if dist else None)
    du_pool = _pool_bwd_window(d_pooled)
    dy, du_direct, g_glu_w, g_glu_b, g_ssm_d, g_ssm_g = _ssm_bwd_local(dh1_b, y0, y1, u, ssm_d, glu_w, glu_b, g_ssm, w_out)
    reduce2 = None
    if dist:
        chip_sums = [_add_half(h, r, c_arr, "sum_pair_" + k, BF16) for k, h, r in zip(late, halves, from_sibling)]
        reduce2 = _comm_chip_exchange(chip_sums, scatter=True)
    (du0, gb0r, gb0i, gc0r, gc0i, gv0), from_chips = _ssm_scan_bwd(dy, u, s0r, s0i, *ssm_args[0], 1, True, comm=reduce2)
    reduce3 = _comm_pair_gather([_sum4(r, "sum_chips_" + k) for k, r in zip(late, from_chips)]) if dist else None
    (du1, gb1r, gb1i, gc1r, gc1i, gv1), shards_out = _ssm_scan_bwd(dy, u, s1r, s1i, *ssm_args[1], 3, False, comm=reduce3)
    gvec = lambda j: jnp.stack([gv0[j], gv1[j]]).reshape(2 * N_SSM_GROUPS, SSM_STATE)
    g_lnar, g_aim, g_ldt = _ssm_params_bwd(lnar, aim, ldt, gvec(0), gvec(1), gvec(2), gvec(3))
    grad_x, d_u_b, g_mix_g = _in_bwd(du_pool, du_direct, du0, du1, dh1, x, g_mix, w_in)
    g_w_in = _matmul_tn(xn, d_u_b, 512, D, "grad_w_in")

    small = {
        "norm_mix_g": g_mix_g, "pool_w": g_pool_w, "pool_scale": g_pool_scale,
        "ssm_log_neg_a_re": g_lnar, "ssm_a_im": g_aim, "ssm_log_dt": g_ldt,
        "ssm_b_re": jnp.swapaxes(jnp.stack([gb0r, gb1r]), 2, 3), "ssm_b_im": jnp.swapaxes(jnp.stack([gb0i, gb1i]), 2, 3),
        "ssm_c_re": jnp.stack([gc0r, gc1r]), "ssm_c_im": jnp.stack([gc0i, gc1i]),
        "ssm_d": g_ssm_d, "glu_b": g_glu_b, "out_norm_pool_g": g_pool_g, "out_norm_ssm_g": g_ssm_g,
        "norm_ffn_g": g_ffn_g, "conv_b": jnp.concatenate([gcv[3], gcg[3]]), "final_norm_g": g_final,
        "conv_w": jnp.concatenate([gcv[0:3], gcg[0:3]], axis=1),
    }
    big = {"w_in": g_w_in, "glu_w": g_glu_w, "w_out": g_w_out}
    reduced = dict(zip(late, shards_out))
    if not dist:
        big.update({"w_up": g_w_up, "w_down": g_w_down})
    return loss, grad_x, small, big, reduced


def kernel(x, norm_mix_g, w_in, pool_w, pool_scale, ssm_log_neg_a_re, ssm_a_im, ssm_log_dt, ssm_b_re, ssm_b_im, ssm_c_re, ssm_c_im, ssm_d, glu_w, glu_b, out_norm_pool_g, out_norm_ssm_g, w_out, norm_ffn_g, w_up, conv_w, conv_b, w_down, final_norm_g, loss_target, m_norm_mix_g, m_w_in, m_pool_w, m_pool_scale, m_ssm_log_neg_a_re, m_ssm_a_im, m_ssm_log_dt, m_ssm_b_re, m_ssm_b_im, m_ssm_c_re, m_ssm_c_im, m_ssm_d, m_glu_w, m_glu_b, m_out_norm_pool_g, m_out_norm_ssm_g, m_w_out, m_norm_ffn_g, m_w_up, m_conv_w, m_conv_b, m_w_down, m_final_norm_g, v_norm_mix_g, v_w_in, v_pool_w, v_pool_scale, v_ssm_log_neg_a_re, v_ssm_a_im, v_ssm_log_dt, v_ssm_b_re, v_ssm_b_im, v_ssm_c_re, v_ssm_c_im, v_ssm_d, v_glu_w, v_glu_b, v_out_norm_pool_g, v_out_norm_ssm_g, v_w_out, v_norm_ffn_g, v_w_up, v_conv_w, v_conv_b, v_w_down, v_final_norm_g):
    args = locals()
    w = {k: args[k] for k in WEIGHTS}
    m = {k: args["m_" + k] for k in WEIGHTS}
    v = {k: args["v_" + k] for k in WEIGHTS}
    chip = 2 * lax.axis_index("x") + lax.axis_index("y")
    c_arr = lax.axis_index("c").astype(jnp.int32).reshape(1)

    shards = {k: w[k].astype(BF16) for k in BIG}
    shards["conv_w"] = conv_w
    w_in_full = _comm_call(_comm_chip_exchange([shards["w_in"]], scatter=False), "gather_w_in")[0]
    loss, grad_x, g_small, g_big, reduced = _local_step(
        x[0], loss_target[0], w, {"w_in": w_in_full.reshape(-1, w_in_full.shape[-1])}, shards, c_arr)

    tail = ("w_in", "glu_w", "w_out")
    packed = _pack([loss] + [g_small[k] for k in SMALL] + [g_small["conv_w"]], 1024)
    halves = [g_big[k].reshape(4, 2, g_big[k].shape[0] // 8, g_big[k].shape[1]) for k in tail]
    halves.append(packed.reshape(1, 2, packed.shape[0] // 2, LANES))
    from_sibling = _comm_call(_comm_pair_swap(halves, half=True), "reduce_pair")
    names = tail + ("small",)
    sums = [_add_half(h, r, c_arr, "sum_pair_" + k, F32 if k == "small" else BF16)
            for k, h, r in zip(names, halves, from_sibling)]
    from_chips = _comm_call(_comm_join(_comm_chip_exchange(sums[:3], scatter=True),
                                       _comm_chip_exchange([sums[3][0]], scatter=False)), "reduce_chips")
    mine = [_sum4(r, "sum_chips_" + k) for k, r in zip(names, from_chips)]
    theirs = _comm_call(_comm_pair_swap(mine), "swap_halves")
    grads = {k: s.reshape(w[k].shape) for k, s in reduced.items()}
    shapes = [loss.shape] + [w[k].shape for k in SMALL] + [(3, 4 * FF_BLK)]
    small_all = _join_rows(mine[3], theirs[3], c_arr, "join_small")
    for k, g in zip(["loss"] + SMALL + ["conv_w_full"], _unpack(small_all, shapes)):
        grads[k] = g
    loss = grads.pop("loss")[0, 0]
    grads["conv_w"] = lax.dynamic_slice_in_dim(grads.pop("conv_w_full"), chip * FF_BLK, FF_BLK, axis=1)

    delta, new_m, new_v = {}, {}, {}
    for k in reduced:
        delta[k], new_m[k], new_v[k] = _adamw(w[k], grads[k], m[k], v[k], "adamw_" + k)
    for k, own, other in zip(tail, mine, theirs):
        grads[k], delta[k], new_m[k], new_v[k] = _adamw_halves(w[k], own, other, m[k], v[k], c_arr, "adamw_" + k)
    wide = ["ssm_b_re", "ssm_b_im"]
    for keys, name in ((wide, "adamw_ssm_b"), ([k for k in SMALL + ["conv_w"] if k not in wide], "adamw_small")):
        outs = _adamw_many(*([d[k] for k in keys] for d in (w, grads, m, v)), name)
        for d, o in zip((delta, new_m, new_v), outs):
            d.update(zip(keys, o))

    return (loss, grad_x[None], *[grads[k] for k in WEIGHTS], *[delta[k] for k in WEIGHTS],
            *[new_m[k] for k in WEIGHTS], *[new_v[k] for k in WEIGHTS])
```

```python
import numpy as np
import jax
import jax.numpy as jnp
from jax import lax
from jax.experimental import pallas as pl
from jax.experimental.pallas import tpu as pltpu

F32 = jnp.float32
BF16 = jnp.bfloat16
MESH = pl.DeviceIdType.MESH

EPS = 1e-6
POOL_WINDOWS = (2, 4, 8, 16)
POOL_GROUP = 128
SSM_GROUP = 16
SSM_STATE = 64
N_SSM_GROUPS = 32
N_STATE = N_SSM_GROUPS * SSM_STATE
QUAD = 256
N_QUAD = N_STATE // QUAD
SLAB = 256
D_SSM = 512
D_POOL = 512
D_FF = 2816
FF_BLK = 1408
HALO = 8
HALO_B = 16
LANES = 128
ADAM_LR, ADAM_B1, ADAM_B2, ADAM_EPS, ADAM_WD, ADAM_STEP = 0.001, 0.9, 0.999, 1e-08, 0.01, 10
VMEM_LIMIT = 56 * 2 ** 20

TL = 512
TF = 256
TC = 256
SCAN_W = 512


def _cp(*sem):
    return pltpu.CompilerParams(dimension_semantics=sem, vmem_limit_bytes=VMEM_LIMIT)


def _dot_nn(a, b):
    return jnp.dot(a, b, preferred_element_type=F32)


def _dot_nt(a, b):
    return lax.dot_general(a, b, (((1,), (1,)), ((), ())), preferred_element_type=F32)


def _dot_tn(a, b):
    return lax.dot_general(a, b, (((0,), (0,)), ((), ())), preferred_element_type=F32)


def _rms_fwd(x, g):
    inv = lax.rsqrt(jnp.mean(x * x, axis=-1, keepdims=True) + EPS)
    xh = x * inv
    return xh * g, xh, inv


def _rms_bwd(dy, xh, inv, g):
    dg = jnp.sum(dy * xh, axis=0, keepdims=True)
    dxh = dy * g
    dx = inv * (dxh - xh * jnp.mean(dxh * xh, axis=-1, keepdims=True))
    return dx, dg


_GELU_C = 0.7978845608028654
_GELU_A = 0.044715


def _gelu(y):
    t = jnp.tanh(_GELU_C * (y + _GELU_A * (y * y * y)))
    return 0.5 * y * (1.0 + t), t


def _gelu_grad(y, t):
    return 0.5 * (1.0 + t) + 0.5 * y * (1.0 - t * t) * (_GELU_C * (1.0 + 3.0 * _GELU_A * y * y))


def _sigmoid(x):
    return 1.0 / (1.0 + jnp.exp(-x))


def _full(shape):
    n = len(shape)
    return pl.BlockSpec(shape, lambda *_: (0,) * n)


def _fill_ext(ext_ref, prev_ref, cur_ref, next_ref, i, n, rows):
    ext_ref[0:HALO, :] = jnp.where(i > 0, prev_ref[...], 0.0).astype(ext_ref.dtype)
    ext_ref[HALO:HALO + rows, :] = cur_ref[...]
    ext_ref[HALO + rows:2 * HALO + rows, :] = jnp.where(i < n - 1, next_ref[...], 0.0).astype(ext_ref.dtype)


def _in_proj(x, g, w):
    L, D = x.shape
    E = w.shape[1]

    def body(x_ref, g_ref, w_ref, u_ref, xn_ref):
        y, _, _ = _rms_fwd(x_ref[...], g_ref[...])
        yb = y.astype(BF16)
        xn_ref[...] = yb
        u_ref[...] = _dot_nn(yb, w_ref[...])

    return pl.pallas_call(
        body, name="in_proj", grid=(L // TL,),
        in_specs=[pl.BlockSpec((TL, D), lambda i: (i, 0)), _full((1, D)), _full(w.shape)],
        out_specs=[pl.BlockSpec((TL, E), lambda i: (i, 0)), pl.BlockSpec((TL, D), lambda i: (i, 0))],
        out_shape=[jax.ShapeDtypeStruct((L, E), F32), jax.ShapeDtypeStruct((L, D), BF16)],
        compiler_params=_cp("parallel"))(x, g, w)


def _halo_specs_1d(rows, width, L, col):
    rb = rows // HALO
    last = L // HALO - 1
    return [pl.BlockSpec((HALO, width), lambda i: (jnp.maximum(i * rb - 1, 0), col)),
            pl.BlockSpec((rows, width), lambda i: (i, col)),
            pl.BlockSpec((HALO, width), lambda i: (jnp.minimum((i + 1) * rb, last), col))]


def _pooled_from_ext(ext_ref, t0, rows, L):
    t = t0 + lax.broadcasted_iota(jnp.int32, (rows, 1), 0)
    outs = []
    for gi, w in enumerate(POOL_WINDOWS):
        half = w // 2
        cs = slice(gi * POOL_GROUP, (gi + 1) * POOL_GROUP)
        acc = ext_ref[pl.ds(HALO - half, rows), cs]
        for s in range(-half + 1, half):
            acc = acc + ext_ref[pl.ds(HALO + s, rows), cs]
        cnt = (jnp.minimum(t + half, L) - jnp.maximum(t - half, 0)).astype(F32)
        outs.append(acc / cnt - ext_ref[pl.ds(HALO, rows), cs])
    return outs


def _pool_fwd(u, pool_w_b, pool_scale, g_pool):
    L = u.shape[0]
    n = L // TL

    def body(prev_ref, cur_ref, next_ref, pw_ref, ps_ref, g_ref, out_ref, ext_ref):
        i = pl.program_id(0)
        _fill_ext(ext_ref, prev_ref, cur_ref, next_ref, i, n, TL)
        pooled = _pooled_from_ext(ext_ref, i * TL, TL, L)
        ypre = jnp.concatenate([_dot_nn(pooled[gi].astype(BF16), pw_ref[gi]) for gi in range(4)], axis=-1)
        yn, _, _ = _rms_fwd(ypre * ps_ref[...], g_ref[...])
        out_ref[...] = yn.astype(BF16)

    return pl.pallas_call(
        body, name="pool_fwd", grid=(n,),
        in_specs=_halo_specs_1d(TL, D_POOL, L, 0) + [_full(pool_w_b.shape), _full((1, D_POOL)), _full((1, D_POOL))],
        out_specs=pl.BlockSpec((TL, D_POOL), lambda i: (i, 0)),
        out_shape=jax.ShapeDtypeStruct((L, D_POOL), BF16),
        scratch_shapes=[pltpu.VMEM((TL + 2 * HALO, D_POOL), F32)],
        compiler_params=_cp("parallel"))(u, u, u, pool_w_b, pool_scale, g_pool)


def _pool_bwd_local(dh1, u, w_out_b, pool_w_b, pool_scale, g_pool, comm=None):
    L = u.shape[0]
    n = L // TL
    D = dh1.shape[1]

    def body(dh_ref, prev_ref, cur_ref, next_ref, wo_ref, pw_ref, ps_ref, g_ref,
             dp_ref, gpw_ref, gps_ref, gg_ref, ext_ref):
        i = pl.program_id(0)

        @pl.when(i == 0)
        def _():
            gpw_ref[...] = jnp.zeros_like(gpw_ref)
            gps_ref[...] = jnp.zeros_like(gps_ref)
            gg_ref[...] = jnp.zeros_like(gg_ref)

        _fill_ext(ext_ref, prev_ref, cur_ref, next_ref, i, n, TL)
        pooled = [p.astype(BF16) for p in _pooled_from_ext(ext_ref, i * TL, TL, L)]
        ypre = jnp.concatenate([_dot_nn(pooled[gi], pw_ref[gi]) for gi in range(4)], axis=-1)
        ps = ps_ref[...]
        g = g_ref[...]
        _, xh, inv = _rms_fwd(ypre * ps, g)
        d_yn = _dot_nt(dh_ref[...].astype(BF16), wo_ref[...])
        d_y, dg = _rms_bwd(d_yn, xh, inv, g)
        gg_ref[...] += dg
        gps_ref[...] += jnp.sum(d_y * ypre, axis=0, keepdims=True)
        d_ypre = (d_y * ps).astype(BF16)
        for gi in range(4):
            cs = slice(gi * POOL_GROUP, (gi + 1) * POOL_GROUP)
            dp_ref[:, cs] = _dot_nt(d_ypre[:, cs], pw_ref[gi])
            gpw_ref[gi] += _dot_tn(pooled[gi], d_ypre[:, cs])

    return _hosted_call(
        body, comm, name="pool_bwd_local", grid=(n,),
        in_specs=[pl.BlockSpec((TL, D), lambda i: (i, 0))] + _halo_specs_1d(TL, D_POOL, L, 0)
        + [pl.BlockSpec((D_POOL, D), lambda i: (0, 0)), _full(pool_w_b.shape), _full((1, D_POOL)), _full((1, D_POOL))],
        out_specs=[pl.BlockSpec((TL, D_POOL), lambda i: (i, 0)), _full(pool_w_b.shape),
                   _full((1, D_POOL)), _full((1, D_POOL))],
        out_shape=[jax.ShapeDtypeStruct((L, D_POOL), F32), jax.ShapeDtypeStruct(pool_w_b.shape, F32),
                   jax.ShapeDtypeStruct((1, D_POOL), F32), jax.ShapeDtypeStruct((1, D_POOL), F32)],
        scratch_shapes=[pltpu.VMEM((TL + 2 * HALO, D_POOL), F32)],
        args=(dh1, u, u, u, w_out_b, pool_w_b, pool_scale, g_pool))


def _pool_bwd_window(d_pooled):
    L = d_pooled.shape[0]
    n = L // TL
    R = TL + 2 * HALO

    def body(prev_ref, cur_ref, next_ref, out_ref, ext_ref, q_ref):
        i = pl.program_id(0)
        _fill_ext(ext_ref, prev_ref, cur_ref, next_ref, i, n, TL)
        tr = i * TL - HALO + lax.broadcasted_iota(jnp.int32, (R, 1), 0)
        for gi, w in enumerate(POOL_WINDOWS):
            half = w // 2
            cs = slice(gi * POOL_GROUP, (gi + 1) * POOL_GROUP)
            cnt = jnp.maximum(jnp.minimum(tr + half, L) - jnp.maximum(tr - half, 0), 1).astype(F32)
            q_ref[:, cs] = ext_ref[:, cs] / cnt
        for gi, w in enumerate(POOL_WINDOWS):
            half = w // 2
            cs = slice(gi * POOL_GROUP, (gi + 1) * POOL_GROUP)
            acc = q_ref[pl.ds(HALO - half + 1, TL), cs]
            for s in range(-half + 2, half + 1):
                acc = acc + q_ref[pl.ds(HALO + s, TL), cs]
            out_ref[:, cs] = acc - ext_ref[pl.ds(HALO, TL), cs]

    return pl.pallas_call(
        body, name="pool_bwd_window", grid=(n,),
        in_specs=_halo_specs_1d(TL, D_POOL, L, 0),
        out_specs=pl.BlockSpec((TL, D_POOL), lambda i: (i, 0)),
        out_shape=jax.ShapeDtypeStruct((L, D_POOL), F32),
        scratch_shapes=[pltpu.VMEM((R, D_POOL), F32), pltpu.VMEM((R, D_POOL), F32)],
        compiler_params=_cp("parallel"))(d_pooled, d_pooled, d_pooled)


def _ssm_param_fn(lnar, aim, ldt):
    dt = jnp.exp(ldt)
    a_re = -jnp.exp(lnar)
    mag = jnp.exp(a_re * dt)
    ang = aim * dt
    lr, li = mag * jnp.cos(ang), mag * jnp.sin(ang)
    den = a_re * a_re + aim * aim
    fr = ((lr - 1.0) * a_re + li * aim) / den
    fi = (li * a_re - (lr - 1.0) * aim) / den
    return lr, li, fr, fi


def _ssm_params(lnar, aim, ldt):
    def body(a_ref, b_ref, c_ref, lr_ref, li_ref, fr_ref, fi_ref):
        lr, li, fr, fi = _ssm_param_fn(a_ref[...], b_ref[...], c_ref[...])
        lr_ref[...] = lr
        li_ref[...] = li
        fr_ref[...] = fr
        fi_ref[...] = fi

    sh = jax.ShapeDtypeStruct(lnar.shape, F32)
    return pl.pallas_call(body, name="ssm_params", out_shape=[sh] * 4)(lnar, aim, ldt)


def _ssm_params_bwd(lnar, aim, ldt, glr, gli, gfr, gfi):
    def body(a_ref, b_ref, c_ref, g0, g1, g2, g3, da_ref, db_ref, dc_ref):
        _, vjp = jax.vjp(_ssm_param_fn, a_ref[...], b_ref[...], c_ref[...])
        da, db, dc = vjp((g0[...], g1[...], g2[...], g3[...]))
        da_ref[...] = da
        db_ref[...] = db
        dc_ref[...] = jnp.sum(dc, axis=1, keepdims=True)

    return pl.pallas_call(
        body, name="ssm_params_bwd",
        out_shape=[jax.ShapeDtypeStruct(lnar.shape, F32), jax.ShapeDtypeStruct(aim.shape, F32),
                   jax.ShapeDtypeStruct((ldt.shape[0], 1), F32)])(lnar, aim, ldt, glr, gli, gfr, gfi)


def _scan_tables(lam4):
    def build(lr, li, reverse, out_ref, k):
        row = lax.broadcasted_iota(jnp.int32, (8, N_STATE), 0)
        lrb = jnp.broadcast_to(lr, (8, N_STATE))
        lib = jnp.broadcast_to(li, (8, N_STATE))
        pr, pi = lrb, lib
        for s, sh in enumerate((1, 2, 4)):
            mask = (row < 8 - sh) if reverse else (row >= sh)
            out_ref[k, 2 * s] = jnp.where(mask, pr, 0.0)
            out_ref[k, 2 * s + 1] = jnp.where(mask, pi, 0.0)
            pr, pi = pr * pr - pi * pi, 2.0 * pr * pi
        pr, pi = lrb, lib
        p8r = jnp.zeros((8, N_STATE), F32)
        p8i = jnp.zeros((8, N_STATE), F32)
        for j in range(8):
            r = 7 - j if reverse else j
            p8r = jnp.where(row == r, pr, p8r)
            p8i = jnp.where(row == r, pi, p8i)
            pr, pi = pr * lrb - pi * lib, pr * lib + pi * lrb
        out_ref[k, 6] = p8r
        out_ref[k, 7] = p8i

    def body(lam_ref, out_ref):
        l0r, l0i, l1r, l1i = (lam_ref[j:j + 1, :] for j in range(4))
        build(l0r, l0i, False, out_ref, 0)
        build(l0r, -l0i, True, out_ref, 1)
        build(l1r, l1i, True, out_ref, 2)
        build(l1r, -l1i, False, out_ref, 3)

    return pl.pallas_call(body, name="scan_tables",
                          out_shape=jax.ShapeDtypeStruct((4, 8, 8, N_STATE), F32))(lam4)


def _b_block(g):
    q, gl = divmod(g, 4)
    r0, c0 = gl * SSM_STATE, (q % 4) * 4 * SSM_GROUP + gl * SSM_GROUP
    return q, slice(r0, r0 + SSM_STATE), slice(c0, c0 + SSM_GROUP)


def _c_block(g):
    q, rows, cols = _b_block(g)
    return q, cols, rows


def _ssm_expand(b_re, b_im, c_re, c_im):
    def body(bre_ref, bim_ref, cre_ref, cim_ref, *rest):
        outs, tmp = rest[:8], rest[8]
        for d in range(2):
            for j, (src, where) in enumerate(((bre_ref, _b_block), (bim_ref, _b_block),
                                              (cre_ref, _c_block), (cim_ref, _c_block))):
                tmp[...] = jnp.zeros_like(tmp)
                for g in range(N_SSM_GROUPS):
                    q, rows, cols = where(g)
                    tmp[q, rows, cols] = src[d, g]
                outs[4 * d + j][...] = tmp[...].astype(BF16)

    dense = jax.ShapeDtypeStruct((N_QUAD, QUAD, SLAB), BF16)
    return pl.pallas_call(body, name="ssm_expand", out_shape=[dense] * 8,
                          scratch_shapes=[pltpu.VMEM((N_QUAD, QUAD, SLAB), F32)],
                          compiler_params=pltpu.CompilerParams(vmem_limit_bytes=VMEM_LIMIT))(b_re, b_im, c_re, c_im)


def _scan_rows(src_re, src_im, dst_re, dst_im, tab_ref, k, carry_re, carry_im, rows, reverse, s_refs=None):
    ng = rows // 8
    edge = 0 if reverse else 7
    row_id = lax.broadcasted_iota(jnp.int32, (8, SCAN_W), 0)
    sums = []
    for lt in range(N_STATE // SCAN_W):
        sl = slice(lt * SCAN_W, (lt + 1) * SCAN_W)

        def step(r, c, sl=sl):
            tabs = [tab_ref[k, j, :, sl] for j in range(8)]
            cr, ci = c[0], c[1]
            row = pl.multiple_of((ng - 1 - r) * 8 if reverse else r * 8, 8)
            xr = src_re[pl.ds(row, 8), sl]
            xi = src_im[pl.ds(row, 8), sl]
            for s, sh in enumerate((1, 2, 4)):
                amt = 8 - sh if reverse else sh
                rr = pltpu.roll(xr, amt, 0)
                ri = pltpu.roll(xi, amt, 0)
                mr, mi = tabs[2 * s], tabs[2 * s + 1]
                xr, xi = xr + mr * rr - mi * ri, xi + mr * ri + mi * rr
            xr, xi = xr + tabs[6] * cr - tabs[7] * ci, xi + tabs[6] * ci + tabs[7] * cr
            dst_re[pl.ds(row, 8), sl] = xr
            dst_im[pl.ds(row, 8), sl] = xi
            ncr = jnp.broadcast_to(xr[edge:edge + 1, :], (8, SCAN_W))
            nci = jnp.broadcast_to(xi[edge:edge + 1, :], (8, SCAN_W))
            if s_refs is None:
                return ncr, nci
            amt = 7 if reverse else 1
            far = 7 if reverse else 0
            nr = jnp.where(row_id == far, cr, pltpu.roll(xr, amt, 0))
            ni = jnp.where(row_id == far, ci, pltpu.roll(xi, amt, 0))
            sr = s_refs[0][pl.ds(row, 8), sl]
            si = s_refs[1][pl.ds(row, 8), sl]
            return ncr, nci, c[2] + nr * sr + ni * si, c[3] + ni * sr - nr * si

        init = (carry_re[:, sl], carry_im[:, sl])
        if s_refs is not None:
            init = init + (jnp.zeros((8, SCAN_W), F32), jnp.zeros((8, SCAN_W), F32))
        out = lax.fori_loop(0, ng, step, init)
        carry_re[:, sl] = out[0]
        carry_im[:, sl] = out[1]
        if s_refs is not None:
            sums.append((jnp.sum(out[2], axis=0, keepdims=True), jnp.sum(out[3], axis=0, keepdims=True)))
    return sums


def _ssm_scan_fwd(u, b_re, b_im, c_re, c_im, f2, tables, k, reverse, comm=None):
    L = u.shape[0]
    nc = L // TC
    chunk = (lambda i: nc - 1 - i) if reverse else (lambda i: i)

    def body(u_ref, bre_ref, bim_ref, cre_ref, cim_ref, f_ref, tab_ref,
             y_ref, sre_ref, sim_ref, in_re, in_im, carry_re, carry_im):
        @pl.when(pl.program_id(0) == 0)
        def _():
            carry_re[...] = jnp.zeros_like(carry_re)
            carry_im[...] = jnp.zeros_like(carry_im)

        ub = u_ref[...].astype(BF16)
        for q in range(N_QUAD):
            qs = slice(q * QUAD, (q + 1) * QUAD)
            us = ub[:, (q // 4) * SLAB:(q // 4 + 1) * SLAB]
            bur = _dot_nt(us, bre_ref[q])
            bui = _dot_nt(us, bim_ref[q])
            fr = f_ref[0:1, qs]
            fi = f_ref[1:2, qs]
            in_re[:, qs] = fr * bur - fi * bui
            in_im[:, qs] = fr * bui + fi * bur
        _scan_rows(in_re, in_im, sre_ref, sim_ref, tab_ref, k, carry_re, carry_im, TC, reverse)
        for j in range(D_SSM // SLAB):
            acc = jnp.zeros((TC, SLAB), F32)
            for q in range(4 * j, 4 * j + 4):
                qs = slice(q * QUAD, (q + 1) * QUAD)
                acc = acc + _dot_nt(sre_ref[:, qs].astype(BF16), cre_ref[q])
                acc = acc - _dot_nt(sim_ref[:, qs].astype(BF16), cim_ref[q])
            y_ref[:, j * SLAB:(j + 1) * SLAB] = acc

    return _hosted_call(
        body, comm, name="ssm_scan_rev" if reverse else "ssm_scan_fwd", grid=(nc,),
        in_specs=[pl.BlockSpec((TC, D_SSM), lambda i: (chunk(i), 1))]
        + [_full(b_re.shape)] * 4 + [_full(f2.shape), _full(tables.shape)],
        out_specs=[pl.BlockSpec((TC, D_SSM), lambda i: (chunk(i), 0)),
                   pl.BlockSpec((TC, N_STATE), lambda i: (chunk(i), 0)),
                   pl.BlockSpec((TC, N_STATE), lambda i: (chunk(i), 0))],
        out_shape=[jax.ShapeDtypeStruct((L, D_SSM), F32), jax.ShapeDtypeStruct((L, N_STATE), F32),
                   jax.ShapeDtypeStruct((L, N_STATE), F32)],
        scratch_shapes=[pltpu.VMEM((TC, N_STATE), F32), pltpu.VMEM((TC, N_STATE), F32),
                        pltpu.VMEM((8, N_STATE), F32), pltpu.VMEM((8, N_STATE), F32)],
        args=(u, b_re, b_im, c_re, c_im, f2, tables))


def _quad_channels(q):
    c0 = (q // 4) * SLAB + (q % 4) * 4 * SSM_GROUP
    return slice(c0, c0 + 4 * SSM_GROUP)


def _ssm_scan_bwd(dy, u, s_re, s_im, b_re, b_im, c_re, c_im, f2, tables, k, reverse, comm=None):
    L = u.shape[0]
    nc = L // TC
    chunk = (lambda i: nc - 1 - i) if reverse else (lambda i: i)

    def body(dy_ref, u_ref, sre_ref, sim_ref, bre_ref, bim_ref, cre_ref, cim_ref, f_ref, tab_ref,
             du_ref, ob_re, ob_im, oc_re, oc_im, gv_ref,
             a_re, a_im, carry_re, carry_im, gbr_ref, gbi_ref, gcr_ref, gci_ref):
        @pl.when(pl.program_id(0) == 0)
        def _():
            carry_re[...] = jnp.zeros_like(carry_re)
            carry_im[...] = jnp.zeros_like(carry_im)
            for r in (gbr_ref, gbi_ref, gcr_ref, gci_ref, gv_ref):
                r[...] = jnp.zeros_like(r)

        dyb = dy_ref[...].astype(BF16)
        ub = u_ref[...].astype(BF16)
        for q in range(N_QUAD):
            qs = slice(q * QUAD, (q + 1) * QUAD)
            ds = dyb[:, (q // 4) * SLAB:(q // 4 + 1) * SLAB]
            a_re[:, qs] = _dot_nn(ds, cre_ref[q])
            a_im[:, qs] = -_dot_nn(ds, cim_ref[q])
            dq = dyb[:, _quad_channels(q)]
            gcr_ref[q] += _dot_tn(dq, sre_ref[:, qs].astype(BF16))
            gci_ref[q] -= _dot_tn(dq, sim_ref[:, qs].astype(BF16))
        sums = _scan_rows(a_re, a_im, a_re, a_im, tab_ref, k, carry_re, carry_im, TC, reverse,
                          s_refs=(sre_ref, sim_ref))
        for lt, (glr, gli) in enumerate(sums):
            sl = slice(lt * SCAN_W, (lt + 1) * SCAN_W)
            gv_ref[0:1, sl] += glr
            gv_ref[1:2, sl] += gli
        for j in range(D_SSM // SLAB):
            us = ub[:, j * SLAB:(j + 1) * SLAB]
            acc = jnp.zeros((TC, SLAB), F32)
            for q in range(4 * j, 4 * j + 4):
                qs = slice(q * QUAD, (q + 1) * QUAD)
                ar = a_re[:, qs]
                ai = a_im[:, qs]
                bur = _dot_nt(us, bre_ref[q])
                bui = _dot_nt(us, bim_ref[q])
                gv_ref[2:3, qs] += jnp.sum(ar * bur + ai * bui, axis=0, keepdims=True)
                gv_ref[3:4, qs] += jnp.sum(ai * bur - ar * bui, axis=0, keepdims=True)
                fr = f_ref[0:1, qs]
                fi = f_ref[1:2, qs]
                dbr = (fr * ar + fi * ai).astype(BF16)
                dbi = (fr * ai - fi * ar).astype(BF16)
                uq = ub[:, _quad_channels(q)]
                gbr_ref[q] += _dot_tn(uq, dbr)
                gbi_ref[q] += _dot_tn(uq, dbi)
                acc = acc + _dot_nn(dbr, bre_ref[q]) + _dot_nn(dbi, bim_ref[q])
            du_ref[:, j * SLAB:(j + 1) * SLAB] = acc

        @pl.when(pl.program_id(0) == nc - 1)
        def _():
            for g in range(N_SSM_GROUPS):
                q, gl = divmod(g, 4)
                rows = slice(gl * SSM_GROUP, (gl + 1) * SSM_GROUP)
                cols = slice(gl * SSM_STATE, (gl + 1) * SSM_STATE)
                for out, acc_ref in ((ob_re, gbr_ref), (ob_im, gbi_ref), (oc_re, gcr_ref), (oc_im, gci_ref)):
                    out[g] = acc_ref[q, rows, cols]

    gshape = jax.ShapeDtypeStruct((N_SSM_GROUPS, SSM_GROUP, SSM_STATE), F32)
    compact = pltpu.VMEM((N_QUAD, 4 * SSM_GROUP, QUAD), F32)
    return _hosted_call(
        body, comm, name="ssm_bwd_rev" if reverse else "ssm_bwd_fwd", grid=(nc,),
        in_specs=[pl.BlockSpec((TC, D_SSM), lambda i: (chunk(i), 0)),
                  pl.BlockSpec((TC, D_SSM), lambda i: (chunk(i), 1)),
                  pl.BlockSpec((TC, N_STATE), lambda i: (chunk(i), 0)),
                  pl.BlockSpec((TC, N_STATE), lambda i: (chunk(i), 0))]
        + [_full(b_re.shape)] * 4 + [_full(f2.shape), _full(tables.shape)],
        out_specs=[pl.BlockSpec((TC, D_SSM), lambda i: (chunk(i), 0))] + [_full(gshape.shape)] * 4
        + [_full((4, N_STATE))],
        out_shape=[jax.ShapeDtypeStruct((L, D_SSM), F32), gshape, gshape, gshape, gshape,
                   jax.ShapeDtypeStruct((4, N_STATE), F32)],
        scratch_shapes=[pltpu.VMEM((TC, N_STATE), F32), pltpu.VMEM((TC, N_STATE), F32),
                        pltpu.VMEM((8, N_STATE), F32), pltpu.VMEM((8, N_STATE), F32),
                        compact, compact, compact, compact],
        args=(dy, u, s_re, s_im, b_re, b_im, c_re, c_im, f2, tables))


def _ssm_post(yf, yb, u, d, glu_w, glu_b):
    y = yf + yb + d * u
    z, t = _gelu(y)
    zb = z.astype(BF16)
    gate = _sigmoid(_dot_nn(zb, glu_w) + glu_b)
    return y, z, t, zb, gate


def _mix_out(yn_pool, yf, yb, u, x, ssm_d, glu_w_b, glu_b, g_ssm, w_out_b, g_ffn):
    L, D = x.shape

    def body(ynp_ref, yf_ref, yb_ref, u_ref, x_ref, d_ref, gw_ref, gb_ref, gs_ref, wo_ref, gf_ref,
             h1_ref, hn_ref, ycat_ref):
        _, z, _, _, gate = _ssm_post(yf_ref[...], yb_ref[...], u_ref[...], d_ref[...], gw_ref[...], gb_ref[...])
        yns, _, _ = _rms_fwd(z * gate, gs_ref[...])
        ynsb = yns.astype(BF16)
        ynp = ynp_ref[...]
        ycat_ref[:, 0:D_POOL] = ynp
        ycat_ref[:, D_POOL:D] = ynsb
        h1 = x_ref[...] + _dot_nn(ynp, wo_ref[0:D_POOL, :]) + _dot_nn(ynsb, wo_ref[D_POOL:D, :])
        h1_ref[...] = h1
        hn, _, _ = _rms_fwd(h1, gf_ref[...])
        hn_ref[...] = hn.astype(BF16)

    half = lambda c: pl.BlockSpec((TL, D_SSM), lambda i: (i, c))
    row = pl.BlockSpec((TL, D), lambda i: (i, 0))
    return pl.pallas_call(
        body, name="mix_out", grid=(L // TL,),
        in_specs=[half(0), half(0), half(0), half(1), row, _full((1, D_SSM)), _full(glu_w_b.shape),
                  _full((1, D_SSM)), _full((1, D_SSM)), _full(w_out_b.shape), _full((1, D))],
        out_specs=[row, row, row],
        out_shape=[jax.ShapeDtypeStruct((L, D), F32), jax.ShapeDtypeStruct((L, D), BF16),
                   jax.ShapeDtypeStruct((L, D), BF16)],
        compiler_params=_cp("parallel"))(yn_pool, yf, yb, u, x, ssm_d, glu_w_b, glu_b, g_ssm, w_out_b, g_ffn)


def _ssm_bwd_local(dh1, yf, yb, u, ssm_d, glu_w_b, glu_b, g_ssm, w_out_b):
    L, D = dh1.shape

    def body(dh_ref, yf_ref, yb_ref, u_ref, d_ref, gw_ref, gb_ref, gs_ref, wo_ref,
             dy_ref, du_ref, ggw_ref, ggb_ref, gd_ref, ggs_ref):
        @pl.when(pl.program_id(0) == 0)
        def _():
            for r in (ggw_ref, ggb_ref, gd_ref, ggs_ref):
                r[...] = jnp.zeros_like(r)

        u = u_ref[...]
        d = d_ref[...]
        y, z, t, zb, gate = _ssm_post(yf_ref[...], yb_ref[...], u, d, gw_ref[...], gb_ref[...])
        gs = gs_ref[...]
        _, xh, inv = _rms_fwd(z * gate, gs)
        d_yn = _dot_nt(dh_ref[...].astype(BF16), wo_ref[...])
        d_o, dgs = _rms_bwd(d_yn, xh, inv, gs)
        ggs_ref[...] += dgs
        d_zg = d_o * z * gate * (1.0 - gate)
        d_zgb = d_zg.astype(BF16)
        ggb_ref[...] += jnp.sum(d_zg, axis=0, keepdims=True)
        ggw_ref[...] += _dot_tn(zb, d_zgb)
        d_z = d_o * gate + _dot_nt(d_zgb, gw_ref[...])
        d_y = d_z * _gelu_grad(y, t)
        gd_ref[...] += jnp.sum(d_y * u, axis=0, keepdims=True)
        dy_ref[...] = d_y
        du_ref[...] = d_y * d

    half = lambda c: pl.BlockSpec((TL, D_SSM), lambda i: (i, c))
    vec = _full((1, D_SSM))
    return pl.pallas_call(
        body, name="ssm_bwd_local", grid=(L // TL,),
        in_specs=[pl.BlockSpec((TL, D), lambda i: (i, 0)), half(0), half(0), half(1), vec, _full(glu_w_b.shape),
                  vec, vec, pl.BlockSpec((D_SSM, D), lambda i: (1, 0))],
        out_specs=[half(0), half(0), _full(glu_w_b.shape), vec, vec, vec],
        out_shape=[jax.ShapeDtypeStruct((L, D_SSM), F32), jax.ShapeDtypeStruct((L, D_SSM), F32),
                   jax.ShapeDtypeStruct(glu_w_b.shape, F32)] + [jax.ShapeDtypeStruct((1, D_SSM), F32)] * 3,
        compiler_params=_cp("arbitrary"))(dh1, yf, yb, u, ssm_d, glu_w_b, glu_b, g_ssm, w_out_b)


def _in_bwd(du_pool, du_a, du_b, du_c, dh1, x, g, w_in_b):
    L, D = x.shape

    def body(p_ref, a_ref, b_ref, c_ref, dh_ref, x_ref, g_ref, w_ref, dx_ref, dub_ref, gg_ref):
        @pl.when(pl.program_id(0) == 0)
        def _():
            gg_ref[...] = jnp.zeros_like(gg_ref)

        dub_ref[:, 0:D_POOL] = p_ref[...].astype(BF16)
        dub_ref[:, D_POOL:D] = (a_ref[...] + b_ref[...] + c_ref[...]).astype(BF16)
        d_xn = _dot_nt(dub_ref[...], w_ref[...])
        gv = g_ref[...]
        _, xh, inv = _rms_fwd(x_ref[...], gv)
        dx, dg = _rms_bwd(d_xn, xh, inv, gv)
        gg_ref[...] += dg
        dx_ref[...] = dh_ref[...] + dx

    half = pl.BlockSpec((TL, D_SSM), lambda i: (i, 0))
    row = pl.BlockSpec((TL, D), lambda i: (i, 0))
    return pl.pallas_call(
        body, name="in_bwd", grid=(L // TL,),
        in_specs=[half, half, half, half, row, row, _full((1, D)), _full(w_in_b.shape)],
        out_specs=[row, row, _full((1, D))],
        out_shape=[jax.ShapeDtypeStruct((L, D), F32), jax.ShapeDtypeStruct((L, D), BF16),
                   jax.ShapeDtypeStruct((1, D), F32)],
        compiler_params=_cp("arbitrary"))(du_pool, du_a, du_b, du_c, dh1, x, g, w_in_b)


def _ffn_up(hn, w_up4):
    L, D = hn.shape

    def body(h_ref, w_ref, o_ref):
        o_ref[...] = _dot_nn(h_ref[...], w_ref[...]).astype(BF16)

    return pl.pallas_call(
        body, name="ffn_up", grid=(4, L // TF),
        in_specs=[pl.BlockSpec((TF, D), lambda j, i: (i, 0)), pl.BlockSpec((None, D, FF_BLK), lambda j, i: (j, 0, 0))],
        out_specs=pl.BlockSpec((TF, FF_BLK), lambda j, i: (i, j)),
        out_shape=jax.ShapeDtypeStruct((L, 4 * FF_BLK), BF16),
        compiler_params=_cp("parallel", "parallel"))(hn, w_up4)


def _halo_specs_2d(rows, width, L, col, order):
    rb = rows // HALO_B
    last = L // HALO_B - 1
    if order == "ik":
        wrap = lambda f: (lambda i, k: f(i, k))
    else:
        wrap = lambda f: (lambda k, i: f(i, k))
    return [pl.BlockSpec((HALO_B, width), wrap(lambda i, k: (jnp.maximum(i * rb - 1, 0), col(k)))),
            pl.BlockSpec((rows, width), wrap(lambda i, k: (i, col(k)))),
            pl.BlockSpec((HALO_B, width), wrap(lambda i, k: (jnp.minimum((i + 1) * rb, last), col(k))))]


def _shift_mats(rows):
    r = lax.broadcasted_iota(jnp.int32, (rows, rows), 0)
    c = lax.broadcasted_iota(jnp.int32, (rows, rows), 1)
    return (c == r - 1).astype(BF16), (c == r + 1).astype(BF16)


def _neighbours(x, prev_ref, next_ref, cs, i, n, mats):
    rows = x.shape[0]
    row = lax.broadcasted_iota(jnp.int32, (rows, 1), 0)
    before = jnp.where(i > 0, prev_ref[:, cs].astype(F32)[HALO_B - 1:HALO_B, :], 0.0)
    after = jnp.where(i < n - 1, next_ref[:, cs].astype(F32)[0:1, :], 0.0)
    if mats is None:
        xf = x.astype(F32)
        down, up = pltpu.roll(xf, 1, 0), pltpu.roll(xf, rows - 1, 0)
    else:
        down, up = _dot_nn(mats[0], x), _dot_nn(mats[1], x)
    return jnp.where(row == 0, before, down), jnp.where(row == rows - 1, after, up)


def _conv3(x, before, after, w, b):
    return before * w[0:1, :] + x.astype(F32) * w[1:2, :] + after * w[2:3, :] + b


def _col_chunks(width, size=256):
    return [slice(c, min(c + size, width)) for c in range(0, width, size)]


def _ffn_down_loss(up, conv_w, conv_b, w_down_b, h1, target, g_final):
    L, D = h1.shape
    n = L // TF
    nk = D_FF // FF_BLK

    def body(vp, vc, vn, gp, gc, gn, wv_ref, wg_ref, bv_ref, bg_ref, wd_ref, h1_ref, t_ref, gf_ref,
             a_ref, dh2_ref, dh2b_ref, loss_ref, gg_ref, acc_ref):
        i = pl.program_id(0)
        k = pl.program_id(1)

        @pl.when((i == 0) & (k == 0))
        def _():
            loss_ref[...] = jnp.zeros_like(loss_ref)
            gg_ref[...] = jnp.zeros_like(gg_ref)

        @pl.when(k == 0)
        def _():
            acc_ref[...] = jnp.zeros_like(acc_ref)

        mats = _shift_mats(TF)
        for cs in _col_chunks(FF_BLK):
            xv, xg = vc[:, cs], gc[:, cs]
            val = _conv3(xv, *_neighbours(xv, vp, vn, cs, i, n, mats), wv_ref[:, cs], bv_ref[:, cs])
            gate = _conv3(xg, *_neighbours(xg, gp, gn, cs, i, n, mats), wg_ref[:, cs], bg_ref[:, cs])
            a_ref[:, cs] = (val * (gate * _sigmoid(gate))).astype(BF16)
        acc_ref[...] += _dot_nn(a_ref[...], wd_ref[...])

        @pl.when(k == nk - 1)
        def _():
            gf = gf_ref[...]
            y, xh, inv = _rms_fwd(h1_ref[...] + acc_ref[...], gf)
            diff = y - t_ref[...]
            part = 0.5 * jnp.sum(jnp.mean(diff * diff, axis=-1, keepdims=True), axis=0, keepdims=True)
            loss_ref[...] += jnp.broadcast_to(part, loss_ref.shape)
            dx, dg = _rms_bwd(diff * (1.0 / D), xh, inv, gf)
            gg_ref[...] += dg
            dh2_ref[...] = dx
            dh2b_ref[...] = dx.astype(BF16)

    row = pl.BlockSpec((TF, D), lambda i, k: (i, 0))
    cw = lambda off: pl.BlockSpec((3, FF_BLK), lambda i, k: (0, k + off))
    cb = lambda off: pl.BlockSpec((1, FF_BLK), lambda i, k: (0, k + off))
    return pl.pallas_call(
        body, name="ffn_down_loss", grid=(n, nk),
        in_specs=_halo_specs_2d(TF, FF_BLK, L, lambda k: k, "ik") + _halo_specs_2d(TF, FF_BLK, L, lambda k: k + nk, "ik")
        + [cw(0), cw(nk), cb(0), cb(nk), pl.BlockSpec((FF_BLK, D), lambda i, k: (k, 0)), row, row, _full((1, D))],
        out_specs=[pl.BlockSpec((TF, FF_BLK), lambda i, k: (i, k)), row, row, _full((1, LANES)), _full((1, D))],
        out_shape=[jax.ShapeDtypeStruct((L, D_FF), BF16), jax.ShapeDtypeStruct((L, D), F32),
                   jax.ShapeDtypeStruct((L, D), BF16), jax.ShapeDtypeStruct((1, LANES), F32),
                   jax.ShapeDtypeStruct((1, D), F32)],
        scratch_shapes=[pltpu.VMEM((TF, D), F32)],
        compiler_params=_cp("arbitrary", "arbitrary"))(
            up, up, up, up, up, up, conv_w, conv_w, conv_b, conv_b, w_down_b, h1, target, g_final)


def _ffn_act_bwd(up, conv_w, conv_b, w_down_b, dh2):
    L, D = dh2.shape
    n = L // TF
    nk = D_FF // FF_BLK

    def body(vp, vc, vn, gp, gc, gn, wv_ref, wg_ref, bv_ref, bg_ref, wd_ref, dh_ref,
             dv_ref, dg_ref, gcv_ref, gcg_ref):
        i = pl.program_id(1)

        @pl.when(i == 0)
        def _():
            gcv_ref[...] = jnp.zeros_like(gcv_ref)
            gcg_ref[...] = jnp.zeros_like(gcg_ref)

        mats = _shift_mats(TF)
        dh = dh_ref[...]
        for cs in _col_chunks(FF_BLK):
            xv, xg = vc[:, cs], gc[:, cs]
            v_rows = _neighbours(xv, vp, vn, cs, i, n, mats)
            g_rows = _neighbours(xg, gp, gn, cs, i, n, mats)
            val = _conv3(xv, *v_rows, wv_ref[:, cs], bv_ref[:, cs])
            gate = _conv3(xg, *g_rows, wg_ref[:, cs], bg_ref[:, cs])
            d_a = _dot_nt(dh, wd_ref[cs, :])
            sg = _sigmoid(gate)
            d_val = d_a * (gate * sg)
            d_gate = d_a * val * (sg * (1.0 + gate * (1.0 - sg)))
            dv_ref[:, cs] = d_val.astype(BF16)
            dg_ref[:, cs] = d_gate.astype(BF16)
            for d, x, (before, after), gref in ((d_val, xv, v_rows, gcv_ref), (d_gate, xg, g_rows, gcg_ref)):
                for j, shifted in enumerate((before, x.astype(F32), after)):
                    gref[j:j + 1, cs] += jnp.sum(d * shifted, axis=0, keepdims=True)
                gref[3:4, cs] += jnp.sum(d, axis=0, keepdims=True)

    cw = lambda off: pl.BlockSpec((3, FF_BLK), lambda k, i: (0, k + off))
    cb = lambda off: pl.BlockSpec((1, FF_BLK), lambda k, i: (0, k + off))
    blk = pl.BlockSpec((TF, FF_BLK), lambda k, i: (i, k))
    acc = pl.BlockSpec((4, FF_BLK), lambda k, i: (0, k))
    return pl.pallas_call(
        body, name="ffn_act_bwd", grid=(nk, n),
        in_specs=_halo_specs_2d(TF, FF_BLK, L, lambda k: k, "ki") + _halo_specs_2d(TF, FF_BLK, L, lambda k: k + nk, "ki")
        + [cw(0), cw(nk), cb(0), cb(nk), pl.BlockSpec((FF_BLK, D), lambda k, i: (k, 0)),
           pl.BlockSpec((TF, D), lambda k, i: (i, 0))],
        out_specs=[blk, blk, acc, acc],
        out_shape=[jax.ShapeDtypeStruct((L, D_FF), BF16), jax.ShapeDtypeStruct((L, D_FF), BF16),
                   jax.ShapeDtypeStruct((4, D_FF), F32), jax.ShapeDtypeStruct((4, D_FF), F32)],
        compiler_params=_cp("arbitrary", "arbitrary"))(
            up, up, up, up, up, up, conv_w, conv_w, conv_b, conv_b, w_down_b, dh2)


def _ffn_up_bwd(d_val, d_gate, conv_w, w_up4, h1, dh2, g_ffn):
    L, D = h1.shape
    n = L // TF
    nk = D_FF // FF_BLK

    def body(vp, vc, vn, gp, gc, gn, wv_ref, wg_ref, uv_ref, ug_ref, h1_ref, dh2_ref, g_ref,
             dup_ref, dh1_ref, dh1b_ref, gg_ref, acc_ref):
        i = pl.program_id(0)
        k = pl.program_id(1)

        @pl.when((i == 0) & (k == 0))
        def _():
            gg_ref[...] = jnp.zeros_like(gg_ref)

        @pl.when(k == 0)
        def _():
            acc_ref[...] = jnp.zeros_like(acc_ref)

        for j, (blocks, w_ref, wu_ref) in enumerate((((vp, vc, vn), wv_ref, uv_ref), ((gp, gc, gn), wg_ref, ug_ref))):
            for cs in _col_chunks(FF_BLK):
                d = blocks[1][:, cs]
                before, after = _neighbours(d, blocks[0], blocks[2], cs, i, n, None)
                w = w_ref[:, cs]
                dup_ref[j, :, cs] = (after * w[0:1, :] + d.astype(F32) * w[1:2, :] + before * w[2:3, :]).astype(BF16)
            acc_ref[...] += _dot_nt(dup_ref[j], wu_ref[...])

        @pl.when(k == nk - 1)
        def _():
            g = g_ref[...]
            _, xh, inv = _rms_fwd(h1_ref[...], g)
            dx, dg = _rms_bwd(acc_ref[...], xh, inv, g)
            gg_ref[...] += dg
            dh1 = dh2_ref[...] + dx
            dh1_ref[...] = dh1
            dh1b_ref[...] = dh1.astype(BF16)

    row = pl.BlockSpec((TF, D), lambda i, k: (i, 0))
    cw = lambda off: pl.BlockSpec((3, FF_BLK), lambda i, k: (0, k + off))
    wu = lambda off: pl.BlockSpec((None, D, FF_BLK), lambda i, k: (k + off, 0, 0))
    return pl.pallas_call(
        body, name="ffn_up_bwd", grid=(n, nk),
        in_specs=_halo_specs_2d(TF, FF_BLK, L, lambda k: k, "ik") + _halo_specs_2d(TF, FF_BLK, L, lambda k: k, "ik")
        + [cw(0), cw(nk), wu(0), wu(nk), row, row, _full((1, D))],
        out_specs=[pl.BlockSpec((2, None, TF, FF_BLK), lambda i, k: (0, k, i, 0)), row, row, _full((1, D))],
        out_shape=[jax.ShapeDtypeStruct((2, nk, L, FF_BLK), BF16), jax.ShapeDtypeStruct((L, D), F32),
                   jax.ShapeDtypeStruct((L, D), BF16), jax.ShapeDtypeStruct((1, D), F32)],
        scratch_shapes=[pltpu.VMEM((TF, D), F32)],
        compiler_params=_cp("arbitrary", "arbitrary"))(
            d_val, d_val, d_val, d_gate, d_gate, d_gate, conv_w, conv_w, w_up4, w_up4, h1, dh2, g_ffn)


def _matmul_tn(a, b, tm, tn, name, tk=2048):
    L, M = a.shape
    N = b.shape[1]
    tk = min(tk, L)

    def body(a_ref, b_ref, o_ref):
        @pl.when(pl.program_id(2) == 0)
        def _():
            o_ref[...] = jnp.zeros_like(o_ref)

        o_ref[...] += _dot_tn(a_ref[...], b_ref[...])

    return pl.pallas_call(
        body, name=name, grid=(M // tm, N // tn, L // tk),
        in_specs=[pl.BlockSpec((tk, tm), lambda m, n, l: (l, m)), pl.BlockSpec((tk, tn), lambda m, n, l: (l, n))],
        out_specs=pl.BlockSpec((tm, tn), lambda m, n, l: (m, n)),
        out_shape=jax.ShapeDtypeStruct((M, N), F32),
        compiler_params=_cp("parallel", "parallel", "arbitrary"))(a, b)


def _matmul_tn_blocks(a, b, tm, name, tk=2048):
    L, M = a.shape
    J, _, N = b.shape
    tk = min(tk, L)

    def body(a_ref, b_ref, o_ref):
        @pl.when(pl.program_id(2) == 0)
        def _():
            o_ref[...] = jnp.zeros_like(o_ref)

        o_ref[...] += _dot_tn(a_ref[...], b_ref[...])

    return pl.pallas_call(
        body, name=name, grid=(M // tm, J, L // tk),
        in_specs=[pl.BlockSpec((tk, tm), lambda m, j, l: (l, m)), pl.BlockSpec((None, tk, N), lambda m, j, l: (j, l, 0))],
        out_specs=pl.BlockSpec((None, tm, N), lambda m, j, l: (j, m, 0)),
        out_shape=jax.ShapeDtypeStruct((J, M, N), F32),
        compiler_params=_cp("parallel", "parallel", "arbitrary"))(a, b)


def _row_tile(rows):
    for t in (512, 352, 256, 128, 64, 8):
        if rows % t == 0:
            return t
    return rows


def _add_half(g, r, c_arr, name, out_dtype=F32):
    _, _, R, C = g.shape
    tr = _row_tile(R)

    def body(c_ref, g_ref, r_ref, o_ref):
        o_ref[...] = (g_ref[...] + r_ref[...]).astype(out_dtype)

    return pl.pallas_call(
        body, name=name,
        grid_spec=pltpu.PrefetchScalarGridSpec(
            num_scalar_prefetch=1, grid=(g.shape[0], R // tr),
            in_specs=[pl.BlockSpec((None, None, tr, C), lambda j, i, c: (j, c[0], i, 0)),
                      pl.BlockSpec((None, tr, C), lambda j, i, c: (j, i, 0))],
            out_specs=pl.BlockSpec((None, tr, C), lambda j, i, c: (j, i, 0))),
        out_shape=jax.ShapeDtypeStruct(r.shape, out_dtype),
        compiler_params=_cp("parallel", "parallel"))(c_arr, g, r)


def _add2(a, b, name):
    R, C = a.shape
    tr = _row_tile(R)

    def body(a_ref, b_ref, o_ref):
        o_ref[...] = a_ref[...] + b_ref[...]

    spec = pl.BlockSpec((tr, C), lambda i: (i, 0))
    return pl.pallas_call(body, name=name, grid=(R // tr,), in_specs=[spec, spec], out_specs=spec,
                          out_shape=jax.ShapeDtypeStruct(a.shape, F32), compiler_params=_cp("parallel"))(a, b)


def _sum4(p, name):
    _, R, C = p.shape
    tr = _row_tile(R)

    def body(p_ref, o_ref):
        q = [p_ref[j].astype(F32) for j in range(4)]
        o_ref[...] = ((q[0] + q[1]) + q[2]) + q[3]

    return pl.pallas_call(
        body, name=name, grid=(R // tr,),
        in_specs=[pl.BlockSpec((4, tr, C), lambda i: (0, i, 0))],
        out_specs=pl.BlockSpec((tr, C), lambda i: (i, 0)),
        out_shape=jax.ShapeDtypeStruct((R, C), F32), compiler_params=_cp("parallel"))(p)


def _adamw_refs(w_ref, g_ref, m_ref, v_ref, d_ref, nm_ref, nv_ref):
    gv = g_ref[...]
    nm = ADAM_B1 * m_ref[...] + (1.0 - ADAM_B1) * gv
    nv = ADAM_B2 * v_ref[...] + (1.0 - ADAM_B2) * (gv * gv)
    m_hat = nm / (1.0 - ADAM_B1 ** ADAM_STEP)
    v_hat = nv / (1.0 - ADAM_B2 ** ADAM_STEP)
    d_ref[...] = -ADAM_LR * (m_hat / (jnp.sqrt(v_hat) + ADAM_EPS) + ADAM_WD * w_ref[...])
    nm_ref[...] = nm
    nv_ref[...] = nv


def _adamw_many(ws, gs, ms, vs, name):
    n = len(ws)

    def body(*refs):
        for k in range(n):
            _adamw_refs(*(refs[j * n + k] for j in range(7)))

    out_shape = [jax.ShapeDtypeStruct(w.shape, F32) for w in ws] * 3
    res = pl.pallas_call(body, name=name, out_shape=out_shape,
                         compiler_params=pltpu.CompilerParams(vmem_limit_bytes=VMEM_LIMIT))(*ws, *gs, *ms, *vs)
    return res[:n], res[n:2 * n], res[2 * n:]


def _adamw(w, g, m, v, name):
    R, C = w.shape
    tr = _row_tile(R)
    body = lambda *refs: _adamw_refs(*refs)

    spec = pl.BlockSpec((tr, C), lambda i: (i, 0))
    sh = jax.ShapeDtypeStruct((R, C), F32)
    return pl.pallas_call(body, name=name, grid=(R // tr,), in_specs=[spec] * 4, out_specs=[spec] * 3,
                          out_shape=[sh] * 3, compiler_params=_cp("parallel"))(w, g, m, v)


def _join_rows(own, other, c_arr, name):
    R, C = own.shape
    tr = _row_tile(R)

    def body(c_ref, own_ref, other_ref, o_ref):
        o_ref[...] = jnp.where(pl.program_id(0) == c_ref[0], own_ref[...], other_ref[...])

    half = pl.BlockSpec((tr, C), lambda h, i, c: (i, 0))
    return pl.pallas_call(
        body, name=name,
        grid_spec=pltpu.PrefetchScalarGridSpec(
            num_scalar_prefetch=1, grid=(2, R // tr), in_specs=[half, half],
            out_specs=pl.BlockSpec((tr, C), lambda h, i, c: (h * (R // tr) + i, 0))),
        out_shape=jax.ShapeDtypeStruct((2 * R, C), F32),
        compiler_params=_cp("parallel", "parallel"))(c_arr, own, other)


def _adamw_halves(w, own, other, m, v, c_arr, name):
    R, C = own.shape
    tr = _row_tile(R)

    def body(c_ref, w_ref, own_ref, other_ref, m_ref, v_ref, g_ref, d_ref, nm_ref, nv_ref):
        g_ref[...] = jnp.where(pl.program_id(0) == c_ref[0], own_ref[...], other_ref[...])
        _adamw_refs(w_ref, g_ref, m_ref, v_ref, d_ref, nm_ref, nv_ref)

    half = pl.BlockSpec((tr, C), lambda h, i, c: (i, 0))
    full = pl.BlockSpec((tr, C), lambda h, i, c: (h * (R // tr) + i, 0))
    sh = jax.ShapeDtypeStruct((2 * R, C), F32)
    return pl.pallas_call(
        body, name=name,
        grid_spec=pltpu.PrefetchScalarGridSpec(
            num_scalar_prefetch=1, grid=(2, R // tr), in_specs=[full, half, half, full, full], out_specs=[full] * 4),
        out_shape=[sh] * 4, compiler_params=_cp("parallel", "parallel"))(c_arr, w, own, other, m, v)


_ANY = pl.BlockSpec(memory_space=pl.ANY)


def _position():
    return lax.axis_index("x"), lax.axis_index("y"), lax.axis_index("c")


class _Comm:
    def __init__(self, arrs, out_shape, sems, start, finish):
        self.arrs, self.out_shape, self.sems, self.start, self.finish = arrs, out_shape, sems, start, finish


def _comm_call(comm, name):
    n, m = len(comm.arrs), len(comm.out_shape)

    def body(*refs):
        ins, outs, sems = refs[:n], refs[n:n + m], refs[n + m:]
        comm.start(ins, outs, sems)
        comm.finish(ins, outs, sems)

    return pl.pallas_call(
        body, name=name, in_specs=[_ANY] * n, out_specs=[_ANY] * m, out_shape=comm.out_shape,
        scratch_shapes=comm.sems, compiler_params=pltpu.CompilerParams(has_side_effects=True))(*comm.arrs)


def _hosted_call(body, comm, *, name, grid, in_specs, out_specs, out_shape, scratch_shapes, args):
    sem = ("arbitrary",) * len(grid)
    if comm is None:
        return pl.pallas_call(body, name=name, grid=grid, in_specs=in_specs, out_specs=out_specs, out_shape=out_shape,
                              scratch_shapes=scratch_shapes, compiler_params=_cp(*sem))(*args), []
    n_in, n_out, n_scr = len(in_specs), len(out_specs), len(scratch_shapes)
    ci, co = len(comm.arrs), len(comm.out_shape)

    def full(*refs):
        ins, refs = refs[:n_in], refs[n_in:]
        cins, refs = refs[:ci], refs[ci:]
        outs, refs = refs[:n_out], refs[n_out:]
        couts, refs = refs[:co], refs[co:]
        scr, csems = refs[:n_scr], refs[n_scr:]
        first, last = True, True
        for d, size in enumerate(grid):
            first = first & (pl.program_id(d) == 0)
            last = last & (pl.program_id(d) == size - 1)

        @pl.when(first)
        def _():
            comm.start(cins, couts, csems)

        body(*ins, *outs, *scr)

        @pl.when(last)
        def _():
            comm.finish(cins, couts, csems)

    res = pl.pallas_call(
        full, name=name, grid=grid, in_specs=list(in_specs) + [_ANY] * ci, out_specs=list(out_specs) + [_ANY] * co,
        out_shape=list(out_shape) + list(comm.out_shape), scratch_shapes=list(scratch_shapes) + list(comm.sems),
        compiler_params=_cp(*sem))(*args, *comm.arrs)
    return res[:n_out], res[n_out:]


def _comm_join(*comms):
    def parts(xs, attr):
        out, at = [], 0
        for cm in comms:
            n = len(getattr(cm, attr))
            out.append(xs[at:at + n])
            at += n
        return out

    def start(ins, outs, sems):
        for cm, i, o, s in zip(comms, parts(ins, "arrs"), parts(outs, "out_shape"), parts(sems, "sems")):
            cm.start(i, o, s)

    def finish(ins, outs, sems):
        for cm, i, o, s in zip(comms, parts(ins, "arrs"), parts(outs, "out_shape"), parts(sems, "sems")):
            cm.finish(i, o, s)

    cat = lambda attr: [x for cm in comms for x in getattr(cm, attr)]
    return _Comm(cat("arrs"), cat("out_shape"), cat("sems"), start, finish)


def _dma_sems(*counts):
    return [pltpu.SemaphoreType.DMA((n,)) for n in counts]


def _comm_pair_swap(arrs, half=False):
    n = len(arrs)
    out_shape = [jax.ShapeDtypeStruct(a.shape[:1] + a.shape[2:] if half else a.shape, a.dtype) for a in arrs]

    def copies(ins, outs, sems):
        x, y, c = _position()
        return [pltpu.make_async_remote_copy(
            src_ref=ins[k].at[:, 1 - c] if half else ins[k], dst_ref=outs[k], send_sem=sems[0].at[k],
            recv_sem=sems[1].at[k], device_id=(x, y, 1 - c), device_id_type=MESH) for k in range(n)]

    def start(ins, outs, sems):
        for cp in copies(ins, outs, sems):
            cp.start()

    def finish(ins, outs, sems):
        for cp in copies(ins, outs, sems):
            cp.wait()

    return _Comm(arrs, out_shape, _dma_sems(n, n), start, finish)


def _chip_of(j, c):
    return (jnp.right_shift(j, 1), jnp.bitwise_and(j, 1), c)


def _comm_chip_exchange(arrs, scatter):
    n = len(arrs)
    out_shape = [jax.ShapeDtypeStruct(a.shape if scatter else (4,) + a.shape, a.dtype) for a in arrs]

    def copies(ins, outs, sems):
        x, y, c = _position()
        me = 2 * x + y
        local, sent, landed = [], [], []
        for k in range(n):
            local.append(pltpu.make_async_copy(ins[k].at[me] if scatter else ins[k], outs[k].at[me], sems[2].at[k]))
            for d in (1, 2, 3):
                j = jnp.bitwise_xor(me, d)
                s = 3 * k + d - 1
                src = ins[k].at[j] if scatter else ins[k]
                for dst, group in ((outs[k].at[me], sent), (outs[k].at[j], landed)):
                    group.append(pltpu.make_async_remote_copy(
                        src_ref=src, dst_ref=dst, send_sem=sems[0].at[s], recv_sem=sems[1].at[s],
                        device_id=_chip_of(j, c), device_id_type=MESH))
        return local, sent, landed

    def start(ins, outs, sems):
        local, sent, _ = copies(ins, outs, sems)
        for cp in local + sent:
            cp.start()

    def finish(ins, outs, sems):
        local, sent, landed = copies(ins, outs, sems)
        for cp in sent:
            cp.wait_send()
        for cp in landed:
            cp.wait_recv()
        for cp in local:
            cp.wait()

    return _Comm(arrs, out_shape, _dma_sems(3 * n, 3 * n, n), start, finish)


def _comm_pair_gather(arrs):
    n = len(arrs)
    out_shape = [jax.ShapeDtypeStruct((2,) + a.shape, a.dtype) for a in arrs]

    def copies(ins, outs, sems):
        x, y, c = _position()
        local, sent, landed = [], [], []
        for k in range(n):
            local.append(pltpu.make_async_copy(ins[k], outs[k].at[c], sems[2].at[k]))
            for dst, group in ((outs[k].at[c], sent), (outs[k].at[1 - c], landed)):
                group.append(pltpu.make_async_remote_copy(
                    src_ref=ins[k], dst_ref=dst, send_sem=sems[0].at[k], recv_sem=sems[1].at[k],
                    device_id=(x, y, 1 - c), device_id_type=MESH))
        return local, sent, landed

    def start(ins, outs, sems):
        local, sent, _ = copies(ins, outs, sems)
        for cp in local + sent:
            cp.start()

    def finish(ins, outs, sems):
        local, sent, landed = copies(ins, outs, sems)
        for cp in sent:
            cp.wait_send()
        for cp in landed:
            cp.wait_recv()
        for cp in local:
            cp.wait()

    return _Comm(arrs, out_shape, _dma_sems(n, n, n), start, finish)


def _comm_gather_split(shards, whole):
    n, nw = len(shards), len(whole)
    arrs = list(shards) + list(whole)
    out_shape = [jax.ShapeDtypeStruct((4,) + a.shape, a.dtype) for a in arrs]

    def copies(ins, outs, sems):
        x, y, c = _position()
        me = 2 * x + y
        local, sent, landed, passed, passed_in = [], [], [], [], []
        for k in range(n + nw):
            local.append(pltpu.make_async_copy(ins[k], outs[k].at[me], sems[4].at[k]))
            for d in (1, 2, 3):
                j = jnp.bitwise_xor(me, d)
                s = 3 * k + d - 1
                if k >= n:
                    src, mine, theirs = ins[k], outs[k].at[me], outs[k].at[j]
                else:
                    h = shards[k].shape[0] // 2
                    rows = pl.ds(pl.multiple_of(c * h, 16), h)
                    other = pl.ds(pl.multiple_of((1 - c) * h, 16), h)
                    src, mine, theirs = ins[k].at[rows], outs[k].at[me, rows], outs[k].at[j, rows]
                    for dst, group in ((theirs, passed), (outs[k].at[j, other], passed_in)):
                        group.append(pltpu.make_async_remote_copy(
                            src_ref=theirs, dst_ref=dst, send_sem=sems[2].at[s], recv_sem=sems[3].at[s],
                            device_id=(x, y, 1 - c), device_id_type=MESH))
                for dst, group in ((mine, sent), (theirs, landed)):
                    group.append(pltpu.make_async_remote_copy(
                        src_ref=src, dst_ref=dst, send_sem=sems[0].at[s], recv_sem=sems[1].at[s],
                        device_id=_chip_of(j, c), device_id_type=MESH))
        return local, sent, landed, passed, passed_in

    def start(ins, outs, sems):
        local, sent, _, _, _ = copies(ins, outs, sems)
        for cp in local + sent:
            cp.start()

    def finish(ins, outs, sems):
        local, sent, landed, passed, passed_in = copies(ins, outs, sems)
        for cp in landed[:3 * n]:
            cp.wait_recv()
        for cp in passed:
            cp.start()
        for cp in landed[3 * n:]:
            cp.wait_recv()
        for cp in sent:
            cp.wait_send()
        for cp in passed:
            cp.wait_send()
        for cp in passed_in:
            cp.wait_recv()
        for cp in local:
            cp.wait()

    t = 3 * (n + nw)
    return _Comm(arrs, out_shape, _dma_sems(t, t, max(3 * n, 1), max(3 * n, 1), n + nw), start, finish)


def _pack(arrs, row_multiple):
    parts = []
    for a in arrs:
        flat = a.reshape(-1).astype(F32)
        pad = (-flat.shape[0]) % LANES
        parts.append(jnp.pad(flat, (0, pad)) if pad else flat)
    flat = jnp.concatenate(parts)
    rows = -(-flat.shape[0] // LANES)
    rows_p = -(-rows // row_multiple) * row_multiple
    return jnp.pad(flat, (0, rows_p * LANES - flat.shape[0])).reshape(rows_p, LANES)


def _unpack(packed, shapes):
    flat = packed.reshape(-1)
    outs, off = [], 0
    for sh in shapes:
        size = int(np.prod(sh))
        outs.append(flat[off:off + size].reshape(sh))
        off += size + (-size) % LANES
    return outs


SMALL = ["norm_mix_g", "pool_w", "pool_scale", "ssm_log_neg_a_re", "ssm_a_im", "ssm_log_dt", "ssm_b_re", "ssm_b_im",
         "ssm_c_re", "ssm_c_im", "ssm_d", "glu_b", "out_norm_pool_g", "out_norm_ssm_g", "norm_ffn_g", "conv_b",
         "final_norm_g"]
BIG = ["w_in", "glu_w", "w_out", "w_up", "w_down"]
WIDE = ["pool_w", "ssm_b_re", "ssm_b_im", "ssm_c_re", "ssm_c_im"]
WEIGHTS = ['norm_mix_g', 'w_in', 'pool_w', 'pool_scale', 'ssm_log_neg_a_re', 'ssm_a_im', 'ssm_log_dt', 'ssm_b_re',
           'ssm_b_im', 'ssm_c_re', 'ssm_c_im', 'ssm_d', 'glu_w', 'glu_b', 'out_norm_pool_g', 'out_norm_ssm_g', 'w_out',
           'norm_ffn_g', 'w_up', 'conv_w', 'conv_b', 'w_down', 'final_norm_g']


def _local_step(x, target, p, full, shards=None, c_arr=None):
    L, D = x.shape
    dist = shards is not None
    row = lambda a: a.reshape(1, -1)
    w_in = full["w_in"]
    pool_w_b = p["pool_w"].astype(BF16)
    g_mix, g_pool, g_ssm, g_ffn, g_fin = (row(p[k]) for k in (
        "norm_mix_g", "out_norm_pool_g", "out_norm_ssm_g", "norm_ffn_g", "final_norm_g"))
    pool_scale, ssm_d, glu_b, conv_b = (row(p[k]) for k in ("pool_scale", "ssm_d", "glu_b", "conv_b"))

    lnar = p["ssm_log_neg_a_re"].reshape(2 * N_SSM_GROUPS, SSM_STATE)
    aim = p["ssm_a_im"].reshape(2 * N_SSM_GROUPS, SSM_STATE)
    ldt = jnp.broadcast_to(p["ssm_log_dt"].reshape(2 * N_SSM_GROUPS, 1), lnar.shape)
    lam_re, lam_im, f_re, f_im = _ssm_params(lnar, aim, ldt)
    flat2 = lambda a: a.reshape(2, N_STATE)
    lam4 = jnp.stack([flat2(lam_re)[0], flat2(lam_im)[0], flat2(lam_re)[1], flat2(lam_im)[1]])
    tables = _scan_tables(lam4)
    f2 = [jnp.stack([flat2(f_re)[d], flat2(f_im)[d]]) for d in range(2)]
    dense = _ssm_expand(p["ssm_b_re"], p["ssm_b_im"], p["ssm_c_re"], p["ssm_c_im"])
    ssm_args = [tuple(dense[4 * d:4 * d + 4]) + (f2[d], tables) for d in range(2)]

    u, xn = _in_proj(x, g_mix, w_in)
    yn_pool = _pool_fwd(u, pool_w_b, pool_scale, g_pool)
    gather1 = _comm_gather_split([shards[k] for k in ("glu_w", "w_out", "w_down")], [shards["conv_w"]]) if dist else None
    (y0, s0r, s0i), got1 = _ssm_scan_fwd(u, *ssm_args[0], 0, False, comm=gather1)
    gather2 = _comm_gather_split([shards["w_up"]], []) if dist else None
    (y1, s1r, s1i), got2 = _ssm_scan_fwd(u, *ssm_args[1], 2, True, comm=gather2)
    if dist:
        glu_w, w_out, w_down = (g.reshape((-1,) + g.shape[2:]) for g in got1[:3])
        conv_w = jnp.transpose(got1[3], (1, 0, 2)).reshape(3, -1)
        w_up4 = got2[0]
    else:
        glu_w, w_out, w_up4, w_down, conv_w = (full[k] for k in ("glu_w", "w_out", "w_up", "w_down", "conv_w"))
    h1, hn, ycat = _mix_out(yn_pool, y0, y1, u, x, ssm_d, glu_w, glu_b, g_ssm, w_out, g_ffn)
    up = _ffn_up(hn, w_up4)
    a, dh2, dh2_b, loss, g_final = _ffn_down_loss(up, conv_w, conv_b, w_down, h1, target, g_fin)

    d_val, d_gate, gcv, gcg = _ffn_act_bwd(up, conv_w, conv_b, w_down, dh2_b)
    g_w_down = _matmul_tn(a, dh2_b, FF_BLK, D, "grad_w_down")
    d_up, dh1, dh1_b, g_ffn_g = _ffn_up_bwd(d_val, d_gate, conv_w, w_up4, h1, dh2, g_ffn)
    g_w_up = _matmul_tn_blocks(hn, d_up.reshape(4, L, FF_BLK), 512, "grad_w_up")
    g_w_out = _matmul_tn(ycat, dh1_b, 512, D, "grad_w_out")
    dy, du_direct, g_glu_w, g_glu_b, g_ssm_d, g_ssm_g = _ssm_bwd_local(dh1_b, y0, y1, u, ssm_d, glu_w, glu_b, g_ssm, w_out)
    late = ("w_up", "w_down", "w_out", "glu_w")
    halves = [g_w_up.reshape(4, 2, D // 2, FF_BLK), g_w_down.reshape(4, 2, D_FF // 8, D),
              g_w_out.reshape(4, 2, D // 8, D), g_glu_w.reshape(4, 2, D_SSM // 8, D_SSM)]
    (d_pooled, g_pool_w, g_pool_scale, g_pool_g), from_sibling = _pool_bwd_local(
        dh1_b, u, w_out, pool_w_b, pool_scale, g_pool, comm=_comm_pair_swap(halves, half=True) if dist else None)
    du_pool = _pool_bwd_window(d_pooled)
    reduce2 = None
    if dist:
        chip_sums = [_add_half(h, r, c_arr, "sum_pair_" + k, BF16) for k, h, r in zip(late, halves, from_sibling)]
        reduce2 = _comm_chip_exchange(chip_sums, scatter=True)
    (du0, gb0r, gb0i, gc0r, gc0i, gv0), from_chips = _ssm_scan_bwd(dy, u, s0r, s0i, *ssm_args[0], 1, True, comm=reduce2)
    reduce3 = _comm_pair_gather([_sum4(r, "sum_chips_" + k) for k, r in zip(late, from_chips)]) if dist else None
    (du1, gb1r, gb1i, gc1r, gc1i, gv1), shards_out = _ssm_scan_bwd(dy, u, s1r, s1i, *ssm_args[1], 3, False, comm=reduce3)
    gvec = lambda j: jnp.stack([gv0[j], gv1[j]]).reshape(2 * N_SSM_GROUPS, SSM_STATE)
    g_lnar, g_aim, g_ldt = _ssm_params_bwd(lnar, aim, ldt, gvec(0), gvec(1), gvec(2), gvec(3))
    grad_x, d_u_b, g_mix_g = _in_bwd(du_pool, du_direct, du0, du1, dh1, x, g_mix, w_in)
    g_w_in = _matmul_tn(xn, d_u_b, 512, D, "grad_w_in")

    small = {
        "norm_mix_g": g_mix_g, "pool_w": g_pool_w, "pool_scale": g_pool_scale,
        "ssm_log_neg_a_re": g_lnar, "ssm_a_im": g_aim, "ssm_log_dt": g_ldt,
        "ssm_b_re": jnp.swapaxes(jnp.stack([gb0r, gb1r]), 2, 3), "ssm_b_im": jnp.swapaxes(jnp.stack([gb0i, gb1i]), 2, 3),
        "ssm_c_re": jnp.stack([gc0r, gc1r]), "ssm_c_im": jnp.stack([gc0i, gc1i]),
        "ssm_d": g_ssm_d, "glu_b": g_glu_b, "out_norm_pool_g": g_pool_g, "out_norm_ssm_g": g_ssm_g,
        "norm_ffn_g": g_ffn_g, "conv_b": jnp.concatenate([gcv[3], gcg[3]]), "final_norm_g": g_final,
        "conv_w": jnp.concatenate([gcv[0:3], gcg[0:3]], axis=1),
    }
    big = {"w_in": g_w_in}
    reduced = dict(zip(late, shards_out))
    if not dist:
        big.update({"w_up": g_w_up, "w_down": g_w_down, "w_out": g_w_out, "glu_w": g_glu_w})
    return loss, grad_x, small, big, reduced


def kernel(x, norm_mix_g, w_in, pool_w, pool_scale, ssm_log_neg_a_re, ssm_a_im, ssm_log_dt, ssm_b_re, ssm_b_im, ssm_c_re, ssm_c_im, ssm_d, glu_w, glu_b, out_norm_pool_g, out_norm_ssm_g, w_out, norm_ffn_g, w_up, conv_w, conv_b, w_down, final_norm_g, loss_target, m_norm_mix_g, m_w_in, m_pool_w, m_pool_scale, m_ssm_log_neg_a_re, m_ssm_a_im, m_ssm_log_dt, m_ssm_b_re, m_ssm_b_im, m_ssm_c_re, m_ssm_c_im, m_ssm_d, m_glu_w, m_glu_b, m_out_norm_pool_g, m_out_norm_ssm_g, m_w_out, m_norm_ffn_g, m_w_up, m_conv_w, m_conv_b, m_w_down, m_final_norm_g, v_norm_mix_g, v_w_in, v_pool_w, v_pool_scale, v_ssm_log_neg_a_re, v_ssm_a_im, v_ssm_log_dt, v_ssm_b_re, v_ssm_b_im, v_ssm_c_re, v_ssm_c_im, v_ssm_d, v_glu_w, v_glu_b, v_out_norm_pool_g, v_out_norm_ssm_g, v_w_out, v_norm_ffn_g, v_w_up, v_conv_w, v_conv_b, v_w_down, v_final_norm_g):
    args = locals()
    w = {k: args[k] for k in WEIGHTS}
    m = {k: args["m_" + k] for k in WEIGHTS}
    v = {k: args["v_" + k] for k in WEIGHTS}
    chip = 2 * lax.axis_index("x") + lax.axis_index("y")
    c_arr = lax.axis_index("c").astype(jnp.int32).reshape(1)

    shards = {k: w[k].astype(BF16) for k in BIG}
    shards["conv_w"] = conv_w
    w_in_full = _comm_call(_comm_chip_exchange([shards["w_in"]], scatter=False), "gather_w_in")[0]
    loss, grad_x, g_small, g_big, reduced = _local_step(
        x[0], loss_target[0], w, {"w_in": w_in_full.reshape(-1, w_in_full.shape[-1])}, shards, c_arr)

    exact = [k for k in SMALL if k not in WIDE]
    packs = [_pack([loss] + [g_small[k] for k in exact] + [g_small["conv_w"]], 512),
             _pack([g_small[k] for k in WIDE], 512)]
    halves = [g_big["w_in"].reshape(4, 2, g_big["w_in"].shape[0] // 8, -1)]
    halves += [pk.reshape(1, 2, pk.shape[0] // 2, LANES) for pk in packs]
    from_sibling = _comm_call(_comm_pair_swap(halves, half=True), "reduce_pair")
    names = ("w_in", "exact", "wide")
    sums = [_add_half(h, r, c_arr, "sum_pair_" + k, dt)
            for k, h, r, dt in zip(names, halves, from_sibling, (BF16, F32, BF16))]
    from_chips = _comm_call(_comm_join(_comm_chip_exchange(sums[:1], scatter=True),
                                       _comm_chip_exchange([s[0] for s in sums[1:]], scatter=False)), "reduce_chips")
    mine = [_sum4(r, "sum_chips_" + k) for k, r in zip(names, from_chips)]
    theirs = _comm_call(_comm_pair_swap(mine), "swap_halves")
    grads = {k: s.reshape(w[k].shape) for k, s in reduced.items()}
    exact_all = _join_rows(mine[1], theirs[1], c_arr, "join_exact")
    wide_all = _join_rows(mine[2], theirs[2], c_arr, "join_wide")
    shapes = [loss.shape] + [w[k].shape for k in exact] + [(3, 4 * FF_BLK)]
    grads.update(zip(["loss"] + exact + ["conv_w_full"], _unpack(exact_all, shapes)))
    grads.update(zip(WIDE, _unpack(wide_all, [w[k].shape for k in WIDE])))
    loss = grads.pop("loss")[0, 0]
    grads["conv_w"] = lax.dynamic_slice_in_dim(grads.pop("conv_w_full"), chip * FF_BLK, FF_BLK, axis=1)

    delta, new_m, new_v = {}, {}, {}
    for k in reduced:
        delta[k], new_m[k], new_v[k] = _adamw(w[k], grads[k], m[k], v[k], "adamw_" + k)
    grads["w_in"], delta["w_in"], new_m["w_in"], new_v["w_in"] = _adamw_halves(
        w["w_in"], mine[0], theirs[0], m["w_in"], v["w_in"], c_arr, "adamw_w_in")
    padded = ["ssm_b_re", "ssm_b_im"]
    for keys, name in ((padded, "adamw_ssm_b"), ([k for k in SMALL + ["conv_w"] if k not in padded], "adamw_small")):
        outs = _adamw_many(*([d[k] for k in keys] for d in (w, grads, m, v)), name)
        for d, o in zip((delta, new_m, new_v), outs):
            d.update(zip(keys, o))

    return (loss, grad_x[None], *[grads[k] for k in WEIGHTS], *[delta[k] for k in WEIGHTS],
            *[new_m[k] for k in WEIGHTS], *[new_v[k] for k in WEIGHTS])
```

```python
import numpy as np
import jax
import jax.numpy as jnp
from jax import lax
from jax.experimental import pallas as pl
from jax.experimental.pallas import tpu as pltpu

F32 = jnp.float32
BF16 = jnp.bfloat16
MESH = pl.DeviceIdType.MESH

EPS = 1e-6
POOL_WINDOWS = (2, 4, 8, 16)
POOL_GROUP = 128
SSM_GROUP = 16
SSM_STATE = 64
N_SSM_GROUPS = 32
N_STATE = N_SSM_GROUPS * SSM_STATE
QUAD = 256
N_QUAD = N_STATE // QUAD
SLAB = 256
D_SSM = 512
D_POOL = 512
D_FF = 2816
FF_BLK = 1408
HALO = 8
HALO_B = 16
LANES = 128
ADAM_LR, ADAM_B1, ADAM_B2, ADAM_EPS, ADAM_WD, ADAM_STEP = 0.001, 0.9, 0.999, 1e-08, 0.01, 10
VMEM_LIMIT = 56 * 2 ** 20

TL = 512
TF = 256
TC = 256
SCAN_W = 512


def _cp(*sem):
    return pltpu.CompilerParams(dimension_semantics=sem, vmem_limit_bytes=VMEM_LIMIT)


def _dot_nn(a, b):
    return jnp.dot(a, b, preferred_element_type=F32)


def _dot_nt(a, b):
    return lax.dot_general(a, b, (((1,), (1,)), ((), ())), preferred_element_type=F32)


def _dot_tn(a, b):
    return lax.dot_general(a, b, (((0,), (0,)), ((), ())), preferred_element_type=F32)


def _rms_fwd(x, g):
    inv = lax.rsqrt(jnp.mean(x * x, axis=-1, keepdims=True) + EPS)
    xh = x * inv
    return xh * g, xh, inv


def _rms_bwd(dy, xh, inv, g):
    dg = jnp.sum(dy * xh, axis=0, keepdims=True)
    dxh = dy * g
    dx = inv * (dxh - xh * jnp.mean(dxh * xh, axis=-1, keepdims=True))
    return dx, dg


_GELU_C = 0.7978845608028654
_GELU_A = 0.044715


def _gelu(y):
    t = jnp.tanh(_GELU_C * (y + _GELU_A * (y * y * y)))
    return 0.5 * y * (1.0 + t), t


def _gelu_grad(y, t):
    return 0.5 * (1.0 + t) + 0.5 * y * (1.0 - t * t) * (_GELU_C * (1.0 + 3.0 * _GELU_A * y * y))


def _sigmoid(x):
    return 1.0 / (1.0 + jnp.exp(-x))


def _full(shape):
    n = len(shape)
    return pl.BlockSpec(shape, lambda *_: (0,) * n)


def _fill_ext(ext_ref, prev_ref, cur_ref, next_ref, i, n, rows):
    ext_ref[0:HALO, :] = jnp.where(i > 0, prev_ref[...], 0.0).astype(ext_ref.dtype)
    ext_ref[HALO:HALO + rows, :] = cur_ref[...]
    ext_ref[HALO + rows:2 * HALO + rows, :] = jnp.where(i < n - 1, next_ref[...], 0.0).astype(ext_ref.dtype)


def _in_proj(x, g, w):
    L, D = x.shape
    E = w.shape[1]

    def body(x_ref, g_ref, w_ref, u_ref, xn_ref):
        y, _, _ = _rms_fwd(x_ref[...], g_ref[...])
        yb = y.astype(BF16)
        xn_ref[...] = yb
        u_ref[...] = _dot_nn(yb, w_ref[...])

    return pl.pallas_call(
        body, name="in_proj", grid=(L // TL,),
        in_specs=[pl.BlockSpec((TL, D), lambda i: (i, 0)), _full((1, D)), _full(w.shape)],
        out_specs=[pl.BlockSpec((TL, E), lambda i: (i, 0)), pl.BlockSpec((TL, D), lambda i: (i, 0))],
        out_shape=[jax.ShapeDtypeStruct((L, E), F32), jax.ShapeDtypeStruct((L, D), BF16)],
        compiler_params=_cp("parallel"))(x, g, w)


def _halo_specs_1d(rows, width, L, col):
    rb = rows // HALO
    last = L // HALO - 1
    return [pl.BlockSpec((HALO, width), lambda i: (jnp.maximum(i * rb - 1, 0), col)),
            pl.BlockSpec((rows, width), lambda i: (i, col)),
            pl.BlockSpec((HALO, width), lambda i: (jnp.minimum((i + 1) * rb, last), col))]


def _pooled_from_ext(ext_ref, t0, rows, L):
    t = t0 + lax.broadcasted_iota(jnp.int32, (rows, 1), 0)
    outs = []
    for gi, w in enumerate(POOL_WINDOWS):
        half = w // 2
        cs = slice(gi * POOL_GROUP, (gi + 1) * POOL_GROUP)
        acc = ext_ref[pl.ds(HALO - half, rows), cs]
        for s in range(-half + 1, half):
            acc = acc + ext_ref[pl.ds(HALO + s, rows), cs]
        cnt = (jnp.minimum(t + half, L) - jnp.maximum(t - half, 0)).astype(F32)
        outs.append(acc / cnt - ext_ref[pl.ds(HALO, rows), cs])
    return outs


def _pool_fwd(u, pool_w_b, pool_scale, g_pool):
    L = u.shape[0]
    n = L // TL

    def body(prev_ref, cur_ref, next_ref, pw_ref, ps_ref, g_ref, out_ref, ext_ref):
        i = pl.program_id(0)
        _fill_ext(ext_ref, prev_ref, cur_ref, next_ref, i, n, TL)
        pooled = _pooled_from_ext(ext_ref, i * TL, TL, L)
        ypre = jnp.concatenate([_dot_nn(pooled[gi].astype(BF16), pw_ref[gi]) for gi in range(4)], axis=-1)
        yn, _, _ = _rms_fwd(ypre * ps_ref[...], g_ref[...])
        out_ref[...] = yn.astype(BF16)

    return pl.pallas_call(
        body, name="pool_fwd", grid=(n,),
        in_specs=_halo_specs_1d(TL, D_POOL, L, 0) + [_full(pool_w_b.shape), _full((1, D_POOL)), _full((1, D_POOL))],
        out_specs=pl.BlockSpec((TL, D_POOL), lambda i: (i, 0)),
        out_shape=jax.ShapeDtypeStruct((L, D_POOL), BF16),
        scratch_shapes=[pltpu.VMEM((TL + 2 * HALO, D_POOL), F32)],
        compiler_params=_cp("parallel"))(u, u, u, pool_w_b, pool_scale, g_pool)


def _pool_bwd_local(dh1, u, w_out_b, pool_w_b, pool_scale, g_pool, comm=None):
    L = u.shape[0]
    n = L // TL
    D = dh1.shape[1]

    def body(dh_ref, prev_ref, cur_ref, next_ref, wo_ref, pw_ref, ps_ref, g_ref,
             dp_ref, gpw_ref, gps_ref, gg_ref, ext_ref):
        i = pl.program_id(0)

        @pl.when(i == 0)
        def _():
            gpw_ref[...] = jnp.zeros_like(gpw_ref)
            gps_ref[...] = jnp.zeros_like(gps_ref)
            gg_ref[...] = jnp.zeros_like(gg_ref)

        _fill_ext(ext_ref, prev_ref, cur_ref, next_ref, i, n, TL)
        pooled = [p.astype(BF16) for p in _pooled_from_ext(ext_ref, i * TL, TL, L)]
        ypre = jnp.concatenate([_dot_nn(pooled[gi], pw_ref[gi]) for gi in range(4)], axis=-1)
        ps = ps_ref[...]
        g = g_ref[...]
        _, xh, inv = _rms_fwd(ypre * ps, g)
        d_yn = _dot_nt(dh_ref[...].astype(BF16), wo_ref[...])
        d_y, dg = _rms_bwd(d_yn, xh, inv, g)
        gg_ref[...] += dg
        gps_ref[...] += jnp.sum(d_y * ypre, axis=0, keepdims=True)
        d_ypre = (d_y * ps).astype(BF16)
        for gi in range(4):
            cs = slice(gi * POOL_GROUP, (gi + 1) * POOL_GROUP)
            dp_ref[:, cs] = _dot_nt(d_ypre[:, cs], pw_ref[gi])
            gpw_ref[gi] += _dot_tn(pooled[gi], d_ypre[:, cs])

    return _hosted_call(
        body, comm, name="pool_bwd_local", grid=(n,),
        in_specs=[pl.BlockSpec((TL, D), lambda i: (i, 0))] + _halo_specs_1d(TL, D_POOL, L, 0)
        + [pl.BlockSpec((D_POOL, D), lambda i: (0, 0)), _full(pool_w_b.shape), _full((1, D_POOL)), _full((1, D_POOL))],
        out_specs=[pl.BlockSpec((TL, D_POOL), lambda i: (i, 0)), _full(pool_w_b.shape),
                   _full((1, D_POOL)), _full((1, D_POOL))],
        out_shape=[jax.ShapeDtypeStruct((L, D_POOL), F32), jax.ShapeDtypeStruct(pool_w_b.shape, F32),
                   jax.ShapeDtypeStruct((1, D_POOL), F32), jax.ShapeDtypeStruct((1, D_POOL), F32)],
        scratch_shapes=[pltpu.VMEM((TL + 2 * HALO, D_POOL), F32)],
        args=(dh1, u, u, u, w_out_b, pool_w_b, pool_scale, g_pool))


def _pool_bwd_window(d_pooled):
    L = d_pooled.shape[0]
    n = L // TL
    R = TL + 2 * HALO

    def body(prev_ref, cur_ref, next_ref, out_ref, ext_ref, q_ref):
        i = pl.program_id(0)
        _fill_ext(ext_ref, prev_ref, cur_ref, next_ref, i, n, TL)
        tr = i * TL - HALO + lax.broadcasted_iota(jnp.int32, (R, 1), 0)
        for gi, w in enumerate(POOL_WINDOWS):
            half = w // 2
            cs = slice(gi * POOL_GROUP, (gi + 1) * POOL_GROUP)
            cnt = jnp.maximum(jnp.minimum(tr + half, L) - jnp.maximum(tr - half, 0), 1).astype(F32)
            q_ref[:, cs] = ext_ref[:, cs] / cnt
        for gi, w in enumerate(POOL_WINDOWS):
            half = w // 2
            cs = slice(gi * POOL_GROUP, (gi + 1) * POOL_GROUP)
            acc = q_ref[pl.ds(HALO - half + 1, TL), cs]
            for s in range(-half + 2, half + 1):
                acc = acc + q_ref[pl.ds(HALO + s, TL), cs]
            out_ref[:, cs] = acc - ext_ref[pl.ds(HALO, TL), cs]

    return pl.pallas_call(
        body, name="pool_bwd_window", grid=(n,),
        in_specs=_halo_specs_1d(TL, D_POOL, L, 0),
        out_specs=pl.BlockSpec((TL, D_POOL), lambda i: (i, 0)),
        out_shape=jax.ShapeDtypeStruct((L, D_POOL), F32),
        scratch_shapes=[pltpu.VMEM((R, D_POOL), F32), pltpu.VMEM((R, D_POOL), F32)],
        compiler_params=_cp("parallel"))(d_pooled, d_pooled, d_pooled)


def _ssm_param_fn(lnar, aim, ldt):
    dt = jnp.exp(ldt)
    a_re = -jnp.exp(lnar)
    mag = jnp.exp(a_re * dt)
    ang = aim * dt
    lr, li = mag * jnp.cos(ang), mag * jnp.sin(ang)
    den = a_re * a_re + aim * aim
    fr = ((lr - 1.0) * a_re + li * aim) / den
    fi = (li * a_re - (lr - 1.0) * aim) / den
    return lr, li, fr, fi


def _ssm_params(lnar, aim, ldt):
    def body(a_ref, b_ref, c_ref, lr_ref, li_ref, fr_ref, fi_ref):
        lr, li, fr, fi = _ssm_param_fn(a_ref[...], b_ref[...], c_ref[...])
        lr_ref[...] = lr
        li_ref[...] = li
        fr_ref[...] = fr
        fi_ref[...] = fi

    sh = jax.ShapeDtypeStruct(lnar.shape, F32)
    return pl.pallas_call(body, name="ssm_params", out_shape=[sh] * 4)(lnar, aim, ldt)


def _ssm_params_bwd(lnar, aim, ldt, glr, gli, gfr, gfi):
    def body(a_ref, b_ref, c_ref, g0, g1, g2, g3, da_ref, db_ref, dc_ref):
        _, vjp = jax.vjp(_ssm_param_fn, a_ref[...], b_ref[...], c_ref[...])
        da, db, dc = vjp((g0[...], g1[...], g2[...], g3[...]))
        da_ref[...] = da
        db_ref[...] = db
        dc_ref[...] = jnp.sum(dc, axis=1, keepdims=True)

    return pl.pallas_call(
        body, name="ssm_params_bwd",
        out_shape=[jax.ShapeDtypeStruct(lnar.shape, F32), jax.ShapeDtypeStruct(aim.shape, F32),
                   jax.ShapeDtypeStruct((ldt.shape[0], 1), F32)])(lnar, aim, ldt, glr, gli, gfr, gfi)


def _scan_tables(lam4):
    def build(lr, li, reverse, out_ref, k):
        row = lax.broadcasted_iota(jnp.int32, (8, N_STATE), 0)
        lrb = jnp.broadcast_to(lr, (8, N_STATE))
        lib = jnp.broadcast_to(li, (8, N_STATE))
        pr, pi = lrb, lib
        for s, sh in enumerate((1, 2, 4)):
            mask = (row < 8 - sh) if reverse else (row >= sh)
            out_ref[k, 2 * s] = jnp.where(mask, pr, 0.0)
            out_ref[k, 2 * s + 1] = jnp.where(mask, pi, 0.0)
            pr, pi = pr * pr - pi * pi, 2.0 * pr * pi
        pr, pi = lrb, lib
        p8r = jnp.zeros((8, N_STATE), F32)
        p8i = jnp.zeros((8, N_STATE), F32)
        for j in range(8):
            r = 7 - j if reverse else j
            p8r = jnp.where(row == r, pr, p8r)
            p8i = jnp.where(row == r, pi, p8i)
            pr, pi = pr * lrb - pi * lib, pr * lib + pi * lrb
        out_ref[k, 6] = p8r
        out_ref[k, 7] = p8i

    def body(lam_ref, out_ref):
        l0r, l0i, l1r, l1i = (lam_ref[j:j + 1, :] for j in range(4))
        build(l0r, l0i, False, out_ref, 0)
        build(l0r, -l0i, True, out_ref, 1)
        build(l1r, l1i, True, out_ref, 2)
        build(l1r, -l1i, False, out_ref, 3)

    return pl.pallas_call(body, name="scan_tables",
                          out_shape=jax.ShapeDtypeStruct((4, 8, 8, N_STATE), F32))(lam4)


def _b_block(g):
    q, gl = divmod(g, 4)
    r0, c0 = gl * SSM_STATE, (q % 4) * 4 * SSM_GROUP + gl * SSM_GROUP
    return q, slice(r0, r0 + SSM_STATE), slice(c0, c0 + SSM_GROUP)


def _c_block(g):
    q, rows, cols = _b_block(g)
    return q, cols, rows


def _ssm_expand(b_re, b_im, c_re, c_im):
    def body(bre_ref, bim_ref, cre_ref, cim_ref, *rest):
        outs, tmp = rest[:8], rest[8]
        for d in range(2):
            for j, (src, where) in enumerate(((bre_ref, _b_block), (bim_ref, _b_block),
                                              (cre_ref, _c_block), (cim_ref, _c_block))):
                tmp[...] = jnp.zeros_like(tmp)
                for g in range(N_SSM_GROUPS):
                    q, rows, cols = where(g)
                    tmp[q, rows, cols] = src[d, g]
                outs[4 * d + j][...] = tmp[...].astype(BF16)

    dense = jax.ShapeDtypeStruct((N_QUAD, QUAD, SLAB), BF16)
    return pl.pallas_call(body, name="ssm_expand", out_shape=[dense] * 8,
                          scratch_shapes=[pltpu.VMEM((N_QUAD, QUAD, SLAB), F32)],
                          compiler_params=pltpu.CompilerParams(vmem_limit_bytes=VMEM_LIMIT))(b_re, b_im, c_re, c_im)


def _scan_rows(src_re, src_im, dst_re, dst_im, tab_ref, k, carry_re, carry_im, rows, reverse, s_refs=None):
    ng = rows // 8
    edge = 0 if reverse else 7
    row_id = lax.broadcasted_iota(jnp.int32, (8, SCAN_W), 0)
    sums = []
    for lt in range(N_STATE // SCAN_W):
        sl = slice(lt * SCAN_W, (lt + 1) * SCAN_W)

        def step(r, c, sl=sl):
            tabs = [tab_ref[k, j, :, sl] for j in range(8)]
            cr, ci = c[0], c[1]
            row = pl.multiple_of((ng - 1 - r) * 8 if reverse else r * 8, 8)
            xr = src_re[pl.ds(row, 8), sl]
            xi = src_im[pl.ds(row, 8), sl]
            for s, sh in enumerate((1, 2, 4)):
                amt = 8 - sh if reverse else sh
                rr = pltpu.roll(xr, amt, 0)
                ri = pltpu.roll(xi, amt, 0)
                mr, mi = tabs[2 * s], tabs[2 * s + 1]
                xr, xi = xr + mr * rr - mi * ri, xi + mr * ri + mi * rr
            xr, xi = xr + tabs[6] * cr - tabs[7] * ci, xi + tabs[6] * ci + tabs[7] * cr
            dst_re[pl.ds(row, 8), sl] = xr
            dst_im[pl.ds(row, 8), sl] = xi
            ncr = jnp.broadcast_to(xr[edge:edge + 1, :], (8, SCAN_W))
            nci = jnp.broadcast_to(xi[edge:edge + 1, :], (8, SCAN_W))
            if s_refs is None:
                return ncr, nci
            amt = 7 if reverse else 1
            far = 7 if reverse else 0
            nr = jnp.where(row_id == far, cr, pltpu.roll(xr, amt, 0))
            ni = jnp.where(row_id == far, ci, pltpu.roll(xi, amt, 0))
            sr = s_refs[0][pl.ds(row, 8), sl]
            si = s_refs[1][pl.ds(row, 8), sl]
            return ncr, nci, c[2] + nr * sr + ni * si, c[3] + ni * sr - nr * si

        init = (carry_re[:, sl], carry_im[:, sl])
        if s_refs is not None:
            init = init + (jnp.zeros((8, SCAN_W), F32), jnp.zeros((8, SCAN_W), F32))
        out = lax.fori_loop(0, ng, step, init)
        carry_re[:, sl] = out[0]
        carry_im[:, sl] = out[1]
        if s_refs is not None:
            sums.append((jnp.sum(out[2], axis=0, keepdims=True), jnp.sum(out[3], axis=0, keepdims=True)))
    return sums


def _ssm_scan_fwd(u, b_re, b_im, c_re, c_im, f2, tables, k, reverse, comm=None):
    L = u.shape[0]
    nc = L // TC
    chunk = (lambda i: nc - 1 - i) if reverse else (lambda i: i)

    def body(u_ref, bre_ref, bim_ref, cre_ref, cim_ref, f_ref, tab_ref,
             y_ref, sre_ref, sim_ref, in_re, in_im, carry_re, carry_im):
        @pl.when(pl.program_id(0) == 0)
        def _():
            carry_re[...] = jnp.zeros_like(carry_re)
            carry_im[...] = jnp.zeros_like(carry_im)

        ub = u_ref[...].astype(BF16)
        for q in range(N_QUAD):
            qs = slice(q * QUAD, (q + 1) * QUAD)
            us = ub[:, (q // 4) * SLAB:(q // 4 + 1) * SLAB]
            bur = _dot_nt(us, bre_ref[q])
            bui = _dot_nt(us, bim_ref[q])
            fr = f_ref[0:1, qs]
            fi = f_ref[1:2, qs]
            in_re[:, qs] = fr * bur - fi * bui
            in_im[:, qs] = fr * bui + fi * bur
        _scan_rows(in_re, in_im, sre_ref, sim_ref, tab_ref, k, carry_re, carry_im, TC, reverse)
        for j in range(D_SSM // SLAB):
            acc = jnp.zeros((TC, SLAB), F32)
            for q in range(4 * j, 4 * j + 4):
                qs = slice(q * QUAD, (q + 1) * QUAD)
                acc = acc + _dot_nt(sre_ref[:, qs].astype(BF16), cre_ref[q])
                acc = acc - _dot_nt(sim_ref[:, qs].astype(BF16), cim_ref[q])
            y_ref[:, j * SLAB:(j + 1) * SLAB] = acc

    return _hosted_call(
        body, comm, name="ssm_scan_rev" if reverse else "ssm_scan_fwd", grid=(nc,),
        in_specs=[pl.BlockSpec((TC, D_SSM), lambda i: (chunk(i), 1))]
        + [_full(b_re.shape)] * 4 + [_full(f2.shape), _full(tables.shape)],
        out_specs=[pl.BlockSpec((TC, D_SSM), lambda i: (chunk(i), 0)),
                   pl.BlockSpec((TC, N_STATE), lambda i: (chunk(i), 0)),
                   pl.BlockSpec((TC, N_STATE), lambda i: (chunk(i), 0))],
        out_shape=[jax.ShapeDtypeStruct((L, D_SSM), F32), jax.ShapeDtypeStruct((L, N_STATE), F32),
                   jax.ShapeDtypeStruct((L, N_STATE), F32)],
        scratch_shapes=[pltpu.VMEM((TC, N_STATE), F32), pltpu.VMEM((TC, N_STATE), F32),
                        pltpu.VMEM((8, N_STATE), F32), pltpu.VMEM((8, N_STATE), F32)],
        args=(u, b_re, b_im, c_re, c_im, f2, tables))


def _quad_channels(q):
    c0 = (q // 4) * SLAB + (q % 4) * 4 * SSM_GROUP
    return slice(c0, c0 + 4 * SSM_GROUP)


def _ssm_scan_bwd(dy, u, s_re, s_im, b_re, b_im, c_re, c_im, f2, tables, k, reverse, comm=None):
    L = u.shape[0]
    nc = L // TC
    chunk = (lambda i: nc - 1 - i) if reverse else (lambda i: i)

    def body(dy_ref, u_ref, sre_ref, sim_ref, bre_ref, bim_ref, cre_ref, cim_ref, f_ref, tab_ref,
             du_ref, ob_re, ob_im, oc_re, oc_im, gv_ref,
             a_re, a_im, carry_re, carry_im, gbr_ref, gbi_ref, gcr_ref, gci_ref):
        @pl.when(pl.program_id(0) == 0)
        def _():
            carry_re[...] = jnp.zeros_like(carry_re)
            carry_im[...] = jnp.zeros_like(carry_im)
            for r in (gbr_ref, gbi_ref, gcr_ref, gci_ref, gv_ref):
                r[...] = jnp.zeros_like(r)

        dyb = dy_ref[...].astype(BF16)
        ub = u_ref[...].astype(BF16)
        for q in range(N_QUAD):
            qs = slice(q * QUAD, (q + 1) * QUAD)
            ds = dyb[:, (q // 4) * SLAB:(q // 4 + 1) * SLAB]
            a_re[:, qs] = _dot_nn(ds, cre_ref[q])
            a_im[:, qs] = -_dot_nn(ds, cim_ref[q])
            dq = dyb[:, _quad_channels(q)]
            gcr_ref[q] += _dot_tn(dq, sre_ref[:, qs].astype(BF16))
            gci_ref[q] -= _dot_tn(dq, sim_ref[:, qs].astype(BF16))
        sums = _scan_rows(a_re, a_im, a_re, a_im, tab_ref, k, carry_re, carry_im, TC, reverse,
                          s_refs=(sre_ref, sim_ref))
        for lt, (glr, gli) in enumerate(sums):
            sl = slice(lt * SCAN_W, (lt + 1) * SCAN_W)
            gv_ref[0:1, sl] += glr
            gv_ref[1:2, sl] += gli
        for j in range(D_SSM // SLAB):
            us = ub[:, j * SLAB:(j + 1) * SLAB]
            acc = jnp.zeros((TC, SLAB), F32)
            for q in range(4 * j, 4 * j + 4):
                qs = slice(q * QUAD, (q + 1) * QUAD)
                ar = a_re[:, qs]
                ai = a_im[:, qs]
                bur = _dot_nt(us, bre_ref[q])
                bui = _dot_nt(us, bim_ref[q])
                gv_ref[2:3, qs] += jnp.sum(ar * bur + ai * bui, axis=0, keepdims=True)
                gv_ref[3:4, qs] += jnp.sum(ai * bur - ar * bui, axis=0, keepdims=True)
                fr = f_ref[0:1, qs]
                fi = f_ref[1:2, qs]
                dbr = (fr * ar + fi * ai).astype(BF16)
                dbi = (fr * ai - fi * ar).astype(BF16)
                uq = ub[:, _quad_channels(q)]
                gbr_ref[q] += _dot_tn(uq, dbr)
                gbi_ref[q] += _dot_tn(uq, dbi)
                acc = acc + _dot_nn(dbr, bre_ref[q]) + _dot_nn(dbi, bim_ref[q])
            du_ref[:, j * SLAB:(j + 1) * SLAB] = acc

        @pl.when(pl.program_id(0) == nc - 1)
        def _():
            for g in range(N_SSM_GROUPS):
                q, gl = divmod(g, 4)
                rows = slice(gl * SSM_GROUP, (gl + 1) * SSM_GROUP)
                cols = slice(gl * SSM_STATE, (gl + 1) * SSM_STATE)
                for out, acc_ref in ((ob_re, gbr_ref), (ob_im, gbi_ref), (oc_re, gcr_ref), (oc_im, gci_ref)):
                    out[g] = acc_ref[q, rows, cols]

    gshape = jax.ShapeDtypeStruct((N_SSM_GROUPS, SSM_GROUP, SSM_STATE), F32)
    compact = pltpu.VMEM((N_QUAD, 4 * SSM_GROUP, QUAD), F32)
    return _hosted_call(
        body, comm, name="ssm_bwd_rev" if reverse else "ssm_bwd_fwd", grid=(nc,),
        in_specs=[pl.BlockSpec((TC, D_SSM), lambda i: (chunk(i), 0)),
                  pl.BlockSpec((TC, D_SSM), lambda i: (chunk(i), 1)),
                  pl.BlockSpec((TC, N_STATE), lambda i: (chunk(i), 0)),
                  pl.BlockSpec((TC, N_STATE), lambda i: (chunk(i), 0))]
        + [_full(b_re.shape)] * 4 + [_full(f2.shape), _full(tables.shape)],
        out_specs=[pl.BlockSpec((TC, D_SSM), lambda i: (chunk(i), 0))] + [_full(gshape.shape)] * 4
        + [_full((4, N_STATE))],
        out_shape=[jax.ShapeDtypeStruct((L, D_SSM), F32), gshape, gshape, gshape, gshape,
                   jax.ShapeDtypeStruct((4, N_STATE), F32)],
        scratch_shapes=[pltpu.VMEM((TC, N_STATE), F32), pltpu.VMEM((TC, N_STATE), F32),
                        pltpu.VMEM((8, N_STATE), F32), pltpu.VMEM((8, N_STATE), F32),
                        compact, compact, compact, compact],
        args=(dy, u, s_re, s_im, b_re, b_im, c_re, c_im, f2, tables))


def _ssm_post(yf, yb, u, d, glu_w, glu_b):
    y = yf + yb + d * u
    z, t = _gelu(y)
    zb = z.astype(BF16)
    gate = _sigmoid(_dot_nn(zb, glu_w) + glu_b)
    return y, z, t, zb, gate


def _mix_out(yn_pool, yf, yb, u, x, ssm_d, glu_w_b, glu_b, g_ssm, w_out_b, g_ffn):
    L, D = x.shape

    def body(ynp_ref, yf_ref, yb_ref, u_ref, x_ref, d_ref, gw_ref, gb_ref, gs_ref, wo_ref, gf_ref,
             h1_ref, hn_ref, ycat_ref):
        _, z, _, _, gate = _ssm_post(yf_ref[...], yb_ref[...], u_ref[...], d_ref[...], gw_ref[...], gb_ref[...])
        yns, _, _ = _rms_fwd(z * gate, gs_ref[...])
        ynsb = yns.astype(BF16)
        ynp = ynp_ref[...]
        ycat_ref[:, 0:D_POOL] = ynp
        ycat_ref[:, D_POOL:D] = ynsb
        h1 = x_ref[...] + _dot_nn(ynp, wo_ref[0:D_POOL, :]) + _dot_nn(ynsb, wo_ref[D_POOL:D, :])
        h1_ref[...] = h1
        hn, _, _ = _rms_fwd(h1, gf_ref[...])
        hn_ref[...] = hn.astype(BF16)

    half = lambda c: pl.BlockSpec((TL, D_SSM), lambda i: (i, c))
    row = pl.BlockSpec((TL, D), lambda i: (i, 0))
    return pl.pallas_call(
        body, name="mix_out", grid=(L // TL,),
        in_specs=[half(0), half(0), half(0), half(1), row, _full((1, D_SSM)), _full(glu_w_b.shape),
                  _full((1, D_SSM)), _full((1, D_SSM)), _full(w_out_b.shape), _full((1, D))],
        out_specs=[row, row, row],
        out_shape=[jax.ShapeDtypeStruct((L, D), F32), jax.ShapeDtypeStruct((L, D), BF16),
                   jax.ShapeDtypeStruct((L, D), BF16)],
        compiler_params=_cp("parallel"))(yn_pool, yf, yb, u, x, ssm_d, glu_w_b, glu_b, g_ssm, w_out_b, g_ffn)


def _ssm_bwd_local(dh1, yf, yb, u, ssm_d, glu_w_b, glu_b, g_ssm, w_out_b):
    L, D = dh1.shape

    def body(dh_ref, yf_ref, yb_ref, u_ref, d_ref, gw_ref, gb_ref, gs_ref, wo_ref,
             dy_ref, du_ref, ggw_ref, ggb_ref, gd_ref, ggs_ref):
        @pl.when(pl.program_id(0) == 0)
        def _():
            for r in (ggw_ref, ggb_ref, gd_ref, ggs_ref):
                r[...] = jnp.zeros_like(r)

        u = u_ref[...]
        d = d_ref[...]
        y, z, t, zb, gate = _ssm_post(yf_ref[...], yb_ref[...], u, d, gw_ref[...], gb_ref[...])
        gs = gs_ref[...]
        _, xh, inv = _rms_fwd(z * gate, gs)
        d_yn = _dot_nt(dh_ref[...].astype(BF16), wo_ref[...])
        d_o, dgs = _rms_bwd(d_yn, xh, inv, gs)
        ggs_ref[...] += dgs
        d_zg = d_o * z * gate * (1.0 - gate)
        d_zgb = d_zg.astype(BF16)
        ggb_ref[...] += jnp.sum(d_zg, axis=0, keepdims=True)
        ggw_ref[...] += _dot_tn(zb, d_zgb)
        d_z = d_o * gate + _dot_nt(d_zgb, gw_ref[...])
        d_y = d_z * _gelu_grad(y, t)
        gd_ref[...] += jnp.sum(d_y * u, axis=0, keepdims=True)
        dy_ref[...] = d_y
        du_ref[...] = d_y * d

    half = lambda c: pl.BlockSpec((TL, D_SSM), lambda i: (i, c))
    vec = _full((1, D_SSM))
    return pl.pallas_call(
        body, name="ssm_bwd_local", grid=(L // TL,),
        in_specs=[pl.BlockSpec((TL, D), lambda i: (i, 0)), half(0), half(0), half(1), vec, _full(glu_w_b.shape),
                  vec, vec, pl.BlockSpec((D_SSM, D), lambda i: (1, 0))],
        out_specs=[half(0), half(0), _full(glu_w_b.shape), vec, vec, vec],
        out_shape=[jax.ShapeDtypeStruct((L, D_SSM), F32), jax.ShapeDtypeStruct((L, D_SSM), F32),
                   jax.ShapeDtypeStruct(glu_w_b.shape, F32)] + [jax.ShapeDtypeStruct((1, D_SSM), F32)] * 3,
        compiler_params=_cp("arbitrary"))(dh1, yf, yb, u, ssm_d, glu_w_b, glu_b, g_ssm, w_out_b)


def _in_bwd(du_pool, du_a, du_b, du_c, dh1, x, g, w_in_b):
    L, D = x.shape

    def body(p_ref, a_ref, b_ref, c_ref, dh_ref, x_ref, g_ref, w_ref, dx_ref, dub_ref, gg_ref):
        @pl.when(pl.program_id(0) == 0)
        def _():
            gg_ref[...] = jnp.zeros_like(gg_ref)

        dub_ref[:, 0:D_POOL] = p_ref[...].astype(BF16)
        dub_ref[:, D_POOL:D] = (a_ref[...] + b_ref[...] + c_ref[...]).astype(BF16)
        d_xn = _dot_nt(dub_ref[...], w_ref[...])
        gv = g_ref[...]
        _, xh, inv = _rms_fwd(x_ref[...], gv)
        dx, dg = _rms_bwd(d_xn, xh, inv, gv)
        gg_ref[...] += dg
        dx_ref[...] = dh_ref[...] + dx

    half = pl.BlockSpec((TL, D_SSM), lambda i: (i, 0))
    row = pl.BlockSpec((TL, D), lambda i: (i, 0))
    return pl.pallas_call(
        body, name="in_bwd", grid=(L // TL,),
        in_specs=[half, half, half, half, row, row, _full((1, D)), _full(w_in_b.shape)],
        out_specs=[row, row, _full((1, D))],
        out_shape=[jax.ShapeDtypeStruct((L, D), F32), jax.ShapeDtypeStruct((L, D), BF16),
                   jax.ShapeDtypeStruct((1, D), F32)],
        compiler_params=_cp("arbitrary"))(du_pool, du_a, du_b, du_c, dh1, x, g, w_in_b)


def _ffn_up(hn, w_up4):
    L, D = hn.shape

    def body(h_ref, w_ref, o_ref):
        o_ref[...] = _dot_nn(h_ref[...], w_ref[...]).astype(BF16)

    return pl.pallas_call(
        body, name="ffn_up", grid=(4, L // TF),
        in_specs=[pl.BlockSpec((TF, D), lambda j, i: (i, 0)), pl.BlockSpec((None, D, FF_BLK), lambda j, i: (j, 0, 0))],
        out_specs=pl.BlockSpec((TF, FF_BLK), lambda j, i: (i, j)),
        out_shape=jax.ShapeDtypeStruct((L, 4 * FF_BLK), BF16),
        compiler_params=_cp("parallel", "parallel"))(hn, w_up4)


def _halo_specs_2d(rows, width, L, col, order):
    rb = rows // HALO_B
    last = L // HALO_B - 1
    if order == "ik":
        wrap = lambda f: (lambda i, k: f(i, k))
    else:
        wrap = lambda f: (lambda k, i: f(i, k))
    return [pl.BlockSpec((HALO_B, width), wrap(lambda i, k: (jnp.maximum(i * rb - 1, 0), col(k)))),
            pl.BlockSpec((rows, width), wrap(lambda i, k: (i, col(k)))),
            pl.BlockSpec((HALO_B, width), wrap(lambda i, k: (jnp.minimum((i + 1) * rb, last), col(k))))]


def _shift_mats(rows):
    r = lax.broadcasted_iota(jnp.int32, (rows, rows), 0)
    c = lax.broadcasted_iota(jnp.int32, (rows, rows), 1)
    return (c == r - 1).astype(BF16), (c == r + 1).astype(BF16)


def _neighbours(x, prev_ref, next_ref, cs, i, n, mats):
    rows = x.shape[0]
    row = lax.broadcasted_iota(jnp.int32, (rows, 1), 0)
    before = jnp.where(i > 0, prev_ref[:, cs].astype(F32)[HALO_B - 1:HALO_B, :], 0.0)
    after = jnp.where(i < n - 1, next_ref[:, cs].astype(F32)[0:1, :], 0.0)
    if mats is None:
        xf = x.astype(F32)
        down, up = pltpu.roll(xf, 1, 0), pltpu.roll(xf, rows - 1, 0)
    else:
        down, up = _dot_nn(mats[0], x), _dot_nn(mats[1], x)
    return jnp.where(row == 0, before, down), jnp.where(row == rows - 1, after, up)


def _conv3(x, before, after, w, b):
    return before * w[0:1, :] + x.astype(F32) * w[1:2, :] + after * w[2:3, :] + b


def _col_chunks(width, size=256):
    return [slice(c, min(c + size, width)) for c in range(0, width, size)]


def _ffn_down_loss(up, conv_w, conv_b, w_down_b, h1, target, g_final):
    L, D = h1.shape
    n = L // TF
    nk = D_FF // FF_BLK

    def body(vp, vc, vn, gp, gc, gn, wv_ref, wg_ref, bv_ref, bg_ref, wd_ref, h1_ref, t_ref, gf_ref,
             a_ref, dh2_ref, dh2b_ref, loss_ref, gg_ref, acc_ref):
        i = pl.program_id(0)
        k = pl.program_id(1)

        @pl.when((i == 0) & (k == 0))
        def _():
            loss_ref[...] = jnp.zeros_like(loss_ref)
            gg_ref[...] = jnp.zeros_like(gg_ref)

        @pl.when(k == 0)
        def _():
            acc_ref[...] = jnp.zeros_like(acc_ref)

        mats = _shift_mats(TF)
        for cs in _col_chunks(FF_BLK):
            xv, xg = vc[:, cs], gc[:, cs]
            val = _conv3(xv, *_neighbours(xv, vp, vn, cs, i, n, mats), wv_ref[:, cs], bv_ref[:, cs])
            gate = _conv3(xg, *_neighbours(xg, gp, gn, cs, i, n, mats), wg_ref[:, cs], bg_ref[:, cs])
            a_ref[:, cs] = (val * (gate * _sigmoid(gate))).astype(BF16)
        acc_ref[...] += _dot_nn(a_ref[...], wd_ref[pl.ds(pl.multiple_of(k * FF_BLK, LANES), FF_BLK), :])

        @pl.when(k == nk - 1)
        def _():
            gf = gf_ref[...]
            y, xh, inv = _rms_fwd(h1_ref[...] + acc_ref[...], gf)
            diff = y - t_ref[...]
            part = 0.5 * jnp.sum(jnp.mean(diff * diff, axis=-1, keepdims=True), axis=0, keepdims=True)
            loss_ref[...] += jnp.broadcast_to(part, loss_ref.shape)
            dx, dg = _rms_bwd(diff * (1.0 / D), xh, inv, gf)
            gg_ref[...] += dg
            dh2_ref[...] = dx
            dh2b_ref[...] = dx.astype(BF16)

    row = pl.BlockSpec((TF, D), lambda i, k: (i, 0))
    cw = lambda off: pl.BlockSpec((3, FF_BLK), lambda i, k: (0, k + off))
    cb = lambda off: pl.BlockSpec((1, FF_BLK), lambda i, k: (0, k + off))
    return pl.pallas_call(
        body, name="ffn_down_loss", grid=(n, nk),
        in_specs=_halo_specs_2d(TF, FF_BLK, L, lambda k: k, "ik") + _halo_specs_2d(TF, FF_BLK, L, lambda k: k + nk, "ik")
        + [cw(0), cw(nk), cb(0), cb(nk), _full(w_down_b.shape), row, row, _full((1, D))],
        out_specs=[pl.BlockSpec((TF, FF_BLK), lambda i, k: (i, k)), row, row, _full((1, LANES)), _full((1, D))],
        out_shape=[jax.ShapeDtypeStruct((L, D_FF), BF16), jax.ShapeDtypeStruct((L, D), F32),
                   jax.ShapeDtypeStruct((L, D), BF16), jax.ShapeDtypeStruct((1, LANES), F32),
                   jax.ShapeDtypeStruct((1, D), F32)],
        scratch_shapes=[pltpu.VMEM((TF, D), F32)],
        compiler_params=_cp("arbitrary", "arbitrary"))(
            up, up, up, up, up, up, conv_w, conv_w, conv_b, conv_b, w_down_b, h1, target, g_final)


def _ffn_act_bwd(up, conv_w, conv_b, w_down_b, dh2):
    L, D = dh2.shape
    n = L // TF
    nk = D_FF // FF_BLK

    def body(vp, vc, vn, gp, gc, gn, wv_ref, wg_ref, bv_ref, bg_ref, wd_ref, dh_ref,
             dv_ref, dg_ref, gcv_ref, gcg_ref):
        i = pl.program_id(1)

        @pl.when(i == 0)
        def _():
            gcv_ref[...] = jnp.zeros_like(gcv_ref)
            gcg_ref[...] = jnp.zeros_like(gcg_ref)

        mats = _shift_mats(TF)
        dh = dh_ref[...]
        for cs in _col_chunks(FF_BLK):
            xv, xg = vc[:, cs], gc[:, cs]
            v_rows = _neighbours(xv, vp, vn, cs, i, n, mats)
            g_rows = _neighbours(xg, gp, gn, cs, i, n, mats)
            val = _conv3(xv, *v_rows, wv_ref[:, cs], bv_ref[:, cs])
            gate = _conv3(xg, *g_rows, wg_ref[:, cs], bg_ref[:, cs])
            d_a = _dot_nt(dh, wd_ref[cs, :])
            sg = _sigmoid(gate)
            d_val = d_a * (gate * sg)
            d_gate = d_a * val * (sg * (1.0 + gate * (1.0 - sg)))
            dv_ref[:, cs] = d_val.astype(BF16)
            dg_ref[:, cs] = d_gate.astype(BF16)
            for d, x, (before, after), gref in ((d_val, xv, v_rows, gcv_ref), (d_gate, xg, g_rows, gcg_ref)):
                for j, shifted in enumerate((before, x.astype(F32), after)):
                    gref[j:j + 1, cs] += jnp.sum(d * shifted, axis=0, keepdims=True)
                gref[3:4, cs] += jnp.sum(d, axis=0, keepdims=True)

    cw = lambda off: pl.BlockSpec((3, FF_BLK), lambda k, i: (0, k + off))
    cb = lambda off: pl.BlockSpec((1, FF_BLK), lambda k, i: (0, k + off))
    blk = pl.BlockSpec((TF, FF_BLK), lambda k, i: (i, k))
    acc = pl.BlockSpec((4, FF_BLK), lambda k, i: (0, k))
    return pl.pallas_call(
        body, name="ffn_act_bwd", grid=(nk, n),
        in_specs=_halo_specs_2d(TF, FF_BLK, L, lambda k: k, "ki") + _halo_specs_2d(TF, FF_BLK, L, lambda k: k + nk, "ki")
        + [cw(0), cw(nk), cb(0), cb(nk), pl.BlockSpec((FF_BLK, D), lambda k, i: (k, 0)),
           pl.BlockSpec((TF, D), lambda k, i: (i, 0))],
        out_specs=[blk, blk, acc, acc],
        out_shape=[jax.ShapeDtypeStruct((L, D_FF), BF16), jax.ShapeDtypeStruct((L, D_FF), BF16),
                   jax.ShapeDtypeStruct((4, D_FF), F32), jax.ShapeDtypeStruct((4, D_FF), F32)],
        compiler_params=_cp("arbitrary", "arbitrary"))(
            up, up, up, up, up, up, conv_w, conv_w, conv_b, conv_b, w_down_b, dh2)


def _ffn_up_bwd(d_val, d_gate, conv_w, w_up4, h1, dh2, g_ffn):
    L, D = h1.shape
    n = L // TF
    nk = D_FF // FF_BLK

    def body(vp, vc, vn, gp, gc, gn, wv_ref, wg_ref, wu_ref, h1_ref, dh2_ref, g_ref,
             dup_ref, dh1_ref, dh1b_ref, gg_ref, acc_ref):
        i = pl.program_id(0)
        k = pl.program_id(1)

        @pl.when((i == 0) & (k == 0))
        def _():
            gg_ref[...] = jnp.zeros_like(gg_ref)

        @pl.when(k == 0)
        def _():
            acc_ref[...] = jnp.zeros_like(acc_ref)

        for j, (blocks, w_ref) in enumerate((((vp, vc, vn), wv_ref), ((gp, gc, gn), wg_ref))):
            for cs in _col_chunks(FF_BLK):
                d = blocks[1][:, cs]
                before, after = _neighbours(d, blocks[0], blocks[2], cs, i, n, None)
                w = w_ref[:, cs]
                dup_ref[j, :, cs] = (after * w[0:1, :] + d.astype(F32) * w[1:2, :] + before * w[2:3, :]).astype(BF16)
            acc_ref[...] += _dot_nt(dup_ref[j], wu_ref[k + j * nk])

        @pl.when(k == nk - 1)
        def _():
            g = g_ref[...]
            _, xh, inv = _rms_fwd(h1_ref[...], g)
            dx, dg = _rms_bwd(acc_ref[...], xh, inv, g)
            gg_ref[...] += dg
            dh1 = dh2_ref[...] + dx
            dh1_ref[...] = dh1
            dh1b_ref[...] = dh1.astype(BF16)

    row = pl.BlockSpec((TF, D), lambda i, k: (i, 0))
    cw = lambda off: pl.BlockSpec((3, FF_BLK), lambda i, k: (0, k + off))
    return pl.pallas_call(
        body, name="ffn_up_bwd", grid=(n, nk),
        in_specs=_halo_specs_2d(TF, FF_BLK, L, lambda k: k, "ik") + _halo_specs_2d(TF, FF_BLK, L, lambda k: k, "ik")
        + [cw(0), cw(nk), _full(w_up4.shape), row, row, _full((1, D))],
        out_specs=[pl.BlockSpec((2, None, TF, FF_BLK), lambda i, k: (0, k, i, 0)), row, row, _full((1, D))],
        out_shape=[jax.ShapeDtypeStruct((2, nk, L, FF_BLK), BF16), jax.ShapeDtypeStruct((L, D), F32),
                   jax.ShapeDtypeStruct((L, D), BF16), jax.ShapeDtypeStruct((1, D), F32)],
        scratch_shapes=[pltpu.VMEM((TF, D), F32)],
        compiler_params=_cp("arbitrary", "arbitrary"))(
            d_val, d_val, d_val, d_gate, d_gate, d_gate, conv_w, conv_w, w_up4, h1, dh2, g_ffn)


def _matmul_tn(a, b, tm, tn, name, tk=2048):
    L, M = a.shape
    N = b.shape[1]
    tk = min(tk, L)

    def body(a_ref, b_ref, o_ref):
        @pl.when(pl.program_id(2) == 0)
        def _():
            o_ref[...] = jnp.zeros_like(o_ref)

        o_ref[...] += _dot_tn(a_ref[...], b_ref[...])

    return pl.pallas_call(
        body, name=name, grid=(M // tm, N // tn, L // tk),
        in_specs=[pl.BlockSpec((tk, tm), lambda m, n, l: (l, m)), pl.BlockSpec((tk, tn), lambda m, n, l: (l, n))],
        out_specs=pl.BlockSpec((tm, tn), lambda m, n, l: (m, n)),
        out_shape=jax.ShapeDtypeStruct((M, N), F32),
        compiler_params=_cp("parallel", "parallel", "arbitrary"))(a, b)


def _matmul_tn_blocks(a, b, tm, name, tk=2048):
    L, M = a.shape
    J, _, N = b.shape
    tk = min(tk, L)

    def body(a_ref, b_ref, o_ref):
        @pl.when(pl.program_id(2) == 0)
        def _():
            o_ref[...] = jnp.zeros_like(o_ref)

        o_ref[...] += _dot_tn(a_ref[...], b_ref[...])

    return pl.pallas_call(
        body, name=name, grid=(M // tm, J, L // tk),
        in_specs=[pl.BlockSpec((tk, tm), lambda m, j, l: (l, m)), pl.BlockSpec((None, tk, N), lambda m, j, l: (j, l, 0))],
        out_specs=pl.BlockSpec((None, tm, N), lambda m, j, l: (j, m, 0)),
        out_shape=jax.ShapeDtypeStruct((J, M, N), F32),
        compiler_params=_cp("parallel", "parallel", "arbitrary"))(a, b)


def _row_tile(rows):
    for t in (512, 352, 256, 128, 64, 8):
        if rows % t == 0:
            return t
    return rows


def _add_half(g, r, c_arr, name, out_dtype=F32):
    _, _, R, C = g.shape
    tr = _row_tile(R)

    def body(c_ref, g_ref, r_ref, o_ref):
        o_ref[...] = (g_ref[...] + r_ref[...]).astype(out_dtype)

    return pl.pallas_call(
        body, name=name,
        grid_spec=pltpu.PrefetchScalarGridSpec(
            num_scalar_prefetch=1, grid=(g.shape[0], R // tr),
            in_specs=[pl.BlockSpec((None, None, tr, C), lambda j, i, c: (j, c[0], i, 0)),
                      pl.BlockSpec((None, tr, C), lambda j, i, c: (j, i, 0))],
            out_specs=pl.BlockSpec((None, tr, C), lambda j, i, c: (j, i, 0))),
        out_shape=jax.ShapeDtypeStruct(r.shape, out_dtype),
        compiler_params=_cp("parallel", "parallel"))(c_arr, g, r)


def _add2(a, b, name):
    R, C = a.shape
    tr = _row_tile(R)

    def body(a_ref, b_ref, o_ref):
        o_ref[...] = a_ref[...] + b_ref[...]

    spec = pl.BlockSpec((tr, C), lambda i: (i, 0))
    return pl.pallas_call(body, name=name, grid=(R // tr,), in_specs=[spec, spec], out_specs=spec,
                          out_shape=jax.ShapeDtypeStruct(a.shape, F32), compiler_params=_cp("parallel"))(a, b)


def _sum4(p, name):
    _, R, C = p.shape
    tr = _row_tile(R)

    def body(p_ref, o_ref):
        q = [p_ref[j].astype(F32) for j in range(4)]
        o_ref[...] = ((q[0] + q[1]) + q[2]) + q[3]

    return pl.pallas_call(
        body, name=name, grid=(R // tr,),
        in_specs=[pl.BlockSpec((4, tr, C), lambda i: (0, i, 0))],
        out_specs=pl.BlockSpec((tr, C), lambda i: (i, 0)),
        out_shape=jax.ShapeDtypeStruct((R, C), F32), compiler_params=_cp("parallel"))(p)


def _adamw_refs(w_ref, g_ref, m_ref, v_ref, d_ref, nm_ref, nv_ref):
    gv = g_ref[...]
    nm = ADAM_B1 * m_ref[...] + (1.0 - ADAM_B1) * gv
    nv = ADAM_B2 * v_ref[...] + (1.0 - ADAM_B2) * (gv * gv)
    m_hat = nm / (1.0 - ADAM_B1 ** ADAM_STEP)
    v_hat = nv / (1.0 - ADAM_B2 ** ADAM_STEP)
    d_ref[...] = -ADAM_LR * (m_hat / (jnp.sqrt(v_hat) + ADAM_EPS) + ADAM_WD * w_ref[...])
    nm_ref[...] = nm
    nv_ref[...] = nv


def _adamw_many(ws, gs, ms, vs, name):
    n = len(ws)

    def body(*refs):
        for k in range(n):
            _adamw_refs(*(refs[j * n + k] for j in range(7)))

    out_shape = [jax.ShapeDtypeStruct(w.shape, F32) for w in ws] * 3
    res = pl.pallas_call(body, name=name, out_shape=out_shape,
                         compiler_params=pltpu.CompilerParams(vmem_limit_bytes=VMEM_LIMIT))(*ws, *gs, *ms, *vs)
    return res[:n], res[n:2 * n], res[2 * n:]


def _adamw(w, g, m, v, name):
    R, C = w.shape
    tr = _row_tile(R)
    body = lambda *refs: _adamw_refs(*refs)

    spec = pl.BlockSpec((tr, C), lambda i: (i, 0))
    sh = jax.ShapeDtypeStruct((R, C), F32)
    return pl.pallas_call(body, name=name, grid=(R // tr,), in_specs=[spec] * 4, out_specs=[spec] * 3,
                          out_shape=[sh] * 3, compiler_params=_cp("parallel"))(w, g, m, v)


def _join_rows(own, other, c_arr, name):
    R, C = own.shape
    tr = _row_tile(R)

    def body(c_ref, own_ref, other_ref, o_ref):
        o_ref[...] = jnp.where(pl.program_id(0) == c_ref[0], own_ref[...], other_ref[...])

    half = pl.BlockSpec((tr, C), lambda h, i, c: (i, 0))
    return pl.pallas_call(
        body, name=name,
        grid_spec=pltpu.PrefetchScalarGridSpec(
            num_scalar_prefetch=1, grid=(2, R // tr), in_specs=[half, half],
            out_specs=pl.BlockSpec((tr, C), lambda h, i, c: (h * (R // tr) + i, 0))),
        out_shape=jax.ShapeDtypeStruct((2 * R, C), F32),
        compiler_params=_cp("parallel", "parallel"))(c_arr, own, other)


def _adamw_halves(w, own, other, m, v, c_arr, name):
    R, C = own.shape
    tr = _row_tile(R)
    while tr * C * 4 > 2 ** 20 and tr % 16 == 0:
        tr //= 2

    def body(c_ref, w_ref, own_ref, other_ref, m_ref, v_ref, g_ref, d_ref, nm_ref, nv_ref):
        g_ref[...] = jnp.where(pl.program_id(0) == c_ref[0], own_ref[...], other_ref[...])
        _adamw_refs(w_ref, g_ref, m_ref, v_ref, d_ref, nm_ref, nv_ref)

    half = pl.BlockSpec((tr, C), lambda h, i, c: (i, 0))
    full = pl.BlockSpec((tr, C), lambda h, i, c: (h * (R // tr) + i, 0))
    sh = jax.ShapeDtypeStruct((2 * R, C), F32)
    return pl.pallas_call(
        body, name=name,
        grid_spec=pltpu.PrefetchScalarGridSpec(
            num_scalar_prefetch=1, grid=(2, R // tr), in_specs=[full, half, half, full, full], out_specs=[full] * 4),
        out_shape=[sh] * 4, compiler_params=_cp("parallel", "parallel"))(c_arr, w, own, other, m, v)


_ANY = pl.BlockSpec(memory_space=pl.ANY)


def _position():
    return lax.axis_index("x"), lax.axis_index("y"), lax.axis_index("c")


class _Comm:
    def __init__(self, arrs, out_shape, sems, start, finish):
        self.arrs, self.out_shape, self.sems, self.start, self.finish = arrs, out_shape, sems, start, finish


def _comm_call(comm, name):
    n, m = len(comm.arrs), len(comm.out_shape)

    def body(*refs):
        ins, outs, sems = refs[:n], refs[n:n + m], refs[n + m:]
        comm.start(ins, outs, sems)
        comm.finish(ins, outs, sems)

    return pl.pallas_call(
        body, name=name, in_specs=[_ANY] * n, out_specs=[_ANY] * m, out_shape=comm.out_shape,
        scratch_shapes=comm.sems, compiler_params=pltpu.CompilerParams(has_side_effects=True))(*comm.arrs)


def _hosted_call(body, comm, *, name, grid, in_specs, out_specs, out_shape, scratch_shapes, args):
    sem = ("arbitrary",) * len(grid)
    if comm is None:
        return pl.pallas_call(body, name=name, grid=grid, in_specs=in_specs, out_specs=out_specs, out_shape=out_shape,
                              scratch_shapes=scratch_shapes, compiler_params=_cp(*sem))(*args), []
    n_in, n_out, n_scr = len(in_specs), len(out_specs), len(scratch_shapes)
    ci, co = len(comm.arrs), len(comm.out_shape)

    def full(*refs):
        ins, refs = refs[:n_in], refs[n_in:]
        cins, refs = refs[:ci], refs[ci:]
        outs, refs = refs[:n_out], refs[n_out:]
        couts, refs = refs[:co], refs[co:]
        scr, csems = refs[:n_scr], refs[n_scr:]
        first, last = True, True
        for d, size in enumerate(grid):
            first = first & (pl.program_id(d) == 0)
            last = last & (pl.program_id(d) == size - 1)

        @pl.when(first)
        def _():
            comm.start(cins, couts, csems)

        body(*ins, *outs, *scr)

        @pl.when(last)
        def _():
            comm.finish(cins, couts, csems)

    res = pl.pallas_call(
        full, name=name, grid=grid, in_specs=list(in_specs) + [_ANY] * ci, out_specs=list(out_specs) + [_ANY] * co,
        out_shape=list(out_shape) + list(comm.out_shape), scratch_shapes=list(scratch_shapes) + list(comm.sems),
        compiler_params=_cp(*sem))(*args, *comm.arrs)
    return res[:n_out], res[n_out:]


def _comm_join(*comms):
    def parts(xs, attr):
        out, at = [], 0
        for cm in comms:
            n = len(getattr(cm, attr))
            out.append(xs[at:at + n])
            at += n
        return out

    def start(ins, outs, sems):
        for cm, i, o, s in zip(comms, parts(ins, "arrs"), parts(outs, "out_shape"), parts(sems, "sems")):
            cm.start(i, o, s)

    def finish(ins, outs, sems):
        for cm, i, o, s in zip(comms, parts(ins, "arrs"), parts(outs, "out_shape"), parts(sems, "sems")):
            cm.finish(i, o, s)

    cat = lambda attr: [x for cm in comms for x in getattr(cm, attr)]
    return _Comm(cat("arrs"), cat("out_shape"), cat("sems"), start, finish)


def _dma_sems(*counts):
    return [pltpu.SemaphoreType.DMA((n,)) for n in counts]


def _comm_pair_swap(arrs, half=False):
    n = len(arrs)
    out_shape = [jax.ShapeDtypeStruct(a.shape[:1] + a.shape[2:] if half else a.shape, a.dtype) for a in arrs]

    def copies(ins, outs, sems):
        x, y, c = _position()
        return [pltpu.make_async_remote_copy(
            src_ref=ins[k].at[:, 1 - c] if half else ins[k], dst_ref=outs[k], send_sem=sems[0].at[k],
            recv_sem=sems[1].at[k], device_id=(x, y, 1 - c), device_id_type=MESH) for k in range(n)]

    def start(ins, outs, sems):
        for cp in copies(ins, outs, sems):
            cp.start()

    def finish(ins, outs, sems):
        for cp in copies(ins, outs, sems):
            cp.wait()

    return _Comm(arrs, out_shape, _dma_sems(n, n), start, finish)


def _chip_of(j, c):
    return (jnp.right_shift(j, 1), jnp.bitwise_and(j, 1), c)


def _comm_chip_exchange(arrs, scatter):
    n = len(arrs)
    out_shape = [jax.ShapeDtypeStruct(a.shape if scatter else (4,) + a.shape, a.dtype) for a in arrs]

    def copies(ins, outs, sems):
        x, y, c = _position()
        me = 2 * x + y
        local, sent, landed = [], [], []
        for k in range(n):
            local.append(pltpu.make_async_copy(ins[k].at[me] if scatter else ins[k], outs[k].at[me], sems[2].at[k]))
            for d in (1, 2, 3):
                j = jnp.bitwise_xor(me, d)
                s = 3 * k + d - 1
                src = ins[k].at[j] if scatter else ins[k]
                for dst, group in ((outs[k].at[me], sent), (outs[k].at[j], landed)):
                    group.append(pltpu.make_async_remote_copy(
                        src_ref=src, dst_ref=dst, send_sem=sems[0].at[s], recv_sem=sems[1].at[s],
                        device_id=_chip_of(j, c), device_id_type=MESH))
        return local, sent, landed

    def start(ins, outs, sems):
        local, sent, _ = copies(ins, outs, sems)
        for cp in local + sent:
            cp.start()

    def finish(ins, outs, sems):
        local, sent, landed = copies(ins, outs, sems)
        for cp in sent:
            cp.wait_send()
        for cp in landed:
            cp.wait_recv()
        for cp in local:
            cp.wait()

    return _Comm(arrs, out_shape, _dma_sems(3 * n, 3 * n, n), start, finish)


def _comm_pair_gather(arrs):
    n = len(arrs)
    out_shape = [jax.ShapeDtypeStruct((2,) + a.shape, a.dtype) for a in arrs]

    def copies(ins, outs, sems):
        x, y, c = _position()
        local, sent, landed = [], [], []
        for k in range(n):
            local.append(pltpu.make_async_copy(ins[k], outs[k].at[c], sems[2].at[k]))
            for dst, group in ((outs[k].at[c], sent), (outs[k].at[1 - c], landed)):
                group.append(pltpu.make_async_remote_copy(
                    src_ref=ins[k], dst_ref=dst, send_sem=sems[0].at[k], recv_sem=sems[1].at[k],
                    device_id=(x, y, 1 - c), device_id_type=MESH))
        return local, sent, landed

    def start(ins, outs, sems):
        local, sent, _ = copies(ins, outs, sems)
        for cp in local + sent:
            cp.start()

    def finish(ins, outs, sems):
        local, sent, landed = copies(ins, outs, sems)
        for cp in sent:
            cp.wait_send()
        for cp in landed:
            cp.wait_recv()
        for cp in local:
            cp.wait()

    return _Comm(arrs, out_shape, _dma_sems(n, n, n), start, finish)


LOCAL_PARTS = 4


def _comm_gather_split(shards, whole):
    n, nw = len(shards), len(whole)
    arrs = list(shards) + list(whole)
    out_shape = [jax.ShapeDtypeStruct((4,) + a.shape, a.dtype) for a in arrs]

    def copies(ins, outs, sems):
        x, y, c = _position()
        me = 2 * x + y
        local, sent, landed, passed, passed_in = [], [], [], [], []
        for k in range(n + nw):
            if k >= n:
                local.append(pltpu.make_async_copy(ins[k], outs[k].at[me], sems[4].at[LOCAL_PARTS * k]))
            else:
                part = shards[k].shape[0] // LOCAL_PARTS
                for r in range(LOCAL_PARTS):
                    local.append(pltpu.make_async_copy(ins[k].at[pl.ds(r * part, part)],
                                                       outs[k].at[me, pl.ds(r * part, part)],
                                                       sems[4].at[LOCAL_PARTS * k + r]))
            for d in (1, 2, 3):
                j = jnp.bitwise_xor(me, d)
                s = 3 * k + d - 1
                if k >= n:
                    src, mine, theirs = ins[k], outs[k].at[me], outs[k].at[j]
                else:
                    h = shards[k].shape[0] // 2
                    rows = pl.ds(pl.multiple_of(c * h, 16), h)
                    other = pl.ds(pl.multiple_of((1 - c) * h, 16), h)
                    src, mine, theirs = ins[k].at[rows], outs[k].at[me, rows], outs[k].at[j, rows]
                    for dst, group in ((theirs, passed), (outs[k].at[j, other], passed_in)):
                        group.append(pltpu.make_async_remote_copy(
                            src_ref=theirs, dst_ref=dst, send_sem=sems[2].at[s], recv_sem=sems[3].at[s],
                            device_id=(x, y, 1 - c), device_id_type=MESH))
                for dst, group in ((mine, sent), (theirs, landed)):
                    group.append(pltpu.make_async_remote_copy(
                        src_ref=src, dst_ref=dst, send_sem=sems[0].at[s], recv_sem=sems[1].at[s],
                        device_id=_chip_of(j, c), device_id_type=MESH))
        return local, sent, landed, passed, passed_in

    def start(ins, outs, sems):
        local, sent, _, _, _ = copies(ins, outs, sems)
        for cp in local + sent:
            cp.start()

    def finish(ins, outs, sems):
        local, sent, landed, passed, passed_in = copies(ins, outs, sems)
        for cp in landed[:3 * n]:
            cp.wait_recv()
        for cp in passed:
            cp.start()
        for cp in landed[3 * n:]:
            cp.wait_recv()
        for cp in sent:
            cp.wait_send()
        for cp in passed:
            cp.wait_send()
        for cp in passed_in:
            cp.wait_recv()
        for cp in local:
            cp.wait()

    t = 3 * (n + nw)
    return _Comm(arrs, out_shape, _dma_sems(t, t, max(3 * n, 1), max(3 * n, 1), LOCAL_PARTS * (n + nw)), start, finish)


def _pack(arrs, row_multiple):
    parts = []
    for a in arrs:
        flat = a.reshape(-1).astype(F32)
        pad = (-flat.shape[0]) % LANES
        parts.append(jnp.pad(flat, (0, pad)) if pad else flat)
    flat = jnp.concatenate(parts)
    rows = -(-flat.shape[0] // LANES)
    rows_p = -(-rows // row_multiple) * row_multiple
    return jnp.pad(flat, (0, rows_p * LANES - flat.shape[0])).reshape(rows_p, LANES)


def _unpack(packed, shapes):
    flat = packed.reshape(-1)
    outs, off = [], 0
    for sh in shapes:
        size = int(np.prod(sh))
        outs.append(flat[off:off + size].reshape(sh))
        off += size + (-size) % LANES
    return outs


SMALL = ["norm_mix_g", "pool_w", "pool_scale", "ssm_log_neg_a_re", "ssm_a_im", "ssm_log_dt", "ssm_b_re", "ssm_b_im",
         "ssm_c_re", "ssm_c_im", "ssm_d", "glu_b", "out_norm_pool_g", "out_norm_ssm_g", "norm_ffn_g", "conv_b",
         "final_norm_g"]
BIG = ["w_in", "glu_w", "w_out", "w_up", "w_down"]
WIDE = ["pool_w", "ssm_b_re", "ssm_b_im", "ssm_c_re", "ssm_c_im"]
WEIGHTS = ['norm_mix_g', 'w_in', 'pool_w', 'pool_scale', 'ssm_log_neg_a_re', 'ssm_a_im', 'ssm_log_dt', 'ssm_b_re',
           'ssm_b_im', 'ssm_c_re', 'ssm_c_im', 'ssm_d', 'glu_w', 'glu_b', 'out_norm_pool_g', 'out_norm_ssm_g', 'w_out',
           'norm_ffn_g', 'w_up', 'conv_w', 'conv_b', 'w_down', 'final_norm_g']


def _local_step(x, target, p, full, shards=None, c_arr=None):
    L, D = x.shape
    dist = shards is not None
    row = lambda a: a.reshape(1, -1)
    w_in = full["w_in"]
    pool_w_b = p["pool_w"].astype(BF16)
    g_mix, g_pool, g_ssm, g_ffn, g_fin = (row(p[k]) for k in (
        "norm_mix_g", "out_norm_pool_g", "out_norm_ssm_g", "norm_ffn_g", "final_norm_g"))
    pool_scale, ssm_d, glu_b, conv_b = (row(p[k]) for k in ("pool_scale", "ssm_d", "glu_b", "conv_b"))

    lnar = p["ssm_log_neg_a_re"].reshape(2 * N_SSM_GROUPS, SSM_STATE)
    aim = p["ssm_a_im"].reshape(2 * N_SSM_GROUPS, SSM_STATE)
    ldt = jnp.broadcast_to(p["ssm_log_dt"].reshape(2 * N_SSM_GROUPS, 1), lnar.shape)
    lam_re, lam_im, f_re, f_im = _ssm_params(lnar, aim, ldt)
    flat2 = lambda a: a.reshape(2, N_STATE)
    lam4 = jnp.stack([flat2(lam_re)[0], flat2(lam_im)[0], flat2(lam_re)[1], flat2(lam_im)[1]])
    tables = _scan_tables(lam4)
    f2 = [jnp.stack([flat2(f_re)[d], flat2(f_im)[d]]) for d in range(2)]
    dense = _ssm_expand(p["ssm_b_re"], p["ssm_b_im"], p["ssm_c_re"], p["ssm_c_im"])
    ssm_args = [tuple(dense[4 * d:4 * d + 4]) + (f2[d], tables) for d in range(2)]

    u, xn = _in_proj(x, g_mix, w_in)
    yn_pool = _pool_fwd(u, pool_w_b, pool_scale, g_pool)
    gather1 = _comm_gather_split([shards[k] for k in ("glu_w", "w_out", "w_down")], [shards["conv_w"]]) if dist else None
    (y0, s0r, s0i), got1 = _ssm_scan_fwd(u, *ssm_args[0], 0, False, comm=gather1)
    gather2 = _comm_gather_split([shards["w_up"]], []) if dist else None
    (y1, s1r, s1i), got2 = _ssm_scan_fwd(u, *ssm_args[1], 2, True, comm=gather2)
    if dist:
        glu_w, w_out, w_down = (g.reshape((-1,) + g.shape[2:]) for g in got1[:3])
        conv_w = jnp.transpose(got1[3], (1, 0, 2)).reshape(3, -1)
        w_up4 = got2[0]
    else:
        glu_w, w_out, w_up4, w_down, conv_w = (full[k] for k in ("glu_w", "w_out", "w_up", "w_down", "conv_w"))
    h1, hn, ycat = _mix_out(yn_pool, y0, y1, u, x, ssm_d, glu_w, glu_b, g_ssm, w_out, g_ffn)
    up = _ffn_up(hn, w_up4)
    a, dh2, dh2_b, loss, g_final = _ffn_down_loss(up, conv_w, conv_b, w_down, h1, target, g_fin)

    d_val, d_gate, gcv, gcg = _ffn_act_bwd(up, conv_w, conv_b, w_down, dh2_b)
    g_w_down = _matmul_tn(a, dh2_b, FF_BLK, D, "grad_w_down")
    d_up, dh1, dh1_b, g_ffn_g = _ffn_up_bwd(d_val, d_gate, conv_w, w_up4, h1, dh2, g_ffn)
    g_w_up = _matmul_tn_blocks(hn, d_up.reshape(4, L, FF_BLK), 512, "grad_w_up")
    g_w_out = _matmul_tn(ycat, dh1_b, 512, D, "grad_w_out")
    dy, du_direct, g_glu_w, g_glu_b, g_ssm_d, g_ssm_g = _ssm_bwd_local(dh1_b, y0, y1, u, ssm_d, glu_w, glu_b, g_ssm, w_out)
    late = ("w_up", "w_down", "w_out", "glu_w")
    halves = [g_w_up.reshape(4, 2, D // 2, FF_BLK), g_w_down.reshape(4, 2, D_FF // 8, D),
              g_w_out.reshape(4, 2, D // 8, D), g_glu_w.reshape(4, 2, D_SSM // 8, D_SSM)]
    (d_pooled, g_pool_w, g_pool_scale, g_pool_g), from_sibling = _pool_bwd_local(
        dh1_b, u, w_out, pool_w_b, pool_scale, g_pool, comm=_comm_pair_swap(halves, half=True) if dist else None)
    du_pool = _pool_bwd_window(d_pooled)
    reduce2 = None
    if dist:
        chip_sums = [_add_half(h, r, c_arr, "sum_pair_" + k, BF16) for k, h, r in zip(late, halves, from_sibling)]
        reduce2 = _comm_chip_exchange(chip_sums, scatter=True)
    (du0, gb0r, gb0i, gc0r, gc0i, gv0), from_chips = _ssm_scan_bwd(dy, u, s0r, s0i, *ssm_args[0], 1, True, comm=reduce2)
    mine = [_sum4(r, "sum_chips_" + k) for k, r in zip(late, from_chips)]
    (du1, gb1r, gb1i, gc1r, gc1i, gv1), theirs = _ssm_scan_bwd(
        dy, u, s1r, s1i, *ssm_args[1], 3, False, comm=_comm_pair_swap(mine) if dist else None)
    gvec = lambda j: jnp.stack([gv0[j], gv1[j]]).reshape(2 * N_SSM_GROUPS, SSM_STATE)
    g_lnar, g_aim, g_ldt = _ssm_params_bwd(lnar, aim, ldt, gvec(0), gvec(1), gvec(2), gvec(3))
    grad_x, d_u_b, g_mix_g = _in_bwd(du_pool, du_direct, du0, du1, dh1, x, g_mix, w_in)
    g_w_in = _matmul_tn(xn, d_u_b, 512, D, "grad_w_in")

    small = {
        "norm_mix_g": g_mix_g, "pool_w": g_pool_w, "pool_scale": g_pool_scale,
        "ssm_log_neg_a_re": g_lnar, "ssm_a_im": g_aim, "ssm_log_dt": g_ldt,
        "ssm_b_re": jnp.swapaxes(jnp.stack([gb0r, gb1r]), 2, 3), "ssm_b_im": jnp.swapaxes(jnp.stack([gb0i, gb1i]), 2, 3),
        "ssm_c_re": jnp.stack([gc0r, gc1r]), "ssm_c_im": jnp.stack([gc0i, gc1i]),
        "ssm_d": g_ssm_d, "glu_b": g_glu_b, "out_norm_pool_g": g_pool_g, "out_norm_ssm_g": g_ssm_g,
        "norm_ffn_g": g_ffn_g, "conv_b": jnp.concatenate([gcv[3], gcg[3]]), "final_norm_g": g_final,
        "conv_w": jnp.concatenate([gcv[0:3], gcg[0:3]], axis=1),
    }
    big = {"w_in": g_w_in}
    reduced = dict(zip(late, zip(mine, theirs)))
    if not dist:
        big.update({"w_up": g_w_up, "w_down": g_w_down, "w_out": g_w_out, "glu_w": g_glu_w})
    return loss, grad_x, small, big, reduced


def kernel(x, norm_mix_g, w_in, pool_w, pool_scale, ssm_log_neg_a_re, ssm_a_im, ssm_log_dt, ssm_b_re, ssm_b_im, ssm_c_re, ssm_c_im, ssm_d, glu_w, glu_b, out_norm_pool_g, out_norm_ssm_g, w_out, norm_ffn_g, w_up, conv_w, conv_b, w_down, final_norm_g, loss_target, m_norm_mix_g, m_w_in, m_pool_w, m_pool_scale, m_ssm_log_neg_a_re, m_ssm_a_im, m_ssm_log_dt, m_ssm_b_re, m_ssm_b_im, m_ssm_c_re, m_ssm_c_im, m_ssm_d, m_glu_w, m_glu_b, m_out_norm_pool_g, m_out_norm_ssm_g, m_w_out, m_norm_ffn_g, m_w_up, m_conv_w, m_conv_b, m_w_down, m_final_norm_g, v_norm_mix_g, v_w_in, v_pool_w, v_pool_scale, v_ssm_log_neg_a_re, v_ssm_a_im, v_ssm_log_dt, v_ssm_b_re, v_ssm_b_im, v_ssm_c_re, v_ssm_c_im, v_ssm_d, v_glu_w, v_glu_b, v_out_norm_pool_g, v_out_norm_ssm_g, v_w_out, v_norm_ffn_g, v_w_up, v_conv_w, v_conv_b, v_w_down, v_final_norm_g):
    args = locals()
    w = {k: args[k] for k in WEIGHTS}
    m = {k: args["m_" + k] for k in WEIGHTS}
    v = {k: args["v_" + k] for k in WEIGHTS}
    chip = 2 * lax.axis_index("x") + lax.axis_index("y")
    c_arr = lax.axis_index("c").astype(jnp.int32).reshape(1)

    shards = {k: w[k].astype(BF16) for k in BIG}
    shards["conv_w"] = conv_w
    w_in_full = _comm_call(_comm_chip_exchange([shards["w_in"]], scatter=False), "gather_w_in")[0]
    loss, grad_x, g_small, g_big, reduced = _local_step(
        x[0], loss_target[0], w, {"w_in": w_in_full.reshape(-1, w_in_full.shape[-1])}, shards, c_arr)

    exact = [k for k in SMALL if k not in WIDE]
    packs = [_pack([loss] + [g_small[k] for k in exact] + [g_small["conv_w"]], 512),
             _pack([g_small[k] for k in WIDE], 512)]
    halves = [g_big["w_in"].reshape(4, 2, g_big["w_in"].shape[0] // 8, -1)]
    halves += [pk.reshape(1, 2, pk.shape[0] // 2, LANES) for pk in packs]
    from_sibling = _comm_call(_comm_pair_swap(halves, half=True), "reduce_pair")
    names = ("w_in", "exact", "wide")
    sums = [_add_half(h, r, c_arr, "sum_pair_" + k, dt)
            for k, h, r, dt in zip(names, halves, from_sibling, (BF16, F32, BF16))]
    from_chips = _comm_call(_comm_join(_comm_chip_exchange(sums[:1], scatter=True),
                                       _comm_chip_exchange([s[0] for s in sums[1:]], scatter=False)), "reduce_chips")
    mine = [_sum4(r, "sum_chips_" + k) for k, r in zip(names, from_chips)]
    theirs = _comm_call(_comm_pair_swap(mine), "swap_halves")
    grads = {}
    exact_all = _join_rows(mine[1], theirs[1], c_arr, "join_exact")
    wide_all = _join_rows(mine[2], theirs[2], c_arr, "join_wide")
    shapes = [loss.shape] + [w[k].shape for k in exact] + [(3, 4 * FF_BLK)]
    grads.update(zip(["loss"] + exact + ["conv_w_full"], _unpack(exact_all, shapes)))
    grads.update(zip(WIDE, _unpack(wide_all, [w[k].shape for k in WIDE])))
    loss = grads.pop("loss")[0, 0]
    grads["conv_w"] = lax.dynamic_slice_in_dim(grads.pop("conv_w_full"), chip * FF_BLK, FF_BLK, axis=1)

    delta, new_m, new_v = {}, {}, {}
    reduced["w_in"] = (mine[0], theirs[0])
    for k, (own, other) in reduced.items():
        grads[k], delta[k], new_m[k], new_v[k] = _adamw_halves(w[k], own, other, m[k], v[k], c_arr, "adamw_" + k)
    padded = ["ssm_b_re", "ssm_b_im"]
    for keys, name in ((padded, "adamw_ssm_b"), ([k for k in SMALL + ["conv_w"] if k not in padded], "adamw_small")):
        outs = _adamw_many(*([d[k] for k in keys] for d in (w, grads, m, v)), name)
        for d, o in zip((delta, new_m, new_v), outs):
            d.update(zip(keys, o))

    return (loss, grad_x[None], *[grads[k] for k in WEIGHTS], *[delta[k] for k in WEIGHTS],
            *[new_m[k] for k in WEIGHTS], *[new_v[k] for k in WEIGHTS])
```

```python
import numpy as np
import jax
import jax.numpy as jnp
from jax import lax
from jax.experimental import pallas as pl
from jax.experimental.pallas import tpu as pltpu

F32 = jnp.float32
BF16 = jnp.bfloat16
MESH = pl.DeviceIdType.MESH

EPS = 1e-6
POOL_WINDOWS = (2, 4, 8, 16)
POOL_GROUP = 128
SSM_GROUP = 16
SSM_STATE = 64
N_SSM_GROUPS = 32
N_STATE = N_SSM_GROUPS * SSM_STATE
QUAD = 256
N_QUAD = N_STATE // QUAD
SLAB = 256
D_SSM = 512
D_POOL = 512
D_FF = 2816
FF_BLK = 1408
HALO = 8
HALO_B = 16
LANES = 128
ADAM_LR, ADAM_B1, ADAM_B2, ADAM_EPS, ADAM_WD, ADAM_STEP = 0.001, 0.9, 0.999, 1e-08, 0.01, 10
VMEM_LIMIT = 56 * 2 ** 20

TL = 512
TF = 256
TC = 256
SCAN_W = 512


def _cp(*sem):
    return pltpu.CompilerParams(dimension_semantics=sem, vmem_limit_bytes=VMEM_LIMIT)


def _dot_nn(a, b):
    return jnp.dot(a, b, preferred_element_type=F32)


def _dot_nt(a, b):
    return lax.dot_general(a, b, (((1,), (1,)), ((), ())), preferred_element_type=F32)


def _dot_tn(a, b):
    return lax.dot_general(a, b, (((0,), (0,)), ((), ())), preferred_element_type=F32)


def _rms_fwd(x, g):
    inv = lax.rsqrt(jnp.mean(x * x, axis=-1, keepdims=True) + EPS)
    xh = x * inv
    return xh * g, xh, inv


def _rms_bwd(dy, xh, inv, g):
    dg = jnp.sum(dy * xh, axis=0, keepdims=True)
    dxh = dy * g
    dx = inv * (dxh - xh * jnp.mean(dxh * xh, axis=-1, keepdims=True))
    return dx, dg


_GELU_C = 0.7978845608028654
_GELU_A = 0.044715


def _gelu(y):
    t = jnp.tanh(_GELU_C * (y + _GELU_A * (y * y * y)))
    return 0.5 * y * (1.0 + t), t


def _gelu_grad(y, t):
    return 0.5 * (1.0 + t) + 0.5 * y * (1.0 - t * t) * (_GELU_C * (1.0 + 3.0 * _GELU_A * y * y))


def _sigmoid(x):
    return 1.0 / (1.0 + jnp.exp(-x))


def _full(shape):
    n = len(shape)
    return pl.BlockSpec(shape, lambda *_: (0,) * n)


def _fill_ext(ext_ref, prev_ref, cur_ref, next_ref, i, n, rows):
    ext_ref[0:HALO, :] = jnp.where(i > 0, prev_ref[...], 0.0).astype(ext_ref.dtype)
    ext_ref[HALO:HALO + rows, :] = cur_ref[...]
    ext_ref[HALO + rows:2 * HALO + rows, :] = jnp.where(i < n - 1, next_ref[...], 0.0).astype(ext_ref.dtype)


def _in_proj(x, g, w):
    L, D = x.shape
    E = w.shape[1]

    def body(x_ref, g_ref, w_ref, u_ref, xn_ref):
        y, _, _ = _rms_fwd(x_ref[...], g_ref[...])
        yb = y.astype(BF16)
        xn_ref[...] = yb
        u_ref[...] = _dot_nn(yb, w_ref[...])

    return pl.pallas_call(
        body, name="in_proj", grid=(L // TL,),
        in_specs=[pl.BlockSpec((TL, D), lambda i: (i, 0)), _full((1, D)), _full(w.shape)],
        out_specs=[pl.BlockSpec((TL, E), lambda i: (i, 0)), pl.BlockSpec((TL, D), lambda i: (i, 0))],
        out_shape=[jax.ShapeDtypeStruct((L, E), F32), jax.ShapeDtypeStruct((L, D), BF16)],
        compiler_params=_cp("parallel"))(x, g, w)


def _halo_specs_1d(rows, width, L, col):
    rb = rows // HALO
    last = L // HALO - 1
    return [pl.BlockSpec((HALO, width), lambda i: (jnp.maximum(i * rb - 1, 0), col)),
            pl.BlockSpec((rows, width), lambda i: (i, col)),
            pl.BlockSpec((HALO, width), lambda i: (jnp.minimum((i + 1) * rb, last), col))]


def _pooled_from_ext(ext_ref, t0, rows, L):
    t = t0 + lax.broadcasted_iota(jnp.int32, (rows, 1), 0)
    outs = []
    for gi, w in enumerate(POOL_WINDOWS):
        half = w // 2
        cs = slice(gi * POOL_GROUP, (gi + 1) * POOL_GROUP)
        acc = ext_ref[pl.ds(HALO - half, rows), cs]
        for s in range(-half + 1, half):
            acc = acc + ext_ref[pl.ds(HALO + s, rows), cs]
        cnt = (jnp.minimum(t + half, L) - jnp.maximum(t - half, 0)).astype(F32)
        outs.append(acc / cnt - ext_ref[pl.ds(HALO, rows), cs])
    return outs


def _pool_fwd(u, pool_w_b, pool_scale, g_pool):
    L = u.shape[0]
    n = L // TL

    def body(prev_ref, cur_ref, next_ref, pw_ref, ps_ref, g_ref, out_ref, ext_ref):
        i = pl.program_id(0)
        _fill_ext(ext_ref, prev_ref, cur_ref, next_ref, i, n, TL)
        pooled = _pooled_from_ext(ext_ref, i * TL, TL, L)
        ypre = jnp.concatenate([_dot_nn(pooled[gi].astype(BF16), pw_ref[gi]) for gi in range(4)], axis=-1)
        yn, _, _ = _rms_fwd(ypre * ps_ref[...], g_ref[...])
        out_ref[...] = yn.astype(BF16)

    return pl.pallas_call(
        body, name="pool_fwd", grid=(n,),
        in_specs=_halo_specs_1d(TL, D_POOL, L, 0) + [_full(pool_w_b.shape), _full((1, D_POOL)), _full((1, D_POOL))],
        out_specs=pl.BlockSpec((TL, D_POOL), lambda i: (i, 0)),
        out_shape=jax.ShapeDtypeStruct((L, D_POOL), BF16),
        scratch_shapes=[pltpu.VMEM((TL + 2 * HALO, D_POOL), F32)],
        compiler_params=_cp("parallel"))(u, u, u, pool_w_b, pool_scale, g_pool)


def _pool_bwd_local(dh1, u, w_out_b, pool_w_b, pool_scale, g_pool, comm=None):
    L = u.shape[0]
    n = L // TL
    D = dh1.shape[1]

    def body(dh_ref, prev_ref, cur_ref, next_ref, wo_ref, pw_ref, ps_ref, g_ref,
             dp_ref, gpw_ref, gps_ref, gg_ref, ext_ref):
        i = pl.program_id(0)

        @pl.when(i == 0)
        def _():
            gpw_ref[...] = jnp.zeros_like(gpw_ref)
            gps_ref[...] = jnp.zeros_like(gps_ref)
            gg_ref[...] = jnp.zeros_like(gg_ref)

        _fill_ext(ext_ref, prev_ref, cur_ref, next_ref, i, n, TL)
        pooled = [p.astype(BF16) for p in _pooled_from_ext(ext_ref, i * TL, TL, L)]
        ypre = jnp.concatenate([_dot_nn(pooled[gi], pw_ref[gi]) for gi in range(4)], axis=-1)
        ps = ps_ref[...]
        g = g_ref[...]
        _, xh, inv = _rms_fwd(ypre * ps, g)
        d_yn = _dot_nt(dh_ref[...].astype(BF16), wo_ref[...])
        d_y, dg = _rms_bwd(d_yn, xh, inv, g)
        gg_ref[...] += dg
        gps_ref[...] += jnp.sum(d_y * ypre, axis=0, keepdims=True)
        d_ypre = (d_y * ps).astype(BF16)
        for gi in range(4):
            cs = slice(gi * POOL_GROUP, (gi + 1) * POOL_GROUP)
            dp_ref[:, cs] = _dot_nt(d_ypre[:, cs], pw_ref[gi])
            gpw_ref[gi] += _dot_tn(pooled[gi], d_ypre[:, cs])

    return _hosted_call(
        body, comm, name="pool_bwd_local", grid=(n,),
        in_specs=[pl.BlockSpec((TL, D), lambda i: (i, 0))] + _halo_specs_1d(TL, D_POOL, L, 0)
        + [pl.BlockSpec((D_POOL, D), lambda i: (0, 0)), _full(pool_w_b.shape), _full((1, D_POOL)), _full((1, D_POOL))],
        out_specs=[pl.BlockSpec((TL, D_POOL), lambda i: (i, 0)), _full(pool_w_b.shape),
                   _full((1, D_POOL)), _full((1, D_POOL))],
        out_shape=[jax.ShapeDtypeStruct((L, D_POOL), F32), jax.ShapeDtypeStruct(pool_w_b.shape, F32),
                   jax.ShapeDtypeStruct((1, D_POOL), F32), jax.ShapeDtypeStruct((1, D_POOL), F32)],
        scratch_shapes=[pltpu.VMEM((TL + 2 * HALO, D_POOL), F32)],
        args=(dh1, u, u, u, w_out_b, pool_w_b, pool_scale, g_pool))


def _pool_bwd_window(d_pooled):
    L = d_pooled.shape[0]
    n = L // TL
    R = TL + 2 * HALO

    def body(prev_ref, cur_ref, next_ref, out_ref, ext_ref, q_ref):
        i = pl.program_id(0)
        _fill_ext(ext_ref, prev_ref, cur_ref, next_ref, i, n, TL)
        tr = i * TL - HALO + lax.broadcasted_iota(jnp.int32, (R, 1), 0)
        for gi, w in enumerate(POOL_WINDOWS):
            half = w // 2
            cs = slice(gi * POOL_GROUP, (gi + 1) * POOL_GROUP)
            cnt = jnp.maximum(jnp.minimum(tr + half, L) - jnp.maximum(tr - half, 0), 1).astype(F32)
            q_ref[:, cs] = ext_ref[:, cs] / cnt
        for gi, w in enumerate(POOL_WINDOWS):
            half = w // 2
            cs = slice(gi * POOL_GROUP, (gi + 1) * POOL_GROUP)
            acc = q_ref[pl.ds(HALO - half + 1, TL), cs]
            for s in range(-half + 2, half + 1):
                acc = acc + q_ref[pl.ds(HALO + s, TL), cs]
            out_ref[:, cs] = acc - ext_ref[pl.ds(HALO, TL), cs]

    return pl.pallas_call(
        body, name="pool_bwd_window", grid=(n,),
        in_specs=_halo_specs_1d(TL, D_POOL, L, 0),
        out_specs=pl.BlockSpec((TL, D_POOL), lambda i: (i, 0)),
        out_shape=jax.ShapeDtypeStruct((L, D_POOL), F32),
        scratch_shapes=[pltpu.VMEM((R, D_POOL), F32), pltpu.VMEM((R, D_POOL), F32)],
        compiler_params=_cp("parallel"))(d_pooled, d_pooled, d_pooled)


def _ssm_param_fn(lnar, aim, ldt):
    dt = jnp.exp(ldt)
    a_re = -jnp.exp(lnar)
    mag = jnp.exp(a_re * dt)
    ang = aim * dt
    lr, li = mag * jnp.cos(ang), mag * jnp.sin(ang)
    den = a_re * a_re + aim * aim
    fr = ((lr - 1.0) * a_re + li * aim) / den
    fi = (li * a_re - (lr - 1.0) * aim) / den
    return lr, li, fr, fi


def _ssm_params(lnar, aim, ldt):
    def body(a_ref, b_ref, c_ref, lr_ref, li_ref, fr_ref, fi_ref):
        lr, li, fr, fi = _ssm_param_fn(a_ref[...], b_ref[...], c_ref[...])
        lr_ref[...] = lr
        li_ref[...] = li
        fr_ref[...] = fr
        fi_ref[...] = fi

    sh = jax.ShapeDtypeStruct(lnar.shape, F32)
    return pl.pallas_call(body, name="ssm_params", out_shape=[sh] * 4)(lnar, aim, ldt)


def _ssm_params_bwd(lnar, aim, ldt, glr, gli, gfr, gfi):
    def body(a_ref, b_ref, c_ref, g0, g1, g2, g3, da_ref, db_ref, dc_ref):
        _, vjp = jax.vjp(_ssm_param_fn, a_ref[...], b_ref[...], c_ref[...])
        da, db, dc = vjp((g0[...], g1[...], g2[...], g3[...]))
        da_ref[...] = da
        db_ref[...] = db
        dc_ref[...] = jnp.sum(dc, axis=1, keepdims=True)

    return pl.pallas_call(
        body, name="ssm_params_bwd",
        out_shape=[jax.ShapeDtypeStruct(lnar.shape, F32), jax.ShapeDtypeStruct(aim.shape, F32),
                   jax.ShapeDtypeStruct((ldt.shape[0], 1), F32)])(lnar, aim, ldt, glr, gli, gfr, gfi)


def _scan_tables(lam4):
    def build(lr, li, reverse, out_ref, k):
        row = lax.broadcasted_iota(jnp.int32, (8, N_STATE), 0)
        lrb = jnp.broadcast_to(lr, (8, N_STATE))
        lib = jnp.broadcast_to(li, (8, N_STATE))
        pr, pi = lrb, lib
        for s, sh in enumerate((1, 2, 4)):
            mask = (row < 8 - sh) if reverse else (row >= sh)
            out_ref[k, 2 * s] = jnp.where(mask, pr, 0.0)
            out_ref[k, 2 * s + 1] = jnp.where(mask, pi, 0.0)
            pr, pi = pr * pr - pi * pi, 2.0 * pr * pi
        pr, pi = lrb, lib
        p8r = jnp.zeros((8, N_STATE), F32)
        p8i = jnp.zeros((8, N_STATE), F32)
        for j in range(8):
            r = 7 - j if reverse else j
            p8r = jnp.where(row == r, pr, p8r)
            p8i = jnp.where(row == r, pi, p8i)
            pr, pi = pr * lrb - pi * lib, pr * lib + pi * lrb
        out_ref[k, 6] = p8r
        out_ref[k, 7] = p8i

    def body(lam_ref, out_ref):
        l0r, l0i, l1r, l1i = (lam_ref[j:j + 1, :] for j in range(4))
        build(l0r, l0i, False, out_ref, 0)
        build(l0r, -l0i, True, out_ref, 1)
        build(l1r, l1i, True, out_ref, 2)
        build(l1r, -l1i, False, out_ref, 3)

    return pl.pallas_call(body, name="scan_tables",
                          out_shape=jax.ShapeDtypeStruct((4, 8, 8, N_STATE), F32))(lam4)


def _b_block(g):
    q, gl = divmod(g, 4)
    r0, c0 = gl * SSM_STATE, (q % 4) * 4 * SSM_GROUP + gl * SSM_GROUP
    return q, slice(r0, r0 + SSM_STATE), slice(c0, c0 + SSM_GROUP)


def _c_block(g):
    q, rows, cols = _b_block(g)
    return q, cols, rows


def _ssm_expand(b_re, b_im, c_re, c_im):
    def body(bre_ref, bim_ref, cre_ref, cim_ref, *rest):
        outs, tmp = rest[:8], rest[8]
        for d in range(2):
            for j, (src, where) in enumerate(((bre_ref, _b_block), (bim_ref, _b_block),
                                              (cre_ref, _c_block), (cim_ref, _c_block))):
                tmp[...] = jnp.zeros_like(tmp)
                for g in range(N_SSM_GROUPS):
                    q, rows, cols = where(g)
                    tmp[q, rows, cols] = src[d, g]
                outs[4 * d + j][...] = tmp[...].astype(BF16)

    dense = jax.ShapeDtypeStruct((N_QUAD, QUAD, SLAB), BF16)
    return pl.pallas_call(body, name="ssm_expand", out_shape=[dense] * 8,
                          scratch_shapes=[pltpu.VMEM((N_QUAD, QUAD, SLAB), F32)],
                          compiler_params=pltpu.CompilerParams(vmem_limit_bytes=VMEM_LIMIT))(b_re, b_im, c_re, c_im)


def _scan_rows(src_re, src_im, dst_re, dst_im, tab_ref, k, carry_re, carry_im, rows, reverse, s_refs=None):
    ng = rows // 8
    edge = 0 if reverse else 7
    row_id = lax.broadcasted_iota(jnp.int32, (8, SCAN_W), 0)
    sums = []
    for lt in range(N_STATE // SCAN_W):
        sl = slice(lt * SCAN_W, (lt + 1) * SCAN_W)

        def step(r, c, sl=sl):
            tabs = [tab_ref[k, j, :, sl] for j in range(8)]
            cr, ci = c[0], c[1]
            row = pl.multiple_of((ng - 1 - r) * 8 if reverse else r * 8, 8)
            xr = src_re[pl.ds(row, 8), sl]
            xi = src_im[pl.ds(row, 8), sl]
            for s, sh in enumerate((1, 2, 4)):
                amt = 8 - sh if reverse else sh
                rr = pltpu.roll(xr, amt, 0)
                ri = pltpu.roll(xi, amt, 0)
                mr, mi = tabs[2 * s], tabs[2 * s + 1]
                xr, xi = xr + mr * rr - mi * ri, xi + mr * ri + mi * rr
            xr, xi = xr + tabs[6] * cr - tabs[7] * ci, xi + tabs[6] * ci + tabs[7] * cr
            dst_re[pl.ds(row, 8), sl] = xr
            dst_im[pl.ds(row, 8), sl] = xi
            ncr = jnp.broadcast_to(xr[edge:edge + 1, :], (8, SCAN_W))
            nci = jnp.broadcast_to(xi[edge:edge + 1, :], (8, SCAN_W))
            if s_refs is None:
                return ncr, nci
            amt = 7 if reverse else 1
            far = 7 if reverse else 0
            nr = jnp.where(row_id == far, cr, pltpu.roll(xr, amt, 0))
            ni = jnp.where(row_id == far, ci, pltpu.roll(xi, amt, 0))
            sr = s_refs[0][pl.ds(row, 8), sl]
            si = s_refs[1][pl.ds(row, 8), sl]
            return ncr, nci, c[2] + nr * sr + ni * si, c[3] + ni * sr - nr * si

        init = (carry_re[:, sl], carry_im[:, sl])
        if s_refs is not None:
            init = init + (jnp.zeros((8, SCAN_W), F32), jnp.zeros((8, SCAN_W), F32))
        out = lax.fori_loop(0, ng, step, init)
        carry_re[:, sl] = out[0]
        carry_im[:, sl] = out[1]
        if s_refs is not None:
            sums.append((jnp.sum(out[2], axis=0, keepdims=True), jnp.sum(out[3], axis=0, keepdims=True)))
    return sums


def _ssm_scan_fwd(u, b_re, b_im, c_re, c_im, f2, tables, k, reverse, comm=None):
    L = u.shape[0]
    nc = L // TC
    chunk = (lambda i: nc - 1 - i) if reverse else (lambda i: i)

    def body(u_ref, bre_ref, bim_ref, cre_ref, cim_ref, f_ref, tab_ref,
             y_ref, sre_ref, sim_ref, in_re, in_im, carry_re, carry_im):
        @pl.when(pl.program_id(0) == 0)
        def _():
            carry_re[...] = jnp.zeros_like(carry_re)
            carry_im[...] = jnp.zeros_like(carry_im)

        ub = u_ref[...].astype(BF16)
        for q in range(N_QUAD):
            qs = slice(q * QUAD, (q + 1) * QUAD)
            us = ub[:, (q // 4) * SLAB:(q // 4 + 1) * SLAB]
            bur = _dot_nt(us, bre_ref[q])
            bui = _dot_nt(us, bim_ref[q])
            fr = f_ref[0:1, qs]
            fi = f_ref[1:2, qs]
            in_re[:, qs] = fr * bur - fi * bui
            in_im[:, qs] = fr * bui + fi * bur
        _scan_rows(in_re, in_im, sre_ref, sim_ref, tab_ref, k, carry_re, carry_im, TC, reverse)
        for j in range(D_SSM // SLAB):
            acc = jnp.zeros((TC, SLAB), F32)
            for q in range(4 * j, 4 * j + 4):
                qs = slice(q * QUAD, (q + 1) * QUAD)
                acc = acc + _dot_nt(sre_ref[:, qs].astype(BF16), cre_ref[q])
                acc = acc - _dot_nt(sim_ref[:, qs].astype(BF16), cim_ref[q])
            y_ref[:, j * SLAB:(j + 1) * SLAB] = acc

    return _hosted_call(
        body, comm, name="ssm_scan_rev" if reverse else "ssm_scan_fwd", grid=(nc,),
        in_specs=[pl.BlockSpec((TC, D_SSM), lambda i: (chunk(i), 1))]
        + [_full(b_re.shape)] * 4 + [_full(f2.shape), _full(tables.shape)],
        out_specs=[pl.BlockSpec((TC, D_SSM), lambda i: (chunk(i), 0)),
                   pl.BlockSpec((TC, N_STATE), lambda i: (chunk(i), 0)),
                   pl.BlockSpec((TC, N_STATE), lambda i: (chunk(i), 0))],
        out_shape=[jax.ShapeDtypeStruct((L, D_SSM), F32), jax.ShapeDtypeStruct((L, N_STATE), F32),
                   jax.ShapeDtypeStruct((L, N_STATE), F32)],
        scratch_shapes=[pltpu.VMEM((TC, N_STATE), F32), pltpu.VMEM((TC, N_STATE), F32),
                        pltpu.VMEM((8, N_STATE), F32), pltpu.VMEM((8, N_STATE), F32)],
        args=(u, b_re, b_im, c_re, c_im, f2, tables))


def _quad_channels(q):
    c0 = (q // 4) * SLAB + (q % 4) * 4 * SSM_GROUP
    return slice(c0, c0 + 4 * SSM_GROUP)


def _ssm_scan_bwd(dy, u, s_re, s_im, b_re, b_im, c_re, c_im, f2, tables, k, reverse, comm=None):
    L = u.shape[0]
    nc = L // TC
    chunk = (lambda i: nc - 1 - i) if reverse else (lambda i: i)

    def body(dy_ref, u_ref, sre_ref, sim_ref, bre_ref, bim_ref, cre_ref, cim_ref, f_ref, tab_ref,
             du_ref, ob_re, ob_im, oc_re, oc_im, gv_ref,
             a_re, a_im, carry_re, carry_im, gbr_ref, gbi_ref, gcr_ref, gci_ref):
        @pl.when(pl.program_id(0) == 0)
        def _():
            carry_re[...] = jnp.zeros_like(carry_re)
            carry_im[...] = jnp.zeros_like(carry_im)
            for r in (gbr_ref, gbi_ref, gcr_ref, gci_ref, gv_ref):
                r[...] = jnp.zeros_like(r)

        dyb = dy_ref[...].astype(BF16)
        ub = u_ref[...].astype(BF16)
        for q in range(N_QUAD):
            qs = slice(q * QUAD, (q + 1) * QUAD)
            ds = dyb[:, (q // 4) * SLAB:(q // 4 + 1) * SLAB]
            a_re[:, qs] = _dot_nn(ds, cre_ref[q])
            a_im[:, qs] = -_dot_nn(ds, cim_ref[q])
            dq = dyb[:, _quad_channels(q)]
            gcr_ref[q] += _dot_tn(dq, sre_ref[:, qs].astype(BF16))
            gci_ref[q] -= _dot_tn(dq, sim_ref[:, qs].astype(BF16))
        sums = _scan_rows(a_re, a_im, a_re, a_im, tab_ref, k, carry_re, carry_im, TC, reverse,
                          s_refs=(sre_ref, sim_ref))
        for lt, (glr, gli) in enumerate(sums):
            sl = slice(lt * SCAN_W, (lt + 1) * SCAN_W)
            gv_ref[0:1, sl] += glr
            gv_ref[1:2, sl] += gli
        for j in range(D_SSM // SLAB):
            us = ub[:, j * SLAB:(j + 1) * SLAB]
            acc = jnp.zeros((TC, SLAB), F32)
            for q in range(4 * j, 4 * j + 4):
                qs = slice(q * QUAD, (q + 1) * QUAD)
                ar = a_re[:, qs]
                ai = a_im[:, qs]
                bur = _dot_nt(us, bre_ref[q])
                bui = _dot_nt(us, bim_ref[q])
                gv_ref[2:3, qs] += jnp.sum(ar * bur + ai * bui, axis=0, keepdims=True)
                gv_ref[3:4, qs] += jnp.sum(ai * bur - ar * bui, axis=0, keepdims=True)
                fr = f_ref[0:1, qs]
                fi = f_ref[1:2, qs]
                dbr = (fr * ar + fi * ai).astype(BF16)
                dbi = (fr * ai - fi * ar).astype(BF16)
                uq = ub[:, _quad_channels(q)]
                gbr_ref[q] += _dot_tn(uq, dbr)
                gbi_ref[q] += _dot_tn(uq, dbi)
                acc = acc + _dot_nn(dbr, bre_ref[q]) + _dot_nn(dbi, bim_ref[q])
            du_ref[:, j * SLAB:(j + 1) * SLAB] = acc

        @pl.when(pl.program_id(0) == nc - 1)
        def _():
            for g in range(N_SSM_GROUPS):
                q, gl = divmod(g, 4)
                rows = slice(gl * SSM_GROUP, (gl + 1) * SSM_GROUP)
                cols = slice(gl * SSM_STATE, (gl + 1) * SSM_STATE)
                for out, acc_ref in ((ob_re, gbr_ref), (ob_im, gbi_ref), (oc_re, gcr_ref), (oc_im, gci_ref)):
                    out[g] = acc_ref[q, rows, cols]

    gshape = jax.ShapeDtypeStruct((N_SSM_GROUPS, SSM_GROUP, SSM_STATE), F32)
    compact = pltpu.VMEM((N_QUAD, 4 * SSM_GROUP, QUAD), F32)
    return _hosted_call(
        body, comm, name="ssm_bwd_rev" if reverse else "ssm_bwd_fwd", grid=(nc,),
        in_specs=[pl.BlockSpec((TC, D_SSM), lambda i: (chunk(i), 0)),
                  pl.BlockSpec((TC, D_SSM), lambda i: (chunk(i), 1)),
                  pl.BlockSpec((TC, N_STATE), lambda i: (chunk(i), 0)),
                  pl.BlockSpec((TC, N_STATE), lambda i: (chunk(i), 0))]
        + [_full(b_re.shape)] * 4 + [_full(f2.shape), _full(tables.shape)],
        out_specs=[pl.BlockSpec((TC, D_SSM), lambda i: (chunk(i), 0))] + [_full(gshape.shape)] * 4
        + [_full((4, N_STATE))],
        out_shape=[jax.ShapeDtypeStruct((L, D_SSM), F32), gshape, gshape, gshape, gshape,
                   jax.ShapeDtypeStruct((4, N_STATE), F32)],
        scratch_shapes=[pltpu.VMEM((TC, N_STATE), F32), pltpu.VMEM((TC, N_STATE), F32),
                        pltpu.VMEM((8, N_STATE), F32), pltpu.VMEM((8, N_STATE), F32),
                        compact, compact, compact, compact],
        args=(dy, u, s_re, s_im, b_re, b_im, c_re, c_im, f2, tables))


def _ssm_post(yf, yb, u, d, glu_w, glu_b):
    y = yf + yb + d * u
    z, t = _gelu(y)
    zb = z.astype(BF16)
    gate = _sigmoid(_dot_nn(zb, glu_w) + glu_b)
    return y, z, t, zb, gate


def _mix_out(yn_pool, yf, yb, u, x, ssm_d, glu_w_b, glu_b, g_ssm, w_out_b, g_ffn):
    L, D = x.shape

    def body(ynp_ref, yf_ref, yb_ref, u_ref, x_ref, d_ref, gw_ref, gb_ref, gs_ref, wo_ref, gf_ref,
             h1_ref, hn_ref, ycat_ref):
        _, z, _, _, gate = _ssm_post(yf_ref[...], yb_ref[...], u_ref[...], d_ref[...], gw_ref[...], gb_ref[...])
        yns, _, _ = _rms_fwd(z * gate, gs_ref[...])
        ynsb = yns.astype(BF16)
        ynp = ynp_ref[...]
        ycat_ref[:, 0:D_POOL] = ynp
        ycat_ref[:, D_POOL:D] = ynsb
        h1 = x_ref[...] + _dot_nn(ynp, wo_ref[0:D_POOL, :]) + _dot_nn(ynsb, wo_ref[D_POOL:D, :])
        h1_ref[...] = h1
        hn, _, _ = _rms_fwd(h1, gf_ref[...])
        hn_ref[...] = hn.astype(BF16)

    half = lambda c: pl.BlockSpec((TL, D_SSM), lambda i: (i, c))
    row = pl.BlockSpec((TL, D), lambda i: (i, 0))
    return pl.pallas_call(
        body, name="mix_out", grid=(L // TL,),
        in_specs=[half(0), half(0), half(0), half(1), row, _full((1, D_SSM)), _full(glu_w_b.shape),
                  _full((1, D_SSM)), _full((1, D_SSM)), _full(w_out_b.shape), _full((1, D))],
        out_specs=[row, row, row],
        out_shape=[jax.ShapeDtypeStruct((L, D), F32), jax.ShapeDtypeStruct((L, D), BF16),
                   jax.ShapeDtypeStruct((L, D), BF16)],
        compiler_params=_cp("parallel"))(yn_pool, yf, yb, u, x, ssm_d, glu_w_b, glu_b, g_ssm, w_out_b, g_ffn)


def _ssm_bwd_local(dh1, yf, yb, u, ssm_d, glu_w_b, glu_b, g_ssm, w_out_b):
    L, D = dh1.shape

    def body(dh_ref, yf_ref, yb_ref, u_ref, d_ref, gw_ref, gb_ref, gs_ref, wo_ref,
             dy_ref, du_ref, ggw_ref, ggb_ref, gd_ref, ggs_ref):
        @pl.when(pl.program_id(0) == 0)
        def _():
            for r in (ggw_ref, ggb_ref, gd_ref, ggs_ref):
                r[...] = jnp.zeros_like(r)

        u = u_ref[...]
        d = d_ref[...]
        y, z, t, zb, gate = _ssm_post(yf_ref[...], yb_ref[...], u, d, gw_ref[...], gb_ref[...])
        gs = gs_ref[...]
        _, xh, inv = _rms_fwd(z * gate, gs)
        d_yn = _dot_nt(dh_ref[...].astype(BF16), wo_ref[...])
        d_o, dgs = _rms_bwd(d_yn, xh, inv, gs)
        ggs_ref[...] += dgs
        d_zg = d_o * z * gate * (1.0 - gate)
        d_zgb = d_zg.astype(BF16)
        ggb_ref[...] += jnp.sum(d_zg, axis=0, keepdims=True)
        ggw_ref[...] += _dot_tn(zb, d_zgb)
        d_z = d_o * gate + _dot_nt(d_zgb, gw_ref[...])
        d_y = d_z * _gelu_grad(y, t)
        gd_ref[...] += jnp.sum(d_y * u, axis=0, keepdims=True)
        dy_ref[...] = d_y
        du_ref[...] = d_y * d

    half = lambda c: pl.BlockSpec((TL, D_SSM), lambda i: (i, c))
    vec = _full((1, D_SSM))
    return pl.pallas_call(
        body, name="ssm_bwd_local", grid=(L // TL,),
        in_specs=[pl.BlockSpec((TL, D), lambda i: (i, 0)), half(0), half(0), half(1), vec, _full(glu_w_b.shape),
                  vec, vec, pl.BlockSpec((D_SSM, D), lambda i: (1, 0))],
        out_specs=[half(0), half(0), _full(glu_w_b.shape), vec, vec, vec],
        out_shape=[jax.ShapeDtypeStruct((L, D_SSM), F32), jax.ShapeDtypeStruct((L, D_SSM), F32),
                   jax.ShapeDtypeStruct(glu_w_b.shape, F32)] + [jax.ShapeDtypeStruct((1, D_SSM), F32)] * 3,
        compiler_params=_cp("arbitrary"))(dh1, yf, yb, u, ssm_d, glu_w_b, glu_b, g_ssm, w_out_b)


def _in_bwd(du_pool, du_a, du_b, du_c, dh1, x, g, w_in_b):
    L, D = x.shape

    def body(p_ref, a_ref, b_ref, c_ref, dh_ref, x_ref, g_ref, w_ref, dx_ref, dub_ref, gg_ref):
        @pl.when(pl.program_id(0) == 0)
        def _():
            gg_ref[...] = jnp.zeros_like(gg_ref)

        dub_ref[:, 0:D_POOL] = p_ref[...].astype(BF16)
        dub_ref[:, D_POOL:D] = (a_ref[...] + b_ref[...] + c_ref[...]).astype(BF16)
        d_xn = _dot_nt(dub_ref[...], w_ref[...])
        gv = g_ref[...]
        _, xh, inv = _rms_fwd(x_ref[...], gv)
        dx, dg = _rms_bwd(d_xn, xh, inv, gv)
        gg_ref[...] += dg
        dx_ref[...] = dh_ref[...] + dx

    half = pl.BlockSpec((TL, D_SSM), lambda i: (i, 0))
    row = pl.BlockSpec((TL, D), lambda i: (i, 0))
    return pl.pallas_call(
        body, name="in_bwd", grid=(L // TL,),
        in_specs=[half, half, half, half, row, row, _full((1, D)), _full(w_in_b.shape)],
        out_specs=[row, row, _full((1, D))],
        out_shape=[jax.ShapeDtypeStruct((L, D), F32), jax.ShapeDtypeStruct((L, D), BF16),
                   jax.ShapeDtypeStruct((1, D), F32)],
        compiler_params=_cp("arbitrary"))(du_pool, du_a, du_b, du_c, dh1, x, g, w_in_b)


def _ffn_up(hn, w_up4):
    L, D = hn.shape

    def body(h_ref, w_ref, o_ref):
        o_ref[...] = _dot_nn(h_ref[...], w_ref[...]).astype(BF16)

    return pl.pallas_call(
        body, name="ffn_up", grid=(4, L // TL),
        in_specs=[pl.BlockSpec((TL, D), lambda j, i: (i, 0)), pl.BlockSpec((None, D, FF_BLK), lambda j, i: (j, 0, 0))],
        out_specs=pl.BlockSpec((TL, FF_BLK), lambda j, i: (i, j)),
        out_shape=jax.ShapeDtypeStruct((L, 4 * FF_BLK), BF16),
        compiler_params=_cp("parallel", "parallel"))(hn, w_up4)


def _halo_specs_2d(rows, width, L, col, order):
    rb = rows // HALO_B
    last = L // HALO_B - 1
    if order == "ik":
        wrap = lambda f: (lambda i, k: f(i, k))
    else:
        wrap = lambda f: (lambda k, i: f(i, k))
    return [pl.BlockSpec((HALO_B, width), wrap(lambda i, k: (jnp.maximum(i * rb - 1, 0), col(k)))),
            pl.BlockSpec((rows, width), wrap(lambda i, k: (i, col(k)))),
            pl.BlockSpec((HALO_B, width), wrap(lambda i, k: (jnp.minimum((i + 1) * rb, last), col(k))))]


def _shift_mats(rows):
    r = lax.broadcasted_iota(jnp.int32, (rows, rows), 0)
    c = lax.broadcasted_iota(jnp.int32, (rows, rows), 1)
    return (c == r - 1).astype(BF16), (c == r + 1).astype(BF16)


def _neighbours(x, prev_ref, next_ref, cs, i, n, mats):
    rows = x.shape[0]
    row = lax.broadcasted_iota(jnp.int32, (rows, 1), 0)
    before = jnp.where(i > 0, prev_ref[:, cs].astype(F32)[HALO_B - 1:HALO_B, :], 0.0)
    after = jnp.where(i < n - 1, next_ref[:, cs].astype(F32)[0:1, :], 0.0)
    if mats is None:
        xf = x.astype(F32)
        down, up = pltpu.roll(xf, 1, 0), pltpu.roll(xf, rows - 1, 0)
    else:
        down, up = _dot_nn(mats[0], x), _dot_nn(mats[1], x)
    return jnp.where(row == 0, before, down), jnp.where(row == rows - 1, after, up)


def _conv3(x, before, after, w, b):
    return before * w[0:1, :] + x.astype(F32) * w[1:2, :] + after * w[2:3, :] + b


def _col_chunks(width, size=256):
    return [slice(c, min(c + size, width)) for c in range(0, width, size)]


def _ffn_down_loss(up, conv_w, conv_b, w_down_b, h1, target, g_final):
    L, D = h1.shape
    n = L // TF
    nk = D_FF // FF_BLK

    def body(vp, vc, vn, gp, gc, gn, wv_ref, wg_ref, bv_ref, bg_ref, wd_ref, h1_ref, t_ref, gf_ref,
             a_ref, cv_ref, cg_ref, dh2_ref, dh2b_ref, loss_ref, gg_ref, acc_ref):
        i = pl.program_id(0)
        k = pl.program_id(1)

        @pl.when((i == 0) & (k == 0))
        def _():
            loss_ref[...] = jnp.zeros_like(loss_ref)
            gg_ref[...] = jnp.zeros_like(gg_ref)

        @pl.when(k == 0)
        def _():
            acc_ref[...] = jnp.zeros_like(acc_ref)

        mats = _shift_mats(TF)
        for cs in _col_chunks(FF_BLK):
            xv, xg = vc[:, cs], gc[:, cs]
            val = _conv3(xv, *_neighbours(xv, vp, vn, cs, i, n, mats), wv_ref[:, cs], bv_ref[:, cs])
            gate = _conv3(xg, *_neighbours(xg, gp, gn, cs, i, n, mats), wg_ref[:, cs], bg_ref[:, cs])
            a_ref[:, cs] = (val * (gate * _sigmoid(gate))).astype(BF16)
            cv_ref[:, cs] = val.astype(BF16)
            cg_ref[:, cs] = gate.astype(BF16)
        acc_ref[...] += _dot_nn(a_ref[...], wd_ref[pl.ds(pl.multiple_of(k * FF_BLK, LANES), FF_BLK), :])

        @pl.when(k == nk - 1)
        def _():
            gf = gf_ref[...]
            y, xh, inv = _rms_fwd(h1_ref[...] + acc_ref[...], gf)
            diff = y - t_ref[...]
            part = 0.5 * jnp.sum(jnp.mean(diff * diff, axis=-1, keepdims=True), axis=0, keepdims=True)
            loss_ref[...] += jnp.broadcast_to(part, loss_ref.shape)
            dx, dg = _rms_bwd(diff * (1.0 / D), xh, inv, gf)
            gg_ref[...] += dg
            dh2_ref[...] = dx
            dh2b_ref[...] = dx.astype(BF16)

    row = pl.BlockSpec((TF, D), lambda i, k: (i, 0))
    cw = lambda off: pl.BlockSpec((3, FF_BLK), lambda i, k: (0, k + off))
    cb = lambda off: pl.BlockSpec((1, FF_BLK), lambda i, k: (0, k + off))
    return pl.pallas_call(
        body, name="ffn_down_loss", grid=(n, nk),
        in_specs=_halo_specs_2d(TF, FF_BLK, L, lambda k: k, "ik") + _halo_specs_2d(TF, FF_BLK, L, lambda k: k + nk, "ik")
        + [cw(0), cw(nk), cb(0), cb(nk), _full(w_down_b.shape), row, row, _full((1, D))],
        out_specs=[pl.BlockSpec((TF, FF_BLK), lambda i, k: (i, k))] * 3 + [row, row, _full((1, LANES)), _full((1, D))],
        out_shape=[jax.ShapeDtypeStruct((L, D_FF), BF16)] * 3
        + [jax.ShapeDtypeStruct((L, D), F32), jax.ShapeDtypeStruct((L, D), BF16),
           jax.ShapeDtypeStruct((1, LANES), F32), jax.ShapeDtypeStruct((1, D), F32)],
        scratch_shapes=[pltpu.VMEM((TF, D), F32)],
        compiler_params=_cp("arbitrary", "arbitrary"))(
            up, up, up, up, up, up, conv_w, conv_w, conv_b, conv_b, w_down_b, h1, target, g_final)


def _ffn_act_bwd(c_val, c_gate, w_down_b, dh2):
    L, D = dh2.shape
    n = L // TF
    nk = D_FF // FF_BLK

    def body(v_ref, g_ref, wd_ref, dh_ref, dv_ref, dg_ref, gbv_ref, gbg_ref):
        @pl.when(pl.program_id(1) == 0)
        def _():
            gbv_ref[...] = jnp.zeros_like(gbv_ref)
            gbg_ref[...] = jnp.zeros_like(gbg_ref)

        dh = dh_ref[...]
        for cs in _col_chunks(FF_BLK):
            val, gate = v_ref[:, cs].astype(F32), g_ref[:, cs].astype(F32)
            d_a = _dot_nt(dh, wd_ref[cs, :])
            sg = _sigmoid(gate)
            d_val = d_a * (gate * sg)
            d_gate = d_a * val * (sg * (1.0 + gate * (1.0 - sg)))
            dv_ref[:, cs] = d_val.astype(BF16)
            dg_ref[:, cs] = d_gate.astype(BF16)
            gbv_ref[:, cs] += jnp.sum(d_val, axis=0, keepdims=True)
            gbg_ref[:, cs] += jnp.sum(d_gate, axis=0, keepdims=True)

    blk = pl.BlockSpec((TF, FF_BLK), lambda k, i: (i, k))
    acc = pl.BlockSpec((1, FF_BLK), lambda k, i: (0, k))
    return pl.pallas_call(
        body, name="ffn_act_bwd", grid=(nk, n),
        in_specs=[blk, blk, pl.BlockSpec((FF_BLK, D), lambda k, i: (k, 0)), pl.BlockSpec((TF, D), lambda k, i: (i, 0))],
        out_specs=[blk, blk, acc, acc],
        out_shape=[jax.ShapeDtypeStruct((L, D_FF), BF16), jax.ShapeDtypeStruct((L, D_FF), BF16),
                   jax.ShapeDtypeStruct((1, D_FF), F32), jax.ShapeDtypeStruct((1, D_FF), F32)],
        compiler_params=_cp("arbitrary", "arbitrary"))(c_val, c_gate, w_down_b, dh2)


def _ffn_up_bwd(d_val, d_gate, up, conv_w, w_up4, h1, dh2, g_ffn):
    L, D = h1.shape
    n = L // TF
    nk = D_FF // FF_BLK

    def body(vp, vc, vn, gp, gc, gn, uv_ref, ug_ref, wv_ref, wg_ref, wu_ref, h1_ref, dh2_ref, g_ref,
             dup_ref, dh1_ref, dh1b_ref, gg_ref, gcw_ref, acc_ref):
        i = pl.program_id(0)
        k = pl.program_id(1)

        @pl.when((i == 0) & (k == 0))
        def _():
            gg_ref[...] = jnp.zeros_like(gg_ref)
            gcw_ref[...] = jnp.zeros_like(gcw_ref)

        @pl.when(k == 0)
        def _():
            acc_ref[...] = jnp.zeros_like(acc_ref)

        for j, (blocks, u_ref, w_ref) in enumerate((((vp, vc, vn), uv_ref, wv_ref), ((gp, gc, gn), ug_ref, wg_ref))):
            for cs in _col_chunks(FF_BLK, LANES):
                d = blocks[1][:, cs]
                before, after = _neighbours(d, blocks[0], blocks[2], cs, i, n, None)
                taps = (after, d.astype(F32), before)
                w = w_ref[:, cs]
                dup_ref[j, :, cs] = (taps[0] * w[0:1, :] + taps[1] * w[1:2, :] + taps[2] * w[2:3, :]).astype(BF16)
                x = u_ref[:, cs].astype(F32)
                for r in range(3):
                    gcw_ref[j, k, r:r + 1, cs] += jnp.sum(taps[r] * x, axis=0, keepdims=True)
            acc_ref[...] += _dot_nt(dup_ref[j], wu_ref[k + j * nk])

        @pl.when(k == nk - 1)
        def _():
            g = g_ref[...]
            _, xh, inv = _rms_fwd(h1_ref[...], g)
            dx, dg = _rms_bwd(acc_ref[...], xh, inv, g)
            gg_ref[...] += dg
            dh1 = dh2_ref[...] + dx
            dh1_ref[...] = dh1
            dh1b_ref[...] = dh1.astype(BF16)

    row = pl.BlockSpec((TF, D), lambda i, k: (i, 0))
    cw = lambda off: pl.BlockSpec((3, FF_BLK), lambda i, k: (0, k + off))
    tile = lambda off: pl.BlockSpec((TF, FF_BLK), lambda i, k: (i, k + off))
    return pl.pallas_call(
        body, name="ffn_up_bwd", grid=(n, nk),
        in_specs=_halo_specs_2d(TF, FF_BLK, L, lambda k: k, "ik") + _halo_specs_2d(TF, FF_BLK, L, lambda k: k, "ik")
        + [tile(0), tile(nk), cw(0), cw(nk), _full(w_up4.shape), row, row, _full((1, D))],
        out_specs=[pl.BlockSpec((2, None, TF, FF_BLK), lambda i, k: (0, k, i, 0)), row, row, _full((1, D)),
                   _full((2, nk, 3, FF_BLK))],
        out_shape=[jax.ShapeDtypeStruct((2, nk, L, FF_BLK), BF16), jax.ShapeDtypeStruct((L, D), F32),
                   jax.ShapeDtypeStruct((L, D), BF16), jax.ShapeDtypeStruct((1, D), F32),
                   jax.ShapeDtypeStruct((2, nk, 3, FF_BLK), F32)],
        scratch_shapes=[pltpu.VMEM((TF, D), F32)],
        compiler_params=_cp("arbitrary", "arbitrary"))(
            d_val, d_val, d_val, d_gate, d_gate, d_gate, up, up, conv_w, conv_w, w_up4, h1, dh2, g_ffn)


def _matmul_tn(a, b, tm, tn, name, tk=2048):
    L, M = a.shape
    N = b.shape[1]
    tk = min(tk, L)

    def body(a_ref, b_ref, o_ref):
        @pl.when(pl.program_id(2) == 0)
        def _():
            o_ref[...] = jnp.zeros_like(o_ref)

        o_ref[...] += _dot_tn(a_ref[...], b_ref[...])

    return pl.pallas_call(
        body, name=name, grid=(M // tm, N // tn, L // tk),
        in_specs=[pl.BlockSpec((tk, tm), lambda m, n, l: (l, m)), pl.BlockSpec((tk, tn), lambda m, n, l: (l, n))],
        out_specs=pl.BlockSpec((tm, tn), lambda m, n, l: (m, n)),
        out_shape=jax.ShapeDtypeStruct((M, N), F32),
        compiler_params=_cp("parallel", "parallel", "arbitrary"))(a, b)


def _matmul_tn_blocks(a, b, tm, name, tk=2048):
    L, M = a.shape
    J, _, N = b.shape
    tk = min(tk, L)

    def body(a_ref, b_ref, o_ref):
        @pl.when(pl.program_id(2) == 0)
        def _():
            o_ref[...] = jnp.zeros_like(o_ref)

        o_ref[...] += _dot_tn(a_ref[...], b_ref[...])

    return pl.pallas_call(
        body, name=name, grid=(M // tm, J, L // tk),
        in_specs=[pl.BlockSpec((tk, tm), lambda m, j, l: (l, m)), pl.BlockSpec((None, tk, N), lambda m, j, l: (j, l, 0))],
        out_specs=pl.BlockSpec((None, tm, N), lambda m, j, l: (j, m, 0)),
        out_shape=jax.ShapeDtypeStruct((J, M, N), F32),
        compiler_params=_cp("parallel", "parallel", "arbitrary"))(a, b)


def _row_tile(rows):
    for t in (512, 352, 256, 128, 64, 8):
        if rows % t == 0:
            return t
    return rows


def _add_half(g, r, c_arr, name, out_dtype=F32):
    _, _, R, C = g.shape
    tr = _row_tile(R)

    def body(c_ref, g_ref, r_ref, o_ref):
        o_ref[...] = (g_ref[...] + r_ref[...]).astype(out_dtype)

    return pl.pallas_call(
        body, name=name,
        grid_spec=pltpu.PrefetchScalarGridSpec(
            num_scalar_prefetch=1, grid=(g.shape[0], R // tr),
            in_specs=[pl.BlockSpec((None, None, tr, C), lambda j, i, c: (j, c[0], i, 0)),
                      pl.BlockSpec((None, tr, C), lambda j, i, c: (j, i, 0))],
            out_specs=pl.BlockSpec((None, tr, C), lambda j, i, c: (j, i, 0))),
        out_shape=jax.ShapeDtypeStruct(r.shape, out_dtype),
        compiler_params=_cp("parallel", "parallel"))(c_arr, g, r)


def _add2(a, b, name):
    R, C = a.shape
    tr = _row_tile(R)

    def body(a_ref, b_ref, o_ref):
        o_ref[...] = a_ref[...] + b_ref[...]

    spec = pl.BlockSpec((tr, C), lambda i: (i, 0))
    return pl.pallas_call(body, name=name, grid=(R // tr,), in_specs=[spec, spec], out_specs=spec,
                          out_shape=jax.ShapeDtypeStruct(a.shape, F32), compiler_params=_cp("parallel"))(a, b)


def _sum4(p, name):
    _, R, C = p.shape
    tr = _row_tile(R)

    def body(p_ref, o_ref):
        q = [p_ref[j].astype(F32) for j in range(4)]
        o_ref[...] = ((q[0] + q[1]) + q[2]) + q[3]

    return pl.pallas_call(
        body, name=name, grid=(R // tr,),
        in_specs=[pl.BlockSpec((4, tr, C), lambda i: (0, i, 0))],
        out_specs=pl.BlockSpec((tr, C), lambda i: (i, 0)),
        out_shape=jax.ShapeDtypeStruct((R, C), F32), compiler_params=_cp("parallel"))(p)


def _adamw_refs(w_ref, g_ref, m_ref, v_ref, d_ref, nm_ref, nv_ref):
    gv = g_ref[...]
    nm = ADAM_B1 * m_ref[...] + (1.0 - ADAM_B1) * gv
    nv = ADAM_B2 * v_ref[...] + (1.0 - ADAM_B2) * (gv * gv)
    m_hat = nm / (1.0 - ADAM_B1 ** ADAM_STEP)
    v_hat = nv / (1.0 - ADAM_B2 ** ADAM_STEP)
    d_ref[...] = -ADAM_LR * (m_hat / (jnp.sqrt(v_hat) + ADAM_EPS) + ADAM_WD * w_ref[...])
    nm_ref[...] = nm
    nv_ref[...] = nv


def _adamw_many(ws, gs, ms, vs, name):
    n = len(ws)

    def body(*refs):
        for k in range(n):
            _adamw_refs(*(refs[j * n + k] for j in range(7)))

    out_shape = [jax.ShapeDtypeStruct(w.shape, F32) for w in ws] * 3
    res = pl.pallas_call(body, name=name, out_shape=out_shape,
                         compiler_params=pltpu.CompilerParams(vmem_limit_bytes=VMEM_LIMIT))(*ws, *gs, *ms, *vs)
    return res[:n], res[n:2 * n], res[2 * n:]


def _adamw(w, g, m, v, name):
    R, C = w.shape
    tr = _row_tile(R)
    body = lambda *refs: _adamw_refs(*refs)

    spec = pl.BlockSpec((tr, C), lambda i: (i, 0))
    sh = jax.ShapeDtypeStruct((R, C), F32)
    return pl.pallas_call(body, name=name, grid=(R // tr,), in_specs=[spec] * 4, out_specs=[spec] * 3,
                          out_shape=[sh] * 3, compiler_params=_cp("parallel"))(w, g, m, v)


def _join_rows(own, other, c_arr, name):
    R, C = own.shape
    tr = _row_tile(R)

    def body(c_ref, own_ref, other_ref, o_ref):
        o_ref[...] = jnp.where(pl.program_id(0) == c_ref[0], own_ref[...], other_ref[...])

    half = pl.BlockSpec((tr, C), lambda h, i, c: (i, 0))
    return pl.pallas_call(
        body, name=name,
        grid_spec=pltpu.PrefetchScalarGridSpec(
            num_scalar_prefetch=1, grid=(2, R // tr), in_specs=[half, half],
            out_specs=pl.BlockSpec((tr, C), lambda h, i, c: (h * (R // tr) + i, 0))),
        out_shape=jax.ShapeDtypeStruct((2 * R, C), F32),
        compiler_params=_cp("parallel", "parallel"))(c_arr, own, other)


def _adamw_halves(w, own, other, m, v, c_arr, name):
    R, C = own.shape
    tr = _row_tile(R)
    while tr * C * 4 > 2 ** 20 and tr % 16 == 0:
        tr //= 2

    def body(c_ref, w_ref, own_ref, other_ref, m_ref, v_ref, g_ref, d_ref, nm_ref, nv_ref):
        g_ref[...] = jnp.where(pl.program_id(0) == c_ref[0], own_ref[...], other_ref[...])
        _adamw_refs(w_ref, g_ref, m_ref, v_ref, d_ref, nm_ref, nv_ref)

    half = pl.BlockSpec((tr, C), lambda h, i, c: (i, 0))
    full = pl.BlockSpec((tr, C), lambda h, i, c: (h * (R // tr) + i, 0))
    sh = jax.ShapeDtypeStruct((2 * R, C), F32)
    return pl.pallas_call(
        body, name=name,
        grid_spec=pltpu.PrefetchScalarGridSpec(
            num_scalar_prefetch=1, grid=(2, R // tr), in_specs=[full, half, half, full, full], out_specs=[full] * 4),
        out_shape=[sh] * 4, compiler_params=_cp("parallel", "parallel"))(c_arr, w, own, other, m, v)


_ANY = pl.BlockSpec(memory_space=pl.ANY)


def _position():
    return lax.axis_index("x"), lax.axis_index("y"), lax.axis_index("c")


class _Comm:
    def __init__(self, arrs, out_shape, sems, start, finish):
        self.arrs, self.out_shape, self.sems, self.start, self.finish = arrs, out_shape, sems, start, finish


def _comm_call(comm, name):
    n, m = len(comm.arrs), len(comm.out_shape)

    def body(*refs):
        ins, outs, sems = refs[:n], refs[n:n + m], refs[n + m:]
        comm.start(ins, outs, sems)
        comm.finish(ins, outs, sems)

    return pl.pallas_call(
        body, name=name, in_specs=[_ANY] * n, out_specs=[_ANY] * m, out_shape=comm.out_shape,
        scratch_shapes=comm.sems, compiler_params=pltpu.CompilerParams(has_side_effects=True))(*comm.arrs)


def _hosted_call(body, comm, *, name, grid, in_specs, out_specs, out_shape, scratch_shapes, args):
    sem = ("arbitrary",) * len(grid)
    if comm is None:
        return pl.pallas_call(body, name=name, grid=grid, in_specs=in_specs, out_specs=out_specs, out_shape=out_shape,
                              scratch_shapes=scratch_shapes, compiler_params=_cp(*sem))(*args), []
    n_in, n_out, n_scr = len(in_specs), len(out_specs), len(scratch_shapes)
    ci, co = len(comm.arrs), len(comm.out_shape)

    def full(*refs):
        ins, refs = refs[:n_in], refs[n_in:]
        cins, refs = refs[:ci], refs[ci:]
        outs, refs = refs[:n_out], refs[n_out:]
        couts, refs = refs[:co], refs[co:]
        scr, csems = refs[:n_scr], refs[n_scr:]
        first, last = True, True
        for d, size in enumerate(grid):
            first = first & (pl.program_id(d) == 0)
            last = last & (pl.program_id(d) == size - 1)

        @pl.when(first)
        def _():
            comm.start(cins, couts, csems)

        body(*ins, *outs, *scr)

        @pl.when(last)
        def _():
            comm.finish(cins, couts, csems)

    res = pl.pallas_call(
        full, name=name, grid=grid, in_specs=list(in_specs) + [_ANY] * ci, out_specs=list(out_specs) + [_ANY] * co,
        out_shape=list(out_shape) + list(comm.out_shape), scratch_shapes=list(scratch_shapes) + list(comm.sems),
        compiler_params=_cp(*sem))(*args, *comm.arrs)
    return res[:n_out], res[n_out:]


def _comm_join(*comms):
    def parts(xs, attr):
        out, at = [], 0
        for cm in comms:
            n = len(getattr(cm, attr))
            out.append(xs[at:at + n])
            at += n
        return out

    def start(ins, outs, sems):
        for cm, i, o, s in zip(comms, parts(ins, "arrs"), parts(outs, "out_shape"), parts(sems, "sems")):
            cm.start(i, o, s)

    def finish(ins, outs, sems):
        for cm, i, o, s in zip(comms, parts(ins, "arrs"), parts(outs, "out_shape"), parts(sems, "sems")):
            cm.finish(i, o, s)

    cat = lambda attr: [x for cm in comms for x in getattr(cm, attr)]
    return _Comm(cat("arrs"), cat("out_shape"), cat("sems"), start, finish)


def _dma_sems(*counts):
    return [pltpu.SemaphoreType.DMA((n,)) for n in counts]


def _comm_pair_swap(arrs, half=False):
    n = len(arrs)
    out_shape = [jax.ShapeDtypeStruct(a.shape[:1] + a.shape[2:] if half else a.shape, a.dtype) for a in arrs]

    def copies(ins, outs, sems):
        x, y, c = _position()
        return [pltpu.make_async_remote_copy(
            src_ref=ins[k].at[:, 1 - c] if half else ins[k], dst_ref=outs[k], send_sem=sems[0].at[k],
            recv_sem=sems[1].at[k], device_id=(x, y, 1 - c), device_id_type=MESH) for k in range(n)]

    def start(ins, outs, sems):
        for cp in copies(ins, outs, sems):
            cp.start()

    def finish(ins, outs, sems):
        for cp in copies(ins, outs, sems):
            cp.wait()

    return _Comm(arrs, out_shape, _dma_sems(n, n), start, finish)


def _chip_of(j, c):
    return (jnp.right_shift(j, 1), jnp.bitwise_and(j, 1), c)


def _comm_chip_exchange(arrs, scatter):
    n = len(arrs)
    out_shape = [jax.ShapeDtypeStruct(a.shape if scatter else (4,) + a.shape, a.dtype) for a in arrs]

    def copies(ins, outs, sems):
        x, y, c = _position()
        me = 2 * x + y
        local, sent, landed = [], [], []
        for k in range(n):
            local.append(pltpu.make_async_copy(ins[k].at[me] if scatter else ins[k], outs[k].at[me], sems[2].at[k]))
            for d in (1, 2, 3):
                j = jnp.bitwise_xor(me, d)
                s = 3 * k + d - 1
                src = ins[k].at[j] if scatter else ins[k]
                for dst, group in ((outs[k].at[me], sent), (outs[k].at[j], landed)):
                    group.append(pltpu.make_async_remote_copy(
                        src_ref=src, dst_ref=dst, send_sem=sems[0].at[s], recv_sem=sems[1].at[s],
                        device_id=_chip_of(j, c), device_id_type=MESH))
        return local, sent, landed

    def start(ins, outs, sems):
        local, sent, _ = copies(ins, outs, sems)
        for cp in local + sent:
            cp.start()

    def finish(ins, outs, sems):
        local, sent, landed = copies(ins, outs, sems)
        for cp in sent:
            cp.wait_send()
        for cp in landed:
            cp.wait_recv()
        for cp in local:
            cp.wait()

    return _Comm(arrs, out_shape, _dma_sems(3 * n, 3 * n, n), start, finish)


def _comm_pair_gather(arrs):
    n = len(arrs)
    out_shape = [jax.ShapeDtypeStruct((2,) + a.shape, a.dtype) for a in arrs]

    def copies(ins, outs, sems):
        x, y, c = _position()
        local, sent, landed = [], [], []
        for k in range(n):
            local.append(pltpu.make_async_copy(ins[k], outs[k].at[c], sems[2].at[k]))
            for dst, group in ((outs[k].at[c], sent), (outs[k].at[1 - c], landed)):
                group.append(pltpu.make_async_remote_copy(
                    src_ref=ins[k], dst_ref=dst, send_sem=sems[0].at[k], recv_sem=sems[1].at[k],
                    device_id=(x, y, 1 - c), device_id_type=MESH))
        return local, sent, landed

    def start(ins, outs, sems):
        local, sent, _ = copies(ins, outs, sems)
        for cp in local + sent:
            cp.start()

    def finish(ins, outs, sems):
        local, sent, landed = copies(ins, outs, sems)
        for cp in sent:
            cp.wait_send()
        for cp in landed:
            cp.wait_recv()
        for cp in local:
            cp.wait()

    return _Comm(arrs, out_shape, _dma_sems(n, n, n), start, finish)


LOCAL_PARTS = 4


def _comm_gather_split(shards, whole):
    n, nw = len(shards), len(whole)
    arrs = list(shards) + list(whole)
    out_shape = [jax.ShapeDtypeStruct((4,) + a.shape, a.dtype) for a in arrs]

    def copies(ins, outs, sems):
        x, y, c = _position()
        me = 2 * x + y
        local, sent, landed, passed, passed_in = [], [], [], [], []
        for k in range(n + nw):
            if k >= n:
                local.append(pltpu.make_async_copy(ins[k], outs[k].at[me], sems[4].at[LOCAL_PARTS * k]))
            else:
                part = shards[k].shape[0] // LOCAL_PARTS
                for r in range(LOCAL_PARTS):
                    local.append(pltpu.make_async_copy(ins[k].at[pl.ds(r * part, part)],
                                                       outs[k].at[me, pl.ds(r * part, part)],
                                                       sems[4].at[LOCAL_PARTS * k + r]))
            for d in (1, 2, 3):
                j = jnp.bitwise_xor(me, d)
                s = 3 * k + d - 1
                if k >= n:
                    src, mine, theirs = ins[k], outs[k].at[me], outs[k].at[j]
                else:
                    h = shards[k].shape[0] // 2
                    rows = pl.ds(pl.multiple_of(c * h, 16), h)
                    other = pl.ds(pl.multiple_of((1 - c) * h, 16), h)
                    src, mine, theirs = ins[k].at[rows], outs[k].at[me, rows], outs[k].at[j, rows]
                    for dst, group in ((theirs, passed), (outs[k].at[j, other], passed_in)):
                        group.append(pltpu.make_async_remote_copy(
                            src_ref=theirs, dst_ref=dst, send_sem=sems[2].at[s], recv_sem=sems[3].at[s],
                            device_id=(x, y, 1 - c), device_id_type=MESH))
                for dst, group in ((mine, sent), (theirs, landed)):
                    group.append(pltpu.make_async_remote_copy(
                        src_ref=src, dst_ref=dst, send_sem=sems[0].at[s], recv_sem=sems[1].at[s],
                        device_id=_chip_of(j, c), device_id_type=MESH))
        return local, sent, landed, passed, passed_in

    def start(ins, outs, sems):
        local, sent, _, _, _ = copies(ins, outs, sems)
        for cp in local + sent:
            cp.start()

    def finish(ins, outs, sems):
        local, sent, landed, passed, passed_in = copies(ins, outs, sems)
        for cp in landed[:3 * n]:
            cp.wait_recv()
        for cp in passed:
            cp.start()
        for cp in landed[3 * n:]:
            cp.wait_recv()
        for cp in sent:
            cp.wait_send()
        for cp in passed:
            cp.wait_send()
        for cp in passed_in:
            cp.wait_recv()
        for cp in local:
            cp.wait()

    t = 3 * (n + nw)
    return _Comm(arrs, out_shape, _dma_sems(t, t, max(3 * n, 1), max(3 * n, 1), LOCAL_PARTS * (n + nw)), start, finish)


def _pack(arrs, row_multiple):
    parts = []
    for a in arrs:
        flat = a.reshape(-1).astype(F32)
        pad = (-flat.shape[0]) % LANES
        parts.append(jnp.pad(flat, (0, pad)) if pad else flat)
    flat = jnp.concatenate(parts)
    rows = -(-flat.shape[0] // LANES)
    rows_p = -(-rows // row_multiple) * row_multiple
    return jnp.pad(flat, (0, rows_p * LANES - flat.shape[0])).reshape(rows_p, LANES)


def _unpack(packed, shapes):
    flat = packed.reshape(-1)
    outs, off = [], 0
    for sh in shapes:
        size = int(np.prod(sh))
        outs.append(flat[off:off + size].reshape(sh))
        off += size + (-size) % LANES
    return outs


SMALL = ["norm_mix_g", "pool_w", "pool_scale", "ssm_log_neg_a_re", "ssm_a_im", "ssm_log_dt", "ssm_b_re", "ssm_b_im",
         "ssm_c_re", "ssm_c_im", "ssm_d", "glu_b", "out_norm_pool_g", "out_norm_ssm_g", "norm_ffn_g", "conv_b",
         "final_norm_g"]
BIG = ["w_in", "glu_w", "w_out", "w_up", "w_down"]
WIDE = ["pool_w", "ssm_b_re", "ssm_b_im", "ssm_c_re", "ssm_c_im"]
WEIGHTS = ['norm_mix_g', 'w_in', 'pool_w', 'pool_scale', 'ssm_log_neg_a_re', 'ssm_a_im', 'ssm_log_dt', 'ssm_b_re',
           'ssm_b_im', 'ssm_c_re', 'ssm_c_im', 'ssm_d', 'glu_w', 'glu_b', 'out_norm_pool_g', 'out_norm_ssm_g', 'w_out',
           'norm_ffn_g', 'w_up', 'conv_w', 'conv_b', 'w_down', 'final_norm_g']


def _local_step(x, target, p, full, shards=None, c_arr=None):
    L, D = x.shape
    dist = shards is not None
    row = lambda a: a.reshape(1, -1)
    w_in = full["w_in"]
    pool_w_b = p["pool_w"].astype(BF16)
    g_mix, g_pool, g_ssm, g_ffn, g_fin = (row(p[k]) for k in (
        "norm_mix_g", "out_norm_pool_g", "out_norm_ssm_g", "norm_ffn_g", "final_norm_g"))
    pool_scale, ssm_d, glu_b, conv_b = (row(p[k]) for k in ("pool_scale", "ssm_d", "glu_b", "conv_b"))

    lnar = p["ssm_log_neg_a_re"].reshape(2 * N_SSM_GROUPS, SSM_STATE)
    aim = p["ssm_a_im"].reshape(2 * N_SSM_GROUPS, SSM_STATE)
    ldt = jnp.broadcast_to(p["ssm_log_dt"].reshape(2 * N_SSM_GROUPS, 1), lnar.shape)
    lam_re, lam_im, f_re, f_im = _ssm_params(lnar, aim, ldt)
    flat2 = lambda a: a.reshape(2, N_STATE)
    lam4 = jnp.stack([flat2(lam_re)[0], flat2(lam_im)[0], flat2(lam_re)[1], flat2(lam_im)[1]])
    tables = _scan_tables(lam4)
    f2 = [jnp.stack([flat2(f_re)[d], flat2(f_im)[d]]) for d in range(2)]
    dense = _ssm_expand(p["ssm_b_re"], p["ssm_b_im"], p["ssm_c_re"], p["ssm_c_im"])
    ssm_args = [tuple(dense[4 * d:4 * d + 4]) + (f2[d], tables) for d in range(2)]

    u, xn = _in_proj(x, g_mix, w_in)
    yn_pool = _pool_fwd(u, pool_w_b, pool_scale, g_pool)
    gather1 = _comm_gather_split([shards[k] for k in ("glu_w", "w_out", "w_down")], [shards["conv_w"]]) if dist else None
    (y0, s0r, s0i), got1 = _ssm_scan_fwd(u, *ssm_args[0], 0, False, comm=gather1)
    gather2 = _comm_gather_split([shards["w_up"]], []) if dist else None
    (y1, s1r, s1i), got2 = _ssm_scan_fwd(u, *ssm_args[1], 2, True, comm=gather2)
    if dist:
        glu_w, w_out, w_down = (g.reshape((-1,) + g.shape[2:]) for g in got1[:3])
        conv_w = jnp.transpose(got1[3], (1, 0, 2)).reshape(3, -1)
        w_up4 = got2[0]
    else:
        glu_w, w_out, w_up4, w_down, conv_w = (full[k] for k in ("glu_w", "w_out", "w_up", "w_down", "conv_w"))
    h1, hn, ycat = _mix_out(yn_pool, y0, y1, u, x, ssm_d, glu_w, glu_b, g_ssm, w_out, g_ffn)
    up = _ffn_up(hn, w_up4)
    a, c_val, c_gate, dh2, dh2_b, loss, g_final = _ffn_down_loss(up, conv_w, conv_b, w_down, h1, target, g_fin)

    d_val, d_gate, gbv, gbg = _ffn_act_bwd(c_val, c_gate, w_down, dh2_b)
    g_w_down = _matmul_tn(a, dh2_b, FF_BLK, D, "grad_w_down")
    d_up, dh1, dh1_b, g_ffn_g, gcw = _ffn_up_bwd(d_val, d_gate, up, conv_w, w_up4, h1, dh2, g_ffn)
    g_w_up = _matmul_tn_blocks(hn, d_up.reshape(4, L, FF_BLK), 512, "grad_w_up")
    g_w_out = _matmul_tn(ycat, dh1_b, 512, D, "grad_w_out")
    dy, du_direct, g_glu_w, g_glu_b, g_ssm_d, g_ssm_g = _ssm_bwd_local(dh1_b, y0, y1, u, ssm_d, glu_w, glu_b, g_ssm, w_out)
    late = ("w_up", "w_down", "w_out", "glu_w")
    halves = [g_w_up.reshape(4, 2, D // 2, FF_BLK), g_w_down.reshape(4, 2, D_FF // 8, D),
              g_w_out.reshape(4, 2, D // 8, D), g_glu_w.reshape(4, 2, D_SSM // 8, D_SSM)]
    (d_pooled, g_pool_w, g_pool_scale, g_pool_g), from_sibling = _pool_bwd_local(
        dh1_b, u, w_out, pool_w_b, pool_scale, g_pool, comm=_comm_pair_swap(halves, half=True) if dist else None)
    du_pool = _pool_bwd_window(d_pooled)
    reduce2 = None
    if dist:
        chip_sums = [_add_half(h, r, c_arr, "sum_pair_" + k, BF16) for k, h, r in zip(late, halves, from_sibling)]
        reduce2 = _comm_chip_exchange(chip_sums, scatter=True)
    (du0, gb0r, gb0i, gc0r, gc0i, gv0), from_chips = _ssm_scan_bwd(dy, u, s0r, s0i, *ssm_args[0], 1, True, comm=reduce2)
    mine = [_sum4(r, "sum_chips_" + k) for k, r in zip(late, from_chips)]
    (du1, gb1r, gb1i, gc1r, gc1i, gv1), theirs = _ssm_scan_bwd(
        dy, u, s1r, s1i, *ssm_args[1], 3, False, comm=_comm_pair_swap(mine) if dist else None)
    gvec = lambda j: jnp.stack([gv0[j], gv1[j]]).reshape(2 * N_SSM_GROUPS, SSM_STATE)
    g_lnar, g_aim, g_ldt = _ssm_params_bwd(lnar, aim, ldt, gvec(0), gvec(1), gvec(2), gvec(3))
    grad_x, d_u_b, g_mix_g = _in_bwd(du_pool, du_direct, du0, du1, dh1, x, g_mix, w_in)
    g_w_in = _matmul_tn(xn, d_u_b, 512, D, "grad_w_in")

    small = {
        "norm_mix_g": g_mix_g, "pool_w": g_pool_w, "pool_scale": g_pool_scale,
        "ssm_log_neg_a_re": g_lnar, "ssm_a_im": g_aim, "ssm_log_dt": g_ldt,
        "ssm_b_re": jnp.swapaxes(jnp.stack([gb0r, gb1r]), 2, 3), "ssm_b_im": jnp.swapaxes(jnp.stack([gb0i, gb1i]), 2, 3),
        "ssm_c_re": jnp.stack([gc0r, gc1r]), "ssm_c_im": jnp.stack([gc0i, gc1i]),
        "ssm_d": g_ssm_d, "glu_b": g_glu_b, "out_norm_pool_g": g_pool_g, "out_norm_ssm_g": g_ssm_g,
        "norm_ffn_g": g_ffn_g, "conv_b": jnp.concatenate([gbv[0], gbg[0]]), "final_norm_g": g_final,
        "conv_w": jnp.transpose(gcw, (2, 0, 1, 3)).reshape(3, -1),
    }
    big = {"w_in": g_w_in}
    reduced = dict(zip(late, zip(mine, theirs)))
    if not dist:
        big.update({"w_up": g_w_up, "w_down": g_w_down, "w_out": g_w_out, "glu_w": g_glu_w})
    return loss, grad_x, small, big, reduced


def kernel(x, norm_mix_g, w_in, pool_w, pool_scale, ssm_log_neg_a_re, ssm_a_im, ssm_log_dt, ssm_b_re, ssm_b_im, ssm_c_re, ssm_c_im, ssm_d, glu_w, glu_b, out_norm_pool_g, out_norm_ssm_g, w_out, norm_ffn_g, w_up, conv_w, conv_b, w_down, final_norm_g, loss_target, m_norm_mix_g, m_w_in, m_pool_w, m_pool_scale, m_ssm_log_neg_a_re, m_ssm_a_im, m_ssm_log_dt, m_ssm_b_re, m_ssm_b_im, m_ssm_c_re, m_ssm_c_im, m_ssm_d, m_glu_w, m_glu_b, m_out_norm_pool_g, m_out_norm_ssm_g, m_w_out, m_norm_ffn_g, m_w_up, m_conv_w, m_conv_b, m_w_down, m_final_norm_g, v_norm_mix_g, v_w_in, v_pool_w, v_pool_scale, v_ssm_log_neg_a_re, v_ssm_a_im, v_ssm_log_dt, v_ssm_b_re, v_ssm_b_im, v_ssm_c_re, v_ssm_c_im, v_ssm_d, v_glu_w, v_glu_b, v_out_norm_pool_g, v_out_norm_ssm_g, v_w_out, v_norm_ffn_g, v_w_up, v_conv_w, v_conv_b, v_w_down, v_final_norm_g):
    args = locals()
    w = {k: args[k] for k in WEIGHTS}
    m = {k: args["m_" + k] for k in WEIGHTS}
    v = {k: args["v_" + k] for k in WEIGHTS}
    chip = 2 * lax.axis_index("x") + lax.axis_index("y")
    c_arr = lax.axis_index("c").astype(jnp.int32).reshape(1)

    shards = {k: w[k].astype(BF16) for k in BIG}
    shards["conv_w"] = conv_w
    w_in_full = _comm_call(_comm_gather_split([shards["w_in"]], []), "gather_w_in")[0]
    loss, grad_x, g_small, g_big, reduced = _local_step(
        x[0], loss_target[0], w, {"w_in": w_in_full.reshape(-1, w_in_full.shape[-1])}, shards, c_arr)

    exact = [k for k in SMALL if k not in WIDE]
    packs = [_pack([loss] + [g_small[k] for k in exact] + [g_small["conv_w"]], 512),
             _pack([g_small[k] for k in WIDE], 512)]
    halves = [g_big["w_in"].reshape(4, 2, g_big["w_in"].shape[0] // 8, -1)]
    halves += [pk.reshape(1, 2, pk.shape[0] // 2, LANES) for pk in packs]
    from_sibling = _comm_call(_comm_pair_swap(halves, half=True), "reduce_pair")
    names = ("w_in", "exact", "wide")
    sums = [_add_half(h, r, c_arr, "sum_pair_" + k, dt)
            for k, h, r, dt in zip(names, halves, from_sibling, (BF16, F32, BF16))]
    from_chips = _comm_call(_comm_join(_comm_chip_exchange(sums[:1], scatter=True),
                                       _comm_chip_exchange([s[0] for s in sums[1:]], scatter=False)), "reduce_chips")
    mine = [_sum4(r, "sum_chips_" + k) for k, r in zip(names, from_chips)]
    theirs = _comm_call(_comm_pair_swap(mine), "swap_halves")
    grads = {}
    exact_all = _join_rows(mine[1], theirs[1], c_arr, "join_exact")
    wide_all = _join_rows(mine[2], theirs[2], c_arr, "join_wide")
    shapes = [loss.shape] + [w[k].shape for k in exact] + [(3, 4 * FF_BLK)]
    grads.update(zip(["loss"] + exact + ["conv_w_full"], _unpack(exact_all, shapes)))
    grads.update(zip(WIDE, _unpack(wide_all, [w[k].shape for k in WIDE])))
    loss = grads.pop("loss")[0, 0]
    grads["conv_w"] = lax.dynamic_slice_in_dim(grads.pop("conv_w_full"), chip * FF_BLK, FF_BLK, axis=1)

    delta, new_m, new_v = {}, {}, {}
    reduced["w_in"] = (mine[0], theirs[0])
    for k, (own, other) in reduced.items():
        grads[k], delta[k], new_m[k], new_v[k] = _adamw_halves(w[k], own, other, m[k], v[k], c_arr, "adamw_" + k)
    padded = ["ssm_b_re", "ssm_b_im"]
    for keys, name in ((padded, "adamw_ssm_b"), ([k for k in SMALL + ["conv_w"] if k not in padded], "adamw_small")):
        outs = _adamw_many(*([d[k] for k in keys] for d in (w, grads, m, v)), name)
        for d, o in zip((delta, new_m, new_v), outs):
            d.update(zip(keys, o))

    return (loss, grad_x[None], *[grads[k] for k in WEIGHTS], *[delta[k] for k in WEIGHTS],
            *[new_m[k] for k in WEIGHTS], *[new_v[k] for k in WEIGHTS])
```

```python
import numpy as np
import jax
import jax.numpy as jnp
from jax import lax
from jax.experimental import pallas as pl
from jax.experimental.pallas import tpu as pltpu

F32 = jnp.float32
BF16 = jnp.bfloat16
MESH = pl.DeviceIdType.MESH

EPS = 1e-6
POOL_WINDOWS = (2, 4, 8, 16)
POOL_GROUP = 128
SSM_GROUP = 16
SSM_STATE = 64
N_SSM_GROUPS = 32
N_STATE = N_SSM_GROUPS * SSM_STATE
QUAD = 256
N_QUAD = N_STATE // QUAD
SLAB = 256
D_SSM = 512
D_POOL = 512
D_FF = 2816
FF_BLK = 1408
HALO = 8
HALO_B = 16
LANES = 128
ADAM_LR, ADAM_B1, ADAM_B2, ADAM_EPS, ADAM_WD, ADAM_STEP = 0.001, 0.9, 0.999, 1e-08, 0.01, 10
VMEM_LIMIT = 56 * 2 ** 20

TL = 512
TF = 256
TC = 512
SCAN_W = 512


def _cp(*sem):
    return pltpu.CompilerParams(dimension_semantics=sem, vmem_limit_bytes=VMEM_LIMIT)


def _dot_nn(a, b):
    return jnp.dot(a, b, preferred_element_type=F32)


def _dot_nt(a, b):
    return lax.dot_general(a, b, (((1,), (1,)), ((), ())), preferred_element_type=F32)


def _dot_tn(a, b):
    return lax.dot_general(a, b, (((0,), (0,)), ((), ())), preferred_element_type=F32)


def _rms_fwd(x, g):
    inv = lax.rsqrt(jnp.mean(x * x, axis=-1, keepdims=True) + EPS)
    xh = x * inv
    return xh * g, xh, inv


def _rms_bwd(dy, xh, inv, g):
    dg = jnp.sum(dy * xh, axis=0, keepdims=True)
    dxh = dy * g
    dx = inv * (dxh - xh * jnp.mean(dxh * xh, axis=-1, keepdims=True))
    return dx, dg


_GELU_C = 0.7978845608028654
_GELU_A = 0.044715


def _gelu(y):
    t = jnp.tanh(_GELU_C * (y + _GELU_A * (y * y * y)))
    return 0.5 * y * (1.0 + t), t


def _gelu_grad(y, t):
    return 0.5 * (1.0 + t) + 0.5 * y * (1.0 - t * t) * (_GELU_C * (1.0 + 3.0 * _GELU_A * y * y))


def _sigmoid(x):
    return 1.0 / (1.0 + jnp.exp(-x))


def _full(shape):
    n = len(shape)
    return pl.BlockSpec(shape, lambda *_: (0,) * n)


def _fill_ext(ext_ref, prev_ref, cur_ref, next_ref, i, n, rows):
    ext_ref[0:HALO, :] = jnp.where(i > 0, prev_ref[...], 0.0).astype(ext_ref.dtype)
    ext_ref[HALO:HALO + rows, :] = cur_ref[...]
    ext_ref[HALO + rows:2 * HALO + rows, :] = jnp.where(i < n - 1, next_ref[...], 0.0).astype(ext_ref.dtype)


def _in_proj(x, g, w):
    L, D = x.shape
    E = w.shape[1]

    def body(x_ref, g_ref, w_ref, u_ref, xn_ref):
        y, _, _ = _rms_fwd(x_ref[...], g_ref[...])
        yb = y.astype(BF16)
        xn_ref[...] = yb
        u_ref[...] = _dot_nn(yb, w_ref[...])

    return pl.pallas_call(
        body, name="in_proj", grid=(L // TL,),
        in_specs=[pl.BlockSpec((TL, D), lambda i: (i, 0)), _full((1, D)), _full(w.shape)],
        out_specs=[pl.BlockSpec((TL, E), lambda i: (i, 0)), pl.BlockSpec((TL, D), lambda i: (i, 0))],
        out_shape=[jax.ShapeDtypeStruct((L, E), F32), jax.ShapeDtypeStruct((L, D), BF16)],
        compiler_params=_cp("parallel"))(x, g, w)


def _halo_specs_1d(rows, width, L, col):
    rb = rows // HALO
    last = L // HALO - 1
    return [pl.BlockSpec((HALO, width), lambda i: (jnp.maximum(i * rb - 1, 0), col)),
            pl.BlockSpec((rows, width), lambda i: (i, col)),
            pl.BlockSpec((HALO, width), lambda i: (jnp.minimum((i + 1) * rb, last), col))]


def _pooled_from_ext(ext_ref, t0, rows, L):
    t = t0 + lax.broadcasted_iota(jnp.int32, (rows, 1), 0)
    outs = []
    for gi, w in enumerate(POOL_WINDOWS):
        half = w // 2
        cs = slice(gi * POOL_GROUP, (gi + 1) * POOL_GROUP)
        acc = ext_ref[pl.ds(HALO - half, rows), cs]
        for s in range(-half + 1, half):
            acc = acc + ext_ref[pl.ds(HALO + s, rows), cs]
        cnt = (jnp.minimum(t + half, L) - jnp.maximum(t - half, 0)).astype(F32)
        outs.append(acc / cnt - ext_ref[pl.ds(HALO, rows), cs])
    return outs


def _pool_fwd(u, pool_w_b, pool_scale, g_pool):
    L = u.shape[0]
    n = L // TL

    def body(prev_ref, cur_ref, next_ref, pw_ref, ps_ref, g_ref, out_ref, ext_ref):
        i = pl.program_id(0)
        _fill_ext(ext_ref, prev_ref, cur_ref, next_ref, i, n, TL)
        pooled = _pooled_from_ext(ext_ref, i * TL, TL, L)
        ypre = jnp.concatenate([_dot_nn(pooled[gi].astype(BF16), pw_ref[gi]) for gi in range(4)], axis=-1)
        yn, _, _ = _rms_fwd(ypre * ps_ref[...], g_ref[...])
        out_ref[...] = yn.astype(BF16)

    return pl.pallas_call(
        body, name="pool_fwd", grid=(n,),
        in_specs=_halo_specs_1d(TL, D_POOL, L, 0) + [_full(pool_w_b.shape), _full((1, D_POOL)), _full((1, D_POOL))],
        out_specs=pl.BlockSpec((TL, D_POOL), lambda i: (i, 0)),
        out_shape=jax.ShapeDtypeStruct((L, D_POOL), BF16),
        scratch_shapes=[pltpu.VMEM((TL + 2 * HALO, D_POOL), F32)],
        compiler_params=_cp("parallel"))(u, u, u, pool_w_b, pool_scale, g_pool)


def _pool_bwd_local(dh1, u, w_out_b, pool_w_b, pool_scale, g_pool, comm=None):
    L = u.shape[0]
    n = L // TL
    D = dh1.shape[1]

    def body(dh_ref, prev_ref, cur_ref, next_ref, wo_ref, pw_ref, ps_ref, g_ref,
             dp_ref, gpw_ref, gps_ref, gg_ref, ext_ref):
        i = pl.program_id(0)

        @pl.when(i == 0)
        def _():
            gpw_ref[...] = jnp.zeros_like(gpw_ref)
            gps_ref[...] = jnp.zeros_like(gps_ref)
            gg_ref[...] = jnp.zeros_like(gg_ref)

        _fill_ext(ext_ref, prev_ref, cur_ref, next_ref, i, n, TL)
        pooled = [p.astype(BF16) for p in _pooled_from_ext(ext_ref, i * TL, TL, L)]
        ypre = jnp.concatenate([_dot_nn(pooled[gi], pw_ref[gi]) for gi in range(4)], axis=-1)
        ps = ps_ref[...]
        g = g_ref[...]
        _, xh, inv = _rms_fwd(ypre * ps, g)
        d_yn = _dot_nt(dh_ref[...].astype(BF16), wo_ref[...])
        d_y, dg = _rms_bwd(d_yn, xh, inv, g)
        gg_ref[...] += dg
        gps_ref[...] += jnp.sum(d_y * ypre, axis=0, keepdims=True)
        d_ypre = (d_y * ps).astype(BF16)
        for gi in range(4):
            cs = slice(gi * POOL_GROUP, (gi + 1) * POOL_GROUP)
            dp_ref[:, cs] = _dot_nt(d_ypre[:, cs], pw_ref[gi])
            gpw_ref[gi] += _dot_tn(pooled[gi], d_ypre[:, cs])

    return _hosted_call(
        body, comm, name="pool_bwd_local", grid=(n,),
        in_specs=[pl.BlockSpec((TL, D), lambda i: (i, 0))] + _halo_specs_1d(TL, D_POOL, L, 0)
        + [pl.BlockSpec((D_POOL, D), lambda i: (0, 0)), _full(pool_w_b.shape), _full((1, D_POOL)), _full((1, D_POOL))],
        out_specs=[pl.BlockSpec((TL, D_POOL), lambda i: (i, 0)), _full(pool_w_b.shape),
                   _full((1, D_POOL)), _full((1, D_POOL))],
        out_shape=[jax.ShapeDtypeStruct((L, D_POOL), F32), jax.ShapeDtypeStruct(pool_w_b.shape, F32),
                   jax.ShapeDtypeStruct((1, D_POOL), F32), jax.ShapeDtypeStruct((1, D_POOL), F32)],
        scratch_shapes=[pltpu.VMEM((TL + 2 * HALO, D_POOL), F32)],
        args=(dh1, u, u, u, w_out_b, pool_w_b, pool_scale, g_pool))


def _pool_bwd_window(d_pooled):
    L = d_pooled.shape[0]
    n = L // TL
    R = TL + 2 * HALO

    def body(prev_ref, cur_ref, next_ref, out_ref, ext_ref, q_ref):
        i = pl.program_id(0)
        _fill_ext(ext_ref, prev_ref, cur_ref, next_ref, i, n, TL)
        tr = i * TL - HALO + lax.broadcasted_iota(jnp.int32, (R, 1), 0)
        for gi, w in enumerate(POOL_WINDOWS):
            half = w // 2
            cs = slice(gi * POOL_GROUP, (gi + 1) * POOL_GROUP)
            cnt = jnp.maximum(jnp.minimum(tr + half, L) - jnp.maximum(tr - half, 0), 1).astype(F32)
            q_ref[:, cs] = ext_ref[:, cs] / cnt
        for gi, w in enumerate(POOL_WINDOWS):
            half = w // 2
            cs = slice(gi * POOL_GROUP, (gi + 1) * POOL_GROUP)
            acc = q_ref[pl.ds(HALO - half + 1, TL), cs]
            for s in range(-half + 2, half + 1):
                acc = acc + q_ref[pl.ds(HALO + s, TL), cs]
            out_ref[:, cs] = acc - ext_ref[pl.ds(HALO, TL), cs]

    return pl.pallas_call(
        body, name="pool_bwd_window", grid=(n,),
        in_specs=_halo_specs_1d(TL, D_POOL, L, 0),
        out_specs=pl.BlockSpec((TL, D_POOL), lambda i: (i, 0)),
        out_shape=jax.ShapeDtypeStruct((L, D_POOL), F32),
        scratch_shapes=[pltpu.VMEM((R, D_POOL), F32), pltpu.VMEM((R, D_POOL), F32)],
        compiler_params=_cp("parallel"))(d_pooled, d_pooled, d_pooled)


def _ssm_param_fn(lnar, aim, ldt):
    dt = jnp.exp(ldt)
    a_re = -jnp.exp(lnar)
    mag = jnp.exp(a_re * dt)
    ang = aim * dt
    lr, li = mag * jnp.cos(ang), mag * jnp.sin(ang)
    den = a_re * a_re + aim * aim
    fr = ((lr - 1.0) * a_re + li * aim) / den
    fi = (li * a_re - (lr - 1.0) * aim) / den
    return lr, li, fr, fi


def _ssm_params(lnar, aim, ldt):
    def body(a_ref, b_ref, c_ref, lr_ref, li_ref, fr_ref, fi_ref):
        lr, li, fr, fi = _ssm_param_fn(a_ref[...], b_ref[...], c_ref[...])
        lr_ref[...] = lr
        li_ref[...] = li
        fr_ref[...] = fr
        fi_ref[...] = fi

    sh = jax.ShapeDtypeStruct(lnar.shape, F32)
    return pl.pallas_call(body, name="ssm_params", out_shape=[sh] * 4)(lnar, aim, ldt)


def _ssm_params_bwd(lnar, aim, ldt, glr, gli, gfr, gfi):
    def body(a_ref, b_ref, c_ref, g0, g1, g2, g3, da_ref, db_ref, dc_ref):
        _, vjp = jax.vjp(_ssm_param_fn, a_ref[...], b_ref[...], c_ref[...])
        da, db, dc = vjp((g0[...], g1[...], g2[...], g3[...]))
        da_ref[...] = da
        db_ref[...] = db
        dc_ref[...] = jnp.sum(dc, axis=1, keepdims=True)

    return pl.pallas_call(
        body, name="ssm_params_bwd",
        out_shape=[jax.ShapeDtypeStruct(lnar.shape, F32), jax.ShapeDtypeStruct(aim.shape, F32),
                   jax.ShapeDtypeStruct((ldt.shape[0], 1), F32)])(lnar, aim, ldt, glr, gli, gfr, gfi)


def _scan_tables(lam4):
    def build(lr, li, reverse, out_ref, k):
        row = lax.broadcasted_iota(jnp.int32, (8, N_STATE), 0)
        lrb = jnp.broadcast_to(lr, (8, N_STATE))
        lib = jnp.broadcast_to(li, (8, N_STATE))
        pr, pi = lrb, lib
        for s, sh in enumerate((1, 2, 4)):
            mask = (row < 8 - sh) if reverse else (row >= sh)
            out_ref[k, 2 * s] = jnp.where(mask, pr, 0.0)
            out_ref[k, 2 * s + 1] = jnp.where(mask, pi, 0.0)
            pr, pi = pr * pr - pi * pi, 2.0 * pr * pi
        pr, pi = lrb, lib
        p8r = jnp.zeros((8, N_STATE), F32)
        p8i = jnp.zeros((8, N_STATE), F32)
        for j in range(8):
            r = 7 - j if reverse else j
            p8r = jnp.where(row == r, pr, p8r)
            p8i = jnp.where(row == r, pi, p8i)
            pr, pi = pr * lrb - pi * lib, pr * lib + pi * lrb
        out_ref[k, 6] = p8r
        out_ref[k, 7] = p8i

    def body(lam_ref, out_ref):
        l0r, l0i, l1r, l1i = (lam_ref[j:j + 1, :] for j in range(4))
        build(l0r, l0i, False, out_ref, 0)
        build(l0r, -l0i, True, out_ref, 1)
        build(l1r, l1i, True, out_ref, 2)
        build(l1r, -l1i, False, out_ref, 3)

    return pl.pallas_call(body, name="scan_tables",
                          out_shape=jax.ShapeDtypeStruct((4, 8, 8, N_STATE), F32))(lam4)


def _b_block(g):
    q, gl = divmod(g, 4)
    r0, c0 = gl * SSM_STATE, (q % 4) * 4 * SSM_GROUP + gl * SSM_GROUP
    return q, slice(r0, r0 + SSM_STATE), slice(c0, c0 + SSM_GROUP)


def _c_block(g):
    q, rows, cols = _b_block(g)
    return q, cols, rows


def _ssm_expand(b_re, b_im, c_re, c_im):
    def body(bre_ref, bim_ref, cre_ref, cim_ref, *rest):
        outs, tmp = rest[:8], rest[8]
        for d in range(2):
            for j, (src, where) in enumerate(((bre_ref, _b_block), (bim_ref, _b_block),
                                              (cre_ref, _c_block), (cim_ref, _c_block))):
                tmp[...] = jnp.zeros_like(tmp)
                for g in range(N_SSM_GROUPS):
                    q, rows, cols = where(g)
                    tmp[q, rows, cols] = src[d, g]
                outs[4 * d + j][...] = tmp[...].astype(BF16)

    dense = jax.ShapeDtypeStruct((N_QUAD, QUAD, SLAB), BF16)
    return pl.pallas_call(body, name="ssm_expand", out_shape=[dense] * 8,
                          scratch_shapes=[pltpu.VMEM((N_QUAD, QUAD, SLAB), F32)],
                          compiler_params=pltpu.CompilerParams(vmem_limit_bytes=VMEM_LIMIT))(b_re, b_im, c_re, c_im)


def _scan_rows(src_re, src_im, dst_re, dst_im, tab_ref, k, carry_re, carry_im, rows, reverse, s_refs=None):
    ng = rows // 8
    edge = 0 if reverse else 7
    row_id = lax.broadcasted_iota(jnp.int32, (8, SCAN_W), 0)
    sums = []
    for lt in range(N_STATE // SCAN_W):
        sl = slice(lt * SCAN_W, (lt + 1) * SCAN_W)

        def step(r, c, sl=sl):
            tabs = [tab_ref[k, j, :, sl] for j in range(8)]
            cr, ci = c[0], c[1]
            row = pl.multiple_of((ng - 1 - r) * 8 if reverse else r * 8, 8)
            xr = src_re[pl.ds(row, 8), sl]
            xi = src_im[pl.ds(row, 8), sl]
            for s, sh in enumerate((1, 2, 4)):
                amt = 8 - sh if reverse else sh
                rr = pltpu.roll(xr, amt, 0)
                ri = pltpu.roll(xi, amt, 0)
                mr, mi = tabs[2 * s], tabs[2 * s + 1]
                xr, xi = xr + mr * rr - mi * ri, xi + mr * ri + mi * rr
            xr, xi = xr + tabs[6] * cr - tabs[7] * ci, xi + tabs[6] * ci + tabs[7] * cr
            dst_re[pl.ds(row, 8), sl] = xr
            dst_im[pl.ds(row, 8), sl] = xi
            ncr = jnp.broadcast_to(xr[edge:edge + 1, :], (8, SCAN_W))
            nci = jnp.broadcast_to(xi[edge:edge + 1, :], (8, SCAN_W))
            if s_refs is None:
                return ncr, nci
            amt = 7 if reverse else 1
            far = 7 if reverse else 0
            nr = jnp.where(row_id == far, cr, pltpu.roll(xr, amt, 0))
            ni = jnp.where(row_id == far, ci, pltpu.roll(xi, amt, 0))
            sr = s_refs[0][pl.ds(row, 8), sl]
            si = s_refs[1][pl.ds(row, 8), sl]
            return ncr, nci, c[2] + nr * sr + ni * si, c[3] + ni * sr - nr * si

        init = (carry_re[:, sl], carry_im[:, sl])
        if s_refs is not None:
            init = init + (jnp.zeros((8, SCAN_W), F32), jnp.zeros((8, SCAN_W), F32))
        out = lax.fori_loop(0, ng, step, init)
        carry_re[:, sl] = out[0]
        carry_im[:, sl] = out[1]
        if s_refs is not None:
            sums.append((jnp.sum(out[2], axis=0, keepdims=True), jnp.sum(out[3], axis=0, keepdims=True)))
    return sums


def _ssm_scan_fwd(u, b_re, b_im, c_re, c_im, f2, tables, k, reverse, comm=None):
    L = u.shape[0]
    nc = L // TC
    chunk = (lambda i: nc - 1 - i) if reverse else (lambda i: i)

    def body(u_ref, bre_ref, bim_ref, cre_ref, cim_ref, f_ref, tab_ref,
             y_ref, sre_ref, sim_ref, in_re, in_im, carry_re, carry_im):
        @pl.when(pl.program_id(0) == 0)
        def _():
            carry_re[...] = jnp.zeros_like(carry_re)
            carry_im[...] = jnp.zeros_like(carry_im)

        ub = u_ref[...].astype(BF16)
        for q in range(N_QUAD):
            qs = slice(q * QUAD, (q + 1) * QUAD)
            us = ub[:, (q // 4) * SLAB:(q // 4 + 1) * SLAB]
            bur = _dot_nt(us, bre_ref[q])
            bui = _dot_nt(us, bim_ref[q])
            fr = f_ref[0:1, qs]
            fi = f_ref[1:2, qs]
            in_re[:, qs] = fr * bur - fi * bui
            in_im[:, qs] = fr * bui + fi * bur
        _scan_rows(in_re, in_im, sre_ref, sim_ref, tab_ref, k, carry_re, carry_im, TC, reverse)
        for j in range(D_SSM // SLAB):
            acc = jnp.zeros((TC, SLAB), F32)
            for q in range(4 * j, 4 * j + 4):
                qs = slice(q * QUAD, (q + 1) * QUAD)
                acc = acc + _dot_nt(sre_ref[:, qs].astype(BF16), cre_ref[q])
                acc = acc - _dot_nt(sim_ref[:, qs].astype(BF16), cim_ref[q])
            y_ref[:, j * SLAB:(j + 1) * SLAB] = acc

    return _hosted_call(
        body, comm, name="ssm_scan_rev" if reverse else "ssm_scan_fwd", grid=(nc,),
        in_specs=[pl.BlockSpec((TC, D_SSM), lambda i: (chunk(i), 1))]
        + [_full(b_re.shape)] * 4 + [_full(f2.shape), _full(tables.shape)],
        out_specs=[pl.BlockSpec((TC, D_SSM), lambda i: (chunk(i), 0)),
                   pl.BlockSpec((TC, N_STATE), lambda i: (chunk(i), 0)),
                   pl.BlockSpec((TC, N_STATE), lambda i: (chunk(i), 0))],
        out_shape=[jax.ShapeDtypeStruct((L, D_SSM), F32), jax.ShapeDtypeStruct((L, N_STATE), F32),
                   jax.ShapeDtypeStruct((L, N_STATE), F32)],
        scratch_shapes=[pltpu.VMEM((TC, N_STATE), F32), pltpu.VMEM((TC, N_STATE), F32),
                        pltpu.VMEM((8, N_STATE), F32), pltpu.VMEM((8, N_STATE), F32)],
        args=(u, b_re, b_im, c_re, c_im, f2, tables))


def _quad_channels(q):
    c0 = (q // 4) * SLAB + (q % 4) * 4 * SSM_GROUP
    return slice(c0, c0 + 4 * SSM_GROUP)


def _ssm_scan_bwd(dy, u, s_re, s_im, b_re, b_im, c_re, c_im, f2, tables, k, reverse, comm=None):
    L = u.shape[0]
    nc = L // TC
    chunk = (lambda i: nc - 1 - i) if reverse else (lambda i: i)

    def body(dy_ref, u_ref, sre_ref, sim_ref, bre_ref, bim_ref, cre_ref, cim_ref, f_ref, tab_ref,
             du_ref, ob_re, ob_im, oc_re, oc_im, gv_ref,
             a_re, a_im, carry_re, carry_im, gbr_ref, gbi_ref, gcr_ref, gci_ref):
        @pl.when(pl.program_id(0) == 0)
        def _():
            carry_re[...] = jnp.zeros_like(carry_re)
            carry_im[...] = jnp.zeros_like(carry_im)
            for r in (gbr_ref, gbi_ref, gcr_ref, gci_ref, gv_ref):
                r[...] = jnp.zeros_like(r)

        dyb = dy_ref[...].astype(BF16)
        ub = u_ref[...].astype(BF16)
        for q in range(N_QUAD):
            qs = slice(q * QUAD, (q + 1) * QUAD)
            ds = dyb[:, (q // 4) * SLAB:(q // 4 + 1) * SLAB]
            a_re[:, qs] = _dot_nn(ds, cre_ref[q])
            a_im[:, qs] = -_dot_nn(ds, cim_ref[q])
            dq = dyb[:, _quad_channels(q)]
            gcr_ref[q] += _dot_tn(dq, sre_ref[:, qs].astype(BF16))
            gci_ref[q] -= _dot_tn(dq, sim_ref[:, qs].astype(BF16))
        sums = _scan_rows(a_re, a_im, a_re, a_im, tab_ref, k, carry_re, carry_im, TC, reverse,
                          s_refs=(sre_ref, sim_ref))
        for lt, (glr, gli) in enumerate(sums):
            sl = slice(lt * SCAN_W, (lt + 1) * SCAN_W)
            gv_ref[0:1, sl] += glr
            gv_ref[1:2, sl] += gli
        for j in range(D_SSM // SLAB):
            us = ub[:, j * SLAB:(j + 1) * SLAB]
            acc = jnp.zeros((TC, SLAB), F32)
            for q in range(4 * j, 4 * j + 4):
                qs = slice(q * QUAD, (q + 1) * QUAD)
                ar = a_re[:, qs]
                ai = a_im[:, qs]
                bur = _dot_nt(us, bre_ref[q])
                bui = _dot_nt(us, bim_ref[q])
                gv_ref[2:3, qs] += jnp.sum(ar * bur + ai * bui, axis=0, keepdims=True)
                gv_ref[3:4, qs] += jnp.sum(ai * bur - ar * bui, axis=0, keepdims=True)
                fr = f_ref[0:1, qs]
                fi = f_ref[1:2, qs]
                dbr = (fr * ar + fi * ai).astype(BF16)
                dbi = (fr * ai - fi * ar).astype(BF16)
                uq = ub[:, _quad_channels(q)]
                gbr_ref[q] += _dot_tn(uq, dbr)
                gbi_ref[q] += _dot_tn(uq, dbi)
                acc = acc + _dot_nn(dbr, bre_ref[q]) + _dot_nn(dbi, bim_ref[q])
            du_ref[:, j * SLAB:(j + 1) * SLAB] = acc

        @pl.when(pl.program_id(0) == nc - 1)
        def _():
            for g in range(N_SSM_GROUPS):
                q, gl = divmod(g, 4)
                rows = slice(gl * SSM_GROUP, (gl + 1) * SSM_GROUP)
                cols = slice(gl * SSM_STATE, (gl + 1) * SSM_STATE)
                for out, acc_ref in ((ob_re, gbr_ref), (ob_im, gbi_ref), (oc_re, gcr_ref), (oc_im, gci_ref)):
                    out[g] = acc_ref[q, rows, cols]

    gshape = jax.ShapeDtypeStruct((N_SSM_GROUPS, SSM_GROUP, SSM_STATE), F32)
    compact = pltpu.VMEM((N_QUAD, 4 * SSM_GROUP, QUAD), F32)
    return _hosted_call(
        body, comm, name="ssm_bwd_rev" if reverse else "ssm_bwd_fwd", grid=(nc,),
        in_specs=[pl.BlockSpec((TC, D_SSM), lambda i: (chunk(i), 0)),
                  pl.BlockSpec((TC, D_SSM), lambda i: (chunk(i), 1)),
                  pl.BlockSpec((TC, N_STATE), lambda i: (chunk(i), 0)),
                  pl.BlockSpec((TC, N_STATE), lambda i: (chunk(i), 0))]
        + [_full(b_re.shape)] * 4 + [_full(f2.shape), _full(tables.shape)],
        out_specs=[pl.BlockSpec((TC, D_SSM), lambda i: (chunk(i), 0))] + [_full(gshape.shape)] * 4
        + [_full((4, N_STATE))],
        out_shape=[jax.ShapeDtypeStruct((L, D_SSM), F32), gshape, gshape, gshape, gshape,
                   jax.ShapeDtypeStruct((4, N_STATE), F32)],
        scratch_shapes=[pltpu.VMEM((TC, N_STATE), F32), pltpu.VMEM((TC, N_STATE), F32),
                        pltpu.VMEM((8, N_STATE), F32), pltpu.VMEM((8, N_STATE), F32),
                        compact, compact, compact, compact],
        args=(dy, u, s_re, s_im, b_re, b_im, c_re, c_im, f2, tables))


def _ssm_post(yf, yb, u, d, glu_w, glu_b):
    y = yf + yb + d * u
    z, t = _gelu(y)
    zb = z.astype(BF16)
    gate = _sigmoid(_dot_nn(zb, glu_w) + glu_b)
    return y, z, t, zb, gate


def _mix_out(yn_pool, yf, yb, u, x, ssm_d, glu_w_b, glu_b, g_ssm, w_out_b, g_ffn):
    L, D = x.shape

    def body(ynp_ref, yf_ref, yb_ref, u_ref, x_ref, d_ref, gw_ref, gb_ref, gs_ref, wo_ref, gf_ref,
             h1_ref, hn_ref, ycat_ref):
        _, z, _, _, gate = _ssm_post(yf_ref[...], yb_ref[...], u_ref[...], d_ref[...], gw_ref[...], gb_ref[...])
        yns, _, _ = _rms_fwd(z * gate, gs_ref[...])
        ynsb = yns.astype(BF16)
        ynp = ynp_ref[...]
        ycat_ref[:, 0:D_POOL] = ynp
        ycat_ref[:, D_POOL:D] = ynsb
        h1 = x_ref[...] + _dot_nn(ynp, wo_ref[0:D_POOL, :]) + _dot_nn(ynsb, wo_ref[D_POOL:D, :])
        h1_ref[...] = h1
        hn, _, _ = _rms_fwd(h1, gf_ref[...])
        hn_ref[...] = hn.astype(BF16)

    half = lambda c: pl.BlockSpec((TL, D_SSM), lambda i: (i, c))
    row = pl.BlockSpec((TL, D), lambda i: (i, 0))
    return pl.pallas_call(
        body, name="mix_out", grid=(L // TL,),
        in_specs=[half(0), half(0), half(0), half(1), row, _full((1, D_SSM)), _full(glu_w_b.shape),
                  _full((1, D_SSM)), _full((1, D_SSM)), _full(w_out_b.shape), _full((1, D))],
        out_specs=[row, row, row],
        out_shape=[jax.ShapeDtypeStruct((L, D), F32), jax.ShapeDtypeStruct((L, D), BF16),
                   jax.ShapeDtypeStruct((L, D), BF16)],
        compiler_params=_cp("parallel"))(yn_pool, yf, yb, u, x, ssm_d, glu_w_b, glu_b, g_ssm, w_out_b, g_ffn)


def _ssm_bwd_local(dh1, yf, yb, u, ssm_d, glu_w_b, glu_b, g_ssm, w_out_b):
    L, D = dh1.shape

    def body(dh_ref, yf_ref, yb_ref, u_ref, d_ref, gw_ref, gb_ref, gs_ref, wo_ref,
             dy_ref, du_ref, ggw_ref, ggb_ref, gd_ref, ggs_ref):
        @pl.when(pl.program_id(0) == 0)
        def _():
            for r in (ggw_ref, ggb_ref, gd_ref, ggs_ref):
                r[...] = jnp.zeros_like(r)

        u = u_ref[...]
        d = d_ref[...]
        y, z, t, zb, gate = _ssm_post(yf_ref[...], yb_ref[...], u, d, gw_ref[...], gb_ref[...])
        gs = gs_ref[...]
        _, xh, inv = _rms_fwd(z * gate, gs)
        d_yn = _dot_nt(dh_ref[...].astype(BF16), wo_ref[...])
        d_o, dgs = _rms_bwd(d_yn, xh, inv, gs)
        ggs_ref[...] += dgs
        d_zg = d_o * z * gate * (1.0 - gate)
        d_zgb = d_zg.astype(BF16)
        ggb_ref[...] += jnp.sum(d_zg, axis=0, keepdims=True)
        ggw_ref[...] += _dot_tn(zb, d_zgb)
        d_z = d_o * gate + _dot_nt(d_zgb, gw_ref[...])
        d_y = d_z * _gelu_grad(y, t)
        gd_ref[...] += jnp.sum(d_y * u, axis=0, keepdims=True)
        dy_ref[...] = d_y
        du_ref[...] = d_y * d

    half = lambda c: pl.BlockSpec((TL, D_SSM), lambda i: (i, c))
    vec = _full((1, D_SSM))
    return pl.pallas_call(
        body, name="ssm_bwd_local", grid=(L // TL,),
        in_specs=[pl.BlockSpec((TL, D), lambda i: (i, 0)), half(0), half(0), half(1), vec, _full(glu_w_b.shape),
                  vec, vec, pl.BlockSpec((D_SSM, D), lambda i: (1, 0))],
        out_specs=[half(0), half(0), _full(glu_w_b.shape), vec, vec, vec],
        out_shape=[jax.ShapeDtypeStruct((L, D_SSM), F32), jax.ShapeDtypeStruct((L, D_SSM), F32),
                   jax.ShapeDtypeStruct(glu_w_b.shape, F32)] + [jax.ShapeDtypeStruct((1, D_SSM), F32)] * 3,
        compiler_params=_cp("arbitrary"))(dh1, yf, yb, u, ssm_d, glu_w_b, glu_b, g_ssm, w_out_b)


def _in_bwd(du_pool, du_a, du_b, du_c, dh1, x, g, w_in_b):
    L, D = x.shape

    def body(p_ref, a_ref, b_ref, c_ref, dh_ref, x_ref, g_ref, w_ref, dx_ref, dub_ref, gg_ref):
        @pl.when(pl.program_id(0) == 0)
        def _():
            gg_ref[...] = jnp.zeros_like(gg_ref)

        dub_ref[:, 0:D_POOL] = p_ref[...].astype(BF16)
        dub_ref[:, D_POOL:D] = (a_ref[...] + b_ref[...] + c_ref[...]).astype(BF16)
        d_xn = _dot_nt(dub_ref[...], w_ref[...])
        gv = g_ref[...]
        _, xh, inv = _rms_fwd(x_ref[...], gv)
        dx, dg = _rms_bwd(d_xn, xh, inv, gv)
        gg_ref[...] += dg
        dx_ref[...] = dh_ref[...] + dx

    half = pl.BlockSpec((TL, D_SSM), lambda i: (i, 0))
    row = pl.BlockSpec((TL, D), lambda i: (i, 0))
    return pl.pallas_call(
        body, name="in_bwd", grid=(L // TL,),
        in_specs=[half, half, half, half, row, row, _full((1, D)), _full(w_in_b.shape)],
        out_specs=[row, row, _full((1, D))],
        out_shape=[jax.ShapeDtypeStruct((L, D), F32), jax.ShapeDtypeStruct((L, D), BF16),
                   jax.ShapeDtypeStruct((1, D), F32)],
        compiler_params=_cp("arbitrary"))(du_pool, du_a, du_b, du_c, dh1, x, g, w_in_b)


def _ffn_up(hn, w_up4):
    L, D = hn.shape

    def body(h_ref, w_ref, o_ref):
        o_ref[...] = _dot_nn(h_ref[...], w_ref[...]).astype(BF16)

    return pl.pallas_call(
        body, name="ffn_up", grid=(4, L // TL),
        in_specs=[pl.BlockSpec((TL, D), lambda j, i: (i, 0)), pl.BlockSpec((None, D, FF_BLK), lambda j, i: (j, 0, 0))],
        out_specs=pl.BlockSpec((TL, FF_BLK), lambda j, i: (i, j)),
        out_shape=jax.ShapeDtypeStruct((L, 4 * FF_BLK), BF16),
        compiler_params=_cp("parallel", "parallel"))(hn, w_up4)


def _halo_specs_2d(rows, width, L, col, order):
    rb = rows // HALO_B
    last = L // HALO_B - 1
    if order == "ik":
        wrap = lambda f: (lambda i, k: f(i, k))
    else:
        wrap = lambda f: (lambda k, i: f(i, k))
    return [pl.BlockSpec((HALO_B, width), wrap(lambda i, k: (jnp.maximum(i * rb - 1, 0), col(k)))),
            pl.BlockSpec((rows, width), wrap(lambda i, k: (i, col(k)))),
            pl.BlockSpec((HALO_B, width), wrap(lambda i, k: (jnp.minimum((i + 1) * rb, last), col(k))))]


def _shift_mats(rows):
    r = lax.broadcasted_iota(jnp.int32, (rows, rows), 0)
    c = lax.broadcasted_iota(jnp.int32, (rows, rows), 1)
    return (c == r - 1).astype(BF16), (c == r + 1).astype(BF16)


def _neighbours(x, prev_ref, next_ref, cs, i, n, mats):
    rows = x.shape[0]
    row = lax.broadcasted_iota(jnp.int32, (rows, 1), 0)
    before = jnp.where(i > 0, prev_ref[:, cs].astype(F32)[HALO_B - 1:HALO_B, :], 0.0)
    after = jnp.where(i < n - 1, next_ref[:, cs].astype(F32)[0:1, :], 0.0)
    if mats is None:
        xf = x.astype(F32)
        down, up = pltpu.roll(xf, 1, 0), pltpu.roll(xf, rows - 1, 0)
    else:
        down, up = _dot_nn(mats[0], x), _dot_nn(mats[1], x)
    return jnp.where(row == 0, before, down), jnp.where(row == rows - 1, after, up)


def _conv3(x, before, after, w, b):
    return before * w[0:1, :] + x.astype(F32) * w[1:2, :] + after * w[2:3, :] + b


def _col_chunks(width, size=256):
    return [slice(c, min(c + size, width)) for c in range(0, width, size)]


def _ffn_down_loss(up, conv_w, conv_b, w_down_b, h1, target, g_final):
    L, D = h1.shape
    n = L // TF
    nk = D_FF // FF_BLK

    def body(vp, vc, vn, gp, gc, gn, wv_ref, wg_ref, bv_ref, bg_ref, wd_ref, h1_ref, t_ref, gf_ref,
             a_ref, cv_ref, cg_ref, dh2_ref, dh2b_ref, loss_ref, gg_ref, acc_ref):
        i = pl.program_id(0)
        k = pl.program_id(1)

        @pl.when((i == 0) & (k == 0))
        def _():
            loss_ref[...] = jnp.zeros_like(loss_ref)
            gg_ref[...] = jnp.zeros_like(gg_ref)

        @pl.when(k == 0)
        def _():
            acc_ref[...] = jnp.zeros_like(acc_ref)

        mats = _shift_mats(TF)
        for cs in _col_chunks(FF_BLK):
            xv, xg = vc[:, cs], gc[:, cs]
            val = _conv3(xv, *_neighbours(xv, vp, vn, cs, i, n, mats), wv_ref[:, cs], bv_ref[:, cs])
            gate = _conv3(xg, *_neighbours(xg, gp, gn, cs, i, n, mats), wg_ref[:, cs], bg_ref[:, cs])
            a_ref[:, cs] = (val * (gate * _sigmoid(gate))).astype(BF16)
            cv_ref[:, cs] = val.astype(BF16)
            cg_ref[:, cs] = gate.astype(BF16)
        acc_ref[...] += _dot_nn(a_ref[...], wd_ref[pl.ds(pl.multiple_of(k * FF_BLK, LANES), FF_BLK), :])

        @pl.when(k == nk - 1)
        def _():
            gf = gf_ref[...]
            y, xh, inv = _rms_fwd(h1_ref[...] + acc_ref[...], gf)
            diff = y - t_ref[...]
            part = 0.5 * jnp.sum(jnp.mean(diff * diff, axis=-1, keepdims=True), axis=0, keepdims=True)
            loss_ref[...] += jnp.broadcast_to(part, loss_ref.shape)
            dx, dg = _rms_bwd(diff * (1.0 / D), xh, inv, gf)
            gg_ref[...] += dg
            dh2_ref[...] = dx
            dh2b_ref[...] = dx.astype(BF16)

    row = pl.BlockSpec((TF, D), lambda i, k: (i, 0))
    cw = lambda off: pl.BlockSpec((3, FF_BLK), lambda i, k: (0, k + off))
    cb = lambda off: pl.BlockSpec((1, FF_BLK), lambda i, k: (0, k + off))
    return pl.pallas_call(
        body, name="ffn_down_loss", grid=(n, nk),
        in_specs=_halo_specs_2d(TF, FF_BLK, L, lambda k: k, "ik") + _halo_specs_2d(TF, FF_BLK, L, lambda k: k + nk, "ik")
        + [cw(0), cw(nk), cb(0), cb(nk), _full(w_down_b.shape), row, row, _full((1, D))],
        out_specs=[pl.BlockSpec((TF, FF_BLK), lambda i, k: (i, k))] * 3 + [row, row, _full((1, LANES)), _full((1, D))],
        out_shape=[jax.ShapeDtypeStruct((L, D_FF), BF16)] * 3
        + [jax.ShapeDtypeStruct((L, D), F32), jax.ShapeDtypeStruct((L, D), BF16),
           jax.ShapeDtypeStruct((1, LANES), F32), jax.ShapeDtypeStruct((1, D), F32)],
        scratch_shapes=[pltpu.VMEM((TF, D), F32)],
        compiler_params=_cp("arbitrary", "arbitrary"))(
            up, up, up, up, up, up, conv_w, conv_w, conv_b, conv_b, w_down_b, h1, target, g_final)


def _ffn_act_bwd(c_val, c_gate, w_down_b, dh2):
    L, D = dh2.shape
    n = L // TF
    nk = D_FF // FF_BLK

    def body(v_ref, g_ref, wd_ref, dh_ref, dv_ref, dg_ref, gbv_ref, gbg_ref):
        @pl.when(pl.program_id(1) == 0)
        def _():
            gbv_ref[...] = jnp.zeros_like(gbv_ref)
            gbg_ref[...] = jnp.zeros_like(gbg_ref)

        dh = dh_ref[...]
        for cs in _col_chunks(FF_BLK):
            val, gate = v_ref[:, cs].astype(F32), g_ref[:, cs].astype(F32)
            d_a = _dot_nt(dh, wd_ref[cs, :])
            sg = _sigmoid(gate)
            d_val = d_a * (gate * sg)
            d_gate = d_a * val * (sg * (1.0 + gate * (1.0 - sg)))
            dv_ref[:, cs] = d_val.astype(BF16)
            dg_ref[:, cs] = d_gate.astype(BF16)
            gbv_ref[:, cs] += jnp.sum(d_val, axis=0, keepdims=True)
            gbg_ref[:, cs] += jnp.sum(d_gate, axis=0, keepdims=True)

    blk = pl.BlockSpec((TF, FF_BLK), lambda k, i: (i, k))
    acc = pl.BlockSpec((1, FF_BLK), lambda k, i: (0, k))
    return pl.pallas_call(
        body, name="ffn_act_bwd", grid=(nk, n),
        in_specs=[blk, blk, pl.BlockSpec((FF_BLK, D), lambda k, i: (k, 0)), pl.BlockSpec((TF, D), lambda k, i: (i, 0))],
        out_specs=[blk, blk, acc, acc],
        out_shape=[jax.ShapeDtypeStruct((L, D_FF), BF16), jax.ShapeDtypeStruct((L, D_FF), BF16),
                   jax.ShapeDtypeStruct((1, D_FF), F32), jax.ShapeDtypeStruct((1, D_FF), F32)],
        compiler_params=_cp("arbitrary", "arbitrary"))(c_val, c_gate, w_down_b, dh2)


def _ffn_up_bwd(d_val, d_gate, up, conv_w, w_up4, h1, dh2, g_ffn):
    L, D = h1.shape
    n = L // TF
    nk = D_FF // FF_BLK

    def body(vp, vc, vn, gp, gc, gn, uv_ref, ug_ref, wv_ref, wg_ref, wu_ref, h1_ref, dh2_ref, g_ref,
             dup_ref, dh1_ref, dh1b_ref, gg_ref, gcw_ref, acc_ref):
        i = pl.program_id(0)
        k = pl.program_id(1)

        @pl.when((i == 0) & (k == 0))
        def _():
            gg_ref[...] = jnp.zeros_like(gg_ref)
            gcw_ref[...] = jnp.zeros_like(gcw_ref)

        @pl.when(k == 0)
        def _():
            acc_ref[...] = jnp.zeros_like(acc_ref)

        acc = jnp.zeros((TF, D), F32)
        for j, (blocks, u_ref, w_ref) in enumerate((((vp, vc, vn), uv_ref, wv_ref), ((gp, gc, gn), ug_ref, wg_ref))):
            for cs in _col_chunks(FF_BLK):
                d = blocks[1][:, cs]
                before, after = _neighbours(d, blocks[0], blocks[2], cs, i, n, None)
                taps = (after, d.astype(F32), before)
                w = w_ref[:, cs]
                d_up = (taps[0] * w[0:1, :] + taps[1] * w[1:2, :] + taps[2] * w[2:3, :]).astype(BF16)
                dup_ref[j, :, cs] = d_up
                acc = acc + _dot_nt(d_up, wu_ref[k + j * nk, :, cs])
                x = u_ref[:, cs].astype(F32)
                for r in range(3):
                    gcw_ref[j, k, r:r + 1, cs] += jnp.sum(taps[r] * x, axis=0, keepdims=True)
        acc_ref[...] += acc

        @pl.when(k == nk - 1)
        def _():
            g = g_ref[...]
            _, xh, inv = _rms_fwd(h1_ref[...], g)
            dx, dg = _rms_bwd(acc_ref[...], xh, inv, g)
            gg_ref[...] += dg
            dh1 = dh2_ref[...] + dx
            dh1_ref[...] = dh1
            dh1b_ref[...] = dh1.astype(BF16)

    row = pl.BlockSpec((TF, D), lambda i, k: (i, 0))
    cw = lambda off: pl.BlockSpec((3, FF_BLK), lambda i, k: (0, k + off))
    tile = lambda off: pl.BlockSpec((TF, FF_BLK), lambda i, k: (i, k + off))
    return pl.pallas_call(
        body, name="ffn_up_bwd", grid=(n, nk),
        in_specs=_halo_specs_2d(TF, FF_BLK, L, lambda k: k, "ik") + _halo_specs_2d(TF, FF_BLK, L, lambda k: k, "ik")
        + [tile(0), tile(nk), cw(0), cw(nk), _full(w_up4.shape), row, row, _full((1, D))],
        out_specs=[pl.BlockSpec((2, None, TF, FF_BLK), lambda i, k: (0, k, i, 0)), row, row, _full((1, D)),
                   _full((2, nk, 3, FF_BLK))],
        out_shape=[jax.ShapeDtypeStruct((2, nk, L, FF_BLK), BF16), jax.ShapeDtypeStruct((L, D), F32),
                   jax.ShapeDtypeStruct((L, D), BF16), jax.ShapeDtypeStruct((1, D), F32),
                   jax.ShapeDtypeStruct((2, nk, 3, FF_BLK), F32)],
        scratch_shapes=[pltpu.VMEM((TF, D), F32)],
        compiler_params=_cp("arbitrary", "arbitrary"))(
            d_val, d_val, d_val, d_gate, d_gate, d_gate, up, up, conv_w, conv_w, w_up4, h1, dh2, g_ffn)


def _matmul_tn(a, b, tm, tn, name, tk=2048):
    L, M = a.shape
    N = b.shape[1]
    tk = min(tk, L)

    def body(a_ref, b_ref, o_ref):
        @pl.when(pl.program_id(2) == 0)
        def _():
            o_ref[...] = jnp.zeros_like(o_ref)

        o_ref[...] += _dot_tn(a_ref[...], b_ref[...])

    return pl.pallas_call(
        body, name=name, grid=(M // tm, N // tn, L // tk),
        in_specs=[pl.BlockSpec((tk, tm), lambda m, n, l: (l, m)), pl.BlockSpec((tk, tn), lambda m, n, l: (l, n))],
        out_specs=pl.BlockSpec((tm, tn), lambda m, n, l: (m, n)),
        out_shape=jax.ShapeDtypeStruct((M, N), F32),
        compiler_params=_cp("parallel", "parallel", "arbitrary"))(a, b)


def _matmul_tn_blocks(a, b, tm, name, tk=2048):
    L, M = a.shape
    J, _, N = b.shape
    tk = min(tk, L)

    def body(a_ref, b_ref, o_ref):
        @pl.when(pl.program_id(2) == 0)
        def _():
            o_ref[...] = jnp.zeros_like(o_ref)

        o_ref[...] += _dot_tn(a_ref[...], b_ref[...])

    return pl.pallas_call(
        body, name=name, grid=(M // tm, J, L // tk),
        in_specs=[pl.BlockSpec((tk, tm), lambda m, j, l: (l, m)), pl.BlockSpec((None, tk, N), lambda m, j, l: (j, l, 0))],
        out_specs=pl.BlockSpec((None, tm, N), lambda m, j, l: (j, m, 0)),
        out_shape=jax.ShapeDtypeStruct((J, M, N), F32),
        compiler_params=_cp("parallel", "parallel", "arbitrary"))(a, b)


def _row_tile(rows):
    for t in (512, 352, 256, 128, 64, 8):
        if rows % t == 0:
            return t
    return rows


def _add_half(g, r, c_arr, name, out_dtype=F32):
    _, _, R, C = g.shape
    tr = _row_tile(R)

    def body(c_ref, g_ref, r_ref, o_ref):
        o_ref[...] = (g_ref[...] + r_ref[...]).astype(out_dtype)

    return pl.pallas_call(
        body, name=name,
        grid_spec=pltpu.PrefetchScalarGridSpec(
            num_scalar_prefetch=1, grid=(g.shape[0], R // tr),
            in_specs=[pl.BlockSpec((None, None, tr, C), lambda j, i, c: (j, c[0], i, 0)),
                      pl.BlockSpec((None, tr, C), lambda j, i, c: (j, i, 0))],
            out_specs=pl.BlockSpec((None, tr, C), lambda j, i, c: (j, i, 0))),
        out_shape=jax.ShapeDtypeStruct(r.shape, out_dtype),
        compiler_params=_cp("parallel", "parallel"))(c_arr, g, r)


def _add2(a, b, name):
    R, C = a.shape
    tr = _row_tile(R)

    def body(a_ref, b_ref, o_ref):
        o_ref[...] = a_ref[...] + b_ref[...]

    spec = pl.BlockSpec((tr, C), lambda i: (i, 0))
    return pl.pallas_call(body, name=name, grid=(R // tr,), in_specs=[spec, spec], out_specs=spec,
                          out_shape=jax.ShapeDtypeStruct(a.shape, F32), compiler_params=_cp("parallel"))(a, b)


def _sum4(p, name):
    _, R, C = p.shape
    tr = _row_tile(R)

    def body(p_ref, o_ref):
        q = [p_ref[j].astype(F32) for j in range(4)]
        o_ref[...] = ((q[0] + q[1]) + q[2]) + q[3]

    return pl.pallas_call(
        body, name=name, grid=(R // tr,),
        in_specs=[pl.BlockSpec((4, tr, C), lambda i: (0, i, 0))],
        out_specs=pl.BlockSpec((tr, C), lambda i: (i, 0)),
        out_shape=jax.ShapeDtypeStruct((R, C), F32), compiler_params=_cp("parallel"))(p)


def _adamw_refs(w_ref, g_ref, m_ref, v_ref, d_ref, nm_ref, nv_ref):
    gv = g_ref[...]
    nm = ADAM_B1 * m_ref[...] + (1.0 - ADAM_B1) * gv
    nv = ADAM_B2 * v_ref[...] + (1.0 - ADAM_B2) * (gv * gv)
    m_hat = nm / (1.0 - ADAM_B1 ** ADAM_STEP)
    v_hat = nv / (1.0 - ADAM_B2 ** ADAM_STEP)
    d_ref[...] = -ADAM_LR * (m_hat / (jnp.sqrt(v_hat) + ADAM_EPS) + ADAM_WD * w_ref[...])
    nm_ref[...] = nm
    nv_ref[...] = nv


def _adamw_many(ws, gs, ms, vs, name):
    n = len(ws)

    def body(*refs):
        for k in range(n):
            _adamw_refs(*(refs[j * n + k] for j in range(7)))

    out_shape = [jax.ShapeDtypeStruct(w.shape, F32) for w in ws] * 3
    res = pl.pallas_call(body, name=name, out_shape=out_shape,
                         compiler_params=pltpu.CompilerParams(vmem_limit_bytes=VMEM_LIMIT))(*ws, *gs, *ms, *vs)
    return res[:n], res[n:2 * n], res[2 * n:]


def _adamw(w, g, m, v, name):
    R, C = w.shape
    tr = _row_tile(R)
    body = lambda *refs: _adamw_refs(*refs)

    spec = pl.BlockSpec((tr, C), lambda i: (i, 0))
    sh = jax.ShapeDtypeStruct((R, C), F32)
    return pl.pallas_call(body, name=name, grid=(R // tr,), in_specs=[spec] * 4, out_specs=[spec] * 3,
                          out_shape=[sh] * 3, compiler_params=_cp("parallel"))(w, g, m, v)


def _join_rows(own, other, c_arr, name):
    R, C = own.shape
    tr = _row_tile(R)

    def body(c_ref, own_ref, other_ref, o_ref):
        o_ref[...] = jnp.where(pl.program_id(0) == c_ref[0], own_ref[...], other_ref[...])

    half = pl.BlockSpec((tr, C), lambda h, i, c: (i, 0))
    return pl.pallas_call(
        body, name=name,
        grid_spec=pltpu.PrefetchScalarGridSpec(
            num_scalar_prefetch=1, grid=(2, R // tr), in_specs=[half, half],
            out_specs=pl.BlockSpec((tr, C), lambda h, i, c: (h * (R // tr) + i, 0))),
        out_shape=jax.ShapeDtypeStruct((2 * R, C), F32),
        compiler_params=_cp("parallel", "parallel"))(c_arr, own, other)


def _adamw_halves(w, own, other, m, v, c_arr, name):
    R, C = own.shape
    tr = _row_tile(R)
    while tr * C * 4 > 2 ** 20 and tr % 16 == 0:
        tr //= 2

    def body(c_ref, w_ref, own_ref, other_ref, m_ref, v_ref, g_ref, d_ref, nm_ref, nv_ref):
        g_ref[...] = jnp.where(pl.program_id(0) == c_ref[0], own_ref[...], other_ref[...])
        _adamw_refs(w_ref, g_ref, m_ref, v_ref, d_ref, nm_ref, nv_ref)

    half = pl.BlockSpec((tr, C), lambda h, i, c: (i, 0))
    full = pl.BlockSpec((tr, C), lambda h, i, c: (h * (R // tr) + i, 0))
    sh = jax.ShapeDtypeStruct((2 * R, C), F32)
    return pl.pallas_call(
        body, name=name,
        grid_spec=pltpu.PrefetchScalarGridSpec(
            num_scalar_prefetch=1, grid=(2, R // tr), in_specs=[full, half, half, full, full], out_specs=[full] * 4),
        out_shape=[sh] * 4, compiler_params=_cp("parallel", "parallel"))(c_arr, w, own, other, m, v)


_ANY = pl.BlockSpec(memory_space=pl.ANY)


def _position():
    return lax.axis_index("x"), lax.axis_index("y"), lax.axis_index("c")


class _Comm:
    def __init__(self, arrs, out_shape, sems, start, finish):
        self.arrs, self.out_shape, self.sems, self.start, self.finish = arrs, out_shape, sems, start, finish


def _comm_call(comm, name):
    n, m = len(comm.arrs), len(comm.out_shape)

    def body(*refs):
        ins, outs, sems = refs[:n], refs[n:n + m], refs[n + m:]
        comm.start(ins, outs, sems)
        comm.finish(ins, outs, sems)

    return pl.pallas_call(
        body, name=name, in_specs=[_ANY] * n, out_specs=[_ANY] * m, out_shape=comm.out_shape,
        scratch_shapes=comm.sems, compiler_params=pltpu.CompilerParams(has_side_effects=True))(*comm.arrs)


def _hosted_call(body, comm, *, name, grid, in_specs, out_specs, out_shape, scratch_shapes, args):
    sem = ("arbitrary",) * len(grid)
    if comm is None:
        return pl.pallas_call(body, name=name, grid=grid, in_specs=in_specs, out_specs=out_specs, out_shape=out_shape,
                              scratch_shapes=scratch_shapes, compiler_params=_cp(*sem))(*args), []
    n_in, n_out, n_scr = len(in_specs), len(out_specs), len(scratch_shapes)
    ci, co = len(comm.arrs), len(comm.out_shape)

    def full(*refs):
        ins, refs = refs[:n_in], refs[n_in:]
        cins, refs = refs[:ci], refs[ci:]
        outs, refs = refs[:n_out], refs[n_out:]
        couts, refs = refs[:co], refs[co:]
        scr, csems = refs[:n_scr], refs[n_scr:]
        first, last = True, True
        for d, size in enumerate(grid):
            first = first & (pl.program_id(d) == 0)
            last = last & (pl.program_id(d) == size - 1)

        @pl.when(first)
        def _():
            comm.start(cins, couts, csems)

        body(*ins, *outs, *scr)

        @pl.when(last)
        def _():
            comm.finish(cins, couts, csems)

    res = pl.pallas_call(
        full, name=name, grid=grid, in_specs=list(in_specs) + [_ANY] * ci, out_specs=list(out_specs) + [_ANY] * co,
        out_shape=list(out_shape) + list(comm.out_shape), scratch_shapes=list(scratch_shapes) + list(comm.sems),
        compiler_params=_cp(*sem))(*args, *comm.arrs)
    return res[:n_out], res[n_out:]


def _comm_join(*comms):
    def parts(xs, attr):
        out, at = [], 0
        for cm in comms:
            n = len(getattr(cm, attr))
            out.append(xs[at:at + n])
            at += n
        return out

    def start(ins, outs, sems):
        for cm, i, o, s in zip(comms, parts(ins, "arrs"), parts(outs, "out_shape"), parts(sems, "sems")):
            cm.start(i, o, s)

    def finish(ins, outs, sems):
        for cm, i, o, s in zip(comms, parts(ins, "arrs"), parts(outs, "out_shape"), parts(sems, "sems")):
            cm.finish(i, o, s)

    cat = lambda attr: [x for cm in comms for x in getattr(cm, attr)]
    return _Comm(cat("arrs"), cat("out_shape"), cat("sems"), start, finish)


def _dma_sems(*counts):
    return [pltpu.SemaphoreType.DMA((n,)) for n in counts]


def _comm_pair_swap(arrs, half=False):
    n = len(arrs)
    out_shape = [jax.ShapeDtypeStruct(a.shape[:1] + a.shape[2:] if half else a.shape, a.dtype) for a in arrs]

    def copies(ins, outs, sems):
        x, y, c = _position()
        return [pltpu.make_async_remote_copy(
            src_ref=ins[k].at[:, 1 - c] if half else ins[k], dst_ref=outs[k], send_sem=sems[0].at[k],
            recv_sem=sems[1].at[k], device_id=(x, y, 1 - c), device_id_type=MESH) for k in range(n)]

    def start(ins, outs, sems):
        for cp in copies(ins, outs, sems):
            cp.start()

    def finish(ins, outs, sems):
        for cp in copies(ins, outs, sems):
            cp.wait()

    return _Comm(arrs, out_shape, _dma_sems(n, n), start, finish)


def _chip_of(j, c):
    return (jnp.right_shift(j, 1), jnp.bitwise_and(j, 1), c)


def _comm_chip_exchange(arrs, scatter):
    n = len(arrs)
    out_shape = [jax.ShapeDtypeStruct(a.shape if scatter else (4,) + a.shape, a.dtype) for a in arrs]

    def copies(ins, outs, sems):
        x, y, c = _position()
        me = 2 * x + y
        local, sent, landed = [], [], []
        for k in range(n):
            local.append(pltpu.make_async_copy(ins[k].at[me] if scatter else ins[k], outs[k].at[me], sems[2].at[k]))
            for d in (1, 2, 3):
                j = jnp.bitwise_xor(me, d)
                s = 3 * k + d - 1
                src = ins[k].at[j] if scatter else ins[k]
                for dst, group in ((outs[k].at[me], sent), (outs[k].at[j], landed)):
                    group.append(pltpu.make_async_remote_copy(
                        src_ref=src, dst_ref=dst, send_sem=sems[0].at[s], recv_sem=sems[1].at[s],
                        device_id=_chip_of(j, c), device_id_type=MESH))
        return local, sent, landed

    def start(ins, outs, sems):
        local, sent, _ = copies(ins, outs, sems)
        for cp in local + sent:
            cp.start()

    def finish(ins, outs, sems):
        local, sent, landed = copies(ins, outs, sems)
        for cp in sent:
            cp.wait_send()
        for cp in landed:
            cp.wait_recv()
        for cp in local:
            cp.wait()

    return _Comm(arrs, out_shape, _dma_sems(3 * n, 3 * n, n), start, finish)


def _comm_pair_gather(arrs):
    n = len(arrs)
    out_shape = [jax.ShapeDtypeStruct((2,) + a.shape, a.dtype) for a in arrs]

    def copies(ins, outs, sems):
        x, y, c = _position()
        local, sent, landed = [], [], []
        for k in range(n):
            local.append(pltpu.make_async_copy(ins[k], outs[k].at[c], sems[2].at[k]))
            for dst, group in ((outs[k].at[c], sent), (outs[k].at[1 - c], landed)):
                group.append(pltpu.make_async_remote_copy(
                    src_ref=ins[k], dst_ref=dst, send_sem=sems[0].at[k], recv_sem=sems[1].at[k],
                    device_id=(x, y, 1 - c), device_id_type=MESH))
        return local, sent, landed

    def start(ins, outs, sems):
        local, sent, _ = copies(ins, outs, sems)
        for cp in local + sent:
            cp.start()

    def finish(ins, outs, sems):
        local, sent, landed = copies(ins, outs, sems)
        for cp in sent:
            cp.wait_send()
        for cp in landed:
            cp.wait_recv()
        for cp in local:
            cp.wait()

    return _Comm(arrs, out_shape, _dma_sems(n, n, n), start, finish)


LOCAL_PARTS = 4


def _comm_gather_split(shards, whole):
    n, nw = len(shards), len(whole)
    arrs = list(shards) + list(whole)
    out_shape = [jax.ShapeDtypeStruct((4,) + a.shape, a.dtype) for a in arrs]

    def copies(ins, outs, sems):
        x, y, c = _position()
        me = 2 * x + y
        local, sent, landed, passed, passed_in = [], [], [], [], []
        for k in range(n + nw):
            if k >= n:
                local.append(pltpu.make_async_copy(ins[k], outs[k].at[me], sems[4].at[LOCAL_PARTS * k]))
            else:
                part = shards[k].shape[0] // LOCAL_PARTS
                for r in range(LOCAL_PARTS):
                    local.append(pltpu.make_async_copy(ins[k].at[pl.ds(r * part, part)],
                                                       outs[k].at[me, pl.ds(r * part, part)],
                                                       sems[4].at[LOCAL_PARTS * k + r]))
            for d in (1, 2, 3):
                j = jnp.bitwise_xor(me, d)
                s = 3 * k + d - 1
                if k >= n:
                    src, mine, theirs = ins[k], outs[k].at[me], outs[k].at[j]
                else:
                    h = shards[k].shape[0] // 2
                    rows = pl.ds(pl.multiple_of(c * h, 16), h)
                    other = pl.ds(pl.multiple_of((1 - c) * h, 16), h)
                    src, mine, theirs = ins[k].at[rows], outs[k].at[me, rows], outs[k].at[j, rows]
                    for dst, group in ((theirs, passed), (outs[k].at[j, other], passed_in)):
                        group.append(pltpu.make_async_remote_copy(
                            src_ref=theirs, dst_ref=dst, send_sem=sems[2].at[s], recv_sem=sems[3].at[s],
                            device_id=(x, y, 1 - c), device_id_type=MESH))
                for dst, group in ((mine, sent), (theirs, landed)):
                    group.append(pltpu.make_async_remote_copy(
                        src_ref=src, dst_ref=dst, send_sem=sems[0].at[s], recv_sem=sems[1].at[s],
                        device_id=_chip_of(j, c), device_id_type=MESH))
        return local, sent, landed, passed, passed_in

    def start(ins, outs, sems):
        local, sent, _, _, _ = copies(ins, outs, sems)
        for cp in local + sent:
            cp.start()

    def finish(ins, outs, sems):
        local, sent, landed, passed, passed_in = copies(ins, outs, sems)
        for cp in landed[:3 * n]:
            cp.wait_recv()
        for cp in passed:
            cp.start()
        for cp in landed[3 * n:]:
            cp.wait_recv()
        for cp in sent:
            cp.wait_send()
        for cp in passed:
            cp.wait_send()
        for cp in passed_in:
            cp.wait_recv()
        for cp in local:
            cp.wait()

    t = 3 * (n + nw)
    return _Comm(arrs, out_shape, _dma_sems(t, t, max(3 * n, 1), max(3 * n, 1), LOCAL_PARTS * (n + nw)), start, finish)


def _pack(arrs, row_multiple):
    parts = []
    for a in arrs:
        flat = a.reshape(-1).astype(F32)
        pad = (-flat.shape[0]) % LANES
        parts.append(jnp.pad(flat, (0, pad)) if pad else flat)
    flat = jnp.concatenate(parts)
    rows = -(-flat.shape[0] // LANES)
    rows_p = -(-rows // row_multiple) * row_multiple
    return jnp.pad(flat, (0, rows_p * LANES - flat.shape[0])).reshape(rows_p, LANES)


def _unpack(packed, shapes):
    flat = packed.reshape(-1)
    outs, off = [], 0
    for sh in shapes:
        size = int(np.prod(sh))
        outs.append(flat[off:off + size].reshape(sh))
        off += size + (-size) % LANES
    return outs


SMALL = ["norm_mix_g", "pool_w", "pool_scale", "ssm_log_neg_a_re", "ssm_a_im", "ssm_log_dt", "ssm_b_re", "ssm_b_im",
         "ssm_c_re", "ssm_c_im", "ssm_d", "glu_b", "out_norm_pool_g", "out_norm_ssm_g", "norm_ffn_g", "conv_b",
         "final_norm_g"]
BIG = ["w_in", "glu_w", "w_out", "w_up", "w_down"]
WIDE = ["pool_w", "ssm_b_re", "ssm_b_im", "ssm_c_re", "ssm_c_im"]
WEIGHTS = ['norm_mix_g', 'w_in', 'pool_w', 'pool_scale', 'ssm_log_neg_a_re', 'ssm_a_im', 'ssm_log_dt', 'ssm_b_re',
           'ssm_b_im', 'ssm_c_re', 'ssm_c_im', 'ssm_d', 'glu_w', 'glu_b', 'out_norm_pool_g', 'out_norm_ssm_g', 'w_out',
           'norm_ffn_g', 'w_up', 'conv_w', 'conv_b', 'w_down', 'final_norm_g']


def _local_step(x, target, p, full, shards=None, c_arr=None):
    L, D = x.shape
    dist = shards is not None
    row = lambda a: a.reshape(1, -1)
    w_in = full["w_in"]
    pool_w_b = p["pool_w"].astype(BF16)
    g_mix, g_pool, g_ssm, g_ffn, g_fin = (row(p[k]) for k in (
        "norm_mix_g", "out_norm_pool_g", "out_norm_ssm_g", "norm_ffn_g", "final_norm_g"))
    pool_scale, ssm_d, glu_b, conv_b = (row(p[k]) for k in ("pool_scale", "ssm_d", "glu_b", "conv_b"))

    lnar = p["ssm_log_neg_a_re"].reshape(2 * N_SSM_GROUPS, SSM_STATE)
    aim = p["ssm_a_im"].reshape(2 * N_SSM_GROUPS, SSM_STATE)
    ldt = jnp.broadcast_to(p["ssm_log_dt"].reshape(2 * N_SSM_GROUPS, 1), lnar.shape)
    lam_re, lam_im, f_re, f_im = _ssm_params(lnar, aim, ldt)
    flat2 = lambda a: a.reshape(2, N_STATE)
    lam4 = jnp.stack([flat2(lam_re)[0], flat2(lam_im)[0], flat2(lam_re)[1], flat2(lam_im)[1]])
    tables = _scan_tables(lam4)
    f2 = [jnp.stack([flat2(f_re)[d], flat2(f_im)[d]]) for d in range(2)]
    dense = _ssm_expand(p["ssm_b_re"], p["ssm_b_im"], p["ssm_c_re"], p["ssm_c_im"])
    ssm_args = [tuple(dense[4 * d:4 * d + 4]) + (f2[d], tables) for d in range(2)]

    u, xn = _in_proj(x, g_mix, w_in)
    yn_pool = _pool_fwd(u, pool_w_b, pool_scale, g_pool)
    gather1 = _comm_gather_split([shards[k] for k in ("glu_w", "w_out", "w_down")], [shards["conv_w"]]) if dist else None
    (y0, s0r, s0i), got1 = _ssm_scan_fwd(u, *ssm_args[0], 0, False, comm=gather1)
    gather2 = _comm_gather_split([shards["w_up"]], []) if dist else None
    (y1, s1r, s1i), got2 = _ssm_scan_fwd(u, *ssm_args[1], 2, True, comm=gather2)
    if dist:
        glu_w, w_out, w_down = (g.reshape((-1,) + g.shape[2:]) for g in got1[:3])
        conv_w = jnp.transpose(got1[3], (1, 0, 2)).reshape(3, -1)
        w_up4 = got2[0]
    else:
        glu_w, w_out, w_up4, w_down, conv_w = (full[k] for k in ("glu_w", "w_out", "w_up", "w_down", "conv_w"))
    h1, hn, ycat = _mix_out(yn_pool, y0, y1, u, x, ssm_d, glu_w, glu_b, g_ssm, w_out, g_ffn)
    up = _ffn_up(hn, w_up4)
    a, c_val, c_gate, dh2, dh2_b, loss, g_final = _ffn_down_loss(up, conv_w, conv_b, w_down, h1, target, g_fin)

    d_val, d_gate, gbv, gbg = _ffn_act_bwd(c_val, c_gate, w_down, dh2_b)
    g_w_down = _matmul_tn(a, dh2_b, FF_BLK, D, "grad_w_down")
    d_up, dh1, dh1_b, g_ffn_g, gcw = _ffn_up_bwd(d_val, d_gate, up, conv_w, w_up4, h1, dh2, g_ffn)
    g_w_up = _matmul_tn_blocks(hn, d_up.reshape(4, L, FF_BLK), 512, "grad_w_up")
    g_w_out = _matmul_tn(ycat, dh1_b, 512, D, "grad_w_out")
    dy, du_direct, g_glu_w, g_glu_b, g_ssm_d, g_ssm_g = _ssm_bwd_local(dh1_b, y0, y1, u, ssm_d, glu_w, glu_b, g_ssm, w_out)
    late = ("w_up", "w_down", "w_out", "glu_w")
    halves = [g_w_up.reshape(4, 2, D // 2, FF_BLK), g_w_down.reshape(4, 2, D_FF // 8, D),
              g_w_out.reshape(4, 2, D // 8, D), g_glu_w.reshape(4, 2, D_SSM // 8, D_SSM)]
    (d_pooled, g_pool_w, g_pool_scale, g_pool_g), from_sibling = _pool_bwd_local(
        dh1_b, u, w_out, pool_w_b, pool_scale, g_pool, comm=_comm_pair_swap(halves, half=True) if dist else None)
    du_pool = _pool_bwd_window(d_pooled)
    reduce2 = None
    if dist:
        chip_sums = [_add_half(h, r, c_arr, "sum_pair_" + k, BF16) for k, h, r in zip(late, halves, from_sibling)]
        reduce2 = _comm_chip_exchange(chip_sums, scatter=True)
    (du0, gb0r, gb0i, gc0r, gc0i, gv0), from_chips = _ssm_scan_bwd(dy, u, s0r, s0i, *ssm_args[0], 1, True, comm=reduce2)
    mine = [_sum4(r, "sum_chips_" + k) for k, r in zip(late, from_chips)]
    (du1, gb1r, gb1i, gc1r, gc1i, gv1), theirs = _ssm_scan_bwd(
        dy, u, s1r, s1i, *ssm_args[1], 3, False, comm=_comm_pair_swap(mine) if dist else None)
    gvec = lambda j: jnp.stack([gv0[j], gv1[j]]).reshape(2 * N_SSM_GROUPS, SSM_STATE)
    g_lnar, g_aim, g_ldt = _ssm_params_bwd(lnar, aim, ldt, gvec(0), gvec(1), gvec(2), gvec(3))
    grad_x, d_u_b, g_mix_g = _in_bwd(du_pool, du_direct, du0, du1, dh1, x, g_mix, w_in)
    g_w_in = _matmul_tn(xn, d_u_b, 512, D, "grad_w_in")

    small = {
        "norm_mix_g": g_mix_g, "pool_w": g_pool_w, "pool_scale": g_pool_scale,
        "ssm_log_neg_a_re": g_lnar, "ssm_a_im": g_aim, "ssm_log_dt": g_ldt,
        "ssm_b_re": jnp.swapaxes(jnp.stack([gb0r, gb1r]), 2, 3), "ssm_b_im": jnp.swapaxes(jnp.stack([gb0i, gb1i]), 2, 3),
        "ssm_c_re": jnp.stack([gc0r, gc1r]), "ssm_c_im": jnp.stack([gc0i, gc1i]),
        "ssm_d": g_ssm_d, "glu_b": g_glu_b, "out_norm_pool_g": g_pool_g, "out_norm_ssm_g": g_ssm_g,
        "norm_ffn_g": g_ffn_g, "conv_b": jnp.concatenate([gbv[0], gbg[0]]), "final_norm_g": g_final,
        "conv_w": jnp.transpose(gcw, (2, 0, 1, 3)).reshape(3, -1),
    }
    big = {"w_in": g_w_in}
    reduced = dict(zip(late, zip(mine, theirs)))
    if not dist:
        big.update({"w_up": g_w_up, "w_down": g_w_down, "w_out": g_w_out, "glu_w": g_glu_w})
    return loss, grad_x, small, big, reduced


def kernel(x, norm_mix_g, w_in, pool_w, pool_scale, ssm_log_neg_a_re, ssm_a_im, ssm_log_dt, ssm_b_re, ssm_b_im, ssm_c_re, ssm_c_im, ssm_d, glu_w, glu_b, out_norm_pool_g, out_norm_ssm_g, w_out, norm_ffn_g, w_up, conv_w, conv_b, w_down, final_norm_g, loss_target, m_norm_mix_g, m_w_in, m_pool_w, m_pool_scale, m_ssm_log_neg_a_re, m_ssm_a_im, m_ssm_log_dt, m_ssm_b_re, m_ssm_b_im, m_ssm_c_re, m_ssm_c_im, m_ssm_d, m_glu_w, m_glu_b, m_out_norm_pool_g, m_out_norm_ssm_g, m_w_out, m_norm_ffn_g, m_w_up, m_conv_w, m_conv_b, m_w_down, m_final_norm_g, v_norm_mix_g, v_w_in, v_pool_w, v_pool_scale, v_ssm_log_neg_a_re, v_ssm_a_im, v_ssm_log_dt, v_ssm_b_re, v_ssm_b_im, v_ssm_c_re, v_ssm_c_im, v_ssm_d, v_glu_w, v_glu_b, v_out_norm_pool_g, v_out_norm_ssm_g, v_w_out, v_norm_ffn_g, v_w_up, v_conv_w, v_conv_b, v_w_down, v_final_norm_g):
    args = locals()
    w = {k: args[k] for k in WEIGHTS}
    m = {k: args["m_" + k] for k in WEIGHTS}
    v = {k: args["v_" + k] for k in WEIGHTS}
    chip = 2 * lax.axis_index("x") + lax.axis_index("y")
    c_arr = lax.axis_index("c").astype(jnp.int32).reshape(1)

    shards = {k: w[k].astype(BF16) for k in BIG}
    shards["conv_w"] = conv_w
    w_in_full = _comm_call(_comm_gather_split([shards["w_in"]], []), "gather_w_in")[0]
    loss, grad_x, g_small, g_big, reduced = _local_step(
        x[0], loss_target[0], w, {"w_in": w_in_full.reshape(-1, w_in_full.shape[-1])}, shards, c_arr)

    exact = [k for k in SMALL if k not in WIDE]
    packs = [_pack([loss] + [g_small[k] for k in exact] + [g_small["conv_w"]], 512),
             _pack([g_small[k] for k in WIDE], 512)]
    halves = [g_big["w_in"].reshape(4, 2, g_big["w_in"].shape[0] // 8, -1)]
    halves += [pk.reshape(1, 2, pk.shape[0] // 2, LANES) for pk in packs]
    from_sibling = _comm_call(_comm_pair_swap(halves, half=True), "reduce_pair")
    names = ("w_in", "exact", "wide")
    sums = [_add_half(h, r, c_arr, "sum_pair_" + k, dt)
            for k, h, r, dt in zip(names, halves, from_sibling, (BF16, F32, BF16))]
    from_chips = _comm_call(_comm_join(_comm_chip_exchange(sums[:1], scatter=True),
                                       _comm_chip_exchange([s[0] for s in sums[1:]], scatter=False)), "reduce_chips")
    mine = [_sum4(r, "sum_chips_" + k) for k, r in zip(names, from_chips)]
    theirs = _comm_call(_comm_pair_swap(mine), "swap_halves")
    grads = {}
    exact_all = _join_rows(mine[1], theirs[1], c_arr, "join_exact")
    wide_all = _join_rows(mine[2], theirs[2], c_arr, "join_wide")
    shapes = [loss.shape] + [w[k].shape for k in exact] + [(3, 4 * FF_BLK)]
    grads.update(zip(["loss"] + exact + ["conv_w_full"], _unpack(exact_all, shapes)))
    grads.update(zip(WIDE, _unpack(wide_all, [w[k].shape for k in WIDE])))
    loss = grads.pop("loss")[0, 0]
    grads["conv_w"] = lax.dynamic_slice_in_dim(grads.pop("conv_w_full"), chip * FF_BLK, FF_BLK, axis=1)

    delta, new_m, new_v = {}, {}, {}
    reduced["w_in"] = (mine[0], theirs[0])
    for k, (own, other) in reduced.items():
        grads[k], delta[k], new_m[k], new_v[k] = _adamw_halves(w[k], own, other, m[k], v[k], c_arr, "adamw_" + k)
    padded = ["ssm_b_re", "ssm_b_im"]
    for keys, name in ((padded, "adamw_ssm_b"), ([k for k in SMALL + ["conv_w"] if k not in padded], "adamw_small")):
        outs = _adamw_many(*([d[k] for k in keys] for d in (w, grads, m, v)), name)
        for d, o in zip((delta, new_m, new_v), outs):
            d.update(zip(keys, o))

    return (loss, grad_x[None], *[grads[k] for k in WEIGHTS], *[delta[k] for k in WEIGHTS],
            *[new_m[k] for k in WEIGHTS], *[new_v[k] for k in WEIGHTS])
```

```python
import numpy as np
import jax
import jax.numpy as jnp
from jax import lax
from jax.experimental import pallas as pl
from jax.experimental.pallas import tpu as pltpu

F32 = jnp.float32
BF16 = jnp.bfloat16
MESH = pl.DeviceIdType.MESH

EPS = 1e-6
POOL_WINDOWS = (2, 4, 8, 16)
POOL_GROUP = 128
SSM_GROUP = 16
SSM_STATE = 64
N_SSM_GROUPS = 32
N_STATE = N_SSM_GROUPS * SSM_STATE
QUAD = 256
N_QUAD = N_STATE // QUAD
SLAB = 256
D_SSM = 512
D_POOL = 512
D_FF = 2816
FF_BLK = 1408
HALO = 8
HALO_B = 16
LANES = 128
ADAM_LR, ADAM_B1, ADAM_B2, ADAM_EPS, ADAM_WD, ADAM_STEP = 0.001, 0.9, 0.999, 1e-08, 0.01, 10
VMEM_LIMIT = 56 * 2 ** 20

TL = 512
TF = 256
TC = 512
SCAN_W = 512


def _cp(*sem):
    return pltpu.CompilerParams(dimension_semantics=sem, vmem_limit_bytes=VMEM_LIMIT)


def _dot_nn(a, b):
    return jnp.dot(a, b, preferred_element_type=F32)


def _dot_nt(a, b):
    return lax.dot_general(a, b, (((1,), (1,)), ((), ())), preferred_element_type=F32)


def _dot_tn(a, b):
    return lax.dot_general(a, b, (((0,), (0,)), ((), ())), preferred_element_type=F32)


def _rms_fwd(x, g):
    inv = lax.rsqrt(jnp.mean(x * x, axis=-1, keepdims=True) + EPS)
    xh = x * inv
    return xh * g, xh, inv


def _rms_bwd(dy, xh, inv, g):
    dg = jnp.sum(dy * xh, axis=0, keepdims=True)
    dxh = dy * g
    dx = inv * (dxh - xh * jnp.mean(dxh * xh, axis=-1, keepdims=True))
    return dx, dg


_GELU_C = 0.7978845608028654
_GELU_A = 0.044715


def _gelu(y):
    t = jnp.tanh(_GELU_C * (y + _GELU_A * (y * y * y)))
    return 0.5 * y * (1.0 + t), t


def _gelu_grad(y, t):
    return 0.5 * (1.0 + t) + 0.5 * y * (1.0 - t * t) * (_GELU_C * (1.0 + 3.0 * _GELU_A * y * y))


def _sigmoid(x):
    return 1.0 / (1.0 + jnp.exp(-x))


def _full(shape):
    n = len(shape)
    return pl.BlockSpec(shape, lambda *_: (0,) * n)


def _fill_ext(ext_ref, prev_ref, cur_ref, next_ref, i, n, rows):
    ext_ref[0:HALO, :] = jnp.where(i > 0, prev_ref[...], 0.0).astype(ext_ref.dtype)
    ext_ref[HALO:HALO + rows, :] = cur_ref[...]
    ext_ref[HALO + rows:2 * HALO + rows, :] = jnp.where(i < n - 1, next_ref[...], 0.0).astype(ext_ref.dtype)


def _in_proj(x, g, w):
    L, D = x.shape
    E = w.shape[1]

    def body(x_ref, g_ref, w_ref, u_ref, xn_ref):
        y, _, _ = _rms_fwd(x_ref[...], g_ref[...])
        yb = y.astype(BF16)
        xn_ref[...] = yb
        u_ref[...] = _dot_nn(yb, w_ref[...])

    return pl.pallas_call(
        body, name="in_proj", grid=(L // TL,),
        in_specs=[pl.BlockSpec((TL, D), lambda i: (i, 0)), _full((1, D)), _full(w.shape)],
        out_specs=[pl.BlockSpec((TL, E), lambda i: (i, 0)), pl.BlockSpec((TL, D), lambda i: (i, 0))],
        out_shape=[jax.ShapeDtypeStruct((L, E), F32), jax.ShapeDtypeStruct((L, D), BF16)],
        compiler_params=_cp("parallel"))(x, g, w)


def _halo_specs_1d(rows, width, L, col):
    rb = rows // HALO
    last = L // HALO - 1
    return [pl.BlockSpec((HALO, width), lambda i: (jnp.maximum(i * rb - 1, 0), col)),
            pl.BlockSpec((rows, width), lambda i: (i, col)),
            pl.BlockSpec((HALO, width), lambda i: (jnp.minimum((i + 1) * rb, last), col))]


def _pooled_from_ext(ext_ref, t0, rows, L):
    t = t0 + lax.broadcasted_iota(jnp.int32, (rows, 1), 0)
    outs = []
    for gi, w in enumerate(POOL_WINDOWS):
        half = w // 2
        cs = slice(gi * POOL_GROUP, (gi + 1) * POOL_GROUP)
        acc = ext_ref[pl.ds(HALO - half, rows), cs]
        for s in range(-half + 1, half):
            acc = acc + ext_ref[pl.ds(HALO + s, rows), cs]
        cnt = (jnp.minimum(t + half, L) - jnp.maximum(t - half, 0)).astype(F32)
        outs.append(acc / cnt - ext_ref[pl.ds(HALO, rows), cs])
    return outs


def _pool_fwd(u, pool_w_b, pool_scale, g_pool):
    L = u.shape[0]
    n = L // TL

    def body(prev_ref, cur_ref, next_ref, pw_ref, ps_ref, g_ref, out_ref, ext_ref):
        i = pl.program_id(0)
        _fill_ext(ext_ref, prev_ref, cur_ref, next_ref, i, n, TL)
        pooled = _pooled_from_ext(ext_ref, i * TL, TL, L)
        ypre = jnp.concatenate([_dot_nn(pooled[gi].astype(BF16), pw_ref[gi]) for gi in range(4)], axis=-1)
        yn, _, _ = _rms_fwd(ypre * ps_ref[...], g_ref[...])
        out_ref[...] = yn.astype(BF16)

    return pl.pallas_call(
        body, name="pool_fwd", grid=(n,),
        in_specs=_halo_specs_1d(TL, D_POOL, L, 0) + [_full(pool_w_b.shape), _full((1, D_POOL)), _full((1, D_POOL))],
        out_specs=pl.BlockSpec((TL, D_POOL), lambda i: (i, 0)),
        out_shape=jax.ShapeDtypeStruct((L, D_POOL), BF16),
        scratch_shapes=[pltpu.VMEM((TL + 2 * HALO, D_POOL), F32)],
        compiler_params=_cp("parallel"))(u, u, u, pool_w_b, pool_scale, g_pool)


def _pool_bwd_local(dh1, u, w_out_b, pool_w_b, pool_scale, g_pool, comm=None):
    L = u.shape[0]
    n = L // TL
    D = dh1.shape[1]

    def body(dh_ref, prev_ref, cur_ref, next_ref, wo_ref, pw_ref, ps_ref, g_ref,
             dp_ref, gpw_ref, gps_ref, gg_ref, ext_ref):
        i = pl.program_id(0)

        @pl.when(i == 0)
        def _():
            gpw_ref[...] = jnp.zeros_like(gpw_ref)
            gps_ref[...] = jnp.zeros_like(gps_ref)
            gg_ref[...] = jnp.zeros_like(gg_ref)

        _fill_ext(ext_ref, prev_ref, cur_ref, next_ref, i, n, TL)
        pooled = [p.astype(BF16) for p in _pooled_from_ext(ext_ref, i * TL, TL, L)]
        ypre = jnp.concatenate([_dot_nn(pooled[gi], pw_ref[gi]) for gi in range(4)], axis=-1)
        ps = ps_ref[...]
        g = g_ref[...]
        _, xh, inv = _rms_fwd(ypre * ps, g)
        d_yn = _dot_nt(dh_ref[...].astype(BF16), wo_ref[...])
        d_y, dg = _rms_bwd(d_yn, xh, inv, g)
        gg_ref[...] += dg
        gps_ref[...] += jnp.sum(d_y * ypre, axis=0, keepdims=True)
        d_ypre = (d_y * ps).astype(BF16)
        for gi in range(4):
            cs = slice(gi * POOL_GROUP, (gi + 1) * POOL_GROUP)
            dp_ref[:, cs] = _dot_nt(d_ypre[:, cs], pw_ref[gi])
            gpw_ref[gi] += _dot_tn(pooled[gi], d_ypre[:, cs])

    return _hosted_call(
        body, comm, name="pool_bwd_local", grid=(n,),
        in_specs=[pl.BlockSpec((TL, D), lambda i: (i, 0))] + _halo_specs_1d(TL, D_POOL, L, 0)
        + [pl.BlockSpec((D_POOL, D), lambda i: (0, 0)), _full(pool_w_b.shape), _full((1, D_POOL)), _full((1, D_POOL))],
        out_specs=[pl.BlockSpec((TL, D_POOL), lambda i: (i, 0)), _full(pool_w_b.shape),
                   _full((1, D_POOL)), _full((1, D_POOL))],
        out_shape=[jax.ShapeDtypeStruct((L, D_POOL), F32), jax.ShapeDtypeStruct(pool_w_b.shape, F32),
                   jax.ShapeDtypeStruct((1, D_POOL), F32), jax.ShapeDtypeStruct((1, D_POOL), F32)],
        scratch_shapes=[pltpu.VMEM((TL + 2 * HALO, D_POOL), F32)],
        args=(dh1, u, u, u, w_out_b, pool_w_b, pool_scale, g_pool))


def _pool_bwd_window(d_pooled):
    L = d_pooled.shape[0]
    n = L // TL
    R = TL + 2 * HALO

    def body(prev_ref, cur_ref, next_ref, out_ref, ext_ref, q_ref):
        i = pl.program_id(0)
        _fill_ext(ext_ref, prev_ref, cur_ref, next_ref, i, n, TL)
        tr = i * TL - HALO + lax.broadcasted_iota(jnp.int32, (R, 1), 0)
        for gi, w in enumerate(POOL_WINDOWS):
            half = w // 2
            cs = slice(gi * POOL_GROUP, (gi + 1) * POOL_GROUP)
            cnt = jnp.maximum(jnp.minimum(tr + half, L) - jnp.maximum(tr - half, 0), 1).astype(F32)
            q_ref[:, cs] = ext_ref[:, cs] / cnt
        for gi, w in enumerate(POOL_WINDOWS):
            half = w // 2
            cs = slice(gi * POOL_GROUP, (gi + 1) * POOL_GROUP)
            acc = q_ref[pl.ds(HALO - half + 1, TL), cs]
            for s in range(-half + 2, half + 1):
                acc = acc + q_ref[pl.ds(HALO + s, TL), cs]
            out_ref[:, cs] = acc - ext_ref[pl.ds(HALO, TL), cs]

    return pl.pallas_call(
        body, name="pool_bwd_window", grid=(n,),
        in_specs=_halo_specs_1d(TL, D_POOL, L, 0),
        out_specs=pl.BlockSpec((TL, D_POOL), lambda i: (i, 0)),
        out_shape=jax.ShapeDtypeStruct((L, D_POOL), F32),
        scratch_shapes=[pltpu.VMEM((R, D_POOL), F32), pltpu.VMEM((R, D_POOL), F32)],
        compiler_params=_cp("parallel"))(d_pooled, d_pooled, d_pooled)


def _ssm_param_fn(lnar, aim, ldt):
    dt = jnp.exp(ldt)
    a_re = -jnp.exp(lnar)
    mag = jnp.exp(a_re * dt)
    ang = aim * dt
    lr, li = mag * jnp.cos(ang), mag * jnp.sin(ang)
    den = a_re * a_re + aim * aim
    fr = ((lr - 1.0) * a_re + li * aim) / den
    fi = (li * a_re - (lr - 1.0) * aim) / den
    return lr, li, fr, fi


def _ssm_params(lnar, aim, ldt):
    def body(a_ref, b_ref, c_ref, lr_ref, li_ref, fr_ref, fi_ref):
        lr, li, fr, fi = _ssm_param_fn(a_ref[...], b_ref[...], c_ref[...])
        lr_ref[...] = lr
        li_ref[...] = li
        fr_ref[...] = fr
        fi_ref[...] = fi

    sh = jax.ShapeDtypeStruct(lnar.shape, F32)
    return pl.pallas_call(body, name="ssm_params", out_shape=[sh] * 4)(lnar, aim, ldt)


def _ssm_params_bwd(lnar, aim, ldt, glr, gli, gfr, gfi):
    def body(a_ref, b_ref, c_ref, g0, g1, g2, g3, da_ref, db_ref, dc_ref):
        _, vjp = jax.vjp(_ssm_param_fn, a_ref[...], b_ref[...], c_ref[...])
        da, db, dc = vjp((g0[...], g1[...], g2[...], g3[...]))
        da_ref[...] = da
        db_ref[...] = db
        dc_ref[...] = jnp.sum(dc, axis=1, keepdims=True)

    return pl.pallas_call(
        body, name="ssm_params_bwd",
        out_shape=[jax.ShapeDtypeStruct(lnar.shape, F32), jax.ShapeDtypeStruct(aim.shape, F32),
                   jax.ShapeDtypeStruct((ldt.shape[0], 1), F32)])(lnar, aim, ldt, glr, gli, gfr, gfi)


def _scan_tables(lam4):
    def build(lr, li, reverse, out_ref, k):
        row = lax.broadcasted_iota(jnp.int32, (8, N_STATE), 0)
        lrb = jnp.broadcast_to(lr, (8, N_STATE))
        lib = jnp.broadcast_to(li, (8, N_STATE))
        pr, pi = lrb, lib
        for s, sh in enumerate((1, 2, 4)):
            mask = (row < 8 - sh) if reverse else (row >= sh)
            out_ref[k, 2 * s] = jnp.where(mask, pr, 0.0)
            out_ref[k, 2 * s + 1] = jnp.where(mask, pi, 0.0)
            pr, pi = pr * pr - pi * pi, 2.0 * pr * pi
        pr, pi = lrb, lib
        p8r = jnp.zeros((8, N_STATE), F32)
        p8i = jnp.zeros((8, N_STATE), F32)
        for j in range(8):
            r = 7 - j if reverse else j
            p8r = jnp.where(row == r, pr, p8r)
            p8i = jnp.where(row == r, pi, p8i)
            pr, pi = pr * lrb - pi * lib, pr * lib + pi * lrb
        out_ref[k, 6] = p8r
        out_ref[k, 7] = p8i

    def body(lam_ref, out_ref):
        l0r, l0i, l1r, l1i = (lam_ref[j:j + 1, :] for j in range(4))
        build(l0r, l0i, False, out_ref, 0)
        build(l0r, -l0i, True, out_ref, 1)
        build(l1r, l1i, True, out_ref, 2)
        build(l1r, -l1i, False, out_ref, 3)

    return pl.pallas_call(body, name="scan_tables",
                          out_shape=jax.ShapeDtypeStruct((4, 8, 8, N_STATE), F32))(lam4)


def _b_block(g):
    q, gl = divmod(g, 4)
    r0, c0 = gl * SSM_STATE, (q % 4) * 4 * SSM_GROUP + gl * SSM_GROUP
    return q, slice(r0, r0 + SSM_STATE), slice(c0, c0 + SSM_GROUP)


def _c_block(g):
    q, rows, cols = _b_block(g)
    return q, cols, rows


def _ssm_expand(b_re, b_im, c_re, c_im):
    def body(bre_ref, bim_ref, cre_ref, cim_ref, *rest):
        outs, tmp = rest[:8], rest[8]
        for d in range(2):
            for j, (src, where) in enumerate(((bre_ref, _b_block), (bim_ref, _b_block),
                                              (cre_ref, _c_block), (cim_ref, _c_block))):
                tmp[...] = jnp.zeros_like(tmp)
                for g in range(N_SSM_GROUPS):
                    q, rows, cols = where(g)
                    tmp[q, rows, cols] = src[d, g]
                outs[4 * d + j][...] = tmp[...].astype(BF16)

    dense = jax.ShapeDtypeStruct((N_QUAD, QUAD, SLAB), BF16)
    return pl.pallas_call(body, name="ssm_expand", out_shape=[dense] * 8,
                          scratch_shapes=[pltpu.VMEM((N_QUAD, QUAD, SLAB), F32)],
                          compiler_params=pltpu.CompilerParams(vmem_limit_bytes=VMEM_LIMIT))(b_re, b_im, c_re, c_im)


def _scan_rows(src_re, src_im, dst_re, dst_im, tab_ref, k, carry_re, carry_im, rows, reverse, s_refs=None):
    ng = rows // 8
    edge = 0 if reverse else 7
    row_id = lax.broadcasted_iota(jnp.int32, (8, SCAN_W), 0)
    sums = []
    for lt in range(N_STATE // SCAN_W):
        sl = slice(lt * SCAN_W, (lt + 1) * SCAN_W)

        def step(r, c, sl=sl):
            tabs = [tab_ref[k, j, :, sl] for j in range(8)]
            cr, ci = c[0], c[1]
            row = pl.multiple_of((ng - 1 - r) * 8 if reverse else r * 8, 8)
            xr = src_re[pl.ds(row, 8), sl]
            xi = src_im[pl.ds(row, 8), sl]
            for s, sh in enumerate((1, 2, 4)):
                amt = 8 - sh if reverse else sh
                rr = pltpu.roll(xr, amt, 0)
                ri = pltpu.roll(xi, amt, 0)
                mr, mi = tabs[2 * s], tabs[2 * s + 1]
                xr, xi = xr + mr * rr - mi * ri, xi + mr * ri + mi * rr
            xr, xi = xr + tabs[6] * cr - tabs[7] * ci, xi + tabs[6] * ci + tabs[7] * cr
            dst_re[pl.ds(row, 8), sl] = xr
            dst_im[pl.ds(row, 8), sl] = xi
            ncr = jnp.broadcast_to(xr[edge:edge + 1, :], (8, SCAN_W))
            nci = jnp.broadcast_to(xi[edge:edge + 1, :], (8, SCAN_W))
            if s_refs is None:
                return ncr, nci
            amt = 7 if reverse else 1
            far = 7 if reverse else 0
            nr = jnp.where(row_id == far, cr, pltpu.roll(xr, amt, 0))
            ni = jnp.where(row_id == far, ci, pltpu.roll(xi, amt, 0))
            sr = s_refs[0][pl.ds(row, 8), sl]
            si = s_refs[1][pl.ds(row, 8), sl]
            return ncr, nci, c[2] + nr * sr + ni * si, c[3] + ni * sr - nr * si

        init = (carry_re[:, sl], carry_im[:, sl])
        if s_refs is not None:
            init = init + (jnp.zeros((8, SCAN_W), F32), jnp.zeros((8, SCAN_W), F32))
        out = lax.fori_loop(0, ng, step, init)
        carry_re[:, sl] = out[0]
        carry_im[:, sl] = out[1]
        if s_refs is not None:
            sums.append((jnp.sum(out[2], axis=0, keepdims=True), jnp.sum(out[3], axis=0, keepdims=True)))
    return sums


def _ssm_scan_fwd(u, b_re, b_im, c_re, c_im, f2, tables, k, reverse, comm=None):
    L = u.shape[0]
    nc = L // TC
    chunk = (lambda i: nc - 1 - i) if reverse else (lambda i: i)

    def body(u_ref, bre_ref, bim_ref, cre_ref, cim_ref, f_ref, tab_ref,
             y_ref, sre_ref, sim_ref, in_re, in_im, carry_re, carry_im):
        @pl.when(pl.program_id(0) == 0)
        def _():
            carry_re[...] = jnp.zeros_like(carry_re)
            carry_im[...] = jnp.zeros_like(carry_im)

        ub = u_ref[...].astype(BF16)
        for q in range(N_QUAD):
            qs = slice(q * QUAD, (q + 1) * QUAD)
            us = ub[:, (q // 4) * SLAB:(q // 4 + 1) * SLAB]
            bur = _dot_nt(us, bre_ref[q])
            bui = _dot_nt(us, bim_ref[q])
            fr = f_ref[0:1, qs]
            fi = f_ref[1:2, qs]
            in_re[:, qs] = fr * bur - fi * bui
            in_im[:, qs] = fr * bui + fi * bur
        _scan_rows(in_re, in_im, sre_ref, sim_ref, tab_ref, k, carry_re, carry_im, TC, reverse)
        for j in range(D_SSM // SLAB):
            acc = jnp.zeros((TC, SLAB), F32)
            for q in range(4 * j, 4 * j + 4):
                qs = slice(q * QUAD, (q + 1) * QUAD)
                acc = acc + _dot_nt(sre_ref[:, qs].astype(BF16), cre_ref[q])
                acc = acc - _dot_nt(sim_ref[:, qs].astype(BF16), cim_ref[q])
            y_ref[:, j * SLAB:(j + 1) * SLAB] = acc

    return _hosted_call(
        body, comm, name="ssm_scan_rev" if reverse else "ssm_scan_fwd", grid=(nc,),
        in_specs=[pl.BlockSpec((TC, D_SSM), lambda i: (chunk(i), 1))]
        + [_full(b_re.shape)] * 4 + [_full(f2.shape), _full(tables.shape)],
        out_specs=[pl.BlockSpec((TC, D_SSM), lambda i: (chunk(i), 0)),
                   pl.BlockSpec((TC, N_STATE), lambda i: (chunk(i), 0)),
                   pl.BlockSpec((TC, N_STATE), lambda i: (chunk(i), 0))],
        out_shape=[jax.ShapeDtypeStruct((L, D_SSM), F32), jax.ShapeDtypeStruct((L, N_STATE), F32),
                   jax.ShapeDtypeStruct((L, N_STATE), F32)],
        scratch_shapes=[pltpu.VMEM((TC, N_STATE), F32), pltpu.VMEM((TC, N_STATE), F32),
                        pltpu.VMEM((8, N_STATE), F32), pltpu.VMEM((8, N_STATE), F32)],
        args=(u, b_re, b_im, c_re, c_im, f2, tables))


def _quad_channels(q):
    c0 = (q // 4) * SLAB + (q % 4) * 4 * SSM_GROUP
    return slice(c0, c0 + 4 * SSM_GROUP)


def _ssm_scan_bwd(dy, u, s_re, s_im, b_re, b_im, c_re, c_im, f2, tables, k, reverse, comm=None):
    L = u.shape[0]
    nc = L // TC
    chunk = (lambda i: nc - 1 - i) if reverse else (lambda i: i)

    def body(dy_ref, u_ref, sre_ref, sim_ref, bre_ref, bim_ref, cre_ref, cim_ref, f_ref, tab_ref,
             du_ref, ob_re, ob_im, oc_re, oc_im, gv_ref,
             a_re, a_im, carry_re, carry_im, gbr_ref, gbi_ref, gcr_ref, gci_ref):
        @pl.when(pl.program_id(0) == 0)
        def _():
            carry_re[...] = jnp.zeros_like(carry_re)
            carry_im[...] = jnp.zeros_like(carry_im)
            for r in (gbr_ref, gbi_ref, gcr_ref, gci_ref, gv_ref):
                r[...] = jnp.zeros_like(r)

        dyb = dy_ref[...].astype(BF16)
        ub = u_ref[...].astype(BF16)
        for q in range(N_QUAD):
            qs = slice(q * QUAD, (q + 1) * QUAD)
            ds = dyb[:, (q // 4) * SLAB:(q // 4 + 1) * SLAB]
            a_re[:, qs] = _dot_nn(ds, cre_ref[q])
            a_im[:, qs] = -_dot_nn(ds, cim_ref[q])
            dq = dyb[:, _quad_channels(q)]
            gcr_ref[q] += _dot_tn(dq, sre_ref[:, qs].astype(BF16))
            gci_ref[q] -= _dot_tn(dq, sim_ref[:, qs].astype(BF16))
        sums = _scan_rows(a_re, a_im, a_re, a_im, tab_ref, k, carry_re, carry_im, TC, reverse,
                          s_refs=(sre_ref, sim_ref))
        for lt, (glr, gli) in enumerate(sums):
            sl = slice(lt * SCAN_W, (lt + 1) * SCAN_W)
            gv_ref[0:1, sl] += glr
            gv_ref[1:2, sl] += gli
        for j in range(D_SSM // SLAB):
            us = ub[:, j * SLAB:(j + 1) * SLAB]
            acc = jnp.zeros((TC, SLAB), F32)
            for q in range(4 * j, 4 * j + 4):
                qs = slice(q * QUAD, (q + 1) * QUAD)
                ar = a_re[:, qs]
                ai = a_im[:, qs]
                bur = _dot_nt(us, bre_ref[q])
                bui = _dot_nt(us, bim_ref[q])
                gv_ref[2:3, qs] += jnp.sum(ar * bur + ai * bui, axis=0, keepdims=True)
                gv_ref[3:4, qs] += jnp.sum(ai * bur - ar * bui, axis=0, keepdims=True)
                fr = f_ref[0:1, qs]
                fi = f_ref[1:2, qs]
                dbr = (fr * ar + fi * ai).astype(BF16)
                dbi = (fr * ai - fi * ar).astype(BF16)
                uq = ub[:, _quad_channels(q)]
                gbr_ref[q] += _dot_tn(uq, dbr)
                gbi_ref[q] += _dot_tn(uq, dbi)
                acc = acc + _dot_nn(dbr, bre_ref[q]) + _dot_nn(dbi, bim_ref[q])
            du_ref[:, j * SLAB:(j + 1) * SLAB] = acc

        @pl.when(pl.program_id(0) == nc - 1)
        def _():
            for g in range(N_SSM_GROUPS):
                q, gl = divmod(g, 4)
                rows = slice(gl * SSM_GROUP, (gl + 1) * SSM_GROUP)
                cols = slice(gl * SSM_STATE, (gl + 1) * SSM_STATE)
                for out, acc_ref in ((ob_re, gbr_ref), (ob_im, gbi_ref), (oc_re, gcr_ref), (oc_im, gci_ref)):
                    out[g] = acc_ref[q, rows, cols]

    gshape = jax.ShapeDtypeStruct((N_SSM_GROUPS, SSM_GROUP, SSM_STATE), F32)
    compact = pltpu.VMEM((N_QUAD, 4 * SSM_GROUP, QUAD), F32)
    return _hosted_call(
        body, comm, name="ssm_bwd_rev" if reverse else "ssm_bwd_fwd", grid=(nc,),
        in_specs=[pl.BlockSpec((TC, D_SSM), lambda i: (chunk(i), 0)),
                  pl.BlockSpec((TC, D_SSM), lambda i: (chunk(i), 1)),
                  pl.BlockSpec((TC, N_STATE), lambda i: (chunk(i), 0)),
                  pl.BlockSpec((TC, N_STATE), lambda i: (chunk(i), 0))]
        + [_full(b_re.shape)] * 4 + [_full(f2.shape), _full(tables.shape)],
        out_specs=[pl.BlockSpec((TC, D_SSM), lambda i: (chunk(i), 0))] + [_full(gshape.shape)] * 4
        + [_full((4, N_STATE))],
        out_shape=[jax.ShapeDtypeStruct((L, D_SSM), F32), gshape, gshape, gshape, gshape,
                   jax.ShapeDtypeStruct((4, N_STATE), F32)],
        scratch_shapes=[pltpu.VMEM((TC, N_STATE), F32), pltpu.VMEM((TC, N_STATE), F32),
                        pltpu.VMEM((8, N_STATE), F32), pltpu.VMEM((8, N_STATE), F32),
                        compact, compact, compact, compact],
        args=(dy, u, s_re, s_im, b_re, b_im, c_re, c_im, f2, tables))


def _ssm_post(yf, yb, u, d, glu_w, glu_b):
    y = yf + yb + d * u
    z, t = _gelu(y)
    zb = z.astype(BF16)
    gate = _sigmoid(_dot_nn(zb, glu_w) + glu_b)
    return y, z, t, zb, gate


def _mix_out(yn_pool, yf, yb, u, x, ssm_d, glu_w_b, glu_b, g_ssm, w_out_b, g_ffn):
    L, D = x.shape

    def body(ynp_ref, yf_ref, yb_ref, u_ref, x_ref, d_ref, gw_ref, gb_ref, gs_ref, wo_ref, gf_ref,
             h1_ref, hn_ref, ycat_ref):
        _, z, _, _, gate = _ssm_post(yf_ref[...], yb_ref[...], u_ref[...], d_ref[...], gw_ref[...], gb_ref[...])
        yns, _, _ = _rms_fwd(z * gate, gs_ref[...])
        ynsb = yns.astype(BF16)
        ynp = ynp_ref[...]
        ycat_ref[:, 0:D_POOL] = ynp
        ycat_ref[:, D_POOL:D] = ynsb
        h1 = x_ref[...] + _dot_nn(ynp, wo_ref[0:D_POOL, :]) + _dot_nn(ynsb, wo_ref[D_POOL:D, :])
        h1_ref[...] = h1
        hn, _, _ = _rms_fwd(h1, gf_ref[...])
        hn_ref[...] = hn.astype(BF16)

    half = lambda c: pl.BlockSpec((TL, D_SSM), lambda i: (i, c))
    row = pl.BlockSpec((TL, D), lambda i: (i, 0))
    return pl.pallas_call(
        body, name="mix_out", grid=(L // TL,),
        in_specs=[half(0), half(0), half(0), half(1), row, _full((1, D_SSM)), _full(glu_w_b.shape),
                  _full((1, D_SSM)), _full((1, D_SSM)), _full(w_out_b.shape), _full((1, D))],
        out_specs=[row, row, row],
        out_shape=[jax.ShapeDtypeStruct((L, D), F32), jax.ShapeDtypeStruct((L, D), BF16),
                   jax.ShapeDtypeStruct((L, D), BF16)],
        compiler_params=_cp("parallel"))(yn_pool, yf, yb, u, x, ssm_d, glu_w_b, glu_b, g_ssm, w_out_b, g_ffn)


def _ssm_bwd_local(dh1, yf, yb, u, ssm_d, glu_w_b, glu_b, g_ssm, w_out_b, comm=None):
    L, D = dh1.shape

    def body(dh_ref, yf_ref, yb_ref, u_ref, d_ref, gw_ref, gb_ref, gs_ref, wo_ref,
             dy_ref, du_ref, ggw_ref, ggb_ref, gd_ref, ggs_ref):
        @pl.when(pl.program_id(0) == 0)
        def _():
            for r in (ggw_ref, ggb_ref, gd_ref, ggs_ref):
                r[...] = jnp.zeros_like(r)

        u = u_ref[...]
        d = d_ref[...]
        y, z, t, zb, gate = _ssm_post(yf_ref[...], yb_ref[...], u, d, gw_ref[...], gb_ref[...])
        gs = gs_ref[...]
        _, xh, inv = _rms_fwd(z * gate, gs)
        d_yn = _dot_nt(dh_ref[...].astype(BF16), wo_ref[...])
        d_o, dgs = _rms_bwd(d_yn, xh, inv, gs)
        ggs_ref[...] += dgs
        d_zg = d_o * z * gate * (1.0 - gate)
        d_zgb = d_zg.astype(BF16)
        ggb_ref[...] += jnp.sum(d_zg, axis=0, keepdims=True)
        ggw_ref[...] += _dot_tn(zb, d_zgb)
        d_z = d_o * gate + _dot_nt(d_zgb, gw_ref[...])
        d_y = d_z * _gelu_grad(y, t)
        gd_ref[...] += jnp.sum(d_y * u, axis=0, keepdims=True)
        dy_ref[...] = d_y
        du_ref[...] = d_y * d

    half = lambda c: pl.BlockSpec((TL, D_SSM), lambda i: (i, c))
    vec = _full((1, D_SSM))
    return _hosted_call(
        body, comm, name="ssm_bwd_local", grid=(L // TL,),
        in_specs=[pl.BlockSpec((TL, D), lambda i: (i, 0)), half(0), half(0), half(1), vec, _full(glu_w_b.shape),
                  vec, vec, pl.BlockSpec((D_SSM, D), lambda i: (1, 0))],
        out_specs=[half(0), half(0), _full(glu_w_b.shape), vec, vec, vec],
        out_shape=[jax.ShapeDtypeStruct((L, D_SSM), F32), jax.ShapeDtypeStruct((L, D_SSM), F32),
                   jax.ShapeDtypeStruct(glu_w_b.shape, F32)] + [jax.ShapeDtypeStruct((1, D_SSM), F32)] * 3,
        scratch_shapes=[], args=(dh1, yf, yb, u, ssm_d, glu_w_b, glu_b, g_ssm, w_out_b))


def _in_bwd(du_pool, du_a, du_b, du_c, dh1, x, g, w_in_b):
    L, D = x.shape

    def body(p_ref, a_ref, b_ref, c_ref, dh_ref, x_ref, g_ref, w_ref, dx_ref, dub_ref, gg_ref):
        @pl.when(pl.program_id(0) == 0)
        def _():
            gg_ref[...] = jnp.zeros_like(gg_ref)

        dub_ref[:, 0:D_POOL] = p_ref[...].astype(BF16)
        dub_ref[:, D_POOL:D] = (a_ref[...] + b_ref[...] + c_ref[...]).astype(BF16)
        d_xn = _dot_nt(dub_ref[...], w_ref[...])
        gv = g_ref[...]
        _, xh, inv = _rms_fwd(x_ref[...], gv)
        dx, dg = _rms_bwd(d_xn, xh, inv, gv)
        gg_ref[...] += dg
        dx_ref[...] = dh_ref[...] + dx

    half = pl.BlockSpec((TL, D_SSM), lambda i: (i, 0))
    row = pl.BlockSpec((TL, D), lambda i: (i, 0))
    return pl.pallas_call(
        body, name="in_bwd", grid=(L // TL,),
        in_specs=[half, half, half, half, row, row, _full((1, D)), _full(w_in_b.shape)],
        out_specs=[row, row, _full((1, D))],
        out_shape=[jax.ShapeDtypeStruct((L, D), F32), jax.ShapeDtypeStruct((L, D), BF16),
                   jax.ShapeDtypeStruct((1, D), F32)],
        compiler_params=_cp("arbitrary"))(du_pool, du_a, du_b, du_c, dh1, x, g, w_in_b)


def _ffn_up(hn, w_up4):
    L, D = hn.shape

    def body(h_ref, w_ref, o_ref):
        o_ref[...] = _dot_nn(h_ref[...], w_ref[...]).astype(BF16)

    return pl.pallas_call(
        body, name="ffn_up", grid=(4, L // TL),
        in_specs=[pl.BlockSpec((TL, D), lambda j, i: (i, 0)), pl.BlockSpec((None, D, FF_BLK), lambda j, i: (j, 0, 0))],
        out_specs=pl.BlockSpec((TL, FF_BLK), lambda j, i: (i, j)),
        out_shape=jax.ShapeDtypeStruct((L, 4 * FF_BLK), BF16),
        compiler_params=_cp("parallel", "parallel"))(hn, w_up4)


def _halo_specs_2d(rows, width, L, col, order):
    rb = rows // HALO_B
    last = L // HALO_B - 1
    if order == "ik":
        wrap = lambda f: (lambda i, k: f(i, k))
    else:
        wrap = lambda f: (lambda k, i: f(i, k))
    return [pl.BlockSpec((HALO_B, width), wrap(lambda i, k: (jnp.maximum(i * rb - 1, 0), col(k)))),
            pl.BlockSpec((rows, width), wrap(lambda i, k: (i, col(k)))),
            pl.BlockSpec((HALO_B, width), wrap(lambda i, k: (jnp.minimum((i + 1) * rb, last), col(k))))]


def _shift_mats(rows):
    r = lax.broadcasted_iota(jnp.int32, (rows, rows), 0)
    c = lax.broadcasted_iota(jnp.int32, (rows, rows), 1)
    return (c == r - 1).astype(BF16), (c == r + 1).astype(BF16)


def _neighbours(x, prev_ref, next_ref, cs, i, n, mats):
    rows = x.shape[0]
    row = lax.broadcasted_iota(jnp.int32, (rows, 1), 0)
    before = jnp.where(i > 0, prev_ref[:, cs].astype(F32)[HALO_B - 1:HALO_B, :], 0.0)
    after = jnp.where(i < n - 1, next_ref[:, cs].astype(F32)[0:1, :], 0.0)
    if mats is None:
        xf = x.astype(F32)
        down, up = pltpu.roll(xf, 1, 0), pltpu.roll(xf, rows - 1, 0)
    else:
        down, up = _dot_nn(mats[0], x), _dot_nn(mats[1], x)
    return jnp.where(row == 0, before, down), jnp.where(row == rows - 1, after, up)


def _conv3(x, before, after, w, b):
    return before * w[0:1, :] + x.astype(F32) * w[1:2, :] + after * w[2:3, :] + b


def _col_chunks(width, size=256):
    return [slice(c, min(c + size, width)) for c in range(0, width, size)]


def _ffn_down_loss(up, conv_w, conv_b, w_down_b, h1, target, g_final):
    L, D = h1.shape
    n = L // TF
    nk = D_FF // FF_BLK

    def body(vp, vc, vn, gp, gc, gn, wv_ref, wg_ref, bv_ref, bg_ref, wd_ref, h1_ref, t_ref, gf_ref,
             a_ref, cv_ref, cg_ref, dh2_ref, dh2b_ref, loss_ref, gg_ref, acc_ref):
        i = pl.program_id(0)
        k = pl.program_id(1)

        @pl.when((i == 0) & (k == 0))
        def _():
            loss_ref[...] = jnp.zeros_like(loss_ref)
            gg_ref[...] = jnp.zeros_like(gg_ref)

        @pl.when(k == 0)
        def _():
            acc_ref[...] = jnp.zeros_like(acc_ref)

        mats = _shift_mats(TF)
        for cs in _col_chunks(FF_BLK):
            xv, xg = vc[:, cs], gc[:, cs]
            val = _conv3(xv, *_neighbours(xv, vp, vn, cs, i, n, mats), wv_ref[:, cs], bv_ref[:, cs])
            gate = _conv3(xg, *_neighbours(xg, gp, gn, cs, i, n, mats), wg_ref[:, cs], bg_ref[:, cs])
            a_ref[:, cs] = (val * (gate * _sigmoid(gate))).astype(BF16)
            cv_ref[:, cs] = val.astype(BF16)
            cg_ref[:, cs] = gate.astype(BF16)
        acc_ref[...] += _dot_nn(a_ref[...], wd_ref[pl.ds(pl.multiple_of(k * FF_BLK, LANES), FF_BLK), :])

        @pl.when(k == nk - 1)
        def _():
            gf = gf_ref[...]
            y, xh, inv = _rms_fwd(h1_ref[...] + acc_ref[...], gf)
            diff = y - t_ref[...]
            part = 0.5 * jnp.sum(jnp.mean(diff * diff, axis=-1, keepdims=True), axis=0, keepdims=True)
            loss_ref[...] += jnp.broadcast_to(part, loss_ref.shape)
            dx, dg = _rms_bwd(diff * (1.0 / D), xh, inv, gf)
            gg_ref[...] += dg
            dh2_ref[...] = dx
            dh2b_ref[...] = dx.astype(BF16)

    row = pl.BlockSpec((TF, D), lambda i, k: (i, 0))
    cw = lambda off: pl.BlockSpec((3, FF_BLK), lambda i, k: (0, k + off))
    cb = lambda off: pl.BlockSpec((1, FF_BLK), lambda i, k: (0, k + off))
    return pl.pallas_call(
        body, name="ffn_down_loss", grid=(n, nk),
        in_specs=_halo_specs_2d(TF, FF_BLK, L, lambda k: k, "ik") + _halo_specs_2d(TF, FF_BLK, L, lambda k: k + nk, "ik")
        + [cw(0), cw(nk), cb(0), cb(nk), _full(w_down_b.shape), row, row, _full((1, D))],
        out_specs=[pl.BlockSpec((TF, FF_BLK), lambda i, k: (i, k))] * 3 + [row, row, _full((1, LANES)), _full((1, D))],
        out_shape=[jax.ShapeDtypeStruct((L, D_FF), BF16)] * 3
        + [jax.ShapeDtypeStruct((L, D), F32), jax.ShapeDtypeStruct((L, D), BF16),
           jax.ShapeDtypeStruct((1, LANES), F32), jax.ShapeDtypeStruct((1, D), F32)],
        scratch_shapes=[pltpu.VMEM((TF, D), F32)],
        compiler_params=_cp("arbitrary", "arbitrary"))(
            up, up, up, up, up, up, conv_w, conv_w, conv_b, conv_b, w_down_b, h1, target, g_final)


def _ffn_act_bwd(c_val, c_gate, w_down_b, dh2):
    L, D = dh2.shape
    n = L // TL
    nk = D_FF // FF_BLK

    def body(v_ref, g_ref, wd_ref, dh_ref, dv_ref, dg_ref, gbv_ref, gbg_ref):
        @pl.when(pl.program_id(1) == 0)
        def _():
            gbv_ref[...] = jnp.zeros_like(gbv_ref)
            gbg_ref[...] = jnp.zeros_like(gbg_ref)

        dh = dh_ref[...]
        for cs in _col_chunks(FF_BLK):
            val, gate = v_ref[:, cs].astype(F32), g_ref[:, cs].astype(F32)
            d_a = _dot_nt(dh, wd_ref[cs, :])
            sg = _sigmoid(gate)
            d_val = d_a * (gate * sg)
            d_gate = d_a * val * (sg * (1.0 + gate * (1.0 - sg)))
            dv_ref[:, cs] = d_val.astype(BF16)
            dg_ref[:, cs] = d_gate.astype(BF16)
            gbv_ref[:, cs] += jnp.sum(d_val, axis=0, keepdims=True)
            gbg_ref[:, cs] += jnp.sum(d_gate, axis=0, keepdims=True)

    blk = pl.BlockSpec((TL, FF_BLK), lambda k, i: (i, k))
    acc = pl.BlockSpec((1, FF_BLK), lambda k, i: (0, k))
    return pl.pallas_call(
        body, name="ffn_act_bwd", grid=(nk, n),
        in_specs=[blk, blk, pl.BlockSpec((FF_BLK, D), lambda k, i: (k, 0)), pl.BlockSpec((TL, D), lambda k, i: (i, 0))],
        out_specs=[blk, blk, acc, acc],
        out_shape=[jax.ShapeDtypeStruct((L, D_FF), BF16), jax.ShapeDtypeStruct((L, D_FF), BF16),
                   jax.ShapeDtypeStruct((1, D_FF), F32), jax.ShapeDtypeStruct((1, D_FF), F32)],
        compiler_params=_cp("arbitrary", "arbitrary"))(c_val, c_gate, w_down_b, dh2)


def _ffn_up_bwd(d_val, d_gate, up, conv_w, w_up4, h1, dh2, g_ffn):
    L, D = h1.shape
    n = L // TF
    nk = D_FF // FF_BLK

    def body(vp, vc, vn, gp, gc, gn, uv_ref, ug_ref, wv_ref, wg_ref, wu_ref, h1_ref, dh2_ref, g_ref,
             dup_ref, dh1_ref, dh1b_ref, gg_ref, gcw_ref, acc_ref):
        i = pl.program_id(0)
        k = pl.program_id(1)

        @pl.when((i == 0) & (k == 0))
        def _():
            gg_ref[...] = jnp.zeros_like(gg_ref)
            gcw_ref[...] = jnp.zeros_like(gcw_ref)

        @pl.when(k == 0)
        def _():
            acc_ref[...] = jnp.zeros_like(acc_ref)

        acc = jnp.zeros((TF, D), F32)
        for j, (blocks, u_ref, w_ref) in enumerate((((vp, vc, vn), uv_ref, wv_ref), ((gp, gc, gn), ug_ref, wg_ref))):
            for cs in _col_chunks(FF_BLK):
                d = blocks[1][:, cs]
                before, after = _neighbours(d, blocks[0], blocks[2], cs, i, n, None)
                taps = (after, d.astype(F32), before)
                w = w_ref[:, cs]
                d_up = (taps[0] * w[0:1, :] + taps[1] * w[1:2, :] + taps[2] * w[2:3, :]).astype(BF16)
                dup_ref[j, :, cs] = d_up
                acc = acc + _dot_nt(d_up, wu_ref[k + j * nk, :, cs])
                x = u_ref[:, cs].astype(F32)
                for r in range(3):
                    gcw_ref[j, k, r:r + 1, cs] += jnp.sum(taps[r] * x, axis=0, keepdims=True)
        acc_ref[...] += acc

        @pl.when(k == nk - 1)
        def _():
            g = g_ref[...]
            _, xh, inv = _rms_fwd(h1_ref[...], g)
            dx, dg = _rms_bwd(acc_ref[...], xh, inv, g)
            gg_ref[...] += dg
            dh1 = dh2_ref[...] + dx
            dh1_ref[...] = dh1
            dh1b_ref[...] = dh1.astype(BF16)

    row = pl.BlockSpec((TF, D), lambda i, k: (i, 0))
    cw = lambda off: pl.BlockSpec((3, FF_BLK), lambda i, k: (0, k + off))
    tile = lambda off: pl.BlockSpec((TF, FF_BLK), lambda i, k: (i, k + off))
    return pl.pallas_call(
        body, name="ffn_up_bwd", grid=(n, nk),
        in_specs=_halo_specs_2d(TF, FF_BLK, L, lambda k: k, "ik") + _halo_specs_2d(TF, FF_BLK, L, lambda k: k, "ik")
        + [tile(0), tile(nk), cw(0), cw(nk), _full(w_up4.shape), row, row, _full((1, D))],
        out_specs=[pl.BlockSpec((2, None, TF, FF_BLK), lambda i, k: (0, k, i, 0)), row, row, _full((1, D)),
                   _full((2, nk, 3, FF_BLK))],
        out_shape=[jax.ShapeDtypeStruct((2, nk, L, FF_BLK), BF16), jax.ShapeDtypeStruct((L, D), F32),
                   jax.ShapeDtypeStruct((L, D), BF16), jax.ShapeDtypeStruct((1, D), F32),
                   jax.ShapeDtypeStruct((2, nk, 3, FF_BLK), F32)],
        scratch_shapes=[pltpu.VMEM((TF, D), F32)],
        compiler_params=_cp("arbitrary", "arbitrary"))(
            d_val, d_val, d_val, d_gate, d_gate, d_gate, up, up, conv_w, conv_w, w_up4, h1, dh2, g_ffn)


def _matmul_tn(a, b, tm, tn, name, tk=2048):
    L, M = a.shape
    N = b.shape[1]
    tk = min(tk, L)

    def body(a_ref, b_ref, o_ref):
        @pl.when(pl.program_id(2) == 0)
        def _():
            o_ref[...] = jnp.zeros_like(o_ref)

        o_ref[...] += _dot_tn(a_ref[...], b_ref[...])

    return pl.pallas_call(
        body, name=name, grid=(M // tm, N // tn, L // tk),
        in_specs=[pl.BlockSpec((tk, tm), lambda m, n, l: (l, m)), pl.BlockSpec((tk, tn), lambda m, n, l: (l, n))],
        out_specs=pl.BlockSpec((tm, tn), lambda m, n, l: (m, n)),
        out_shape=jax.ShapeDtypeStruct((M, N), F32),
        compiler_params=_cp("parallel", "parallel", "arbitrary"))(a, b)


def _matmul_tn_blocks(a, b, tm, name, tk=2048):
    L, M = a.shape
    J, _, N = b.shape
    tk = min(tk, L)

    def body(a_ref, b_ref, o_ref):
        @pl.when(pl.program_id(2) == 0)
        def _():
            o_ref[...] = jnp.zeros_like(o_ref)

        o_ref[...] += _dot_tn(a_ref[...], b_ref[...])

    return pl.pallas_call(
        body, name=name, grid=(M // tm, J, L // tk),
        in_specs=[pl.BlockSpec((tk, tm), lambda m, j, l: (l, m)), pl.BlockSpec((None, tk, N), lambda m, j, l: (j, l, 0))],
        out_specs=pl.BlockSpec((None, tm, N), lambda m, j, l: (j, m, 0)),
        out_shape=jax.ShapeDtypeStruct((J, M, N), F32),
        compiler_params=_cp("parallel", "parallel", "arbitrary"))(a, b)


def _row_tile(rows):
    for t in (512, 352, 256, 128, 64, 8):
        if rows % t == 0:
            return t
    return rows


def _add_half(g, r, c_arr, name, out_dtype=F32):
    _, _, R, C = g.shape
    tr = _row_tile(R)

    def body(c_ref, g_ref, r_ref, o_ref):
        o_ref[...] = (g_ref[...] + r_ref[...]).astype(out_dtype)

    return pl.pallas_call(
        body, name=name,
        grid_spec=pltpu.PrefetchScalarGridSpec(
            num_scalar_prefetch=1, grid=(g.shape[0], R // tr),
            in_specs=[pl.BlockSpec((None, None, tr, C), lambda j, i, c: (j, c[0], i, 0)),
                      pl.BlockSpec((None, tr, C), lambda j, i, c: (j, i, 0))],
            out_specs=pl.BlockSpec((None, tr, C), lambda j, i, c: (j, i, 0))),
        out_shape=jax.ShapeDtypeStruct(r.shape, out_dtype),
        compiler_params=_cp("parallel", "parallel"))(c_arr, g, r)


def _add2(a, b, name):
    R, C = a.shape
    tr = _row_tile(R)

    def body(a_ref, b_ref, o_ref):
        o_ref[...] = a_ref[...] + b_ref[...]

    spec = pl.BlockSpec((tr, C), lambda i: (i, 0))
    return pl.pallas_call(body, name=name, grid=(R // tr,), in_specs=[spec, spec], out_specs=spec,
                          out_shape=jax.ShapeDtypeStruct(a.shape, F32), compiler_params=_cp("parallel"))(a, b)


def _sum4(p, name):
    _, R, C = p.shape
    tr = _row_tile(R)

    def body(p_ref, o_ref):
        q = [p_ref[j].astype(F32) for j in range(4)]
        o_ref[...] = ((q[0] + q[1]) + q[2]) + q[3]

    return pl.pallas_call(
        body, name=name, grid=(R // tr,),
        in_specs=[pl.BlockSpec((4, tr, C), lambda i: (0, i, 0))],
        out_specs=pl.BlockSpec((tr, C), lambda i: (i, 0)),
        out_shape=jax.ShapeDtypeStruct((R, C), F32), compiler_params=_cp("parallel"))(p)


def _adamw_refs(w_ref, g_ref, m_ref, v_ref, d_ref, nm_ref, nv_ref):
    gv = g_ref[...]
    nm = ADAM_B1 * m_ref[...] + (1.0 - ADAM_B1) * gv
    nv = ADAM_B2 * v_ref[...] + (1.0 - ADAM_B2) * (gv * gv)
    m_hat = nm / (1.0 - ADAM_B1 ** ADAM_STEP)
    v_hat = nv / (1.0 - ADAM_B2 ** ADAM_STEP)
    d_ref[...] = -ADAM_LR * (m_hat / (jnp.sqrt(v_hat) + ADAM_EPS) + ADAM_WD * w_ref[...])
    nm_ref[...] = nm
    nv_ref[...] = nv


def _adamw_many(ws, gs, ms, vs, name):
    n = len(ws)

    def body(*refs):
        for k in range(n):
            _adamw_refs(*(refs[j * n + k] for j in range(7)))

    out_shape = [jax.ShapeDtypeStruct(w.shape, F32) for w in ws] * 3
    res = pl.pallas_call(body, name=name, out_shape=out_shape,
                         compiler_params=pltpu.CompilerParams(vmem_limit_bytes=VMEM_LIMIT))(*ws, *gs, *ms, *vs)
    return res[:n], res[n:2 * n], res[2 * n:]


def _adamw(w, g, m, v, name):
    R, C = w.shape
    tr = _row_tile(R)
    body = lambda *refs: _adamw_refs(*refs)

    spec = pl.BlockSpec((tr, C), lambda i: (i, 0))
    sh = jax.ShapeDtypeStruct((R, C), F32)
    return pl.pallas_call(body, name=name, grid=(R // tr,), in_specs=[spec] * 4, out_specs=[spec] * 3,
                          out_shape=[sh] * 3, compiler_params=_cp("parallel"))(w, g, m, v)


def _join_rows(own, other, c_arr, name):
    R, C = own.shape
    tr = _row_tile(R)

    def body(c_ref, own_ref, other_ref, o_ref):
        o_ref[...] = jnp.where(pl.program_id(0) == c_ref[0], own_ref[...], other_ref[...])

    half = pl.BlockSpec((tr, C), lambda h, i, c: (i, 0))
    return pl.pallas_call(
        body, name=name,
        grid_spec=pltpu.PrefetchScalarGridSpec(
            num_scalar_prefetch=1, grid=(2, R // tr), in_specs=[half, half],
            out_specs=pl.BlockSpec((tr, C), lambda h, i, c: (h * (R // tr) + i, 0))),
        out_shape=jax.ShapeDtypeStruct((2 * R, C), F32),
        compiler_params=_cp("parallel", "parallel"))(c_arr, own, other)


def _adamw_halves(w, own, other, m, v, c_arr, name):
    R, C = own.shape
    tr = _row_tile(R)
    while tr * C * 4 > 2 ** 20 and tr % 16 == 0:
        tr //= 2

    def body(c_ref, w_ref, own_ref, other_ref, m_ref, v_ref, g_ref, d_ref, nm_ref, nv_ref):
        g_ref[...] = jnp.where(pl.program_id(0) == c_ref[0], own_ref[...], other_ref[...])
        _adamw_refs(w_ref, g_ref, m_ref, v_ref, d_ref, nm_ref, nv_ref)

    half = pl.BlockSpec((tr, C), lambda h, i, c: (i, 0))
    full = pl.BlockSpec((tr, C), lambda h, i, c: (h * (R // tr) + i, 0))
    sh = jax.ShapeDtypeStruct((2 * R, C), F32)
    return pl.pallas_call(
        body, name=name,
        grid_spec=pltpu.PrefetchScalarGridSpec(
            num_scalar_prefetch=1, grid=(2, R // tr), in_specs=[full, half, half, full, full], out_specs=[full] * 4),
        out_shape=[sh] * 4, compiler_params=_cp("parallel", "parallel"))(c_arr, w, own, other, m, v)


_ANY = pl.BlockSpec(memory_space=pl.ANY)


def _position():
    return lax.axis_index("x"), lax.axis_index("y"), lax.axis_index("c")


class _Comm:
    def __init__(self, arrs, out_shape, sems, start, finish):
        self.arrs, self.out_shape, self.sems, self.start, self.finish = arrs, out_shape, sems, start, finish


def _comm_call(comm, name):
    n, m = len(comm.arrs), len(comm.out_shape)

    def body(*refs):
        ins, outs, sems = refs[:n], refs[n:n + m], refs[n + m:]
        comm.start(ins, outs, sems)
        comm.finish(ins, outs, sems)

    return pl.pallas_call(
        body, name=name, in_specs=[_ANY] * n, out_specs=[_ANY] * m, out_shape=comm.out_shape,
        scratch_shapes=comm.sems, compiler_params=pltpu.CompilerParams(has_side_effects=True))(*comm.arrs)


def _hosted_call(body, comm, *, name, grid, in_specs, out_specs, out_shape, scratch_shapes, args):
    sem = ("arbitrary",) * len(grid)
    if comm is None:
        return pl.pallas_call(body, name=name, grid=grid, in_specs=in_specs, out_specs=out_specs, out_shape=out_shape,
                              scratch_shapes=scratch_shapes, compiler_params=_cp(*sem))(*args), []
    n_in, n_out, n_scr = len(in_specs), len(out_specs), len(scratch_shapes)
    ci, co = len(comm.arrs), len(comm.out_shape)

    def full(*refs):
        ins, refs = refs[:n_in], refs[n_in:]
        cins, refs = refs[:ci], refs[ci:]
        outs, refs = refs[:n_out], refs[n_out:]
        couts, refs = refs[:co], refs[co:]
        scr, csems = refs[:n_scr], refs[n_scr:]
        first, last = True, True
        for d, size in enumerate(grid):
            first = first & (pl.program_id(d) == 0)
            last = last & (pl.program_id(d) == size - 1)

        @pl.when(first)
        def _():
            comm.start(cins, couts, csems)

        body(*ins, *outs, *scr)

        @pl.when(last)
        def _():
            comm.finish(cins, couts, csems)

    res = pl.pallas_call(
        full, name=name, grid=grid, in_specs=list(in_specs) + [_ANY] * ci, out_specs=list(out_specs) + [_ANY] * co,
        out_shape=list(out_shape) + list(comm.out_shape), scratch_shapes=list(scratch_shapes) + list(comm.sems),
        compiler_params=_cp(*sem))(*args, *comm.arrs)
    return res[:n_out], res[n_out:]


def _comm_join(*comms):
    def parts(xs, attr):
        out, at = [], 0
        for cm in comms:
            n = len(getattr(cm, attr))
            out.append(xs[at:at + n])
            at += n
        return out

    def start(ins, outs, sems):
        for cm, i, o, s in zip(comms, parts(ins, "arrs"), parts(outs, "out_shape"), parts(sems, "sems")):
            cm.start(i, o, s)

    def finish(ins, outs, sems):
        for cm, i, o, s in zip(comms, parts(ins, "arrs"), parts(outs, "out_shape"), parts(sems, "sems")):
            cm.finish(i, o, s)

    cat = lambda attr: [x for cm in comms for x in getattr(cm, attr)]
    return _Comm(cat("arrs"), cat("out_shape"), cat("sems"), start, finish)


def _dma_sems(*counts):
    return [pltpu.SemaphoreType.DMA((n,)) for n in counts]


def _comm_pair_swap(arrs, half=False):
    n = len(arrs)
    out_shape = [jax.ShapeDtypeStruct(a.shape[:1] + a.shape[2:] if half else a.shape, a.dtype) for a in arrs]

    def copies(ins, outs, sems):
        x, y, c = _position()
        return [pltpu.make_async_remote_copy(
            src_ref=ins[k].at[:, 1 - c] if half else ins[k], dst_ref=outs[k], send_sem=sems[0].at[k],
            recv_sem=sems[1].at[k], device_id=(x, y, 1 - c), device_id_type=MESH) for k in range(n)]

    def start(ins, outs, sems):
        for cp in copies(ins, outs, sems):
            cp.start()

    def finish(ins, outs, sems):
        for cp in copies(ins, outs, sems):
            cp.wait()

    return _Comm(arrs, out_shape, _dma_sems(n, n), start, finish)


def _chip_of(j, c):
    return (jnp.right_shift(j, 1), jnp.bitwise_and(j, 1), c)


def _comm_chip_exchange(arrs, scatter):
    n = len(arrs)
    out_shape = [jax.ShapeDtypeStruct(a.shape if scatter else (4,) + a.shape, a.dtype) for a in arrs]

    def copies(ins, outs, sems):
        x, y, c = _position()
        me = 2 * x + y
        local, sent, landed = [], [], []
        for k in range(n):
            local.append(pltpu.make_async_copy(ins[k].at[me] if scatter else ins[k], outs[k].at[me], sems[2].at[k]))
            for d in (1, 2, 3):
                j = jnp.bitwise_xor(me, d)
                s = 3 * k + d - 1
                src = ins[k].at[j] if scatter else ins[k]
                for dst, group in ((outs[k].at[me], sent), (outs[k].at[j], landed)):
                    group.append(pltpu.make_async_remote_copy(
                        src_ref=src, dst_ref=dst, send_sem=sems[0].at[s], recv_sem=sems[1].at[s],
                        device_id=_chip_of(j, c), device_id_type=MESH))
        return local, sent, landed

    def start(ins, outs, sems):
        local, sent, _ = copies(ins, outs, sems)
        for cp in local + sent:
            cp.start()

    def finish(ins, outs, sems):
        local, sent, landed = copies(ins, outs, sems)
        for cp in sent:
            cp.wait_send()
        for cp in landed:
            cp.wait_recv()
        for cp in local:
            cp.wait()

    return _Comm(arrs, out_shape, _dma_sems(3 * n, 3 * n, n), start, finish)


def _comm_pair_gather(arrs):
    n = len(arrs)
    out_shape = [jax.ShapeDtypeStruct((2,) + a.shape, a.dtype) for a in arrs]

    def copies(ins, outs, sems):
        x, y, c = _position()
        local, sent, landed = [], [], []
        for k in range(n):
            local.append(pltpu.make_async_copy(ins[k], outs[k].at[c], sems[2].at[k]))
            for dst, group in ((outs[k].at[c], sent), (outs[k].at[1 - c], landed)):
                group.append(pltpu.make_async_remote_copy(
                    src_ref=ins[k], dst_ref=dst, send_sem=sems[0].at[k], recv_sem=sems[1].at[k],
                    device_id=(x, y, 1 - c), device_id_type=MESH))
        return local, sent, landed

    def start(ins, outs, sems):
        local, sent, _ = copies(ins, outs, sems)
        for cp in local + sent:
            cp.start()

    def finish(ins, outs, sems):
        local, sent, landed = copies(ins, outs, sems)
        for cp in sent:
            cp.wait_send()
        for cp in landed:
            cp.wait_recv()
        for cp in local:
            cp.wait()

    return _Comm(arrs, out_shape, _dma_sems(n, n, n), start, finish)


LOCAL_PARTS = 4


def _comm_gather_split(shards, whole):
    n, nw = len(shards), len(whole)
    arrs = list(shards) + list(whole)
    out_shape = [jax.ShapeDtypeStruct((4,) + a.shape, a.dtype) for a in arrs]

    def copies(ins, outs, sems):
        x, y, c = _position()
        me = 2 * x + y
        local, sent, landed, passed, passed_in = [], [], [], [], []
        for k in range(n + nw):
            if k >= n:
                local.append(pltpu.make_async_copy(ins[k], outs[k].at[me], sems[4].at[LOCAL_PARTS * k]))
            else:
                part = shards[k].shape[0] // LOCAL_PARTS
                for r in range(LOCAL_PARTS):
                    local.append(pltpu.make_async_copy(ins[k].at[pl.ds(r * part, part)],
                                                       outs[k].at[me, pl.ds(r * part, part)],
                                                       sems[4].at[LOCAL_PARTS * k + r]))
            for d in (1, 2, 3):
                j = jnp.bitwise_xor(me, d)
                s = 3 * k + d - 1
                if k >= n:
                    src, mine, theirs = ins[k], outs[k].at[me], outs[k].at[j]
                else:
                    h = shards[k].shape[0] // 2
                    rows = pl.ds(pl.multiple_of(c * h, 16), h)
                    other = pl.ds(pl.multiple_of((1 - c) * h, 16), h)
                    src, mine, theirs = ins[k].at[rows], outs[k].at[me, rows], outs[k].at[j, rows]
                    for dst, group in ((theirs, passed), (outs[k].at[j, other], passed_in)):
                        group.append(pltpu.make_async_remote_copy(
                            src_ref=theirs, dst_ref=dst, send_sem=sems[2].at[s], recv_sem=sems[3].at[s],
                            device_id=(x, y, 1 - c), device_id_type=MESH))
                for dst, group in ((mine, sent), (theirs, landed)):
                    group.append(pltpu.make_async_remote_copy(
                        src_ref=src, dst_ref=dst, send_sem=sems[0].at[s], recv_sem=sems[1].at[s],
                        device_id=_chip_of(j, c), device_id_type=MESH))
        return local, sent, landed, passed, passed_in

    def start(ins, outs, sems):
        local, sent, _, _, _ = copies(ins, outs, sems)
        for cp in local + sent:
            cp.start()

    def finish(ins, outs, sems):
        local, sent, landed, passed, passed_in = copies(ins, outs, sems)
        for cp in landed[:3 * n]:
            cp.wait_recv()
        for cp in passed:
            cp.start()
        for cp in landed[3 * n:]:
            cp.wait_recv()
        for cp in sent:
            cp.wait_send()
        for cp in passed:
            cp.wait_send()
        for cp in passed_in:
            cp.wait_recv()
        for cp in local:
            cp.wait()

    t = 3 * (n + nw)
    return _Comm(arrs, out_shape, _dma_sems(t, t, max(3 * n, 1), max(3 * n, 1), LOCAL_PARTS * (n + nw)), start, finish)


def _pack(arrs, row_multiple):
    parts = []
    for a in arrs:
        flat = a.reshape(-1).astype(F32)
        pad = (-flat.shape[0]) % LANES
        parts.append(jnp.pad(flat, (0, pad)) if pad else flat)
    flat = jnp.concatenate(parts)
    rows = -(-flat.shape[0] // LANES)
    rows_p = -(-rows // row_multiple) * row_multiple
    return jnp.pad(flat, (0, rows_p * LANES - flat.shape[0])).reshape(rows_p, LANES)


def _unpack(packed, shapes):
    flat = packed.reshape(-1)
    outs, off = [], 0
    for sh in shapes:
        size = int(np.prod(sh))
        outs.append(flat[off:off + size].reshape(sh))
        off += size + (-size) % LANES
    return outs


SMALL = ["norm_mix_g", "pool_w", "pool_scale", "ssm_log_neg_a_re", "ssm_a_im", "ssm_log_dt", "ssm_b_re", "ssm_b_im",
         "ssm_c_re", "ssm_c_im", "ssm_d", "glu_b", "out_norm_pool_g", "out_norm_ssm_g", "norm_ffn_g", "conv_b",
         "final_norm_g"]
BIG = ["w_in", "glu_w", "w_out", "w_up", "w_down"]
WIDE = ["pool_w", "ssm_b_re", "ssm_b_im", "ssm_c_re", "ssm_c_im"]
WEIGHTS = ['norm_mix_g', 'w_in', 'pool_w', 'pool_scale', 'ssm_log_neg_a_re', 'ssm_a_im', 'ssm_log_dt', 'ssm_b_re',
           'ssm_b_im', 'ssm_c_re', 'ssm_c_im', 'ssm_d', 'glu_w', 'glu_b', 'out_norm_pool_g', 'out_norm_ssm_g', 'w_out',
           'norm_ffn_g', 'w_up', 'conv_w', 'conv_b', 'w_down', 'final_norm_g']


def _local_step(x, target, p, full, shards=None, c_arr=None):
    L, D = x.shape
    dist = shards is not None
    row = lambda a: a.reshape(1, -1)
    w_in = full["w_in"]
    pool_w_b = p["pool_w"].astype(BF16)
    g_mix, g_pool, g_ssm, g_ffn, g_fin = (row(p[k]) for k in (
        "norm_mix_g", "out_norm_pool_g", "out_norm_ssm_g", "norm_ffn_g", "final_norm_g"))
    pool_scale, ssm_d, glu_b, conv_b = (row(p[k]) for k in ("pool_scale", "ssm_d", "glu_b", "conv_b"))

    lnar = p["ssm_log_neg_a_re"].reshape(2 * N_SSM_GROUPS, SSM_STATE)
    aim = p["ssm_a_im"].reshape(2 * N_SSM_GROUPS, SSM_STATE)
    ldt = jnp.broadcast_to(p["ssm_log_dt"].reshape(2 * N_SSM_GROUPS, 1), lnar.shape)
    lam_re, lam_im, f_re, f_im = _ssm_params(lnar, aim, ldt)
    flat2 = lambda a: a.reshape(2, N_STATE)
    lam4 = jnp.stack([flat2(lam_re)[0], flat2(lam_im)[0], flat2(lam_re)[1], flat2(lam_im)[1]])
    tables = _scan_tables(lam4)
    f2 = [jnp.stack([flat2(f_re)[d], flat2(f_im)[d]]) for d in range(2)]
    dense = _ssm_expand(p["ssm_b_re"], p["ssm_b_im"], p["ssm_c_re"], p["ssm_c_im"])
    ssm_args = [tuple(dense[4 * d:4 * d + 4]) + (f2[d], tables) for d in range(2)]

    u, xn = _in_proj(x, g_mix, w_in)
    yn_pool = _pool_fwd(u, pool_w_b, pool_scale, g_pool)
    gather1 = _comm_gather_split([shards[k] for k in ("glu_w", "w_out", "w_down")], [shards["conv_w"]]) if dist else None
    (y0, s0r, s0i), got1 = _ssm_scan_fwd(u, *ssm_args[0], 0, False, comm=gather1)
    gather2 = _comm_gather_split([shards["w_up"]], []) if dist else None
    (y1, s1r, s1i), got2 = _ssm_scan_fwd(u, *ssm_args[1], 2, True, comm=gather2)
    if dist:
        glu_w, w_out, w_down = (g.reshape((-1,) + g.shape[2:]) for g in got1[:3])
        conv_w = jnp.transpose(got1[3], (1, 0, 2)).reshape(3, -1)
        w_up4 = got2[0]
    else:
        glu_w, w_out, w_up4, w_down, conv_w = (full[k] for k in ("glu_w", "w_out", "w_up", "w_down", "conv_w"))
    h1, hn, ycat = _mix_out(yn_pool, y0, y1, u, x, ssm_d, glu_w, glu_b, g_ssm, w_out, g_ffn)
    up = _ffn_up(hn, w_up4)
    a, c_val, c_gate, dh2, dh2_b, loss, g_final = _ffn_down_loss(up, conv_w, conv_b, w_down, h1, target, g_fin)

    d_val, d_gate, gbv, gbg = _ffn_act_bwd(c_val, c_gate, w_down, dh2_b)
    g_w_down = _matmul_tn(a, dh2_b, FF_BLK, D, "grad_w_down")
    d_up, dh1, dh1_b, g_ffn_g, gcw = _ffn_up_bwd(d_val, d_gate, up, conv_w, w_up4, h1, dh2, g_ffn)
    g_w_up = _matmul_tn_blocks(hn, d_up.reshape(4, L, FF_BLK), 512, "grad_w_up")
    g_w_out = _matmul_tn(ycat, dh1_b, 512, D, "grad_w_out")
    late = ("w_up", "w_down", "w_out", "glu_w")
    halves = [g_w_up.reshape(4, 2, D // 2, FF_BLK), g_w_down.reshape(4, 2, D_FF // 8, D)]
    (dy, du_direct, g_glu_w, g_glu_b, g_ssm_d, g_ssm_g), swapped = _ssm_bwd_local(
        dh1_b, y0, y1, u, ssm_d, glu_w, glu_b, g_ssm, w_out, comm=_comm_pair_swap(halves, half=True) if dist else None)
    more = [g_w_out.reshape(4, 2, D // 8, D), g_glu_w.reshape(4, 2, D_SSM // 8, D_SSM)]
    (d_pooled, g_pool_w, g_pool_scale, g_pool_g), swapped_more = _pool_bwd_local(
        dh1_b, u, w_out, pool_w_b, pool_scale, g_pool, comm=_comm_pair_swap(more, half=True) if dist else None)
    halves, from_sibling = halves + more, list(swapped) + list(swapped_more)
    du_pool = _pool_bwd_window(d_pooled)
    reduce2 = None
    if dist:
        chip_sums = [_add_half(h, r, c_arr, "sum_pair_" + k, BF16) for k, h, r in zip(late, halves, from_sibling)]
        reduce2 = _comm_chip_exchange(chip_sums, scatter=True)
    (du0, gb0r, gb0i, gc0r, gc0i, gv0), from_chips = _ssm_scan_bwd(dy, u, s0r, s0i, *ssm_args[0], 1, True, comm=reduce2)
    mine = [_sum4(r, "sum_chips_" + k) for k, r in zip(late, from_chips)]
    (du1, gb1r, gb1i, gc1r, gc1i, gv1), theirs = _ssm_scan_bwd(
        dy, u, s1r, s1i, *ssm_args[1], 3, False, comm=_comm_pair_swap(mine) if dist else None)
    gvec = lambda j: jnp.stack([gv0[j], gv1[j]]).reshape(2 * N_SSM_GROUPS, SSM_STATE)
    g_lnar, g_aim, g_ldt = _ssm_params_bwd(lnar, aim, ldt, gvec(0), gvec(1), gvec(2), gvec(3))
    grad_x, d_u_b, g_mix_g = _in_bwd(du_pool, du_direct, du0, du1, dh1, x, g_mix, w_in)
    g_w_in = _matmul_tn(xn, d_u_b, 512, D, "grad_w_in")

    small = {
        "norm_mix_g": g_mix_g, "pool_w": g_pool_w, "pool_scale": g_pool_scale,
        "ssm_log_neg_a_re": g_lnar, "ssm_a_im": g_aim, "ssm_log_dt": g_ldt,
        "ssm_b_re": jnp.swapaxes(jnp.stack([gb0r, gb1r]), 2, 3), "ssm_b_im": jnp.swapaxes(jnp.stack([gb0i, gb1i]), 2, 3),
        "ssm_c_re": jnp.stack([gc0r, gc1r]), "ssm_c_im": jnp.stack([gc0i, gc1i]),
        "ssm_d": g_ssm_d, "glu_b": g_glu_b, "out_norm_pool_g": g_pool_g, "out_norm_ssm_g": g_ssm_g,
        "norm_ffn_g": g_ffn_g, "conv_b": jnp.concatenate([gbv[0], gbg[0]]), "final_norm_g": g_final,
        "conv_w": jnp.transpose(gcw, (2, 0, 1, 3)).reshape(3, -1),
    }
    big = {"w_in": g_w_in}
    reduced = dict(zip(late, zip(mine, theirs)))
    if not dist:
        big.update({"w_up": g_w_up, "w_down": g_w_down, "w_out": g_w_out, "glu_w": g_glu_w})
    return loss, grad_x, small, big, reduced


def kernel(x, norm_mix_g, w_in, pool_w, pool_scale, ssm_log_neg_a_re, ssm_a_im, ssm_log_dt, ssm_b_re, ssm_b_im, ssm_c_re, ssm_c_im, ssm_d, glu_w, glu_b, out_norm_pool_g, out_norm_ssm_g, w_out, norm_ffn_g, w_up, conv_w, conv_b, w_down, final_norm_g, loss_target, m_norm_mix_g, m_w_in, m_pool_w, m_pool_scale, m_ssm_log_neg_a_re, m_ssm_a_im, m_ssm_log_dt, m_ssm_b_re, m_ssm_b_im, m_ssm_c_re, m_ssm_c_im, m_ssm_d, m_glu_w, m_glu_b, m_out_norm_pool_g, m_out_norm_ssm_g, m_w_out, m_norm_ffn_g, m_w_up, m_conv_w, m_conv_b, m_w_down, m_final_norm_g, v_norm_mix_g, v_w_in, v_pool_w, v_pool_scale, v_ssm_log_neg_a_re, v_ssm_a_im, v_ssm_log_dt, v_ssm_b_re, v_ssm_b_im, v_ssm_c_re, v_ssm_c_im, v_ssm_d, v_glu_w, v_glu_b, v_out_norm_pool_g, v_out_norm_ssm_g, v_w_out, v_norm_ffn_g, v_w_up, v_conv_w, v_conv_b, v_w_down, v_final_norm_g):
    args = locals()
    w = {k: args[k] for k in WEIGHTS}
    m = {k: args["m_" + k] for k in WEIGHTS}
    v = {k: args["v_" + k] for k in WEIGHTS}
    chip = 2 * lax.axis_index("x") + lax.axis_index("y")
    c_arr = lax.axis_index("c").astype(jnp.int32).reshape(1)

    shards = {k: w[k].astype(BF16) for k in BIG}
    shards["conv_w"] = conv_w
    w_in_full = _comm_call(_comm_gather_split([shards["w_in"]], []), "gather_w_in")[0]
    loss, grad_x, g_small, g_big, reduced = _local_step(
        x[0], loss_target[0], w, {"w_in": w_in_full.reshape(-1, w_in_full.shape[-1])}, shards, c_arr)

    exact = [k for k in SMALL if k not in WIDE]
    packs = [_pack([loss] + [g_small[k] for k in exact] + [g_small["conv_w"]], 512),
             _pack([g_small[k] for k in WIDE], 512)]
    halves = [g_big["w_in"].reshape(4, 2, g_big["w_in"].shape[0] // 8, -1)]
    halves += [pk.reshape(1, 2, pk.shape[0] // 2, LANES) for pk in packs]
    from_sibling = _comm_call(_comm_pair_swap(halves, half=True), "reduce_pair")
    names = ("w_in", "exact", "wide")
    sums = [_add_half(h, r, c_arr, "sum_pair_" + k, dt)
            for k, h, r, dt in zip(names, halves, from_sibling, (BF16, F32, BF16))]
    from_chips = _comm_call(_comm_join(_comm_chip_exchange(sums[:1], scatter=True),
                                       _comm_chip_exchange([s[0] for s in sums[1:]], scatter=False)), "reduce_chips")
    mine = [_sum4(r, "sum_chips_" + k) for k, r in zip(names, from_chips)]
    theirs = _comm_call(_comm_pair_swap(mine), "swap_halves")
    grads = {}
    exact_all = _join_rows(mine[1], theirs[1], c_arr, "join_exact")
    wide_all = _join_rows(mine[2], theirs[2], c_arr, "join_wide")
    shapes = [loss.shape] + [w[k].shape for k in exact] + [(3, 4 * FF_BLK)]
    grads.update(zip(["loss"] + exact + ["conv_w_full"], _unpack(exact_all, shapes)))
    grads.update(zip(WIDE, _unpack(wide_all, [w[k].shape for k in WIDE])))
    loss = grads.pop("loss")[0, 0]
    grads["conv_w"] = lax.dynamic_slice_in_dim(grads.pop("conv_w_full"), chip * FF_BLK, FF_BLK, axis=1)

    delta, new_m, new_v = {}, {}, {}
    reduced["w_in"] = (mine[0], theirs[0])
    for k, (own, other) in reduced.items():
        grads[k], delta[k], new_m[k], new_v[k] = _adamw_halves(w[k], own, other, m[k], v[k], c_arr, "adamw_" + k)
    padded = ["ssm_b_re", "ssm_b_im"]
    for keys, name in ((padded, "adamw_ssm_b"), ([k for k in SMALL + ["conv_w"] if k not in padded], "adamw_small")):
        outs = _adamw_many(*([d[k] for k in keys] for d in (w, grads, m, v)), name)
        for d, o in zip((delta, new_m, new_v), outs):
            d.update(zip(keys, o))

    return (loss, grad_x[None], *[grads[k] for k in WEIGHTS], *[delta[k] for k in WEIGHTS],
            *[new_m[k] for k in WEIGHTS], *[new_v[k] for k in WEIGHTS])
```

```python
import numpy as np
import jax
import jax.numpy as jnp
from jax import lax
from jax.experimental import pallas as pl
from jax.experimental.pallas import tpu as pltpu

F32 = jnp.float32
BF16 = jnp.bfloat16
MESH = pl.DeviceIdType.MESH

EPS = 1e-6
POOL_WINDOWS = (2, 4, 8, 16)
POOL_GROUP = 128
SSM_GROUP = 16
SSM_STATE = 64
N_SSM_GROUPS = 32
N_STATE = N_SSM_GROUPS * SSM_STATE
QUAD = 256
N_QUAD = N_STATE // QUAD
SLAB = 256
D_SSM = 512
D_POOL = 512
D_FF = 2816
FF_BLK = 1408
HALO = 8
HALO_B = 16
LANES = 128
ADAM_LR, ADAM_B1, ADAM_B2, ADAM_EPS, ADAM_WD, ADAM_STEP = 0.001, 0.9, 0.999, 1e-08, 0.01, 10
VMEM_LIMIT = 56 * 2 ** 20

TL = 512
TF = 256
TC = 512
SCAN_W = 512


def _cp(*sem):
    return pltpu.CompilerParams(dimension_semantics=sem, vmem_limit_bytes=VMEM_LIMIT)


def _dot_nn(a, b):
    return jnp.dot(a, b, preferred_element_type=F32)


def _dot_nt(a, b):
    return lax.dot_general(a, b, (((1,), (1,)), ((), ())), preferred_element_type=F32)


def _dot_tn(a, b):
    return lax.dot_general(a, b, (((0,), (0,)), ((), ())), preferred_element_type=F32)


def _rms_fwd(x, g):
    inv = lax.rsqrt(jnp.mean(x * x, axis=-1, keepdims=True) + EPS)
    xh = x * inv
    return xh * g, xh, inv


def _rms_bwd(dy, xh, inv, g):
    dg = jnp.sum(dy * xh, axis=0, keepdims=True)
    dxh = dy * g
    dx = inv * (dxh - xh * jnp.mean(dxh * xh, axis=-1, keepdims=True))
    return dx, dg


_GELU_C = 0.7978845608028654
_GELU_A = 0.044715


def _gelu(y):
    t = jnp.tanh(_GELU_C * (y + _GELU_A * (y * y * y)))
    return 0.5 * y * (1.0 + t), t


def _gelu_grad(y, t):
    return 0.5 * (1.0 + t) + 0.5 * y * (1.0 - t * t) * (_GELU_C * (1.0 + 3.0 * _GELU_A * y * y))


def _sigmoid(x):
    return 1.0 / (1.0 + jnp.exp(-x))


def _full(shape):
    n = len(shape)
    return pl.BlockSpec(shape, lambda *_: (0,) * n)


def _fill_ext(ext_ref, prev_ref, cur_ref, next_ref, i, n, rows):
    ext_ref[0:HALO, :] = jnp.where(i > 0, prev_ref[...], 0.0).astype(ext_ref.dtype)
    ext_ref[HALO:HALO + rows, :] = cur_ref[...]
    ext_ref[HALO + rows:2 * HALO + rows, :] = jnp.where(i < n - 1, next_ref[...], 0.0).astype(ext_ref.dtype)


def _in_proj(x, g, w):
    L, D = x.shape
    E = w.shape[1]

    def body(x_ref, g_ref, w_ref, u_ref, xn_ref):
        y, _, _ = _rms_fwd(x_ref[...], g_ref[...])
        yb = y.astype(BF16)
        xn_ref[...] = yb
        u_ref[...] = _dot_nn(yb, w_ref[...])

    return pl.pallas_call(
        body, name="in_proj", grid=(L // TL,),
        in_specs=[pl.BlockSpec((TL, D), lambda i: (i, 0)), _full((1, D)), _full(w.shape)],
        out_specs=[pl.BlockSpec((TL, E), lambda i: (i, 0)), pl.BlockSpec((TL, D), lambda i: (i, 0))],
        out_shape=[jax.ShapeDtypeStruct((L, E), F32), jax.ShapeDtypeStruct((L, D), BF16)],
        compiler_params=_cp("parallel"))(x, g, w)


def _halo_specs_1d(rows, width, L, col):
    rb = rows // HALO
    last = L // HALO - 1
    return [pl.BlockSpec((HALO, width), lambda i: (jnp.maximum(i * rb - 1, 0), col)),
            pl.BlockSpec((rows, width), lambda i: (i, col)),
            pl.BlockSpec((HALO, width), lambda i: (jnp.minimum((i + 1) * rb, last), col))]


def _pooled_from_ext(ext_ref, t0, rows, L):
    t = t0 + lax.broadcasted_iota(jnp.int32, (rows, 1), 0)
    outs = []
    for gi, w in enumerate(POOL_WINDOWS):
        half = w // 2
        cs = slice(gi * POOL_GROUP, (gi + 1) * POOL_GROUP)
        acc = ext_ref[pl.ds(HALO - half, rows), cs]
        for s in range(-half + 1, half):
            acc = acc + ext_ref[pl.ds(HALO + s, rows), cs]
        cnt = (jnp.minimum(t + half, L) - jnp.maximum(t - half, 0)).astype(F32)
        outs.append(acc / cnt - ext_ref[pl.ds(HALO, rows), cs])
    return outs


def _pool_fwd(u, pool_w_b, pool_scale, g_pool):
    L = u.shape[0]
    n = L // TL

    def body(prev_ref, cur_ref, next_ref, pw_ref, ps_ref, g_ref, out_ref, ext_ref):
        i = pl.program_id(0)
        _fill_ext(ext_ref, prev_ref, cur_ref, next_ref, i, n, TL)
        pooled = _pooled_from_ext(ext_ref, i * TL, TL, L)
        ypre = jnp.concatenate([_dot_nn(pooled[gi].astype(BF16), pw_ref[gi]) for gi in range(4)], axis=-1)
        yn, _, _ = _rms_fwd(ypre * ps_ref[...], g_ref[...])
        out_ref[...] = yn.astype(BF16)

    return pl.pallas_call(
        body, name="pool_fwd", grid=(n,),
        in_specs=_halo_specs_1d(TL, D_POOL, L, 0) + [_full(pool_w_b.shape), _full((1, D_POOL)), _full((1, D_POOL))],
        out_specs=pl.BlockSpec((TL, D_POOL), lambda i: (i, 0)),
        out_shape=jax.ShapeDtypeStruct((L, D_POOL), BF16),
        scratch_shapes=[pltpu.VMEM((TL + 2 * HALO, D_POOL), F32)],
        compiler_params=_cp("parallel"))(u, u, u, pool_w_b, pool_scale, g_pool)


def _pool_bwd_local(dh1, u, w_out_b, pool_w_b, pool_scale, g_pool, comm=None):
    L = u.shape[0]
    n = L // TL
    D = dh1.shape[1]

    def body(dh_ref, prev_ref, cur_ref, next_ref, wo_ref, pw_ref, ps_ref, g_ref,
             dp_ref, gpw_ref, gps_ref, gg_ref, ext_ref):
        i = pl.program_id(0)

        @pl.when(i == 0)
        def _():
            gpw_ref[...] = jnp.zeros_like(gpw_ref)
            gps_ref[...] = jnp.zeros_like(gps_ref)
            gg_ref[...] = jnp.zeros_like(gg_ref)

        _fill_ext(ext_ref, prev_ref, cur_ref, next_ref, i, n, TL)
        pooled = [p.astype(BF16) for p in _pooled_from_ext(ext_ref, i * TL, TL, L)]
        ypre = jnp.concatenate([_dot_nn(pooled[gi], pw_ref[gi]) for gi in range(4)], axis=-1)
        ps = ps_ref[...]
        g = g_ref[...]
        _, xh, inv = _rms_fwd(ypre * ps, g)
        d_yn = _dot_nt(dh_ref[...].astype(BF16), wo_ref[...])
        d_y, dg = _rms_bwd(d_yn, xh, inv, g)
        gg_ref[...] += dg
        gps_ref[...] += jnp.sum(d_y * ypre, axis=0, keepdims=True)
        d_ypre = (d_y * ps).astype(BF16)
        for gi in range(4):
            cs = slice(gi * POOL_GROUP, (gi + 1) * POOL_GROUP)
            dp_ref[:, cs] = _dot_nt(d_ypre[:, cs], pw_ref[gi])
            gpw_ref[gi] += _dot_tn(pooled[gi], d_ypre[:, cs])

    return _hosted_call(
        body, comm, name="pool_bwd_local", grid=(n,),
        in_specs=[pl.BlockSpec((TL, D), lambda i: (i, 0))] + _halo_specs_1d(TL, D_POOL, L, 0)
        + [pl.BlockSpec((D_POOL, D), lambda i: (0, 0)), _full(pool_w_b.shape), _full((1, D_POOL)), _full((1, D_POOL))],
        out_specs=[pl.BlockSpec((TL, D_POOL), lambda i: (i, 0)), _full(pool_w_b.shape),
                   _full((1, D_POOL)), _full((1, D_POOL))],
        out_shape=[jax.ShapeDtypeStruct((L, D_POOL), F32), jax.ShapeDtypeStruct(pool_w_b.shape, F32),
                   jax.ShapeDtypeStruct((1, D_POOL), F32), jax.ShapeDtypeStruct((1, D_POOL), F32)],
        scratch_shapes=[pltpu.VMEM((TL + 2 * HALO, D_POOL), F32)],
        args=(dh1, u, u, u, w_out_b, pool_w_b, pool_scale, g_pool))


def _pool_bwd_window(d_pooled):
    L = d_pooled.shape[0]
    n = L // TL
    R = TL + 2 * HALO

    def body(prev_ref, cur_ref, next_ref, out_ref, ext_ref, q_ref):
        i = pl.program_id(0)
        _fill_ext(ext_ref, prev_ref, cur_ref, next_ref, i, n, TL)
        tr = i * TL - HALO + lax.broadcasted_iota(jnp.int32, (R, 1), 0)
        for gi, w in enumerate(POOL_WINDOWS):
            half = w // 2
            cs = slice(gi * POOL_GROUP, (gi + 1) * POOL_GROUP)
            cnt = jnp.maximum(jnp.minimum(tr + half, L) - jnp.maximum(tr - half, 0), 1).astype(F32)
            q_ref[:, cs] = ext_ref[:, cs] / cnt
        for gi, w in enumerate(POOL_WINDOWS):
            half = w // 2
            cs = slice(gi * POOL_GROUP, (gi + 1) * POOL_GROUP)
            acc = q_ref[pl.ds(HALO - half + 1, TL), cs]
            for s in range(-half + 2, half + 1):
                acc = acc + q_ref[pl.ds(HALO + s, TL), cs]
            out_ref[:, cs] = acc - ext_ref[pl.ds(HALO, TL), cs]

    return pl.pallas_call(
        body, name="pool_bwd_window", grid=(n,),
        in_specs=_halo_specs_1d(TL, D_POOL, L, 0),
        out_specs=pl.BlockSpec((TL, D_POOL), lambda i: (i, 0)),
        out_shape=jax.ShapeDtypeStruct((L, D_POOL), F32),
        scratch_shapes=[pltpu.VMEM((R, D_POOL), F32), pltpu.VMEM((R, D_POOL), F32)],
        compiler_params=_cp("parallel"))(d_pooled, d_pooled, d_pooled)


def _ssm_param_fn(lnar, aim, ldt):
    dt = jnp.exp(ldt)
    a_re = -jnp.exp(lnar)
    mag = jnp.exp(a_re * dt)
    ang = aim * dt
    lr, li = mag * jnp.cos(ang), mag * jnp.sin(ang)
    den = a_re * a_re + aim * aim
    fr = ((lr - 1.0) * a_re + li * aim) / den
    fi = (li * a_re - (lr - 1.0) * aim) / den
    return lr, li, fr, fi


def _ssm_params(lnar, aim, ldt):
    def body(a_ref, b_ref, c_ref, lr_ref, li_ref, fr_ref, fi_ref):
        lr, li, fr, fi = _ssm_param_fn(a_ref[...], b_ref[...], c_ref[...])
        lr_ref[...] = lr
        li_ref[...] = li
        fr_ref[...] = fr
        fi_ref[...] = fi

    sh = jax.ShapeDtypeStruct(lnar.shape, F32)
    return pl.pallas_call(body, name="ssm_params", out_shape=[sh] * 4)(lnar, aim, ldt)


def _ssm_params_bwd(lnar, aim, ldt, glr, gli, gfr, gfi):
    def body(a_ref, b_ref, c_ref, g0, g1, g2, g3, da_ref, db_ref, dc_ref):
        _, vjp = jax.vjp(_ssm_param_fn, a_ref[...], b_ref[...], c_ref[...])
        da, db, dc = vjp((g0[...], g1[...], g2[...], g3[...]))
        da_ref[...] = da
        db_ref[...] = db
        dc_ref[...] = jnp.sum(dc, axis=1, keepdims=True)

    return pl.pallas_call(
        body, name="ssm_params_bwd",
        out_shape=[jax.ShapeDtypeStruct(lnar.shape, F32), jax.ShapeDtypeStruct(aim.shape, F32),
                   jax.ShapeDtypeStruct((ldt.shape[0], 1), F32)])(lnar, aim, ldt, glr, gli, gfr, gfi)


def _scan_tables(lam4):
    def build(lr, li, reverse, out_ref, k):
        row = lax.broadcasted_iota(jnp.int32, (8, N_STATE), 0)
        lrb = jnp.broadcast_to(lr, (8, N_STATE))
        lib = jnp.broadcast_to(li, (8, N_STATE))
        pr, pi = lrb, lib
        for s, sh in enumerate((1, 2, 4)):
            mask = (row < 8 - sh) if reverse else (row >= sh)
            out_ref[k, 2 * s] = jnp.where(mask, pr, 0.0)
            out_ref[k, 2 * s + 1] = jnp.where(mask, pi, 0.0)
            pr, pi = pr * pr - pi * pi, 2.0 * pr * pi
        pr, pi = lrb, lib
        p8r = jnp.zeros((8, N_STATE), F32)
        p8i = jnp.zeros((8, N_STATE), F32)
        for j in range(8):
            r = 7 - j if reverse else j
            p8r = jnp.where(row == r, pr, p8r)
            p8i = jnp.where(row == r, pi, p8i)
            pr, pi = pr * lrb - pi * lib, pr * lib + pi * lrb
        out_ref[k, 6] = p8r
        out_ref[k, 7] = p8i

    def body(lam_ref, out_ref):
        l0r, l0i, l1r, l1i = (lam_ref[j:j + 1, :] for j in range(4))
        build(l0r, l0i, False, out_ref, 0)
        build(l0r, -l0i, True, out_ref, 1)
        build(l1r, l1i, True, out_ref, 2)
        build(l1r, -l1i, False, out_ref, 3)

    return pl.pallas_call(body, name="scan_tables",
                          out_shape=jax.ShapeDtypeStruct((4, 8, 8, N_STATE), F32))(lam4)


def _b_block(g):
    q, gl = divmod(g, 4)
    r0, c0 = gl * SSM_STATE, (q % 4) * 4 * SSM_GROUP + gl * SSM_GROUP
    return q, slice(r0, r0 + SSM_STATE), slice(c0, c0 + SSM_GROUP)


def _c_block(g):
    q, rows, cols = _b_block(g)
    return q, cols, rows


def _ssm_expand(b_re, b_im, c_re, c_im, f_re, f_im):
    def body(bre_ref, bim_ref, cre_ref, cim_ref, fre_ref, fim_ref, *rest):
        outs, tmp = rest[:8], rest[8]

        def b_bar_re(d, g):
            return fre_ref[d, g] * bre_ref[d, g] - fim_ref[d, g] * bim_ref[d, g]

        def b_bar_im(d, g):
            return fre_ref[d, g] * bim_ref[d, g] + fim_ref[d, g] * bre_ref[d, g]

        for d in range(2):
            for j, (src, where) in enumerate(((b_bar_re, _b_block), (b_bar_im, _b_block),
                                              (lambda d, g: cre_ref[d, g], _c_block),
                                              (lambda d, g: cim_ref[d, g], _c_block))):
                tmp[...] = jnp.zeros_like(tmp)
                for g in range(N_SSM_GROUPS):
                    q, rows, cols = where(g)
                    tmp[q, rows, cols] = src(d, g)
                outs[4 * d + j][...] = tmp[...].astype(BF16)

    dense = jax.ShapeDtypeStruct((N_QUAD, QUAD, SLAB), BF16)
    return pl.pallas_call(body, name="ssm_expand", out_shape=[dense] * 8,
                          scratch_shapes=[pltpu.VMEM((N_QUAD, QUAD, SLAB), F32)],
                          compiler_params=pltpu.CompilerParams(vmem_limit_bytes=VMEM_LIMIT))(
                              b_re, b_im, c_re, c_im, f_re, f_im)


def _ssm_unfold(gbb_re, gbb_im, b_re_t, b_im_t, f_re, f_im):
    def body(gr_ref, gi_ref, br_ref, bi_ref, fr_ref, fi_ref, obr_ref, obi_ref, ofr_ref, ofi_ref):
        gr, gi, br, bi, fr, fi = (r[...] for r in (gr_ref, gi_ref, br_ref, bi_ref, fr_ref, fi_ref))
        obr_ref[...] = fr * gr + fi * gi
        obi_ref[...] = fr * gi - fi * gr
        ofr_ref[...] = jnp.sum(br * gr + bi * gi, axis=2, keepdims=True)
        ofi_ref[...] = jnp.sum(br * gi - bi * gr, axis=2, keepdims=True)

    gb = jax.ShapeDtypeStruct(gbb_re.shape, F32)
    gf = jax.ShapeDtypeStruct(f_re.shape, F32)
    return pl.pallas_call(body, name="ssm_unfold", out_shape=[gb, gb, gf, gf])(
        gbb_re, gbb_im, b_re_t, b_im_t, f_re, f_im)


def _scan_rows(src_re, src_im, dst_re, dst_im, tab_ref, k, carry_re, carry_im, rows, reverse, s_refs=None):
    ng = rows // 8
    edge = 0 if reverse else 7
    row_id = lax.broadcasted_iota(jnp.int32, (8, SCAN_W), 0)
    sums = []
    for lt in range(N_STATE // SCAN_W):
        sl = slice(lt * SCAN_W, (lt + 1) * SCAN_W)

        def step(r, c, sl=sl):
            tabs = [tab_ref[k, j, :, sl] for j in range(8)]
            cr, ci = c[0], c[1]
            row = pl.multiple_of((ng - 1 - r) * 8 if reverse else r * 8, 8)
            xr = src_re[pl.ds(row, 8), sl]
            xi = src_im[pl.ds(row, 8), sl]
            for s, sh in enumerate((1, 2, 4)):
                amt = 8 - sh if reverse else sh
                rr = pltpu.roll(xr, amt, 0)
                ri = pltpu.roll(xi, amt, 0)
                mr, mi = tabs[2 * s], tabs[2 * s + 1]
                xr, xi = xr + mr * rr - mi * ri, xi + mr * ri + mi * rr
            xr, xi = xr + tabs[6] * cr - tabs[7] * ci, xi + tabs[6] * ci + tabs[7] * cr
            dst_re[pl.ds(row, 8), sl] = xr
            dst_im[pl.ds(row, 8), sl] = xi
            ncr = jnp.broadcast_to(xr[edge:edge + 1, :], (8, SCAN_W))
            nci = jnp.broadcast_to(xi[edge:edge + 1, :], (8, SCAN_W))
            if s_refs is None:
                return ncr, nci
            amt = 7 if reverse else 1
            far = 7 if reverse else 0
            nr = jnp.where(row_id == far, cr, pltpu.roll(xr, amt, 0))
            ni = jnp.where(row_id == far, ci, pltpu.roll(xi, amt, 0))
            sr = s_refs[0][pl.ds(row, 8), sl]
            si = s_refs[1][pl.ds(row, 8), sl]
            return ncr, nci, c[2] + nr * sr + ni * si, c[3] + ni * sr - nr * si

        init = (carry_re[:, sl], carry_im[:, sl])
        if s_refs is not None:
            init = init + (jnp.zeros((8, SCAN_W), F32), jnp.zeros((8, SCAN_W), F32))
        out = lax.fori_loop(0, ng, step, init)
        carry_re[:, sl] = out[0]
        carry_im[:, sl] = out[1]
        if s_refs is not None:
            sums.append((jnp.sum(out[2], axis=0, keepdims=True), jnp.sum(out[3], axis=0, keepdims=True)))
    return sums


def _ssm_scan_fwd(u, b_re, b_im, c_re, c_im, tables, k, reverse, comm=None):
    L = u.shape[0]
    nc = L // TC
    chunk = (lambda i: nc - 1 - i) if reverse else (lambda i: i)

    def body(u_ref, bre_ref, bim_ref, cre_ref, cim_ref, tab_ref,
             y_ref, sre_ref, sim_ref, in_re, in_im, carry_re, carry_im):
        @pl.when(pl.program_id(0) == 0)
        def _():
            carry_re[...] = jnp.zeros_like(carry_re)
            carry_im[...] = jnp.zeros_like(carry_im)

        ub = u_ref[...].astype(BF16)
        for q in range(N_QUAD):
            qs = slice(q * QUAD, (q + 1) * QUAD)
            us = ub[:, (q // 4) * SLAB:(q // 4 + 1) * SLAB]
            in_re[:, qs] = _dot_nt(us, bre_ref[q])
            in_im[:, qs] = _dot_nt(us, bim_ref[q])
        _scan_rows(in_re, in_im, sre_ref, sim_ref, tab_ref, k, carry_re, carry_im, TC, reverse)
        for j in range(D_SSM // SLAB):
            acc = jnp.zeros((TC, SLAB), F32)
            for q in range(4 * j, 4 * j + 4):
                qs = slice(q * QUAD, (q + 1) * QUAD)
                acc = acc + _dot_nt(sre_ref[:, qs].astype(BF16), cre_ref[q])
                acc = acc - _dot_nt(sim_ref[:, qs].astype(BF16), cim_ref[q])
            y_ref[:, j * SLAB:(j + 1) * SLAB] = acc

    return _hosted_call(
        body, comm, name="ssm_scan_rev" if reverse else "ssm_scan_fwd", grid=(nc,),
        in_specs=[pl.BlockSpec((TC, D_SSM), lambda i: (chunk(i), 1))]
        + [_full(b_re.shape)] * 4 + [_full(tables.shape)],
        out_specs=[pl.BlockSpec((TC, D_SSM), lambda i: (chunk(i), 0)),
                   pl.BlockSpec((TC, N_STATE), lambda i: (chunk(i), 0)),
                   pl.BlockSpec((TC, N_STATE), lambda i: (chunk(i), 0))],
        out_shape=[jax.ShapeDtypeStruct((L, D_SSM), F32), jax.ShapeDtypeStruct((L, N_STATE), F32),
                   jax.ShapeDtypeStruct((L, N_STATE), F32)],
        scratch_shapes=[pltpu.VMEM((TC, N_STATE), F32), pltpu.VMEM((TC, N_STATE), F32),
                        pltpu.VMEM((8, N_STATE), F32), pltpu.VMEM((8, N_STATE), F32)],
        args=(u, b_re, b_im, c_re, c_im, tables))


def _quad_channels(q):
    c0 = (q // 4) * SLAB + (q % 4) * 4 * SSM_GROUP
    return slice(c0, c0 + 4 * SSM_GROUP)


def _ssm_scan_bwd(dy, u, s_re, s_im, b_re, b_im, c_re, c_im, tables, k, reverse, comm=None):
    L = u.shape[0]
    nc = L // TC
    chunk = (lambda i: nc - 1 - i) if reverse else (lambda i: i)

    def body(dy_ref, u_ref, sre_ref, sim_ref, bre_ref, bim_ref, cre_ref, cim_ref, tab_ref,
             du_ref, ob_re, ob_im, oc_re, oc_im, gv_ref,
             a_re, a_im, carry_re, carry_im, gbr_ref, gbi_ref, gcr_ref, gci_ref):
        @pl.when(pl.program_id(0) == 0)
        def _():
            carry_re[...] = jnp.zeros_like(carry_re)
            carry_im[...] = jnp.zeros_like(carry_im)
            for r in (gbr_ref, gbi_ref, gcr_ref, gci_ref, gv_ref):
                r[...] = jnp.zeros_like(r)

        dyb = dy_ref[...].astype(BF16)
        ub = u_ref[...].astype(BF16)
        for q in range(N_QUAD):
            qs = slice(q * QUAD, (q + 1) * QUAD)
            ds = dyb[:, (q // 4) * SLAB:(q // 4 + 1) * SLAB]
            a_re[:, qs] = _dot_nn(ds, cre_ref[q])
            a_im[:, qs] = -_dot_nn(ds, cim_ref[q])
            dq = dyb[:, _quad_channels(q)]
            gcr_ref[q] += _dot_tn(dq, sre_ref[:, qs].astype(BF16))
            gci_ref[q] -= _dot_tn(dq, sim_ref[:, qs].astype(BF16))
        sums = _scan_rows(a_re, a_im, a_re, a_im, tab_ref, k, carry_re, carry_im, TC, reverse,
                          s_refs=(sre_ref, sim_ref))
        for lt, (glr, gli) in enumerate(sums):
            sl = slice(lt * SCAN_W, (lt + 1) * SCAN_W)
            gv_ref[0:1, sl] += glr
            gv_ref[1:2, sl] += gli
        for j in range(D_SSM // SLAB):
            us = ub[:, j * SLAB:(j + 1) * SLAB]
            acc = jnp.zeros((TC, SLAB), F32)
            for q in range(4 * j, 4 * j + 4):
                qs = slice(q * QUAD, (q + 1) * QUAD)
                dbr = a_re[:, qs].astype(BF16)
                dbi = a_im[:, qs].astype(BF16)
                uq = ub[:, _quad_channels(q)]
                gbr_ref[q] += _dot_tn(uq, dbr)
                gbi_ref[q] += _dot_tn(uq, dbi)
                acc = acc + _dot_nn(dbr, bre_ref[q]) + _dot_nn(dbi, bim_ref[q])
            du_ref[:, j * SLAB:(j + 1) * SLAB] = acc

        @pl.when(pl.program_id(0) == nc - 1)
        def _():
            for g in range(N_SSM_GROUPS):
                q, gl = divmod(g, 4)
                rows = slice(gl * SSM_GROUP, (gl + 1) * SSM_GROUP)
                cols = slice(gl * SSM_STATE, (gl + 1) * SSM_STATE)
                for out, acc_ref in ((ob_re, gbr_ref), (ob_im, gbi_ref), (oc_re, gcr_ref), (oc_im, gci_ref)):
                    out[g] = acc_ref[q, rows, cols]

    gshape = jax.ShapeDtypeStruct((N_SSM_GROUPS, SSM_GROUP, SSM_STATE), F32)
    compact = pltpu.VMEM((N_QUAD, 4 * SSM_GROUP, QUAD), F32)
    return _hosted_call(
        body, comm, name="ssm_bwd_rev" if reverse else "ssm_bwd_fwd", grid=(nc,),
        in_specs=[pl.BlockSpec((TC, D_SSM), lambda i: (chunk(i), 0)),
                  pl.BlockSpec((TC, D_SSM), lambda i: (chunk(i), 1)),
                  pl.BlockSpec((TC, N_STATE), lambda i: (chunk(i), 0)),
                  pl.BlockSpec((TC, N_STATE), lambda i: (chunk(i), 0))]
        + [_full(b_re.shape)] * 4 + [_full(tables.shape)],
        out_specs=[pl.BlockSpec((TC, D_SSM), lambda i: (chunk(i), 0))] + [_full(gshape.shape)] * 4
        + [_full((2, N_STATE))],
        out_shape=[jax.ShapeDtypeStruct((L, D_SSM), F32), gshape, gshape, gshape, gshape,
                   jax.ShapeDtypeStruct((2, N_STATE), F32)],
        scratch_shapes=[pltpu.VMEM((TC, N_STATE), F32), pltpu.VMEM((TC, N_STATE), F32),
                        pltpu.VMEM((8, N_STATE), F32), pltpu.VMEM((8, N_STATE), F32),
                        compact, compact, compact, compact],
        args=(dy, u, s_re, s_im, b_re, b_im, c_re, c_im, tables))


def _ssm_post(yf, yb, u, d, glu_w, glu_b):
    y = yf + yb + d * u
    z, t = _gelu(y)
    zb = z.astype(BF16)
    gate = _sigmoid(_dot_nn(zb, glu_w) + glu_b)
    return y, z, t, zb, gate


def _mix_out(yn_pool, yf, yb, u, x, ssm_d, glu_w_b, glu_b, g_ssm, w_out_b, g_ffn):
    L, D = x.shape

    def body(ynp_ref, yf_ref, yb_ref, u_ref, x_ref, d_ref, gw_ref, gb_ref, gs_ref, wo_ref, gf_ref,
             h1_ref, hn_ref, ycat_ref):
        _, z, _, _, gate = _ssm_post(yf_ref[...], yb_ref[...], u_ref[...], d_ref[...], gw_ref[...], gb_ref[...])
        yns, _, _ = _rms_fwd(z * gate, gs_ref[...])
        ynsb = yns.astype(BF16)
        ynp = ynp_ref[...]
        ycat_ref[:, 0:D_POOL] = ynp
        ycat_ref[:, D_POOL:D] = ynsb
        h1 = x_ref[...] + _dot_nn(ynp, wo_ref[0:D_POOL, :]) + _dot_nn(ynsb, wo_ref[D_POOL:D, :])
        h1_ref[...] = h1
        hn, _, _ = _rms_fwd(h1, gf_ref[...])
        hn_ref[...] = hn.astype(BF16)

    half = lambda c: pl.BlockSpec((TL, D_SSM), lambda i: (i, c))
    row = pl.BlockSpec((TL, D), lambda i: (i, 0))
    return pl.pallas_call(
        body, name="mix_out", grid=(L // TL,),
        in_specs=[half(0), half(0), half(0), half(1), row, _full((1, D_SSM)), _full(glu_w_b.shape),
                  _full((1, D_SSM)), _full((1, D_SSM)), _full(w_out_b.shape), _full((1, D))],
        out_specs=[row, row, row],
        out_shape=[jax.ShapeDtypeStruct((L, D), F32), jax.ShapeDtypeStruct((L, D), BF16),
                   jax.ShapeDtypeStruct((L, D), BF16)],
        compiler_params=_cp("parallel"))(yn_pool, yf, yb, u, x, ssm_d, glu_w_b, glu_b, g_ssm, w_out_b, g_ffn)


def _ssm_bwd_local(dh1, yf, yb, u, ssm_d, glu_w_b, glu_b, g_ssm, w_out_b, comm=None):
    L, D = dh1.shape

    def body(dh_ref, yf_ref, yb_ref, u_ref, d_ref, gw_ref, gb_ref, gs_ref, wo_ref,
             dy_ref, du_ref, ggw_ref, ggb_ref, gd_ref, ggs_ref):
        @pl.when(pl.program_id(0) == 0)
        def _():
            for r in (ggw_ref, ggb_ref, gd_ref, ggs_ref):
                r[...] = jnp.zeros_like(r)

        u = u_ref[...]
        d = d_ref[...]
        y, z, t, zb, gate = _ssm_post(yf_ref[...], yb_ref[...], u, d, gw_ref[...], gb_ref[...])
        gs = gs_ref[...]
        _, xh, inv = _rms_fwd(z * gate, gs)
        d_yn = _dot_nt(dh_ref[...].astype(BF16), wo_ref[...])
        d_o, dgs = _rms_bwd(d_yn, xh, inv, gs)
        ggs_ref[...] += dgs
        d_zg = d_o * z * gate * (1.0 - gate)
        d_zgb = d_zg.astype(BF16)
        ggb_ref[...] += jnp.sum(d_zg, axis=0, keepdims=True)
        ggw_ref[...] += _dot_tn(zb, d_zgb)
        d_z = d_o * gate + _dot_nt(d_zgb, gw_ref[...])
        d_y = d_z * _gelu_grad(y, t)
        gd_ref[...] += jnp.sum(d_y * u, axis=0, keepdims=True)
        dy_ref[...] = d_y
        du_ref[...] = d_y * d

    half = lambda c: pl.BlockSpec((TL, D_SSM), lambda i: (i, c))
    vec = _full((1, D_SSM))
    return _hosted_call(
        body, comm, name="ssm_bwd_local", grid=(L // TL,),
        in_specs=[pl.BlockSpec((TL, D), lambda i: (i, 0)), half(0), half(0), half(1), vec, _full(glu_w_b.shape),
                  vec, vec, pl.BlockSpec((D_SSM, D), lambda i: (1, 0))],
        out_specs=[half(0), half(0), _full(glu_w_b.shape), vec, vec, vec],
        out_shape=[jax.ShapeDtypeStruct((L, D_SSM), F32), jax.ShapeDtypeStruct((L, D_SSM), F32),
                   jax.ShapeDtypeStruct(glu_w_b.shape, F32)] + [jax.ShapeDtypeStruct((1, D_SSM), F32)] * 3,
        scratch_shapes=[], args=(dh1, yf, yb, u, ssm_d, glu_w_b, glu_b, g_ssm, w_out_b))


def _in_bwd(du_pool, du_a, du_b, du_c, dh1, x, g, w_in_b):
    L, D = x.shape

    def body(p_ref, a_ref, b_ref, c_ref, dh_ref, x_ref, g_ref, w_ref, dx_ref, dub_ref, gg_ref):
        @pl.when(pl.program_id(0) == 0)
        def _():
            gg_ref[...] = jnp.zeros_like(gg_ref)

        dub_ref[:, 0:D_POOL] = p_ref[...].astype(BF16)
        dub_ref[:, D_POOL:D] = (a_ref[...] + b_ref[...] + c_ref[...]).astype(BF16)
        d_xn = _dot_nt(dub_ref[...], w_ref[...])
        gv = g_ref[...]
        _, xh, inv = _rms_fwd(x_ref[...], gv)
        dx, dg = _rms_bwd(d_xn, xh, inv, gv)
        gg_ref[...] += dg
        dx_ref[...] = dh_ref[...] + dx

    half = pl.BlockSpec((TL, D_SSM), lambda i: (i, 0))
    row = pl.BlockSpec((TL, D), lambda i: (i, 0))
    return pl.pallas_call(
        body, name="in_bwd", grid=(L // TL,),
        in_specs=[half, half, half, half, row, row, _full((1, D)), _full(w_in_b.shape)],
        out_specs=[row, row, _full((1, D))],
        out_shape=[jax.ShapeDtypeStruct((L, D), F32), jax.ShapeDtypeStruct((L, D), BF16),
                   jax.ShapeDtypeStruct((1, D), F32)],
        compiler_params=_cp("arbitrary"))(du_pool, du_a, du_b, du_c, dh1, x, g, w_in_b)


def _ffn_up(hn, w_up4):
    L, D = hn.shape

    def body(h_ref, w_ref, o_ref):
        o_ref[...] = _dot_nn(h_ref[...], w_ref[...]).astype(BF16)

    return pl.pallas_call(
        body, name="ffn_up", grid=(4, L // TL),
        in_specs=[pl.BlockSpec((TL, D), lambda j, i: (i, 0)), pl.BlockSpec((None, D, FF_BLK), lambda j, i: (j, 0, 0))],
        out_specs=pl.BlockSpec((TL, FF_BLK), lambda j, i: (i, j)),
        out_shape=jax.ShapeDtypeStruct((L, 4 * FF_BLK), BF16),
        compiler_params=_cp("parallel", "parallel"))(hn, w_up4)


def _halo_specs_2d(rows, width, L, col, order):
    rb = rows // HALO_B
    last = L // HALO_B - 1
    if order == "ik":
        wrap = lambda f: (lambda i, k: f(i, k))
    else:
        wrap = lambda f: (lambda k, i: f(i, k))
    return [pl.BlockSpec((HALO_B, width), wrap(lambda i, k: (jnp.maximum(i * rb - 1, 0), col(k)))),
            pl.BlockSpec((rows, width), wrap(lambda i, k: (i, col(k)))),
            pl.BlockSpec((HALO_B, width), wrap(lambda i, k: (jnp.minimum((i + 1) * rb, last), col(k))))]


def _shift_mats(rows):
    r = lax.broadcasted_iota(jnp.int32, (rows, rows), 0)
    c = lax.broadcasted_iota(jnp.int32, (rows, rows), 1)
    return (c == r - 1).astype(BF16), (c == r + 1).astype(BF16)


def _neighbours(x, prev_ref, next_ref, cs, i, n, mats):
    rows = x.shape[0]
    row = lax.broadcasted_iota(jnp.int32, (rows, 1), 0)
    before = jnp.where(i > 0, prev_ref[:, cs].astype(F32)[HALO_B - 1:HALO_B, :], 0.0)
    after = jnp.where(i < n - 1, next_ref[:, cs].astype(F32)[0:1, :], 0.0)
    if mats is None:
        xf = x.astype(F32)
        down, up = pltpu.roll(xf, 1, 0), pltpu.roll(xf, rows - 1, 0)
    else:
        down, up = _dot_nn(mats[0], x), _dot_nn(mats[1], x)
    return jnp.where(row == 0, before, down), jnp.where(row == rows - 1, after, up)


def _conv3(x, before, after, w, b):
    return before * w[0:1, :] + x.astype(F32) * w[1:2, :] + after * w[2:3, :] + b


def _col_chunks(width, size=256):
    return [slice(c, min(c + size, width)) for c in range(0, width, size)]


def _ffn_down_loss(up, conv_w, conv_b, w_down_b, h1, target, g_final):
    L, D = h1.shape
    n = L // TF
    nk = D_FF // FF_BLK

    def body(vp, vc, vn, gp, gc, gn, wv_ref, wg_ref, bv_ref, bg_ref, wd_ref, h1_ref, t_ref, gf_ref,
             a_ref, cv_ref, cg_ref, dh2_ref, dh2b_ref, loss_ref, gg_ref, acc_ref):
        i = pl.program_id(0)
        k = pl.program_id(1)

        @pl.when((i == 0) & (k == 0))
        def _():
            loss_ref[...] = jnp.zeros_like(loss_ref)
            gg_ref[...] = jnp.zeros_like(gg_ref)

        @pl.when(k == 0)
        def _():
            acc_ref[...] = jnp.zeros_like(acc_ref)

        mats = _shift_mats(TF)
        for cs in _col_chunks(FF_BLK):
            xv, xg = vc[:, cs], gc[:, cs]
            val = _conv3(xv, *_neighbours(xv, vp, vn, cs, i, n, mats), wv_ref[:, cs], bv_ref[:, cs])
            gate = _conv3(xg, *_neighbours(xg, gp, gn, cs, i, n, mats), wg_ref[:, cs], bg_ref[:, cs])
            a_ref[:, cs] = (val * (gate * _sigmoid(gate))).astype(BF16)
            cv_ref[:, cs] = val.astype(BF16)
            cg_ref[:, cs] = gate.astype(BF16)
        acc_ref[...] += _dot_nn(a_ref[...], wd_ref[pl.ds(pl.multiple_of(k * FF_BLK, LANES), FF_BLK), :])

        @pl.when(k == nk - 1)
        def _():
            gf = gf_ref[...]
            y, xh, inv = _rms_fwd(h1_ref[...] + acc_ref[...], gf)
            diff = y - t_ref[...]
            part = 0.5 * jnp.sum(jnp.mean(diff * diff, axis=-1, keepdims=True), axis=0, keepdims=True)
            loss_ref[...] += jnp.broadcast_to(part, loss_ref.shape)
            dx, dg = _rms_bwd(diff * (1.0 / D), xh, inv, gf)
            gg_ref[...] += dg
            dh2_ref[...] = dx
            dh2b_ref[...] = dx.astype(BF16)

    row = pl.BlockSpec((TF, D), lambda i, k: (i, 0))
    cw = lambda off: pl.BlockSpec((3, FF_BLK), lambda i, k: (0, k + off))
    cb = lambda off: pl.BlockSpec((1, FF_BLK), lambda i, k: (0, k + off))
    return pl.pallas_call(
        body, name="ffn_down_loss", grid=(n, nk),
        in_specs=_halo_specs_2d(TF, FF_BLK, L, lambda k: k, "ik") + _halo_specs_2d(TF, FF_BLK, L, lambda k: k + nk, "ik")
        + [cw(0), cw(nk), cb(0), cb(nk), _full(w_down_b.shape), row, row, _full((1, D))],
        out_specs=[pl.BlockSpec((TF, FF_BLK), lambda i, k: (i, k))] * 3 + [row, row, _full((1, LANES)), _full((1, D))],
        out_shape=[jax.ShapeDtypeStruct((L, D_FF), BF16)] * 3
        + [jax.ShapeDtypeStruct((L, D), F32), jax.ShapeDtypeStruct((L, D), BF16),
           jax.ShapeDtypeStruct((1, LANES), F32), jax.ShapeDtypeStruct((1, D), F32)],
        scratch_shapes=[pltpu.VMEM((TF, D), F32)],
        compiler_params=_cp("arbitrary", "arbitrary"))(
            up, up, up, up, up, up, conv_w, conv_w, conv_b, conv_b, w_down_b, h1, target, g_final)


def _ffn_act_bwd(c_val, c_gate, w_down_b, dh2):
    L, D = dh2.shape
    n = L // TL
    nk = D_FF // FF_BLK

    def body(v_ref, g_ref, wd_ref, dh_ref, dv_ref, dg_ref, gbv_ref, gbg_ref):
        @pl.when(pl.program_id(1) == 0)
        def _():
            gbv_ref[...] = jnp.zeros_like(gbv_ref)
            gbg_ref[...] = jnp.zeros_like(gbg_ref)

        dh = dh_ref[...]
        for cs in _col_chunks(FF_BLK):
            val, gate = v_ref[:, cs].astype(F32), g_ref[:, cs].astype(F32)
            d_a = _dot_nt(dh, wd_ref[cs, :])
            sg = _sigmoid(gate)
            d_val = d_a * (gate * sg)
            d_gate = d_a * val * (sg * (1.0 + gate * (1.0 - sg)))
            dv_ref[:, cs] = d_val.astype(BF16)
            dg_ref[:, cs] = d_gate.astype(BF16)
            gbv_ref[:, cs] += jnp.sum(d_val, axis=0, keepdims=True)
            gbg_ref[:, cs] += jnp.sum(d_gate, axis=0, keepdims=True)

    blk = pl.BlockSpec((TL, FF_BLK), lambda k, i: (i, k))
    acc = pl.BlockSpec((1, FF_BLK), lambda k, i: (0, k))
    return pl.pallas_call(
        body, name="ffn_act_bwd", grid=(nk, n),
        in_specs=[blk, blk, pl.BlockSpec((FF_BLK, D), lambda k, i: (k, 0)), pl.BlockSpec((TL, D), lambda k, i: (i, 0))],
        out_specs=[blk, blk, acc, acc],
        out_shape=[jax.ShapeDtypeStruct((L, D_FF), BF16), jax.ShapeDtypeStruct((L, D_FF), BF16),
                   jax.ShapeDtypeStruct((1, D_FF), F32), jax.ShapeDtypeStruct((1, D_FF), F32)],
        compiler_params=_cp("arbitrary", "arbitrary"))(c_val, c_gate, w_down_b, dh2)


def _ffn_up_bwd(d_val, d_gate, up, conv_w, w_up4, h1, dh2, g_ffn):
    L, D = h1.shape
    n = L // TF
    nk = D_FF // FF_BLK

    def body(vp, vc, vn, gp, gc, gn, uv_ref, ug_ref, wv_ref, wg_ref, wu_ref, h1_ref, dh2_ref, g_ref,
             dup_ref, dh1_ref, dh1b_ref, gg_ref, gcw_ref, acc_ref):
        i = pl.program_id(0)
        k = pl.program_id(1)

        @pl.when((i == 0) & (k == 0))
        def _():
            gg_ref[...] = jnp.zeros_like(gg_ref)
            gcw_ref[...] = jnp.zeros_like(gcw_ref)

        @pl.when(k == 0)
        def _():
            acc_ref[...] = jnp.zeros_like(acc_ref)

        acc = jnp.zeros((TF, D), F32)
        for j, (blocks, u_ref, w_ref) in enumerate((((vp, vc, vn), uv_ref, wv_ref), ((gp, gc, gn), ug_ref, wg_ref))):
            for cs in _col_chunks(FF_BLK):
                d = blocks[1][:, cs]
                before, after = _neighbours(d, blocks[0], blocks[2], cs, i, n, None)
                taps = (after, d.astype(F32), before)
                w = w_ref[:, cs]
                d_up = (taps[0] * w[0:1, :] + taps[1] * w[1:2, :] + taps[2] * w[2:3, :]).astype(BF16)
                dup_ref[j, :, cs] = d_up
                acc = acc + _dot_nt(d_up, wu_ref[k + j * nk, :, cs])
                x = u_ref[:, cs].astype(F32)
                for r in range(3):
                    gcw_ref[j, k, r:r + 1, cs] += jnp.sum(taps[r] * x, axis=0, keepdims=True)
        acc_ref[...] += acc

        @pl.when(k == nk - 1)
        def _():
            g = g_ref[...]
            _, xh, inv = _rms_fwd(h1_ref[...], g)
            dx, dg = _rms_bwd(acc_ref[...], xh, inv, g)
            gg_ref[...] += dg
            dh1 = dh2_ref[...] + dx
            dh1_ref[...] = dh1
            dh1b_ref[...] = dh1.astype(BF16)

    row = pl.BlockSpec((TF, D), lambda i, k: (i, 0))
    cw = lambda off: pl.BlockSpec((3, FF_BLK), lambda i, k: (0, k + off))
    tile = lambda off: pl.BlockSpec((TF, FF_BLK), lambda i, k: (i, k + off))
    return pl.pallas_call(
        body, name="ffn_up_bwd", grid=(n, nk),
        in_specs=_halo_specs_2d(TF, FF_BLK, L, lambda k: k, "ik") + _halo_specs_2d(TF, FF_BLK, L, lambda k: k, "ik")
        + [tile(0), tile(nk), cw(0), cw(nk), _full(w_up4.shape), row, row, _full((1, D))],
        out_specs=[pl.BlockSpec((2, None, TF, FF_BLK), lambda i, k: (0, k, i, 0)), row, row, _full((1, D)),
                   _full((2, nk, 3, FF_BLK))],
        out_shape=[jax.ShapeDtypeStruct((2, nk, L, FF_BLK), BF16), jax.ShapeDtypeStruct((L, D), F32),
                   jax.ShapeDtypeStruct((L, D), BF16), jax.ShapeDtypeStruct((1, D), F32),
                   jax.ShapeDtypeStruct((2, nk, 3, FF_BLK), F32)],
        scratch_shapes=[pltpu.VMEM((TF, D), F32)],
        compiler_params=_cp("arbitrary", "arbitrary"))(
            d_val, d_val, d_val, d_gate, d_gate, d_gate, up, up, conv_w, conv_w, w_up4, h1, dh2, g_ffn)


def _matmul_tn(a, b, tm, tn, name, tk=2048):
    L, M = a.shape
    N = b.shape[1]
    tk = min(tk, L)

    def body(a_ref, b_ref, o_ref):
        @pl.when(pl.program_id(2) == 0)
        def _():
            o_ref[...] = jnp.zeros_like(o_ref)

        o_ref[...] += _dot_tn(a_ref[...], b_ref[...])

    return pl.pallas_call(
        body, name=name, grid=(M // tm, N // tn, L // tk),
        in_specs=[pl.BlockSpec((tk, tm), lambda m, n, l: (l, m)), pl.BlockSpec((tk, tn), lambda m, n, l: (l, n))],
        out_specs=pl.BlockSpec((tm, tn), lambda m, n, l: (m, n)),
        out_shape=jax.ShapeDtypeStruct((M, N), F32),
        compiler_params=_cp("parallel", "parallel", "arbitrary"))(a, b)


def _matmul_tn_blocks(a, b, tm, name, tk=2048):
    L, M = a.shape
    J, _, N = b.shape
    tk = min(tk, L)

    def body(a_ref, b_ref, o_ref):
        @pl.when(pl.program_id(2) == 0)
        def _():
            o_ref[...] = jnp.zeros_like(o_ref)

        o_ref[...] += _dot_tn(a_ref[...], b_ref[...])

    return pl.pallas_call(
        body, name=name, grid=(M // tm, J, L // tk),
        in_specs=[pl.BlockSpec((tk, tm), lambda m, j, l: (l, m)), pl.BlockSpec((None, tk, N), lambda m, j, l: (j, l, 0))],
        out_specs=pl.BlockSpec((None, tm, N), lambda m, j, l: (j, m, 0)),
        out_shape=jax.ShapeDtypeStruct((J, M, N), F32),
        compiler_params=_cp("parallel", "parallel", "arbitrary"))(a, b)


def _row_tile(rows):
    for t in (512, 352, 256, 128, 64, 8):
        if rows % t == 0:
            return t
    return rows


def _add_half(g, r, c_arr, name, out_dtype=F32):
    _, _, R, C = g.shape
    tr = _row_tile(R)

    def body(c_ref, g_ref, r_ref, o_ref):
        o_ref[...] = (g_ref[...] + r_ref[...]).astype(out_dtype)

    return pl.pallas_call(
        body, name=name,
        grid_spec=pltpu.PrefetchScalarGridSpec(
            num_scalar_prefetch=1, grid=(g.shape[0], R // tr),
            in_specs=[pl.BlockSpec((None, None, tr, C), lambda j, i, c: (j, c[0], i, 0)),
                      pl.BlockSpec((None, tr, C), lambda j, i, c: (j, i, 0))],
            out_specs=pl.BlockSpec((None, tr, C), lambda j, i, c: (j, i, 0))),
        out_shape=jax.ShapeDtypeStruct(r.shape, out_dtype),
        compiler_params=_cp("parallel", "parallel"))(c_arr, g, r)


def _add2(a, b, name):
    R, C = a.shape
    tr = _row_tile(R)

    def body(a_ref, b_ref, o_ref):
        o_ref[...] = a_ref[...] + b_ref[...]

    spec = pl.BlockSpec((tr, C), lambda i: (i, 0))
    return pl.pallas_call(body, name=name, grid=(R // tr,), in_specs=[spec, spec], out_specs=spec,
                          out_shape=jax.ShapeDtypeStruct(a.shape, F32), compiler_params=_cp("parallel"))(a, b)


def _sum4(p, name):
    _, R, C = p.shape
    tr = _row_tile(R)

    def body(p_ref, o_ref):
        q = [p_ref[j].astype(F32) for j in range(4)]
        o_ref[...] = ((q[0] + q[1]) + q[2]) + q[3]

    return pl.pallas_call(
        body, name=name, grid=(R // tr,),
        in_specs=[pl.BlockSpec((4, tr, C), lambda i: (0, i, 0))],
        out_specs=pl.BlockSpec((tr, C), lambda i: (i, 0)),
        out_shape=jax.ShapeDtypeStruct((R, C), F32), compiler_params=_cp("parallel"))(p)


def _adamw_refs(w_ref, g_ref, m_ref, v_ref, d_ref, nm_ref, nv_ref):
    gv = g_ref[...]
    nm = ADAM_B1 * m_ref[...] + (1.0 - ADAM_B1) * gv
    nv = ADAM_B2 * v_ref[...] + (1.0 - ADAM_B2) * (gv * gv)
    m_hat = nm / (1.0 - ADAM_B1 ** ADAM_STEP)
    v_hat = nv / (1.0 - ADAM_B2 ** ADAM_STEP)
    d_ref[...] = -ADAM_LR * (m_hat / (jnp.sqrt(v_hat) + ADAM_EPS) + ADAM_WD * w_ref[...])
    nm_ref[...] = nm
    nv_ref[...] = nv


def _adamw_many(ws, gs, ms, vs, name):
    n = len(ws)

    def body(*refs):
        for k in range(n):
            _adamw_refs(*(refs[j * n + k] for j in range(7)))

    out_shape = [jax.ShapeDtypeStruct(w.shape, F32) for w in ws] * 3
    res = pl.pallas_call(body, name=name, out_shape=out_shape,
                         compiler_params=pltpu.CompilerParams(vmem_limit_bytes=VMEM_LIMIT))(*ws, *gs, *ms, *vs)
    return res[:n], res[n:2 * n], res[2 * n:]


def _adamw(w, g, m, v, name):
    R, C = w.shape
    tr = _row_tile(R)
    body = lambda *refs: _adamw_refs(*refs)

    spec = pl.BlockSpec((tr, C), lambda i: (i, 0))
    sh = jax.ShapeDtypeStruct((R, C), F32)
    return pl.pallas_call(body, name=name, grid=(R // tr,), in_specs=[spec] * 4, out_specs=[spec] * 3,
                          out_shape=[sh] * 3, compiler_params=_cp("parallel"))(w, g, m, v)


def _join_rows(own, other, c_arr, name):
    R, C = own.shape
    tr = _row_tile(R)

    def body(c_ref, own_ref, other_ref, o_ref):
        o_ref[...] = jnp.where(pl.program_id(0) == c_ref[0], own_ref[...], other_ref[...])

    half = pl.BlockSpec((tr, C), lambda h, i, c: (i, 0))
    return pl.pallas_call(
        body, name=name,
        grid_spec=pltpu.PrefetchScalarGridSpec(
            num_scalar_prefetch=1, grid=(2, R // tr), in_specs=[half, half],
            out_specs=pl.BlockSpec((tr, C), lambda h, i, c: (h * (R // tr) + i, 0))),
        out_shape=jax.ShapeDtypeStruct((2 * R, C), F32),
        compiler_params=_cp("parallel", "parallel"))(c_arr, own, other)


def _adamw_halves(w, own, other, m, v, c_arr, name):
    R, C = own.shape
    tr = _row_tile(R)
    while tr * C * 4 > 2 ** 20 and tr % 16 == 0:
        tr //= 2

    def body(c_ref, w_ref, own_ref, other_ref, m_ref, v_ref, g_ref, d_ref, nm_ref, nv_ref):
        g_ref[...] = jnp.where(pl.program_id(0) == c_ref[0], own_ref[...], other_ref[...])
        _adamw_refs(w_ref, g_ref, m_ref, v_ref, d_ref, nm_ref, nv_ref)

    half = pl.BlockSpec((tr, C), lambda h, i, c: (i, 0))
    full = pl.BlockSpec((tr, C), lambda h, i, c: (h * (R // tr) + i, 0))
    sh = jax.ShapeDtypeStruct((2 * R, C), F32)
    return pl.pallas_call(
        body, name=name,
        grid_spec=pltpu.PrefetchScalarGridSpec(
            num_scalar_prefetch=1, grid=(2, R // tr), in_specs=[full, half, half, full, full], out_specs=[full] * 4),
        out_shape=[sh] * 4, compiler_params=_cp("parallel", "parallel"))(c_arr, w, own, other, m, v)


_ANY = pl.BlockSpec(memory_space=pl.ANY)


def _position():
    return lax.axis_index("x"), lax.axis_index("y"), lax.axis_index("c")


class _Comm:
    def __init__(self, arrs, out_shape, sems, start, finish):
        self.arrs, self.out_shape, self.sems, self.start, self.finish = arrs, out_shape, sems, start, finish


def _comm_call(comm, name):
    n, m = len(comm.arrs), len(comm.out_shape)

    def body(*refs):
        ins, outs, sems = refs[:n], refs[n:n + m], refs[n + m:]
        comm.start(ins, outs, sems)
        comm.finish(ins, outs, sems)

    return pl.pallas_call(
        body, name=name, in_specs=[_ANY] * n, out_specs=[_ANY] * m, out_shape=comm.out_shape,
        scratch_shapes=comm.sems, compiler_params=pltpu.CompilerParams(has_side_effects=True))(*comm.arrs)


def _hosted_call(body, comm, *, name, grid, in_specs, out_specs, out_shape, scratch_shapes, args):
    sem = ("arbitrary",) * len(grid)
    if comm is None:
        return pl.pallas_call(body, name=name, grid=grid, in_specs=in_specs, out_specs=out_specs, out_shape=out_shape,
                              scratch_shapes=scratch_shapes, compiler_params=_cp(*sem))(*args), []
    n_in, n_out, n_scr = len(in_specs), len(out_specs), len(scratch_shapes)
    ci, co = len(comm.arrs), len(comm.out_shape)

    def full(*refs):
        ins, refs = refs[:n_in], refs[n_in:]
        cins, refs = refs[:ci], refs[ci:]
        outs, refs = refs[:n_out], refs[n_out:]
        couts, refs = refs[:co], refs[co:]
        scr, csems = refs[:n_scr], refs[n_scr:]
        first, last = True, True
        for d, size in enumerate(grid):
            first = first & (pl.program_id(d) == 0)
            last = last & (pl.program_id(d) == size - 1)

        @pl.when(first)
        def _():
            comm.start(cins, couts, csems)

        body(*ins, *outs, *scr)

        @pl.when(last)
        def _():
            comm.finish(cins, couts, csems)

    res = pl.pallas_call(
        full, name=name, grid=grid, in_specs=list(in_specs) + [_ANY] * ci, out_specs=list(out_specs) + [_ANY] * co,
        out_shape=list(out_shape) + list(comm.out_shape), scratch_shapes=list(scratch_shapes) + list(comm.sems),
        compiler_params=_cp(*sem))(*args, *comm.arrs)
    return res[:n_out], res[n_out:]


def _comm_join(*comms):
    def parts(xs, attr):
        out, at = [], 0
        for cm in comms:
            n = len(getattr(cm, attr))
            out.append(xs[at:at + n])
            at += n
        return out

    def start(ins, outs, sems):
        for cm, i, o, s in zip(comms, parts(ins, "arrs"), parts(outs, "out_shape"), parts(sems, "sems")):
            cm.start(i, o, s)

    def finish(ins, outs, sems):
        for cm, i, o, s in zip(comms, parts(ins, "arrs"), parts(outs, "out_shape"), parts(sems, "sems")):
            cm.finish(i, o, s)

    cat = lambda attr: [x for cm in comms for x in getattr(cm, attr)]
    return _Comm(cat("arrs"), cat("out_shape"), cat("sems"), start, finish)


def _dma_sems(*counts):
    return [pltpu.SemaphoreType.DMA((n,)) for n in counts]


def _comm_pair_swap(arrs, half=False):
    n = len(arrs)
    out_shape = [jax.ShapeDtypeStruct(a.shape[:1] + a.shape[2:] if half else a.shape, a.dtype) for a in arrs]

    def copies(ins, outs, sems):
        x, y, c = _position()
        return [pltpu.make_async_remote_copy(
            src_ref=ins[k].at[:, 1 - c] if half else ins[k], dst_ref=outs[k], send_sem=sems[0].at[k],
            recv_sem=sems[1].at[k], device_id=(x, y, 1 - c), device_id_type=MESH) for k in range(n)]

    def start(ins, outs, sems):
        for cp in copies(ins, outs, sems):
            cp.start()

    def finish(ins, outs, sems):
        for cp in copies(ins, outs, sems):
            cp.wait()

    return _Comm(arrs, out_shape, _dma_sems(n, n), start, finish)


def _chip_of(j, c):
    return (jnp.right_shift(j, 1), jnp.bitwise_and(j, 1), c)


def _comm_chip_exchange(arrs, scatter):
    n = len(arrs)
    out_shape = [jax.ShapeDtypeStruct(a.shape if scatter else (4,) + a.shape, a.dtype) for a in arrs]

    def copies(ins, outs, sems):
        x, y, c = _position()
        me = 2 * x + y
        local, sent, landed = [], [], []
        for k in range(n):
            local.append(pltpu.make_async_copy(ins[k].at[me] if scatter else ins[k], outs[k].at[me], sems[2].at[k]))
            for d in (1, 2, 3):
                j = jnp.bitwise_xor(me, d)
                s = 3 * k + d - 1
                src = ins[k].at[j] if scatter else ins[k]
                for dst, group in ((outs[k].at[me], sent), (outs[k].at[j], landed)):
                    group.append(pltpu.make_async_remote_copy(
                        src_ref=src, dst_ref=dst, send_sem=sems[0].at[s], recv_sem=sems[1].at[s],
                        device_id=_chip_of(j, c), device_id_type=MESH))
        return local, sent, landed

    def start(ins, outs, sems):
        local, sent, _ = copies(ins, outs, sems)
        for cp in local + sent:
            cp.start()

    def finish(ins, outs, sems):
        local, sent, landed = copies(ins, outs, sems)
        for cp in sent:
            cp.wait_send()
        for cp in landed:
            cp.wait_recv()
        for cp in local:
            cp.wait()

    return _Comm(arrs, out_shape, _dma_sems(3 * n, 3 * n, n), start, finish)


def _comm_pair_gather(arrs):
    n = len(arrs)
    out_shape = [jax.ShapeDtypeStruct((2,) + a.shape, a.dtype) for a in arrs]

    def copies(ins, outs, sems):
        x, y, c = _position()
        local, sent, landed = [], [], []
        for k in range(n):
            local.append(pltpu.make_async_copy(ins[k], outs[k].at[c], sems[2].at[k]))
            for dst, group in ((outs[k].at[c], sent), (outs[k].at[1 - c], landed)):
                group.append(pltpu.make_async_remote_copy(
                    src_ref=ins[k], dst_ref=dst, send_sem=sems[0].at[k], recv_sem=sems[1].at[k],
                    device_id=(x, y, 1 - c), device_id_type=MESH))
        return local, sent, landed

    def start(ins, outs, sems):
        local, sent, _ = copies(ins, outs, sems)
        for cp in local + sent:
            cp.start()

    def finish(ins, outs, sems):
        local, sent, landed = copies(ins, outs, sems)
        for cp in sent:
            cp.wait_send()
        for cp in landed:
            cp.wait_recv()
        for cp in local:
            cp.wait()

    return _Comm(arrs, out_shape, _dma_sems(n, n, n), start, finish)


LOCAL_PARTS = 4


def _comm_gather_split(shards, whole):
    n, nw = len(shards), len(whole)
    arrs = list(shards) + list(whole)
    out_shape = [jax.ShapeDtypeStruct((4,) + a.shape, a.dtype) for a in arrs]

    def copies(ins, outs, sems):
        x, y, c = _position()
        me = 2 * x + y
        local, sent, landed, passed, passed_in = [], [], [], [], []
        for k in range(n + nw):
            if k >= n:
                local.append(pltpu.make_async_copy(ins[k], outs[k].at[me], sems[4].at[LOCAL_PARTS * k]))
            else:
                part = shards[k].shape[0] // LOCAL_PARTS
                for r in range(LOCAL_PARTS):
                    local.append(pltpu.make_async_copy(ins[k].at[pl.ds(r * part, part)],
                                                       outs[k].at[me, pl.ds(r * part, part)],
                                                       sems[4].at[LOCAL_PARTS * k + r]))
            for d in (1, 2, 3):
                j = jnp.bitwise_xor(me, d)
                s = 3 * k + d - 1
                if k >= n:
                    src, mine, theirs = ins[k], outs[k].at[me], outs[k].at[j]
                else:
                    h = shards[k].shape[0] // 2
                    rows = pl.ds(pl.multiple_of(c * h, 16), h)
                    other = pl.ds(pl.multiple_of((1 - c) * h, 16), h)
                    src, mine, theirs = ins[k].at[rows], outs[k].at[me, rows], outs[k].at[j, rows]
                    for dst, group in ((theirs, passed), (outs[k].at[j, other], passed_in)):
                        group.append(pltpu.make_async_remote_copy(
                            src_ref=theirs, dst_ref=dst, send_sem=sems[2].at[s], recv_sem=sems[3].at[s],
                            device_id=(x, y, 1 - c), device_id_type=MESH))
                for dst, group in ((mine, sent), (theirs, landed)):
                    group.append(pltpu.make_async_remote_copy(
                        src_ref=src, dst_ref=dst, send_sem=sems[0].at[s], recv_sem=sems[1].at[s],
                        device_id=_chip_of(j, c), device_id_type=MESH))
        return local, sent, landed, passed, passed_in

    def start(ins, outs, sems):
        local, sent, _, _, _ = copies(ins, outs, sems)
        for cp in local + sent:
            cp.start()

    def finish(ins, outs, sems):
        local, sent, landed, passed, passed_in = copies(ins, outs, sems)
        for cp in landed[:3 * n]:
            cp.wait_recv()
        for cp in passed:
            cp.start()
        for cp in landed[3 * n:]:
            cp.wait_recv()
        for cp in sent:
            cp.wait_send()
        for cp in passed:
            cp.wait_send()
        for cp in passed_in:
            cp.wait_recv()
        for cp in local:
            cp.wait()

    t = 3 * (n + nw)
    return _Comm(arrs, out_shape, _dma_sems(t, t, max(3 * n, 1), max(3 * n, 1), LOCAL_PARTS * (n + nw)), start, finish)


def _pack(arrs, row_multiple):
    parts = []
    for a in arrs:
        flat = a.reshape(-1).astype(F32)
        pad = (-flat.shape[0]) % LANES
        parts.append(jnp.pad(flat, (0, pad)) if pad else flat)
    flat = jnp.concatenate(parts)
    rows = -(-flat.shape[0] // LANES)
    rows_p = -(-rows // row_multiple) * row_multiple
    return jnp.pad(flat, (0, rows_p * LANES - flat.shape[0])).reshape(rows_p, LANES)


def _unpack(packed, shapes):
    flat = packed.reshape(-1)
    outs, off = [], 0
    for sh in shapes:
        size = int(np.prod(sh))
        outs.append(flat[off:off + size].reshape(sh))
        off += size + (-size) % LANES
    return outs


SMALL = ["norm_mix_g", "pool_w", "pool_scale", "ssm_log_neg_a_re", "ssm_a_im", "ssm_log_dt", "ssm_b_re", "ssm_b_im",
         "ssm_c_re", "ssm_c_im", "ssm_d", "glu_b", "out_norm_pool_g", "out_norm_ssm_g", "norm_ffn_g", "conv_b",
         "final_norm_g"]
BIG = ["w_in", "glu_w", "w_out", "w_up", "w_down"]
WIDE = ["pool_w", "ssm_b_re", "ssm_b_im", "ssm_c_re", "ssm_c_im"]
WEIGHTS = ['norm_mix_g', 'w_in', 'pool_w', 'pool_scale', 'ssm_log_neg_a_re', 'ssm_a_im', 'ssm_log_dt', 'ssm_b_re',
           'ssm_b_im', 'ssm_c_re', 'ssm_c_im', 'ssm_d', 'glu_w', 'glu_b', 'out_norm_pool_g', 'out_norm_ssm_g', 'w_out',
           'norm_ffn_g', 'w_up', 'conv_w', 'conv_b', 'w_down', 'final_norm_g']


def _local_step(x, target, p, full, shards=None, c_arr=None):
    L, D = x.shape
    dist = shards is not None
    row = lambda a: a.reshape(1, -1)
    w_in = full["w_in"]
    pool_w_b = p["pool_w"].astype(BF16)
    g_mix, g_pool, g_ssm, g_ffn, g_fin = (row(p[k]) for k in (
        "norm_mix_g", "out_norm_pool_g", "out_norm_ssm_g", "norm_ffn_g", "final_norm_g"))
    pool_scale, ssm_d, glu_b, conv_b = (row(p[k]) for k in ("pool_scale", "ssm_d", "glu_b", "conv_b"))

    lnar = p["ssm_log_neg_a_re"].reshape(2 * N_SSM_GROUPS, SSM_STATE)
    aim = p["ssm_a_im"].reshape(2 * N_SSM_GROUPS, SSM_STATE)
    ldt = jnp.broadcast_to(p["ssm_log_dt"].reshape(2 * N_SSM_GROUPS, 1), lnar.shape)
    lam_re, lam_im, f_re, f_im = _ssm_params(lnar, aim, ldt)
    flat2 = lambda a: a.reshape(2, N_STATE)
    lam4 = jnp.stack([flat2(lam_re)[0], flat2(lam_im)[0], flat2(lam_re)[1], flat2(lam_im)[1]])
    tables = _scan_tables(lam4)
    per_group = (2, N_SSM_GROUPS, SSM_STATE)
    dense = _ssm_expand(p["ssm_b_re"], p["ssm_b_im"], p["ssm_c_re"], p["ssm_c_im"],
                        f_re.reshape(per_group + (1,)), f_im.reshape(per_group + (1,)))
    ssm_args = [tuple(dense[4 * d:4 * d + 4]) + (tables,) for d in range(2)]

    u, xn = _in_proj(x, g_mix, w_in)
    yn_pool = _pool_fwd(u, pool_w_b, pool_scale, g_pool)
    gather1 = _comm_gather_split([shards[k] for k in ("glu_w", "w_out", "w_down")], [shards["conv_w"]]) if dist else None
    (y0, s0r, s0i), got1 = _ssm_scan_fwd(u, *ssm_args[0], 0, False, comm=gather1)
    gather2 = _comm_gather_split([shards["w_up"]], []) if dist else None
    (y1, s1r, s1i), got2 = _ssm_scan_fwd(u, *ssm_args[1], 2, True, comm=gather2)
    if dist:
        glu_w, w_out, w_down = (g.reshape((-1,) + g.shape[2:]) for g in got1[:3])
        conv_w = jnp.transpose(got1[3], (1, 0, 2)).reshape(3, -1)
        w_up4 = got2[0]
    else:
        glu_w, w_out, w_up4, w_down, conv_w = (full[k] for k in ("glu_w", "w_out", "w_up", "w_down", "conv_w"))
    h1, hn, ycat = _mix_out(yn_pool, y0, y1, u, x, ssm_d, glu_w, glu_b, g_ssm, w_out, g_ffn)
    up = _ffn_up(hn, w_up4)
    a, c_val, c_gate, dh2, dh2_b, loss, g_final = _ffn_down_loss(up, conv_w, conv_b, w_down, h1, target, g_fin)

    d_val, d_gate, gbv, gbg = _ffn_act_bwd(c_val, c_gate, w_down, dh2_b)
    g_w_down = _matmul_tn(a, dh2_b, FF_BLK, D, "grad_w_down")
    d_up, dh1, dh1_b, g_ffn_g, gcw = _ffn_up_bwd(d_val, d_gate, up, conv_w, w_up4, h1, dh2, g_ffn)
    g_w_up = _matmul_tn_blocks(hn, d_up.reshape(4, L, FF_BLK), 512, "grad_w_up")
    g_w_out = _matmul_tn(ycat, dh1_b, 512, D, "grad_w_out")
    late = ("w_up", "w_down", "w_out", "glu_w")
    halves = [g_w_up.reshape(4, 2, D // 2, FF_BLK), g_w_down.reshape(4, 2, D_FF // 8, D)]
    (dy, du_direct, g_glu_w, g_glu_b, g_ssm_d, g_ssm_g), swapped = _ssm_bwd_local(
        dh1_b, y0, y1, u, ssm_d, glu_w, glu_b, g_ssm, w_out, comm=_comm_pair_swap(halves, half=True) if dist else None)
    more = [g_w_out.reshape(4, 2, D // 8, D), g_glu_w.reshape(4, 2, D_SSM // 8, D_SSM)]
    (d_pooled, g_pool_w, g_pool_scale, g_pool_g), swapped_more = _pool_bwd_local(
        dh1_b, u, w_out, pool_w_b, pool_scale, g_pool, comm=_comm_pair_swap(more, half=True) if dist else None)
    halves, from_sibling = halves + more, list(swapped) + list(swapped_more)
    du_pool = _pool_bwd_window(d_pooled)
    reduce2 = None
    if dist:
        chip_sums = [_add_half(h, r, c_arr, "sum_pair_" + k, BF16) for k, h, r in zip(late, halves, from_sibling)]
        reduce2 = _comm_chip_exchange(chip_sums, scatter=True)
    (du0, gb0r, gb0i, gc0r, gc0i, gv0), from_chips = _ssm_scan_bwd(dy, u, s0r, s0i, *ssm_args[0], 1, True, comm=reduce2)
    mine = [_sum4(r, "sum_chips_" + k) for k, r in zip(late, from_chips)]
    (du1, gb1r, gb1i, gc1r, gc1i, gv1), theirs = _ssm_scan_bwd(
        dy, u, s1r, s1i, *ssm_args[1], 3, False, comm=_comm_pair_swap(mine) if dist else None)
    by_state = (2, N_SSM_GROUPS, 1, SSM_STATE)
    g_b_re, g_b_im, g_f_re, g_f_im = _ssm_unfold(
        jnp.stack([gb0r, gb1r]), jnp.stack([gb0i, gb1i]),
        jnp.swapaxes(p["ssm_b_re"], 2, 3), jnp.swapaxes(p["ssm_b_im"], 2, 3),
        f_re.reshape(by_state), f_im.reshape(by_state))
    gvec = lambda j: jnp.stack([gv0[j], gv1[j]]).reshape(2 * N_SSM_GROUPS, SSM_STATE)
    g_lnar, g_aim, g_ldt = _ssm_params_bwd(lnar, aim, ldt, gvec(0), gvec(1),
                                           g_f_re.reshape(lnar.shape), g_f_im.reshape(lnar.shape))
    grad_x, d_u_b, g_mix_g = _in_bwd(du_pool, du_direct, du0, du1, dh1, x, g_mix, w_in)
    g_w_in = _matmul_tn(xn, d_u_b, 512, D, "grad_w_in")

    small = {
        "norm_mix_g": g_mix_g, "pool_w": g_pool_w, "pool_scale": g_pool_scale,
        "ssm_log_neg_a_re": g_lnar, "ssm_a_im": g_aim, "ssm_log_dt": g_ldt,
        "ssm_b_re": jnp.swapaxes(g_b_re, 2, 3), "ssm_b_im": jnp.swapaxes(g_b_im, 2, 3),
        "ssm_c_re": jnp.stack([gc0r, gc1r]), "ssm_c_im": jnp.stack([gc0i, gc1i]),
        "ssm_d": g_ssm_d, "glu_b": g_glu_b, "out_norm_pool_g": g_pool_g, "out_norm_ssm_g": g_ssm_g,
        "norm_ffn_g": g_ffn_g, "conv_b": jnp.concatenate([gbv[0], gbg[0]]), "final_norm_g": g_final,
        "conv_w": jnp.transpose(gcw, (2, 0, 1, 3)).reshape(3, -1),
    }
    big = {"w_in": g_w_in}
    reduced = dict(zip(late, zip(mine, theirs)))
    if not dist:
        big.update({"w_up": g_w_up, "w_down": g_w_down, "w_out": g_w_out, "glu_w": g_glu_w})
    return loss, grad_x, small, big, reduced


def kernel(x, norm_mix_g, w_in, pool_w, pool_scale, ssm_log_neg_a_re, ssm_a_im, ssm_log_dt, ssm_b_re, ssm_b_im, ssm_c_re, ssm_c_im, ssm_d, glu_w, glu_b, out_norm_pool_g, out_norm_ssm_g, w_out, norm_ffn_g, w_up, conv_w, conv_b, w_down, final_norm_g, loss_target, m_norm_mix_g, m_w_in, m_pool_w, m_pool_scale, m_ssm_log_neg_a_re, m_ssm_a_im, m_ssm_log_dt, m_ssm_b_re, m_ssm_b_im, m_ssm_c_re, m_ssm_c_im, m_ssm_d, m_glu_w, m_glu_b, m_out_norm_pool_g, m_out_norm_ssm_g, m_w_out, m_norm_ffn_g, m_w_up, m_conv_w, m_conv_b, m_w_down, m_final_norm_g, v_norm_mix_g, v_w_in, v_pool_w, v_pool_scale, v_ssm_log_neg_a_re, v_ssm_a_im, v_ssm_log_dt, v_ssm_b_re, v_ssm_b_im, v_ssm_c_re, v_ssm_c_im, v_ssm_d, v_glu_w, v_glu_b, v_out_norm_pool_g, v_out_norm_ssm_g, v_w_out, v_norm_ffn_g, v_w_up, v_conv_w, v_conv_b, v_w_down, v_final_norm_g):
    args = locals()
    w = {k: args[k] for k in WEIGHTS}
    m = {k: args["m_" + k] for k in WEIGHTS}
    v = {k: args["v_" + k] for k in WEIGHTS}
    chip = 2 * lax.axis_index("x") + lax.axis_index("y")
    c_arr = lax.axis_index("c").astype(jnp.int32).reshape(1)

    shards = {k: w[k].astype(BF16) for k in BIG}
    shards["conv_w"] = conv_w
    w_in_full = _comm_call(_comm_gather_split([shards["w_in"]], []), "gather_w_in")[0]
    loss, grad_x, g_small, g_big, reduced = _local_step(
        x[0], loss_target[0], w, {"w_in": w_in_full.reshape(-1, w_in_full.shape[-1])}, shards, c_arr)

    exact = [k for k in SMALL if k not in WIDE]
    packs = [_pack([loss] + [g_small[k] for k in exact] + [g_small["conv_w"]], 512),
             _pack([g_small[k] for k in WIDE], 512)]
    halves = [g_big["w_in"].reshape(4, 2, g_big["w_in"].shape[0] // 8, -1)]
    halves += [pk.reshape(1, 2, pk.shape[0] // 2, LANES) for pk in packs]
    from_sibling = _comm_call(_comm_pair_swap(halves, half=True), "reduce_pair")
    names = ("w_in", "exact", "wide")
    sums = [_add_half(h, r, c_arr, "sum_pair_" + k, dt)
            for k, h, r, dt in zip(names, halves, from_sibling, (BF16, F32, BF16))]
    from_chips = _comm_call(_comm_join(_comm_chip_exchange(sums[:1], scatter=True),
                                       _comm_chip_exchange([s[0] for s in sums[1:]], scatter=False)), "reduce_chips")
    mine = [_sum4(r, "sum_chips_" + k) for k, r in zip(names, from_chips)]
    theirs = _comm_call(_comm_pair_swap(mine), "swap_halves")
    grads = {}
    exact_all = _join_rows(mine[1], theirs[1], c_arr, "join_exact")
    wide_all = _join_rows(mine[2], theirs[2], c_arr, "join_wide")
    shapes = [loss.shape] + [w[k].shape for k in exact] + [(3, 4 * FF_BLK)]
    grads.update(zip(["loss"] + exact + ["conv_w_full"], _unpack(exact_all, shapes)))
    grads.update(zip(WIDE, _unpack(wide_all, [w[k].shape for k in WIDE])))
    loss = grads.pop("loss")[0, 0]
    grads["conv_w"] = lax.dynamic_slice_in_dim(grads.pop("conv_w_full"), chip * FF_BLK, FF_BLK, axis=1)

    delta, new_m, new_v = {}, {}, {}
    reduced["w_in"] = (mine[0], theirs[0])
    for k, (own, other) in reduced.items():
        grads[k], delta[k], new_m[k], new_v[k] = _adamw_halves(w[k], own, other, m[k], v[k], c_arr, "adamw_" + k)
    padded = ["ssm_b_re", "ssm_b_im"]
    for keys, name in ((padded, "adamw_ssm_b"), ([k for k in SMALL + ["conv_w"] if k not in padded], "adamw_small")):
        outs = _adamw_many(*([d[k] for k in keys] for d in (w, grads, m, v)), name)
        for d, o in zip((delta, new_m, new_v), outs):
            d.update(zip(keys, o))

    return (loss, grad_x[None], *[grads[k] for k in WEIGHTS], *[delta[k] for k in WEIGHTS],
            *[new_m[k] for k in WEIGHTS], *[new_v[k] for k in WEIGHTS])
```

```python
import numpy as np
import jax
import jax.numpy as jnp
from jax import lax
from jax.experimental import pallas as pl
from jax.experimental.pallas import tpu as pltpu

F32 = jnp.float32
BF16 = jnp.bfloat16
MESH = pl.DeviceIdType.MESH

EPS = 1e-6
POOL_WINDOWS = (2, 4, 8, 16)
POOL_GROUP = 128
SSM_GROUP = 16
SSM_STATE = 64
N_SSM_GROUPS = 32
N_STATE = N_SSM_GROUPS * SSM_STATE
QUAD = 256
N_QUAD = N_STATE // QUAD
SLAB = 256
D_SSM = 512
D_POOL = 512
D_FF = 2816
FF_BLK = 1408
HALO = 8
HALO_B = 16
LANES = 128
ADAM_LR, ADAM_B1, ADAM_B2, ADAM_EPS, ADAM_WD, ADAM_STEP = 0.001, 0.9, 0.999, 1e-08, 0.01, 10
VMEM_LIMIT = 56 * 2 ** 20

TL = 512
TM = 1024
TF = 256
TC = 512
SCAN_W = 512


def _cp(*sem):
    return pltpu.CompilerParams(dimension_semantics=sem, vmem_limit_bytes=VMEM_LIMIT)


def _dot_nn(a, b):
    return jnp.dot(a, b, preferred_element_type=F32)


def _dot_nt(a, b):
    return lax.dot_general(a, b, (((1,), (1,)), ((), ())), preferred_element_type=F32)


def _dot_tn(a, b):
    return lax.dot_general(a, b, (((0,), (0,)), ((), ())), preferred_element_type=F32)


def _rms_fwd(x, g):
    inv = lax.rsqrt(jnp.mean(x * x, axis=-1, keepdims=True) + EPS)
    xh = x * inv
    return xh * g, xh, inv


def _rms_bwd(dy, xh, inv, g):
    dg = jnp.sum(dy * xh, axis=0, keepdims=True)
    dxh = dy * g
    dx = inv * (dxh - xh * jnp.mean(dxh * xh, axis=-1, keepdims=True))
    return dx, dg


_GELU_C = 0.7978845608028654
_GELU_A = 0.044715


def _gelu(y):
    t = jnp.tanh(_GELU_C * (y + _GELU_A * (y * y * y)))
    return 0.5 * y * (1.0 + t), t


def _gelu_grad(y, t):
    return 0.5 * (1.0 + t) + 0.5 * y * (1.0 - t * t) * (_GELU_C * (1.0 + 3.0 * _GELU_A * y * y))


def _sigmoid(x):
    return 1.0 / (1.0 + jnp.exp(-x))


def _full(shape):
    n = len(shape)
    return pl.BlockSpec(shape, lambda *_: (0,) * n)


def _fill_ext(ext_ref, prev_ref, cur_ref, next_ref, i, n, rows):
    ext_ref[0:HALO, :] = jnp.where(i > 0, prev_ref[...], 0.0).astype(ext_ref.dtype)
    ext_ref[HALO:HALO + rows, :] = cur_ref[...]
    ext_ref[HALO + rows:2 * HALO + rows, :] = jnp.where(i < n - 1, next_ref[...], 0.0).astype(ext_ref.dtype)


def _in_proj(x, g, w):
    L, D = x.shape
    E = w.shape[1]

    def body(x_ref, g_ref, w_ref, u_ref, xn_ref):
        y, _, _ = _rms_fwd(x_ref[...], g_ref[...])
        yb = y.astype(BF16)
        xn_ref[...] = yb
        u_ref[...] = _dot_nn(yb, w_ref[...])

    return pl.pallas_call(
        body, name="in_proj", grid=(L // TL,),
        in_specs=[pl.BlockSpec((TL, D), lambda i: (i, 0)), _full((1, D)), _full(w.shape)],
        out_specs=[pl.BlockSpec((TL, E), lambda i: (i, 0)), pl.BlockSpec((TL, D), lambda i: (i, 0))],
        out_shape=[jax.ShapeDtypeStruct((L, E), F32), jax.ShapeDtypeStruct((L, D), BF16)],
        compiler_params=_cp("parallel"))(x, g, w)


def _halo_specs_1d(rows, width, L, col):
    rb = rows // HALO
    last = L // HALO - 1
    return [pl.BlockSpec((HALO, width), lambda i: (jnp.maximum(i * rb - 1, 0), col)),
            pl.BlockSpec((rows, width), lambda i: (i, col)),
            pl.BlockSpec((HALO, width), lambda i: (jnp.minimum((i + 1) * rb, last), col))]


def _pooled_from_ext(ext_ref, t0, rows, L):
    t = t0 + lax.broadcasted_iota(jnp.int32, (rows, 1), 0)
    outs = []
    for gi, w in enumerate(POOL_WINDOWS):
        half = w // 2
        cs = slice(gi * POOL_GROUP, (gi + 1) * POOL_GROUP)
        acc = ext_ref[pl.ds(HALO - half, rows), cs]
        for s in range(-half + 1, half):
            acc = acc + ext_ref[pl.ds(HALO + s, rows), cs]
        cnt = (jnp.minimum(t + half, L) - jnp.maximum(t - half, 0)).astype(F32)
        outs.append(acc / cnt - ext_ref[pl.ds(HALO, rows), cs])
    return outs


def _pool_fwd(u, pool_w_b, pool_scale, g_pool):
    L = u.shape[0]
    n = L // TL

    def body(prev_ref, cur_ref, next_ref, pw_ref, ps_ref, g_ref, out_ref, ext_ref):
        i = pl.program_id(0)
        _fill_ext(ext_ref, prev_ref, cur_ref, next_ref, i, n, TL)
        pooled = _pooled_from_ext(ext_ref, i * TL, TL, L)
        ypre = jnp.concatenate([_dot_nn(pooled[gi].astype(BF16), pw_ref[gi]) for gi in range(4)], axis=-1)
        yn, _, _ = _rms_fwd(ypre * ps_ref[...], g_ref[...])
        out_ref[...] = yn.astype(BF16)

    return pl.pallas_call(
        body, name="pool_fwd", grid=(n,),
        in_specs=_halo_specs_1d(TL, D_POOL, L, 0) + [_full(pool_w_b.shape), _full((1, D_POOL)), _full((1, D_POOL))],
        out_specs=pl.BlockSpec((TL, D_POOL), lambda i: (i, 0)),
        out_shape=jax.ShapeDtypeStruct((L, D_POOL), BF16),
        scratch_shapes=[pltpu.VMEM((TL + 2 * HALO, D_POOL), F32)],
        compiler_params=_cp("parallel"))(u, u, u, pool_w_b, pool_scale, g_pool)


def _pool_bwd_local(dh1, u, w_out_b, pool_w_b, pool_scale, g_pool, comm=None):
    L = u.shape[0]
    n = L // TL
    D = dh1.shape[1]

    def body(dh_ref, prev_ref, cur_ref, next_ref, wo_ref, pw_ref, ps_ref, g_ref,
             dp_ref, gpw_ref, gps_ref, gg_ref, ext_ref):
        i = pl.program_id(0)

        @pl.when(i == 0)
        def _():
            gpw_ref[...] = jnp.zeros_like(gpw_ref)
            gps_ref[...] = jnp.zeros_like(gps_ref)
            gg_ref[...] = jnp.zeros_like(gg_ref)

        _fill_ext(ext_ref, prev_ref, cur_ref, next_ref, i, n, TL)
        pooled = [p.astype(BF16) for p in _pooled_from_ext(ext_ref, i * TL, TL, L)]
        ypre = jnp.concatenate([_dot_nn(pooled[gi], pw_ref[gi]) for gi in range(4)], axis=-1)
        ps = ps_ref[...]
        g = g_ref[...]
        _, xh, inv = _rms_fwd(ypre * ps, g)
        d_yn = _dot_nt(dh_ref[...].astype(BF16), wo_ref[...])
        d_y, dg = _rms_bwd(d_yn, xh, inv, g)
        gg_ref[...] += dg
        gps_ref[...] += jnp.sum(d_y * ypre, axis=0, keepdims=True)
        d_ypre = (d_y * ps).astype(BF16)
        for gi in range(4):
            cs = slice(gi * POOL_GROUP, (gi + 1) * POOL_GROUP)
            dp_ref[:, cs] = _dot_nt(d_ypre[:, cs], pw_ref[gi])
            gpw_ref[gi] += _dot_tn(pooled[gi], d_ypre[:, cs])

    return _hosted_call(
        body, comm, name="pool_bwd_local", grid=(n,),
        in_specs=[pl.BlockSpec((TL, D), lambda i: (i, 0))] + _halo_specs_1d(TL, D_POOL, L, 0)
        + [pl.BlockSpec((D_POOL, D), lambda i: (0, 0)), _full(pool_w_b.shape), _full((1, D_POOL)), _full((1, D_POOL))],
        out_specs=[pl.BlockSpec((TL, D_POOL), lambda i: (i, 0)), _full(pool_w_b.shape),
                   _full((1, D_POOL)), _full((1, D_POOL))],
        out_shape=[jax.ShapeDtypeStruct((L, D_POOL), F32), jax.ShapeDtypeStruct(pool_w_b.shape, F32),
                   jax.ShapeDtypeStruct((1, D_POOL), F32), jax.ShapeDtypeStruct((1, D_POOL), F32)],
        scratch_shapes=[pltpu.VMEM((TL + 2 * HALO, D_POOL), F32)],
        args=(dh1, u, u, u, w_out_b, pool_w_b, pool_scale, g_pool))


def _pool_bwd_window(d_pooled):
    L = d_pooled.shape[0]
    n = L // TL
    R = TL + 2 * HALO

    def body(prev_ref, cur_ref, next_ref, out_ref, ext_ref, q_ref):
        i = pl.program_id(0)
        _fill_ext(ext_ref, prev_ref, cur_ref, next_ref, i, n, TL)
        tr = i * TL - HALO + lax.broadcasted_iota(jnp.int32, (R, 1), 0)
        for gi, w in enumerate(POOL_WINDOWS):
            half = w // 2
            cs = slice(gi * POOL_GROUP, (gi + 1) * POOL_GROUP)
            cnt = jnp.maximum(jnp.minimum(tr + half, L) - jnp.maximum(tr - half, 0), 1).astype(F32)
            q_ref[:, cs] = ext_ref[:, cs] / cnt
        for gi, w in enumerate(POOL_WINDOWS):
            half = w // 2
            cs = slice(gi * POOL_GROUP, (gi + 1) * POOL_GROUP)
            acc = q_ref[pl.ds(HALO - half + 1, TL), cs]
            for s in range(-half + 2, half + 1):
                acc = acc + q_ref[pl.ds(HALO + s, TL), cs]
            out_ref[:, cs] = acc - ext_ref[pl.ds(HALO, TL), cs]

    return pl.pallas_call(
        body, name="pool_bwd_window", grid=(n,),
        in_specs=_halo_specs_1d(TL, D_POOL, L, 0),
        out_specs=pl.BlockSpec((TL, D_POOL), lambda i: (i, 0)),
        out_shape=jax.ShapeDtypeStruct((L, D_POOL), F32),
        scratch_shapes=[pltpu.VMEM((R, D_POOL), F32), pltpu.VMEM((R, D_POOL), F32)],
        compiler_params=_cp("parallel"))(d_pooled, d_pooled, d_pooled)


def _ssm_param_fn(lnar, aim, ldt):
    dt = jnp.exp(ldt)
    a_re = -jnp.exp(lnar)
    mag = jnp.exp(a_re * dt)
    ang = aim * dt
    lr, li = mag * jnp.cos(ang), mag * jnp.sin(ang)
    den = a_re * a_re + aim * aim
    fr = ((lr - 1.0) * a_re + li * aim) / den
    fi = (li * a_re - (lr - 1.0) * aim) / den
    return lr, li, fr, fi


def _ssm_params(lnar, aim, ldt):
    def body(a_ref, b_ref, c_ref, lr_ref, li_ref, fr_ref, fi_ref):
        lr, li, fr, fi = _ssm_param_fn(a_ref[...], b_ref[...], c_ref[...])
        lr_ref[...] = lr
        li_ref[...] = li
        fr_ref[...] = fr
        fi_ref[...] = fi

    sh = jax.ShapeDtypeStruct(lnar.shape, F32)
    return pl.pallas_call(body, name="ssm_params", out_shape=[sh] * 4)(lnar, aim, ldt)


def _ssm_params_bwd(lnar, aim, ldt, glr, gli, gfr, gfi):
    def body(a_ref, b_ref, c_ref, g0, g1, g2, g3, da_ref, db_ref, dc_ref):
        _, vjp = jax.vjp(_ssm_param_fn, a_ref[...], b_ref[...], c_ref[...])
        da, db, dc = vjp((g0[...], g1[...], g2[...], g3[...]))
        da_ref[...] = da
        db_ref[...] = db
        dc_ref[...] = jnp.sum(dc, axis=1, keepdims=True)

    return pl.pallas_call(
        body, name="ssm_params_bwd",
        out_shape=[jax.ShapeDtypeStruct(lnar.shape, F32), jax.ShapeDtypeStruct(aim.shape, F32),
                   jax.ShapeDtypeStruct((ldt.shape[0], 1), F32)])(lnar, aim, ldt, glr, gli, gfr, gfi)


def _scan_tables(lam4):
    def build(lr, li, reverse, out_ref, k):
        row = lax.broadcasted_iota(jnp.int32, (8, N_STATE), 0)
        lrb = jnp.broadcast_to(lr, (8, N_STATE))
        lib = jnp.broadcast_to(li, (8, N_STATE))
        pr, pi = lrb, lib
        for s, sh in enumerate((1, 2, 4)):
            mask = (row < 8 - sh) if reverse else (row >= sh)
            out_ref[k, 2 * s] = jnp.where(mask, pr, 0.0)
            out_ref[k, 2 * s + 1] = jnp.where(mask, pi, 0.0)
            pr, pi = pr * pr - pi * pi, 2.0 * pr * pi
        pr, pi = lrb, lib
        p8r = jnp.zeros((8, N_STATE), F32)
        p8i = jnp.zeros((8, N_STATE), F32)
        for j in range(8):
            r = 7 - j if reverse else j
            p8r = jnp.where(row == r, pr, p8r)
            p8i = jnp.where(row == r, pi, p8i)
            pr, pi = pr * lrb - pi * lib, pr * lib + pi * lrb
        out_ref[k, 6] = p8r
        out_ref[k, 7] = p8i

    def body(lam_ref, out_ref):
        l0r, l0i, l1r, l1i = (lam_ref[j:j + 1, :] for j in range(4))
        build(l0r, l0i, False, out_ref, 0)
        build(l0r, -l0i, True, out_ref, 1)
        build(l1r, l1i, True, out_ref, 2)
        build(l1r, -l1i, False, out_ref, 3)

    return pl.pallas_call(body, name="scan_tables",
                          out_shape=jax.ShapeDtypeStruct((4, 8, 8, N_STATE), F32))(lam4)


def _b_block(g):
    q, gl = divmod(g, 4)
    r0, c0 = gl * SSM_STATE, (q % 4) * 4 * SSM_GROUP + gl * SSM_GROUP
    return q, slice(r0, r0 + SSM_STATE), slice(c0, c0 + SSM_GROUP)


def _c_block(g):
    q, rows, cols = _b_block(g)
    return q, cols, rows


def _ssm_expand(b_re, b_im, c_re, c_im, f_re, f_im):
    def body(bre_ref, bim_ref, cre_ref, cim_ref, fre_ref, fim_ref, *rest):
        outs, tmp = rest[:8], rest[8]

        def b_bar_re(d, g):
            return fre_ref[d, g] * bre_ref[d, g] - fim_ref[d, g] * bim_ref[d, g]

        def b_bar_im(d, g):
            return fre_ref[d, g] * bim_ref[d, g] + fim_ref[d, g] * bre_ref[d, g]

        for d in range(2):
            for j, (src, where) in enumerate(((b_bar_re, _b_block), (b_bar_im, _b_block),
                                              (lambda d, g: cre_ref[d, g], _c_block),
                                              (lambda d, g: cim_ref[d, g], _c_block))):
                tmp[...] = jnp.zeros_like(tmp)
                for g in range(N_SSM_GROUPS):
                    q, rows, cols = where(g)
                    tmp[q, rows, cols] = src(d, g)
                outs[4 * d + j][...] = tmp[...].astype(BF16)

    dense = jax.ShapeDtypeStruct((N_QUAD, QUAD, SLAB), BF16)
    return pl.pallas_call(body, name="ssm_expand", out_shape=[dense] * 8,
                          scratch_shapes=[pltpu.VMEM((N_QUAD, QUAD, SLAB), F32)],
                          compiler_params=pltpu.CompilerParams(vmem_limit_bytes=VMEM_LIMIT))(
                              b_re, b_im, c_re, c_im, f_re, f_im)


def _ssm_unfold(gbb_re, gbb_im, b_re_t, b_im_t, f_re, f_im):
    def body(gr_ref, gi_ref, br_ref, bi_ref, fr_ref, fi_ref, obr_ref, obi_ref, ofr_ref, ofi_ref):
        gr, gi, br, bi, fr, fi = (r[...] for r in (gr_ref, gi_ref, br_ref, bi_ref, fr_ref, fi_ref))
        obr_ref[...] = fr * gr + fi * gi
        obi_ref[...] = fr * gi - fi * gr
        ofr_ref[...] = jnp.sum(br * gr + bi * gi, axis=2, keepdims=True)
        ofi_ref[...] = jnp.sum(br * gi - bi * gr, axis=2, keepdims=True)

    gb = jax.ShapeDtypeStruct(gbb_re.shape, F32)
    gf = jax.ShapeDtypeStruct(f_re.shape, F32)
    return pl.pallas_call(body, name="ssm_unfold", out_shape=[gb, gb, gf, gf])(
        gbb_re, gbb_im, b_re_t, b_im_t, f_re, f_im)


def _scan_rows(src_re, src_im, dst_re, dst_im, tab_ref, k, carry_re, carry_im, rows, reverse, s_refs=None):
    ng = rows // 8
    edge = 0 if reverse else 7
    row_id = lax.broadcasted_iota(jnp.int32, (8, SCAN_W), 0)
    sums = []
    for lt in range(N_STATE // SCAN_W):
        sl = slice(lt * SCAN_W, (lt + 1) * SCAN_W)

        def step(r, c, sl=sl):
            tabs = [tab_ref[k, j, :, sl] for j in range(8)]
            cr, ci = c[0], c[1]
            row = pl.multiple_of((ng - 1 - r) * 8 if reverse else r * 8, 8)
            xr = src_re[pl.ds(row, 8), sl]
            xi = src_im[pl.ds(row, 8), sl]
            for s, sh in enumerate((1, 2, 4)):
                amt = 8 - sh if reverse else sh
                rr = pltpu.roll(xr, amt, 0)
                ri = pltpu.roll(xi, amt, 0)
                mr, mi = tabs[2 * s], tabs[2 * s + 1]
                xr, xi = xr + mr * rr - mi * ri, xi + mr * ri + mi * rr
            xr, xi = xr + tabs[6] * cr - tabs[7] * ci, xi + tabs[6] * ci + tabs[7] * cr
            dst_re[pl.ds(row, 8), sl] = xr
            dst_im[pl.ds(row, 8), sl] = xi
            ncr = jnp.broadcast_to(xr[edge:edge + 1, :], (8, SCAN_W))
            nci = jnp.broadcast_to(xi[edge:edge + 1, :], (8, SCAN_W))
            if s_refs is None:
                return ncr, nci
            amt = 7 if reverse else 1
            far = 7 if reverse else 0
            nr = jnp.where(row_id == far, cr, pltpu.roll(xr, amt, 0))
            ni = jnp.where(row_id == far, ci, pltpu.roll(xi, amt, 0))
            sr = s_refs[0][pl.ds(row, 8), sl]
            si = s_refs[1][pl.ds(row, 8), sl]
            return ncr, nci, c[2] + nr * sr + ni * si, c[3] + ni * sr - nr * si

        init = (carry_re[:, sl], carry_im[:, sl])
        if s_refs is not None:
            init = init + (jnp.zeros((8, SCAN_W), F32), jnp.zeros((8, SCAN_W), F32))
        out = lax.fori_loop(0, ng, step, init)
        carry_re[:, sl] = out[0]
        carry_im[:, sl] = out[1]
        if s_refs is not None:
            sums.append((jnp.sum(out[2], axis=0, keepdims=True), jnp.sum(out[3], axis=0, keepdims=True)))
    return sums


def _ssm_scan_fwd(u, b_re, b_im, c_re, c_im, tables, k, reverse, comm=None):
    L = u.shape[0]
    nc = L // TC
    chunk = (lambda i: nc - 1 - i) if reverse else (lambda i: i)

    def body(u_ref, bre_ref, bim_ref, cre_ref, cim_ref, tab_ref,
             y_ref, sre_ref, sim_ref, in_re, in_im, carry_re, carry_im):
        @pl.when(pl.program_id(0) == 0)
        def _():
            carry_re[...] = jnp.zeros_like(carry_re)
            carry_im[...] = jnp.zeros_like(carry_im)

        ub = u_ref[...].astype(BF16)
        for q in range(N_QUAD):
            qs = slice(q * QUAD, (q + 1) * QUAD)
            us = ub[:, (q // 4) * SLAB:(q // 4 + 1) * SLAB]
            in_re[:, qs] = _dot_nt(us, bre_ref[q])
            in_im[:, qs] = _dot_nt(us, bim_ref[q])
        _scan_rows(in_re, in_im, sre_ref, sim_ref, tab_ref, k, carry_re, carry_im, TC, reverse)
        for j in range(D_SSM // SLAB):
            acc = jnp.zeros((TC, SLAB), F32)
            for q in range(4 * j, 4 * j + 4):
                qs = slice(q * QUAD, (q + 1) * QUAD)
                acc = acc + _dot_nt(sre_ref[:, qs].astype(BF16), cre_ref[q])
                acc = acc - _dot_nt(sim_ref[:, qs].astype(BF16), cim_ref[q])
            y_ref[:, j * SLAB:(j + 1) * SLAB] = acc

    return _hosted_call(
        body, comm, name="ssm_scan_rev" if reverse else "ssm_scan_fwd", grid=(nc,),
        in_specs=[pl.BlockSpec((TC, D_SSM), lambda i: (chunk(i), 1))]
        + [_full(b_re.shape)] * 4 + [_full(tables.shape)],
        out_specs=[pl.BlockSpec((TC, D_SSM), lambda i: (chunk(i), 0)),
                   pl.BlockSpec((TC, N_STATE), lambda i: (chunk(i), 0)),
                   pl.BlockSpec((TC, N_STATE), lambda i: (chunk(i), 0))],
        out_shape=[jax.ShapeDtypeStruct((L, D_SSM), F32), jax.ShapeDtypeStruct((L, N_STATE), F32),
                   jax.ShapeDtypeStruct((L, N_STATE), F32)],
        scratch_shapes=[pltpu.VMEM((TC, N_STATE), F32), pltpu.VMEM((TC, N_STATE), F32),
                        pltpu.VMEM((8, N_STATE), F32), pltpu.VMEM((8, N_STATE), F32)],
        args=(u, b_re, b_im, c_re, c_im, tables))


def _quad_channels(q):
    c0 = (q // 4) * SLAB + (q % 4) * 4 * SSM_GROUP
    return slice(c0, c0 + 4 * SSM_GROUP)


def _ssm_scan_bwd(dy, u, s_re, s_im, b_re, b_im, c_re, c_im, tables, k, reverse, comm=None):
    L = u.shape[0]
    nc = L // TC
    chunk = (lambda i: nc - 1 - i) if reverse else (lambda i: i)

    def body(dy_ref, u_ref, sre_ref, sim_ref, bre_ref, bim_ref, cre_ref, cim_ref, tab_ref,
             du_ref, ob_re, ob_im, oc_re, oc_im, gv_ref,
             a_re, a_im, carry_re, carry_im, gbr_ref, gbi_ref, gcr_ref, gci_ref):
        @pl.when(pl.program_id(0) == 0)
        def _():
            carry_re[...] = jnp.zeros_like(carry_re)
            carry_im[...] = jnp.zeros_like(carry_im)
            for r in (gbr_ref, gbi_ref, gcr_ref, gci_ref, gv_ref):
                r[...] = jnp.zeros_like(r)

        dyb = dy_ref[...].astype(BF16)
        ub = u_ref[...].astype(BF16)
        for q in range(N_QUAD):
            qs = slice(q * QUAD, (q + 1) * QUAD)
            ds = dyb[:, (q // 4) * SLAB:(q // 4 + 1) * SLAB]
            a_re[:, qs] = _dot_nn(ds, cre_ref[q])
            a_im[:, qs] = -_dot_nn(ds, cim_ref[q])
            dq = dyb[:, _quad_channels(q)]
            gcr_ref[q] += _dot_tn(dq, sre_ref[:, qs].astype(BF16))
            gci_ref[q] -= _dot_tn(dq, sim_ref[:, qs].astype(BF16))
        sums = _scan_rows(a_re, a_im, a_re, a_im, tab_ref, k, carry_re, carry_im, TC, reverse,
                          s_refs=(sre_ref, sim_ref))
        for lt, (glr, gli) in enumerate(sums):
            sl = slice(lt * SCAN_W, (lt + 1) * SCAN_W)
            gv_ref[0:1, sl] += glr
            gv_ref[1:2, sl] += gli
        for j in range(D_SSM // SLAB):
            us = ub[:, j * SLAB:(j + 1) * SLAB]
            acc = jnp.zeros((TC, SLAB), F32)
            for q in range(4 * j, 4 * j + 4):
                qs = slice(q * QUAD, (q + 1) * QUAD)
                dbr = a_re[:, qs].astype(BF16)
                dbi = a_im[:, qs].astype(BF16)
                uq = ub[:, _quad_channels(q)]
                gbr_ref[q] += _dot_tn(uq, dbr)
                gbi_ref[q] += _dot_tn(uq, dbi)
                acc = acc + _dot_nn(dbr, bre_ref[q]) + _dot_nn(dbi, bim_ref[q])
            du_ref[:, j * SLAB:(j + 1) * SLAB] = acc

        @pl.when(pl.program_id(0) == nc - 1)
        def _():
            for g in range(N_SSM_GROUPS):
                q, gl = divmod(g, 4)
                rows = slice(gl * SSM_GROUP, (gl + 1) * SSM_GROUP)
                cols = slice(gl * SSM_STATE, (gl + 1) * SSM_STATE)
                for out, acc_ref in ((ob_re, gbr_ref), (ob_im, gbi_ref), (oc_re, gcr_ref), (oc_im, gci_ref)):
                    out[g] = acc_ref[q, rows, cols]

    gshape = jax.ShapeDtypeStruct((N_SSM_GROUPS, SSM_GROUP, SSM_STATE), F32)
    compact = pltpu.VMEM((N_QUAD, 4 * SSM_GROUP, QUAD), F32)
    return _hosted_call(
        body, comm, name="ssm_bwd_rev" if reverse else "ssm_bwd_fwd", grid=(nc,),
        in_specs=[pl.BlockSpec((TC, D_SSM), lambda i: (chunk(i), 0)),
                  pl.BlockSpec((TC, D_SSM), lambda i: (chunk(i), 1)),
                  pl.BlockSpec((TC, N_STATE), lambda i: (chunk(i), 0)),
                  pl.BlockSpec((TC, N_STATE), lambda i: (chunk(i), 0))]
        + [_full(b_re.shape)] * 4 + [_full(tables.shape)],
        out_specs=[pl.BlockSpec((TC, D_SSM), lambda i: (chunk(i), 0))] + [_full(gshape.shape)] * 4
        + [_full((2, N_STATE))],
        out_shape=[jax.ShapeDtypeStruct((L, D_SSM), F32), gshape, gshape, gshape, gshape,
                   jax.ShapeDtypeStruct((2, N_STATE), F32)],
        scratch_shapes=[pltpu.VMEM((TC, N_STATE), F32), pltpu.VMEM((TC, N_STATE), F32),
                        pltpu.VMEM((8, N_STATE), F32), pltpu.VMEM((8, N_STATE), F32),
                        compact, compact, compact, compact],
        args=(dy, u, s_re, s_im, b_re, b_im, c_re, c_im, tables))


def _ssm_post(yf, yb, u, d, glu_w, glu_b):
    y = yf + yb + d * u
    z, t = _gelu(y)
    zb = z.astype(BF16)
    gate = _sigmoid(_dot_nn(zb, glu_w) + glu_b)
    return y, z, t, zb, gate


def _mix_out(yn_pool, yf, yb, u, x, ssm_d, glu_w_b, glu_b, g_ssm, w_out_b, g_ffn):
    L, D = x.shape

    def body(ynp_ref, yf_ref, yb_ref, u_ref, x_ref, d_ref, gw_ref, gb_ref, gs_ref, wo_ref, gf_ref,
             h1_ref, hn_ref, ycat_ref):
        _, z, _, _, gate = _ssm_post(yf_ref[...], yb_ref[...], u_ref[...], d_ref[...], gw_ref[...], gb_ref[...])
        yns, _, _ = _rms_fwd(z * gate, gs_ref[...])
        ynsb = yns.astype(BF16)
        ynp = ynp_ref[...]
        ycat_ref[:, 0:D_POOL] = ynp
        ycat_ref[:, D_POOL:D] = ynsb
        h1 = x_ref[...] + _dot_nn(ynp, wo_ref[0:D_POOL, :]) + _dot_nn(ynsb, wo_ref[D_POOL:D, :])
        h1_ref[...] = h1
        hn, _, _ = _rms_fwd(h1, gf_ref[...])
        hn_ref[...] = hn.astype(BF16)

    half = lambda c: pl.BlockSpec((TL, D_SSM), lambda i: (i, c))
    row = pl.BlockSpec((TL, D), lambda i: (i, 0))
    return pl.pallas_call(
        body, name="mix_out", grid=(L // TL,),
        in_specs=[half(0), half(0), half(0), half(1), row, _full((1, D_SSM)), _full(glu_w_b.shape),
                  _full((1, D_SSM)), _full((1, D_SSM)), _full(w_out_b.shape), _full((1, D))],
        out_specs=[row, row, row],
        out_shape=[jax.ShapeDtypeStruct((L, D), F32), jax.ShapeDtypeStruct((L, D), BF16),
                   jax.ShapeDtypeStruct((L, D), BF16)],
        compiler_params=_cp("parallel"))(yn_pool, yf, yb, u, x, ssm_d, glu_w_b, glu_b, g_ssm, w_out_b, g_ffn)


def _ssm_bwd_local(dh1, yf, yb, u, ssm_d, glu_w_b, glu_b, g_ssm, w_out_b, comm=None):
    L, D = dh1.shape

    def body(dh_ref, yf_ref, yb_ref, u_ref, d_ref, gw_ref, gb_ref, gs_ref, wo_ref,
             dy_ref, du_ref, ggw_ref, ggb_ref, gd_ref, ggs_ref):
        @pl.when(pl.program_id(0) == 0)
        def _():
            for r in (ggw_ref, ggb_ref, gd_ref, ggs_ref):
                r[...] = jnp.zeros_like(r)

        u = u_ref[...]
        d = d_ref[...]
        y, z, t, zb, gate = _ssm_post(yf_ref[...], yb_ref[...], u, d, gw_ref[...], gb_ref[...])
        gs = gs_ref[...]
        _, xh, inv = _rms_fwd(z * gate, gs)
        d_yn = _dot_nt(dh_ref[...].astype(BF16), wo_ref[...])
        d_o, dgs = _rms_bwd(d_yn, xh, inv, gs)
        ggs_ref[...] += dgs
        d_zg = d_o * z * gate * (1.0 - gate)
        d_zgb = d_zg.astype(BF16)
        ggb_ref[...] += jnp.sum(d_zg, axis=0, keepdims=True)
        ggw_ref[...] += _dot_tn(zb, d_zgb)
        d_z = d_o * gate + _dot_nt(d_zgb, gw_ref[...])
        d_y = d_z * _gelu_grad(y, t)
        gd_ref[...] += jnp.sum(d_y * u, axis=0, keepdims=True)
        dy_ref[...] = d_y
        du_ref[...] = d_y * d

    half = lambda c: pl.BlockSpec((TL, D_SSM), lambda i: (i, c))
    vec = _full((1, D_SSM))
    return _hosted_call(
        body, comm, name="ssm_bwd_local", grid=(L // TL,),
        in_specs=[pl.BlockSpec((TL, D), lambda i: (i, 0)), half(0), half(0), half(1), vec, _full(glu_w_b.shape),
                  vec, vec, pl.BlockSpec((D_SSM, D), lambda i: (1, 0))],
        out_specs=[half(0), half(0), _full(glu_w_b.shape), vec, vec, vec],
        out_shape=[jax.ShapeDtypeStruct((L, D_SSM), F32), jax.ShapeDtypeStruct((L, D_SSM), F32),
                   jax.ShapeDtypeStruct(glu_w_b.shape, F32)] + [jax.ShapeDtypeStruct((1, D_SSM), F32)] * 3,
        scratch_shapes=[], args=(dh1, yf, yb, u, ssm_d, glu_w_b, glu_b, g_ssm, w_out_b))


def _in_bwd(du_pool, du_a, du_b, du_c, dh1, x, g, w_in_b):
    L, D = x.shape

    def body(p_ref, a_ref, b_ref, c_ref, dh_ref, x_ref, g_ref, w_ref, dx_ref, dub_ref, gg_ref):
        @pl.when(pl.program_id(0) == 0)
        def _():
            gg_ref[...] = jnp.zeros_like(gg_ref)

        dub_ref[:, 0:D_POOL] = p_ref[...].astype(BF16)
        dub_ref[:, D_POOL:D] = (a_ref[...] + b_ref[...] + c_ref[...]).astype(BF16)
        d_xn = _dot_nt(dub_ref[...], w_ref[...])
        gv = g_ref[...]
        _, xh, inv = _rms_fwd(x_ref[...], gv)
        dx, dg = _rms_bwd(d_xn, xh, inv, gv)
        gg_ref[...] += dg
        dx_ref[...] = dh_ref[...] + dx

    half = pl.BlockSpec((TL, D_SSM), lambda i: (i, 0))
    row = pl.BlockSpec((TL, D), lambda i: (i, 0))
    return pl.pallas_call(
        body, name="in_bwd", grid=(L // TL,),
        in_specs=[half, half, half, half, row, row, _full((1, D)), _full(w_in_b.shape)],
        out_specs=[row, row, _full((1, D))],
        out_shape=[jax.ShapeDtypeStruct((L, D), F32), jax.ShapeDtypeStruct((L, D), BF16),
                   jax.ShapeDtypeStruct((1, D), F32)],
        compiler_params=_cp("arbitrary"))(du_pool, du_a, du_b, du_c, dh1, x, g, w_in_b)


def _ffn_up(hn, w_up4):
    L, D = hn.shape

    def body(h_ref, w_ref, o_ref):
        o_ref[...] = _dot_nn(h_ref[...], w_ref[...]).astype(BF16)

    rows = min(TM, L)
    return pl.pallas_call(
        body, name="ffn_up", grid=(4, L // rows),
        in_specs=[pl.BlockSpec((rows, D), lambda j, i: (i, 0)), pl.BlockSpec((None, D, FF_BLK), lambda j, i: (j, 0, 0))],
        out_specs=pl.BlockSpec((rows, FF_BLK), lambda j, i: (i, j)),
        out_shape=jax.ShapeDtypeStruct((L, 4 * FF_BLK), BF16),
        compiler_params=_cp("parallel", "parallel"))(hn, w_up4)


def _halo_specs_2d(rows, width, L, col, order):
    rb = rows // HALO_B
    last = L // HALO_B - 1
    if order == "ik":
        wrap = lambda f: (lambda i, k: f(i, k))
    else:
        wrap = lambda f: (lambda k, i: f(i, k))
    return [pl.BlockSpec((HALO_B, width), wrap(lambda i, k: (jnp.maximum(i * rb - 1, 0), col(k)))),
            pl.BlockSpec((rows, width), wrap(lambda i, k: (i, col(k)))),
            pl.BlockSpec((HALO_B, width), wrap(lambda i, k: (jnp.minimum((i + 1) * rb, last), col(k))))]


def _shift_mats(rows):
    r = lax.broadcasted_iota(jnp.int32, (rows, rows), 0)
    c = lax.broadcasted_iota(jnp.int32, (rows, rows), 1)
    return (c == r - 1).astype(BF16), (c == r + 1).astype(BF16)


def _neighbours(x, prev_ref, next_ref, cs, i, n, mats):
    rows = x.shape[0]
    row = lax.broadcasted_iota(jnp.int32, (rows, 1), 0)
    before = jnp.where(i > 0, prev_ref[:, cs].astype(F32)[HALO_B - 1:HALO_B, :], 0.0)
    after = jnp.where(i < n - 1, next_ref[:, cs].astype(F32)[0:1, :], 0.0)
    if mats is None:
        xf = x.astype(F32)
        down, up = pltpu.roll(xf, 1, 0), pltpu.roll(xf, rows - 1, 0)
    else:
        down, up = _dot_nn(mats[0], x), _dot_nn(mats[1], x)
    return jnp.where(row == 0, before, down), jnp.where(row == rows - 1, after, up)


def _conv3(x, before, after, w, b):
    return before * w[0:1, :] + x.astype(F32) * w[1:2, :] + after * w[2:3, :] + b


def _col_chunks(width, size=256):
    return [slice(c, min(c + size, width)) for c in range(0, width, size)]


def _ffn_down_loss(up, conv_w, conv_b, w_down_b, h1, target, g_final):
    L, D = h1.shape
    n = L // TF
    nk = D_FF // FF_BLK

    def body(vp, vc, vn, gp, gc, gn, wv_ref, wg_ref, bv_ref, bg_ref, wd_ref, h1_ref, t_ref, gf_ref,
             a_ref, cv_ref, cg_ref, dh2_ref, dh2b_ref, loss_ref, gg_ref, acc_ref):
        i = pl.program_id(0)
        k = pl.program_id(1)

        @pl.when((i == 0) & (k == 0))
        def _():
            loss_ref[...] = jnp.zeros_like(loss_ref)
            gg_ref[...] = jnp.zeros_like(gg_ref)

        @pl.when(k == 0)
        def _():
            acc_ref[...] = jnp.zeros_like(acc_ref)

        mats = _shift_mats(TF)
        for cs in _col_chunks(FF_BLK):
            xv, xg = vc[:, cs], gc[:, cs]
            val = _conv3(xv, *_neighbours(xv, vp, vn, cs, i, n, mats), wv_ref[:, cs], bv_ref[:, cs])
            gate = _conv3(xg, *_neighbours(xg, gp, gn, cs, i, n, mats), wg_ref[:, cs], bg_ref[:, cs])
            a_ref[:, cs] = (val * (gate * _sigmoid(gate))).astype(BF16)
            cv_ref[:, cs] = val.astype(BF16)
            cg_ref[:, cs] = gate.astype(BF16)
        acc_ref[...] += _dot_nn(a_ref[...], wd_ref[pl.ds(pl.multiple_of(k * FF_BLK, LANES), FF_BLK), :])

        @pl.when(k == nk - 1)
        def _():
            gf = gf_ref[...]
            y, xh, inv = _rms_fwd(h1_ref[...] + acc_ref[...], gf)
            diff = y - t_ref[...]
            part = 0.5 * jnp.sum(jnp.mean(diff * diff, axis=-1, keepdims=True), axis=0, keepdims=True)
            loss_ref[...] += jnp.broadcast_to(part, loss_ref.shape)
            dx, dg = _rms_bwd(diff * (1.0 / D), xh, inv, gf)
            gg_ref[...] += dg
            dh2_ref[...] = dx
            dh2b_ref[...] = dx.astype(BF16)

    row = pl.BlockSpec((TF, D), lambda i, k: (i, 0))
    cw = lambda off: pl.BlockSpec((3, FF_BLK), lambda i, k: (0, k + off))
    cb = lambda off: pl.BlockSpec((1, FF_BLK), lambda i, k: (0, k + off))
    return pl.pallas_call(
        body, name="ffn_down_loss", grid=(n, nk),
        in_specs=_halo_specs_2d(TF, FF_BLK, L, lambda k: k, "ik") + _halo_specs_2d(TF, FF_BLK, L, lambda k: k + nk, "ik")
        + [cw(0), cw(nk), cb(0), cb(nk), _full(w_down_b.shape), row, row, _full((1, D))],
        out_specs=[pl.BlockSpec((TF, FF_BLK), lambda i, k: (i, k))] * 3 + [row, row, _full((1, LANES)), _full((1, D))],
        out_shape=[jax.ShapeDtypeStruct((L, D_FF), BF16)] * 3
        + [jax.ShapeDtypeStruct((L, D), F32), jax.ShapeDtypeStruct((L, D), BF16),
           jax.ShapeDtypeStruct((1, LANES), F32), jax.ShapeDtypeStruct((1, D), F32)],
        scratch_shapes=[pltpu.VMEM((TF, D), F32)],
        compiler_params=_cp("arbitrary", "arbitrary"))(
            up, up, up, up, up, up, conv_w, conv_w, conv_b, conv_b, w_down_b, h1, target, g_final)


def _ffn_act_bwd(c_val, c_gate, w_down_b, dh2):
    L, D = dh2.shape
    n = L // TL
    nk = D_FF // FF_BLK

    def body(v_ref, g_ref, wd_ref, dh_ref, dv_ref, dg_ref, gbv_ref, gbg_ref):
        @pl.when(pl.program_id(1) == 0)
        def _():
            gbv_ref[...] = jnp.zeros_like(gbv_ref)
            gbg_ref[...] = jnp.zeros_like(gbg_ref)

        dh = dh_ref[...]
        for cs in _col_chunks(FF_BLK):
            val, gate = v_ref[:, cs].astype(F32), g_ref[:, cs].astype(F32)
            d_a = _dot_nt(dh, wd_ref[cs, :])
            sg = _sigmoid(gate)
            d_val = d_a * (gate * sg)
            d_gate = d_a * val * (sg * (1.0 + gate * (1.0 - sg)))
            dv_ref[:, cs] = d_val.astype(BF16)
            dg_ref[:, cs] = d_gate.astype(BF16)
            gbv_ref[:, cs] += jnp.sum(d_val, axis=0, keepdims=True)
            gbg_ref[:, cs] += jnp.sum(d_gate, axis=0, keepdims=True)

    blk = pl.BlockSpec((TL, FF_BLK), lambda k, i: (i, k))
    acc = pl.BlockSpec((1, FF_BLK), lambda k, i: (0, k))
    return pl.pallas_call(
        body, name="ffn_act_bwd", grid=(nk, n),
        in_specs=[blk, blk, pl.BlockSpec((FF_BLK, D), lambda k, i: (k, 0)), pl.BlockSpec((TL, D), lambda k, i: (i, 0))],
        out_specs=[blk, blk, acc, acc],
        out_shape=[jax.ShapeDtypeStruct((L, D_FF), BF16), jax.ShapeDtypeStruct((L, D_FF), BF16),
                   jax.ShapeDtypeStruct((1, D_FF), F32), jax.ShapeDtypeStruct((1, D_FF), F32)],
        compiler_params=_cp("arbitrary", "arbitrary"))(c_val, c_gate, w_down_b, dh2)


def _ffn_up_bwd(d_val, d_gate, up, conv_w, w_up4, h1, dh2, g_ffn):
    L, D = h1.shape
    n = L // TF
    nk = D_FF // FF_BLK

    def body(vp, vc, vn, gp, gc, gn, uv_ref, ug_ref, wv_ref, wg_ref, wu_ref, h1_ref, dh2_ref, g_ref,
             dup_ref, dh1_ref, dh1b_ref, gg_ref, gcw_ref, acc_ref):
        i = pl.program_id(0)
        k = pl.program_id(1)

        @pl.when((i == 0) & (k == 0))
        def _():
            gg_ref[...] = jnp.zeros_like(gg_ref)
            gcw_ref[...] = jnp.zeros_like(gcw_ref)

        @pl.when(k == 0)
        def _():
            acc_ref[...] = jnp.zeros_like(acc_ref)

        acc = jnp.zeros((TF, D), F32)
        for j, (blocks, u_ref, w_ref) in enumerate((((vp, vc, vn), uv_ref, wv_ref), ((gp, gc, gn), ug_ref, wg_ref))):
            for cs in _col_chunks(FF_BLK):
                d = blocks[1][:, cs]
                before, after = _neighbours(d, blocks[0], blocks[2], cs, i, n, None)
                taps = (after, d.astype(F32), before)
                w = w_ref[:, cs]
                d_up = (taps[0] * w[0:1, :] + taps[1] * w[1:2, :] + taps[2] * w[2:3, :]).astype(BF16)
                dup_ref[j, :, cs] = d_up
                acc = acc + _dot_nt(d_up, wu_ref[k + j * nk, :, cs])
                x = u_ref[:, cs].astype(F32)
                for r in range(3):
                    gcw_ref[j, k, r:r + 1, cs] += jnp.sum(taps[r] * x, axis=0, keepdims=True)
        acc_ref[...] += acc

        @pl.when(k == nk - 1)
        def _():
            g = g_ref[...]
            _, xh, inv = _rms_fwd(h1_ref[...], g)
            dx, dg = _rms_bwd(acc_ref[...], xh, inv, g)
            gg_ref[...] += dg
            dh1 = dh2_ref[...] + dx
            dh1_ref[...] = dh1
            dh1b_ref[...] = dh1.astype(BF16)

    row = pl.BlockSpec((TF, D), lambda i, k: (i, 0))
    cw = lambda off: pl.BlockSpec((3, FF_BLK), lambda i, k: (0, k + off))
    tile = lambda off: pl.BlockSpec((TF, FF_BLK), lambda i, k: (i, k + off))
    return pl.pallas_call(
        body, name="ffn_up_bwd", grid=(n, nk),
        in_specs=_halo_specs_2d(TF, FF_BLK, L, lambda k: k, "ik") + _halo_specs_2d(TF, FF_BLK, L, lambda k: k, "ik")
        + [tile(0), tile(nk), cw(0), cw(nk), _full(w_up4.shape), row, row, _full((1, D))],
        out_specs=[pl.BlockSpec((2, None, TF, FF_BLK), lambda i, k: (0, k, i, 0)), row, row, _full((1, D)),
                   _full((2, nk, 3, FF_BLK))],
        out_shape=[jax.ShapeDtypeStruct((2, nk, L, FF_BLK), BF16), jax.ShapeDtypeStruct((L, D), F32),
                   jax.ShapeDtypeStruct((L, D), BF16), jax.ShapeDtypeStruct((1, D), F32),
                   jax.ShapeDtypeStruct((2, nk, 3, FF_BLK), F32)],
        scratch_shapes=[pltpu.VMEM((TF, D), F32)],
        compiler_params=_cp("arbitrary", "arbitrary"))(
            d_val, d_val, d_val, d_gate, d_gate, d_gate, up, up, conv_w, conv_w, w_up4, h1, dh2, g_ffn)


def _matmul_tn(a, b, tm, tn, name, tk=2048):
    L, M = a.shape
    N = b.shape[1]
    tk = min(tk, L)

    def body(a_ref, b_ref, o_ref):
        @pl.when(pl.program_id(2) == 0)
        def _():
            o_ref[...] = jnp.zeros_like(o_ref)

        o_ref[...] += _dot_tn(a_ref[...], b_ref[...])

    return pl.pallas_call(
        body, name=name, grid=(M // tm, N // tn, L // tk),
        in_specs=[pl.BlockSpec((tk, tm), lambda m, n, l: (l, m)), pl.BlockSpec((tk, tn), lambda m, n, l: (l, n))],
        out_specs=pl.BlockSpec((tm, tn), lambda m, n, l: (m, n)),
        out_shape=jax.ShapeDtypeStruct((M, N), F32),
        compiler_params=_cp("parallel", "parallel", "arbitrary"))(a, b)


def _matmul_tn_blocks(a, b, tm, name, tk=2048):
    L, M = a.shape
    J, _, N = b.shape
    tk = min(tk, L)

    def body(a_ref, b_ref, o_ref):
        @pl.when(pl.program_id(2) == 0)
        def _():
            o_ref[...] = jnp.zeros_like(o_ref)

        o_ref[...] += _dot_tn(a_ref[...], b_ref[...])

    return pl.pallas_call(
        body, name=name, grid=(M // tm, J, L // tk),
        in_specs=[pl.BlockSpec((tk, tm), lambda m, j, l: (l, m)), pl.BlockSpec((None, tk, N), lambda m, j, l: (j, l, 0))],
        out_specs=pl.BlockSpec((None, tm, N), lambda m, j, l: (j, m, 0)),
        out_shape=jax.ShapeDtypeStruct((J, M, N), F32),
        compiler_params=_cp("parallel", "parallel", "arbitrary"))(a, b)


def _row_tile(rows):
    for t in (512, 352, 256, 128, 64, 8):
        if rows % t == 0:
            return t
    return rows


def _add_half(g, r, c_arr, name, out_dtype=F32):
    _, _, R, C = g.shape
    tr = _row_tile(R)

    def body(c_ref, g_ref, r_ref, o_ref):
        o_ref[...] = (g_ref[...] + r_ref[...]).astype(out_dtype)

    return pl.pallas_call(
        body, name=name,
        grid_spec=pltpu.PrefetchScalarGridSpec(
            num_scalar_prefetch=1, grid=(g.shape[0], R // tr),
            in_specs=[pl.BlockSpec((None, None, tr, C), lambda j, i, c: (j, c[0], i, 0)),
                      pl.BlockSpec((None, tr, C), lambda j, i, c: (j, i, 0))],
            out_specs=pl.BlockSpec((None, tr, C), lambda j, i, c: (j, i, 0))),
        out_shape=jax.ShapeDtypeStruct(r.shape, out_dtype),
        compiler_params=_cp("parallel", "parallel"))(c_arr, g, r)


def _add2(a, b, name):
    R, C = a.shape
    tr = _row_tile(R)

    def body(a_ref, b_ref, o_ref):
        o_ref[...] = a_ref[...] + b_ref[...]

    spec = pl.BlockSpec((tr, C), lambda i: (i, 0))
    return pl.pallas_call(body, name=name, grid=(R // tr,), in_specs=[spec, spec], out_specs=spec,
                          out_shape=jax.ShapeDtypeStruct(a.shape, F32), compiler_params=_cp("parallel"))(a, b)


def _sum4(p, name):
    _, R, C = p.shape
    tr = _row_tile(R)

    def body(p_ref, o_ref):
        q = [p_ref[j].astype(F32) for j in range(4)]
        o_ref[...] = ((q[0] + q[1]) + q[2]) + q[3]

    return pl.pallas_call(
        body, name=name, grid=(R // tr,),
        in_specs=[pl.BlockSpec((4, tr, C), lambda i: (0, i, 0))],
        out_specs=pl.BlockSpec((tr, C), lambda i: (i, 0)),
        out_shape=jax.ShapeDtypeStruct((R, C), F32), compiler_params=_cp("parallel"))(p)


def _adamw_refs(w_ref, g_ref, m_ref, v_ref, d_ref, nm_ref, nv_ref):
    gv = g_ref[...]
    nm = ADAM_B1 * m_ref[...] + (1.0 - ADAM_B1) * gv
    nv = ADAM_B2 * v_ref[...] + (1.0 - ADAM_B2) * (gv * gv)
    m_hat = nm / (1.0 - ADAM_B1 ** ADAM_STEP)
    v_hat = nv / (1.0 - ADAM_B2 ** ADAM_STEP)
    d_ref[...] = -ADAM_LR * (m_hat / (jnp.sqrt(v_hat) + ADAM_EPS) + ADAM_WD * w_ref[...])
    nm_ref[...] = nm
    nv_ref[...] = nv


def _adamw_many(ws, gs, ms, vs, name):
    n = len(ws)

    def body(*refs):
        for k in range(n):
            _adamw_refs(*(refs[j * n + k] for j in range(7)))

    out_shape = [jax.ShapeDtypeStruct(w.shape, F32) for w in ws] * 3
    res = pl.pallas_call(body, name=name, out_shape=out_shape,
                         compiler_params=pltpu.CompilerParams(vmem_limit_bytes=VMEM_LIMIT))(*ws, *gs, *ms, *vs)
    return res[:n], res[n:2 * n], res[2 * n:]


def _adamw(w, g, m, v, name):
    R, C = w.shape
    tr = _row_tile(R)
    body = lambda *refs: _adamw_refs(*refs)

    spec = pl.BlockSpec((tr, C), lambda i: (i, 0))
    sh = jax.ShapeDtypeStruct((R, C), F32)
    return pl.pallas_call(body, name=name, grid=(R // tr,), in_specs=[spec] * 4, out_specs=[spec] * 3,
                          out_shape=[sh] * 3, compiler_params=_cp("parallel"))(w, g, m, v)


def _join_rows(own, other, c_arr, name):
    R, C = own.shape
    tr = _row_tile(R)

    def body(c_ref, own_ref, other_ref, o_ref):
        o_ref[...] = jnp.where(pl.program_id(0) == c_ref[0], own_ref[...], other_ref[...])

    half = pl.BlockSpec((tr, C), lambda h, i, c: (i, 0))
    return pl.pallas_call(
        body, name=name,
        grid_spec=pltpu.PrefetchScalarGridSpec(
            num_scalar_prefetch=1, grid=(2, R // tr), in_specs=[half, half],
            out_specs=pl.BlockSpec((tr, C), lambda h, i, c: (h * (R // tr) + i, 0))),
        out_shape=jax.ShapeDtypeStruct((2 * R, C), F32),
        compiler_params=_cp("parallel", "parallel"))(c_arr, own, other)


def _adamw_halves(w, own, other, m, v, name, comm=None):
    R, C = own.shape
    tr = _row_tile(R)
    while tr * C * 4 > 2 ** 20 and tr % 16 == 0:
        tr //= 2

    def body(w_ref, own_ref, other_ref, m_ref, v_ref, g_ref, d_ref, nm_ref, nv_ref):
        g_ref[...] = jnp.where(pl.program_id(0) == lax.axis_index("c"), own_ref[...], other_ref[...])
        _adamw_refs(w_ref, g_ref, m_ref, v_ref, d_ref, nm_ref, nv_ref)

    half = pl.BlockSpec((tr, C), lambda h, i: (i, 0))
    full = pl.BlockSpec((tr, C), lambda h, i: (h * (R // tr) + i, 0))
    sh = jax.ShapeDtypeStruct((2 * R, C), F32)
    return _hosted_call(body, comm, name=name, grid=(2, R // tr), in_specs=[full, half, half, full, full],
                        out_specs=[full] * 4, out_shape=[sh] * 4, scratch_shapes=[], args=(w, own, other, m, v))


_ANY = pl.BlockSpec(memory_space=pl.ANY)


def _position():
    return lax.axis_index("x"), lax.axis_index("y"), lax.axis_index("c")


class _Comm:
    def __init__(self, arrs, out_shape, sems, start, finish):
        self.arrs, self.out_shape, self.sems, self.start, self.finish = arrs, out_shape, sems, start, finish


def _comm_call(comm, name):
    n, m = len(comm.arrs), len(comm.out_shape)

    def body(*refs):
        ins, outs, sems = refs[:n], refs[n:n + m], refs[n + m:]
        comm.start(ins, outs, sems)
        comm.finish(ins, outs, sems)

    return pl.pallas_call(
        body, name=name, in_specs=[_ANY] * n, out_specs=[_ANY] * m, out_shape=comm.out_shape,
        scratch_shapes=comm.sems, compiler_params=pltpu.CompilerParams(has_side_effects=True))(*comm.arrs)


def _hosted_call(body, comm, *, name, grid, in_specs, out_specs, out_shape, scratch_shapes, args):
    sem = ("arbitrary",) * len(grid)
    if comm is None:
        return pl.pallas_call(body, name=name, grid=grid, in_specs=in_specs, out_specs=out_specs, out_shape=out_shape,
                              scratch_shapes=scratch_shapes, compiler_params=_cp(*sem))(*args), []
    n_in, n_out, n_scr = len(in_specs), len(out_specs), len(scratch_shapes)
    ci, co = len(comm.arrs), len(comm.out_shape)

    def full(*refs):
        ins, refs = refs[:n_in], refs[n_in:]
        cins, refs = refs[:ci], refs[ci:]
        outs, refs = refs[:n_out], refs[n_out:]
        couts, refs = refs[:co], refs[co:]
        scr, csems = refs[:n_scr], refs[n_scr:]
        first, last = True, True
        for d, size in enumerate(grid):
            first = first & (pl.program_id(d) == 0)
            last = last & (pl.program_id(d) == size - 1)

        @pl.when(first)
        def _():
            comm.start(cins, couts, csems)

        body(*ins, *outs, *scr)

        @pl.when(last)
        def _():
            comm.finish(cins, couts, csems)

    res = pl.pallas_call(
        full, name=name, grid=grid, in_specs=list(in_specs) + [_ANY] * ci, out_specs=list(out_specs) + [_ANY] * co,
        out_shape=list(out_shape) + list(comm.out_shape), scratch_shapes=list(scratch_shapes) + list(comm.sems),
        compiler_params=_cp(*sem))(*args, *comm.arrs)
    return res[:n_out], res[n_out:]


def _comm_join(*comms):
    def parts(xs, attr):
        out, at = [], 0
        for cm in comms:
            n = len(getattr(cm, attr))
            out.append(xs[at:at + n])
            at += n
        return out

    def start(ins, outs, sems):
        for cm, i, o, s in zip(comms, parts(ins, "arrs"), parts(outs, "out_shape"), parts(sems, "sems")):
            cm.start(i, o, s)

    def finish(ins, outs, sems):
        for cm, i, o, s in zip(comms, parts(ins, "arrs"), parts(outs, "out_shape"), parts(sems, "sems")):
            cm.finish(i, o, s)

    cat = lambda attr: [x for cm in comms for x in getattr(cm, attr)]
    return _Comm(cat("arrs"), cat("out_shape"), cat("sems"), start, finish)


def _dma_sems(*counts):
    return [pltpu.SemaphoreType.DMA((n,)) for n in counts]


def _comm_pair_swap(arrs, half=False):
    n = len(arrs)
    out_shape = [jax.ShapeDtypeStruct(a.shape[:1] + a.shape[2:] if half else a.shape, a.dtype) for a in arrs]

    def copies(ins, outs, sems):
        x, y, c = _position()
        return [pltpu.make_async_remote_copy(
            src_ref=ins[k].at[:, 1 - c] if half else ins[k], dst_ref=outs[k], send_sem=sems[0].at[k],
            recv_sem=sems[1].at[k], device_id=(x, y, 1 - c), device_id_type=MESH) for k in range(n)]

    def start(ins, outs, sems):
        for cp in copies(ins, outs, sems):
            cp.start()

    def finish(ins, outs, sems):
        for cp in copies(ins, outs, sems):
            cp.wait()

    return _Comm(arrs, out_shape, _dma_sems(n, n), start, finish)


def _chip_of(j, c):
    return (jnp.right_shift(j, 1), jnp.bitwise_and(j, 1), c)


def _comm_chip_exchange(arrs, scatter):
    n = len(arrs)
    out_shape = [jax.ShapeDtypeStruct(a.shape if scatter else (4,) + a.shape, a.dtype) for a in arrs]

    def copies(ins, outs, sems):
        x, y, c = _position()
        me = 2 * x + y
        local, sent, landed = [], [], []
        for k in range(n):
            local.append(pltpu.make_async_copy(ins[k].at[me] if scatter else ins[k], outs[k].at[me], sems[2].at[k]))
            for d in (1, 2, 3):
                j = jnp.bitwise_xor(me, d)
                s = 3 * k + d - 1
                src = ins[k].at[j] if scatter else ins[k]
                for dst, group in ((outs[k].at[me], sent), (outs[k].at[j], landed)):
                    group.append(pltpu.make_async_remote_copy(
                        src_ref=src, dst_ref=dst, send_sem=sems[0].at[s], recv_sem=sems[1].at[s],
                        device_id=_chip_of(j, c), device_id_type=MESH))
        return local, sent, landed

    def start(ins, outs, sems):
        local, sent, _ = copies(ins, outs, sems)
        for cp in local + sent:
            cp.start()

    def finish(ins, outs, sems):
        local, sent, landed = copies(ins, outs, sems)
        for cp in sent:
            cp.wait_send()
        for cp in landed:
            cp.wait_recv()
        for cp in local:
            cp.wait()

    return _Comm(arrs, out_shape, _dma_sems(3 * n, 3 * n, n), start, finish)


def _comm_pair_gather(arrs):
    n = len(arrs)
    out_shape = [jax.ShapeDtypeStruct((2,) + a.shape, a.dtype) for a in arrs]

    def copies(ins, outs, sems):
        x, y, c = _position()
        local, sent, landed = [], [], []
        for k in range(n):
            local.append(pltpu.make_async_copy(ins[k], outs[k].at[c], sems[2].at[k]))
            for dst, group in ((outs[k].at[c], sent), (outs[k].at[1 - c], landed)):
                group.append(pltpu.make_async_remote_copy(
                    src_ref=ins[k], dst_ref=dst, send_sem=sems[0].at[k], recv_sem=sems[1].at[k],
                    device_id=(x, y, 1 - c), device_id_type=MESH))
        return local, sent, landed

    def start(ins, outs, sems):
        local, sent, _ = copies(ins, outs, sems)
        for cp in local + sent:
            cp.start()

    def finish(ins, outs, sems):
        local, sent, landed = copies(ins, outs, sems)
        for cp in sent:
            cp.wait_send()
        for cp in landed:
            cp.wait_recv()
        for cp in local:
            cp.wait()

    return _Comm(arrs, out_shape, _dma_sems(n, n, n), start, finish)


LOCAL_PARTS = 4


def _comm_gather_split(shards, whole):
    n, nw = len(shards), len(whole)
    arrs = list(shards) + list(whole)
    out_shape = [jax.ShapeDtypeStruct((4,) + a.shape, a.dtype) for a in arrs]

    def copies(ins, outs, sems):
        x, y, c = _position()
        me = 2 * x + y
        local, sent, landed, passed, passed_in = [], [], [], [], []
        for k in range(n + nw):
            if k >= n:
                local.append(pltpu.make_async_copy(ins[k], outs[k].at[me], sems[4].at[LOCAL_PARTS * k]))
            else:
                part = shards[k].shape[0] // LOCAL_PARTS
                for r in range(LOCAL_PARTS):
                    local.append(pltpu.make_async_copy(ins[k].at[pl.ds(r * part, part)],
                                                       outs[k].at[me, pl.ds(r * part, part)],
                                                       sems[4].at[LOCAL_PARTS * k + r]))
            for d in (1, 2, 3):
                j = jnp.bitwise_xor(me, d)
                s = 3 * k + d - 1
                if k >= n:
                    src, mine, theirs = ins[k], outs[k].at[me], outs[k].at[j]
                else:
                    h = shards[k].shape[0] // 2
                    rows = pl.ds(pl.multiple_of(c * h, 16), h)
                    other = pl.ds(pl.multiple_of((1 - c) * h, 16), h)
                    src, mine, theirs = ins[k].at[rows], outs[k].at[me, rows], outs[k].at[j, rows]
                    for dst, group in ((theirs, passed), (outs[k].at[j, other], passed_in)):
                        group.append(pltpu.make_async_remote_copy(
                            src_ref=theirs, dst_ref=dst, send_sem=sems[2].at[s], recv_sem=sems[3].at[s],
                            device_id=(x, y, 1 - c), device_id_type=MESH))
                for dst, group in ((mine, sent), (theirs, landed)):
                    group.append(pltpu.make_async_remote_copy(
                        src_ref=src, dst_ref=dst, send_sem=sems[0].at[s], recv_sem=sems[1].at[s],
                        device_id=_chip_of(j, c), device_id_type=MESH))
        return local, sent, landed, passed, passed_in

    def start(ins, outs, sems):
        local, sent, _, _, _ = copies(ins, outs, sems)
        for cp in local + sent:
            cp.start()

    def finish(ins, outs, sems):
        local, sent, landed, passed, passed_in = copies(ins, outs, sems)
        for cp in landed[:3 * n]:
            cp.wait_recv()
        for cp in passed:
            cp.start()
        for cp in landed[3 * n:]:
            cp.wait_recv()
        for cp in sent:
            cp.wait_send()
        for cp in passed:
            cp.wait_send()
        for cp in passed_in:
            cp.wait_recv()
        for cp in local:
            cp.wait()

    t = 3 * (n + nw)
    return _Comm(arrs, out_shape, _dma_sems(t, t, max(3 * n, 1), max(3 * n, 1), LOCAL_PARTS * (n + nw)), start, finish)


def _pack(arrs, row_multiple):
    parts = []
    for a in arrs:
        flat = a.reshape(-1).astype(F32)
        pad = (-flat.shape[0]) % LANES
        parts.append(jnp.pad(flat, (0, pad)) if pad else flat)
    flat = jnp.concatenate(parts)
    rows = -(-flat.shape[0] // LANES)
    rows_p = -(-rows // row_multiple) * row_multiple
    return jnp.pad(flat, (0, rows_p * LANES - flat.shape[0])).reshape(rows_p, LANES)


def _unpack(packed, shapes):
    flat = packed.reshape(-1)
    outs, off = [], 0
    for sh in shapes:
        size = int(np.prod(sh))
        outs.append(flat[off:off + size].reshape(sh))
        off += size + (-size) % LANES
    return outs


SMALL = ["norm_mix_g", "pool_w", "pool_scale", "ssm_log_neg_a_re", "ssm_a_im", "ssm_log_dt", "ssm_b_re", "ssm_b_im",
         "ssm_c_re", "ssm_c_im", "ssm_d", "glu_b", "out_norm_pool_g", "out_norm_ssm_g", "norm_ffn_g", "conv_b",
         "final_norm_g"]
BIG = ["w_in", "glu_w", "w_out", "w_up", "w_down"]
WIDE = ["pool_w", "ssm_b_re", "ssm_b_im", "ssm_c_re", "ssm_c_im"]
WEIGHTS = ['norm_mix_g', 'w_in', 'pool_w', 'pool_scale', 'ssm_log_neg_a_re', 'ssm_a_im', 'ssm_log_dt', 'ssm_b_re',
           'ssm_b_im', 'ssm_c_re', 'ssm_c_im', 'ssm_d', 'glu_w', 'glu_b', 'out_norm_pool_g', 'out_norm_ssm_g', 'w_out',
           'norm_ffn_g', 'w_up', 'conv_w', 'conv_b', 'w_down', 'final_norm_g']


def _local_step(x, target, p, full, shards=None, c_arr=None):
    L, D = x.shape
    dist = shards is not None
    row = lambda a: a.reshape(1, -1)
    w_in = full["w_in"]
    pool_w_b = p["pool_w"].astype(BF16)
    g_mix, g_pool, g_ssm, g_ffn, g_fin = (row(p[k]) for k in (
        "norm_mix_g", "out_norm_pool_g", "out_norm_ssm_g", "norm_ffn_g", "final_norm_g"))
    pool_scale, ssm_d, glu_b, conv_b = (row(p[k]) for k in ("pool_scale", "ssm_d", "glu_b", "conv_b"))

    lnar = p["ssm_log_neg_a_re"].reshape(2 * N_SSM_GROUPS, SSM_STATE)
    aim = p["ssm_a_im"].reshape(2 * N_SSM_GROUPS, SSM_STATE)
    ldt = jnp.broadcast_to(p["ssm_log_dt"].reshape(2 * N_SSM_GROUPS, 1), lnar.shape)
    lam_re, lam_im, f_re, f_im = _ssm_params(lnar, aim, ldt)
    flat2 = lambda a: a.reshape(2, N_STATE)
    lam4 = jnp.stack([flat2(lam_re)[0], flat2(lam_im)[0], flat2(lam_re)[1], flat2(lam_im)[1]])
    tables = _scan_tables(lam4)
    per_group = (2, N_SSM_GROUPS, SSM_STATE)
    dense = _ssm_expand(p["ssm_b_re"], p["ssm_b_im"], p["ssm_c_re"], p["ssm_c_im"],
                        f_re.reshape(per_group + (1,)), f_im.reshape(per_group + (1,)))
    ssm_args = [tuple(dense[4 * d:4 * d + 4]) + (tables,) for d in range(2)]

    u, xn = _in_proj(x, g_mix, w_in)
    yn_pool = _pool_fwd(u, pool_w_b, pool_scale, g_pool)
    gather1 = _comm_gather_split([shards[k] for k in ("glu_w", "w_out", "w_down")], [shards["conv_w"]]) if dist else None
    (y0, s0r, s0i), got1 = _ssm_scan_fwd(u, *ssm_args[0], 0, False, comm=gather1)
    gather2 = _comm_gather_split([shards["w_up"]], []) if dist else None
    (y1, s1r, s1i), got2 = _ssm_scan_fwd(u, *ssm_args[1], 2, True, comm=gather2)
    if dist:
        glu_w, w_out, w_down = (g.reshape((-1,) + g.shape[2:]) for g in got1[:3])
        conv_w = jnp.transpose(got1[3], (1, 0, 2)).reshape(3, -1)
        w_up4 = got2[0]
    else:
        glu_w, w_out, w_up4, w_down, conv_w = (full[k] for k in ("glu_w", "w_out", "w_up", "w_down", "conv_w"))
    h1, hn, ycat = _mix_out(yn_pool, y0, y1, u, x, ssm_d, glu_w, glu_b, g_ssm, w_out, g_ffn)
    up = _ffn_up(hn, w_up4)
    a, c_val, c_gate, dh2, dh2_b, loss, g_final = _ffn_down_loss(up, conv_w, conv_b, w_down, h1, target, g_fin)

    d_val, d_gate, gbv, gbg = _ffn_act_bwd(c_val, c_gate, w_down, dh2_b)
    g_w_down = _matmul_tn(a, dh2_b, FF_BLK, D, "grad_w_down")
    d_up, dh1, dh1_b, g_ffn_g, gcw = _ffn_up_bwd(d_val, d_gate, up, conv_w, w_up4, h1, dh2, g_ffn)
    g_w_up = _matmul_tn_blocks(hn, d_up.reshape(4, L, FF_BLK), TM, "grad_w_up")
    g_w_out = _matmul_tn(ycat, dh1_b, TM, D, "grad_w_out")
    late = ("w_up", "w_down", "w_out", "glu_w")
    halves = [g_w_up.reshape(4, 2, D // 2, FF_BLK), g_w_down.reshape(4, 2, D_FF // 8, D)]
    (dy, du_direct, g_glu_w, g_glu_b, g_ssm_d, g_ssm_g), swapped = _ssm_bwd_local(
        dh1_b, y0, y1, u, ssm_d, glu_w, glu_b, g_ssm, w_out, comm=_comm_pair_swap(halves, half=True) if dist else None)
    more = [g_w_out.reshape(4, 2, D // 8, D), g_glu_w.reshape(4, 2, D_SSM // 8, D_SSM)]
    (d_pooled, g_pool_w, g_pool_scale, g_pool_g), swapped_more = _pool_bwd_local(
        dh1_b, u, w_out, pool_w_b, pool_scale, g_pool, comm=_comm_pair_swap(more, half=True) if dist else None)
    halves, from_sibling = halves + more, list(swapped) + list(swapped_more)
    du_pool = _pool_bwd_window(d_pooled)
    reduce2 = None
    if dist:
        chip_sums = [_add_half(h, r, c_arr, "sum_pair_" + k, BF16) for k, h, r in zip(late, halves, from_sibling)]
        reduce2 = _comm_chip_exchange(chip_sums, scatter=True)
    (du0, gb0r, gb0i, gc0r, gc0i, gv0), from_chips = _ssm_scan_bwd(dy, u, s0r, s0i, *ssm_args[0], 1, True, comm=reduce2)
    mine = [_sum4(r, "sum_chips_" + k) for k, r in zip(late, from_chips)]
    (du1, gb1r, gb1i, gc1r, gc1i, gv1), theirs = _ssm_scan_bwd(
        dy, u, s1r, s1i, *ssm_args[1], 3, False, comm=_comm_pair_swap(mine) if dist else None)
    by_state = (2, N_SSM_GROUPS, 1, SSM_STATE)
    g_b_re, g_b_im, g_f_re, g_f_im = _ssm_unfold(
        jnp.stack([gb0r, gb1r]), jnp.stack([gb0i, gb1i]),
        jnp.swapaxes(p["ssm_b_re"], 2, 3), jnp.swapaxes(p["ssm_b_im"], 2, 3),
        f_re.reshape(by_state), f_im.reshape(by_state))
    gvec = lambda j: jnp.stack([gv0[j], gv1[j]]).reshape(2 * N_SSM_GROUPS, SSM_STATE)
    g_lnar, g_aim, g_ldt = _ssm_params_bwd(lnar, aim, ldt, gvec(0), gvec(1),
                                           g_f_re.reshape(lnar.shape), g_f_im.reshape(lnar.shape))
    grad_x, d_u_b, g_mix_g = _in_bwd(du_pool, du_direct, du0, du1, dh1, x, g_mix, w_in)
    g_w_in = _matmul_tn(xn, d_u_b, TM, D, "grad_w_in")

    small = {
        "norm_mix_g": g_mix_g, "pool_w": g_pool_w, "pool_scale": g_pool_scale,
        "ssm_log_neg_a_re": g_lnar, "ssm_a_im": g_aim, "ssm_log_dt": g_ldt,
        "ssm_b_re": jnp.swapaxes(g_b_re, 2, 3), "ssm_b_im": jnp.swapaxes(g_b_im, 2, 3),
        "ssm_c_re": jnp.stack([gc0r, gc1r]), "ssm_c_im": jnp.stack([gc0i, gc1i]),
        "ssm_d": g_ssm_d, "glu_b": g_glu_b, "out_norm_pool_g": g_pool_g, "out_norm_ssm_g": g_ssm_g,
        "norm_ffn_g": g_ffn_g, "conv_b": jnp.concatenate([gbv[0], gbg[0]]), "final_norm_g": g_final,
        "conv_w": jnp.transpose(gcw, (2, 0, 1, 3)).reshape(3, -1),
    }
    big = {"w_in": g_w_in}
    reduced = dict(zip(late, zip(mine, theirs)))
    if not dist:
        big.update({"w_up": g_w_up, "w_down": g_w_down, "w_out": g_w_out, "glu_w": g_glu_w})
    return loss, grad_x, small, big, reduced


def kernel(x, norm_mix_g, w_in, pool_w, pool_scale, ssm_log_neg_a_re, ssm_a_im, ssm_log_dt, ssm_b_re, ssm_b_im, ssm_c_re, ssm_c_im, ssm_d, glu_w, glu_b, out_norm_pool_g, out_norm_ssm_g, w_out, norm_ffn_g, w_up, conv_w, conv_b, w_down, final_norm_g, loss_target, m_norm_mix_g, m_w_in, m_pool_w, m_pool_scale, m_ssm_log_neg_a_re, m_ssm_a_im, m_ssm_log_dt, m_ssm_b_re, m_ssm_b_im, m_ssm_c_re, m_ssm_c_im, m_ssm_d, m_glu_w, m_glu_b, m_out_norm_pool_g, m_out_norm_ssm_g, m_w_out, m_norm_ffn_g, m_w_up, m_conv_w, m_conv_b, m_w_down, m_final_norm_g, v_norm_mix_g, v_w_in, v_pool_w, v_pool_scale, v_ssm_log_neg_a_re, v_ssm_a_im, v_ssm_log_dt, v_ssm_b_re, v_ssm_b_im, v_ssm_c_re, v_ssm_c_im, v_ssm_d, v_glu_w, v_glu_b, v_out_norm_pool_g, v_out_norm_ssm_g, v_w_out, v_norm_ffn_g, v_w_up, v_conv_w, v_conv_b, v_w_down, v_final_norm_g):
    args = locals()
    w = {k: args[k] for k in WEIGHTS}
    m = {k: args["m_" + k] for k in WEIGHTS}
    v = {k: args["v_" + k] for k in WEIGHTS}
    chip = 2 * lax.axis_index("x") + lax.axis_index("y")
    c_arr = lax.axis_index("c").astype(jnp.int32).reshape(1)

    shards = {k: w[k].astype(BF16) for k in BIG}
    shards["conv_w"] = conv_w
    w_in_full = _comm_call(_comm_gather_split([shards["w_in"]], []), "gather_w_in")[0]
    loss, grad_x, g_small, g_big, reduced = _local_step(
        x[0], loss_target[0], w, {"w_in": w_in_full.reshape(-1, w_in_full.shape[-1])}, shards, c_arr)

    exact = [k for k in SMALL if k not in WIDE]
    packs = [_pack([loss] + [g_small[k] for k in exact] + [g_small["conv_w"]], 512),
             _pack([g_small[k] for k in WIDE], 512)]
    halves = [g_big["w_in"].reshape(4, 2, g_big["w_in"].shape[0] // 8, -1)]
    halves += [pk.reshape(1, 2, pk.shape[0] // 2, LANES) for pk in packs]
    from_sibling = _comm_call(_comm_pair_swap(halves, half=True), "reduce_pair")
    names = ("w_in", "exact", "wide")
    sums = [_add_half(h, r, c_arr, "sum_pair_" + k, dt)
            for k, h, r, dt in zip(names, halves, from_sibling, (BF16, F32, BF16))]
    grads, delta, new_m, new_v = {}, {}, {}, {}

    def adamw_behind(k, comm):
        own, other = reduced[k]
        (grads[k], delta[k], new_m[k], new_v[k]), got = _adamw_halves(
            w[k], own, other, m[k], v[k], "adamw_" + k, comm=comm)
        return got

    from_chips = adamw_behind("w_up", _comm_join(_comm_chip_exchange(sums[:1], scatter=True),
                                                  _comm_chip_exchange([s[0] for s in sums[1:]], scatter=False)))
    mine = [_sum4(r, "sum_chips_" + k) for k, r in zip(names, from_chips)]
    theirs = adamw_behind("w_down", _comm_pair_swap(mine))
    for k in ("w_out", "glu_w"):
        adamw_behind(k, None)
    exact_all = _join_rows(mine[1], theirs[1], c_arr, "join_exact")
    wide_all = _join_rows(mine[2], theirs[2], c_arr, "join_wide")
    shapes = [loss.shape] + [w[k].shape for k in exact] + [(3, 4 * FF_BLK)]
    grads.update(zip(["loss"] + exact + ["conv_w_full"], _unpack(exact_all, shapes)))
    grads.update(zip(WIDE, _unpack(wide_all, [w[k].shape for k in WIDE])))
    loss = grads.pop("loss")[0, 0]
    grads["conv_w"] = lax.dynamic_slice_in_dim(grads.pop("conv_w_full"), chip * FF_BLK, FF_BLK, axis=1)

    reduced["w_in"] = (mine[0], theirs[0])
    adamw_behind("w_in", None)
    padded = ["ssm_b_re", "ssm_b_im"]
    for keys, name in ((padded, "adamw_ssm_b"), ([k for k in SMALL + ["conv_w"] if k not in padded], "adamw_small")):
        outs = _adamw_many(*([d[k] for k in keys] for d in (w, grads, m, v)), name)
        for d, o in zip((delta, new_m, new_v), outs):
            d.update(zip(keys, o))

    return (loss, grad_x[None], *[grads[k] for k in WEIGHTS], *[delta[k] for k in WEIGHTS],
            *[new_m[k] for k in WEIGHTS], *[new_v[k] for k in WEIGHTS])
```

```python
import numpy as np
import jax
import jax.numpy as jnp
from jax import lax
from jax.experimental import pallas as pl
from jax.experimental.pallas import tpu as pltpu

F32 = jnp.float32
BF16 = jnp.bfloat16
MESH = pl.DeviceIdType.MESH

EPS = 1e-6
POOL_WINDOWS = (2, 4, 8, 16)
POOL_GROUP = 128
SSM_GROUP = 16
SSM_STATE = 64
N_SSM_GROUPS = 32
N_STATE = N_SSM_GROUPS * SSM_STATE
QUAD = 256
N_QUAD = N_STATE // QUAD
SLAB = 256
D_SSM = 512
D_POOL = 512
D_FF = 2816
FF_BLK = 1408
HALO = 8
HALO_B = 16
LANES = 128
ADAM_LR, ADAM_B1, ADAM_B2, ADAM_EPS, ADAM_WD, ADAM_STEP = 0.001, 0.9, 0.999, 1e-08, 0.01, 10
VMEM_LIMIT = 56 * 2 ** 20
ADAMW_BLOCK_BYTES = 2 ** 20

TL = 512
TM = 1024
TF = 256
TC = 512
SCAN_W = 512


def _cp(*sem):
    return pltpu.CompilerParams(dimension_semantics=sem, vmem_limit_bytes=VMEM_LIMIT)


def _dot_nn(a, b):
    return jnp.dot(a, b, preferred_element_type=F32)


def _dot_nt(a, b):
    return lax.dot_general(a, b, (((1,), (1,)), ((), ())), preferred_element_type=F32)


def _dot_tn(a, b):
    return lax.dot_general(a, b, (((0,), (0,)), ((), ())), preferred_element_type=F32)


def _rms_fwd(x, g):
    inv = lax.rsqrt(jnp.mean(x * x, axis=-1, keepdims=True) + EPS)
    xh = x * inv
    return xh * g, xh, inv


def _rms_bwd(dy, xh, inv, g):
    dg = jnp.sum(dy * xh, axis=0, keepdims=True)
    dxh = dy * g
    dx = inv * (dxh - xh * jnp.mean(dxh * xh, axis=-1, keepdims=True))
    return dx, dg


_GELU_C = 0.7978845608028654
_GELU_A = 0.044715


def _gelu(y):
    t = jnp.tanh(_GELU_C * (y + _GELU_A * (y * y * y)))
    return 0.5 * y * (1.0 + t), t


def _gelu_grad(y, t):
    return 0.5 * (1.0 + t) + 0.5 * y * (1.0 - t * t) * (_GELU_C * (1.0 + 3.0 * _GELU_A * y * y))


def _sigmoid(x):
    return 1.0 / (1.0 + jnp.exp(-x))


def _full(shape):
    n = len(shape)
    return pl.BlockSpec(shape, lambda *_: (0,) * n)


def _fill_ext(ext_ref, prev_ref, cur_ref, next_ref, i, n, rows):
    ext_ref[0:HALO, :] = jnp.where(i > 0, prev_ref[...], 0.0).astype(ext_ref.dtype)
    ext_ref[HALO:HALO + rows, :] = cur_ref[...]
    ext_ref[HALO + rows:2 * HALO + rows, :] = jnp.where(i < n - 1, next_ref[...], 0.0).astype(ext_ref.dtype)


def _in_proj(x, g, w):
    L, D = x.shape
    E = w.shape[1]

    def body(x_ref, g_ref, w_ref, u_ref, xn_ref):
        y, _, _ = _rms_fwd(x_ref[...], g_ref[...])
        yb = y.astype(BF16)
        xn_ref[...] = yb
        u_ref[...] = _dot_nn(yb, w_ref[...])

    return pl.pallas_call(
        body, name="in_proj", grid=(L // TL,),
        in_specs=[pl.BlockSpec((TL, D), lambda i: (i, 0)), _full((1, D)), _full(w.shape)],
        out_specs=[pl.BlockSpec((TL, E), lambda i: (i, 0)), pl.BlockSpec((TL, D), lambda i: (i, 0))],
        out_shape=[jax.ShapeDtypeStruct((L, E), F32), jax.ShapeDtypeStruct((L, D), BF16)],
        compiler_params=_cp("parallel"))(x, g, w)


def _halo_specs_1d(rows, width, L, col):
    rb = rows // HALO
    last = L // HALO - 1
    return [pl.BlockSpec((HALO, width), lambda i: (jnp.maximum(i * rb - 1, 0), col)),
            pl.BlockSpec((rows, width), lambda i: (i, col)),
            pl.BlockSpec((HALO, width), lambda i: (jnp.minimum((i + 1) * rb, last), col))]


def _pooled_from_ext(ext_ref, t0, rows, L):
    t = t0 + lax.broadcasted_iota(jnp.int32, (rows, 1), 0)
    outs = []
    for gi, w in enumerate(POOL_WINDOWS):
        half = w // 2
        cs = slice(gi * POOL_GROUP, (gi + 1) * POOL_GROUP)
        acc = ext_ref[pl.ds(HALO - half, rows), cs]
        for s in range(-half + 1, half):
            acc = acc + ext_ref[pl.ds(HALO + s, rows), cs]
        cnt = (jnp.minimum(t + half, L) - jnp.maximum(t - half, 0)).astype(F32)
        outs.append(acc / cnt - ext_ref[pl.ds(HALO, rows), cs])
    return outs


def _pool_fwd(u, pool_w_b, pool_scale, g_pool):
    L = u.shape[0]
    n = L // TL

    def body(prev_ref, cur_ref, next_ref, pw_ref, ps_ref, g_ref, out_ref, ext_ref):
        i = pl.program_id(0)
        _fill_ext(ext_ref, prev_ref, cur_ref, next_ref, i, n, TL)
        pooled = _pooled_from_ext(ext_ref, i * TL, TL, L)
        ypre = jnp.concatenate([_dot_nn(pooled[gi].astype(BF16), pw_ref[gi]) for gi in range(4)], axis=-1)
        yn, _, _ = _rms_fwd(ypre * ps_ref[...], g_ref[...])
        out_ref[...] = yn.astype(BF16)

    return pl.pallas_call(
        body, name="pool_fwd", grid=(n,),
        in_specs=_halo_specs_1d(TL, D_POOL, L, 0) + [_full(pool_w_b.shape), _full((1, D_POOL)), _full((1, D_POOL))],
        out_specs=pl.BlockSpec((TL, D_POOL), lambda i: (i, 0)),
        out_shape=jax.ShapeDtypeStruct((L, D_POOL), BF16),
        scratch_shapes=[pltpu.VMEM((TL + 2 * HALO, D_POOL), F32)],
        compiler_params=_cp("parallel"))(u, u, u, pool_w_b, pool_scale, g_pool)


def _pool_bwd_local(dh1, u, w_out_b, pool_w_b, pool_scale, g_pool, comm=None):
    L = u.shape[0]
    n = L // TL
    D = dh1.shape[1]

    def body(dh_ref, prev_ref, cur_ref, next_ref, wo_ref, pw_ref, ps_ref, g_ref,
             dp_ref, gpw_ref, gps_ref, gg_ref, ext_ref):
        i = pl.program_id(0)

        @pl.when(i == 0)
        def _():
            gpw_ref[...] = jnp.zeros_like(gpw_ref)
            gps_ref[...] = jnp.zeros_like(gps_ref)
            gg_ref[...] = jnp.zeros_like(gg_ref)

        _fill_ext(ext_ref, prev_ref, cur_ref, next_ref, i, n, TL)
        pooled = [p.astype(BF16) for p in _pooled_from_ext(ext_ref, i * TL, TL, L)]
        ypre = jnp.concatenate([_dot_nn(pooled[gi], pw_ref[gi]) for gi in range(4)], axis=-1)
        ps = ps_ref[...]
        g = g_ref[...]
        _, xh, inv = _rms_fwd(ypre * ps, g)
        d_yn = _dot_nt(dh_ref[...], wo_ref[...])
        d_y, dg = _rms_bwd(d_yn, xh, inv, g)
        gg_ref[...] += dg
        gps_ref[...] += jnp.sum(d_y * ypre, axis=0, keepdims=True)
        d_ypre = (d_y * ps).astype(BF16)
        for gi in range(4):
            cs = slice(gi * POOL_GROUP, (gi + 1) * POOL_GROUP)
            dp_ref[:, cs] = _dot_nt(d_ypre[:, cs], pw_ref[gi])
            gpw_ref[gi] += _dot_tn(pooled[gi], d_ypre[:, cs])

    return _hosted_call(
        body, comm, name="pool_bwd_local", grid=(n,),
        in_specs=[pl.BlockSpec((TL, D), lambda i: (i, 0))] + _halo_specs_1d(TL, D_POOL, L, 0)
        + [pl.BlockSpec((D_POOL, D), lambda i: (0, 0)), _full(pool_w_b.shape), _full((1, D_POOL)), _full((1, D_POOL))],
        out_specs=[pl.BlockSpec((TL, D_POOL), lambda i: (i, 0)), _full(pool_w_b.shape),
                   _full((1, D_POOL)), _full((1, D_POOL))],
        out_shape=[jax.ShapeDtypeStruct((L, D_POOL), F32), jax.ShapeDtypeStruct(pool_w_b.shape, F32),
                   jax.ShapeDtypeStruct((1, D_POOL), F32), jax.ShapeDtypeStruct((1, D_POOL), F32)],
        scratch_shapes=[pltpu.VMEM((TL + 2 * HALO, D_POOL), F32)],
        args=(dh1, u, u, u, w_out_b, pool_w_b, pool_scale, g_pool))


def _pool_bwd_window(d_pooled):
    L = d_pooled.shape[0]
    n = L // TL
    R = TL + 2 * HALO

    def body(prev_ref, cur_ref, next_ref, out_ref, ext_ref, q_ref):
        i = pl.program_id(0)
        _fill_ext(ext_ref, prev_ref, cur_ref, next_ref, i, n, TL)
        tr = i * TL - HALO + lax.broadcasted_iota(jnp.int32, (R, 1), 0)
        for gi, w in enumerate(POOL_WINDOWS):
            half = w // 2
            cs = slice(gi * POOL_GROUP, (gi + 1) * POOL_GROUP)
            cnt = jnp.maximum(jnp.minimum(tr + half, L) - jnp.maximum(tr - half, 0), 1).astype(F32)
            q_ref[:, cs] = ext_ref[:, cs] / cnt
        for gi, w in enumerate(POOL_WINDOWS):
            half = w // 2
            cs = slice(gi * POOL_GROUP, (gi + 1) * POOL_GROUP)
            acc = q_ref[pl.ds(HALO - half + 1, TL), cs]
            for s in range(-half + 2, half + 1):
                acc = acc + q_ref[pl.ds(HALO + s, TL), cs]
            out_ref[:, cs] = acc - ext_ref[pl.ds(HALO, TL), cs]

    return pl.pallas_call(
        body, name="pool_bwd_window", grid=(n,),
        in_specs=_halo_specs_1d(TL, D_POOL, L, 0),
        out_specs=pl.BlockSpec((TL, D_POOL), lambda i: (i, 0)),
        out_shape=jax.ShapeDtypeStruct((L, D_POOL), F32),
        scratch_shapes=[pltpu.VMEM((R, D_POOL), F32), pltpu.VMEM((R, D_POOL), F32)],
        compiler_params=_cp("parallel"))(d_pooled, d_pooled, d_pooled)


def _ssm_param_fn(lnar, aim, ldt):
    dt = jnp.exp(ldt)
    a_re = -jnp.exp(lnar)
    mag = jnp.exp(a_re * dt)
    ang = aim * dt
    lr, li = mag * jnp.cos(ang), mag * jnp.sin(ang)
    den = a_re * a_re + aim * aim
    fr = ((lr - 1.0) * a_re + li * aim) / den
    fi = (li * a_re - (lr - 1.0) * aim) / den
    return lr, li, fr, fi


def _ssm_params(lnar, aim, ldt):
    def body(a_ref, b_ref, c_ref, lr_ref, li_ref, fr_ref, fi_ref):
        lr, li, fr, fi = _ssm_param_fn(a_ref[...], b_ref[...], c_ref[...])
        lr_ref[...] = lr
        li_ref[...] = li
        fr_ref[...] = fr
        fi_ref[...] = fi

    sh = jax.ShapeDtypeStruct(lnar.shape, F32)
    return pl.pallas_call(body, name="ssm_params", out_shape=[sh] * 4)(lnar, aim, ldt)


def _ssm_params_bwd(lnar, aim, ldt, glr, gli, gfr, gfi):
    def body(a_ref, b_ref, c_ref, g0, g1, g2, g3, da_ref, db_ref, dc_ref):
        _, vjp = jax.vjp(_ssm_param_fn, a_ref[...], b_ref[...], c_ref[...])
        da, db, dc = vjp((g0[...], g1[...], g2[...], g3[...]))
        da_ref[...] = da
        db_ref[...] = db
        dc_ref[...] = jnp.sum(dc, axis=1, keepdims=True)

    return pl.pallas_call(
        body, name="ssm_params_bwd",
        out_shape=[jax.ShapeDtypeStruct(lnar.shape, F32), jax.ShapeDtypeStruct(aim.shape, F32),
                   jax.ShapeDtypeStruct((ldt.shape[0], 1), F32)])(lnar, aim, ldt, glr, gli, gfr, gfi)


def _scan_tables(lam4):
    def build(lr, li, reverse, out_ref, k):
        row = lax.broadcasted_iota(jnp.int32, (8, N_STATE), 0)
        lrb = jnp.broadcast_to(lr, (8, N_STATE))
        lib = jnp.broadcast_to(li, (8, N_STATE))
        pr, pi = lrb, lib
        for s, sh in enumerate((1, 2, 4)):
            mask = (row < 8 - sh) if reverse else (row >= sh)
            out_ref[k, 2 * s] = jnp.where(mask, pr, 0.0)
            out_ref[k, 2 * s + 1] = jnp.where(mask, pi, 0.0)
            pr, pi = pr * pr - pi * pi, 2.0 * pr * pi
        pr, pi = lrb, lib
        p8r = jnp.zeros((8, N_STATE), F32)
        p8i = jnp.zeros((8, N_STATE), F32)
        for j in range(8):
            r = 7 - j if reverse else j
            p8r = jnp.where(row == r, pr, p8r)
            p8i = jnp.where(row == r, pi, p8i)
            pr, pi = pr * lrb - pi * lib, pr * lib + pi * lrb
        out_ref[k, 6] = p8r
        out_ref[k, 7] = p8i

    def body(lam_ref, out_ref):
        l0r, l0i, l1r, l1i = (lam_ref[j:j + 1, :] for j in range(4))
        build(l0r, l0i, False, out_ref, 0)
        build(l0r, -l0i, True, out_ref, 1)
        build(l1r, l1i, True, out_ref, 2)
        build(l1r, -l1i, False, out_ref, 3)

    return pl.pallas_call(body, name="scan_tables",
                          out_shape=jax.ShapeDtypeStruct((4, 8, 8, N_STATE), F32))(lam4)


def _b_block(g):
    q, gl = divmod(g, 4)
    r0, c0 = gl * SSM_STATE, (q % 4) * 4 * SSM_GROUP + gl * SSM_GROUP
    return q, slice(r0, r0 + SSM_STATE), slice(c0, c0 + SSM_GROUP)


def _c_block(g):
    q, rows, cols = _b_block(g)
    return q, cols, rows


def _ssm_expand(b_re, b_im, c_re, c_im, f_re, f_im):
    def body(bre_ref, bim_ref, cre_ref, cim_ref, fre_ref, fim_ref, *rest):
        outs, tmp, bbr_ref, bbi_ref = rest[:8], rest[8], rest[9], rest[10]
        fr, fi, br, bi = fre_ref[...], fim_ref[...], bre_ref[...], bim_ref[...]
        bbr_ref[...] = fr * br - fi * bi
        bbi_ref[...] = fr * bi + fi * br
        for d in range(2):
            for j, (src, where) in enumerate(((bbr_ref, _b_block), (bbi_ref, _b_block),
                                              (cre_ref, _c_block), (cim_ref, _c_block))):
                tmp[...] = jnp.zeros_like(tmp)
                for g in range(N_SSM_GROUPS):
                    q, rows, cols = where(g)
                    tmp[q, rows, cols] = src[d, g]
                outs[4 * d + j][...] = tmp[...].astype(BF16)

    dense = jax.ShapeDtypeStruct((N_QUAD, QUAD, SLAB), BF16)
    return pl.pallas_call(body, name="ssm_expand", out_shape=[dense] * 8,
                          scratch_shapes=[pltpu.VMEM((N_QUAD, QUAD, SLAB), F32), pltpu.VMEM(b_re.shape, F32),
                                          pltpu.VMEM(b_re.shape, F32)],
                          compiler_params=pltpu.CompilerParams(vmem_limit_bytes=VMEM_LIMIT))(
                              b_re, b_im, c_re, c_im, f_re, f_im)


def _ssm_unfold(gbb_re, gbb_im, b_re_t, b_im_t, f_re, f_im):
    def body(gr_ref, gi_ref, br_ref, bi_ref, fr_ref, fi_ref, obr_ref, obi_ref, ofr_ref, ofi_ref):
        gr, gi, br, bi, fr, fi = (r[...] for r in (gr_ref, gi_ref, br_ref, bi_ref, fr_ref, fi_ref))
        obr_ref[...] = fr * gr + fi * gi
        obi_ref[...] = fr * gi - fi * gr
        ofr_ref[...] = jnp.sum(br * gr + bi * gi, axis=2, keepdims=True)
        ofi_ref[...] = jnp.sum(br * gi - bi * gr, axis=2, keepdims=True)

    gb = jax.ShapeDtypeStruct(gbb_re.shape, F32)
    gf = jax.ShapeDtypeStruct(f_re.shape, F32)
    return pl.pallas_call(body, name="ssm_unfold", out_shape=[gb, gb, gf, gf])(
        gbb_re, gbb_im, b_re_t, b_im_t, f_re, f_im)


def _scan_rows(src_re, src_im, dst_re, dst_im, tab_ref, k, carry_re, carry_im, rows, reverse, s_refs=None):
    ng = rows // 8
    edge = 0 if reverse else 7
    row_id = lax.broadcasted_iota(jnp.int32, (8, SCAN_W), 0)
    sums = []
    for lt in range(N_STATE // SCAN_W):
        sl = slice(lt * SCAN_W, (lt + 1) * SCAN_W)

        def step(r, c, sl=sl):
            tabs = [tab_ref[k, j, :, sl] for j in range(8)]
            cr, ci = c[0], c[1]
            row = pl.multiple_of((ng - 1 - r) * 8 if reverse else r * 8, 8)
            xr = src_re[pl.ds(row, 8), sl]
            xi = src_im[pl.ds(row, 8), sl]
            for s, sh in enumerate((1, 2, 4)):
                amt = 8 - sh if reverse else sh
                rr = pltpu.roll(xr, amt, 0)
                ri = pltpu.roll(xi, amt, 0)
                mr, mi = tabs[2 * s], tabs[2 * s + 1]
                xr, xi = xr + mr * rr - mi * ri, xi + mr * ri + mi * rr
            xr, xi = xr + tabs[6] * cr - tabs[7] * ci, xi + tabs[6] * ci + tabs[7] * cr
            dst_re[pl.ds(row, 8), sl] = xr
            dst_im[pl.ds(row, 8), sl] = xi
            ncr = jnp.broadcast_to(xr[edge:edge + 1, :], (8, SCAN_W))
            nci = jnp.broadcast_to(xi[edge:edge + 1, :], (8, SCAN_W))
            if s_refs is None:
                return ncr, nci
            amt = 7 if reverse else 1
            far = 7 if reverse else 0
            nr = jnp.where(row_id == far, cr, pltpu.roll(xr, amt, 0))
            ni = jnp.where(row_id == far, ci, pltpu.roll(xi, amt, 0))
            sr = s_refs[0][pl.ds(row, 8), sl]
            si = s_refs[1][pl.ds(row, 8), sl]
            return ncr, nci, c[2] + nr * sr + ni * si, c[3] + ni * sr - nr * si

        init = (carry_re[:, sl], carry_im[:, sl])
        if s_refs is not None:
            init = init + (jnp.zeros((8, SCAN_W), F32), jnp.zeros((8, SCAN_W), F32))
        out = lax.fori_loop(0, ng, step, init)
        carry_re[:, sl] = out[0]
        carry_im[:, sl] = out[1]
        if s_refs is not None:
            sums.append((jnp.sum(out[2], axis=0, keepdims=True), jnp.sum(out[3], axis=0, keepdims=True)))
    return sums


def _ssm_scan_fwd(u, b_re, b_im, c_re, c_im, tables, k, reverse, comm=None):
    L = u.shape[0]
    nc = L // TC
    chunk = (lambda i: nc - 1 - i) if reverse else (lambda i: i)

    def body(u_ref, bre_ref, bim_ref, cre_ref, cim_ref, tab_ref,
             y_ref, sre_ref, sim_ref, in_re, in_im, carry_re, carry_im):
        @pl.when(pl.program_id(0) == 0)
        def _():
            carry_re[...] = jnp.zeros_like(carry_re)
            carry_im[...] = jnp.zeros_like(carry_im)

        ub = u_ref[...].astype(BF16)
        for q in range(N_QUAD):
            qs = slice(q * QUAD, (q + 1) * QUAD)
            us = ub[:, (q // 4) * SLAB:(q // 4 + 1) * SLAB]
            in_re[:, qs] = _dot_nt(us, bre_ref[q])
            in_im[:, qs] = _dot_nt(us, bim_ref[q])
        _scan_rows(in_re, in_im, sre_ref, sim_ref, tab_ref, k, carry_re, carry_im, TC, reverse)
        for j in range(D_SSM // SLAB):
            acc = jnp.zeros((TC, SLAB), F32)
            for q in range(4 * j, 4 * j + 4):
                qs = slice(q * QUAD, (q + 1) * QUAD)
                acc = acc + _dot_nt(sre_ref[:, qs].astype(BF16), cre_ref[q])
                acc = acc - _dot_nt(sim_ref[:, qs].astype(BF16), cim_ref[q])
            y_ref[:, j * SLAB:(j + 1) * SLAB] = acc

    return _hosted_call(
        body, comm, name="ssm_scan_rev" if reverse else "ssm_scan_fwd", grid=(nc,),
        in_specs=[pl.BlockSpec((TC, D_SSM), lambda i: (chunk(i), 1))]
        + [_full(b_re.shape)] * 4 + [_full(tables.shape)],
        out_specs=[pl.BlockSpec((TC, D_SSM), lambda i: (chunk(i), 0)),
                   pl.BlockSpec((TC, N_STATE), lambda i: (chunk(i), 0)),
                   pl.BlockSpec((TC, N_STATE), lambda i: (chunk(i), 0))],
        out_shape=[jax.ShapeDtypeStruct((L, D_SSM), F32), jax.ShapeDtypeStruct((L, N_STATE), F32),
                   jax.ShapeDtypeStruct((L, N_STATE), F32)],
        scratch_shapes=[pltpu.VMEM((TC, N_STATE), F32), pltpu.VMEM((TC, N_STATE), F32),
                        pltpu.VMEM((8, N_STATE), F32), pltpu.VMEM((8, N_STATE), F32)],
        args=(u, b_re, b_im, c_re, c_im, tables))


def _quad_channels(q):
    c0 = (q // 4) * SLAB + (q % 4) * 4 * SSM_GROUP
    return slice(c0, c0 + 4 * SSM_GROUP)


def _ssm_scan_bwd(dy, u, s_re, s_im, b_re, b_im, c_re, c_im, tables, k, reverse, comm=None):
    L = u.shape[0]
    nc = L // TC
    chunk = (lambda i: nc - 1 - i) if reverse else (lambda i: i)

    def body(dy_ref, u_ref, sre_ref, sim_ref, bre_ref, bim_ref, cre_ref, cim_ref, tab_ref,
             du_ref, ob_re, ob_im, oc_re, oc_im, gv_ref,
             a_re, a_im, carry_re, carry_im, gbr_ref, gbi_ref, gcr_ref, gci_ref):
        @pl.when(pl.program_id(0) == 0)
        def _():
            carry_re[...] = jnp.zeros_like(carry_re)
            carry_im[...] = jnp.zeros_like(carry_im)
            for r in (gbr_ref, gbi_ref, gcr_ref, gci_ref, gv_ref):
                r[...] = jnp.zeros_like(r)

        dyb = dy_ref[...].astype(BF16)
        ub = u_ref[...].astype(BF16)
        for q in range(N_QUAD):
            qs = slice(q * QUAD, (q + 1) * QUAD)
            ds = dyb[:, (q // 4) * SLAB:(q // 4 + 1) * SLAB]
            a_re[:, qs] = _dot_nn(ds, cre_ref[q])
            a_im[:, qs] = -_dot_nn(ds, cim_ref[q])
            dq = dyb[:, _quad_channels(q)]
            gcr_ref[q] += _dot_tn(dq, sre_ref[:, qs].astype(BF16))
            gci_ref[q] -= _dot_tn(dq, sim_ref[:, qs].astype(BF16))
        sums = _scan_rows(a_re, a_im, a_re, a_im, tab_ref, k, carry_re, carry_im, TC, reverse,
                          s_refs=(sre_ref, sim_ref))
        for lt, (glr, gli) in enumerate(sums):
            sl = slice(lt * SCAN_W, (lt + 1) * SCAN_W)
            gv_ref[0:1, sl] += glr
            gv_ref[1:2, sl] += gli
        for j in range(D_SSM // SLAB):
            us = ub[:, j * SLAB:(j + 1) * SLAB]
            acc = jnp.zeros((TC, SLAB), F32)
            for q in range(4 * j, 4 * j + 4):
                qs = slice(q * QUAD, (q + 1) * QUAD)
                dbr = a_re[:, qs].astype(BF16)
                dbi = a_im[:, qs].astype(BF16)
                uq = ub[:, _quad_channels(q)]
                gbr_ref[q] += _dot_tn(uq, dbr)
                gbi_ref[q] += _dot_tn(uq, dbi)
                acc = acc + _dot_nn(dbr, bre_ref[q]) + _dot_nn(dbi, bim_ref[q])
            du_ref[:, j * SLAB:(j + 1) * SLAB] = acc

        @pl.when(pl.program_id(0) == nc - 1)
        def _():
            for g in range(N_SSM_GROUPS):
                q, gl = divmod(g, 4)
                rows = slice(gl * SSM_GROUP, (gl + 1) * SSM_GROUP)
                cols = slice(gl * SSM_STATE, (gl + 1) * SSM_STATE)
                for out, acc_ref in ((ob_re, gbr_ref), (ob_im, gbi_ref), (oc_re, gcr_ref), (oc_im, gci_ref)):
                    out[g] = acc_ref[q, rows, cols]

    gshape = jax.ShapeDtypeStruct((N_SSM_GROUPS, SSM_GROUP, SSM_STATE), F32)
    compact = pltpu.VMEM((N_QUAD, 4 * SSM_GROUP, QUAD), F32)
    return _hosted_call(
        body, comm, name="ssm_bwd_rev" if reverse else "ssm_bwd_fwd", grid=(nc,),
        in_specs=[pl.BlockSpec((TC, D_SSM), lambda i: (chunk(i), 0)),
                  pl.BlockSpec((TC, D_SSM), lambda i: (chunk(i), 1)),
                  pl.BlockSpec((TC, N_STATE), lambda i: (chunk(i), 0)),
                  pl.BlockSpec((TC, N_STATE), lambda i: (chunk(i), 0))]
        + [_full(b_re.shape)] * 4 + [_full(tables.shape)],
        out_specs=[pl.BlockSpec((TC, D_SSM), lambda i: (chunk(i), 0))] + [_full(gshape.shape)] * 4
        + [_full((2, N_STATE))],
        out_shape=[jax.ShapeDtypeStruct((L, D_SSM), F32), gshape, gshape, gshape, gshape,
                   jax.ShapeDtypeStruct((2, N_STATE), F32)],
        scratch_shapes=[pltpu.VMEM((TC, N_STATE), F32), pltpu.VMEM((TC, N_STATE), F32),
                        pltpu.VMEM((8, N_STATE), F32), pltpu.VMEM((8, N_STATE), F32),
                        compact, compact, compact, compact],
        args=(dy, u, s_re, s_im, b_re, b_im, c_re, c_im, tables))


def _ssm_post(yf, yb, u, d, glu_w, glu_b):
    y = yf + yb + d * u
    z, t = _gelu(y)
    zb = z.astype(BF16)
    gate = _sigmoid(_dot_nn(zb, glu_w) + glu_b)
    return y, z, t, zb, gate


def _mix_out(yn_pool, yf, yb, u, x, ssm_d, glu_w_b, glu_b, g_ssm, w_out_b, g_ffn):
    L, D = x.shape

    def body(ynp_ref, yf_ref, yb_ref, u_ref, x_ref, d_ref, gw_ref, gb_ref, gs_ref, wo_ref, gf_ref,
             h1_ref, hn_ref, ycat_ref):
        _, z, _, _, gate = _ssm_post(yf_ref[...], yb_ref[...], u_ref[...], d_ref[...], gw_ref[...], gb_ref[...])
        yns, _, _ = _rms_fwd(z * gate, gs_ref[...])
        ynsb = yns.astype(BF16)
        ynp = ynp_ref[...]
        ycat_ref[:, 0:D_POOL] = ynp
        ycat_ref[:, D_POOL:D] = ynsb
        h1 = x_ref[...] + _dot_nn(ynp, wo_ref[0:D_POOL, :]) + _dot_nn(ynsb, wo_ref[D_POOL:D, :])
        h1_ref[...] = h1
        hn, _, _ = _rms_fwd(h1, gf_ref[...])
        hn_ref[...] = hn.astype(BF16)

    half = lambda c: pl.BlockSpec((TL, D_SSM), lambda i: (i, c))
    row = pl.BlockSpec((TL, D), lambda i: (i, 0))
    return pl.pallas_call(
        body, name="mix_out", grid=(L // TL,),
        in_specs=[half(0), half(0), half(0), half(1), row, _full((1, D_SSM)), _full(glu_w_b.shape),
                  _full((1, D_SSM)), _full((1, D_SSM)), _full(w_out_b.shape), _full((1, D))],
        out_specs=[row, row, row],
        out_shape=[jax.ShapeDtypeStruct((L, D), F32), jax.ShapeDtypeStruct((L, D), BF16),
                   jax.ShapeDtypeStruct((L, D), BF16)],
        compiler_params=_cp("parallel"))(yn_pool, yf, yb, u, x, ssm_d, glu_w_b, glu_b, g_ssm, w_out_b, g_ffn)


def _ssm_bwd_local(dh1, yf, yb, u, ssm_d, glu_w_b, glu_b, g_ssm, w_out_b, comm=None):
    L, D = dh1.shape

    def body(dh_ref, yf_ref, yb_ref, u_ref, d_ref, gw_ref, gb_ref, gs_ref, wo_ref,
             dy_ref, du_ref, ggw_ref, ggb_ref, gd_ref, ggs_ref):
        @pl.when(pl.program_id(0) == 0)
        def _():
            for r in (ggw_ref, ggb_ref, gd_ref, ggs_ref):
                r[...] = jnp.zeros_like(r)

        u = u_ref[...]
        d = d_ref[...]
        y, z, t, zb, gate = _ssm_post(yf_ref[...], yb_ref[...], u, d, gw_ref[...], gb_ref[...])
        gs = gs_ref[...]
        _, xh, inv = _rms_fwd(z * gate, gs)
        d_yn = _dot_nt(dh_ref[...], wo_ref[...])
        d_o, dgs = _rms_bwd(d_yn, xh, inv, gs)
        ggs_ref[...] += dgs
        d_zg = d_o * z * gate * (1.0 - gate)
        d_zgb = d_zg.astype(BF16)
        ggb_ref[...] += jnp.sum(d_zg, axis=0, keepdims=True)
        ggw_ref[...] += _dot_tn(zb, d_zgb)
        d_z = d_o * gate + _dot_nt(d_zgb, gw_ref[...])
        d_y = d_z * _gelu_grad(y, t)
        gd_ref[...] += jnp.sum(d_y * u, axis=0, keepdims=True)
        dy_ref[...] = d_y
        du_ref[...] = d_y * d

    half = lambda c: pl.BlockSpec((TL, D_SSM), lambda i: (i, c))
    vec = _full((1, D_SSM))
    return _hosted_call(
        body, comm, name="ssm_bwd_local", grid=(L // TL,),
        in_specs=[pl.BlockSpec((TL, D), lambda i: (i, 0)), half(0), half(0), half(1), vec, _full(glu_w_b.shape),
                  vec, vec, pl.BlockSpec((D_SSM, D), lambda i: (1, 0))],
        out_specs=[half(0), half(0), _full(glu_w_b.shape), vec, vec, vec],
        out_shape=[jax.ShapeDtypeStruct((L, D_SSM), F32), jax.ShapeDtypeStruct((L, D_SSM), F32),
                   jax.ShapeDtypeStruct(glu_w_b.shape, F32)] + [jax.ShapeDtypeStruct((1, D_SSM), F32)] * 3,
        scratch_shapes=[], args=(dh1, yf, yb, u, ssm_d, glu_w_b, glu_b, g_ssm, w_out_b))


def _in_bwd(du_pool, du_a, du_b, du_c, dh1, x, g, w_in_b):
    L, D = x.shape

    def body(p_ref, a_ref, b_ref, c_ref, dh_ref, x_ref, g_ref, w_ref, dx_ref, dub_ref, gg_ref):
        @pl.when(pl.program_id(0) == 0)
        def _():
            gg_ref[...] = jnp.zeros_like(gg_ref)

        dub_ref[:, 0:D_POOL] = p_ref[...].astype(BF16)
        dub_ref[:, D_POOL:D] = (a_ref[...] + b_ref[...] + c_ref[...]).astype(BF16)
        d_xn = _dot_nt(dub_ref[...], w_ref[...])
        gv = g_ref[...]
        _, xh, inv = _rms_fwd(x_ref[...], gv)
        dx, dg = _rms_bwd(d_xn, xh, inv, gv)
        gg_ref[...] += dg
        dx_ref[...] = dh_ref[...] + dx

    half = pl.BlockSpec((TL, D_SSM), lambda i: (i, 0))
    row = pl.BlockSpec((TL, D), lambda i: (i, 0))
    return pl.pallas_call(
        body, name="in_bwd", grid=(L // TL,),
        in_specs=[half, half, half, half, row, row, _full((1, D)), _full(w_in_b.shape)],
        out_specs=[row, row, _full((1, D))],
        out_shape=[jax.ShapeDtypeStruct((L, D), F32), jax.ShapeDtypeStruct((L, D), BF16),
                   jax.ShapeDtypeStruct((1, D), F32)],
        compiler_params=_cp("arbitrary"))(du_pool, du_a, du_b, du_c, dh1, x, g, w_in_b)


def _ffn_up(hn, w_up4):
    L, D = hn.shape

    def body(h_ref, w_ref, o_ref):
        o_ref[...] = _dot_nn(h_ref[...], w_ref[...]).astype(BF16)

    rows = min(TM, L)
    return pl.pallas_call(
        body, name="ffn_up", grid=(4, L // rows),
        in_specs=[pl.BlockSpec((rows, D), lambda j, i: (i, 0)), pl.BlockSpec((None, D, FF_BLK), lambda j, i: (j, 0, 0))],
        out_specs=pl.BlockSpec((rows, FF_BLK), lambda j, i: (i, j)),
        out_shape=jax.ShapeDtypeStruct((L, 4 * FF_BLK), BF16),
        compiler_params=_cp("parallel", "parallel"))(hn, w_up4)


def _halo_specs_2d(rows, width, L, col, order):
    rb = rows // HALO_B
    last = L // HALO_B - 1
    if order == "ik":
        wrap = lambda f: (lambda i, k: f(i, k))
    else:
        wrap = lambda f: (lambda k, i: f(i, k))
    return [pl.BlockSpec((HALO_B, width), wrap(lambda i, k: (jnp.maximum(i * rb - 1, 0), col(k)))),
            pl.BlockSpec((rows, width), wrap(lambda i, k: (i, col(k)))),
            pl.BlockSpec((HALO_B, width), wrap(lambda i, k: (jnp.minimum((i + 1) * rb, last), col(k))))]


def _shift_mats(rows):
    r = lax.broadcasted_iota(jnp.int32, (rows, rows), 0)
    c = lax.broadcasted_iota(jnp.int32, (rows, rows), 1)
    return (c == r - 1).astype(BF16), (c == r + 1).astype(BF16)


def _neighbours(x, prev_ref, next_ref, cs, i, n, mats):
    rows = x.shape[0]
    row = lax.broadcasted_iota(jnp.int32, (rows, 1), 0)
    before = jnp.where(i > 0, prev_ref[:, cs].astype(F32)[HALO_B - 1:HALO_B, :], 0.0)
    after = jnp.where(i < n - 1, next_ref[:, cs].astype(F32)[0:1, :], 0.0)
    if mats is None:
        xf = x.astype(F32)
        down, up = pltpu.roll(xf, 1, 0), pltpu.roll(xf, rows - 1, 0)
    else:
        down, up = _dot_nn(mats[0], x), _dot_nn(mats[1], x)
    return jnp.where(row == 0, before, down), jnp.where(row == rows - 1, after, up)


def _conv3(x, before, after, w, b):
    return before * w[0:1, :] + x.astype(F32) * w[1:2, :] + after * w[2:3, :] + b


def _col_chunks(width, size=256):
    return [slice(c, min(c + size, width)) for c in range(0, width, size)]


def _ffn_down_loss(up, conv_w, conv_b, w_down_b, h1, target, g_final):
    L, D = h1.shape
    n = L // TF
    nk = D_FF // FF_BLK

    def body(vp, vc, vn, gp, gc, gn, wv_ref, wg_ref, bv_ref, bg_ref, wd_ref, h1_ref, t_ref, gf_ref,
             a_ref, cv_ref, cg_ref, dh2_ref, dh2b_ref, loss_ref, gg_ref, acc_ref):
        i = pl.program_id(0)
        k = pl.program_id(1)

        @pl.when((i == 0) & (k == 0))
        def _():
            loss_ref[...] = jnp.zeros_like(loss_ref)
            gg_ref[...] = jnp.zeros_like(gg_ref)

        @pl.when(k == 0)
        def _():
            acc_ref[...] = jnp.zeros_like(acc_ref)

        mats = _shift_mats(TF)
        for cs in _col_chunks(FF_BLK):
            xv, xg = vc[:, cs], gc[:, cs]
            val = _conv3(xv, *_neighbours(xv, vp, vn, cs, i, n, mats), wv_ref[:, cs], bv_ref[:, cs])
            gate = _conv3(xg, *_neighbours(xg, gp, gn, cs, i, n, mats), wg_ref[:, cs], bg_ref[:, cs])
            a_ref[:, cs] = (val * (gate * _sigmoid(gate))).astype(BF16)
            cv_ref[:, cs] = val.astype(BF16)
            cg_ref[:, cs] = gate.astype(BF16)
        acc_ref[...] += _dot_nn(a_ref[...], wd_ref[pl.ds(pl.multiple_of(k * FF_BLK, LANES), FF_BLK), :])

        @pl.when(k == nk - 1)
        def _():
            gf = gf_ref[...]
            y, xh, inv = _rms_fwd(h1_ref[...] + acc_ref[...], gf)
            diff = y - t_ref[...]
            part = 0.5 * jnp.sum(jnp.mean(diff * diff, axis=-1, keepdims=True), axis=0, keepdims=True)
            loss_ref[...] += jnp.broadcast_to(part, loss_ref.shape)
            dx, dg = _rms_bwd(diff * (1.0 / D), xh, inv, gf)
            gg_ref[...] += dg
            dh2_ref[...] = dx
            dh2b_ref[...] = dx.astype(BF16)

    row = pl.BlockSpec((TF, D), lambda i, k: (i, 0))
    cw = lambda off: pl.BlockSpec((3, FF_BLK), lambda i, k: (0, k + off))
    cb = lambda off: pl.BlockSpec((1, FF_BLK), lambda i, k: (0, k + off))
    return pl.pallas_call(
        body, name="ffn_down_loss", grid=(n, nk),
        in_specs=_halo_specs_2d(TF, FF_BLK, L, lambda k: k, "ik") + _halo_specs_2d(TF, FF_BLK, L, lambda k: k + nk, "ik")
        + [cw(0), cw(nk), cb(0), cb(nk), _full(w_down_b.shape), row, row, _full((1, D))],
        out_specs=[pl.BlockSpec((TF, FF_BLK), lambda i, k: (i, k))] * 3 + [row, row, _full((1, LANES)), _full((1, D))],
        out_shape=[jax.ShapeDtypeStruct((L, D_FF), BF16)] * 3
        + [jax.ShapeDtypeStruct((L, D), F32), jax.ShapeDtypeStruct((L, D), BF16),
           jax.ShapeDtypeStruct((1, LANES), F32), jax.ShapeDtypeStruct((1, D), F32)],
        scratch_shapes=[pltpu.VMEM((TF, D), F32)],
        compiler_params=_cp("arbitrary", "arbitrary"))(
            up, up, up, up, up, up, conv_w, conv_w, conv_b, conv_b, w_down_b, h1, target, g_final)


def _ffn_act_bwd(c_val, c_gate, w_down_b, dh2):
    L, D = dh2.shape
    n = L // TL
    nk = D_FF // FF_BLK

    def body(v_ref, g_ref, wd_ref, dh_ref, dv_ref, dg_ref, gbv_ref, gbg_ref):
        @pl.when(pl.program_id(1) == 0)
        def _():
            gbv_ref[...] = jnp.zeros_like(gbv_ref)
            gbg_ref[...] = jnp.zeros_like(gbg_ref)

        dh = dh_ref[...]
        for cs in _col_chunks(FF_BLK):
            val, gate = v_ref[:, cs].astype(F32), g_ref[:, cs].astype(F32)
            d_a = _dot_nt(dh, wd_ref[cs, :])
            sg = _sigmoid(gate)
            d_val = d_a * (gate * sg)
            d_gate = d_a * val * (sg * (1.0 + gate * (1.0 - sg)))
            dv_ref[:, cs] = d_val.astype(BF16)
            dg_ref[:, cs] = d_gate.astype(BF16)
            gbv_ref[:, cs] += jnp.sum(d_val, axis=0, keepdims=True)
            gbg_ref[:, cs] += jnp.sum(d_gate, axis=0, keepdims=True)

    blk = pl.BlockSpec((TL, FF_BLK), lambda k, i: (i, k))
    acc = pl.BlockSpec((1, FF_BLK), lambda k, i: (0, k))
    return pl.pallas_call(
        body, name="ffn_act_bwd", grid=(nk, n),
        in_specs=[blk, blk, pl.BlockSpec((FF_BLK, D), lambda k, i: (k, 0)), pl.BlockSpec((TL, D), lambda k, i: (i, 0))],
        out_specs=[blk, blk, acc, acc],
        out_shape=[jax.ShapeDtypeStruct((L, D_FF), BF16), jax.ShapeDtypeStruct((L, D_FF), BF16),
                   jax.ShapeDtypeStruct((1, D_FF), F32), jax.ShapeDtypeStruct((1, D_FF), F32)],
        compiler_params=_cp("arbitrary", "arbitrary"))(c_val, c_gate, w_down_b, dh2)


def _ffn_up_bwd(d_val, d_gate, up, conv_w, w_up4, h1, dh2, g_ffn):
    L, D = h1.shape
    n = L // TF
    nk = D_FF // FF_BLK

    def body(vp, vc, vn, gp, gc, gn, uv_ref, ug_ref, wv_ref, wg_ref, wu_ref, h1_ref, dh2_ref, g_ref,
             dup_ref, dh1_ref, dh1b_ref, gg_ref, gcw_ref, acc_ref):
        i = pl.program_id(0)
        k = pl.program_id(1)

        @pl.when((i == 0) & (k == 0))
        def _():
            gg_ref[...] = jnp.zeros_like(gg_ref)
            gcw_ref[...] = jnp.zeros_like(gcw_ref)

        @pl.when(k == 0)
        def _():
            acc_ref[...] = jnp.zeros_like(acc_ref)

        acc = jnp.zeros((TF, D), F32)
        for j, (blocks, u_ref, w_ref) in enumerate((((vp, vc, vn), uv_ref, wv_ref), ((gp, gc, gn), ug_ref, wg_ref))):
            for cs in _col_chunks(FF_BLK):
                d = blocks[1][:, cs]
                before, after = _neighbours(d, blocks[0], blocks[2], cs, i, n, None)
                taps = (after, d.astype(F32), before)
                w = w_ref[:, cs]
                d_up = (taps[0] * w[0:1, :] + taps[1] * w[1:2, :] + taps[2] * w[2:3, :]).astype(BF16)
                dup_ref[j, :, cs] = d_up
                acc = acc + _dot_nt(d_up, wu_ref[k + j * nk, :, cs])
                x = u_ref[:, cs].astype(F32)
                for r in range(3):
                    gcw_ref[j, k, r:r + 1, cs] += jnp.sum(taps[r] * x, axis=0, keepdims=True)
        acc_ref[...] += acc

        @pl.when(k == nk - 1)
        def _():
            g = g_ref[...]
            _, xh, inv = _rms_fwd(h1_ref[...], g)
            dx, dg = _rms_bwd(acc_ref[...], xh, inv, g)
            gg_ref[...] += dg
            dh1 = dh2_ref[...] + dx
            dh1_ref[...] = dh1
            dh1b_ref[...] = dh1.astype(BF16)

    row = pl.BlockSpec((TF, D), lambda i, k: (i, 0))
    cw = lambda off: pl.BlockSpec((3, FF_BLK), lambda i, k: (0, k + off))
    tile = lambda off: pl.BlockSpec((TF, FF_BLK), lambda i, k: (i, k + off))
    return pl.pallas_call(
        body, name="ffn_up_bwd", grid=(n, nk),
        in_specs=_halo_specs_2d(TF, FF_BLK, L, lambda k: k, "ik") + _halo_specs_2d(TF, FF_BLK, L, lambda k: k, "ik")
        + [tile(0), tile(nk), cw(0), cw(nk), _full(w_up4.shape), row, row, _full((1, D))],
        out_specs=[pl.BlockSpec((2, None, TF, FF_BLK), lambda i, k: (0, k, i, 0)), row, row, _full((1, D)),
                   _full((2, nk, 3, FF_BLK))],
        out_shape=[jax.ShapeDtypeStruct((2, nk, L, FF_BLK), BF16), jax.ShapeDtypeStruct((L, D), F32),
                   jax.ShapeDtypeStruct((L, D), BF16), jax.ShapeDtypeStruct((1, D), F32),
                   jax.ShapeDtypeStruct((2, nk, 3, FF_BLK), F32)],
        scratch_shapes=[pltpu.VMEM((TF, D), F32)],
        compiler_params=_cp("arbitrary", "arbitrary"))(
            d_val, d_val, d_val, d_gate, d_gate, d_gate, up, up, conv_w, conv_w, w_up4, h1, dh2, g_ffn)


def _matmul_tn(a, b, tm, tn, name, tk=2048):
    L, M = a.shape
    N = b.shape[1]
    tk = min(tk, L)

    def body(a_ref, b_ref, o_ref):
        @pl.when(pl.program_id(2) == 0)
        def _():
            o_ref[...] = jnp.zeros_like(o_ref)

        o_ref[...] += _dot_tn(a_ref[...], b_ref[...])

    return pl.pallas_call(
        body, name=name, grid=(M // tm, N // tn, L // tk),
        in_specs=[pl.BlockSpec((tk, tm), lambda m, n, l: (l, m)), pl.BlockSpec((tk, tn), lambda m, n, l: (l, n))],
        out_specs=pl.BlockSpec((tm, tn), lambda m, n, l: (m, n)),
        out_shape=jax.ShapeDtypeStruct((M, N), F32),
        compiler_params=_cp("parallel", "parallel", "arbitrary"))(a, b)


def _matmul_tn_blocks(a, b, tm, name, tk=2048):
    L, M = a.shape
    J, _, N = b.shape
    tk = min(tk, L)

    def body(a_ref, b_ref, o_ref):
        @pl.when(pl.program_id(2) == 0)
        def _():
            o_ref[...] = jnp.zeros_like(o_ref)

        o_ref[...] += _dot_tn(a_ref[...], b_ref[...])

    return pl.pallas_call(
        body, name=name, grid=(M // tm, J, L // tk),
        in_specs=[pl.BlockSpec((tk, tm), lambda m, j, l: (l, m)), pl.BlockSpec((None, tk, N), lambda m, j, l: (j, l, 0))],
        out_specs=pl.BlockSpec((None, tm, N), lambda m, j, l: (j, m, 0)),
        out_shape=jax.ShapeDtypeStruct((J, M, N), F32),
        compiler_params=_cp("parallel", "parallel", "arbitrary"))(a, b)


def _row_tile(rows):
    for t in (512, 352, 256, 128, 64, 8):
        if rows % t == 0:
            return t
    return rows


def _add_half(g, r, c_arr, name, out_dtype=F32):
    _, _, R, C = g.shape
    tr = _row_tile(R)

    def body(c_ref, g_ref, r_ref, o_ref):
        o_ref[...] = (g_ref[...] + r_ref[...]).astype(out_dtype)

    return pl.pallas_call(
        body, name=name,
        grid_spec=pltpu.PrefetchScalarGridSpec(
            num_scalar_prefetch=1, grid=(g.shape[0], R // tr),
            in_specs=[pl.BlockSpec((None, None, tr, C), lambda j, i, c: (j, c[0], i, 0)),
                      pl.BlockSpec((None, tr, C), lambda j, i, c: (j, i, 0))],
            out_specs=pl.BlockSpec((None, tr, C), lambda j, i, c: (j, i, 0))),
        out_shape=jax.ShapeDtypeStruct(r.shape, out_dtype),
        compiler_params=_cp("parallel", "parallel"))(c_arr, g, r)


def _sum4(p, name):
    _, R, C = p.shape
    tr = _row_tile(R)

    def body(p_ref, o_ref):
        q = [p_ref[j].astype(F32) for j in range(4)]
        o_ref[...] = ((q[0] + q[1]) + q[2]) + q[3]

    return pl.pallas_call(
        body, name=name, grid=(R // tr,),
        in_specs=[pl.BlockSpec((4, tr, C), lambda i: (0, i, 0))],
        out_specs=pl.BlockSpec((tr, C), lambda i: (i, 0)),
        out_shape=jax.ShapeDtypeStruct((R, C), F32), compiler_params=_cp("parallel"))(p)


def _adamw_refs(w_ref, g_ref, m_ref, v_ref, d_ref, nm_ref, nv_ref):
    gv = g_ref[...]
    nm = ADAM_B1 * m_ref[...] + (1.0 - ADAM_B1) * gv
    nv = ADAM_B2 * v_ref[...] + (1.0 - ADAM_B2) * (gv * gv)
    m_hat = nm / (1.0 - ADAM_B1 ** ADAM_STEP)
    v_hat = nv / (1.0 - ADAM_B2 ** ADAM_STEP)
    d_ref[...] = -ADAM_LR * (m_hat / (jnp.sqrt(v_hat) + ADAM_EPS) + ADAM_WD * w_ref[...])
    nm_ref[...] = nm
    nv_ref[...] = nv


def _adamw_many(ws, gs, ms, vs, name):
    n = len(ws)

    def body(*refs):
        for k in range(n):
            _adamw_refs(*(refs[j * n + k] for j in range(7)))

    out_shape = [jax.ShapeDtypeStruct(w.shape, F32) for w in ws] * 3
    res = pl.pallas_call(body, name=name, out_shape=out_shape,
                         compiler_params=pltpu.CompilerParams(vmem_limit_bytes=VMEM_LIMIT))(*ws, *gs, *ms, *vs)
    return res[:n], res[n:2 * n], res[2 * n:]


def _join_rows(own, other, c_arr, name):
    R, C = own.shape
    tr = _row_tile(R)

    def body(c_ref, own_ref, other_ref, o_ref):
        o_ref[...] = jnp.where(pl.program_id(0) == c_ref[0], own_ref[...], other_ref[...])

    half = pl.BlockSpec((tr, C), lambda h, i, c: (i, 0))
    return pl.pallas_call(
        body, name=name,
        grid_spec=pltpu.PrefetchScalarGridSpec(
            num_scalar_prefetch=1, grid=(2, R // tr), in_specs=[half, half],
            out_specs=pl.BlockSpec((tr, C), lambda h, i, c: (h * (R // tr) + i, 0))),
        out_shape=jax.ShapeDtypeStruct((2 * R, C), F32),
        compiler_params=_cp("parallel", "parallel"))(c_arr, own, other)


def _adamw_halves(w, own, other, m, v, name, comm=None):
    R, C = own.shape
    tr = _row_tile(R)
    while tr * C * 4 > ADAMW_BLOCK_BYTES and tr % 16 == 0:
        tr //= 2

    def body(w_ref, own_ref, other_ref, m_ref, v_ref, g_ref, d_ref, nm_ref, nv_ref):
        g_ref[...] = jnp.where(pl.program_id(0) == lax.axis_index("c"), own_ref[...], other_ref[...])
        _adamw_refs(w_ref, g_ref, m_ref, v_ref, d_ref, nm_ref, nv_ref)

    half = pl.BlockSpec((tr, C), lambda h, i: (i, 0))
    full = pl.BlockSpec((tr, C), lambda h, i: (h * (R // tr) + i, 0))
    sh = jax.ShapeDtypeStruct((2 * R, C), F32)
    return _hosted_call(body, comm, name=name, grid=(2, R // tr), in_specs=[full, half, half, full, full],
                        out_specs=[full] * 4, out_shape=[sh] * 4, scratch_shapes=[], args=(w, own, other, m, v))


_ANY = pl.BlockSpec(memory_space=pl.ANY)


def _position():
    return lax.axis_index("x"), lax.axis_index("y"), lax.axis_index("c")


class _Comm:
    def __init__(self, arrs, out_shape, sems, start, finish):
        self.arrs, self.out_shape, self.sems, self.start, self.finish = arrs, out_shape, sems, start, finish


def _comm_call(comm, name):
    n, m = len(comm.arrs), len(comm.out_shape)

    def body(*refs):
        ins, outs, sems = refs[:n], refs[n:n + m], refs[n + m:]
        comm.start(ins, outs, sems)
        comm.finish(ins, outs, sems)

    return pl.pallas_call(
        body, name=name, in_specs=[_ANY] * n, out_specs=[_ANY] * m, out_shape=comm.out_shape,
        scratch_shapes=comm.sems, compiler_params=pltpu.CompilerParams(has_side_effects=True))(*comm.arrs)


def _hosted_call(body, comm, *, name, grid, in_specs, out_specs, out_shape, scratch_shapes, args):
    sem = ("arbitrary",) * len(grid)
    if comm is None:
        return pl.pallas_call(body, name=name, grid=grid, in_specs=in_specs, out_specs=out_specs, out_shape=out_shape,
                              scratch_shapes=scratch_shapes, compiler_params=_cp(*sem))(*args), []
    n_in, n_out, n_scr = len(in_specs), len(out_specs), len(scratch_shapes)
    ci, co = len(comm.arrs), len(comm.out_shape)

    def full(*refs):
        ins, refs = refs[:n_in], refs[n_in:]
        cins, refs = refs[:ci], refs[ci:]
        outs, refs = refs[:n_out], refs[n_out:]
        couts, refs = refs[:co], refs[co:]
        scr, csems = refs[:n_scr], refs[n_scr:]
        first, last = True, True
        for d, size in enumerate(grid):
            first = first & (pl.program_id(d) == 0)
            last = last & (pl.program_id(d) == size - 1)

        @pl.when(first)
        def _():
            comm.start(cins, couts, csems)

        body(*ins, *outs, *scr)

        @pl.when(last)
        def _():
            comm.finish(cins, couts, csems)

    res = pl.pallas_call(
        full, name=name, grid=grid, in_specs=list(in_specs) + [_ANY] * ci, out_specs=list(out_specs) + [_ANY] * co,
        out_shape=list(out_shape) + list(comm.out_shape), scratch_shapes=list(scratch_shapes) + list(comm.sems),
        compiler_params=_cp(*sem))(*args, *comm.arrs)
    return res[:n_out], res[n_out:]


def _comm_join(*comms):
    def parts(xs, attr):
        out, at = [], 0
        for cm in comms:
            n = len(getattr(cm, attr))
            out.append(xs[at:at + n])
            at += n
        return out

    def start(ins, outs, sems):
        for cm, i, o, s in zip(comms, parts(ins, "arrs"), parts(outs, "out_shape"), parts(sems, "sems")):
            cm.start(i, o, s)

    def finish(ins, outs, sems):
        for cm, i, o, s in zip(comms, parts(ins, "arrs"), parts(outs, "out_shape"), parts(sems, "sems")):
            cm.finish(i, o, s)

    cat = lambda attr: [x for cm in comms for x in getattr(cm, attr)]
    return _Comm(cat("arrs"), cat("out_shape"), cat("sems"), start, finish)


def _dma_sems(*counts):
    return [pltpu.SemaphoreType.DMA((n,)) for n in counts]


def _comm_pair_swap(arrs, half=False):
    n = len(arrs)
    out_shape = [jax.ShapeDtypeStruct(a.shape[:1] + a.shape[2:] if half else a.shape, a.dtype) for a in arrs]

    def copies(ins, outs, sems):
        x, y, c = _position()
        return [pltpu.make_async_remote_copy(
            src_ref=ins[k].at[:, 1 - c] if half else ins[k], dst_ref=outs[k], send_sem=sems[0].at[k],
            recv_sem=sems[1].at[k], device_id=(x, y, 1 - c), device_id_type=MESH) for k in range(n)]

    def start(ins, outs, sems):
        for cp in copies(ins, outs, sems):
            cp.start()

    def finish(ins, outs, sems):
        for cp in copies(ins, outs, sems):
            cp.wait()

    return _Comm(arrs, out_shape, _dma_sems(n, n), start, finish)


def _chip_of(j, c):
    return (jnp.right_shift(j, 1), jnp.bitwise_and(j, 1), c)


def _comm_chip_exchange(arrs, scatter):
    n = len(arrs)
    out_shape = [jax.ShapeDtypeStruct(a.shape if scatter else (4,) + a.shape, a.dtype) for a in arrs]

    def copies(ins, outs, sems):
        x, y, c = _position()
        me = 2 * x + y
        local, sent, landed = [], [], []
        for k in range(n):
            local.append(pltpu.make_async_copy(ins[k].at[me] if scatter else ins[k], outs[k].at[me], sems[2].at[k]))
            for d in (1, 2, 3):
                j = jnp.bitwise_xor(me, d)
                s = 3 * k + d - 1
                src = ins[k].at[j] if scatter else ins[k]
                for dst, group in ((outs[k].at[me], sent), (outs[k].at[j], landed)):
                    group.append(pltpu.make_async_remote_copy(
                        src_ref=src, dst_ref=dst, send_sem=sems[0].at[s], recv_sem=sems[1].at[s],
                        device_id=_chip_of(j, c), device_id_type=MESH))
        return local, sent, landed

    def start(ins, outs, sems):
        local, sent, _ = copies(ins, outs, sems)
        for cp in local + sent:
            cp.start()

    def finish(ins, outs, sems):
        local, sent, landed = copies(ins, outs, sems)
        for cp in sent:
            cp.wait_send()
        for cp in landed:
            cp.wait_recv()
        for cp in local:
            cp.wait()

    return _Comm(arrs, out_shape, _dma_sems(3 * n, 3 * n, n), start, finish)


LOCAL_PARTS = 4


def _comm_gather_split(shards, whole):
    n, nw = len(shards), len(whole)
    arrs = list(shards) + list(whole)
    out_shape = [jax.ShapeDtypeStruct((4,) + a.shape, a.dtype) for a in arrs]

    def copies(ins, outs, sems):
        x, y, c = _position()
        me = 2 * x + y
        local, sent, landed, passed, passed_in = [], [], [], [], []
        for k in range(n + nw):
            if k >= n:
                local.append(pltpu.make_async_copy(ins[k], outs[k].at[me], sems[4].at[LOCAL_PARTS * k]))
            else:
                part = shards[k].shape[0] // LOCAL_PARTS
                for r in range(LOCAL_PARTS):
                    local.append(pltpu.make_async_copy(ins[k].at[pl.ds(r * part, part)],
                                                       outs[k].at[me, pl.ds(r * part, part)],
                                                       sems[4].at[LOCAL_PARTS * k + r]))
            for d in (1, 2, 3):
                j = jnp.bitwise_xor(me, d)
                s = 3 * k + d - 1
                if k >= n:
                    src, mine, theirs = ins[k], outs[k].at[me], outs[k].at[j]
                else:
                    h = shards[k].shape[0] // 2
                    rows = pl.ds(pl.multiple_of(c * h, 16), h)
                    other = pl.ds(pl.multiple_of((1 - c) * h, 16), h)
                    src, mine, theirs = ins[k].at[rows], outs[k].at[me, rows], outs[k].at[j, rows]
                    for dst, group in ((theirs, passed), (outs[k].at[j, other], passed_in)):
                        group.append(pltpu.make_async_remote_copy(
                            src_ref=theirs, dst_ref=dst, send_sem=sems[2].at[s], recv_sem=sems[3].at[s],
                            device_id=(x, y, 1 - c), device_id_type=MESH))
                for dst, group in ((mine, sent), (theirs, landed)):
                    group.append(pltpu.make_async_remote_copy(
                        src_ref=src, dst_ref=dst, send_sem=sems[0].at[s], recv_sem=sems[1].at[s],
                        device_id=_chip_of(j, c), device_id_type=MESH))
        return local, sent, landed, passed, passed_in

    def start(ins, outs, sems):
        local, sent, _, _, _ = copies(ins, outs, sems)
        for cp in local + sent:
            cp.start()

    def finish(ins, outs, sems):
        local, sent, landed, passed, passed_in = copies(ins, outs, sems)
        for cp in landed[:3 * n]:
            cp.wait_recv()
        for cp in passed:
            cp.start()
        for cp in landed[3 * n:]:
            cp.wait_recv()
        for cp in sent:
            cp.wait_send()
        for cp in passed:
            cp.wait_send()
        for cp in passed_in:
            cp.wait_recv()
        for cp in local:
            cp.wait()

    t = 3 * (n + nw)
    return _Comm(arrs, out_shape, _dma_sems(t, t, max(3 * n, 1), max(3 * n, 1), LOCAL_PARTS * (n + nw)), start, finish)


def _pack(arrs, row_multiple):
    parts = []
    for a in arrs:
        flat = a.reshape(-1).astype(F32)
        pad = (-flat.shape[0]) % LANES
        parts.append(jnp.pad(flat, (0, pad)) if pad else flat)
    flat = jnp.concatenate(parts)
    rows = -(-flat.shape[0] // LANES)
    rows_p = -(-rows // row_multiple) * row_multiple
    return jnp.pad(flat, (0, rows_p * LANES - flat.shape[0])).reshape(rows_p, LANES)


def _unpack(packed, shapes):
    flat = packed.reshape(-1)
    outs, off = [], 0
    for sh in shapes:
        size = int(np.prod(sh))
        outs.append(flat[off:off + size].reshape(sh))
        off += size + (-size) % LANES
    return outs


SMALL = ["norm_mix_g", "pool_w", "pool_scale", "ssm_log_neg_a_re", "ssm_a_im", "ssm_log_dt", "ssm_b_re", "ssm_b_im",
         "ssm_c_re", "ssm_c_im", "ssm_d", "glu_b", "out_norm_pool_g", "out_norm_ssm_g", "norm_ffn_g", "conv_b",
         "final_norm_g"]
BIG = ["w_in", "glu_w", "w_out", "w_up", "w_down"]
WIDE = ["pool_w", "ssm_b_re", "ssm_b_im", "ssm_c_re", "ssm_c_im"]
WEIGHTS = ['norm_mix_g', 'w_in', 'pool_w', 'pool_scale', 'ssm_log_neg_a_re', 'ssm_a_im', 'ssm_log_dt', 'ssm_b_re',
           'ssm_b_im', 'ssm_c_re', 'ssm_c_im', 'ssm_d', 'glu_w', 'glu_b', 'out_norm_pool_g', 'out_norm_ssm_g', 'w_out',
           'norm_ffn_g', 'w_up', 'conv_w', 'conv_b', 'w_down', 'final_norm_g']


def _local_step(x, target, p, full, shards=None, c_arr=None):
    L, D = x.shape
    dist = shards is not None
    row = lambda a: a.reshape(1, -1)
    w_in = full["w_in"]
    pool_w_b = p["pool_w"].astype(BF16)
    g_mix, g_pool, g_ssm, g_ffn, g_fin = (row(p[k]) for k in (
        "norm_mix_g", "out_norm_pool_g", "out_norm_ssm_g", "norm_ffn_g", "final_norm_g"))
    pool_scale, ssm_d, glu_b, conv_b = (row(p[k]) for k in ("pool_scale", "ssm_d", "glu_b", "conv_b"))

    lnar = p["ssm_log_neg_a_re"].reshape(2 * N_SSM_GROUPS, SSM_STATE)
    aim = p["ssm_a_im"].reshape(2 * N_SSM_GROUPS, SSM_STATE)
    ldt = jnp.broadcast_to(p["ssm_log_dt"].reshape(2 * N_SSM_GROUPS, 1), lnar.shape)
    lam_re, lam_im, f_re, f_im = _ssm_params(lnar, aim, ldt)
    flat2 = lambda a: a.reshape(2, N_STATE)
    lam4 = jnp.stack([flat2(lam_re)[0], flat2(lam_im)[0], flat2(lam_re)[1], flat2(lam_im)[1]])
    tables = _scan_tables(lam4)
    per_group = (2, N_SSM_GROUPS, SSM_STATE)
    dense = _ssm_expand(p["ssm_b_re"], p["ssm_b_im"], p["ssm_c_re"], p["ssm_c_im"],
                        f_re.reshape(per_group + (1,)), f_im.reshape(per_group + (1,)))
    ssm_args = [tuple(dense[4 * d:4 * d + 4]) + (tables,) for d in range(2)]

    u, xn = _in_proj(x, g_mix, w_in)
    yn_pool = _pool_fwd(u, pool_w_b, pool_scale, g_pool)
    gather1 = _comm_gather_split([shards[k] for k in ("glu_w", "w_out", "w_down")], [shards["conv_w"]]) if dist else None
    (y0, s0r, s0i), got1 = _ssm_scan_fwd(u, *ssm_args[0], 0, False, comm=gather1)
    gather2 = _comm_gather_split([shards["w_up"]], []) if dist else None
    (y1, s1r, s1i), got2 = _ssm_scan_fwd(u, *ssm_args[1], 2, True, comm=gather2)
    if dist:
        glu_w, w_out, w_down = (g.reshape((-1,) + g.shape[2:]) for g in got1[:3])
        conv_w = jnp.transpose(got1[3], (1, 0, 2)).reshape(3, -1)
        w_up4 = got2[0]
    else:
        glu_w, w_out, w_up4, w_down, conv_w = (full[k] for k in ("glu_w", "w_out", "w_up", "w_down", "conv_w"))
    h1, hn, ycat = _mix_out(yn_pool, y0, y1, u, x, ssm_d, glu_w, glu_b, g_ssm, w_out, g_ffn)
    up = _ffn_up(hn, w_up4)
    a, c_val, c_gate, dh2, dh2_b, loss, g_final = _ffn_down_loss(up, conv_w, conv_b, w_down, h1, target, g_fin)

    d_val, d_gate, gbv, gbg = _ffn_act_bwd(c_val, c_gate, w_down, dh2_b)
    g_w_down = _matmul_tn(a, dh2_b, FF_BLK, D, "grad_w_down")
    d_up, dh1, dh1_b, g_ffn_g, gcw = _ffn_up_bwd(d_val, d_gate, up, conv_w, w_up4, h1, dh2, g_ffn)
    g_w_up = _matmul_tn_blocks(hn, d_up.reshape(4, L, FF_BLK), TM, "grad_w_up")
    g_w_out = _matmul_tn(ycat, dh1_b, TM, D, "grad_w_out")
    late = ("w_up", "w_down", "w_out", "glu_w")
    halves = [g_w_up.reshape(4, 2, D // 2, FF_BLK), g_w_down.reshape(4, 2, D_FF // 8, D)]
    (dy, du_direct, g_glu_w, g_glu_b, g_ssm_d, g_ssm_g), swapped = _ssm_bwd_local(
        dh1_b, y0, y1, u, ssm_d, glu_w, glu_b, g_ssm, w_out, comm=_comm_pair_swap(halves, half=True) if dist else None)
    more = [g_w_out.reshape(4, 2, D // 8, D), g_glu_w.reshape(4, 2, D_SSM // 8, D_SSM)]
    (d_pooled, g_pool_w, g_pool_scale, g_pool_g), swapped_more = _pool_bwd_local(
        dh1_b, u, w_out, pool_w_b, pool_scale, g_pool, comm=_comm_pair_swap(more, half=True) if dist else None)
    halves, from_sibling = halves + more, list(swapped) + list(swapped_more)
    du_pool = _pool_bwd_window(d_pooled)
    reduce2 = None
    if dist:
        chip_sums = [_add_half(h, r, c_arr, "sum_pair_" + k, BF16) for k, h, r in zip(late, halves, from_sibling)]
        reduce2 = _comm_chip_exchange(chip_sums, scatter=True)
    (du0, gb0r, gb0i, gc0r, gc0i, gv0), from_chips = _ssm_scan_bwd(dy, u, s0r, s0i, *ssm_args[0], 1, True, comm=reduce2)
    mine = [_sum4(r, "sum_chips_" + k) for k, r in zip(late, from_chips)]
    (du1, gb1r, gb1i, gc1r, gc1i, gv1), theirs = _ssm_scan_bwd(
        dy, u, s1r, s1i, *ssm_args[1], 3, False, comm=_comm_pair_swap(mine) if dist else None)
    by_state = (2, N_SSM_GROUPS, 1, SSM_STATE)
    g_b_re, g_b_im, g_f_re, g_f_im = _ssm_unfold(
        jnp.stack([gb0r, gb1r]), jnp.stack([gb0i, gb1i]),
        jnp.swapaxes(p["ssm_b_re"], 2, 3), jnp.swapaxes(p["ssm_b_im"], 2, 3),
        f_re.reshape(by_state), f_im.reshape(by_state))
    gvec = lambda j: jnp.stack([gv0[j], gv1[j]]).reshape(2 * N_SSM_GROUPS, SSM_STATE)
    g_lnar, g_aim, g_ldt = _ssm_params_bwd(lnar, aim, ldt, gvec(0), gvec(1),
                                           g_f_re.reshape(lnar.shape), g_f_im.reshape(lnar.shape))
    grad_x, d_u_b, g_mix_g = _in_bwd(du_pool, du_direct, du0, du1, dh1, x, g_mix, w_in)
    g_w_in = _matmul_tn(xn, d_u_b, TM, D, "grad_w_in")

    small = {
        "norm_mix_g": g_mix_g, "pool_w": g_pool_w, "pool_scale": g_pool_scale,
        "ssm_log_neg_a_re": g_lnar, "ssm_a_im": g_aim, "ssm_log_dt": g_ldt,
        "ssm_b_re": jnp.swapaxes(g_b_re, 2, 3), "ssm_b_im": jnp.swapaxes(g_b_im, 2, 3),
        "ssm_c_re": jnp.stack([gc0r, gc1r]), "ssm_c_im": jnp.stack([gc0i, gc1i]),
        "ssm_d": g_ssm_d, "glu_b": g_glu_b, "out_norm_pool_g": g_pool_g, "out_norm_ssm_g": g_ssm_g,
        "norm_ffn_g": g_ffn_g, "conv_b": jnp.concatenate([gbv[0], gbg[0]]), "final_norm_g": g_final,
        "conv_w": jnp.transpose(gcw, (2, 0, 1, 3)).reshape(3, -1),
    }
    big = {"w_in": g_w_in}
    reduced = dict(zip(late, zip(mine, theirs)))
    if not dist:
        big.update({"w_up": g_w_up, "w_down": g_w_down, "w_out": g_w_out, "glu_w": g_glu_w})
    return loss, grad_x, small, big, reduced


def kernel(x, norm_mix_g, w_in, pool_w, pool_scale, ssm_log_neg_a_re, ssm_a_im, ssm_log_dt, ssm_b_re, ssm_b_im, ssm_c_re, ssm_c_im, ssm_d, glu_w, glu_b, out_norm_pool_g, out_norm_ssm_g, w_out, norm_ffn_g, w_up, conv_w, conv_b, w_down, final_norm_g, loss_target, m_norm_mix_g, m_w_in, m_pool_w, m_pool_scale, m_ssm_log_neg_a_re, m_ssm_a_im, m_ssm_log_dt, m_ssm_b_re, m_ssm_b_im, m_ssm_c_re, m_ssm_c_im, m_ssm_d, m_glu_w, m_glu_b, m_out_norm_pool_g, m_out_norm_ssm_g, m_w_out, m_norm_ffn_g, m_w_up, m_conv_w, m_conv_b, m_w_down, m_final_norm_g, v_norm_mix_g, v_w_in, v_pool_w, v_pool_scale, v_ssm_log_neg_a_re, v_ssm_a_im, v_ssm_log_dt, v_ssm_b_re, v_ssm_b_im, v_ssm_c_re, v_ssm_c_im, v_ssm_d, v_glu_w, v_glu_b, v_out_norm_pool_g, v_out_norm_ssm_g, v_w_out, v_norm_ffn_g, v_w_up, v_conv_w, v_conv_b, v_w_down, v_final_norm_g):
    args = locals()
    w = {k: args[k] for k in WEIGHTS}
    m = {k: args["m_" + k] for k in WEIGHTS}
    v = {k: args["v_" + k] for k in WEIGHTS}
    chip = 2 * lax.axis_index("x") + lax.axis_index("y")
    c_arr = lax.axis_index("c").astype(jnp.int32).reshape(1)

    shards = {k: w[k].astype(BF16) for k in BIG}
    shards["conv_w"] = conv_w
    w_in_full = _comm_call(_comm_gather_split([shards["w_in"]], []), "gather_w_in")[0]
    loss, grad_x, g_small, g_big, reduced = _local_step(
        x[0], loss_target[0], w, {"w_in": w_in_full.reshape(-1, w_in_full.shape[-1])}, shards, c_arr)

    exact = [k for k in SMALL if k not in WIDE]
    packs = [_pack([loss] + [g_small[k] for k in exact] + [g_small["conv_w"]], 512),
             _pack([g_small[k] for k in WIDE], 512)]
    halves = [g_big["w_in"].reshape(4, 2, g_big["w_in"].shape[0] // 8, -1)]
    halves += [pk.reshape(1, 2, pk.shape[0] // 2, LANES) for pk in packs]
    from_sibling = _comm_call(_comm_pair_swap(halves, half=True), "reduce_pair")
    names = ("w_in", "exact", "wide")
    sums = [_add_half(h, r, c_arr, "sum_pair_" + k, dt)
            for k, h, r, dt in zip(names, halves, from_sibling, (BF16, F32, BF16))]
    grads, delta, new_m, new_v = {}, {}, {}, {}

    def adamw_behind(k, comm):
        own, other = reduced[k]
        (grads[k], delta[k], new_m[k], new_v[k]), got = _adamw_halves(
            w[k], own, other, m[k], v[k], "adamw_" + k, comm=comm)
        return got

    from_chips = adamw_behind("w_up", _comm_join(_comm_chip_exchange(sums[:1], scatter=True),
                                                  _comm_chip_exchange([s[0] for s in sums[1:]], scatter=False)))
    mine = [_sum4(r, "sum_chips_" + k) for k, r in zip(names, from_chips)]
    theirs = adamw_behind("w_down", _comm_pair_swap(mine))
    for k in ("w_out", "glu_w"):
        adamw_behind(k, None)
    exact_all = _join_rows(mine[1], theirs[1], c_arr, "join_exact")
    wide_all = _join_rows(mine[2], theirs[2], c_arr, "join_wide")
    shapes = [loss.shape] + [w[k].shape for k in exact] + [(3, 4 * FF_BLK)]
    grads.update(zip(["loss"] + exact + ["conv_w_full"], _unpack(exact_all, shapes)))
    grads.update(zip(WIDE, _unpack(wide_all, [w[k].shape for k in WIDE])))
    loss = grads.pop("loss")[0, 0]
    grads["conv_w"] = lax.dynamic_slice_in_dim(grads.pop("conv_w_full"), chip * FF_BLK, FF_BLK, axis=1)

    reduced["w_in"] = (mine[0], theirs[0])
    adamw_behind("w_in", None)
    padded = ["ssm_b_re", "ssm_b_im"]
    for keys, name in ((padded, "adamw_ssm_b"), ([k for k in SMALL + ["conv_w"] if k not in padded], "adamw_small")):
        outs = _adamw_many(*([d[k] for k in keys] for d in (w, grads, m, v)), name)
        for d, o in zip((delta, new_m, new_v), outs):
            d.update(zip(keys, o))

    return (loss, grad_x[None], *[grads[k] for k in WEIGHTS], *[delta[k] for k in WEIGHTS],
            *[new_m[k] for k in WEIGHTS], *[new_v[k] for k in WEIGHTS])
```

```python
import numpy as np
import jax
import jax.numpy as jnp
from jax import lax
from jax.experimental import pallas as pl
from jax.experimental.pallas import tpu as pltpu

F32 = jnp.float32
BF16 = jnp.bfloat16
MESH = pl.DeviceIdType.MESH

EPS = 1e-6
POOL_WINDOWS = (2, 4, 8, 16)
POOL_GROUP = 128
SSM_GROUP = 16
SSM_STATE = 64
N_SSM_GROUPS = 32
N_STATE = N_SSM_GROUPS * SSM_STATE
QUAD = 256
N_QUAD = N_STATE // QUAD
SLAB = 256
D_SSM = 512
D_POOL = 512
D_FF = 2816
FF_BLK = 1408
HALO = 8
HALO_B = 16
LANES = 128
ADAM_LR, ADAM_B1, ADAM_B2, ADAM_EPS, ADAM_WD, ADAM_STEP = 0.001, 0.9, 0.999, 1e-08, 0.01, 10
VMEM_LIMIT = 56 * 2 ** 20
ADAMW_BLOCK_BYTES = 2 ** 20

TL = 512
TM = 1024
TF = 256
TC = 512
SEG = 8
SEG_LEN = TC // SEG
SCAN_W = 512


def _cp(*sem):
    return pltpu.CompilerParams(dimension_semantics=sem, vmem_limit_bytes=VMEM_LIMIT)


def _dot_nn(a, b):
    return jnp.dot(a, b, preferred_element_type=F32)


def _dot_nt(a, b):
    return lax.dot_general(a, b, (((1,), (1,)), ((), ())), preferred_element_type=F32)


def _dot_tn(a, b):
    return lax.dot_general(a, b, (((0,), (0,)), ((), ())), preferred_element_type=F32)


def _rms_fwd(x, g):
    inv = lax.rsqrt(jnp.mean(x * x, axis=-1, keepdims=True) + EPS)
    xh = x * inv
    return xh * g, xh, inv


def _rms_bwd(dy, xh, inv, g):
    dg = jnp.sum(dy * xh, axis=0, keepdims=True)
    dxh = dy * g
    dx = inv * (dxh - xh * jnp.mean(dxh * xh, axis=-1, keepdims=True))
    return dx, dg


_GELU_C = 0.7978845608028654
_GELU_A = 0.044715


def _gelu(y):
    t = jnp.tanh(_GELU_C * (y + _GELU_A * (y * y * y)))
    return 0.5 * y * (1.0 + t), t


def _gelu_grad(y, t):
    return 0.5 * (1.0 + t) + 0.5 * y * (1.0 - t * t) * (_GELU_C * (1.0 + 3.0 * _GELU_A * y * y))


def _sigmoid(x):
    return 1.0 / (1.0 + jnp.exp(-x))


def _full(shape):
    n = len(shape)
    return pl.BlockSpec(shape, lambda *_: (0,) * n)


def _fill_ext(ext_ref, prev_ref, cur_ref, next_ref, i, n, rows):
    ext_ref[0:HALO, :] = jnp.where(i > 0, prev_ref[...], 0.0).astype(ext_ref.dtype)
    ext_ref[HALO:HALO + rows, :] = cur_ref[...]
    ext_ref[HALO + rows:2 * HALO + rows, :] = jnp.where(i < n - 1, next_ref[...], 0.0).astype(ext_ref.dtype)


def _in_proj(x, g, w):
    L, D = x.shape
    E = w.shape[1]

    def body(x_ref, g_ref, w_ref, u_ref, xn_ref):
        y, _, _ = _rms_fwd(x_ref[...], g_ref[...])
        yb = y.astype(BF16)
        xn_ref[...] = yb
        u_ref[...] = _dot_nn(yb, w_ref[...])

    return pl.pallas_call(
        body, name="in_proj", grid=(L // TL,),
        in_specs=[pl.BlockSpec((TL, D), lambda i: (i, 0)), _full((1, D)), _full(w.shape)],
        out_specs=[pl.BlockSpec((TL, E), lambda i: (i, 0)), pl.BlockSpec((TL, D), lambda i: (i, 0))],
        out_shape=[jax.ShapeDtypeStruct((L, E), F32), jax.ShapeDtypeStruct((L, D), BF16)],
        compiler_params=_cp("parallel"))(x, g, w)


def _halo_specs_1d(rows, width, L, col):
    rb = rows // HALO
    last = L // HALO - 1
    return [pl.BlockSpec((HALO, width), lambda i: (jnp.maximum(i * rb - 1, 0), col)),
            pl.BlockSpec((rows, width), lambda i: (i, col)),
            pl.BlockSpec((HALO, width), lambda i: (jnp.minimum((i + 1) * rb, last), col))]


def _pooled_from_ext(ext_ref, t0, rows, L):
    t = t0 + lax.broadcasted_iota(jnp.int32, (rows, 1), 0)
    outs = []
    for gi, w in enumerate(POOL_WINDOWS):
        half = w // 2
        cs = slice(gi * POOL_GROUP, (gi + 1) * POOL_GROUP)
        acc = ext_ref[pl.ds(HALO - half, rows), cs]
        for s in range(-half + 1, half):
            acc = acc + ext_ref[pl.ds(HALO + s, rows), cs]
        cnt = (jnp.minimum(t + half, L) - jnp.maximum(t - half, 0)).astype(F32)
        outs.append(acc / cnt - ext_ref[pl.ds(HALO, rows), cs])
    return outs


def _pool_fwd(u, pool_w_b, pool_scale, g_pool):
    L = u.shape[0]
    n = L // TL

    def body(prev_ref, cur_ref, next_ref, pw_ref, ps_ref, g_ref, out_ref, ext_ref):
        i = pl.program_id(0)
        _fill_ext(ext_ref, prev_ref, cur_ref, next_ref, i, n, TL)
        pooled = _pooled_from_ext(ext_ref, i * TL, TL, L)
        ypre = jnp.concatenate([_dot_nn(pooled[gi].astype(BF16), pw_ref[gi]) for gi in range(4)], axis=-1)
        yn, _, _ = _rms_fwd(ypre * ps_ref[...], g_ref[...])
        out_ref[...] = yn.astype(BF16)

    return pl.pallas_call(
        body, name="pool_fwd", grid=(n,),
        in_specs=_halo_specs_1d(TL, D_POOL, L, 0) + [_full(pool_w_b.shape), _full((1, D_POOL)), _full((1, D_POOL))],
        out_specs=pl.BlockSpec((TL, D_POOL), lambda i: (i, 0)),
        out_shape=jax.ShapeDtypeStruct((L, D_POOL), BF16),
        scratch_shapes=[pltpu.VMEM((TL + 2 * HALO, D_POOL), F32)],
        compiler_params=_cp("parallel"))(u, u, u, pool_w_b, pool_scale, g_pool)


def _pool_bwd_local(dh1, u, w_out_b, pool_w_b, pool_scale, g_pool, comm=None):
    L = u.shape[0]
    n = L // TL
    D = dh1.shape[1]

    def body(dh_ref, prev_ref, cur_ref, next_ref, wo_ref, pw_ref, ps_ref, g_ref,
             dp_ref, gpw_ref, gps_ref, gg_ref, ext_ref):
        i = pl.program_id(0)

        @pl.when(i == 0)
        def _():
            gpw_ref[...] = jnp.zeros_like(gpw_ref)
            gps_ref[...] = jnp.zeros_like(gps_ref)
            gg_ref[...] = jnp.zeros_like(gg_ref)

        _fill_ext(ext_ref, prev_ref, cur_ref, next_ref, i, n, TL)
        pooled = [p.astype(BF16) for p in _pooled_from_ext(ext_ref, i * TL, TL, L)]
        ypre = jnp.concatenate([_dot_nn(pooled[gi], pw_ref[gi]) for gi in range(4)], axis=-1)
        ps = ps_ref[...]
        g = g_ref[...]
        _, xh, inv = _rms_fwd(ypre * ps, g)
        d_yn = _dot_nt(dh_ref[...], wo_ref[...])
        d_y, dg = _rms_bwd(d_yn, xh, inv, g)
        gg_ref[...] += dg
        gps_ref[...] += jnp.sum(d_y * ypre, axis=0, keepdims=True)
        d_ypre = (d_y * ps).astype(BF16)
        for gi in range(4):
            cs = slice(gi * POOL_GROUP, (gi + 1) * POOL_GROUP)
            dp_ref[:, cs] = _dot_nt(d_ypre[:, cs], pw_ref[gi])
            gpw_ref[gi] += _dot_tn(pooled[gi], d_ypre[:, cs])

    return _hosted_call(
        body, comm, name="pool_bwd_local", grid=(n,),
        in_specs=[pl.BlockSpec((TL, D), lambda i: (i, 0))] + _halo_specs_1d(TL, D_POOL, L, 0)
        + [pl.BlockSpec((D_POOL, D), lambda i: (0, 0)), _full(pool_w_b.shape), _full((1, D_POOL)), _full((1, D_POOL))],
        out_specs=[pl.BlockSpec((TL, D_POOL), lambda i: (i, 0)), _full(pool_w_b.shape),
                   _full((1, D_POOL)), _full((1, D_POOL))],
        out_shape=[jax.ShapeDtypeStruct((L, D_POOL), F32), jax.ShapeDtypeStruct(pool_w_b.shape, F32),
                   jax.ShapeDtypeStruct((1, D_POOL), F32), jax.ShapeDtypeStruct((1, D_POOL), F32)],
        scratch_shapes=[pltpu.VMEM((TL + 2 * HALO, D_POOL), F32)],
        args=(dh1, u, u, u, w_out_b, pool_w_b, pool_scale, g_pool))


def _pool_bwd_window(d_pooled):
    L = d_pooled.shape[0]
    n = L // TL
    R = TL + 2 * HALO

    def body(prev_ref, cur_ref, next_ref, out_ref, ext_ref, q_ref):
        i = pl.program_id(0)
        _fill_ext(ext_ref, prev_ref, cur_ref, next_ref, i, n, TL)
        tr = i * TL - HALO + lax.broadcasted_iota(jnp.int32, (R, 1), 0)
        for gi, w in enumerate(POOL_WINDOWS):
            half = w // 2
            cs = slice(gi * POOL_GROUP, (gi + 1) * POOL_GROUP)
            cnt = jnp.maximum(jnp.minimum(tr + half, L) - jnp.maximum(tr - half, 0), 1).astype(F32)
            q_ref[:, cs] = ext_ref[:, cs] / cnt
        for gi, w in enumerate(POOL_WINDOWS):
            half = w // 2
            cs = slice(gi * POOL_GROUP, (gi + 1) * POOL_GROUP)
            acc = q_ref[pl.ds(HALO - half + 1, TL), cs]
            for s in range(-half + 2, half + 1):
                acc = acc + q_ref[pl.ds(HALO + s, TL), cs]
            out_ref[:, cs] = acc - ext_ref[pl.ds(HALO, TL), cs]

    return pl.pallas_call(
        body, name="pool_bwd_window", grid=(n,),
        in_specs=_halo_specs_1d(TL, D_POOL, L, 0),
        out_specs=pl.BlockSpec((TL, D_POOL), lambda i: (i, 0)),
        out_shape=jax.ShapeDtypeStruct((L, D_POOL), F32),
        scratch_shapes=[pltpu.VMEM((R, D_POOL), F32), pltpu.VMEM((R, D_POOL), F32)],
        compiler_params=_cp("parallel"))(d_pooled, d_pooled, d_pooled)


def _ssm_param_fn(lnar, aim, ldt):
    dt = jnp.exp(ldt)
    a_re = -jnp.exp(lnar)
    mag = jnp.exp(a_re * dt)
    ang = aim * dt
    lr, li = mag * jnp.cos(ang), mag * jnp.sin(ang)
    den = a_re * a_re + aim * aim
    fr = ((lr - 1.0) * a_re + li * aim) / den
    fi = (li * a_re - (lr - 1.0) * aim) / den
    return lr, li, fr, fi


def _ssm_params(lnar, aim, ldt):
    def body(a_ref, b_ref, c_ref, lr_ref, li_ref, fr_ref, fi_ref):
        lr, li, fr, fi = _ssm_param_fn(a_ref[...], b_ref[...], c_ref[...])
        lr_ref[...] = lr
        li_ref[...] = li
        fr_ref[...] = fr
        fi_ref[...] = fi

    sh = jax.ShapeDtypeStruct(lnar.shape, F32)
    return pl.pallas_call(body, name="ssm_params", out_shape=[sh] * 4)(lnar, aim, ldt)


def _ssm_params_bwd(lnar, aim, ldt, glr, gli, gfr, gfi):
    def body(a_ref, b_ref, c_ref, g0, g1, g2, g3, da_ref, db_ref, dc_ref):
        _, vjp = jax.vjp(_ssm_param_fn, a_ref[...], b_ref[...], c_ref[...])
        da, db, dc = vjp((g0[...], g1[...], g2[...], g3[...]))
        da_ref[...] = da
        db_ref[...] = db
        dc_ref[...] = jnp.sum(dc, axis=1, keepdims=True)

    return pl.pallas_call(
        body, name="ssm_params_bwd",
        out_shape=[jax.ShapeDtypeStruct(lnar.shape, F32), jax.ShapeDtypeStruct(aim.shape, F32),
                   jax.ShapeDtypeStruct((ldt.shape[0], 1), F32)])(lnar, aim, ldt, glr, gli, gfr, gfi)


def _scan_tables(lam4):
    def build(lr, li, reverse, out_ref, k):
        pr, pi = lr, li
        for d in range(SEG_LEN):
            j = SEG_LEN - 1 - d if reverse else d
            out_ref[k, 0, j:j + 1, :] = pr
            out_ref[k, 1, j:j + 1, :] = pi
            pr, pi = pr * lr - pi * li, pr * li + pi * lr

    def body(lam_ref, out_ref):
        l0r, l0i, l1r, l1i = (lam_ref[j:j + 1, :] for j in range(4))
        build(l0r, l0i, False, out_ref, 0)
        build(l0r, -l0i, True, out_ref, 1)
        build(l1r, l1i, True, out_ref, 2)
        build(l1r, -l1i, False, out_ref, 3)

    return pl.pallas_call(body, name="scan_tables",
                          out_shape=jax.ShapeDtypeStruct((4, 2, SEG_LEN, N_STATE), F32))(lam4)


def _b_block(g):
    q, gl = divmod(g, 4)
    r0, c0 = gl * SSM_STATE, (q % 4) * 4 * SSM_GROUP + gl * SSM_GROUP
    return q, slice(r0, r0 + SSM_STATE), slice(c0, c0 + SSM_GROUP)


def _c_block(g):
    q, rows, cols = _b_block(g)
    return q, cols, rows


def _ssm_expand(b_re, b_im, c_re, c_im, f_re, f_im):
    def body(bre_ref, bim_ref, cre_ref, cim_ref, fre_ref, fim_ref, *rest):
        outs, tmp, bbr_ref, bbi_ref = rest[:8], rest[8], rest[9], rest[10]
        fr, fi, br, bi = fre_ref[...], fim_ref[...], bre_ref[...], bim_ref[...]
        bbr_ref[...] = fr * br - fi * bi
        bbi_ref[...] = fr * bi + fi * br
        for d in range(2):
            for j, (src, where) in enumerate(((bbr_ref, _b_block), (bbi_ref, _b_block),
                                              (cre_ref, _c_block), (cim_ref, _c_block))):
                tmp[...] = jnp.zeros_like(tmp)
                for g in range(N_SSM_GROUPS):
                    q, rows, cols = where(g)
                    tmp[q, rows, cols] = src[d, g]
                outs[4 * d + j][...] = tmp[...].astype(BF16)

    dense = jax.ShapeDtypeStruct((N_QUAD, QUAD, SLAB), BF16)
    return pl.pallas_call(body, name="ssm_expand", out_shape=[dense] * 8,
                          scratch_shapes=[pltpu.VMEM((N_QUAD, QUAD, SLAB), F32), pltpu.VMEM(b_re.shape, F32),
                                          pltpu.VMEM(b_re.shape, F32)],
                          compiler_params=pltpu.CompilerParams(vmem_limit_bytes=VMEM_LIMIT))(
                              b_re, b_im, c_re, c_im, f_re, f_im)


def _ssm_unfold(gbb_re, gbb_im, b_re_t, b_im_t, f_re, f_im):
    def body(gr_ref, gi_ref, br_ref, bi_ref, fr_ref, fi_ref, obr_ref, obi_ref, ofr_ref, ofi_ref):
        gr, gi, br, bi, fr, fi = (r[...] for r in (gr_ref, gi_ref, br_ref, bi_ref, fr_ref, fi_ref))
        obr_ref[...] = fr * gr + fi * gi
        obi_ref[...] = fr * gi - fi * gr
        ofr_ref[...] = jnp.sum(br * gr + bi * gi, axis=2, keepdims=True)
        ofi_ref[...] = jnp.sum(br * gi - bi * gr, axis=2, keepdims=True)

    gb = jax.ShapeDtypeStruct(gbb_re.shape, F32)
    gf = jax.ShapeDtypeStruct(f_re.shape, F32)
    return pl.pallas_call(body, name="ssm_unfold", out_shape=[gb, gb, gf, gf])(
        gbb_re, gbb_im, b_re_t, b_im_t, f_re, f_im)


_SEGMENT_ORDER = np.zeros((TC, TC), np.float32)
for _p in range(TC):
    _SEGMENT_ORDER[_p, (_p % SEG) * SEG_LEN + _p // SEG] = 1.0


def _store_tokens(ref, col0, val, tmp_ref):
    for h in range(val.shape[1] // LANES):
        for j in range(SEG_LEN):
            tmp_ref[pl.ds(h * TC + j, SEG, stride=SEG_LEN), :] = val[SEG * j:SEG * (j + 1), h * LANES:(h + 1) * LANES]
        ref[:, col0 + h * LANES:col0 + (h + 1) * LANES] = tmp_ref[pl.ds(h * TC, TC), :]


def _segment_scan(src_re, src_im, dst_re, dst_im, tab_ref, k, carry_re, carry_im, reverse, s_refs=None):
    lam1, lam_seg = (SEG_LEN - 1, 0) if reverse else (0, SEG_LEN - 1)
    token = (lambda i: SEG_LEN - 1 - i) if reverse else (lambda i: i)
    row_id = lax.broadcasted_iota(jnp.int32, (SEG, SCAN_W), 0)
    zero = jnp.zeros((SEG, SCAN_W), F32)
    sums = []
    for lt in range(N_STATE // SCAN_W):
        sl = slice(lt * SCAN_W, (lt + 1) * SCAN_W)
        lr = jnp.broadcast_to(tab_ref[k, 0, lam1:lam1 + 1, sl], (SEG, SCAN_W))
        li = jnp.broadcast_to(tab_ref[k, 1, lam1:lam1 + 1, sl], (SEG, SCAN_W))

        def local(i, c, sl=sl, lr=lr, li=li):
            rows = pl.ds(pl.multiple_of(token(i) * SEG, SEG), SEG)
            nr = lr * c[0] - li * c[1] + src_re[rows, sl]
            ni = lr * c[1] + li * c[0] + src_im[rows, sl]
            dst_re[rows, sl] = nr
            dst_im[rows, sl] = ni
            return nr, ni

        er, ei = lax.fori_loop(0, SEG_LEN, local, (zero, zero))

        sr_, si_ = tab_ref[k, 0, lam_seg:lam_seg + 1, sl], tab_ref[k, 1, lam_seg:lam_seg + 1, sl]
        c_r, c_i = carry_re[0:1, sl], carry_im[0:1, sl]
        in_r, in_i = zero, zero
        for r in (range(SEG - 1, -1, -1) if reverse else range(SEG)):
            in_r = jnp.where(row_id == r, c_r, in_r)
            in_i = jnp.where(row_id == r, c_i, in_i)
            c_r, c_i = (er[r:r + 1, :] + sr_ * c_r - si_ * c_i, ei[r:r + 1, :] + sr_ * c_i + si_ * c_r)
        carry_re[0:1, sl] = c_r
        carry_im[0:1, sl] = c_i

        def fix(i, c, sl=sl, in_r=in_r, in_i=in_i):
            j = token(i)
            rows = pl.ds(pl.multiple_of(j * SEG, SEG), SEG)
            pr = jnp.broadcast_to(tab_ref[k, 0, pl.ds(j, 1), sl], (SEG, SCAN_W))
            pi = jnp.broadcast_to(tab_ref[k, 1, pl.ds(j, 1), sl], (SEG, SCAN_W))
            nr = dst_re[rows, sl] + pr * in_r - pi * in_i
            ni = dst_im[rows, sl] + pr * in_i + pi * in_r
            dst_re[rows, sl] = nr
            dst_im[rows, sl] = ni
            if s_refs is None:
                return c
            sr = s_refs[0][rows, sl]
            si = s_refs[1][rows, sl]
            return nr, ni, c[2] + c[0] * sr + c[1] * si, c[3] + c[1] * sr - c[0] * si

        if s_refs is None:
            lax.fori_loop(0, SEG_LEN, fix, 0)
        else:
            out = lax.fori_loop(0, SEG_LEN, fix, (in_r, in_i, zero, zero))
            sums.append((jnp.sum(out[2], axis=0, keepdims=True), jnp.sum(out[3], axis=0, keepdims=True)))
    return sums


def _ssm_scan_fwd(u, b_re, b_im, c_re, c_im, tables, k, reverse, comm=None):
    L = u.shape[0]
    nc = L // TC
    chunk = (lambda i: nc - 1 - i) if reverse else (lambda i: i)
    order = jnp.asarray(_SEGMENT_ORDER, BF16)

    def body(u_ref, ord_ref, bre_ref, bim_ref, cre_ref, cim_ref, tab_ref,
             y_ref, sre_ref, sim_ref, in_re, in_im, carry_re, carry_im, tmp_ref):
        @pl.when(pl.program_id(0) == 0)
        def _():
            carry_re[...] = jnp.zeros_like(carry_re)
            carry_im[...] = jnp.zeros_like(carry_im)

        ub = _dot_nn(ord_ref[...], u_ref[...].astype(BF16)).astype(BF16)
        for q in range(N_QUAD):
            qs = slice(q * QUAD, (q + 1) * QUAD)
            us = ub[:, (q // 4) * SLAB:(q // 4 + 1) * SLAB]
            in_re[:, qs] = _dot_nt(us, bre_ref[q])
            in_im[:, qs] = _dot_nt(us, bim_ref[q])
        _segment_scan(in_re, in_im, sre_ref, sim_ref, tab_ref, k, carry_re, carry_im, reverse)
        for j in range(D_SSM // SLAB):
            acc = jnp.zeros((TC, SLAB), F32)
            for q in range(4 * j, 4 * j + 4):
                qs = slice(q * QUAD, (q + 1) * QUAD)
                acc = acc + _dot_nt(sre_ref[:, qs].astype(BF16), cre_ref[q])
                acc = acc - _dot_nt(sim_ref[:, qs].astype(BF16), cim_ref[q])
            _store_tokens(y_ref, j * SLAB, acc, tmp_ref)

    return _hosted_call(
        body, comm, name="ssm_scan_rev" if reverse else "ssm_scan_fwd", grid=(nc,),
        in_specs=[pl.BlockSpec((TC, D_SSM), lambda i: (chunk(i), 1)), _full(order.shape)]
        + [_full(b_re.shape)] * 4 + [_full(tables.shape)],
        out_specs=[pl.BlockSpec((TC, D_SSM), lambda i: (chunk(i), 0)),
                   pl.BlockSpec((TC, N_STATE), lambda i: (chunk(i), 0)),
                   pl.BlockSpec((TC, N_STATE), lambda i: (chunk(i), 0))],
        out_shape=[jax.ShapeDtypeStruct((L, D_SSM), F32), jax.ShapeDtypeStruct((L, N_STATE), F32),
                   jax.ShapeDtypeStruct((L, N_STATE), F32)],
        scratch_shapes=[pltpu.VMEM((TC, N_STATE), F32), pltpu.VMEM((TC, N_STATE), F32),
                        pltpu.VMEM((8, N_STATE), F32), pltpu.VMEM((8, N_STATE), F32),
                        pltpu.VMEM((SLAB // LANES * TC, LANES), F32)],
        args=(u, order, b_re, b_im, c_re, c_im, tables))


def _quad_channels(q):
    c0 = (q // 4) * SLAB + (q % 4) * 4 * SSM_GROUP
    return slice(c0, c0 + 4 * SSM_GROUP)


def _ssm_scan_bwd(dy, u, s_re, s_im, b_re, b_im, c_re, c_im, tables, k, reverse, comm=None):
    L = u.shape[0]
    nc = L // TC
    chunk = (lambda i: nc - 1 - i) if reverse else (lambda i: i)

    order = jnp.asarray(_SEGMENT_ORDER, BF16)

    def body(dy_ref, u_ref, ord_ref, sre_ref, sim_ref, bre_ref, bim_ref, cre_ref, cim_ref, tab_ref,
             du_ref, ob_re, ob_im, oc_re, oc_im, gv_ref,
             a_re, a_im, carry_re, carry_im, gbr_ref, gbi_ref, gcr_ref, gci_ref, tmp_ref):
        @pl.when(pl.program_id(0) == 0)
        def _():
            carry_re[...] = jnp.zeros_like(carry_re)
            carry_im[...] = jnp.zeros_like(carry_im)
            for r in (gbr_ref, gbi_ref, gcr_ref, gci_ref, gv_ref):
                r[...] = jnp.zeros_like(r)

        dyb = _dot_nn(ord_ref[...], dy_ref[...].astype(BF16)).astype(BF16)
        ub = _dot_nn(ord_ref[...], u_ref[...].astype(BF16)).astype(BF16)
        for q in range(N_QUAD):
            qs = slice(q * QUAD, (q + 1) * QUAD)
            ds = dyb[:, (q // 4) * SLAB:(q // 4 + 1) * SLAB]
            a_re[:, qs] = _dot_nn(ds, cre_ref[q])
            a_im[:, qs] = -_dot_nn(ds, cim_ref[q])
            dq = dyb[:, _quad_channels(q)]
            gcr_ref[q] += _dot_tn(dq, sre_ref[:, qs].astype(BF16))
            gci_ref[q] -= _dot_tn(dq, sim_ref[:, qs].astype(BF16))
        sums = _segment_scan(a_re, a_im, a_re, a_im, tab_ref, k, carry_re, carry_im, reverse,
                             s_refs=(sre_ref, sim_ref))
        for lt, (glr, gli) in enumerate(sums):
            sl = slice(lt * SCAN_W, (lt + 1) * SCAN_W)
            gv_ref[0:1, sl] += glr
            gv_ref[1:2, sl] += gli
        for j in range(D_SSM // SLAB):
            us = ub[:, j * SLAB:(j + 1) * SLAB]
            acc = jnp.zeros((TC, SLAB), F32)
            for q in range(4 * j, 4 * j + 4):
                qs = slice(q * QUAD, (q + 1) * QUAD)
                dbr = a_re[:, qs].astype(BF16)
                dbi = a_im[:, qs].astype(BF16)
                uq = ub[:, _quad_channels(q)]
                gbr_ref[q] += _dot_tn(uq, dbr)
                gbi_ref[q] += _dot_tn(uq, dbi)
                acc = acc + _dot_nn(dbr, bre_ref[q]) + _dot_nn(dbi, bim_ref[q])
            _store_tokens(du_ref, j * SLAB, acc, tmp_ref)

        @pl.when(pl.program_id(0) == nc - 1)
        def _():
            for g in range(N_SSM_GROUPS):
                q, gl = divmod(g, 4)
                rows = slice(gl * SSM_GROUP, (gl + 1) * SSM_GROUP)
                cols = slice(gl * SSM_STATE, (gl + 1) * SSM_STATE)
                for out, acc_ref in ((ob_re, gbr_ref), (ob_im, gbi_ref), (oc_re, gcr_ref), (oc_im, gci_ref)):
                    out[g] = acc_ref[q, rows, cols]

    gshape = jax.ShapeDtypeStruct((N_SSM_GROUPS, SSM_GROUP, SSM_STATE), F32)
    compact = pltpu.VMEM((N_QUAD, 4 * SSM_GROUP, QUAD), F32)
    return _hosted_call(
        body, comm, name="ssm_bwd_rev" if reverse else "ssm_bwd_fwd", grid=(nc,),
        in_specs=[pl.BlockSpec((TC, D_SSM), lambda i: (chunk(i), 0)),
                  pl.BlockSpec((TC, D_SSM), lambda i: (chunk(i), 1)), _full(order.shape),
                  pl.BlockSpec((TC, N_STATE), lambda i: (chunk(i), 0)),
                  pl.BlockSpec((TC, N_STATE), lambda i: (chunk(i), 0))]
        + [_full(b_re.shape)] * 4 + [_full(tables.shape)],
        out_specs=[pl.BlockSpec((TC, D_SSM), lambda i: (chunk(i), 0))] + [_full(gshape.shape)] * 4
        + [_full((2, N_STATE))],
        out_shape=[jax.ShapeDtypeStruct((L, D_SSM), F32), gshape, gshape, gshape, gshape,
                   jax.ShapeDtypeStruct((2, N_STATE), F32)],
        scratch_shapes=[pltpu.VMEM((TC, N_STATE), F32), pltpu.VMEM((TC, N_STATE), F32),
                        pltpu.VMEM((8, N_STATE), F32), pltpu.VMEM((8, N_STATE), F32),
                        compact, compact, compact, compact, pltpu.VMEM((SLAB // LANES * TC, LANES), F32)],
        args=(dy, u, order, s_re, s_im, b_re, b_im, c_re, c_im, tables))


def _ssm_post(yf, yb, u, d, glu_w, glu_b):
    y = yf + yb + d * u
    z, t = _gelu(y)
    zb = z.astype(BF16)
    gate = _sigmoid(_dot_nn(zb, glu_w) + glu_b)
    return y, z, t, zb, gate


def _mix_out(yn_pool, yf, yb, u, x, ssm_d, glu_w_b, glu_b, g_ssm, w_out_b, g_ffn):
    L, D = x.shape

    def body(ynp_ref, yf_ref, yb_ref, u_ref, x_ref, d_ref, gw_ref, gb_ref, gs_ref, wo_ref, gf_ref,
             h1_ref, hn_ref, ycat_ref):
        _, z, _, _, gate = _ssm_post(yf_ref[...], yb_ref[...], u_ref[...], d_ref[...], gw_ref[...], gb_ref[...])
        yns, _, _ = _rms_fwd(z * gate, gs_ref[...])
        ynsb = yns.astype(BF16)
        ynp = ynp_ref[...]
        ycat_ref[:, 0:D_POOL] = ynp
        ycat_ref[:, D_POOL:D] = ynsb
        h1 = x_ref[...] + _dot_nn(ynp, wo_ref[0:D_POOL, :]) + _dot_nn(ynsb, wo_ref[D_POOL:D, :])
        h1_ref[...] = h1
        hn, _, _ = _rms_fwd(h1, gf_ref[...])
        hn_ref[...] = hn.astype(BF16)

    half = lambda c: pl.BlockSpec((TL, D_SSM), lambda i: (i, c))
    row = pl.BlockSpec((TL, D), lambda i: (i, 0))
    return pl.pallas_call(
        body, name="mix_out", grid=(L // TL,),
        in_specs=[half(0), half(0), half(0), half(1), row, _full((1, D_SSM)), _full(glu_w_b.shape),
                  _full((1, D_SSM)), _full((1, D_SSM)), _full(w_out_b.shape), _full((1, D))],
        out_specs=[row, row, row],
        out_shape=[jax.ShapeDtypeStruct((L, D), F32), jax.ShapeDtypeStruct((L, D), BF16),
                   jax.ShapeDtypeStruct((L, D), BF16)],
        compiler_params=_cp("parallel"))(yn_pool, yf, yb, u, x, ssm_d, glu_w_b, glu_b, g_ssm, w_out_b, g_ffn)


def _ssm_bwd_local(dh1, yf, yb, u, ssm_d, glu_w_b, glu_b, g_ssm, w_out_b, comm=None):
    L, D = dh1.shape

    def body(dh_ref, yf_ref, yb_ref, u_ref, d_ref, gw_ref, gb_ref, gs_ref, wo_ref,
             dy_ref, du_ref, ggw_ref, ggb_ref, gd_ref, ggs_ref):
        @pl.when(pl.program_id(0) == 0)
        def _():
            for r in (ggw_ref, ggb_ref, gd_ref, ggs_ref):
                r[...] = jnp.zeros_like(r)

        u = u_ref[...]
        d = d_ref[...]
        y, z, t, zb, gate = _ssm_post(yf_ref[...], yb_ref[...], u, d, gw_ref[...], gb_ref[...])
        gs = gs_ref[...]
        _, xh, inv = _rms_fwd(z * gate, gs)
        d_yn = _dot_nt(dh_ref[...], wo_ref[...])
        d_o, dgs = _rms_bwd(d_yn, xh, inv, gs)
        ggs_ref[...] += dgs
        d_zg = d_o * z * gate * (1.0 - gate)
        d_zgb = d_zg.astype(BF16)
        ggb_ref[...] += jnp.sum(d_zg, axis=0, keepdims=True)
        ggw_ref[...] += _dot_tn(zb, d_zgb)
        d_z = d_o * gate + _dot_nt(d_zgb, gw_ref[...])
        d_y = d_z * _gelu_grad(y, t)
        gd_ref[...] += jnp.sum(d_y * u, axis=0, keepdims=True)
        dy_ref[...] = d_y
        du_ref[...] = d_y * d

    half = lambda c: pl.BlockSpec((TL, D_SSM), lambda i: (i, c))
    vec = _full((1, D_SSM))
    return _hosted_call(
        body, comm, name="ssm_bwd_local", grid=(L // TL,),
        in_specs=[pl.BlockSpec((TL, D), lambda i: (i, 0)), half(0), half(0), half(1), vec, _full(glu_w_b.shape),
                  vec, vec, pl.BlockSpec((D_SSM, D), lambda i: (1, 0))],
        out_specs=[half(0), half(0), _full(glu_w_b.shape), vec, vec, vec],
        out_shape=[jax.ShapeDtypeStruct((L, D_SSM), F32), jax.ShapeDtypeStruct((L, D_SSM), F32),
                   jax.ShapeDtypeStruct(glu_w_b.shape, F32)] + [jax.ShapeDtypeStruct((1, D_SSM), F32)] * 3,
        scratch_shapes=[], args=(dh1, yf, yb, u, ssm_d, glu_w_b, glu_b, g_ssm, w_out_b))


def _in_bwd(du_pool, du_a, du_b, du_c, dh1, x, g, w_in_b):
    L, D = x.shape

    def body(p_ref, a_ref, b_ref, c_ref, dh_ref, x_ref, g_ref, w_ref, dx_ref, dub_ref, gg_ref):
        @pl.when(pl.program_id(0) == 0)
        def _():
            gg_ref[...] = jnp.zeros_like(gg_ref)

        dub_ref[:, 0:D_POOL] = p_ref[...].astype(BF16)
        dub_ref[:, D_POOL:D] = (a_ref[...] + b_ref[...] + c_ref[...]).astype(BF16)
        d_xn = _dot_nt(dub_ref[...], w_ref[...])
        gv = g_ref[...]
        _, xh, inv = _rms_fwd(x_ref[...], gv)
        dx, dg = _rms_bwd(d_xn, xh, inv, gv)
        gg_ref[...] += dg
        dx_ref[...] = dh_ref[...] + dx

    half = pl.BlockSpec((TL, D_SSM), lambda i: (i, 0))
    row = pl.BlockSpec((TL, D), lambda i: (i, 0))
    return pl.pallas_call(
        body, name="in_bwd", grid=(L // TL,),
        in_specs=[half, half, half, half, row, row, _full((1, D)), _full(w_in_b.shape)],
        out_specs=[row, row, _full((1, D))],
        out_shape=[jax.ShapeDtypeStruct((L, D), F32), jax.ShapeDtypeStruct((L, D), BF16),
                   jax.ShapeDtypeStruct((1, D), F32)],
        compiler_params=_cp("arbitrary"))(du_pool, du_a, du_b, du_c, dh1, x, g, w_in_b)


def _ffn_up(hn, w_up4):
    L, D = hn.shape

    def body(h_ref, w_ref, o_ref):
        o_ref[...] = _dot_nn(h_ref[...], w_ref[...]).astype(BF16)

    rows = min(TM, L)
    return pl.pallas_call(
        body, name="ffn_up", grid=(4, L // rows),
        in_specs=[pl.BlockSpec((rows, D), lambda j, i: (i, 0)), pl.BlockSpec((None, D, FF_BLK), lambda j, i: (j, 0, 0))],
        out_specs=pl.BlockSpec((rows, FF_BLK), lambda j, i: (i, j)),
        out_shape=jax.ShapeDtypeStruct((L, 4 * FF_BLK), BF16),
        compiler_params=_cp("parallel", "parallel"))(hn, w_up4)


def _halo_specs_2d(rows, width, L, col, order):
    rb = rows // HALO_B
    last = L // HALO_B - 1
    if order == "ik":
        wrap = lambda f: (lambda i, k: f(i, k))
    else:
        wrap = lambda f: (lambda k, i: f(i, k))
    return [pl.BlockSpec((HALO_B, width), wrap(lambda i, k: (jnp.maximum(i * rb - 1, 0), col(k)))),
            pl.BlockSpec((rows, width), wrap(lambda i, k: (i, col(k)))),
            pl.BlockSpec((HALO_B, width), wrap(lambda i, k: (jnp.minimum((i + 1) * rb, last), col(k))))]


def _shift_mats(rows):
    r = lax.broadcasted_iota(jnp.int32, (rows, rows), 0)
    c = lax.broadcasted_iota(jnp.int32, (rows, rows), 1)
    return (c == r - 1).astype(BF16), (c == r + 1).astype(BF16)


def _neighbours(x, prev_ref, next_ref, cs, i, n, mats):
    rows = x.shape[0]
    row = lax.broadcasted_iota(jnp.int32, (rows, 1), 0)
    before = jnp.where(i > 0, prev_ref[:, cs].astype(F32)[HALO_B - 1:HALO_B, :], 0.0)
    after = jnp.where(i < n - 1, next_ref[:, cs].astype(F32)[0:1, :], 0.0)
    if mats is None:
        xf = x.astype(F32)
        down, up = pltpu.roll(xf, 1, 0), pltpu.roll(xf, rows - 1, 0)
    else:
        down, up = _dot_nn(mats[0], x), _dot_nn(mats[1], x)
    return jnp.where(row == 0, before, down), jnp.where(row == rows - 1, after, up)


def _conv3(x, before, after, w, b):
    return before * w[0:1, :] + x.astype(F32) * w[1:2, :] + after * w[2:3, :] + b


def _col_chunks(width, size=256):
    return [slice(c, min(c + size, width)) for c in range(0, width, size)]


def _ffn_down_loss(up, conv_w, conv_b, w_down_b, h1, target, g_final):
    L, D = h1.shape
    n = L // TF
    nk = D_FF // FF_BLK

    def body(vp, vc, vn, gp, gc, gn, wv_ref, wg_ref, bv_ref, bg_ref, wd_ref, h1_ref, t_ref, gf_ref,
             a_ref, cv_ref, cg_ref, dh2_ref, dh2b_ref, loss_ref, gg_ref, acc_ref):
        i = pl.program_id(0)
        k = pl.program_id(1)

        @pl.when((i == 0) & (k == 0))
        def _():
            loss_ref[...] = jnp.zeros_like(loss_ref)
            gg_ref[...] = jnp.zeros_like(gg_ref)

        @pl.when(k == 0)
        def _():
            acc_ref[...] = jnp.zeros_like(acc_ref)

        mats = _shift_mats(TF)
        for cs in _col_chunks(FF_BLK):
            xv, xg = vc[:, cs], gc[:, cs]
            val = _conv3(xv, *_neighbours(xv, vp, vn, cs, i, n, mats), wv_ref[:, cs], bv_ref[:, cs])
            gate = _conv3(xg, *_neighbours(xg, gp, gn, cs, i, n, mats), wg_ref[:, cs], bg_ref[:, cs])
            a_ref[:, cs] = (val * (gate * _sigmoid(gate))).astype(BF16)
            cv_ref[:, cs] = val.astype(BF16)
            cg_ref[:, cs] = gate.astype(BF16)
        acc_ref[...] += _dot_nn(a_ref[...], wd_ref[pl.ds(pl.multiple_of(k * FF_BLK, LANES), FF_BLK), :])

        @pl.when(k == nk - 1)
        def _():
            gf = gf_ref[...]
            y, xh, inv = _rms_fwd(h1_ref[...] + acc_ref[...], gf)
            diff = y - t_ref[...]
            part = 0.5 * jnp.sum(jnp.mean(diff * diff, axis=-1, keepdims=True), axis=0, keepdims=True)
            loss_ref[...] += jnp.broadcast_to(part, loss_ref.shape)
            dx, dg = _rms_bwd(diff * (1.0 / D), xh, inv, gf)
            gg_ref[...] += dg
            dh2_ref[...] = dx
            dh2b_ref[...] = dx.astype(BF16)

    row = pl.BlockSpec((TF, D), lambda i, k: (i, 0))
    cw = lambda off: pl.BlockSpec((3, FF_BLK), lambda i, k: (0, k + off))
    cb = lambda off: pl.BlockSpec((1, FF_BLK), lambda i, k: (0, k + off))
    return pl.pallas_call(
        body, name="ffn_down_loss", grid=(n, nk),
        in_specs=_halo_specs_2d(TF, FF_BLK, L, lambda k: k, "ik") + _halo_specs_2d(TF, FF_BLK, L, lambda k: k + nk, "ik")
        + [cw(0), cw(nk), cb(0), cb(nk), _full(w_down_b.shape), row, row, _full((1, D))],
        out_specs=[pl.BlockSpec((TF, FF_BLK), lambda i, k: (i, k))] * 3 + [row, row, _full((1, LANES)), _full((1, D))],
        out_shape=[jax.ShapeDtypeStruct((L, D_FF), BF16)] * 3
        + [jax.ShapeDtypeStruct((L, D), F32), jax.ShapeDtypeStruct((L, D), BF16),
           jax.ShapeDtypeStruct((1, LANES), F32), jax.ShapeDtypeStruct((1, D), F32)],
        scratch_shapes=[pltpu.VMEM((TF, D), F32)],
        compiler_params=_cp("arbitrary", "arbitrary"))(
            up, up, up, up, up, up, conv_w, conv_w, conv_b, conv_b, w_down_b, h1, target, g_final)


def _ffn_act_bwd(c_val, c_gate, w_down_b, dh2):
    L, D = dh2.shape
    n = L // TL
    nk = D_FF // FF_BLK

    def body(v_ref, g_ref, wd_ref, dh_ref, dv_ref, dg_ref, gbv_ref, gbg_ref):
        @pl.when(pl.program_id(1) == 0)
        def _():
            gbv_ref[...] = jnp.zeros_like(gbv_ref)
            gbg_ref[...] = jnp.zeros_like(gbg_ref)

        dh = dh_ref[...]
        for cs in _col_chunks(FF_BLK):
            val, gate = v_ref[:, cs].astype(F32), g_ref[:, cs].astype(F32)
            d_a = _dot_nt(dh, wd_ref[cs, :])
            sg = _sigmoid(gate)
            d_val = d_a * (gate * sg)
            d_gate = d_a * val * (sg * (1.0 + gate * (1.0 - sg)))
            dv_ref[:, cs] = d_val.astype(BF16)
            dg_ref[:, cs] = d_gate.astype(BF16)
            gbv_ref[:, cs] += jnp.sum(d_val, axis=0, keepdims=True)
            gbg_ref[:, cs] += jnp.sum(d_gate, axis=0, keepdims=True)

    blk = pl.BlockSpec((TL, FF_BLK), lambda k, i: (i, k))
    acc = pl.BlockSpec((1, FF_BLK), lambda k, i: (0, k))
    return pl.pallas_call(
        body, name="ffn_act_bwd", grid=(nk, n),
        in_specs=[blk, blk, pl.BlockSpec((FF_BLK, D), lambda k, i: (k, 0)), pl.BlockSpec((TL, D), lambda k, i: (i, 0))],
        out_specs=[blk, blk, acc, acc],
        out_shape=[jax.ShapeDtypeStruct((L, D_FF), BF16), jax.ShapeDtypeStruct((L, D_FF), BF16),
                   jax.ShapeDtypeStruct((1, D_FF), F32), jax.ShapeDtypeStruct((1, D_FF), F32)],
        compiler_params=_cp("arbitrary", "arbitrary"))(c_val, c_gate, w_down_b, dh2)


def _ffn_up_bwd(d_val, d_gate, up, conv_w, w_up4, h1, dh2, g_ffn):
    L, D = h1.shape
    n = L // TF
    nk = D_FF // FF_BLK

    def body(vp, vc, vn, gp, gc, gn, uv_ref, ug_ref, wv_ref, wg_ref, wu_ref, h1_ref, dh2_ref, g_ref,
             dup_ref, dh1_ref, dh1b_ref, gg_ref, gcw_ref, acc_ref):
        i = pl.program_id(0)
        k = pl.program_id(1)

        @pl.when((i == 0) & (k == 0))
        def _():
            gg_ref[...] = jnp.zeros_like(gg_ref)
            gcw_ref[...] = jnp.zeros_like(gcw_ref)

        @pl.when(k == 0)
        def _():
            acc_ref[...] = jnp.zeros_like(acc_ref)

        acc = jnp.zeros((TF, D), F32)
        for j, (blocks, u_ref, w_ref) in enumerate((((vp, vc, vn), uv_ref, wv_ref), ((gp, gc, gn), ug_ref, wg_ref))):
            for cs in _col_chunks(FF_BLK):
                d = blocks[1][:, cs]
                before, after = _neighbours(d, blocks[0], blocks[2], cs, i, n, None)
                taps = (after, d.astype(F32), before)
                w = w_ref[:, cs]
                d_up = (taps[0] * w[0:1, :] + taps[1] * w[1:2, :] + taps[2] * w[2:3, :]).astype(BF16)
                dup_ref[j, :, cs] = d_up
                acc = acc + _dot_nt(d_up, wu_ref[k + j * nk, :, cs])
                x = u_ref[:, cs].astype(F32)
                for r in range(3):
                    gcw_ref[j, k, r:r + 1, cs] += jnp.sum(taps[r] * x, axis=0, keepdims=True)
        acc_ref[...] += acc

        @pl.when(k == nk - 1)
        def _():
            g = g_ref[...]
            _, xh, inv = _rms_fwd(h1_ref[...], g)
            dx, dg = _rms_bwd(acc_ref[...], xh, inv, g)
            gg_ref[...] += dg
            dh1 = dh2_ref[...] + dx
            dh1_ref[...] = dh1
            dh1b_ref[...] = dh1.astype(BF16)

    row = pl.BlockSpec((TF, D), lambda i, k: (i, 0))
    cw = lambda off: pl.BlockSpec((3, FF_BLK), lambda i, k: (0, k + off))
    tile = lambda off: pl.BlockSpec((TF, FF_BLK), lambda i, k: (i, k + off))
    return pl.pallas_call(
        body, name="ffn_up_bwd", grid=(n, nk),
        in_specs=_halo_specs_2d(TF, FF_BLK, L, lambda k: k, "ik") + _halo_specs_2d(TF, FF_BLK, L, lambda k: k, "ik")
        + [tile(0), tile(nk), cw(0), cw(nk), _full(w_up4.shape), row, row, _full((1, D))],
        out_specs=[pl.BlockSpec((2, None, TF, FF_BLK), lambda i, k: (0, k, i, 0)), row, row, _full((1, D)),
                   _full((2, nk, 3, FF_BLK))],
        out_shape=[jax.ShapeDtypeStruct((2, nk, L, FF_BLK), BF16), jax.ShapeDtypeStruct((L, D), F32),
                   jax.ShapeDtypeStruct((L, D), BF16), jax.ShapeDtypeStruct((1, D), F32),
                   jax.ShapeDtypeStruct((2, nk, 3, FF_BLK), F32)],
        scratch_shapes=[pltpu.VMEM((TF, D), F32)],
        compiler_params=_cp("arbitrary", "arbitrary"))(
            d_val, d_val, d_val, d_gate, d_gate, d_gate, up, up, conv_w, conv_w, w_up4, h1, dh2, g_ffn)


def _matmul_tn(a, b, tm, tn, name, tk=2048):
    L, M = a.shape
    N = b.shape[1]
    tk = min(tk, L)

    def body(a_ref, b_ref, o_ref):
        @pl.when(pl.program_id(2) == 0)
        def _():
            o_ref[...] = jnp.zeros_like(o_ref)

        o_ref[...] += _dot_tn(a_ref[...], b_ref[...])

    return pl.pallas_call(
        body, name=name, grid=(M // tm, N // tn, L // tk),
        in_specs=[pl.BlockSpec((tk, tm), lambda m, n, l: (l, m)), pl.BlockSpec((tk, tn), lambda m, n, l: (l, n))],
        out_specs=pl.BlockSpec((tm, tn), lambda m, n, l: (m, n)),
        out_shape=jax.ShapeDtypeStruct((M, N), F32),
        compiler_params=_cp("parallel", "parallel", "arbitrary"))(a, b)


def _matmul_tn_blocks(a, b, tm, name, tk=2048):
    L, M = a.shape
    J, _, N = b.shape
    tk = min(tk, L)

    def body(a_ref, b_ref, o_ref):
        @pl.when(pl.program_id(2) == 0)
        def _():
            o_ref[...] = jnp.zeros_like(o_ref)

        o_ref[...] += _dot_tn(a_ref[...], b_ref[...])

    return pl.pallas_call(
        body, name=name, grid=(M // tm, J, L // tk),
        in_specs=[pl.BlockSpec((tk, tm), lambda m, j, l: (l, m)), pl.BlockSpec((None, tk, N), lambda m, j, l: (j, l, 0))],
        out_specs=pl.BlockSpec((None, tm, N), lambda m, j, l: (j, m, 0)),
        out_shape=jax.ShapeDtypeStruct((J, M, N), F32),
        compiler_params=_cp("parallel", "parallel", "arbitrary"))(a, b)


def _row_tile(rows):
    for t in (512, 352, 256, 128, 64, 8):
        if rows % t == 0:
            return t
    return rows


def _add_half(g, r, c_arr, name, out_dtype=F32):
    _, _, R, C = g.shape
    tr = _row_tile(R)

    def body(c_ref, g_ref, r_ref, o_ref):
        o_ref[...] = (g_ref[...] + r_ref[...]).astype(out_dtype)

    return pl.pallas_call(
        body, name=name,
        grid_spec=pltpu.PrefetchScalarGridSpec(
            num_scalar_prefetch=1, grid=(g.shape[0], R // tr),
            in_specs=[pl.BlockSpec((None, None, tr, C), lambda j, i, c: (j, c[0], i, 0)),
                      pl.BlockSpec((None, tr, C), lambda j, i, c: (j, i, 0))],
            out_specs=pl.BlockSpec((None, tr, C), lambda j, i, c: (j, i, 0))),
        out_shape=jax.ShapeDtypeStruct(r.shape, out_dtype),
        compiler_params=_cp("parallel", "parallel"))(c_arr, g, r)


def _sum4(p, name):
    _, R, C = p.shape
    tr = _row_tile(R)

    def body(p_ref, o_ref):
        q = [p_ref[j].astype(F32) for j in range(4)]
        o_ref[...] = ((q[0] + q[1]) + q[2]) + q[3]

    return pl.pallas_call(
        body, name=name, grid=(R // tr,),
        in_specs=[pl.BlockSpec((4, tr, C), lambda i: (0, i, 0))],
        out_specs=pl.BlockSpec((tr, C), lambda i: (i, 0)),
        out_shape=jax.ShapeDtypeStruct((R, C), F32), compiler_params=_cp("parallel"))(p)


def _adamw_refs(w_ref, g_ref, m_ref, v_ref, d_ref, nm_ref, nv_ref):
    gv = g_ref[...]
    nm = ADAM_B1 * m_ref[...] + (1.0 - ADAM_B1) * gv
    nv = ADAM_B2 * v_ref[...] + (1.0 - ADAM_B2) * (gv * gv)
    m_hat = nm / (1.0 - ADAM_B1 ** ADAM_STEP)
    v_hat = nv / (1.0 - ADAM_B2 ** ADAM_STEP)
    d_ref[...] = -ADAM_LR * (m_hat / (jnp.sqrt(v_hat) + ADAM_EPS) + ADAM_WD * w_ref[...])
    nm_ref[...] = nm
    nv_ref[...] = nv


def _adamw_many(ws, gs, ms, vs, name):
    n = len(ws)

    def body(*refs):
        for k in range(n):
            _adamw_refs(*(refs[j * n + k] for j in range(7)))

    out_shape = [jax.ShapeDtypeStruct(w.shape, F32) for w in ws] * 3
    res = pl.pallas_call(body, name=name, out_shape=out_shape,
                         compiler_params=pltpu.CompilerParams(vmem_limit_bytes=VMEM_LIMIT))(*ws, *gs, *ms, *vs)
    return res[:n], res[n:2 * n], res[2 * n:]


def _join_rows(own, other, c_arr, name):
    R, C = own.shape
    tr = _row_tile(R)

    def body(c_ref, own_ref, other_ref, o_ref):
        o_ref[...] = jnp.where(pl.program_id(0) == c_ref[0], own_ref[...], other_ref[...])

    half = pl.BlockSpec((tr, C), lambda h, i, c: (i, 0))
    return pl.pallas_call(
        body, name=name,
        grid_spec=pltpu.PrefetchScalarGridSpec(
            num_scalar_prefetch=1, grid=(2, R // tr), in_specs=[half, half],
            out_specs=pl.BlockSpec((tr, C), lambda h, i, c: (h * (R // tr) + i, 0))),
        out_shape=jax.ShapeDtypeStruct((2 * R, C), F32),
        compiler_params=_cp("parallel", "parallel"))(c_arr, own, other)


def _adamw_halves(w, own, other, m, v, name, comm=None):
    R, C = own.shape
    tr = _row_tile(R)
    while tr * C * 4 > ADAMW_BLOCK_BYTES and tr % 16 == 0:
        tr //= 2

    def body(w_ref, own_ref, other_ref, m_ref, v_ref, g_ref, d_ref, nm_ref, nv_ref):
        g_ref[...] = jnp.where(pl.program_id(0) == lax.axis_index("c"), own_ref[...], other_ref[...])
        _adamw_refs(w_ref, g_ref, m_ref, v_ref, d_ref, nm_ref, nv_ref)

    half = pl.BlockSpec((tr, C), lambda h, i: (i, 0))
    full = pl.BlockSpec((tr, C), lambda h, i: (h * (R // tr) + i, 0))
    sh = jax.ShapeDtypeStruct((2 * R, C), F32)
    return _hosted_call(body, comm, name=name, grid=(2, R // tr), in_specs=[full, half, half, full, full],
                        out_specs=[full] * 4, out_shape=[sh] * 4, scratch_shapes=[], args=(w, own, other, m, v))


_ANY = pl.BlockSpec(memory_space=pl.ANY)


def _position():
    return lax.axis_index("x"), lax.axis_index("y"), lax.axis_index("c")


class _Comm:
    def __init__(self, arrs, out_shape, sems, start, finish):
        self.arrs, self.out_shape, self.sems, self.start, self.finish = arrs, out_shape, sems, start, finish


def _comm_call(comm, name):
    n, m = len(comm.arrs), len(comm.out_shape)

    def body(*refs):
        ins, outs, sems = refs[:n], refs[n:n + m], refs[n + m:]
        comm.start(ins, outs, sems)
        comm.finish(ins, outs, sems)

    return pl.pallas_call(
        body, name=name, in_specs=[_ANY] * n, out_specs=[_ANY] * m, out_shape=comm.out_shape,
        scratch_shapes=comm.sems, compiler_params=pltpu.CompilerParams(has_side_effects=True))(*comm.arrs)


def _hosted_call(body, comm, *, name, grid, in_specs, out_specs, out_shape, scratch_shapes, args):
    sem = ("arbitrary",) * len(grid)
    if comm is None:
        return pl.pallas_call(body, name=name, grid=grid, in_specs=in_specs, out_specs=out_specs, out_shape=out_shape,
                              scratch_shapes=scratch_shapes, compiler_params=_cp(*sem))(*args), []
    n_in, n_out, n_scr = len(in_specs), len(out_specs), len(scratch_shapes)
    ci, co = len(comm.arrs), len(comm.out_shape)

    def full(*refs):
        ins, refs = refs[:n_in], refs[n_in:]
        cins, refs = refs[:ci], refs[ci:]
        outs, refs = refs[:n_out], refs[n_out:]
        couts, refs = refs[:co], refs[co:]
        scr, csems = refs[:n_scr], refs[n_scr:]
        first, last = True, True
        for d, size in enumerate(grid):
            first = first & (pl.program_id(d) == 0)
            last = last & (pl.program_id(d) == size - 1)

        @pl.when(first)
        def _():
            comm.start(cins, couts, csems)

        body(*ins, *outs, *scr)

        @pl.when(last)
        def _():
            comm.finish(cins, couts, csems)

    res = pl.pallas_call(
        full, name=name, grid=grid, in_specs=list(in_specs) + [_ANY] * ci, out_specs=list(out_specs) + [_ANY] * co,
        out_shape=list(out_shape) + list(comm.out_shape), scratch_shapes=list(scratch_shapes) + list(comm.sems),
        compiler_params=_cp(*sem))(*args, *comm.arrs)
    return res[:n_out], res[n_out:]


def _comm_join(*comms):
    def parts(xs, attr):
        out, at = [], 0
        for cm in comms:
            n = len(getattr(cm, attr))
            out.append(xs[at:at + n])
            at += n
        return out

    def start(ins, outs, sems):
        for cm, i, o, s in zip(comms, parts(ins, "arrs"), parts(outs, "out_shape"), parts(sems, "sems")):
            cm.start(i, o, s)

    def finish(ins, outs, sems):
        for cm, i, o, s in zip(comms, parts(ins, "arrs"), parts(outs, "out_shape"), parts(sems, "sems")):
            cm.finish(i, o, s)

    cat = lambda attr: [x for cm in comms for x in getattr(cm, attr)]
    return _Comm(cat("arrs"), cat("out_shape"), cat("sems"), start, finish)


def _dma_sems(*counts):
    return [pltpu.SemaphoreType.DMA((n,)) for n in counts]


def _comm_pair_swap(arrs, half=False):
    n = len(arrs)
    out_shape = [jax.ShapeDtypeStruct(a.shape[:1] + a.shape[2:] if half else a.shape, a.dtype) for a in arrs]

    def copies(ins, outs, sems):
        x, y, c = _position()
        return [pltpu.make_async_remote_copy(
            src_ref=ins[k].at[:, 1 - c] if half else ins[k], dst_ref=outs[k], send_sem=sems[0].at[k],
            recv_sem=sems[1].at[k], device_id=(x, y, 1 - c), device_id_type=MESH) for k in range(n)]

    def start(ins, outs, sems):
        for cp in copies(ins, outs, sems):
            cp.start()

    def finish(ins, outs, sems):
        for cp in copies(ins, outs, sems):
            cp.wait()

    return _Comm(arrs, out_shape, _dma_sems(n, n), start, finish)


def _chip_of(j, c):
    return (jnp.right_shift(j, 1), jnp.bitwise_and(j, 1), c)


def _comm_chip_exchange(arrs, scatter):
    n = len(arrs)
    out_shape = [jax.ShapeDtypeStruct(a.shape if scatter else (4,) + a.shape, a.dtype) for a in arrs]

    def copies(ins, outs, sems):
        x, y, c = _position()
        me = 2 * x + y
        local, sent, landed = [], [], []
        for k in range(n):
            local.append(pltpu.make_async_copy(ins[k].at[me] if scatter else ins[k], outs[k].at[me], sems[2].at[k]))
            for d in (1, 2, 3):
                j = jnp.bitwise_xor(me, d)
                s = 3 * k + d - 1
                src = ins[k].at[j] if scatter else ins[k]
                for dst, group in ((outs[k].at[me], sent), (outs[k].at[j], landed)):
                    group.append(pltpu.make_async_remote_copy(
                        src_ref=src, dst_ref=dst, send_sem=sems[0].at[s], recv_sem=sems[1].at[s],
                        device_id=_chip_of(j, c), device_id_type=MESH))
        return local, sent, landed

    def start(ins, outs, sems):
        local, sent, _ = copies(ins, outs, sems)
        for cp in local + sent:
            cp.start()

    def finish(ins, outs, sems):
        local, sent, landed = copies(ins, outs, sems)
        for cp in sent:
            cp.wait_send()
        for cp in landed:
            cp.wait_recv()
        for cp in local:
            cp.wait()

    return _Comm(arrs, out_shape, _dma_sems(3 * n, 3 * n, n), start, finish)


LOCAL_PARTS = 4


def _comm_gather_split(shards, whole):
    n, nw = len(shards), len(whole)
    arrs = list(shards) + list(whole)
    out_shape = [jax.ShapeDtypeStruct((4,) + a.shape, a.dtype) for a in arrs]

    def copies(ins, outs, sems):
        x, y, c = _position()
        me = 2 * x + y
        local, sent, landed, passed, passed_in = [], [], [], [], []
        for k in range(n + nw):
            if k >= n:
                local.append(pltpu.make_async_copy(ins[k], outs[k].at[me], sems[4].at[LOCAL_PARTS * k]))
            else:
                part = shards[k].shape[0] // LOCAL_PARTS
                for r in range(LOCAL_PARTS):
                    local.append(pltpu.make_async_copy(ins[k].at[pl.ds(r * part, part)],
                                                       outs[k].at[me, pl.ds(r * part, part)],
                                                       sems[4].at[LOCAL_PARTS * k + r]))
            for d in (1, 2, 3):
                j = jnp.bitwise_xor(me, d)
                s = 3 * k + d - 1
                if k >= n:
                    src, mine, theirs = ins[k], outs[k].at[me], outs[k].at[j]
                else:
                    h = shards[k].shape[0] // 2
                    rows = pl.ds(pl.multiple_of(c * h, 16), h)
                    other = pl.ds(pl.multiple_of((1 - c) * h, 16), h)
                    src, mine, theirs = ins[k].at[rows], outs[k].at[me, rows], outs[k].at[j, rows]
                    for dst, group in ((theirs, passed), (outs[k].at[j, other], passed_in)):
                        group.append(pltpu.make_async_remote_copy(
                            src_ref=theirs, dst_ref=dst, send_sem=sems[2].at[s], recv_sem=sems[3].at[s],
                            device_id=(x, y, 1 - c), device_id_type=MESH))
                for dst, group in ((mine, sent), (theirs, landed)):
                    group.append(pltpu.make_async_remote_copy(
                        src_ref=src, dst_ref=dst, send_sem=sems[0].at[s], recv_sem=sems[1].at[s],
                        device_id=_chip_of(j, c), device_id_type=MESH))
        return local, sent, landed, passed, passed_in

    def start(ins, outs, sems):
        local, sent, _, _, _ = copies(ins, outs, sems)
        for cp in local + sent:
            cp.start()

    def finish(ins, outs, sems):
        local, sent, landed, passed, passed_in = copies(ins, outs, sems)
        for cp in landed[:3 * n]:
            cp.wait_recv()
        for cp in passed:
            cp.start()
        for cp in landed[3 * n:]:
            cp.wait_recv()
        for cp in sent:
            cp.wait_send()
        for cp in passed:
            cp.wait_send()
        for cp in passed_in:
            cp.wait_recv()
        for cp in local:
            cp.wait()

    t = 3 * (n + nw)
    return _Comm(arrs, out_shape, _dma_sems(t, t, max(3 * n, 1), max(3 * n, 1), LOCAL_PARTS * (n + nw)), start, finish)


def _pack(arrs, row_multiple):
    parts = []
    for a in arrs:
        flat = a.reshape(-1).astype(F32)
        pad = (-flat.shape[0]) % LANES
        parts.append(jnp.pad(flat, (0, pad)) if pad else flat)
    flat = jnp.concatenate(parts)
    rows = -(-flat.shape[0] // LANES)
    rows_p = -(-rows // row_multiple) * row_multiple
    return jnp.pad(flat, (0, rows_p * LANES - flat.shape[0])).reshape(rows_p, LANES)


def _unpack(packed, shapes):
    flat = packed.reshape(-1)
    outs, off = [], 0
    for sh in shapes:
        size = int(np.prod(sh))
        outs.append(flat[off:off + size].reshape(sh))
        off += size + (-size) % LANES
    return outs


SMALL = ["norm_mix_g", "pool_w", "pool_scale", "ssm_log_neg_a_re", "ssm_a_im", "ssm_log_dt", "ssm_b_re", "ssm_b_im",
         "ssm_c_re", "ssm_c_im", "ssm_d", "glu_b", "out_norm_pool_g", "out_norm_ssm_g", "norm_ffn_g", "conv_b",
         "final_norm_g"]
BIG = ["w_in", "glu_w", "w_out", "w_up", "w_down"]
WIDE = ["pool_w", "ssm_b_re", "ssm_b_im", "ssm_c_re", "ssm_c_im"]
WEIGHTS = ['norm_mix_g', 'w_in', 'pool_w', 'pool_scale', 'ssm_log_neg_a_re', 'ssm_a_im', 'ssm_log_dt', 'ssm_b_re',
           'ssm_b_im', 'ssm_c_re', 'ssm_c_im', 'ssm_d', 'glu_w', 'glu_b', 'out_norm_pool_g', 'out_norm_ssm_g', 'w_out',
           'norm_ffn_g', 'w_up', 'conv_w', 'conv_b', 'w_down', 'final_norm_g']


def _local_step(x, target, p, full, shards=None, c_arr=None):
    L, D = x.shape
    dist = shards is not None
    row = lambda a: a.reshape(1, -1)
    w_in = full["w_in"]
    pool_w_b = p["pool_w"].astype(BF16)
    g_mix, g_pool, g_ssm, g_ffn, g_fin = (row(p[k]) for k in (
        "norm_mix_g", "out_norm_pool_g", "out_norm_ssm_g", "norm_ffn_g", "final_norm_g"))
    pool_scale, ssm_d, glu_b, conv_b = (row(p[k]) for k in ("pool_scale", "ssm_d", "glu_b", "conv_b"))

    lnar = p["ssm_log_neg_a_re"].reshape(2 * N_SSM_GROUPS, SSM_STATE)
    aim = p["ssm_a_im"].reshape(2 * N_SSM_GROUPS, SSM_STATE)
    ldt = jnp.broadcast_to(p["ssm_log_dt"].reshape(2 * N_SSM_GROUPS, 1), lnar.shape)
    lam_re, lam_im, f_re, f_im = _ssm_params(lnar, aim, ldt)
    flat2 = lambda a: a.reshape(2, N_STATE)
    lam4 = jnp.stack([flat2(lam_re)[0], flat2(lam_im)[0], flat2(lam_re)[1], flat2(lam_im)[1]])
    tables = _scan_tables(lam4)
    per_group = (2, N_SSM_GROUPS, SSM_STATE)
    dense = _ssm_expand(p["ssm_b_re"], p["ssm_b_im"], p["ssm_c_re"], p["ssm_c_im"],
                        f_re.reshape(per_group + (1,)), f_im.reshape(per_group + (1,)))
    ssm_args = [tuple(dense[4 * d:4 * d + 4]) + (tables,) for d in range(2)]

    u, xn = _in_proj(x, g_mix, w_in)
    yn_pool = _pool_fwd(u, pool_w_b, pool_scale, g_pool)
    gather1 = _comm_gather_split([shards[k] for k in ("glu_w", "w_out", "w_down")], [shards["conv_w"]]) if dist else None
    (y0, s0r, s0i), got1 = _ssm_scan_fwd(u, *ssm_args[0], 0, False, comm=gather1)
    gather2 = _comm_gather_split([shards["w_up"]], []) if dist else None
    (y1, s1r, s1i), got2 = _ssm_scan_fwd(u, *ssm_args[1], 2, True, comm=gather2)
    if dist:
        glu_w, w_out, w_down = (g.reshape((-1,) + g.shape[2:]) for g in got1[:3])
        conv_w = jnp.transpose(got1[3], (1, 0, 2)).reshape(3, -1)
        w_up4 = got2[0]
    else:
        glu_w, w_out, w_up4, w_down, conv_w = (full[k] for k in ("glu_w", "w_out", "w_up", "w_down", "conv_w"))
    h1, hn, ycat = _mix_out(yn_pool, y0, y1, u, x, ssm_d, glu_w, glu_b, g_ssm, w_out, g_ffn)
    up = _ffn_up(hn, w_up4)
    a, c_val, c_gate, dh2, dh2_b, loss, g_final = _ffn_down_loss(up, conv_w, conv_b, w_down, h1, target, g_fin)

    d_val, d_gate, gbv, gbg = _ffn_act_bwd(c_val, c_gate, w_down, dh2_b)
    g_w_down = _matmul_tn(a, dh2_b, FF_BLK, D, "grad_w_down")
    d_up, dh1, dh1_b, g_ffn_g, gcw = _ffn_up_bwd(d_val, d_gate, up, conv_w, w_up4, h1, dh2, g_ffn)
    g_w_up = _matmul_tn_blocks(hn, d_up.reshape(4, L, FF_BLK), TM, "grad_w_up")
    g_w_out = _matmul_tn(ycat, dh1_b, TM, D, "grad_w_out")
    late = ("w_up", "w_down", "w_out", "glu_w")
    halves = [g_w_up.reshape(4, 2, D // 2, FF_BLK), g_w_down.reshape(4, 2, D_FF // 8, D)]
    (dy, du_direct, g_glu_w, g_glu_b, g_ssm_d, g_ssm_g), swapped = _ssm_bwd_local(
        dh1_b, y0, y1, u, ssm_d, glu_w, glu_b, g_ssm, w_out, comm=_comm_pair_swap(halves, half=True) if dist else None)
    more = [g_w_out.reshape(4, 2, D // 8, D), g_glu_w.reshape(4, 2, D_SSM // 8, D_SSM)]
    (d_pooled, g_pool_w, g_pool_scale, g_pool_g), swapped_more = _pool_bwd_local(
        dh1_b, u, w_out, pool_w_b, pool_scale, g_pool, comm=_comm_pair_swap(more, half=True) if dist else None)
    halves, from_sibling = halves + more, list(swapped) + list(swapped_more)
    du_pool = _pool_bwd_window(d_pooled)
    reduce2 = None
    if dist:
        chip_sums = [_add_half(h, r, c_arr, "sum_pair_" + k, BF16) for k, h, r in zip(late, halves, from_sibling)]
        reduce2 = _comm_chip_exchange(chip_sums, scatter=True)
    (du0, gb0r, gb0i, gc0r, gc0i, gv0), from_chips = _ssm_scan_bwd(dy, u, s0r, s0i, *ssm_args[0], 1, True, comm=reduce2)
    mine = [_sum4(r, "sum_chips_" + k) for k, r in zip(late, from_chips)]
    (du1, gb1r, gb1i, gc1r, gc1i, gv1), theirs = _ssm_scan_bwd(
        dy, u, s1r, s1i, *ssm_args[1], 3, False, comm=_comm_pair_swap(mine) if dist else None)
    by_state = (2, N_SSM_GROUPS, 1, SSM_STATE)
    g_b_re, g_b_im, g_f_re, g_f_im = _ssm_unfold(
        jnp.stack([gb0r, gb1r]), jnp.stack([gb0i, gb1i]),
        jnp.swapaxes(p["ssm_b_re"], 2, 3), jnp.swapaxes(p["ssm_b_im"], 2, 3),
        f_re.reshape(by_state), f_im.reshape(by_state))
    gvec = lambda j: jnp.stack([gv0[j], gv1[j]]).reshape(2 * N_SSM_GROUPS, SSM_STATE)
    g_lnar, g_aim, g_ldt = _ssm_params_bwd(lnar, aim, ldt, gvec(0), gvec(1),
                                           g_f_re.reshape(lnar.shape), g_f_im.reshape(lnar.shape))
    grad_x, d_u_b, g_mix_g = _in_bwd(du_pool, du_direct, du0, du1, dh1, x, g_mix, w_in)
    g_w_in = _matmul_tn(xn, d_u_b, TM, D, "grad_w_in")

    small = {
        "norm_mix_g": g_mix_g, "pool_w": g_pool_w, "pool_scale": g_pool_scale,
        "ssm_log_neg_a_re": g_lnar, "ssm_a_im": g_aim, "ssm_log_dt": g_ldt,
        "ssm_b_re": jnp.swapaxes(g_b_re, 2, 3), "ssm_b_im": jnp.swapaxes(g_b_im, 2, 3),
        "ssm_c_re": jnp.stack([gc0r, gc1r]), "ssm_c_im": jnp.stack([gc0i, gc1i]),
        "ssm_d": g_ssm_d, "glu_b": g_glu_b, "out_norm_pool_g": g_pool_g, "out_norm_ssm_g": g_ssm_g,
        "norm_ffn_g": g_ffn_g, "conv_b": jnp.concatenate([gbv[0], gbg[0]]), "final_norm_g": g_final,
        "conv_w": jnp.transpose(gcw, (2, 0, 1, 3)).reshape(3, -1),
    }
    big = {"w_in": g_w_in}
    reduced = dict(zip(late, zip(mine, theirs)))
    if not dist:
        big.update({"w_up": g_w_up, "w_down": g_w_down, "w_out": g_w_out, "glu_w": g_glu_w})
    return loss, grad_x, small, big, reduced


def kernel(x, norm_mix_g, w_in, pool_w, pool_scale, ssm_log_neg_a_re, ssm_a_im, ssm_log_dt, ssm_b_re, ssm_b_im, ssm_c_re, ssm_c_im, ssm_d, glu_w, glu_b, out_norm_pool_g, out_norm_ssm_g, w_out, norm_ffn_g, w_up, conv_w, conv_b, w_down, final_norm_g, loss_target, m_norm_mix_g, m_w_in, m_pool_w, m_pool_scale, m_ssm_log_neg_a_re, m_ssm_a_im, m_ssm_log_dt, m_ssm_b_re, m_ssm_b_im, m_ssm_c_re, m_ssm_c_im, m_ssm_d, m_glu_w, m_glu_b, m_out_norm_pool_g, m_out_norm_ssm_g, m_w_out, m_norm_ffn_g, m_w_up, m_conv_w, m_conv_b, m_w_down, m_final_norm_g, v_norm_mix_g, v_w_in, v_pool_w, v_pool_scale, v_ssm_log_neg_a_re, v_ssm_a_im, v_ssm_log_dt, v_ssm_b_re, v_ssm_b_im, v_ssm_c_re, v_ssm_c_im, v_ssm_d, v_glu_w, v_glu_b, v_out_norm_pool_g, v_out_norm_ssm_g, v_w_out, v_norm_ffn_g, v_w_up, v_conv_w, v_conv_b, v_w_down, v_final_norm_g):
    args = locals()
    w = {k: args[k] for k in WEIGHTS}
    m = {k: args["m_" + k] for k in WEIGHTS}
    v = {k: args["v_" + k] for k in WEIGHTS}
    chip = 2 * lax.axis_index("x") + lax.axis_index("y")
    c_arr = lax.axis_index("c").astype(jnp.int32).reshape(1)

    shards = {k: w[k].astype(BF16) for k in BIG}
    shards["conv_w"] = conv_w
    w_in_full = _comm_call(_comm_gather_split([shards["w_in"]], []), "gather_w_in")[0]
    loss, grad_x, g_small, g_big, reduced = _local_step(
        x[0], loss_target[0], w, {"w_in": w_in_full.reshape(-1, w_in_full.shape[-1])}, shards, c_arr)

    exact = [k for k in SMALL if k not in WIDE]
    packs = [_pack([loss] + [g_small[k] for k in exact] + [g_small["conv_w"]], 512),
             _pack([g_small[k] for k in WIDE], 512)]
    halves = [g_big["w_in"].reshape(4, 2, g_big["w_in"].shape[0] // 8, -1)]
    halves += [pk.reshape(1, 2, pk.shape[0] // 2, LANES) for pk in packs]
    from_sibling = _comm_call(_comm_pair_swap(halves, half=True), "reduce_pair")
    names = ("w_in", "exact", "wide")
    sums = [_add_half(h, r, c_arr, "sum_pair_" + k, dt)
            for k, h, r, dt in zip(names, halves, from_sibling, (BF16, F32, BF16))]
    grads, delta, new_m, new_v = {}, {}, {}, {}

    def adamw_behind(k, comm):
        own, other = reduced[k]
        (grads[k], delta[k], new_m[k], new_v[k]), got = _adamw_halves(
            w[k], own, other, m[k], v[k], "adamw_" + k, comm=comm)
        return got

    from_chips = adamw_behind("w_up", _comm_join(_comm_chip_exchange(sums[:1], scatter=True),
                                                  _comm_chip_exchange([s[0] for s in sums[1:]], scatter=False)))
    mine = [_sum4(r, "sum_chips_" + k) for k, r in zip(names, from_chips)]
    theirs = adamw_behind("w_down", _comm_pair_swap(mine))
    for k in ("w_out", "glu_w"):
        adamw_behind(k, None)
    exact_all = _join_rows(mine[1], theirs[1], c_arr, "join_exact")
    wide_all = _join_rows(mine[2], theirs[2], c_arr, "join_wide")
    shapes = [loss.shape] + [w[k].shape for k in exact] + [(3, 4 * FF_BLK)]
    grads.update(zip(["loss"] + exact + ["conv_w_full"], _unpack(exact_all, shapes)))
    grads.update(zip(WIDE, _unpack(wide_all, [w[k].shape for k in WIDE])))
    loss = grads.pop("loss")[0, 0]
    grads["conv_w"] = lax.dynamic_slice_in_dim(grads.pop("conv_w_full"), chip * FF_BLK, FF_BLK, axis=1)

    reduced["w_in"] = (mine[0], theirs[0])
    adamw_behind("w_in", None)
    padded = ["ssm_b_re", "ssm_b_im"]
    for keys, name in ((padded, "adamw_ssm_b"), ([k for k in SMALL + ["conv_w"] if k not in padded], "adamw_small")):
        outs = _adamw_many(*([d[k] for k in keys] for d in (w, grads, m, v)), name)
        for d, o in zip((delta, new_m, new_v), outs):
            d.update(zip(keys, o))

    return (loss, grad_x[None], *[grads[k] for k in WEIGHTS], *[delta[k] for k in WEIGHTS],
            *[new_m[k] for k in WEIGHTS], *[new_v[k] for k in WEIGHTS])
```

```python
import numpy as np
import jax
import jax.numpy as jnp
from jax import lax
from jax.experimental import pallas as pl
from jax.experimental.pallas import tpu as pltpu

F32 = jnp.float32
BF16 = jnp.bfloat16
MESH = pl.DeviceIdType.MESH

EPS = 1e-6
POOL_WINDOWS = (2, 4, 8, 16)
POOL_GROUP = 128
SSM_GROUP = 16
SSM_STATE = 64
N_SSM_GROUPS = 32
N_STATE = N_SSM_GROUPS * SSM_STATE
QUAD = 256
N_QUAD = N_STATE // QUAD
SLAB = 256
D_SSM = 512
D_POOL = 512
D_FF = 2816
FF_BLK = 1408
HALO = 8
HALO_B = 16
LANES = 128
ADAM_LR, ADAM_B1, ADAM_B2, ADAM_EPS, ADAM_WD, ADAM_STEP = 0.001, 0.9, 0.999, 1e-08, 0.01, 10
VMEM_LIMIT = 56 * 2 ** 20
ADAMW_BLOCK_BYTES = 2 ** 20

TL = 512
TM = 1024
TF = 256
TC = 512
SEG = 8
SEG_LEN = TC // SEG
SCAN_W = 512


def _cp(*sem):
    return pltpu.CompilerParams(dimension_semantics=sem, vmem_limit_bytes=VMEM_LIMIT)


def _dot_nn(a, b):
    return jnp.dot(a, b, preferred_element_type=F32)


def _dot_nt(a, b):
    return lax.dot_general(a, b, (((1,), (1,)), ((), ())), preferred_element_type=F32)


def _dot_tn(a, b):
    return lax.dot_general(a, b, (((0,), (0,)), ((), ())), preferred_element_type=F32)


def _rms_fwd(x, g):
    inv = lax.rsqrt(jnp.mean(x * x, axis=-1, keepdims=True) + EPS)
    xh = x * inv
    return xh * g, xh, inv


def _rms_bwd(dy, xh, inv, g):
    dg = jnp.sum(dy * xh, axis=0, keepdims=True)
    dxh = dy * g
    dx = inv * (dxh - xh * jnp.mean(dxh * xh, axis=-1, keepdims=True))
    return dx, dg


_GELU_C = 0.7978845608028654
_GELU_A = 0.044715


def _gelu(y):
    t = jnp.tanh(_GELU_C * (y + _GELU_A * (y * y * y)))
    return 0.5 * y * (1.0 + t), t


def _gelu_grad(y, t):
    return 0.5 * (1.0 + t) + 0.5 * y * (1.0 - t * t) * (_GELU_C * (1.0 + 3.0 * _GELU_A * y * y))


def _sigmoid(x):
    return 1.0 / (1.0 + jnp.exp(-x))


def _full(shape):
    n = len(shape)
    return pl.BlockSpec(shape, lambda *_: (0,) * n)


def _fill_ext(ext_ref, prev_ref, cur_ref, next_ref, i, n, rows):
    ext_ref[0:HALO, :] = jnp.where(i > 0, prev_ref[...], 0.0).astype(ext_ref.dtype)
    ext_ref[HALO:HALO + rows, :] = cur_ref[...]
    ext_ref[HALO + rows:2 * HALO + rows, :] = jnp.where(i < n - 1, next_ref[...], 0.0).astype(ext_ref.dtype)


def _in_proj(x, g, w):
    L, D = x.shape
    E = w.shape[1]

    def body(x_ref, g_ref, w_ref, u_ref, xn_ref):
        y, _, _ = _rms_fwd(x_ref[...], g_ref[...])
        yb = y.astype(BF16)
        xn_ref[...] = yb
        u_ref[...] = _dot_nn(yb, w_ref[...])

    return pl.pallas_call(
        body, name="in_proj", grid=(L // TL,),
        in_specs=[pl.BlockSpec((TL, D), lambda i: (i, 0)), _full((1, D)), _full(w.shape)],
        out_specs=[pl.BlockSpec((TL, E), lambda i: (i, 0)), pl.BlockSpec((TL, D), lambda i: (i, 0))],
        out_shape=[jax.ShapeDtypeStruct((L, E), F32), jax.ShapeDtypeStruct((L, D), BF16)],
        compiler_params=_cp("parallel"))(x, g, w)


def _halo_specs_1d(rows, width, L, col):
    rb = rows // HALO
    last = L // HALO - 1
    return [pl.BlockSpec((HALO, width), lambda i: (jnp.maximum(i * rb - 1, 0), col)),
            pl.BlockSpec((rows, width), lambda i: (i, col)),
            pl.BlockSpec((HALO, width), lambda i: (jnp.minimum((i + 1) * rb, last), col))]


def _pooled_from_ext(ext_ref, t0, rows, L):
    t = t0 + lax.broadcasted_iota(jnp.int32, (rows, 1), 0)
    outs = []
    for gi, w in enumerate(POOL_WINDOWS):
        half = w // 2
        cs = slice(gi * POOL_GROUP, (gi + 1) * POOL_GROUP)
        acc = ext_ref[pl.ds(HALO - half, rows), cs]
        for s in range(-half + 1, half):
            acc = acc + ext_ref[pl.ds(HALO + s, rows), cs]
        cnt = (jnp.minimum(t + half, L) - jnp.maximum(t - half, 0)).astype(F32)
        outs.append(acc / cnt - ext_ref[pl.ds(HALO, rows), cs])
    return outs


def _pool_fwd(u, pool_w_b, pool_scale, g_pool):
    L = u.shape[0]
    n = L // TL

    def body(prev_ref, cur_ref, next_ref, pw_ref, ps_ref, g_ref, out_ref, ext_ref):
        i = pl.program_id(0)
        _fill_ext(ext_ref, prev_ref, cur_ref, next_ref, i, n, TL)
        pooled = _pooled_from_ext(ext_ref, i * TL, TL, L)
        ypre = jnp.concatenate([_dot_nn(pooled[gi].astype(BF16), pw_ref[gi]) for gi in range(4)], axis=-1)
        yn, _, _ = _rms_fwd(ypre * ps_ref[...], g_ref[...])
        out_ref[...] = yn.astype(BF16)

    return pl.pallas_call(
        body, name="pool_fwd", grid=(n,),
        in_specs=_halo_specs_1d(TL, D_POOL, L, 0) + [_full(pool_w_b.shape), _full((1, D_POOL)), _full((1, D_POOL))],
        out_specs=pl.BlockSpec((TL, D_POOL), lambda i: (i, 0)),
        out_shape=jax.ShapeDtypeStruct((L, D_POOL), BF16),
        scratch_shapes=[pltpu.VMEM((TL + 2 * HALO, D_POOL), F32)],
        compiler_params=_cp("parallel"))(u, u, u, pool_w_b, pool_scale, g_pool)


def _pool_bwd_local(dh1, u, w_out_b, pool_w_b, pool_scale, g_pool, comm=None):
    L = u.shape[0]
    n = L // TL
    D = dh1.shape[1]

    def body(dh_ref, prev_ref, cur_ref, next_ref, wo_ref, pw_ref, ps_ref, g_ref,
             dp_ref, gpw_ref, gps_ref, gg_ref, ext_ref):
        i = pl.program_id(0)

        @pl.when(i == 0)
        def _():
            gpw_ref[...] = jnp.zeros_like(gpw_ref)
            gps_ref[...] = jnp.zeros_like(gps_ref)
            gg_ref[...] = jnp.zeros_like(gg_ref)

        _fill_ext(ext_ref, prev_ref, cur_ref, next_ref, i, n, TL)
        pooled = [p.astype(BF16) for p in _pooled_from_ext(ext_ref, i * TL, TL, L)]
        ypre = jnp.concatenate([_dot_nn(pooled[gi], pw_ref[gi]) for gi in range(4)], axis=-1)
        ps = ps_ref[...]
        g = g_ref[...]
        _, xh, inv = _rms_fwd(ypre * ps, g)
        d_yn = _dot_nt(dh_ref[...], wo_ref[...])
        d_y, dg = _rms_bwd(d_yn, xh, inv, g)
        gg_ref[...] += dg
        gps_ref[...] += jnp.sum(d_y * ypre, axis=0, keepdims=True)
        d_ypre = (d_y * ps).astype(BF16)
        for gi in range(4):
            cs = slice(gi * POOL_GROUP, (gi + 1) * POOL_GROUP)
            dp_ref[:, cs] = _dot_nt(d_ypre[:, cs], pw_ref[gi])
            gpw_ref[gi] += _dot_tn(pooled[gi], d_ypre[:, cs])

    return _hosted_call(
        body, comm, name="pool_bwd_local", grid=(n,),
        in_specs=[pl.BlockSpec((TL, D), lambda i: (i, 0))] + _halo_specs_1d(TL, D_POOL, L, 0)
        + [pl.BlockSpec((D_POOL, D), lambda i: (0, 0)), _full(pool_w_b.shape), _full((1, D_POOL)), _full((1, D_POOL))],
        out_specs=[pl.BlockSpec((TL, D_POOL), lambda i: (i, 0)), _full(pool_w_b.shape),
                   _full((1, D_POOL)), _full((1, D_POOL))],
        out_shape=[jax.ShapeDtypeStruct((L, D_POOL), F32), jax.ShapeDtypeStruct(pool_w_b.shape, F32),
                   jax.ShapeDtypeStruct((1, D_POOL), F32), jax.ShapeDtypeStruct((1, D_POOL), F32)],
        scratch_shapes=[pltpu.VMEM((TL + 2 * HALO, D_POOL), F32)],
        args=(dh1, u, u, u, w_out_b, pool_w_b, pool_scale, g_pool))


def _pool_bwd_window(d_pooled):
    L = d_pooled.shape[0]
    n = L // TL
    R = TL + 2 * HALO

    def body(prev_ref, cur_ref, next_ref, out_ref, ext_ref, q_ref):
        i = pl.program_id(0)
        _fill_ext(ext_ref, prev_ref, cur_ref, next_ref, i, n, TL)
        tr = i * TL - HALO + lax.broadcasted_iota(jnp.int32, (R, 1), 0)
        for gi, w in enumerate(POOL_WINDOWS):
            half = w // 2
            cs = slice(gi * POOL_GROUP, (gi + 1) * POOL_GROUP)
            cnt = jnp.maximum(jnp.minimum(tr + half, L) - jnp.maximum(tr - half, 0), 1).astype(F32)
            q_ref[:, cs] = ext_ref[:, cs] / cnt
        for gi, w in enumerate(POOL_WINDOWS):
            half = w // 2
            cs = slice(gi * POOL_GROUP, (gi + 1) * POOL_GROUP)
            acc = q_ref[pl.ds(HALO - half + 1, TL), cs]
            for s in range(-half + 2, half + 1):
                acc = acc + q_ref[pl.ds(HALO + s, TL), cs]
            out_ref[:, cs] = acc - ext_ref[pl.ds(HALO, TL), cs]

    return pl.pallas_call(
        body, name="pool_bwd_window", grid=(n,),
        in_specs=_halo_specs_1d(TL, D_POOL, L, 0),
        out_specs=pl.BlockSpec((TL, D_POOL), lambda i: (i, 0)),
        out_shape=jax.ShapeDtypeStruct((L, D_POOL), F32),
        scratch_shapes=[pltpu.VMEM((R, D_POOL), F32), pltpu.VMEM((R, D_POOL), F32)],
        compiler_params=_cp("parallel"))(d_pooled, d_pooled, d_pooled)


def _ssm_param_fn(lnar, aim, ldt):
    dt = jnp.exp(ldt)
    a_re = -jnp.exp(lnar)
    mag = jnp.exp(a_re * dt)
    ang = aim * dt
    lr, li = mag * jnp.cos(ang), mag * jnp.sin(ang)
    den = a_re * a_re + aim * aim
    fr = ((lr - 1.0) * a_re + li * aim) / den
    fi = (li * a_re - (lr - 1.0) * aim) / den
    return lr, li, fr, fi


def _ssm_params(lnar, aim, ldt):
    def body(a_ref, b_ref, c_ref, lr_ref, li_ref, fr_ref, fi_ref):
        lr, li, fr, fi = _ssm_param_fn(a_ref[...], b_ref[...], c_ref[...])
        lr_ref[...] = lr
        li_ref[...] = li
        fr_ref[...] = fr
        fi_ref[...] = fi

    sh = jax.ShapeDtypeStruct(lnar.shape, F32)
    return pl.pallas_call(body, name="ssm_params", out_shape=[sh] * 4)(lnar, aim, ldt)


def _ssm_params_bwd(lnar, aim, ldt, glr, gli, gfr, gfi):
    def body(a_ref, b_ref, c_ref, g0, g1, g2, g3, da_ref, db_ref, dc_ref):
        _, vjp = jax.vjp(_ssm_param_fn, a_ref[...], b_ref[...], c_ref[...])
        da, db, dc = vjp((g0[...], g1[...], g2[...], g3[...]))
        da_ref[...] = da
        db_ref[...] = db
        dc_ref[...] = jnp.sum(dc, axis=1, keepdims=True)

    return pl.pallas_call(
        body, name="ssm_params_bwd",
        out_shape=[jax.ShapeDtypeStruct(lnar.shape, F32), jax.ShapeDtypeStruct(aim.shape, F32),
                   jax.ShapeDtypeStruct((ldt.shape[0], 1), F32)])(lnar, aim, ldt, glr, gli, gfr, gfi)


def _scan_tables(lam4):
    def build(lr, li, reverse, out_ref, k):
        pr, pi = lr, li
        for d in range(SEG_LEN):
            j = SEG_LEN - 1 - d if reverse else d
            out_ref[k, 0, j:j + 1, :] = pr
            out_ref[k, 1, j:j + 1, :] = pi
            pr, pi = pr * lr - pi * li, pr * li + pi * lr

    def body(lam_ref, out_ref):
        l0r, l0i, l1r, l1i = (lam_ref[j:j + 1, :] for j in range(4))
        build(l0r, l0i, False, out_ref, 0)
        build(l0r, -l0i, True, out_ref, 1)
        build(l1r, l1i, True, out_ref, 2)
        build(l1r, -l1i, False, out_ref, 3)

    return pl.pallas_call(body, name="scan_tables",
                          out_shape=jax.ShapeDtypeStruct((4, 2, SEG_LEN, N_STATE), F32))(lam4)


def _b_block(g):
    q, gl = divmod(g, 4)
    r0, c0 = gl * SSM_STATE, (q % 4) * 4 * SSM_GROUP + gl * SSM_GROUP
    return q, slice(r0, r0 + SSM_STATE), slice(c0, c0 + SSM_GROUP)


def _c_block(g):
    q, rows, cols = _b_block(g)
    return q, cols, rows


def _ssm_expand(b_re, b_im, c_re, c_im, f_re, f_im):
    def body(bre_ref, bim_ref, cre_ref, cim_ref, fre_ref, fim_ref, *rest):
        outs, tmp, bbr_ref, bbi_ref = rest[:8], rest[8], rest[9], rest[10]
        fr, fi, br, bi = fre_ref[...], fim_ref[...], bre_ref[...], bim_ref[...]
        bbr_ref[...] = fr * br - fi * bi
        bbi_ref[...] = fr * bi + fi * br
        for d in range(2):
            for j, (src, where) in enumerate(((bbr_ref, _b_block), (bbi_ref, _b_block),
                                              (cre_ref, _c_block), (cim_ref, _c_block))):
                tmp[...] = jnp.zeros_like(tmp)
                for g in range(N_SSM_GROUPS):
                    q, rows, cols = where(g)
                    tmp[q, rows, cols] = src[d, g]
                outs[4 * d + j][...] = tmp[...].astype(BF16)

    dense = jax.ShapeDtypeStruct((N_QUAD, QUAD, SLAB), BF16)
    return pl.pallas_call(body, name="ssm_expand", out_shape=[dense] * 8,
                          scratch_shapes=[pltpu.VMEM((N_QUAD, QUAD, SLAB), F32), pltpu.VMEM(b_re.shape, F32),
                                          pltpu.VMEM(b_re.shape, F32)],
                          compiler_params=pltpu.CompilerParams(vmem_limit_bytes=VMEM_LIMIT))(
                              b_re, b_im, c_re, c_im, f_re, f_im)


def _ssm_unfold(gbb_re, gbb_im, b_re_t, b_im_t, f_re, f_im):
    def body(gr_ref, gi_ref, br_ref, bi_ref, fr_ref, fi_ref, obr_ref, obi_ref, ofr_ref, ofi_ref):
        gr, gi, br, bi, fr, fi = (r[...] for r in (gr_ref, gi_ref, br_ref, bi_ref, fr_ref, fi_ref))
        obr_ref[...] = fr * gr + fi * gi
        obi_ref[...] = fr * gi - fi * gr
        ofr_ref[...] = jnp.sum(br * gr + bi * gi, axis=2, keepdims=True)
        ofi_ref[...] = jnp.sum(br * gi - bi * gr, axis=2, keepdims=True)

    gb = jax.ShapeDtypeStruct(gbb_re.shape, F32)
    gf = jax.ShapeDtypeStruct(f_re.shape, F32)
    return pl.pallas_call(body, name="ssm_unfold", out_shape=[gb, gb, gf, gf])(
        gbb_re, gbb_im, b_re_t, b_im_t, f_re, f_im)


_SEGMENT_ORDER = np.zeros((TC, TC), np.float32)
for _p in range(TC):
    _SEGMENT_ORDER[_p, (_p % SEG) * SEG_LEN + _p // SEG] = 1.0


def _store_tokens(ref, col0, val, tmp_ref):
    for h in range(val.shape[1] // LANES):
        for j in range(SEG_LEN):
            tmp_ref[pl.ds(h * TC + j, SEG, stride=SEG_LEN), :] = val[SEG * j:SEG * (j + 1), h * LANES:(h + 1) * LANES]
        ref[:, col0 + h * LANES:col0 + (h + 1) * LANES] = tmp_ref[pl.ds(h * TC, TC), :]


def _segment_scan(src_re, src_im, dst_re, dst_im, tab_ref, k, carry_re, carry_im, reverse, s_refs=None):
    lam1, lam_seg = (SEG_LEN - 1, 0) if reverse else (0, SEG_LEN - 1)
    token = (lambda i: SEG_LEN - 1 - i) if reverse else (lambda i: i)
    row_id = lax.broadcasted_iota(jnp.int32, (SEG, SCAN_W), 0)
    zero = jnp.zeros((SEG, SCAN_W), F32)
    sums = []
    for lt in range(N_STATE // SCAN_W):
        sl = slice(lt * SCAN_W, (lt + 1) * SCAN_W)
        lr = jnp.broadcast_to(tab_ref[k, 0, lam1:lam1 + 1, sl], (SEG, SCAN_W))
        li = jnp.broadcast_to(tab_ref[k, 1, lam1:lam1 + 1, sl], (SEG, SCAN_W))

        def local(i, c, sl=sl, lr=lr, li=li):
            rows = pl.ds(pl.multiple_of(token(i) * SEG, SEG), SEG)
            nr = lr * c[0] - li * c[1] + src_re[rows, sl]
            ni = lr * c[1] + li * c[0] + src_im[rows, sl]
            dst_re[rows, sl] = nr
            dst_im[rows, sl] = ni
            return nr, ni

        er, ei = lax.fori_loop(0, SEG_LEN, local, (zero, zero))

        sr_, si_ = tab_ref[k, 0, lam_seg:lam_seg + 1, sl], tab_ref[k, 1, lam_seg:lam_seg + 1, sl]
        c_r, c_i = carry_re[0:1, sl], carry_im[0:1, sl]
        in_r, in_i = zero, zero
        for r in (range(SEG - 1, -1, -1) if reverse else range(SEG)):
            in_r = jnp.where(row_id == r, c_r, in_r)
            in_i = jnp.where(row_id == r, c_i, in_i)
            c_r, c_i = (er[r:r + 1, :] + sr_ * c_r - si_ * c_i, ei[r:r + 1, :] + sr_ * c_i + si_ * c_r)
        carry_re[0:1, sl] = c_r
        carry_im[0:1, sl] = c_i

        def fix(i, c, sl=sl, in_r=in_r, in_i=in_i):
            j = token(i)
            rows = pl.ds(pl.multiple_of(j * SEG, SEG), SEG)
            pr = jnp.broadcast_to(tab_ref[k, 0, pl.ds(j, 1), sl], (SEG, SCAN_W))
            pi = jnp.broadcast_to(tab_ref[k, 1, pl.ds(j, 1), sl], (SEG, SCAN_W))
            nr = dst_re[rows, sl] + pr * in_r - pi * in_i
            ni = dst_im[rows, sl] + pr * in_i + pi * in_r
            dst_re[rows, sl] = nr
            dst_im[rows, sl] = ni
            if s_refs is None:
                return c
            sr = s_refs[0][rows, sl]
            si = s_refs[1][rows, sl]
            return nr, ni, c[2] + c[0] * sr + c[1] * si, c[3] + c[1] * sr - c[0] * si

        if s_refs is None:
            lax.fori_loop(0, SEG_LEN, fix, 0)
        else:
            out = lax.fori_loop(0, SEG_LEN, fix, (in_r, in_i, zero, zero))
            sums.append((jnp.sum(out[2], axis=0, keepdims=True), jnp.sum(out[3], axis=0, keepdims=True)))
    return sums


def _ssm_scan_fwd(u, b_re, b_im, c_re, c_im, tables, k, reverse, comm=None):
    L = u.shape[0]
    nc = L // TC
    chunk = (lambda i: nc - 1 - i) if reverse else (lambda i: i)
    order = jnp.asarray(_SEGMENT_ORDER, BF16)

    def body(u_ref, ord_ref, bre_ref, bim_ref, cre_ref, cim_ref, tab_ref,
             y_ref, sre_ref, sim_ref, in_re, in_im, carry_re, carry_im, tmp_ref):
        @pl.when(pl.program_id(0) == 0)
        def _():
            carry_re[...] = jnp.zeros_like(carry_re)
            carry_im[...] = jnp.zeros_like(carry_im)

        ub = _dot_nn(ord_ref[...], u_ref[...].astype(BF16)).astype(BF16)
        for q in range(N_QUAD):
            qs = slice(q * QUAD, (q + 1) * QUAD)
            us = ub[:, (q // 4) * SLAB:(q // 4 + 1) * SLAB]
            in_re[:, qs] = _dot_nt(us, bre_ref[q])
            in_im[:, qs] = _dot_nt(us, bim_ref[q])
        _segment_scan(in_re, in_im, sre_ref, sim_ref, tab_ref, k, carry_re, carry_im, reverse)
        for j in range(D_SSM // SLAB):
            acc = jnp.zeros((TC, SLAB), F32)
            for q in range(4 * j, 4 * j + 4):
                qs = slice(q * QUAD, (q + 1) * QUAD)
                acc = acc + _dot_nt(sre_ref[:, qs].astype(BF16), cre_ref[q])
                acc = acc - _dot_nt(sim_ref[:, qs].astype(BF16), cim_ref[q])
            _store_tokens(y_ref, j * SLAB, acc, tmp_ref)

    return _hosted_call(
        body, comm, name="ssm_scan_rev" if reverse else "ssm_scan_fwd", grid=(nc,),
        in_specs=[pl.BlockSpec((TC, D_SSM), lambda i: (chunk(i), 1)), _full(order.shape)]
        + [_full(b_re.shape)] * 4 + [_full(tables.shape)],
        out_specs=[pl.BlockSpec((TC, D_SSM), lambda i: (chunk(i), 0)),
                   pl.BlockSpec((TC, N_STATE), lambda i: (chunk(i), 0)),
                   pl.BlockSpec((TC, N_STATE), lambda i: (chunk(i), 0))],
        out_shape=[jax.ShapeDtypeStruct((L, D_SSM), F32), jax.ShapeDtypeStruct((L, N_STATE), F32),
                   jax.ShapeDtypeStruct((L, N_STATE), F32)],
        scratch_shapes=[pltpu.VMEM((TC, N_STATE), F32), pltpu.VMEM((TC, N_STATE), F32),
                        pltpu.VMEM((8, N_STATE), F32), pltpu.VMEM((8, N_STATE), F32),
                        pltpu.VMEM((SLAB // LANES * TC, LANES), F32)],
        args=(u, order, b_re, b_im, c_re, c_im, tables))


def _quad_channels(q):
    c0 = (q // 4) * SLAB + (q % 4) * 4 * SSM_GROUP
    return slice(c0, c0 + 4 * SSM_GROUP)


def _ssm_scan_bwd(dy, u, s_re, s_im, b_re, b_im, c_re, c_im, tables, k, reverse, comm=None):
    L = u.shape[0]
    nc = L // TC
    chunk = (lambda i: nc - 1 - i) if reverse else (lambda i: i)

    order = jnp.asarray(_SEGMENT_ORDER, BF16)

    def body(dy_ref, u_ref, ord_ref, sre_ref, sim_ref, bre_ref, bim_ref, cre_ref, cim_ref, tab_ref,
             du_ref, ob_re, ob_im, oc_re, oc_im, gv_ref,
             a_re, a_im, carry_re, carry_im, gbr_ref, gbi_ref, gcr_ref, gci_ref, tmp_ref):
        @pl.when(pl.program_id(0) == 0)
        def _():
            carry_re[...] = jnp.zeros_like(carry_re)
            carry_im[...] = jnp.zeros_like(carry_im)
            for r in (gbr_ref, gbi_ref, gcr_ref, gci_ref, gv_ref):
                r[...] = jnp.zeros_like(r)

        dyb = _dot_nn(ord_ref[...], dy_ref[...].astype(BF16)).astype(BF16)
        ub = _dot_nn(ord_ref[...], u_ref[...].astype(BF16)).astype(BF16)
        for q in range(N_QUAD):
            qs = slice(q * QUAD, (q + 1) * QUAD)
            ds = dyb[:, (q // 4) * SLAB:(q // 4 + 1) * SLAB]
            a_re[:, qs] = _dot_nn(ds, cre_ref[q])
            a_im[:, qs] = -_dot_nn(ds, cim_ref[q])
            dq = dyb[:, _quad_channels(q)]
            gcr_ref[q] += _dot_tn(dq, sre_ref[:, qs].astype(BF16))
            gci_ref[q] -= _dot_tn(dq, sim_ref[:, qs].astype(BF16))
        sums = _segment_scan(a_re, a_im, a_re, a_im, tab_ref, k, carry_re, carry_im, reverse,
                             s_refs=(sre_ref, sim_ref))
        for lt, (glr, gli) in enumerate(sums):
            sl = slice(lt * SCAN_W, (lt + 1) * SCAN_W)
            gv_ref[0:1, sl] += glr
            gv_ref[1:2, sl] += gli
        for j in range(D_SSM // SLAB):
            us = ub[:, j * SLAB:(j + 1) * SLAB]
            acc = jnp.zeros((TC, SLAB), F32)
            for q in range(4 * j, 4 * j + 4):
                qs = slice(q * QUAD, (q + 1) * QUAD)
                dbr = a_re[:, qs].astype(BF16)
                dbi = a_im[:, qs].astype(BF16)
                uq = ub[:, _quad_channels(q)]
                gbr_ref[q] += _dot_tn(uq, dbr)
                gbi_ref[q] += _dot_tn(uq, dbi)
                acc = acc + _dot_nn(dbr, bre_ref[q]) + _dot_nn(dbi, bim_ref[q])
            _store_tokens(du_ref, j * SLAB, acc, tmp_ref)

        @pl.when(pl.program_id(0) == nc - 1)
        def _():
            for g in range(N_SSM_GROUPS):
                q, gl = divmod(g, 4)
                rows = slice(gl * SSM_GROUP, (gl + 1) * SSM_GROUP)
                cols = slice(gl * SSM_STATE, (gl + 1) * SSM_STATE)
                for out, acc_ref in ((ob_re, gbr_ref), (ob_im, gbi_ref), (oc_re, gcr_ref), (oc_im, gci_ref)):
                    out[g] = acc_ref[q, rows, cols]

    gshape = jax.ShapeDtypeStruct((N_SSM_GROUPS, SSM_GROUP, SSM_STATE), F32)
    compact = pltpu.VMEM((N_QUAD, 4 * SSM_GROUP, QUAD), F32)
    return _hosted_call(
        body, comm, name="ssm_bwd_rev" if reverse else "ssm_bwd_fwd", grid=(nc,),
        in_specs=[pl.BlockSpec((TC, D_SSM), lambda i: (chunk(i), 0)),
                  pl.BlockSpec((TC, D_SSM), lambda i: (chunk(i), 1)), _full(order.shape),
                  pl.BlockSpec((TC, N_STATE), lambda i: (chunk(i), 0)),
                  pl.BlockSpec((TC, N_STATE), lambda i: (chunk(i), 0))]
        + [_full(b_re.shape)] * 4 + [_full(tables.shape)],
        out_specs=[pl.BlockSpec((TC, D_SSM), lambda i: (chunk(i), 0))] + [_full(gshape.shape)] * 4
        + [_full((2, N_STATE))],
        out_shape=[jax.ShapeDtypeStruct((L, D_SSM), F32), gshape, gshape, gshape, gshape,
                   jax.ShapeDtypeStruct((2, N_STATE), F32)],
        scratch_shapes=[pltpu.VMEM((TC, N_STATE), F32), pltpu.VMEM((TC, N_STATE), F32),
                        pltpu.VMEM((8, N_STATE), F32), pltpu.VMEM((8, N_STATE), F32),
                        compact, compact, compact, compact, pltpu.VMEM((SLAB // LANES * TC, LANES), F32)],
        args=(dy, u, order, s_re, s_im, b_re, b_im, c_re, c_im, tables))


def _ssm_post(yf, yb, u, d, glu_w, glu_b):
    y = yf + yb + d * u
    z, t = _gelu(y)
    zb = z.astype(BF16)
    gate = _sigmoid(_dot_nn(zb, glu_w) + glu_b)
    return y, z, t, zb, gate


def _mix_out(yn_pool, yf, yb, u, x, ssm_d, glu_w_b, glu_b, g_ssm, w_out_b, g_ffn):
    L, D = x.shape

    def body(ynp_ref, yf_ref, yb_ref, u_ref, x_ref, d_ref, gw_ref, gb_ref, gs_ref, wo_ref, gf_ref,
             h1_ref, hn_ref, ycat_ref):
        _, z, _, _, gate = _ssm_post(yf_ref[...], yb_ref[...], u_ref[...], d_ref[...], gw_ref[...], gb_ref[...])
        yns, _, _ = _rms_fwd(z * gate, gs_ref[...])
        ynsb = yns.astype(BF16)
        ynp = ynp_ref[...]
        ycat_ref[:, 0:D_POOL] = ynp
        ycat_ref[:, D_POOL:D] = ynsb
        h1 = x_ref[...] + _dot_nn(ynp, wo_ref[0:D_POOL, :]) + _dot_nn(ynsb, wo_ref[D_POOL:D, :])
        h1_ref[...] = h1
        hn, _, _ = _rms_fwd(h1, gf_ref[...])
        hn_ref[...] = hn.astype(BF16)

    half = lambda c: pl.BlockSpec((TL, D_SSM), lambda i: (i, c))
    row = pl.BlockSpec((TL, D), lambda i: (i, 0))
    return pl.pallas_call(
        body, name="mix_out", grid=(L // TL,),
        in_specs=[half(0), half(0), half(0), half(1), row, _full((1, D_SSM)), _full(glu_w_b.shape),
                  _full((1, D_SSM)), _full((1, D_SSM)), _full(w_out_b.shape), _full((1, D))],
        out_specs=[row, row, row],
        out_shape=[jax.ShapeDtypeStruct((L, D), F32), jax.ShapeDtypeStruct((L, D), BF16),
                   jax.ShapeDtypeStruct((L, D), BF16)],
        compiler_params=_cp("parallel"))(yn_pool, yf, yb, u, x, ssm_d, glu_w_b, glu_b, g_ssm, w_out_b, g_ffn)


def _ssm_bwd_local(dh1, yf, yb, u, ssm_d, glu_w_b, glu_b, g_ssm, w_out_b, comm=None):
    L, D = dh1.shape

    def body(dh_ref, yf_ref, yb_ref, u_ref, d_ref, gw_ref, gb_ref, gs_ref, wo_ref,
             dy_ref, du_ref, ggw_ref, ggb_ref, gd_ref, ggs_ref):
        @pl.when(pl.program_id(0) == 0)
        def _():
            for r in (ggw_ref, ggb_ref, gd_ref, ggs_ref):
                r[...] = jnp.zeros_like(r)

        u = u_ref[...]
        d = d_ref[...]
        y, z, t, zb, gate = _ssm_post(yf_ref[...], yb_ref[...], u, d, gw_ref[...], gb_ref[...])
        gs = gs_ref[...]
        _, xh, inv = _rms_fwd(z * gate, gs)
        d_yn = _dot_nt(dh_ref[...], wo_ref[...])
        d_o, dgs = _rms_bwd(d_yn, xh, inv, gs)
        ggs_ref[...] += dgs
        d_zg = d_o * z * gate * (1.0 - gate)
        d_zgb = d_zg.astype(BF16)
        ggb_ref[...] += jnp.sum(d_zg, axis=0, keepdims=True)
        ggw_ref[...] += _dot_tn(zb, d_zgb)
        d_z = d_o * gate + _dot_nt(d_zgb, gw_ref[...])
        d_y = d_z * _gelu_grad(y, t)
        gd_ref[...] += jnp.sum(d_y * u, axis=0, keepdims=True)
        dy_ref[...] = d_y
        du_ref[...] = d_y * d

    half = lambda c: pl.BlockSpec((TL, D_SSM), lambda i: (i, c))
    vec = _full((1, D_SSM))
    return _hosted_call(
        body, comm, name="ssm_bwd_local", grid=(L // TL,),
        in_specs=[pl.BlockSpec((TL, D), lambda i: (i, 0)), half(0), half(0), half(1), vec, _full(glu_w_b.shape),
                  vec, vec, pl.BlockSpec((D_SSM, D), lambda i: (1, 0))],
        out_specs=[half(0), half(0), _full(glu_w_b.shape), vec, vec, vec],
        out_shape=[jax.ShapeDtypeStruct((L, D_SSM), F32), jax.ShapeDtypeStruct((L, D_SSM), F32),
                   jax.ShapeDtypeStruct(glu_w_b.shape, F32)] + [jax.ShapeDtypeStruct((1, D_SSM), F32)] * 3,
        scratch_shapes=[], args=(dh1, yf, yb, u, ssm_d, glu_w_b, glu_b, g_ssm, w_out_b))


def _in_bwd(du_pool, du_a, du_b, du_c, dh1, x, g, w_in_b, comm=None):
    L, D = x.shape

    def body(p_ref, a_ref, b_ref, c_ref, dh_ref, x_ref, g_ref, w_ref, dx_ref, dub_ref, gg_ref):
        @pl.when(pl.program_id(0) == 0)
        def _():
            gg_ref[...] = jnp.zeros_like(gg_ref)

        dub_ref[:, 0:D_POOL] = p_ref[...].astype(BF16)
        dub_ref[:, D_POOL:D] = (a_ref[...] + b_ref[...] + c_ref[...]).astype(BF16)
        d_xn = _dot_nt(dub_ref[...], w_ref[...])
        gv = g_ref[...]
        _, xh, inv = _rms_fwd(x_ref[...], gv)
        dx, dg = _rms_bwd(d_xn, xh, inv, gv)
        gg_ref[...] += dg
        dx_ref[...] = dh_ref[...] + dx

    half = pl.BlockSpec((TL, D_SSM), lambda i: (i, 0))
    row = pl.BlockSpec((TL, D), lambda i: (i, 0))
    return _hosted_call(
        body, comm, name="in_bwd", grid=(L // TL,),
        in_specs=[half, half, half, half, row, row, _full((1, D)), _full(w_in_b.shape)],
        out_specs=[row, row, _full((1, D))],
        out_shape=[jax.ShapeDtypeStruct((L, D), F32), jax.ShapeDtypeStruct((L, D), BF16),
                   jax.ShapeDtypeStruct((1, D), F32)],
        scratch_shapes=[], args=(du_pool, du_a, du_b, du_c, dh1, x, g, w_in_b))


def _ffn_up(hn, w_up4):
    L, D = hn.shape

    def body(h_ref, w_ref, o_ref):
        o_ref[...] = _dot_nn(h_ref[...], w_ref[...]).astype(BF16)

    rows = min(TM, L)
    return pl.pallas_call(
        body, name="ffn_up", grid=(4, L // rows),
        in_specs=[pl.BlockSpec((rows, D), lambda j, i: (i, 0)), pl.BlockSpec((None, D, FF_BLK), lambda j, i: (j, 0, 0))],
        out_specs=pl.BlockSpec((rows, FF_BLK), lambda j, i: (i, j)),
        out_shape=jax.ShapeDtypeStruct((L, 4 * FF_BLK), BF16),
        compiler_params=_cp("parallel", "parallel"))(hn, w_up4)


def _halo_specs_2d(rows, width, L, col, order):
    rb = rows // HALO_B
    last = L // HALO_B - 1
    if order == "ik":
        wrap = lambda f: (lambda i, k: f(i, k))
    else:
        wrap = lambda f: (lambda k, i: f(i, k))
    return [pl.BlockSpec((HALO_B, width), wrap(lambda i, k: (jnp.maximum(i * rb - 1, 0), col(k)))),
            pl.BlockSpec((rows, width), wrap(lambda i, k: (i, col(k)))),
            pl.BlockSpec((HALO_B, width), wrap(lambda i, k: (jnp.minimum((i + 1) * rb, last), col(k))))]


def _shift_mats(rows):
    r = lax.broadcasted_iota(jnp.int32, (rows, rows), 0)
    c = lax.broadcasted_iota(jnp.int32, (rows, rows), 1)
    return (c == r - 1).astype(BF16), (c == r + 1).astype(BF16)


def _neighbours(x, prev_ref, next_ref, cs, i, n, mats):
    rows = x.shape[0]
    row = lax.broadcasted_iota(jnp.int32, (rows, 1), 0)
    before = jnp.where(i > 0, prev_ref[:, cs].astype(F32)[HALO_B - 1:HALO_B, :], 0.0)
    after = jnp.where(i < n - 1, next_ref[:, cs].astype(F32)[0:1, :], 0.0)
    if mats is None:
        xf = x.astype(F32)
        down, up = pltpu.roll(xf, 1, 0), pltpu.roll(xf, rows - 1, 0)
    else:
        down, up = _dot_nn(mats[0], x), _dot_nn(mats[1], x)
    return jnp.where(row == 0, before, down), jnp.where(row == rows - 1, after, up)


def _conv3(x, before, after, w, b):
    return before * w[0:1, :] + x.astype(F32) * w[1:2, :] + after * w[2:3, :] + b


def _col_chunks(width, size=256):
    return [slice(c, min(c + size, width)) for c in range(0, width, size)]


def _ffn_down_loss(up, conv_w, conv_b, w_down_b, h1, target, g_final):
    L, D = h1.shape
    n = L // TF
    nk = D_FF // FF_BLK

    def body(vp, vc, vn, gp, gc, gn, wv_ref, wg_ref, bv_ref, bg_ref, wd_ref, h1_ref, t_ref, gf_ref,
             a_ref, cv_ref, cg_ref, dh2_ref, dh2b_ref, loss_ref, gg_ref, acc_ref):
        i = pl.program_id(0)
        k = pl.program_id(1)

        @pl.when((i == 0) & (k == 0))
        def _():
            loss_ref[...] = jnp.zeros_like(loss_ref)
            gg_ref[...] = jnp.zeros_like(gg_ref)

        @pl.when(k == 0)
        def _():
            acc_ref[...] = jnp.zeros_like(acc_ref)

        mats = _shift_mats(TF)
        for cs in _col_chunks(FF_BLK):
            xv, xg = vc[:, cs], gc[:, cs]
            val = _conv3(xv, *_neighbours(xv, vp, vn, cs, i, n, mats), wv_ref[:, cs], bv_ref[:, cs])
            gate = _conv3(xg, *_neighbours(xg, gp, gn, cs, i, n, mats), wg_ref[:, cs], bg_ref[:, cs])
            a_ref[:, cs] = (val * (gate * _sigmoid(gate))).astype(BF16)
            cv_ref[:, cs] = val.astype(BF16)
            cg_ref[:, cs] = gate.astype(BF16)
        acc_ref[...] += _dot_nn(a_ref[...], wd_ref[pl.ds(pl.multiple_of(k * FF_BLK, LANES), FF_BLK), :])

        @pl.when(k == nk - 1)
        def _():
            gf = gf_ref[...]
            y, xh, inv = _rms_fwd(h1_ref[...] + acc_ref[...], gf)
            diff = y - t_ref[...]
            part = 0.5 * jnp.sum(jnp.mean(diff * diff, axis=-1, keepdims=True), axis=0, keepdims=True)
            loss_ref[...] += jnp.broadcast_to(part, loss_ref.shape)
            dx, dg = _rms_bwd(diff * (1.0 / D), xh, inv, gf)
            gg_ref[...] += dg
            dh2_ref[...] = dx
            dh2b_ref[...] = dx.astype(BF16)

    row = pl.BlockSpec((TF, D), lambda i, k: (i, 0))
    cw = lambda off: pl.BlockSpec((3, FF_BLK), lambda i, k: (0, k + off))
    cb = lambda off: pl.BlockSpec((1, FF_BLK), lambda i, k: (0, k + off))
    return pl.pallas_call(
        body, name="ffn_down_loss", grid=(n, nk),
        in_specs=_halo_specs_2d(TF, FF_BLK, L, lambda k: k, "ik") + _halo_specs_2d(TF, FF_BLK, L, lambda k: k + nk, "ik")
        + [cw(0), cw(nk), cb(0), cb(nk), _full(w_down_b.shape), row, row, _full((1, D))],
        out_specs=[pl.BlockSpec((TF, FF_BLK), lambda i, k: (i, k))] * 3 + [row, row, _full((1, LANES)), _full((1, D))],
        out_shape=[jax.ShapeDtypeStruct((L, D_FF), BF16)] * 3
        + [jax.ShapeDtypeStruct((L, D), F32), jax.ShapeDtypeStruct((L, D), BF16),
           jax.ShapeDtypeStruct((1, LANES), F32), jax.ShapeDtypeStruct((1, D), F32)],
        scratch_shapes=[pltpu.VMEM((TF, D), F32)],
        compiler_params=_cp("arbitrary", "arbitrary"))(
            up, up, up, up, up, up, conv_w, conv_w, conv_b, conv_b, w_down_b, h1, target, g_final)


def _ffn_act_bwd(c_val, c_gate, w_down_b, dh2):
    L, D = dh2.shape
    n = L // TL
    nk = D_FF // FF_BLK

    def body(v_ref, g_ref, wd_ref, dh_ref, dv_ref, dg_ref, gbv_ref, gbg_ref):
        @pl.when(pl.program_id(1) == 0)
        def _():
            gbv_ref[...] = jnp.zeros_like(gbv_ref)
            gbg_ref[...] = jnp.zeros_like(gbg_ref)

        dh = dh_ref[...]
        for cs in _col_chunks(FF_BLK):
            val, gate = v_ref[:, cs].astype(F32), g_ref[:, cs].astype(F32)
            d_a = _dot_nt(dh, wd_ref[cs, :])
            sg = _sigmoid(gate)
            d_val = d_a * (gate * sg)
            d_gate = d_a * val * (sg * (1.0 + gate * (1.0 - sg)))
            dv_ref[:, cs] = d_val.astype(BF16)
            dg_ref[:, cs] = d_gate.astype(BF16)
            gbv_ref[:, cs] += jnp.sum(d_val, axis=0, keepdims=True)
            gbg_ref[:, cs] += jnp.sum(d_gate, axis=0, keepdims=True)

    blk = pl.BlockSpec((TL, FF_BLK), lambda k, i: (i, k))
    acc = pl.BlockSpec((1, FF_BLK), lambda k, i: (0, k))
    return pl.pallas_call(
        body, name="ffn_act_bwd", grid=(nk, n),
        in_specs=[blk, blk, pl.BlockSpec((FF_BLK, D), lambda k, i: (k, 0)), pl.BlockSpec((TL, D), lambda k, i: (i, 0))],
        out_specs=[blk, blk, acc, acc],
        out_shape=[jax.ShapeDtypeStruct((L, D_FF), BF16), jax.ShapeDtypeStruct((L, D_FF), BF16),
                   jax.ShapeDtypeStruct((1, D_FF), F32), jax.ShapeDtypeStruct((1, D_FF), F32)],
        compiler_params=_cp("arbitrary", "arbitrary"))(c_val, c_gate, w_down_b, dh2)


def _ffn_up_bwd(d_val, d_gate, up, conv_w, w_up4, h1, dh2, g_ffn):
    L, D = h1.shape
    n = L // TF
    nk = D_FF // FF_BLK

    def body(vp, vc, vn, gp, gc, gn, uv_ref, ug_ref, wv_ref, wg_ref, wu_ref, h1_ref, dh2_ref, g_ref,
             dup_ref, dh1_ref, dh1b_ref, gg_ref, gcw_ref, acc_ref):
        i = pl.program_id(0)
        k = pl.program_id(1)

        @pl.when((i == 0) & (k == 0))
        def _():
            gg_ref[...] = jnp.zeros_like(gg_ref)
            gcw_ref[...] = jnp.zeros_like(gcw_ref)

        @pl.when(k == 0)
        def _():
            acc_ref[...] = jnp.zeros_like(acc_ref)

        acc = jnp.zeros((TF, D), F32)
        for j, (blocks, u_ref, w_ref) in enumerate((((vp, vc, vn), uv_ref, wv_ref), ((gp, gc, gn), ug_ref, wg_ref))):
            for cs in _col_chunks(FF_BLK):
                d = blocks[1][:, cs]
                before, after = _neighbours(d, blocks[0], blocks[2], cs, i, n, None)
                taps = (after, d.astype(F32), before)
                w = w_ref[:, cs]
                d_up = (taps[0] * w[0:1, :] + taps[1] * w[1:2, :] + taps[2] * w[2:3, :]).astype(BF16)
                dup_ref[j, :, cs] = d_up
                acc = acc + _dot_nt(d_up, wu_ref[k + j * nk, :, cs])
                x = u_ref[:, cs].astype(F32)
                for r in range(3):
                    gcw_ref[j, k, r:r + 1, cs] += jnp.sum(taps[r] * x, axis=0, keepdims=True)
        acc_ref[...] += acc

        @pl.when(k == nk - 1)
        def _():
            g = g_ref[...]
            _, xh, inv = _rms_fwd(h1_ref[...], g)
            dx, dg = _rms_bwd(acc_ref[...], xh, inv, g)
            gg_ref[...] += dg
            dh1 = dh2_ref[...] + dx
            dh1_ref[...] = dh1
            dh1b_ref[...] = dh1.astype(BF16)

    row = pl.BlockSpec((TF, D), lambda i, k: (i, 0))
    cw = lambda off: pl.BlockSpec((3, FF_BLK), lambda i, k: (0, k + off))
    tile = lambda off: pl.BlockSpec((TF, FF_BLK), lambda i, k: (i, k + off))
    return pl.pallas_call(
        body, name="ffn_up_bwd", grid=(n, nk),
        in_specs=_halo_specs_2d(TF, FF_BLK, L, lambda k: k, "ik") + _halo_specs_2d(TF, FF_BLK, L, lambda k: k, "ik")
        + [tile(0), tile(nk), cw(0), cw(nk), _full(w_up4.shape), row, row, _full((1, D))],
        out_specs=[pl.BlockSpec((2, None, TF, FF_BLK), lambda i, k: (0, k, i, 0)), row, row, _full((1, D)),
                   _full((2, nk, 3, FF_BLK))],
        out_shape=[jax.ShapeDtypeStruct((2, nk, L, FF_BLK), BF16), jax.ShapeDtypeStruct((L, D), F32),
                   jax.ShapeDtypeStruct((L, D), BF16), jax.ShapeDtypeStruct((1, D), F32),
                   jax.ShapeDtypeStruct((2, nk, 3, FF_BLK), F32)],
        scratch_shapes=[pltpu.VMEM((TF, D), F32)],
        compiler_params=_cp("arbitrary", "arbitrary"))(
            d_val, d_val, d_val, d_gate, d_gate, d_gate, up, up, conv_w, conv_w, w_up4, h1, dh2, g_ffn)


def _matmul_tn(a, b, tm, tn, name, tk=2048):
    L, M = a.shape
    N = b.shape[1]
    tk = min(tk, L)

    def body(a_ref, b_ref, o_ref):
        @pl.when(pl.program_id(2) == 0)
        def _():
            o_ref[...] = jnp.zeros_like(o_ref)

        o_ref[...] += _dot_tn(a_ref[...], b_ref[...])

    return pl.pallas_call(
        body, name=name, grid=(M // tm, N // tn, L // tk),
        in_specs=[pl.BlockSpec((tk, tm), lambda m, n, l: (l, m)), pl.BlockSpec((tk, tn), lambda m, n, l: (l, n))],
        out_specs=pl.BlockSpec((tm, tn), lambda m, n, l: (m, n)),
        out_shape=jax.ShapeDtypeStruct((M, N), F32),
        compiler_params=_cp("parallel", "parallel", "arbitrary"))(a, b)


def _matmul_tn_blocks(a, b, tm, name, tk=2048):
    L, M = a.shape
    J, _, N = b.shape
    tk = min(tk, L)

    def body(a_ref, b_ref, o_ref):
        @pl.when(pl.program_id(2) == 0)
        def _():
            o_ref[...] = jnp.zeros_like(o_ref)

        o_ref[...] += _dot_tn(a_ref[...], b_ref[...])

    return pl.pallas_call(
        body, name=name, grid=(M // tm, J, L // tk),
        in_specs=[pl.BlockSpec((tk, tm), lambda m, j, l: (l, m)), pl.BlockSpec((None, tk, N), lambda m, j, l: (j, l, 0))],
        out_specs=pl.BlockSpec((None, tm, N), lambda m, j, l: (j, m, 0)),
        out_shape=jax.ShapeDtypeStruct((J, M, N), F32),
        compiler_params=_cp("parallel", "parallel", "arbitrary"))(a, b)


def _row_tile(rows):
    for t in (512, 352, 256, 128, 64, 8):
        if rows % t == 0:
            return t
    return rows


def _add_half(g, r, c_arr, name, out_dtype=F32):
    _, _, R, C = g.shape
    tr = _row_tile(R)

    def body(c_ref, g_ref, r_ref, o_ref):
        o_ref[...] = (g_ref[...] + r_ref[...]).astype(out_dtype)

    return pl.pallas_call(
        body, name=name,
        grid_spec=pltpu.PrefetchScalarGridSpec(
            num_scalar_prefetch=1, grid=(g.shape[0], R // tr),
            in_specs=[pl.BlockSpec((None, None, tr, C), lambda j, i, c: (j, c[0], i, 0)),
                      pl.BlockSpec((None, tr, C), lambda j, i, c: (j, i, 0))],
            out_specs=pl.BlockSpec((None, tr, C), lambda j, i, c: (j, i, 0))),
        out_shape=jax.ShapeDtypeStruct(r.shape, out_dtype),
        compiler_params=_cp("parallel", "parallel"))(c_arr, g, r)


def _sum4(p, name):
    _, R, C = p.shape
    tr = _row_tile(R)

    def body(p_ref, o_ref):
        q = [p_ref[j].astype(F32) for j in range(4)]
        o_ref[...] = ((q[0] + q[1]) + q[2]) + q[3]

    return pl.pallas_call(
        body, name=name, grid=(R // tr,),
        in_specs=[pl.BlockSpec((4, tr, C), lambda i: (0, i, 0))],
        out_specs=pl.BlockSpec((tr, C), lambda i: (i, 0)),
        out_shape=jax.ShapeDtypeStruct((R, C), F32), compiler_params=_cp("parallel"))(p)


def _adamw_refs(w_ref, g_ref, m_ref, v_ref, d_ref, nm_ref, nv_ref):
    gv = g_ref[...]
    nm = ADAM_B1 * m_ref[...] + (1.0 - ADAM_B1) * gv
    nv = ADAM_B2 * v_ref[...] + (1.0 - ADAM_B2) * (gv * gv)
    m_hat = nm / (1.0 - ADAM_B1 ** ADAM_STEP)
    v_hat = nv / (1.0 - ADAM_B2 ** ADAM_STEP)
    d_ref[...] = -ADAM_LR * (m_hat / (jnp.sqrt(v_hat) + ADAM_EPS) + ADAM_WD * w_ref[...])
    nm_ref[...] = nm
    nv_ref[...] = nv


def _adamw_many(ws, gs, ms, vs, name):
    n = len(ws)

    def body(*refs):
        for k in range(n):
            _adamw_refs(*(refs[j * n + k] for j in range(7)))

    out_shape = [jax.ShapeDtypeStruct(w.shape, F32) for w in ws] * 3
    res = pl.pallas_call(body, name=name, out_shape=out_shape,
                         compiler_params=pltpu.CompilerParams(vmem_limit_bytes=VMEM_LIMIT))(*ws, *gs, *ms, *vs)
    return res[:n], res[n:2 * n], res[2 * n:]


def _join_rows(own, other, c_arr, name):
    R, C = own.shape
    tr = _row_tile(R)

    def body(c_ref, own_ref, other_ref, o_ref):
        o_ref[...] = jnp.where(pl.program_id(0) == c_ref[0], own_ref[...], other_ref[...])

    half = pl.BlockSpec((tr, C), lambda h, i, c: (i, 0))
    return pl.pallas_call(
        body, name=name,
        grid_spec=pltpu.PrefetchScalarGridSpec(
            num_scalar_prefetch=1, grid=(2, R // tr), in_specs=[half, half],
            out_specs=pl.BlockSpec((tr, C), lambda h, i, c: (h * (R // tr) + i, 0))),
        out_shape=jax.ShapeDtypeStruct((2 * R, C), F32),
        compiler_params=_cp("parallel", "parallel"))(c_arr, own, other)


def _adamw_halves(w, own, other, m, v, name, comm=None):
    R, C = own.shape
    tr = _row_tile(R)
    while tr * C * 4 > ADAMW_BLOCK_BYTES and tr % 16 == 0:
        tr //= 2

    def body(w_ref, own_ref, other_ref, m_ref, v_ref, g_ref, d_ref, nm_ref, nv_ref):
        g_ref[...] = jnp.where(pl.program_id(0) == lax.axis_index("c"), own_ref[...], other_ref[...])
        _adamw_refs(w_ref, g_ref, m_ref, v_ref, d_ref, nm_ref, nv_ref)

    half = pl.BlockSpec((tr, C), lambda h, i: (i, 0))
    full = pl.BlockSpec((tr, C), lambda h, i: (h * (R // tr) + i, 0))
    sh = jax.ShapeDtypeStruct((2 * R, C), F32)
    return _hosted_call(body, comm, name=name, grid=(2, R // tr), in_specs=[full, half, half, full, full],
                        out_specs=[full] * 4, out_shape=[sh] * 4, scratch_shapes=[], args=(w, own, other, m, v))


_ANY = pl.BlockSpec(memory_space=pl.ANY)


def _position():
    return lax.axis_index("x"), lax.axis_index("y"), lax.axis_index("c")


class _Comm:
    def __init__(self, arrs, out_shape, sems, start, finish):
        self.arrs, self.out_shape, self.sems, self.start, self.finish = arrs, out_shape, sems, start, finish


def _comm_call(comm, name):
    n, m = len(comm.arrs), len(comm.out_shape)

    def body(*refs):
        ins, outs, sems = refs[:n], refs[n:n + m], refs[n + m:]
        comm.start(ins, outs, sems)
        comm.finish(ins, outs, sems)

    return pl.pallas_call(
        body, name=name, in_specs=[_ANY] * n, out_specs=[_ANY] * m, out_shape=comm.out_shape,
        scratch_shapes=comm.sems, compiler_params=pltpu.CompilerParams(has_side_effects=True))(*comm.arrs)


def _hosted_call(body, comm, *, name, grid, in_specs, out_specs, out_shape, scratch_shapes, args):
    sem = ("arbitrary",) * len(grid)
    if comm is None:
        return pl.pallas_call(body, name=name, grid=grid, in_specs=in_specs, out_specs=out_specs, out_shape=out_shape,
                              scratch_shapes=scratch_shapes, compiler_params=_cp(*sem))(*args), []
    n_in, n_out, n_scr = len(in_specs), len(out_specs), len(scratch_shapes)
    ci, co = len(comm.arrs), len(comm.out_shape)

    def full(*refs):
        ins, refs = refs[:n_in], refs[n_in:]
        cins, refs = refs[:ci], refs[ci:]
        outs, refs = refs[:n_out], refs[n_out:]
        couts, refs = refs[:co], refs[co:]
        scr, csems = refs[:n_scr], refs[n_scr:]
        first, last = True, True
        for d, size in enumerate(grid):
            first = first & (pl.program_id(d) == 0)
            last = last & (pl.program_id(d) == size - 1)

        @pl.when(first)
        def _():
            comm.start(cins, couts, csems)

        body(*ins, *outs, *scr)

        @pl.when(last)
        def _():
            comm.finish(cins, couts, csems)

    res = pl.pallas_call(
        full, name=name, grid=grid, in_specs=list(in_specs) + [_ANY] * ci, out_specs=list(out_specs) + [_ANY] * co,
        out_shape=list(out_shape) + list(comm.out_shape), scratch_shapes=list(scratch_shapes) + list(comm.sems),
        compiler_params=_cp(*sem))(*args, *comm.arrs)
    return res[:n_out], res[n_out:]


def _comm_join(*comms):
    def parts(xs, attr):
        out, at = [], 0
        for cm in comms:
            n = len(getattr(cm, attr))
            out.append(xs[at:at + n])
            at += n
        return out

    def start(ins, outs, sems):
        for cm, i, o, s in zip(comms, parts(ins, "arrs"), parts(outs, "out_shape"), parts(sems, "sems")):
            cm.start(i, o, s)

    def finish(ins, outs, sems):
        for cm, i, o, s in zip(comms, parts(ins, "arrs"), parts(outs, "out_shape"), parts(sems, "sems")):
            cm.finish(i, o, s)

    cat = lambda attr: [x for cm in comms for x in getattr(cm, attr)]
    return _Comm(cat("arrs"), cat("out_shape"), cat("sems"), start, finish)


def _dma_sems(*counts):
    return [pltpu.SemaphoreType.DMA((n,)) for n in counts]


def _comm_pair_swap(arrs, half=False):
    n = len(arrs)
    out_shape = [jax.ShapeDtypeStruct(a.shape[:1] + a.shape[2:] if half else a.shape, a.dtype) for a in arrs]

    def copies(ins, outs, sems):
        x, y, c = _position()
        return [pltpu.make_async_remote_copy(
            src_ref=ins[k].at[:, 1 - c] if half else ins[k], dst_ref=outs[k], send_sem=sems[0].at[k],
            recv_sem=sems[1].at[k], device_id=(x, y, 1 - c), device_id_type=MESH) for k in range(n)]

    def start(ins, outs, sems):
        for cp in copies(ins, outs, sems):
            cp.start()

    def finish(ins, outs, sems):
        for cp in copies(ins, outs, sems):
            cp.wait()

    return _Comm(arrs, out_shape, _dma_sems(n, n), start, finish)


def _chip_of(j, c):
    return (jnp.right_shift(j, 1), jnp.bitwise_and(j, 1), c)


def _comm_chip_exchange(arrs, scatter):
    n = len(arrs)
    out_shape = [jax.ShapeDtypeStruct(a.shape if scatter else (4,) + a.shape, a.dtype) for a in arrs]

    def copies(ins, outs, sems):
        x, y, c = _position()
        me = 2 * x + y
        local, sent, landed = [], [], []
        for k in range(n):
            local.append(pltpu.make_async_copy(ins[k].at[me] if scatter else ins[k], outs[k].at[me], sems[2].at[k]))
            for d in (1, 2, 3):
                j = jnp.bitwise_xor(me, d)
                s = 3 * k + d - 1
                src = ins[k].at[j] if scatter else ins[k]
                for dst, group in ((outs[k].at[me], sent), (outs[k].at[j], landed)):
                    group.append(pltpu.make_async_remote_copy(
                        src_ref=src, dst_ref=dst, send_sem=sems[0].at[s], recv_sem=sems[1].at[s],
                        device_id=_chip_of(j, c), device_id_type=MESH))
        return local, sent, landed

    def start(ins, outs, sems):
        local, sent, _ = copies(ins, outs, sems)
        for cp in local + sent:
            cp.start()

    def finish(ins, outs, sems):
        local, sent, landed = copies(ins, outs, sems)
        for cp in sent:
            cp.wait_send()
        for cp in landed:
            cp.wait_recv()
        for cp in local:
            cp.wait()

    return _Comm(arrs, out_shape, _dma_sems(3 * n, 3 * n, n), start, finish)


LOCAL_PARTS = 4


def _comm_gather_split(shards, whole):
    n, nw = len(shards), len(whole)
    arrs = list(shards) + list(whole)
    out_shape = [jax.ShapeDtypeStruct((4,) + a.shape, a.dtype) for a in arrs]

    def copies(ins, outs, sems):
        x, y, c = _position()
        me = 2 * x + y
        local, sent, landed, passed, passed_in = [], [], [], [], []
        for k in range(n + nw):
            if k >= n:
                local.append(pltpu.make_async_copy(ins[k], outs[k].at[me], sems[4].at[LOCAL_PARTS * k]))
            else:
                part = shards[k].shape[0] // LOCAL_PARTS
                for r in range(LOCAL_PARTS):
                    local.append(pltpu.make_async_copy(ins[k].at[pl.ds(r * part, part)],
                                                       outs[k].at[me, pl.ds(r * part, part)],
                                                       sems[4].at[LOCAL_PARTS * k + r]))
            for d in (1, 2, 3):
                j = jnp.bitwise_xor(me, d)
                s = 3 * k + d - 1
                if k >= n:
                    src, mine, theirs = ins[k], outs[k].at[me], outs[k].at[j]
                else:
                    h = shards[k].shape[0] // 2
                    rows = pl.ds(pl.multiple_of(c * h, 16), h)
                    other = pl.ds(pl.multiple_of((1 - c) * h, 16), h)
                    src, mine, theirs = ins[k].at[rows], outs[k].at[me, rows], outs[k].at[j, rows]
                    for dst, group in ((theirs, passed), (outs[k].at[j, other], passed_in)):
                        group.append(pltpu.make_async_remote_copy(
                            src_ref=theirs, dst_ref=dst, send_sem=sems[2].at[s], recv_sem=sems[3].at[s],
                            device_id=(x, y, 1 - c), device_id_type=MESH))
                for dst, group in ((mine, sent), (theirs, landed)):
                    group.append(pltpu.make_async_remote_copy(
                        src_ref=src, dst_ref=dst, send_sem=sems[0].at[s], recv_sem=sems[1].at[s],
                        device_id=_chip_of(j, c), device_id_type=MESH))
        return local, sent, landed, passed, passed_in

    def start(ins, outs, sems):
        local, sent, _, _, _ = copies(ins, outs, sems)
        for cp in local + sent:
            cp.start()

    def finish(ins, outs, sems):
        local, sent, landed, passed, passed_in = copies(ins, outs, sems)
        for cp in landed[:3 * n]:
            cp.wait_recv()
        for cp in passed:
            cp.start()
        for cp in landed[3 * n:]:
            cp.wait_recv()
        for cp in sent:
            cp.wait_send()
        for cp in passed:
            cp.wait_send()
        for cp in passed_in:
            cp.wait_recv()
        for cp in local:
            cp.wait()

    t = 3 * (n + nw)
    return _Comm(arrs, out_shape, _dma_sems(t, t, max(3 * n, 1), max(3 * n, 1), LOCAL_PARTS * (n + nw)), start, finish)


def _pack(arrs, row_multiple):
    parts = []
    for a in arrs:
        flat = a.reshape(-1).astype(F32)
        pad = (-flat.shape[0]) % LANES
        parts.append(jnp.pad(flat, (0, pad)) if pad else flat)
    flat = jnp.concatenate(parts)
    rows = -(-flat.shape[0] // LANES)
    rows_p = -(-rows // row_multiple) * row_multiple
    return jnp.pad(flat, (0, rows_p * LANES - flat.shape[0])).reshape(rows_p, LANES)


def _unpack(packed, shapes):
    flat = packed.reshape(-1)
    outs, off = [], 0
    for sh in shapes:
        size = int(np.prod(sh))
        outs.append(flat[off:off + size].reshape(sh))
        off += size + (-size) % LANES
    return outs


SMALL = ["norm_mix_g", "pool_w", "pool_scale", "ssm_log_neg_a_re", "ssm_a_im", "ssm_log_dt", "ssm_b_re", "ssm_b_im",
         "ssm_c_re", "ssm_c_im", "ssm_d", "glu_b", "out_norm_pool_g", "out_norm_ssm_g", "norm_ffn_g", "conv_b",
         "final_norm_g"]
BIG = ["w_in", "glu_w", "w_out", "w_up", "w_down"]
WIDE = ["pool_w", "ssm_b_re", "ssm_b_im", "ssm_c_re", "ssm_c_im"]
WEIGHTS = ['norm_mix_g', 'w_in', 'pool_w', 'pool_scale', 'ssm_log_neg_a_re', 'ssm_a_im', 'ssm_log_dt', 'ssm_b_re',
           'ssm_b_im', 'ssm_c_re', 'ssm_c_im', 'ssm_d', 'glu_w', 'glu_b', 'out_norm_pool_g', 'out_norm_ssm_g', 'w_out',
           'norm_ffn_g', 'w_up', 'conv_w', 'conv_b', 'w_down', 'final_norm_g']


def _local_step(x, target, p, full, shards=None, c_arr=None):
    L, D = x.shape
    dist = shards is not None
    row = lambda a: a.reshape(1, -1)
    w_in = full["w_in"]
    pool_w_b = p["pool_w"].astype(BF16)
    g_mix, g_pool, g_ssm, g_ffn, g_fin = (row(p[k]) for k in (
        "norm_mix_g", "out_norm_pool_g", "out_norm_ssm_g", "norm_ffn_g", "final_norm_g"))
    pool_scale, ssm_d, glu_b, conv_b = (row(p[k]) for k in ("pool_scale", "ssm_d", "glu_b", "conv_b"))

    lnar = p["ssm_log_neg_a_re"].reshape(2 * N_SSM_GROUPS, SSM_STATE)
    aim = p["ssm_a_im"].reshape(2 * N_SSM_GROUPS, SSM_STATE)
    ldt = jnp.broadcast_to(p["ssm_log_dt"].reshape(2 * N_SSM_GROUPS, 1), lnar.shape)
    lam_re, lam_im, f_re, f_im = _ssm_params(lnar, aim, ldt)
    flat2 = lambda a: a.reshape(2, N_STATE)
    lam4 = jnp.stack([flat2(lam_re)[0], flat2(lam_im)[0], flat2(lam_re)[1], flat2(lam_im)[1]])
    tables = _scan_tables(lam4)
    per_group = (2, N_SSM_GROUPS, SSM_STATE)
    dense = _ssm_expand(p["ssm_b_re"], p["ssm_b_im"], p["ssm_c_re"], p["ssm_c_im"],
                        f_re.reshape(per_group + (1,)), f_im.reshape(per_group + (1,)))
    ssm_args = [tuple(dense[4 * d:4 * d + 4]) + (tables,) for d in range(2)]

    u, xn = _in_proj(x, g_mix, w_in)
    yn_pool = _pool_fwd(u, pool_w_b, pool_scale, g_pool)
    gather1 = _comm_gather_split([shards[k] for k in ("glu_w", "w_out", "w_down")], [shards["conv_w"]]) if dist else None
    (y0, s0r, s0i), got1 = _ssm_scan_fwd(u, *ssm_args[0], 0, False, comm=gather1)
    gather2 = _comm_gather_split([shards["w_up"]], []) if dist else None
    (y1, s1r, s1i), got2 = _ssm_scan_fwd(u, *ssm_args[1], 2, True, comm=gather2)
    if dist:
        glu_w, w_out, w_down = (g.reshape((-1,) + g.shape[2:]) for g in got1[:3])
        conv_w = jnp.transpose(got1[3], (1, 0, 2)).reshape(3, -1)
        w_up4 = got2[0]
    else:
        glu_w, w_out, w_up4, w_down, conv_w = (full[k] for k in ("glu_w", "w_out", "w_up", "w_down", "conv_w"))
    h1, hn, ycat = _mix_out(yn_pool, y0, y1, u, x, ssm_d, glu_w, glu_b, g_ssm, w_out, g_ffn)
    up = _ffn_up(hn, w_up4)
    a, c_val, c_gate, dh2, dh2_b, loss, g_final = _ffn_down_loss(up, conv_w, conv_b, w_down, h1, target, g_fin)

    d_val, d_gate, gbv, gbg = _ffn_act_bwd(c_val, c_gate, w_down, dh2_b)
    g_w_down = _matmul_tn(a, dh2_b, FF_BLK, D, "grad_w_down")
    d_up, dh1, dh1_b, g_ffn_g, gcw = _ffn_up_bwd(d_val, d_gate, up, conv_w, w_up4, h1, dh2, g_ffn)
    g_w_up = _matmul_tn_blocks(hn, d_up.reshape(4, L, FF_BLK), TM, "grad_w_up")
    g_w_out = _matmul_tn(ycat, dh1_b, TM, D, "grad_w_out")
    late = ("w_up", "w_down", "w_out", "glu_w")
    halves = [g_w_up.reshape(4, 2, D // 2, FF_BLK), g_w_down.reshape(4, 2, D_FF // 8, D)]
    (dy, du_direct, g_glu_w, g_glu_b, g_ssm_d, g_ssm_g), swapped = _ssm_bwd_local(
        dh1_b, y0, y1, u, ssm_d, glu_w, glu_b, g_ssm, w_out, comm=_comm_pair_swap(halves, half=True) if dist else None)
    more = [g_w_out.reshape(4, 2, D // 8, D), g_glu_w.reshape(4, 2, D_SSM // 8, D_SSM)]
    (d_pooled, g_pool_w, g_pool_scale, g_pool_g), swapped_more = _pool_bwd_local(
        dh1_b, u, w_out, pool_w_b, pool_scale, g_pool, comm=_comm_pair_swap(more, half=True) if dist else None)
    halves, from_sibling = halves + more, list(swapped) + list(swapped_more)
    du_pool = _pool_bwd_window(d_pooled)
    reduce_a, reduce_b = None, None
    if dist:
        chip_sums = [_add_half(h, r, c_arr, "sum_pair_" + k, BF16) for k, h, r in zip(late, halves, from_sibling)]
        reduce_a = _comm_chip_exchange(chip_sums[:1], scatter=True)
        reduce_b = _comm_chip_exchange(chip_sums[1:], scatter=True)
    (du0, gb0r, gb0i, gc0r, gc0i, gv0), chips_a = _ssm_scan_bwd(dy, u, s0r, s0i, *ssm_args[0], 1, True, comm=reduce_a)
    (du1, gb1r, gb1i, gc1r, gc1i, gv1), chips_b = _ssm_scan_bwd(dy, u, s1r, s1i, *ssm_args[1], 3, False, comm=reduce_b)
    mine = [_sum4(r, "sum_chips_" + k) for k, r in zip(late, list(chips_a) + list(chips_b))]
    by_state = (2, N_SSM_GROUPS, 1, SSM_STATE)
    g_b_re, g_b_im, g_f_re, g_f_im = _ssm_unfold(
        jnp.stack([gb0r, gb1r]), jnp.stack([gb0i, gb1i]),
        jnp.swapaxes(p["ssm_b_re"], 2, 3), jnp.swapaxes(p["ssm_b_im"], 2, 3),
        f_re.reshape(by_state), f_im.reshape(by_state))
    gvec = lambda j: jnp.stack([gv0[j], gv1[j]]).reshape(2 * N_SSM_GROUPS, SSM_STATE)
    g_lnar, g_aim, g_ldt = _ssm_params_bwd(lnar, aim, ldt, gvec(0), gvec(1),
                                           g_f_re.reshape(lnar.shape), g_f_im.reshape(lnar.shape))
    (grad_x, d_u_b, g_mix_g), theirs = _in_bwd(du_pool, du_direct, du0, du1, dh1, x, g_mix, w_in,
                                               comm=_comm_pair_swap(mine) if dist else None)
    g_w_in = _matmul_tn(xn, d_u_b, TM, D, "grad_w_in")

    small = {
        "norm_mix_g": g_mix_g, "pool_w": g_pool_w, "pool_scale": g_pool_scale,
        "ssm_log_neg_a_re": g_lnar, "ssm_a_im": g_aim, "ssm_log_dt": g_ldt,
        "ssm_b_re": jnp.swapaxes(g_b_re, 2, 3), "ssm_b_im": jnp.swapaxes(g_b_im, 2, 3),
        "ssm_c_re": jnp.stack([gc0r, gc1r]), "ssm_c_im": jnp.stack([gc0i, gc1i]),
        "ssm_d": g_ssm_d, "glu_b": g_glu_b, "out_norm_pool_g": g_pool_g, "out_norm_ssm_g": g_ssm_g,
        "norm_ffn_g": g_ffn_g, "conv_b": jnp.concatenate([gbv[0], gbg[0]]), "final_norm_g": g_final,
        "conv_w": jnp.transpose(gcw, (2, 0, 1, 3)).reshape(3, -1),
    }
    big = {"w_in": g_w_in}
    reduced = dict(zip(late, zip(mine, theirs)))
    if not dist:
        big.update({"w_up": g_w_up, "w_down": g_w_down, "w_out": g_w_out, "glu_w": g_glu_w})
    return loss, grad_x, small, big, reduced


def kernel(x, norm_mix_g, w_in, pool_w, pool_scale, ssm_log_neg_a_re, ssm_a_im, ssm_log_dt, ssm_b_re, ssm_b_im, ssm_c_re, ssm_c_im, ssm_d, glu_w, glu_b, out_norm_pool_g, out_norm_ssm_g, w_out, norm_ffn_g, w_up, conv_w, conv_b, w_down, final_norm_g, loss_target, m_norm_mix_g, m_w_in, m_pool_w, m_pool_scale, m_ssm_log_neg_a_re, m_ssm_a_im, m_ssm_log_dt, m_ssm_b_re, m_ssm_b_im, m_ssm_c_re, m_ssm_c_im, m_ssm_d, m_glu_w, m_glu_b, m_out_norm_pool_g, m_out_norm_ssm_g, m_w_out, m_norm_ffn_g, m_w_up, m_conv_w, m_conv_b, m_w_down, m_final_norm_g, v_norm_mix_g, v_w_in, v_pool_w, v_pool_scale, v_ssm_log_neg_a_re, v_ssm_a_im, v_ssm_log_dt, v_ssm_b_re, v_ssm_b_im, v_ssm_c_re, v_ssm_c_im, v_ssm_d, v_glu_w, v_glu_b, v_out_norm_pool_g, v_out_norm_ssm_g, v_w_out, v_norm_ffn_g, v_w_up, v_conv_w, v_conv_b, v_w_down, v_final_norm_g):
    args = locals()
    w = {k: args[k] for k in WEIGHTS}
    m = {k: args["m_" + k] for k in WEIGHTS}
    v = {k: args["v_" + k] for k in WEIGHTS}
    chip = 2 * lax.axis_index("x") + lax.axis_index("y")
    c_arr = lax.axis_index("c").astype(jnp.int32).reshape(1)

    shards = {k: w[k].astype(BF16) for k in BIG}
    shards["conv_w"] = conv_w
    w_in_full = _comm_call(_comm_gather_split([shards["w_in"]], []), "gather_w_in")[0]
    loss, grad_x, g_small, g_big, reduced = _local_step(
        x[0], loss_target[0], w, {"w_in": w_in_full.reshape(-1, w_in_full.shape[-1])}, shards, c_arr)

    exact = [k for k in SMALL if k not in WIDE]
    packs = [_pack([loss] + [g_small[k] for k in exact] + [g_small["conv_w"]], 512),
             _pack([g_small[k] for k in WIDE], 512)]
    halves = [g_big["w_in"].reshape(4, 2, g_big["w_in"].shape[0] // 8, -1)]
    halves += [pk.reshape(1, 2, pk.shape[0] // 2, LANES) for pk in packs]
    from_sibling = _comm_call(_comm_pair_swap(halves, half=True), "reduce_pair")
    names = ("w_in", "exact", "wide")
    sums = [_add_half(h, r, c_arr, "sum_pair_" + k, dt)
            for k, h, r, dt in zip(names, halves, from_sibling, (BF16, F32, BF16))]
    grads, delta, new_m, new_v = {}, {}, {}, {}

    def adamw_behind(k, comm):
        own, other = reduced[k]
        (grads[k], delta[k], new_m[k], new_v[k]), got = _adamw_halves(
            w[k], own, other, m[k], v[k], "adamw_" + k, comm=comm)
        return got

    from_chips = adamw_behind("w_up", _comm_join(_comm_chip_exchange(sums[:1], scatter=True),
                                                  _comm_chip_exchange([s[0] for s in sums[1:]], scatter=False)))
    mine = [_sum4(r, "sum_chips_" + k) for k, r in zip(names, from_chips)]
    theirs = adamw_behind("w_down", _comm_pair_swap(mine))
    for k in ("w_out", "glu_w"):
        adamw_behind(k, None)
    exact_all = _join_rows(mine[1], theirs[1], c_arr, "join_exact")
    wide_all = _join_rows(mine[2], theirs[2], c_arr, "join_wide")
    shapes = [loss.shape] + [w[k].shape for k in exact] + [(3, 4 * FF_BLK)]
    grads.update(zip(["loss"] + exact + ["conv_w_full"], _unpack(exact_all, shapes)))
    grads.update(zip(WIDE, _unpack(wide_all, [w[k].shape for k in WIDE])))
    loss = grads.pop("loss")[0, 0]
    grads["conv_w"] = lax.dynamic_slice_in_dim(grads.pop("conv_w_full"), chip * FF_BLK, FF_BLK, axis=1)

    reduced["w_in"] = (mine[0], theirs[0])
    adamw_behind("w_in", None)
    padded = ["ssm_b_re", "ssm_b_im"]
    for keys, name in ((padded, "adamw_ssm_b"), ([k for k in SMALL + ["conv_w"] if k not in padded], "adamw_small")):
        outs = _adamw_many(*([d[k] for k in keys] for d in (w, grads, m, v)), name)
        for d, o in zip((delta, new_m, new_v), outs):
            d.update(zip(keys, o))

    return (loss, grad_x[None], *[grads[k] for k in WEIGHTS], *[delta[k] for k in WEIGHTS],
            *[new_m[k] for k in WEIGHTS], *[new_v[k] for k in WEIGHTS])
```

```python
import numpy as np
import jax
import jax.numpy as jnp
from jax import lax
from jax.experimental import pallas as pl
from jax.experimental.pallas import tpu as pltpu

F32 = jnp.float32
BF16 = jnp.bfloat16
MESH = pl.DeviceIdType.MESH

EPS = 1e-6
POOL_WINDOWS = (2, 4, 8, 16)
POOL_GROUP = 128
SSM_GROUP = 16
SSM_STATE = 64
N_SSM_GROUPS = 32
N_STATE = N_SSM_GROUPS * SSM_STATE
QUAD = 256
N_QUAD = N_STATE // QUAD
SLAB = 256
D_SSM = 512
D_POOL = 512
D_FF = 2816
FF_BLK = 1408
HALO = 8
HALO_B = 16
LANES = 128
ADAM_LR, ADAM_B1, ADAM_B2, ADAM_EPS, ADAM_WD, ADAM_STEP = 0.001, 0.9, 0.999, 1e-08, 0.01, 10
VMEM_LIMIT = 56 * 2 ** 20
ADAMW_BLOCK_BYTES = 2 ** 20

TL = 512
TM = 1024
TF = 256
TC = 512
SEG = 8
SEG_LEN = TC // SEG
SCAN_UNROLL = 4
SCAN_W = 512


def _cp(*sem):
    return pltpu.CompilerParams(dimension_semantics=sem, vmem_limit_bytes=VMEM_LIMIT)


def _dot_nn(a, b):
    return jnp.dot(a, b, preferred_element_type=F32)


def _dot_nt(a, b):
    return lax.dot_general(a, b, (((1,), (1,)), ((), ())), preferred_element_type=F32)


def _dot_tn(a, b):
    return lax.dot_general(a, b, (((0,), (0,)), ((), ())), preferred_element_type=F32)


def _rms_fwd(x, g):
    inv = lax.rsqrt(jnp.mean(x * x, axis=-1, keepdims=True) + EPS)
    xh = x * inv
    return xh * g, xh, inv


def _rms_bwd(dy, xh, inv, g):
    dg = jnp.sum(dy * xh, axis=0, keepdims=True)
    dxh = dy * g
    dx = inv * (dxh - xh * jnp.mean(dxh * xh, axis=-1, keepdims=True))
    return dx, dg


_GELU_C = 0.7978845608028654
_GELU_A = 0.044715


def _gelu(y):
    t = jnp.tanh(_GELU_C * (y + _GELU_A * (y * y * y)))
    return 0.5 * y * (1.0 + t), t


def _gelu_grad(y, t):
    return 0.5 * (1.0 + t) + 0.5 * y * (1.0 - t * t) * (_GELU_C * (1.0 + 3.0 * _GELU_A * y * y))


def _sigmoid(x):
    return 1.0 / (1.0 + jnp.exp(-x))


def _full(shape):
    n = len(shape)
    return pl.BlockSpec(shape, lambda *_: (0,) * n)


def _fill_ext(ext_ref, prev_ref, cur_ref, next_ref, i, n, rows):
    ext_ref[0:HALO, :] = jnp.where(i > 0, prev_ref[...], 0.0).astype(ext_ref.dtype)
    ext_ref[HALO:HALO + rows, :] = cur_ref[...]
    ext_ref[HALO + rows:2 * HALO + rows, :] = jnp.where(i < n - 1, next_ref[...], 0.0).astype(ext_ref.dtype)


def _in_proj(x, g, w):
    L, D = x.shape
    E = w.shape[1]

    def body(x_ref, g_ref, w_ref, u_ref, xn_ref):
        y, _, _ = _rms_fwd(x_ref[...], g_ref[...])
        yb = y.astype(BF16)
        xn_ref[...] = yb
        u_ref[...] = _dot_nn(yb, w_ref[...])

    return pl.pallas_call(
        body, name="in_proj", grid=(L // TL,),
        in_specs=[pl.BlockSpec((TL, D), lambda i: (i, 0)), _full((1, D)), _full(w.shape)],
        out_specs=[pl.BlockSpec((TL, E), lambda i: (i, 0)), pl.BlockSpec((TL, D), lambda i: (i, 0))],
        out_shape=[jax.ShapeDtypeStruct((L, E), F32), jax.ShapeDtypeStruct((L, D), BF16)],
        compiler_params=_cp("parallel"))(x, g, w)


def _halo_specs_1d(rows, width, L, col):
    rb = rows // HALO
    last = L // HALO - 1
    return [pl.BlockSpec((HALO, width), lambda i: (jnp.maximum(i * rb - 1, 0), col)),
            pl.BlockSpec((rows, width), lambda i: (i, col)),
            pl.BlockSpec((HALO, width), lambda i: (jnp.minimum((i + 1) * rb, last), col))]


def _pooled_from_ext(ext_ref, t0, rows, L):
    t = t0 + lax.broadcasted_iota(jnp.int32, (rows, 1), 0)
    outs = []
    for gi, w in enumerate(POOL_WINDOWS):
        half = w // 2
        cs = slice(gi * POOL_GROUP, (gi + 1) * POOL_GROUP)
        acc = ext_ref[pl.ds(HALO - half, rows), cs]
        for s in range(-half + 1, half):
            acc = acc + ext_ref[pl.ds(HALO + s, rows), cs]
        cnt = (jnp.minimum(t + half, L) - jnp.maximum(t - half, 0)).astype(F32)
        outs.append(acc / cnt - ext_ref[pl.ds(HALO, rows), cs])
    return outs


def _pool_fwd(u, pool_w_b, pool_scale, g_pool):
    L = u.shape[0]
    n = L // TL

    def body(prev_ref, cur_ref, next_ref, pw_ref, ps_ref, g_ref, out_ref, ext_ref):
        i = pl.program_id(0)
        _fill_ext(ext_ref, prev_ref, cur_ref, next_ref, i, n, TL)
        pooled = _pooled_from_ext(ext_ref, i * TL, TL, L)
        ypre = jnp.concatenate([_dot_nn(pooled[gi].astype(BF16), pw_ref[gi]) for gi in range(4)], axis=-1)
        yn, _, _ = _rms_fwd(ypre * ps_ref[...], g_ref[...])
        out_ref[...] = yn.astype(BF16)

    return pl.pallas_call(
        body, name="pool_fwd", grid=(n,),
        in_specs=_halo_specs_1d(TL, D_POOL, L, 0) + [_full(pool_w_b.shape), _full((1, D_POOL)), _full((1, D_POOL))],
        out_specs=pl.BlockSpec((TL, D_POOL), lambda i: (i, 0)),
        out_shape=jax.ShapeDtypeStruct((L, D_POOL), BF16),
        scratch_shapes=[pltpu.VMEM((TL + 2 * HALO, D_POOL), F32)],
        compiler_params=_cp("parallel"))(u, u, u, pool_w_b, pool_scale, g_pool)


def _pool_bwd_local(dh1, u, w_out_b, pool_w_b, pool_scale, g_pool, comm=None):
    L = u.shape[0]
    n = L // TL
    D = dh1.shape[1]

    def body(dh_ref, prev_ref, cur_ref, next_ref, wo_ref, pw_ref, ps_ref, g_ref,
             dp_ref, gpw_ref, gps_ref, gg_ref, ext_ref):
        i = pl.program_id(0)

        @pl.when(i == 0)
        def _():
            gpw_ref[...] = jnp.zeros_like(gpw_ref)
            gps_ref[...] = jnp.zeros_like(gps_ref)
            gg_ref[...] = jnp.zeros_like(gg_ref)

        _fill_ext(ext_ref, prev_ref, cur_ref, next_ref, i, n, TL)
        pooled = [p.astype(BF16) for p in _pooled_from_ext(ext_ref, i * TL, TL, L)]
        ypre = jnp.concatenate([_dot_nn(pooled[gi], pw_ref[gi]) for gi in range(4)], axis=-1)
        ps = ps_ref[...]
        g = g_ref[...]
        _, xh, inv = _rms_fwd(ypre * ps, g)
        d_yn = _dot_nt(dh_ref[...], wo_ref[...])
        d_y, dg = _rms_bwd(d_yn, xh, inv, g)
        gg_ref[...] += dg
        gps_ref[...] += jnp.sum(d_y * ypre, axis=0, keepdims=True)
        d_ypre = (d_y * ps).astype(BF16)
        for gi in range(4):
            cs = slice(gi * POOL_GROUP, (gi + 1) * POOL_GROUP)
            dp_ref[:, cs] = _dot_nt(d_ypre[:, cs], pw_ref[gi])
            gpw_ref[gi] += _dot_tn(pooled[gi], d_ypre[:, cs])

    return _hosted_call(
        body, comm, name="pool_bwd_local", grid=(n,),
        in_specs=[pl.BlockSpec((TL, D), lambda i: (i, 0))] + _halo_specs_1d(TL, D_POOL, L, 0)
        + [pl.BlockSpec((D_POOL, D), lambda i: (0, 0)), _full(pool_w_b.shape), _full((1, D_POOL)), _full((1, D_POOL))],
        out_specs=[pl.BlockSpec((TL, D_POOL), lambda i: (i, 0)), _full(pool_w_b.shape),
                   _full((1, D_POOL)), _full((1, D_POOL))],
        out_shape=[jax.ShapeDtypeStruct((L, D_POOL), F32), jax.ShapeDtypeStruct(pool_w_b.shape, F32),
                   jax.ShapeDtypeStruct((1, D_POOL), F32), jax.ShapeDtypeStruct((1, D_POOL), F32)],
        scratch_shapes=[pltpu.VMEM((TL + 2 * HALO, D_POOL), F32)],
        args=(dh1, u, u, u, w_out_b, pool_w_b, pool_scale, g_pool))


def _pool_bwd_window(d_pooled):
    L = d_pooled.shape[0]
    n = L // TL
    R = TL + 2 * HALO

    def body(prev_ref, cur_ref, next_ref, out_ref, ext_ref, q_ref):
        i = pl.program_id(0)
        _fill_ext(ext_ref, prev_ref, cur_ref, next_ref, i, n, TL)
        tr = i * TL - HALO + lax.broadcasted_iota(jnp.int32, (R, 1), 0)
        for gi, w in enumerate(POOL_WINDOWS):
            half = w // 2
            cs = slice(gi * POOL_GROUP, (gi + 1) * POOL_GROUP)
            cnt = jnp.maximum(jnp.minimum(tr + half, L) - jnp.maximum(tr - half, 0), 1).astype(F32)
            q_ref[:, cs] = ext_ref[:, cs] / cnt
        for gi, w in enumerate(POOL_WINDOWS):
            half = w // 2
            cs = slice(gi * POOL_GROUP, (gi + 1) * POOL_GROUP)
            acc = q_ref[pl.ds(HALO - half + 1, TL), cs]
            for s in range(-half + 2, half + 1):
                acc = acc + q_ref[pl.ds(HALO + s, TL), cs]
            out_ref[:, cs] = acc - ext_ref[pl.ds(HALO, TL), cs]

    return pl.pallas_call(
        body, name="pool_bwd_window", grid=(n,),
        in_specs=_halo_specs_1d(TL, D_POOL, L, 0),
        out_specs=pl.BlockSpec((TL, D_POOL), lambda i: (i, 0)),
        out_shape=jax.ShapeDtypeStruct((L, D_POOL), F32),
        scratch_shapes=[pltpu.VMEM((R, D_POOL), F32), pltpu.VMEM((R, D_POOL), F32)],
        compiler_params=_cp("parallel"))(d_pooled, d_pooled, d_pooled)


def _ssm_param_fn(lnar, aim, ldt):
    dt = jnp.exp(ldt)
    a_re = -jnp.exp(lnar)
    mag = jnp.exp(a_re * dt)
    ang = aim * dt
    lr, li = mag * jnp.cos(ang), mag * jnp.sin(ang)
    den = a_re * a_re + aim * aim
    fr = ((lr - 1.0) * a_re + li * aim) / den
    fi = (li * a_re - (lr - 1.0) * aim) / den
    return lr, li, fr, fi


def _ssm_params(lnar, aim, ldt):
    def body(a_ref, b_ref, c_ref, lr_ref, li_ref, fr_ref, fi_ref):
        lr, li, fr, fi = _ssm_param_fn(a_ref[...], b_ref[...], c_ref[...])
        lr_ref[...] = lr
        li_ref[...] = li
        fr_ref[...] = fr
        fi_ref[...] = fi

    sh = jax.ShapeDtypeStruct(lnar.shape, F32)
    return pl.pallas_call(body, name="ssm_params", out_shape=[sh] * 4)(lnar, aim, ldt)


def _ssm_params_bwd(lnar, aim, ldt, glr, gli, gfr, gfi):
    def body(a_ref, b_ref, c_ref, g0, g1, g2, g3, da_ref, db_ref, dc_ref):
        _, vjp = jax.vjp(_ssm_param_fn, a_ref[...], b_ref[...], c_ref[...])
        da, db, dc = vjp((g0[...], g1[...], g2[...], g3[...]))
        da_ref[...] = da
        db_ref[...] = db
        dc_ref[...] = jnp.sum(dc, axis=1, keepdims=True)

    return pl.pallas_call(
        body, name="ssm_params_bwd",
        out_shape=[jax.ShapeDtypeStruct(lnar.shape, F32), jax.ShapeDtypeStruct(aim.shape, F32),
                   jax.ShapeDtypeStruct((ldt.shape[0], 1), F32)])(lnar, aim, ldt, glr, gli, gfr, gfi)


def _scan_tables(lam4):
    def build(lr, li, reverse, out_ref, k):
        pr, pi = lr, li
        for d in range(SEG_LEN):
            j = SEG_LEN - 1 - d if reverse else d
            out_ref[k, 0, j:j + 1, :] = pr
            out_ref[k, 1, j:j + 1, :] = pi
            pr, pi = pr * lr - pi * li, pr * li + pi * lr

    def body(lam_ref, out_ref):
        l0r, l0i, l1r, l1i = (lam_ref[j:j + 1, :] for j in range(4))
        build(l0r, l0i, False, out_ref, 0)
        build(l0r, -l0i, True, out_ref, 1)
        build(l1r, l1i, True, out_ref, 2)
        build(l1r, -l1i, False, out_ref, 3)

    return pl.pallas_call(body, name="scan_tables",
                          out_shape=jax.ShapeDtypeStruct((4, 2, SEG_LEN, N_STATE), F32))(lam4)


def _b_block(g):
    q, gl = divmod(g, 4)
    r0, c0 = gl * SSM_STATE, (q % 4) * 4 * SSM_GROUP + gl * SSM_GROUP
    return q, slice(r0, r0 + SSM_STATE), slice(c0, c0 + SSM_GROUP)


def _c_block(g):
    q, rows, cols = _b_block(g)
    return q, cols, rows


def _ssm_expand(b_re, b_im, c_re, c_im, f_re, f_im):
    def body(bre_ref, bim_ref, cre_ref, cim_ref, fre_ref, fim_ref, *rest):
        outs, tmp, bbr_ref, bbi_ref = rest[:8], rest[8], rest[9], rest[10]
        fr, fi, br, bi = fre_ref[...], fim_ref[...], bre_ref[...], bim_ref[...]
        bbr_ref[...] = fr * br - fi * bi
        bbi_ref[...] = fr * bi + fi * br
        for d in range(2):
            for j, (src, where) in enumerate(((bbr_ref, _b_block), (bbi_ref, _b_block),
                                              (cre_ref, _c_block), (cim_ref, _c_block))):
                tmp[...] = jnp.zeros_like(tmp)
                for g in range(N_SSM_GROUPS):
                    q, rows, cols = where(g)
                    tmp[q, rows, cols] = src[d, g]
                outs[4 * d + j][...] = tmp[...].astype(BF16)

    dense = jax.ShapeDtypeStruct((N_QUAD, QUAD, SLAB), BF16)
    return pl.pallas_call(body, name="ssm_expand", out_shape=[dense] * 8,
                          scratch_shapes=[pltpu.VMEM((N_QUAD, QUAD, SLAB), F32), pltpu.VMEM(b_re.shape, F32),
                                          pltpu.VMEM(b_re.shape, F32)],
                          compiler_params=pltpu.CompilerParams(vmem_limit_bytes=VMEM_LIMIT))(
                              b_re, b_im, c_re, c_im, f_re, f_im)


def _ssm_unfold(gbb_re, gbb_im, b_re_t, b_im_t, f_re, f_im):
    def body(gr_ref, gi_ref, br_ref, bi_ref, fr_ref, fi_ref, obr_ref, obi_ref, ofr_ref, ofi_ref):
        gr, gi, br, bi, fr, fi = (r[...] for r in (gr_ref, gi_ref, br_ref, bi_ref, fr_ref, fi_ref))
        obr_ref[...] = fr * gr + fi * gi
        obi_ref[...] = fr * gi - fi * gr
        ofr_ref[...] = jnp.sum(br * gr + bi * gi, axis=2, keepdims=True)
        ofi_ref[...] = jnp.sum(br * gi - bi * gr, axis=2, keepdims=True)

    gb = jax.ShapeDtypeStruct(gbb_re.shape, F32)
    gf = jax.ShapeDtypeStruct(f_re.shape, F32)
    return pl.pallas_call(body, name="ssm_unfold", out_shape=[gb, gb, gf, gf])(
        gbb_re, gbb_im, b_re_t, b_im_t, f_re, f_im)


_SEGMENT_ORDER = np.zeros((TC, TC), np.float32)
for _p in range(TC):
    _SEGMENT_ORDER[_p, (_p % SEG) * SEG_LEN + _p // SEG] = 1.0


def _store_tokens(ref, col0, val, tmp_ref):
    for h in range(val.shape[1] // LANES):
        for j in range(SEG_LEN):
            tmp_ref[pl.ds(h * TC + j, SEG, stride=SEG_LEN), :] = val[SEG * j:SEG * (j + 1), h * LANES:(h + 1) * LANES]
        ref[:, col0 + h * LANES:col0 + (h + 1) * LANES] = tmp_ref[pl.ds(h * TC, TC), :]


def _segment_scan(src_re, src_im, dst_re, dst_im, tab_ref, k, carry_re, carry_im, reverse, s_refs=None):
    lam1, lam_seg = (SEG_LEN - 1, 0) if reverse else (0, SEG_LEN - 1)
    token = (lambda i: SEG_LEN - 1 - i) if reverse else (lambda i: i)
    row_id = lax.broadcasted_iota(jnp.int32, (SEG, SCAN_W), 0)
    zero = jnp.zeros((SEG, SCAN_W), F32)
    sums = []
    for lt in range(N_STATE // SCAN_W):
        sl = slice(lt * SCAN_W, (lt + 1) * SCAN_W)
        lr = jnp.broadcast_to(tab_ref[k, 0, lam1:lam1 + 1, sl], (SEG, SCAN_W))
        li = jnp.broadcast_to(tab_ref[k, 1, lam1:lam1 + 1, sl], (SEG, SCAN_W))

        def local(i, c, sl=sl, lr=lr, li=li):
            for step in range(SCAN_UNROLL):
                rows = pl.ds(pl.multiple_of(token(i * SCAN_UNROLL + step) * SEG, SEG), SEG)
                c = (lr * c[0] - li * c[1] + src_re[rows, sl], lr * c[1] + li * c[0] + src_im[rows, sl])
                dst_re[rows, sl] = c[0]
                dst_im[rows, sl] = c[1]
            return c

        er, ei = lax.fori_loop(0, SEG_LEN // SCAN_UNROLL, local, (zero, zero))

        sr_, si_ = tab_ref[k, 0, lam_seg:lam_seg + 1, sl], tab_ref[k, 1, lam_seg:lam_seg + 1, sl]
        c_r, c_i = carry_re[0:1, sl], carry_im[0:1, sl]
        in_r, in_i = zero, zero
        for r in (range(SEG - 1, -1, -1) if reverse else range(SEG)):
            in_r = jnp.where(row_id == r, c_r, in_r)
            in_i = jnp.where(row_id == r, c_i, in_i)
            c_r, c_i = (er[r:r + 1, :] + sr_ * c_r - si_ * c_i, ei[r:r + 1, :] + sr_ * c_i + si_ * c_r)
        carry_re[0:1, sl] = c_r
        carry_im[0:1, sl] = c_i

        def fix(i, c, sl=sl, in_r=in_r, in_i=in_i):
            for step in range(SCAN_UNROLL):
                j = token(i * SCAN_UNROLL + step)
                rows = pl.ds(pl.multiple_of(j * SEG, SEG), SEG)
                pr = jnp.broadcast_to(tab_ref[k, 0, pl.ds(j, 1), sl], (SEG, SCAN_W))
                pi = jnp.broadcast_to(tab_ref[k, 1, pl.ds(j, 1), sl], (SEG, SCAN_W))
                nr = dst_re[rows, sl] + pr * in_r - pi * in_i
                ni = dst_im[rows, sl] + pr * in_i + pi * in_r
                dst_re[rows, sl] = nr
                dst_im[rows, sl] = ni
                if s_refs is not None:
                    sr = s_refs[0][rows, sl]
                    si = s_refs[1][rows, sl]
                    c = (nr, ni, c[2] + c[0] * sr + c[1] * si, c[3] + c[1] * sr - c[0] * si)
            return c

        if s_refs is None:
            lax.fori_loop(0, SEG_LEN // SCAN_UNROLL, fix, 0)
        else:
            out = lax.fori_loop(0, SEG_LEN // SCAN_UNROLL, fix, (in_r, in_i, zero, zero))
            sums.append((jnp.sum(out[2], axis=0, keepdims=True), jnp.sum(out[3], axis=0, keepdims=True)))
    return sums


def _ssm_scan_fwd(u, b_re, b_im, c_re, c_im, tables, k, reverse, comm=None):
    L = u.shape[0]
    nc = L // TC
    chunk = (lambda i: nc - 1 - i) if reverse else (lambda i: i)
    order = jnp.asarray(_SEGMENT_ORDER, BF16)

    def body(u_ref, ord_ref, bre_ref, bim_ref, cre_ref, cim_ref, tab_ref,
             y_ref, sre_ref, sim_ref, in_re, in_im, carry_re, carry_im, tmp_ref):
        @pl.when(pl.program_id(0) == 0)
        def _():
            carry_re[...] = jnp.zeros_like(carry_re)
            carry_im[...] = jnp.zeros_like(carry_im)

        ub = _dot_nn(ord_ref[...], u_ref[...].astype(BF16)).astype(BF16)
        for q in range(N_QUAD):
            qs = slice(q * QUAD, (q + 1) * QUAD)
            us = ub[:, (q // 4) * SLAB:(q // 4 + 1) * SLAB]
            in_re[:, qs] = _dot_nt(us, bre_ref[q])
            in_im[:, qs] = _dot_nt(us, bim_ref[q])
        _segment_scan(in_re, in_im, sre_ref, sim_ref, tab_ref, k, carry_re, carry_im, reverse)
        for j in range(D_SSM // SLAB):
            acc = jnp.zeros((TC, SLAB), F32)
            for q in range(4 * j, 4 * j + 4):
                qs = slice(q * QUAD, (q + 1) * QUAD)
                acc = acc + _dot_nt(sre_ref[:, qs].astype(BF16), cre_ref[q])
                acc = acc - _dot_nt(sim_ref[:, qs].astype(BF16), cim_ref[q])
            _store_tokens(y_ref, j * SLAB, acc, tmp_ref)

    return _hosted_call(
        body, comm, name="ssm_scan_rev" if reverse else "ssm_scan_fwd", grid=(nc,),
        in_specs=[pl.BlockSpec((TC, D_SSM), lambda i: (chunk(i), 1)), _full(order.shape)]
        + [_full(b_re.shape)] * 4 + [_full(tables.shape)],
        out_specs=[pl.BlockSpec((TC, D_SSM), lambda i: (chunk(i), 0)),
                   pl.BlockSpec((TC, N_STATE), lambda i: (chunk(i), 0)),
                   pl.BlockSpec((TC, N_STATE), lambda i: (chunk(i), 0))],
        out_shape=[jax.ShapeDtypeStruct((L, D_SSM), F32), jax.ShapeDtypeStruct((L, N_STATE), F32),
                   jax.ShapeDtypeStruct((L, N_STATE), F32)],
        scratch_shapes=[pltpu.VMEM((TC, N_STATE), F32), pltpu.VMEM((TC, N_STATE), F32),
                        pltpu.VMEM((8, N_STATE), F32), pltpu.VMEM((8, N_STATE), F32),
                        pltpu.VMEM((SLAB // LANES * TC, LANES), F32)],
        args=(u, order, b_re, b_im, c_re, c_im, tables))


def _quad_channels(q):
    c0 = (q // 4) * SLAB + (q % 4) * 4 * SSM_GROUP
    return slice(c0, c0 + 4 * SSM_GROUP)


def _ssm_scan_bwd(dy, u, s_re, s_im, b_re, b_im, c_re, c_im, tables, k, reverse, comm=None):
    L = u.shape[0]
    nc = L // TC
    chunk = (lambda i: nc - 1 - i) if reverse else (lambda i: i)

    order = jnp.asarray(_SEGMENT_ORDER, BF16)

    def body(dy_ref, u_ref, ord_ref, sre_ref, sim_ref, bre_ref, bim_ref, cre_ref, cim_ref, tab_ref,
             du_ref, ob_re, ob_im, oc_re, oc_im, gv_ref,
             a_re, a_im, carry_re, carry_im, gbr_ref, gbi_ref, gcr_ref, gci_ref, tmp_ref):
        @pl.when(pl.program_id(0) == 0)
        def _():
            carry_re[...] = jnp.zeros_like(carry_re)
            carry_im[...] = jnp.zeros_like(carry_im)
            for r in (gbr_ref, gbi_ref, gcr_ref, gci_ref, gv_ref):
                r[...] = jnp.zeros_like(r)

        dyb = _dot_nn(ord_ref[...], dy_ref[...].astype(BF16)).astype(BF16)
        ub = _dot_nn(ord_ref[...], u_ref[...].astype(BF16)).astype(BF16)
        for q in range(N_QUAD):
            qs = slice(q * QUAD, (q + 1) * QUAD)
            ds = dyb[:, (q // 4) * SLAB:(q // 4 + 1) * SLAB]
            a_re[:, qs] = _dot_nn(ds, cre_ref[q])
            a_im[:, qs] = -_dot_nn(ds, cim_ref[q])
            dq = dyb[:, _quad_channels(q)]
            gcr_ref[q] += _dot_tn(dq, sre_ref[:, qs].astype(BF16))
            gci_ref[q] -= _dot_tn(dq, sim_ref[:, qs].astype(BF16))
        sums = _segment_scan(a_re, a_im, a_re, a_im, tab_ref, k, carry_re, carry_im, reverse,
                             s_refs=(sre_ref, sim_ref))
        for lt, (glr, gli) in enumerate(sums):
            sl = slice(lt * SCAN_W, (lt + 1) * SCAN_W)
            gv_ref[0:1, sl] += glr
            gv_ref[1:2, sl] += gli
        for j in range(D_SSM // SLAB):
            us = ub[:, j * SLAB:(j + 1) * SLAB]
            acc = jnp.zeros((TC, SLAB), F32)
            for q in range(4 * j, 4 * j + 4):
                qs = slice(q * QUAD, (q + 1) * QUAD)
                dbr = a_re[:, qs].astype(BF16)
                dbi = a_im[:, qs].astype(BF16)
                uq = ub[:, _quad_channels(q)]
                gbr_ref[q] += _dot_tn(uq, dbr)
                gbi_ref[q] += _dot_tn(uq, dbi)
                acc = acc + _dot_nn(dbr, bre_ref[q]) + _dot_nn(dbi, bim_ref[q])
            _store_tokens(du_ref, j * SLAB, acc, tmp_ref)

        @pl.when(pl.program_id(0) == nc - 1)
        def _():
            for g in range(N_SSM_GROUPS):
                q, gl = divmod(g, 4)
                rows = slice(gl * SSM_GROUP, (gl + 1) * SSM_GROUP)
                cols = slice(gl * SSM_STATE, (gl + 1) * SSM_STATE)
                for out, acc_ref in ((ob_re, gbr_ref), (ob_im, gbi_ref), (oc_re, gcr_ref), (oc_im, gci_ref)):
                    out[g] = acc_ref[q, rows, cols]

    gshape = jax.ShapeDtypeStruct((N_SSM_GROUPS, SSM_GROUP, SSM_STATE), F32)
    compact = pltpu.VMEM((N_QUAD, 4 * SSM_GROUP, QUAD), F32)
    return _hosted_call(
        body, comm, name="ssm_bwd_rev" if reverse else "ssm_bwd_fwd", grid=(nc,),
        in_specs=[pl.BlockSpec((TC, D_SSM), lambda i: (chunk(i), 0)),
                  pl.BlockSpec((TC, D_SSM), lambda i: (chunk(i), 1)), _full(order.shape),
                  pl.BlockSpec((TC, N_STATE), lambda i: (chunk(i), 0)),
                  pl.BlockSpec((TC, N_STATE), lambda i: (chunk(i), 0))]
        + [_full(b_re.shape)] * 4 + [_full(tables.shape)],
        out_specs=[pl.BlockSpec((TC, D_SSM), lambda i: (chunk(i), 0))] + [_full(gshape.shape)] * 4
        + [_full((2, N_STATE))],
        out_shape=[jax.ShapeDtypeStruct((L, D_SSM), F32), gshape, gshape, gshape, gshape,
                   jax.ShapeDtypeStruct((2, N_STATE), F32)],
        scratch_shapes=[pltpu.VMEM((TC, N_STATE), F32), pltpu.VMEM((TC, N_STATE), F32),
                        pltpu.VMEM((8, N_STATE), F32), pltpu.VMEM((8, N_STATE), F32),
                        compact, compact, compact, compact, pltpu.VMEM((SLAB // LANES * TC, LANES), F32)],
        args=(dy, u, order, s_re, s_im, b_re, b_im, c_re, c_im, tables))


def _ssm_post(yf, yb, u, d, glu_w, glu_b):
    y = yf + yb + d * u
    z, t = _gelu(y)
    zb = z.astype(BF16)
    gate = _sigmoid(_dot_nn(zb, glu_w) + glu_b)
    return y, z, t, zb, gate


def _mix_out(yn_pool, yf, yb, u, x, ssm_d, glu_w_b, glu_b, g_ssm, w_out_b, g_ffn):
    L, D = x.shape

    def body(ynp_ref, yf_ref, yb_ref, u_ref, x_ref, d_ref, gw_ref, gb_ref, gs_ref, wo_ref, gf_ref,
             h1_ref, hn_ref, ycat_ref):
        _, z, _, _, gate = _ssm_post(yf_ref[...], yb_ref[...], u_ref[...], d_ref[...], gw_ref[...], gb_ref[...])
        yns, _, _ = _rms_fwd(z * gate, gs_ref[...])
        ynsb = yns.astype(BF16)
        ynp = ynp_ref[...]
        ycat_ref[:, 0:D_POOL] = ynp
        ycat_ref[:, D_POOL:D] = ynsb
        h1 = x_ref[...] + _dot_nn(ynp, wo_ref[0:D_POOL, :]) + _dot_nn(ynsb, wo_ref[D_POOL:D, :])
        h1_ref[...] = h1
        hn, _, _ = _rms_fwd(h1, gf_ref[...])
        hn_ref[...] = hn.astype(BF16)

    half = lambda c: pl.BlockSpec((TL, D_SSM), lambda i: (i, c))
    row = pl.BlockSpec((TL, D), lambda i: (i, 0))
    return pl.pallas_call(
        body, name="mix_out", grid=(L // TL,),
        in_specs=[half(0), half(0), half(0), half(1), row, _full((1, D_SSM)), _full(glu_w_b.shape),
                  _full((1, D_SSM)), _full((1, D_SSM)), _full(w_out_b.shape), _full((1, D))],
        out_specs=[row, row, row],
        out_shape=[jax.ShapeDtypeStruct((L, D), F32), jax.ShapeDtypeStruct((L, D), BF16),
                   jax.ShapeDtypeStruct((L, D), BF16)],
        compiler_params=_cp("parallel"))(yn_pool, yf, yb, u, x, ssm_d, glu_w_b, glu_b, g_ssm, w_out_b, g_ffn)


def _ssm_bwd_local(dh1, yf, yb, u, ssm_d, glu_w_b, glu_b, g_ssm, w_out_b, comm=None):
    L, D = dh1.shape

    def body(dh_ref, yf_ref, yb_ref, u_ref, d_ref, gw_ref, gb_ref, gs_ref, wo_ref,
             dy_ref, du_ref, ggw_ref, ggb_ref, gd_ref, ggs_ref):
        @pl.when(pl.program_id(0) == 0)
        def _():
            for r in (ggw_ref, ggb_ref, gd_ref, ggs_ref):
                r[...] = jnp.zeros_like(r)

        u = u_ref[...]
        d = d_ref[...]
        y, z, t, zb, gate = _ssm_post(yf_ref[...], yb_ref[...], u, d, gw_ref[...], gb_ref[...])
        gs = gs_ref[...]
        _, xh, inv = _rms_fwd(z * gate, gs)
        d_yn = _dot_nt(dh_ref[...], wo_ref[...])
        d_o, dgs = _rms_bwd(d_yn, xh, inv, gs)
        ggs_ref[...] += dgs
        d_zg = d_o * z * gate * (1.0 - gate)
        d_zgb = d_zg.astype(BF16)
        ggb_ref[...] += jnp.sum(d_zg, axis=0, keepdims=True)
        ggw_ref[...] += _dot_tn(zb, d_zgb)
        d_z = d_o * gate + _dot_nt(d_zgb, gw_ref[...])
        d_y = d_z * _gelu_grad(y, t)
        gd_ref[...] += jnp.sum(d_y * u, axis=0, keepdims=True)
        dy_ref[...] = d_y
        du_ref[...] = d_y * d

    half = lambda c: pl.BlockSpec((TL, D_SSM), lambda i: (i, c))
    vec = _full((1, D_SSM))
    return _hosted_call(
        body, comm, name="ssm_bwd_local", grid=(L // TL,),
        in_specs=[pl.BlockSpec((TL, D), lambda i: (i, 0)), half(0), half(0), half(1), vec, _full(glu_w_b.shape),
                  vec, vec, pl.BlockSpec((D_SSM, D), lambda i: (1, 0))],
        out_specs=[half(0), half(0), _full(glu_w_b.shape), vec, vec, vec],
        out_shape=[jax.ShapeDtypeStruct((L, D_SSM), F32), jax.ShapeDtypeStruct((L, D_SSM), F32),
                   jax.ShapeDtypeStruct(glu_w_b.shape, F32)] + [jax.ShapeDtypeStruct((1, D_SSM), F32)] * 3,
        scratch_shapes=[], args=(dh1, yf, yb, u, ssm_d, glu_w_b, glu_b, g_ssm, w_out_b))


def _in_bwd(du_pool, du_a, du_b, du_c, dh1, x, g, w_in_b, comm=None):
    L, D = x.shape

    def body(p_ref, a_ref, b_ref, c_ref, dh_ref, x_ref, g_ref, w_ref, dx_ref, dub_ref, gg_ref):
        @pl.when(pl.program_id(0) == 0)
        def _():
            gg_ref[...] = jnp.zeros_like(gg_ref)

        dub_ref[:, 0:D_POOL] = p_ref[...].astype(BF16)
        dub_ref[:, D_POOL:D] = (a_ref[...] + b_ref[...] + c_ref[...]).astype(BF16)
        d_xn = _dot_nt(dub_ref[...], w_ref[...])
        gv = g_ref[...]
        _, xh, inv = _rms_fwd(x_ref[...], gv)
        dx, dg = _rms_bwd(d_xn, xh, inv, gv)
        gg_ref[...] += dg
        dx_ref[...] = dh_ref[...] + dx

    half = pl.BlockSpec((TL, D_SSM), lambda i: (i, 0))
    row = pl.BlockSpec((TL, D), lambda i: (i, 0))
    return _hosted_call(
        body, comm, name="in_bwd", grid=(L // TL,),
        in_specs=[half, half, half, half, row, row, _full((1, D)), _full(w_in_b.shape)],
        out_specs=[row, row, _full((1, D))],
        out_shape=[jax.ShapeDtypeStruct((L, D), F32), jax.ShapeDtypeStruct((L, D), BF16),
                   jax.ShapeDtypeStruct((1, D), F32)],
        scratch_shapes=[], args=(du_pool, du_a, du_b, du_c, dh1, x, g, w_in_b))


def _ffn_up(hn, w_up4):
    L, D = hn.shape

    def body(h_ref, w_ref, o_ref):
        o_ref[...] = _dot_nn(h_ref[...], w_ref[...]).astype(BF16)

    rows = min(TM, L)
    return pl.pallas_call(
        body, name="ffn_up", grid=(4, L // rows),
        in_specs=[pl.BlockSpec((rows, D), lambda j, i: (i, 0)), pl.BlockSpec((None, D, FF_BLK), lambda j, i: (j, 0, 0))],
        out_specs=pl.BlockSpec((rows, FF_BLK), lambda j, i: (i, j)),
        out_shape=jax.ShapeDtypeStruct((L, 4 * FF_BLK), BF16),
        compiler_params=_cp("parallel", "parallel"))(hn, w_up4)


def _halo_specs_2d(rows, width, L, col, order):
    rb = rows // HALO_B
    last = L // HALO_B - 1
    if order == "ik":
        wrap = lambda f: (lambda i, k: f(i, k))
    else:
        wrap = lambda f: (lambda k, i: f(i, k))
    return [pl.BlockSpec((HALO_B, width), wrap(lambda i, k: (jnp.maximum(i * rb - 1, 0), col(k)))),
            pl.BlockSpec((rows, width), wrap(lambda i, k: (i, col(k)))),
            pl.BlockSpec((HALO_B, width), wrap(lambda i, k: (jnp.minimum((i + 1) * rb, last), col(k))))]


def _shift_mats(rows):
    r = lax.broadcasted_iota(jnp.int32, (rows, rows), 0)
    c = lax.broadcasted_iota(jnp.int32, (rows, rows), 1)
    return (c == r - 1).astype(BF16), (c == r + 1).astype(BF16)


def _neighbours(x, prev_ref, next_ref, cs, i, n, mats):
    rows = x.shape[0]
    row = lax.broadcasted_iota(jnp.int32, (rows, 1), 0)
    before = jnp.where(i > 0, prev_ref[:, cs].astype(F32)[HALO_B - 1:HALO_B, :], 0.0)
    after = jnp.where(i < n - 1, next_ref[:, cs].astype(F32)[0:1, :], 0.0)
    if mats is None:
        xf = x.astype(F32)
        down, up = pltpu.roll(xf, 1, 0), pltpu.roll(xf, rows - 1, 0)
    else:
        down, up = _dot_nn(mats[0], x), _dot_nn(mats[1], x)
    return jnp.where(row == 0, before, down), jnp.where(row == rows - 1, after, up)


def _conv3(x, before, after, w, b):
    return before * w[0:1, :] + x.astype(F32) * w[1:2, :] + after * w[2:3, :] + b


def _col_chunks(width, size=256):
    return [slice(c, min(c + size, width)) for c in range(0, width, size)]


def _ffn_down_loss(up, conv_w, conv_b, w_down_b, h1, target, g_final):
    L, D = h1.shape
    n = L // TF
    nk = D_FF // FF_BLK

    def body(vp, vc, vn, gp, gc, gn, wv_ref, wg_ref, bv_ref, bg_ref, wd_ref, h1_ref, t_ref, gf_ref,
             a_ref, cv_ref, cg_ref, dh2_ref, dh2b_ref, loss_ref, gg_ref, acc_ref):
        i = pl.program_id(0)
        k = pl.program_id(1)

        @pl.when((i == 0) & (k == 0))
        def _():
            loss_ref[...] = jnp.zeros_like(loss_ref)
            gg_ref[...] = jnp.zeros_like(gg_ref)

        @pl.when(k == 0)
        def _():
            acc_ref[...] = jnp.zeros_like(acc_ref)

        mats = _shift_mats(TF)
        for cs in _col_chunks(FF_BLK):
            xv, xg = vc[:, cs], gc[:, cs]
            val = _conv3(xv, *_neighbours(xv, vp, vn, cs, i, n, mats), wv_ref[:, cs], bv_ref[:, cs])
            gate = _conv3(xg, *_neighbours(xg, gp, gn, cs, i, n, mats), wg_ref[:, cs], bg_ref[:, cs])
            a_ref[:, cs] = (val * (gate * _sigmoid(gate))).astype(BF16)
            cv_ref[:, cs] = val.astype(BF16)
            cg_ref[:, cs] = gate.astype(BF16)
        acc_ref[...] += _dot_nn(a_ref[...], wd_ref[pl.ds(pl.multiple_of(k * FF_BLK, LANES), FF_BLK), :])

        @pl.when(k == nk - 1)
        def _():
            gf = gf_ref[...]
            y, xh, inv = _rms_fwd(h1_ref[...] + acc_ref[...], gf)
            diff = y - t_ref[...]
            part = 0.5 * jnp.sum(jnp.mean(diff * diff, axis=-1, keepdims=True), axis=0, keepdims=True)
            loss_ref[...] += jnp.broadcast_to(part, loss_ref.shape)
            dx, dg = _rms_bwd(diff * (1.0 / D), xh, inv, gf)
            gg_ref[...] += dg
            dh2_ref[...] = dx
            dh2b_ref[...] = dx.astype(BF16)

    row = pl.BlockSpec((TF, D), lambda i, k: (i, 0))
    cw = lambda off: pl.BlockSpec((3, FF_BLK), lambda i, k: (0, k + off))
    cb = lambda off: pl.BlockSpec((1, FF_BLK), lambda i, k: (0, k + off))
    return pl.pallas_call(
        body, name="ffn_down_loss", grid=(n, nk),
        in_specs=_halo_specs_2d(TF, FF_BLK, L, lambda k: k, "ik") + _halo_specs_2d(TF, FF_BLK, L, lambda k: k + nk, "ik")
        + [cw(0), cw(nk), cb(0), cb(nk), _full(w_down_b.shape), row, row, _full((1, D))],
        out_specs=[pl.BlockSpec((TF, FF_BLK), lambda i, k: (i, k))] * 3 + [row, row, _full((1, LANES)), _full((1, D))],
        out_shape=[jax.ShapeDtypeStruct((L, D_FF), BF16)] * 3
        + [jax.ShapeDtypeStruct((L, D), F32), jax.ShapeDtypeStruct((L, D), BF16),
           jax.ShapeDtypeStruct((1, LANES), F32), jax.ShapeDtypeStruct((1, D), F32)],
        scratch_shapes=[pltpu.VMEM((TF, D), F32)],
        compiler_params=_cp("arbitrary", "arbitrary"))(
            up, up, up, up, up, up, conv_w, conv_w, conv_b, conv_b, w_down_b, h1, target, g_final)


def _ffn_act_bwd(c_val, c_gate, w_down_b, dh2):
    L, D = dh2.shape
    n = L // TL
    nk = D_FF // FF_BLK

    def body(v_ref, g_ref, wd_ref, dh_ref, dv_ref, dg_ref, gbv_ref, gbg_ref):
        @pl.when(pl.program_id(1) == 0)
        def _():
            gbv_ref[...] = jnp.zeros_like(gbv_ref)
            gbg_ref[...] = jnp.zeros_like(gbg_ref)

        dh = dh_ref[...]
        for cs in _col_chunks(FF_BLK):
            val, gate = v_ref[:, cs].astype(F32), g_ref[:, cs].astype(F32)
            d_a = _dot_nt(dh, wd_ref[cs, :])
            sg = _sigmoid(gate)
            d_val = d_a * (gate * sg)
            d_gate = d_a * val * (sg * (1.0 + gate * (1.0 - sg)))
            dv_ref[:, cs] = d_val.astype(BF16)
            dg_ref[:, cs] = d_gate.astype(BF16)
            gbv_ref[:, cs] += jnp.sum(d_val, axis=0, keepdims=True)
            gbg_ref[:, cs] += jnp.sum(d_gate, axis=0, keepdims=True)

    blk = pl.BlockSpec((TL, FF_BLK), lambda k, i: (i, k))
    acc = pl.BlockSpec((1, FF_BLK), lambda k, i: (0, k))
    return pl.pallas_call(
        body, name="ffn_act_bwd", grid=(nk, n),
        in_specs=[blk, blk, pl.BlockSpec((FF_BLK, D), lambda k, i: (k, 0)), pl.BlockSpec((TL, D), lambda k, i: (i, 0))],
        out_specs=[blk, blk, acc, acc],
        out_shape=[jax.ShapeDtypeStruct((L, D_FF), BF16), jax.ShapeDtypeStruct((L, D_FF), BF16),
                   jax.ShapeDtypeStruct((1, D_FF), F32), jax.ShapeDtypeStruct((1, D_FF), F32)],
        compiler_params=_cp("arbitrary", "arbitrary"))(c_val, c_gate, w_down_b, dh2)


def _ffn_up_bwd(d_val, d_gate, up, conv_w, w_up4, h1, dh2, g_ffn):
    L, D = h1.shape
    n = L // TF
    nk = D_FF // FF_BLK

    def body(vp, vc, vn, gp, gc, gn, uv_ref, ug_ref, wv_ref, wg_ref, wu_ref, h1_ref, dh2_ref, g_ref,
             dup_ref, dh1_ref, dh1b_ref, gg_ref, gcw_ref, acc_ref):
        i = pl.program_id(0)
        k = pl.program_id(1)

        @pl.when((i == 0) & (k == 0))
        def _():
            gg_ref[...] = jnp.zeros_like(gg_ref)
            gcw_ref[...] = jnp.zeros_like(gcw_ref)

        @pl.when(k == 0)
        def _():
            acc_ref[...] = jnp.zeros_like(acc_ref)

        acc = jnp.zeros((TF, D), F32)
        for j, (blocks, u_ref, w_ref) in enumerate((((vp, vc, vn), uv_ref, wv_ref), ((gp, gc, gn), ug_ref, wg_ref))):
            for cs in _col_chunks(FF_BLK):
                d = blocks[1][:, cs]
                before, after = _neighbours(d, blocks[0], blocks[2], cs, i, n, None)
                taps = (after, d.astype(F32), before)
                w = w_ref[:, cs]
                d_up = (taps[0] * w[0:1, :] + taps[1] * w[1:2, :] + taps[2] * w[2:3, :]).astype(BF16)
                dup_ref[j, :, cs] = d_up
                acc = acc + _dot_nt(d_up, wu_ref[k + j * nk, :, cs])
                x = u_ref[:, cs].astype(F32)
                for r in range(3):
                    gcw_ref[j, k, r:r + 1, cs] += jnp.sum(taps[r] * x, axis=0, keepdims=True)
        acc_ref[...] += acc

        @pl.when(k == nk - 1)
        def _():
            g = g_ref[...]
            _, xh, inv = _rms_fwd(h1_ref[...], g)
            dx, dg = _rms_bwd(acc_ref[...], xh, inv, g)
            gg_ref[...] += dg
            dh1 = dh2_ref[...] + dx
            dh1_ref[...] = dh1
            dh1b_ref[...] = dh1.astype(BF16)

    row = pl.BlockSpec((TF, D), lambda i, k: (i, 0))
    cw = lambda off: pl.BlockSpec((3, FF_BLK), lambda i, k: (0, k + off))
    tile = lambda off: pl.BlockSpec((TF, FF_BLK), lambda i, k: (i, k + off))
    return pl.pallas_call(
        body, name="ffn_up_bwd", grid=(n, nk),
        in_specs=_halo_specs_2d(TF, FF_BLK, L, lambda k: k, "ik") + _halo_specs_2d(TF, FF_BLK, L, lambda k: k, "ik")
        + [tile(0), tile(nk), cw(0), cw(nk), _full(w_up4.shape), row, row, _full((1, D))],
        out_specs=[pl.BlockSpec((2, None, TF, FF_BLK), lambda i, k: (0, k, i, 0)), row, row, _full((1, D)),
                   _full((2, nk, 3, FF_BLK))],
        out_shape=[jax.ShapeDtypeStruct((2, nk, L, FF_BLK), BF16), jax.ShapeDtypeStruct((L, D), F32),
                   jax.ShapeDtypeStruct((L, D), BF16), jax.ShapeDtypeStruct((1, D), F32),
                   jax.ShapeDtypeStruct((2, nk, 3, FF_BLK), F32)],
        scratch_shapes=[pltpu.VMEM((TF, D), F32)],
        compiler_params=_cp("arbitrary", "arbitrary"))(
            d_val, d_val, d_val, d_gate, d_gate, d_gate, up, up, conv_w, conv_w, w_up4, h1, dh2, g_ffn)


def _matmul_tn(a, b, tm, tn, name, tk=2048):
    L, M = a.shape
    N = b.shape[1]
    tk = min(tk, L)

    def body(a_ref, b_ref, o_ref):
        @pl.when(pl.program_id(2) == 0)
        def _():
            o_ref[...] = jnp.zeros_like(o_ref)

        o_ref[...] += _dot_tn(a_ref[...], b_ref[...])

    return pl.pallas_call(
        body, name=name, grid=(M // tm, N // tn, L // tk),
        in_specs=[pl.BlockSpec((tk, tm), lambda m, n, l: (l, m)), pl.BlockSpec((tk, tn), lambda m, n, l: (l, n))],
        out_specs=pl.BlockSpec((tm, tn), lambda m, n, l: (m, n)),
        out_shape=jax.ShapeDtypeStruct((M, N), F32),
        compiler_params=_cp("parallel", "parallel", "arbitrary"))(a, b)


def _matmul_tn_blocks(a, b, tm, name, tk=2048):
    L, M = a.shape
    J, _, N = b.shape
    tk = min(tk, L)

    def body(a_ref, b_ref, o_ref):
        @pl.when(pl.program_id(2) == 0)
        def _():
            o_ref[...] = jnp.zeros_like(o_ref)

        o_ref[...] += _dot_tn(a_ref[...], b_ref[...])

    return pl.pallas_call(
        body, name=name, grid=(M // tm, J, L // tk),
        in_specs=[pl.BlockSpec((tk, tm), lambda m, j, l: (l, m)), pl.BlockSpec((None, tk, N), lambda m, j, l: (j, l, 0))],
        out_specs=pl.BlockSpec((None, tm, N), lambda m, j, l: (j, m, 0)),
        out_shape=jax.ShapeDtypeStruct((J, M, N), F32),
        compiler_params=_cp("parallel", "parallel", "arbitrary"))(a, b)


def _row_tile(rows):
    for t in (512, 352, 256, 128, 64, 8):
        if rows % t == 0:
            return t
    return rows


def _add_half(g, r, c_arr, name, out_dtype=F32):
    _, _, R, C = g.shape
    tr = _row_tile(R)

    def body(c_ref, g_ref, r_ref, o_ref):
        o_ref[...] = (g_ref[...] + r_ref[...]).astype(out_dtype)

    return pl.pallas_call(
        body, name=name,
        grid_spec=pltpu.PrefetchScalarGridSpec(
            num_scalar_prefetch=1, grid=(g.shape[0], R // tr),
            in_specs=[pl.BlockSpec((None, None, tr, C), lambda j, i, c: (j, c[0], i, 0)),
                      pl.BlockSpec((None, tr, C), lambda j, i, c: (j, i, 0))],
            out_specs=pl.BlockSpec((None, tr, C), lambda j, i, c: (j, i, 0))),
        out_shape=jax.ShapeDtypeStruct(r.shape, out_dtype),
        compiler_params=_cp("parallel", "parallel"))(c_arr, g, r)


def _sum4(p, name):
    _, R, C = p.shape
    tr = _row_tile(R)

    def body(p_ref, o_ref):
        q = [p_ref[j].astype(F32) for j in range(4)]
        o_ref[...] = ((q[0] + q[1]) + q[2]) + q[3]

    return pl.pallas_call(
        body, name=name, grid=(R // tr,),
        in_specs=[pl.BlockSpec((4, tr, C), lambda i: (0, i, 0))],
        out_specs=pl.BlockSpec((tr, C), lambda i: (i, 0)),
        out_shape=jax.ShapeDtypeStruct((R, C), F32), compiler_params=_cp("parallel"))(p)


def _adamw_refs(w_ref, g_ref, m_ref, v_ref, d_ref, nm_ref, nv_ref):
    gv = g_ref[...]
    nm = ADAM_B1 * m_ref[...] + (1.0 - ADAM_B1) * gv
    nv = ADAM_B2 * v_ref[...] + (1.0 - ADAM_B2) * (gv * gv)
    m_hat = nm / (1.0 - ADAM_B1 ** ADAM_STEP)
    v_hat = nv / (1.0 - ADAM_B2 ** ADAM_STEP)
    d_ref[...] = -ADAM_LR * (m_hat / (jnp.sqrt(v_hat) + ADAM_EPS) + ADAM_WD * w_ref[...])
    nm_ref[...] = nm
    nv_ref[...] = nv


def _adamw_many(ws, gs, ms, vs, name):
    n = len(ws)

    def body(*refs):
        for k in range(n):
            _adamw_refs(*(refs[j * n + k] for j in range(7)))

    out_shape = [jax.ShapeDtypeStruct(w.shape, F32) for w in ws] * 3
    res = pl.pallas_call(body, name=name, out_shape=out_shape,
                         compiler_params=pltpu.CompilerParams(vmem_limit_bytes=VMEM_LIMIT))(*ws, *gs, *ms, *vs)
    return res[:n], res[n:2 * n], res[2 * n:]


def _join_rows(own, other, c_arr, name):
    R, C = own.shape
    tr = _row_tile(R)

    def body(c_ref, own_ref, other_ref, o_ref):
        o_ref[...] = jnp.where(pl.program_id(0) == c_ref[0], own_ref[...], other_ref[...])

    half = pl.BlockSpec((tr, C), lambda h, i, c: (i, 0))
    return pl.pallas_call(
        body, name=name,
        grid_spec=pltpu.PrefetchScalarGridSpec(
            num_scalar_prefetch=1, grid=(2, R // tr), in_specs=[half, half],
            out_specs=pl.BlockSpec((tr, C), lambda h, i, c: (h * (R // tr) + i, 0))),
        out_shape=jax.ShapeDtypeStruct((2 * R, C), F32),
        compiler_params=_cp("parallel", "parallel"))(c_arr, own, other)


def _adamw_halves(w, own, other, m, v, name, comm=None):
    R, C = own.shape
    tr = _row_tile(R)
    while tr * C * 4 > ADAMW_BLOCK_BYTES and tr % 16 == 0:
        tr //= 2

    def body(w_ref, own_ref, other_ref, m_ref, v_ref, g_ref, d_ref, nm_ref, nv_ref):
        g_ref[...] = jnp.where(pl.program_id(0) == lax.axis_index("c"), own_ref[...], other_ref[...])
        _adamw_refs(w_ref, g_ref, m_ref, v_ref, d_ref, nm_ref, nv_ref)

    half = pl.BlockSpec((tr, C), lambda h, i: (i, 0))
    full = pl.BlockSpec((tr, C), lambda h, i: (h * (R // tr) + i, 0))
    sh = jax.ShapeDtypeStruct((2 * R, C), F32)
    return _hosted_call(body, comm, name=name, grid=(2, R // tr), in_specs=[full, half, half, full, full],
                        out_specs=[full] * 4, out_shape=[sh] * 4, scratch_shapes=[], args=(w, own, other, m, v))


_ANY = pl.BlockSpec(memory_space=pl.ANY)


def _position():
    return lax.axis_index("x"), lax.axis_index("y"), lax.axis_index("c")


class _Comm:
    def __init__(self, arrs, out_shape, sems, start, finish):
        self.arrs, self.out_shape, self.sems, self.start, self.finish = arrs, out_shape, sems, start, finish


def _comm_call(comm, name):
    n, m = len(comm.arrs), len(comm.out_shape)

    def body(*refs):
        ins, outs, sems = refs[:n], refs[n:n + m], refs[n + m:]
        comm.start(ins, outs, sems)
        comm.finish(ins, outs, sems)

    return pl.pallas_call(
        body, name=name, in_specs=[_ANY] * n, out_specs=[_ANY] * m, out_shape=comm.out_shape,
        scratch_shapes=comm.sems, compiler_params=pltpu.CompilerParams(has_side_effects=True))(*comm.arrs)


def _hosted_call(body, comm, *, name, grid, in_specs, out_specs, out_shape, scratch_shapes, args):
    sem = ("arbitrary",) * len(grid)
    if comm is None:
        return pl.pallas_call(body, name=name, grid=grid, in_specs=in_specs, out_specs=out_specs, out_shape=out_shape,
                              scratch_shapes=scratch_shapes, compiler_params=_cp(*sem))(*args), []
    n_in, n_out, n_scr = len(in_specs), len(out_specs), len(scratch_shapes)
    ci, co = len(comm.arrs), len(comm.out_shape)

    def full(*refs):
        ins, refs = refs[:n_in], refs[n_in:]
        cins, refs = refs[:ci], refs[ci:]
        outs, refs = refs[:n_out], refs[n_out:]
        couts, refs = refs[:co], refs[co:]
        scr, csems = refs[:n_scr], refs[n_scr:]
        first, last = True, True
        for d, size in enumerate(grid):
            first = first & (pl.program_id(d) == 0)
            last = last & (pl.program_id(d) == size - 1)

        @pl.when(first)
        def _():
            comm.start(cins, couts, csems)

        body(*ins, *outs, *scr)

        @pl.when(last)
        def _():
            comm.finish(cins, couts, csems)

    res = pl.pallas_call(
        full, name=name, grid=grid, in_specs=list(in_specs) + [_ANY] * ci, out_specs=list(out_specs) + [_ANY] * co,
        out_shape=list(out_shape) + list(comm.out_shape), scratch_shapes=list(scratch_shapes) + list(comm.sems),
        compiler_params=_cp(*sem))(*args, *comm.arrs)
    return res[:n_out], res[n_out:]


def _comm_join(*comms):
    def parts(xs, attr):
        out, at = [], 0
        for cm in comms:
            n = len(getattr(cm, attr))
            out.append(xs[at:at + n])
            at += n
        return out

    def start(ins, outs, sems):
        for cm, i, o, s in zip(comms, parts(ins, "arrs"), parts(outs, "out_shape"), parts(sems, "sems")):
            cm.start(i, o, s)

    def finish(ins, outs, sems):
        for cm, i, o, s in zip(comms, parts(ins, "arrs"), parts(outs, "out_shape"), parts(sems, "sems")):
            cm.finish(i, o, s)

    cat = lambda attr: [x for cm in comms for x in getattr(cm, attr)]
    return _Comm(cat("arrs"), cat("out_shape"), cat("sems"), start, finish)


def _dma_sems(*counts):
    return [pltpu.SemaphoreType.DMA((n,)) for n in counts]


def _comm_pair_swap(arrs, half=False):
    n = len(arrs)
    out_shape = [jax.ShapeDtypeStruct(a.shape[:1] + a.shape[2:] if half else a.shape, a.dtype) for a in arrs]

    def copies(ins, outs, sems):
        x, y, c = _position()
        return [pltpu.make_async_remote_copy(
            src_ref=ins[k].at[:, 1 - c] if half else ins[k], dst_ref=outs[k], send_sem=sems[0].at[k],
            recv_sem=sems[1].at[k], device_id=(x, y, 1 - c), device_id_type=MESH) for k in range(n)]

    def start(ins, outs, sems):
        for cp in copies(ins, outs, sems):
            cp.start()

    def finish(ins, outs, sems):
        for cp in copies(ins, outs, sems):
            cp.wait()

    return _Comm(arrs, out_shape, _dma_sems(n, n), start, finish)


def _chip_of(j, c):
    return (jnp.right_shift(j, 1), jnp.bitwise_and(j, 1), c)


def _comm_chip_exchange(arrs, scatter):
    n = len(arrs)
    out_shape = [jax.ShapeDtypeStruct(a.shape if scatter else (4,) + a.shape, a.dtype) for a in arrs]

    def copies(ins, outs, sems):
        x, y, c = _position()
        me = 2 * x + y
        local, sent, landed = [], [], []
        for k in range(n):
            local.append(pltpu.make_async_copy(ins[k].at[me] if scatter else ins[k], outs[k].at[me], sems[2].at[k]))
            for d in (1, 2, 3):
                j = jnp.bitwise_xor(me, d)
                s = 3 * k + d - 1
                src = ins[k].at[j] if scatter else ins[k]
                for dst, group in ((outs[k].at[me], sent), (outs[k].at[j], landed)):
                    group.append(pltpu.make_async_remote_copy(
                        src_ref=src, dst_ref=dst, send_sem=sems[0].at[s], recv_sem=sems[1].at[s],
                        device_id=_chip_of(j, c), device_id_type=MESH))
        return local, sent, landed

    def start(ins, outs, sems):
        local, sent, _ = copies(ins, outs, sems)
        for cp in local + sent:
            cp.start()

    def finish(ins, outs, sems):
        local, sent, landed = copies(ins, outs, sems)
        for cp in sent:
            cp.wait_send()
        for cp in landed:
            cp.wait_recv()
        for cp in local:
            cp.wait()

    return _Comm(arrs, out_shape, _dma_sems(3 * n, 3 * n, n), start, finish)


LOCAL_PARTS = 4


def _comm_gather_split(shards, whole):
    n, nw = len(shards), len(whole)
    arrs = list(shards) + list(whole)
    out_shape = [jax.ShapeDtypeStruct((4,) + a.shape, a.dtype) for a in arrs]

    def copies(ins, outs, sems):
        x, y, c = _position()
        me = 2 * x + y
        local, sent, landed, passed, passed_in = [], [], [], [], []
        for k in range(n + nw):
            if k >= n:
                local.append(pltpu.make_async_copy(ins[k], outs[k].at[me], sems[4].at[LOCAL_PARTS * k]))
            else:
                part = shards[k].shape[0] // LOCAL_PARTS
                for r in range(LOCAL_PARTS):
                    local.append(pltpu.make_async_copy(ins[k].at[pl.ds(r * part, part)],
                                                       outs[k].at[me, pl.ds(r * part, part)],
                                                       sems[4].at[LOCAL_PARTS * k + r]))
            for d in (1, 2, 3):
                j = jnp.bitwise_xor(me, d)
                s = 3 * k + d - 1
                if k >= n:
                    src, mine, theirs = ins[k], outs[k].at[me], outs[k].at[j]
                else:
                    h = shards[k].shape[0] // 2
                    rows = pl.ds(pl.multiple_of(c * h, 16), h)
                    other = pl.ds(pl.multiple_of((1 - c) * h, 16), h)
                    src, mine, theirs = ins[k].at[rows], outs[k].at[me, rows], outs[k].at[j, rows]
                    for dst, group in ((theirs, passed), (outs[k].at[j, other], passed_in)):
                        group.append(pltpu.make_async_remote_copy(
                            src_ref=theirs, dst_ref=dst, send_sem=sems[2].at[s], recv_sem=sems[3].at[s],
                            device_id=(x, y, 1 - c), device_id_type=MESH))
                for dst, group in ((mine, sent), (theirs, landed)):
                    group.append(pltpu.make_async_remote_copy(
                        src_ref=src, dst_ref=dst, send_sem=sems[0].at[s], recv_sem=sems[1].at[s],
                        device_id=_chip_of(j, c), device_id_type=MESH))
        return local, sent, landed, passed, passed_in

    def start(ins, outs, sems):
        local, sent, _, _, _ = copies(ins, outs, sems)
        for cp in local + sent:
            cp.start()

    def finish(ins, outs, sems):
        local, sent, landed, passed, passed_in = copies(ins, outs, sems)
        for cp in landed[:3 * n]:
            cp.wait_recv()
        for cp in passed:
            cp.start()
        for cp in landed[3 * n:]:
            cp.wait_recv()
        for cp in sent:
            cp.wait_send()
        for cp in passed:
            cp.wait_send()
        for cp in passed_in:
            cp.wait_recv()
        for cp in local:
            cp.wait()

    t = 3 * (n + nw)
    return _Comm(arrs, out_shape, _dma_sems(t, t, max(3 * n, 1), max(3 * n, 1), LOCAL_PARTS * (n + nw)), start, finish)


def _pack(arrs, row_multiple):
    parts = []
    for a in arrs:
        flat = a.reshape(-1).astype(F32)
        pad = (-flat.shape[0]) % LANES
        parts.append(jnp.pad(flat, (0, pad)) if pad else flat)
    flat = jnp.concatenate(parts)
    rows = -(-flat.shape[0] // LANES)
    rows_p = -(-rows // row_multiple) * row_multiple
    return jnp.pad(flat, (0, rows_p * LANES - flat.shape[0])).reshape(rows_p, LANES)


def _unpack(packed, shapes):
    flat = packed.reshape(-1)
    outs, off = [], 0
    for sh in shapes:
        size = int(np.prod(sh))
        outs.append(flat[off:off + size].reshape(sh))
        off += size + (-size) % LANES
    return outs


SMALL = ["norm_mix_g", "pool_w", "pool_scale", "ssm_log_neg_a_re", "ssm_a_im", "ssm_log_dt", "ssm_b_re", "ssm_b_im",
         "ssm_c_re", "ssm_c_im", "ssm_d", "glu_b", "out_norm_pool_g", "out_norm_ssm_g", "norm_ffn_g", "conv_b",
         "final_norm_g"]
BIG = ["w_in", "glu_w", "w_out", "w_up", "w_down"]
WIDE = ["pool_w", "ssm_b_re", "ssm_b_im", "ssm_c_re", "ssm_c_im"]
WEIGHTS = ['norm_mix_g', 'w_in', 'pool_w', 'pool_scale', 'ssm_log_neg_a_re', 'ssm_a_im', 'ssm_log_dt', 'ssm_b_re',
           'ssm_b_im', 'ssm_c_re', 'ssm_c_im', 'ssm_d', 'glu_w', 'glu_b', 'out_norm_pool_g', 'out_norm_ssm_g', 'w_out',
           'norm_ffn_g', 'w_up', 'conv_w', 'conv_b', 'w_down', 'final_norm_g']


def _local_step(x, target, p, full, shards=None, c_arr=None):
    L, D = x.shape
    dist = shards is not None
    row = lambda a: a.reshape(1, -1)
    w_in = full["w_in"]
    pool_w_b = p["pool_w"].astype(BF16)
    g_mix, g_pool, g_ssm, g_ffn, g_fin = (row(p[k]) for k in (
        "norm_mix_g", "out_norm_pool_g", "out_norm_ssm_g", "norm_ffn_g", "final_norm_g"))
    pool_scale, ssm_d, glu_b, conv_b = (row(p[k]) for k in ("pool_scale", "ssm_d", "glu_b", "conv_b"))

    lnar = p["ssm_log_neg_a_re"].reshape(2 * N_SSM_GROUPS, SSM_STATE)
    aim = p["ssm_a_im"].reshape(2 * N_SSM_GROUPS, SSM_STATE)
    ldt = jnp.broadcast_to(p["ssm_log_dt"].reshape(2 * N_SSM_GROUPS, 1), lnar.shape)
    lam_re, lam_im, f_re, f_im = _ssm_params(lnar, aim, ldt)
    flat2 = lambda a: a.reshape(2, N_STATE)
    lam4 = jnp.stack([flat2(lam_re)[0], flat2(lam_im)[0], flat2(lam_re)[1], flat2(lam_im)[1]])
    tables = _scan_tables(lam4)
    per_group = (2, N_SSM_GROUPS, SSM_STATE)
    dense = _ssm_expand(p["ssm_b_re"], p["ssm_b_im"], p["ssm_c_re"], p["ssm_c_im"],
                        f_re.reshape(per_group + (1,)), f_im.reshape(per_group + (1,)))
    ssm_args = [tuple(dense[4 * d:4 * d + 4]) + (tables,) for d in range(2)]

    u, xn = _in_proj(x, g_mix, w_in)
    yn_pool = _pool_fwd(u, pool_w_b, pool_scale, g_pool)
    gather1 = _comm_gather_split([shards[k] for k in ("glu_w", "w_out", "w_down")], [shards["conv_w"]]) if dist else None
    (y0, s0r, s0i), got1 = _ssm_scan_fwd(u, *ssm_args[0], 0, False, comm=gather1)
    gather2 = _comm_gather_split([shards["w_up"]], []) if dist else None
    (y1, s1r, s1i), got2 = _ssm_scan_fwd(u, *ssm_args[1], 2, True, comm=gather2)
    if dist:
        glu_w, w_out, w_down = (g.reshape((-1,) + g.shape[2:]) for g in got1[:3])
        conv_w = jnp.transpose(got1[3], (1, 0, 2)).reshape(3, -1)
        w_up4 = got2[0]
    else:
        glu_w, w_out, w_up4, w_down, conv_w = (full[k] for k in ("glu_w", "w_out", "w_up", "w_down", "conv_w"))
    h1, hn, ycat = _mix_out(yn_pool, y0, y1, u, x, ssm_d, glu_w, glu_b, g_ssm, w_out, g_ffn)
    up = _ffn_up(hn, w_up4)
    a, c_val, c_gate, dh2, dh2_b, loss, g_final = _ffn_down_loss(up, conv_w, conv_b, w_down, h1, target, g_fin)

    d_val, d_gate, gbv, gbg = _ffn_act_bwd(c_val, c_gate, w_down, dh2_b)
    g_w_down = _matmul_tn(a, dh2_b, FF_BLK, D, "grad_w_down")
    d_up, dh1, dh1_b, g_ffn_g, gcw = _ffn_up_bwd(d_val, d_gate, up, conv_w, w_up4, h1, dh2, g_ffn)
    g_w_up = _matmul_tn_blocks(hn, d_up.reshape(4, L, FF_BLK), TM, "grad_w_up")
    g_w_out = _matmul_tn(ycat, dh1_b, TM, D, "grad_w_out")
    late = ("w_up", "w_down", "w_out", "glu_w")
    halves = [g_w_up.reshape(4, 2, D // 2, FF_BLK), g_w_down.reshape(4, 2, D_FF // 8, D)]
    (dy, du_direct, g_glu_w, g_glu_b, g_ssm_d, g_ssm_g), swapped = _ssm_bwd_local(
        dh1_b, y0, y1, u, ssm_d, glu_w, glu_b, g_ssm, w_out, comm=_comm_pair_swap(halves, half=True) if dist else None)
    more = [g_w_out.reshape(4, 2, D // 8, D), g_glu_w.reshape(4, 2, D_SSM // 8, D_SSM)]
    (d_pooled, g_pool_w, g_pool_scale, g_pool_g), swapped_more = _pool_bwd_local(
        dh1_b, u, w_out, pool_w_b, pool_scale, g_pool, comm=_comm_pair_swap(more, half=True) if dist else None)
    halves, from_sibling = halves + more, list(swapped) + list(swapped_more)
    du_pool = _pool_bwd_window(d_pooled)
    reduce_a, reduce_b = None, None
    if dist:
        chip_sums = [_add_half(h, r, c_arr, "sum_pair_" + k, BF16) for k, h, r in zip(late, halves, from_sibling)]
        reduce_a = _comm_chip_exchange(chip_sums[:1], scatter=True)
        reduce_b = _comm_chip_exchange(chip_sums[1:], scatter=True)
    (du0, gb0r, gb0i, gc0r, gc0i, gv0), chips_a = _ssm_scan_bwd(dy, u, s0r, s0i, *ssm_args[0], 1, True, comm=reduce_a)
    (du1, gb1r, gb1i, gc1r, gc1i, gv1), chips_b = _ssm_scan_bwd(dy, u, s1r, s1i, *ssm_args[1], 3, False, comm=reduce_b)
    mine = [_sum4(r, "sum_chips_" + k) for k, r in zip(late, list(chips_a) + list(chips_b))]
    by_state = (2, N_SSM_GROUPS, 1, SSM_STATE)
    g_b_re, g_b_im, g_f_re, g_f_im = _ssm_unfold(
        jnp.stack([gb0r, gb1r]), jnp.stack([gb0i, gb1i]),
        jnp.swapaxes(p["ssm_b_re"], 2, 3), jnp.swapaxes(p["ssm_b_im"], 2, 3),
        f_re.reshape(by_state), f_im.reshape(by_state))
    gvec = lambda j: jnp.stack([gv0[j], gv1[j]]).reshape(2 * N_SSM_GROUPS, SSM_STATE)
    g_lnar, g_aim, g_ldt = _ssm_params_bwd(lnar, aim, ldt, gvec(0), gvec(1),
                                           g_f_re.reshape(lnar.shape), g_f_im.reshape(lnar.shape))
    (grad_x, d_u_b, g_mix_g), theirs = _in_bwd(du_pool, du_direct, du0, du1, dh1, x, g_mix, w_in,
                                               comm=_comm_pair_swap(mine) if dist else None)
    g_w_in = _matmul_tn(xn, d_u_b, TM, D, "grad_w_in")

    small = {
        "norm_mix_g": g_mix_g, "pool_w": g_pool_w, "pool_scale": g_pool_scale,
        "ssm_log_neg_a_re": g_lnar, "ssm_a_im": g_aim, "ssm_log_dt": g_ldt,
        "ssm_b_re": jnp.swapaxes(g_b_re, 2, 3), "ssm_b_im": jnp.swapaxes(g_b_im, 2, 3),
        "ssm_c_re": jnp.stack([gc0r, gc1r]), "ssm_c_im": jnp.stack([gc0i, gc1i]),
        "ssm_d": g_ssm_d, "glu_b": g_glu_b, "out_norm_pool_g": g_pool_g, "out_norm_ssm_g": g_ssm_g,
        "norm_ffn_g": g_ffn_g, "conv_b": jnp.concatenate([gbv[0], gbg[0]]), "final_norm_g": g_final,
        "conv_w": jnp.transpose(gcw, (2, 0, 1, 3)).reshape(3, -1),
    }
    big = {"w_in": g_w_in}
    reduced = dict(zip(late, zip(mine, theirs)))
    if not dist:
        big.update({"w_up": g_w_up, "w_down": g_w_down, "w_out": g_w_out, "glu_w": g_glu_w})
    return loss, grad_x, small, big, reduced


def kernel(x, norm_mix_g, w_in, pool_w, pool_scale, ssm_log_neg_a_re, ssm_a_im, ssm_log_dt, ssm_b_re, ssm_b_im, ssm_c_re, ssm_c_im, ssm_d, glu_w, glu_b, out_norm_pool_g, out_norm_ssm_g, w_out, norm_ffn_g, w_up, conv_w, conv_b, w_down, final_norm_g, loss_target, m_norm_mix_g, m_w_in, m_pool_w, m_pool_scale, m_ssm_log_neg_a_re, m_ssm_a_im, m_ssm_log_dt, m_ssm_b_re, m_ssm_b_im, m_ssm_c_re, m_ssm_c_im, m_ssm_d, m_glu_w, m_glu_b, m_out_norm_pool_g, m_out_norm_ssm_g, m_w_out, m_norm_ffn_g, m_w_up, m_conv_w, m_conv_b, m_w_down, m_final_norm_g, v_norm_mix_g, v_w_in, v_pool_w, v_pool_scale, v_ssm_log_neg_a_re, v_ssm_a_im, v_ssm_log_dt, v_ssm_b_re, v_ssm_b_im, v_ssm_c_re, v_ssm_c_im, v_ssm_d, v_glu_w, v_glu_b, v_out_norm_pool_g, v_out_norm_ssm_g, v_w_out, v_norm_ffn_g, v_w_up, v_conv_w, v_conv_b, v_w_down, v_final_norm_g):
    args = locals()
    w = {k: args[k] for k in WEIGHTS}
    m = {k: args["m_" + k] for k in WEIGHTS}
    v = {k: args["v_" + k] for k in WEIGHTS}
    chip = 2 * lax.axis_index("x") + lax.axis_index("y")
    c_arr = lax.axis_index("c").astype(jnp.int32).reshape(1)

    shards = {k: w[k].astype(BF16) for k in BIG}
    shards["conv_w"] = conv_w
    w_in_full = _comm_call(_comm_gather_split([shards["w_in"]], []), "gather_w_in")[0]
    loss, grad_x, g_small, g_big, reduced = _local_step(
        x[0], loss_target[0], w, {"w_in": w_in_full.reshape(-1, w_in_full.shape[-1])}, shards, c_arr)

    exact = [k for k in SMALL if k not in WIDE]
    packs = [_pack([loss] + [g_small[k] for k in exact] + [g_small["conv_w"]], 512),
             _pack([g_small[k] for k in WIDE], 512)]
    halves = [g_big["w_in"].reshape(4, 2, g_big["w_in"].shape[0] // 8, -1)]
    halves += [pk.reshape(1, 2, pk.shape[0] // 2, LANES) for pk in packs]
    from_sibling = _comm_call(_comm_pair_swap(halves, half=True), "reduce_pair")
    names = ("w_in", "exact", "wide")
    sums = [_add_half(h, r, c_arr, "sum_pair_" + k, dt)
            for k, h, r, dt in zip(names, halves, from_sibling, (BF16, F32, BF16))]
    grads, delta, new_m, new_v = {}, {}, {}, {}

    def adamw_behind(k, comm):
        own, other = reduced[k]
        (grads[k], delta[k], new_m[k], new_v[k]), got = _adamw_halves(
            w[k], own, other, m[k], v[k], "adamw_" + k, comm=comm)
        return got

    from_chips = adamw_behind("w_up", _comm_join(_comm_chip_exchange(sums[:1], scatter=True),
                                                  _comm_chip_exchange([s[0] for s in sums[1:]], scatter=False)))
    mine = [_sum4(r, "sum_chips_" + k) for k, r in zip(names, from_chips)]
    theirs = adamw_behind("w_down", _comm_pair_swap(mine))
    for k in ("w_out", "glu_w"):
        adamw_behind(k, None)
    exact_all = _join_rows(mine[1], theirs[1], c_arr, "join_exact")
    wide_all = _join_rows(mine[2], theirs[2], c_arr, "join_wide")
    shapes = [loss.shape] + [w[k].shape for k in exact] + [(3, 4 * FF_BLK)]
    grads.update(zip(["loss"] + exact + ["conv_w_full"], _unpack(exact_all, shapes)))
    grads.update(zip(WIDE, _unpack(wide_all, [w[k].shape for k in WIDE])))
    loss = grads.pop("loss")[0, 0]
    grads["conv_w"] = lax.dynamic_slice_in_dim(grads.pop("conv_w_full"), chip * FF_BLK, FF_BLK, axis=1)

    reduced["w_in"] = (mine[0], theirs[0])
    adamw_behind("w_in", None)
    padded = ["ssm_b_re", "ssm_b_im"]
    for keys, name in ((padded, "adamw_ssm_b"), ([k for k in SMALL + ["conv_w"] if k not in padded], "adamw_small")):
        outs = _adamw_many(*([d[k] for k in keys] for d in (w, grads, m, v)), name)
        for d, o in zip((delta, new_m, new_v), outs):
            d.update(zip(keys, o))

    return (loss, grad_x[None], *[grads[k] for k in WEIGHTS], *[delta[k] for k in WEIGHTS],
            *[new_m[k] for k in WEIGHTS], *[new_v[k] for k in WEIGHTS])
```

```python
import numpy as np
import jax
import jax.numpy as jnp
from jax import lax
from jax.experimental import pallas as pl
from jax.experimental.pallas import tpu as pltpu

F32 = jnp.float32
BF16 = jnp.bfloat16
MESH = pl.DeviceIdType.MESH

EPS = 1e-6
POOL_WINDOWS = (2, 4, 8, 16)
POOL_GROUP = 128
SSM_GROUP = 16
SSM_STATE = 64
N_SSM_GROUPS = 32
N_STATE = N_SSM_GROUPS * SSM_STATE
QUAD = 256
N_QUAD = N_STATE // QUAD
SLAB = 256
D_SSM = 512
D_POOL = 512
D_FF = 2816
FF_BLK = 1408
UP_TOP = 256
HALO = 8
HALO_B = 16
LANES = 128
ADAM_LR, ADAM_B1, ADAM_B2, ADAM_EPS, ADAM_WD, ADAM_STEP = 0.001, 0.9, 0.999, 1e-08, 0.01, 10
VMEM_LIMIT = 56 * 2 ** 20
ADAMW_BLOCK_BYTES = 2 ** 20

TL = 512
TM = 1024
TF = 256
TC = 512
SEG = 8
SEG_LEN = TC // SEG
SCAN_UNROLL = 4
SCAN_W = 512


def _cp(*sem):
    return pltpu.CompilerParams(dimension_semantics=sem, vmem_limit_bytes=VMEM_LIMIT)


def _dot_nn(a, b):
    return jnp.dot(a, b, preferred_element_type=F32)


def _dot_nt(a, b):
    return lax.dot_general(a, b, (((1,), (1,)), ((), ())), preferred_element_type=F32)


def _dot_tn(a, b):
    return lax.dot_general(a, b, (((0,), (0,)), ((), ())), preferred_element_type=F32)


def _rms_fwd(x, g):
    inv = lax.rsqrt(jnp.mean(x * x, axis=-1, keepdims=True) + EPS)
    xh = x * inv
    return xh * g, xh, inv


def _rms_bwd(dy, xh, inv, g):
    dg = jnp.sum(dy * xh, axis=0, keepdims=True)
    dxh = dy * g
    dx = inv * (dxh - xh * jnp.mean(dxh * xh, axis=-1, keepdims=True))
    return dx, dg


_GELU_C = 0.7978845608028654
_GELU_A = 0.044715


def _gelu(y):
    t = jnp.tanh(_GELU_C * (y + _GELU_A * (y * y * y)))
    return 0.5 * y * (1.0 + t), t


def _gelu_grad(y, t):
    return 0.5 * (1.0 + t) + 0.5 * y * (1.0 - t * t) * (_GELU_C * (1.0 + 3.0 * _GELU_A * y * y))


def _sigmoid(x):
    return 1.0 / (1.0 + jnp.exp(-x))


def _full(shape):
    n = len(shape)
    return pl.BlockSpec(shape, lambda *_: (0,) * n)


def _fill_ext(ext_ref, prev_ref, cur_ref, next_ref, i, n, rows):
    ext_ref[0:HALO, :] = jnp.where(i > 0, prev_ref[...], 0.0).astype(ext_ref.dtype)
    ext_ref[HALO:HALO + rows, :] = cur_ref[...]
    ext_ref[HALO + rows:2 * HALO + rows, :] = jnp.where(i < n - 1, next_ref[...], 0.0).astype(ext_ref.dtype)


def _in_proj(x, g, w):
    L, D = x.shape
    E = w.shape[1]

    def body(x_ref, g_ref, w_ref, u_ref, xn_ref):
        y, _, _ = _rms_fwd(x_ref[...], g_ref[...])
        yb = y.astype(BF16)
        xn_ref[...] = yb
        u_ref[...] = _dot_nn(yb, w_ref[...])

    return pl.pallas_call(
        body, name="in_proj", grid=(L // TL,),
        in_specs=[pl.BlockSpec((TL, D), lambda i: (i, 0)), _full((1, D)), _full(w.shape)],
        out_specs=[pl.BlockSpec((TL, E), lambda i: (i, 0)), pl.BlockSpec((TL, D), lambda i: (i, 0))],
        out_shape=[jax.ShapeDtypeStruct((L, E), F32), jax.ShapeDtypeStruct((L, D), BF16)],
        compiler_params=_cp("parallel"))(x, g, w)


def _halo_specs_1d(rows, width, L, col):
    rb = rows // HALO
    last = L // HALO - 1
    return [pl.BlockSpec((HALO, width), lambda i: (jnp.maximum(i * rb - 1, 0), col)),
            pl.BlockSpec((rows, width), lambda i: (i, col)),
            pl.BlockSpec((HALO, width), lambda i: (jnp.minimum((i + 1) * rb, last), col))]


def _pooled_from_ext(ext_ref, t0, rows, L):
    t = t0 + lax.broadcasted_iota(jnp.int32, (rows, 1), 0)
    outs = []
    for gi, w in enumerate(POOL_WINDOWS):
        half = w // 2
        cs = slice(gi * POOL_GROUP, (gi + 1) * POOL_GROUP)
        acc = ext_ref[pl.ds(HALO - half, rows), cs]
        for s in range(-half + 1, half):
            acc = acc + ext_ref[pl.ds(HALO + s, rows), cs]
        cnt = (jnp.minimum(t + half, L) - jnp.maximum(t - half, 0)).astype(F32)
        outs.append(acc / cnt - ext_ref[pl.ds(HALO, rows), cs])
    return outs


def _pool_fwd(u, pool_w_b, pool_scale, g_pool):
    L = u.shape[0]
    n = L // TL

    def body(prev_ref, cur_ref, next_ref, pw_ref, ps_ref, g_ref, out_ref, ext_ref):
        i = pl.program_id(0)
        _fill_ext(ext_ref, prev_ref, cur_ref, next_ref, i, n, TL)
        pooled = _pooled_from_ext(ext_ref, i * TL, TL, L)
        ypre = jnp.concatenate([_dot_nn(pooled[gi].astype(BF16), pw_ref[gi]) for gi in range(4)], axis=-1)
        yn, _, _ = _rms_fwd(ypre * ps_ref[...], g_ref[...])
        out_ref[...] = yn.astype(BF16)

    return pl.pallas_call(
        body, name="pool_fwd", grid=(n,),
        in_specs=_halo_specs_1d(TL, D_POOL, L, 0) + [_full(pool_w_b.shape), _full((1, D_POOL)), _full((1, D_POOL))],
        out_specs=pl.BlockSpec((TL, D_POOL), lambda i: (i, 0)),
        out_shape=jax.ShapeDtypeStruct((L, D_POOL), BF16),
        scratch_shapes=[pltpu.VMEM((TL + 2 * HALO, D_POOL), F32)],
        compiler_params=_cp("parallel"))(u, u, u, pool_w_b, pool_scale, g_pool)


def _pool_bwd_local(dh1, u, w_out_b, pool_w_b, pool_scale, g_pool, comm=None):
    L = u.shape[0]
    n = L // TL
    D = dh1.shape[1]

    def body(dh_ref, prev_ref, cur_ref, next_ref, wo_ref, pw_ref, ps_ref, g_ref,
             dp_ref, gpw_ref, gps_ref, gg_ref, ext_ref):
        i = pl.program_id(0)

        @pl.when(i == 0)
        def _():
            gpw_ref[...] = jnp.zeros_like(gpw_ref)
            gps_ref[...] = jnp.zeros_like(gps_ref)
            gg_ref[...] = jnp.zeros_like(gg_ref)

        _fill_ext(ext_ref, prev_ref, cur_ref, next_ref, i, n, TL)
        pooled = [p.astype(BF16) for p in _pooled_from_ext(ext_ref, i * TL, TL, L)]
        ypre = jnp.concatenate([_dot_nn(pooled[gi], pw_ref[gi]) for gi in range(4)], axis=-1)
        ps = ps_ref[...]
        g = g_ref[...]
        _, xh, inv = _rms_fwd(ypre * ps, g)
        d_yn = _dot_nt(dh_ref[...], wo_ref[...])
        d_y, dg = _rms_bwd(d_yn, xh, inv, g)
        gg_ref[...] += dg
        gps_ref[...] += jnp.sum(d_y * ypre, axis=0, keepdims=True)
        d_ypre = (d_y * ps).astype(BF16)
        for gi in range(4):
            cs = slice(gi * POOL_GROUP, (gi + 1) * POOL_GROUP)
            dp_ref[:, cs] = _dot_nt(d_ypre[:, cs], pw_ref[gi])
            gpw_ref[gi] += _dot_tn(pooled[gi], d_ypre[:, cs])

    return _hosted_call(
        body, comm, name="pool_bwd_local", grid=(n,),
        in_specs=[pl.BlockSpec((TL, D), lambda i: (i, 0))] + _halo_specs_1d(TL, D_POOL, L, 0)
        + [pl.BlockSpec((D_POOL, D), lambda i: (0, 0)), _full(pool_w_b.shape), _full((1, D_POOL)), _full((1, D_POOL))],
        out_specs=[pl.BlockSpec((TL, D_POOL), lambda i: (i, 0)), _full(pool_w_b.shape),
                   _full((1, D_POOL)), _full((1, D_POOL))],
        out_shape=[jax.ShapeDtypeStruct((L, D_POOL), F32), jax.ShapeDtypeStruct(pool_w_b.shape, F32),
                   jax.ShapeDtypeStruct((1, D_POOL), F32), jax.ShapeDtypeStruct((1, D_POOL), F32)],
        scratch_shapes=[pltpu.VMEM((TL + 2 * HALO, D_POOL), F32)],
        args=(dh1, u, u, u, w_out_b, pool_w_b, pool_scale, g_pool))


def _pool_bwd_window(d_pooled):
    L = d_pooled.shape[0]
    n = L // TL
    R = TL + 2 * HALO

    def body(prev_ref, cur_ref, next_ref, out_ref, ext_ref, q_ref):
        i = pl.program_id(0)
        _fill_ext(ext_ref, prev_ref, cur_ref, next_ref, i, n, TL)
        tr = i * TL - HALO + lax.broadcasted_iota(jnp.int32, (R, 1), 0)
        for gi, w in enumerate(POOL_WINDOWS):
            half = w // 2
            cs = slice(gi * POOL_GROUP, (gi + 1) * POOL_GROUP)
            cnt = jnp.maximum(jnp.minimum(tr + half, L) - jnp.maximum(tr - half, 0), 1).astype(F32)
            q_ref[:, cs] = ext_ref[:, cs] / cnt
        for gi, w in enumerate(POOL_WINDOWS):
            half = w // 2
            cs = slice(gi * POOL_GROUP, (gi + 1) * POOL_GROUP)
            acc = q_ref[pl.ds(HALO - half + 1, TL), cs]
            for s in range(-half + 2, half + 1):
                acc = acc + q_ref[pl.ds(HALO + s, TL), cs]
            out_ref[:, cs] = acc - ext_ref[pl.ds(HALO, TL), cs]

    return pl.pallas_call(
        body, name="pool_bwd_window", grid=(n,),
        in_specs=_halo_specs_1d(TL, D_POOL, L, 0),
        out_specs=pl.BlockSpec((TL, D_POOL), lambda i: (i, 0)),
        out_shape=jax.ShapeDtypeStruct((L, D_POOL), F32),
        scratch_shapes=[pltpu.VMEM((R, D_POOL), F32), pltpu.VMEM((R, D_POOL), F32)],
        compiler_params=_cp("parallel"))(d_pooled, d_pooled, d_pooled)


def _ssm_param_fn(lnar, aim, ldt):
    dt = jnp.exp(ldt)
    a_re = -jnp.exp(lnar)
    mag = jnp.exp(a_re * dt)
    ang = aim * dt
    lr, li = mag * jnp.cos(ang), mag * jnp.sin(ang)
    den = a_re * a_re + aim * aim
    fr = ((lr - 1.0) * a_re + li * aim) / den
    fi = (li * a_re - (lr - 1.0) * aim) / den
    return lr, li, fr, fi


def _ssm_params(lnar, aim, ldt):
    def body(a_ref, b_ref, c_ref, lr_ref, li_ref, fr_ref, fi_ref):
        lr, li, fr, fi = _ssm_param_fn(a_ref[...], b_ref[...], c_ref[...])
        lr_ref[...] = lr
        li_ref[...] = li
        fr_ref[...] = fr
        fi_ref[...] = fi

    sh = jax.ShapeDtypeStruct(lnar.shape, F32)
    return pl.pallas_call(body, name="ssm_params", out_shape=[sh] * 4)(lnar, aim, ldt)


def _ssm_params_bwd(lnar, aim, ldt, glr, gli, gfr, gfi):
    def body(a_ref, b_ref, c_ref, g0, g1, g2, g3, da_ref, db_ref, dc_ref):
        _, vjp = jax.vjp(_ssm_param_fn, a_ref[...], b_ref[...], c_ref[...])
        da, db, dc = vjp((g0[...], g1[...], g2[...], g3[...]))
        da_ref[...] = da
        db_ref[...] = db
        dc_ref[...] = jnp.sum(dc, axis=1, keepdims=True)

    return pl.pallas_call(
        body, name="ssm_params_bwd",
        out_shape=[jax.ShapeDtypeStruct(lnar.shape, F32), jax.ShapeDtypeStruct(aim.shape, F32),
                   jax.ShapeDtypeStruct((ldt.shape[0], 1), F32)])(lnar, aim, ldt, glr, gli, gfr, gfi)


def _scan_tables(lam4):
    def build(lr, li, reverse, out_ref, k):
        pr, pi = lr, li
        for d in range(SEG_LEN):
            j = SEG_LEN - 1 - d if reverse else d
            out_ref[k, 0, j:j + 1, :] = pr
            out_ref[k, 1, j:j + 1, :] = pi
            pr, pi = pr * lr - pi * li, pr * li + pi * lr

    def body(lam_ref, out_ref):
        l0r, l0i, l1r, l1i = (lam_ref[j:j + 1, :] for j in range(4))
        build(l0r, l0i, False, out_ref, 0)
        build(l0r, -l0i, True, out_ref, 1)
        build(l1r, l1i, True, out_ref, 2)
        build(l1r, -l1i, False, out_ref, 3)

    return pl.pallas_call(body, name="scan_tables",
                          out_shape=jax.ShapeDtypeStruct((4, 2, SEG_LEN, N_STATE), F32))(lam4)


def _b_block(g):
    q, gl = divmod(g, 4)
    r0, c0 = gl * SSM_STATE, (q % 4) * 4 * SSM_GROUP + gl * SSM_GROUP
    return q, slice(r0, r0 + SSM_STATE), slice(c0, c0 + SSM_GROUP)


def _c_block(g):
    q, rows, cols = _b_block(g)
    return q, cols, rows


def _ssm_expand(b_re, b_im, c_re, c_im, f_re, f_im):
    def body(bre_ref, bim_ref, cre_ref, cim_ref, fre_ref, fim_ref, *rest):
        outs, tmp, bbr_ref, bbi_ref = rest[:8], rest[8], rest[9], rest[10]
        fr, fi, br, bi = fre_ref[...], fim_ref[...], bre_ref[...], bim_ref[...]
        bbr_ref[...] = fr * br - fi * bi
        bbi_ref[...] = fr * bi + fi * br
        for d in range(2):
            for j, (src, where) in enumerate(((bbr_ref, _b_block), (bbi_ref, _b_block),
                                              (cre_ref, _c_block), (cim_ref, _c_block))):
                tmp[...] = jnp.zeros_like(tmp)
                for g in range(N_SSM_GROUPS):
                    q, rows, cols = where(g)
                    tmp[q, rows, cols] = src[d, g]
                outs[4 * d + j][...] = tmp[...].astype(BF16)

    dense = jax.ShapeDtypeStruct((N_QUAD, QUAD, SLAB), BF16)
    return pl.pallas_call(body, name="ssm_expand", out_shape=[dense] * 8,
                          scratch_shapes=[pltpu.VMEM((N_QUAD, QUAD, SLAB), F32), pltpu.VMEM(b_re.shape, F32),
                                          pltpu.VMEM(b_re.shape, F32)],
                          compiler_params=pltpu.CompilerParams(vmem_limit_bytes=VMEM_LIMIT))(
                              b_re, b_im, c_re, c_im, f_re, f_im)


def _ssm_unfold(gbb_re, gbb_im, b_re_t, b_im_t, f_re, f_im):
    def body(gr_ref, gi_ref, br_ref, bi_ref, fr_ref, fi_ref, obr_ref, obi_ref, ofr_ref, ofi_ref):
        gr, gi, br, bi, fr, fi = (r[...] for r in (gr_ref, gi_ref, br_ref, bi_ref, fr_ref, fi_ref))
        obr_ref[...] = fr * gr + fi * gi
        obi_ref[...] = fr * gi - fi * gr
        ofr_ref[...] = jnp.sum(br * gr + bi * gi, axis=2, keepdims=True)
        ofi_ref[...] = jnp.sum(br * gi - bi * gr, axis=2, keepdims=True)

    gb = jax.ShapeDtypeStruct(gbb_re.shape, F32)
    gf = jax.ShapeDtypeStruct(f_re.shape, F32)
    return pl.pallas_call(body, name="ssm_unfold", out_shape=[gb, gb, gf, gf])(
        gbb_re, gbb_im, b_re_t, b_im_t, f_re, f_im)


_SEGMENT_ORDER = np.zeros((TC, TC), np.float32)
for _p in range(TC):
    _SEGMENT_ORDER[_p, (_p % SEG) * SEG_LEN + _p // SEG] = 1.0


def _store_tokens(ref, col0, val, tmp_ref):
    for h in range(val.shape[1] // LANES):
        for j in range(SEG_LEN):
            tmp_ref[pl.ds(h * TC + j, SEG, stride=SEG_LEN), :] = val[SEG * j:SEG * (j + 1), h * LANES:(h + 1) * LANES]
        ref[:, col0 + h * LANES:col0 + (h + 1) * LANES] = tmp_ref[pl.ds(h * TC, TC), :]


def _segment_scan(src_re, src_im, dst_re, dst_im, tab_ref, k, carry_re, carry_im, reverse, s_refs=None):
    lam1, lam_seg = (SEG_LEN - 1, 0) if reverse else (0, SEG_LEN - 1)
    token = (lambda i: SEG_LEN - 1 - i) if reverse else (lambda i: i)
    row_id = lax.broadcasted_iota(jnp.int32, (SEG, SCAN_W), 0)
    zero = jnp.zeros((SEG, SCAN_W), F32)
    sums = []
    for lt in range(N_STATE // SCAN_W):
        sl = slice(lt * SCAN_W, (lt + 1) * SCAN_W)
        lr = jnp.broadcast_to(tab_ref[k, 0, lam1:lam1 + 1, sl], (SEG, SCAN_W))
        li = jnp.broadcast_to(tab_ref[k, 1, lam1:lam1 + 1, sl], (SEG, SCAN_W))

        def local(i, c, sl=sl, lr=lr, li=li):
            for step in range(SCAN_UNROLL):
                rows = pl.ds(pl.multiple_of(token(i * SCAN_UNROLL + step) * SEG, SEG), SEG)
                c = (lr * c[0] - li * c[1] + src_re[rows, sl], lr * c[1] + li * c[0] + src_im[rows, sl])
                dst_re[rows, sl] = c[0]
                dst_im[rows, sl] = c[1]
            return c

        er, ei = lax.fori_loop(0, SEG_LEN // SCAN_UNROLL, local, (zero, zero))

        sr_, si_ = tab_ref[k, 0, lam_seg:lam_seg + 1, sl], tab_ref[k, 1, lam_seg:lam_seg + 1, sl]
        c_r, c_i = carry_re[0:1, sl], carry_im[0:1, sl]
        in_r, in_i = zero, zero
        for r in (range(SEG - 1, -1, -1) if reverse else range(SEG)):
            in_r = jnp.where(row_id == r, c_r, in_r)
            in_i = jnp.where(row_id == r, c_i, in_i)
            c_r, c_i = (er[r:r + 1, :] + sr_ * c_r - si_ * c_i, ei[r:r + 1, :] + sr_ * c_i + si_ * c_r)
        carry_re[0:1, sl] = c_r
        carry_im[0:1, sl] = c_i

        def fix(i, c, sl=sl, in_r=in_r, in_i=in_i):
            for step in range(SCAN_UNROLL):
                j = token(i * SCAN_UNROLL + step)
                rows = pl.ds(pl.multiple_of(j * SEG, SEG), SEG)
                pr = jnp.broadcast_to(tab_ref[k, 0, pl.ds(j, 1), sl], (SEG, SCAN_W))
                pi = jnp.broadcast_to(tab_ref[k, 1, pl.ds(j, 1), sl], (SEG, SCAN_W))
                nr = dst_re[rows, sl] + pr * in_r - pi * in_i
                ni = dst_im[rows, sl] + pr * in_i + pi * in_r
                dst_re[rows, sl] = nr
                dst_im[rows, sl] = ni
                if s_refs is not None:
                    sr = s_refs[0][rows, sl]
                    si = s_refs[1][rows, sl]
                    c = (nr, ni, c[2] + c[0] * sr + c[1] * si, c[3] + c[1] * sr - c[0] * si)
            return c

        if s_refs is None:
            lax.fori_loop(0, SEG_LEN // SCAN_UNROLL, fix, 0)
        else:
            out = lax.fori_loop(0, SEG_LEN // SCAN_UNROLL, fix, (in_r, in_i, zero, zero))
            sums.append((jnp.sum(out[2], axis=0, keepdims=True), jnp.sum(out[3], axis=0, keepdims=True)))
    return sums


def _ssm_scan_fwd(u, b_re, b_im, c_re, c_im, tables, k, reverse, comm=None):
    L = u.shape[0]
    nc = L // TC
    chunk = (lambda i: nc - 1 - i) if reverse else (lambda i: i)
    order = jnp.asarray(_SEGMENT_ORDER, BF16)

    def body(u_ref, ord_ref, bre_ref, bim_ref, cre_ref, cim_ref, tab_ref,
             y_ref, sre_ref, sim_ref, in_re, in_im, carry_re, carry_im, tmp_ref):
        @pl.when(pl.program_id(0) == 0)
        def _():
            carry_re[...] = jnp.zeros_like(carry_re)
            carry_im[...] = jnp.zeros_like(carry_im)

        ub = _dot_nn(ord_ref[...], u_ref[...].astype(BF16)).astype(BF16)
        for q in range(N_QUAD):
            qs = slice(q * QUAD, (q + 1) * QUAD)
            us = ub[:, (q // 4) * SLAB:(q // 4 + 1) * SLAB]
            in_re[:, qs] = _dot_nt(us, bre_ref[q])
            in_im[:, qs] = _dot_nt(us, bim_ref[q])
        _segment_scan(in_re, in_im, sre_ref, sim_ref, tab_ref, k, carry_re, carry_im, reverse)
        for j in range(D_SSM // SLAB):
            acc = jnp.zeros((TC, SLAB), F32)
            for q in range(4 * j, 4 * j + 4):
                qs = slice(q * QUAD, (q + 1) * QUAD)
                acc = acc + _dot_nt(sre_ref[:, qs].astype(BF16), cre_ref[q])
                acc = acc - _dot_nt(sim_ref[:, qs].astype(BF16), cim_ref[q])
            _store_tokens(y_ref, j * SLAB, acc, tmp_ref)

    return _hosted_call(
        body, comm, name="ssm_scan_rev" if reverse else "ssm_scan_fwd", grid=(nc,),
        in_specs=[pl.BlockSpec((TC, D_SSM), lambda i: (chunk(i), 1)), _full(order.shape)]
        + [_full(b_re.shape)] * 4 + [_full(tables.shape)],
        out_specs=[pl.BlockSpec((TC, D_SSM), lambda i: (chunk(i), 0)),
                   pl.BlockSpec((TC, N_STATE), lambda i: (chunk(i), 0)),
                   pl.BlockSpec((TC, N_STATE), lambda i: (chunk(i), 0))],
        out_shape=[jax.ShapeDtypeStruct((L, D_SSM), F32), jax.ShapeDtypeStruct((L, N_STATE), F32),
                   jax.ShapeDtypeStruct((L, N_STATE), F32)],
        scratch_shapes=[pltpu.VMEM((TC, N_STATE), F32), pltpu.VMEM((TC, N_STATE), F32),
                        pltpu.VMEM((8, N_STATE), F32), pltpu.VMEM((8, N_STATE), F32),
                        pltpu.VMEM((SLAB // LANES * TC, LANES), F32)],
        args=(u, order, b_re, b_im, c_re, c_im, tables))


def _quad_channels(q):
    c0 = (q // 4) * SLAB + (q % 4) * 4 * SSM_GROUP
    return slice(c0, c0 + 4 * SSM_GROUP)


def _ssm_scan_bwd(dy, u, s_re, s_im, b_re, b_im, c_re, c_im, tables, k, reverse, comm=None):
    L = u.shape[0]
    nc = L // TC
    chunk = (lambda i: nc - 1 - i) if reverse else (lambda i: i)

    order = jnp.asarray(_SEGMENT_ORDER, BF16)

    def body(dy_ref, u_ref, ord_ref, sre_ref, sim_ref, bre_ref, bim_ref, cre_ref, cim_ref, tab_ref,
             du_ref, ob_re, ob_im, oc_re, oc_im, gv_ref,
             a_re, a_im, carry_re, carry_im, gbr_ref, gbi_ref, gcr_ref, gci_ref, tmp_ref):
        @pl.when(pl.program_id(0) == 0)
        def _():
            carry_re[...] = jnp.zeros_like(carry_re)
            carry_im[...] = jnp.zeros_like(carry_im)
            for r in (gbr_ref, gbi_ref, gcr_ref, gci_ref, gv_ref):
                r[...] = jnp.zeros_like(r)

        dyb = _dot_nn(ord_ref[...], dy_ref[...].astype(BF16)).astype(BF16)
        ub = _dot_nn(ord_ref[...], u_ref[...].astype(BF16)).astype(BF16)
        for q in range(N_QUAD):
            qs = slice(q * QUAD, (q + 1) * QUAD)
            ds = dyb[:, (q // 4) * SLAB:(q // 4 + 1) * SLAB]
            a_re[:, qs] = _dot_nn(ds, cre_ref[q])
            a_im[:, qs] = -_dot_nn(ds, cim_ref[q])
            dq = dyb[:, _quad_channels(q)]
            gcr_ref[q] += _dot_tn(dq, sre_ref[:, qs].astype(BF16))
            gci_ref[q] -= _dot_tn(dq, sim_ref[:, qs].astype(BF16))
        sums = _segment_scan(a_re, a_im, a_re, a_im, tab_ref, k, carry_re, carry_im, reverse,
                             s_refs=(sre_ref, sim_ref))
        for lt, (glr, gli) in enumerate(sums):
            sl = slice(lt * SCAN_W, (lt + 1) * SCAN_W)
            gv_ref[0:1, sl] += glr
            gv_ref[1:2, sl] += gli
        for j in range(D_SSM // SLAB):
            us = ub[:, j * SLAB:(j + 1) * SLAB]
            acc = jnp.zeros((TC, SLAB), F32)
            for q in range(4 * j, 4 * j + 4):
                qs = slice(q * QUAD, (q + 1) * QUAD)
                dbr = a_re[:, qs].astype(BF16)
                dbi = a_im[:, qs].astype(BF16)
                uq = ub[:, _quad_channels(q)]
                gbr_ref[q] += _dot_tn(uq, dbr)
                gbi_ref[q] += _dot_tn(uq, dbi)
                acc = acc + _dot_nn(dbr, bre_ref[q]) + _dot_nn(dbi, bim_ref[q])
            _store_tokens(du_ref, j * SLAB, acc, tmp_ref)

        @pl.when(pl.program_id(0) == nc - 1)
        def _():
            for g in range(N_SSM_GROUPS):
                q, gl = divmod(g, 4)
                rows = slice(gl * SSM_GROUP, (gl + 1) * SSM_GROUP)
                cols = slice(gl * SSM_STATE, (gl + 1) * SSM_STATE)
                for out, acc_ref in ((ob_re, gbr_ref), (ob_im, gbi_ref), (oc_re, gcr_ref), (oc_im, gci_ref)):
                    out[g] = acc_ref[q, rows, cols]

    gshape = jax.ShapeDtypeStruct((N_SSM_GROUPS, SSM_GROUP, SSM_STATE), F32)
    compact = pltpu.VMEM((N_QUAD, 4 * SSM_GROUP, QUAD), F32)
    return _hosted_call(
        body, comm, name="ssm_bwd_rev" if reverse else "ssm_bwd_fwd", grid=(nc,),
        in_specs=[pl.BlockSpec((TC, D_SSM), lambda i: (chunk(i), 0)),
                  pl.BlockSpec((TC, D_SSM), lambda i: (chunk(i), 1)), _full(order.shape),
                  pl.BlockSpec((TC, N_STATE), lambda i: (chunk(i), 0)),
                  pl.BlockSpec((TC, N_STATE), lambda i: (chunk(i), 0))]
        + [_full(b_re.shape)] * 4 + [_full(tables.shape)],
        out_specs=[pl.BlockSpec((TC, D_SSM), lambda i: (chunk(i), 0))] + [_full(gshape.shape)] * 4
        + [_full((2, N_STATE))],
        out_shape=[jax.ShapeDtypeStruct((L, D_SSM), F32), gshape, gshape, gshape, gshape,
                   jax.ShapeDtypeStruct((2, N_STATE), F32)],
        scratch_shapes=[pltpu.VMEM((TC, N_STATE), F32), pltpu.VMEM((TC, N_STATE), F32),
                        pltpu.VMEM((8, N_STATE), F32), pltpu.VMEM((8, N_STATE), F32),
                        compact, compact, compact, compact, pltpu.VMEM((SLAB // LANES * TC, LANES), F32)],
        args=(dy, u, order, s_re, s_im, b_re, b_im, c_re, c_im, tables))


def _ssm_post(yf, yb, u, d, glu_w, glu_b):
    y = yf + yb + d * u
    z, t = _gelu(y)
    zb = z.astype(BF16)
    gate = _sigmoid(_dot_nn(zb, glu_w) + glu_b)
    return y, z, t, zb, gate


def _mix_out(yn_pool, yf, yb, u, x, ssm_d, glu_w_b, glu_b, g_ssm, w_out_b, g_ffn):
    L, D = x.shape

    def body(ynp_ref, yf_ref, yb_ref, u_ref, x_ref, d_ref, gw_ref, gb_ref, gs_ref, wo_ref, gf_ref,
             h1_ref, hn_ref, ycat_ref):
        _, z, _, _, gate = _ssm_post(yf_ref[...], yb_ref[...], u_ref[...], d_ref[...], gw_ref[...], gb_ref[...])
        yns, _, _ = _rms_fwd(z * gate, gs_ref[...])
        ynsb = yns.astype(BF16)
        ynp = ynp_ref[...]
        ycat_ref[:, 0:D_POOL] = ynp
        ycat_ref[:, D_POOL:D] = ynsb
        h1 = x_ref[...] + _dot_nn(ynp, wo_ref[0:D_POOL, :]) + _dot_nn(ynsb, wo_ref[D_POOL:D, :])
        h1_ref[...] = h1
        hn, _, _ = _rms_fwd(h1, gf_ref[...])
        hn_ref[...] = hn.astype(BF16)

    half = lambda c: pl.BlockSpec((TL, D_SSM), lambda i: (i, c))
    row = pl.BlockSpec((TL, D), lambda i: (i, 0))
    return pl.pallas_call(
        body, name="mix_out", grid=(L // TL,),
        in_specs=[half(0), half(0), half(0), half(1), row, _full((1, D_SSM)), _full(glu_w_b.shape),
                  _full((1, D_SSM)), _full((1, D_SSM)), _full(w_out_b.shape), _full((1, D))],
        out_specs=[row, row, row],
        out_shape=[jax.ShapeDtypeStruct((L, D), F32), jax.ShapeDtypeStruct((L, D), BF16),
                   jax.ShapeDtypeStruct((L, D), BF16)],
        compiler_params=_cp("parallel"))(yn_pool, yf, yb, u, x, ssm_d, glu_w_b, glu_b, g_ssm, w_out_b, g_ffn)


def _ssm_bwd_local(dh1, yf, yb, u, ssm_d, glu_w_b, glu_b, g_ssm, w_out_b, comm=None):
    L, D = dh1.shape

    def body(dh_ref, yf_ref, yb_ref, u_ref, d_ref, gw_ref, gb_ref, gs_ref, wo_ref,
             dy_ref, du_ref, ggw_ref, ggb_ref, gd_ref, ggs_ref):
        @pl.when(pl.program_id(0) == 0)
        def _():
            for r in (ggw_ref, ggb_ref, gd_ref, ggs_ref):
                r[...] = jnp.zeros_like(r)

        u = u_ref[...]
        d = d_ref[...]
        y, z, t, zb, gate = _ssm_post(yf_ref[...], yb_ref[...], u, d, gw_ref[...], gb_ref[...])
        gs = gs_ref[...]
        _, xh, inv = _rms_fwd(z * gate, gs)
        d_yn = _dot_nt(dh_ref[...], wo_ref[...])
        d_o, dgs = _rms_bwd(d_yn, xh, inv, gs)
        ggs_ref[...] += dgs
        d_zg = d_o * z * gate * (1.0 - gate)
        d_zgb = d_zg.astype(BF16)
        ggb_ref[...] += jnp.sum(d_zg, axis=0, keepdims=True)
        ggw_ref[...] += _dot_tn(zb, d_zgb)
        d_z = d_o * gate + _dot_nt(d_zgb, gw_ref[...])
        d_y = d_z * _gelu_grad(y, t)
        gd_ref[...] += jnp.sum(d_y * u, axis=0, keepdims=True)
        dy_ref[...] = d_y
        du_ref[...] = d_y * d

    half = lambda c: pl.BlockSpec((TL, D_SSM), lambda i: (i, c))
    vec = _full((1, D_SSM))
    return _hosted_call(
        body, comm, name="ssm_bwd_local", grid=(L // TL,),
        in_specs=[pl.BlockSpec((TL, D), lambda i: (i, 0)), half(0), half(0), half(1), vec, _full(glu_w_b.shape),
                  vec, vec, pl.BlockSpec((D_SSM, D), lambda i: (1, 0))],
        out_specs=[half(0), half(0), _full(glu_w_b.shape), vec, vec, vec],
        out_shape=[jax.ShapeDtypeStruct((L, D_SSM), F32), jax.ShapeDtypeStruct((L, D_SSM), F32),
                   jax.ShapeDtypeStruct(glu_w_b.shape, F32)] + [jax.ShapeDtypeStruct((1, D_SSM), F32)] * 3,
        scratch_shapes=[], args=(dh1, yf, yb, u, ssm_d, glu_w_b, glu_b, g_ssm, w_out_b))


def _in_bwd(du_pool, du_a, du_b, du_c, dh1, x, g, w_in_b, comm=None):
    L, D = x.shape

    def body(p_ref, a_ref, b_ref, c_ref, dh_ref, x_ref, g_ref, w_ref, dx_ref, dub_ref, gg_ref):
        @pl.when(pl.program_id(0) == 0)
        def _():
            gg_ref[...] = jnp.zeros_like(gg_ref)

        dub_ref[:, 0:D_POOL] = p_ref[...].astype(BF16)
        dub_ref[:, D_POOL:D] = (a_ref[...] + b_ref[...] + c_ref[...]).astype(BF16)
        d_xn = _dot_nt(dub_ref[...], w_ref[...])
        gv = g_ref[...]
        _, xh, inv = _rms_fwd(x_ref[...], gv)
        dx, dg = _rms_bwd(d_xn, xh, inv, gv)
        gg_ref[...] += dg
        dx_ref[...] = dh_ref[...] + dx

    half = pl.BlockSpec((TL, D_SSM), lambda i: (i, 0))
    row = pl.BlockSpec((TL, D), lambda i: (i, 0))
    return _hosted_call(
        body, comm, name="in_bwd", grid=(L // TL,),
        in_specs=[half, half, half, half, row, row, _full((1, D)), _full(w_in_b.shape)],
        out_specs=[row, row, _full((1, D))],
        out_shape=[jax.ShapeDtypeStruct((L, D), F32), jax.ShapeDtypeStruct((L, D), BF16),
                   jax.ShapeDtypeStruct((1, D), F32)],
        scratch_shapes=[], args=(du_pool, du_a, du_b, du_c, dh1, x, g, w_in_b))


def _ffn_up(hn, w_top, w_rest):
    L, D = hn.shape

    def body(h_ref, wt_ref, wr_ref, o_ref):
        o_ref[...] = (_dot_nn(h_ref[:, 0:UP_TOP], wt_ref[...]) + _dot_nn(h_ref[:, UP_TOP:D], wr_ref[...])).astype(BF16)

    rows = min(TM, L)
    part = lambda w: pl.BlockSpec((None, w.shape[1], FF_BLK), lambda j, i: (j, 0, 0))
    return pl.pallas_call(
        body, name="ffn_up", grid=(4, L // rows),
        in_specs=[pl.BlockSpec((rows, D), lambda j, i: (i, 0)), part(w_top), part(w_rest)],
        out_specs=pl.BlockSpec((rows, FF_BLK), lambda j, i: (i, j)),
        out_shape=jax.ShapeDtypeStruct((L, 4 * FF_BLK), BF16),
        compiler_params=_cp("parallel", "parallel"))(hn, w_top, w_rest)


def _halo_specs_2d(rows, width, L, col, order):
    rb = rows // HALO_B
    last = L // HALO_B - 1
    if order == "ik":
        wrap = lambda f: (lambda i, k: f(i, k))
    else:
        wrap = lambda f: (lambda k, i: f(i, k))
    return [pl.BlockSpec((HALO_B, width), wrap(lambda i, k: (jnp.maximum(i * rb - 1, 0), col(k)))),
            pl.BlockSpec((rows, width), wrap(lambda i, k: (i, col(k)))),
            pl.BlockSpec((HALO_B, width), wrap(lambda i, k: (jnp.minimum((i + 1) * rb, last), col(k))))]


def _shift_mats(rows):
    r = lax.broadcasted_iota(jnp.int32, (rows, rows), 0)
    c = lax.broadcasted_iota(jnp.int32, (rows, rows), 1)
    return (c == r - 1).astype(BF16), (c == r + 1).astype(BF16)


def _neighbours(x, prev_ref, next_ref, cs, i, n, mats):
    rows = x.shape[0]
    row = lax.broadcasted_iota(jnp.int32, (rows, 1), 0)
    before = jnp.where(i > 0, prev_ref[:, cs].astype(F32)[HALO_B - 1:HALO_B, :], 0.0)
    after = jnp.where(i < n - 1, next_ref[:, cs].astype(F32)[0:1, :], 0.0)
    if mats is None:
        xf = x.astype(F32)
        down, up = pltpu.roll(xf, 1, 0), pltpu.roll(xf, rows - 1, 0)
    else:
        down, up = _dot_nn(mats[0], x), _dot_nn(mats[1], x)
    return jnp.where(row == 0, before, down), jnp.where(row == rows - 1, after, up)


def _conv3(x, before, after, w, b):
    return before * w[0:1, :] + x.astype(F32) * w[1:2, :] + after * w[2:3, :] + b


def _col_chunks(width, size=256):
    return [slice(c, min(c + size, width)) for c in range(0, width, size)]


def _ffn_down_loss(up, conv_w, conv_b, w_down_b, h1, target, g_final):
    L, D = h1.shape
    n = L // TF
    nk = D_FF // FF_BLK

    def body(vp, vc, vn, gp, gc, gn, wv_ref, wg_ref, bv_ref, bg_ref, wd_ref, h1_ref, t_ref, gf_ref,
             a_ref, cv_ref, cg_ref, dh2_ref, dh2b_ref, loss_ref, gg_ref, acc_ref):
        i = pl.program_id(0)
        k = pl.program_id(1)

        @pl.when((i == 0) & (k == 0))
        def _():
            loss_ref[...] = jnp.zeros_like(loss_ref)
            gg_ref[...] = jnp.zeros_like(gg_ref)

        @pl.when(k == 0)
        def _():
            acc_ref[...] = jnp.zeros_like(acc_ref)

        mats = _shift_mats(TF)
        for cs in _col_chunks(FF_BLK):
            xv, xg = vc[:, cs], gc[:, cs]
            val = _conv3(xv, *_neighbours(xv, vp, vn, cs, i, n, mats), wv_ref[:, cs], bv_ref[:, cs])
            gate = _conv3(xg, *_neighbours(xg, gp, gn, cs, i, n, mats), wg_ref[:, cs], bg_ref[:, cs])
            a_ref[:, cs] = (val * (gate * _sigmoid(gate))).astype(BF16)
            cv_ref[:, cs] = val.astype(BF16)
            cg_ref[:, cs] = gate.astype(BF16)
        acc_ref[...] += _dot_nn(a_ref[...], wd_ref[pl.ds(pl.multiple_of(k * FF_BLK, LANES), FF_BLK), :])

        @pl.when(k == nk - 1)
        def _():
            gf = gf_ref[...]
            y, xh, inv = _rms_fwd(h1_ref[...] + acc_ref[...], gf)
            diff = y - t_ref[...]
            part = 0.5 * jnp.sum(jnp.mean(diff * diff, axis=-1, keepdims=True), axis=0, keepdims=True)
            loss_ref[...] += jnp.broadcast_to(part, loss_ref.shape)
            dx, dg = _rms_bwd(diff * (1.0 / D), xh, inv, gf)
            gg_ref[...] += dg
            dh2_ref[...] = dx
            dh2b_ref[...] = dx.astype(BF16)

    row = pl.BlockSpec((TF, D), lambda i, k: (i, 0))
    cw = lambda off: pl.BlockSpec((3, FF_BLK), lambda i, k: (0, k + off))
    cb = lambda off: pl.BlockSpec((1, FF_BLK), lambda i, k: (0, k + off))
    return pl.pallas_call(
        body, name="ffn_down_loss", grid=(n, nk),
        in_specs=_halo_specs_2d(TF, FF_BLK, L, lambda k: k, "ik") + _halo_specs_2d(TF, FF_BLK, L, lambda k: k + nk, "ik")
        + [cw(0), cw(nk), cb(0), cb(nk), _full(w_down_b.shape), row, row, _full((1, D))],
        out_specs=[pl.BlockSpec((TF, FF_BLK), lambda i, k: (i, k))] * 3 + [row, row, _full((1, LANES)), _full((1, D))],
        out_shape=[jax.ShapeDtypeStruct((L, D_FF), BF16)] * 3
        + [jax.ShapeDtypeStruct((L, D), F32), jax.ShapeDtypeStruct((L, D), BF16),
           jax.ShapeDtypeStruct((1, LANES), F32), jax.ShapeDtypeStruct((1, D), F32)],
        scratch_shapes=[pltpu.VMEM((TF, D), F32)],
        compiler_params=_cp("arbitrary", "arbitrary"))(
            up, up, up, up, up, up, conv_w, conv_w, conv_b, conv_b, w_down_b, h1, target, g_final)


def _ffn_act_bwd(c_val, c_gate, w_down_b, dh2):
    L, D = dh2.shape
    n = L // TL
    nk = D_FF // FF_BLK

    def body(v_ref, g_ref, wd_ref, dh_ref, dv_ref, dg_ref, gbv_ref, gbg_ref):
        @pl.when(pl.program_id(1) == 0)
        def _():
            gbv_ref[...] = jnp.zeros_like(gbv_ref)
            gbg_ref[...] = jnp.zeros_like(gbg_ref)

        dh = dh_ref[...]
        for cs in _col_chunks(FF_BLK):
            val, gate = v_ref[:, cs].astype(F32), g_ref[:, cs].astype(F32)
            d_a = _dot_nt(dh, wd_ref[cs, :])
            sg = _sigmoid(gate)
            d_val = d_a * (gate * sg)
            d_gate = d_a * val * (sg * (1.0 + gate * (1.0 - sg)))
            dv_ref[:, cs] = d_val.astype(BF16)
            dg_ref[:, cs] = d_gate.astype(BF16)
            gbv_ref[:, cs] += jnp.sum(d_val, axis=0, keepdims=True)
            gbg_ref[:, cs] += jnp.sum(d_gate, axis=0, keepdims=True)

    blk = pl.BlockSpec((TL, FF_BLK), lambda k, i: (i, k))
    acc = pl.BlockSpec((1, FF_BLK), lambda k, i: (0, k))
    return pl.pallas_call(
        body, name="ffn_act_bwd", grid=(nk, n),
        in_specs=[blk, blk, pl.BlockSpec((FF_BLK, D), lambda k, i: (k, 0)), pl.BlockSpec((TL, D), lambda k, i: (i, 0))],
        out_specs=[blk, blk, acc, acc],
        out_shape=[jax.ShapeDtypeStruct((L, D_FF), BF16), jax.ShapeDtypeStruct((L, D_FF), BF16),
                   jax.ShapeDtypeStruct((1, D_FF), F32), jax.ShapeDtypeStruct((1, D_FF), F32)],
        compiler_params=_cp("arbitrary", "arbitrary"))(c_val, c_gate, w_down_b, dh2)


def _ffn_up_bwd(d_val, d_gate, up, conv_w, w_top, w_rest, h1, dh2, g_ffn):
    L, D = h1.shape
    n = L // TF
    nk = D_FF // FF_BLK

    def body(vp, vc, vn, gp, gc, gn, uv_ref, ug_ref, wv_ref, wg_ref, wt_ref, wr_ref, h1_ref, dh2_ref, g_ref,
             dup_ref, dh1_ref, dh1b_ref, gg_ref, gcw_ref, acc_ref):
        i = pl.program_id(0)
        k = pl.program_id(1)

        @pl.when((i == 0) & (k == 0))
        def _():
            gg_ref[...] = jnp.zeros_like(gg_ref)
            gcw_ref[...] = jnp.zeros_like(gcw_ref)

        @pl.when(k == 0)
        def _():
            acc_ref[...] = jnp.zeros_like(acc_ref)

        acc = jnp.zeros((TF, D), F32)
        for j, (blocks, u_ref, w_ref) in enumerate((((vp, vc, vn), uv_ref, wv_ref), ((gp, gc, gn), ug_ref, wg_ref))):
            for cs in _col_chunks(FF_BLK):
                d = blocks[1][:, cs]
                before, after = _neighbours(d, blocks[0], blocks[2], cs, i, n, None)
                taps = (after, d.astype(F32), before)
                w = w_ref[:, cs]
                d_up = (taps[0] * w[0:1, :] + taps[1] * w[1:2, :] + taps[2] * w[2:3, :]).astype(BF16)
                dup_ref[j, :, cs] = d_up
                blk = k + j * nk
                acc = acc + jnp.concatenate([_dot_nt(d_up, wt_ref[blk, :, cs]), _dot_nt(d_up, wr_ref[blk, :, cs])],
                                            axis=1)
                x = u_ref[:, cs].astype(F32)
                for r in range(3):
                    gcw_ref[j, k, r:r + 1, cs] += jnp.sum(taps[r] * x, axis=0, keepdims=True)
        acc_ref[...] += acc

        @pl.when(k == nk - 1)
        def _():
            g = g_ref[...]
            _, xh, inv = _rms_fwd(h1_ref[...], g)
            dx, dg = _rms_bwd(acc_ref[...], xh, inv, g)
            gg_ref[...] += dg
            dh1 = dh2_ref[...] + dx
            dh1_ref[...] = dh1
            dh1b_ref[...] = dh1.astype(BF16)

    row = pl.BlockSpec((TF, D), lambda i, k: (i, 0))
    cw = lambda off: pl.BlockSpec((3, FF_BLK), lambda i, k: (0, k + off))
    tile = lambda off: pl.BlockSpec((TF, FF_BLK), lambda i, k: (i, k + off))
    return pl.pallas_call(
        body, name="ffn_up_bwd", grid=(n, nk),
        in_specs=_halo_specs_2d(TF, FF_BLK, L, lambda k: k, "ik") + _halo_specs_2d(TF, FF_BLK, L, lambda k: k, "ik")
        + [tile(0), tile(nk), cw(0), cw(nk), _full(w_top.shape), _full(w_rest.shape), row, row, _full((1, D))],
        out_specs=[pl.BlockSpec((2, None, TF, FF_BLK), lambda i, k: (0, k, i, 0)), row, row, _full((1, D)),
                   _full((2, nk, 3, FF_BLK))],
        out_shape=[jax.ShapeDtypeStruct((2, nk, L, FF_BLK), BF16), jax.ShapeDtypeStruct((L, D), F32),
                   jax.ShapeDtypeStruct((L, D), BF16), jax.ShapeDtypeStruct((1, D), F32),
                   jax.ShapeDtypeStruct((2, nk, 3, FF_BLK), F32)],
        scratch_shapes=[pltpu.VMEM((TF, D), F32)],
        compiler_params=_cp("arbitrary", "arbitrary"))(
            d_val, d_val, d_val, d_gate, d_gate, d_gate, up, up, conv_w, conv_w, w_top, w_rest, h1, dh2, g_ffn)


def _matmul_tn(a, b, tm, tn, name, tk=2048):
    L, M = a.shape
    N = b.shape[1]
    tk = min(tk, L)

    def body(a_ref, b_ref, o_ref):
        @pl.when(pl.program_id(2) == 0)
        def _():
            o_ref[...] = jnp.zeros_like(o_ref)

        o_ref[...] += _dot_tn(a_ref[...], b_ref[...])

    return pl.pallas_call(
        body, name=name, grid=(M // tm, N // tn, L // tk),
        in_specs=[pl.BlockSpec((tk, tm), lambda m, n, l: (l, m)), pl.BlockSpec((tk, tn), lambda m, n, l: (l, n))],
        out_specs=pl.BlockSpec((tm, tn), lambda m, n, l: (m, n)),
        out_shape=jax.ShapeDtypeStruct((M, N), F32),
        compiler_params=_cp("parallel", "parallel", "arbitrary"))(a, b)


def _matmul_tn_blocks(a, b, tm, name, tk=2048):
    L, M = a.shape
    J, _, N = b.shape
    tk = min(tk, L)

    def body(a_ref, b_ref, o_ref):
        @pl.when(pl.program_id(2) == 0)
        def _():
            o_ref[...] = jnp.zeros_like(o_ref)

        o_ref[...] += _dot_tn(a_ref[...], b_ref[...])

    return pl.pallas_call(
        body, name=name, grid=(M // tm, J, L // tk),
        in_specs=[pl.BlockSpec((tk, tm), lambda m, j, l: (l, m)), pl.BlockSpec((None, tk, N), lambda m, j, l: (j, l, 0))],
        out_specs=pl.BlockSpec((None, tm, N), lambda m, j, l: (j, m, 0)),
        out_shape=jax.ShapeDtypeStruct((J, M, N), F32),
        compiler_params=_cp("parallel", "parallel", "arbitrary"))(a, b)


def _row_tile(rows):
    for t in (512, 352, 256, 128, 64, 8):
        if rows % t == 0:
            return t
    return rows


def _add_half(g, r, c_arr, name, out_dtype=F32):
    _, _, R, C = g.shape
    tr = _row_tile(R)

    def body(c_ref, g_ref, r_ref, o_ref):
        o_ref[...] = (g_ref[...] + r_ref[...]).astype(out_dtype)

    return pl.pallas_call(
        body, name=name,
        grid_spec=pltpu.PrefetchScalarGridSpec(
            num_scalar_prefetch=1, grid=(g.shape[0], R // tr),
            in_specs=[pl.BlockSpec((None, None, tr, C), lambda j, i, c: (j, c[0], i, 0)),
                      pl.BlockSpec((None, tr, C), lambda j, i, c: (j, i, 0))],
            out_specs=pl.BlockSpec((None, tr, C), lambda j, i, c: (j, i, 0))),
        out_shape=jax.ShapeDtypeStruct(r.shape, out_dtype),
        compiler_params=_cp("parallel", "parallel"))(c_arr, g, r)


def _sum4(p, name):
    _, R, C = p.shape
    tr = _row_tile(R)

    def body(p_ref, o_ref):
        q = [p_ref[j].astype(F32) for j in range(4)]
        o_ref[...] = ((q[0] + q[1]) + q[2]) + q[3]

    return pl.pallas_call(
        body, name=name, grid=(R // tr,),
        in_specs=[pl.BlockSpec((4, tr, C), lambda i: (0, i, 0))],
        out_specs=pl.BlockSpec((tr, C), lambda i: (i, 0)),
        out_shape=jax.ShapeDtypeStruct((R, C), F32), compiler_params=_cp("parallel"))(p)


def _adamw_refs(w_ref, g_ref, m_ref, v_ref, d_ref, nm_ref, nv_ref):
    gv = g_ref[...]
    nm = ADAM_B1 * m_ref[...] + (1.0 - ADAM_B1) * gv
    nv = ADAM_B2 * v_ref[...] + (1.0 - ADAM_B2) * (gv * gv)
    m_hat = nm / (1.0 - ADAM_B1 ** ADAM_STEP)
    v_hat = nv / (1.0 - ADAM_B2 ** ADAM_STEP)
    d_ref[...] = -ADAM_LR * (m_hat / (jnp.sqrt(v_hat) + ADAM_EPS) + ADAM_WD * w_ref[...])
    nm_ref[...] = nm
    nv_ref[...] = nv


def _adamw_many(ws, gs, ms, vs, name):
    n = len(ws)

    def body(*refs):
        for k in range(n):
            _adamw_refs(*(refs[j * n + k] for j in range(7)))

    out_shape = [jax.ShapeDtypeStruct(w.shape, F32) for w in ws] * 3
    res = pl.pallas_call(body, name=name, out_shape=out_shape,
                         compiler_params=pltpu.CompilerParams(vmem_limit_bytes=VMEM_LIMIT))(*ws, *gs, *ms, *vs)
    return res[:n], res[n:2 * n], res[2 * n:]


def _join_rows(own, other, c_arr, name):
    R, C = own.shape
    tr = _row_tile(R)

    def body(c_ref, own_ref, other_ref, o_ref):
        o_ref[...] = jnp.where(pl.program_id(0) == c_ref[0], own_ref[...], other_ref[...])

    half = pl.BlockSpec((tr, C), lambda h, i, c: (i, 0))
    return pl.pallas_call(
        body, name=name,
        grid_spec=pltpu.PrefetchScalarGridSpec(
            num_scalar_prefetch=1, grid=(2, R // tr), in_specs=[half, half],
            out_specs=pl.BlockSpec((tr, C), lambda h, i, c: (h * (R // tr) + i, 0))),
        out_shape=jax.ShapeDtypeStruct((2 * R, C), F32),
        compiler_params=_cp("parallel", "parallel"))(c_arr, own, other)


def _adamw_halves(w, own, other, m, v, name, comm=None):
    R, C = own.shape
    tr = _row_tile(R)
    while tr * C * 4 > ADAMW_BLOCK_BYTES and tr % 16 == 0:
        tr //= 2

    def body(w_ref, own_ref, other_ref, m_ref, v_ref, g_ref, d_ref, nm_ref, nv_ref):
        g_ref[...] = jnp.where(pl.program_id(0) == lax.axis_index("c"), own_ref[...], other_ref[...])
        _adamw_refs(w_ref, g_ref, m_ref, v_ref, d_ref, nm_ref, nv_ref)

    half = pl.BlockSpec((tr, C), lambda h, i: (i, 0))
    full = pl.BlockSpec((tr, C), lambda h, i: (h * (R // tr) + i, 0))
    sh = jax.ShapeDtypeStruct((2 * R, C), F32)
    return _hosted_call(body, comm, name=name, grid=(2, R // tr), in_specs=[full, half, half, full, full],
                        out_specs=[full] * 4, out_shape=[sh] * 4, scratch_shapes=[], args=(w, own, other, m, v))


_ANY = pl.BlockSpec(memory_space=pl.ANY)


def _position():
    return lax.axis_index("x"), lax.axis_index("y"), lax.axis_index("c")


class _Comm:
    def __init__(self, arrs, out_shape, sems, start, finish):
        self.arrs, self.out_shape, self.sems, self.start, self.finish = arrs, out_shape, sems, start, finish


def _comm_call(comm, name):
    n, m = len(comm.arrs), len(comm.out_shape)

    def body(*refs):
        ins, outs, sems = refs[:n], refs[n:n + m], refs[n + m:]
        comm.start(ins, outs, sems)
        comm.finish(ins, outs, sems)

    return pl.pallas_call(
        body, name=name, in_specs=[_ANY] * n, out_specs=[_ANY] * m, out_shape=comm.out_shape,
        scratch_shapes=comm.sems, compiler_params=pltpu.CompilerParams(has_side_effects=True))(*comm.arrs)


def _hosted_call(body, comm, *, name, grid, in_specs, out_specs, out_shape, scratch_shapes, args):
    sem = ("arbitrary",) * len(grid)
    if comm is None:
        return pl.pallas_call(body, name=name, grid=grid, in_specs=in_specs, out_specs=out_specs, out_shape=out_shape,
                              scratch_shapes=scratch_shapes, compiler_params=_cp(*sem))(*args), []
    n_in, n_out, n_scr = len(in_specs), len(out_specs), len(scratch_shapes)
    ci, co = len(comm.arrs), len(comm.out_shape)

    def full(*refs):
        ins, refs = refs[:n_in], refs[n_in:]
        cins, refs = refs[:ci], refs[ci:]
        outs, refs = refs[:n_out], refs[n_out:]
        couts, refs = refs[:co], refs[co:]
        scr, csems = refs[:n_scr], refs[n_scr:]
        first, last = True, True
        for d, size in enumerate(grid):
            first = first & (pl.program_id(d) == 0)
            last = last & (pl.program_id(d) == size - 1)

        @pl.when(first)
        def _():
            comm.start(cins, couts, csems)

        body(*ins, *outs, *scr)

        @pl.when(last)
        def _():
            comm.finish(cins, couts, csems)

    res = pl.pallas_call(
        full, name=name, grid=grid, in_specs=list(in_specs) + [_ANY] * ci, out_specs=list(out_specs) + [_ANY] * co,
        out_shape=list(out_shape) + list(comm.out_shape), scratch_shapes=list(scratch_shapes) + list(comm.sems),
        compiler_params=_cp(*sem))(*args, *comm.arrs)
    return res[:n_out], res[n_out:]


def _comm_join(*comms):
    def parts(xs, attr):
        out, at = [], 0
        for cm in comms:
            n = len(getattr(cm, attr))
            out.append(xs[at:at + n])
            at += n
        return out

    def start(ins, outs, sems):
        for cm, i, o, s in zip(comms, parts(ins, "arrs"), parts(outs, "out_shape"), parts(sems, "sems")):
            cm.start(i, o, s)

    def finish(ins, outs, sems):
        for cm, i, o, s in zip(comms, parts(ins, "arrs"), parts(outs, "out_shape"), parts(sems, "sems")):
            cm.finish(i, o, s)

    cat = lambda attr: [x for cm in comms for x in getattr(cm, attr)]
    return _Comm(cat("arrs"), cat("out_shape"), cat("sems"), start, finish)


def _dma_sems(*counts):
    return [pltpu.SemaphoreType.DMA((n,)) for n in counts]


def _comm_pair_swap(arrs, half=False):
    n = len(arrs)
    out_shape = [jax.ShapeDtypeStruct(a.shape[:1] + a.shape[2:] if half else a.shape, a.dtype) for a in arrs]

    def copies(ins, outs, sems):
        x, y, c = _position()
        return [pltpu.make_async_remote_copy(
            src_ref=ins[k].at[:, 1 - c] if half else ins[k], dst_ref=outs[k], send_sem=sems[0].at[k],
            recv_sem=sems[1].at[k], device_id=(x, y, 1 - c), device_id_type=MESH) for k in range(n)]

    def start(ins, outs, sems):
        for cp in copies(ins, outs, sems):
            cp.start()

    def finish(ins, outs, sems):
        for cp in copies(ins, outs, sems):
            cp.wait()

    return _Comm(arrs, out_shape, _dma_sems(n, n), start, finish)


def _chip_of(j, c):
    return (jnp.right_shift(j, 1), jnp.bitwise_and(j, 1), c)


def _comm_chip_exchange(arrs, scatter):
    n = len(arrs)
    out_shape = [jax.ShapeDtypeStruct(a.shape if scatter else (4,) + a.shape, a.dtype) for a in arrs]

    def copies(ins, outs, sems):
        x, y, c = _position()
        me = 2 * x + y
        local, sent, landed = [], [], []
        for k in range(n):
            local.append(pltpu.make_async_copy(ins[k].at[me] if scatter else ins[k], outs[k].at[me], sems[2].at[k]))
            for d in (1, 2, 3):
                j = jnp.bitwise_xor(me, d)
                s = 3 * k + d - 1
                src = ins[k].at[j] if scatter else ins[k]
                for dst, group in ((outs[k].at[me], sent), (outs[k].at[j], landed)):
                    group.append(pltpu.make_async_remote_copy(
                        src_ref=src, dst_ref=dst, send_sem=sems[0].at[s], recv_sem=sems[1].at[s],
                        device_id=_chip_of(j, c), device_id_type=MESH))
        return local, sent, landed

    def start(ins, outs, sems):
        local, sent, _ = copies(ins, outs, sems)
        for cp in local + sent:
            cp.start()

    def finish(ins, outs, sems):
        local, sent, landed = copies(ins, outs, sems)
        for cp in sent:
            cp.wait_send()
        for cp in landed:
            cp.wait_recv()
        for cp in local:
            cp.wait()

    return _Comm(arrs, out_shape, _dma_sems(3 * n, 3 * n, n), start, finish)


LOCAL_PARTS = 4


def _comm_gather_split(shards, whole):
    n, nw = len(shards), len(whole)
    arrs = list(shards) + list(whole)
    out_shape = [jax.ShapeDtypeStruct((4,) + a.shape, a.dtype) for a in arrs]

    def copies(ins, outs, sems):
        x, y, c = _position()
        me = 2 * x + y
        local, sent, landed, passed, passed_in = [], [], [], [], []
        for k in range(n + nw):
            if k >= n:
                local.append(pltpu.make_async_copy(ins[k], outs[k].at[me], sems[4].at[LOCAL_PARTS * k]))
            else:
                part = shards[k].shape[0] // LOCAL_PARTS
                for r in range(LOCAL_PARTS):
                    local.append(pltpu.make_async_copy(ins[k].at[pl.ds(r * part, part)],
                                                       outs[k].at[me, pl.ds(r * part, part)],
                                                       sems[4].at[LOCAL_PARTS * k + r]))
            for d in (1, 2, 3):
                j = jnp.bitwise_xor(me, d)
                s = 3 * k + d - 1
                if k >= n:
                    src, mine, theirs = ins[k], outs[k].at[me], outs[k].at[j]
                else:
                    h = shards[k].shape[0] // 2
                    rows = pl.ds(pl.multiple_of(c * h, 16), h)
                    other = pl.ds(pl.multiple_of((1 - c) * h, 16), h)
                    src, mine, theirs = ins[k].at[rows], outs[k].at[me, rows], outs[k].at[j, rows]
                    for dst, group in ((theirs, passed), (outs[k].at[j, other], passed_in)):
                        group.append(pltpu.make_async_remote_copy(
                            src_ref=theirs, dst_ref=dst, send_sem=sems[2].at[s], recv_sem=sems[3].at[s],
                            device_id=(x, y, 1 - c), device_id_type=MESH))
                for dst, group in ((mine, sent), (theirs, landed)):
                    group.append(pltpu.make_async_remote_copy(
                        src_ref=src, dst_ref=dst, send_sem=sems[0].at[s], recv_sem=sems[1].at[s],
                        device_id=_chip_of(j, c), device_id_type=MESH))
        return local, sent, landed, passed, passed_in

    def start(ins, outs, sems):
        local, sent, _, _, _ = copies(ins, outs, sems)
        for cp in local + sent:
            cp.start()

    def finish(ins, outs, sems):
        local, sent, landed, passed, passed_in = copies(ins, outs, sems)
        for cp in landed[:3 * n]:
            cp.wait_recv()
        for cp in passed:
            cp.start()
        for cp in landed[3 * n:]:
            cp.wait_recv()
        for cp in sent:
            cp.wait_send()
        for cp in passed:
            cp.wait_send()
        for cp in passed_in:
            cp.wait_recv()
        for cp in local:
            cp.wait()

    t = 3 * (n + nw)
    return _Comm(arrs, out_shape, _dma_sems(t, t, max(3 * n, 1), max(3 * n, 1), LOCAL_PARTS * (n + nw)), start, finish)


def _pack(arrs, row_multiple):
    parts = []
    for a in arrs:
        flat = a.reshape(-1).astype(F32)
        pad = (-flat.shape[0]) % LANES
        parts.append(jnp.pad(flat, (0, pad)) if pad else flat)
    flat = jnp.concatenate(parts)
    rows = -(-flat.shape[0] // LANES)
    rows_p = -(-rows // row_multiple) * row_multiple
    return jnp.pad(flat, (0, rows_p * LANES - flat.shape[0])).reshape(rows_p, LANES)


def _unpack(packed, shapes):
    flat = packed.reshape(-1)
    outs, off = [], 0
    for sh in shapes:
        size = int(np.prod(sh))
        outs.append(flat[off:off + size].reshape(sh))
        off += size + (-size) % LANES
    return outs


SMALL = ["norm_mix_g", "pool_w", "pool_scale", "ssm_log_neg_a_re", "ssm_a_im", "ssm_log_dt", "ssm_b_re", "ssm_b_im",
         "ssm_c_re", "ssm_c_im", "ssm_d", "glu_b", "out_norm_pool_g", "out_norm_ssm_g", "norm_ffn_g", "conv_b",
         "final_norm_g"]
BIG = ["w_in", "glu_w", "w_out", "w_up", "w_down"]
WIDE = ["pool_w", "ssm_b_re", "ssm_b_im", "ssm_c_re", "ssm_c_im"]
WEIGHTS = ['norm_mix_g', 'w_in', 'pool_w', 'pool_scale', 'ssm_log_neg_a_re', 'ssm_a_im', 'ssm_log_dt', 'ssm_b_re',
           'ssm_b_im', 'ssm_c_re', 'ssm_c_im', 'ssm_d', 'glu_w', 'glu_b', 'out_norm_pool_g', 'out_norm_ssm_g', 'w_out',
           'norm_ffn_g', 'w_up', 'conv_w', 'conv_b', 'w_down', 'final_norm_g']


def _local_step(x, target, p, full, shards=None, c_arr=None):
    L, D = x.shape
    dist = shards is not None
    row = lambda a: a.reshape(1, -1)
    w_in = full["w_in"]
    pool_w_b = p["pool_w"].astype(BF16)
    g_mix, g_pool, g_ssm, g_ffn, g_fin = (row(p[k]) for k in (
        "norm_mix_g", "out_norm_pool_g", "out_norm_ssm_g", "norm_ffn_g", "final_norm_g"))
    pool_scale, ssm_d, glu_b, conv_b = (row(p[k]) for k in ("pool_scale", "ssm_d", "glu_b", "conv_b"))

    lnar = p["ssm_log_neg_a_re"].reshape(2 * N_SSM_GROUPS, SSM_STATE)
    aim = p["ssm_a_im"].reshape(2 * N_SSM_GROUPS, SSM_STATE)
    ldt = jnp.broadcast_to(p["ssm_log_dt"].reshape(2 * N_SSM_GROUPS, 1), lnar.shape)
    lam_re, lam_im, f_re, f_im = _ssm_params(lnar, aim, ldt)
    flat2 = lambda a: a.reshape(2, N_STATE)
    lam4 = jnp.stack([flat2(lam_re)[0], flat2(lam_im)[0], flat2(lam_re)[1], flat2(lam_im)[1]])
    tables = _scan_tables(lam4)
    per_group = (2, N_SSM_GROUPS, SSM_STATE)
    dense = _ssm_expand(p["ssm_b_re"], p["ssm_b_im"], p["ssm_c_re"], p["ssm_c_im"],
                        f_re.reshape(per_group + (1,)), f_im.reshape(per_group + (1,)))
    ssm_args = [tuple(dense[4 * d:4 * d + 4]) + (tables,) for d in range(2)]

    u, xn = _in_proj(x, g_mix, w_in)
    yn_pool = _pool_fwd(u, pool_w_b, pool_scale, g_pool)
    gather1, gather2 = None, None
    if dist:
        gather1 = _comm_gather_split([shards[k] for k in ("glu_w", "w_out", "w_down")] + [shards["w_up"][:UP_TOP]],
                                     [shards["conv_w"]])
        gather2 = _comm_gather_split([shards["w_up"][UP_TOP:]], [])
    (y0, s0r, s0i), got1 = _ssm_scan_fwd(u, *ssm_args[0], 0, False, comm=gather1)
    (y1, s1r, s1i), got2 = _ssm_scan_fwd(u, *ssm_args[1], 2, True, comm=gather2)
    if dist:
        glu_w, w_out, w_down = (g.reshape((-1,) + g.shape[2:]) for g in got1[:3])
        conv_w = jnp.transpose(got1[4], (1, 0, 2)).reshape(3, -1)
        w_top, w_rest = got1[3], got2[0]
    else:
        glu_w, w_out, w_down, conv_w = (full[k] for k in ("glu_w", "w_out", "w_down", "conv_w"))
        w_top, w_rest = full["w_up"][:, :UP_TOP], full["w_up"][:, UP_TOP:]
    h1, hn, ycat = _mix_out(yn_pool, y0, y1, u, x, ssm_d, glu_w, glu_b, g_ssm, w_out, g_ffn)
    up = _ffn_up(hn, w_top, w_rest)
    a, c_val, c_gate, dh2, dh2_b, loss, g_final = _ffn_down_loss(up, conv_w, conv_b, w_down, h1, target, g_fin)

    d_val, d_gate, gbv, gbg = _ffn_act_bwd(c_val, c_gate, w_down, dh2_b)
    g_w_down = _matmul_tn(a, dh2_b, FF_BLK, D, "grad_w_down")
    d_up, dh1, dh1_b, g_ffn_g, gcw = _ffn_up_bwd(d_val, d_gate, up, conv_w, w_top, w_rest, h1, dh2, g_ffn)
    g_w_up = _matmul_tn_blocks(hn, d_up.reshape(4, L, FF_BLK), TM, "grad_w_up")
    g_w_out = _matmul_tn(ycat, dh1_b, TM, D, "grad_w_out")
    late = ("w_up", "w_down", "w_out", "glu_w")
    halves = [g_w_up.reshape(4, 2, D // 2, FF_BLK), g_w_down.reshape(4, 2, D_FF // 8, D)]
    (dy, du_direct, g_glu_w, g_glu_b, g_ssm_d, g_ssm_g), swapped = _ssm_bwd_local(
        dh1_b, y0, y1, u, ssm_d, glu_w, glu_b, g_ssm, w_out, comm=_comm_pair_swap(halves, half=True) if dist else None)
    more = [g_w_out.reshape(4, 2, D // 8, D), g_glu_w.reshape(4, 2, D_SSM // 8, D_SSM)]
    (d_pooled, g_pool_w, g_pool_scale, g_pool_g), swapped_more = _pool_bwd_local(
        dh1_b, u, w_out, pool_w_b, pool_scale, g_pool, comm=_comm_pair_swap(more, half=True) if dist else None)
    halves, from_sibling = halves + more, list(swapped) + list(swapped_more)
    du_pool = _pool_bwd_window(d_pooled)
    reduce_a, reduce_b = None, None
    if dist:
        chip_sums = [_add_half(h, r, c_arr, "sum_pair_" + k, BF16) for k, h, r in zip(late, halves, from_sibling)]
        reduce_a = _comm_chip_exchange(chip_sums[:1], scatter=True)
        reduce_b = _comm_chip_exchange(chip_sums[1:], scatter=True)
    (du0, gb0r, gb0i, gc0r, gc0i, gv0), chips_a = _ssm_scan_bwd(dy, u, s0r, s0i, *ssm_args[0], 1, True, comm=reduce_a)
    (du1, gb1r, gb1i, gc1r, gc1i, gv1), chips_b = _ssm_scan_bwd(dy, u, s1r, s1i, *ssm_args[1], 3, False, comm=reduce_b)
    mine = [_sum4(r, "sum_chips_" + k) for k, r in zip(late, list(chips_a) + list(chips_b))]
    by_state = (2, N_SSM_GROUPS, 1, SSM_STATE)
    g_b_re, g_b_im, g_f_re, g_f_im = _ssm_unfold(
        jnp.stack([gb0r, gb1r]), jnp.stack([gb0i, gb1i]),
        jnp.swapaxes(p["ssm_b_re"], 2, 3), jnp.swapaxes(p["ssm_b_im"], 2, 3),
        f_re.reshape(by_state), f_im.reshape(by_state))
    gvec = lambda j: jnp.stack([gv0[j], gv1[j]]).reshape(2 * N_SSM_GROUPS, SSM_STATE)
    g_lnar, g_aim, g_ldt = _ssm_params_bwd(lnar, aim, ldt, gvec(0), gvec(1),
                                           g_f_re.reshape(lnar.shape), g_f_im.reshape(lnar.shape))
    (grad_x, d_u_b, g_mix_g), theirs = _in_bwd(du_pool, du_direct, du0, du1, dh1, x, g_mix, w_in,
                                               comm=_comm_pair_swap(mine) if dist else None)
    g_w_in = _matmul_tn(xn, d_u_b, TM, D, "grad_w_in")

    small = {
        "norm_mix_g": g_mix_g, "pool_w": g_pool_w, "pool_scale": g_pool_scale,
        "ssm_log_neg_a_re": g_lnar, "ssm_a_im": g_aim, "ssm_log_dt": g_ldt,
        "ssm_b_re": jnp.swapaxes(g_b_re, 2, 3), "ssm_b_im": jnp.swapaxes(g_b_im, 2, 3),
        "ssm_c_re": jnp.stack([gc0r, gc1r]), "ssm_c_im": jnp.stack([gc0i, gc1i]),
        "ssm_d": g_ssm_d, "glu_b": g_glu_b, "out_norm_pool_g": g_pool_g, "out_norm_ssm_g": g_ssm_g,
        "norm_ffn_g": g_ffn_g, "conv_b": jnp.concatenate([gbv[0], gbg[0]]), "final_norm_g": g_final,
        "conv_w": jnp.transpose(gcw, (2, 0, 1, 3)).reshape(3, -1),
    }
    big = {"w_in": g_w_in}
    reduced = dict(zip(late, zip(mine, theirs)))
    if not dist:
        big.update({"w_up": g_w_up, "w_down": g_w_down, "w_out": g_w_out, "glu_w": g_glu_w})
    return loss, grad_x, small, big, reduced


def kernel(x, norm_mix_g, w_in, pool_w, pool_scale, ssm_log_neg_a_re, ssm_a_im, ssm_log_dt, ssm_b_re, ssm_b_im, ssm_c_re, ssm_c_im, ssm_d, glu_w, glu_b, out_norm_pool_g, out_norm_ssm_g, w_out, norm_ffn_g, w_up, conv_w, conv_b, w_down, final_norm_g, loss_target, m_norm_mix_g, m_w_in, m_pool_w, m_pool_scale, m_ssm_log_neg_a_re, m_ssm_a_im, m_ssm_log_dt, m_ssm_b_re, m_ssm_b_im, m_ssm_c_re, m_ssm_c_im, m_ssm_d, m_glu_w, m_glu_b, m_out_norm_pool_g, m_out_norm_ssm_g, m_w_out, m_norm_ffn_g, m_w_up, m_conv_w, m_conv_b, m_w_down, m_final_norm_g, v_norm_mix_g, v_w_in, v_pool_w, v_pool_scale, v_ssm_log_neg_a_re, v_ssm_a_im, v_ssm_log_dt, v_ssm_b_re, v_ssm_b_im, v_ssm_c_re, v_ssm_c_im, v_ssm_d, v_glu_w, v_glu_b, v_out_norm_pool_g, v_out_norm_ssm_g, v_w_out, v_norm_ffn_g, v_w_up, v_conv_w, v_conv_b, v_w_down, v_final_norm_g):
    args = locals()
    w = {k: args[k] for k in WEIGHTS}
    m = {k: args["m_" + k] for k in WEIGHTS}
    v = {k: args["v_" + k] for k in WEIGHTS}
    chip = 2 * lax.axis_index("x") + lax.axis_index("y")
    c_arr = lax.axis_index("c").astype(jnp.int32).reshape(1)

    shards = {k: w[k].astype(BF16) for k in BIG}
    shards["conv_w"] = conv_w
    w_in_full = _comm_call(_comm_gather_split([shards["w_in"]], []), "gather_w_in")[0]
    loss, grad_x, g_small, g_big, reduced = _local_step(
        x[0], loss_target[0], w, {"w_in": w_in_full.reshape(-1, w_in_full.shape[-1])}, shards, c_arr)

    exact = [k for k in SMALL if k not in WIDE]
    packs = [_pack([loss] + [g_small[k] for k in exact] + [g_small["conv_w"]], 512),
             _pack([g_small[k] for k in WIDE], 512)]
    halves = [g_big["w_in"].reshape(4, 2, g_big["w_in"].shape[0] // 8, -1)]
    halves += [pk.reshape(1, 2, pk.shape[0] // 2, LANES) for pk in packs]
    from_sibling = _comm_call(_comm_pair_swap(halves, half=True), "reduce_pair")
    names = ("w_in", "exact", "wide")
    sums = [_add_half(h, r, c_arr, "sum_pair_" + k, dt)
            for k, h, r, dt in zip(names, halves, from_sibling, (BF16, F32, BF16))]
    grads, delta, new_m, new_v = {}, {}, {}, {}

    def adamw_behind(k, comm):
        own, other = reduced[k]
        (grads[k], delta[k], new_m[k], new_v[k]), got = _adamw_halves(
            w[k], own, other, m[k], v[k], "adamw_" + k, comm=comm)
        return got

    from_chips = adamw_behind("w_up", _comm_join(_comm_chip_exchange(sums[:1], scatter=True),
                                                  _comm_chip_exchange([s[0] for s in sums[1:]], scatter=False)))
    mine = [_sum4(r, "sum_chips_" + k) for k, r in zip(names, from_chips)]
    theirs = adamw_behind("w_down", _comm_pair_swap(mine))
    for k in ("w_out", "glu_w"):
        adamw_behind(k, None)
    exact_all = _join_rows(mine[1], theirs[1], c_arr, "join_exact")
    wide_all = _join_rows(mine[2], theirs[2], c_arr, "join_wide")
    shapes = [loss.shape] + [w[k].shape for k in exact] + [(3, 4 * FF_BLK)]
    grads.update(zip(["loss"] + exact + ["conv_w_full"], _unpack(exact_all, shapes)))
    grads.update(zip(WIDE, _unpack(wide_all, [w[k].shape for k in WIDE])))
    loss = grads.pop("loss")[0, 0]
    grads["conv_w"] = lax.dynamic_slice_in_dim(grads.pop("conv_w_full"), chip * FF_BLK, FF_BLK, axis=1)

    reduced["w_in"] = (mine[0], theirs[0])
    adamw_behind("w_in", None)
    padded = ["ssm_b_re", "ssm_b_im"]
    for keys, name in ((padded, "adamw_ssm_b"), ([k for k in SMALL + ["conv_w"] if k not in padded], "adamw_small")):
        outs = _adamw_many(*([d[k] for k in keys] for d in (w, grads, m, v)), name)
        for d, o in zip((delta, new_m, new_v), outs):
            d.update(zip(keys, o))

    return (loss, grad_x[None], *[grads[k] for k in WEIGHTS], *[delta[k] for k in WEIGHTS],
            *[new_m[k] for k in WEIGHTS], *[new_v[k] for k in WEIGHTS])
```

```python
import numpy as np
import jax
import jax.numpy as jnp
from jax import lax
from jax.experimental import pallas as pl
from jax.experimental.pallas import tpu as pltpu

F32 = jnp.float32
BF16 = jnp.bfloat16
MESH = pl.DeviceIdType.MESH

EPS = 1e-6
POOL_WINDOWS = (2, 4, 8, 16)
POOL_GROUP = 128
SSM_GROUP = 16
SSM_STATE = 64
N_SSM_GROUPS = 32
N_STATE = N_SSM_GROUPS * SSM_STATE
QUAD = 256
N_QUAD = N_STATE // QUAD
SLAB = 256
D_SSM = 512
D_POOL = 512
D_FF = 2816
FF_BLK = 1408
HALO = 8
HALO_B = 16
LANES = 128
ADAM_LR, ADAM_B1, ADAM_B2, ADAM_EPS, ADAM_WD, ADAM_STEP = 0.001, 0.9, 0.999, 1e-08, 0.01, 10
VMEM_LIMIT = 56 * 2 ** 20
ADAMW_BLOCK_BYTES = 2 ** 20

TL = 512
TM = 1024
TF = 256
TC = 512
SEG = 8
SEG_LEN = TC // SEG
SCAN_UNROLL = 4
SCAN_W = 512


def _cp(*sem):
    return pltpu.CompilerParams(dimension_semantics=sem, vmem_limit_bytes=VMEM_LIMIT)


def _dot_nn(a, b):
    return jnp.dot(a, b, preferred_element_type=F32)


def _dot_nt(a, b):
    return lax.dot_general(a, b, (((1,), (1,)), ((), ())), preferred_element_type=F32)


def _dot_tn(a, b):
    return lax.dot_general(a, b, (((0,), (0,)), ((), ())), preferred_element_type=F32)


def _rms_fwd(x, g):
    inv = lax.rsqrt(jnp.mean(x * x, axis=-1, keepdims=True) + EPS)
    xh = x * inv
    return xh * g, xh, inv


def _rms_bwd(dy, xh, inv, g):
    dg = jnp.sum(dy * xh, axis=0, keepdims=True)
    dxh = dy * g
    dx = inv * (dxh - xh * jnp.mean(dxh * xh, axis=-1, keepdims=True))
    return dx, dg


_GELU_C = 0.7978845608028654
_GELU_A = 0.044715


def _gelu(y):
    t = jnp.tanh(_GELU_C * (y + _GELU_A * (y * y * y)))
    return 0.5 * y * (1.0 + t), t


def _gelu_grad(y, t):
    return 0.5 * (1.0 + t) + 0.5 * y * (1.0 - t * t) * (_GELU_C * (1.0 + 3.0 * _GELU_A * y * y))


def _sigmoid(x):
    return 1.0 / (1.0 + jnp.exp(-x))


def _full(shape):
    n = len(shape)
    return pl.BlockSpec(shape, lambda *_: (0,) * n)


def _fill_ext(ext_ref, prev_ref, cur_ref, next_ref, i, n, rows):
    ext_ref[0:HALO, :] = jnp.where(i > 0, prev_ref[...], 0.0).astype(ext_ref.dtype)
    ext_ref[HALO:HALO + rows, :] = cur_ref[...]
    ext_ref[HALO + rows:2 * HALO + rows, :] = jnp.where(i < n - 1, next_ref[...], 0.0).astype(ext_ref.dtype)


def _in_proj(x, g, w):
    L, D = x.shape
    E = w.shape[1]

    def body(x_ref, g_ref, w_ref, u_ref, xn_ref):
        y, _, _ = _rms_fwd(x_ref[...], g_ref[...])
        yb = y.astype(BF16)
        xn_ref[...] = yb
        u_ref[...] = _dot_nn(yb, w_ref[...])

    return pl.pallas_call(
        body, name="in_proj", grid=(L // TL,),
        in_specs=[pl.BlockSpec((TL, D), lambda i: (i, 0)), _full((1, D)), _full(w.shape)],
        out_specs=[pl.BlockSpec((TL, E), lambda i: (i, 0)), pl.BlockSpec((TL, D), lambda i: (i, 0))],
        out_shape=[jax.ShapeDtypeStruct((L, E), F32), jax.ShapeDtypeStruct((L, D), BF16)],
        compiler_params=_cp("parallel"))(x, g, w)


def _halo_specs_1d(rows, width, L, col):
    rb = rows // HALO
    last = L // HALO - 1
    return [pl.BlockSpec((HALO, width), lambda i: (jnp.maximum(i * rb - 1, 0), col)),
            pl.BlockSpec((rows, width), lambda i: (i, col)),
            pl.BlockSpec((HALO, width), lambda i: (jnp.minimum((i + 1) * rb, last), col))]


def _pooled_from_ext(ext_ref, t0, rows, L):
    t = t0 + lax.broadcasted_iota(jnp.int32, (rows, 1), 0)
    outs = []
    for gi, w in enumerate(POOL_WINDOWS):
        half = w // 2
        cs = slice(gi * POOL_GROUP, (gi + 1) * POOL_GROUP)
        acc = ext_ref[pl.ds(HALO - half, rows), cs]
        for s in range(-half + 1, half):
            acc = acc + ext_ref[pl.ds(HALO + s, rows), cs]
        cnt = (jnp.minimum(t + half, L) - jnp.maximum(t - half, 0)).astype(F32)
        outs.append(acc / cnt - ext_ref[pl.ds(HALO, rows), cs])
    return outs


def _pool_fwd(u, pool_w_b, pool_scale, g_pool):
    L = u.shape[0]
    n = L // TL

    def body(prev_ref, cur_ref, next_ref, pw_ref, ps_ref, g_ref, out_ref, ext_ref):
        i = pl.program_id(0)
        _fill_ext(ext_ref, prev_ref, cur_ref, next_ref, i, n, TL)
        pooled = _pooled_from_ext(ext_ref, i * TL, TL, L)
        ypre = jnp.concatenate([_dot_nn(pooled[gi].astype(BF16), pw_ref[gi]) for gi in range(4)], axis=-1)
        yn, _, _ = _rms_fwd(ypre * ps_ref[...], g_ref[...])
        out_ref[...] = yn.astype(BF16)

    return pl.pallas_call(
        body, name="pool_fwd", grid=(n,),
        in_specs=_halo_specs_1d(TL, D_POOL, L, 0) + [_full(pool_w_b.shape), _full((1, D_POOL)), _full((1, D_POOL))],
        out_specs=pl.BlockSpec((TL, D_POOL), lambda i: (i, 0)),
        out_shape=jax.ShapeDtypeStruct((L, D_POOL), BF16),
        scratch_shapes=[pltpu.VMEM((TL + 2 * HALO, D_POOL), F32)],
        compiler_params=_cp("parallel"))(u, u, u, pool_w_b, pool_scale, g_pool)


def _pool_bwd_local(dh1, u, w_out_b, pool_w_b, pool_scale, g_pool, comm=None):
    L = u.shape[0]
    n = L // TL
    D = dh1.shape[1]

    def body(dh_ref, prev_ref, cur_ref, next_ref, wo_ref, pw_ref, ps_ref, g_ref,
             dp_ref, gpw_ref, gps_ref, gg_ref, ext_ref):
        i = pl.program_id(0)

        @pl.when(i == 0)
        def _():
            gpw_ref[...] = jnp.zeros_like(gpw_ref)
            gps_ref[...] = jnp.zeros_like(gps_ref)
            gg_ref[...] = jnp.zeros_like(gg_ref)

        _fill_ext(ext_ref, prev_ref, cur_ref, next_ref, i, n, TL)
        pooled = [p.astype(BF16) for p in _pooled_from_ext(ext_ref, i * TL, TL, L)]
        ypre = jnp.concatenate([_dot_nn(pooled[gi], pw_ref[gi]) for gi in range(4)], axis=-1)
        ps = ps_ref[...]
        g = g_ref[...]
        _, xh, inv = _rms_fwd(ypre * ps, g)
        d_yn = _dot_nt(dh_ref[...], wo_ref[...])
        d_y, dg = _rms_bwd(d_yn, xh, inv, g)
        gg_ref[...] += dg
        gps_ref[...] += jnp.sum(d_y * ypre, axis=0, keepdims=True)
        d_ypre = (d_y * ps).astype(BF16)
        for gi in range(4):
            cs = slice(gi * POOL_GROUP, (gi + 1) * POOL_GROUP)
            dp_ref[:, cs] = _dot_nt(d_ypre[:, cs], pw_ref[gi])
            gpw_ref[gi] += _dot_tn(pooled[gi], d_ypre[:, cs])

    return _hosted_call(
        body, comm, name="pool_bwd_local", grid=(n,),
        in_specs=[pl.BlockSpec((TL, D), lambda i: (i, 0))] + _halo_specs_1d(TL, D_POOL, L, 0)
        + [pl.BlockSpec((D_POOL, D), lambda i: (0, 0)), _full(pool_w_b.shape), _full((1, D_POOL)), _full((1, D_POOL))],
        out_specs=[pl.BlockSpec((TL, D_POOL), lambda i: (i, 0)), _full(pool_w_b.shape),
                   _full((1, D_POOL)), _full((1, D_POOL))],
        out_shape=[jax.ShapeDtypeStruct((L, D_POOL), F32), jax.ShapeDtypeStruct(pool_w_b.shape, F32),
                   jax.ShapeDtypeStruct((1, D_POOL), F32), jax.ShapeDtypeStruct((1, D_POOL), F32)],
        scratch_shapes=[pltpu.VMEM((TL + 2 * HALO, D_POOL), F32)],
        args=(dh1, u, u, u, w_out_b, pool_w_b, pool_scale, g_pool))


def _pool_bwd_window(d_pooled):
    L = d_pooled.shape[0]
    n = L // TL
    R = TL + 2 * HALO

    def body(prev_ref, cur_ref, next_ref, out_ref, ext_ref, q_ref):
        i = pl.program_id(0)
        _fill_ext(ext_ref, prev_ref, cur_ref, next_ref, i, n, TL)
        tr = i * TL - HALO + lax.broadcasted_iota(jnp.int32, (R, 1), 0)
        for gi, w in enumerate(POOL_WINDOWS):
            half = w // 2
            cs = slice(gi * POOL_GROUP, (gi + 1) * POOL_GROUP)
            cnt = jnp.maximum(jnp.minimum(tr + half, L) - jnp.maximum(tr - half, 0), 1).astype(F32)
            q_ref[:, cs] = ext_ref[:, cs] / cnt
        for gi, w in enumerate(POOL_WINDOWS):
            half = w // 2
            cs = slice(gi * POOL_GROUP, (gi + 1) * POOL_GROUP)
            acc = q_ref[pl.ds(HALO - half + 1, TL), cs]
            for s in range(-half + 2, half + 1):
                acc = acc + q_ref[pl.ds(HALO + s, TL), cs]
            out_ref[:, cs] = acc - ext_ref[pl.ds(HALO, TL), cs]

    return pl.pallas_call(
        body, name="pool_bwd_window", grid=(n,),
        in_specs=_halo_specs_1d(TL, D_POOL, L, 0),
        out_specs=pl.BlockSpec((TL, D_POOL), lambda i: (i, 0)),
        out_shape=jax.ShapeDtypeStruct((L, D_POOL), F32),
        scratch_shapes=[pltpu.VMEM((R, D_POOL), F32), pltpu.VMEM((R, D_POOL), F32)],
        compiler_params=_cp("parallel"))(d_pooled, d_pooled, d_pooled)


def _ssm_param_fn(lnar, aim, ldt):
    dt = jnp.exp(ldt)
    a_re = -jnp.exp(lnar)
    mag = jnp.exp(a_re * dt)
    ang = aim * dt
    lr, li = mag * jnp.cos(ang), mag * jnp.sin(ang)
    den = a_re * a_re + aim * aim
    fr = ((lr - 1.0) * a_re + li * aim) / den
    fi = (li * a_re - (lr - 1.0) * aim) / den
    return lr, li, fr, fi


def _ssm_params(lnar, aim, ldt):
    def body(a_ref, b_ref, c_ref, lr_ref, li_ref, fr_ref, fi_ref):
        lr, li, fr, fi = _ssm_param_fn(a_ref[...], b_ref[...], c_ref[...])
        lr_ref[...] = lr
        li_ref[...] = li
        fr_ref[...] = fr
        fi_ref[...] = fi

    sh = jax.ShapeDtypeStruct(lnar.shape, F32)
    return pl.pallas_call(body, name="ssm_params", out_shape=[sh] * 4)(lnar, aim, ldt)


def _ssm_params_bwd(lnar, aim, ldt, glr, gli, gfr, gfi):
    def body(a_ref, b_ref, c_ref, g0, g1, g2, g3, da_ref, db_ref, dc_ref):
        _, vjp = jax.vjp(_ssm_param_fn, a_ref[...], b_ref[...], c_ref[...])
        da, db, dc = vjp((g0[...], g1[...], g2[...], g3[...]))
        da_ref[...] = da
        db_ref[...] = db
        dc_ref[...] = jnp.sum(dc, axis=1, keepdims=True)

    return pl.pallas_call(
        body, name="ssm_params_bwd",
        out_shape=[jax.ShapeDtypeStruct(lnar.shape, F32), jax.ShapeDtypeStruct(aim.shape, F32),
                   jax.ShapeDtypeStruct((ldt.shape[0], 1), F32)])(lnar, aim, ldt, glr, gli, gfr, gfi)


def _scan_tables(lam4):
    def build(lr, li, reverse, out_ref, k):
        pr, pi = lr, li
        for d in range(SEG_LEN):
            j = SEG_LEN - 1 - d if reverse else d
            out_ref[k, 0, j:j + 1, :] = pr
            out_ref[k, 1, j:j + 1, :] = pi
            pr, pi = pr * lr - pi * li, pr * li + pi * lr

    def body(lam_ref, out_ref):
        l0r, l0i, l1r, l1i = (lam_ref[j:j + 1, :] for j in range(4))
        build(l0r, l0i, False, out_ref, 0)
        build(l0r, -l0i, True, out_ref, 1)
        build(l1r, l1i, True, out_ref, 2)
        build(l1r, -l1i, False, out_ref, 3)

    return pl.pallas_call(body, name="scan_tables",
                          out_shape=jax.ShapeDtypeStruct((4, 2, SEG_LEN, N_STATE), F32))(lam4)


def _b_block(g):
    q, gl = divmod(g, 4)
    r0, c0 = gl * SSM_STATE, (q % 4) * 4 * SSM_GROUP + gl * SSM_GROUP
    return q, slice(r0, r0 + SSM_STATE), slice(c0, c0 + SSM_GROUP)


def _c_block(g):
    q, rows, cols = _b_block(g)
    return q, cols, rows


def _ssm_expand(b_re, b_im, c_re, c_im, f_re, f_im):
    def body(bre_ref, bim_ref, cre_ref, cim_ref, fre_ref, fim_ref, *rest):
        outs, tmp, bbr_ref, bbi_ref = rest[:8], rest[8], rest[9], rest[10]
        fr, fi, br, bi = fre_ref[...], fim_ref[...], bre_ref[...], bim_ref[...]
        bbr_ref[...] = fr * br - fi * bi
        bbi_ref[...] = fr * bi + fi * br
        for d in range(2):
            for j, (src, where) in enumerate(((bbr_ref, _b_block), (bbi_ref, _b_block),
                                              (cre_ref, _c_block), (cim_ref, _c_block))):
                tmp[...] = jnp.zeros_like(tmp)
                for g in range(N_SSM_GROUPS):
                    q, rows, cols = where(g)
                    tmp[q, rows, cols] = src[d, g]
                outs[4 * d + j][...] = tmp[...].astype(BF16)

    dense = jax.ShapeDtypeStruct((N_QUAD, QUAD, SLAB), BF16)
    return pl.pallas_call(body, name="ssm_expand", out_shape=[dense] * 8,
                          scratch_shapes=[pltpu.VMEM((N_QUAD, QUAD, SLAB), F32), pltpu.VMEM(b_re.shape, F32),
                                          pltpu.VMEM(b_re.shape, F32)],
                          compiler_params=pltpu.CompilerParams(vmem_limit_bytes=VMEM_LIMIT))(
                              b_re, b_im, c_re, c_im, f_re, f_im)


def _ssm_unfold(gbb_re, gbb_im, b_re_t, b_im_t, f_re, f_im):
    def body(gr_ref, gi_ref, br_ref, bi_ref, fr_ref, fi_ref, obr_ref, obi_ref, ofr_ref, ofi_ref):
        gr, gi, br, bi, fr, fi = (r[...] for r in (gr_ref, gi_ref, br_ref, bi_ref, fr_ref, fi_ref))
        obr_ref[...] = fr * gr + fi * gi
        obi_ref[...] = fr * gi - fi * gr
        ofr_ref[...] = jnp.sum(br * gr + bi * gi, axis=2, keepdims=True)
        ofi_ref[...] = jnp.sum(br * gi - bi * gr, axis=2, keepdims=True)

    gb = jax.ShapeDtypeStruct(gbb_re.shape, F32)
    gf = jax.ShapeDtypeStruct(f_re.shape, F32)
    return pl.pallas_call(body, name="ssm_unfold", out_shape=[gb, gb, gf, gf])(
        gbb_re, gbb_im, b_re_t, b_im_t, f_re, f_im)


_SEGMENT_ORDER = np.zeros((TC, TC), np.float32)
for _p in range(TC):
    _SEGMENT_ORDER[_p, (_p % SEG) * SEG_LEN + _p // SEG] = 1.0


def _store_tokens(ref, col0, val, tmp_ref):
    for h in range(val.shape[1] // LANES):
        for j in range(SEG_LEN):
            tmp_ref[pl.ds(h * TC + j, SEG, stride=SEG_LEN), :] = val[SEG * j:SEG * (j + 1), h * LANES:(h + 1) * LANES]
        ref[:, col0 + h * LANES:col0 + (h + 1) * LANES] = tmp_ref[pl.ds(h * TC, TC), :]


def _segment_scan(src_re, src_im, dst_re, dst_im, tab_ref, k, carry_re, carry_im, reverse, s_refs=None):
    lam1, lam_seg = (SEG_LEN - 1, 0) if reverse else (0, SEG_LEN - 1)
    token = (lambda i: SEG_LEN - 1 - i) if reverse else (lambda i: i)
    row_id = lax.broadcasted_iota(jnp.int32, (SEG, SCAN_W), 0)
    zero = jnp.zeros((SEG, SCAN_W), F32)
    sums = []
    for lt in range(N_STATE // SCAN_W):
        sl = slice(lt * SCAN_W, (lt + 1) * SCAN_W)
        lr = jnp.broadcast_to(tab_ref[k, 0, lam1:lam1 + 1, sl], (SEG, SCAN_W))
        li = jnp.broadcast_to(tab_ref[k, 1, lam1:lam1 + 1, sl], (SEG, SCAN_W))

        def local(i, c, sl=sl, lr=lr, li=li):
            for step in range(SCAN_UNROLL):
                rows = pl.ds(pl.multiple_of(token(i * SCAN_UNROLL + step) * SEG, SEG), SEG)
                c = (lr * c[0] - li * c[1] + src_re[rows, sl], lr * c[1] + li * c[0] + src_im[rows, sl])
                dst_re[rows, sl] = c[0]
                dst_im[rows, sl] = c[1]
            return c

        er, ei = lax.fori_loop(0, SEG_LEN // SCAN_UNROLL, local, (zero, zero))

        sr_, si_ = tab_ref[k, 0, lam_seg:lam_seg + 1, sl], tab_ref[k, 1, lam_seg:lam_seg + 1, sl]
        c_r, c_i = carry_re[0:1, sl], carry_im[0:1, sl]
        in_r, in_i = zero, zero
        for r in (range(SEG - 1, -1, -1) if reverse else range(SEG)):
            in_r = jnp.where(row_id == r, c_r, in_r)
            in_i = jnp.where(row_id == r, c_i, in_i)
            c_r, c_i = (er[r:r + 1, :] + sr_ * c_r - si_ * c_i, ei[r:r + 1, :] + sr_ * c_i + si_ * c_r)
        carry_re[0:1, sl] = c_r
        carry_im[0:1, sl] = c_i

        def fix(i, c, sl=sl, in_r=in_r, in_i=in_i):
            for step in range(SCAN_UNROLL):
                j = token(i * SCAN_UNROLL + step)
                rows = pl.ds(pl.multiple_of(j * SEG, SEG), SEG)
                pr = jnp.broadcast_to(tab_ref[k, 0, pl.ds(j, 1), sl], (SEG, SCAN_W))
                pi = jnp.broadcast_to(tab_ref[k, 1, pl.ds(j, 1), sl], (SEG, SCAN_W))
                nr = dst_re[rows, sl] + pr * in_r - pi * in_i
                ni = dst_im[rows, sl] + pr * in_i + pi * in_r
                dst_re[rows, sl] = nr
                dst_im[rows, sl] = ni
                if s_refs is not None:
                    sr = s_refs[0][rows, sl]
                    si = s_refs[1][rows, sl]
                    c = (nr, ni, c[2] + c[0] * sr + c[1] * si, c[3] + c[1] * sr - c[0] * si)
            return c

        if s_refs is None:
            lax.fori_loop(0, SEG_LEN // SCAN_UNROLL, fix, 0)
        else:
            out = lax.fori_loop(0, SEG_LEN // SCAN_UNROLL, fix, (in_r, in_i, zero, zero))
            sums.append((jnp.sum(out[2], axis=0, keepdims=True), jnp.sum(out[3], axis=0, keepdims=True)))
    return sums


def _ssm_scan_fwd(u, b_re, b_im, c_re, c_im, tables, k, reverse, comm=None):
    L = u.shape[0]
    nc = L // TC
    chunk = (lambda i: nc - 1 - i) if reverse else (lambda i: i)
    order = jnp.asarray(_SEGMENT_ORDER, BF16)

    def body(u_ref, ord_ref, bre_ref, bim_ref, cre_ref, cim_ref, tab_ref,
             y_ref, sre_ref, sim_ref, in_re, in_im, carry_re, carry_im, tmp_ref):
        @pl.when(pl.program_id(0) == 0)
        def _():
            carry_re[...] = jnp.zeros_like(carry_re)
            carry_im[...] = jnp.zeros_like(carry_im)

        ub = _dot_nn(ord_ref[...], u_ref[...].astype(BF16)).astype(BF16)
        for q in range(N_QUAD):
            qs = slice(q * QUAD, (q + 1) * QUAD)
            us = ub[:, (q // 4) * SLAB:(q // 4 + 1) * SLAB]
            in_re[:, qs] = _dot_nt(us, bre_ref[q])
            in_im[:, qs] = _dot_nt(us, bim_ref[q])
        _segment_scan(in_re, in_im, sre_ref, sim_ref, tab_ref, k, carry_re, carry_im, reverse)
        for j in range(D_SSM // SLAB):
            acc = jnp.zeros((TC, SLAB), F32)
            for q in range(4 * j, 4 * j + 4):
                qs = slice(q * QUAD, (q + 1) * QUAD)
                acc = acc + _dot_nt(sre_ref[:, qs].astype(BF16), cre_ref[q])
                acc = acc - _dot_nt(sim_ref[:, qs].astype(BF16), cim_ref[q])
            _store_tokens(y_ref, j * SLAB, acc, tmp_ref)

    return _hosted_call(
        body, comm, name="ssm_scan_rev" if reverse else "ssm_scan_fwd", grid=(nc,),
        in_specs=[pl.BlockSpec((TC, D_SSM), lambda i: (chunk(i), 1)), _full(order.shape)]
        + [_full(b_re.shape)] * 4 + [_full(tables.shape)],
        out_specs=[pl.BlockSpec((TC, D_SSM), lambda i: (chunk(i), 0)),
                   pl.BlockSpec((TC, N_STATE), lambda i: (chunk(i), 0)),
                   pl.BlockSpec((TC, N_STATE), lambda i: (chunk(i), 0))],
        out_shape=[jax.ShapeDtypeStruct((L, D_SSM), F32), jax.ShapeDtypeStruct((L, N_STATE), F32),
                   jax.ShapeDtypeStruct((L, N_STATE), F32)],
        scratch_shapes=[pltpu.VMEM((TC, N_STATE), F32), pltpu.VMEM((TC, N_STATE), F32),
                        pltpu.VMEM((8, N_STATE), F32), pltpu.VMEM((8, N_STATE), F32),
                        pltpu.VMEM((SLAB // LANES * TC, LANES), F32)],
        args=(u, order, b_re, b_im, c_re, c_im, tables))


def _quad_channels(q):
    c0 = (q // 4) * SLAB + (q % 4) * 4 * SSM_GROUP
    return slice(c0, c0 + 4 * SSM_GROUP)


def _ssm_scan_bwd(dy, u, s_re, s_im, b_re, b_im, c_re, c_im, tables, k, reverse, comm=None):
    L = u.shape[0]
    nc = L // TC
    chunk = (lambda i: nc - 1 - i) if reverse else (lambda i: i)

    order = jnp.asarray(_SEGMENT_ORDER, BF16)

    def body(dy_ref, u_ref, ord_ref, sre_ref, sim_ref, bre_ref, bim_ref, cre_ref, cim_ref, tab_ref,
             du_ref, ob_re, ob_im, oc_re, oc_im, gv_ref,
             a_re, a_im, carry_re, carry_im, gbr_ref, gbi_ref, gcr_ref, gci_ref, tmp_ref):
        @pl.when(pl.program_id(0) == 0)
        def _():
            carry_re[...] = jnp.zeros_like(carry_re)
            carry_im[...] = jnp.zeros_like(carry_im)
            for r in (gbr_ref, gbi_ref, gcr_ref, gci_ref, gv_ref):
                r[...] = jnp.zeros_like(r)

        dyb = _dot_nn(ord_ref[...], dy_ref[...].astype(BF16)).astype(BF16)
        ub = _dot_nn(ord_ref[...], u_ref[...].astype(BF16)).astype(BF16)
        for q in range(N_QUAD):
            qs = slice(q * QUAD, (q + 1) * QUAD)
            ds = dyb[:, (q // 4) * SLAB:(q // 4 + 1) * SLAB]
            a_re[:, qs] = _dot_nn(ds, cre_ref[q])
            a_im[:, qs] = -_dot_nn(ds, cim_ref[q])
            dq = dyb[:, _quad_channels(q)]
            gcr_ref[q] += _dot_tn(dq, sre_ref[:, qs].astype(BF16))
            gci_ref[q] -= _dot_tn(dq, sim_ref[:, qs].astype(BF16))
        sums = _segment_scan(a_re, a_im, a_re, a_im, tab_ref, k, carry_re, carry_im, reverse,
                             s_refs=(sre_ref, sim_ref))
        for lt, (glr, gli) in enumerate(sums):
            sl = slice(lt * SCAN_W, (lt + 1) * SCAN_W)
            gv_ref[0:1, sl] += glr
            gv_ref[1:2, sl] += gli
        for j in range(D_SSM // SLAB):
            us = ub[:, j * SLAB:(j + 1) * SLAB]
            acc = jnp.zeros((TC, SLAB), F32)
            for q in range(4 * j, 4 * j + 4):
                qs = slice(q * QUAD, (q + 1) * QUAD)
                dbr = a_re[:, qs].astype(BF16)
                dbi = a_im[:, qs].astype(BF16)
                uq = ub[:, _quad_channels(q)]
                gbr_ref[q] += _dot_tn(uq, dbr)
                gbi_ref[q] += _dot_tn(uq, dbi)
                acc = acc + _dot_nn(dbr, bre_ref[q]) + _dot_nn(dbi, bim_ref[q])
            _store_tokens(du_ref, j * SLAB, acc, tmp_ref)

        @pl.when(pl.program_id(0) == nc - 1)
        def _():
            for g in range(N_SSM_GROUPS):
                q, gl = divmod(g, 4)
                rows = slice(gl * SSM_GROUP, (gl + 1) * SSM_GROUP)
                cols = slice(gl * SSM_STATE, (gl + 1) * SSM_STATE)
                for out, acc_ref in ((ob_re, gbr_ref), (ob_im, gbi_ref), (oc_re, gcr_ref), (oc_im, gci_ref)):
                    out[g] = acc_ref[q, rows, cols]

    gshape = jax.ShapeDtypeStruct((N_SSM_GROUPS, SSM_GROUP, SSM_STATE), F32)
    compact = pltpu.VMEM((N_QUAD, 4 * SSM_GROUP, QUAD), F32)
    return _hosted_call(
        body, comm, name="ssm_bwd_rev" if reverse else "ssm_bwd_fwd", grid=(nc,),
        in_specs=[pl.BlockSpec((TC, D_SSM), lambda i: (chunk(i), 0)),
                  pl.BlockSpec((TC, D_SSM), lambda i: (chunk(i), 1)), _full(order.shape),
                  pl.BlockSpec((TC, N_STATE), lambda i: (chunk(i), 0)),
                  pl.BlockSpec((TC, N_STATE), lambda i: (chunk(i), 0))]
        + [_full(b_re.shape)] * 4 + [_full(tables.shape)],
        out_specs=[pl.BlockSpec((TC, D_SSM), lambda i: (chunk(i), 0))] + [_full(gshape.shape)] * 4
        + [_full((2, N_STATE))],
        out_shape=[jax.ShapeDtypeStruct((L, D_SSM), F32), gshape, gshape, gshape, gshape,
                   jax.ShapeDtypeStruct((2, N_STATE), F32)],
        scratch_shapes=[pltpu.VMEM((TC, N_STATE), F32), pltpu.VMEM((TC, N_STATE), F32),
                        pltpu.VMEM((8, N_STATE), F32), pltpu.VMEM((8, N_STATE), F32),
                        compact, compact, compact, compact, pltpu.VMEM((SLAB // LANES * TC, LANES), F32)],
        args=(dy, u, order, s_re, s_im, b_re, b_im, c_re, c_im, tables))


def _ssm_post(yf, yb, u, d, glu_w, glu_b):
    y = yf + yb + d * u
    z, t = _gelu(y)
    zb = z.astype(BF16)
    gate = _sigmoid(_dot_nn(zb, glu_w) + glu_b)
    return y, z, t, zb, gate


def _mix_out(yn_pool, yf, yb, u, x, ssm_d, glu_w_b, glu_b, g_ssm, w_out_b, g_ffn):
    L, D = x.shape

    def body(ynp_ref, yf_ref, yb_ref, u_ref, x_ref, d_ref, gw_ref, gb_ref, gs_ref, wo_ref, gf_ref,
             h1_ref, hn_ref, ycat_ref):
        _, z, _, _, gate = _ssm_post(yf_ref[...], yb_ref[...], u_ref[...], d_ref[...], gw_ref[...], gb_ref[...])
        yns, _, _ = _rms_fwd(z * gate, gs_ref[...])
        ynsb = yns.astype(BF16)
        ynp = ynp_ref[...]
        ycat_ref[:, 0:D_POOL] = ynp
        ycat_ref[:, D_POOL:D] = ynsb
        h1 = x_ref[...] + _dot_nn(ynp, wo_ref[0:D_POOL, :]) + _dot_nn(ynsb, wo_ref[D_POOL:D, :])
        h1_ref[...] = h1
        hn, _, _ = _rms_fwd(h1, gf_ref[...])
        hn_ref[...] = hn.astype(BF16)

    half = lambda c: pl.BlockSpec((TL, D_SSM), lambda i: (i, c))
    row = pl.BlockSpec((TL, D), lambda i: (i, 0))
    return pl.pallas_call(
        body, name="mix_out", grid=(L // TL,),
        in_specs=[half(0), half(0), half(0), half(1), row, _full((1, D_SSM)), _full(glu_w_b.shape),
                  _full((1, D_SSM)), _full((1, D_SSM)), _full(w_out_b.shape), _full((1, D))],
        out_specs=[row, row, row],
        out_shape=[jax.ShapeDtypeStruct((L, D), F32), jax.ShapeDtypeStruct((L, D), BF16),
                   jax.ShapeDtypeStruct((L, D), BF16)],
        compiler_params=_cp("parallel"))(yn_pool, yf, yb, u, x, ssm_d, glu_w_b, glu_b, g_ssm, w_out_b, g_ffn)


def _ssm_bwd_local(dh1, yf, yb, u, ssm_d, glu_w_b, glu_b, g_ssm, w_out_b, comm=None):
    L, D = dh1.shape

    def body(dh_ref, yf_ref, yb_ref, u_ref, d_ref, gw_ref, gb_ref, gs_ref, wo_ref,
             dy_ref, du_ref, ggw_ref, ggb_ref, gd_ref, ggs_ref):
        @pl.when(pl.program_id(0) == 0)
        def _():
            for r in (ggw_ref, ggb_ref, gd_ref, ggs_ref):
                r[...] = jnp.zeros_like(r)

        u = u_ref[...]
        d = d_ref[...]
        y, z, t, zb, gate = _ssm_post(yf_ref[...], yb_ref[...], u, d, gw_ref[...], gb_ref[...])
        gs = gs_ref[...]
        _, xh, inv = _rms_fwd(z * gate, gs)
        d_yn = _dot_nt(dh_ref[...], wo_ref[...])
        d_o, dgs = _rms_bwd(d_yn, xh, inv, gs)
        ggs_ref[...] += dgs
        d_zg = d_o * z * gate * (1.0 - gate)
        d_zgb = d_zg.astype(BF16)
        ggb_ref[...] += jnp.sum(d_zg, axis=0, keepdims=True)
        ggw_ref[...] += _dot_tn(zb, d_zgb)
        d_z = d_o * gate + _dot_nt(d_zgb, gw_ref[...])
        d_y = d_z * _gelu_grad(y, t)
        gd_ref[...] += jnp.sum(d_y * u, axis=0, keepdims=True)
        dy_ref[...] = d_y
        du_ref[...] = d_y * d

    half = lambda c: pl.BlockSpec((TL, D_SSM), lambda i: (i, c))
    vec = _full((1, D_SSM))
    return _hosted_call(
        body, comm, name="ssm_bwd_local", grid=(L // TL,),
        in_specs=[pl.BlockSpec((TL, D), lambda i: (i, 0)), half(0), half(0), half(1), vec, _full(glu_w_b.shape),
                  vec, vec, pl.BlockSpec((D_SSM, D), lambda i: (1, 0))],
        out_specs=[half(0), half(0), _full(glu_w_b.shape), vec, vec, vec],
        out_shape=[jax.ShapeDtypeStruct((L, D_SSM), F32), jax.ShapeDtypeStruct((L, D_SSM), F32),
                   jax.ShapeDtypeStruct(glu_w_b.shape, F32)] + [jax.ShapeDtypeStruct((1, D_SSM), F32)] * 3,
        scratch_shapes=[], args=(dh1, yf, yb, u, ssm_d, glu_w_b, glu_b, g_ssm, w_out_b))


def _in_bwd(du_pool, du_a, du_b, du_c, dh1, x, g, w_in_b, comm=None):
    L, D = x.shape

    def body(p_ref, a_ref, b_ref, c_ref, dh_ref, x_ref, g_ref, w_ref, dx_ref, dub_ref, gg_ref):
        @pl.when(pl.program_id(0) == 0)
        def _():
            gg_ref[...] = jnp.zeros_like(gg_ref)

        dub_ref[:, 0:D_POOL] = p_ref[...].astype(BF16)
        dub_ref[:, D_POOL:D] = (a_ref[...] + b_ref[...] + c_ref[...]).astype(BF16)
        d_xn = _dot_nt(dub_ref[...], w_ref[...])
        gv = g_ref[...]
        _, xh, inv = _rms_fwd(x_ref[...], gv)
        dx, dg = _rms_bwd(d_xn, xh, inv, gv)
        gg_ref[...] += dg
        dx_ref[...] = dh_ref[...] + dx

    half = pl.BlockSpec((TL, D_SSM), lambda i: (i, 0))
    row = pl.BlockSpec((TL, D), lambda i: (i, 0))
    return _hosted_call(
        body, comm, name="in_bwd", grid=(L // TL,),
        in_specs=[half, half, half, half, row, row, _full((1, D)), _full(w_in_b.shape)],
        out_specs=[row, row, _full((1, D))],
        out_shape=[jax.ShapeDtypeStruct((L, D), F32), jax.ShapeDtypeStruct((L, D), BF16),
                   jax.ShapeDtypeStruct((1, D), F32)],
        scratch_shapes=[], args=(du_pool, du_a, du_b, du_c, dh1, x, g, w_in_b))


def _ffn_up(hn, w_up4):
    L, D = hn.shape

    def body(h_ref, w_ref, o_ref):
        o_ref[...] = _dot_nn(h_ref[...], w_ref[...]).astype(BF16)

    rows = min(TM, L)
    return pl.pallas_call(
        body, name="ffn_up", grid=(4, L // rows),
        in_specs=[pl.BlockSpec((rows, D), lambda j, i: (i, 0)), pl.BlockSpec((None, D, FF_BLK), lambda j, i: (j, 0, 0))],
        out_specs=pl.BlockSpec((rows, FF_BLK), lambda j, i: (i, j)),
        out_shape=jax.ShapeDtypeStruct((L, 4 * FF_BLK), BF16),
        compiler_params=_cp("parallel", "parallel"))(hn, w_up4)


def _halo_specs_2d(rows, width, L, col, order):
    rb = rows // HALO_B
    last = L // HALO_B - 1
    if order == "ik":
        wrap = lambda f: (lambda i, k: f(i, k))
    else:
        wrap = lambda f: (lambda k, i: f(i, k))
    return [pl.BlockSpec((HALO_B, width), wrap(lambda i, k: (jnp.maximum(i * rb - 1, 0), col(k)))),
            pl.BlockSpec((rows, width), wrap(lambda i, k: (i, col(k)))),
            pl.BlockSpec((HALO_B, width), wrap(lambda i, k: (jnp.minimum((i + 1) * rb, last), col(k))))]


def _neighbours(x, prev_ref, next_ref, cs, i, n):
    rows = x.shape[0]
    row = lax.broadcasted_iota(jnp.int32, (rows, 1), 0)
    before = jnp.where(i > 0, prev_ref[:, cs].astype(F32)[HALO_B - 1:HALO_B, :], 0.0)
    after = jnp.where(i < n - 1, next_ref[:, cs].astype(F32)[0:1, :], 0.0)
    xf = x.astype(F32)
    return (jnp.where(row == 0, before, pltpu.roll(xf, 1, 0)),
            jnp.where(row == rows - 1, after, pltpu.roll(xf, rows - 1, 0)))


def _conv3(x, before, after, w, b):
    return before * w[0:1, :] + x.astype(F32) * w[1:2, :] + after * w[2:3, :] + b


def _col_chunks(width, size=256):
    return [slice(c, min(c + size, width)) for c in range(0, width, size)]


def _ffn_down_loss(up, conv_w, conv_b, w_down_b, h1, target, g_final):
    L, D = h1.shape
    n = L // TF
    nk = D_FF // FF_BLK

    def body(vp, vc, vn, gp, gc, gn, wv_ref, wg_ref, bv_ref, bg_ref, wd_ref, h1_ref, t_ref, gf_ref,
             a_ref, cv_ref, cg_ref, dh2_ref, dh2b_ref, loss_ref, gg_ref, acc_ref):
        i = pl.program_id(0)
        k = pl.program_id(1)

        @pl.when((i == 0) & (k == 0))
        def _():
            loss_ref[...] = jnp.zeros_like(loss_ref)
            gg_ref[...] = jnp.zeros_like(gg_ref)

        @pl.when(k == 0)
        def _():
            acc_ref[...] = jnp.zeros_like(acc_ref)

        acc = jnp.zeros((TF, D), F32)
        for cs in _col_chunks(FF_BLK):
            xv, xg = vc[:, cs], gc[:, cs]
            val = _conv3(xv, *_neighbours(xv, vp, vn, cs, i, n), wv_ref[:, cs], bv_ref[:, cs])
            gate = _conv3(xg, *_neighbours(xg, gp, gn, cs, i, n), wg_ref[:, cs], bg_ref[:, cs])
            a = (val * (gate * _sigmoid(gate))).astype(BF16)
            a_ref[:, cs] = a
            cv_ref[:, cs] = val.astype(BF16)
            cg_ref[:, cs] = gate.astype(BF16)
            rows = pl.ds(pl.multiple_of(k * FF_BLK + cs.start, LANES), cs.stop - cs.start)
            acc = acc + _dot_nn(a, wd_ref[rows, :])
        acc_ref[...] += acc

        @pl.when(k == nk - 1)
        def _():
            gf = gf_ref[...]
            y, xh, inv = _rms_fwd(h1_ref[...] + acc_ref[...], gf)
            diff = y - t_ref[...]
            part = 0.5 * jnp.sum(jnp.mean(diff * diff, axis=-1, keepdims=True), axis=0, keepdims=True)
            loss_ref[...] += jnp.broadcast_to(part, loss_ref.shape)
            dx, dg = _rms_bwd(diff * (1.0 / D), xh, inv, gf)
            gg_ref[...] += dg
            dh2_ref[...] = dx
            dh2b_ref[...] = dx.astype(BF16)

    row = pl.BlockSpec((TF, D), lambda i, k: (i, 0))
    cw = lambda off: pl.BlockSpec((3, FF_BLK), lambda i, k: (0, k + off))
    cb = lambda off: pl.BlockSpec((1, FF_BLK), lambda i, k: (0, k + off))
    return pl.pallas_call(
        body, name="ffn_down_loss", grid=(n, nk),
        in_specs=_halo_specs_2d(TF, FF_BLK, L, lambda k: k, "ik") + _halo_specs_2d(TF, FF_BLK, L, lambda k: k + nk, "ik")
        + [cw(0), cw(nk), cb(0), cb(nk), _full(w_down_b.shape), row, row, _full((1, D))],
        out_specs=[pl.BlockSpec((TF, FF_BLK), lambda i, k: (i, k))] * 3 + [row, row, _full((1, LANES)), _full((1, D))],
        out_shape=[jax.ShapeDtypeStruct((L, D_FF), BF16)] * 3
        + [jax.ShapeDtypeStruct((L, D), F32), jax.ShapeDtypeStruct((L, D), BF16),
           jax.ShapeDtypeStruct((1, LANES), F32), jax.ShapeDtypeStruct((1, D), F32)],
        scratch_shapes=[pltpu.VMEM((TF, D), F32)],
        compiler_params=_cp("arbitrary", "arbitrary"))(
            up, up, up, up, up, up, conv_w, conv_w, conv_b, conv_b, w_down_b, h1, target, g_final)


def _ffn_act_bwd(c_val, c_gate, w_down_b, dh2):
    L, D = dh2.shape
    n = L // TL
    nk = D_FF // FF_BLK

    def body(v_ref, g_ref, wd_ref, dh_ref, dv_ref, dg_ref, gbv_ref, gbg_ref):
        @pl.when(pl.program_id(1) == 0)
        def _():
            gbv_ref[...] = jnp.zeros_like(gbv_ref)
            gbg_ref[...] = jnp.zeros_like(gbg_ref)

        dh = dh_ref[...]
        for cs in _col_chunks(FF_BLK):
            val, gate = v_ref[:, cs].astype(F32), g_ref[:, cs].astype(F32)
            d_a = _dot_nt(dh, wd_ref[cs, :])
            sg = _sigmoid(gate)
            d_val = d_a * (gate * sg)
            d_gate = d_a * val * (sg * (1.0 + gate * (1.0 - sg)))
            dv_ref[:, cs] = d_val.astype(BF16)
            dg_ref[:, cs] = d_gate.astype(BF16)
            gbv_ref[:, cs] += jnp.sum(d_val, axis=0, keepdims=True)
            gbg_ref[:, cs] += jnp.sum(d_gate, axis=0, keepdims=True)

    blk = pl.BlockSpec((TL, FF_BLK), lambda k, i: (i, k))
    acc = pl.BlockSpec((1, FF_BLK), lambda k, i: (0, k))
    return pl.pallas_call(
        body, name="ffn_act_bwd", grid=(nk, n),
        in_specs=[blk, blk, pl.BlockSpec((FF_BLK, D), lambda k, i: (k, 0)), pl.BlockSpec((TL, D), lambda k, i: (i, 0))],
        out_specs=[blk, blk, acc, acc],
        out_shape=[jax.ShapeDtypeStruct((L, D_FF), BF16), jax.ShapeDtypeStruct((L, D_FF), BF16),
                   jax.ShapeDtypeStruct((1, D_FF), F32), jax.ShapeDtypeStruct((1, D_FF), F32)],
        compiler_params=_cp("arbitrary", "arbitrary"))(c_val, c_gate, w_down_b, dh2)


def _ffn_up_bwd(d_val, d_gate, up, conv_w, w_up4, h1, dh2, g_ffn):
    L, D = h1.shape
    n = L // TF
    nk = D_FF // FF_BLK

    def body(vp, vc, vn, gp, gc, gn, uv_ref, ug_ref, wv_ref, wg_ref, wu_ref, h1_ref, dh2_ref, g_ref,
             dup_ref, dh1_ref, dh1b_ref, gg_ref, gcw_ref, acc_ref):
        i = pl.program_id(0)
        k = pl.program_id(1)

        @pl.when((i == 0) & (k == 0))
        def _():
            gg_ref[...] = jnp.zeros_like(gg_ref)
            gcw_ref[...] = jnp.zeros_like(gcw_ref)

        @pl.when(k == 0)
        def _():
            acc_ref[...] = jnp.zeros_like(acc_ref)

        acc = jnp.zeros((TF, D), F32)
        for j, (blocks, u_ref, w_ref) in enumerate((((vp, vc, vn), uv_ref, wv_ref), ((gp, gc, gn), ug_ref, wg_ref))):
            for cs in _col_chunks(FF_BLK):
                d = blocks[1][:, cs]
                before, after = _neighbours(d, blocks[0], blocks[2], cs, i, n)
                taps = (after, d.astype(F32), before)
                w = w_ref[:, cs]
                d_up = (taps[0] * w[0:1, :] + taps[1] * w[1:2, :] + taps[2] * w[2:3, :]).astype(BF16)
                dup_ref[j, :, cs] = d_up
                acc = acc + _dot_nt(d_up, wu_ref[k + j * nk, :, cs])
                x = u_ref[:, cs].astype(F32)
                for r in range(3):
                    gcw_ref[j, k, r:r + 1, cs] += jnp.sum(taps[r] * x, axis=0, keepdims=True)
        acc_ref[...] += acc

        @pl.when(k == nk - 1)
        def _():
            g = g_ref[...]
            _, xh, inv = _rms_fwd(h1_ref[...], g)
            dx, dg = _rms_bwd(acc_ref[...], xh, inv, g)
            gg_ref[...] += dg
            dh1 = dh2_ref[...] + dx
            dh1_ref[...] = dh1
            dh1b_ref[...] = dh1.astype(BF16)

    row = pl.BlockSpec((TF, D), lambda i, k: (i, 0))
    cw = lambda off: pl.BlockSpec((3, FF_BLK), lambda i, k: (0, k + off))
    tile = lambda off: pl.BlockSpec((TF, FF_BLK), lambda i, k: (i, k + off))
    return pl.pallas_call(
        body, name="ffn_up_bwd", grid=(n, nk),
        in_specs=_halo_specs_2d(TF, FF_BLK, L, lambda k: k, "ik") + _halo_specs_2d(TF, FF_BLK, L, lambda k: k, "ik")
        + [tile(0), tile(nk), cw(0), cw(nk), _full(w_up4.shape), row, row, _full((1, D))],
        out_specs=[pl.BlockSpec((2, None, TF, FF_BLK), lambda i, k: (0, k, i, 0)), row, row, _full((1, D)),
                   _full((2, nk, 3, FF_BLK))],
        out_shape=[jax.ShapeDtypeStruct((2, nk, L, FF_BLK), BF16), jax.ShapeDtypeStruct((L, D), F32),
                   jax.ShapeDtypeStruct((L, D), BF16), jax.ShapeDtypeStruct((1, D), F32),
                   jax.ShapeDtypeStruct((2, nk, 3, FF_BLK), F32)],
        scratch_shapes=[pltpu.VMEM((TF, D), F32)],
        compiler_params=_cp("arbitrary", "arbitrary"))(
            d_val, d_val, d_val, d_gate, d_gate, d_gate, up, up, conv_w, conv_w, w_up4, h1, dh2, g_ffn)


def _matmul_tn(a, b, tm, tn, name, tk=2048):
    L, M = a.shape
    N = b.shape[1]
    tk = min(tk, L)

    def body(a_ref, b_ref, o_ref):
        @pl.when(pl.program_id(2) == 0)
        def _():
            o_ref[...] = jnp.zeros_like(o_ref)

        o_ref[...] += _dot_tn(a_ref[...], b_ref[...])

    return pl.pallas_call(
        body, name=name, grid=(M // tm, N // tn, L // tk),
        in_specs=[pl.BlockSpec((tk, tm), lambda m, n, l: (l, m)), pl.BlockSpec((tk, tn), lambda m, n, l: (l, n))],
        out_specs=pl.BlockSpec((tm, tn), lambda m, n, l: (m, n)),
        out_shape=jax.ShapeDtypeStruct((M, N), F32),
        compiler_params=_cp("parallel", "parallel", "arbitrary"))(a, b)


def _matmul_tn_blocks(a, b, tm, name, tk=2048):
    L, M = a.shape
    J, _, N = b.shape
    tk = min(tk, L)

    def body(a_ref, b_ref, o_ref):
        @pl.when(pl.program_id(2) == 0)
        def _():
            o_ref[...] = jnp.zeros_like(o_ref)

        o_ref[...] += _dot_tn(a_ref[...], b_ref[...])

    return pl.pallas_call(
        body, name=name, grid=(M // tm, J, L // tk),
        in_specs=[pl.BlockSpec((tk, tm), lambda m, j, l: (l, m)), pl.BlockSpec((None, tk, N), lambda m, j, l: (j, l, 0))],
        out_specs=pl.BlockSpec((None, tm, N), lambda m, j, l: (j, m, 0)),
        out_shape=jax.ShapeDtypeStruct((J, M, N), F32),
        compiler_params=_cp("parallel", "parallel", "arbitrary"))(a, b)


def _row_tile(rows):
    for t in (512, 352, 256, 128, 64, 8):
        if rows % t == 0:
            return t
    return rows


def _add_half(g, r, c_arr, name, out_dtype=F32):
    _, _, R, C = g.shape
    tr = _row_tile(R)

    def body(c_ref, g_ref, r_ref, o_ref):
        o_ref[...] = (g_ref[...] + r_ref[...]).astype(out_dtype)

    return pl.pallas_call(
        body, name=name,
        grid_spec=pltpu.PrefetchScalarGridSpec(
            num_scalar_prefetch=1, grid=(g.shape[0], R // tr),
            in_specs=[pl.BlockSpec((None, None, tr, C), lambda j, i, c: (j, c[0], i, 0)),
                      pl.BlockSpec((None, tr, C), lambda j, i, c: (j, i, 0))],
            out_specs=pl.BlockSpec((None, tr, C), lambda j, i, c: (j, i, 0))),
        out_shape=jax.ShapeDtypeStruct(r.shape, out_dtype),
        compiler_params=_cp("parallel", "parallel"))(c_arr, g, r)


def _sum4(p, name):
    _, R, C = p.shape
    tr = _row_tile(R)

    def body(p_ref, o_ref):
        q = [p_ref[j].astype(F32) for j in range(4)]
        o_ref[...] = ((q[0] + q[1]) + q[2]) + q[3]

    return pl.pallas_call(
        body, name=name, grid=(R // tr,),
        in_specs=[pl.BlockSpec((4, tr, C), lambda i: (0, i, 0))],
        out_specs=pl.BlockSpec((tr, C), lambda i: (i, 0)),
        out_shape=jax.ShapeDtypeStruct((R, C), F32), compiler_params=_cp("parallel"))(p)


def _adamw_refs(w_ref, g_ref, m_ref, v_ref, d_ref, nm_ref, nv_ref):
    gv = g_ref[...]
    nm = ADAM_B1 * m_ref[...] + (1.0 - ADAM_B1) * gv
    nv = ADAM_B2 * v_ref[...] + (1.0 - ADAM_B2) * (gv * gv)
    m_hat = nm / (1.0 - ADAM_B1 ** ADAM_STEP)
    v_hat = nv / (1.0 - ADAM_B2 ** ADAM_STEP)
    d_ref[...] = -ADAM_LR * (m_hat / (jnp.sqrt(v_hat) + ADAM_EPS) + ADAM_WD * w_ref[...])
    nm_ref[...] = nm
    nv_ref[...] = nv


def _adamw_many(ws, gs, ms, vs, name):
    n = len(ws)

    def body(*refs):
        for k in range(n):
            _adamw_refs(*(refs[j * n + k] for j in range(7)))

    out_shape = [jax.ShapeDtypeStruct(w.shape, F32) for w in ws] * 3
    res = pl.pallas_call(body, name=name, out_shape=out_shape,
                         compiler_params=pltpu.CompilerParams(vmem_limit_bytes=VMEM_LIMIT))(*ws, *gs, *ms, *vs)
    return res[:n], res[n:2 * n], res[2 * n:]


def _join_rows(own, other, c_arr, name):
    R, C = own.shape
    tr = _row_tile(R)

    def body(c_ref, own_ref, other_ref, o_ref):
        o_ref[...] = jnp.where(pl.program_id(0) == c_ref[0], own_ref[...], other_ref[...])

    half = pl.BlockSpec((tr, C), lambda h, i, c: (i, 0))
    return pl.pallas_call(
        body, name=name,
        grid_spec=pltpu.PrefetchScalarGridSpec(
            num_scalar_prefetch=1, grid=(2, R // tr), in_specs=[half, half],
            out_specs=pl.BlockSpec((tr, C), lambda h, i, c: (h * (R // tr) + i, 0))),
        out_shape=jax.ShapeDtypeStruct((2 * R, C), F32),
        compiler_params=_cp("parallel", "parallel"))(c_arr, own, other)


def _adamw_halves(w, own, other, m, v, name, comm=None):
    R, C = own.shape
    tr = _row_tile(R)
    while tr * C * 4 > ADAMW_BLOCK_BYTES and tr % 16 == 0:
        tr //= 2

    def body(w_ref, own_ref, other_ref, m_ref, v_ref, g_ref, d_ref, nm_ref, nv_ref):
        g_ref[...] = jnp.where(pl.program_id(0) == lax.axis_index("c"), own_ref[...], other_ref[...])
        _adamw_refs(w_ref, g_ref, m_ref, v_ref, d_ref, nm_ref, nv_ref)

    half = pl.BlockSpec((tr, C), lambda h, i: (i, 0))
    full = pl.BlockSpec((tr, C), lambda h, i: (h * (R // tr) + i, 0))
    sh = jax.ShapeDtypeStruct((2 * R, C), F32)
    return _hosted_call(body, comm, name=name, grid=(2, R // tr), in_specs=[full, half, half, full, full],
                        out_specs=[full] * 4, out_shape=[sh] * 4, scratch_shapes=[], args=(w, own, other, m, v))


_ANY = pl.BlockSpec(memory_space=pl.ANY)


def _position():
    return lax.axis_index("x"), lax.axis_index("y"), lax.axis_index("c")


class _Comm:
    def __init__(self, arrs, out_shape, sems, start, finish):
        self.arrs, self.out_shape, self.sems, self.start, self.finish = arrs, out_shape, sems, start, finish


def _comm_call(comm, name):
    n, m = len(comm.arrs), len(comm.out_shape)

    def body(*refs):
        ins, outs, sems = refs[:n], refs[n:n + m], refs[n + m:]
        comm.start(ins, outs, sems)
        comm.finish(ins, outs, sems)

    return pl.pallas_call(
        body, name=name, in_specs=[_ANY] * n, out_specs=[_ANY] * m, out_shape=comm.out_shape,
        scratch_shapes=comm.sems, compiler_params=pltpu.CompilerParams(has_side_effects=True))(*comm.arrs)


def _hosted_call(body, comm, *, name, grid, in_specs, out_specs, out_shape, scratch_shapes, args):
    sem = ("arbitrary",) * len(grid)
    if comm is None:
        return pl.pallas_call(body, name=name, grid=grid, in_specs=in_specs, out_specs=out_specs, out_shape=out_shape,
                              scratch_shapes=scratch_shapes, compiler_params=_cp(*sem))(*args), []
    n_in, n_out, n_scr = len(in_specs), len(out_specs), len(scratch_shapes)
    ci, co = len(comm.arrs), len(comm.out_shape)

    def full(*refs):
        ins, refs = refs[:n_in], refs[n_in:]
        cins, refs = refs[:ci], refs[ci:]
        outs, refs = refs[:n_out], refs[n_out:]
        couts, refs = refs[:co], refs[co:]
        scr, csems = refs[:n_scr], refs[n_scr:]
        first, last = True, True
        for d, size in enumerate(grid):
            first = first & (pl.program_id(d) == 0)
            last = last & (pl.program_id(d) == size - 1)

        @pl.when(first)
        def _():
            comm.start(cins, couts, csems)

        body(*ins, *outs, *scr)

        @pl.when(last)
        def _():
            comm.finish(cins, couts, csems)

    res = pl.pallas_call(
        full, name=name, grid=grid, in_specs=list(in_specs) + [_ANY] * ci, out_specs=list(out_specs) + [_ANY] * co,
        out_shape=list(out_shape) + list(comm.out_shape), scratch_shapes=list(scratch_shapes) + list(comm.sems),
        compiler_params=_cp(*sem))(*args, *comm.arrs)
    return res[:n_out], res[n_out:]


def _comm_join(*comms):
    def parts(xs, attr):
        out, at = [], 0
        for cm in comms:
            n = len(getattr(cm, attr))
            out.append(xs[at:at + n])
            at += n
        return out

    def start(ins, outs, sems):
        for cm, i, o, s in zip(comms, parts(ins, "arrs"), parts(outs, "out_shape"), parts(sems, "sems")):
            cm.start(i, o, s)

    def finish(ins, outs, sems):
        for cm, i, o, s in zip(comms, parts(ins, "arrs"), parts(outs, "out_shape"), parts(sems, "sems")):
            cm.finish(i, o, s)

    cat = lambda attr: [x for cm in comms for x in getattr(cm, attr)]
    return _Comm(cat("arrs"), cat("out_shape"), cat("sems"), start, finish)


def _dma_sems(*counts):
    return [pltpu.SemaphoreType.DMA((n,)) for n in counts]


def _comm_pair_swap(arrs, half=False):
    n = len(arrs)
    out_shape = [jax.ShapeDtypeStruct(a.shape[:1] + a.shape[2:] if half else a.shape, a.dtype) for a in arrs]

    def copies(ins, outs, sems):
        x, y, c = _position()
        return [pltpu.make_async_remote_copy(
            src_ref=ins[k].at[:, 1 - c] if half else ins[k], dst_ref=outs[k], send_sem=sems[0].at[k],
            recv_sem=sems[1].at[k], device_id=(x, y, 1 - c), device_id_type=MESH) for k in range(n)]

    def start(ins, outs, sems):
        for cp in copies(ins, outs, sems):
            cp.start()

    def finish(ins, outs, sems):
        for cp in copies(ins, outs, sems):
            cp.wait()

    return _Comm(arrs, out_shape, _dma_sems(n, n), start, finish)


def _chip_of(j, c):
    return (jnp.right_shift(j, 1), jnp.bitwise_and(j, 1), c)


def _comm_chip_exchange(arrs, scatter):
    n = len(arrs)
    out_shape = [jax.ShapeDtypeStruct(a.shape if scatter else (4,) + a.shape, a.dtype) for a in arrs]

    def copies(ins, outs, sems):
        x, y, c = _position()
        me = 2 * x + y
        local, sent, landed = [], [], []
        for k in range(n):
            local.append(pltpu.make_async_copy(ins[k].at[me] if scatter else ins[k], outs[k].at[me], sems[2].at[k]))
            for d in (1, 2, 3):
                j = jnp.bitwise_xor(me, d)
                s = 3 * k + d - 1
                src = ins[k].at[j] if scatter else ins[k]
                for dst, group in ((outs[k].at[me], sent), (outs[k].at[j], landed)):
                    group.append(pltpu.make_async_remote_copy(
                        src_ref=src, dst_ref=dst, send_sem=sems[0].at[s], recv_sem=sems[1].at[s],
                        device_id=_chip_of(j, c), device_id_type=MESH))
        return local, sent, landed

    def start(ins, outs, sems):
        local, sent, _ = copies(ins, outs, sems)
        for cp in local + sent:
            cp.start()

    def finish(ins, outs, sems):
        local, sent, landed = copies(ins, outs, sems)
        for cp in sent:
            cp.wait_send()
        for cp in landed:
            cp.wait_recv()
        for cp in local:
            cp.wait()

    return _Comm(arrs, out_shape, _dma_sems(3 * n, 3 * n, n), start, finish)


LOCAL_PARTS = 4


def _comm_gather_split(shards, whole):
    n, nw = len(shards), len(whole)
    arrs = list(shards) + list(whole)
    out_shape = [jax.ShapeDtypeStruct((4,) + a.shape, a.dtype) for a in arrs]

    def copies(ins, outs, sems):
        x, y, c = _position()
        me = 2 * x + y
        local, sent, landed, passed, passed_in = [], [], [], [], []
        for k in range(n + nw):
            if k >= n:
                local.append(pltpu.make_async_copy(ins[k], outs[k].at[me], sems[4].at[LOCAL_PARTS * k]))
            else:
                part = shards[k].shape[0] // LOCAL_PARTS
                for r in range(LOCAL_PARTS):
                    local.append(pltpu.make_async_copy(ins[k].at[pl.ds(r * part, part)],
                                                       outs[k].at[me, pl.ds(r * part, part)],
                                                       sems[4].at[LOCAL_PARTS * k + r]))
            for d in (1, 2, 3):
                j = jnp.bitwise_xor(me, d)
                s = 3 * k + d - 1
                if k >= n:
                    src, mine, theirs = ins[k], outs[k].at[me], outs[k].at[j]
                else:
                    h = shards[k].shape[0] // 2
                    rows = pl.ds(pl.multiple_of(c * h, 16), h)
                    other = pl.ds(pl.multiple_of((1 - c) * h, 16), h)
                    src, mine, theirs = ins[k].at[rows], outs[k].at[me, rows], outs[k].at[j, rows]
                    for dst, group in ((theirs, passed), (outs[k].at[j, other], passed_in)):
                        group.append(pltpu.make_async_remote_copy(
                            src_ref=theirs, dst_ref=dst, send_sem=sems[2].at[s], recv_sem=sems[3].at[s],
                            device_id=(x, y, 1 - c), device_id_type=MESH))
                for dst, group in ((mine, sent), (theirs, landed)):
                    group.append(pltpu.make_async_remote_copy(
                        src_ref=src, dst_ref=dst, send_sem=sems[0].at[s], recv_sem=sems[1].at[s],
                        device_id=_chip_of(j, c), device_id_type=MESH))
        return local, sent, landed, passed, passed_in

    def start(ins, outs, sems):
        local, sent, _, _, _ = copies(ins, outs, sems)
        for cp in local + sent:
            cp.start()

    def finish(ins, outs, sems):
        local, sent, landed, passed, passed_in = copies(ins, outs, sems)
        for cp in landed[:3 * n]:
            cp.wait_recv()
        for cp in passed:
            cp.start()
        for cp in landed[3 * n:]:
            cp.wait_recv()
        for cp in sent:
            cp.wait_send()
        for cp in passed:
            cp.wait_send()
        for cp in passed_in:
            cp.wait_recv()
        for cp in local:
            cp.wait()

    t = 3 * (n + nw)
    return _Comm(arrs, out_shape, _dma_sems(t, t, max(3 * n, 1), max(3 * n, 1), LOCAL_PARTS * (n + nw)), start, finish)


def _pack(arrs, row_multiple):
    parts = []
    for a in arrs:
        flat = a.reshape(-1).astype(F32)
        pad = (-flat.shape[0]) % LANES
        parts.append(jnp.pad(flat, (0, pad)) if pad else flat)
    flat = jnp.concatenate(parts)
    rows = -(-flat.shape[0] // LANES)
    rows_p = -(-rows // row_multiple) * row_multiple
    return jnp.pad(flat, (0, rows_p * LANES - flat.shape[0])).reshape(rows_p, LANES)


def _unpack(packed, shapes):
    flat = packed.reshape(-1)
    outs, off = [], 0
    for sh in shapes:
        size = int(np.prod(sh))
        outs.append(flat[off:off + size].reshape(sh))
        off += size + (-size) % LANES
    return outs


SMALL = ["norm_mix_g", "pool_w", "pool_scale", "ssm_log_neg_a_re", "ssm_a_im", "ssm_log_dt", "ssm_b_re", "ssm_b_im",
         "ssm_c_re", "ssm_c_im", "ssm_d", "glu_b", "out_norm_pool_g", "out_norm_ssm_g", "norm_ffn_g", "conv_b",
         "final_norm_g"]
BIG = ["w_in", "glu_w", "w_out", "w_up", "w_down"]
WIDE = ["pool_w", "ssm_b_re", "ssm_b_im", "ssm_c_re", "ssm_c_im"]
WEIGHTS = ['norm_mix_g', 'w_in', 'pool_w', 'pool_scale', 'ssm_log_neg_a_re', 'ssm_a_im', 'ssm_log_dt', 'ssm_b_re',
           'ssm_b_im', 'ssm_c_re', 'ssm_c_im', 'ssm_d', 'glu_w', 'glu_b', 'out_norm_pool_g', 'out_norm_ssm_g', 'w_out',
           'norm_ffn_g', 'w_up', 'conv_w', 'conv_b', 'w_down', 'final_norm_g']


def _local_step(x, target, p, full, shards=None, c_arr=None):
    L, D = x.shape
    dist = shards is not None
    row = lambda a: a.reshape(1, -1)
    w_in = full["w_in"]
    pool_w_b = p["pool_w"].astype(BF16)
    g_mix, g_pool, g_ssm, g_ffn, g_fin = (row(p[k]) for k in (
        "norm_mix_g", "out_norm_pool_g", "out_norm_ssm_g", "norm_ffn_g", "final_norm_g"))
    pool_scale, ssm_d, glu_b, conv_b = (row(p[k]) for k in ("pool_scale", "ssm_d", "glu_b", "conv_b"))

    lnar = p["ssm_log_neg_a_re"].reshape(2 * N_SSM_GROUPS, SSM_STATE)
    aim = p["ssm_a_im"].reshape(2 * N_SSM_GROUPS, SSM_STATE)
    ldt = jnp.broadcast_to(p["ssm_log_dt"].reshape(2 * N_SSM_GROUPS, 1), lnar.shape)
    lam_re, lam_im, f_re, f_im = _ssm_params(lnar, aim, ldt)
    flat2 = lambda a: a.reshape(2, N_STATE)
    lam4 = jnp.stack([flat2(lam_re)[0], flat2(lam_im)[0], flat2(lam_re)[1], flat2(lam_im)[1]])
    tables = _scan_tables(lam4)
    per_group = (2, N_SSM_GROUPS, SSM_STATE)
    dense = _ssm_expand(p["ssm_b_re"], p["ssm_b_im"], p["ssm_c_re"], p["ssm_c_im"],
                        f_re.reshape(per_group + (1,)), f_im.reshape(per_group + (1,)))
    ssm_args = [tuple(dense[4 * d:4 * d + 4]) + (tables,) for d in range(2)]

    u, xn = _in_proj(x, g_mix, w_in)
    yn_pool = _pool_fwd(u, pool_w_b, pool_scale, g_pool)
    gather1 = _comm_gather_split([shards[k] for k in ("glu_w", "w_out", "w_down")], [shards["conv_w"]]) if dist else None
    (y0, s0r, s0i), got1 = _ssm_scan_fwd(u, *ssm_args[0], 0, False, comm=gather1)
    gather2 = _comm_gather_split([shards["w_up"]], []) if dist else None
    (y1, s1r, s1i), got2 = _ssm_scan_fwd(u, *ssm_args[1], 2, True, comm=gather2)
    if dist:
        glu_w, w_out, w_down = (g.reshape((-1,) + g.shape[2:]) for g in got1[:3])
        conv_w = jnp.transpose(got1[3], (1, 0, 2)).reshape(3, -1)
        w_up4 = got2[0]
    else:
        glu_w, w_out, w_up4, w_down, conv_w = (full[k] for k in ("glu_w", "w_out", "w_up", "w_down", "conv_w"))
    h1, hn, ycat = _mix_out(yn_pool, y0, y1, u, x, ssm_d, glu_w, glu_b, g_ssm, w_out, g_ffn)
    up = _ffn_up(hn, w_up4)
    a, c_val, c_gate, dh2, dh2_b, loss, g_final = _ffn_down_loss(up, conv_w, conv_b, w_down, h1, target, g_fin)

    d_val, d_gate, gbv, gbg = _ffn_act_bwd(c_val, c_gate, w_down, dh2_b)
    g_w_down = _matmul_tn(a, dh2_b, FF_BLK, D, "grad_w_down")
    d_up, dh1, dh1_b, g_ffn_g, gcw = _ffn_up_bwd(d_val, d_gate, up, conv_w, w_up4, h1, dh2, g_ffn)
    g_w_up = _matmul_tn_blocks(hn, d_up.reshape(4, L, FF_BLK), TM, "grad_w_up")
    g_w_out = _matmul_tn(ycat, dh1_b, TM, D, "grad_w_out")
    late = ("w_up", "w_down", "w_out", "glu_w")
    halves = [g_w_up.reshape(4, 2, D // 2, FF_BLK), g_w_down.reshape(4, 2, D_FF // 8, D)]
    (dy, du_direct, g_glu_w, g_glu_b, g_ssm_d, g_ssm_g), swapped = _ssm_bwd_local(
        dh1_b, y0, y1, u, ssm_d, glu_w, glu_b, g_ssm, w_out, comm=_comm_pair_swap(halves, half=True) if dist else None)
    more = [g_w_out.reshape(4, 2, D // 8, D), g_glu_w.reshape(4, 2, D_SSM // 8, D_SSM)]
    (d_pooled, g_pool_w, g_pool_scale, g_pool_g), swapped_more = _pool_bwd_local(
        dh1_b, u, w_out, pool_w_b, pool_scale, g_pool, comm=_comm_pair_swap(more, half=True) if dist else None)
    halves, from_sibling = halves + more, list(swapped) + list(swapped_more)
    du_pool = _pool_bwd_window(d_pooled)
    reduce_a, reduce_b = None, None
    if dist:
        chip_sums = [_add_half(h, r, c_arr, "sum_pair_" + k, BF16) for k, h, r in zip(late, halves, from_sibling)]
        reduce_a = _comm_chip_exchange(chip_sums[:1], scatter=True)
        reduce_b = _comm_chip_exchange(chip_sums[1:], scatter=True)
    (du0, gb0r, gb0i, gc0r, gc0i, gv0), chips_a = _ssm_scan_bwd(dy, u, s0r, s0i, *ssm_args[0], 1, True, comm=reduce_a)
    (du1, gb1r, gb1i, gc1r, gc1i, gv1), chips_b = _ssm_scan_bwd(dy, u, s1r, s1i, *ssm_args[1], 3, False, comm=reduce_b)
    mine = [_sum4(r, "sum_chips_" + k) for k, r in zip(late, list(chips_a) + list(chips_b))]
    by_state = (2, N_SSM_GROUPS, 1, SSM_STATE)
    g_b_re, g_b_im, g_f_re, g_f_im = _ssm_unfold(
        jnp.stack([gb0r, gb1r]), jnp.stack([gb0i, gb1i]),
        jnp.swapaxes(p["ssm_b_re"], 2, 3), jnp.swapaxes(p["ssm_b_im"], 2, 3),
        f_re.reshape(by_state), f_im.reshape(by_state))
    gvec = lambda j: jnp.stack([gv0[j], gv1[j]]).reshape(2 * N_SSM_GROUPS, SSM_STATE)
    g_lnar, g_aim, g_ldt = _ssm_params_bwd(lnar, aim, ldt, gvec(0), gvec(1),
                                           g_f_re.reshape(lnar.shape), g_f_im.reshape(lnar.shape))
    (grad_x, d_u_b, g_mix_g), theirs = _in_bwd(du_pool, du_direct, du0, du1, dh1, x, g_mix, w_in,
                                               comm=_comm_pair_swap(mine) if dist else None)
    g_w_in = _matmul_tn(xn, d_u_b, TM, D, "grad_w_in")

    small = {
        "norm_mix_g": g_mix_g, "pool_w": g_pool_w, "pool_scale": g_pool_scale,
        "ssm_log_neg_a_re": g_lnar, "ssm_a_im": g_aim, "ssm_log_dt": g_ldt,
        "ssm_b_re": jnp.swapaxes(g_b_re, 2, 3), "ssm_b_im": jnp.swapaxes(g_b_im, 2, 3),
        "ssm_c_re": jnp.stack([gc0r, gc1r]), "ssm_c_im": jnp.stack([gc0i, gc1i]),
        "ssm_d": g_ssm_d, "glu_b": g_glu_b, "out_norm_pool_g": g_pool_g, "out_norm_ssm_g": g_ssm_g,
        "norm_ffn_g": g_ffn_g, "conv_b": jnp.concatenate([gbv[0], gbg[0]]), "final_norm_g": g_final,
        "conv_w": jnp.transpose(gcw, (2, 0, 1, 3)).reshape(3, -1),
    }
    big = {"w_in": g_w_in}
    reduced = dict(zip(late, zip(mine, theirs)))
    if not dist:
        big.update({"w_up": g_w_up, "w_down": g_w_down, "w_out": g_w_out, "glu_w": g_glu_w})
    return loss, grad_x, small, big, reduced


def kernel(x, norm_mix_g, w_in, pool_w, pool_scale, ssm_log_neg_a_re, ssm_a_im, ssm_log_dt, ssm_b_re, ssm_b_im, ssm_c_re, ssm_c_im, ssm_d, glu_w, glu_b, out_norm_pool_g, out_norm_ssm_g, w_out, norm_ffn_g, w_up, conv_w, conv_b, w_down, final_norm_g, loss_target, m_norm_mix_g, m_w_in, m_pool_w, m_pool_scale, m_ssm_log_neg_a_re, m_ssm_a_im, m_ssm_log_dt, m_ssm_b_re, m_ssm_b_im, m_ssm_c_re, m_ssm_c_im, m_ssm_d, m_glu_w, m_glu_b, m_out_norm_pool_g, m_out_norm_ssm_g, m_w_out, m_norm_ffn_g, m_w_up, m_conv_w, m_conv_b, m_w_down, m_final_norm_g, v_norm_mix_g, v_w_in, v_pool_w, v_pool_scale, v_ssm_log_neg_a_re, v_ssm_a_im, v_ssm_log_dt, v_ssm_b_re, v_ssm_b_im, v_ssm_c_re, v_ssm_c_im, v_ssm_d, v_glu_w, v_glu_b, v_out_norm_pool_g, v_out_norm_ssm_g, v_w_out, v_norm_ffn_g, v_w_up, v_conv_w, v_conv_b, v_w_down, v_final_norm_g):
    args = locals()
    w = {k: args[k] for k in WEIGHTS}
    m = {k: args["m_" + k] for k in WEIGHTS}
    v = {k: args["v_" + k] for k in WEIGHTS}
    chip = 2 * lax.axis_index("x") + lax.axis_index("y")
    c_arr = lax.axis_index("c").astype(jnp.int32).reshape(1)

    shards = {k: w[k].astype(BF16) for k in BIG}
    shards["conv_w"] = conv_w
    w_in_full = _comm_call(_comm_gather_split([shards["w_in"]], []), "gather_w_in")[0]
    loss, grad_x, g_small, g_big, reduced = _local_step(
        x[0], loss_target[0], w, {"w_in": w_in_full.reshape(-1, w_in_full.shape[-1])}, shards, c_arr)

    exact = [k for k in SMALL if k not in WIDE]
    packs = [_pack([loss] + [g_small[k] for k in exact] + [g_small["conv_w"]], 512),
             _pack([g_small[k] for k in WIDE], 512)]
    halves = [g_big["w_in"].reshape(4, 2, g_big["w_in"].shape[0] // 8, -1)]
    halves += [pk.reshape(1, 2, pk.shape[0] // 2, LANES) for pk in packs]
    from_sibling = _comm_call(_comm_pair_swap(halves, half=True), "reduce_pair")
    names = ("w_in", "exact", "wide")
    sums = [_add_half(h, r, c_arr, "sum_pair_" + k, dt)
            for k, h, r, dt in zip(names, halves, from_sibling, (BF16, F32, BF16))]
    grads, delta, new_m, new_v = {}, {}, {}, {}

    def adamw_behind(k, comm):
        own, other = reduced[k]
        (grads[k], delta[k], new_m[k], new_v[k]), got = _adamw_halves(
            w[k], own, other, m[k], v[k], "adamw_" + k, comm=comm)
        return got

    from_chips = adamw_behind("w_up", _comm_join(_comm_chip_exchange(sums[:1], scatter=True),
                                                  _comm_chip_exchange([s[0] for s in sums[1:]], scatter=False)))
    mine = [_sum4(r, "sum_chips_" + k) for k, r in zip(names, from_chips)]
    theirs = adamw_behind("w_down", _comm_pair_swap(mine))
    for k in ("w_out", "glu_w"):
        adamw_behind(k, None)
    exact_all = _join_rows(mine[1], theirs[1], c_arr, "join_exact")
    wide_all = _join_rows(mine[2], theirs[2], c_arr, "join_wide")
    shapes = [loss.shape] + [w[k].shape for k in exact] + [(3, 4 * FF_BLK)]
    grads.update(zip(["loss"] + exact + ["conv_w_full"], _unpack(exact_all, shapes)))
    grads.update(zip(WIDE, _unpack(wide_all, [w[k].shape for k in WIDE])))
    loss = grads.pop("loss")[0, 0]
    grads["conv_w"] = lax.dynamic_slice_in_dim(grads.pop("conv_w_full"), chip * FF_BLK, FF_BLK, axis=1)

    reduced["w_in"] = (mine[0], theirs[0])
    adamw_behind("w_in", None)
    padded = ["ssm_b_re", "ssm_b_im"]
    for keys, name in ((padded, "adamw_ssm_b"), ([k for k in SMALL + ["conv_w"] if k not in padded], "adamw_small")):
        outs = _adamw_many(*([d[k] for k in keys] for d in (w, grads, m, v)), name)
        for d, o in zip((delta, new_m, new_v), outs):
            d.update(zip(keys, o))

    return (loss, grad_x[None], *[grads[k] for k in WEIGHTS], *[delta[k] for k in WEIGHTS],
            *[new_m[k] for k in WEIGHTS], *[new_v[k] for k in WEIGHTS])
```

```python
import numpy as np
import jax
import jax.numpy as jnp
from jax import lax
from jax.experimental import pallas as pl
from jax.experimental.pallas import tpu as pltpu

F32 = jnp.float32
BF16 = jnp.bfloat16
MESH = pl.DeviceIdType.MESH

EPS = 1e-6
POOL_WINDOWS = (2, 4, 8, 16)
POOL_GROUP = 128
SSM_GROUP = 16
SSM_STATE = 64
N_SSM_GROUPS = 32
N_STATE = N_SSM_GROUPS * SSM_STATE
QUAD = 256
N_QUAD = N_STATE // QUAD
SLAB = 256
D_SSM = 512
D_POOL = 512
D_FF = 2816
FF_BLK = 1408
HALO = 8
HALO_B = 16
LANES = 128
ADAM_LR, ADAM_B1, ADAM_B2, ADAM_EPS, ADAM_WD, ADAM_STEP = 0.001, 0.9, 0.999, 1e-08, 0.01, 10
VMEM_LIMIT = 56 * 2 ** 20
ADAMW_BLOCK_BYTES = 2 ** 20

TL = 512
TM = 1024
TF = 256
TC = 512
SEG = 8
SEG_LEN = TC // SEG
SCAN_UNROLL = 4
SCAN_W = 512


def _cp(*sem):
    return pltpu.CompilerParams(dimension_semantics=sem, vmem_limit_bytes=VMEM_LIMIT)


def _dot_nn(a, b):
    return jnp.dot(a, b, preferred_element_type=F32)


def _dot_nt(a, b):
    return lax.dot_general(a, b, (((1,), (1,)), ((), ())), preferred_element_type=F32)


def _dot_tn(a, b):
    return lax.dot_general(a, b, (((0,), (0,)), ((), ())), preferred_element_type=F32)


def _rms_fwd(x, g):
    inv = lax.rsqrt(jnp.mean(x * x, axis=-1, keepdims=True) + EPS)
    xh = x * inv
    return xh * g, xh, inv


def _rms_bwd(dy, xh, inv, g):
    dg = jnp.sum(dy * xh, axis=0, keepdims=True)
    dxh = dy * g
    dx = inv * (dxh - xh * jnp.mean(dxh * xh, axis=-1, keepdims=True))
    return dx, dg


_GELU_C = 0.7978845608028654
_GELU_A = 0.044715


def _gelu(y):
    t = jnp.tanh(_GELU_C * (y + _GELU_A * (y * y * y)))
    return 0.5 * y * (1.0 + t), t


def _gelu_grad(y, t):
    return 0.5 * (1.0 + t) + 0.5 * y * (1.0 - t * t) * (_GELU_C * (1.0 + 3.0 * _GELU_A * y * y))


def _sigmoid(x):
    return 1.0 / (1.0 + jnp.exp(-x))


def _full(shape):
    n = len(shape)
    return pl.BlockSpec(shape, lambda *_: (0,) * n)


def _fill_ext(ext_ref, prev_ref, cur_ref, next_ref, i, n, rows):
    ext_ref[0:HALO, :] = jnp.where(i > 0, prev_ref[...], 0.0).astype(ext_ref.dtype)
    ext_ref[HALO:HALO + rows, :] = cur_ref[...]
    ext_ref[HALO + rows:2 * HALO + rows, :] = jnp.where(i < n - 1, next_ref[...], 0.0).astype(ext_ref.dtype)


def _in_proj(x, g, w):
    L, D = x.shape
    E = w.shape[1]

    def body(x_ref, g_ref, w_ref, u_ref, xn_ref):
        y, _, _ = _rms_fwd(x_ref[...], g_ref[...])
        yb = y.astype(BF16)
        xn_ref[...] = yb
        u_ref[...] = _dot_nn(yb, w_ref[...])

    return pl.pallas_call(
        body, name="in_proj", grid=(L // TL,),
        in_specs=[pl.BlockSpec((TL, D), lambda i: (i, 0)), _full((1, D)), _full(w.shape)],
        out_specs=[pl.BlockSpec((TL, E), lambda i: (i, 0)), pl.BlockSpec((TL, D), lambda i: (i, 0))],
        out_shape=[jax.ShapeDtypeStruct((L, E), F32), jax.ShapeDtypeStruct((L, D), BF16)],
        compiler_params=_cp("parallel"))(x, g, w)


def _halo_specs_1d(rows, width, L, col):
    rb = rows // HALO
    last = L // HALO - 1
    return [pl.BlockSpec((HALO, width), lambda i: (jnp.maximum(i * rb - 1, 0), col)),
            pl.BlockSpec((rows, width), lambda i: (i, col)),
            pl.BlockSpec((HALO, width), lambda i: (jnp.minimum((i + 1) * rb, last), col))]


def _pooled_from_ext(ext_ref, t0, rows, L):
    t = t0 + lax.broadcasted_iota(jnp.int32, (rows, 1), 0)
    outs = []
    for gi, w in enumerate(POOL_WINDOWS):
        half = w // 2
        cs = slice(gi * POOL_GROUP, (gi + 1) * POOL_GROUP)
        acc = ext_ref[pl.ds(HALO - half, rows), cs]
        for s in range(-half + 1, half):
            acc = acc + ext_ref[pl.ds(HALO + s, rows), cs]
        cnt = (jnp.minimum(t + half, L) - jnp.maximum(t - half, 0)).astype(F32)
        outs.append(acc / cnt - ext_ref[pl.ds(HALO, rows), cs])
    return outs


def _pool_fwd(u, pool_w_b, pool_scale, g_pool):
    L = u.shape[0]
    n = L // TL

    def body(prev_ref, cur_ref, next_ref, pw_ref, ps_ref, g_ref, out_ref, ext_ref):
        i = pl.program_id(0)
        _fill_ext(ext_ref, prev_ref, cur_ref, next_ref, i, n, TL)
        pooled = _pooled_from_ext(ext_ref, i * TL, TL, L)
        ypre = jnp.concatenate([_dot_nn(pooled[gi].astype(BF16), pw_ref[gi]) for gi in range(4)], axis=-1)
        yn, _, _ = _rms_fwd(ypre * ps_ref[...], g_ref[...])
        out_ref[...] = yn.astype(BF16)

    return pl.pallas_call(
        body, name="pool_fwd", grid=(n,),
        in_specs=_halo_specs_1d(TL, D_POOL, L, 0) + [_full(pool_w_b.shape), _full((1, D_POOL)), _full((1, D_POOL))],
        out_specs=pl.BlockSpec((TL, D_POOL), lambda i: (i, 0)),
        out_shape=jax.ShapeDtypeStruct((L, D_POOL), BF16),
        scratch_shapes=[pltpu.VMEM((TL + 2 * HALO, D_POOL), F32)],
        compiler_params=_cp("parallel"))(u, u, u, pool_w_b, pool_scale, g_pool)


def _pool_bwd_local(dh1, u, w_out_b, pool_w_b, pool_scale, g_pool, comm=None):
    L = u.shape[0]
    n = L // TL
    D = dh1.shape[1]

    def body(dh_ref, prev_ref, cur_ref, next_ref, wo_ref, pw_ref, ps_ref, g_ref,
             dp_ref, gpw_ref, gps_ref, gg_ref, ext_ref):
        i = pl.program_id(0)

        @pl.when(i == 0)
        def _():
            gpw_ref[...] = jnp.zeros_like(gpw_ref)
            gps_ref[...] = jnp.zeros_like(gps_ref)
            gg_ref[...] = jnp.zeros_like(gg_ref)

        _fill_ext(ext_ref, prev_ref, cur_ref, next_ref, i, n, TL)
        pooled = [p.astype(BF16) for p in _pooled_from_ext(ext_ref, i * TL, TL, L)]
        ypre = jnp.concatenate([_dot_nn(pooled[gi], pw_ref[gi]) for gi in range(4)], axis=-1)
        ps = ps_ref[...]
        g = g_ref[...]
        _, xh, inv = _rms_fwd(ypre * ps, g)
        d_yn = _dot_nt(dh_ref[...], wo_ref[...])
        d_y, dg = _rms_bwd(d_yn, xh, inv, g)
        gg_ref[...] += dg
        gps_ref[...] += jnp.sum(d_y * ypre, axis=0, keepdims=True)
        d_ypre = (d_y * ps).astype(BF16)
        for gi in range(4):
            cs = slice(gi * POOL_GROUP, (gi + 1) * POOL_GROUP)
            dp_ref[:, cs] = _dot_nt(d_ypre[:, cs], pw_ref[gi])
            gpw_ref[gi] += _dot_tn(pooled[gi], d_ypre[:, cs])

    return _hosted_call(
        body, comm, name="pool_bwd_local", grid=(n,),
        in_specs=[pl.BlockSpec((TL, D), lambda i: (i, 0))] + _halo_specs_1d(TL, D_POOL, L, 0)
        + [pl.BlockSpec((D_POOL, D), lambda i: (0, 0)), _full(pool_w_b.shape), _full((1, D_POOL)), _full((1, D_POOL))],
        out_specs=[pl.BlockSpec((TL, D_POOL), lambda i: (i, 0)), _full(pool_w_b.shape),
                   _full((1, D_POOL)), _full((1, D_POOL))],
        out_shape=[jax.ShapeDtypeStruct((L, D_POOL), F32), jax.ShapeDtypeStruct(pool_w_b.shape, F32),
                   jax.ShapeDtypeStruct((1, D_POOL), F32), jax.ShapeDtypeStruct((1, D_POOL), F32)],
        scratch_shapes=[pltpu.VMEM((TL + 2 * HALO, D_POOL), F32)],
        args=(dh1, u, u, u, w_out_b, pool_w_b, pool_scale, g_pool))


def _pool_bwd_window(d_pooled):
    L = d_pooled.shape[0]
    n = L // TL
    R = TL + 2 * HALO

    def body(prev_ref, cur_ref, next_ref, out_ref, ext_ref, q_ref):
        i = pl.program_id(0)
        _fill_ext(ext_ref, prev_ref, cur_ref, next_ref, i, n, TL)
        tr = i * TL - HALO + lax.broadcasted_iota(jnp.int32, (R, 1), 0)
        for gi, w in enumerate(POOL_WINDOWS):
            half = w // 2
            cs = slice(gi * POOL_GROUP, (gi + 1) * POOL_GROUP)
            cnt = jnp.maximum(jnp.minimum(tr + half, L) - jnp.maximum(tr - half, 0), 1).astype(F32)
            q_ref[:, cs] = ext_ref[:, cs] / cnt
        for gi, w in enumerate(POOL_WINDOWS):
            half = w // 2
            cs = slice(gi * POOL_GROUP, (gi + 1) * POOL_GROUP)
            acc = q_ref[pl.ds(HALO - half + 1, TL), cs]
            for s in range(-half + 2, half + 1):
                acc = acc + q_ref[pl.ds(HALO + s, TL), cs]
            out_ref[:, cs] = acc - ext_ref[pl.ds(HALO, TL), cs]

    return pl.pallas_call(
        body, name="pool_bwd_window", grid=(n,),
        in_specs=_halo_specs_1d(TL, D_POOL, L, 0),
        out_specs=pl.BlockSpec((TL, D_POOL), lambda i: (i, 0)),
        out_shape=jax.ShapeDtypeStruct((L, D_POOL), F32),
        scratch_shapes=[pltpu.VMEM((R, D_POOL), F32), pltpu.VMEM((R, D_POOL), F32)],
        compiler_params=_cp("parallel"))(d_pooled, d_pooled, d_pooled)


def _ssm_param_fn(lnar, aim, ldt):
    dt = jnp.exp(ldt)
    a_re = -jnp.exp(lnar)
    mag = jnp.exp(a_re * dt)
    ang = aim * dt
    lr, li = mag * jnp.cos(ang), mag * jnp.sin(ang)
    den = a_re * a_re + aim * aim
    fr = ((lr - 1.0) * a_re + li * aim) / den
    fi = (li * a_re - (lr - 1.0) * aim) / den
    return lr, li, fr, fi


def _ssm_params(lnar, aim, ldt):
    def body(a_ref, b_ref, c_ref, lr_ref, li_ref, fr_ref, fi_ref):
        lr, li, fr, fi = _ssm_param_fn(a_ref[...], b_ref[...], c_ref[...])
        lr_ref[...] = lr
        li_ref[...] = li
        fr_ref[...] = fr
        fi_ref[...] = fi

    sh = jax.ShapeDtypeStruct(lnar.shape, F32)
    return pl.pallas_call(body, name="ssm_params", out_shape=[sh] * 4)(lnar, aim, ldt)


def _ssm_params_bwd(lnar, aim, ldt, glr, gli, gfr, gfi):
    def body(a_ref, b_ref, c_ref, g0, g1, g2, g3, da_ref, db_ref, dc_ref):
        _, vjp = jax.vjp(_ssm_param_fn, a_ref[...], b_ref[...], c_ref[...])
        da, db, dc = vjp((g0[...], g1[...], g2[...], g3[...]))
        da_ref[...] = da
        db_ref[...] = db
        dc_ref[...] = jnp.sum(dc, axis=1, keepdims=True)

    return pl.pallas_call(
        body, name="ssm_params_bwd",
        out_shape=[jax.ShapeDtypeStruct(lnar.shape, F32), jax.ShapeDtypeStruct(aim.shape, F32),
                   jax.ShapeDtypeStruct((ldt.shape[0], 1), F32)])(lnar, aim, ldt, glr, gli, gfr, gfi)


def _scan_tables(lam4):
    def build(lr, li, reverse, out_ref, k):
        pr, pi = lr, li
        for d in range(SEG_LEN):
            j = SEG_LEN - 1 - d if reverse else d
            out_ref[k, 0, j:j + 1, :] = pr
            out_ref[k, 1, j:j + 1, :] = pi
            pr, pi = pr * lr - pi * li, pr * li + pi * lr

    def body(lam_ref, out_ref):
        l0r, l0i, l1r, l1i = (lam_ref[j:j + 1, :] for j in range(4))
        build(l0r, l0i, False, out_ref, 0)
        build(l0r, -l0i, True, out_ref, 1)
        build(l1r, l1i, True, out_ref, 2)
        build(l1r, -l1i, False, out_ref, 3)

    return pl.pallas_call(body, name="scan_tables",
                          out_shape=jax.ShapeDtypeStruct((4, 2, SEG_LEN, N_STATE), F32))(lam4)


def _b_block(g):
    q, gl = divmod(g, 4)
    r0, c0 = gl * SSM_STATE, (q % 4) * 4 * SSM_GROUP + gl * SSM_GROUP
    return q, slice(r0, r0 + SSM_STATE), slice(c0, c0 + SSM_GROUP)


def _c_block(g):
    q, rows, cols = _b_block(g)
    return q, cols, rows


def _ssm_expand(b_re, b_im, c_re, c_im, f_re, f_im, comm=None):
    def body(bre_ref, bim_ref, cre_ref, cim_ref, fre_ref, fim_ref, *rest):
        outs, tmp, bbr_ref, bbi_ref = rest[:8], rest[8], rest[9], rest[10]
        fr, fi, br, bi = fre_ref[...], fim_ref[...], bre_ref[...], bim_ref[...]
        bbr_ref[...] = fr * br - fi * bi
        bbi_ref[...] = fr * bi + fi * br
        for d in range(2):
            for j, (src, where) in enumerate(((bbr_ref, _b_block), (bbi_ref, _b_block),
                                              (cre_ref, _c_block), (cim_ref, _c_block))):
                tmp[...] = jnp.zeros_like(tmp)
                for g in range(N_SSM_GROUPS):
                    q, rows, cols = where(g)
                    tmp[q, rows, cols] = src[d, g]
                outs[4 * d + j][...] = tmp[...].astype(BF16)

    dense = jax.ShapeDtypeStruct((N_QUAD, QUAD, SLAB), BF16)
    args = (b_re, b_im, c_re, c_im, f_re, f_im)
    return _hosted_call(body, comm, name="ssm_expand", grid=(1,), in_specs=[_full(a.shape) for a in args],
                        out_specs=[_full(dense.shape)] * 8, out_shape=[dense] * 8,
                        scratch_shapes=[pltpu.VMEM((N_QUAD, QUAD, SLAB), F32), pltpu.VMEM(b_re.shape, F32),
                                        pltpu.VMEM(b_re.shape, F32)], args=args)


def _ssm_unfold(gbb_re, gbb_im, b_re_t, b_im_t, f_re, f_im):
    def body(gr_ref, gi_ref, br_ref, bi_ref, fr_ref, fi_ref, obr_ref, obi_ref, ofr_ref, ofi_ref):
        gr, gi, br, bi, fr, fi = (r[...] for r in (gr_ref, gi_ref, br_ref, bi_ref, fr_ref, fi_ref))
        obr_ref[...] = fr * gr + fi * gi
        obi_ref[...] = fr * gi - fi * gr
        ofr_ref[...] = jnp.sum(br * gr + bi * gi, axis=2, keepdims=True)
        ofi_ref[...] = jnp.sum(br * gi - bi * gr, axis=2, keepdims=True)

    gb = jax.ShapeDtypeStruct(gbb_re.shape, F32)
    gf = jax.ShapeDtypeStruct(f_re.shape, F32)
    return pl.pallas_call(body, name="ssm_unfold", out_shape=[gb, gb, gf, gf])(
        gbb_re, gbb_im, b_re_t, b_im_t, f_re, f_im)


_SEGMENT_ORDER = np.zeros((TC, TC), np.float32)
for _p in range(TC):
    _SEGMENT_ORDER[_p, (_p % SEG) * SEG_LEN + _p // SEG] = 1.0


def _store_tokens(ref, col0, val, tmp_ref):
    for h in range(val.shape[1] // LANES):
        for j in range(SEG_LEN):
            tmp_ref[pl.ds(h * TC + j, SEG, stride=SEG_LEN), :] = val[SEG * j:SEG * (j + 1), h * LANES:(h + 1) * LANES]
        ref[:, col0 + h * LANES:col0 + (h + 1) * LANES] = tmp_ref[pl.ds(h * TC, TC), :]


def _segment_scan(src_re, src_im, dst_re, dst_im, tab_ref, k, carry_re, carry_im, reverse, s_refs=None):
    lam1, lam_seg = (SEG_LEN - 1, 0) if reverse else (0, SEG_LEN - 1)
    token = (lambda i: SEG_LEN - 1 - i) if reverse else (lambda i: i)
    row_id = lax.broadcasted_iota(jnp.int32, (SEG, SCAN_W), 0)
    zero = jnp.zeros((SEG, SCAN_W), F32)
    sums = []
    for lt in range(N_STATE // SCAN_W):
        sl = slice(lt * SCAN_W, (lt + 1) * SCAN_W)
        lr = jnp.broadcast_to(tab_ref[k, 0, lam1:lam1 + 1, sl], (SEG, SCAN_W))
        li = jnp.broadcast_to(tab_ref[k, 1, lam1:lam1 + 1, sl], (SEG, SCAN_W))

        def local(i, c, sl=sl, lr=lr, li=li):
            for step in range(SCAN_UNROLL):
                rows = pl.ds(pl.multiple_of(token(i * SCAN_UNROLL + step) * SEG, SEG), SEG)
                c = (lr * c[0] - li * c[1] + src_re[rows, sl], lr * c[1] + li * c[0] + src_im[rows, sl])
                dst_re[rows, sl] = c[0]
                dst_im[rows, sl] = c[1]
            return c

        er, ei = lax.fori_loop(0, SEG_LEN // SCAN_UNROLL, local, (zero, zero))

        sr_, si_ = tab_ref[k, 0, lam_seg:lam_seg + 1, sl], tab_ref[k, 1, lam_seg:lam_seg + 1, sl]
        c_r, c_i = carry_re[0:1, sl], carry_im[0:1, sl]
        in_r, in_i = zero, zero
        for r in (range(SEG - 1, -1, -1) if reverse else range(SEG)):
            in_r = jnp.where(row_id == r, c_r, in_r)
            in_i = jnp.where(row_id == r, c_i, in_i)
            c_r, c_i = (er[r:r + 1, :] + sr_ * c_r - si_ * c_i, ei[r:r + 1, :] + sr_ * c_i + si_ * c_r)
        carry_re[0:1, sl] = c_r
        carry_im[0:1, sl] = c_i

        def fix(i, c, sl=sl, in_r=in_r, in_i=in_i):
            for step in range(SCAN_UNROLL):
                j = token(i * SCAN_UNROLL + step)
                rows = pl.ds(pl.multiple_of(j * SEG, SEG), SEG)
                pr = jnp.broadcast_to(tab_ref[k, 0, pl.ds(j, 1), sl], (SEG, SCAN_W))
                pi = jnp.broadcast_to(tab_ref[k, 1, pl.ds(j, 1), sl], (SEG, SCAN_W))
                nr = dst_re[rows, sl] + pr * in_r - pi * in_i
                ni = dst_im[rows, sl] + pr * in_i + pi * in_r
                dst_re[rows, sl] = nr
                dst_im[rows, sl] = ni
                if s_refs is not None:
                    sr = s_refs[0][rows, sl]
                    si = s_refs[1][rows, sl]
                    c = (nr, ni, c[2] + c[0] * sr + c[1] * si, c[3] + c[1] * sr - c[0] * si)
            return c

        if s_refs is None:
            lax.fori_loop(0, SEG_LEN // SCAN_UNROLL, fix, 0)
        else:
            out = lax.fori_loop(0, SEG_LEN // SCAN_UNROLL, fix, (in_r, in_i, zero, zero))
            sums.append((jnp.sum(out[2], axis=0, keepdims=True), jnp.sum(out[3], axis=0, keepdims=True)))
    return sums


def _ssm_scan_fwd(u, b_re, b_im, c_re, c_im, tables, k, reverse, comm=None):
    L = u.shape[0]
    nc = L // TC
    chunk = (lambda i: nc - 1 - i) if reverse else (lambda i: i)
    order = jnp.asarray(_SEGMENT_ORDER, BF16)

    def body(u_ref, ord_ref, bre_ref, bim_ref, cre_ref, cim_ref, tab_ref,
             y_ref, sre_ref, sim_ref, in_re, in_im, carry_re, carry_im, tmp_ref):
        @pl.when(pl.program_id(0) == 0)
        def _():
            carry_re[...] = jnp.zeros_like(carry_re)
            carry_im[...] = jnp.zeros_like(carry_im)

        ub = _dot_nn(ord_ref[...], u_ref[...].astype(BF16)).astype(BF16)
        for q in range(N_QUAD):
            qs = slice(q * QUAD, (q + 1) * QUAD)
            us = ub[:, (q // 4) * SLAB:(q // 4 + 1) * SLAB]
            in_re[:, qs] = _dot_nt(us, bre_ref[q])
            in_im[:, qs] = _dot_nt(us, bim_ref[q])
        _segment_scan(in_re, in_im, sre_ref, sim_ref, tab_ref, k, carry_re, carry_im, reverse)
        for j in range(D_SSM // SLAB):
            acc = jnp.zeros((TC, SLAB), F32)
            for q in range(4 * j, 4 * j + 4):
                qs = slice(q * QUAD, (q + 1) * QUAD)
                acc = acc + _dot_nt(sre_ref[:, qs].astype(BF16), cre_ref[q])
                acc = acc - _dot_nt(sim_ref[:, qs].astype(BF16), cim_ref[q])
            _store_tokens(y_ref, j * SLAB, acc, tmp_ref)

    return _hosted_call(
        body, comm, name="ssm_scan_rev" if reverse else "ssm_scan_fwd", grid=(nc,),
        in_specs=[pl.BlockSpec((TC, D_SSM), lambda i: (chunk(i), 1)), _full(order.shape)]
        + [_full(b_re.shape)] * 4 + [_full(tables.shape)],
        out_specs=[pl.BlockSpec((TC, D_SSM), lambda i: (chunk(i), 0)),
                   pl.BlockSpec((TC, N_STATE), lambda i: (chunk(i), 0)),
                   pl.BlockSpec((TC, N_STATE), lambda i: (chunk(i), 0))],
        out_shape=[jax.ShapeDtypeStruct((L, D_SSM), F32), jax.ShapeDtypeStruct((L, N_STATE), F32),
                   jax.ShapeDtypeStruct((L, N_STATE), F32)],
        scratch_shapes=[pltpu.VMEM((TC, N_STATE), F32), pltpu.VMEM((TC, N_STATE), F32),
                        pltpu.VMEM((8, N_STATE), F32), pltpu.VMEM((8, N_STATE), F32),
                        pltpu.VMEM((SLAB // LANES * TC, LANES), F32)],
        args=(u, order, b_re, b_im, c_re, c_im, tables))


def _quad_channels(q):
    c0 = (q // 4) * SLAB + (q % 4) * 4 * SSM_GROUP
    return slice(c0, c0 + 4 * SSM_GROUP)


def _ssm_scan_bwd(dy, u, s_re, s_im, b_re, b_im, c_re, c_im, tables, k, reverse, comm=None):
    L = u.shape[0]
    nc = L // TC
    chunk = (lambda i: nc - 1 - i) if reverse else (lambda i: i)

    order = jnp.asarray(_SEGMENT_ORDER, BF16)

    def body(dy_ref, u_ref, ord_ref, sre_ref, sim_ref, bre_ref, bim_ref, cre_ref, cim_ref, tab_ref,
             du_ref, ob_re, ob_im, oc_re, oc_im, gv_ref,
             a_re, a_im, carry_re, carry_im, gbr_ref, gbi_ref, gcr_ref, gci_ref, tmp_ref):
        @pl.when(pl.program_id(0) == 0)
        def _():
            carry_re[...] = jnp.zeros_like(carry_re)
            carry_im[...] = jnp.zeros_like(carry_im)
            for r in (gbr_ref, gbi_ref, gcr_ref, gci_ref, gv_ref):
                r[...] = jnp.zeros_like(r)

        dyb = _dot_nn(ord_ref[...], dy_ref[...].astype(BF16)).astype(BF16)
        ub = _dot_nn(ord_ref[...], u_ref[...].astype(BF16)).astype(BF16)
        for q in range(N_QUAD):
            qs = slice(q * QUAD, (q + 1) * QUAD)
            ds = dyb[:, (q // 4) * SLAB:(q // 4 + 1) * SLAB]
            a_re[:, qs] = _dot_nn(ds, cre_ref[q])
            a_im[:, qs] = -_dot_nn(ds, cim_ref[q])
            dq = dyb[:, _quad_channels(q)]
            gcr_ref[q] += _dot_tn(dq, sre_ref[:, qs].astype(BF16))
            gci_ref[q] -= _dot_tn(dq, sim_ref[:, qs].astype(BF16))
        sums = _segment_scan(a_re, a_im, a_re, a_im, tab_ref, k, carry_re, carry_im, reverse,
                             s_refs=(sre_ref, sim_ref))
        for lt, (glr, gli) in enumerate(sums):
            sl = slice(lt * SCAN_W, (lt + 1) * SCAN_W)
            gv_ref[0:1, sl] += glr
            gv_ref[1:2, sl] += gli
        for j in range(D_SSM // SLAB):
            us = ub[:, j * SLAB:(j + 1) * SLAB]
            acc = jnp.zeros((TC, SLAB), F32)
            for q in range(4 * j, 4 * j + 4):
                qs = slice(q * QUAD, (q + 1) * QUAD)
                dbr = a_re[:, qs].astype(BF16)
                dbi = a_im[:, qs].astype(BF16)
                uq = ub[:, _quad_channels(q)]
                gbr_ref[q] += _dot_tn(uq, dbr)
                gbi_ref[q] += _dot_tn(uq, dbi)
                acc = acc + _dot_nn(dbr, bre_ref[q]) + _dot_nn(dbi, bim_ref[q])
            _store_tokens(du_ref, j * SLAB, acc, tmp_ref)

        @pl.when(pl.program_id(0) == nc - 1)
        def _():
            for g in range(N_SSM_GROUPS):
                q, gl = divmod(g, 4)
                rows = slice(gl * SSM_GROUP, (gl + 1) * SSM_GROUP)
                cols = slice(gl * SSM_STATE, (gl + 1) * SSM_STATE)
                for out, acc_ref in ((ob_re, gbr_ref), (ob_im, gbi_ref), (oc_re, gcr_ref), (oc_im, gci_ref)):
                    out[g] = acc_ref[q, rows, cols]

    gshape = jax.ShapeDtypeStruct((N_SSM_GROUPS, SSM_GROUP, SSM_STATE), F32)
    compact = pltpu.VMEM((N_QUAD, 4 * SSM_GROUP, QUAD), F32)
    return _hosted_call(
        body, comm, name="ssm_bwd_rev" if reverse else "ssm_bwd_fwd", grid=(nc,),
        in_specs=[pl.BlockSpec((TC, D_SSM), lambda i: (chunk(i), 0)),
                  pl.BlockSpec((TC, D_SSM), lambda i: (chunk(i), 1)), _full(order.shape),
                  pl.BlockSpec((TC, N_STATE), lambda i: (chunk(i), 0)),
                  pl.BlockSpec((TC, N_STATE), lambda i: (chunk(i), 0))]
        + [_full(b_re.shape)] * 4 + [_full(tables.shape)],
        out_specs=[pl.BlockSpec((TC, D_SSM), lambda i: (chunk(i), 0))] + [_full(gshape.shape)] * 4
        + [_full((2, N_STATE))],
        out_shape=[jax.ShapeDtypeStruct((L, D_SSM), F32), gshape, gshape, gshape, gshape,
                   jax.ShapeDtypeStruct((2, N_STATE), F32)],
        scratch_shapes=[pltpu.VMEM((TC, N_STATE), F32), pltpu.VMEM((TC, N_STATE), F32),
                        pltpu.VMEM((8, N_STATE), F32), pltpu.VMEM((8, N_STATE), F32),
                        compact, compact, compact, compact, pltpu.VMEM((SLAB // LANES * TC, LANES), F32)],
        args=(dy, u, order, s_re, s_im, b_re, b_im, c_re, c_im, tables))


def _ssm_post(yf, yb, u, d, glu_w, glu_b):
    y = yf + yb + d * u
    z, t = _gelu(y)
    zb = z.astype(BF16)
    gate = _sigmoid(_dot_nn(zb, glu_w) + glu_b)
    return y, z, t, zb, gate


def _mix_out(yn_pool, yf, yb, u, x, ssm_d, glu_w_b, glu_b, g_ssm, w_out_b, g_ffn):
    L, D = x.shape

    def body(ynp_ref, yf_ref, yb_ref, u_ref, x_ref, d_ref, gw_ref, gb_ref, gs_ref, wo_ref, gf_ref,
             h1_ref, hn_ref, ycat_ref):
        _, z, _, _, gate = _ssm_post(yf_ref[...], yb_ref[...], u_ref[...], d_ref[...], gw_ref[...], gb_ref[...])
        yns, _, _ = _rms_fwd(z * gate, gs_ref[...])
        ynsb = yns.astype(BF16)
        ynp = ynp_ref[...]
        ycat_ref[:, 0:D_POOL] = ynp
        ycat_ref[:, D_POOL:D] = ynsb
        h1 = x_ref[...] + _dot_nn(ynp, wo_ref[0:D_POOL, :]) + _dot_nn(ynsb, wo_ref[D_POOL:D, :])
        h1_ref[...] = h1
        hn, _, _ = _rms_fwd(h1, gf_ref[...])
        hn_ref[...] = hn.astype(BF16)

    half = lambda c: pl.BlockSpec((TL, D_SSM), lambda i: (i, c))
    row = pl.BlockSpec((TL, D), lambda i: (i, 0))
    return pl.pallas_call(
        body, name="mix_out", grid=(L // TL,),
        in_specs=[half(0), half(0), half(0), half(1), row, _full((1, D_SSM)), _full(glu_w_b.shape),
                  _full((1, D_SSM)), _full((1, D_SSM)), _full(w_out_b.shape), _full((1, D))],
        out_specs=[row, row, row],
        out_shape=[jax.ShapeDtypeStruct((L, D), F32), jax.ShapeDtypeStruct((L, D), BF16),
                   jax.ShapeDtypeStruct((L, D), BF16)],
        compiler_params=_cp("parallel"))(yn_pool, yf, yb, u, x, ssm_d, glu_w_b, glu_b, g_ssm, w_out_b, g_ffn)


def _ssm_bwd_local(dh1, yf, yb, u, ssm_d, glu_w_b, glu_b, g_ssm, w_out_b, comm=None):
    L, D = dh1.shape

    def body(dh_ref, yf_ref, yb_ref, u_ref, d_ref, gw_ref, gb_ref, gs_ref, wo_ref,
             dy_ref, du_ref, ggw_ref, ggb_ref, gd_ref, ggs_ref):
        @pl.when(pl.program_id(0) == 0)
        def _():
            for r in (ggw_ref, ggb_ref, gd_ref, ggs_ref):
                r[...] = jnp.zeros_like(r)

        u = u_ref[...]
        d = d_ref[...]
        y, z, t, zb, gate = _ssm_post(yf_ref[...], yb_ref[...], u, d, gw_ref[...], gb_ref[...])
        gs = gs_ref[...]
        _, xh, inv = _rms_fwd(z * gate, gs)
        d_yn = _dot_nt(dh_ref[...], wo_ref[...])
        d_o, dgs = _rms_bwd(d_yn, xh, inv, gs)
        ggs_ref[...] += dgs
        d_zg = d_o * z * gate * (1.0 - gate)
        d_zgb = d_zg.astype(BF16)
        ggb_ref[...] += jnp.sum(d_zg, axis=0, keepdims=True)
        ggw_ref[...] += _dot_tn(zb, d_zgb)
        d_z = d_o * gate + _dot_nt(d_zgb, gw_ref[...])
        d_y = d_z * _gelu_grad(y, t)
        gd_ref[...] += jnp.sum(d_y * u, axis=0, keepdims=True)
        dy_ref[...] = d_y
        du_ref[...] = d_y * d

    half = lambda c: pl.BlockSpec((TL, D_SSM), lambda i: (i, c))
    vec = _full((1, D_SSM))
    return _hosted_call(
        body, comm, name="ssm_bwd_local", grid=(L // TL,),
        in_specs=[pl.BlockSpec((TL, D), lambda i: (i, 0)), half(0), half(0), half(1), vec, _full(glu_w_b.shape),
                  vec, vec, pl.BlockSpec((D_SSM, D), lambda i: (1, 0))],
        out_specs=[half(0), half(0), _full(glu_w_b.shape), vec, vec, vec],
        out_shape=[jax.ShapeDtypeStruct((L, D_SSM), F32), jax.ShapeDtypeStruct((L, D_SSM), F32),
                   jax.ShapeDtypeStruct(glu_w_b.shape, F32)] + [jax.ShapeDtypeStruct((1, D_SSM), F32)] * 3,
        scratch_shapes=[], args=(dh1, yf, yb, u, ssm_d, glu_w_b, glu_b, g_ssm, w_out_b))


def _in_bwd(du_pool, du_a, du_b, du_c, dh1, x, g, w_in_b, comm=None):
    L, D = x.shape

    def body(p_ref, a_ref, b_ref, c_ref, dh_ref, x_ref, g_ref, w_ref, dx_ref, dub_ref, gg_ref):
        @pl.when(pl.program_id(0) == 0)
        def _():
            gg_ref[...] = jnp.zeros_like(gg_ref)

        dub_ref[:, 0:D_POOL] = p_ref[...].astype(BF16)
        dub_ref[:, D_POOL:D] = (a_ref[...] + b_ref[...] + c_ref[...]).astype(BF16)
        d_xn = _dot_nt(dub_ref[...], w_ref[...])
        gv = g_ref[...]
        _, xh, inv = _rms_fwd(x_ref[...], gv)
        dx, dg = _rms_bwd(d_xn, xh, inv, gv)
        gg_ref[...] += dg
        dx_ref[...] = dh_ref[...] + dx

    half = pl.BlockSpec((TL, D_SSM), lambda i: (i, 0))
    row = pl.BlockSpec((TL, D), lambda i: (i, 0))
    return _hosted_call(
        body, comm, name="in_bwd", grid=(L // TL,),
        in_specs=[half, half, half, half, row, row, _full((1, D)), _full(w_in_b.shape)],
        out_specs=[row, row, _full((1, D))],
        out_shape=[jax.ShapeDtypeStruct((L, D), F32), jax.ShapeDtypeStruct((L, D), BF16),
                   jax.ShapeDtypeStruct((1, D), F32)],
        scratch_shapes=[], args=(du_pool, du_a, du_b, du_c, dh1, x, g, w_in_b))


def _ffn_up(hn, w_up4):
    L, D = hn.shape

    def body(h_ref, w_ref, o_ref):
        o_ref[...] = _dot_nn(h_ref[...], w_ref[...]).astype(BF16)

    rows = min(TM, L)
    return pl.pallas_call(
        body, name="ffn_up", grid=(4, L // rows),
        in_specs=[pl.BlockSpec((rows, D), lambda j, i: (i, 0)), pl.BlockSpec((None, D, FF_BLK), lambda j, i: (j, 0, 0))],
        out_specs=pl.BlockSpec((rows, FF_BLK), lambda j, i: (i, j)),
        out_shape=jax.ShapeDtypeStruct((L, 4 * FF_BLK), BF16),
        compiler_params=_cp("parallel", "parallel"))(hn, w_up4)


def _halo_specs_2d(rows, width, L, col, order):
    rb = rows // HALO_B
    last = L // HALO_B - 1
    if order == "ik":
        wrap = lambda f: (lambda i, k: f(i, k))
    else:
        wrap = lambda f: (lambda k, i: f(i, k))
    return [pl.BlockSpec((HALO_B, width), wrap(lambda i, k: (jnp.maximum(i * rb - 1, 0), col(k)))),
            pl.BlockSpec((rows, width), wrap(lambda i, k: (i, col(k)))),
            pl.BlockSpec((HALO_B, width), wrap(lambda i, k: (jnp.minimum((i + 1) * rb, last), col(k))))]


def _neighbours(x, prev_ref, next_ref, cs, i, n):
    rows = x.shape[0]
    row = lax.broadcasted_iota(jnp.int32, (rows, 1), 0)
    before = jnp.where(i > 0, prev_ref[:, cs].astype(F32)[HALO_B - 1:HALO_B, :], 0.0)
    after = jnp.where(i < n - 1, next_ref[:, cs].astype(F32)[0:1, :], 0.0)
    xf = x.astype(F32)
    return (jnp.where(row == 0, before, pltpu.roll(xf, 1, 0)),
            jnp.where(row == rows - 1, after, pltpu.roll(xf, rows - 1, 0)))


def _conv3(x, before, after, w, b):
    return before * w[0:1, :] + x.astype(F32) * w[1:2, :] + after * w[2:3, :] + b


def _col_chunks(width, size=256):
    return [slice(c, min(c + size, width)) for c in range(0, width, size)]


def _ffn_down_loss(up, conv_w, conv_b, w_down_b, h1, target, g_final):
    L, D = h1.shape
    n = L // TF
    nk = D_FF // FF_BLK

    def body(vp, vc, vn, gp, gc, gn, wv_ref, wg_ref, bv_ref, bg_ref, wd_ref, h1_ref, t_ref, gf_ref,
             a_ref, cv_ref, cg_ref, dh2_ref, dh2b_ref, loss_ref, gg_ref, acc_ref):
        i = pl.program_id(0)
        k = pl.program_id(1)

        @pl.when((i == 0) & (k == 0))
        def _():
            loss_ref[...] = jnp.zeros_like(loss_ref)
            gg_ref[...] = jnp.zeros_like(gg_ref)

        @pl.when(k == 0)
        def _():
            acc_ref[...] = jnp.zeros_like(acc_ref)

        acc = jnp.zeros((TF, D), F32)
        for cs in _col_chunks(FF_BLK):
            xv, xg = vc[:, cs], gc[:, cs]
            val = _conv3(xv, *_neighbours(xv, vp, vn, cs, i, n), wv_ref[:, cs], bv_ref[:, cs])
            gate = _conv3(xg, *_neighbours(xg, gp, gn, cs, i, n), wg_ref[:, cs], bg_ref[:, cs])
            a = (val * (gate * _sigmoid(gate))).astype(BF16)
            a_ref[:, cs] = a
            cv_ref[:, cs] = val.astype(BF16)
            cg_ref[:, cs] = gate.astype(BF16)
            rows = pl.ds(pl.multiple_of(k * FF_BLK + cs.start, LANES), cs.stop - cs.start)
            acc = acc + _dot_nn(a, wd_ref[rows, :])
        acc_ref[...] += acc

        @pl.when(k == nk - 1)
        def _():
            gf = gf_ref[...]
            y, xh, inv = _rms_fwd(h1_ref[...] + acc_ref[...], gf)
            diff = y - t_ref[...]
            part = 0.5 * jnp.sum(jnp.mean(diff * diff, axis=-1, keepdims=True), axis=0, keepdims=True)
            loss_ref[...] += jnp.broadcast_to(part, loss_ref.shape)
            dx, dg = _rms_bwd(diff * (1.0 / D), xh, inv, gf)
            gg_ref[...] += dg
            dh2_ref[...] = dx
            dh2b_ref[...] = dx.astype(BF16)

    row = pl.BlockSpec((TF, D), lambda i, k: (i, 0))
    cw = lambda off: pl.BlockSpec((3, FF_BLK), lambda i, k: (0, k + off))
    cb = lambda off: pl.BlockSpec((1, FF_BLK), lambda i, k: (0, k + off))
    return pl.pallas_call(
        body, name="ffn_down_loss", grid=(n, nk),
        in_specs=_halo_specs_2d(TF, FF_BLK, L, lambda k: k, "ik") + _halo_specs_2d(TF, FF_BLK, L, lambda k: k + nk, "ik")
        + [cw(0), cw(nk), cb(0), cb(nk), _full(w_down_b.shape), row, row, _full((1, D))],
        out_specs=[pl.BlockSpec((TF, FF_BLK), lambda i, k: (i, k))] * 3 + [row, row, _full((1, LANES)), _full((1, D))],
        out_shape=[jax.ShapeDtypeStruct((L, D_FF), BF16)] * 3
        + [jax.ShapeDtypeStruct((L, D), F32), jax.ShapeDtypeStruct((L, D), BF16),
           jax.ShapeDtypeStruct((1, LANES), F32), jax.ShapeDtypeStruct((1, D), F32)],
        scratch_shapes=[pltpu.VMEM((TF, D), F32)],
        compiler_params=_cp("arbitrary", "arbitrary"))(
            up, up, up, up, up, up, conv_w, conv_w, conv_b, conv_b, w_down_b, h1, target, g_final)


def _ffn_act_bwd(c_val, c_gate, w_down_b, dh2):
    L, D = dh2.shape
    n = L // TL
    nk = D_FF // FF_BLK

    def body(v_ref, g_ref, wd_ref, dh_ref, dv_ref, dg_ref, gbv_ref, gbg_ref):
        @pl.when(pl.program_id(1) == 0)
        def _():
            gbv_ref[...] = jnp.zeros_like(gbv_ref)
            gbg_ref[...] = jnp.zeros_like(gbg_ref)

        dh = dh_ref[...]
        for cs in _col_chunks(FF_BLK):
            val, gate = v_ref[:, cs].astype(F32), g_ref[:, cs].astype(F32)
            d_a = _dot_nt(dh, wd_ref[cs, :])
            sg = _sigmoid(gate)
            d_val = d_a * (gate * sg)
            d_gate = d_a * val * (sg * (1.0 + gate * (1.0 - sg)))
            dv_ref[:, cs] = d_val.astype(BF16)
            dg_ref[:, cs] = d_gate.astype(BF16)
            gbv_ref[:, cs] += jnp.sum(d_val, axis=0, keepdims=True)
            gbg_ref[:, cs] += jnp.sum(d_gate, axis=0, keepdims=True)

    blk = pl.BlockSpec((TL, FF_BLK), lambda k, i: (i, k))
    acc = pl.BlockSpec((1, FF_BLK), lambda k, i: (0, k))
    return pl.pallas_call(
        body, name="ffn_act_bwd", grid=(nk, n),
        in_specs=[blk, blk, pl.BlockSpec((FF_BLK, D), lambda k, i: (k, 0)), pl.BlockSpec((TL, D), lambda k, i: (i, 0))],
        out_specs=[blk, blk, acc, acc],
        out_shape=[jax.ShapeDtypeStruct((L, D_FF), BF16), jax.ShapeDtypeStruct((L, D_FF), BF16),
                   jax.ShapeDtypeStruct((1, D_FF), F32), jax.ShapeDtypeStruct((1, D_FF), F32)],
        compiler_params=_cp("arbitrary", "arbitrary"))(c_val, c_gate, w_down_b, dh2)


def _ffn_up_bwd(d_val, d_gate, up, conv_w, w_up4, h1, dh2, g_ffn):
    L, D = h1.shape
    n = L // TF
    nk = D_FF // FF_BLK

    def body(vp, vc, vn, gp, gc, gn, uv_ref, ug_ref, wv_ref, wg_ref, wu_ref, h1_ref, dh2_ref, g_ref,
             dup_ref, dh1_ref, dh1b_ref, gg_ref, gcw_ref, acc_ref):
        i = pl.program_id(0)
        k = pl.program_id(1)

        @pl.when((i == 0) & (k == 0))
        def _():
            gg_ref[...] = jnp.zeros_like(gg_ref)
            gcw_ref[...] = jnp.zeros_like(gcw_ref)

        @pl.when(k == 0)
        def _():
            acc_ref[...] = jnp.zeros_like(acc_ref)

        acc = jnp.zeros((TF, D), F32)
        for j, (blocks, u_ref, w_ref) in enumerate((((vp, vc, vn), uv_ref, wv_ref), ((gp, gc, gn), ug_ref, wg_ref))):
            for cs in _col_chunks(FF_BLK):
                d = blocks[1][:, cs]
                before, after = _neighbours(d, blocks[0], blocks[2], cs, i, n)
                taps = (after, d.astype(F32), before)
                w = w_ref[:, cs]
                d_up = (taps[0] * w[0:1, :] + taps[1] * w[1:2, :] + taps[2] * w[2:3, :]).astype(BF16)
                dup_ref[j, :, cs] = d_up
                acc = acc + _dot_nt(d_up, wu_ref[k + j * nk, :, cs])
                x = u_ref[:, cs].astype(F32)
                for r in range(3):
                    gcw_ref[j, k, r:r + 1, cs] += jnp.sum(taps[r] * x, axis=0, keepdims=True)
        acc_ref[...] += acc

        @pl.when(k == nk - 1)
        def _():
            g = g_ref[...]
            _, xh, inv = _rms_fwd(h1_ref[...], g)
            dx, dg = _rms_bwd(acc_ref[...], xh, inv, g)
            gg_ref[...] += dg
            dh1 = dh2_ref[...] + dx
            dh1_ref[...] = dh1
            dh1b_ref[...] = dh1.astype(BF16)

    row = pl.BlockSpec((TF, D), lambda i, k: (i, 0))
    cw = lambda off: pl.BlockSpec((3, FF_BLK), lambda i, k: (0, k + off))
    tile = lambda off: pl.BlockSpec((TF, FF_BLK), lambda i, k: (i, k + off))
    return pl.pallas_call(
        body, name="ffn_up_bwd", grid=(n, nk),
        in_specs=_halo_specs_2d(TF, FF_BLK, L, lambda k: k, "ik") + _halo_specs_2d(TF, FF_BLK, L, lambda k: k, "ik")
        + [tile(0), tile(nk), cw(0), cw(nk), _full(w_up4.shape), row, row, _full((1, D))],
        out_specs=[pl.BlockSpec((2, None, TF, FF_BLK), lambda i, k: (0, k, i, 0)), row, row, _full((1, D)),
                   _full((2, nk, 3, FF_BLK))],
        out_shape=[jax.ShapeDtypeStruct((2, nk, L, FF_BLK), BF16), jax.ShapeDtypeStruct((L, D), F32),
                   jax.ShapeDtypeStruct((L, D), BF16), jax.ShapeDtypeStruct((1, D), F32),
                   jax.ShapeDtypeStruct((2, nk, 3, FF_BLK), F32)],
        scratch_shapes=[pltpu.VMEM((TF, D), F32)],
        compiler_params=_cp("arbitrary", "arbitrary"))(
            d_val, d_val, d_val, d_gate, d_gate, d_gate, up, up, conv_w, conv_w, w_up4, h1, dh2, g_ffn)


def _matmul_tn(a, b, tm, tn, name, tk=2048):
    L, M = a.shape
    N = b.shape[1]
    tk = min(tk, L)

    def body(a_ref, b_ref, o_ref):
        @pl.when(pl.program_id(2) == 0)
        def _():
            o_ref[...] = jnp.zeros_like(o_ref)

        o_ref[...] += _dot_tn(a_ref[...], b_ref[...])

    return pl.pallas_call(
        body, name=name, grid=(M // tm, N // tn, L // tk),
        in_specs=[pl.BlockSpec((tk, tm), lambda m, n, l: (l, m)), pl.BlockSpec((tk, tn), lambda m, n, l: (l, n))],
        out_specs=pl.BlockSpec((tm, tn), lambda m, n, l: (m, n)),
        out_shape=jax.ShapeDtypeStruct((M, N), F32),
        compiler_params=_cp("parallel", "parallel", "arbitrary"))(a, b)


def _matmul_tn_blocks(a, b, tm, name, tk=2048):
    L, M = a.shape
    J, _, N = b.shape
    tk = min(tk, L)

    def body(a_ref, b_ref, o_ref):
        @pl.when(pl.program_id(2) == 0)
        def _():
            o_ref[...] = jnp.zeros_like(o_ref)

        o_ref[...] += _dot_tn(a_ref[...], b_ref[...])

    return pl.pallas_call(
        body, name=name, grid=(M // tm, J, L // tk),
        in_specs=[pl.BlockSpec((tk, tm), lambda m, j, l: (l, m)), pl.BlockSpec((None, tk, N), lambda m, j, l: (j, l, 0))],
        out_specs=pl.BlockSpec((None, tm, N), lambda m, j, l: (j, m, 0)),
        out_shape=jax.ShapeDtypeStruct((J, M, N), F32),
        compiler_params=_cp("parallel", "parallel", "arbitrary"))(a, b)


def _row_tile(rows):
    for t in (512, 352, 256, 128, 64, 8):
        if rows % t == 0:
            return t
    return rows


def _add_half(g, r, c_arr, name, out_dtype=F32):
    _, _, R, C = g.shape
    tr = _row_tile(R)

    def body(c_ref, g_ref, r_ref, o_ref):
        o_ref[...] = (g_ref[...] + r_ref[...]).astype(out_dtype)

    return pl.pallas_call(
        body, name=name,
        grid_spec=pltpu.PrefetchScalarGridSpec(
            num_scalar_prefetch=1, grid=(g.shape[0], R // tr),
            in_specs=[pl.BlockSpec((None, None, tr, C), lambda j, i, c: (j, c[0], i, 0)),
                      pl.BlockSpec((None, tr, C), lambda j, i, c: (j, i, 0))],
            out_specs=pl.BlockSpec((None, tr, C), lambda j, i, c: (j, i, 0))),
        out_shape=jax.ShapeDtypeStruct(r.shape, out_dtype),
        compiler_params=_cp("parallel", "parallel"))(c_arr, g, r)


def _sum4(p, name):
    _, R, C = p.shape
    tr = _row_tile(R)

    def body(p_ref, o_ref):
        q = [p_ref[j].astype(F32) for j in range(4)]
        o_ref[...] = ((q[0] + q[1]) + q[2]) + q[3]

    return pl.pallas_call(
        body, name=name, grid=(R // tr,),
        in_specs=[pl.BlockSpec((4, tr, C), lambda i: (0, i, 0))],
        out_specs=pl.BlockSpec((tr, C), lambda i: (i, 0)),
        out_shape=jax.ShapeDtypeStruct((R, C), F32), compiler_params=_cp("parallel"))(p)


def _adamw_refs(w_ref, g_ref, m_ref, v_ref, d_ref, nm_ref, nv_ref):
    gv = g_ref[...]
    nm = ADAM_B1 * m_ref[...] + (1.0 - ADAM_B1) * gv
    nv = ADAM_B2 * v_ref[...] + (1.0 - ADAM_B2) * (gv * gv)
    m_hat = nm / (1.0 - ADAM_B1 ** ADAM_STEP)
    v_hat = nv / (1.0 - ADAM_B2 ** ADAM_STEP)
    d_ref[...] = -ADAM_LR * (m_hat / (jnp.sqrt(v_hat) + ADAM_EPS) + ADAM_WD * w_ref[...])
    nm_ref[...] = nm
    nv_ref[...] = nv


def _adamw_many(ws, gs, ms, vs, name):
    n = len(ws)

    def body(*refs):
        for k in range(n):
            _adamw_refs(*(refs[j * n + k] for j in range(7)))

    out_shape = [jax.ShapeDtypeStruct(w.shape, F32) for w in ws] * 3
    res = pl.pallas_call(body, name=name, out_shape=out_shape,
                         compiler_params=pltpu.CompilerParams(vmem_limit_bytes=VMEM_LIMIT))(*ws, *gs, *ms, *vs)
    return res[:n], res[n:2 * n], res[2 * n:]


def _join_rows(own, other, c_arr, name):
    R, C = own.shape
    tr = _row_tile(R)

    def body(c_ref, own_ref, other_ref, o_ref):
        o_ref[...] = jnp.where(pl.program_id(0) == c_ref[0], own_ref[...], other_ref[...])

    half = pl.BlockSpec((tr, C), lambda h, i, c: (i, 0))
    return pl.pallas_call(
        body, name=name,
        grid_spec=pltpu.PrefetchScalarGridSpec(
            num_scalar_prefetch=1, grid=(2, R // tr), in_specs=[half, half],
            out_specs=pl.BlockSpec((tr, C), lambda h, i, c: (h * (R // tr) + i, 0))),
        out_shape=jax.ShapeDtypeStruct((2 * R, C), F32),
        compiler_params=_cp("parallel", "parallel"))(c_arr, own, other)


def _adamw_halves(w, own, other, m, v, name, comm=None):
    R, C = own.shape
    tr = _row_tile(R)
    while tr * C * 4 > ADAMW_BLOCK_BYTES and tr % 16 == 0:
        tr //= 2

    def body(w_ref, own_ref, other_ref, m_ref, v_ref, g_ref, d_ref, nm_ref, nv_ref):
        g_ref[...] = jnp.where(pl.program_id(0) == lax.axis_index("c"), own_ref[...], other_ref[...])
        _adamw_refs(w_ref, g_ref, m_ref, v_ref, d_ref, nm_ref, nv_ref)

    half = pl.BlockSpec((tr, C), lambda h, i: (i, 0))
    full = pl.BlockSpec((tr, C), lambda h, i: (h * (R // tr) + i, 0))
    sh = jax.ShapeDtypeStruct((2 * R, C), F32)
    return _hosted_call(body, comm, name=name, grid=(2, R // tr), in_specs=[full, half, half, full, full],
                        out_specs=[full] * 4, out_shape=[sh] * 4, scratch_shapes=[], args=(w, own, other, m, v))


_ANY = pl.BlockSpec(memory_space=pl.ANY)


def _position():
    return lax.axis_index("x"), lax.axis_index("y"), lax.axis_index("c")


class _Comm:
    def __init__(self, arrs, out_shape, sems, start, finish):
        self.arrs, self.out_shape, self.sems, self.start, self.finish = arrs, out_shape, sems, start, finish


def _comm_call(comm, name):
    n, m = len(comm.arrs), len(comm.out_shape)

    def body(*refs):
        ins, outs, sems = refs[:n], refs[n:n + m], refs[n + m:]
        comm.start(ins, outs, sems)
        comm.finish(ins, outs, sems)

    return pl.pallas_call(
        body, name=name, in_specs=[_ANY] * n, out_specs=[_ANY] * m, out_shape=comm.out_shape,
        scratch_shapes=comm.sems, compiler_params=pltpu.CompilerParams(has_side_effects=True))(*comm.arrs)


def _hosted_call(body, comm, *, name, grid, in_specs, out_specs, out_shape, scratch_shapes, args):
    sem = ("arbitrary",) * len(grid)
    if comm is None:
        return pl.pallas_call(body, name=name, grid=grid, in_specs=in_specs, out_specs=out_specs, out_shape=out_shape,
                              scratch_shapes=scratch_shapes, compiler_params=_cp(*sem))(*args), []
    n_in, n_out, n_scr = len(in_specs), len(out_specs), len(scratch_shapes)
    ci, co = len(comm.arrs), len(comm.out_shape)

    def full(*refs):
        ins, refs = refs[:n_in], refs[n_in:]
        cins, refs = refs[:ci], refs[ci:]
        outs, refs = refs[:n_out], refs[n_out:]
        couts, refs = refs[:co], refs[co:]
        scr, csems = refs[:n_scr], refs[n_scr:]
        first, last = True, True
        for d, size in enumerate(grid):
            first = first & (pl.program_id(d) == 0)
            last = last & (pl.program_id(d) == size - 1)

        @pl.when(first)
        def _():
            comm.start(cins, couts, csems)

        body(*ins, *outs, *scr)

        @pl.when(last)
        def _():
            comm.finish(cins, couts, csems)

    res = pl.pallas_call(
        full, name=name, grid=grid, in_specs=list(in_specs) + [_ANY] * ci, out_specs=list(out_specs) + [_ANY] * co,
        out_shape=list(out_shape) + list(comm.out_shape), scratch_shapes=list(scratch_shapes) + list(comm.sems),
        compiler_params=_cp(*sem))(*args, *comm.arrs)
    return res[:n_out], res[n_out:]


def _comm_join(*comms):
    def parts(xs, attr):
        out, at = [], 0
        for cm in comms:
            n = len(getattr(cm, attr))
            out.append(xs[at:at + n])
            at += n
        return out

    def start(ins, outs, sems):
        for cm, i, o, s in zip(comms, parts(ins, "arrs"), parts(outs, "out_shape"), parts(sems, "sems")):
            cm.start(i, o, s)

    def finish(ins, outs, sems):
        for cm, i, o, s in zip(comms, parts(ins, "arrs"), parts(outs, "out_shape"), parts(sems, "sems")):
            cm.finish(i, o, s)

    cat = lambda attr: [x for cm in comms for x in getattr(cm, attr)]
    return _Comm(cat("arrs"), cat("out_shape"), cat("sems"), start, finish)


def _dma_sems(*counts):
    return [pltpu.SemaphoreType.DMA((n,)) for n in counts]


def _comm_pair_swap(arrs, half=False):
    n = len(arrs)
    out_shape = [jax.ShapeDtypeStruct(a.shape[:1] + a.shape[2:] if half else a.shape, a.dtype) for a in arrs]

    def copies(ins, outs, sems):
        x, y, c = _position()
        return [pltpu.make_async_remote_copy(
            src_ref=ins[k].at[:, 1 - c] if half else ins[k], dst_ref=outs[k], send_sem=sems[0].at[k],
            recv_sem=sems[1].at[k], device_id=(x, y, 1 - c), device_id_type=MESH) for k in range(n)]

    def start(ins, outs, sems):
        for cp in copies(ins, outs, sems):
            cp.start()

    def finish(ins, outs, sems):
        for cp in copies(ins, outs, sems):
            cp.wait()

    return _Comm(arrs, out_shape, _dma_sems(n, n), start, finish)


def _chip_of(j, c):
    return (jnp.right_shift(j, 1), jnp.bitwise_and(j, 1), c)


def _comm_chip_exchange(arrs, scatter):
    n = len(arrs)
    out_shape = [jax.ShapeDtypeStruct(a.shape if scatter else (4,) + a.shape, a.dtype) for a in arrs]

    def copies(ins, outs, sems):
        x, y, c = _position()
        me = 2 * x + y
        local, sent, landed = [], [], []
        for k in range(n):
            local.append(pltpu.make_async_copy(ins[k].at[me] if scatter else ins[k], outs[k].at[me], sems[2].at[k]))
            for d in (1, 2, 3):
                j = jnp.bitwise_xor(me, d)
                s = 3 * k + d - 1
                src = ins[k].at[j] if scatter else ins[k]
                for dst, group in ((outs[k].at[me], sent), (outs[k].at[j], landed)):
                    group.append(pltpu.make_async_remote_copy(
                        src_ref=src, dst_ref=dst, send_sem=sems[0].at[s], recv_sem=sems[1].at[s],
                        device_id=_chip_of(j, c), device_id_type=MESH))
        return local, sent, landed

    def start(ins, outs, sems):
        local, sent, _ = copies(ins, outs, sems)
        for cp in local + sent:
            cp.start()

    def finish(ins, outs, sems):
        local, sent, landed = copies(ins, outs, sems)
        for cp in sent:
            cp.wait_send()
        for cp in landed:
            cp.wait_recv()
        for cp in local:
            cp.wait()

    return _Comm(arrs, out_shape, _dma_sems(3 * n, 3 * n, n), start, finish)


LOCAL_PARTS = 4


def _comm_gather_split(shards, whole):
    n, nw = len(shards), len(whole)
    arrs = list(shards) + list(whole)
    out_shape = [jax.ShapeDtypeStruct((4,) + a.shape, a.dtype) for a in arrs]

    def copies(ins, outs, sems):
        x, y, c = _position()
        me = 2 * x + y
        local, sent, landed, passed, passed_in = [], [], [], [], []
        for k in range(n + nw):
            if k >= n:
                local.append(pltpu.make_async_copy(ins[k], outs[k].at[me], sems[4].at[LOCAL_PARTS * k]))
            else:
                part = shards[k].shape[0] // LOCAL_PARTS
                for r in range(LOCAL_PARTS):
                    local.append(pltpu.make_async_copy(ins[k].at[pl.ds(r * part, part)],
                                                       outs[k].at[me, pl.ds(r * part, part)],
                                                       sems[4].at[LOCAL_PARTS * k + r]))
            for d in (1, 2, 3):
                j = jnp.bitwise_xor(me, d)
                s = 3 * k + d - 1
                if k >= n:
                    src, mine, theirs = ins[k], outs[k].at[me], outs[k].at[j]
                else:
                    h = shards[k].shape[0] // 2
                    rows = pl.ds(pl.multiple_of(c * h, 16), h)
                    other = pl.ds(pl.multiple_of((1 - c) * h, 16), h)
                    src, mine, theirs = ins[k].at[rows], outs[k].at[me, rows], outs[k].at[j, rows]
                    for dst, group in ((theirs, passed), (outs[k].at[j, other], passed_in)):
                        group.append(pltpu.make_async_remote_copy(
                            src_ref=theirs, dst_ref=dst, send_sem=sems[2].at[s], recv_sem=sems[3].at[s],
                            device_id=(x, y, 1 - c), device_id_type=MESH))
                for dst, group in ((mine, sent), (theirs, landed)):
                    group.append(pltpu.make_async_remote_copy(
                        src_ref=src, dst_ref=dst, send_sem=sems[0].at[s], recv_sem=sems[1].at[s],
                        device_id=_chip_of(j, c), device_id_type=MESH))
        return local, sent, landed, passed, passed_in

    def start(ins, outs, sems):
        local, sent, _, _, _ = copies(ins, outs, sems)
        for cp in local + sent:
            cp.start()

    def finish(ins, outs, sems):
        local, sent, landed, passed, passed_in = copies(ins, outs, sems)
        for cp in landed[:3 * n]:
            cp.wait_recv()
        for cp in passed:
            cp.start()
        for cp in landed[3 * n:]:
            cp.wait_recv()
        for cp in sent:
            cp.wait_send()
        for cp in passed:
            cp.wait_send()
        for cp in passed_in:
            cp.wait_recv()
        for cp in local:
            cp.wait()

    t = 3 * (n + nw)
    return _Comm(arrs, out_shape, _dma_sems(t, t, max(3 * n, 1), max(3 * n, 1), LOCAL_PARTS * (n + nw)), start, finish)


def _pack(arrs, row_multiple):
    parts = []
    for a in arrs:
        flat = a.reshape(-1).astype(F32)
        pad = (-flat.shape[0]) % LANES
        parts.append(jnp.pad(flat, (0, pad)) if pad else flat)
    flat = jnp.concatenate(parts)
    rows = -(-flat.shape[0] // LANES)
    rows_p = -(-rows // row_multiple) * row_multiple
    return jnp.pad(flat, (0, rows_p * LANES - flat.shape[0])).reshape(rows_p, LANES)


def _unpack(packed, shapes):
    flat = packed.reshape(-1)
    outs, off = [], 0
    for sh in shapes:
        size = int(np.prod(sh))
        outs.append(flat[off:off + size].reshape(sh))
        off += size + (-size) % LANES
    return outs


SMALL = ["norm_mix_g", "pool_w", "pool_scale", "ssm_log_neg_a_re", "ssm_a_im", "ssm_log_dt", "ssm_b_re", "ssm_b_im",
         "ssm_c_re", "ssm_c_im", "ssm_d", "glu_b", "out_norm_pool_g", "out_norm_ssm_g", "norm_ffn_g", "conv_b",
         "final_norm_g"]
BIG = ["w_in", "glu_w", "w_out", "w_up", "w_down"]
WIDE = ["pool_w", "ssm_b_re", "ssm_b_im", "ssm_c_re", "ssm_c_im"]
WEIGHTS = ['norm_mix_g', 'w_in', 'pool_w', 'pool_scale', 'ssm_log_neg_a_re', 'ssm_a_im', 'ssm_log_dt', 'ssm_b_re',
           'ssm_b_im', 'ssm_c_re', 'ssm_c_im', 'ssm_d', 'glu_w', 'glu_b', 'out_norm_pool_g', 'out_norm_ssm_g', 'w_out',
           'norm_ffn_g', 'w_up', 'conv_w', 'conv_b', 'w_down', 'final_norm_g']


def _local_step(x, target, p, full, shards=None, c_arr=None):
    L, D = x.shape
    dist = shards is not None
    row = lambda a: a.reshape(1, -1)
    pool_w_b = p["pool_w"].astype(BF16)
    g_mix, g_pool, g_ssm, g_ffn, g_fin = (row(p[k]) for k in (
        "norm_mix_g", "out_norm_pool_g", "out_norm_ssm_g", "norm_ffn_g", "final_norm_g"))
    pool_scale, ssm_d, glu_b, conv_b = (row(p[k]) for k in ("pool_scale", "ssm_d", "glu_b", "conv_b"))

    lnar = p["ssm_log_neg_a_re"].reshape(2 * N_SSM_GROUPS, SSM_STATE)
    aim = p["ssm_a_im"].reshape(2 * N_SSM_GROUPS, SSM_STATE)
    ldt = jnp.broadcast_to(p["ssm_log_dt"].reshape(2 * N_SSM_GROUPS, 1), lnar.shape)
    lam_re, lam_im, f_re, f_im = _ssm_params(lnar, aim, ldt)
    flat2 = lambda a: a.reshape(2, N_STATE)
    lam4 = jnp.stack([flat2(lam_re)[0], flat2(lam_im)[0], flat2(lam_re)[1], flat2(lam_im)[1]])
    tables = _scan_tables(lam4)
    per_group = (2, N_SSM_GROUPS, SSM_STATE)
    dense, got0 = _ssm_expand(p["ssm_b_re"], p["ssm_b_im"], p["ssm_c_re"], p["ssm_c_im"],
                              f_re.reshape(per_group + (1,)), f_im.reshape(per_group + (1,)),
                              comm=_comm_gather_split([shards["w_in"]], []) if dist else None)
    w_in = got0[0].reshape(-1, got0[0].shape[-1]) if dist else full["w_in"]
    ssm_args = [tuple(dense[4 * d:4 * d + 4]) + (tables,) for d in range(2)]

    u, xn = _in_proj(x, g_mix, w_in)
    yn_pool = _pool_fwd(u, pool_w_b, pool_scale, g_pool)
    gather1 = _comm_gather_split([shards[k] for k in ("glu_w", "w_out", "w_down")], [shards["conv_w"]]) if dist else None
    (y0, s0r, s0i), got1 = _ssm_scan_fwd(u, *ssm_args[0], 0, False, comm=gather1)
    gather2 = _comm_gather_split([shards["w_up"]], []) if dist else None
    (y1, s1r, s1i), got2 = _ssm_scan_fwd(u, *ssm_args[1], 2, True, comm=gather2)
    if dist:
        glu_w, w_out, w_down = (g.reshape((-1,) + g.shape[2:]) for g in got1[:3])
        conv_w = jnp.transpose(got1[3], (1, 0, 2)).reshape(3, -1)
        w_up4 = got2[0]
    else:
        glu_w, w_out, w_up4, w_down, conv_w = (full[k] for k in ("glu_w", "w_out", "w_up", "w_down", "conv_w"))
    h1, hn, ycat = _mix_out(yn_pool, y0, y1, u, x, ssm_d, glu_w, glu_b, g_ssm, w_out, g_ffn)
    up = _ffn_up(hn, w_up4)
    a, c_val, c_gate, dh2, dh2_b, loss, g_final = _ffn_down_loss(up, conv_w, conv_b, w_down, h1, target, g_fin)

    d_val, d_gate, gbv, gbg = _ffn_act_bwd(c_val, c_gate, w_down, dh2_b)
    g_w_down = _matmul_tn(a, dh2_b, FF_BLK, D, "grad_w_down")
    d_up, dh1, dh1_b, g_ffn_g, gcw = _ffn_up_bwd(d_val, d_gate, up, conv_w, w_up4, h1, dh2, g_ffn)
    g_w_up = _matmul_tn_blocks(hn, d_up.reshape(4, L, FF_BLK), TM, "grad_w_up")
    g_w_out = _matmul_tn(ycat, dh1_b, TM, D, "grad_w_out")
    late = ("w_up", "w_down", "w_out", "glu_w")
    halves = [g_w_up.reshape(4, 2, D // 2, FF_BLK), g_w_down.reshape(4, 2, D_FF // 8, D)]
    (dy, du_direct, g_glu_w, g_glu_b, g_ssm_d, g_ssm_g), swapped = _ssm_bwd_local(
        dh1_b, y0, y1, u, ssm_d, glu_w, glu_b, g_ssm, w_out, comm=_comm_pair_swap(halves, half=True) if dist else None)
    more = [g_w_out.reshape(4, 2, D // 8, D), g_glu_w.reshape(4, 2, D_SSM // 8, D_SSM)]
    (d_pooled, g_pool_w, g_pool_scale, g_pool_g), swapped_more = _pool_bwd_local(
        dh1_b, u, w_out, pool_w_b, pool_scale, g_pool, comm=_comm_pair_swap(more, half=True) if dist else None)
    halves, from_sibling = halves + more, list(swapped) + list(swapped_more)
    du_pool = _pool_bwd_window(d_pooled)
    reduce_a, reduce_b = None, None
    if dist:
        chip_sums = [_add_half(h, r, c_arr, "sum_pair_" + k, BF16) for k, h, r in zip(late, halves, from_sibling)]
        reduce_a = _comm_chip_exchange(chip_sums[:1], scatter=True)
        reduce_b = _comm_chip_exchange(chip_sums[1:], scatter=True)
    (du0, gb0r, gb0i, gc0r, gc0i, gv0), chips_a = _ssm_scan_bwd(dy, u, s0r, s0i, *ssm_args[0], 1, True, comm=reduce_a)
    (du1, gb1r, gb1i, gc1r, gc1i, gv1), chips_b = _ssm_scan_bwd(dy, u, s1r, s1i, *ssm_args[1], 3, False, comm=reduce_b)
    mine = [_sum4(r, "sum_chips_" + k) for k, r in zip(late, list(chips_a) + list(chips_b))]
    by_state = (2, N_SSM_GROUPS, 1, SSM_STATE)
    g_b_re, g_b_im, g_f_re, g_f_im = _ssm_unfold(
        jnp.stack([gb0r, gb1r]), jnp.stack([gb0i, gb1i]),
        jnp.swapaxes(p["ssm_b_re"], 2, 3), jnp.swapaxes(p["ssm_b_im"], 2, 3),
        f_re.reshape(by_state), f_im.reshape(by_state))
    gvec = lambda j: jnp.stack([gv0[j], gv1[j]]).reshape(2 * N_SSM_GROUPS, SSM_STATE)
    g_lnar, g_aim, g_ldt = _ssm_params_bwd(lnar, aim, ldt, gvec(0), gvec(1),
                                           g_f_re.reshape(lnar.shape), g_f_im.reshape(lnar.shape))
    (grad_x, d_u_b, g_mix_g), theirs = _in_bwd(du_pool, du_direct, du0, du1, dh1, x, g_mix, w_in,
                                               comm=_comm_pair_swap(mine) if dist else None)
    g_w_in = _matmul_tn(xn, d_u_b, TM, D, "grad_w_in")

    small = {
        "norm_mix_g": g_mix_g, "pool_w": g_pool_w, "pool_scale": g_pool_scale,
        "ssm_log_neg_a_re": g_lnar, "ssm_a_im": g_aim, "ssm_log_dt": g_ldt,
        "ssm_b_re": jnp.swapaxes(g_b_re, 2, 3), "ssm_b_im": jnp.swapaxes(g_b_im, 2, 3),
        "ssm_c_re": jnp.stack([gc0r, gc1r]), "ssm_c_im": jnp.stack([gc0i, gc1i]),
        "ssm_d": g_ssm_d, "glu_b": g_glu_b, "out_norm_pool_g": g_pool_g, "out_norm_ssm_g": g_ssm_g,
        "norm_ffn_g": g_ffn_g, "conv_b": jnp.concatenate([gbv[0], gbg[0]]), "final_norm_g": g_final,
        "conv_w": jnp.transpose(gcw, (2, 0, 1, 3)).reshape(3, -1),
    }
    big = {"w_in": g_w_in}
    reduced = dict(zip(late, zip(mine, theirs)))
    if not dist:
        big.update({"w_up": g_w_up, "w_down": g_w_down, "w_out": g_w_out, "glu_w": g_glu_w})
    return loss, grad_x, small, big, reduced


def kernel(x, norm_mix_g, w_in, pool_w, pool_scale, ssm_log_neg_a_re, ssm_a_im, ssm_log_dt, ssm_b_re, ssm_b_im, ssm_c_re, ssm_c_im, ssm_d, glu_w, glu_b, out_norm_pool_g, out_norm_ssm_g, w_out, norm_ffn_g, w_up, conv_w, conv_b, w_down, final_norm_g, loss_target, m_norm_mix_g, m_w_in, m_pool_w, m_pool_scale, m_ssm_log_neg_a_re, m_ssm_a_im, m_ssm_log_dt, m_ssm_b_re, m_ssm_b_im, m_ssm_c_re, m_ssm_c_im, m_ssm_d, m_glu_w, m_glu_b, m_out_norm_pool_g, m_out_norm_ssm_g, m_w_out, m_norm_ffn_g, m_w_up, m_conv_w, m_conv_b, m_w_down, m_final_norm_g, v_norm_mix_g, v_w_in, v_pool_w, v_pool_scale, v_ssm_log_neg_a_re, v_ssm_a_im, v_ssm_log_dt, v_ssm_b_re, v_ssm_b_im, v_ssm_c_re, v_ssm_c_im, v_ssm_d, v_glu_w, v_glu_b, v_out_norm_pool_g, v_out_norm_ssm_g, v_w_out, v_norm_ffn_g, v_w_up, v_conv_w, v_conv_b, v_w_down, v_final_norm_g):
    args = locals()
    w = {k: args[k] for k in WEIGHTS}
    m = {k: args["m_" + k] for k in WEIGHTS}
    v = {k: args["v_" + k] for k in WEIGHTS}
    chip = 2 * lax.axis_index("x") + lax.axis_index("y")
    c_arr = lax.axis_index("c").astype(jnp.int32).reshape(1)

    shards = {k: w[k].astype(BF16) for k in BIG}
    shards["conv_w"] = conv_w
    loss, grad_x, g_small, g_big, reduced = _local_step(x[0], loss_target[0], w, {}, shards, c_arr)

    exact = [k for k in SMALL if k not in WIDE]
    packs = [_pack([loss] + [g_small[k] for k in exact] + [g_small["conv_w"]], 512),
             _pack([g_small[k] for k in WIDE], 512)]
    halves = [g_big["w_in"].reshape(4, 2, g_big["w_in"].shape[0] // 8, -1)]
    halves += [pk.reshape(1, 2, pk.shape[0] // 2, LANES) for pk in packs]
    from_sibling = _comm_call(_comm_pair_swap(halves, half=True), "reduce_pair")
    names = ("w_in", "exact", "wide")
    sums = [_add_half(h, r, c_arr, "sum_pair_" + k, dt)
            for k, h, r, dt in zip(names, halves, from_sibling, (BF16, F32, BF16))]
    grads, delta, new_m, new_v = {}, {}, {}, {}

    def adamw_behind(k, comm):
        own, other = reduced[k]
        (grads[k], delta[k], new_m[k], new_v[k]), got = _adamw_halves(
            w[k], own, other, m[k], v[k], "adamw_" + k, comm=comm)
        return got

    from_chips = adamw_behind("w_up", _comm_join(_comm_chip_exchange(sums[:1], scatter=True),
                                                  _comm_chip_exchange([s[0] for s in sums[1:]], scatter=False)))
    mine = [_sum4(r, "sum_chips_" + k) for k, r in zip(names, from_chips)]
    theirs = adamw_behind("w_down", _comm_pair_swap(mine))
    for k in ("w_out", "glu_w"):
        adamw_behind(k, None)
    exact_all = _join_rows(mine[1], theirs[1], c_arr, "join_exact")
    wide_all = _join_rows(mine[2], theirs[2], c_arr, "join_wide")
    shapes = [loss.shape] + [w[k].shape for k in exact] + [(3, 4 * FF_BLK)]
    grads.update(zip(["loss"] + exact + ["conv_w_full"], _unpack(exact_all, shapes)))
    grads.update(zip(WIDE, _unpack(wide_all, [w[k].shape for k in WIDE])))
    loss = grads.pop("loss")[0, 0]
    grads["conv_w"] = lax.dynamic_slice_in_dim(grads.pop("conv_w_full"), chip * FF_BLK, FF_BLK, axis=1)

    reduced["w_in"] = (mine[0], theirs[0])
    adamw_behind("w_in", None)
    padded = ["ssm_b_re", "ssm_b_im"]
    for keys, name in ((padded, "adamw_ssm_b"), ([k for k in SMALL + ["conv_w"] if k not in padded], "adamw_small")):
        outs = _adamw_many(*([d[k] for k in keys] for d in (w, grads, m, v)), name)
        for d, o in zip((delta, new_m, new_v), outs):
            d.update(zip(keys, o))

    return (loss, grad_x[None], *[grads[k] for k in WEIGHTS], *[delta[k] for k in WEIGHTS],
            *[new_m[k] for k in WEIGHTS], *[new_v[k] for k in WEIGHTS])
```

```python
import numpy as np
import jax
import jax.numpy as jnp
from jax import lax
from jax.experimental import pallas as pl
from jax.experimental.pallas import tpu as pltpu

F32 = jnp.float32
BF16 = jnp.bfloat16
MESH = pl.DeviceIdType.MESH

EPS = 1e-6
POOL_WINDOWS = (2, 4, 8, 16)
POOL_GROUP = 128
SSM_GROUP = 16
SSM_STATE = 64
N_SSM_GROUPS = 32
N_STATE = N_SSM_GROUPS * SSM_STATE
QUAD = 256
N_QUAD = N_STATE // QUAD
SLAB = 256
D_SSM = 512
D_POOL = 512
D_FF = 2816
FF_BLK = 1408
HALO = 8
HALO_B = 16
LANES = 128
ADAM_LR, ADAM_B1, ADAM_B2, ADAM_EPS, ADAM_WD, ADAM_STEP = 0.001, 0.9, 0.999, 1e-08, 0.01, 10
VMEM_LIMIT = 56 * 2 ** 20
ADAMW_BLOCK_BYTES = 2 ** 20

TL = 512
TM = 1024
TF = 256
TC = 512
SEG = 8
SEG_LEN = TC // SEG
SCAN_UNROLL = 4
SCAN_W = 512


def _cp(*sem):
    return pltpu.CompilerParams(dimension_semantics=sem, vmem_limit_bytes=VMEM_LIMIT)


def _dot_nn(a, b):
    return jnp.dot(a, b, preferred_element_type=F32)


def _dot_nt(a, b):
    return lax.dot_general(a, b, (((1,), (1,)), ((), ())), preferred_element_type=F32)


def _dot_tn(a, b):
    return lax.dot_general(a, b, (((0,), (0,)), ((), ())), preferred_element_type=F32)


def _rms_fwd(x, g):
    inv = lax.rsqrt(jnp.mean(x * x, axis=-1, keepdims=True) + EPS)
    xh = x * inv
    return xh * g, xh, inv


def _rms_bwd(dy, xh, inv, g):
    dg = jnp.sum(dy * xh, axis=0, keepdims=True)
    dxh = dy * g
    dx = inv * (dxh - xh * jnp.mean(dxh * xh, axis=-1, keepdims=True))
    return dx, dg


_GELU_C = 0.7978845608028654
_GELU_A = 0.044715


def _gelu(y):
    t = jnp.tanh(_GELU_C * (y + _GELU_A * (y * y * y)))
    return 0.5 * y * (1.0 + t), t


def _gelu_grad(y, t):
    return 0.5 * (1.0 + t) + 0.5 * y * (1.0 - t * t) * (_GELU_C * (1.0 + 3.0 * _GELU_A * y * y))


def _sigmoid(x):
    return 1.0 / (1.0 + jnp.exp(-x))


def _full(shape):
    n = len(shape)
    return pl.BlockSpec(shape, lambda *_: (0,) * n)


def _fill_ext(ext_ref, prev_ref, cur_ref, next_ref, i, n, rows):
    ext_ref[0:HALO, :] = jnp.where(i > 0, prev_ref[...], 0.0).astype(ext_ref.dtype)
    ext_ref[HALO:HALO + rows, :] = cur_ref[...]
    ext_ref[HALO + rows:2 * HALO + rows, :] = jnp.where(i < n - 1, next_ref[...], 0.0).astype(ext_ref.dtype)


def _in_proj(x, g, w):
    L, D = x.shape
    E = w.shape[1]
    assert TL == TC
    order = jnp.asarray(_SEGMENT_ORDER, BF16)

    def body(x_ref, g_ref, w_ref, ord_ref, u_ref, xn_ref, useg_ref):
        y, _, _ = _rms_fwd(x_ref[...], g_ref[...])
        yb = y.astype(BF16)
        xn_ref[...] = yb
        u = _dot_nn(yb, w_ref[...])
        u_ref[...] = u
        useg_ref[...] = _dot_nn(ord_ref[...], u[:, D_POOL:].astype(BF16)).astype(BF16)

    return pl.pallas_call(
        body, name="in_proj", grid=(L // TL,),
        in_specs=[pl.BlockSpec((TL, D), lambda i: (i, 0)), _full((1, D)), _full(w.shape), _full(order.shape)],
        out_specs=[pl.BlockSpec((TL, E), lambda i: (i, 0)), pl.BlockSpec((TL, D), lambda i: (i, 0)),
                   pl.BlockSpec((TL, D_SSM), lambda i: (i, 0))],
        out_shape=[jax.ShapeDtypeStruct((L, E), F32), jax.ShapeDtypeStruct((L, D), BF16),
                   jax.ShapeDtypeStruct((L, D_SSM), BF16)],
        compiler_params=_cp("parallel"))(x, g, w, order)


def _halo_specs_1d(rows, width, L, col):
    rb = rows // HALO
    last = L // HALO - 1
    return [pl.BlockSpec((HALO, width), lambda i: (jnp.maximum(i * rb - 1, 0), col)),
            pl.BlockSpec((rows, width), lambda i: (i, col)),
            pl.BlockSpec((HALO, width), lambda i: (jnp.minimum((i + 1) * rb, last), col))]


def _pooled_from_ext(ext_ref, t0, rows, L):
    t = t0 + lax.broadcasted_iota(jnp.int32, (rows, 1), 0)
    outs = []
    for gi, w in enumerate(POOL_WINDOWS):
        half = w // 2
        cs = slice(gi * POOL_GROUP, (gi + 1) * POOL_GROUP)
        acc = ext_ref[pl.ds(HALO - half, rows), cs]
        for s in range(-half + 1, half):
            acc = acc + ext_ref[pl.ds(HALO + s, rows), cs]
        cnt = (jnp.minimum(t + half, L) - jnp.maximum(t - half, 0)).astype(F32)
        outs.append(acc / cnt - ext_ref[pl.ds(HALO, rows), cs])
    return outs


def _pool_fwd(u, pool_w_b, pool_scale, g_pool):
    L = u.shape[0]
    n = L // TL

    def body(prev_ref, cur_ref, next_ref, pw_ref, ps_ref, g_ref, out_ref, ext_ref):
        i = pl.program_id(0)
        _fill_ext(ext_ref, prev_ref, cur_ref, next_ref, i, n, TL)
        pooled = _pooled_from_ext(ext_ref, i * TL, TL, L)
        ypre = jnp.concatenate([_dot_nn(pooled[gi].astype(BF16), pw_ref[gi]) for gi in range(4)], axis=-1)
        yn, _, _ = _rms_fwd(ypre * ps_ref[...], g_ref[...])
        out_ref[...] = yn.astype(BF16)

    return pl.pallas_call(
        body, name="pool_fwd", grid=(n,),
        in_specs=_halo_specs_1d(TL, D_POOL, L, 0) + [_full(pool_w_b.shape), _full((1, D_POOL)), _full((1, D_POOL))],
        out_specs=pl.BlockSpec((TL, D_POOL), lambda i: (i, 0)),
        out_shape=jax.ShapeDtypeStruct((L, D_POOL), BF16),
        scratch_shapes=[pltpu.VMEM((TL + 2 * HALO, D_POOL), F32)],
        compiler_params=_cp("parallel"))(u, u, u, pool_w_b, pool_scale, g_pool)


def _pool_bwd_local(dh1, u, w_out_b, pool_w_b, pool_scale, g_pool, comm=None):
    L = u.shape[0]
    n = L // TL
    D = dh1.shape[1]

    def body(dh_ref, prev_ref, cur_ref, next_ref, wo_ref, pw_ref, ps_ref, g_ref,
             dp_ref, gpw_ref, gps_ref, gg_ref, ext_ref):
        i = pl.program_id(0)

        @pl.when(i == 0)
        def _():
            gpw_ref[...] = jnp.zeros_like(gpw_ref)
            gps_ref[...] = jnp.zeros_like(gps_ref)
            gg_ref[...] = jnp.zeros_like(gg_ref)

        _fill_ext(ext_ref, prev_ref, cur_ref, next_ref, i, n, TL)
        pooled = [p.astype(BF16) for p in _pooled_from_ext(ext_ref, i * TL, TL, L)]
        ypre = jnp.concatenate([_dot_nn(pooled[gi], pw_ref[gi]) for gi in range(4)], axis=-1)
        ps = ps_ref[...]
        g = g_ref[...]
        _, xh, inv = _rms_fwd(ypre * ps, g)
        d_yn = _dot_nt(dh_ref[...], wo_ref[...])
        d_y, dg = _rms_bwd(d_yn, xh, inv, g)
        gg_ref[...] += dg
        gps_ref[...] += jnp.sum(d_y * ypre, axis=0, keepdims=True)
        d_ypre = (d_y * ps).astype(BF16)
        for gi in range(4):
            cs = slice(gi * POOL_GROUP, (gi + 1) * POOL_GROUP)
            dp_ref[:, cs] = _dot_nt(d_ypre[:, cs], pw_ref[gi])
            gpw_ref[gi] += _dot_tn(pooled[gi], d_ypre[:, cs])

    return _hosted_call(
        body, comm, name="pool_bwd_local", grid=(n,),
        in_specs=[pl.BlockSpec((TL, D), lambda i: (i, 0))] + _halo_specs_1d(TL, D_POOL, L, 0)
        + [pl.BlockSpec((D_POOL, D), lambda i: (0, 0)), _full(pool_w_b.shape), _full((1, D_POOL)), _full((1, D_POOL))],
        out_specs=[pl.BlockSpec((TL, D_POOL), lambda i: (i, 0)), _full(pool_w_b.shape),
                   _full((1, D_POOL)), _full((1, D_POOL))],
        out_shape=[jax.ShapeDtypeStruct((L, D_POOL), F32), jax.ShapeDtypeStruct(pool_w_b.shape, F32),
                   jax.ShapeDtypeStruct((1, D_POOL), F32), jax.ShapeDtypeStruct((1, D_POOL), F32)],
        scratch_shapes=[pltpu.VMEM((TL + 2 * HALO, D_POOL), F32)],
        args=(dh1, u, u, u, w_out_b, pool_w_b, pool_scale, g_pool))


def _pool_bwd_window(d_pooled):
    L = d_pooled.shape[0]
    n = L // TL
    R = TL + 2 * HALO

    def body(prev_ref, cur_ref, next_ref, out_ref, ext_ref, q_ref):
        i = pl.program_id(0)
        _fill_ext(ext_ref, prev_ref, cur_ref, next_ref, i, n, TL)
        tr = i * TL - HALO + lax.broadcasted_iota(jnp.int32, (R, 1), 0)
        for gi, w in enumerate(POOL_WINDOWS):
            half = w // 2
            cs = slice(gi * POOL_GROUP, (gi + 1) * POOL_GROUP)
            cnt = jnp.maximum(jnp.minimum(tr + half, L) - jnp.maximum(tr - half, 0), 1).astype(F32)
            q_ref[:, cs] = ext_ref[:, cs] / cnt
        for gi, w in enumerate(POOL_WINDOWS):
            half = w // 2
            cs = slice(gi * POOL_GROUP, (gi + 1) * POOL_GROUP)
            acc = q_ref[pl.ds(HALO - half + 1, TL), cs]
            for s in range(-half + 2, half + 1):
                acc = acc + q_ref[pl.ds(HALO + s, TL), cs]
            out_ref[:, cs] = acc - ext_ref[pl.ds(HALO, TL), cs]

    return pl.pallas_call(
        body, name="pool_bwd_window", grid=(n,),
        in_specs=_halo_specs_1d(TL, D_POOL, L, 0),
        out_specs=pl.BlockSpec((TL, D_POOL), lambda i: (i, 0)),
        out_shape=jax.ShapeDtypeStruct((L, D_POOL), F32),
        scratch_shapes=[pltpu.VMEM((R, D_POOL), F32), pltpu.VMEM((R, D_POOL), F32)],
        compiler_params=_cp("parallel"))(d_pooled, d_pooled, d_pooled)


def _ssm_param_fn(lnar, aim, ldt):
    dt = jnp.exp(ldt)
    a_re = -jnp.exp(lnar)
    mag = jnp.exp(a_re * dt)
    ang = aim * dt
    lr, li = mag * jnp.cos(ang), mag * jnp.sin(ang)
    den = a_re * a_re + aim * aim
    fr = ((lr - 1.0) * a_re + li * aim) / den
    fi = (li * a_re - (lr - 1.0) * aim) / den
    return lr, li, fr, fi


def _ssm_params(lnar, aim, ldt):
    def body(a_ref, b_ref, c_ref, lr_ref, li_ref, fr_ref, fi_ref):
        lr, li, fr, fi = _ssm_param_fn(a_ref[...], b_ref[...], c_ref[...])
        lr_ref[...] = lr
        li_ref[...] = li
        fr_ref[...] = fr
        fi_ref[...] = fi

    sh = jax.ShapeDtypeStruct(lnar.shape, F32)
    return pl.pallas_call(body, name="ssm_params", out_shape=[sh] * 4)(lnar, aim, ldt)


def _ssm_params_bwd(lnar, aim, ldt, glr, gli, gfr, gfi):
    def body(a_ref, b_ref, c_ref, g0, g1, g2, g3, da_ref, db_ref, dc_ref):
        _, vjp = jax.vjp(_ssm_param_fn, a_ref[...], b_ref[...], c_ref[...])
        da, db, dc = vjp((g0[...], g1[...], g2[...], g3[...]))
        da_ref[...] = da
        db_ref[...] = db
        dc_ref[...] = jnp.sum(dc, axis=1, keepdims=True)

    return pl.pallas_call(
        body, name="ssm_params_bwd",
        out_shape=[jax.ShapeDtypeStruct(lnar.shape, F32), jax.ShapeDtypeStruct(aim.shape, F32),
                   jax.ShapeDtypeStruct((ldt.shape[0], 1), F32)])(lnar, aim, ldt, glr, gli, gfr, gfi)


def _scan_tables(lam4):
    def build(lr, li, reverse, out_ref, k):
        pr, pi = lr, li
        for d in range(SEG_LEN):
            j = SEG_LEN - 1 - d if reverse else d
            out_ref[k, 0, j:j + 1, :] = pr
            out_ref[k, 1, j:j + 1, :] = pi
            pr, pi = pr * lr - pi * li, pr * li + pi * lr

    def body(lam_ref, out_ref):
        l0r, l0i, l1r, l1i = (lam_ref[j:j + 1, :] for j in range(4))
        build(l0r, l0i, False, out_ref, 0)
        build(l0r, -l0i, True, out_ref, 1)
        build(l1r, l1i, True, out_ref, 2)
        build(l1r, -l1i, False, out_ref, 3)

    return pl.pallas_call(body, name="scan_tables",
                          out_shape=jax.ShapeDtypeStruct((4, 2, SEG_LEN, N_STATE), F32))(lam4)


def _b_block(g):
    q, gl = divmod(g, 4)
    r0, c0 = gl * SSM_STATE, (q % 4) * 4 * SSM_GROUP + gl * SSM_GROUP
    return q, slice(r0, r0 + SSM_STATE), slice(c0, c0 + SSM_GROUP)


def _c_block(g):
    q, rows, cols = _b_block(g)
    return q, cols, rows


def _ssm_expand(b_re, b_im, c_re, c_im, f_re, f_im, comm=None):
    def body(bre_ref, bim_ref, cre_ref, cim_ref, fre_ref, fim_ref, *rest):
        outs, tmp, bbr_ref, bbi_ref = rest[:8], rest[8], rest[9], rest[10]
        fr, fi, br, bi = fre_ref[...], fim_ref[...], bre_ref[...], bim_ref[...]
        bbr_ref[...] = fr * br - fi * bi
        bbi_ref[...] = fr * bi + fi * br
        for d in range(2):
            for j, (src, where) in enumerate(((bbr_ref, _b_block), (bbi_ref, _b_block),
                                              (cre_ref, _c_block), (cim_ref, _c_block))):
                tmp[...] = jnp.zeros_like(tmp)
                for g in range(N_SSM_GROUPS):
                    q, rows, cols = where(g)
                    tmp[q, rows, cols] = src[d, g]
                outs[4 * d + j][...] = tmp[...].astype(BF16)

    dense = jax.ShapeDtypeStruct((N_QUAD, QUAD, SLAB), BF16)
    args = (b_re, b_im, c_re, c_im, f_re, f_im)
    return _hosted_call(body, comm, name="ssm_expand", grid=(1,), in_specs=[_full(a.shape) for a in args],
                        out_specs=[_full(dense.shape)] * 8, out_shape=[dense] * 8,
                        scratch_shapes=[pltpu.VMEM((N_QUAD, QUAD, SLAB), F32), pltpu.VMEM(b_re.shape, F32),
                                        pltpu.VMEM(b_re.shape, F32)], args=args)


def _ssm_unfold(gbb_re, gbb_im, b_re_t, b_im_t, f_re, f_im):
    def body(gr_ref, gi_ref, br_ref, bi_ref, fr_ref, fi_ref, obr_ref, obi_ref, ofr_ref, ofi_ref):
        gr, gi, br, bi, fr, fi = (r[...] for r in (gr_ref, gi_ref, br_ref, bi_ref, fr_ref, fi_ref))
        obr_ref[...] = fr * gr + fi * gi
        obi_ref[...] = fr * gi - fi * gr
        ofr_ref[...] = jnp.sum(br * gr + bi * gi, axis=2, keepdims=True)
        ofi_ref[...] = jnp.sum(br * gi - bi * gr, axis=2, keepdims=True)

    gb = jax.ShapeDtypeStruct(gbb_re.shape, F32)
    gf = jax.ShapeDtypeStruct(f_re.shape, F32)
    return pl.pallas_call(body, name="ssm_unfold", out_shape=[gb, gb, gf, gf])(
        gbb_re, gbb_im, b_re_t, b_im_t, f_re, f_im)


_SEGMENT_ORDER = np.zeros((TC, TC), np.float32)
for _p in range(TC):
    _SEGMENT_ORDER[_p, (_p % SEG) * SEG_LEN + _p // SEG] = 1.0


def _store_tokens(ref, col0, val, tmp_ref):
    for h in range(val.shape[1] // LANES):
        for j in range(SEG_LEN):
            tmp_ref[pl.ds(h * TC + j, SEG, stride=SEG_LEN), :] = val[SEG * j:SEG * (j + 1), h * LANES:(h + 1) * LANES]
        ref[:, col0 + h * LANES:col0 + (h + 1) * LANES] = tmp_ref[pl.ds(h * TC, TC), :]


def _segment_scan(src_re, src_im, dst_re, dst_im, tab_ref, k, carry_re, carry_im, reverse, s_refs=None):
    lam1, lam_seg = (SEG_LEN - 1, 0) if reverse else (0, SEG_LEN - 1)
    token = (lambda i: SEG_LEN - 1 - i) if reverse else (lambda i: i)
    row_id = lax.broadcasted_iota(jnp.int32, (SEG, SCAN_W), 0)
    zero = jnp.zeros((SEG, SCAN_W), F32)
    sums = []
    for lt in range(N_STATE // SCAN_W):
        sl = slice(lt * SCAN_W, (lt + 1) * SCAN_W)
        lr = jnp.broadcast_to(tab_ref[k, 0, lam1:lam1 + 1, sl], (SEG, SCAN_W))
        li = jnp.broadcast_to(tab_ref[k, 1, lam1:lam1 + 1, sl], (SEG, SCAN_W))

        def local(i, c, sl=sl, lr=lr, li=li):
            for step in range(SCAN_UNROLL):
                rows = pl.ds(pl.multiple_of(token(i * SCAN_UNROLL + step) * SEG, SEG), SEG)
                c = (lr * c[0] - li * c[1] + src_re[rows, sl], lr * c[1] + li * c[0] + src_im[rows, sl])
                dst_re[rows, sl] = c[0]
                dst_im[rows, sl] = c[1]
            return c

        er, ei = lax.fori_loop(0, SEG_LEN // SCAN_UNROLL, local, (zero, zero))

        sr_, si_ = tab_ref[k, 0, lam_seg:lam_seg + 1, sl], tab_ref[k, 1, lam_seg:lam_seg + 1, sl]
        c_r, c_i = carry_re[0:1, sl], carry_im[0:1, sl]
        in_r, in_i = zero, zero
        for r in (range(SEG - 1, -1, -1) if reverse else range(SEG)):
            in_r = jnp.where(row_id == r, c_r, in_r)
            in_i = jnp.where(row_id == r, c_i, in_i)
            c_r, c_i = (er[r:r + 1, :] + sr_ * c_r - si_ * c_i, ei[r:r + 1, :] + sr_ * c_i + si_ * c_r)
        carry_re[0:1, sl] = c_r
        carry_im[0:1, sl] = c_i

        def fix(i, c, sl=sl, in_r=in_r, in_i=in_i):
            for step in range(SCAN_UNROLL):
                j = token(i * SCAN_UNROLL + step)
                rows = pl.ds(pl.multiple_of(j * SEG, SEG), SEG)
                pr = jnp.broadcast_to(tab_ref[k, 0, pl.ds(j, 1), sl], (SEG, SCAN_W))
                pi = jnp.broadcast_to(tab_ref[k, 1, pl.ds(j, 1), sl], (SEG, SCAN_W))
                nr = dst_re[rows, sl] + pr * in_r - pi * in_i
                ni = dst_im[rows, sl] + pr * in_i + pi * in_r
                dst_re[rows, sl] = nr
                dst_im[rows, sl] = ni
                if s_refs is not None:
                    sr = s_refs[0][rows, sl]
                    si = s_refs[1][rows, sl]
                    c = (nr, ni, c[2] + c[0] * sr + c[1] * si, c[3] + c[1] * sr - c[0] * si)
            return c

        if s_refs is None:
            lax.fori_loop(0, SEG_LEN // SCAN_UNROLL, fix, 0)
        else:
            out = lax.fori_loop(0, SEG_LEN // SCAN_UNROLL, fix, (in_r, in_i, zero, zero))
            sums.append((jnp.sum(out[2], axis=0, keepdims=True), jnp.sum(out[3], axis=0, keepdims=True)))
    return sums


def _ssm_scan_fwd(u, b_re, b_im, c_re, c_im, tables, k, reverse, comm=None):
    L = u.shape[0]
    nc = L // TC
    chunk = (lambda i: nc - 1 - i) if reverse else (lambda i: i)

    def body(u_ref, bre_ref, bim_ref, cre_ref, cim_ref, tab_ref,
             y_ref, sre_ref, sim_ref, in_re, in_im, carry_re, carry_im, tmp_ref):
        @pl.when(pl.program_id(0) == 0)
        def _():
            carry_re[...] = jnp.zeros_like(carry_re)
            carry_im[...] = jnp.zeros_like(carry_im)

        ub = u_ref[...]
        for q in range(N_QUAD):
            qs = slice(q * QUAD, (q + 1) * QUAD)
            us = ub[:, (q // 4) * SLAB:(q // 4 + 1) * SLAB]
            in_re[:, qs] = _dot_nt(us, bre_ref[q])
            in_im[:, qs] = _dot_nt(us, bim_ref[q])
        _segment_scan(in_re, in_im, sre_ref, sim_ref, tab_ref, k, carry_re, carry_im, reverse)
        for j in range(D_SSM // SLAB):
            acc = jnp.zeros((TC, SLAB), F32)
            for q in range(4 * j, 4 * j + 4):
                qs = slice(q * QUAD, (q + 1) * QUAD)
                acc = acc + _dot_nt(sre_ref[:, qs].astype(BF16), cre_ref[q])
                acc = acc - _dot_nt(sim_ref[:, qs].astype(BF16), cim_ref[q])
            _store_tokens(y_ref, j * SLAB, acc, tmp_ref)

    return _hosted_call(
        body, comm, name="ssm_scan_rev" if reverse else "ssm_scan_fwd", grid=(nc,),
        in_specs=[pl.BlockSpec((TC, D_SSM), lambda i: (chunk(i), 0))]
        + [_full(b_re.shape)] * 4 + [_full(tables.shape)],
        out_specs=[pl.BlockSpec((TC, D_SSM), lambda i: (chunk(i), 0)),
                   pl.BlockSpec((TC, N_STATE), lambda i: (chunk(i), 0)),
                   pl.BlockSpec((TC, N_STATE), lambda i: (chunk(i), 0))],
        out_shape=[jax.ShapeDtypeStruct((L, D_SSM), F32), jax.ShapeDtypeStruct((L, N_STATE), F32),
                   jax.ShapeDtypeStruct((L, N_STATE), F32)],
        scratch_shapes=[pltpu.VMEM((TC, N_STATE), F32), pltpu.VMEM((TC, N_STATE), F32),
                        pltpu.VMEM((8, N_STATE), F32), pltpu.VMEM((8, N_STATE), F32),
                        pltpu.VMEM((SLAB // LANES * TC, LANES), F32)],
        args=(u, b_re, b_im, c_re, c_im, tables))


def _quad_channels(q):
    c0 = (q // 4) * SLAB + (q % 4) * 4 * SSM_GROUP
    return slice(c0, c0 + 4 * SSM_GROUP)


def _ssm_scan_bwd(dy, u, s_re, s_im, b_re, b_im, c_re, c_im, tables, k, reverse, comm=None):
    L = u.shape[0]
    nc = L // TC
    chunk = (lambda i: nc - 1 - i) if reverse else (lambda i: i)


    def body(dy_ref, u_ref, sre_ref, sim_ref, bre_ref, bim_ref, cre_ref, cim_ref, tab_ref,
             du_ref, ob_re, ob_im, oc_re, oc_im, gv_ref,
             a_re, a_im, carry_re, carry_im, gbr_ref, gbi_ref, gcr_ref, gci_ref, tmp_ref):
        @pl.when(pl.program_id(0) == 0)
        def _():
            carry_re[...] = jnp.zeros_like(carry_re)
            carry_im[...] = jnp.zeros_like(carry_im)
            for r in (gbr_ref, gbi_ref, gcr_ref, gci_ref, gv_ref):
                r[...] = jnp.zeros_like(r)

        dyb = dy_ref[...]
        ub = u_ref[...]
        for q in range(N_QUAD):
            qs = slice(q * QUAD, (q + 1) * QUAD)
            ds = dyb[:, (q // 4) * SLAB:(q // 4 + 1) * SLAB]
            a_re[:, qs] = _dot_nn(ds, cre_ref[q])
            a_im[:, qs] = -_dot_nn(ds, cim_ref[q])
            dq = dyb[:, _quad_channels(q)]
            gcr_ref[q] += _dot_tn(dq, sre_ref[:, qs].astype(BF16))
            gci_ref[q] -= _dot_tn(dq, sim_ref[:, qs].astype(BF16))
        sums = _segment_scan(a_re, a_im, a_re, a_im, tab_ref, k, carry_re, carry_im, reverse,
                             s_refs=(sre_ref, sim_ref))
        for lt, (glr, gli) in enumerate(sums):
            sl = slice(lt * SCAN_W, (lt + 1) * SCAN_W)
            gv_ref[0:1, sl] += glr
            gv_ref[1:2, sl] += gli
        for j in range(D_SSM // SLAB):
            us = ub[:, j * SLAB:(j + 1) * SLAB]
            acc = jnp.zeros((TC, SLAB), F32)
            for q in range(4 * j, 4 * j + 4):
                qs = slice(q * QUAD, (q + 1) * QUAD)
                dbr = a_re[:, qs].astype(BF16)
                dbi = a_im[:, qs].astype(BF16)
                uq = ub[:, _quad_channels(q)]
                gbr_ref[q] += _dot_tn(uq, dbr)
                gbi_ref[q] += _dot_tn(uq, dbi)
                acc = acc + _dot_nn(dbr, bre_ref[q]) + _dot_nn(dbi, bim_ref[q])
            _store_tokens(du_ref, j * SLAB, acc, tmp_ref)

        @pl.when(pl.program_id(0) == nc - 1)
        def _():
            for g in range(N_SSM_GROUPS):
                q, gl = divmod(g, 4)
                rows = slice(gl * SSM_GROUP, (gl + 1) * SSM_GROUP)
                cols = slice(gl * SSM_STATE, (gl + 1) * SSM_STATE)
                for out, acc_ref in ((ob_re, gbr_ref), (ob_im, gbi_ref), (oc_re, gcr_ref), (oc_im, gci_ref)):
                    out[g] = acc_ref[q, rows, cols]

    gshape = jax.ShapeDtypeStruct((N_SSM_GROUPS, SSM_GROUP, SSM_STATE), F32)
    compact = pltpu.VMEM((N_QUAD, 4 * SSM_GROUP, QUAD), F32)
    return _hosted_call(
        body, comm, name="ssm_bwd_rev" if reverse else "ssm_bwd_fwd", grid=(nc,),
        in_specs=[pl.BlockSpec((TC, D_SSM), lambda i: (chunk(i), 0)),
                  pl.BlockSpec((TC, D_SSM), lambda i: (chunk(i), 0)),
                  pl.BlockSpec((TC, N_STATE), lambda i: (chunk(i), 0)),
                  pl.BlockSpec((TC, N_STATE), lambda i: (chunk(i), 0))]
        + [_full(b_re.shape)] * 4 + [_full(tables.shape)],
        out_specs=[pl.BlockSpec((TC, D_SSM), lambda i: (chunk(i), 0))] + [_full(gshape.shape)] * 4
        + [_full((2, N_STATE))],
        out_shape=[jax.ShapeDtypeStruct((L, D_SSM), F32), gshape, gshape, gshape, gshape,
                   jax.ShapeDtypeStruct((2, N_STATE), F32)],
        scratch_shapes=[pltpu.VMEM((TC, N_STATE), F32), pltpu.VMEM((TC, N_STATE), F32),
                        pltpu.VMEM((8, N_STATE), F32), pltpu.VMEM((8, N_STATE), F32),
                        compact, compact, compact, compact, pltpu.VMEM((SLAB // LANES * TC, LANES), F32)],
        args=(dy, u, s_re, s_im, b_re, b_im, c_re, c_im, tables))


def _ssm_post(yf, yb, u, d, glu_w, glu_b):
    y = yf + yb + d * u
    z, t = _gelu(y)
    zb = z.astype(BF16)
    gate = _sigmoid(_dot_nn(zb, glu_w) + glu_b)
    return y, z, t, zb, gate


def _mix_out(yn_pool, yf, yb, u, x, ssm_d, glu_w_b, glu_b, g_ssm, w_out_b, g_ffn):
    L, D = x.shape

    def body(ynp_ref, yf_ref, yb_ref, u_ref, x_ref, d_ref, gw_ref, gb_ref, gs_ref, wo_ref, gf_ref,
             h1_ref, hn_ref, ycat_ref):
        _, z, _, _, gate = _ssm_post(yf_ref[...], yb_ref[...], u_ref[...], d_ref[...], gw_ref[...], gb_ref[...])
        yns, _, _ = _rms_fwd(z * gate, gs_ref[...])
        ynsb = yns.astype(BF16)
        ynp = ynp_ref[...]
        ycat_ref[:, 0:D_POOL] = ynp
        ycat_ref[:, D_POOL:D] = ynsb
        h1 = x_ref[...] + _dot_nn(ynp, wo_ref[0:D_POOL, :]) + _dot_nn(ynsb, wo_ref[D_POOL:D, :])
        h1_ref[...] = h1
        hn, _, _ = _rms_fwd(h1, gf_ref[...])
        hn_ref[...] = hn.astype(BF16)

    half = lambda c: pl.BlockSpec((TL, D_SSM), lambda i: (i, c))
    row = pl.BlockSpec((TL, D), lambda i: (i, 0))
    return pl.pallas_call(
        body, name="mix_out", grid=(L // TL,),
        in_specs=[half(0), half(0), half(0), half(1), row, _full((1, D_SSM)), _full(glu_w_b.shape),
                  _full((1, D_SSM)), _full((1, D_SSM)), _full(w_out_b.shape), _full((1, D))],
        out_specs=[row, row, row],
        out_shape=[jax.ShapeDtypeStruct((L, D), F32), jax.ShapeDtypeStruct((L, D), BF16),
                   jax.ShapeDtypeStruct((L, D), BF16)],
        compiler_params=_cp("parallel"))(yn_pool, yf, yb, u, x, ssm_d, glu_w_b, glu_b, g_ssm, w_out_b, g_ffn)


def _ssm_bwd_local(dh1, yf, yb, u, ssm_d, glu_w_b, glu_b, g_ssm, w_out_b, comm=None):
    L, D = dh1.shape
    assert TL == TC
    order = jnp.asarray(_SEGMENT_ORDER, BF16)

    def body(dh_ref, yf_ref, yb_ref, u_ref, d_ref, gw_ref, gb_ref, gs_ref, wo_ref, ord_ref,
             dy_ref, du_ref, ggw_ref, ggb_ref, gd_ref, ggs_ref):
        @pl.when(pl.program_id(0) == 0)
        def _():
            for r in (ggw_ref, ggb_ref, gd_ref, ggs_ref):
                r[...] = jnp.zeros_like(r)

        u = u_ref[...]
        d = d_ref[...]
        y, z, t, zb, gate = _ssm_post(yf_ref[...], yb_ref[...], u, d, gw_ref[...], gb_ref[...])
        gs = gs_ref[...]
        _, xh, inv = _rms_fwd(z * gate, gs)
        d_yn = _dot_nt(dh_ref[...], wo_ref[...])
        d_o, dgs = _rms_bwd(d_yn, xh, inv, gs)
        ggs_ref[...] += dgs
        d_zg = d_o * z * gate * (1.0 - gate)
        d_zgb = d_zg.astype(BF16)
        ggb_ref[...] += jnp.sum(d_zg, axis=0, keepdims=True)
        ggw_ref[...] += _dot_tn(zb, d_zgb)
        d_z = d_o * gate + _dot_nt(d_zgb, gw_ref[...])
        d_y = d_z * _gelu_grad(y, t)
        gd_ref[...] += jnp.sum(d_y * u, axis=0, keepdims=True)
        dy_ref[...] = _dot_nn(ord_ref[...], d_y.astype(BF16)).astype(BF16)
        du_ref[...] = d_y * d

    half = lambda c: pl.BlockSpec((TL, D_SSM), lambda i: (i, c))
    vec = _full((1, D_SSM))
    return _hosted_call(
        body, comm, name="ssm_bwd_local", grid=(L // TL,),
        in_specs=[pl.BlockSpec((TL, D), lambda i: (i, 0)), half(0), half(0), half(1), vec, _full(glu_w_b.shape),
                  vec, vec, pl.BlockSpec((D_SSM, D), lambda i: (1, 0)), _full(order.shape)],
        out_specs=[half(0), half(0), _full(glu_w_b.shape), vec, vec, vec],
        out_shape=[jax.ShapeDtypeStruct((L, D_SSM), BF16), jax.ShapeDtypeStruct((L, D_SSM), F32),
                   jax.ShapeDtypeStruct(glu_w_b.shape, F32)] + [jax.ShapeDtypeStruct((1, D_SSM), F32)] * 3,
        scratch_shapes=[], args=(dh1, yf, yb, u, ssm_d, glu_w_b, glu_b, g_ssm, w_out_b, order))


def _in_bwd(du_pool, du_a, du_b, du_c, dh1, x, g, w_in_b, comm=None):
    L, D = x.shape

    def body(p_ref, a_ref, b_ref, c_ref, dh_ref, x_ref, g_ref, w_ref, dx_ref, dub_ref, gg_ref):
        @pl.when(pl.program_id(0) == 0)
        def _():
            gg_ref[...] = jnp.zeros_like(gg_ref)

        dub_ref[:, 0:D_POOL] = p_ref[...].astype(BF16)
        dub_ref[:, D_POOL:D] = (a_ref[...] + b_ref[...] + c_ref[...]).astype(BF16)
        d_xn = _dot_nt(dub_ref[...], w_ref[...])
        gv = g_ref[...]
        _, xh, inv = _rms_fwd(x_ref[...], gv)
        dx, dg = _rms_bwd(d_xn, xh, inv, gv)
        gg_ref[...] += dg
        dx_ref[...] = dh_ref[...] + dx

    half = pl.BlockSpec((TL, D_SSM), lambda i: (i, 0))
    row = pl.BlockSpec((TL, D), lambda i: (i, 0))
    return _hosted_call(
        body, comm, name="in_bwd", grid=(L // TL,),
        in_specs=[half, half, half, half, row, row, _full((1, D)), _full(w_in_b.shape)],
        out_specs=[row, row, _full((1, D))],
        out_shape=[jax.ShapeDtypeStruct((L, D), F32), jax.ShapeDtypeStruct((L, D), BF16),
                   jax.ShapeDtypeStruct((1, D), F32)],
        scratch_shapes=[], args=(du_pool, du_a, du_b, du_c, dh1, x, g, w_in_b))


def _ffn_up(hn, w_up4):
    L, D = hn.shape

    def body(h_ref, w_ref, o_ref):
        o_ref[...] = _dot_nn(h_ref[...], w_ref[...]).astype(BF16)

    rows = min(TM, L)
    return pl.pallas_call(
        body, name="ffn_up", grid=(4, L // rows),
        in_specs=[pl.BlockSpec((rows, D), lambda j, i: (i, 0)), pl.BlockSpec((None, D, FF_BLK), lambda j, i: (j, 0, 0))],
        out_specs=pl.BlockSpec((rows, FF_BLK), lambda j, i: (i, j)),
        out_shape=jax.ShapeDtypeStruct((L, 4 * FF_BLK), BF16),
        compiler_params=_cp("parallel", "parallel"))(hn, w_up4)


def _halo_specs_2d(rows, width, L, col, order):
    rb = rows // HALO_B
    last = L // HALO_B - 1
    if order == "ik":
        wrap = lambda f: (lambda i, k: f(i, k))
    else:
        wrap = lambda f: (lambda k, i: f(i, k))
    return [pl.BlockSpec((HALO_B, width), wrap(lambda i, k: (jnp.maximum(i * rb - 1, 0), col(k)))),
            pl.BlockSpec((rows, width), wrap(lambda i, k: (i, col(k)))),
            pl.BlockSpec((HALO_B, width), wrap(lambda i, k: (jnp.minimum((i + 1) * rb, last), col(k))))]


def _neighbours(x, prev_ref, next_ref, cs, i, n):
    rows = x.shape[0]
    row = lax.broadcasted_iota(jnp.int32, (rows, 1), 0)
    before = jnp.where(i > 0, prev_ref[:, cs].astype(F32)[HALO_B - 1:HALO_B, :], 0.0)
    after = jnp.where(i < n - 1, next_ref[:, cs].astype(F32)[0:1, :], 0.0)
    xf = x.astype(F32)
    return (jnp.where(row == 0, before, pltpu.roll(xf, 1, 0)),
            jnp.where(row == rows - 1, after, pltpu.roll(xf, rows - 1, 0)))


def _conv3(x, before, after, w, b):
    return before * w[0:1, :] + x.astype(F32) * w[1:2, :] + after * w[2:3, :] + b


def _col_chunks(width, size=256):
    return [slice(c, min(c + size, width)) for c in range(0, width, size)]


def _ffn_down_loss(up, conv_w, conv_b, w_down_b, h1, target, g_final):
    L, D = h1.shape
    n = L // TF
    nk = D_FF // FF_BLK

    def body(vp, vc, vn, gp, gc, gn, wv_ref, wg_ref, bv_ref, bg_ref, wd_ref, h1_ref, t_ref, gf_ref,
             a_ref, cv_ref, cg_ref, dh2_ref, dh2b_ref, loss_ref, gg_ref, acc_ref):
        i = pl.program_id(0)
        k = pl.program_id(1)

        @pl.when((i == 0) & (k == 0))
        def _():
            loss_ref[...] = jnp.zeros_like(loss_ref)
            gg_ref[...] = jnp.zeros_like(gg_ref)

        @pl.when(k == 0)
        def _():
            acc_ref[...] = jnp.zeros_like(acc_ref)

        acc = jnp.zeros((TF, D), F32)
        for cs in _col_chunks(FF_BLK):
            xv, xg = vc[:, cs], gc[:, cs]
            val = _conv3(xv, *_neighbours(xv, vp, vn, cs, i, n), wv_ref[:, cs], bv_ref[:, cs])
            gate = _conv3(xg, *_neighbours(xg, gp, gn, cs, i, n), wg_ref[:, cs], bg_ref[:, cs])
            a = (val * (gate * _sigmoid(gate))).astype(BF16)
            a_ref[:, cs] = a
            cv_ref[:, cs] = val.astype(BF16)
            cg_ref[:, cs] = gate.astype(BF16)
            rows = pl.ds(pl.multiple_of(k * FF_BLK + cs.start, LANES), cs.stop - cs.start)
            acc = acc + _dot_nn(a, wd_ref[rows, :])
        acc_ref[...] += acc

        @pl.when(k == nk - 1)
        def _():
            gf = gf_ref[...]
            y, xh, inv = _rms_fwd(h1_ref[...] + acc_ref[...], gf)
            diff = y - t_ref[...]
            part = 0.5 * jnp.sum(jnp.mean(diff * diff, axis=-1, keepdims=True), axis=0, keepdims=True)
            loss_ref[...] += jnp.broadcast_to(part, loss_ref.shape)
            dx, dg = _rms_bwd(diff * (1.0 / D), xh, inv, gf)
            gg_ref[...] += dg
            dh2_ref[...] = dx
            dh2b_ref[...] = dx.astype(BF16)

    row = pl.BlockSpec((TF, D), lambda i, k: (i, 0))
    cw = lambda off: pl.BlockSpec((3, FF_BLK), lambda i, k: (0, k + off))
    cb = lambda off: pl.BlockSpec((1, FF_BLK), lambda i, k: (0, k + off))
    return pl.pallas_call(
        body, name="ffn_down_loss", grid=(n, nk),
        in_specs=_halo_specs_2d(TF, FF_BLK, L, lambda k: k, "ik") + _halo_specs_2d(TF, FF_BLK, L, lambda k: k + nk, "ik")
        + [cw(0), cw(nk), cb(0), cb(nk), _full(w_down_b.shape), row, row, _full((1, D))],
        out_specs=[pl.BlockSpec((TF, FF_BLK), lambda i, k: (i, k))] * 3 + [row, row, _full((1, LANES)), _full((1, D))],
        out_shape=[jax.ShapeDtypeStruct((L, D_FF), BF16)] * 3
        + [jax.ShapeDtypeStruct((L, D), F32), jax.ShapeDtypeStruct((L, D), BF16),
           jax.ShapeDtypeStruct((1, LANES), F32), jax.ShapeDtypeStruct((1, D), F32)],
        scratch_shapes=[pltpu.VMEM((TF, D), F32)],
        compiler_params=_cp("arbitrary", "arbitrary"))(
            up, up, up, up, up, up, conv_w, conv_w, conv_b, conv_b, w_down_b, h1, target, g_final)


def _ffn_act_bwd(c_val, c_gate, w_down_b, dh2):
    L, D = dh2.shape
    n = L // TL
    nk = D_FF // FF_BLK

    def body(v_ref, g_ref, wd_ref, dh_ref, dv_ref, dg_ref, gbv_ref, gbg_ref):
        @pl.when(pl.program_id(1) == 0)
        def _():
            gbv_ref[...] = jnp.zeros_like(gbv_ref)
            gbg_ref[...] = jnp.zeros_like(gbg_ref)

        dh = dh_ref[...]
        for cs in _col_chunks(FF_BLK):
            val, gate = v_ref[:, cs].astype(F32), g_ref[:, cs].astype(F32)
            d_a = _dot_nt(dh, wd_ref[cs, :])
            sg = _sigmoid(gate)
            d_val = d_a * (gate * sg)
            d_gate = d_a * val * (sg * (1.0 + gate * (1.0 - sg)))
            dv_ref[:, cs] = d_val.astype(BF16)
            dg_ref[:, cs] = d_gate.astype(BF16)
            gbv_ref[:, cs] += jnp.sum(d_val, axis=0, keepdims=True)
            gbg_ref[:, cs] += jnp.sum(d_gate, axis=0, keepdims=True)

    blk = pl.BlockSpec((TL, FF_BLK), lambda k, i: (i, k))
    acc = pl.BlockSpec((1, FF_BLK), lambda k, i: (0, k))
    return pl.pallas_call(
        body, name="ffn_act_bwd", grid=(nk, n),
        in_specs=[blk, blk, pl.BlockSpec((FF_BLK, D), lambda k, i: (k, 0)), pl.BlockSpec((TL, D), lambda k, i: (i, 0))],
        out_specs=[blk, blk, acc, acc],
        out_shape=[jax.ShapeDtypeStruct((L, D_FF), BF16), jax.ShapeDtypeStruct((L, D_FF), BF16),
                   jax.ShapeDtypeStruct((1, D_FF), F32), jax.ShapeDtypeStruct((1, D_FF), F32)],
        compiler_params=_cp("arbitrary", "arbitrary"))(c_val, c_gate, w_down_b, dh2)


def _ffn_up_bwd(d_val, d_gate, up, conv_w, w_up4, h1, dh2, g_ffn):
    L, D = h1.shape
    n = L // TF
    nk = D_FF // FF_BLK

    def body(vp, vc, vn, gp, gc, gn, uv_ref, ug_ref, wv_ref, wg_ref, wu_ref, h1_ref, dh2_ref, g_ref,
             dup_ref, dh1_ref, dh1b_ref, gg_ref, gcw_ref, acc_ref):
        i = pl.program_id(0)
        k = pl.program_id(1)

        @pl.when((i == 0) & (k == 0))
        def _():
            gg_ref[...] = jnp.zeros_like(gg_ref)
            gcw_ref[...] = jnp.zeros_like(gcw_ref)

        @pl.when(k == 0)
        def _():
            acc_ref[...] = jnp.zeros_like(acc_ref)

        acc = jnp.zeros((TF, D), F32)
        for j, (blocks, u_ref, w_ref) in enumerate((((vp, vc, vn), uv_ref, wv_ref), ((gp, gc, gn), ug_ref, wg_ref))):
            for cs in _col_chunks(FF_BLK):
                d = blocks[1][:, cs]
                before, after = _neighbours(d, blocks[0], blocks[2], cs, i, n)
                taps = (after, d.astype(F32), before)
                w = w_ref[:, cs]
                d_up = (taps[0] * w[0:1, :] + taps[1] * w[1:2, :] + taps[2] * w[2:3, :]).astype(BF16)
                dup_ref[j, :, cs] = d_up
                acc = acc + _dot_nt(d_up, wu_ref[k + j * nk, :, cs])
                x = u_ref[:, cs].astype(F32)
                for r in range(3):
                    gcw_ref[j, k, r:r + 1, cs] += jnp.sum(taps[r] * x, axis=0, keepdims=True)
        acc_ref[...] += acc

        @pl.when(k == nk - 1)
        def _():
            g = g_ref[...]
            _, xh, inv = _rms_fwd(h1_ref[...], g)
            dx, dg = _rms_bwd(acc_ref[...], xh, inv, g)
            gg_ref[...] += dg
            dh1 = dh2_ref[...] + dx
            dh1_ref[...] = dh1
            dh1b_ref[...] = dh1.astype(BF16)

    row = pl.BlockSpec((TF, D), lambda i, k: (i, 0))
    cw = lambda off: pl.BlockSpec((3, FF_BLK), lambda i, k: (0, k + off))
    tile = lambda off: pl.BlockSpec((TF, FF_BLK), lambda i, k: (i, k + off))
    return pl.pallas_call(
        body, name="ffn_up_bwd", grid=(n, nk),
        in_specs=_halo_specs_2d(TF, FF_BLK, L, lambda k: k, "ik") + _halo_specs_2d(TF, FF_BLK, L, lambda k: k, "ik")
        + [tile(0), tile(nk), cw(0), cw(nk), _full(w_up4.shape), row, row, _full((1, D))],
        out_specs=[pl.BlockSpec((2, None, TF, FF_BLK), lambda i, k: (0, k, i, 0)), row, row, _full((1, D)),
                   _full((2, nk, 3, FF_BLK))],
        out_shape=[jax.ShapeDtypeStruct((2, nk, L, FF_BLK), BF16), jax.ShapeDtypeStruct((L, D), F32),
                   jax.ShapeDtypeStruct((L, D), BF16), jax.ShapeDtypeStruct((1, D), F32),
                   jax.ShapeDtypeStruct((2, nk, 3, FF_BLK), F32)],
        scratch_shapes=[pltpu.VMEM((TF, D), F32)],
        compiler_params=_cp("arbitrary", "arbitrary"))(
            d_val, d_val, d_val, d_gate, d_gate, d_gate, up, up, conv_w, conv_w, w_up4, h1, dh2, g_ffn)


def _matmul_tn(a, b, tm, tn, name, tk=2048):
    L, M = a.shape
    N = b.shape[1]
    tk = min(tk, L)

    def body(a_ref, b_ref, o_ref):
        @pl.when(pl.program_id(2) == 0)
        def _():
            o_ref[...] = jnp.zeros_like(o_ref)

        o_ref[...] += _dot_tn(a_ref[...], b_ref[...])

    return pl.pallas_call(
        body, name=name, grid=(M // tm, N // tn, L // tk),
        in_specs=[pl.BlockSpec((tk, tm), lambda m, n, l: (l, m)), pl.BlockSpec((tk, tn), lambda m, n, l: (l, n))],
        out_specs=pl.BlockSpec((tm, tn), lambda m, n, l: (m, n)),
        out_shape=jax.ShapeDtypeStruct((M, N), F32),
        compiler_params=_cp("parallel", "parallel", "arbitrary"))(a, b)


def _matmul_tn_blocks(a, b, tm, name, tk=2048):
    L, M = a.shape
    J, _, N = b.shape
    tk = min(tk, L)

    def body(a_ref, b_ref, o_ref):
        @pl.when(pl.program_id(2) == 0)
        def _():
            o_ref[...] = jnp.zeros_like(o_ref)

        o_ref[...] += _dot_tn(a_ref[...], b_ref[...])

    return pl.pallas_call(
        body, name=name, grid=(M // tm, J, L // tk),
        in_specs=[pl.BlockSpec((tk, tm), lambda m, j, l: (l, m)), pl.BlockSpec((None, tk, N), lambda m, j, l: (j, l, 0))],
        out_specs=pl.BlockSpec((None, tm, N), lambda m, j, l: (j, m, 0)),
        out_shape=jax.ShapeDtypeStruct((J, M, N), F32),
        compiler_params=_cp("parallel", "parallel", "arbitrary"))(a, b)


def _row_tile(rows):
    for t in (512, 352, 256, 128, 64, 8):
        if rows % t == 0:
            return t
    return rows


def _add_half(g, r, c_arr, name, out_dtype=F32):
    _, _, R, C = g.shape
    tr = _row_tile(R)

    def body(c_ref, g_ref, r_ref, o_ref):
        o_ref[...] = (g_ref[...] + r_ref[...]).astype(out_dtype)

    return pl.pallas_call(
        body, name=name,
        grid_spec=pltpu.PrefetchScalarGridSpec(
            num_scalar_prefetch=1, grid=(g.shape[0], R // tr),
            in_specs=[pl.BlockSpec((None, None, tr, C), lambda j, i, c: (j, c[0], i, 0)),
                      pl.BlockSpec((None, tr, C), lambda j, i, c: (j, i, 0))],
            out_specs=pl.BlockSpec((None, tr, C), lambda j, i, c: (j, i, 0))),
        out_shape=jax.ShapeDtypeStruct(r.shape, out_dtype),
        compiler_params=_cp("parallel", "parallel"))(c_arr, g, r)


def _sum4(p, name):
    _, R, C = p.shape
    tr = _row_tile(R)

    def body(p_ref, o_ref):
        q = [p_ref[j].astype(F32) for j in range(4)]
        o_ref[...] = ((q[0] + q[1]) + q[2]) + q[3]

    return pl.pallas_call(
        body, name=name, grid=(R // tr,),
        in_specs=[pl.BlockSpec((4, tr, C), lambda i: (0, i, 0))],
        out_specs=pl.BlockSpec((tr, C), lambda i: (i, 0)),
        out_shape=jax.ShapeDtypeStruct((R, C), F32), compiler_params=_cp("parallel"))(p)


def _adamw_refs(w_ref, g_ref, m_ref, v_ref, d_ref, nm_ref, nv_ref):
    gv = g_ref[...]
    nm = ADAM_B1 * m_ref[...] + (1.0 - ADAM_B1) * gv
    nv = ADAM_B2 * v_ref[...] + (1.0 - ADAM_B2) * (gv * gv)
    m_hat = nm / (1.0 - ADAM_B1 ** ADAM_STEP)
    v_hat = nv / (1.0 - ADAM_B2 ** ADAM_STEP)
    d_ref[...] = -ADAM_LR * (m_hat / (jnp.sqrt(v_hat) + ADAM_EPS) + ADAM_WD * w_ref[...])
    nm_ref[...] = nm
    nv_ref[...] = nv


def _adamw_many(ws, gs, ms, vs, name):
    n = len(ws)

    def body(*refs):
        for k in range(n):
            _adamw_refs(*(refs[j * n + k] for j in range(7)))

    out_shape = [jax.ShapeDtypeStruct(w.shape, F32) for w in ws] * 3
    res = pl.pallas_call(body, name=name, out_shape=out_shape,
                         compiler_params=pltpu.CompilerParams(vmem_limit_bytes=VMEM_LIMIT))(*ws, *gs, *ms, *vs)
    return res[:n], res[n:2 * n], res[2 * n:]


def _join_rows(own, other, c_arr, name):
    R, C = own.shape
    tr = _row_tile(R)

    def body(c_ref, own_ref, other_ref, o_ref):
        o_ref[...] = jnp.where(pl.program_id(0) == c_ref[0], own_ref[...], other_ref[...])

    half = pl.BlockSpec((tr, C), lambda h, i, c: (i, 0))
    return pl.pallas_call(
        body, name=name,
        grid_spec=pltpu.PrefetchScalarGridSpec(
            num_scalar_prefetch=1, grid=(2, R // tr), in_specs=[half, half],
            out_specs=pl.BlockSpec((tr, C), lambda h, i, c: (h * (R // tr) + i, 0))),
        out_shape=jax.ShapeDtypeStruct((2 * R, C), F32),
        compiler_params=_cp("parallel", "parallel"))(c_arr, own, other)


def _adamw_halves(w, own, other, m, v, name, comm=None):
    R, C = own.shape
    tr = _row_tile(R)
    while tr * C * 4 > ADAMW_BLOCK_BYTES and tr % 16 == 0:
        tr //= 2

    def body(w_ref, own_ref, other_ref, m_ref, v_ref, g_ref, d_ref, nm_ref, nv_ref):
        g_ref[...] = jnp.where(pl.program_id(0) == lax.axis_index("c"), own_ref[...], other_ref[...])
        _adamw_refs(w_ref, g_ref, m_ref, v_ref, d_ref, nm_ref, nv_ref)

    half = pl.BlockSpec((tr, C), lambda h, i: (i, 0))
    full = pl.BlockSpec((tr, C), lambda h, i: (h * (R // tr) + i, 0))
    sh = jax.ShapeDtypeStruct((2 * R, C), F32)
    return _hosted_call(body, comm, name=name, grid=(2, R // tr), in_specs=[full, half, half, full, full],
                        out_specs=[full] * 4, out_shape=[sh] * 4, scratch_shapes=[], args=(w, own, other, m, v))


_ANY = pl.BlockSpec(memory_space=pl.ANY)


def _position():
    return lax.axis_index("x"), lax.axis_index("y"), lax.axis_index("c")


class _Comm:
    def __init__(self, arrs, out_shape, sems, start, finish):
        self.arrs, self.out_shape, self.sems, self.start, self.finish = arrs, out_shape, sems, start, finish


def _comm_call(comm, name):
    n, m = len(comm.arrs), len(comm.out_shape)

    def body(*refs):
        ins, outs, sems = refs[:n], refs[n:n + m], refs[n + m:]
        comm.start(ins, outs, sems)
        comm.finish(ins, outs, sems)

    return pl.pallas_call(
        body, name=name, in_specs=[_ANY] * n, out_specs=[_ANY] * m, out_shape=comm.out_shape,
        scratch_shapes=comm.sems, compiler_params=pltpu.CompilerParams(has_side_effects=True))(*comm.arrs)


def _hosted_call(body, comm, *, name, grid, in_specs, out_specs, out_shape, scratch_shapes, args):
    sem = ("arbitrary",) * len(grid)
    if comm is None:
        return pl.pallas_call(body, name=name, grid=grid, in_specs=in_specs, out_specs=out_specs, out_shape=out_shape,
                              scratch_shapes=scratch_shapes, compiler_params=_cp(*sem))(*args), []
    n_in, n_out, n_scr = len(in_specs), len(out_specs), len(scratch_shapes)
    ci, co = len(comm.arrs), len(comm.out_shape)

    def full(*refs):
        ins, refs = refs[:n_in], refs[n_in:]
        cins, refs = refs[:ci], refs[ci:]
        outs, refs = refs[:n_out], refs[n_out:]
        couts, refs = refs[:co], refs[co:]
        scr, csems = refs[:n_scr], refs[n_scr:]
        first, last = True, True
        for d, size in enumerate(grid):
            first = first & (pl.program_id(d) == 0)
            last = last & (pl.program_id(d) == size - 1)

        @pl.when(first)
        def _():
            comm.start(cins, couts, csems)

        body(*ins, *outs, *scr)

        @pl.when(last)
        def _():
            comm.finish(cins, couts, csems)

    res = pl.pallas_call(
        full, name=name, grid=grid, in_specs=list(in_specs) + [_ANY] * ci, out_specs=list(out_specs) + [_ANY] * co,
        out_shape=list(out_shape) + list(comm.out_shape), scratch_shapes=list(scratch_shapes) + list(comm.sems),
        compiler_params=_cp(*sem))(*args, *comm.arrs)
    return res[:n_out], res[n_out:]


def _comm_join(*comms):
    def parts(xs, attr):
        out, at = [], 0
        for cm in comms:
            n = len(getattr(cm, attr))
            out.append(xs[at:at + n])
            at += n
        return out

    def start(ins, outs, sems):
        for cm, i, o, s in zip(comms, parts(ins, "arrs"), parts(outs, "out_shape"), parts(sems, "sems")):
            cm.start(i, o, s)

    def finish(ins, outs, sems):
        for cm, i, o, s in zip(comms, parts(ins, "arrs"), parts(outs, "out_shape"), parts(sems, "sems")):
            cm.finish(i, o, s)

    cat = lambda attr: [x for cm in comms for x in getattr(cm, attr)]
    return _Comm(cat("arrs"), cat("out_shape"), cat("sems"), start, finish)


def _dma_sems(*counts):
    return [pltpu.SemaphoreType.DMA((n,)) for n in counts]


def _comm_pair_swap(arrs, half=False):
    n = len(arrs)
    out_shape = [jax.ShapeDtypeStruct(a.shape[:1] + a.shape[2:] if half else a.shape, a.dtype) for a in arrs]

    def copies(ins, outs, sems):
        x, y, c = _position()
        return [pltpu.make_async_remote_copy(
            src_ref=ins[k].at[:, 1 - c] if half else ins[k], dst_ref=outs[k], send_sem=sems[0].at[k],
            recv_sem=sems[1].at[k], device_id=(x, y, 1 - c), device_id_type=MESH) for k in range(n)]

    def start(ins, outs, sems):
        for cp in copies(ins, outs, sems):
            cp.start()

    def finish(ins, outs, sems):
        for cp in copies(ins, outs, sems):
            cp.wait()

    return _Comm(arrs, out_shape, _dma_sems(n, n), start, finish)


def _chip_of(j, c):
    return (jnp.right_shift(j, 1), jnp.bitwise_and(j, 1), c)


def _comm_chip_exchange(arrs, scatter):
    n = len(arrs)
    out_shape = [jax.ShapeDtypeStruct(a.shape if scatter else (4,) + a.shape, a.dtype) for a in arrs]

    def copies(ins, outs, sems):
        x, y, c = _position()
        me = 2 * x + y
        local, sent, landed = [], [], []
        for k in range(n):
            local.append(pltpu.make_async_copy(ins[k].at[me] if scatter else ins[k], outs[k].at[me], sems[2].at[k]))
            for d in (1, 2, 3):
                j = jnp.bitwise_xor(me, d)
                s = 3 * k + d - 1
                src = ins[k].at[j] if scatter else ins[k]
                for dst, group in ((outs[k].at[me], sent), (outs[k].at[j], landed)):
                    group.append(pltpu.make_async_remote_copy(
                        src_ref=src, dst_ref=dst, send_sem=sems[0].at[s], recv_sem=sems[1].at[s],
                        device_id=_chip_of(j, c), device_id_type=MESH))
        return local, sent, landed

    def start(ins, outs, sems):
        local, sent, _ = copies(ins, outs, sems)
        for cp in local + sent:
            cp.start()

    def finish(ins, outs, sems):
        local, sent, landed = copies(ins, outs, sems)
        for cp in sent:
            cp.wait_send()
        for cp in landed:
            cp.wait_recv()
        for cp in local:
            cp.wait()

    return _Comm(arrs, out_shape, _dma_sems(3 * n, 3 * n, n), start, finish)


LOCAL_PARTS = 4


def _comm_gather_split(shards, whole):
    n, nw = len(shards), len(whole)
    arrs = list(shards) + list(whole)
    out_shape = [jax.ShapeDtypeStruct((4,) + a.shape, a.dtype) for a in arrs]

    def copies(ins, outs, sems):
        x, y, c = _position()
        me = 2 * x + y
        local, sent, landed, passed, passed_in = [], [], [], [], []
        for k in range(n + nw):
            if k >= n:
                local.append(pltpu.make_async_copy(ins[k], outs[k].at[me], sems[4].at[LOCAL_PARTS * k]))
            else:
                part = shards[k].shape[0] // LOCAL_PARTS
                for r in range(LOCAL_PARTS):
                    local.append(pltpu.make_async_copy(ins[k].at[pl.ds(r * part, part)],
                                                       outs[k].at[me, pl.ds(r * part, part)],
                                                       sems[4].at[LOCAL_PARTS * k + r]))
            for d in (1, 2, 3):
                j = jnp.bitwise_xor(me, d)
                s = 3 * k + d - 1
                if k >= n:
                    src, mine, theirs = ins[k], outs[k].at[me], outs[k].at[j]
                else:
                    h = shards[k].shape[0] // 2
                    rows = pl.ds(pl.multiple_of(c * h, 16), h)
                    other = pl.ds(pl.multiple_of((1 - c) * h, 16), h)
                    src, mine, theirs = ins[k].at[rows], outs[k].at[me, rows], outs[k].at[j, rows]
                    for dst, group in ((theirs, passed), (outs[k].at[j, other], passed_in)):
                        group.append(pltpu.make_async_remote_copy(
                            src_ref=theirs, dst_ref=dst, send_sem=sems[2].at[s], recv_sem=sems[3].at[s],
                            device_id=(x, y, 1 - c), device_id_type=MESH))
                for dst, group in ((mine, sent), (theirs, landed)):
                    group.append(pltpu.make_async_remote_copy(
                        src_ref=src, dst_ref=dst, send_sem=sems[0].at[s], recv_sem=sems[1].at[s],
                        device_id=_chip_of(j, c), device_id_type=MESH))
        return local, sent, landed, passed, passed_in

    def start(ins, outs, sems):
        local, sent, _, _, _ = copies(ins, outs, sems)
        for cp in local + sent:
            cp.start()

    def finish(ins, outs, sems):
        local, sent, landed, passed, passed_in = copies(ins, outs, sems)
        for cp in landed[:3 * n]:
            cp.wait_recv()
        for cp in passed:
            cp.start()
        for cp in landed[3 * n:]:
            cp.wait_recv()
        for cp in sent:
            cp.wait_send()
        for cp in passed:
            cp.wait_send()
        for cp in passed_in:
            cp.wait_recv()
        for cp in local:
            cp.wait()

    t = 3 * (n + nw)
    return _Comm(arrs, out_shape, _dma_sems(t, t, max(3 * n, 1), max(3 * n, 1), LOCAL_PARTS * (n + nw)), start, finish)


def _pack(arrs, row_multiple):
    parts = []
    for a in arrs:
        flat = a.reshape(-1).astype(F32)
        pad = (-flat.shape[0]) % LANES
        parts.append(jnp.pad(flat, (0, pad)) if pad else flat)
    flat = jnp.concatenate(parts)
    rows = -(-flat.shape[0] // LANES)
    rows_p = -(-rows // row_multiple) * row_multiple
    return jnp.pad(flat, (0, rows_p * LANES - flat.shape[0])).reshape(rows_p, LANES)


def _unpack(packed, shapes):
    flat = packed.reshape(-1)
    outs, off = [], 0
    for sh in shapes:
        size = int(np.prod(sh))
        outs.append(flat[off:off + size].reshape(sh))
        off += size + (-size) % LANES
    return outs


SMALL = ["norm_mix_g", "pool_w", "pool_scale", "ssm_log_neg_a_re", "ssm_a_im", "ssm_log_dt", "ssm_b_re", "ssm_b_im",
         "ssm_c_re", "ssm_c_im", "ssm_d", "glu_b", "out_norm_pool_g", "out_norm_ssm_g", "norm_ffn_g", "conv_b",
         "final_norm_g"]
BIG = ["w_in", "glu_w", "w_out", "w_up", "w_down"]
WIDE = ["pool_w", "ssm_b_re", "ssm_b_im", "ssm_c_re", "ssm_c_im"]
WEIGHTS = ['norm_mix_g', 'w_in', 'pool_w', 'pool_scale', 'ssm_log_neg_a_re', 'ssm_a_im', 'ssm_log_dt', 'ssm_b_re',
           'ssm_b_im', 'ssm_c_re', 'ssm_c_im', 'ssm_d', 'glu_w', 'glu_b', 'out_norm_pool_g', 'out_norm_ssm_g', 'w_out',
           'norm_ffn_g', 'w_up', 'conv_w', 'conv_b', 'w_down', 'final_norm_g']


def _local_step(x, target, p, full, shards=None, c_arr=None):
    L, D = x.shape
    dist = shards is not None
    row = lambda a: a.reshape(1, -1)
    pool_w_b = p["pool_w"].astype(BF16)
    g_mix, g_pool, g_ssm, g_ffn, g_fin = (row(p[k]) for k in (
        "norm_mix_g", "out_norm_pool_g", "out_norm_ssm_g", "norm_ffn_g", "final_norm_g"))
    pool_scale, ssm_d, glu_b, conv_b = (row(p[k]) for k in ("pool_scale", "ssm_d", "glu_b", "conv_b"))

    lnar = p["ssm_log_neg_a_re"].reshape(2 * N_SSM_GROUPS, SSM_STATE)
    aim = p["ssm_a_im"].reshape(2 * N_SSM_GROUPS, SSM_STATE)
    ldt = jnp.broadcast_to(p["ssm_log_dt"].reshape(2 * N_SSM_GROUPS, 1), lnar.shape)
    lam_re, lam_im, f_re, f_im = _ssm_params(lnar, aim, ldt)
    flat2 = lambda a: a.reshape(2, N_STATE)
    lam4 = jnp.stack([flat2(lam_re)[0], flat2(lam_im)[0], flat2(lam_re)[1], flat2(lam_im)[1]])
    tables = _scan_tables(lam4)
    per_group = (2, N_SSM_GROUPS, SSM_STATE)
    dense, got0 = _ssm_expand(p["ssm_b_re"], p["ssm_b_im"], p["ssm_c_re"], p["ssm_c_im"],
                              f_re.reshape(per_group + (1,)), f_im.reshape(per_group + (1,)),
                              comm=_comm_gather_split([shards["w_in"]], []) if dist else None)
    w_in = got0[0].reshape(-1, got0[0].shape[-1]) if dist else full["w_in"]
    ssm_args = [tuple(dense[4 * d:4 * d + 4]) + (tables,) for d in range(2)]

    u, xn, u_seg = _in_proj(x, g_mix, w_in)
    yn_pool = _pool_fwd(u, pool_w_b, pool_scale, g_pool)
    gather1 = _comm_gather_split([shards[k] for k in ("glu_w", "w_out", "w_down")], [shards["conv_w"]]) if dist else None
    (y0, s0r, s0i), got1 = _ssm_scan_fwd(u_seg, *ssm_args[0], 0, False, comm=gather1)
    gather2 = _comm_gather_split([shards["w_up"]], []) if dist else None
    (y1, s1r, s1i), got2 = _ssm_scan_fwd(u_seg, *ssm_args[1], 2, True, comm=gather2)
    if dist:
        glu_w, w_out, w_down = (g.reshape((-1,) + g.shape[2:]) for g in got1[:3])
        conv_w = jnp.transpose(got1[3], (1, 0, 2)).reshape(3, -1)
        w_up4 = got2[0]
    else:
        glu_w, w_out, w_up4, w_down, conv_w = (full[k] for k in ("glu_w", "w_out", "w_up", "w_down", "conv_w"))
    h1, hn, ycat = _mix_out(yn_pool, y0, y1, u, x, ssm_d, glu_w, glu_b, g_ssm, w_out, g_ffn)
    up = _ffn_up(hn, w_up4)
    a, c_val, c_gate, dh2, dh2_b, loss, g_final = _ffn_down_loss(up, conv_w, conv_b, w_down, h1, target, g_fin)

    d_val, d_gate, gbv, gbg = _ffn_act_bwd(c_val, c_gate, w_down, dh2_b)
    g_w_down = _matmul_tn(a, dh2_b, FF_BLK, D, "grad_w_down")
    d_up, dh1, dh1_b, g_ffn_g, gcw = _ffn_up_bwd(d_val, d_gate, up, conv_w, w_up4, h1, dh2, g_ffn)
    g_w_up = _matmul_tn_blocks(hn, d_up.reshape(4, L, FF_BLK), TM, "grad_w_up")
    g_w_out = _matmul_tn(ycat, dh1_b, TM, D, "grad_w_out")
    late = ("w_up", "w_down", "w_out", "glu_w")
    halves = [g_w_up.reshape(4, 2, D // 2, FF_BLK), g_w_down.reshape(4, 2, D_FF // 8, D)]
    (dy, du_direct, g_glu_w, g_glu_b, g_ssm_d, g_ssm_g), swapped = _ssm_bwd_local(
        dh1_b, y0, y1, u, ssm_d, glu_w, glu_b, g_ssm, w_out, comm=_comm_pair_swap(halves, half=True) if dist else None)
    more = [g_w_out.reshape(4, 2, D // 8, D), g_glu_w.reshape(4, 2, D_SSM // 8, D_SSM)]
    (d_pooled, g_pool_w, g_pool_scale, g_pool_g), swapped_more = _pool_bwd_local(
        dh1_b, u, w_out, pool_w_b, pool_scale, g_pool, comm=_comm_pair_swap(more, half=True) if dist else None)
    halves, from_sibling = halves + more, list(swapped) + list(swapped_more)
    du_pool = _pool_bwd_window(d_pooled)
    reduce_a, reduce_b = None, None
    if dist:
        chip_sums = [_add_half(h, r, c_arr, "sum_pair_" + k, BF16) for k, h, r in zip(late, halves, from_sibling)]
        reduce_a = _comm_chip_exchange(chip_sums[:1], scatter=True)
        reduce_b = _comm_chip_exchange(chip_sums[1:], scatter=True)
    (du0, gb0r, gb0i, gc0r, gc0i, gv0), chips_a = _ssm_scan_bwd(dy, u_seg, s0r, s0i, *ssm_args[0], 1, True, comm=reduce_a)
    (du1, gb1r, gb1i, gc1r, gc1i, gv1), chips_b = _ssm_scan_bwd(dy, u_seg, s1r, s1i, *ssm_args[1], 3, False, comm=reduce_b)
    mine = [_sum4(r, "sum_chips_" + k) for k, r in zip(late, list(chips_a) + list(chips_b))]
    by_state = (2, N_SSM_GROUPS, 1, SSM_STATE)
    g_b_re, g_b_im, g_f_re, g_f_im = _ssm_unfold(
        jnp.stack([gb0r, gb1r]), jnp.stack([gb0i, gb1i]),
        jnp.swapaxes(p["ssm_b_re"], 2, 3), jnp.swapaxes(p["ssm_b_im"], 2, 3),
        f_re.reshape(by_state), f_im.reshape(by_state))
    gvec = lambda j: jnp.stack([gv0[j], gv1[j]]).reshape(2 * N_SSM_GROUPS, SSM_STATE)
    g_lnar, g_aim, g_ldt = _ssm_params_bwd(lnar, aim, ldt, gvec(0), gvec(1),
                                           g_f_re.reshape(lnar.shape), g_f_im.reshape(lnar.shape))
    (grad_x, d_u_b, g_mix_g), theirs = _in_bwd(du_pool, du_direct, du0, du1, dh1, x, g_mix, w_in,
                                               comm=_comm_pair_swap(mine) if dist else None)
    g_w_in = _matmul_tn(xn, d_u_b, TM, D, "grad_w_in")

    small = {
        "norm_mix_g": g_mix_g, "pool_w": g_pool_w, "pool_scale": g_pool_scale,
        "ssm_log_neg_a_re": g_lnar, "ssm_a_im": g_aim, "ssm_log_dt": g_ldt,
        "ssm_b_re": jnp.swapaxes(g_b_re, 2, 3), "ssm_b_im": jnp.swapaxes(g_b_im, 2, 3),
        "ssm_c_re": jnp.stack([gc0r, gc1r]), "ssm_c_im": jnp.stack([gc0i, gc1i]),
        "ssm_d": g_ssm_d, "glu_b": g_glu_b, "out_norm_pool_g": g_pool_g, "out_norm_ssm_g": g_ssm_g,
        "norm_ffn_g": g_ffn_g, "conv_b": jnp.concatenate([gbv[0], gbg[0]]), "final_norm_g": g_final,
        "conv_w": jnp.transpose(gcw, (2, 0, 1, 3)).reshape(3, -1),
    }
    big = {"w_in": g_w_in}
    reduced = dict(zip(late, zip(mine, theirs)))
    if not dist:
        big.update({"w_up": g_w_up, "w_down": g_w_down, "w_out": g_w_out, "glu_w": g_glu_w})
    return loss, grad_x, small, big, reduced


def kernel(x, norm_mix_g, w_in, pool_w, pool_scale, ssm_log_neg_a_re, ssm_a_im, ssm_log_dt, ssm_b_re, ssm_b_im, ssm_c_re, ssm_c_im, ssm_d, glu_w, glu_b, out_norm_pool_g, out_norm_ssm_g, w_out, norm_ffn_g, w_up, conv_w, conv_b, w_down, final_norm_g, loss_target, m_norm_mix_g, m_w_in, m_pool_w, m_pool_scale, m_ssm_log_neg_a_re, m_ssm_a_im, m_ssm_log_dt, m_ssm_b_re, m_ssm_b_im, m_ssm_c_re, m_ssm_c_im, m_ssm_d, m_glu_w, m_glu_b, m_out_norm_pool_g, m_out_norm_ssm_g, m_w_out, m_norm_ffn_g, m_w_up, m_conv_w, m_conv_b, m_w_down, m_final_norm_g, v_norm_mix_g, v_w_in, v_pool_w, v_pool_scale, v_ssm_log_neg_a_re, v_ssm_a_im, v_ssm_log_dt, v_ssm_b_re, v_ssm_b_im, v_ssm_c_re, v_ssm_c_im, v_ssm_d, v_glu_w, v_glu_b, v_out_norm_pool_g, v_out_norm_ssm_g, v_w_out, v_norm_ffn_g, v_w_up, v_conv_w, v_conv_b, v_w_down, v_final_norm_g):
    args = locals()
    w = {k: args[k] for k in WEIGHTS}
    m = {k: args["m_" + k] for k in WEIGHTS}
    v = {k: args["v_" + k] for k in WEIGHTS}
    chip = 2 * lax.axis_index("x") + lax.axis_index("y")
    c_arr = lax.axis_index("c").astype(jnp.int32).reshape(1)

    shards = {k: w[k].astype(BF16) for k in BIG}
    shards["conv_w"] = conv_w
    loss, grad_x, g_small, g_big, reduced = _local_step(x[0], loss_target[0], w, {}, shards, c_arr)

    exact = [k for k in SMALL if k not in WIDE]
    packs = [_pack([loss] + [g_small[k] for k in exact] + [g_small["conv_w"]], 512),
             _pack([g_small[k] for k in WIDE], 512)]
    halves = [g_big["w_in"].reshape(4, 2, g_big["w_in"].shape[0] // 8, -1)]
    halves += [pk.reshape(1, 2, pk.shape[0] // 2, LANES) for pk in packs]
    from_sibling = _comm_call(_comm_pair_swap(halves, half=True), "reduce_pair")
    names = ("w_in", "exact", "wide")
    sums = [_add_half(h, r, c_arr, "sum_pair_" + k, dt)
            for k, h, r, dt in zip(names, halves, from_sibling, (BF16, F32, BF16))]
    grads, delta, new_m, new_v = {}, {}, {}, {}

    def adamw_behind(k, comm):
        own, other = reduced[k]
        (grads[k], delta[k], new_m[k], new_v[k]), got = _adamw_halves(
            w[k], own, other, m[k], v[k], "adamw_" + k, comm=comm)
        return got

    from_chips = adamw_behind("w_up", _comm_join(_comm_chip_exchange(sums[:1], scatter=True),
                                                  _comm_chip_exchange([s[0] for s in sums[1:]], scatter=False)))
    mine = [_sum4(r, "sum_chips_" + k) for k, r in zip(names, from_chips)]
    theirs = adamw_behind("w_down", _comm_pair_swap(mine))
    for k in ("w_out", "glu_w"):
        adamw_behind(k, None)
    exact_all = _join_rows(mine[1], theirs[1], c_arr, "join_exact")
    wide_all = _join_rows(mine[2], theirs[2], c_arr, "join_wide")
    shapes = [loss.shape] + [w[k].shape for k in exact] + [(3, 4 * FF_BLK)]
    grads.update(zip(["loss"] + exact + ["conv_w_full"], _unpack(exact_all, shapes)))
    grads.update(zip(WIDE, _unpack(wide_all, [w[k].shape for k in WIDE])))
    loss = grads.pop("loss")[0, 0]
    grads["conv_w"] = lax.dynamic_slice_in_dim(grads.pop("conv_w_full"), chip * FF_BLK, FF_BLK, axis=1)

    reduced["w_in"] = (mine[0], theirs[0])
    adamw_behind("w_in", None)
    padded = ["ssm_b_re", "ssm_b_im"]
    for keys, name in ((padded, "adamw_ssm_b"), ([k for k in SMALL + ["conv_w"] if k not in padded], "adamw_small")):
        outs = _adamw_many(*([d[k] for k in keys] for d in (w, grads, m, v)), name)
        for d, o in zip((delta, new_m, new_v), outs):
            d.update(zip(keys, o))

    return (loss, grad_x[None], *[grads[k] for k in WEIGHTS], *[delta[k] for k in WEIGHTS],
            *[new_m[k] for k in WEIGHTS], *[new_v[k] for k in WEIGHTS])
```
